```python
import math
import jax, jax.numpy as jnp
from jax import lax
import numpy as np

D_MODEL = 1024
BATCH = 8
SEQ = 4096
DEPTH = 2

CHUNK = 64
N_HEADS = 8
N_KV_HEADS = 2
HEAD_DIM = 64
Q_PER_KV = N_HEADS // N_KV_HEADS
WINDOW = 128
WIN_CHUNKS = WINDOW // CHUNK
ATT_W = N_HEADS * HEAD_DIM
KV_W = N_KV_HEADS * HEAD_DIM
SSM_W = 512
SSM_GROUP = 16
SSM_GROUPS = SSM_W // SSM_GROUP
SSM_STATE = 64
POOL_W = 512
POOL_WINDOWS = (2, 4, 8, 16)
POOL_GROUPS = len(POOL_WINDOWS)
POOL_GW = POOL_W // POOL_GROUPS
N_BRANCH = 3
SPLIT_SIZES = (ATT_W, KV_W, KV_W, SSM_W, POOL_W, ATT_W, SSM_W, POOL_W, N_BRANCH * D_MODEL)
IN_W = sum(SPLIT_SIZES)
EPS = 1e-6
NEG_INF = -1e30

kernel_name = "hybrid_gated_swa_s5_pool_adaln"


def rmsnorm(x, g):
    xf = x.astype(jnp.float32)
    y = xf * lax.rsqrt(jnp.mean(xf * xf, axis=-1, keepdims=True) + EPS)
    return (y * g.astype(jnp.float32)).astype(x.dtype)


def alibi_slopes(n):
    return jnp.asarray([2.0 ** (-8.0 * (h + 1) / n) for h in range(n)], dtype=jnp.float32)


def window_attention(q, k, v, sinks):
    b, l = q.shape[:2]
    nc = l // CHUNK
    pad = WIN_CHUNKS * CHUNK
    nk = (WIN_CHUNKS + 1) * CHUNK
    kp = jnp.pad(k, ((0, 0), (pad, 0), (0, 0), (0, 0))).reshape(b, nc + WIN_CHUNKS, CHUNK, N_KV_HEADS, HEAD_DIM)
    vp = jnp.pad(v, ((0, 0), (pad, 0), (0, 0), (0, 0))).reshape(b, nc + WIN_CHUNKS, CHUNK, N_KV_HEADS, HEAD_DIM)
    kb = jnp.concatenate([kp[:, j:j + nc] for j in range(WIN_CHUNKS + 1)], axis=2)
    vb = jnp.concatenate([vp[:, j:j + nc] for j in range(WIN_CHUNKS + 1)], axis=2)
    qb = q.reshape(b, nc, CHUNK, N_KV_HEADS, Q_PER_KV, HEAD_DIM)
    s = jnp.einsum('bcqkgd,bcskd->bckgqs', qb, kb).astype(jnp.float32) * (1.0 / math.sqrt(HEAD_DIM))
    qi = jnp.arange(CHUNK)[:, None]
    kj = jnp.arange(nk)[None, :]
    dist = jnp.abs(qi + pad - kj).astype(jnp.float32)
    slopes = alibi_slopes(N_HEADS).reshape(N_KV_HEADS, Q_PER_KV)
    s = s - slopes[:, :, None, None] * dist[None, None]
    valid = (jnp.arange(nc)[:, None] * CHUNK + jnp.arange(nk)[None, :]) >= pad
    s = jnp.where(valid[None, :, None, None, None, :], s, NEG_INF)
    sink = jnp.broadcast_to(sinks.astype(jnp.float32).reshape(N_KV_HEADS, Q_PER_KV)[None, None, :, :, None, None],
                            s.shape[:-1] + (1,))
    p = jax.nn.softmax(jnp.concatenate([s, sink], axis=-1), axis=-1)[..., :-1]
    o = jnp.einsum('bckgqs,bcskd->bcqkgd', p.astype(v.dtype), vb)
    return o.reshape(b, l, ATT_W)


def s5_layer(u, a_re, a_im, log_dt, b_re, b_im, c_re, c_im, d_skip, w_glu, b_glu):
    b, l = u.shape[:2]
    uf = u.astype(jnp.float32)
    lam = lax.complex(a_re.astype(jnp.float32), a_im.astype(jnp.float32))
    dt = jnp.exp(log_dt.astype(jnp.float32))[:, None]
    lam_bar = jnp.exp(lam * dt)
    bmat = lax.complex(b_re.astype(jnp.float32), b_im.astype(jnp.float32))
    b_bar = ((lam_bar - 1.0) / lam)[..., None] * bmat
    ug = uf.reshape(b, l, SSM_GROUPS, SSM_GROUP).astype(jnp.complex64)
    bu = jnp.einsum('gpc,blgc->blgp', b_bar, ug)
    a = jnp.broadcast_to(lam_bar, bu.shape)

    def combine(e1, e2):
        a1, x1 = e1
        a2, x2 = e2
        return a1 * a2, a2 * x1 + x2

    _, states = lax.associative_scan(combine, (a, bu), axis=1)
    cmat = lax.complex(c_re.astype(jnp.float32), c_im.astype(jnp.float32))
    y = jnp.real(jnp.einsum('gcp,blgp->blgc', cmat, states)).reshape(b, l, SSM_W)
    y = y + d_skip.astype(jnp.float32) * uf
    y = jax.nn.gelu(y)
    y = y * jax.nn.sigmoid(y @ w_glu.astype(jnp.float32) + b_glu.astype(jnp.float32))
    return y.astype(u.dtype)


def multiscale_pool(u, w_pool, pool_scale):
    b, l = u.shape[:2]
    uf = u.astype(jnp.float32).reshape(b, l, POOL_GROUPS, POOL_GW)
    cs = jnp.concatenate([jnp.zeros((b, 1, POOL_GROUPS, POOL_GW), jnp.float32), jnp.cumsum(uf, axis=1)], axis=1)
    t = jnp.arange(l)
    pooled = []
    for gi, w in enumerate(POOL_WINDOWS):
        csp = jnp.pad(cs[:, :, gi], ((0, 0), (w - 1, 0), (0, 0)))
        ssum = csp[:, w:w + l] - csp[:, :l]
        cnt = jnp.minimum(t + 1, w).astype(jnp.float32)[None, :, None]
        pooled.append(ssum / cnt - uf[:, :, gi])
    pooled = jnp.stack(pooled, axis=2)
    y = jnp.einsum('blgi,gio->blgo', pooled, w_pool.astype(jnp.float32)).reshape(b, l, POOL_W)
    return (y * pool_scale.astype(jnp.float32)).astype(u.dtype)


def _fwd_setup_inputs(seed: int = 0) -> dict:
    key = jax.random.key(seed)
    ks = jax.random.split(key, 32)
    f32 = jnp.float32
    nrm = lambda k, shape, s: jax.random.normal(k, shape, f32) * s
    D = D_MODEL
    n_idx = jnp.arange(SSM_STATE, dtype=f32)
    a_re = -0.5 * (1.0 + 0.02 * jax.random.normal(ks[6], (DEPTH, SSM_GROUPS, SSM_STATE), f32))
    a_im = math.pi * n_idx[None, None, :] + 0.02 * jax.random.normal(ks[7], (DEPTH, SSM_GROUPS, SSM_STATE), f32)
    log_dt = jax.random.uniform(ks[8], (DEPTH, SSM_GROUPS), f32, math.log(1e-3), math.log(1e-1))
    return {
        "x": nrm(ks[0], (BATCH, SEQ, D), 1.0),
        "c": nrm(ks[1], (BATCH, D), 1.0),
        "norm_g": 1.0 + nrm(ks[2], (DEPTH, D), 0.02),
        "w_ada": nrm(ks[3], (DEPTH, D, 3 * D), 0.5 * D ** -0.5),
        "b_ada": nrm(ks[4], (DEPTH, 3 * D), 0.02),
        "w_in": nrm(ks[5], (DEPTH, D, IN_W), D ** -0.5),
        "attn_sinks": nrm(ks[9], (DEPTH, N_HEADS), 0.5),
        "ssm_a_re": a_re,
        "ssm_a_im": a_im,
        "ssm_log_dt": log_dt,
        "ssm_b_re": nrm(ks[10], (DEPTH, SSM_GROUPS, SSM_STATE, SSM_GROUP), (2 * SSM_GROUP) ** -0.5),
        "ssm_b_im": nrm(ks[11], (DEPTH, SSM_GROUPS, SSM_STATE, SSM_GROUP), (2 * SSM_GROUP) ** -0.5),
        "ssm_c_re": nrm(ks[12], (DEPTH, SSM_GROUPS, SSM_GROUP, SSM_STATE), (2 * SSM_STATE) ** -0.5),
        "ssm_c_im": nrm(ks[13], (DEPTH, SSM_GROUPS, SSM_GROUP, SSM_STATE), (2 * SSM_STATE) ** -0.5),
        "ssm_d": nrm(ks[14], (DEPTH, SSM_W), 1.0),
        "w_glu": nrm(ks[15], (DEPTH, SSM_W, SSM_W), SSM_W ** -0.5),
        "b_glu": nrm(ks[16], (DEPTH, SSM_W), 0.02),
        "w_pool": nrm(ks[17], (DEPTH, POOL_GROUPS, POOL_GW, POOL_GW), POOL_GW ** -0.5),
        "pool_scale": 1.0 + nrm(ks[18], (DEPTH, POOL_W), 0.1),
        "w_br_att": nrm(ks[19], (DEPTH, ATT_W, D), ATT_W ** -0.5),
        "w_br_ssm": nrm(ks[20], (DEPTH, SSM_W, D), SSM_W ** -0.5),
        "w_br_pool": nrm(ks[21], (DEPTH, POOL_W, D), POOL_W ** -0.5),
        "w_out": nrm(ks[22], (DEPTH, D, D), D ** -0.5),
        "final_g": 1.0 + nrm(ks[23], (D,), 0.02),
    }


def _fwd_reference(x, c, norm_g, w_ada, b_ada, w_in, attn_sinks, ssm_a_re, ssm_a_im, ssm_log_dt,
              ssm_b_re, ssm_b_im, ssm_c_re, ssm_c_im, ssm_d, w_glu, b_glu, w_pool, pool_scale,
              w_br_att, w_br_ssm, w_br_pool, w_out, final_g):
    b, l, _ = x.shape
    split_idx = [int(v) for v in np.cumsum(SPLIT_SIZES)[:-1]]
    c_act = jax.nn.silu(c)
    for li in range(DEPTH):
        mod = c_act @ w_ada[li] + b_ada[li]
        shift, scale, gate = jnp.split(mod, 3, axis=-1)
        h = rmsnorm(x, norm_g[li]) * (1.0 + scale[:, None, :]) + shift[:, None, :]
        proj = h @ w_in[li]
        q, k, v, u_ssm, u_pool, z_att, z_ssm, z_pool, g_logits = jnp.split(proj, split_idx, axis=-1)
        y_att = window_attention(q.reshape(b, l, N_HEADS, HEAD_DIM),
                                 k.reshape(b, l, N_KV_HEADS, HEAD_DIM),
                                 v.reshape(b, l, N_KV_HEADS, HEAD_DIM), attn_sinks[li]) * jax.nn.silu(z_att)
        y_ssm = s5_layer(u_ssm, ssm_a_re[li], ssm_a_im[li], ssm_log_dt[li], ssm_b_re[li], ssm_b_im[li],
                         ssm_c_re[li], ssm_c_im[li], ssm_d[li], w_glu[li], b_glu[li]) * jax.nn.silu(z_ssm)
        y_pool = multiscale_pool(u_pool, w_pool[li], pool_scale[li]) * jax.nn.silu(z_pool)
        gates = jax.nn.sigmoid(g_logits).reshape(b, l, N_BRANCH, D_MODEL)
        merged = (gates[:, :, 0] * (y_att @ w_br_att[li])
                  + gates[:, :, 1] * (y_ssm @ w_br_ssm[li])
                  + gates[:, :, 2] * (y_pool @ w_br_pool[li]))
        x = x + gate[:, None, :] * (merged @ w_out[li])
    return rmsnorm(x, final_g)


import jax as _jax
import jax.numpy as _jnp

TWIN_FORMAT = 'train_step'
FWD_PARAMS = ['x', 'c', 'norm_g', 'w_ada', 'b_ada', 'w_in', 'attn_sinks', 'ssm_a_re', 'ssm_a_im', 'ssm_log_dt', 'ssm_b_re', 'ssm_b_im', 'ssm_c_re', 'ssm_c_im', 'ssm_d', 'w_glu', 'b_glu', 'w_pool', 'pool_scale', 'w_br_att', 'w_br_ssm', 'w_br_pool', 'w_out', 'final_g']
TWIN_WEIGHTS = ['norm_g', 'w_ada', 'b_ada', 'w_in', 'attn_sinks', 'ssm_a_re', 'ssm_a_im', 'ssm_log_dt', 'ssm_b_re', 'ssm_b_im', 'ssm_c_re', 'ssm_c_im', 'ssm_d', 'w_glu', 'b_glu', 'w_pool', 'pool_scale', 'w_br_att', 'w_br_ssm', 'w_br_pool', 'w_out', 'final_g']
TWIN_DIFF_INPUT = 'x'
TWIN_INPUTS = ['x', 'c', 'norm_g', 'w_ada', 'b_ada', 'w_in', 'attn_sinks', 'ssm_a_re', 'ssm_a_im', 'ssm_log_dt', 'ssm_b_re', 'ssm_b_im', 'ssm_c_re', 'ssm_c_im', 'ssm_d', 'w_glu', 'b_glu', 'w_pool', 'pool_scale', 'w_br_att', 'w_br_ssm', 'w_br_pool', 'w_out', 'final_g', 'loss_target', 'm_norm_g', 'm_w_ada', 'm_b_ada', 'm_w_in', 'm_attn_sinks', 'm_ssm_a_re', 'm_ssm_a_im', 'm_ssm_log_dt', 'm_ssm_b_re', 'm_ssm_b_im', 'm_ssm_c_re', 'm_ssm_c_im', 'm_ssm_d', 'm_w_glu', 'm_b_glu', 'm_w_pool', 'm_pool_scale', 'm_w_br_att', 'm_w_br_ssm', 'm_w_br_pool', 'm_w_out', 'm_final_g', 'v_norm_g', 'v_w_ada', 'v_b_ada', 'v_w_in', 'v_attn_sinks', 'v_ssm_a_re', 'v_ssm_a_im', 'v_ssm_log_dt', 'v_ssm_b_re', 'v_ssm_b_im', 'v_ssm_c_re', 'v_ssm_c_im', 'v_ssm_d', 'v_w_glu', 'v_b_glu', 'v_w_pool', 'v_pool_scale', 'v_w_br_att', 'v_w_br_ssm', 'v_w_br_pool', 'v_w_out', 'v_final_g']
TWIN_OUTPUTS = ['loss', 'grad_x', 'grad_norm_g', 'grad_w_ada', 'grad_b_ada', 'grad_w_in', 'grad_attn_sinks', 'grad_ssm_a_re', 'grad_ssm_a_im', 'grad_ssm_log_dt', 'grad_ssm_b_re', 'grad_ssm_b_im', 'grad_ssm_c_re', 'grad_ssm_c_im', 'grad_ssm_d', 'grad_w_glu', 'grad_b_glu', 'grad_w_pool', 'grad_pool_scale', 'grad_w_br_att', 'grad_w_br_ssm', 'grad_w_br_pool', 'grad_w_out', 'grad_final_g', 'delta_norm_g', 'delta_w_ada', 'delta_b_ada', 'delta_w_in', 'delta_attn_sinks', 'delta_ssm_a_re', 'delta_ssm_a_im', 'delta_ssm_log_dt', 'delta_ssm_b_re', 'delta_ssm_b_im', 'delta_ssm_c_re', 'delta_ssm_c_im', 'delta_ssm_d', 'delta_w_glu', 'delta_b_glu', 'delta_w_pool', 'delta_pool_scale', 'delta_w_br_att', 'delta_w_br_ssm', 'delta_w_br_pool', 'delta_w_out', 'delta_final_g', 'new_m_norm_g', 'new_m_w_ada', 'new_m_b_ada', 'new_m_w_in', 'new_m_attn_sinks', 'new_m_ssm_a_re', 'new_m_ssm_a_im', 'new_m_ssm_log_dt', 'new_m_ssm_b_re', 'new_m_ssm_b_im', 'new_m_ssm_c_re', 'new_m_ssm_c_im', 'new_m_ssm_d', 'new_m_w_glu', 'new_m_b_glu', 'new_m_w_pool', 'new_m_pool_scale', 'new_m_w_br_att', 'new_m_w_br_ssm', 'new_m_w_br_pool', 'new_m_w_out', 'new_m_final_g', 'new_v_norm_g', 'new_v_w_ada', 'new_v_b_ada', 'new_v_w_in', 'new_v_attn_sinks', 'new_v_ssm_a_re', 'new_v_ssm_a_im', 'new_v_ssm_log_dt', 'new_v_ssm_b_re', 'new_v_ssm_b_im', 'new_v_ssm_c_re', 'new_v_ssm_c_im', 'new_v_ssm_d', 'new_v_w_glu', 'new_v_b_glu', 'new_v_w_pool', 'new_v_pool_scale', 'new_v_w_br_att', 'new_v_w_br_ssm', 'new_v_w_br_pool', 'new_v_w_out', 'new_v_final_g']
TWIN_LEAF_KINDS = {'loss': 'loss', 'grad_x': 'grad_x', 'grad_norm_g': 'grad_w', 'grad_w_ada': 'grad_w', 'grad_b_ada': 'grad_w', 'grad_w_in': 'grad_w', 'grad_attn_sinks': 'grad_w', 'grad_ssm_a_re': 'grad_w', 'grad_ssm_a_im': 'grad_w', 'grad_ssm_log_dt': 'grad_w', 'grad_ssm_b_re': 'grad_w', 'grad_ssm_b_im': 'grad_w', 'grad_ssm_c_re': 'grad_w', 'grad_ssm_c_im': 'grad_w', 'grad_ssm_d': 'grad_w', 'grad_w_glu': 'grad_w', 'grad_b_glu': 'grad_w', 'grad_w_pool': 'grad_w', 'grad_pool_scale': 'grad_w', 'grad_w_br_att': 'grad_w', 'grad_w_br_ssm': 'grad_w', 'grad_w_br_pool': 'grad_w', 'grad_w_out': 'grad_w', 'grad_final_g': 'grad_w', 'delta_norm_g': 'delta_w', 'delta_w_ada': 'delta_w', 'delta_b_ada': 'delta_w', 'delta_w_in': 'delta_w', 'delta_attn_sinks': 'delta_w', 'delta_ssm_a_re': 'delta_w', 'delta_ssm_a_im': 'delta_w', 'delta_ssm_log_dt': 'delta_w', 'delta_ssm_b_re': 'delta_w', 'delta_ssm_b_im': 'delta_w', 'delta_ssm_c_re': 'delta_w', 'delta_ssm_c_im': 'delta_w', 'delta_ssm_d': 'delta_w', 'delta_w_glu': 'delta_w', 'delta_b_glu': 'delta_w', 'delta_w_pool': 'delta_w', 'delta_pool_scale': 'delta_w', 'delta_w_br_att': 'delta_w', 'delta_w_br_ssm': 'delta_w', 'delta_w_br_pool': 'delta_w', 'delta_w_out': 'delta_w', 'delta_final_g': 'delta_w', 'new_m_norm_g': 'new_m', 'new_m_w_ada': 'new_m', 'new_m_b_ada': 'new_m', 'new_m_w_in': 'new_m', 'new_m_attn_sinks': 'new_m', 'new_m_ssm_a_re': 'new_m', 'new_m_ssm_a_im': 'new_m', 'new_m_ssm_log_dt': 'new_m', 'new_m_ssm_b_re': 'new_m', 'new_m_ssm_b_im': 'new_m', 'new_m_ssm_c_re': 'new_m', 'new_m_ssm_c_im': 'new_m', 'new_m_ssm_d': 'new_m', 'new_m_w_glu': 'new_m', 'new_m_b_glu': 'new_m', 'new_m_w_pool': 'new_m', 'new_m_pool_scale': 'new_m', 'new_m_w_br_att': 'new_m', 'new_m_w_br_ssm': 'new_m', 'new_m_w_br_pool': 'new_m', 'new_m_w_out': 'new_m', 'new_m_final_g': 'new_m', 'new_v_norm_g': 'new_v', 'new_v_w_ada': 'new_v', 'new_v_b_ada': 'new_v', 'new_v_w_in': 'new_v', 'new_v_attn_sinks': 'new_v', 'new_v_ssm_a_re': 'new_v', 'new_v_ssm_a_im': 'new_v', 'new_v_ssm_log_dt': 'new_v', 'new_v_ssm_b_re': 'new_v', 'new_v_ssm_b_im': 'new_v', 'new_v_ssm_c_re': 'new_v', 'new_v_ssm_c_im': 'new_v', 'new_v_ssm_d': 'new_v', 'new_v_w_glu': 'new_v', 'new_v_b_glu': 'new_v', 'new_v_w_pool': 'new_v', 'new_v_pool_scale': 'new_v', 'new_v_w_br_att': 'new_v', 'new_v_w_br_ssm': 'new_v', 'new_v_w_br_pool': 'new_v', 'new_v_w_out': 'new_v', 'new_v_final_g': 'new_v'}


def _forward(args):
    return _fwd_reference(*[args[k] for k in FWD_PARAMS])


def _output_shape():
    out = _jax.eval_shape(lambda: _forward(_fwd_setup_inputs(0)))
    return out.shape, out.dtype

N_MICROBATCH = 1
ADAM_LR = 0.001
ADAM_B1 = 0.9
ADAM_B2 = 0.999
ADAM_EPS = 1e-08
ADAM_WD = 0.01
ADAM_STEP = 10
PER_EXAMPLE_BATCH_AXIS = {'x': 0, 'c': 0, 'loss_target': 0}
SHARED_INPUTS = []
_WEIGHT_DTYPES = {'norm_g': _jnp.float32, 'w_ada': _jnp.float32, 'b_ada': _jnp.float32, 'w_in': _jnp.float32, 'attn_sinks': _jnp.float32, 'ssm_a_re': _jnp.float32, 'ssm_a_im': _jnp.float32, 'ssm_log_dt': _jnp.float32, 'ssm_b_re': _jnp.float32, 'ssm_b_im': _jnp.float32, 'ssm_c_re': _jnp.float32, 'ssm_c_im': _jnp.float32, 'ssm_d': _jnp.float32, 'w_glu': _jnp.float32, 'b_glu': _jnp.float32, 'w_pool': _jnp.float32, 'pool_scale': _jnp.float32, 'w_br_att': _jnp.float32, 'w_br_ssm': _jnp.float32, 'w_br_pool': _jnp.float32, 'w_out': _jnp.float32, 'final_g': _jnp.float32}
MOMENT_SCALE = {'norm_g': 3.116798e-02, 'w_ada': 2.914338e-02, 'b_ada': 4.682250e-02, 'w_in': 1.310339e-02, 'attn_sinks': 7.877113e-03, 'ssm_a_re': 8.670722e-04, 'ssm_a_im': 6.652508e-04, 'ssm_log_dt': 3.191981e-01, 'ssm_b_re': 4.102963e-04, 'ssm_b_im': 4.841431e-04, 'ssm_c_re': 9.260887e-04, 'ssm_c_im': 9.090737e-04, 'ssm_d': 1.198376e-02, 'w_glu': 3.360757e-03, 'b_glu': 4.543526e-03, 'w_pool': 2.560319e-02, 'pool_scale': 2.598587e-02, 'w_br_att': 8.092981e-03, 'w_br_ssm': 7.641834e-03, 'w_br_pool': 1.809617e-02, 'w_out': 2.113189e-02, 'final_g': 3.195163e+01}


def _to_microbatches(a, axis):
    t = _jnp.moveaxis(a, axis, 0)
    t = t.reshape((N_MICROBATCH, t.shape[0] // N_MICROBATCH) + t.shape[1:])
    return _jnp.moveaxis(t, 1, axis + 1)


def setup_inputs(seed: int = 0) -> dict:
    inp = _fwd_setup_inputs(seed)
    key = _jax.random.fold_in(_jax.random.key(seed), 7919)
    shape, _ = _output_shape()
    out = dict(inp)
    out["loss_target"] = _jax.random.normal(_jax.random.fold_in(key, 0), shape, _jnp.float32)
    for i, name in enumerate(TWIN_WEIGHTS):
        w = inp[name].astype(_jnp.float32)
        if MOMENT_SCALE is None:
            s = _jnp.sqrt(_jnp.mean(_jnp.square(w)) + 1e-30)
        else:
            s = MOMENT_SCALE[name]
        km, kv = _jax.random.split(_jax.random.fold_in(key, i + 1))
        out[name] = w
        out["m_" + name] = s * _jax.random.normal(km, w.shape, _jnp.float32)
        out["v_" + name] = (s * s) * _jax.random.uniform(kv, w.shape, _jnp.float32, 0.5, 1.5)
    if N_MICROBATCH > 1:
        for name, axis in PER_EXAMPLE_BATCH_AXIS.items():
            out[name] = _to_microbatches(out[name], axis)
    return {'x': out['x'], 'c': out['c'], 'norm_g': out['norm_g'], 'w_ada': out['w_ada'], 'b_ada': out['b_ada'], 'w_in': out['w_in'], 'attn_sinks': out['attn_sinks'], 'ssm_a_re': out['ssm_a_re'], 'ssm_a_im': out['ssm_a_im'], 'ssm_log_dt': out['ssm_log_dt'], 'ssm_b_re': out['ssm_b_re'], 'ssm_b_im': out['ssm_b_im'], 'ssm_c_re': out['ssm_c_re'], 'ssm_c_im': out['ssm_c_im'], 'ssm_d': out['ssm_d'], 'w_glu': out['w_glu'], 'b_glu': out['b_glu'], 'w_pool': out['w_pool'], 'pool_scale': out['pool_scale'], 'w_br_att': out['w_br_att'], 'w_br_ssm': out['w_br_ssm'], 'w_br_pool': out['w_br_pool'], 'w_out': out['w_out'], 'final_g': out['final_g'], 'loss_target': out['loss_target'], 'm_norm_g': out['m_norm_g'], 'm_w_ada': out['m_w_ada'], 'm_b_ada': out['m_b_ada'], 'm_w_in': out['m_w_in'], 'm_attn_sinks': out['m_attn_sinks'], 'm_ssm_a_re': out['m_ssm_a_re'], 'm_ssm_a_im': out['m_ssm_a_im'], 'm_ssm_log_dt': out['m_ssm_log_dt'], 'm_ssm_b_re': out['m_ssm_b_re'], 'm_ssm_b_im': out['m_ssm_b_im'], 'm_ssm_c_re': out['m_ssm_c_re'], 'm_ssm_c_im': out['m_ssm_c_im'], 'm_ssm_d': out['m_ssm_d'], 'm_w_glu': out['m_w_glu'], 'm_b_glu': out['m_b_glu'], 'm_w_pool': out['m_w_pool'], 'm_pool_scale': out['m_pool_scale'], 'm_w_br_att': out['m_w_br_att'], 'm_w_br_ssm': out['m_w_br_ssm'], 'm_w_br_pool': out['m_w_br_pool'], 'm_w_out': out['m_w_out'], 'm_final_g': out['m_final_g'], 'v_norm_g': out['v_norm_g'], 'v_w_ada': out['v_w_ada'], 'v_b_ada': out['v_b_ada'], 'v_w_in': out['v_w_in'], 'v_attn_sinks': out['v_attn_sinks'], 'v_ssm_a_re': out['v_ssm_a_re'], 'v_ssm_a_im': out['v_ssm_a_im'], 'v_ssm_log_dt': out['v_ssm_log_dt'], 'v_ssm_b_re': out['v_ssm_b_re'], 'v_ssm_b_im': out['v_ssm_b_im'], 'v_ssm_c_re': out['v_ssm_c_re'], 'v_ssm_c_im': out['v_ssm_c_im'], 'v_ssm_d': out['v_ssm_d'], 'v_w_glu': out['v_w_glu'], 'v_b_glu': out['v_b_glu'], 'v_w_pool': out['v_w_pool'], 'v_pool_scale': out['v_pool_scale'], 'v_w_br_att': out['v_w_br_att'], 'v_w_br_ssm': out['v_w_br_ssm'], 'v_w_br_pool': out['v_w_br_pool'], 'v_w_out': out['v_w_out'], 'v_final_g': out['v_final_g']}


def _loss(weights, diff, rest, loss_target):
    with _jax.named_scope("forward"):
        args = {**rest, TWIN_DIFF_INPUT: diff, **{k: w.astype(_WEIGHT_DTYPES[k]) for k, w in weights.items()}}
        y = _forward(args)
    with _jax.named_scope("loss_head"):
        err = _jnp.square(y.astype(_jnp.float32) - loss_target)
        return 0.5 * _jnp.sum(_jnp.mean(err, axis=-1)) if err.ndim else 0.5 * err


def _adamw(w, g, m, v):
    m = ADAM_B1 * m + (1.0 - ADAM_B1) * g
    v = ADAM_B2 * v + (1.0 - ADAM_B2) * _jnp.square(g)
    m_hat = m / (1.0 - ADAM_B1 ** ADAM_STEP)
    v_hat = v / (1.0 - ADAM_B2 ** ADAM_STEP)
    delta = -ADAM_LR * (m_hat / (_jnp.sqrt(v_hat) + ADAM_EPS) + ADAM_WD * w)
    return delta, m, v


def reference(x, c, norm_g, w_ada, b_ada, w_in, attn_sinks, ssm_a_re, ssm_a_im, ssm_log_dt, ssm_b_re, ssm_b_im, ssm_c_re, ssm_c_im, ssm_d, w_glu, b_glu, w_pool, pool_scale, w_br_att, w_br_ssm, w_br_pool, w_out, final_g, loss_target, m_norm_g, m_w_ada, m_b_ada, m_w_in, m_attn_sinks, m_ssm_a_re, m_ssm_a_im, m_ssm_log_dt, m_ssm_b_re, m_ssm_b_im, m_ssm_c_re, m_ssm_c_im, m_ssm_d, m_w_glu, m_b_glu, m_w_pool, m_pool_scale, m_w_br_att, m_w_br_ssm, m_w_br_pool, m_w_out, m_final_g, v_norm_g, v_w_ada, v_b_ada, v_w_in, v_attn_sinks, v_ssm_a_re, v_ssm_a_im, v_ssm_log_dt, v_ssm_b_re, v_ssm_b_im, v_ssm_c_re, v_ssm_c_im, v_ssm_d, v_w_glu, v_b_glu, v_w_pool, v_pool_scale, v_w_br_att, v_w_br_ssm, v_w_br_pool, v_w_out, v_final_g):
    given = dict(x=x, c=c, norm_g=norm_g, w_ada=w_ada, b_ada=b_ada, w_in=w_in, attn_sinks=attn_sinks, ssm_a_re=ssm_a_re, ssm_a_im=ssm_a_im, ssm_log_dt=ssm_log_dt, ssm_b_re=ssm_b_re, ssm_b_im=ssm_b_im, ssm_c_re=ssm_c_re, ssm_c_im=ssm_c_im, ssm_d=ssm_d, w_glu=w_glu, b_glu=b_glu, w_pool=w_pool, pool_scale=pool_scale, w_br_att=w_br_att, w_br_ssm=w_br_ssm, w_br_pool=w_br_pool, w_out=w_out, final_g=final_g, loss_target=loss_target, m_norm_g=m_norm_g, m_w_ada=m_w_ada, m_b_ada=m_b_ada, m_w_in=m_w_in, m_attn_sinks=m_attn_sinks, m_ssm_a_re=m_ssm_a_re, m_ssm_a_im=m_ssm_a_im, m_ssm_log_dt=m_ssm_log_dt, m_ssm_b_re=m_ssm_b_re, m_ssm_b_im=m_ssm_b_im, m_ssm_c_re=m_ssm_c_re, m_ssm_c_im=m_ssm_c_im, m_ssm_d=m_ssm_d, m_w_glu=m_w_glu, m_b_glu=m_b_glu, m_w_pool=m_w_pool, m_pool_scale=m_pool_scale, m_w_br_att=m_w_br_att, m_w_br_ssm=m_w_br_ssm, m_w_br_pool=m_w_br_pool, m_w_out=m_w_out, m_final_g=m_final_g, v_norm_g=v_norm_g, v_w_ada=v_w_ada, v_b_ada=v_b_ada, v_w_in=v_w_in, v_attn_sinks=v_attn_sinks, v_ssm_a_re=v_ssm_a_re, v_ssm_a_im=v_ssm_a_im, v_ssm_log_dt=v_ssm_log_dt, v_ssm_b_re=v_ssm_b_re, v_ssm_b_im=v_ssm_b_im, v_ssm_c_re=v_ssm_c_re, v_ssm_c_im=v_ssm_c_im, v_ssm_d=v_ssm_d, v_w_glu=v_w_glu, v_b_glu=v_b_glu, v_w_pool=v_w_pool, v_pool_scale=v_pool_scale, v_w_br_att=v_w_br_att, v_w_br_ssm=v_w_br_ssm, v_w_br_pool=v_w_br_pool, v_w_out=v_w_out, v_final_g=v_final_g)
    weights = {n: given[n] for n in TWIN_WEIGHTS}
    shared = {n: given[n] for n in SHARED_INPUTS}
    per_example = {n: given[n] for n in ['x', 'c']}
    grad_fn = _jax.value_and_grad(_loss, argnums=(0, 1))

    def one_microbatch(ex, loss_target):
        ex = dict(ex)
        diff = ex.pop(TWIN_DIFF_INPUT)
        return grad_fn(weights, diff, {**shared, **ex}, loss_target)

    if N_MICROBATCH == 1:
        loss, (grad_w, grad_x) = one_microbatch(per_example, given["loss_target"])
    else:
        def body(carry, xs):
            loss_sum, grad_sum = carry
            l_k, (gw_k, gx_k) = one_microbatch(xs[0], xs[1])
            with _jax.named_scope("update"):
                return (loss_sum + l_k, _jax.tree.map(_jnp.add, grad_sum, gw_k)), gx_k

        init = (_jnp.zeros((), _jnp.float32), _jax.tree.map(_jnp.zeros_like, weights))
        (loss, grad_w), grad_x = _jax.lax.scan(body, init, (per_example, given["loss_target"]))
    with _jax.named_scope("update"):
        delta_w, new_m, new_v = {}, {}, {}
        for n in TWIN_WEIGHTS:
            delta_w[n], new_m[n], new_v[n] = _adamw(weights[n], grad_w[n], given["m_" + n], given["v_" + n])
    return (loss, grad_x, *[grad_w[n] for n in TWIN_WEIGHTS], *[delta_w[n] for n in TWIN_WEIGHTS],
            *[new_m[n] for n in TWIN_WEIGHTS], *[new_v[n] for n in TWIN_WEIGHTS])
```

```python
import functools
import math

import jax
import jax.numpy as jnp
from jax import lax
from jax.experimental import pallas as pl
from jax.experimental.pallas import tpu as pltpu

F32 = jnp.float32
BF16 = jnp.bfloat16

N_DEV = 8
D_MODEL = 1024
DEPTH = 2
CHUNK = 64
N_HEADS = 8
N_KV_HEADS = 2
HEAD_DIM = 64
Q_PER_KV = N_HEADS // N_KV_HEADS
WINDOW = 128
ATT_W = 512
KV_W = 128
SSM_W = 512
SSM_GROUP = 16
SSM_GROUPS = 32
SSM_STATE = 64
SSM_N = SSM_GROUPS * SSM_STATE
POOL_W = 512
POOL_WINDOWS = (2, 4, 8, 16)
POOL_GW = 128
POOL_HALO = 16
EPS = 1e-6
NEG_INF = -1e30
ADAM_LR = 0.001
ADAM_B1 = 0.9
ADAM_B2 = 0.999
ADAM_EPS = 1e-08
ADAM_WD = 0.01
ADAM_STEP = 10

SEQ_BLOCK = 256
VMEM_LIMIT = 56 * 1024 * 1024

NN = (((1,), (0,)), ((), ()))
NT = (((1,), (1,)), ((), ()))
TN = (((0,), (0,)), ((), ()))


def _dot(a, b, dims=NN):
    return lax.dot_general(a.astype(BF16), b.astype(BF16), dims, preferred_element_type=F32)


def _params(*sem):
    return pltpu.CompilerParams(dimension_semantics=sem, vmem_limit_bytes=VMEM_LIMIT)


def _sigmoid(x):
    return 1.0 / (1.0 + jnp.exp(-x))


def _silu_and_grad(z):
    s = _sigmoid(z)
    return z * s, s * (1.0 + z * (1.0 - s))


_GELU_K = math.sqrt(2.0 / math.pi)


def _gelu_and_grad(x):
    inner = _GELU_K * (x + 0.044715 * x * x * x)
    t = jnp.tanh(inner)
    val = 0.5 * x * (1.0 + t)
    grad = 0.5 * (1.0 + t) + 0.5 * x * (1.0 - t * t) * _GELU_K * (1.0 + 3.0 * 0.044715 * x * x)
    return val, grad


def _mm(a, b, *, nt=False, out_dtype=F32, tm=512, tn=512, name):
    m, k = a.shape
    n = b.shape[0] if nt else b.shape[1]
    tm, tn = min(tm, m), min(tn, n)
    dims = NT if nt else NN

    def body(a_ref, b_ref, o_ref):
        o_ref[...] = _dot(a_ref[...], b_ref[...], dims).astype(out_dtype)

    b_spec = pl.BlockSpec((tn, k), lambda i, j: (j, 0)) if nt else pl.BlockSpec((k, tn), lambda i, j: (0, j))
    return pl.pallas_call(
        body, grid=(m // tm, n // tn),
        in_specs=[pl.BlockSpec((tm, k), lambda i, j: (i, 0)), b_spec],
        out_specs=pl.BlockSpec((tm, tn), lambda i, j: (i, j)),
        out_shape=jax.ShapeDtypeStruct((m, n), out_dtype),
        compiler_params=_params("parallel", "parallel"), name=name)(a, b)


def _mm_nt_sum(pairs, *, out_dtype=F32, tm=512, tn=512, name):
    m = pairs[0][0].shape[0]
    n = pairs[0][1].shape[0]
    np_ = len(pairs)

    def body(*refs):
        o_ref = refs[-1]
        acc = _dot(refs[0][...], refs[1][...], NT)
        for p in range(1, np_):
            acc = acc + _dot(refs[2 * p][...], refs[2 * p + 1][...], NT)
        o_ref[...] = acc.astype(out_dtype)

    in_specs, args = [], []
    for a, b in pairs:
        in_specs.append(pl.BlockSpec((tm, a.shape[1]), lambda i, j: (i, 0)))
        in_specs.append(pl.BlockSpec((tn, b.shape[1]), lambda i, j: (j, 0)))
        args += [a, b]
    return pl.pallas_call(
        body, grid=(m // tm, n // tn), in_specs=in_specs,
        out_specs=pl.BlockSpec((tm, tn), lambda i, j: (i, j)),
        out_shape=jax.ShapeDtypeStruct((m, n), out_dtype),
        compiler_params=_params("parallel", "parallel"), name=name)(*args)


def _mm_tn(a, b, *, out_dtype=F32, tm=512, tn=512, tk=512, name):
    k, m = a.shape
    n = b.shape[1]
    tm, tn, tk = min(tm, m), min(tn, n), min(tk, k)
    nk = k // tk

    def body(a_ref, b_ref, o_ref, acc_ref):
        kk = pl.program_id(2)

        @pl.when(kk == 0)
        def _():
            acc_ref[...] = jnp.zeros_like(acc_ref)

        acc_ref[...] += _dot(a_ref[...], b_ref[...], TN)

        @pl.when(kk == nk - 1)
        def _():
            o_ref[...] = acc_ref[...].astype(out_dtype)

    return pl.pallas_call(
        body, grid=(m // tm, n // tn, nk),
        in_specs=[pl.BlockSpec((tk, tm), lambda i, j, kk: (kk, i)), pl.BlockSpec((tk, tn), lambda i, j, kk: (kk, j))],
        out_specs=pl.BlockSpec((tm, tn), lambda i, j, kk: (i, j)),
        out_shape=jax.ShapeDtypeStruct((m, n), out_dtype),
        scratch_shapes=[pltpu.VMEM((tm, tn), F32)],
        compiler_params=_params("parallel", "parallel", "arbitrary"), name=name)(a, b)


def _ln_fwd(x, g, shift, scale, *, name, tm=512):
    l, d = x.shape

    def body(x_ref, g_ref, sh_ref, sc_ref, h_ref):
        xv = x_ref[...]
        n = xv * lax.rsqrt(jnp.mean(xv * xv, axis=-1, keepdims=True) + EPS)
        h_ref[...] = ((n * g_ref[...]) * (1.0 + sc_ref[...]) + sh_ref[...]).astype(BF16)

    vec = pl.BlockSpec((1, d), lambda i: (0, 0))
    return pl.pallas_call(
        body, grid=(l // tm,),
        in_specs=[pl.BlockSpec((tm, d), lambda i: (i, 0)), vec, vec, vec],
        out_specs=pl.BlockSpec((tm, d), lambda i: (i, 0)),
        out_shape=jax.ShapeDtypeStruct((l, d), BF16),
        compiler_params=_params("parallel"), name=name)(x, g, shift, scale)


def _ln_bwd(x, dh, dres, g, scale, *, name, tm=512):
    l, d = x.shape

    def body(x_ref, dh_ref, dres_ref, g_ref, sc_ref, dx_ref, sums_ref):
        xv = x_ref[...]
        dhv = dh_ref[...]
        rstd = lax.rsqrt(jnp.mean(xv * xv, axis=-1, keepdims=True) + EPS)
        n = xv * rstd
        gv = g_ref[...]
        dr = dhv * (1.0 + sc_ref[...])
        dn = dr * gv
        dx_ref[...] = dres_ref[...] + rstd * (dn - n * jnp.mean(dn * n, axis=-1, keepdims=True))

        @pl.when(pl.program_id(0) == 0)
        def _():
            sums_ref[...] = jnp.zeros_like(sums_ref)

        sums_ref[0:1, :] += jnp.sum(dhv, axis=0, keepdims=True)
        sums_ref[1:2, :] += jnp.sum(dhv * (n * gv), axis=0, keepdims=True)
        sums_ref[2:3, :] += jnp.sum(dr * n, axis=0, keepdims=True)

    vec = pl.BlockSpec((1, d), lambda i: (0, 0))
    row = pl.BlockSpec((tm, d), lambda i: (i, 0))
    return pl.pallas_call(
        body, grid=(l // tm,),
        in_specs=[row, row, row, vec, vec],
        out_specs=[row, pl.BlockSpec((8, d), lambda i: (0, 0))],
        out_shape=[jax.ShapeDtypeStruct((l, d), F32), jax.ShapeDtypeStruct((8, d), F32)],
        compiler_params=_params("arbitrary"), name=name)(x, dh, dres, g, scale)


def _final_loss(x, g, target, *, tm=512):
    l, d = x.shape

    def body(x_ref, g_ref, t_ref, dx_ref, sums_ref):
        xv = x_ref[...]
        rstd = lax.rsqrt(jnp.mean(xv * xv, axis=-1, keepdims=True) + EPS)
        n = xv * rstd
        gv = g_ref[...]
        err = n * gv - t_ref[...]
        dy = err * (1.0 / d)
        dn = dy * gv
        dx_ref[...] = rstd * (dn - n * jnp.mean(dn * n, axis=-1, keepdims=True))

        @pl.when(pl.program_id(0) == 0)
        def _():
            sums_ref[...] = jnp.zeros_like(sums_ref)

        sums_ref[0:1, :] += jnp.sum(dy * n, axis=0, keepdims=True)
        sums_ref[1:2, :] += jnp.sum(err * err, axis=0, keepdims=True) * (0.5 / d)

    vec = pl.BlockSpec((1, d), lambda i: (0, 0))
    row = pl.BlockSpec((tm, d), lambda i: (i, 0))
    dx, sums = pl.pallas_call(
        body, grid=(l // tm,),
        in_specs=[row, vec, row],
        out_specs=[row, pl.BlockSpec((8, d), lambda i: (0, 0))],
        out_shape=[jax.ShapeDtypeStruct((l, d), F32), jax.ShapeDtypeStruct((8, d), F32)],
        compiler_params=_params("arbitrary"), name="final_loss")(x, g, target)
    return dx, sums


def _attn_geometry(i, t):
    nk = t + WINDOW
    qi = lax.broadcasted_iota(jnp.int32, (t, nk), 0)
    kj = lax.broadcasted_iota(jnp.int32, (t, nk), 1)
    dist = jnp.abs(qi + WINDOW - kj).astype(F32)
    qc = jnp.right_shift(qi, 6)
    kc = jnp.right_shift(kj, 6)
    valid = (kc >= qc) & (kc <= qc + WINDOW // CHUNK) & ((i > 0) | (kj >= WINDOW))
    return dist, valid


def _attn_head(q, k_all, v_all, sink, slope, dist, valid):
    s = _dot(q, k_all, NT) * (1.0 / math.sqrt(HEAD_DIM)) - slope * dist
    s = jnp.where(valid, s, NEG_INF)
    m = jnp.maximum(jnp.max(s, axis=-1, keepdims=True), sink)
    e = jnp.exp(s - m)
    es = jnp.exp(sink - m)
    inv = 1.0 / (jnp.sum(e, axis=-1, keepdims=True) + es)
    p = e * inv
    o = _dot(p, v_all, NN)
    return p, o, es * inv


def _attn_specs(t):
    cur = pl.BlockSpec((t, ATT_W * 2 + KV_W * 2), lambda i: (i, 0))
    halo_blocks = t // WINDOW
    prev = pl.BlockSpec((WINDOW, 2 * KV_W), lambda i: (jnp.maximum(i * halo_blocks - 1, 0), (2 * ATT_W) // (2 * KV_W)))
    return cur, prev


def _attn_fwd(pa, sinks, *, name, t=SEQ_BLOCK):
    l = pa.shape[0]
    t = min(t, l)

    def body(sink_ref, cur_ref, prev_ref, ya_ref):
        i = pl.program_id(0)
        dist, valid = _attn_geometry(i, t)
        for h in range(N_HEADS):
            kh = h // Q_PER_KV
            q = cur_ref[:, h * HEAD_DIM:(h + 1) * HEAD_DIM]
            z = cur_ref[:, ATT_W + h * HEAD_DIM:ATT_W + (h + 1) * HEAD_DIM]
            k_all = jnp.concatenate([prev_ref[:, kh * HEAD_DIM:(kh + 1) * HEAD_DIM],
                                     cur_ref[:, 2 * ATT_W + kh * HEAD_DIM:2 * ATT_W + (kh + 1) * HEAD_DIM]], axis=0)
            v_all = jnp.concatenate([prev_ref[:, KV_W + kh * HEAD_DIM:KV_W + (kh + 1) * HEAD_DIM],
                                     cur_ref[:, 2 * ATT_W + KV_W + kh * HEAD_DIM:2 * ATT_W + KV_W + (kh + 1) * HEAD_DIM]], axis=0)
            _, o, _ = _attn_head(q, k_all, v_all, sink_ref[h], 2.0 ** (-(h + 1)), dist, valid)
            sz, _ = _silu_and_grad(z)
            ya_ref[:, h * HEAD_DIM:(h + 1) * HEAD_DIM] = (o * sz).astype(BF16)

    cur, prev = _attn_specs(t)
    return pl.pallas_call(
        body, grid=(l // t,),
        in_specs=[pl.BlockSpec(memory_space=pltpu.SMEM), cur, prev],
        out_specs=pl.BlockSpec((t, ATT_W), lambda i: (i, 0)),
        out_shape=jax.ShapeDtypeStruct((l, ATT_W), BF16),
        compiler_params=_params("parallel"), name=name)(sinks, pa, pa)


def _attn_bwd(pa, sinks, dya, *, name, t=SEQ_BLOCK):
    l = pa.shape[0]
    t = min(t, l)
    nb = l // t
    scale = 1.0 / math.sqrt(HEAD_DIM)

    def body(sink_ref, cur_ref, prev_ref, dya_ref, dpa_ref, dsink_ref, carry_ref):
        n = pl.program_id(0)
        i = nb - 1 - n
        dist, valid = _attn_geometry(i, t)

        @pl.when(n == 0)
        def _():
            carry_ref[...] = jnp.zeros_like(carry_ref)
            dsink_ref[...] = jnp.zeros_like(dsink_ref)

        dk_acc = [jnp.zeros((t + WINDOW, HEAD_DIM), F32) for _ in range(N_KV_HEADS)]
        dv_acc = [jnp.zeros((t + WINDOW, HEAD_DIM), F32) for _ in range(N_KV_HEADS)]
        for h in range(N_HEADS):
            kh = h // Q_PER_KV
            q = cur_ref[:, h * HEAD_DIM:(h + 1) * HEAD_DIM]
            z = cur_ref[:, ATT_W + h * HEAD_DIM:ATT_W + (h + 1) * HEAD_DIM]
            k_all = jnp.concatenate([prev_ref[:, kh * HEAD_DIM:(kh + 1) * HEAD_DIM],
                                     cur_ref[:, 2 * ATT_W + kh * HEAD_DIM:2 * ATT_W + (kh + 1) * HEAD_DIM]], axis=0)
            v_all = jnp.concatenate([prev_ref[:, KV_W + kh * HEAD_DIM:KV_W + (kh + 1) * HEAD_DIM],
                                     cur_ref[:, 2 * ATT_W + KV_W + kh * HEAD_DIM:2 * ATT_W + KV_W + (kh + 1) * HEAD_DIM]], axis=0)
            p, o, p_sink = _attn_head(q, k_all, v_all, sink_ref[h], 2.0 ** (-(h + 1)), dist, valid)
            dy = dya_ref[:, h * HEAD_DIM:(h + 1) * HEAD_DIM]
            sz, dsz = _silu_and_grad(z)
            do = dy * sz
            dpa_ref[:, ATT_W + h * HEAD_DIM:ATT_W + (h + 1) * HEAD_DIM] = (dy * o * dsz).astype(BF16)
            delta = jnp.sum(do * o, axis=-1, keepdims=True)
            dp = _dot(do, v_all, NT)
            ds = p * (dp - delta)
            dpa_ref[:, h * HEAD_DIM:(h + 1) * HEAD_DIM] = (_dot(ds, k_all, NN) * scale).astype(BF16)
            dk_acc[kh] = dk_acc[kh] + _dot(ds, q, TN) * scale
            dv_acc[kh] = dv_acc[kh] + _dot(p, do, TN)
            dsink_ref[h:h + 1, :] += jnp.broadcast_to(-jnp.sum(p_sink * delta, axis=0, keepdims=True), (1, 128))

        for kh in range(N_KV_HEADS):
            for which, acc in ((0, dk_acc[kh]), (1, dv_acc[kh])):
                c0 = which * KV_W + kh * HEAD_DIM
                own = acc[WINDOW:, :]
                tail = own[t - WINDOW:, :] + carry_ref[:, c0:c0 + HEAD_DIM]
                dpa_ref[0:t - WINDOW, 2 * ATT_W + c0:2 * ATT_W + c0 + HEAD_DIM] = own[:t - WINDOW, :].astype(BF16)
                dpa_ref[t - WINDOW:t, 2 * ATT_W + c0:2 * ATT_W + c0 + HEAD_DIM] = tail.astype(BF16)
                carry_ref[:, c0:c0 + HEAD_DIM] = acc[:WINDOW, :]

    halo_blocks = t // WINDOW
    wpa = 2 * ATT_W + 2 * KV_W
    cur = pl.BlockSpec((t, wpa), lambda n: (nb - 1 - n, 0))
    prev = pl.BlockSpec((WINDOW, 2 * KV_W),
                        lambda n: (jnp.maximum((nb - 1 - n) * halo_blocks - 1, 0), (2 * ATT_W) // (2 * KV_W)))
    return pl.pallas_call(
        body, grid=(nb,),
        in_specs=[pl.BlockSpec(memory_space=pltpu.SMEM), cur, prev, pl.BlockSpec((t, ATT_W), lambda n: (nb - 1 - n, 0))],
        out_specs=[pl.BlockSpec((t, wpa), lambda n: (nb - 1 - n, 0)), pl.BlockSpec((8, 128), lambda n: (0, 0))],
        out_shape=[jax.ShapeDtypeStruct((l, wpa), BF16), jax.ShapeDtypeStruct((8, 128), F32)],
        scratch_shapes=[pltpu.VMEM((WINDOW, 2 * KV_W), F32)],
        compiler_params=_params("arbitrary"), name=name)(sinks, pa, pa, dya)


def _scan(xr, xi, lr, li, t, reverse):
    row = lax.broadcasted_iota(jnp.int32, (t, 1), 0)
    d = 1
    pr, pi = lr, li
    while d < t:
        if reverse:
            sr = jnp.where(row < t - d, pltpu.roll(xr, t - d, 0), 0.0)
            si = jnp.where(row < t - d, pltpu.roll(xi, t - d, 0), 0.0)
        else:
            sr = jnp.where(row >= d, pltpu.roll(xr, d, 0), 0.0)
            si = jnp.where(row >= d, pltpu.roll(xi, d, 0), 0.0)
        xr, xi = xr + pr * sr - pi * si, xi + pr * si + pi * sr
        pr, pi = pr * pr - pi * pi, 2.0 * pr * pi
        d *= 2
    return xr, xi


def _ssm_states(u, s0r, s0i, lr, li, bre, bim, t):
    row = lax.broadcasted_iota(jnp.int32, (t, 1), 0)
    xr = _dot(u, bre)
    xi = _dot(u, bim)
    first = row == 0
    xr = xr + jnp.where(first, lr * s0r - li * s0i, 0.0)
    xi = xi + jnp.where(first, lr * s0i + li * s0r, 0.0)
    return _scan(xr, xi, lr, li, t, False)


def _ssm_head(u, z, xr, xi, cre, cim, dskip, wglu, bglu):
    y = _dot(xr, cre) - _dot(xi, cim) + dskip * u
    y2, dgelu = _gelu_and_grad(y)
    gate = _sigmoid(_dot(y2, wglu) + bglu)
    y3 = y2 * gate
    return y2, dgelu, gate, y3


def _ssm_fwd(ps, lam, bblk, cblk, dskip, wglu, bglu, *, name, t=SEQ_BLOCK):
    l = ps.shape[0]
    t = min(t, l)
    nb = l // t

    def body(ps_ref, lam_ref, b_ref, c_ref, d_ref, w_ref, bg_ref, ys_ref, chk_ref, st_ref):
        @pl.when(pl.program_id(0) == 0)
        def _():
            st_ref[...] = jnp.zeros_like(st_ref)

        chk_ref[...] = jnp.broadcast_to(st_ref[...], chk_ref.shape)
        lr, li = lam_ref[0:1, :], lam_ref[1:2, :]
        u = ps_ref[:, :SSM_W]
        z = ps_ref[:, SSM_W:]
        xr, xi = _ssm_states(u, st_ref[:, :SSM_N], st_ref[:, SSM_N:], lr, li, b_ref[0], b_ref[1], t)
        st_ref[:, :SSM_N] = xr[t - 1:t, :]
        st_ref[:, SSM_N:] = xi[t - 1:t, :]
        _, _, _, y3 = _ssm_head(u, z, xr, xi, c_ref[0], c_ref[1], d_ref[...], w_ref[...], bg_ref[...])
        sz, _ = _silu_and_grad(z)
        ys_ref[...] = (y3 * sz).astype(BF16)

    full = lambda shape: pl.BlockSpec(shape, lambda i: (0,) * len(shape))
    return pl.pallas_call(
        body, grid=(nb,),
        in_specs=[pl.BlockSpec((t, 2 * SSM_W), lambda i: (i, 0)), full((8, SSM_N)), full((2, SSM_W, SSM_N)),
                  full((2, SSM_N, SSM_W)), full((1, SSM_W)), full((SSM_W, SSM_W)), full((1, SSM_W))],
        out_specs=[pl.BlockSpec((t, SSM_W), lambda i: (i, 0)), pl.BlockSpec((8, 2 * SSM_N), lambda i: (i, 0))],
        out_shape=[jax.ShapeDtypeStruct((l, SSM_W), BF16), jax.ShapeDtypeStruct((nb * 8, 2 * SSM_N), F32)],
        scratch_shapes=[pltpu.VMEM((1, 2 * SSM_N), F32)],
        compiler_params=_params("arbitrary"), name=name)(ps, lam, bblk, cblk, dskip, wglu, bglu)


def _ssm_bwd(ps, dys, chk, lam, bblk, cblk, dskip, wglu, bglu, *, name, t=SEQ_BLOCK):
    l = ps.shape[0]
    t = min(t, l)
    nb = l // t

    def body(ps_ref, dys_ref, chk_ref, lam_ref, b_ref, c_ref, d_ref, w_ref, bg_ref,
             dps_ref, db_ref, dc_ref, dw_ref, sums_ref, gc_ref, db_acc, dc_acc, dw_acc, sums_acc, sem):
        n = pl.program_id(0)

        @pl.when(n == 0)
        def _():
            gc_ref[...] = jnp.zeros_like(gc_ref)
            db_acc[...] = jnp.zeros_like(db_acc)
            dc_acc[...] = jnp.zeros_like(dc_acc)
            dw_acc[...] = jnp.zeros_like(dw_acc)
            sums_acc[...] = jnp.zeros_like(sums_acc)

        row = lax.broadcasted_iota(jnp.int32, (t, 1), 0)
        lr, li = lam_ref[0:1, :], lam_ref[1:2, :]
        u = ps_ref[:, :SSM_W]
        z = ps_ref[:, SSM_W:]
        s0r, s0i = chk_ref[0:1, :SSM_N], chk_ref[0:1, SSM_N:]
        xr, xi = _ssm_states(u, s0r, s0i, lr, li, b_ref[0], b_ref[1], t)
        dskip = d_ref[...]
        y2, dgelu, gate, y3 = _ssm_head(u, z, xr, xi, c_ref[0], c_ref[1], dskip, w_ref[...], bg_ref[...])
        sz, dsz = _silu_and_grad(z)
        dys_v = dys_ref[...]
        dps_ref[:, SSM_W:] = (dys_v * y3 * dsz).astype(BF16)
        dy3 = dys_v * sz
        da = dy3 * y2 * gate * (1.0 - gate)
        dy2 = dy3 * gate + _dot(da, w_ref[...], NT)
        dw_acc[...] += _dot(y2, da, TN)
        dy = dy2 * dgelu
        sums_acc[2:3, :SSM_W] += jnp.sum(dy * u, axis=0, keepdims=True)
        sums_acc[3:4, :SSM_W] += jnp.sum(da, axis=0, keepdims=True)
        dc_acc[0] += _dot(xr, dy, TN)
        dc_acc[1] += -_dot(xi, dy, TN)
        gr = _dot(dy, c_ref[0], NT)
        gi = -_dot(dy, c_ref[1], NT)
        last = row == t - 1
        gcr, gci = gc_ref[:, :SSM_N], gc_ref[:, SSM_N:]
        gr = gr + jnp.where(last, lr * gcr + li * gci, 0.0)
        gi = gi + jnp.where(last, lr * gci - li * gcr, 0.0)
        gr, gi = _scan(gr, gi, lr, -li, t, True)
        gc_ref[:, :SSM_N] = gr[0:1, :]
        gc_ref[:, SSM_N:] = gi[0:1, :]
        db_acc[0] += _dot(u, gr, TN)
        db_acc[1] += _dot(u, gi, TN)
        du = dskip * dy + _dot(gr, b_ref[0], NT) + _dot(gi, b_ref[1], NT)
        dps_ref[:, :SSM_W] = du.astype(BF16)
        spr = jnp.where(row == 0, s0r, pltpu.roll(xr, 1, 0))
        spi = jnp.where(row == 0, s0i, pltpu.roll(xi, 1, 0))
        sums_acc[0:1, :] += jnp.sum(gr * spr + gi * spi, axis=0, keepdims=True)
        sums_acc[1:2, :] += jnp.sum(gi * spr - gr * spi, axis=0, keepdims=True)

        @pl.when(n == nb - 1)
        def _():
            copies = [pltpu.make_async_copy(db_acc, db_ref, sem.at[0]),
                      pltpu.make_async_copy(dc_acc, dc_ref, sem.at[1]),
                      pltpu.make_async_copy(dw_acc, dw_ref, sem.at[2]),
                      pltpu.make_async_copy(sums_acc, sums_ref, sem.at[3])]
            for cp in copies:
                cp.start()
            for cp in copies:
                cp.wait()

    full = lambda shape: pl.BlockSpec(shape, lambda n: (0,) * len(shape))
    anyspec = pl.BlockSpec(memory_space=pl.ANY)
    return pl.pallas_call(
        body, grid=(nb,),
        in_specs=[pl.BlockSpec((t, 2 * SSM_W), lambda n: (nb - 1 - n, 0)),
                  pl.BlockSpec((t, SSM_W), lambda n: (nb - 1 - n, 0)),
                  pl.BlockSpec((8, 2 * SSM_N), lambda n: (nb - 1 - n, 0)),
                  full((8, SSM_N)), full((2, SSM_W, SSM_N)), full((2, SSM_N, SSM_W)), full((1, SSM_W)),
                  full((SSM_W, SSM_W)), full((1, SSM_W))],
        out_specs=[pl.BlockSpec((t, 2 * SSM_W), lambda n: (nb - 1 - n, 0)), anyspec, anyspec, anyspec, anyspec],
        out_shape=[jax.ShapeDtypeStruct((l, 2 * SSM_W), BF16),
                   jax.ShapeDtypeStruct((2, SSM_W, SSM_N), F32),
                   jax.ShapeDtypeStruct((2, SSM_N, SSM_W), F32),
                   jax.ShapeDtypeStruct((SSM_W, SSM_W), F32),
                   jax.ShapeDtypeStruct((8, SSM_N), F32)],
        scratch_shapes=[pltpu.VMEM((1, 2 * SSM_N), F32), pltpu.VMEM((2, SSM_W, SSM_N), F32),
                        pltpu.VMEM((2, SSM_N, SSM_W), F32), pltpu.VMEM((SSM_W, SSM_W), F32),
                        pltpu.VMEM((8, SSM_N), F32), pltpu.SemaphoreType.DMA((4,))],
        compiler_params=_params("arbitrary"), name=name)(ps, dys, chk, lam, bblk, cblk, dskip, wglu, bglu)


def _pool_count(i, t):
    pos = lax.broadcasted_iota(jnp.int32, (t, POOL_W), 0) + i * t + 1
    col = lax.broadcasted_iota(jnp.int32, (t, POOL_W), 1)
    win = jnp.where(col < POOL_GW, 2, jnp.where(col < 2 * POOL_GW, 4, jnp.where(col < 3 * POOL_GW, 8, 16)))
    return 1.0 / jnp.minimum(pos, win).astype(F32), col


def _window_sums(ext, n_rows, forward):
    col = lax.broadcasted_iota(jnp.int32, ext.shape, 1)
    sh = (lambda a, d: pltpu.roll(a, d, 0)) if forward else (lambda a, d: pltpu.roll(a, n_rows - d, 0))
    a2 = ext + sh(ext, 1)
    a4 = a2 + sh(a2, 2)
    a8 = a4 + sh(a4, 4)
    a16 = a8 + sh(a8, 8)
    return jnp.where(col < POOL_GW, a2, jnp.where(col < 2 * POOL_GW, a4, jnp.where(col < 3 * POOL_GW, a8, a16)))


def _pool_mix(pooled, wp_ref):
    return jnp.concatenate([_dot(pooled[:, g * POOL_GW:(g + 1) * POOL_GW], wp_ref[g]) for g in range(4)], axis=1)


def _pool_pooled(i, cur_u, prev_u, t):
    prev = jnp.where(i > 0, prev_u, 0.0)
    ext = jnp.concatenate([prev, cur_u], axis=0)
    inv_cnt, _ = _pool_count(i, t)
    return _window_sums(ext, t + POOL_HALO, True)[POOL_HALO:, :] * inv_cnt - cur_u


def _pool_fwd(pp, wpool, pscale, *, name, t=SEQ_BLOCK):
    l = pp.shape[0]
    t = min(t, l)

    def body(cur_ref, prev_ref, wp_ref, sc_ref, yp_ref):
        i = pl.program_id(0)
        pooled = _pool_pooled(i, cur_ref[:, :POOL_W], prev_ref[...], t)
        lin = _pool_mix(pooled, wp_ref)
        sz, _ = _silu_and_grad(cur_ref[:, POOL_W:])
        yp_ref[...] = (lin * sc_ref[...] * sz).astype(BF16)

    hb = t // POOL_HALO
    return pl.pallas_call(
        body, grid=(l // t,),
        in_specs=[pl.BlockSpec((t, 2 * POOL_W), lambda i: (i, 0)),
                  pl.BlockSpec((POOL_HALO, POOL_W), lambda i: (jnp.maximum(i * hb - 1, 0), 0)),
                  pl.BlockSpec((4, POOL_GW, POOL_GW), lambda i: (0, 0, 0)),
                  pl.BlockSpec((1, POOL_W), lambda i: (0, 0))],
        out_specs=pl.BlockSpec((t, POOL_W), lambda i: (i, 0)),
        out_shape=jax.ShapeDtypeStruct((l, POOL_W), BF16),
        compiler_params=_params("parallel"), name=name)(pp, pp, wpool, pscale)


def _pool_bwd(pp, dyp, wpool, pscale, *, name, t=SEQ_BLOCK):
    l = pp.shape[0]
    t = min(t, l)
    nb = l // t

    def body(cur_ref, prev_ref, dyp_ref, wp_ref, sc_ref, dpp_ref, dwp_ref, sums_ref, carry_ref):
        n = pl.program_id(0)
        i = nb - 1 - n

        @pl.when(n == 0)
        def _():
            carry_ref[...] = jnp.zeros_like(carry_ref)
            dwp_ref[...] = jnp.zeros_like(dwp_ref)
            sums_ref[...] = jnp.zeros_like(sums_ref)

        cur_u = cur_ref[:, :POOL_W]
        pooled = _pool_pooled(i, cur_u, prev_ref[...], t)
        lin = _pool_mix(pooled, wp_ref)
        sz, dsz = _silu_and_grad(cur_ref[:, POOL_W:])
        dyp_v = dyp_ref[...]
        scale = sc_ref[...]
        dpp_ref[:, POOL_W:] = (dyp_v * lin * scale * dsz).astype(BF16)
        dpre = dyp_v * sz
        sums_ref[0:1, :] += jnp.sum(dpre * lin, axis=0, keepdims=True)
        dlin = dpre * scale
        dpooled = []
        for g in range(4):
            dl = dlin[:, g * POOL_GW:(g + 1) * POOL_GW]
            dwp_ref[g] += _dot(pooled[:, g * POOL_GW:(g + 1) * POOL_GW], dl, TN)
            dpooled.append(_dot(dl, wp_ref[g], NT))
        dpooled = jnp.concatenate(dpooled, axis=1)
        inv_cnt, _ = _pool_count(i, t)
        dq = dpooled * inv_cnt
        ext = jnp.concatenate([dq, carry_ref[...]], axis=0)
        du = _window_sums(ext, t + POOL_HALO, False)[:t, :] - dpooled
        dpp_ref[:, :POOL_W] = du.astype(BF16)
        carry_ref[...] = dq[:POOL_HALO, :]

    hb = t // POOL_HALO
    return pl.pallas_call(
        body, grid=(nb,),
        in_specs=[pl.BlockSpec((t, 2 * POOL_W), lambda n: (nb - 1 - n, 0)),
                  pl.BlockSpec((POOL_HALO, POOL_W), lambda n: (jnp.maximum((nb - 1 - n) * hb - 1, 0), 0)),
                  pl.BlockSpec((t, POOL_W), lambda n: (nb - 1 - n, 0)),
                  pl.BlockSpec((4, POOL_GW, POOL_GW), lambda n: (0, 0, 0)),
                  pl.BlockSpec((1, POOL_W), lambda n: (0, 0))],
        out_specs=[pl.BlockSpec((t, 2 * POOL_W), lambda n: (nb - 1 - n, 0)),
                   pl.BlockSpec((4, POOL_GW, POOL_GW), lambda n: (0, 0, 0)),
                   pl.BlockSpec((8, POOL_W), lambda n: (0, 0))],
        out_shape=[jax.ShapeDtypeStruct((l, 2 * POOL_W), BF16), jax.ShapeDtypeStruct((4, POOL_GW, POOL_GW), F32),
                   jax.ShapeDtypeStruct((8, POOL_W), F32)],
        scratch_shapes=[pltpu.VMEM((POOL_HALO, POOL_W), F32)],
        compiler_params=_params("arbitrary"), name=name)(pp, pp, dyp, wpool, pscale)


def _merge_fwd(ya, ys, yp, wa, ws, wp, pg, *, name, tm=256):
    l = ya.shape[0]
    tm = min(tm, l)
    d = D_MODEL

    def body(ya_ref, ys_ref, yp_ref, wa_ref, ws_ref, wp_ref, pg_ref, mg_ref, ba_ref, bs_ref, bp_ref):
        acc = None
        for k, (y_ref, w_ref, b_ref) in enumerate(((ya_ref, wa_ref, ba_ref), (ys_ref, ws_ref, bs_ref),
                                                   (yp_ref, wp_ref, bp_ref))):
            br = _dot(y_ref[...], w_ref[...])
            b_ref[...] = br
            term = _sigmoid(pg_ref[:, k * d:(k + 1) * d]) * br
            acc = term if acc is None else acc + term
        mg_ref[...] = acc.astype(BF16)

    rowy = pl.BlockSpec((tm, ATT_W), lambda i: (i, 0))
    wsp = pl.BlockSpec((ATT_W, d), lambda i: (0, 0))
    rowd = pl.BlockSpec((tm, d), lambda i: (i, 0))
    return pl.pallas_call(
        body, grid=(l // tm,),
        in_specs=[rowy, rowy, rowy, wsp, wsp, wsp, pl.BlockSpec((tm, 3 * d), lambda i: (i, 0))],
        out_specs=[rowd, rowd, rowd, rowd],
        out_shape=[jax.ShapeDtypeStruct((l, d), BF16)] + [jax.ShapeDtypeStruct((l, d), F32)] * 3,
        compiler_params=_params("parallel"), name=name)(ya, ys, yp, wa, ws, wp, pg)


def _out_fwd(merged, wout, x, gate, *, name, tm=512):
    l, d = x.shape
    tm = min(tm, l)

    def body(m_ref, w_ref, x_ref, g_ref, xn_ref, out_ref):
        out = _dot(m_ref[...], w_ref[...])
        out_ref[...] = out
        xn_ref[...] = x_ref[...] + g_ref[...] * out

    row = pl.BlockSpec((tm, d), lambda i: (i, 0))
    return pl.pallas_call(
        body, grid=(l // tm,),
        in_specs=[row, pl.BlockSpec((d, d), lambda i: (0, 0)), row, pl.BlockSpec((1, d), lambda i: (0, 0))],
        out_specs=[row, row],
        out_shape=[jax.ShapeDtypeStruct((l, d), F32)] * 2,
        compiler_params=_params("parallel"), name=name)(merged, wout, x, gate)


def _merge_bwd(dx, out, gate, wout, pg, ba, bs, bp, *, name, tm=256):
    l, d = dx.shape
    tm = min(tm, l)

    def body(dx_ref, out_ref, g_ref, w_ref, pg_ref, ba_ref, bs_ref, bp_ref,
             dmo_ref, dba_ref, dbs_ref, dbp_ref, dpg_ref, sums_ref):
        @pl.when(pl.program_id(0) == 0)
        def _():
            sums_ref[...] = jnp.zeros_like(sums_ref)

        dxv = dx_ref[...]
        sums_ref[0:1, :] += jnp.sum(dxv * out_ref[...], axis=0, keepdims=True)
        dmo = (dxv * g_ref[...]).astype(BF16)
        dmo_ref[...] = dmo
        dmerged = _dot(dmo, w_ref[...], NT)
        for k, (b_ref, db_ref) in enumerate(((ba_ref, dba_ref), (bs_ref, dbs_ref), (bp_ref, dbp_ref))):
            gk = _sigmoid(pg_ref[:, k * d:(k + 1) * d])
            db_ref[...] = (dmerged * gk).astype(BF16)
            dpg_ref[:, k * d:(k + 1) * d] = (dmerged * b_ref[...] * gk * (1.0 - gk)).astype(BF16)

    row = pl.BlockSpec((tm, d), lambda i: (i, 0))
    wide = pl.BlockSpec((tm, 3 * d), lambda i: (i, 0))
    return pl.pallas_call(
        body, grid=(l // tm,),
        in_specs=[row, row, pl.BlockSpec((1, d), lambda i: (0, 0)), pl.BlockSpec((d, d), lambda i: (0, 0)),
                  wide, row, row, row],
        out_specs=[row, row, row, row, wide, pl.BlockSpec((8, d), lambda i: (0, 0))],
        out_shape=[jax.ShapeDtypeStruct((l, d), BF16)] * 4 + [jax.ShapeDtypeStruct((l, 3 * d), BF16),
                                                             jax.ShapeDtypeStruct((8, d), F32)],
        compiler_params=_params("arbitrary"), name=name)(dx, out, gate, wout, pg, ba, bs, bp)


def _adamw(w, g, m, v, *, name, tr=256):
    r, c = w.shape
    p = g.shape[0]
    tr = min(tr, r)
    assert r % tr == 0
    c1 = 1.0 / (1.0 - ADAM_B1 ** ADAM_STEP)
    c2 = 1.0 / (1.0 - ADAM_B2 ** ADAM_STEP)

    def body(w_ref, g_ref, m_ref, v_ref, go_ref, d_ref, mo_ref, vo_ref):
        gv = g_ref[0].astype(F32)
        for k in range(1, p):
            gv = gv + g_ref[k].astype(F32)
        go_ref[...] = gv
        mn = ADAM_B1 * m_ref[...] + (1.0 - ADAM_B1) * gv
        vn = ADAM_B2 * v_ref[...] + (1.0 - ADAM_B2) * (gv * gv)
        mo_ref[...] = mn
        vo_ref[...] = vn
        d_ref[...] = -ADAM_LR * ((mn * c1) / (jnp.sqrt(vn * c2) + ADAM_EPS) + ADAM_WD * w_ref[...])

    row = pl.BlockSpec((tr, c), lambda i: (i, 0))
    return pl.pallas_call(
        body, grid=(r // tr,),
        in_specs=[row, pl.BlockSpec((p, tr, c), lambda i: (0, i, 0)), row, row],
        out_specs=[row] * 4,
        out_shape=[jax.ShapeDtypeStruct((r, c), F32)] * 4,
        compiler_params=_params("parallel"), name=name)(w, g, m, v)


def _sum_parts(parts, *, name):
    p, r, c = parts.shape
    tr = 8
    for cand in (512, 256, 128, 64, 32, 16):
        if r % cand == 0:
            tr = cand
            break

    def body(p_ref, o_ref):
        acc = p_ref[0]
        for k in range(1, p):
            acc = acc + p_ref[k]
        o_ref[...] = acc

    return pl.pallas_call(
        body, grid=(r // tr,),
        in_specs=[pl.BlockSpec((p, tr, c), lambda i: (0, i, 0))],
        out_specs=pl.BlockSpec((tr, c), lambda i: (i, 0)),
        out_shape=jax.ShapeDtypeStruct((r, c), F32),
        compiler_params=_params("parallel"), name=name)(parts)


def _exchange(arrs, *, scatter, name):
    n = len(arrs)
    out_shape = [jax.ShapeDtypeStruct(a.shape if scatter else (N_DEV,) + a.shape, a.dtype) for a in arrs]

    def body(*refs):
        ins, outs = refs[:n], refs[n:2 * n]
        send_sems, recv_sems, loc_sems = refs[2 * n:]
        me = 4 * lax.axis_index("x") + 2 * lax.axis_index("y") + lax.axis_index("c")
        local = []
        for k in range(n):
            src = ins[k].at[me] if scatter else ins[k]
            cp = pltpu.make_async_copy(src, outs[k].at[me], loc_sems.at[k])
            cp.start()
            local.append(cp)
        remote = []
        for r in range(1, N_DEV):
            peer = me ^ r
            for k in range(n):
                src = ins[k].at[peer] if scatter else ins[k]
                cp = pltpu.make_async_remote_copy(
                    src_ref=src, dst_ref=outs[k].at[me], send_sem=send_sems.at[k, r - 1], recv_sem=recv_sems.at[k, r - 1],
                    device_id=(peer // 4, (peer // 2) % 2, peer % 2), device_id_type=pl.DeviceIdType.MESH)
                cp.start()
                remote.append(cp)
        for cp in remote:
            cp.wait()
        for cp in local:
            cp.wait()

    anyspec = pl.BlockSpec(memory_space=pl.ANY)
    return pl.pallas_call(
        body, in_specs=[anyspec] * n, out_specs=[anyspec] * n, out_shape=out_shape,
        scratch_shapes=[pltpu.SemaphoreType.DMA((n, N_DEV - 1)), pltpu.SemaphoreType.DMA((n, N_DEV - 1)),
                        pltpu.SemaphoreType.DMA((n,))],
        name=name)(*arrs)


def _ssm_discretize(a_re, a_im, log_dt, b_re, b_im):
    dt = jnp.exp(log_dt)[:, None]
    mag = jnp.exp(a_re * dt)
    lr = mag * jnp.cos(a_im * dt)
    li = mag * jnp.sin(a_im * dt)
    den = a_re * a_re + a_im * a_im
    cr = ((lr - 1.0) * a_re + li * a_im) / den
    ci = (li * a_re - (lr - 1.0) * a_im) / den
    bbr = cr[..., None] * b_re - ci[..., None] * b_im
    bbi = cr[..., None] * b_im + ci[..., None] * b_re
    return lr, li, bbr, bbi


def _ssm_dense(lr, li, bbr, bbi, c_re, c_im):
    eye = jnp.eye(SSM_GROUPS, dtype=F32)
    lam = jnp.zeros((8, SSM_N), F32).at[0].set(lr.reshape(-1)).at[1].set(li.reshape(-1))
    bblk = jnp.stack([jnp.einsum('gpc,gh->gchp', b, eye).reshape(SSM_W, SSM_N) for b in (bbr, bbi)]).astype(BF16)
    cblk = jnp.stack([jnp.einsum('gcp,gh->gphc', c, eye).reshape(SSM_N, SSM_W) for c in (c_re, c_im)]).astype(BF16)
    return lam, bblk, cblk


def _ssm_extract(db_dense, dc_dense, sums):
    eye = jnp.eye(SSM_GROUPS, dtype=F32)
    db = [jnp.einsum('gchp,gh->gpc', db_dense[k].reshape(SSM_GROUPS, SSM_GROUP, SSM_GROUPS, SSM_STATE), eye) for k in (0, 1)]
    dc = [jnp.einsum('gphc,gh->gcp', dc_dense[k].reshape(SSM_GROUPS, SSM_STATE, SSM_GROUPS, SSM_GROUP), eye) for k in (0, 1)]
    dlr = sums[0].reshape(SSM_GROUPS, SSM_STATE)
    dli = sums[1].reshape(SSM_GROUPS, SSM_STATE)
    return dlr, dli, db[0], db[1], dc[0], dc[1]


IN_SPLITS = (ATT_W, KV_W, KV_W, SSM_W, POOL_W, ATT_W, SSM_W, POOL_W, 3 * D_MODEL)


def _split_w_in(w):
    idx = [0]
    for s in IN_SPLITS:
        idx.append(idx[-1] + s)
    seg = [w[..., idx[k]:idx[k + 1]] for k in range(len(IN_SPLITS))]
    q, k, v, us, up, za, zs, zp, gl = seg
    return (jnp.concatenate([q, za, k, v], axis=-1), jnp.concatenate([us, zs], axis=-1),
            jnp.concatenate([up, zp], axis=-1), gl)


def _merge_w_in(da, ds, dp, dg):
    q, za, k, v = da[..., :ATT_W], da[..., ATT_W:2 * ATT_W], da[..., 2 * ATT_W:2 * ATT_W + KV_W], da[..., 2 * ATT_W + KV_W:]
    us, zs = ds[..., :SSM_W], ds[..., SSM_W:]
    up, zp = dp[..., :POOL_W], dp[..., POOL_W:]
    return jnp.concatenate([q, k, v, us, up, za, zs, zp, dg], axis=-1)


def _layer_fwd(x, lw, li):
    tag = f"l{li}"
    h = _ln_fwd(x, lw["norm_g"], lw["shift"], lw["scale"], name=f"ln_fwd_{tag}")
    pa = _mm(h, lw["w_a"], tn=640, name=f"proj_a_{tag}")
    ps = _mm(h, lw["w_s"], name=f"proj_s_{tag}")
    pp = _mm(h, lw["w_p"], name=f"proj_p_{tag}")
    pg = _mm(h, lw["w_g"], name=f"proj_g_{tag}")
    ya = _attn_fwd(pa, lw["sinks"], name=f"attn_fwd_{tag}")
    ys, chk = _ssm_fwd(ps, lw["lam"], lw["bblk"], lw["cblk"], lw["ssm_d"], lw["w_glu"], lw["b_glu"], name=f"ssm_fwd_{tag}")
    yp = _pool_fwd(pp, lw["w_pool"], lw["pool_scale"], name=f"pool_fwd_{tag}")
    merged, ba, bs, bp = _merge_fwd(ya, ys, yp, lw["w_br_att"], lw["w_br_ssm"], lw["w_br_pool"], pg, name=f"merge_fwd_{tag}")
    x_new, out = _out_fwd(merged, lw["w_out"], x, lw["gate"], name=f"out_fwd_{tag}")
    saved = dict(x=x, h=h, pa=pa, ps=ps, pp=pp, pg=pg, ya=ya, ys=ys, yp=yp, chk=chk, merged=merged,
                 ba=ba, bs=bs, bp=bp, out=out)
    return x_new, saved


def _layer_bwd(dx, lw, sv, li):
    tag = f"l{li}"
    dmo, dba, dbs, dbp, dpg, gate_sums = _merge_bwd(dx, sv["out"], lw["gate"], lw["w_out"], sv["pg"],
                                                    sv["ba"], sv["bs"], sv["bp"], name=f"merge_bwd_{tag}")
    g = {}
    g["w_out"] = _mm_tn(sv["merged"], dmo, out_dtype=BF16, name=f"dw_out_{tag}")
    dya = _mm(dba, lw["w_br_att"], nt=True, name=f"dy_att_{tag}")
    dys = _mm(dbs, lw["w_br_ssm"], nt=True, name=f"dy_ssm_{tag}")
    dyp = _mm(dbp, lw["w_br_pool"], nt=True, name=f"dy_pool_{tag}")
    g["w_br_att"] = _mm_tn(sv["ya"], dba, out_dtype=BF16, name=f"dw_br_att_{tag}")
    g["w_br_ssm"] = _mm_tn(sv["ys"], dbs, out_dtype=BF16, name=f"dw_br_ssm_{tag}")
    g["w_br_pool"] = _mm_tn(sv["yp"], dbp, out_dtype=BF16, name=f"dw_br_pool_{tag}")
    dpa, dsink = _attn_bwd(sv["pa"], lw["sinks"], dya, name=f"attn_bwd_{tag}")
    dps, db_dense, dc_dense, dwglu, ssm_sums = _ssm_bwd(
        sv["ps"], dys, sv["chk"], lw["lam"], lw["bblk"], lw["cblk"], lw["ssm_d"], lw["w_glu"], lw["b_glu"],
        name=f"ssm_bwd_{tag}")
    dpp, dwpool, pool_sums = _pool_bwd(sv["pp"], dyp, lw["w_pool"], lw["pool_scale"], name=f"pool_bwd_{tag}")
    dh = _mm_nt_sum([(dpa, lw["w_a"]), (dps, lw["w_s"]), (dpp, lw["w_p"]), (dpg, lw["w_g"])], name=f"dh_{tag}")
    h = sv["h"]
    g["w_in"] = _merge_w_in(_mm_tn(h, dpa, out_dtype=BF16, tn=640, name=f"dw_a_{tag}"),
                            _mm_tn(h, dps, out_dtype=BF16, name=f"dw_s_{tag}"),
                            _mm_tn(h, dpp, out_dtype=BF16, name=f"dw_p_{tag}"),
                            _mm_tn(h, dpg, out_dtype=BF16, name=f"dw_g_{tag}"))
    dx_in, ln_sums = _ln_bwd(sv["x"], dh, dx, lw["norm_g"], lw["scale"], name=f"ln_bwd_{tag}")
    g["w_glu"] = dwglu.astype(BF16)
    g["dmod"] = jnp.concatenate([ln_sums[0], ln_sums[1], gate_sums[0]])
    g["norm_g"] = ln_sums[2]
    g["attn_sinks"] = dsink[:, 0]
    g["ssm_raw"] = _ssm_extract(db_dense, dc_dense, ssm_sums)
    g["ssm_d"] = ssm_sums[2, :SSM_W]
    g["b_glu"] = ssm_sums[3, :SSM_W]
    g["w_pool"] = dwpool
    g["pool_scale"] = pool_sums[0]
    return dx_in, g


def _local_step(xs, target, layers, final_g):
    saved = []
    for li in range(DEPTH):
        xs, sv = _layer_fwd(xs, layers[li], li)
        saved.append(sv)
    dx, fin_sums = _final_loss(xs, final_g[None, :], target)
    grads = [None] * DEPTH
    for li in reversed(range(DEPTH)):
        dx, grads[li] = _layer_bwd(dx, layers[li], saved[li], li)
    return dx, fin_sums, grads


def _prepare_layer(li, mod, norm_g, w_a, w_s, w_p, w_g, attn_sinks, disc, ssm_c_re, ssm_c_im, ssm_d, w_glu_f, b_glu,
                   w_pool, pool_scale, w_ba_f, w_bs_f, w_bp_f, w_out_f):
    d = D_MODEL
    lr, li_, bbr, bbi = disc
    lam, bblk, cblk = _ssm_dense(lr[li], li_[li], bbr[li], bbi[li], ssm_c_re[li], ssm_c_im[li])
    return dict(
        norm_g=norm_g[li][None, :], shift=mod[li, :d][None, :], scale=mod[li, d:2 * d][None, :],
        gate=mod[li, 2 * d:][None, :], w_a=w_a[li], w_s=w_s[li], w_p=w_p[li], w_g=w_g[li],
        sinks=attn_sinks[li], lam=lam, bblk=bblk, cblk=cblk, ssm_d=ssm_d[li][None, :], w_glu=w_glu_f[li],
        b_glu=b_glu[li][None, :], w_pool=w_pool[li].astype(BF16), pool_scale=pool_scale[li][None, :],
        w_br_att=w_ba_f[li], w_br_ssm=w_bs_f[li], w_br_pool=w_bp_f[li], w_out=w_out_f[li])


SMALL_ROWS = 64
SMALL_ORDER = ("norm_g", "attn_sinks", "ssm_d", "b_glu", "w_pool", "pool_scale", "dmod")


def _pack_small(loss, dfinal_g, layer_grads):
    parts = [jnp.broadcast_to(loss.reshape(1), (128,)), dfinal_g]
    for g in layer_grads:
        for k in SMALL_ORDER:
            v = g[k].reshape(-1)
            if v.shape[0] % 128:
                v = jnp.pad(v, (0, 128 - v.shape[0] % 128))
            parts.append(v)
        for v in g["ssm_raw"]:
            parts.append(v.reshape(-1))
    flat = jnp.concatenate(parts)
    return jnp.pad(flat, (0, (-flat.shape[0]) % (SMALL_ROWS * 128))).reshape(-1, 128)


def _unpack_small(flat, shapes):
    out, off = [], 0
    for s in shapes:
        n = int(math.prod(s))
        out.append(flat[off:off + n].reshape(s))
        off += n + (-n) % 128
    return out


def kernel(x, c, norm_g, w_ada, b_ada, w_in, attn_sinks, ssm_a_re, ssm_a_im, ssm_log_dt, ssm_b_re, ssm_b_im, ssm_c_re, ssm_c_im, ssm_d, w_glu, b_glu, w_pool, pool_scale, w_br_att, w_br_ssm, w_br_pool, w_out, final_g, loss_target, m_norm_g, m_w_ada, m_b_ada, m_w_in, m_attn_sinks, m_ssm_a_re, m_ssm_a_im, m_ssm_log_dt, m_ssm_b_re, m_ssm_b_im, m_ssm_c_re, m_ssm_c_im, m_ssm_d, m_w_glu, m_b_glu, m_w_pool, m_pool_scale, m_w_br_att, m_w_br_ssm, m_w_br_pool, m_w_out, m_final_g, v_norm_g, v_w_ada, v_b_ada, v_w_in, v_attn_sinks, v_ssm_a_re, v_ssm_a_im, v_ssm_log_dt, v_ssm_b_re, v_ssm_b_im, v_ssm_c_re, v_ssm_c_im, v_ssm_d, v_w_glu, v_b_glu, v_w_pool, v_pool_scale, v_w_br_att, v_w_br_ssm, v_w_br_pool, v_w_out, v_final_g):
    me = 4 * lax.axis_index("x") + 2 * lax.axis_index("y") + lax.axis_index("c")
    d = D_MODEL
    ada_w = 3 * d // N_DEV

    (c_all,) = _exchange([c.reshape(8, 128)], scatter=False, name="gather_c")
    c_act = jax.nn.silu(c_all.reshape(N_DEV, d))
    b_cols = lax.dynamic_slice(b_ada, (0, me * ada_w), (DEPTH, ada_w))
    mod_part = jnp.concatenate(
        [_mm(c_act, w_ada[li], name=f"ada_fwd_l{li}") + b_cols[li][None, :] for li in range(DEPTH)], axis=0)
    (mod_all,) = _exchange([mod_part], scatter=False, name="gather_mod")
    mod_all = mod_all.reshape(N_DEV, DEPTH, N_DEV, ada_w)
    mod_mine = lax.dynamic_index_in_dim(mod_all, me, axis=2, keepdims=False)
    mod_mine = mod_mine.transpose(1, 0, 2).reshape(DEPTH, 3 * d)

    gathered = _exchange([w_in.astype(BF16), w_glu.astype(BF16), w_br_att.astype(BF16), w_br_ssm.astype(BF16),
                          w_br_pool.astype(BF16), w_out.astype(BF16)], scatter=False, name="gather_weights")
    cols = lambda g: g.transpose(1, 2, 0, 3).reshape(g.shape[1], g.shape[2], N_DEV * g.shape[3])
    rows = lambda g: g.transpose(1, 0, 2, 3).reshape(g.shape[1], N_DEV * g.shape[2], g.shape[3])
    w_in_f, w_glu_f, w_ba_f, w_bs_f, w_bp_f, w_out_f = (cols(gathered[0]), rows(gathered[1]), cols(gathered[2]),
                                                        cols(gathered[3]), cols(gathered[4]), rows(gathered[5]))
    w_a, w_s, w_p, w_g = _split_w_in(w_in_f)

    disc, disc_vjp = jax.vjp(jax.vmap(_ssm_discretize), ssm_a_re, ssm_a_im, ssm_log_dt, ssm_b_re, ssm_b_im)
    layers = [_prepare_layer(li, mod_mine, norm_g, w_a, w_s, w_p, w_g, attn_sinks, disc, ssm_c_re, ssm_c_im, ssm_d,
                             w_glu_f, b_glu, w_pool, pool_scale, w_ba_f, w_bs_f, w_bp_f, w_out_f) for li in range(DEPTH)]

    dx, fin_sums, grads = _local_step(x[0], loss_target[0], layers, final_g)
    loss_part = jnp.sum(fin_sums[1])
    grad_x = dx[None]

    stack = lambda k: jnp.stack([grads[li][k] for li in range(DEPTH)])
    to_cols = lambda g: g.reshape(g.shape[0], g.shape[1], N_DEV, g.shape[2] // N_DEV).transpose(2, 0, 1, 3)
    to_rows = lambda g: g.reshape(g.shape[0], N_DEV, g.shape[1] // N_DEV, g.shape[2]).transpose(1, 0, 2, 3)
    big = _exchange([to_cols(stack("w_in")), to_rows(stack("w_glu")), to_cols(stack("w_br_att")),
                     to_cols(stack("w_br_ssm")), to_cols(stack("w_br_pool")), to_rows(stack("w_out"))],
                    scatter=True, name="scatter_grads")

    small = _pack_small(loss_part, fin_sums[0], grads)
    (small_all,) = _exchange([small], scatter=False, name="gather_small")
    out = {}

    def adam(name, w, g_parts, m, v):
        shp = w.shape
        r = int(math.prod(shp[:-1])) if len(shp) > 1 else 1
        w2, m2, v2 = (a.reshape(r, shp[-1]) for a in (w, m, v))
        g2 = g_parts.reshape(g_parts.shape[0], r, shp[-1])
        res = _adamw(w2, g2, m2, v2, name=f"adamw_{name}")
        out[name] = tuple(a.reshape(shp) for a in res)

    flat = _sum_parts(small_all, name="sum_small").reshape(-1)
    shapes = [(128,), (d,)]
    for _ in range(DEPTH):
        shapes += [(d,), (N_HEADS,), (SSM_W,), (SSM_W,), (4, POOL_GW, POOL_GW), (POOL_W,), (3 * d,),
                   (SSM_GROUPS, SSM_STATE), (SSM_GROUPS, SSM_STATE), (SSM_GROUPS, SSM_STATE, SSM_GROUP),
                   (SSM_GROUPS, SSM_STATE, SSM_GROUP), (SSM_GROUPS, SSM_GROUP, SSM_STATE), (SSM_GROUPS, SSM_GROUP, SSM_STATE)]
    un = _unpack_small(flat, shapes)
    loss = un[0][0]
    g_final_g = un[1]
    per = 13
    gl = [un[2 + li * per: 2 + (li + 1) * per] for li in range(DEPTH)]
    st = lambda j: jnp.stack([gl[li][j] for li in range(DEPTH)])
    g_norm_g, g_sinks, g_ssm_d, g_b_glu, g_w_pool, g_pool_scale, g_b_ada = (st(j) for j in range(7))
    d_lr, d_li, d_bbr, d_bbi, g_c_re, g_c_im = (st(j) for j in range(7, 13))
    g_a_re, g_a_im, g_log_dt, g_b_re, g_b_im = disc_vjp((d_lr, d_li, d_bbr, d_bbi))

    dmod_all = jnp.stack([lax.dynamic_slice(
        small_all.reshape(N_DEV, -1)[:, _small_offset(shapes, 2 + li * per + 6):][:, :3 * d], (0, me * ada_w), (N_DEV, ada_w))
        for li in range(DEPTH)])
    g_w_ada = jnp.stack([_mm_tn(c_act, dmod_all[li], tm=d, tn=ada_w, tk=N_DEV, name=f"dw_ada_l{li}") for li in range(DEPTH)])

    adam("w_ada", w_ada, g_w_ada[None], m_w_ada, v_w_ada)
    adam("w_in", w_in, big[0], m_w_in, v_w_in)
    adam("w_glu", w_glu, big[1], m_w_glu, v_w_glu)
    adam("w_br_att", w_br_att, big[2], m_w_br_att, v_w_br_att)
    adam("w_br_ssm", w_br_ssm, big[3], m_w_br_ssm, v_w_br_ssm)
    adam("w_br_pool", w_br_pool, big[4], m_w_br_pool, v_w_br_pool)
    adam("w_out", w_out, big[5], m_w_out, v_w_out)

    small_names = ["norm_g", "b_ada", "attn_sinks", "ssm_a_re", "ssm_a_im", "ssm_log_dt", "ssm_b_re", "ssm_b_im",
                   "ssm_c_re", "ssm_c_im", "ssm_d", "b_glu", "w_pool", "pool_scale", "final_g"]
    small_w = [norm_g, b_ada, attn_sinks, ssm_a_re, ssm_a_im, ssm_log_dt, ssm_b_re, ssm_b_im, ssm_c_re, ssm_c_im,
               ssm_d, b_glu, w_pool, pool_scale, final_g]
    small_m = [m_norm_g, m_b_ada, m_attn_sinks, m_ssm_a_re, m_ssm_a_im, m_ssm_log_dt, m_ssm_b_re, m_ssm_b_im,
               m_ssm_c_re, m_ssm_c_im, m_ssm_d, m_b_glu, m_w_pool, m_pool_scale, m_final_g]
    small_v = [v_norm_g, v_b_ada, v_attn_sinks, v_ssm_a_re, v_ssm_a_im, v_ssm_log_dt, v_ssm_b_re, v_ssm_b_im,
               v_ssm_c_re, v_ssm_c_im, v_ssm_d, v_b_glu, v_w_pool, v_pool_scale, v_final_g]
    small_g = [g_norm_g, g_b_ada, g_sinks, g_a_re, g_a_im, g_log_dt, g_b_re, g_b_im, g_c_re, g_c_im,
               g_ssm_d, g_b_glu, g_w_pool, g_pool_scale, g_final_g]

    def flat_pad(arrs):
        v = jnp.concatenate([a.reshape(-1) for a in arrs])
        return jnp.pad(v, (0, (-v.shape[0]) % (SMALL_ROWS * 128))).reshape(-1, 128)

    fw, fg, fm, fv = flat_pad(small_w), flat_pad(small_g), flat_pad(small_m), flat_pad(small_v)
    _, s_delta, s_m, s_v = _adamw(fw, fg[None], fm, fv, name="adamw_small", tr=SMALL_ROWS)
    off = 0
    for nm, w, g in zip(small_names, small_w, small_g):
        n = int(math.prod(w.shape))
        take = lambda a: a.reshape(-1)[off:off + n].reshape(w.shape)
        out[nm] = (g, take(s_delta), take(s_m), take(s_v))
        off += n

    order = ["norm_g", "w_ada", "b_ada", "w_in", "attn_sinks", "ssm_a_re", "ssm_a_im", "ssm_log_dt", "ssm_b_re",
             "ssm_b_im", "ssm_c_re", "ssm_c_im", "ssm_d", "w_glu", "b_glu", "w_pool", "pool_scale", "w_br_att",
             "w_br_ssm", "w_br_pool", "w_out", "final_g"]
    return (loss, grad_x, *[out[k][0] for k in order], *[out[k][1] for k in order],
            *[out[k][2] for k in order], *[out[k][3] for k in order])


def _small_offset(shapes, idx):
    off = 0
    for s in shapes[:idx]:
        n = int(math.prod(s))
        off += n + (-n) % 128
    return off
```

```python
import functools
import math

import jax
import jax.numpy as jnp
from jax import lax
from jax.experimental import pallas as pl
from jax.experimental.pallas import tpu as pltpu

F32 = jnp.float32
BF16 = jnp.bfloat16

N_DEV = 8
D_MODEL = 1024
DEPTH = 2
CHUNK = 64
N_HEADS = 8
N_KV_HEADS = 2
HEAD_DIM = 64
Q_PER_KV = N_HEADS // N_KV_HEADS
WINDOW = 128
ATT_W = 512
KV_W = 128
SSM_W = 512
SSM_GROUP = 16
SSM_GROUPS = 32
SSM_STATE = 64
SSM_N = SSM_GROUPS * SSM_STATE
POOL_W = 512
POOL_WINDOWS = (2, 4, 8, 16)
POOL_GW = 128
POOL_HALO = 16
EPS = 1e-6
NEG_INF = -1e30
ADAM_LR = 0.001
ADAM_B1 = 0.9
ADAM_B2 = 0.999
ADAM_EPS = 1e-08
ADAM_WD = 0.01
ADAM_STEP = 10

SEQ_BLOCK = 256
VMEM_LIMIT = 56 * 1024 * 1024

NN = (((1,), (0,)), ((), ()))
NT = (((1,), (1,)), ((), ()))
TN = (((0,), (0,)), ((), ()))


def _dot(a, b, dims=NN):
    return lax.dot_general(a.astype(BF16), b.astype(BF16), dims, preferred_element_type=F32)


def _params(*sem):
    return pltpu.CompilerParams(dimension_semantics=sem, vmem_limit_bytes=VMEM_LIMIT)


def _sigmoid(x):
    return 1.0 / (1.0 + jnp.exp(-x))


def _silu_and_grad(z):
    s = _sigmoid(z)
    return z * s, s * (1.0 + z * (1.0 - s))


_GELU_K = math.sqrt(2.0 / math.pi)


def _gelu_and_grad(x):
    inner = _GELU_K * (x + 0.044715 * x * x * x)
    t = jnp.tanh(inner)
    val = 0.5 * x * (1.0 + t)
    grad = 0.5 * (1.0 + t) + 0.5 * x * (1.0 - t * t) * _GELU_K * (1.0 + 3.0 * 0.044715 * x * x)
    return val, grad


def _mm(a, b, *, nt=False, out_dtype=F32, tm=512, tn=512, name):
    m, k = a.shape
    n = b.shape[0] if nt else b.shape[1]
    tm, tn = min(tm, m), min(tn, n)
    dims = NT if nt else NN

    def body(a_ref, b_ref, o_ref):
        o_ref[...] = _dot(a_ref[...], b_ref[...], dims).astype(out_dtype)

    b_spec = pl.BlockSpec((tn, k), lambda i, j: (j, 0)) if nt else pl.BlockSpec((k, tn), lambda i, j: (0, j))
    return pl.pallas_call(
        body, grid=(m // tm, n // tn),
        in_specs=[pl.BlockSpec((tm, k), lambda i, j: (i, 0)), b_spec],
        out_specs=pl.BlockSpec((tm, tn), lambda i, j: (i, j)),
        out_shape=jax.ShapeDtypeStruct((m, n), out_dtype),
        compiler_params=_params("parallel", "parallel"), name=name)(a, b)


def _mm_nt_sum(pairs, *, out_dtype=F32, tm=512, tn=512, name):
    m = pairs[0][0].shape[0]
    n = pairs[0][1].shape[0]
    np_ = len(pairs)

    def body(*refs):
        o_ref = refs[-1]
        acc = _dot(refs[0][...], refs[1][...], NT)
        for p in range(1, np_):
            acc = acc + _dot(refs[2 * p][...], refs[2 * p + 1][...], NT)
        o_ref[...] = acc.astype(out_dtype)

    in_specs, args = [], []
    for a, b in pairs:
        in_specs.append(pl.BlockSpec((tm, a.shape[1]), lambda i, j: (i, 0)))
        in_specs.append(pl.BlockSpec((tn, b.shape[1]), lambda i, j: (j, 0)))
        args += [a, b]
    return pl.pallas_call(
        body, grid=(m // tm, n // tn), in_specs=in_specs,
        out_specs=pl.BlockSpec((tm, tn), lambda i, j: (i, j)),
        out_shape=jax.ShapeDtypeStruct((m, n), out_dtype),
        compiler_params=_params("parallel", "parallel"), name=name)(*args)


def _mm_tn(a, b, *, out_dtype=F32, tm=512, tn=512, tk=512, name):
    k, m = a.shape
    n = b.shape[1]
    tm, tn, tk = min(tm, m), min(tn, n), min(tk, k)
    nk = k // tk

    def body(a_ref, b_ref, o_ref, acc_ref):
        kk = pl.program_id(2)

        @pl.when(kk == 0)
        def _():
            acc_ref[...] = jnp.zeros_like(acc_ref)

        acc_ref[...] += _dot(a_ref[...], b_ref[...], TN)

        @pl.when(kk == nk - 1)
        def _():
            o_ref[...] = acc_ref[...].astype(out_dtype)

    return pl.pallas_call(
        body, grid=(m // tm, n // tn, nk),
        in_specs=[pl.BlockSpec((tk, tm), lambda i, j, kk: (kk, i)), pl.BlockSpec((tk, tn), lambda i, j, kk: (kk, j))],
        out_specs=pl.BlockSpec((tm, tn), lambda i, j, kk: (i, j)),
        out_shape=jax.ShapeDtypeStruct((m, n), out_dtype),
        scratch_shapes=[pltpu.VMEM((tm, tn), F32)],
        compiler_params=_params("parallel", "parallel", "arbitrary"), name=name)(a, b)


def _ln_fwd(x, g, shift, scale, *, name, tm=512):
    l, d = x.shape

    def body(x_ref, g_ref, sh_ref, sc_ref, h_ref):
        xv = x_ref[...]
        n = xv * lax.rsqrt(jnp.mean(xv * xv, axis=-1, keepdims=True) + EPS)
        h_ref[...] = ((n * g_ref[...]) * (1.0 + sc_ref[...]) + sh_ref[...]).astype(BF16)

    vec = pl.BlockSpec((1, d), lambda i: (0, 0))
    return pl.pallas_call(
        body, grid=(l // tm,),
        in_specs=[pl.BlockSpec((tm, d), lambda i: (i, 0)), vec, vec, vec],
        out_specs=pl.BlockSpec((tm, d), lambda i: (i, 0)),
        out_shape=jax.ShapeDtypeStruct((l, d), BF16),
        compiler_params=_params("parallel"), name=name)(x, g, shift, scale)


def _ln_bwd(x, dh, dres, g, scale, *, name, tm=512):
    l, d = x.shape

    def body(x_ref, dh_ref, dres_ref, g_ref, sc_ref, dx_ref, sums_ref):
        xv = x_ref[...]
        dhv = dh_ref[...]
        rstd = lax.rsqrt(jnp.mean(xv * xv, axis=-1, keepdims=True) + EPS)
        n = xv * rstd
        gv = g_ref[...]
        dr = dhv * (1.0 + sc_ref[...])
        dn = dr * gv
        dx_ref[...] = dres_ref[...] + rstd * (dn - n * jnp.mean(dn * n, axis=-1, keepdims=True))

        @pl.when(pl.program_id(0) == 0)
        def _():
            sums_ref[...] = jnp.zeros_like(sums_ref)

        sums_ref[0:1, :] += jnp.sum(dhv, axis=0, keepdims=True)
        sums_ref[1:2, :] += jnp.sum(dhv * (n * gv), axis=0, keepdims=True)
        sums_ref[2:3, :] += jnp.sum(dr * n, axis=0, keepdims=True)

    vec = pl.BlockSpec((1, d), lambda i: (0, 0))
    row = pl.BlockSpec((tm, d), lambda i: (i, 0))
    return pl.pallas_call(
        body, grid=(l // tm,),
        in_specs=[row, row, row, vec, vec],
        out_specs=[row, pl.BlockSpec((8, d), lambda i: (0, 0))],
        out_shape=[jax.ShapeDtypeStruct((l, d), F32), jax.ShapeDtypeStruct((8, d), F32)],
        compiler_params=_params("arbitrary"), name=name)(x, dh, dres, g, scale)


def _final_loss(x, g, target, *, tm=512):
    l, d = x.shape

    def body(x_ref, g_ref, t_ref, dx_ref, sums_ref):
        xv = x_ref[...]
        rstd = lax.rsqrt(jnp.mean(xv * xv, axis=-1, keepdims=True) + EPS)
        n = xv * rstd
        gv = g_ref[...]
        err = n * gv - t_ref[...]
        dy = err * (1.0 / d)
        dn = dy * gv
        dx_ref[...] = rstd * (dn - n * jnp.mean(dn * n, axis=-1, keepdims=True))

        @pl.when(pl.program_id(0) == 0)
        def _():
            sums_ref[...] = jnp.zeros_like(sums_ref)

        sums_ref[0:1, :] += jnp.sum(dy * n, axis=0, keepdims=True)
        sums_ref[1:2, :] += jnp.sum(err * err, axis=0, keepdims=True) * (0.5 / d)

    vec = pl.BlockSpec((1, d), lambda i: (0, 0))
    row = pl.BlockSpec((tm, d), lambda i: (i, 0))
    dx, sums = pl.pallas_call(
        body, grid=(l // tm,),
        in_specs=[row, vec, row],
        out_specs=[row, pl.BlockSpec((8, d), lambda i: (0, 0))],
        out_shape=[jax.ShapeDtypeStruct((l, d), F32), jax.ShapeDtypeStruct((8, d), F32)],
        compiler_params=_params("arbitrary"), name="final_loss")(x, g, target)
    return dx, sums


def _attn_geometry(i, t):
    nk = t + WINDOW
    qi = lax.broadcasted_iota(jnp.int32, (t, nk), 0)
    kj = lax.broadcasted_iota(jnp.int32, (t, nk), 1)
    dist = jnp.abs(qi + WINDOW - kj).astype(F32)
    qc = jnp.right_shift(qi, 6)
    kc = jnp.right_shift(kj, 6)
    valid = (kc >= qc) & (kc <= qc + WINDOW // CHUNK) & ((i > 0) | (kj >= WINDOW))
    return dist, valid


def _attn_head(q, k_all, v_all, sink, slope, dist, valid):
    s = _dot(q, k_all, NT) * (1.0 / math.sqrt(HEAD_DIM)) - slope * dist
    s = jnp.where(valid, s, NEG_INF)
    m = jnp.maximum(jnp.max(s, axis=-1, keepdims=True), sink)
    e = jnp.exp(s - m)
    es = jnp.exp(sink - m)
    inv = 1.0 / (jnp.sum(e, axis=-1, keepdims=True) + es)
    p = e * inv
    o = _dot(p, v_all, NN)
    return p, o, es * inv


def _attn_specs(t):
    cur = pl.BlockSpec((t, ATT_W * 2 + KV_W * 2), lambda i: (i, 0))
    halo_blocks = t // WINDOW
    prev = pl.BlockSpec((WINDOW, 2 * KV_W), lambda i: (jnp.maximum(i * halo_blocks - 1, 0), (2 * ATT_W) // (2 * KV_W)))
    return cur, prev


def _attn_fwd(pa, sinks, *, name, t=SEQ_BLOCK):
    l = pa.shape[0]
    t = min(t, l)

    def body(sink_ref, cur_ref, prev_ref, ya_ref):
        i = pl.program_id(0)
        dist, valid = _attn_geometry(i, t)
        for h in range(N_HEADS):
            kh = h // Q_PER_KV
            q = cur_ref[:, h * HEAD_DIM:(h + 1) * HEAD_DIM]
            z = cur_ref[:, ATT_W + h * HEAD_DIM:ATT_W + (h + 1) * HEAD_DIM]
            k_all = jnp.concatenate([prev_ref[:, kh * HEAD_DIM:(kh + 1) * HEAD_DIM],
                                     cur_ref[:, 2 * ATT_W + kh * HEAD_DIM:2 * ATT_W + (kh + 1) * HEAD_DIM]], axis=0)
            v_all = jnp.concatenate([prev_ref[:, KV_W + kh * HEAD_DIM:KV_W + (kh + 1) * HEAD_DIM],
                                     cur_ref[:, 2 * ATT_W + KV_W + kh * HEAD_DIM:2 * ATT_W + KV_W + (kh + 1) * HEAD_DIM]], axis=0)
            _, o, _ = _attn_head(q, k_all, v_all, sink_ref[h], 2.0 ** (-(h + 1)), dist, valid)
            sz, _ = _silu_and_grad(z)
            ya_ref[:, h * HEAD_DIM:(h + 1) * HEAD_DIM] = (o * sz).astype(BF16)

    cur, prev = _attn_specs(t)
    return pl.pallas_call(
        body, grid=(l // t,),
        in_specs=[pl.BlockSpec(memory_space=pltpu.SMEM), cur, prev],
        out_specs=pl.BlockSpec((t, ATT_W), lambda i: (i, 0)),
        out_shape=jax.ShapeDtypeStruct((l, ATT_W), BF16),
        compiler_params=_params("parallel"), name=name)(sinks, pa, pa)


def _attn_bwd(pa, sinks, dya, *, name, t=SEQ_BLOCK):
    l = pa.shape[0]
    t = min(t, l)
    nb = l // t
    scale = 1.0 / math.sqrt(HEAD_DIM)

    def body(sink_ref, cur_ref, prev_ref, dya_ref, dpa_ref, dsink_ref, carry_ref):
        n = pl.program_id(0)
        i = nb - 1 - n
        dist, valid = _attn_geometry(i, t)

        @pl.when(n == 0)
        def _():
            carry_ref[...] = jnp.zeros_like(carry_ref)
            dsink_ref[...] = jnp.zeros_like(dsink_ref)

        dk_acc = [jnp.zeros((t + WINDOW, HEAD_DIM), F32) for _ in range(N_KV_HEADS)]
        dv_acc = [jnp.zeros((t + WINDOW, HEAD_DIM), F32) for _ in range(N_KV_HEADS)]
        for h in range(N_HEADS):
            kh = h // Q_PER_KV
            q = cur_ref[:, h * HEAD_DIM:(h + 1) * HEAD_DIM]
            z = cur_ref[:, ATT_W + h * HEAD_DIM:ATT_W + (h + 1) * HEAD_DIM]
            k_all = jnp.concatenate([prev_ref[:, kh * HEAD_DIM:(kh + 1) * HEAD_DIM],
                                     cur_ref[:, 2 * ATT_W + kh * HEAD_DIM:2 * ATT_W + (kh + 1) * HEAD_DIM]], axis=0)
            v_all = jnp.concatenate([prev_ref[:, KV_W + kh * HEAD_DIM:KV_W + (kh + 1) * HEAD_DIM],
                                     cur_ref[:, 2 * ATT_W + KV_W + kh * HEAD_DIM:2 * ATT_W + KV_W + (kh + 1) * HEAD_DIM]], axis=0)
            p, o, p_sink = _attn_head(q, k_all, v_all, sink_ref[h], 2.0 ** (-(h + 1)), dist, valid)
            dy = dya_ref[:, h * HEAD_DIM:(h + 1) * HEAD_DIM]
            sz, dsz = _silu_and_grad(z)
            do = dy * sz
            dpa_ref[:, ATT_W + h * HEAD_DIM:ATT_W + (h + 1) * HEAD_DIM] = (dy * o * dsz).astype(BF16)
            delta = jnp.sum(do * o, axis=-1, keepdims=True)
            dp = _dot(do, v_all, NT)
            ds = p * (dp - delta)
            dpa_ref[:, h * HEAD_DIM:(h + 1) * HEAD_DIM] = (_dot(ds, k_all, NN) * scale).astype(BF16)
            dk_acc[kh] = dk_acc[kh] + _dot(ds, q, TN) * scale
            dv_acc[kh] = dv_acc[kh] + _dot(p, do, TN)
            dsink_ref[h:h + 1, :] += jnp.broadcast_to(-jnp.sum(p_sink * delta, axis=0, keepdims=True), (1, 128))

        for kh in range(N_KV_HEADS):
            for which, acc in ((0, dk_acc[kh]), (1, dv_acc[kh])):
                c0 = which * KV_W + kh * HEAD_DIM
                own = acc[WINDOW:, :]
                tail = own[t - WINDOW:, :] + carry_ref[:, c0:c0 + HEAD_DIM]
                dpa_ref[0:t - WINDOW, 2 * ATT_W + c0:2 * ATT_W + c0 + HEAD_DIM] = own[:t - WINDOW, :].astype(BF16)
                dpa_ref[t - WINDOW:t, 2 * ATT_W + c0:2 * ATT_W + c0 + HEAD_DIM] = tail.astype(BF16)
                carry_ref[:, c0:c0 + HEAD_DIM] = acc[:WINDOW, :]

    halo_blocks = t // WINDOW
    wpa = 2 * ATT_W + 2 * KV_W
    cur = pl.BlockSpec((t, wpa), lambda n: (nb - 1 - n, 0))
    prev = pl.BlockSpec((WINDOW, 2 * KV_W),
                        lambda n: (jnp.maximum((nb - 1 - n) * halo_blocks - 1, 0), (2 * ATT_W) // (2 * KV_W)))
    return pl.pallas_call(
        body, grid=(nb,),
        in_specs=[pl.BlockSpec(memory_space=pltpu.SMEM), cur, prev, pl.BlockSpec((t, ATT_W), lambda n: (nb - 1 - n, 0))],
        out_specs=[pl.BlockSpec((t, wpa), lambda n: (nb - 1 - n, 0)), pl.BlockSpec((8, 128), lambda n: (0, 0))],
        out_shape=[jax.ShapeDtypeStruct((l, wpa), BF16), jax.ShapeDtypeStruct((8, 128), F32)],
        scratch_shapes=[pltpu.VMEM((WINDOW, 2 * KV_W), F32)],
        compiler_params=_params("arbitrary"), name=name)(sinks, pa, pa, dya)


def _scan(xr, xi, lr, li, t, reverse):
    row = lax.broadcasted_iota(jnp.int32, (t, 1), 0)
    d = 1
    pr, pi = lr, li
    while d < t:
        if reverse:
            sr = jnp.where(row < t - d, pltpu.roll(xr, t - d, 0), 0.0)
            si = jnp.where(row < t - d, pltpu.roll(xi, t - d, 0), 0.0)
        else:
            sr = jnp.where(row >= d, pltpu.roll(xr, d, 0), 0.0)
            si = jnp.where(row >= d, pltpu.roll(xi, d, 0), 0.0)
        xr, xi = xr + pr * sr - pi * si, xi + pr * si + pi * sr
        pr, pi = pr * pr - pi * pi, 2.0 * pr * pi
        d *= 2
    return xr, xi


SCAN_SUB = 8


def _split_hi_lo(a):
    hi = a.astype(BF16)
    lo = (a - hi.astype(F32)).astype(BF16)
    return jnp.concatenate([hi, lo], axis=0)


def _scan_mxu(xr, xi, tab, lam3, lam8, tri, expand, cr, ci, t, reverse):
    ns = t // SCAN_SUB
    n = xr.shape[1]
    v3 = lambda a: a.reshape(ns, SCAN_SUB, n)
    x3r, x3i = v3(xr), v3(xi)
    br = (x3r * tab[0] - x3i * tab[1]).reshape(t, n)
    bi = (x3r * tab[1] + x3i * tab[0]).reshape(t, n)
    pm = jnp.dot(tri, _split_hi_lo(jnp.concatenate([br, bi], axis=1)), preferred_element_type=F32)
    p3r, p3i = v3(pm[:t, :n]), v3(pm[:t, n:])
    slr = p3r * tab[2] - p3i * tab[3]
    sli = p3r * tab[3] + p3i * tab[2]
    totr, toti = pm[t:, :n], pm[t:, n:]
    l3r, l3i = lam3
    l8r, l8i = lam8
    row = lax.broadcasted_iota(jnp.int32, (ns, 1), 0)
    edge = row == (ns - 1 if reverse else 0)
    er = totr * l3r - toti * l3i + jnp.where(edge, l8r * cr - l8i * ci, 0.0)
    ei = totr * l3i + toti * l3r + jnp.where(edge, l8r * ci + l8i * cr, 0.0)
    er, ei = _scan(er, ei, l8r, l8i, ns, reverse)
    shift = ns - 1 if reverse else 1
    nbr = jnp.where(edge, cr, pltpu.roll(er, shift, 0))
    nbi = jnp.where(edge, ci, pltpu.roll(ei, shift, 0))
    ex = jnp.dot(expand, _split_hi_lo(jnp.concatenate([nbr, nbi], axis=1)), preferred_element_type=F32)
    e3r, e3i = v3(ex[:, :n]), v3(ex[:, n:])
    sr = (slr + e3r * tab[4] - e3i * tab[5]).reshape(t, n)
    si = (sli + e3r * tab[5] + e3i * tab[4]).reshape(t, n)
    out = 0 if reverse else ns - 1
    return sr, si, er[out:out + 1, :], ei[out:out + 1, :]


def _scan_consts(t):
    import numpy as np
    ns = t // SCAN_SUB
    r = np.arange(t)
    same = (r[:, None] // SCAN_SUB) == (r[None, :] // SCAN_SUB)
    sums = (np.arange(ns)[:, None] == (r[None, :] // SCAN_SUB))
    tri = []
    for keep in (r[None, :] <= r[:, None], r[None, :] >= r[:, None]):
        m = np.concatenate([same & keep, sums], axis=0).astype(np.float32)
        tri.append(np.concatenate([m, m], axis=1))
    ex = ((r[:, None] // SCAN_SUB) == np.arange(ns)[None, :]).astype(np.float32)
    return jnp.asarray(np.stack(tri), BF16), jnp.asarray(np.concatenate([ex, ex], axis=1), BF16)


def _scan_tables(lr, li):
    den = lr * lr + li * li
    ir, ii = lr / den, -li / den
    mul = lambda a, b: (a[0] * b[0] - a[1] * b[1], a[0] * b[1] + a[1] * b[0])
    pw = {0: (jnp.ones_like(lr), jnp.zeros_like(lr))}
    for e in range(1, 9):
        pw[e] = mul(pw[e - 1], (lr, li))
    for e in range(-1, -5, -1):
        pw[e] = mul(pw[e + 1], (ir, ii))
    stack = lambda es, sign: (jnp.stack([pw[e][0] for e in es]), sign * jnp.stack([pw[e][1] for e in es]))
    j = range(SCAN_SUB)
    parts = [stack([4 - k for k in j], 1.0), stack([k - 4 for k in j], 1.0), stack([k + 1 for k in j], 1.0),
             stack([k - 3 for k in j], -1.0), stack([3 - k for k in j], -1.0), stack([8 - k for k in j], -1.0)]
    tabs = jnp.stack([a for pair in parts for a in pair])
    lam = jnp.zeros((8, lr.shape[0]), F32)
    for k, v in enumerate((lr, li, pw[3][0], pw[3][1], pw[8][0], pw[8][1])):
        lam = lam.at[k].set(v)
    return lam, tabs


def _ssm_states(u, s0r, s0i, lam_ref, tab_ref, tri_ref, ex_ref, bre, bim, t):
    tab = tuple(tab_ref[k] for k in range(6))
    return _scan_mxu(_dot(u, bre), _dot(u, bim), tab, (lam_ref[2:3, :], lam_ref[3:4, :]),
                     (lam_ref[4:5, :], lam_ref[5:6, :]), tri_ref[0], ex_ref[...], s0r, s0i, t, False)


def _ssm_head(u, z, xr, xi, cre, cim, dskip, wglu, bglu):
    y = _dot(xr, cre) - _dot(xi, cim) + dskip * u
    y2, dgelu = _gelu_and_grad(y)
    gate = _sigmoid(_dot(y2, wglu) + bglu)
    y3 = y2 * gate
    return y2, dgelu, gate, y3


def _ssm_fwd(ps, scan_ops, bblk, cblk, dskip, wglu, bglu, *, name, t=SEQ_BLOCK):
    l = ps.shape[0]
    assert l % t == 0
    nb = l // t
    ns = t // SCAN_SUB

    def body(ps_ref, lam_ref, tab_ref, tri_ref, ex_ref, b_ref, c_ref, d_ref, w_ref, bg_ref, ys_ref, chk_ref, st_ref):
        @pl.when(pl.program_id(0) == 0)
        def _():
            st_ref[...] = jnp.zeros_like(st_ref)

        chk_ref[...] = jnp.broadcast_to(st_ref[...], chk_ref.shape)
        u = ps_ref[:, :SSM_W]
        z = ps_ref[:, SSM_W:]
        xr, xi, er, ei = _ssm_states(u, st_ref[:, :SSM_N], st_ref[:, SSM_N:], lam_ref, tab_ref, tri_ref, ex_ref,
                                     b_ref[0], b_ref[1], t)
        st_ref[:, :SSM_N] = er
        st_ref[:, SSM_N:] = ei
        _, _, _, y3 = _ssm_head(u, z, xr, xi, c_ref[0], c_ref[1], d_ref[...], w_ref[...], bg_ref[...])
        sz, _ = _silu_and_grad(z)
        ys_ref[...] = (y3 * sz).astype(BF16)

    full = lambda shape: pl.BlockSpec(shape, lambda i: (0,) * len(shape))
    return pl.pallas_call(
        body, grid=(nb,),
        in_specs=[pl.BlockSpec((t, 2 * SSM_W), lambda i: (i, 0)), full((8, SSM_N)), full((12, SCAN_SUB, SSM_N)),
                  full((2, t + ns, 2 * t)), full((t, 2 * ns)), full((2, SSM_W, SSM_N)),
                  full((2, SSM_N, SSM_W)), full((1, SSM_W)), full((SSM_W, SSM_W)), full((1, SSM_W))],
        out_specs=[pl.BlockSpec((t, SSM_W), lambda i: (i, 0)), pl.BlockSpec((8, 2 * SSM_N), lambda i: (i, 0))],
        out_shape=[jax.ShapeDtypeStruct((l, SSM_W), BF16), jax.ShapeDtypeStruct((nb * 8, 2 * SSM_N), F32)],
        scratch_shapes=[pltpu.VMEM((1, 2 * SSM_N), F32)],
        compiler_params=_params("arbitrary"), name=name)(ps, *scan_ops, bblk, cblk, dskip, wglu, bglu)


def _ssm_bwd(ps, dys, chk, scan_ops, bblk, cblk, dskip, wglu, bglu, *, name, t=SEQ_BLOCK):
    l = ps.shape[0]
    assert l % t == 0
    nb = l // t
    ns = t // SCAN_SUB

    def body(ps_ref, dys_ref, chk_ref, lam_ref, tab_ref, tri_ref, ex_ref, b_ref, c_ref, d_ref, w_ref, bg_ref,
             dps_ref, db_ref, dc_ref, dw_ref, sums_ref, gc_ref, db_acc, dc_acc, dw_acc, sums_acc, sem):
        n = pl.program_id(0)

        @pl.when(n == 0)
        def _():
            gc_ref[...] = jnp.zeros_like(gc_ref)
            db_acc[...] = jnp.zeros_like(db_acc)
            dc_acc[...] = jnp.zeros_like(dc_acc)
            dw_acc[...] = jnp.zeros_like(dw_acc)
            sums_acc[...] = jnp.zeros_like(sums_acc)

        row = lax.broadcasted_iota(jnp.int32, (t, 1), 0)
        u = ps_ref[:, :SSM_W]
        z = ps_ref[:, SSM_W:]
        s0r, s0i = chk_ref[0:1, :SSM_N], chk_ref[0:1, SSM_N:]
        xr, xi, _, _ = _ssm_states(u, s0r, s0i, lam_ref, tab_ref, tri_ref, ex_ref, b_ref[0], b_ref[1], t)
        dskip = d_ref[...]
        y2, dgelu, gate, y3 = _ssm_head(u, z, xr, xi, c_ref[0], c_ref[1], dskip, w_ref[...], bg_ref[...])
        sz, dsz = _silu_and_grad(z)
        dys_v = dys_ref[...]
        dps_ref[:, SSM_W:] = (dys_v * y3 * dsz).astype(BF16)
        dy3 = dys_v * sz
        da = dy3 * y2 * gate * (1.0 - gate)
        dy2 = dy3 * gate + _dot(da, w_ref[...], NT)
        dw_acc[...] += _dot(y2, da, TN)
        dy = dy2 * dgelu
        sums_acc[2:3, :SSM_W] += jnp.sum(dy * u, axis=0, keepdims=True)
        sums_acc[3:4, :SSM_W] += jnp.sum(da, axis=0, keepdims=True)
        dc_acc[0] += _dot(xr, dy, TN)
        dc_acc[1] += -_dot(xi, dy, TN)
        rev_tab = tuple(tab_ref[k] for k in range(6, 12))
        gr, gi, gcr, gci = _scan_mxu(
            _dot(dy, c_ref[0], NT), -_dot(dy, c_ref[1], NT), rev_tab, (lam_ref[2:3, :], -lam_ref[3:4, :]),
            (lam_ref[4:5, :], -lam_ref[5:6, :]), tri_ref[1], ex_ref[...], gc_ref[:, :SSM_N], gc_ref[:, SSM_N:], t, True)
        gc_ref[:, :SSM_N] = gcr
        gc_ref[:, SSM_N:] = gci
        db_acc[0] += _dot(u, gr, TN)
        db_acc[1] += _dot(u, gi, TN)
        du = dskip * dy + _dot(gr, b_ref[0], NT) + _dot(gi, b_ref[1], NT)
        dps_ref[:, :SSM_W] = du.astype(BF16)
        spr = jnp.where(row == 0, s0r, pltpu.roll(xr, 1, 0))
        spi = jnp.where(row == 0, s0i, pltpu.roll(xi, 1, 0))
        sums_acc[0:1, :] += jnp.sum(gr * spr + gi * spi, axis=0, keepdims=True)
        sums_acc[1:2, :] += jnp.sum(gi * spr - gr * spi, axis=0, keepdims=True)

        @pl.when(n == nb - 1)
        def _():
            copies = [pltpu.make_async_copy(db_acc, db_ref, sem.at[0]),
                      pltpu.make_async_copy(dc_acc, dc_ref, sem.at[1]),
                      pltpu.make_async_copy(dw_acc, dw_ref, sem.at[2]),
                      pltpu.make_async_copy(sums_acc, sums_ref, sem.at[3])]
            for cp in copies:
                cp.start()
            for cp in copies:
                cp.wait()

    full = lambda shape: pl.BlockSpec(shape, lambda n: (0,) * len(shape))
    anyspec = pl.BlockSpec(memory_space=pl.ANY)
    return pl.pallas_call(
        body, grid=(nb,),
        in_specs=[pl.BlockSpec((t, 2 * SSM_W), lambda n: (nb - 1 - n, 0)),
                  pl.BlockSpec((t, SSM_W), lambda n: (nb - 1 - n, 0)),
                  pl.BlockSpec((8, 2 * SSM_N), lambda n: (nb - 1 - n, 0)),
                  full((8, SSM_N)), full((12, SCAN_SUB, SSM_N)), full((2, t + ns, 2 * t)), full((t, 2 * ns)),
                  full((2, SSM_W, SSM_N)), full((2, SSM_N, SSM_W)), full((1, SSM_W)),
                  full((SSM_W, SSM_W)), full((1, SSM_W))],
        out_specs=[pl.BlockSpec((t, 2 * SSM_W), lambda n: (nb - 1 - n, 0)), anyspec, anyspec, anyspec, anyspec],
        out_shape=[jax.ShapeDtypeStruct((l, 2 * SSM_W), BF16),
                   jax.ShapeDtypeStruct((2, SSM_W, SSM_N), F32),
                   jax.ShapeDtypeStruct((2, SSM_N, SSM_W), F32),
                   jax.ShapeDtypeStruct((SSM_W, SSM_W), F32),
                   jax.ShapeDtypeStruct((8, SSM_N), F32)],
        scratch_shapes=[pltpu.VMEM((1, 2 * SSM_N), F32), pltpu.VMEM((2, SSM_W, SSM_N), F32),
                        pltpu.VMEM((2, SSM_N, SSM_W), F32), pltpu.VMEM((SSM_W, SSM_W), F32),
                        pltpu.VMEM((8, SSM_N), F32), pltpu.SemaphoreType.DMA((4,))],
        compiler_params=_params("arbitrary"), name=name)(ps, dys, chk, *scan_ops, bblk, cblk, dskip, wglu, bglu)


def _pool_count(i, t):
    pos = lax.broadcasted_iota(jnp.int32, (t, POOL_W), 0) + i * t + 1
    col = lax.broadcasted_iota(jnp.int32, (t, POOL_W), 1)
    win = jnp.where(col < POOL_GW, 2, jnp.where(col < 2 * POOL_GW, 4, jnp.where(col < 3 * POOL_GW, 8, 16)))
    return 1.0 / jnp.minimum(pos, win).astype(F32), col


def _window_sums(ext, n_rows, forward):
    col = lax.broadcasted_iota(jnp.int32, ext.shape, 1)
    sh = (lambda a, d: pltpu.roll(a, d, 0)) if forward else (lambda a, d: pltpu.roll(a, n_rows - d, 0))
    a2 = ext + sh(ext, 1)
    a4 = a2 + sh(a2, 2)
    a8 = a4 + sh(a4, 4)
    a16 = a8 + sh(a8, 8)
    return jnp.where(col < POOL_GW, a2, jnp.where(col < 2 * POOL_GW, a4, jnp.where(col < 3 * POOL_GW, a8, a16)))


def _pool_mix(pooled, wp_ref):
    return jnp.concatenate([_dot(pooled[:, g * POOL_GW:(g + 1) * POOL_GW], wp_ref[g]) for g in range(4)], axis=1)


def _pool_pooled(i, cur_u, prev_u, t):
    prev = jnp.where(i > 0, prev_u, 0.0)
    ext = jnp.concatenate([prev, cur_u], axis=0)
    inv_cnt, _ = _pool_count(i, t)
    return _window_sums(ext, t + POOL_HALO, True)[POOL_HALO:, :] * inv_cnt - cur_u


def _pool_fwd(pp, wpool, pscale, *, name, t=SEQ_BLOCK):
    l = pp.shape[0]
    t = min(t, l)

    def body(cur_ref, prev_ref, wp_ref, sc_ref, yp_ref):
        i = pl.program_id(0)
        pooled = _pool_pooled(i, cur_ref[:, :POOL_W], prev_ref[...], t)
        lin = _pool_mix(pooled, wp_ref)
        sz, _ = _silu_and_grad(cur_ref[:, POOL_W:])
        yp_ref[...] = (lin * sc_ref[...] * sz).astype(BF16)

    hb = t // POOL_HALO
    return pl.pallas_call(
        body, grid=(l // t,),
        in_specs=[pl.BlockSpec((t, 2 * POOL_W), lambda i: (i, 0)),
                  pl.BlockSpec((POOL_HALO, POOL_W), lambda i: (jnp.maximum(i * hb - 1, 0), 0)),
                  pl.BlockSpec((4, POOL_GW, POOL_GW), lambda i: (0, 0, 0)),
                  pl.BlockSpec((1, POOL_W), lambda i: (0, 0))],
        out_specs=pl.BlockSpec((t, POOL_W), lambda i: (i, 0)),
        out_shape=jax.ShapeDtypeStruct((l, POOL_W), BF16),
        compiler_params=_params("parallel"), name=name)(pp, pp, wpool, pscale)


def _pool_bwd(pp, dyp, wpool, pscale, *, name, t=SEQ_BLOCK):
    l = pp.shape[0]
    t = min(t, l)
    nb = l // t

    def body(cur_ref, prev_ref, dyp_ref, wp_ref, sc_ref, dpp_ref, dwp_ref, sums_ref, carry_ref):
        n = pl.program_id(0)
        i = nb - 1 - n

        @pl.when(n == 0)
        def _():
            carry_ref[...] = jnp.zeros_like(carry_ref)
            dwp_ref[...] = jnp.zeros_like(dwp_ref)
            sums_ref[...] = jnp.zeros_like(sums_ref)

        cur_u = cur_ref[:, :POOL_W]
        pooled = _pool_pooled(i, cur_u, prev_ref[...], t)
        lin = _pool_mix(pooled, wp_ref)
        sz, dsz = _silu_and_grad(cur_ref[:, POOL_W:])
        dyp_v = dyp_ref[...]
        scale = sc_ref[...]
        dpp_ref[:, POOL_W:] = (dyp_v * lin * scale * dsz).astype(BF16)
        dpre = dyp_v * sz
        sums_ref[0:1, :] += jnp.sum(dpre * lin, axis=0, keepdims=True)
        dlin = dpre * scale
        dpooled = []
        for g in range(4):
            dl = dlin[:, g * POOL_GW:(g + 1) * POOL_GW]
            dwp_ref[g] += _dot(pooled[:, g * POOL_GW:(g + 1) * POOL_GW], dl, TN)
            dpooled.append(_dot(dl, wp_ref[g], NT))
        dpooled = jnp.concatenate(dpooled, axis=1)
        inv_cnt, _ = _pool_count(i, t)
        dq = dpooled * inv_cnt
        ext = jnp.concatenate([dq, carry_ref[...]], axis=0)
        du = _window_sums(ext, t + POOL_HALO, False)[:t, :] - dpooled
        dpp_ref[:, :POOL_W] = du.astype(BF16)
        carry_ref[...] = dq[:POOL_HALO, :]

    hb = t // POOL_HALO
    return pl.pallas_call(
        body, grid=(nb,),
        in_specs=[pl.BlockSpec((t, 2 * POOL_W), lambda n: (nb - 1 - n, 0)),
                  pl.BlockSpec((POOL_HALO, POOL_W), lambda n: (jnp.maximum((nb - 1 - n) * hb - 1, 0), 0)),
                  pl.BlockSpec((t, POOL_W), lambda n: (nb - 1 - n, 0)),
                  pl.BlockSpec((4, POOL_GW, POOL_GW), lambda n: (0, 0, 0)),
                  pl.BlockSpec((1, POOL_W), lambda n: (0, 0))],
        out_specs=[pl.BlockSpec((t, 2 * POOL_W), lambda n: (nb - 1 - n, 0)),
                   pl.BlockSpec((4, POOL_GW, POOL_GW), lambda n: (0, 0, 0)),
                   pl.BlockSpec((8, POOL_W), lambda n: (0, 0))],
        out_shape=[jax.ShapeDtypeStruct((l, 2 * POOL_W), BF16), jax.ShapeDtypeStruct((4, POOL_GW, POOL_GW), F32),
                   jax.ShapeDtypeStruct((8, POOL_W), F32)],
        scratch_shapes=[pltpu.VMEM((POOL_HALO, POOL_W), F32)],
        compiler_params=_params("arbitrary"), name=name)(pp, pp, dyp, wpool, pscale)


def _merge_fwd(ya, ys, yp, wa, ws, wp, pg, *, name, tm=256):
    l = ya.shape[0]
    tm = min(tm, l)
    d = D_MODEL

    def body(ya_ref, ys_ref, yp_ref, wa_ref, ws_ref, wp_ref, pg_ref, mg_ref, ba_ref, bs_ref, bp_ref):
        acc = None
        for k, (y_ref, w_ref, b_ref) in enumerate(((ya_ref, wa_ref, ba_ref), (ys_ref, ws_ref, bs_ref),
                                                   (yp_ref, wp_ref, bp_ref))):
            br = _dot(y_ref[...], w_ref[...])
            b_ref[...] = br
            term = _sigmoid(pg_ref[:, k * d:(k + 1) * d]) * br
            acc = term if acc is None else acc + term
        mg_ref[...] = acc.astype(BF16)

    rowy = pl.BlockSpec((tm, ATT_W), lambda i: (i, 0))
    wsp = pl.BlockSpec((ATT_W, d), lambda i: (0, 0))
    rowd = pl.BlockSpec((tm, d), lambda i: (i, 0))
    return pl.pallas_call(
        body, grid=(l // tm,),
        in_specs=[rowy, rowy, rowy, wsp, wsp, wsp, pl.BlockSpec((tm, 3 * d), lambda i: (i, 0))],
        out_specs=[rowd, rowd, rowd, rowd],
        out_shape=[jax.ShapeDtypeStruct((l, d), BF16)] + [jax.ShapeDtypeStruct((l, d), F32)] * 3,
        compiler_params=_params("parallel"), name=name)(ya, ys, yp, wa, ws, wp, pg)


def _out_fwd(merged, wout, x, gate, *, name, tm=512):
    l, d = x.shape
    tm = min(tm, l)

    def body(m_ref, w_ref, x_ref, g_ref, xn_ref, out_ref):
        out = _dot(m_ref[...], w_ref[...])
        out_ref[...] = out
        xn_ref[...] = x_ref[...] + g_ref[...] * out

    row = pl.BlockSpec((tm, d), lambda i: (i, 0))
    return pl.pallas_call(
        body, grid=(l // tm,),
        in_specs=[row, pl.BlockSpec((d, d), lambda i: (0, 0)), row, pl.BlockSpec((1, d), lambda i: (0, 0))],
        out_specs=[row, row],
        out_shape=[jax.ShapeDtypeStruct((l, d), F32)] * 2,
        compiler_params=_params("parallel"), name=name)(merged, wout, x, gate)


def _merge_bwd(dx, out, gate, wout, pg, ba, bs, bp, *, name, tm=256):
    l, d = dx.shape
    tm = min(tm, l)

    def body(dx_ref, out_ref, g_ref, w_ref, pg_ref, ba_ref, bs_ref, bp_ref,
             dmo_ref, dba_ref, dbs_ref, dbp_ref, dpg_ref, sums_ref):
        @pl.when(pl.program_id(0) == 0)
        def _():
            sums_ref[...] = jnp.zeros_like(sums_ref)

        dxv = dx_ref[...]
        sums_ref[0:1, :] += jnp.sum(dxv * out_ref[...], axis=0, keepdims=True)
        dmo = (dxv * g_ref[...]).astype(BF16)
        dmo_ref[...] = dmo
        dmerged = _dot(dmo, w_ref[...], NT)
        for k, (b_ref, db_ref) in enumerate(((ba_ref, dba_ref), (bs_ref, dbs_ref), (bp_ref, dbp_ref))):
            gk = _sigmoid(pg_ref[:, k * d:(k + 1) * d])
            db_ref[...] = (dmerged * gk).astype(BF16)
            dpg_ref[:, k * d:(k + 1) * d] = (dmerged * b_ref[...] * gk * (1.0 - gk)).astype(BF16)

    row = pl.BlockSpec((tm, d), lambda i: (i, 0))
    wide = pl.BlockSpec((tm, 3 * d), lambda i: (i, 0))
    return pl.pallas_call(
        body, grid=(l // tm,),
        in_specs=[row, row, pl.BlockSpec((1, d), lambda i: (0, 0)), pl.BlockSpec((d, d), lambda i: (0, 0)),
                  wide, row, row, row],
        out_specs=[row, row, row, row, wide, pl.BlockSpec((8, d), lambda i: (0, 0))],
        out_shape=[jax.ShapeDtypeStruct((l, d), BF16)] * 4 + [jax.ShapeDtypeStruct((l, 3 * d), BF16),
                                                             jax.ShapeDtypeStruct((8, d), F32)],
        compiler_params=_params("arbitrary"), name=name)(dx, out, gate, wout, pg, ba, bs, bp)


def _adamw(w, g, m, v, *, name, tr=256):
    r, c = w.shape
    p = g.shape[0]
    tr = min(tr, r)
    assert r % tr == 0
    c1 = 1.0 / (1.0 - ADAM_B1 ** ADAM_STEP)
    c2 = 1.0 / (1.0 - ADAM_B2 ** ADAM_STEP)

    def body(w_ref, g_ref, m_ref, v_ref, go_ref, d_ref, mo_ref, vo_ref):
        gv = g_ref[0].astype(F32)
        for k in range(1, p):
            gv = gv + g_ref[k].astype(F32)
        go_ref[...] = gv
        mn = ADAM_B1 * m_ref[...] + (1.0 - ADAM_B1) * gv
        vn = ADAM_B2 * v_ref[...] + (1.0 - ADAM_B2) * (gv * gv)
        mo_ref[...] = mn
        vo_ref[...] = vn
        d_ref[...] = -ADAM_LR * ((mn * c1) / (jnp.sqrt(vn * c2) + ADAM_EPS) + ADAM_WD * w_ref[...])

    row = pl.BlockSpec((tr, c), lambda i: (i, 0))
    return pl.pallas_call(
        body, grid=(r // tr,),
        in_specs=[row, pl.BlockSpec((p, tr, c), lambda i: (0, i, 0)), row, row],
        out_specs=[row] * 4,
        out_shape=[jax.ShapeDtypeStruct((r, c), F32)] * 4,
        compiler_params=_params("parallel"), name=name)(w, g, m, v)


def _sum_parts(parts, *, name):
    p, r, c = parts.shape
    tr = 8
    for cand in (512, 256, 128, 64, 32, 16):
        if r % cand == 0:
            tr = cand
            break

    def body(p_ref, o_ref):
        acc = p_ref[0]
        for k in range(1, p):
            acc = acc + p_ref[k]
        o_ref[...] = acc

    return pl.pallas_call(
        body, grid=(r // tr,),
        in_specs=[pl.BlockSpec((p, tr, c), lambda i: (0, i, 0))],
        out_specs=pl.BlockSpec((tr, c), lambda i: (i, 0)),
        out_shape=jax.ShapeDtypeStruct((r, c), F32),
        compiler_params=_params("parallel"), name=name)(parts)


def _exchange(arrs, *, scatter, name):
    n = len(arrs)
    out_shape = [jax.ShapeDtypeStruct(a.shape if scatter else (N_DEV,) + a.shape, a.dtype) for a in arrs]

    def body(*refs):
        ins, outs = refs[:n], refs[n:2 * n]
        send_sems, recv_sems, loc_sems = refs[2 * n:]
        me = 4 * lax.axis_index("x") + 2 * lax.axis_index("y") + lax.axis_index("c")
        local = []
        for k in range(n):
            src = ins[k].at[me] if scatter else ins[k]
            cp = pltpu.make_async_copy(src, outs[k].at[me], loc_sems.at[k])
            cp.start()
            local.append(cp)
        remote = []
        for r in range(1, N_DEV):
            peer = me ^ r
            for k in range(n):
                src = ins[k].at[peer] if scatter else ins[k]
                cp = pltpu.make_async_remote_copy(
                    src_ref=src, dst_ref=outs[k].at[me], send_sem=send_sems.at[k, r - 1], recv_sem=recv_sems.at[k, r - 1],
                    device_id=(peer // 4, (peer // 2) % 2, peer % 2), device_id_type=pl.DeviceIdType.MESH)
                cp.start()
                remote.append(cp)
        for cp in remote:
            cp.wait()
        for cp in local:
            cp.wait()

    anyspec = pl.BlockSpec(memory_space=pl.ANY)
    return pl.pallas_call(
        body, in_specs=[anyspec] * n, out_specs=[anyspec] * n, out_shape=out_shape,
        scratch_shapes=[pltpu.SemaphoreType.DMA((n, N_DEV - 1)), pltpu.SemaphoreType.DMA((n, N_DEV - 1)),
                        pltpu.SemaphoreType.DMA((n,))],
        name=name)(*arrs)


def _mesh_place():
    x, y, c = lax.axis_index("x"), lax.axis_index("y"), lax.axis_index("c")
    other_chips = [(1 - x, y), (x, 1 - y), (1 - x, 1 - y)]
    return x, y, c, other_chips


def _gather_two_level(arrs, *, name):
    n = len(arrs)
    out_shape = [jax.ShapeDtypeStruct((N_DEV,) + a.shape, a.dtype) for a in arrs]

    def body(*refs):
        ins, outs = refs[:n], refs[n:2 * n]
        send_sems, recv_sems, loc_sems = refs[2 * n:]
        x, y, c, chips = _mesh_place()
        me = 4 * x + 2 * y + c
        slot = lambda px, py, pc: 4 * px + 2 * py + pc

        def copy(k, j, src, block, to):
            return pltpu.make_async_remote_copy(
                src_ref=src, dst_ref=outs[k].at[block], send_sem=send_sems.at[k, j], recv_sem=recv_sems.at[k, j],
                device_id=to, device_id_type=pl.DeviceIdType.MESH)

        local = [pltpu.make_async_copy(ins[k], outs[k].at[me], loc_sems.at[k]) for k in range(n)]
        for cp in local:
            cp.start()
        first = []
        for k in range(n):
            first.append(copy(k, 0, ins[k], me, (x, y, 1 - c)))
            for j, chip in enumerate(chips):
                first.append(copy(k, 1 + j, ins[k], me, (*chip, c)))
        for cp in first:
            cp.start()
        passed = []
        for j, chip in enumerate(chips):
            for k in range(n):
                block = slot(*chip, c)
                copy(k, 1 + j, ins[k], block, (x, y, c)).wait_recv()
                fwd = copy(k, 4 + j, outs[k].at[block], block, (x, y, 1 - c))
                fwd.start()
                passed.append(fwd)
        for k in range(n):
            copy(k, 0, ins[k], slot(x, y, 1 - c), (x, y, c)).wait_recv()
            for j, chip in enumerate(chips):
                copy(k, 4 + j, ins[k], slot(*chip, 1 - c), (x, y, c)).wait_recv()
        for cp in first + passed:
            cp.wait_send()
        for cp in local:
            cp.wait()

    anyspec = pl.BlockSpec(memory_space=pl.ANY)
    return pl.pallas_call(
        body, in_specs=[anyspec] * n, out_specs=[anyspec] * n, out_shape=out_shape,
        scratch_shapes=[pltpu.SemaphoreType.DMA((n, 7)), pltpu.SemaphoreType.DMA((n, 7)), pltpu.SemaphoreType.DMA((n,))],
        name=name)(*arrs)


def _sibling_swap(arrs, *, name):
    n = len(arrs)
    out_shape = [jax.ShapeDtypeStruct(a.shape[1:], a.dtype) for a in arrs]

    def body(*refs):
        ins, outs = refs[:n], refs[n:2 * n]
        send_sems, recv_sems = refs[2 * n:]
        x, y, c, _ = _mesh_place()
        copies = [pltpu.make_async_remote_copy(
            src_ref=ins[k].at[1 - c], dst_ref=outs[k], send_sem=send_sems.at[k], recv_sem=recv_sems.at[k],
            device_id=(x, y, 1 - c), device_id_type=pl.DeviceIdType.MESH) for k in range(n)]
        for cp in copies:
            cp.start()
        for cp in copies:
            cp.wait()

    anyspec = pl.BlockSpec(memory_space=pl.ANY)
    return pl.pallas_call(
        body, in_specs=[anyspec] * n, out_specs=[anyspec] * n, out_shape=out_shape,
        scratch_shapes=[pltpu.SemaphoreType.DMA((n,)), pltpu.SemaphoreType.DMA((n,))], name=name)(*arrs)


def _pair_add(mine, theirs, core, *, name, tr=256):
    _, r, c = mine.shape
    tr = min(tr, r)
    assert r % tr == 0

    def body(core_ref, m_ref, t_ref, o_ref):
        o_ref[...] = (m_ref[0].astype(F32) + t_ref[...].astype(F32)).astype(BF16)

    return pl.pallas_call(
        body,
        grid_spec=pltpu.PrefetchScalarGridSpec(
            num_scalar_prefetch=1, grid=(r // tr,),
            in_specs=[pl.BlockSpec((1, tr, c), lambda i, core_ref: (core_ref[0], i, 0)),
                      pl.BlockSpec((tr, c), lambda i, core_ref: (i, 0))],
            out_specs=pl.BlockSpec((tr, c), lambda i, core_ref: (i, 0))),
        out_shape=jax.ShapeDtypeStruct((r, c), BF16),
        compiler_params=_params("parallel"), name=name)(core, mine, theirs)


def _chip_scatter(arrs, *, name):
    n = len(arrs)
    out_shape = [jax.ShapeDtypeStruct(a.shape, a.dtype) for a in arrs]

    def body(*refs):
        ins, outs = refs[:n], refs[n:2 * n]
        send_sems, recv_sems, loc_sems = refs[2 * n:]
        x, y, c, chips = _mesh_place()
        mine = 2 * x + y
        local = [pltpu.make_async_copy(ins[k].at[mine], outs[k].at[mine], loc_sems.at[k]) for k in range(n)]
        for cp in local:
            cp.start()
        remote = []
        for j, (px, py) in enumerate(chips):
            for k in range(n):
                remote.append(pltpu.make_async_remote_copy(
                    src_ref=ins[k].at[2 * px + py], dst_ref=outs[k].at[mine], send_sem=send_sems.at[k, j],
                    recv_sem=recv_sems.at[k, j], device_id=(px, py, c), device_id_type=pl.DeviceIdType.MESH))
        for cp in remote:
            cp.start()
        for cp in remote:
            cp.wait()
        for cp in local:
            cp.wait()

    anyspec = pl.BlockSpec(memory_space=pl.ANY)
    return pl.pallas_call(
        body, in_specs=[anyspec] * n, out_specs=[anyspec] * n, out_shape=out_shape,
        scratch_shapes=[pltpu.SemaphoreType.DMA((n, 3)), pltpu.SemaphoreType.DMA((n, 3)), pltpu.SemaphoreType.DMA((n,))],
        name=name)(*arrs)


def _ssm_discretize(a_re, a_im, log_dt, b_re, b_im):
    dt = jnp.exp(log_dt)[:, None]
    mag = jnp.exp(a_re * dt)
    lr = mag * jnp.cos(a_im * dt)
    li = mag * jnp.sin(a_im * dt)
    den = a_re * a_re + a_im * a_im
    cr = ((lr - 1.0) * a_re + li * a_im) / den
    ci = (li * a_re - (lr - 1.0) * a_im) / den
    bbr = cr[..., None] * b_re - ci[..., None] * b_im
    bbi = cr[..., None] * b_im + ci[..., None] * b_re
    return lr, li, bbr, bbi


def _ssm_dense(lr, li, bbr, bbi, c_re, c_im):
    eye = jnp.eye(SSM_GROUPS, dtype=F32)
    lam = _scan_tables(lr.reshape(-1), li.reshape(-1)) + _scan_consts(SEQ_BLOCK)
    bblk = jnp.stack([jnp.einsum('gpc,gh->gchp', b, eye).reshape(SSM_W, SSM_N) for b in (bbr, bbi)]).astype(BF16)
    cblk = jnp.stack([jnp.einsum('gcp,gh->gphc', c, eye).reshape(SSM_N, SSM_W) for c in (c_re, c_im)]).astype(BF16)
    return lam, bblk, cblk


def _ssm_extract(db_dense, dc_dense, sums):
    eye = jnp.eye(SSM_GROUPS, dtype=F32)
    db = [jnp.einsum('gchp,gh->gpc', db_dense[k].reshape(SSM_GROUPS, SSM_GROUP, SSM_GROUPS, SSM_STATE), eye) for k in (0, 1)]
    dc = [jnp.einsum('gphc,gh->gcp', dc_dense[k].reshape(SSM_GROUPS, SSM_STATE, SSM_GROUPS, SSM_GROUP), eye) for k in (0, 1)]
    dlr = sums[0].reshape(SSM_GROUPS, SSM_STATE)
    dli = sums[1].reshape(SSM_GROUPS, SSM_STATE)
    return dlr, dli, db[0], db[1], dc[0], dc[1]


IN_SPLITS = (ATT_W, KV_W, KV_W, SSM_W, POOL_W, ATT_W, SSM_W, POOL_W, 3 * D_MODEL)


def _split_w_in(w):
    idx = [0]
    for s in IN_SPLITS:
        idx.append(idx[-1] + s)
    seg = [w[..., idx[k]:idx[k + 1]] for k in range(len(IN_SPLITS))]
    q, k, v, us, up, za, zs, zp, gl = seg
    return (jnp.concatenate([q, za, k, v], axis=-1), jnp.concatenate([us, zs], axis=-1),
            jnp.concatenate([up, zp], axis=-1), gl)


def _merge_w_in(da, ds, dp, dg):
    q, za, k, v = da[..., :ATT_W], da[..., ATT_W:2 * ATT_W], da[..., 2 * ATT_W:2 * ATT_W + KV_W], da[..., 2 * ATT_W + KV_W:]
    us, zs = ds[..., :SSM_W], ds[..., SSM_W:]
    up, zp = dp[..., :POOL_W], dp[..., POOL_W:]
    return jnp.concatenate([q, k, v, us, up, za, zs, zp, dg], axis=-1)


def _layer_fwd(x, lw, li):
    tag = f"l{li}"
    h = _ln_fwd(x, lw["norm_g"], lw["shift"], lw["scale"], name=f"ln_fwd_{tag}")
    pa = _mm(h, lw["w_a"], tn=640, name=f"proj_a_{tag}")
    ps = _mm(h, lw["w_s"], name=f"proj_s_{tag}")
    pp = _mm(h, lw["w_p"], name=f"proj_p_{tag}")
    pg = _mm(h, lw["w_g"], name=f"proj_g_{tag}")
    ya = _attn_fwd(pa, lw["sinks"], name=f"attn_fwd_{tag}")
    ys, chk = _ssm_fwd(ps, lw["lam"], lw["bblk"], lw["cblk"], lw["ssm_d"], lw["w_glu"], lw["b_glu"], name=f"ssm_fwd_{tag}")
    yp = _pool_fwd(pp, lw["w_pool"], lw["pool_scale"], name=f"pool_fwd_{tag}")
    merged, ba, bs, bp = _merge_fwd(ya, ys, yp, lw["w_br_att"], lw["w_br_ssm"], lw["w_br_pool"], pg, name=f"merge_fwd_{tag}")
    x_new, out = _out_fwd(merged, lw["w_out"], x, lw["gate"], name=f"out_fwd_{tag}")
    saved = dict(x=x, h=h, pa=pa, ps=ps, pp=pp, pg=pg, ya=ya, ys=ys, yp=yp, chk=chk, merged=merged,
                 ba=ba, bs=bs, bp=bp, out=out)
    return x_new, saved


def _layer_bwd(dx, lw, sv, li):
    tag = f"l{li}"
    dmo, dba, dbs, dbp, dpg, gate_sums = _merge_bwd(dx, sv["out"], lw["gate"], lw["w_out"], sv["pg"],
                                                    sv["ba"], sv["bs"], sv["bp"], name=f"merge_bwd_{tag}")
    g = {}
    g["w_out"] = _mm_tn(sv["merged"], dmo, out_dtype=BF16, name=f"dw_out_{tag}")
    dya = _mm(dba, lw["w_br_att"], nt=True, name=f"dy_att_{tag}")
    dys = _mm(dbs, lw["w_br_ssm"], nt=True, name=f"dy_ssm_{tag}")
    dyp = _mm(dbp, lw["w_br_pool"], nt=True, name=f"dy_pool_{tag}")
    g["w_br_att"] = _mm_tn(sv["ya"], dba, out_dtype=BF16, name=f"dw_br_att_{tag}")
    g["w_br_ssm"] = _mm_tn(sv["ys"], dbs, out_dtype=BF16, name=f"dw_br_ssm_{tag}")
    g["w_br_pool"] = _mm_tn(sv["yp"], dbp, out_dtype=BF16, name=f"dw_br_pool_{tag}")
    dpa, dsink = _attn_bwd(sv["pa"], lw["sinks"], dya, name=f"attn_bwd_{tag}")
    dps, db_dense, dc_dense, dwglu, ssm_sums = _ssm_bwd(
        sv["ps"], dys, sv["chk"], lw["lam"], lw["bblk"], lw["cblk"], lw["ssm_d"], lw["w_glu"], lw["b_glu"],
        name=f"ssm_bwd_{tag}")
    dpp, dwpool, pool_sums = _pool_bwd(sv["pp"], dyp, lw["w_pool"], lw["pool_scale"], name=f"pool_bwd_{tag}")
    dh = _mm_nt_sum([(dpa, lw["w_a"]), (dps, lw["w_s"]), (dpp, lw["w_p"]), (dpg, lw["w_g"])], name=f"dh_{tag}")
    h = sv["h"]
    g["w_in"] = _merge_w_in(_mm_tn(h, dpa, out_dtype=BF16, tn=640, name=f"dw_a_{tag}"),
                            _mm_tn(h, dps, out_dtype=BF16, name=f"dw_s_{tag}"),
                            _mm_tn(h, dpp, out_dtype=BF16, name=f"dw_p_{tag}"),
                            _mm_tn(h, dpg, out_dtype=BF16, name=f"dw_g_{tag}"))
    dx_in, ln_sums = _ln_bwd(sv["x"], dh, dx, lw["norm_g"], lw["scale"], name=f"ln_bwd_{tag}")
    g["w_glu"] = dwglu.astype(BF16)
    g["dmod"] = jnp.concatenate([ln_sums[0], ln_sums[1], gate_sums[0]])
    g["norm_g"] = ln_sums[2]
    g["attn_sinks"] = dsink[:, 0]
    g["ssm_raw"] = _ssm_extract(db_dense, dc_dense, ssm_sums)
    g["ssm_d"] = ssm_sums[2, :SSM_W]
    g["b_glu"] = ssm_sums[3, :SSM_W]
    g["w_pool"] = dwpool
    g["pool_scale"] = pool_sums[0]
    return dx_in, g


def _local_step(xs, target, layers, final_g):
    saved = []
    for li in range(DEPTH):
        xs, sv = _layer_fwd(xs, layers[li], li)
        saved.append(sv)
    dx, fin_sums = _final_loss(xs, final_g[None, :], target)
    grads = [None] * DEPTH
    for li in reversed(range(DEPTH)):
        dx, grads[li] = _layer_bwd(dx, layers[li], saved[li], li)
    return dx, fin_sums, grads


def _prepare_layer(li, mod, norm_g, w_a, w_s, w_p, w_g, attn_sinks, disc, ssm_c_re, ssm_c_im, ssm_d, w_glu_f, b_glu,
                   w_pool, pool_scale, w_ba_f, w_bs_f, w_bp_f, w_out_f):
    d = D_MODEL
    lr, li_, bbr, bbi = disc
    lam, bblk, cblk = _ssm_dense(lr[li], li_[li], bbr[li], bbi[li], ssm_c_re[li], ssm_c_im[li])
    return dict(
        norm_g=norm_g[li][None, :], shift=mod[li, :d][None, :], scale=mod[li, d:2 * d][None, :],
        gate=mod[li, 2 * d:][None, :], w_a=w_a[li], w_s=w_s[li], w_p=w_p[li], w_g=w_g[li],
        sinks=attn_sinks[li], lam=lam, bblk=bblk, cblk=cblk, ssm_d=ssm_d[li][None, :], w_glu=w_glu_f[li],
        b_glu=b_glu[li][None, :], w_pool=w_pool[li].astype(BF16), pool_scale=pool_scale[li][None, :],
        w_br_att=w_ba_f[li], w_br_ssm=w_bs_f[li], w_br_pool=w_bp_f[li], w_out=w_out_f[li])


SMALL_ROWS = 64
SMALL_ORDER = ("norm_g", "attn_sinks", "ssm_d", "b_glu", "w_pool", "pool_scale", "dmod")


def _pack_small(loss, dfinal_g, layer_grads):
    parts = [jnp.broadcast_to(loss.reshape(1), (128,)), dfinal_g]
    for g in layer_grads:
        for k in SMALL_ORDER:
            v = g[k].reshape(-1)
            if v.shape[0] % 128:
                v = jnp.pad(v, (0, 128 - v.shape[0] % 128))
            parts.append(v)
        for v in g["ssm_raw"]:
            parts.append(v.reshape(-1))
    flat = jnp.concatenate(parts)
    return jnp.pad(flat, (0, (-flat.shape[0]) % (SMALL_ROWS * 128))).reshape(-1, 128)


def _unpack_small(flat, shapes):
    out, off = [], 0
    for s in shapes:
        n = int(math.prod(s))
        out.append(flat[off:off + n].reshape(s))
        off += n + (-n) % 128
    return out


def kernel(x, c, norm_g, w_ada, b_ada, w_in, attn_sinks, ssm_a_re, ssm_a_im, ssm_log_dt, ssm_b_re, ssm_b_im, ssm_c_re, ssm_c_im, ssm_d, w_glu, b_glu, w_pool, pool_scale, w_br_att, w_br_ssm, w_br_pool, w_out, final_g, loss_target, m_norm_g, m_w_ada, m_b_ada, m_w_in, m_attn_sinks, m_ssm_a_re, m_ssm_a_im, m_ssm_log_dt, m_ssm_b_re, m_ssm_b_im, m_ssm_c_re, m_ssm_c_im, m_ssm_d, m_w_glu, m_b_glu, m_w_pool, m_pool_scale, m_w_br_att, m_w_br_ssm, m_w_br_pool, m_w_out, m_final_g, v_norm_g, v_w_ada, v_b_ada, v_w_in, v_attn_sinks, v_ssm_a_re, v_ssm_a_im, v_ssm_log_dt, v_ssm_b_re, v_ssm_b_im, v_ssm_c_re, v_ssm_c_im, v_ssm_d, v_w_glu, v_b_glu, v_w_pool, v_pool_scale, v_w_br_att, v_w_br_ssm, v_w_br_pool, v_w_out, v_final_g):
    me = 4 * lax.axis_index("x") + 2 * lax.axis_index("y") + lax.axis_index("c")
    d = D_MODEL
    ada_w = 3 * d // N_DEV

    (c_all,) = _exchange([c.reshape(8, 128)], scatter=False, name="gather_c")
    c_act = jax.nn.silu(c_all.reshape(N_DEV, d))
    b_cols = lax.dynamic_slice(b_ada, (0, me * ada_w), (DEPTH, ada_w))
    mod_part = jnp.concatenate(
        [_mm(c_act, w_ada[li], name=f"ada_fwd_l{li}") + b_cols[li][None, :] for li in range(DEPTH)], axis=0)
    (mod_all,) = _exchange([mod_part], scatter=False, name="gather_mod")
    mod_all = mod_all.reshape(N_DEV, DEPTH, N_DEV, ada_w)
    mod_mine = lax.dynamic_index_in_dim(mod_all, me, axis=2, keepdims=False)
    mod_mine = mod_mine.transpose(1, 0, 2).reshape(DEPTH, 3 * d)

    gathered = _gather_two_level([w_in.astype(BF16), w_glu.astype(BF16), w_br_att.astype(BF16), w_br_ssm.astype(BF16),
                                  w_br_pool.astype(BF16), w_out.astype(BF16)], name="gather_weights")
    cols = lambda g: g.transpose(1, 2, 0, 3).reshape(g.shape[1], g.shape[2], N_DEV * g.shape[3])
    rows = lambda g: g.transpose(1, 0, 2, 3).reshape(g.shape[1], N_DEV * g.shape[2], g.shape[3])
    w_in_f, w_glu_f, w_ba_f, w_bs_f, w_bp_f, w_out_f = (cols(gathered[0]), rows(gathered[1]), cols(gathered[2]),
                                                        cols(gathered[3]), cols(gathered[4]), rows(gathered[5]))
    w_a, w_s, w_p, w_g = _split_w_in(w_in_f)

    disc, disc_vjp = jax.vjp(jax.vmap(_ssm_discretize), ssm_a_re, ssm_a_im, ssm_log_dt, ssm_b_re, ssm_b_im)
    layers = [_prepare_layer(li, mod_mine, norm_g, w_a, w_s, w_p, w_g, attn_sinks, disc, ssm_c_re, ssm_c_im, ssm_d,
                             w_glu_f, b_glu, w_pool, pool_scale, w_ba_f, w_bs_f, w_bp_f, w_out_f) for li in range(DEPTH)]

    dx, fin_sums, grads = _local_step(x[0], loss_target[0], layers, final_g)
    loss_part = jnp.sum(fin_sums[1])
    grad_x = dx[None]

    stack = lambda k: jnp.stack([grads[li][k] for li in range(DEPTH)])
    to_cols = lambda g: g.reshape(g.shape[0], g.shape[1], 4, 2, g.shape[2] // N_DEV).transpose(3, 2, 0, 1, 4)
    to_rows = lambda g: g.reshape(g.shape[0], 4, 2, g.shape[1] // N_DEV, g.shape[2]).transpose(2, 1, 0, 3, 4)
    by_dest = [to_cols(stack("w_in")), to_rows(stack("w_glu")), to_cols(stack("w_br_att")),
               to_cols(stack("w_br_ssm")), to_cols(stack("w_br_pool")), to_rows(stack("w_out"))]
    from_sibling = _sibling_swap(by_dest, name="grads_sibling_swap")
    core = lax.axis_index("c").astype(jnp.int32).reshape(1)
    chip_sums = [
        _pair_add(a.reshape(2, -1, a.shape[-1]), b.reshape(-1, b.shape[-1]), core, name=f"grads_pair_add_{k}").reshape(b.shape)
        for k, (a, b) in enumerate(zip(by_dest, from_sibling))]
    big = _chip_scatter(chip_sums, name="grads_chip_scatter")

    small = _pack_small(loss_part, fin_sums[0], grads)
    (small_all,) = _exchange([small], scatter=False, name="gather_small")
    out = {}

    def adam(name, w, g_parts, m, v):
        shp = w.shape
        r = int(math.prod(shp[:-1])) if len(shp) > 1 else 1
        w2, m2, v2 = (a.reshape(r, shp[-1]) for a in (w, m, v))
        g2 = g_parts.reshape(g_parts.shape[0], r, shp[-1])
        res = _adamw(w2, g2, m2, v2, name=f"adamw_{name}")
        out[name] = tuple(a.reshape(shp) for a in res)

    flat = _sum_parts(small_all, name="sum_small").reshape(-1)
    shapes = [(128,), (d,)]
    for _ in range(DEPTH):
        shapes += [(d,), (N_HEADS,), (SSM_W,), (SSM_W,), (4, POOL_GW, POOL_GW), (POOL_W,), (3 * d,),
                   (SSM_GROUPS, SSM_STATE), (SSM_GROUPS, SSM_STATE), (SSM_GROUPS, SSM_STATE, SSM_GROUP),
                   (SSM_GROUPS, SSM_STATE, SSM_GROUP), (SSM_GROUPS, SSM_GROUP, SSM_STATE), (SSM_GROUPS, SSM_GROUP, SSM_STATE)]
    un = _unpack_small(flat, shapes)
    loss = un[0][0]
    g_final_g = un[1]
    per = 13
    gl = [un[2 + li * per: 2 + (li + 1) * per] for li in range(DEPTH)]
    st = lambda j: jnp.stack([gl[li][j] for li in range(DEPTH)])
    g_norm_g, g_sinks, g_ssm_d, g_b_glu, g_w_pool, g_pool_scale, g_b_ada = (st(j) for j in range(7))
    d_lr, d_li, d_bbr, d_bbi, g_c_re, g_c_im = (st(j) for j in range(7, 13))
    g_a_re, g_a_im, g_log_dt, g_b_re, g_b_im = disc_vjp((d_lr, d_li, d_bbr, d_bbi))

    dmod_all = jnp.stack([lax.dynamic_slice(
        small_all.reshape(N_DEV, -1)[:, _small_offset(shapes, 2 + li * per + 6):][:, :3 * d], (0, me * ada_w), (N_DEV, ada_w))
        for li in range(DEPTH)])
    g_w_ada = jnp.stack([_mm_tn(c_act, dmod_all[li], tm=d, tn=ada_w, tk=N_DEV, name=f"dw_ada_l{li}") for li in range(DEPTH)])

    adam("w_ada", w_ada, g_w_ada[None], m_w_ada, v_w_ada)
    adam("w_in", w_in, big[0], m_w_in, v_w_in)
    adam("w_glu", w_glu, big[1], m_w_glu, v_w_glu)
    adam("w_br_att", w_br_att, big[2], m_w_br_att, v_w_br_att)
    adam("w_br_ssm", w_br_ssm, big[3], m_w_br_ssm, v_w_br_ssm)
    adam("w_br_pool", w_br_pool, big[4], m_w_br_pool, v_w_br_pool)
    adam("w_out", w_out, big[5], m_w_out, v_w_out)

    small_names = ["norm_g", "b_ada", "attn_sinks", "ssm_a_re", "ssm_a_im", "ssm_log_dt", "ssm_b_re", "ssm_b_im",
                   "ssm_c_re", "ssm_c_im", "ssm_d", "b_glu", "w_pool", "pool_scale", "final_g"]
    small_w = [norm_g, b_ada, attn_sinks, ssm_a_re, ssm_a_im, ssm_log_dt, ssm_b_re, ssm_b_im, ssm_c_re, ssm_c_im,
               ssm_d, b_glu, w_pool, pool_scale, final_g]
    small_m = [m_norm_g, m_b_ada, m_attn_sinks, m_ssm_a_re, m_ssm_a_im, m_ssm_log_dt, m_ssm_b_re, m_ssm_b_im,
               m_ssm_c_re, m_ssm_c_im, m_ssm_d, m_b_glu, m_w_pool, m_pool_scale, m_final_g]
    small_v = [v_norm_g, v_b_ada, v_attn_sinks, v_ssm_a_re, v_ssm_a_im, v_ssm_log_dt, v_ssm_b_re, v_ssm_b_im,
               v_ssm_c_re, v_ssm_c_im, v_ssm_d, v_b_glu, v_w_pool, v_pool_scale, v_final_g]
    small_g = [g_norm_g, g_b_ada, g_sinks, g_a_re, g_a_im, g_log_dt, g_b_re, g_b_im, g_c_re, g_c_im,
               g_ssm_d, g_b_glu, g_w_pool, g_pool_scale, g_final_g]

    def flat_pad(arrs):
        v = jnp.concatenate([a.reshape(-1) for a in arrs])
        return jnp.pad(v, (0, (-v.shape[0]) % (SMALL_ROWS * 128))).reshape(-1, 128)

    fw, fg, fm, fv = flat_pad(small_w), flat_pad(small_g), flat_pad(small_m), flat_pad(small_v)
    _, s_delta, s_m, s_v = _adamw(fw, fg[None], fm, fv, name="adamw_small", tr=SMALL_ROWS)
    off = 0
    for nm, w, g in zip(small_names, small_w, small_g):
        n = int(math.prod(w.shape))
        take = lambda a: a.reshape(-1)[off:off + n].reshape(w.shape)
        out[nm] = (g, take(s_delta), take(s_m), take(s_v))
        off += n

    order = ["norm_g", "w_ada", "b_ada", "w_in", "attn_sinks", "ssm_a_re", "ssm_a_im", "ssm_log_dt", "ssm_b_re",
             "ssm_b_im", "ssm_c_re", "ssm_c_im", "ssm_d", "w_glu", "b_glu", "w_pool", "pool_scale", "w_br_att",
             "w_br_ssm", "w_br_pool", "w_out", "final_g"]
    return (loss, grad_x, *[out[k][0] for k in order], *[out[k][1] for k in order],
            *[out[k][2] for k in order], *[out[k][3] for k in order])


def _small_offset(shapes, idx):
    off = 0
    for s in shapes[:idx]:
        n = int(math.prod(s))
        off += n + (-n) % 128
    return off
```

```python
import functools
import math

import jax
import jax.numpy as jnp
from jax import lax
from jax.experimental import pallas as pl
from jax.experimental.pallas import tpu as pltpu

F32 = jnp.float32
BF16 = jnp.bfloat16

N_DEV = 8
D_MODEL = 1024
DEPTH = 2
CHUNK = 64
N_HEADS = 8
N_KV_HEADS = 2
HEAD_DIM = 64
Q_PER_KV = N_HEADS // N_KV_HEADS
WINDOW = 128
ATT_W = 512
KV_W = 128
SSM_W = 512
SSM_GROUP = 16
SSM_GROUPS = 32
SSM_STATE = 64
SSM_N = SSM_GROUPS * SSM_STATE
POOL_W = 512
POOL_WINDOWS = (2, 4, 8, 16)
POOL_GW = 128
POOL_HALO = 16
EPS = 1e-6
NEG_INF = -1e30
ADAM_LR = 0.001
ADAM_B1 = 0.9
ADAM_B2 = 0.999
ADAM_EPS = 1e-08
ADAM_WD = 0.01
ADAM_STEP = 10

SEQ_BLOCK = 256
VMEM_LIMIT = 56 * 1024 * 1024

NN = (((1,), (0,)), ((), ()))
NT = (((1,), (1,)), ((), ()))
TN = (((0,), (0,)), ((), ()))


def _dot(a, b, dims=NN):
    return lax.dot_general(a.astype(BF16), b.astype(BF16), dims, preferred_element_type=F32)


def _params(*sem):
    return pltpu.CompilerParams(dimension_semantics=sem, vmem_limit_bytes=VMEM_LIMIT)


def _sigmoid(x):
    return 1.0 / (1.0 + jnp.exp(-x))


def _silu_and_grad(z):
    s = _sigmoid(z)
    return z * s, s * (1.0 + z * (1.0 - s))


_GELU_K = math.sqrt(2.0 / math.pi)


def _gelu_and_grad(x):
    inner = _GELU_K * (x + 0.044715 * x * x * x)
    t = jnp.tanh(inner)
    val = 0.5 * x * (1.0 + t)
    grad = 0.5 * (1.0 + t) + 0.5 * x * (1.0 - t * t) * _GELU_K * (1.0 + 3.0 * 0.044715 * x * x)
    return val, grad


def _mm(a, b, *, nt=False, out_dtype=F32, tm=1024, tn=1024, name):
    m, k = a.shape
    n = b.shape[0] if nt else b.shape[1]
    tm, tn = min(tm, m), min(tn, n)
    assert m % tm == 0 and n % tn == 0
    dims = NT if nt else NN

    def body(a_ref, b_ref, o_ref):
        o_ref[...] = _dot(a_ref[...], b_ref[...], dims).astype(out_dtype)

    b_spec = pl.BlockSpec((tn, k), lambda i, j: (j, 0)) if nt else pl.BlockSpec((k, tn), lambda i, j: (0, j))
    return pl.pallas_call(
        body, grid=(m // tm, n // tn),
        in_specs=[pl.BlockSpec((tm, k), lambda i, j: (i, 0)), b_spec],
        out_specs=pl.BlockSpec((tm, tn), lambda i, j: (i, j)),
        out_shape=jax.ShapeDtypeStruct((m, n), out_dtype),
        compiler_params=_params("parallel", "parallel"), name=name)(a, b)


def _mm_nt_sum(pairs, *, out_dtype=F32, tm=512, tn=512, name):
    m = pairs[0][0].shape[0]
    n = pairs[0][1].shape[0]
    np_ = len(pairs)

    def body(*refs):
        o_ref = refs[-1]
        acc = _dot(refs[0][...], refs[1][...], NT)
        for p in range(1, np_):
            acc = acc + _dot(refs[2 * p][...], refs[2 * p + 1][...], NT)
        o_ref[...] = acc.astype(out_dtype)

    in_specs, args = [], []
    for a, b in pairs:
        in_specs.append(pl.BlockSpec((tm, a.shape[1]), lambda i, j: (i, 0)))
        in_specs.append(pl.BlockSpec((tn, b.shape[1]), lambda i, j: (j, 0)))
        args += [a, b]
    return pl.pallas_call(
        body, grid=(m // tm, n // tn), in_specs=in_specs,
        out_specs=pl.BlockSpec((tm, tn), lambda i, j: (i, j)),
        out_shape=jax.ShapeDtypeStruct((m, n), out_dtype),
        compiler_params=_params("parallel", "parallel"), name=name)(*args)


def _mm_tn(a, b, *, out_dtype=F32, tm=1024, tn=1024, tk=1024, name):
    k, m = a.shape
    n = b.shape[1]
    assert m % min(tm, m) == 0 and n % min(tn, n) == 0 and k % min(tk, k) == 0
    tm, tn, tk = min(tm, m), min(tn, n), min(tk, k)
    nk = k // tk

    def body(a_ref, b_ref, o_ref, acc_ref):
        kk = pl.program_id(2)

        @pl.when(kk == 0)
        def _():
            acc_ref[...] = jnp.zeros_like(acc_ref)

        acc_ref[...] += _dot(a_ref[...], b_ref[...], TN)

        @pl.when(kk == nk - 1)
        def _():
            o_ref[...] = acc_ref[...].astype(out_dtype)

    return pl.pallas_call(
        body, grid=(m // tm, n // tn, nk),
        in_specs=[pl.BlockSpec((tk, tm), lambda i, j, kk: (kk, i)), pl.BlockSpec((tk, tn), lambda i, j, kk: (kk, j))],
        out_specs=pl.BlockSpec((tm, tn), lambda i, j, kk: (i, j)),
        out_shape=jax.ShapeDtypeStruct((m, n), out_dtype),
        scratch_shapes=[pltpu.VMEM((tm, tn), F32)],
        compiler_params=_params("parallel", "parallel", "arbitrary"), name=name)(a, b)


def _ln_fwd(x, g, shift, scale, *, name, tm=512):
    l, d = x.shape

    def body(x_ref, g_ref, sh_ref, sc_ref, h_ref):
        xv = x_ref[...]
        n = xv * lax.rsqrt(jnp.mean(xv * xv, axis=-1, keepdims=True) + EPS)
        h_ref[...] = ((n * g_ref[...]) * (1.0 + sc_ref[...]) + sh_ref[...]).astype(BF16)

    vec = pl.BlockSpec((1, d), lambda i: (0, 0))
    return pl.pallas_call(
        body, grid=(l // tm,),
        in_specs=[pl.BlockSpec((tm, d), lambda i: (i, 0)), vec, vec, vec],
        out_specs=pl.BlockSpec((tm, d), lambda i: (i, 0)),
        out_shape=jax.ShapeDtypeStruct((l, d), BF16),
        compiler_params=_params("parallel"), name=name)(x, g, shift, scale)


def _ln_bwd(x, dh, dres, g, scale, *, name, tm=512):
    l, d = x.shape

    def body(x_ref, dh_ref, dres_ref, g_ref, sc_ref, dx_ref, sums_ref):
        xv = x_ref[...]
        dhv = dh_ref[...]
        rstd = lax.rsqrt(jnp.mean(xv * xv, axis=-1, keepdims=True) + EPS)
        n = xv * rstd
        gv = g_ref[...]
        dr = dhv * (1.0 + sc_ref[...])
        dn = dr * gv
        dx_ref[...] = dres_ref[...] + rstd * (dn - n * jnp.mean(dn * n, axis=-1, keepdims=True))

        @pl.when(pl.program_id(0) == 0)
        def _():
            sums_ref[...] = jnp.zeros_like(sums_ref)

        sums_ref[0:1, :] += jnp.sum(dhv, axis=0, keepdims=True)
        sums_ref[1:2, :] += jnp.sum(dhv * (n * gv), axis=0, keepdims=True)
        sums_ref[2:3, :] += jnp.sum(dr * n, axis=0, keepdims=True)

    vec = pl.BlockSpec((1, d), lambda i: (0, 0))
    row = pl.BlockSpec((tm, d), lambda i: (i, 0))
    return pl.pallas_call(
        body, grid=(l // tm,),
        in_specs=[row, row, row, vec, vec],
        out_specs=[row, pl.BlockSpec((8, d), lambda i: (0, 0))],
        out_shape=[jax.ShapeDtypeStruct((l, d), F32), jax.ShapeDtypeStruct((8, d), F32)],
        compiler_params=_params("arbitrary"), name=name)(x, dh, dres, g, scale)


def _final_loss(x, g, target, *, tm=512):
    l, d = x.shape

    def body(x_ref, g_ref, t_ref, dx_ref, sums_ref):
        xv = x_ref[...]
        rstd = lax.rsqrt(jnp.mean(xv * xv, axis=-1, keepdims=True) + EPS)
        n = xv * rstd
        gv = g_ref[...]
        err = n * gv - t_ref[...]
        dy = err * (1.0 / d)
        dn = dy * gv
        dx_ref[...] = rstd * (dn - n * jnp.mean(dn * n, axis=-1, keepdims=True))

        @pl.when(pl.program_id(0) == 0)
        def _():
            sums_ref[...] = jnp.zeros_like(sums_ref)

        sums_ref[0:1, :] += jnp.sum(dy * n, axis=0, keepdims=True)
        sums_ref[1:2, :] += jnp.sum(err * err, axis=0, keepdims=True) * (0.5 / d)

    vec = pl.BlockSpec((1, d), lambda i: (0, 0))
    row = pl.BlockSpec((tm, d), lambda i: (i, 0))
    dx, sums = pl.pallas_call(
        body, grid=(l // tm,),
        in_specs=[row, vec, row],
        out_specs=[row, pl.BlockSpec((8, d), lambda i: (0, 0))],
        out_shape=[jax.ShapeDtypeStruct((l, d), F32), jax.ShapeDtypeStruct((8, d), F32)],
        compiler_params=_params("arbitrary"), name="final_loss")(x, g, target)
    return dx, sums


def _attn_geometry(i, t):
    nk = t + WINDOW
    qi = lax.broadcasted_iota(jnp.int32, (t, nk), 0)
    kj = lax.broadcasted_iota(jnp.int32, (t, nk), 1)
    dist = jnp.abs(qi + WINDOW - kj).astype(F32)
    qc = jnp.right_shift(qi, 6)
    kc = jnp.right_shift(kj, 6)
    valid = (kc >= qc) & (kc <= qc + WINDOW // CHUNK) & ((i > 0) | (kj >= WINDOW))
    return dist, valid


def _attn_head(q, k_all, v_all, sink, slope, dist, valid):
    s = _dot(q, k_all, NT) * (1.0 / math.sqrt(HEAD_DIM)) - slope * dist
    s = jnp.where(valid, s, NEG_INF)
    m = jnp.maximum(jnp.max(s, axis=-1, keepdims=True), sink)
    e = jnp.exp(s - m)
    es = jnp.exp(sink - m)
    inv = 1.0 / (jnp.sum(e, axis=-1, keepdims=True) + es)
    p = e * inv
    o = _dot(p, v_all, NN)
    return p, o, es * inv


def _attn_specs(t):
    cur = pl.BlockSpec((t, ATT_W * 2 + KV_W * 2), lambda i: (i, 0))
    halo_blocks = t // WINDOW
    prev = pl.BlockSpec((WINDOW, 2 * KV_W), lambda i: (jnp.maximum(i * halo_blocks - 1, 0), (2 * ATT_W) // (2 * KV_W)))
    return cur, prev


def _attn_fwd(pa, sinks, *, name, t=SEQ_BLOCK):
    l = pa.shape[0]
    t = min(t, l)

    def body(sink_ref, cur_ref, prev_ref, ya_ref):
        i = pl.program_id(0)
        dist, valid = _attn_geometry(i, t)
        for h in range(N_HEADS):
            kh = h // Q_PER_KV
            q = cur_ref[:, h * HEAD_DIM:(h + 1) * HEAD_DIM]
            z = cur_ref[:, ATT_W + h * HEAD_DIM:ATT_W + (h + 1) * HEAD_DIM]
            k_all = jnp.concatenate([prev_ref[:, kh * HEAD_DIM:(kh + 1) * HEAD_DIM],
                                     cur_ref[:, 2 * ATT_W + kh * HEAD_DIM:2 * ATT_W + (kh + 1) * HEAD_DIM]], axis=0)
            v_all = jnp.concatenate([prev_ref[:, KV_W + kh * HEAD_DIM:KV_W + (kh + 1) * HEAD_DIM],
                                     cur_ref[:, 2 * ATT_W + KV_W + kh * HEAD_DIM:2 * ATT_W + KV_W + (kh + 1) * HEAD_DIM]], axis=0)
            _, o, _ = _attn_head(q, k_all, v_all, sink_ref[h], 2.0 ** (-(h + 1)), dist, valid)
            sz, _ = _silu_and_grad(z)
            ya_ref[:, h * HEAD_DIM:(h + 1) * HEAD_DIM] = (o * sz).astype(BF16)

    cur, prev = _attn_specs(t)
    return pl.pallas_call(
        body, grid=(l // t,),
        in_specs=[pl.BlockSpec(memory_space=pltpu.SMEM), cur, prev],
        out_specs=pl.BlockSpec((t, ATT_W), lambda i: (i, 0)),
        out_shape=jax.ShapeDtypeStruct((l, ATT_W), BF16),
        compiler_params=_params("parallel"), name=name)(sinks, pa, pa)


def _attn_bwd(pa, sinks, dya, *, name, t=SEQ_BLOCK):
    l = pa.shape[0]
    t = min(t, l)
    nb = l // t
    scale = 1.0 / math.sqrt(HEAD_DIM)

    def body(sink_ref, cur_ref, prev_ref, dya_ref, dpa_ref, dsink_ref, carry_ref):
        n = pl.program_id(0)
        i = nb - 1 - n
        dist, valid = _attn_geometry(i, t)

        @pl.when(n == 0)
        def _():
            carry_ref[...] = jnp.zeros_like(carry_ref)
            dsink_ref[...] = jnp.zeros_like(dsink_ref)

        dk_acc = [jnp.zeros((t + WINDOW, HEAD_DIM), F32) for _ in range(N_KV_HEADS)]
        dv_acc = [jnp.zeros((t + WINDOW, HEAD_DIM), F32) for _ in range(N_KV_HEADS)]
        for h in range(N_HEADS):
            kh = h // Q_PER_KV
            q = cur_ref[:, h * HEAD_DIM:(h + 1) * HEAD_DIM]
            z = cur_ref[:, ATT_W + h * HEAD_DIM:ATT_W + (h + 1) * HEAD_DIM]
            k_all = jnp.concatenate([prev_ref[:, kh * HEAD_DIM:(kh + 1) * HEAD_DIM],
                                     cur_ref[:, 2 * ATT_W + kh * HEAD_DIM:2 * ATT_W + (kh + 1) * HEAD_DIM]], axis=0)
            v_all = jnp.concatenate([prev_ref[:, KV_W + kh * HEAD_DIM:KV_W + (kh + 1) * HEAD_DIM],
                                     cur_ref[:, 2 * ATT_W + KV_W + kh * HEAD_DIM:2 * ATT_W + KV_W + (kh + 1) * HEAD_DIM]], axis=0)
            p, o, p_sink = _attn_head(q, k_all, v_all, sink_ref[h], 2.0 ** (-(h + 1)), dist, valid)
            dy = dya_ref[:, h * HEAD_DIM:(h + 1) * HEAD_DIM]
            sz, dsz = _silu_and_grad(z)
            do = dy * sz
            dpa_ref[:, ATT_W + h * HEAD_DIM:ATT_W + (h + 1) * HEAD_DIM] = (dy * o * dsz).astype(BF16)
            delta = jnp.sum(do * o, axis=-1, keepdims=True)
            dp = _dot(do, v_all, NT)
            ds = p * (dp - delta)
            dpa_ref[:, h * HEAD_DIM:(h + 1) * HEAD_DIM] = (_dot(ds, k_all, NN) * scale).astype(BF16)
            dk_acc[kh] = dk_acc[kh] + _dot(ds, q, TN) * scale
            dv_acc[kh] = dv_acc[kh] + _dot(p, do, TN)
            dsink_ref[h:h + 1, :] += jnp.broadcast_to(-jnp.sum(p_sink * delta, axis=0, keepdims=True), (1, 128))

        for kh in range(N_KV_HEADS):
            for which, acc in ((0, dk_acc[kh]), (1, dv_acc[kh])):
                c0 = which * KV_W + kh * HEAD_DIM
                own = acc[WINDOW:, :]
                tail = own[t - WINDOW:, :] + carry_ref[:, c0:c0 + HEAD_DIM]
                dpa_ref[0:t - WINDOW, 2 * ATT_W + c0:2 * ATT_W + c0 + HEAD_DIM] = own[:t - WINDOW, :].astype(BF16)
                dpa_ref[t - WINDOW:t, 2 * ATT_W + c0:2 * ATT_W + c0 + HEAD_DIM] = tail.astype(BF16)
                carry_ref[:, c0:c0 + HEAD_DIM] = acc[:WINDOW, :]

    halo_blocks = t // WINDOW
    wpa = 2 * ATT_W + 2 * KV_W
    cur = pl.BlockSpec((t, wpa), lambda n: (nb - 1 - n, 0))
    prev = pl.BlockSpec((WINDOW, 2 * KV_W),
                        lambda n: (jnp.maximum((nb - 1 - n) * halo_blocks - 1, 0), (2 * ATT_W) // (2 * KV_W)))
    return pl.pallas_call(
        body, grid=(nb,),
        in_specs=[pl.BlockSpec(memory_space=pltpu.SMEM), cur, prev, pl.BlockSpec((t, ATT_W), lambda n: (nb - 1 - n, 0))],
        out_specs=[pl.BlockSpec((t, wpa), lambda n: (nb - 1 - n, 0)), pl.BlockSpec((8, 128), lambda n: (0, 0))],
        out_shape=[jax.ShapeDtypeStruct((l, wpa), BF16), jax.ShapeDtypeStruct((8, 128), F32)],
        scratch_shapes=[pltpu.VMEM((WINDOW, 2 * KV_W), F32)],
        compiler_params=_params("arbitrary"), name=name)(sinks, pa, pa, dya)


def _scan(xr, xi, lr, li, t, reverse):
    row = lax.broadcasted_iota(jnp.int32, (t, 1), 0)
    d = 1
    pr, pi = lr, li
    while d < t:
        if reverse:
            sr = jnp.where(row < t - d, pltpu.roll(xr, t - d, 0), 0.0)
            si = jnp.where(row < t - d, pltpu.roll(xi, t - d, 0), 0.0)
        else:
            sr = jnp.where(row >= d, pltpu.roll(xr, d, 0), 0.0)
            si = jnp.where(row >= d, pltpu.roll(xi, d, 0), 0.0)
        xr, xi = xr + pr * sr - pi * si, xi + pr * si + pi * sr
        pr, pi = pr * pr - pi * pi, 2.0 * pr * pi
        d *= 2
    return xr, xi


SCAN_SUB = 8


def _split_hi_lo(a):
    hi = a.astype(BF16)
    lo = (a - hi.astype(F32)).astype(BF16)
    return jnp.concatenate([hi, lo], axis=0)


def _scan_mxu(xr, xi, tab, lam3, lam8, tri, expand, cr, ci, t, reverse):
    ns = t // SCAN_SUB
    n = xr.shape[1]
    v3 = lambda a: a.reshape(ns, SCAN_SUB, n)
    x3r, x3i = v3(xr), v3(xi)
    br = (x3r * tab[0] - x3i * tab[1]).reshape(t, n)
    bi = (x3r * tab[1] + x3i * tab[0]).reshape(t, n)
    pm = jnp.dot(tri, _split_hi_lo(jnp.concatenate([br, bi], axis=1)), preferred_element_type=F32)
    p3r, p3i = v3(pm[:t, :n]), v3(pm[:t, n:])
    slr = p3r * tab[2] - p3i * tab[3]
    sli = p3r * tab[3] + p3i * tab[2]
    totr, toti = pm[t:, :n], pm[t:, n:]
    l3r, l3i = lam3
    l8r, l8i = lam8
    row = lax.broadcasted_iota(jnp.int32, (ns, 1), 0)
    edge = row == (ns - 1 if reverse else 0)
    er = totr * l3r - toti * l3i + jnp.where(edge, l8r * cr - l8i * ci, 0.0)
    ei = totr * l3i + toti * l3r + jnp.where(edge, l8r * ci + l8i * cr, 0.0)
    er, ei = _scan(er, ei, l8r, l8i, ns, reverse)
    shift = ns - 1 if reverse else 1
    nbr = jnp.where(edge, cr, pltpu.roll(er, shift, 0))
    nbi = jnp.where(edge, ci, pltpu.roll(ei, shift, 0))
    ex = jnp.dot(expand, _split_hi_lo(jnp.concatenate([nbr, nbi], axis=1)), preferred_element_type=F32)
    e3r, e3i = v3(ex[:, :n]), v3(ex[:, n:])
    sr = (slr + e3r * tab[4] - e3i * tab[5]).reshape(t, n)
    si = (sli + e3r * tab[5] + e3i * tab[4]).reshape(t, n)
    out = 0 if reverse else ns - 1
    return sr, si, er[out:out + 1, :], ei[out:out + 1, :]


def _scan_consts(t):
    import numpy as np
    ns = t // SCAN_SUB
    r = np.arange(t)
    same = (r[:, None] // SCAN_SUB) == (r[None, :] // SCAN_SUB)
    sums = (np.arange(ns)[:, None] == (r[None, :] // SCAN_SUB))
    tri = []
    for keep in (r[None, :] <= r[:, None], r[None, :] >= r[:, None]):
        m = np.concatenate([same & keep, sums], axis=0).astype(np.float32)
        tri.append(np.concatenate([m, m], axis=1))
    ex = ((r[:, None] // SCAN_SUB) == np.arange(ns)[None, :]).astype(np.float32)
    return jnp.asarray(np.stack(tri), BF16), jnp.asarray(np.concatenate([ex, ex], axis=1), BF16)


def _scan_tables(lr, li):
    den = lr * lr + li * li
    ir, ii = lr / den, -li / den
    mul = lambda a, b: (a[0] * b[0] - a[1] * b[1], a[0] * b[1] + a[1] * b[0])
    pw = {0: (jnp.ones_like(lr), jnp.zeros_like(lr))}
    for e in range(1, 9):
        pw[e] = mul(pw[e - 1], (lr, li))
    for e in range(-1, -5, -1):
        pw[e] = mul(pw[e + 1], (ir, ii))
    stack = lambda es, sign: (jnp.stack([pw[e][0] for e in es]), sign * jnp.stack([pw[e][1] for e in es]))
    j = range(SCAN_SUB)
    parts = [stack([4 - k for k in j], 1.0), stack([k - 4 for k in j], 1.0), stack([k + 1 for k in j], 1.0),
             stack([k - 3 for k in j], -1.0), stack([3 - k for k in j], -1.0), stack([8 - k for k in j], -1.0)]
    tabs = jnp.stack([a for pair in parts for a in pair])
    lam = jnp.zeros((8, lr.shape[0]), F32)
    for k, v in enumerate((lr, li, pw[3][0], pw[3][1], pw[8][0], pw[8][1])):
        lam = lam.at[k].set(v)
    return lam, tabs


SSM_HALVES = 2
SSM_HW = SSM_W // SSM_HALVES
SSM_HN = SSM_N // SSM_HALVES


def _bd_nn(x, w):
    a = w.shape[1]
    return jnp.concatenate([_dot(x[:, h * a:(h + 1) * a], w[h]) for h in range(SSM_HALVES)], axis=1)


def _bd_nt(x, w):
    b = w.shape[2]
    return jnp.concatenate([_dot(x[:, h * b:(h + 1) * b], w[h], NT) for h in range(SSM_HALVES)], axis=1)


def _bd_tn(x, y):
    a, b = x.shape[1] // SSM_HALVES, y.shape[1] // SSM_HALVES
    return jnp.stack([_dot(x[:, h * a:(h + 1) * a], y[:, h * b:(h + 1) * b], TN) for h in range(SSM_HALVES)])


def _ssm_states(u, s0r, s0i, lam_ref, tab_ref, tri_ref, ex_ref, bre, bim, t):
    tab = tuple(tab_ref[k] for k in range(6))
    return _scan_mxu(_bd_nn(u, bre), _bd_nn(u, bim), tab, (lam_ref[2:3, :], lam_ref[3:4, :]),
                     (lam_ref[4:5, :], lam_ref[5:6, :]), tri_ref[0], ex_ref[...], s0r, s0i, t, False)


def _ssm_head(u, z, xr, xi, cre, cim, dskip, wglu, bglu):
    y = _bd_nn(xr, cre) - _bd_nn(xi, cim) + dskip * u
    y2, dgelu = _gelu_and_grad(y)
    gate = _sigmoid(_dot(y2, wglu) + bglu)
    y3 = y2 * gate
    return y2, dgelu, gate, y3


def _ssm_fwd(ps, scan_ops, bblk, cblk, dskip, wglu, bglu, *, name, t=SEQ_BLOCK):
    l = ps.shape[0]
    assert l % t == 0
    nb = l // t
    ns = t // SCAN_SUB

    def body(ps_ref, lam_ref, tab_ref, tri_ref, ex_ref, b_ref, c_ref, d_ref, w_ref, bg_ref, ys_ref, chk_ref, st_ref):
        @pl.when(pl.program_id(0) == 0)
        def _():
            st_ref[...] = jnp.zeros_like(st_ref)

        chk_ref[...] = jnp.broadcast_to(st_ref[...], chk_ref.shape)
        u = ps_ref[:, :SSM_W]
        z = ps_ref[:, SSM_W:]
        xr, xi, er, ei = _ssm_states(u, st_ref[:, :SSM_N], st_ref[:, SSM_N:], lam_ref, tab_ref, tri_ref, ex_ref,
                                     b_ref[0], b_ref[1], t)
        st_ref[:, :SSM_N] = er
        st_ref[:, SSM_N:] = ei
        _, _, _, y3 = _ssm_head(u, z, xr, xi, c_ref[0], c_ref[1], d_ref[...], w_ref[...], bg_ref[...])
        sz, _ = _silu_and_grad(z)
        ys_ref[...] = (y3 * sz).astype(BF16)

    full = lambda shape: pl.BlockSpec(shape, lambda i: (0,) * len(shape))
    return pl.pallas_call(
        body, grid=(nb,),
        in_specs=[pl.BlockSpec((t, 2 * SSM_W), lambda i: (i, 0)), full((8, SSM_N)), full((12, SCAN_SUB, SSM_N)),
                  full((2, t + ns, 2 * t)), full((t, 2 * ns)), full((2, SSM_HALVES, SSM_HW, SSM_HN)),
                  full((2, SSM_HALVES, SSM_HN, SSM_HW)), full((1, SSM_W)), full((SSM_W, SSM_W)), full((1, SSM_W))],
        out_specs=[pl.BlockSpec((t, SSM_W), lambda i: (i, 0)), pl.BlockSpec((8, 2 * SSM_N), lambda i: (i, 0))],
        out_shape=[jax.ShapeDtypeStruct((l, SSM_W), BF16), jax.ShapeDtypeStruct((nb * 8, 2 * SSM_N), F32)],
        scratch_shapes=[pltpu.VMEM((1, 2 * SSM_N), F32)],
        compiler_params=_params("arbitrary"), name=name)(ps, *scan_ops, bblk, cblk, dskip, wglu, bglu)


def _ssm_bwd(ps, dys, chk, scan_ops, bblk, cblk, dskip, wglu, bglu, *, name, t=SEQ_BLOCK):
    l = ps.shape[0]
    assert l % t == 0
    nb = l // t
    ns = t // SCAN_SUB

    def body(ps_ref, dys_ref, chk_ref, lam_ref, tab_ref, tri_ref, ex_ref, b_ref, c_ref, d_ref, w_ref, bg_ref,
             dps_ref, db_ref, dc_ref, dw_acc, sums_acc, gc_ref, db_acc, dc_acc):
        n = pl.program_id(0)

        @pl.when(n == 0)
        def _():
            gc_ref[...] = jnp.zeros_like(gc_ref)
            db_acc[...] = jnp.zeros_like(db_acc)
            dc_acc[...] = jnp.zeros_like(dc_acc)
            dw_acc[...] = jnp.zeros_like(dw_acc)
            sums_acc[...] = jnp.zeros_like(sums_acc)

        row = lax.broadcasted_iota(jnp.int32, (t, 1), 0)
        u = ps_ref[:, :SSM_W]
        z = ps_ref[:, SSM_W:]
        s0r, s0i = chk_ref[0:1, :SSM_N], chk_ref[0:1, SSM_N:]
        xr, xi, _, _ = _ssm_states(u, s0r, s0i, lam_ref, tab_ref, tri_ref, ex_ref, b_ref[0], b_ref[1], t)
        dskip = d_ref[...]
        y2, dgelu, gate, y3 = _ssm_head(u, z, xr, xi, c_ref[0], c_ref[1], dskip, w_ref[...], bg_ref[...])
        sz, dsz = _silu_and_grad(z)
        dys_v = dys_ref[...]
        dps_ref[:, SSM_W:] = (dys_v * y3 * dsz).astype(BF16)
        dy3 = dys_v * sz
        da = dy3 * y2 * gate * (1.0 - gate)
        dy2 = dy3 * gate + _dot(da, w_ref[...], NT)
        dw_acc[...] += _dot(y2, da, TN)
        dy = dy2 * dgelu
        sums_acc[2:3, :SSM_W] += jnp.sum(dy * u, axis=0, keepdims=True)
        sums_acc[3:4, :SSM_W] += jnp.sum(da, axis=0, keepdims=True)
        dc_acc[0] += _bd_tn(xr, dy)
        dc_acc[1] += -_bd_tn(xi, dy)
        rev_tab = tuple(tab_ref[k] for k in range(6, 12))
        gr, gi, gcr, gci = _scan_mxu(
            _bd_nt(dy, c_ref[0]), -_bd_nt(dy, c_ref[1]), rev_tab, (lam_ref[2:3, :], -lam_ref[3:4, :]),
            (lam_ref[4:5, :], -lam_ref[5:6, :]), tri_ref[1], ex_ref[...], gc_ref[:, :SSM_N], gc_ref[:, SSM_N:], t, True)
        gc_ref[:, :SSM_N] = gcr
        gc_ref[:, SSM_N:] = gci
        db_acc[0] += _bd_tn(u, gr)
        db_acc[1] += _bd_tn(u, gi)
        du = dskip * dy + _bd_nt(gr, b_ref[0]) + _bd_nt(gi, b_ref[1])
        dps_ref[:, :SSM_W] = du.astype(BF16)
        spr = jnp.where(row == 0, s0r, pltpu.roll(xr, 1, 0))
        spi = jnp.where(row == 0, s0i, pltpu.roll(xi, 1, 0))
        sums_acc[0:1, :] += jnp.sum(gr * spr + gi * spi, axis=0, keepdims=True)
        sums_acc[1:2, :] += jnp.sum(gi * spr - gr * spi, axis=0, keepdims=True)

        @pl.when(n == nb - 1)
        def _():
            per_half = SSM_GROUPS // SSM_HALVES
            for k in range(2):
                for g in range(SSM_GROUPS):
                    h, gl = divmod(g, per_half)
                    c0, p0 = gl * SSM_GROUP, gl * SSM_STATE
                    db_ref[k, g * SSM_GROUP:(g + 1) * SSM_GROUP, :] = db_acc[k, h, c0:c0 + SSM_GROUP, p0:p0 + SSM_STATE]
                    dc_ref[k, g * SSM_STATE:(g + 1) * SSM_STATE, :] = dc_acc[k, h, p0:p0 + SSM_STATE, c0:c0 + SSM_GROUP]

    full = lambda shape: pl.BlockSpec(shape, lambda n: (0,) * len(shape))
    return pl.pallas_call(
        body, grid=(nb,),
        in_specs=[pl.BlockSpec((t, 2 * SSM_W), lambda n: (nb - 1 - n, 0)),
                  pl.BlockSpec((t, SSM_W), lambda n: (nb - 1 - n, 0)),
                  pl.BlockSpec((8, 2 * SSM_N), lambda n: (nb - 1 - n, 0)),
                  full((8, SSM_N)), full((12, SCAN_SUB, SSM_N)), full((2, t + ns, 2 * t)), full((t, 2 * ns)),
                  full((2, SSM_HALVES, SSM_HW, SSM_HN)), full((2, SSM_HALVES, SSM_HN, SSM_HW)), full((1, SSM_W)),
                  full((SSM_W, SSM_W)), full((1, SSM_W))],
        out_specs=[pl.BlockSpec((t, 2 * SSM_W), lambda n: (nb - 1 - n, 0)), full((2, SSM_W, SSM_STATE)),
                   full((2, SSM_N, SSM_GROUP)), full((SSM_W, SSM_W)), full((8, SSM_N))],
        out_shape=[jax.ShapeDtypeStruct((l, 2 * SSM_W), BF16),
                   jax.ShapeDtypeStruct((2, SSM_W, SSM_STATE), F32),
                   jax.ShapeDtypeStruct((2, SSM_N, SSM_GROUP), F32),
                   jax.ShapeDtypeStruct((SSM_W, SSM_W), F32),
                   jax.ShapeDtypeStruct((8, SSM_N), F32)],
        scratch_shapes=[pltpu.VMEM((1, 2 * SSM_N), F32), pltpu.VMEM((2, SSM_HALVES, SSM_HW, SSM_HN), F32),
                        pltpu.VMEM((2, SSM_HALVES, SSM_HN, SSM_HW), F32)],
        compiler_params=_params("arbitrary"), name=name)(ps, dys, chk, *scan_ops, bblk, cblk, dskip, wglu, bglu)


def _pool_count(i, t):
    pos = lax.broadcasted_iota(jnp.int32, (t, POOL_W), 0) + i * t + 1
    col = lax.broadcasted_iota(jnp.int32, (t, POOL_W), 1)
    win = jnp.where(col < POOL_GW, 2, jnp.where(col < 2 * POOL_GW, 4, jnp.where(col < 3 * POOL_GW, 8, 16)))
    return 1.0 / jnp.minimum(pos, win).astype(F32), col


def _window_sums(ext, n_rows, forward):
    col = lax.broadcasted_iota(jnp.int32, ext.shape, 1)
    sh = (lambda a, d: pltpu.roll(a, d, 0)) if forward else (lambda a, d: pltpu.roll(a, n_rows - d, 0))
    a2 = ext + sh(ext, 1)
    a4 = a2 + sh(a2, 2)
    a8 = a4 + sh(a4, 4)
    a16 = a8 + sh(a8, 8)
    return jnp.where(col < POOL_GW, a2, jnp.where(col < 2 * POOL_GW, a4, jnp.where(col < 3 * POOL_GW, a8, a16)))


def _pool_mix(pooled, wp_ref):
    return jnp.concatenate([_dot(pooled[:, g * POOL_GW:(g + 1) * POOL_GW], wp_ref[g]) for g in range(4)], axis=1)


def _pool_pooled(i, cur_u, prev_u, t):
    prev = jnp.where(i > 0, prev_u, 0.0)
    ext = jnp.concatenate([prev, cur_u], axis=0)
    inv_cnt, _ = _pool_count(i, t)
    return _window_sums(ext, t + POOL_HALO, True)[POOL_HALO:, :] * inv_cnt - cur_u


def _pool_fwd(pp, wpool, pscale, *, name, t=SEQ_BLOCK):
    l = pp.shape[0]
    t = min(t, l)

    def body(cur_ref, prev_ref, wp_ref, sc_ref, yp_ref):
        i = pl.program_id(0)
        pooled = _pool_pooled(i, cur_ref[:, :POOL_W], prev_ref[...], t)
        lin = _pool_mix(pooled, wp_ref)
        sz, _ = _silu_and_grad(cur_ref[:, POOL_W:])
        yp_ref[...] = (lin * sc_ref[...] * sz).astype(BF16)

    hb = t // POOL_HALO
    return pl.pallas_call(
        body, grid=(l // t,),
        in_specs=[pl.BlockSpec((t, 2 * POOL_W), lambda i: (i, 0)),
                  pl.BlockSpec((POOL_HALO, POOL_W), lambda i: (jnp.maximum(i * hb - 1, 0), 0)),
                  pl.BlockSpec((4, POOL_GW, POOL_GW), lambda i: (0, 0, 0)),
                  pl.BlockSpec((1, POOL_W), lambda i: (0, 0))],
        out_specs=pl.BlockSpec((t, POOL_W), lambda i: (i, 0)),
        out_shape=jax.ShapeDtypeStruct((l, POOL_W), BF16),
        compiler_params=_params("parallel"), name=name)(pp, pp, wpool, pscale)


def _pool_bwd(pp, dyp, wpool, pscale, *, name, t=SEQ_BLOCK):
    l = pp.shape[0]
    t = min(t, l)
    nb = l // t

    def body(cur_ref, prev_ref, dyp_ref, wp_ref, sc_ref, dpp_ref, dwp_ref, sums_ref, carry_ref):
        n = pl.program_id(0)
        i = nb - 1 - n

        @pl.when(n == 0)
        def _():
            carry_ref[...] = jnp.zeros_like(carry_ref)
            dwp_ref[...] = jnp.zeros_like(dwp_ref)
            sums_ref[...] = jnp.zeros_like(sums_ref)

        cur_u = cur_ref[:, :POOL_W]
        pooled = _pool_pooled(i, cur_u, prev_ref[...], t)
        lin = _pool_mix(pooled, wp_ref)
        sz, dsz = _silu_and_grad(cur_ref[:, POOL_W:])
        dyp_v = dyp_ref[...]
        scale = sc_ref[...]
        dpp_ref[:, POOL_W:] = (dyp_v * lin * scale * dsz).astype(BF16)
        dpre = dyp_v * sz
        sums_ref[0:1, :] += jnp.sum(dpre * lin, axis=0, keepdims=True)
        dlin = dpre * scale
        dpooled = []
        for g in range(4):
            dl = dlin[:, g * POOL_GW:(g + 1) * POOL_GW]
            dwp_ref[g] += _dot(pooled[:, g * POOL_GW:(g + 1) * POOL_GW], dl, TN)
            dpooled.append(_dot(dl, wp_ref[g], NT))
        dpooled = jnp.concatenate(dpooled, axis=1)
        inv_cnt, _ = _pool_count(i, t)
        dq = dpooled * inv_cnt
        ext = jnp.concatenate([dq, carry_ref[...]], axis=0)
        du = _window_sums(ext, t + POOL_HALO, False)[:t, :] - dpooled
        dpp_ref[:, :POOL_W] = du.astype(BF16)
        carry_ref[...] = dq[:POOL_HALO, :]

    hb = t // POOL_HALO
    return pl.pallas_call(
        body, grid=(nb,),
        in_specs=[pl.BlockSpec((t, 2 * POOL_W), lambda n: (nb - 1 - n, 0)),
                  pl.BlockSpec((POOL_HALO, POOL_W), lambda n: (jnp.maximum((nb - 1 - n) * hb - 1, 0), 0)),
                  pl.BlockSpec((t, POOL_W), lambda n: (nb - 1 - n, 0)),
                  pl.BlockSpec((4, POOL_GW, POOL_GW), lambda n: (0, 0, 0)),
                  pl.BlockSpec((1, POOL_W), lambda n: (0, 0))],
        out_specs=[pl.BlockSpec((t, 2 * POOL_W), lambda n: (nb - 1 - n, 0)),
                   pl.BlockSpec((4, POOL_GW, POOL_GW), lambda n: (0, 0, 0)),
                   pl.BlockSpec((8, POOL_W), lambda n: (0, 0))],
        out_shape=[jax.ShapeDtypeStruct((l, 2 * POOL_W), BF16), jax.ShapeDtypeStruct((4, POOL_GW, POOL_GW), F32),
                   jax.ShapeDtypeStruct((8, POOL_W), F32)],
        scratch_shapes=[pltpu.VMEM((POOL_HALO, POOL_W), F32)],
        compiler_params=_params("arbitrary"), name=name)(pp, pp, dyp, wpool, pscale)


def _merge_fwd(ya, ys, yp, wa, ws, wp, pg, *, name, tm=256):
    l = ya.shape[0]
    tm = min(tm, l)
    d = D_MODEL

    def body(ya_ref, ys_ref, yp_ref, wa_ref, ws_ref, wp_ref, pg_ref, mg_ref, ba_ref, bs_ref, bp_ref):
        acc = None
        for k, (y_ref, w_ref, b_ref) in enumerate(((ya_ref, wa_ref, ba_ref), (ys_ref, ws_ref, bs_ref),
                                                   (yp_ref, wp_ref, bp_ref))):
            br = _dot(y_ref[...], w_ref[...])
            b_ref[...] = br
            term = _sigmoid(pg_ref[:, k * d:(k + 1) * d]) * br
            acc = term if acc is None else acc + term
        mg_ref[...] = acc.astype(BF16)

    rowy = pl.BlockSpec((tm, ATT_W), lambda i: (i, 0))
    wsp = pl.BlockSpec((ATT_W, d), lambda i: (0, 0))
    rowd = pl.BlockSpec((tm, d), lambda i: (i, 0))
    return pl.pallas_call(
        body, grid=(l // tm,),
        in_specs=[rowy, rowy, rowy, wsp, wsp, wsp, pl.BlockSpec((tm, 3 * d), lambda i: (i, 0))],
        out_specs=[rowd, rowd, rowd, rowd],
        out_shape=[jax.ShapeDtypeStruct((l, d), BF16)] + [jax.ShapeDtypeStruct((l, d), F32)] * 3,
        compiler_params=_params("parallel"), name=name)(ya, ys, yp, wa, ws, wp, pg)


def _out_fwd(merged, wout, x, gate, *, name, tm=512):
    l, d = x.shape
    tm = min(tm, l)

    def body(m_ref, w_ref, x_ref, g_ref, xn_ref, out_ref):
        out = _dot(m_ref[...], w_ref[...])
        out_ref[...] = out
        xn_ref[...] = x_ref[...] + g_ref[...] * out

    row = pl.BlockSpec((tm, d), lambda i: (i, 0))
    return pl.pallas_call(
        body, grid=(l // tm,),
        in_specs=[row, pl.BlockSpec((d, d), lambda i: (0, 0)), row, pl.BlockSpec((1, d), lambda i: (0, 0))],
        out_specs=[row, row],
        out_shape=[jax.ShapeDtypeStruct((l, d), F32)] * 2,
        compiler_params=_params("parallel"), name=name)(merged, wout, x, gate)


def _merge_bwd(dx, out, gate, wout, pg, ba, bs, bp, *, name, tm=256):
    l, d = dx.shape
    tm = min(tm, l)

    def body(dx_ref, out_ref, g_ref, w_ref, pg_ref, ba_ref, bs_ref, bp_ref,
             dmo_ref, dba_ref, dbs_ref, dbp_ref, dpg_ref, sums_ref):
        @pl.when(pl.program_id(0) == 0)
        def _():
            sums_ref[...] = jnp.zeros_like(sums_ref)

        dxv = dx_ref[...]
        sums_ref[0:1, :] += jnp.sum(dxv * out_ref[...], axis=0, keepdims=True)
        dmo = (dxv * g_ref[...]).astype(BF16)
        dmo_ref[...] = dmo
        dmerged = _dot(dmo, w_ref[...], NT)
        for k, (b_ref, db_ref) in enumerate(((ba_ref, dba_ref), (bs_ref, dbs_ref), (bp_ref, dbp_ref))):
            gk = _sigmoid(pg_ref[:, k * d:(k + 1) * d])
            db_ref[...] = (dmerged * gk).astype(BF16)
            dpg_ref[:, k * d:(k + 1) * d] = (dmerged * b_ref[...] * gk * (1.0 - gk)).astype(BF16)

    row = pl.BlockSpec((tm, d), lambda i: (i, 0))
    wide = pl.BlockSpec((tm, 3 * d), lambda i: (i, 0))
    return pl.pallas_call(
        body, grid=(l // tm,),
        in_specs=[row, row, pl.BlockSpec((1, d), lambda i: (0, 0)), pl.BlockSpec((d, d), lambda i: (0, 0)),
                  wide, row, row, row],
        out_specs=[row, row, row, row, wide, pl.BlockSpec((8, d), lambda i: (0, 0))],
        out_shape=[jax.ShapeDtypeStruct((l, d), BF16)] * 4 + [jax.ShapeDtypeStruct((l, 3 * d), BF16),
                                                             jax.ShapeDtypeStruct((8, d), F32)],
        compiler_params=_params("arbitrary"), name=name)(dx, out, gate, wout, pg, ba, bs, bp)


def _adamw(w, g, m, v, *, name, tr=256):
    r, c = w.shape
    p = g.shape[0]
    tr = min(tr, r)
    assert r % tr == 0
    c1 = 1.0 / (1.0 - ADAM_B1 ** ADAM_STEP)
    c2 = 1.0 / (1.0 - ADAM_B2 ** ADAM_STEP)

    def body(w_ref, g_ref, m_ref, v_ref, go_ref, d_ref, mo_ref, vo_ref):
        gv = g_ref[0].astype(F32)
        for k in range(1, p):
            gv = gv + g_ref[k].astype(F32)
        go_ref[...] = gv
        mn = ADAM_B1 * m_ref[...] + (1.0 - ADAM_B1) * gv
        vn = ADAM_B2 * v_ref[...] + (1.0 - ADAM_B2) * (gv * gv)
        mo_ref[...] = mn
        vo_ref[...] = vn
        d_ref[...] = -ADAM_LR * ((mn * c1) / (jnp.sqrt(vn * c2) + ADAM_EPS) + ADAM_WD * w_ref[...])

    row = pl.BlockSpec((tr, c), lambda i: (i, 0))
    return pl.pallas_call(
        body, grid=(r // tr,),
        in_specs=[row, pl.BlockSpec((p, tr, c), lambda i: (0, i, 0)), row, row],
        out_specs=[row] * 4,
        out_shape=[jax.ShapeDtypeStruct((r, c), F32)] * 4,
        compiler_params=_params("parallel"), name=name)(w, g, m, v)


def _exchange(arrs, *, scatter, name):
    n = len(arrs)
    out_shape = [jax.ShapeDtypeStruct(a.shape if scatter else (N_DEV,) + a.shape, a.dtype) for a in arrs]

    def body(*refs):
        ins, outs = refs[:n], refs[n:2 * n]
        send_sems, recv_sems, loc_sems = refs[2 * n:]
        me = 4 * lax.axis_index("x") + 2 * lax.axis_index("y") + lax.axis_index("c")
        local = []
        for k in range(n):
            src = ins[k].at[me] if scatter else ins[k]
            cp = pltpu.make_async_copy(src, outs[k].at[me], loc_sems.at[k])
            cp.start()
            local.append(cp)
        remote = []
        for r in range(1, N_DEV):
            peer = me ^ r
            for k in range(n):
                src = ins[k].at[peer] if scatter else ins[k]
                cp = pltpu.make_async_remote_copy(
                    src_ref=src, dst_ref=outs[k].at[me], send_sem=send_sems.at[k, r - 1], recv_sem=recv_sems.at[k, r - 1],
                    device_id=(peer // 4, (peer // 2) % 2, peer % 2), device_id_type=pl.DeviceIdType.MESH)
                cp.start()
                remote.append(cp)
        for cp in remote:
            cp.wait()
        for cp in local:
            cp.wait()

    anyspec = pl.BlockSpec(memory_space=pl.ANY)
    return pl.pallas_call(
        body, in_specs=[anyspec] * n, out_specs=[anyspec] * n, out_shape=out_shape,
        scratch_shapes=[pltpu.SemaphoreType.DMA((n, N_DEV - 1)), pltpu.SemaphoreType.DMA((n, N_DEV - 1)),
                        pltpu.SemaphoreType.DMA((n,))],
        name=name)(*arrs)


def _mesh_place():
    x, y, c = lax.axis_index("x"), lax.axis_index("y"), lax.axis_index("c")
    other_chips = [(1 - x, y), (x, 1 - y), (1 - x, 1 - y)]
    return x, y, c, other_chips


def _gather_two_level(arrs, *, name):
    n = len(arrs)
    out_shape = [jax.ShapeDtypeStruct((N_DEV,) + a.shape, a.dtype) for a in arrs]

    def body(*refs):
        ins, outs = refs[:n], refs[n:2 * n]
        send_sems, recv_sems, loc_sems = refs[2 * n:]
        x, y, c, chips = _mesh_place()
        me = 4 * x + 2 * y + c
        slot = lambda px, py, pc: 4 * px + 2 * py + pc

        def copy(k, j, src, block, to):
            return pltpu.make_async_remote_copy(
                src_ref=src, dst_ref=outs[k].at[block], send_sem=send_sems.at[k, j], recv_sem=recv_sems.at[k, j],
                device_id=to, device_id_type=pl.DeviceIdType.MESH)

        local = [pltpu.make_async_copy(ins[k], outs[k].at[me], loc_sems.at[k]) for k in range(n)]
        for cp in local:
            cp.start()
        first = []
        for k in range(n):
            first.append(copy(k, 0, ins[k], me, (x, y, 1 - c)))
            for j, chip in enumerate(chips):
                first.append(copy(k, 1 + j, ins[k], me, (*chip, c)))
        for cp in first:
            cp.start()
        passed = []
        for j, chip in enumerate(chips):
            for k in range(n):
                block = slot(*chip, c)
                copy(k, 1 + j, ins[k], block, (x, y, c)).wait_recv()
                fwd = copy(k, 4 + j, outs[k].at[block], block, (x, y, 1 - c))
                fwd.start()
                passed.append(fwd)
        for k in range(n):
            copy(k, 0, ins[k], slot(x, y, 1 - c), (x, y, c)).wait_recv()
            for j, chip in enumerate(chips):
                copy(k, 4 + j, ins[k], slot(*chip, 1 - c), (x, y, c)).wait_recv()
        for cp in first + passed:
            cp.wait_send()
        for cp in local:
            cp.wait()

    anyspec = pl.BlockSpec(memory_space=pl.ANY)
    return pl.pallas_call(
        body, in_specs=[anyspec] * n, out_specs=[anyspec] * n, out_shape=out_shape,
        scratch_shapes=[pltpu.SemaphoreType.DMA((n, 7)), pltpu.SemaphoreType.DMA((n, 7)), pltpu.SemaphoreType.DMA((n,))],
        name=name)(*arrs)


def _allreduce_small(small, extra, *, name):
    r, lanes = small.shape
    assert r % 16 == 0
    h = r // 2
    e = extra.shape[0]

    def body(s_ref, x_ref, out_ref, xall_ref, sib_ref, parts_ref, send_sems, recv_sems):
        x, y, c, chips = _mesh_place()
        me = 4 * x + 2 * y + c
        my_chip = 2 * x + y
        sibling = (x, y, 1 - c)
        mine = pl.ds(pl.multiple_of(c * h, 8), h)
        theirs = pl.ds(pl.multiple_of((1 - c) * h, 8), h)

        def remote(j, src, dst, to):
            return pltpu.make_async_remote_copy(src_ref=src, dst_ref=dst, send_sem=send_sems.at[j],
                                                recv_sem=recv_sems.at[j], device_id=to, device_id_type=pl.DeviceIdType.MESH)

        to_sibling = remote(0, s_ref.at[theirs], sib_ref, sibling)
        to_sibling.start()
        xall_ref[me] = x_ref[...]
        extras = []
        for rr in range(1, N_DEV):
            peer = me ^ rr
            cp = remote(4 + rr, x_ref, xall_ref.at[me], (peer // 4, (peer // 2) % 2, peer % 2))
            cp.start()
            extras.append(cp)
        to_sibling.wait_recv()
        parts_ref[my_chip] = s_ref[mine] + sib_ref[...]
        to_chips = [remote(1 + j, parts_ref.at[my_chip], parts_ref.at[my_chip], (px, py, c))
                    for j, (px, py) in enumerate(chips)]
        for cp in to_chips:
            cp.start()
        for cp in to_chips:
            cp.wait_recv()
        out_ref[mine] = (parts_ref[0] + parts_ref[1]) + (parts_ref[2] + parts_ref[3])
        done = remote(4, out_ref.at[mine], out_ref.at[mine], sibling)
        done.start()
        remote(4, out_ref.at[theirs], out_ref.at[theirs], sibling).wait_recv()
        for cp in extras:
            cp.wait()
        to_sibling.wait_send()
        for cp in to_chips:
            cp.wait_send()
        done.wait_send()

    vmem = pl.BlockSpec(memory_space=pltpu.VMEM)
    return pl.pallas_call(
        body, in_specs=[vmem, vmem], out_specs=[vmem, vmem],
        out_shape=[jax.ShapeDtypeStruct((r, lanes), F32), jax.ShapeDtypeStruct((N_DEV, e, lanes), F32)],
        scratch_shapes=[pltpu.VMEM((h, lanes), F32), pltpu.VMEM((4, h, lanes), F32),
                        pltpu.SemaphoreType.DMA((12,)), pltpu.SemaphoreType.DMA((12,))],
        compiler_params=pltpu.CompilerParams(vmem_limit_bytes=VMEM_LIMIT), name=name)(small, extra)


def _sibling_swap(arrs, *, name):
    n = len(arrs)
    out_shape = [jax.ShapeDtypeStruct(a.shape[1:], a.dtype) for a in arrs]

    def body(*refs):
        ins, outs = refs[:n], refs[n:2 * n]
        send_sems, recv_sems = refs[2 * n:]
        x, y, c, _ = _mesh_place()
        copies = [pltpu.make_async_remote_copy(
            src_ref=ins[k].at[1 - c], dst_ref=outs[k], send_sem=send_sems.at[k], recv_sem=recv_sems.at[k],
            device_id=(x, y, 1 - c), device_id_type=pl.DeviceIdType.MESH) for k in range(n)]
        for cp in copies:
            cp.start()
        for cp in copies:
            cp.wait()

    anyspec = pl.BlockSpec(memory_space=pl.ANY)
    return pl.pallas_call(
        body, in_specs=[anyspec] * n, out_specs=[anyspec] * n, out_shape=out_shape,
        scratch_shapes=[pltpu.SemaphoreType.DMA((n,)), pltpu.SemaphoreType.DMA((n,))], name=name)(*arrs)


def _pair_add(mine, theirs, core, *, name, tr=256):
    _, r, c = mine.shape
    tr = min(tr, r)
    assert r % tr == 0

    def body(core_ref, m_ref, t_ref, o_ref):
        o_ref[...] = (m_ref[0].astype(F32) + t_ref[...].astype(F32)).astype(BF16)

    return pl.pallas_call(
        body,
        grid_spec=pltpu.PrefetchScalarGridSpec(
            num_scalar_prefetch=1, grid=(r // tr,),
            in_specs=[pl.BlockSpec((1, tr, c), lambda i, core_ref: (core_ref[0], i, 0)),
                      pl.BlockSpec((tr, c), lambda i, core_ref: (i, 0))],
            out_specs=pl.BlockSpec((tr, c), lambda i, core_ref: (i, 0))),
        out_shape=jax.ShapeDtypeStruct((r, c), BF16),
        compiler_params=_params("parallel"), name=name)(core, mine, theirs)


def _chip_scatter(arrs, *, name):
    n = len(arrs)
    out_shape = [jax.ShapeDtypeStruct(a.shape, a.dtype) for a in arrs]

    def body(*refs):
        ins, outs = refs[:n], refs[n:2 * n]
        send_sems, recv_sems, loc_sems = refs[2 * n:]
        x, y, c, chips = _mesh_place()
        mine = 2 * x + y
        local = [pltpu.make_async_copy(ins[k].at[mine], outs[k].at[mine], loc_sems.at[k]) for k in range(n)]
        for cp in local:
            cp.start()
        remote = []
        for j, (px, py) in enumerate(chips):
            for k in range(n):
                remote.append(pltpu.make_async_remote_copy(
                    src_ref=ins[k].at[2 * px + py], dst_ref=outs[k].at[mine], send_sem=send_sems.at[k, j],
                    recv_sem=recv_sems.at[k, j], device_id=(px, py, c), device_id_type=pl.DeviceIdType.MESH))
        for cp in remote:
            cp.start()
        for cp in remote:
            cp.wait()
        for cp in local:
            cp.wait()

    anyspec = pl.BlockSpec(memory_space=pl.ANY)
    return pl.pallas_call(
        body, in_specs=[anyspec] * n, out_specs=[anyspec] * n, out_shape=out_shape,
        scratch_shapes=[pltpu.SemaphoreType.DMA((n, 3)), pltpu.SemaphoreType.DMA((n, 3)), pltpu.SemaphoreType.DMA((n,))],
        name=name)(*arrs)


def _ssm_discretize(a_re, a_im, log_dt, b_re, b_im):
    dt = jnp.exp(log_dt)[:, None]
    mag = jnp.exp(a_re * dt)
    lr = mag * jnp.cos(a_im * dt)
    li = mag * jnp.sin(a_im * dt)
    den = a_re * a_re + a_im * a_im
    cr = ((lr - 1.0) * a_re + li * a_im) / den
    ci = (li * a_re - (lr - 1.0) * a_im) / den
    bbr = cr[..., None] * b_re - ci[..., None] * b_im
    bbi = cr[..., None] * b_im + ci[..., None] * b_re
    return lr, li, bbr, bbi


def _ssm_dense(lr, li, bbr, bbi, c_re, c_im):
    scan_ops = _scan_tables(lr.reshape(-1), li.reshape(-1)) + _scan_consts(SEQ_BLOCK)
    per_half = SSM_GROUPS // SSM_HALVES

    def halves(a, rows, cols):
        a = a.reshape(SSM_HALVES, per_half * rows, cols)
        tiled = jnp.tile(a, (1, 1, per_half))
        r = lax.broadcasted_iota(jnp.int32, tiled.shape, 1) // rows
        c = lax.broadcasted_iota(jnp.int32, tiled.shape, 2) // cols
        return jnp.where(r == c, tiled, 0.0)

    bblk = jnp.stack([halves(b.transpose(0, 2, 1), SSM_GROUP, SSM_STATE) for b in (bbr, bbi)]).astype(BF16)
    cblk = jnp.stack([halves(c.transpose(0, 2, 1), SSM_STATE, SSM_GROUP) for c in (c_re, c_im)]).astype(BF16)
    return scan_ops, bblk, cblk


def _ssm_extract(db, dc, sums):
    db = db.reshape(2, SSM_GROUPS, SSM_GROUP, SSM_STATE).transpose(0, 1, 3, 2)
    dc = dc.reshape(2, SSM_GROUPS, SSM_STATE, SSM_GROUP).transpose(0, 1, 3, 2)
    dlr = sums[0].reshape(SSM_GROUPS, SSM_STATE)
    dli = sums[1].reshape(SSM_GROUPS, SSM_STATE)
    return dlr, dli, db[0], db[1], dc[0], dc[1]


IN_SPLITS = (ATT_W, KV_W, KV_W, SSM_W, POOL_W, ATT_W, SSM_W, POOL_W, 3 * D_MODEL)


def _split_w_in(w):
    idx = [0]
    for s in IN_SPLITS:
        idx.append(idx[-1] + s)
    seg = [w[..., idx[k]:idx[k + 1]] for k in range(len(IN_SPLITS))]
    q, k, v, us, up, za, zs, zp, gl = seg
    return (jnp.concatenate([q, za, k, v], axis=-1), jnp.concatenate([us, zs], axis=-1),
            jnp.concatenate([up, zp], axis=-1), gl)


def _merge_w_in(da, ds, dp, dg):
    q, za, k, v = da[..., :ATT_W], da[..., ATT_W:2 * ATT_W], da[..., 2 * ATT_W:2 * ATT_W + KV_W], da[..., 2 * ATT_W + KV_W:]
    us, zs = ds[..., :SSM_W], ds[..., SSM_W:]
    up, zp = dp[..., :POOL_W], dp[..., POOL_W:]
    return jnp.concatenate([q, k, v, us, up, za, zs, zp, dg], axis=-1)


def _layer_fwd(x, lw, li):
    tag = f"l{li}"
    h = _ln_fwd(x, lw["norm_g"], lw["shift"], lw["scale"], name=f"ln_fwd_{tag}")
    pa = _mm(h, lw["w_a"], tn=1280, name=f"proj_a_{tag}")
    ps = _mm(h, lw["w_s"], name=f"proj_s_{tag}")
    pp = _mm(h, lw["w_p"], name=f"proj_p_{tag}")
    pg = _mm(h, lw["w_g"], name=f"proj_g_{tag}")
    ya = _attn_fwd(pa, lw["sinks"], name=f"attn_fwd_{tag}")
    ys, chk = _ssm_fwd(ps, lw["lam"], lw["bblk"], lw["cblk"], lw["ssm_d"], lw["w_glu"], lw["b_glu"], name=f"ssm_fwd_{tag}")
    yp = _pool_fwd(pp, lw["w_pool"], lw["pool_scale"], name=f"pool_fwd_{tag}")
    merged, ba, bs, bp = _merge_fwd(ya, ys, yp, lw["w_br_att"], lw["w_br_ssm"], lw["w_br_pool"], pg, name=f"merge_fwd_{tag}")
    x_new, out = _out_fwd(merged, lw["w_out"], x, lw["gate"], name=f"out_fwd_{tag}")
    saved = dict(x=x, h=h, pa=pa, ps=ps, pp=pp, pg=pg, ya=ya, ys=ys, yp=yp, chk=chk, merged=merged,
                 ba=ba, bs=bs, bp=bp, out=out)
    return x_new, saved


def _layer_bwd(dx, lw, sv, li):
    tag = f"l{li}"
    dmo, dba, dbs, dbp, dpg, gate_sums = _merge_bwd(dx, sv["out"], lw["gate"], lw["w_out"], sv["pg"],
                                                    sv["ba"], sv["bs"], sv["bp"], name=f"merge_bwd_{tag}")
    g = {}
    g["w_out"] = _mm_tn(sv["merged"], dmo, out_dtype=BF16, name=f"dw_out_{tag}")
    dya = _mm(dba, lw["w_br_att"], nt=True, name=f"dy_att_{tag}")
    dys = _mm(dbs, lw["w_br_ssm"], nt=True, name=f"dy_ssm_{tag}")
    dyp = _mm(dbp, lw["w_br_pool"], nt=True, name=f"dy_pool_{tag}")
    g["w_br_att"] = _mm_tn(sv["ya"], dba, out_dtype=BF16, name=f"dw_br_att_{tag}")
    g["w_br_ssm"] = _mm_tn(sv["ys"], dbs, out_dtype=BF16, name=f"dw_br_ssm_{tag}")
    g["w_br_pool"] = _mm_tn(sv["yp"], dbp, out_dtype=BF16, name=f"dw_br_pool_{tag}")
    dpa, dsink = _attn_bwd(sv["pa"], lw["sinks"], dya, name=f"attn_bwd_{tag}")
    dps, db_dense, dc_dense, dwglu, ssm_sums = _ssm_bwd(
        sv["ps"], dys, sv["chk"], lw["lam"], lw["bblk"], lw["cblk"], lw["ssm_d"], lw["w_glu"], lw["b_glu"],
        name=f"ssm_bwd_{tag}")
    dpp, dwpool, pool_sums = _pool_bwd(sv["pp"], dyp, lw["w_pool"], lw["pool_scale"], name=f"pool_bwd_{tag}")
    dh = _mm_nt_sum([(dpa, lw["w_a"]), (dps, lw["w_s"]), (dpp, lw["w_p"]), (dpg, lw["w_g"])], name=f"dh_{tag}")
    h = sv["h"]
    g["w_in"] = _merge_w_in(_mm_tn(h, dpa, out_dtype=BF16, tn=1280, name=f"dw_a_{tag}"),
                            _mm_tn(h, dps, out_dtype=BF16, name=f"dw_s_{tag}"),
                            _mm_tn(h, dpp, out_dtype=BF16, name=f"dw_p_{tag}"),
                            _mm_tn(h, dpg, out_dtype=BF16, name=f"dw_g_{tag}"))
    dx_in, ln_sums = _ln_bwd(sv["x"], dh, dx, lw["norm_g"], lw["scale"], name=f"ln_bwd_{tag}")
    g["w_glu"] = dwglu.astype(BF16)
    g["dmod"] = jnp.concatenate([ln_sums[0], ln_sums[1], gate_sums[0]])
    g["norm_g"] = ln_sums[2]
    g["attn_sinks"] = dsink[:, 0]
    g["ssm_raw"] = _ssm_extract(db_dense, dc_dense, ssm_sums)
    g["ssm_d"] = ssm_sums[2, :SSM_W]
    g["b_glu"] = ssm_sums[3, :SSM_W]
    g["w_pool"] = dwpool
    g["pool_scale"] = pool_sums[0]
    return dx_in, g


def _local_step(xs, target, layers, final_g):
    saved = []
    for li in range(DEPTH):
        xs, sv = _layer_fwd(xs, layers[li], li)
        saved.append(sv)
    dx, fin_sums = _final_loss(xs, final_g[None, :], target)
    grads = [None] * DEPTH
    for li in reversed(range(DEPTH)):
        dx, grads[li] = _layer_bwd(dx, layers[li], saved[li], li)
    return dx, fin_sums, grads


def _prepare_layer(li, mod, norm_g, w_a, w_s, w_p, w_g, attn_sinks, disc, ssm_c_re, ssm_c_im, ssm_d, w_glu_f, b_glu,
                   w_pool, pool_scale, w_ba_f, w_bs_f, w_bp_f, w_out_f):
    d = D_MODEL
    lr, li_, bbr, bbi = disc
    lam, bblk, cblk = _ssm_dense(lr[li], li_[li], bbr[li], bbi[li], ssm_c_re[li], ssm_c_im[li])
    return dict(
        norm_g=norm_g[li][None, :], shift=mod[li, :d][None, :], scale=mod[li, d:2 * d][None, :],
        gate=mod[li, 2 * d:][None, :], w_a=w_a[li], w_s=w_s[li], w_p=w_p[li], w_g=w_g[li],
        sinks=attn_sinks[li], lam=lam, bblk=bblk, cblk=cblk, ssm_d=ssm_d[li][None, :], w_glu=w_glu_f[li],
        b_glu=b_glu[li][None, :], w_pool=w_pool[li].astype(BF16), pool_scale=pool_scale[li][None, :],
        w_br_att=w_ba_f[li], w_br_ssm=w_bs_f[li], w_br_pool=w_bp_f[li], w_out=w_out_f[li])


SMALL_ROWS = 64
SMALL_ORDER = ("norm_g", "attn_sinks", "ssm_d", "b_glu", "w_pool", "pool_scale", "dmod")


def _pack_small(loss, dfinal_g, layer_grads):
    parts = [jnp.broadcast_to(loss.reshape(1), (128,)), dfinal_g]
    for g in layer_grads:
        for k in SMALL_ORDER:
            v = g[k].reshape(-1)
            if v.shape[0] % 128:
                v = jnp.pad(v, (0, 128 - v.shape[0] % 128))
            parts.append(v)
        for v in g["ssm_raw"]:
            parts.append(v.reshape(-1))
    flat = jnp.concatenate(parts)
    return jnp.pad(flat, (0, (-flat.shape[0]) % (SMALL_ROWS * 128))).reshape(-1, 128)


def _unpack_small(flat, shapes):
    out, off = [], 0
    for s in shapes:
        n = int(math.prod(s))
        out.append(flat[off:off + n].reshape(s))
        off += n + (-n) % 128
    return out


def kernel(x, c, norm_g, w_ada, b_ada, w_in, attn_sinks, ssm_a_re, ssm_a_im, ssm_log_dt, ssm_b_re, ssm_b_im, ssm_c_re, ssm_c_im, ssm_d, w_glu, b_glu, w_pool, pool_scale, w_br_att, w_br_ssm, w_br_pool, w_out, final_g, loss_target, m_norm_g, m_w_ada, m_b_ada, m_w_in, m_attn_sinks, m_ssm_a_re, m_ssm_a_im, m_ssm_log_dt, m_ssm_b_re, m_ssm_b_im, m_ssm_c_re, m_ssm_c_im, m_ssm_d, m_w_glu, m_b_glu, m_w_pool, m_pool_scale, m_w_br_att, m_w_br_ssm, m_w_br_pool, m_w_out, m_final_g, v_norm_g, v_w_ada, v_b_ada, v_w_in, v_attn_sinks, v_ssm_a_re, v_ssm_a_im, v_ssm_log_dt, v_ssm_b_re, v_ssm_b_im, v_ssm_c_re, v_ssm_c_im, v_ssm_d, v_w_glu, v_b_glu, v_w_pool, v_pool_scale, v_w_br_att, v_w_br_ssm, v_w_br_pool, v_w_out, v_final_g):
    me = 4 * lax.axis_index("x") + 2 * lax.axis_index("y") + lax.axis_index("c")
    d = D_MODEL
    ada_w = 3 * d // N_DEV

    (c_all,) = _exchange([c.reshape(8, 128)], scatter=False, name="gather_c")
    c_act = jax.nn.silu(c_all.reshape(N_DEV, d))
    b_cols = lax.dynamic_slice(b_ada, (0, me * ada_w), (DEPTH, ada_w))
    mod_part = jnp.concatenate(
        [_mm(c_act, w_ada[li], name=f"ada_fwd_l{li}") + b_cols[li][None, :] for li in range(DEPTH)], axis=0)
    (mod_all,) = _exchange([mod_part], scatter=False, name="gather_mod")
    mod_all = mod_all.reshape(N_DEV, DEPTH, N_DEV, ada_w)
    mod_mine = lax.dynamic_index_in_dim(mod_all, me, axis=2, keepdims=False)
    mod_mine = mod_mine.transpose(1, 0, 2).reshape(DEPTH, 3 * d)

    gathered = _gather_two_level([w_in.astype(BF16), w_glu.astype(BF16), w_br_att.astype(BF16), w_br_ssm.astype(BF16),
                                  w_br_pool.astype(BF16), w_out.astype(BF16)], name="gather_weights")
    cols = lambda g: g.transpose(1, 2, 0, 3).reshape(g.shape[1], g.shape[2], N_DEV * g.shape[3])
    rows = lambda g: g.transpose(1, 0, 2, 3).reshape(g.shape[1], N_DEV * g.shape[2], g.shape[3])
    w_in_f, w_glu_f, w_ba_f, w_bs_f, w_bp_f, w_out_f = (cols(gathered[0]), rows(gathered[1]), cols(gathered[2]),
                                                        cols(gathered[3]), cols(gathered[4]), rows(gathered[5]))
    w_a, w_s, w_p, w_g = _split_w_in(w_in_f)

    disc, disc_vjp = jax.vjp(jax.vmap(_ssm_discretize), ssm_a_re, ssm_a_im, ssm_log_dt, ssm_b_re, ssm_b_im)
    layers = [_prepare_layer(li, mod_mine, norm_g, w_a, w_s, w_p, w_g, attn_sinks, disc, ssm_c_re, ssm_c_im, ssm_d,
                             w_glu_f, b_glu, w_pool, pool_scale, w_ba_f, w_bs_f, w_bp_f, w_out_f) for li in range(DEPTH)]

    dx, fin_sums, grads = _local_step(x[0], loss_target[0], layers, final_g)
    loss_part = jnp.sum(fin_sums[1])
    grad_x = dx[None]

    stack = lambda k: jnp.stack([grads[li][k] for li in range(DEPTH)])
    to_cols = lambda g: g.reshape(g.shape[0], g.shape[1], 4, 2, g.shape[2] // N_DEV).transpose(3, 2, 0, 1, 4)
    to_rows = lambda g: g.reshape(g.shape[0], 4, 2, g.shape[1] // N_DEV, g.shape[2]).transpose(2, 1, 0, 3, 4)
    by_dest = [to_cols(stack("w_in")), to_rows(stack("w_glu")), to_cols(stack("w_br_att")),
               to_cols(stack("w_br_ssm")), to_cols(stack("w_br_pool")), to_rows(stack("w_out"))]
    from_sibling = _sibling_swap(by_dest, name="grads_sibling_swap")
    core = lax.axis_index("c").astype(jnp.int32).reshape(1)
    chip_sums = [
        _pair_add(a.reshape(2, -1, a.shape[-1]), b.reshape(-1, b.shape[-1]), core, name=f"grads_pair_add_{k}").reshape(b.shape)
        for k, (a, b) in enumerate(zip(by_dest, from_sibling))]
    big = _chip_scatter(chip_sums, name="grads_chip_scatter")

    small = _pack_small(loss_part, fin_sums[0], grads)
    dmod_rows = jnp.concatenate([grads[li]["dmod"] for li in range(DEPTH)]).reshape(-1, 128)
    small_sum, dmod_gathered = _allreduce_small(small, dmod_rows, name="allreduce_small")
    out = {}

    def adam(name, w, g_parts, m, v):
        shp = w.shape
        r = int(math.prod(shp[:-1])) if len(shp) > 1 else 1
        w2, m2, v2 = (a.reshape(r, shp[-1]) for a in (w, m, v))
        g2 = g_parts.reshape(g_parts.shape[0], r, shp[-1])
        res = _adamw(w2, g2, m2, v2, name=f"adamw_{name}")
        out[name] = tuple(a.reshape(shp) for a in res)

    flat = small_sum.reshape(-1)
    shapes = [(128,), (d,)]
    for _ in range(DEPTH):
        shapes += [(d,), (N_HEADS,), (SSM_W,), (SSM_W,), (4, POOL_GW, POOL_GW), (POOL_W,), (3 * d,),
                   (SSM_GROUPS, SSM_STATE), (SSM_GROUPS, SSM_STATE), (SSM_GROUPS, SSM_STATE, SSM_GROUP),
                   (SSM_GROUPS, SSM_STATE, SSM_GROUP), (SSM_GROUPS, SSM_GROUP, SSM_STATE), (SSM_GROUPS, SSM_GROUP, SSM_STATE)]
    un = _unpack_small(flat, shapes)
    loss = un[0][0]
    g_final_g = un[1]
    per = 13
    gl = [un[2 + li * per: 2 + (li + 1) * per] for li in range(DEPTH)]
    st = lambda j: jnp.stack([gl[li][j] for li in range(DEPTH)])
    g_norm_g, g_sinks, g_ssm_d, g_b_glu, g_w_pool, g_pool_scale, g_b_ada = (st(j) for j in range(7))
    d_lr, d_li, d_bbr, d_bbi, g_c_re, g_c_im = (st(j) for j in range(7, 13))
    g_a_re, g_a_im, g_log_dt, g_b_re, g_b_im = disc_vjp((d_lr, d_li, d_bbr, d_bbi))

    dmod_all = lax.dynamic_slice(dmod_gathered.reshape(N_DEV, DEPTH, 3 * d), (0, 0, me * ada_w), (N_DEV, DEPTH, ada_w))
    dmod_all = dmod_all.transpose(1, 0, 2)
    g_w_ada = jnp.stack([_mm_tn(c_act, dmod_all[li], tm=d, tn=ada_w, tk=N_DEV, name=f"dw_ada_l{li}") for li in range(DEPTH)])

    adam("w_ada", w_ada, g_w_ada[None], m_w_ada, v_w_ada)
    adam("w_in", w_in, big[0], m_w_in, v_w_in)
    adam("w_glu", w_glu, big[1], m_w_glu, v_w_glu)
    adam("w_br_att", w_br_att, big[2], m_w_br_att, v_w_br_att)
    adam("w_br_ssm", w_br_ssm, big[3], m_w_br_ssm, v_w_br_ssm)
    adam("w_br_pool", w_br_pool, big[4], m_w_br_pool, v_w_br_pool)
    adam("w_out", w_out, big[5], m_w_out, v_w_out)

    small_names = ["norm_g", "b_ada", "attn_sinks", "ssm_a_re", "ssm_a_im", "ssm_log_dt", "ssm_b_re", "ssm_b_im",
                   "ssm_c_re", "ssm_c_im", "ssm_d", "b_glu", "w_pool", "pool_scale", "final_g"]
    small_w = [norm_g, b_ada, attn_sinks, ssm_a_re, ssm_a_im, ssm_log_dt, ssm_b_re, ssm_b_im, ssm_c_re, ssm_c_im,
               ssm_d, b_glu, w_pool, pool_scale, final_g]
    small_m = [m_norm_g, m_b_ada, m_attn_sinks, m_ssm_a_re, m_ssm_a_im, m_ssm_log_dt, m_ssm_b_re, m_ssm_b_im,
               m_ssm_c_re, m_ssm_c_im, m_ssm_d, m_b_glu, m_w_pool, m_pool_scale, m_final_g]
    small_v = [v_norm_g, v_b_ada, v_attn_sinks, v_ssm_a_re, v_ssm_a_im, v_ssm_log_dt, v_ssm_b_re, v_ssm_b_im,
               v_ssm_c_re, v_ssm_c_im, v_ssm_d, v_b_glu, v_w_pool, v_pool_scale, v_final_g]
    small_g = [g_norm_g, g_b_ada, g_sinks, g_a_re, g_a_im, g_log_dt, g_b_re, g_b_im, g_c_re, g_c_im,
               g_ssm_d, g_b_glu, g_w_pool, g_pool_scale, g_final_g]

    for nm, w, g, m, v in zip(small_names, small_w, small_g, small_m, small_v):
        adam(nm, w, g[None], m, v)

    order = ["norm_g", "w_ada", "b_ada", "w_in", "attn_sinks", "ssm_a_re", "ssm_a_im", "ssm_log_dt", "ssm_b_re",
             "ssm_b_im", "ssm_c_re", "ssm_c_im", "ssm_d", "w_glu", "b_glu", "w_pool", "pool_scale", "w_br_att",
             "w_br_ssm", "w_br_pool", "w_out", "final_g"]
    return (loss, grad_x, *[out[k][0] for k in order], *[out[k][1] for k in order],
            *[out[k][2] for k in order], *[out[k][3] for k in order])
```

```python
import functools
import math

import jax
import jax.numpy as jnp
from jax import lax
from jax.experimental import pallas as pl
from jax.experimental.pallas import tpu as pltpu

F32 = jnp.float32
BF16 = jnp.bfloat16

N_DEV = 8
D_MODEL = 1024
DEPTH = 2
CHUNK = 64
N_HEADS = 8
N_KV_HEADS = 2
HEAD_DIM = 64
Q_PER_KV = N_HEADS // N_KV_HEADS
WINDOW = 128
ATT_W = 512
KV_W = 128
SSM_W = 512
SSM_GROUP = 16
SSM_GROUPS = 32
SSM_STATE = 64
SSM_N = SSM_GROUPS * SSM_STATE
POOL_W = 512
POOL_WINDOWS = (2, 4, 8, 16)
POOL_GW = 128
POOL_HALO = 16
EPS = 1e-6
NEG_INF = -1e30
ADAM_LR = 0.001
ADAM_B1 = 0.9
ADAM_B2 = 0.999
ADAM_EPS = 1e-08
ADAM_WD = 0.01
ADAM_STEP = 10

SEQ_BLOCK = 256
VMEM_LIMIT = 56 * 1024 * 1024

NN = (((1,), (0,)), ((), ()))
NT = (((1,), (1,)), ((), ()))
TN = (((0,), (0,)), ((), ()))


def _dot(a, b, dims=NN):
    return lax.dot_general(a.astype(BF16), b.astype(BF16), dims, preferred_element_type=F32)


def _params(*sem):
    return pltpu.CompilerParams(dimension_semantics=sem, vmem_limit_bytes=VMEM_LIMIT)


def _sigmoid(x):
    return 1.0 / (1.0 + jnp.exp(-x))


def _silu_and_grad(z):
    s = _sigmoid(z)
    return z * s, s * (1.0 + z * (1.0 - s))


_GELU_K = math.sqrt(2.0 / math.pi)


def _gelu_and_grad(x):
    inner = _GELU_K * (x + 0.044715 * x * x * x)
    t = jnp.tanh(inner)
    val = 0.5 * x * (1.0 + t)
    grad = 0.5 * (1.0 + t) + 0.5 * x * (1.0 - t * t) * _GELU_K * (1.0 + 3.0 * 0.044715 * x * x)
    return val, grad


def _mm(a, b, *, nt=False, out_dtype=F32, tm=1024, tn=1024, name):
    m, k = a.shape
    n = b.shape[0] if nt else b.shape[1]
    tm, tn = min(tm, m), min(tn, n)
    assert m % tm == 0 and n % tn == 0
    dims = NT if nt else NN

    def body(a_ref, b_ref, o_ref):
        o_ref[...] = _dot(a_ref[...], b_ref[...], dims).astype(out_dtype)

    b_spec = pl.BlockSpec((tn, k), lambda i, j: (j, 0)) if nt else pl.BlockSpec((k, tn), lambda i, j: (0, j))
    return pl.pallas_call(
        body, grid=(m // tm, n // tn),
        in_specs=[pl.BlockSpec((tm, k), lambda i, j: (i, 0)), b_spec],
        out_specs=pl.BlockSpec((tm, tn), lambda i, j: (i, j)),
        out_shape=jax.ShapeDtypeStruct((m, n), out_dtype),
        compiler_params=_params("parallel", "parallel"), name=name)(a, b)


def _mm_nt_sum(pairs, *, out_dtype=F32, tm=512, tn=512, name):
    m = pairs[0][0].shape[0]
    n = pairs[0][1].shape[0]
    np_ = len(pairs)

    def body(*refs):
        o_ref = refs[-1]
        acc = _dot(refs[0][...], refs[1][...], NT)
        for p in range(1, np_):
            acc = acc + _dot(refs[2 * p][...], refs[2 * p + 1][...], NT)
        o_ref[...] = acc.astype(out_dtype)

    in_specs, args = [], []
    for a, b in pairs:
        in_specs.append(pl.BlockSpec((tm, a.shape[1]), lambda i, j: (i, 0)))
        in_specs.append(pl.BlockSpec((tn, b.shape[1]), lambda i, j: (j, 0)))
        args += [a, b]
    return pl.pallas_call(
        body, grid=(m // tm, n // tn), in_specs=in_specs,
        out_specs=pl.BlockSpec((tm, tn), lambda i, j: (i, j)),
        out_shape=jax.ShapeDtypeStruct((m, n), out_dtype),
        compiler_params=_params("parallel", "parallel"), name=name)(*args)


def _mm_tn(a, b, *, out_dtype=F32, tm=1024, tn=1024, tk=1024, name):
    k, m = a.shape
    n = b.shape[1]
    assert m % min(tm, m) == 0 and n % min(tn, n) == 0 and k % min(tk, k) == 0
    tm, tn, tk = min(tm, m), min(tn, n), min(tk, k)
    nk = k // tk

    def body(a_ref, b_ref, o_ref, acc_ref):
        kk = pl.program_id(2)

        @pl.when(kk == 0)
        def _():
            acc_ref[...] = jnp.zeros_like(acc_ref)

        acc_ref[...] += _dot(a_ref[...], b_ref[...], TN)

        @pl.when(kk == nk - 1)
        def _():
            o_ref[...] = acc_ref[...].astype(out_dtype)

    return pl.pallas_call(
        body, grid=(m // tm, n // tn, nk),
        in_specs=[pl.BlockSpec((tk, tm), lambda i, j, kk: (kk, i)), pl.BlockSpec((tk, tn), lambda i, j, kk: (kk, j))],
        out_specs=pl.BlockSpec((tm, tn), lambda i, j, kk: (i, j)),
        out_shape=jax.ShapeDtypeStruct((m, n), out_dtype),
        scratch_shapes=[pltpu.VMEM((tm, tn), F32)],
        compiler_params=_params("parallel", "parallel", "arbitrary"), name=name)(a, b)


def _ln_fwd(x, g, shift, scale, *, name, tm=512):
    l, d = x.shape

    def body(x_ref, g_ref, sh_ref, sc_ref, h_ref):
        xv = x_ref[...]
        n = xv * lax.rsqrt(jnp.mean(xv * xv, axis=-1, keepdims=True) + EPS)
        h_ref[...] = ((n * g_ref[...]) * (1.0 + sc_ref[...]) + sh_ref[...]).astype(BF16)

    vec = pl.BlockSpec((1, d), lambda i: (0, 0))
    return pl.pallas_call(
        body, grid=(l // tm,),
        in_specs=[pl.BlockSpec((tm, d), lambda i: (i, 0)), vec, vec, vec],
        out_specs=pl.BlockSpec((tm, d), lambda i: (i, 0)),
        out_shape=jax.ShapeDtypeStruct((l, d), BF16),
        compiler_params=_params("parallel"), name=name)(x, g, shift, scale)


def _ln_bwd(x, dh, dres, g, scale, *, name, tm=512):
    l, d = x.shape

    def body(x_ref, dh_ref, dres_ref, g_ref, sc_ref, dx_ref, sums_ref):
        xv = x_ref[...]
        dhv = dh_ref[...]
        rstd = lax.rsqrt(jnp.mean(xv * xv, axis=-1, keepdims=True) + EPS)
        n = xv * rstd
        gv = g_ref[...]
        dr = dhv * (1.0 + sc_ref[...])
        dn = dr * gv
        dx_ref[...] = dres_ref[...] + rstd * (dn - n * jnp.mean(dn * n, axis=-1, keepdims=True))

        @pl.when(pl.program_id(0) == 0)
        def _():
            sums_ref[...] = jnp.zeros_like(sums_ref)

        sums_ref[0:1, :] += jnp.sum(dhv, axis=0, keepdims=True)
        sums_ref[1:2, :] += jnp.sum(dhv * (n * gv), axis=0, keepdims=True)
        sums_ref[2:3, :] += jnp.sum(dr * n, axis=0, keepdims=True)

    vec = pl.BlockSpec((1, d), lambda i: (0, 0))
    row = pl.BlockSpec((tm, d), lambda i: (i, 0))
    return pl.pallas_call(
        body, grid=(l // tm,),
        in_specs=[row, row, row, vec, vec],
        out_specs=[row, pl.BlockSpec((8, d), lambda i: (0, 0))],
        out_shape=[jax.ShapeDtypeStruct((l, d), F32), jax.ShapeDtypeStruct((8, d), F32)],
        compiler_params=_params("arbitrary"), name=name)(x, dh, dres, g, scale)


def _final_loss(x, g, target, *, tm=512):
    l, d = x.shape

    def body(x_ref, g_ref, t_ref, dx_ref, sums_ref):
        xv = x_ref[...]
        rstd = lax.rsqrt(jnp.mean(xv * xv, axis=-1, keepdims=True) + EPS)
        n = xv * rstd
        gv = g_ref[...]
        err = n * gv - t_ref[...]
        dy = err * (1.0 / d)
        dn = dy * gv
        dx_ref[...] = rstd * (dn - n * jnp.mean(dn * n, axis=-1, keepdims=True))

        @pl.when(pl.program_id(0) == 0)
        def _():
            sums_ref[...] = jnp.zeros_like(sums_ref)

        sums_ref[0:1, :] += jnp.sum(dy * n, axis=0, keepdims=True)
        sums_ref[1:2, :] += jnp.sum(err * err, axis=0, keepdims=True) * (0.5 / d)

    vec = pl.BlockSpec((1, d), lambda i: (0, 0))
    row = pl.BlockSpec((tm, d), lambda i: (i, 0))
    dx, sums = pl.pallas_call(
        body, grid=(l // tm,),
        in_specs=[row, vec, row],
        out_specs=[row, pl.BlockSpec((8, d), lambda i: (0, 0))],
        out_shape=[jax.ShapeDtypeStruct((l, d), F32), jax.ShapeDtypeStruct((8, d), F32)],
        compiler_params=_params("arbitrary"), name="final_loss")(x, g, target)
    return dx, sums


def _attn_geometry(i, t):
    nk = t + WINDOW
    qi = lax.broadcasted_iota(jnp.int32, (t, nk), 0)
    kj = lax.broadcasted_iota(jnp.int32, (t, nk), 1)
    dist = jnp.abs(qi + WINDOW - kj).astype(F32)
    qc = jnp.right_shift(qi, 6)
    kc = jnp.right_shift(kj, 6)
    valid = (kc >= qc) & (kc <= qc + WINDOW // CHUNK) & ((i > 0) | (kj >= WINDOW))
    return dist, valid


def _attn_head(q, k_all, v_all, sink, slope, dist, valid):
    s = _dot(q, k_all, NT) * (1.0 / math.sqrt(HEAD_DIM)) - slope * dist
    s = jnp.where(valid, s, NEG_INF)
    m = jnp.maximum(jnp.max(s, axis=-1, keepdims=True), sink)
    e = jnp.exp(s - m)
    es = jnp.exp(sink - m)
    inv = 1.0 / (jnp.sum(e, axis=-1, keepdims=True) + es)
    p = e * inv
    o = _dot(p, v_all, NN)
    return p, o, es * inv


def _attn_specs(t):
    cur = pl.BlockSpec((t, ATT_W * 2 + KV_W * 2), lambda i: (i, 0))
    halo_blocks = t // WINDOW
    prev = pl.BlockSpec((WINDOW, 2 * KV_W), lambda i: (jnp.maximum(i * halo_blocks - 1, 0), (2 * ATT_W) // (2 * KV_W)))
    return cur, prev


def _attn_fwd(pa, sinks, *, name, t=SEQ_BLOCK):
    l = pa.shape[0]
    t = min(t, l)

    def body(sink_ref, cur_ref, prev_ref, ya_ref):
        i = pl.program_id(0)
        dist, valid = _attn_geometry(i, t)
        for h in range(N_HEADS):
            kh = h // Q_PER_KV
            q = cur_ref[:, h * HEAD_DIM:(h + 1) * HEAD_DIM]
            z = cur_ref[:, ATT_W + h * HEAD_DIM:ATT_W + (h + 1) * HEAD_DIM]
            k_all = jnp.concatenate([prev_ref[:, kh * HEAD_DIM:(kh + 1) * HEAD_DIM],
                                     cur_ref[:, 2 * ATT_W + kh * HEAD_DIM:2 * ATT_W + (kh + 1) * HEAD_DIM]], axis=0)
            v_all = jnp.concatenate([prev_ref[:, KV_W + kh * HEAD_DIM:KV_W + (kh + 1) * HEAD_DIM],
                                     cur_ref[:, 2 * ATT_W + KV_W + kh * HEAD_DIM:2 * ATT_W + KV_W + (kh + 1) * HEAD_DIM]], axis=0)
            _, o, _ = _attn_head(q, k_all, v_all, sink_ref[h], 2.0 ** (-(h + 1)), dist, valid)
            sz, _ = _silu_and_grad(z)
            ya_ref[:, h * HEAD_DIM:(h + 1) * HEAD_DIM] = (o * sz).astype(BF16)

    cur, prev = _attn_specs(t)
    return pl.pallas_call(
        body, grid=(l // t,),
        in_specs=[pl.BlockSpec(memory_space=pltpu.SMEM), cur, prev],
        out_specs=pl.BlockSpec((t, ATT_W), lambda i: (i, 0)),
        out_shape=jax.ShapeDtypeStruct((l, ATT_W), BF16),
        compiler_params=_params("parallel"), name=name)(sinks, pa, pa)


def _attn_bwd(pa, sinks, dya, *, name, t=SEQ_BLOCK):
    l = pa.shape[0]
    t = min(t, l)
    nb = l // t
    scale = 1.0 / math.sqrt(HEAD_DIM)

    def body(sink_ref, cur_ref, prev_ref, dya_ref, dpa_ref, dsink_ref, carry_ref):
        n = pl.program_id(0)
        i = nb - 1 - n
        dist, valid = _attn_geometry(i, t)

        @pl.when(n == 0)
        def _():
            carry_ref[...] = jnp.zeros_like(carry_ref)
            dsink_ref[...] = jnp.zeros_like(dsink_ref)

        dk_acc = [jnp.zeros((t + WINDOW, HEAD_DIM), F32) for _ in range(N_KV_HEADS)]
        dv_acc = [jnp.zeros((t + WINDOW, HEAD_DIM), F32) for _ in range(N_KV_HEADS)]
        for h in range(N_HEADS):
            kh = h // Q_PER_KV
            q = cur_ref[:, h * HEAD_DIM:(h + 1) * HEAD_DIM]
            z = cur_ref[:, ATT_W + h * HEAD_DIM:ATT_W + (h + 1) * HEAD_DIM]
            k_all = jnp.concatenate([prev_ref[:, kh * HEAD_DIM:(kh + 1) * HEAD_DIM],
                                     cur_ref[:, 2 * ATT_W + kh * HEAD_DIM:2 * ATT_W + (kh + 1) * HEAD_DIM]], axis=0)
            v_all = jnp.concatenate([prev_ref[:, KV_W + kh * HEAD_DIM:KV_W + (kh + 1) * HEAD_DIM],
                                     cur_ref[:, 2 * ATT_W + KV_W + kh * HEAD_DIM:2 * ATT_W + KV_W + (kh + 1) * HEAD_DIM]], axis=0)
            p, o, p_sink = _attn_head(q, k_all, v_all, sink_ref[h], 2.0 ** (-(h + 1)), dist, valid)
            dy = dya_ref[:, h * HEAD_DIM:(h + 1) * HEAD_DIM]
            sz, dsz = _silu_and_grad(z)
            do = dy * sz
            dpa_ref[:, ATT_W + h * HEAD_DIM:ATT_W + (h + 1) * HEAD_DIM] = (dy * o * dsz).astype(BF16)
            delta = jnp.sum(do * o, axis=-1, keepdims=True)
            dp = _dot(do, v_all, NT)
            ds = p * (dp - delta)
            dpa_ref[:, h * HEAD_DIM:(h + 1) * HEAD_DIM] = (_dot(ds, k_all, NN) * scale).astype(BF16)
            dk_acc[kh] = dk_acc[kh] + _dot(ds, q, TN) * scale
            dv_acc[kh] = dv_acc[kh] + _dot(p, do, TN)
            dsink_ref[h:h + 1, :] += jnp.broadcast_to(-jnp.sum(p_sink * delta, axis=0, keepdims=True), (1, 128))

        for kh in range(N_KV_HEADS):
            for which, acc in ((0, dk_acc[kh]), (1, dv_acc[kh])):
                c0 = which * KV_W + kh * HEAD_DIM
                own = acc[WINDOW:, :]
                tail = own[t - WINDOW:, :] + carry_ref[:, c0:c0 + HEAD_DIM]
                dpa_ref[0:t - WINDOW, 2 * ATT_W + c0:2 * ATT_W + c0 + HEAD_DIM] = own[:t - WINDOW, :].astype(BF16)
                dpa_ref[t - WINDOW:t, 2 * ATT_W + c0:2 * ATT_W + c0 + HEAD_DIM] = tail.astype(BF16)
                carry_ref[:, c0:c0 + HEAD_DIM] = acc[:WINDOW, :]

    halo_blocks = t // WINDOW
    wpa = 2 * ATT_W + 2 * KV_W
    cur = pl.BlockSpec((t, wpa), lambda n: (nb - 1 - n, 0))
    prev = pl.BlockSpec((WINDOW, 2 * KV_W),
                        lambda n: (jnp.maximum((nb - 1 - n) * halo_blocks - 1, 0), (2 * ATT_W) // (2 * KV_W)))
    return pl.pallas_call(
        body, grid=(nb,),
        in_specs=[pl.BlockSpec(memory_space=pltpu.SMEM), cur, prev, pl.BlockSpec((t, ATT_W), lambda n: (nb - 1 - n, 0))],
        out_specs=[pl.BlockSpec((t, wpa), lambda n: (nb - 1 - n, 0)), pl.BlockSpec((8, 128), lambda n: (0, 0))],
        out_shape=[jax.ShapeDtypeStruct((l, wpa), BF16), jax.ShapeDtypeStruct((8, 128), F32)],
        scratch_shapes=[pltpu.VMEM((WINDOW, 2 * KV_W), F32)],
        compiler_params=_params("arbitrary"), name=name)(sinks, pa, pa, dya)


def _scan(xr, xi, lr, li, t, reverse):
    row = lax.broadcasted_iota(jnp.int32, (t, 1), 0)
    d = 1
    pr, pi = lr, li
    while d < t:
        if reverse:
            sr = jnp.where(row < t - d, pltpu.roll(xr, t - d, 0), 0.0)
            si = jnp.where(row < t - d, pltpu.roll(xi, t - d, 0), 0.0)
        else:
            sr = jnp.where(row >= d, pltpu.roll(xr, d, 0), 0.0)
            si = jnp.where(row >= d, pltpu.roll(xi, d, 0), 0.0)
        xr, xi = xr + pr * sr - pi * si, xi + pr * si + pi * sr
        pr, pi = pr * pr - pi * pi, 2.0 * pr * pi
        d *= 2
    return xr, xi


SCAN_SUB = 8


def _split_hi_lo(a):
    hi = a.astype(BF16)
    lo = (a - hi.astype(F32)).astype(BF16)
    return jnp.concatenate([hi, lo], axis=0)


def _scan_mxu(xr, xi, tab, lam3, lam8, tri, expand, cr, ci, t, reverse):
    ns = t // SCAN_SUB
    n = xr.shape[1]
    v3 = lambda a: a.reshape(ns, SCAN_SUB, n)
    x3r, x3i = v3(xr), v3(xi)
    br = (x3r * tab[0] - x3i * tab[1]).reshape(t, n)
    bi = (x3r * tab[1] + x3i * tab[0]).reshape(t, n)
    pm = jnp.dot(tri, _split_hi_lo(jnp.concatenate([br, bi], axis=1)), preferred_element_type=F32)
    p3r, p3i = v3(pm[:t, :n]), v3(pm[:t, n:])
    slr = p3r * tab[2] - p3i * tab[3]
    sli = p3r * tab[3] + p3i * tab[2]
    totr, toti = pm[t:, :n], pm[t:, n:]
    l3r, l3i = lam3
    l8r, l8i = lam8
    row = lax.broadcasted_iota(jnp.int32, (ns, 1), 0)
    edge = row == (ns - 1 if reverse else 0)
    er = totr * l3r - toti * l3i + jnp.where(edge, l8r * cr - l8i * ci, 0.0)
    ei = totr * l3i + toti * l3r + jnp.where(edge, l8r * ci + l8i * cr, 0.0)
    er, ei = _scan(er, ei, l8r, l8i, ns, reverse)
    shift = ns - 1 if reverse else 1
    nbr = jnp.where(edge, cr, pltpu.roll(er, shift, 0))
    nbi = jnp.where(edge, ci, pltpu.roll(ei, shift, 0))
    ex = jnp.dot(expand, _split_hi_lo(jnp.concatenate([nbr, nbi], axis=1)), preferred_element_type=F32)
    e3r, e3i = v3(ex[:, :n]), v3(ex[:, n:])
    sr = (slr + e3r * tab[4] - e3i * tab[5]).reshape(t, n)
    si = (sli + e3r * tab[5] + e3i * tab[4]).reshape(t, n)
    out = 0 if reverse else ns - 1
    return sr, si, er[out:out + 1, :], ei[out:out + 1, :]


def _scan_consts(t):
    import numpy as np
    ns = t // SCAN_SUB
    r = np.arange(t)
    same = (r[:, None] // SCAN_SUB) == (r[None, :] // SCAN_SUB)
    sums = (np.arange(ns)[:, None] == (r[None, :] // SCAN_SUB))
    tri = []
    for keep in (r[None, :] <= r[:, None], r[None, :] >= r[:, None]):
        m = np.concatenate([same & keep, sums], axis=0).astype(np.float32)
        tri.append(np.concatenate([m, m], axis=1))
    ex = ((r[:, None] // SCAN_SUB) == np.arange(ns)[None, :]).astype(np.float32)
    return jnp.asarray(np.stack(tri), BF16), jnp.asarray(np.concatenate([ex, ex], axis=1), BF16)


def _scan_tables(lr, li):
    den = lr * lr + li * li
    ir, ii = lr / den, -li / den
    mul = lambda a, b: (a[0] * b[0] - a[1] * b[1], a[0] * b[1] + a[1] * b[0])
    pw = {0: (jnp.ones_like(lr), jnp.zeros_like(lr))}
    for e in range(1, 9):
        pw[e] = mul(pw[e - 1], (lr, li))
    for e in range(-1, -5, -1):
        pw[e] = mul(pw[e + 1], (ir, ii))
    stack = lambda es, sign: (jnp.stack([pw[e][0] for e in es]), sign * jnp.stack([pw[e][1] for e in es]))
    j = range(SCAN_SUB)
    parts = [stack([4 - k for k in j], 1.0), stack([k - 4 for k in j], 1.0), stack([k + 1 for k in j], 1.0),
             stack([k - 3 for k in j], -1.0), stack([3 - k for k in j], -1.0), stack([8 - k for k in j], -1.0)]
    tabs = jnp.stack([a for pair in parts for a in pair])
    lam = jnp.zeros((8, lr.shape[0]), F32)
    for k, v in enumerate((lr, li, pw[3][0], pw[3][1], pw[8][0], pw[8][1])):
        lam = lam.at[k].set(v)
    return lam, tabs


SSM_HALVES = 2
SSM_HW = SSM_W // SSM_HALVES
SSM_HN = SSM_N // SSM_HALVES


def _bd_nn(x, w):
    a = w.shape[1]
    return jnp.concatenate([_dot(x[:, h * a:(h + 1) * a], w[h]) for h in range(SSM_HALVES)], axis=1)


def _bd_nt(x, w):
    b = w.shape[2]
    return jnp.concatenate([_dot(x[:, h * b:(h + 1) * b], w[h], NT) for h in range(SSM_HALVES)], axis=1)


def _bd_tn(x, y):
    a, b = x.shape[1] // SSM_HALVES, y.shape[1] // SSM_HALVES
    return jnp.stack([_dot(x[:, h * a:(h + 1) * a], y[:, h * b:(h + 1) * b], TN) for h in range(SSM_HALVES)])


def _ssm_states(u, s0r, s0i, lam_ref, tab_ref, tri_ref, ex_ref, bre, bim, t):
    tab = tuple(tab_ref[k] for k in range(6))
    return _scan_mxu(_bd_nn(u, bre), _bd_nn(u, bim), tab, (lam_ref[2:3, :], lam_ref[3:4, :]),
                     (lam_ref[4:5, :], lam_ref[5:6, :]), tri_ref[0], ex_ref[...], s0r, s0i, t, False)


def _ssm_head(u, z, xr, xi, cre, cim, dskip, wglu, bglu):
    y = _bd_nn(xr, cre) - _bd_nn(xi, cim) + dskip * u
    y2, dgelu = _gelu_and_grad(y)
    gate = _sigmoid(_dot(y2, wglu) + bglu)
    y3 = y2 * gate
    return y2, dgelu, gate, y3


def _with_comm(body, comm, n_in, n_out, nb, mid_step):
    if comm is None:
        return body
    nc = len(comm["args"])
    n_sem = len(comm["scratch"])

    def hosted(*refs):
        ins, cin = refs[:n_in], refs[n_in:n_in + nc]
        outs, cout = refs[n_in + nc:n_in + nc + n_out], refs[n_in + nc + n_out:n_in + 2 * nc + n_out]
        rest = refs[n_in + 2 * nc + n_out:]
        scratch, csem = rest[:len(rest) - n_sem], rest[len(rest) - n_sem:]
        start, forward, finish = comm["phases"](cin, cout, *csem)
        i = pl.program_id(0)
        pl.when(i == 0)(start)
        pl.when(i == mid_step)(forward)
        body(*ins, *outs, *scratch)
        pl.when(i == nb - 1)(finish)

    return hosted


def _comm_extra(comm):
    if comm is None:
        return [], [], [], [], []
    anyspec = pl.BlockSpec(memory_space=pl.ANY)
    nc = len(comm["args"])
    return comm["args"], [anyspec] * nc, [anyspec] * nc, comm["out_shape"], comm["scratch"]


def _ssm_fwd(ps, scan_ops, bblk, cblk, dskip, wglu, bglu, *, name, t=SEQ_BLOCK, comm=None):
    l = ps.shape[0]
    assert l % t == 0
    nb = l // t
    ns = t // SCAN_SUB
    c_args, c_in, c_out, c_shape, c_scratch = _comm_extra(comm)

    def body(ps_ref, lam_ref, tab_ref, tri_ref, ex_ref, b_ref, c_ref, d_ref, w_ref, bg_ref, ys_ref, chk_ref, st_ref):
        @pl.when(pl.program_id(0) == 0)
        def _():
            st_ref[...] = jnp.zeros_like(st_ref)

        chk_ref[...] = jnp.broadcast_to(st_ref[...], chk_ref.shape)
        u = ps_ref[:, :SSM_W]
        z = ps_ref[:, SSM_W:]
        xr, xi, er, ei = _ssm_states(u, st_ref[:, :SSM_N], st_ref[:, SSM_N:], lam_ref, tab_ref, tri_ref, ex_ref,
                                     b_ref[0], b_ref[1], t)
        st_ref[:, :SSM_N] = er
        st_ref[:, SSM_N:] = ei
        _, _, _, y3 = _ssm_head(u, z, xr, xi, c_ref[0], c_ref[1], d_ref[...], w_ref[...], bg_ref[...])
        sz, _ = _silu_and_grad(z)
        ys_ref[...] = (y3 * sz).astype(BF16)

    full = lambda shape: pl.BlockSpec(shape, lambda i: (0,) * len(shape))
    return pl.pallas_call(
        _with_comm(body, comm, 10, 2, nb, nb // 2), grid=(nb,),
        in_specs=[pl.BlockSpec((t, 2 * SSM_W), lambda i: (i, 0)), full((8, SSM_N)), full((12, SCAN_SUB, SSM_N)),
                  full((2, t + ns, 2 * t)), full((t, 2 * ns)), full((2, SSM_HALVES, SSM_HW, SSM_HN)),
                  full((2, SSM_HALVES, SSM_HN, SSM_HW)), full((1, SSM_W)), full((SSM_W, SSM_W)), full((1, SSM_W))] + c_in,
        out_specs=[pl.BlockSpec((t, SSM_W), lambda i: (i, 0)), pl.BlockSpec((8, 2 * SSM_N), lambda i: (i, 0))] + c_out,
        out_shape=[jax.ShapeDtypeStruct((l, SSM_W), BF16), jax.ShapeDtypeStruct((nb * 8, 2 * SSM_N), F32)] + c_shape,
        scratch_shapes=[pltpu.VMEM((1, 2 * SSM_N), F32)] + c_scratch,
        compiler_params=_params("arbitrary"), name=name)(ps, *scan_ops, bblk, cblk, dskip, wglu, bglu, *c_args)


def _ssm_bwd(ps, dys, chk, scan_ops, bblk, cblk, dskip, wglu, bglu, *, name, t=SEQ_BLOCK, comm=None):
    l = ps.shape[0]
    assert l % t == 0
    nb = l // t
    ns = t // SCAN_SUB
    c_args, c_in, c_out, c_shape, c_scratch = _comm_extra(comm)

    def body(ps_ref, dys_ref, chk_ref, lam_ref, tab_ref, tri_ref, ex_ref, b_ref, c_ref, d_ref, w_ref, bg_ref,
             dps_ref, db_ref, dc_ref, dw_acc, sums_acc, gc_ref, db_acc, dc_acc):
        n = pl.program_id(0)

        @pl.when(n == 0)
        def _():
            gc_ref[...] = jnp.zeros_like(gc_ref)
            db_acc[...] = jnp.zeros_like(db_acc)
            dc_acc[...] = jnp.zeros_like(dc_acc)
            dw_acc[...] = jnp.zeros_like(dw_acc)
            sums_acc[...] = jnp.zeros_like(sums_acc)

        row = lax.broadcasted_iota(jnp.int32, (t, 1), 0)
        u = ps_ref[:, :SSM_W]
        z = ps_ref[:, SSM_W:]
        s0r, s0i = chk_ref[0:1, :SSM_N], chk_ref[0:1, SSM_N:]
        xr, xi, _, _ = _ssm_states(u, s0r, s0i, lam_ref, tab_ref, tri_ref, ex_ref, b_ref[0], b_ref[1], t)
        dskip = d_ref[...]
        y2, dgelu, gate, y3 = _ssm_head(u, z, xr, xi, c_ref[0], c_ref[1], dskip, w_ref[...], bg_ref[...])
        sz, dsz = _silu_and_grad(z)
        dys_v = dys_ref[...]
        dps_ref[:, SSM_W:] = (dys_v * y3 * dsz).astype(BF16)
        dy3 = dys_v * sz
        da = dy3 * y2 * gate * (1.0 - gate)
        dy2 = dy3 * gate + _dot(da, w_ref[...], NT)
        dw_acc[...] += _dot(y2, da, TN)
        dy = dy2 * dgelu
        sums_acc[2:3, :SSM_W] += jnp.sum(dy * u, axis=0, keepdims=True)
        sums_acc[3:4, :SSM_W] += jnp.sum(da, axis=0, keepdims=True)
        dc_acc[0] += _bd_tn(xr, dy)
        dc_acc[1] += -_bd_tn(xi, dy)
        rev_tab = tuple(tab_ref[k] for k in range(6, 12))
        gr, gi, gcr, gci = _scan_mxu(
            _bd_nt(dy, c_ref[0]), -_bd_nt(dy, c_ref[1]), rev_tab, (lam_ref[2:3, :], -lam_ref[3:4, :]),
            (lam_ref[4:5, :], -lam_ref[5:6, :]), tri_ref[1], ex_ref[...], gc_ref[:, :SSM_N], gc_ref[:, SSM_N:], t, True)
        gc_ref[:, :SSM_N] = gcr
        gc_ref[:, SSM_N:] = gci
        db_acc[0] += _bd_tn(u, gr)
        db_acc[1] += _bd_tn(u, gi)
        du = dskip * dy + _bd_nt(gr, b_ref[0]) + _bd_nt(gi, b_ref[1])
        dps_ref[:, :SSM_W] = du.astype(BF16)
        spr = jnp.where(row == 0, s0r, pltpu.roll(xr, 1, 0))
        spi = jnp.where(row == 0, s0i, pltpu.roll(xi, 1, 0))
        sums_acc[0:1, :] += jnp.sum(gr * spr + gi * spi, axis=0, keepdims=True)
        sums_acc[1:2, :] += jnp.sum(gi * spr - gr * spi, axis=0, keepdims=True)

        @pl.when(n == nb - 1)
        def _():
            per_half = SSM_GROUPS // SSM_HALVES
            for k in range(2):
                for g in range(SSM_GROUPS):
                    h, gl = divmod(g, per_half)
                    c0, p0 = gl * SSM_GROUP, gl * SSM_STATE
                    db_ref[k, g * SSM_GROUP:(g + 1) * SSM_GROUP, :] = db_acc[k, h, c0:c0 + SSM_GROUP, p0:p0 + SSM_STATE]
                    dc_ref[k, g * SSM_STATE:(g + 1) * SSM_STATE, :] = dc_acc[k, h, p0:p0 + SSM_STATE, c0:c0 + SSM_GROUP]

    full = lambda shape: pl.BlockSpec(shape, lambda n: (0,) * len(shape))
    return pl.pallas_call(
        _with_comm(body, comm, 12, 5, nb, 0), grid=(nb,),
        in_specs=[pl.BlockSpec((t, 2 * SSM_W), lambda n: (nb - 1 - n, 0)),
                  pl.BlockSpec((t, SSM_W), lambda n: (nb - 1 - n, 0)),
                  pl.BlockSpec((8, 2 * SSM_N), lambda n: (nb - 1 - n, 0)),
                  full((8, SSM_N)), full((12, SCAN_SUB, SSM_N)), full((2, t + ns, 2 * t)), full((t, 2 * ns)),
                  full((2, SSM_HALVES, SSM_HW, SSM_HN)), full((2, SSM_HALVES, SSM_HN, SSM_HW)), full((1, SSM_W)),
                  full((SSM_W, SSM_W)), full((1, SSM_W))] + c_in,
        out_specs=[pl.BlockSpec((t, 2 * SSM_W), lambda n: (nb - 1 - n, 0)), full((2, SSM_W, SSM_STATE)),
                   full((2, SSM_N, SSM_GROUP)), full((SSM_W, SSM_W)), full((8, SSM_N))] + c_out,
        out_shape=[jax.ShapeDtypeStruct((l, 2 * SSM_W), BF16),
                   jax.ShapeDtypeStruct((2, SSM_W, SSM_STATE), F32),
                   jax.ShapeDtypeStruct((2, SSM_N, SSM_GROUP), F32),
                   jax.ShapeDtypeStruct((SSM_W, SSM_W), F32),
                   jax.ShapeDtypeStruct((8, SSM_N), F32)] + c_shape,
        scratch_shapes=[pltpu.VMEM((1, 2 * SSM_N), F32), pltpu.VMEM((2, SSM_HALVES, SSM_HW, SSM_HN), F32),
                        pltpu.VMEM((2, SSM_HALVES, SSM_HN, SSM_HW), F32)] + c_scratch,
        compiler_params=_params("arbitrary"), name=name)(ps, dys, chk, *scan_ops, bblk, cblk, dskip, wglu, bglu, *c_args)


def _pool_count(i, t):
    pos = lax.broadcasted_iota(jnp.int32, (t, POOL_W), 0) + i * t + 1
    col = lax.broadcasted_iota(jnp.int32, (t, POOL_W), 1)
    win = jnp.where(col < POOL_GW, 2, jnp.where(col < 2 * POOL_GW, 4, jnp.where(col < 3 * POOL_GW, 8, 16)))
    return 1.0 / jnp.minimum(pos, win).astype(F32), col


def _window_sums(ext, n_rows, forward):
    col = lax.broadcasted_iota(jnp.int32, ext.shape, 1)
    sh = (lambda a, d: pltpu.roll(a, d, 0)) if forward else (lambda a, d: pltpu.roll(a, n_rows - d, 0))
    a2 = ext + sh(ext, 1)
    a4 = a2 + sh(a2, 2)
    a8 = a4 + sh(a4, 4)
    a16 = a8 + sh(a8, 8)
    return jnp.where(col < POOL_GW, a2, jnp.where(col < 2 * POOL_GW, a4, jnp.where(col < 3 * POOL_GW, a8, a16)))


def _pool_mix(pooled, wp_ref):
    return jnp.concatenate([_dot(pooled[:, g * POOL_GW:(g + 1) * POOL_GW], wp_ref[g]) for g in range(4)], axis=1)


def _pool_pooled(i, cur_u, prev_u, t):
    prev = jnp.where(i > 0, prev_u, 0.0)
    ext = jnp.concatenate([prev, cur_u], axis=0)
    inv_cnt, _ = _pool_count(i, t)
    return _window_sums(ext, t + POOL_HALO, True)[POOL_HALO:, :] * inv_cnt - cur_u


def _pool_fwd(pp, wpool, pscale, *, name, t=SEQ_BLOCK):
    l = pp.shape[0]
    t = min(t, l)

    def body(cur_ref, prev_ref, wp_ref, sc_ref, yp_ref):
        i = pl.program_id(0)
        pooled = _pool_pooled(i, cur_ref[:, :POOL_W], prev_ref[...], t)
        lin = _pool_mix(pooled, wp_ref)
        sz, _ = _silu_and_grad(cur_ref[:, POOL_W:])
        yp_ref[...] = (lin * sc_ref[...] * sz).astype(BF16)

    hb = t // POOL_HALO
    return pl.pallas_call(
        body, grid=(l // t,),
        in_specs=[pl.BlockSpec((t, 2 * POOL_W), lambda i: (i, 0)),
                  pl.BlockSpec((POOL_HALO, POOL_W), lambda i: (jnp.maximum(i * hb - 1, 0), 0)),
                  pl.BlockSpec((4, POOL_GW, POOL_GW), lambda i: (0, 0, 0)),
                  pl.BlockSpec((1, POOL_W), lambda i: (0, 0))],
        out_specs=pl.BlockSpec((t, POOL_W), lambda i: (i, 0)),
        out_shape=jax.ShapeDtypeStruct((l, POOL_W), BF16),
        compiler_params=_params("parallel"), name=name)(pp, pp, wpool, pscale)


def _pool_bwd(pp, dyp, wpool, pscale, *, name, t=SEQ_BLOCK):
    l = pp.shape[0]
    t = min(t, l)
    nb = l // t

    def body(cur_ref, prev_ref, dyp_ref, wp_ref, sc_ref, dpp_ref, dwp_ref, sums_ref, carry_ref):
        n = pl.program_id(0)
        i = nb - 1 - n

        @pl.when(n == 0)
        def _():
            carry_ref[...] = jnp.zeros_like(carry_ref)
            dwp_ref[...] = jnp.zeros_like(dwp_ref)
            sums_ref[...] = jnp.zeros_like(sums_ref)

        cur_u = cur_ref[:, :POOL_W]
        pooled = _pool_pooled(i, cur_u, prev_ref[...], t)
        lin = _pool_mix(pooled, wp_ref)
        sz, dsz = _silu_and_grad(cur_ref[:, POOL_W:])
        dyp_v = dyp_ref[...]
        scale = sc_ref[...]
        dpp_ref[:, POOL_W:] = (dyp_v * lin * scale * dsz).astype(BF16)
        dpre = dyp_v * sz
        sums_ref[0:1, :] += jnp.sum(dpre * lin, axis=0, keepdims=True)
        dlin = dpre * scale
        dpooled = []
        for g in range(4):
            dl = dlin[:, g * POOL_GW:(g + 1) * POOL_GW]
            dwp_ref[g] += _dot(pooled[:, g * POOL_GW:(g + 1) * POOL_GW], dl, TN)
            dpooled.append(_dot(dl, wp_ref[g], NT))
        dpooled = jnp.concatenate(dpooled, axis=1)
        inv_cnt, _ = _pool_count(i, t)
        dq = dpooled * inv_cnt
        ext = jnp.concatenate([dq, carry_ref[...]], axis=0)
        du = _window_sums(ext, t + POOL_HALO, False)[:t, :] - dpooled
        dpp_ref[:, :POOL_W] = du.astype(BF16)
        carry_ref[...] = dq[:POOL_HALO, :]

    hb = t // POOL_HALO
    return pl.pallas_call(
        body, grid=(nb,),
        in_specs=[pl.BlockSpec((t, 2 * POOL_W), lambda n: (nb - 1 - n, 0)),
                  pl.BlockSpec((POOL_HALO, POOL_W), lambda n: (jnp.maximum((nb - 1 - n) * hb - 1, 0), 0)),
                  pl.BlockSpec((t, POOL_W), lambda n: (nb - 1 - n, 0)),
                  pl.BlockSpec((4, POOL_GW, POOL_GW), lambda n: (0, 0, 0)),
                  pl.BlockSpec((1, POOL_W), lambda n: (0, 0))],
        out_specs=[pl.BlockSpec((t, 2 * POOL_W), lambda n: (nb - 1 - n, 0)),
                   pl.BlockSpec((4, POOL_GW, POOL_GW), lambda n: (0, 0, 0)),
                   pl.BlockSpec((8, POOL_W), lambda n: (0, 0))],
        out_shape=[jax.ShapeDtypeStruct((l, 2 * POOL_W), BF16), jax.ShapeDtypeStruct((4, POOL_GW, POOL_GW), F32),
                   jax.ShapeDtypeStruct((8, POOL_W), F32)],
        scratch_shapes=[pltpu.VMEM((POOL_HALO, POOL_W), F32)],
        compiler_params=_params("arbitrary"), name=name)(pp, pp, dyp, wpool, pscale)


def _merge_fwd(ya, ys, yp, wa, ws, wp, pg, *, name, tm=256):
    l = ya.shape[0]
    tm = min(tm, l)
    d = D_MODEL

    def body(ya_ref, ys_ref, yp_ref, wa_ref, ws_ref, wp_ref, pg_ref, mg_ref, ba_ref, bs_ref, bp_ref):
        acc = None
        for k, (y_ref, w_ref, b_ref) in enumerate(((ya_ref, wa_ref, ba_ref), (ys_ref, ws_ref, bs_ref),
                                                   (yp_ref, wp_ref, bp_ref))):
            br = _dot(y_ref[...], w_ref[...])
            b_ref[...] = br
            term = _sigmoid(pg_ref[:, k * d:(k + 1) * d]) * br
            acc = term if acc is None else acc + term
        mg_ref[...] = acc.astype(BF16)

    rowy = pl.BlockSpec((tm, ATT_W), lambda i: (i, 0))
    wsp = pl.BlockSpec((ATT_W, d), lambda i: (0, 0))
    rowd = pl.BlockSpec((tm, d), lambda i: (i, 0))
    return pl.pallas_call(
        body, grid=(l // tm,),
        in_specs=[rowy, rowy, rowy, wsp, wsp, wsp, pl.BlockSpec((tm, 3 * d), lambda i: (i, 0))],
        out_specs=[rowd, rowd, rowd, rowd],
        out_shape=[jax.ShapeDtypeStruct((l, d), BF16)] + [jax.ShapeDtypeStruct((l, d), F32)] * 3,
        compiler_params=_params("parallel"), name=name)(ya, ys, yp, wa, ws, wp, pg)


def _out_fwd(merged, wout, x, gate, *, name, tm=512):
    l, d = x.shape
    tm = min(tm, l)

    def body(m_ref, w_ref, x_ref, g_ref, xn_ref, out_ref):
        out = _dot(m_ref[...], w_ref[...])
        out_ref[...] = out
        xn_ref[...] = x_ref[...] + g_ref[...] * out

    row = pl.BlockSpec((tm, d), lambda i: (i, 0))
    return pl.pallas_call(
        body, grid=(l // tm,),
        in_specs=[row, pl.BlockSpec((d, d), lambda i: (0, 0)), row, pl.BlockSpec((1, d), lambda i: (0, 0))],
        out_specs=[row, row],
        out_shape=[jax.ShapeDtypeStruct((l, d), F32)] * 2,
        compiler_params=_params("parallel"), name=name)(merged, wout, x, gate)


def _merge_bwd(dx, out, gate, wout, pg, ba, bs, bp, *, name, tm=256):
    l, d = dx.shape
    tm = min(tm, l)

    def body(dx_ref, out_ref, g_ref, w_ref, pg_ref, ba_ref, bs_ref, bp_ref,
             dmo_ref, dba_ref, dbs_ref, dbp_ref, dpg_ref, sums_ref):
        @pl.when(pl.program_id(0) == 0)
        def _():
            sums_ref[...] = jnp.zeros_like(sums_ref)

        dxv = dx_ref[...]
        sums_ref[0:1, :] += jnp.sum(dxv * out_ref[...], axis=0, keepdims=True)
        dmo = (dxv * g_ref[...]).astype(BF16)
        dmo_ref[...] = dmo
        dmerged = _dot(dmo, w_ref[...], NT)
        for k, (b_ref, db_ref) in enumerate(((ba_ref, dba_ref), (bs_ref, dbs_ref), (bp_ref, dbp_ref))):
            gk = _sigmoid(pg_ref[:, k * d:(k + 1) * d])
            db_ref[...] = (dmerged * gk).astype(BF16)
            dpg_ref[:, k * d:(k + 1) * d] = (dmerged * b_ref[...] * gk * (1.0 - gk)).astype(BF16)

    row = pl.BlockSpec((tm, d), lambda i: (i, 0))
    wide = pl.BlockSpec((tm, 3 * d), lambda i: (i, 0))
    return pl.pallas_call(
        body, grid=(l // tm,),
        in_specs=[row, row, pl.BlockSpec((1, d), lambda i: (0, 0)), pl.BlockSpec((d, d), lambda i: (0, 0)),
                  wide, row, row, row],
        out_specs=[row, row, row, row, wide, pl.BlockSpec((8, d), lambda i: (0, 0))],
        out_shape=[jax.ShapeDtypeStruct((l, d), BF16)] * 4 + [jax.ShapeDtypeStruct((l, 3 * d), BF16),
                                                             jax.ShapeDtypeStruct((8, d), F32)],
        compiler_params=_params("arbitrary"), name=name)(dx, out, gate, wout, pg, ba, bs, bp)


def _adamw(w, g, m, v, *, name, tr=256):
    r, c = w.shape
    p = g.shape[0]
    tr = min(tr, r)
    assert r % tr == 0
    c1 = 1.0 / (1.0 - ADAM_B1 ** ADAM_STEP)
    c2 = 1.0 / (1.0 - ADAM_B2 ** ADAM_STEP)

    def body(w_ref, g_ref, m_ref, v_ref, go_ref, d_ref, mo_ref, vo_ref):
        gv = g_ref[0].astype(F32)
        for k in range(1, p):
            gv = gv + g_ref[k].astype(F32)
        go_ref[...] = gv
        mn = ADAM_B1 * m_ref[...] + (1.0 - ADAM_B1) * gv
        vn = ADAM_B2 * v_ref[...] + (1.0 - ADAM_B2) * (gv * gv)
        mo_ref[...] = mn
        vo_ref[...] = vn
        d_ref[...] = -ADAM_LR * ((mn * c1) / (jnp.sqrt(vn * c2) + ADAM_EPS) + ADAM_WD * w_ref[...])

    row = pl.BlockSpec((tr, c), lambda i: (i, 0))
    return pl.pallas_call(
        body, grid=(r // tr,),
        in_specs=[row, pl.BlockSpec((p, tr, c), lambda i: (0, i, 0)), row, row],
        out_specs=[row] * 4,
        out_shape=[jax.ShapeDtypeStruct((r, c), F32)] * 4,
        compiler_params=_params("parallel"), name=name)(w, g, m, v)


def _exchange(arrs, *, scatter, name):
    n = len(arrs)
    out_shape = [jax.ShapeDtypeStruct(a.shape if scatter else (N_DEV,) + a.shape, a.dtype) for a in arrs]

    def body(*refs):
        ins, outs = refs[:n], refs[n:2 * n]
        send_sems, recv_sems, loc_sems = refs[2 * n:]
        me = 4 * lax.axis_index("x") + 2 * lax.axis_index("y") + lax.axis_index("c")
        local = []
        for k in range(n):
            src = ins[k].at[me] if scatter else ins[k]
            cp = pltpu.make_async_copy(src, outs[k].at[me], loc_sems.at[k])
            cp.start()
            local.append(cp)
        remote = []
        for r in range(1, N_DEV):
            peer = me ^ r
            for k in range(n):
                src = ins[k].at[peer] if scatter else ins[k]
                cp = pltpu.make_async_remote_copy(
                    src_ref=src, dst_ref=outs[k].at[me], send_sem=send_sems.at[k, r - 1], recv_sem=recv_sems.at[k, r - 1],
                    device_id=(peer // 4, (peer // 2) % 2, peer % 2), device_id_type=pl.DeviceIdType.MESH)
                cp.start()
                remote.append(cp)
        for cp in remote:
            cp.wait()
        for cp in local:
            cp.wait()

    anyspec = pl.BlockSpec(memory_space=pl.ANY)
    return pl.pallas_call(
        body, in_specs=[anyspec] * n, out_specs=[anyspec] * n, out_shape=out_shape,
        scratch_shapes=[pltpu.SemaphoreType.DMA((n, N_DEV - 1)), pltpu.SemaphoreType.DMA((n, N_DEV - 1)),
                        pltpu.SemaphoreType.DMA((n,))],
        name=name)(*arrs)


def _mesh_place():
    x, y, c = lax.axis_index("x"), lax.axis_index("y"), lax.axis_index("c")
    other_chips = [(1 - x, y), (x, 1 - y), (1 - x, 1 - y)]
    return x, y, c, other_chips


def _gather_two_level(arrs, *, name):
    n = len(arrs)
    plan = _gather_plan(arrs)

    def body(*refs):
        start, forward, finish = plan["phases"](refs[:n], refs[n:2 * n], *refs[2 * n:])
        start()
        forward()
        finish()

    anyspec = pl.BlockSpec(memory_space=pl.ANY)
    return pl.pallas_call(
        body, in_specs=[anyspec] * n, out_specs=[anyspec] * n, out_shape=plan["out_shape"],
        scratch_shapes=plan["scratch"], name=name)(*arrs)


def _gather_plan(arrs):
    n = len(arrs)

    def phases(ins, outs, send_sems, recv_sems, loc_sems):
        x, y, c, chips = _mesh_place()
        me = 4 * x + 2 * y + c
        slot = lambda px, py, pc: 4 * px + 2 * py + pc

        def copy(k, j, src, block, to):
            return pltpu.make_async_remote_copy(
                src_ref=src, dst_ref=outs[k].at[block], send_sem=send_sems.at[k, j], recv_sem=recv_sems.at[k, j],
                device_id=to, device_id_type=pl.DeviceIdType.MESH)

        local = [pltpu.make_async_copy(ins[k], outs[k].at[me], loc_sems.at[k]) for k in range(n)]
        first = []
        for k in range(n):
            first.append(copy(k, 0, ins[k], me, (x, y, 1 - c)))
            for j, chip in enumerate(chips):
                first.append(copy(k, 1 + j, ins[k], me, (*chip, c)))
        passed = [copy(k, 4 + j, outs[k].at[slot(*chip, c)], slot(*chip, c), (x, y, 1 - c))
                  for j, chip in enumerate(chips) for k in range(n)]

        def start():
            for cp in local + first:
                cp.start()

        def forward():
            for j, chip in enumerate(chips):
                for k in range(n):
                    copy(k, 1 + j, ins[k], slot(*chip, c), (x, y, c)).wait_recv()
                    passed[j * n + k].start()

        def finish():
            for k in range(n):
                copy(k, 0, ins[k], slot(x, y, 1 - c), (x, y, c)).wait_recv()
                for j, chip in enumerate(chips):
                    copy(k, 4 + j, ins[k], slot(*chip, 1 - c), (x, y, c)).wait_recv()
            for cp in first + passed:
                cp.wait_send()
            for cp in local:
                cp.wait()

        return start, forward, finish

    return dict(
        args=list(arrs), out_shape=[jax.ShapeDtypeStruct((N_DEV,) + a.shape, a.dtype) for a in arrs],
        scratch=[pltpu.SemaphoreType.DMA((n, 7)), pltpu.SemaphoreType.DMA((n, 7)), pltpu.SemaphoreType.DMA((n,))],
        phases=phases)


def _allreduce_small(small, extra, *, name):
    r, lanes = small.shape
    assert r % 16 == 0
    h = r // 2
    e = extra.shape[0]

    def body(s_ref, x_ref, out_ref, xall_ref, sib_ref, parts_ref, send_sems, recv_sems):
        x, y, c, chips = _mesh_place()
        me = 4 * x + 2 * y + c
        my_chip = 2 * x + y
        sibling = (x, y, 1 - c)
        mine = pl.ds(pl.multiple_of(c * h, 8), h)
        theirs = pl.ds(pl.multiple_of((1 - c) * h, 8), h)

        def remote(j, src, dst, to):
            return pltpu.make_async_remote_copy(src_ref=src, dst_ref=dst, send_sem=send_sems.at[j],
                                                recv_sem=recv_sems.at[j], device_id=to, device_id_type=pl.DeviceIdType.MESH)

        to_sibling = remote(0, s_ref.at[theirs], sib_ref, sibling)
        to_sibling.start()
        xall_ref[me] = x_ref[...]
        extras = []
        for rr in range(1, N_DEV):
            peer = me ^ rr
            cp = remote(4 + rr, x_ref, xall_ref.at[me], (peer // 4, (peer // 2) % 2, peer % 2))
            cp.start()
            extras.append(cp)
        to_sibling.wait_recv()
        parts_ref[my_chip] = s_ref[mine] + sib_ref[...]
        to_chips = [remote(1 + j, parts_ref.at[my_chip], parts_ref.at[my_chip], (px, py, c))
                    for j, (px, py) in enumerate(chips)]
        for cp in to_chips:
            cp.start()
        for cp in to_chips:
            cp.wait_recv()
        out_ref[mine] = (parts_ref[0] + parts_ref[1]) + (parts_ref[2] + parts_ref[3])
        done = remote(4, out_ref.at[mine], out_ref.at[mine], sibling)
        done.start()
        remote(4, out_ref.at[theirs], out_ref.at[theirs], sibling).wait_recv()
        for cp in extras:
            cp.wait()
        to_sibling.wait_send()
        for cp in to_chips:
            cp.wait_send()
        done.wait_send()

    vmem = pl.BlockSpec(memory_space=pltpu.VMEM)
    return pl.pallas_call(
        body, in_specs=[vmem, vmem], out_specs=[vmem, vmem],
        out_shape=[jax.ShapeDtypeStruct((r, lanes), F32), jax.ShapeDtypeStruct((N_DEV, e, lanes), F32)],
        scratch_shapes=[pltpu.VMEM((h, lanes), F32), pltpu.VMEM((4, h, lanes), F32),
                        pltpu.SemaphoreType.DMA((12,)), pltpu.SemaphoreType.DMA((12,))],
        compiler_params=pltpu.CompilerParams(vmem_limit_bytes=VMEM_LIMIT), name=name)(small, extra)


def _sibling_swap(arrs, *, name):
    n = len(arrs)
    out_shape = [jax.ShapeDtypeStruct(a.shape[1:], a.dtype) for a in arrs]

    def body(*refs):
        ins, outs = refs[:n], refs[n:2 * n]
        send_sems, recv_sems = refs[2 * n:]
        x, y, c, _ = _mesh_place()
        copies = [pltpu.make_async_remote_copy(
            src_ref=ins[k].at[1 - c], dst_ref=outs[k], send_sem=send_sems.at[k], recv_sem=recv_sems.at[k],
            device_id=(x, y, 1 - c), device_id_type=pl.DeviceIdType.MESH) for k in range(n)]
        for cp in copies:
            cp.start()
        for cp in copies:
            cp.wait()

    anyspec = pl.BlockSpec(memory_space=pl.ANY)
    return pl.pallas_call(
        body, in_specs=[anyspec] * n, out_specs=[anyspec] * n, out_shape=out_shape,
        scratch_shapes=[pltpu.SemaphoreType.DMA((n,)), pltpu.SemaphoreType.DMA((n,))], name=name)(*arrs)


def _pair_add(mine, theirs, core, *, name, tr=256):
    _, r, c = mine.shape
    tr = min(tr, r)
    assert r % tr == 0

    def body(core_ref, m_ref, t_ref, o_ref):
        o_ref[...] = (m_ref[0].astype(F32) + t_ref[...].astype(F32)).astype(BF16)

    return pl.pallas_call(
        body,
        grid_spec=pltpu.PrefetchScalarGridSpec(
            num_scalar_prefetch=1, grid=(r // tr,),
            in_specs=[pl.BlockSpec((1, tr, c), lambda i, core_ref: (core_ref[0], i, 0)),
                      pl.BlockSpec((tr, c), lambda i, core_ref: (i, 0))],
            out_specs=pl.BlockSpec((tr, c), lambda i, core_ref: (i, 0))),
        out_shape=jax.ShapeDtypeStruct((r, c), BF16),
        compiler_params=_params("parallel"), name=name)(core, mine, theirs)


def _chip_scatter(arrs, *, name):
    n = len(arrs)
    plan = _chip_scatter_plan(arrs)

    def body(*refs):
        start, _, finish = plan["phases"](refs[:n], refs[n:2 * n], *refs[2 * n:])
        start()
        finish()

    anyspec = pl.BlockSpec(memory_space=pl.ANY)
    return pl.pallas_call(
        body, in_specs=[anyspec] * n, out_specs=[anyspec] * n, out_shape=plan["out_shape"],
        scratch_shapes=plan["scratch"], name=name)(*arrs)


def _chip_scatter_plan(arrs):
    n = len(arrs)

    def phases(ins, outs, send_sems, recv_sems, loc_sems):
        x, y, c, chips = _mesh_place()
        mine = 2 * x + y
        local = [pltpu.make_async_copy(ins[k].at[mine], outs[k].at[mine], loc_sems.at[k]) for k in range(n)]
        remote = [pltpu.make_async_remote_copy(
            src_ref=ins[k].at[2 * px + py], dst_ref=outs[k].at[mine], send_sem=send_sems.at[k, j],
            recv_sem=recv_sems.at[k, j], device_id=(px, py, c), device_id_type=pl.DeviceIdType.MESH)
            for j, (px, py) in enumerate(chips) for k in range(n)]

        def start():
            for cp in local + remote:
                cp.start()

        def finish():
            for cp in remote:
                cp.wait()
            for cp in local:
                cp.wait()

        return start, (lambda: None), finish

    return dict(
        args=list(arrs), out_shape=[jax.ShapeDtypeStruct(a.shape, a.dtype) for a in arrs],
        scratch=[pltpu.SemaphoreType.DMA((n, 3)), pltpu.SemaphoreType.DMA((n, 3)), pltpu.SemaphoreType.DMA((n,))],
        phases=phases)


def _ssm_discretize(a_re, a_im, log_dt, b_re, b_im):
    dt = jnp.exp(log_dt)[:, None]
    mag = jnp.exp(a_re * dt)
    lr = mag * jnp.cos(a_im * dt)
    li = mag * jnp.sin(a_im * dt)
    den = a_re * a_re + a_im * a_im
    cr = ((lr - 1.0) * a_re + li * a_im) / den
    ci = (li * a_re - (lr - 1.0) * a_im) / den
    bbr = cr[..., None] * b_re - ci[..., None] * b_im
    bbi = cr[..., None] * b_im + ci[..., None] * b_re
    return lr, li, bbr, bbi


def _ssm_dense(lr, li, bbr, bbi, c_re, c_im):
    scan_ops = _scan_tables(lr.reshape(-1), li.reshape(-1)) + _scan_consts(SEQ_BLOCK)
    per_half = SSM_GROUPS // SSM_HALVES

    def halves(a, rows, cols):
        a = a.reshape(SSM_HALVES, per_half * rows, cols)
        tiled = jnp.tile(a, (1, 1, per_half))
        r = lax.broadcasted_iota(jnp.int32, tiled.shape, 1) // rows
        c = lax.broadcasted_iota(jnp.int32, tiled.shape, 2) // cols
        return jnp.where(r == c, tiled, 0.0)

    bblk = jnp.stack([halves(b.transpose(0, 2, 1), SSM_GROUP, SSM_STATE) for b in (bbr, bbi)]).astype(BF16)
    cblk = jnp.stack([halves(c.transpose(0, 2, 1), SSM_STATE, SSM_GROUP) for c in (c_re, c_im)]).astype(BF16)
    return scan_ops, bblk, cblk


def _ssm_extract(db, dc, sums):
    db = db.reshape(2, SSM_GROUPS, SSM_GROUP, SSM_STATE).transpose(0, 1, 3, 2)
    dc = dc.reshape(2, SSM_GROUPS, SSM_STATE, SSM_GROUP).transpose(0, 1, 3, 2)
    dlr = sums[0].reshape(SSM_GROUPS, SSM_STATE)
    dli = sums[1].reshape(SSM_GROUPS, SSM_STATE)
    return dlr, dli, db[0], db[1], dc[0], dc[1]


IN_SPLITS = (ATT_W, KV_W, KV_W, SSM_W, POOL_W, ATT_W, SSM_W, POOL_W, 3 * D_MODEL)


def _split_w_in(w):
    idx = [0]
    for s in IN_SPLITS:
        idx.append(idx[-1] + s)
    seg = [w[..., idx[k]:idx[k + 1]] for k in range(len(IN_SPLITS))]
    q, k, v, us, up, za, zs, zp, gl = seg
    return (jnp.concatenate([q, za, k, v], axis=-1), jnp.concatenate([us, zs], axis=-1),
            jnp.concatenate([up, zp], axis=-1), gl)


def _merge_w_in(da, ds, dp, dg):
    q, za, k, v = da[..., :ATT_W], da[..., ATT_W:2 * ATT_W], da[..., 2 * ATT_W:2 * ATT_W + KV_W], da[..., 2 * ATT_W + KV_W:]
    us, zs = ds[..., :SSM_W], ds[..., SSM_W:]
    up, zp = dp[..., :POOL_W], dp[..., POOL_W:]
    return jnp.concatenate([q, k, v, us, up, za, zs, zp, dg], axis=-1)


def _layer_fwd(x, lw, li, comm=None):
    tag = f"l{li}"
    h = _ln_fwd(x, lw["norm_g"], lw["shift"], lw["scale"], name=f"ln_fwd_{tag}")
    pa = _mm(h, lw["w_a"], tn=1280, name=f"proj_a_{tag}")
    ps = _mm(h, lw["w_s"], name=f"proj_s_{tag}")
    pp = _mm(h, lw["w_p"], name=f"proj_p_{tag}")
    pg = _mm(h, lw["w_g"], name=f"proj_g_{tag}")
    ya = _attn_fwd(pa, lw["sinks"], name=f"attn_fwd_{tag}")
    ys, chk, *exchanged = _ssm_fwd(ps, lw["lam"], lw["bblk"], lw["cblk"], lw["ssm_d"], lw["w_glu"], lw["b_glu"],
                                   name=f"ssm_fwd_{tag}", comm=comm)
    yp = _pool_fwd(pp, lw["w_pool"], lw["pool_scale"], name=f"pool_fwd_{tag}")
    merged, ba, bs, bp = _merge_fwd(ya, ys, yp, lw["w_br_att"], lw["w_br_ssm"], lw["w_br_pool"], pg, name=f"merge_fwd_{tag}")
    x_new, out = _out_fwd(merged, lw["w_out"], x, lw["gate"], name=f"out_fwd_{tag}")
    saved = dict(x=x, h=h, pa=pa, ps=ps, pp=pp, pg=pg, ya=ya, ys=ys, yp=yp, chk=chk, merged=merged,
                 ba=ba, bs=bs, bp=bp, out=out)
    return x_new, saved, exchanged


def _layer_bwd(dx, lw, sv, li, comm=None):
    tag = f"l{li}"
    dmo, dba, dbs, dbp, dpg, gate_sums = _merge_bwd(dx, sv["out"], lw["gate"], lw["w_out"], sv["pg"],
                                                    sv["ba"], sv["bs"], sv["bp"], name=f"merge_bwd_{tag}")
    g = {}
    g["w_out"] = _mm_tn(sv["merged"], dmo, out_dtype=BF16, name=f"dw_out_{tag}")
    dya = _mm(dba, lw["w_br_att"], nt=True, name=f"dy_att_{tag}")
    dys = _mm(dbs, lw["w_br_ssm"], nt=True, name=f"dy_ssm_{tag}")
    dyp = _mm(dbp, lw["w_br_pool"], nt=True, name=f"dy_pool_{tag}")
    g["w_br_att"] = _mm_tn(sv["ya"], dba, out_dtype=BF16, name=f"dw_br_att_{tag}")
    g["w_br_ssm"] = _mm_tn(sv["ys"], dbs, out_dtype=BF16, name=f"dw_br_ssm_{tag}")
    g["w_br_pool"] = _mm_tn(sv["yp"], dbp, out_dtype=BF16, name=f"dw_br_pool_{tag}")
    dpa, dsink = _attn_bwd(sv["pa"], lw["sinks"], dya, name=f"attn_bwd_{tag}")
    dps, db_dense, dc_dense, dwglu, ssm_sums, *exchanged = _ssm_bwd(
        sv["ps"], dys, sv["chk"], lw["lam"], lw["bblk"], lw["cblk"], lw["ssm_d"], lw["w_glu"], lw["b_glu"],
        name=f"ssm_bwd_{tag}", comm=comm)
    dpp, dwpool, pool_sums = _pool_bwd(sv["pp"], dyp, lw["w_pool"], lw["pool_scale"], name=f"pool_bwd_{tag}")
    dh = _mm_nt_sum([(dpa, lw["w_a"]), (dps, lw["w_s"]), (dpp, lw["w_p"]), (dpg, lw["w_g"])], name=f"dh_{tag}")
    h = sv["h"]
    g["w_in"] = _merge_w_in(_mm_tn(h, dpa, out_dtype=BF16, tn=1280, name=f"dw_a_{tag}"),
                            _mm_tn(h, dps, out_dtype=BF16, name=f"dw_s_{tag}"),
                            _mm_tn(h, dpp, out_dtype=BF16, name=f"dw_p_{tag}"),
                            _mm_tn(h, dpg, out_dtype=BF16, name=f"dw_g_{tag}"))
    dx_in, ln_sums = _ln_bwd(sv["x"], dh, dx, lw["norm_g"], lw["scale"], name=f"ln_bwd_{tag}")
    g["w_glu"] = dwglu.astype(BF16)
    g["dmod"] = jnp.concatenate([ln_sums[0], ln_sums[1], gate_sums[0]])
    g["norm_g"] = ln_sums[2]
    g["attn_sinks"] = dsink[:, 0]
    g["ssm_raw"] = _ssm_extract(db_dense, dc_dense, ssm_sums)
    g["ssm_d"] = ssm_sums[2, :SSM_W]
    g["b_glu"] = ssm_sums[3, :SSM_W]
    g["w_pool"] = dwpool
    g["pool_scale"] = pool_sums[0]
    return dx_in, g, exchanged


BIG_WEIGHTS = ("w_in", "w_glu", "w_br_att", "w_br_ssm", "w_br_pool", "w_out")
ROW_SHARDED = ("w_glu", "w_out")


def _full_weights(gathered):
    full = {}
    for k, g in zip(BIG_WEIGHTS, gathered):
        if k in ROW_SHARDED:
            full[k] = g.reshape(N_DEV * g.shape[1], g.shape[2])
        else:
            full[k] = g.transpose(1, 0, 2).reshape(g.shape[1], N_DEV * g.shape[2])
    return full


def _by_destination(grads):
    out = []
    for k in BIG_WEIGHTS:
        g = grads[k]
        if k in ROW_SHARDED:
            out.append(g.reshape(4, 2, g.shape[0] // N_DEV, g.shape[1]).transpose(1, 0, 2, 3))
        else:
            out.append(g.reshape(g.shape[0], 4, 2, g.shape[1] // N_DEV).transpose(2, 1, 0, 3))
    return out


def _prepare_layer(li, mod, norm_g, full, attn_sinks, disc, ssm_c_re, ssm_c_im, ssm_d, b_glu, w_pool, pool_scale):
    d = D_MODEL
    lr, li_, bbr, bbi = disc
    lam, bblk, cblk = _ssm_dense(lr[li], li_[li], bbr[li], bbi[li], ssm_c_re[li], ssm_c_im[li])
    w_a, w_s, w_p, w_g = _split_w_in(full["w_in"])
    return dict(
        norm_g=norm_g[li][None, :], shift=mod[li, :d][None, :], scale=mod[li, d:2 * d][None, :],
        gate=mod[li, 2 * d:][None, :], w_a=w_a, w_s=w_s, w_p=w_p, w_g=w_g,
        sinks=attn_sinks[li], lam=lam, bblk=bblk, cblk=cblk, ssm_d=ssm_d[li][None, :], w_glu=full["w_glu"],
        b_glu=b_glu[li][None, :], w_pool=w_pool[li].astype(BF16), pool_scale=pool_scale[li][None, :],
        w_br_att=full["w_br_att"], w_br_ssm=full["w_br_ssm"], w_br_pool=full["w_br_pool"], w_out=full["w_out"])


SMALL_ROWS = 64
SMALL_ORDER = ("norm_g", "attn_sinks", "ssm_d", "b_glu", "w_pool", "pool_scale", "dmod")


def _pack_small(loss, dfinal_g, layer_grads):
    parts = [jnp.broadcast_to(loss.reshape(1), (128,)), dfinal_g]
    for g in layer_grads:
        for k in SMALL_ORDER:
            v = g[k].reshape(-1)
            if v.shape[0] % 128:
                v = jnp.pad(v, (0, 128 - v.shape[0] % 128))
            parts.append(v)
        for v in g["ssm_raw"]:
            parts.append(v.reshape(-1))
    flat = jnp.concatenate(parts)
    return jnp.pad(flat, (0, (-flat.shape[0]) % (SMALL_ROWS * 128))).reshape(-1, 128)


def _unpack_small(flat, shapes):
    out, off = [], 0
    for s in shapes:
        n = int(math.prod(s))
        out.append(flat[off:off + n].reshape(s))
        off += n + (-n) % 128
    return out


def kernel(x, c, norm_g, w_ada, b_ada, w_in, attn_sinks, ssm_a_re, ssm_a_im, ssm_log_dt, ssm_b_re, ssm_b_im, ssm_c_re, ssm_c_im, ssm_d, w_glu, b_glu, w_pool, pool_scale, w_br_att, w_br_ssm, w_br_pool, w_out, final_g, loss_target, m_norm_g, m_w_ada, m_b_ada, m_w_in, m_attn_sinks, m_ssm_a_re, m_ssm_a_im, m_ssm_log_dt, m_ssm_b_re, m_ssm_b_im, m_ssm_c_re, m_ssm_c_im, m_ssm_d, m_w_glu, m_b_glu, m_w_pool, m_pool_scale, m_w_br_att, m_w_br_ssm, m_w_br_pool, m_w_out, m_final_g, v_norm_g, v_w_ada, v_b_ada, v_w_in, v_attn_sinks, v_ssm_a_re, v_ssm_a_im, v_ssm_log_dt, v_ssm_b_re, v_ssm_b_im, v_ssm_c_re, v_ssm_c_im, v_ssm_d, v_w_glu, v_b_glu, v_w_pool, v_pool_scale, v_w_br_att, v_w_br_ssm, v_w_br_pool, v_w_out, v_final_g):
    me = 4 * lax.axis_index("x") + 2 * lax.axis_index("y") + lax.axis_index("c")
    d = D_MODEL
    ada_w = 3 * d // N_DEV

    (c_all,) = _exchange([c.reshape(8, 128)], scatter=False, name="gather_c")
    c_act = jax.nn.silu(c_all.reshape(N_DEV, d))
    b_cols = lax.dynamic_slice(b_ada, (0, me * ada_w), (DEPTH, ada_w))
    mod_part = jnp.concatenate(
        [_mm(c_act, w_ada[li], name=f"ada_fwd_l{li}") + b_cols[li][None, :] for li in range(DEPTH)], axis=0)
    (mod_all,) = _exchange([mod_part], scatter=False, name="gather_mod")
    mod_all = mod_all.reshape(N_DEV, DEPTH, N_DEV, ada_w)
    mod_mine = lax.dynamic_index_in_dim(mod_all, me, axis=2, keepdims=False)
    mod_mine = mod_mine.transpose(1, 0, 2).reshape(DEPTH, 3 * d)

    sharded = dict(w_in=w_in, w_glu=w_glu, w_br_att=w_br_att, w_br_ssm=w_br_ssm, w_br_pool=w_br_pool, w_out=w_out)
    shards = lambda li: [sharded[k][li].astype(BF16) for k in BIG_WEIGHTS]
    disc, disc_vjp = jax.vjp(jax.vmap(_ssm_discretize), ssm_a_re, ssm_a_im, ssm_log_dt, ssm_b_re, ssm_b_im)
    layer = lambda li, gathered: _prepare_layer(li, mod_mine, norm_g, _full_weights(gathered), attn_sinks, disc,
                                                ssm_c_re, ssm_c_im, ssm_d, b_glu, w_pool, pool_scale)
    core = lax.axis_index("c").astype(jnp.int32).reshape(1)

    def chip_sums_of(grads_li, li):
        by_dest = _by_destination(grads_li)
        from_sibling = _sibling_swap(by_dest, name=f"grads_sibling_swap_l{li}")
        return [_pair_add(a.reshape(2, -1, a.shape[-1]), b.reshape(-1, b.shape[-1]), core,
                          name=f"grads_pair_add_l{li}_{k}").reshape(b.shape)
                for k, (a, b) in enumerate(zip(by_dest, from_sibling))]

    layers, saved, grads = [None] * DEPTH, [None] * DEPTH, [None] * DEPTH
    layers[0] = layer(0, _gather_two_level(shards(0), name="gather_weights_l0"))
    xs, saved[0], gathered1 = _layer_fwd(x[0], layers[0], 0, comm=_gather_plan(shards(1)))
    layers[1] = layer(1, gathered1)
    xs, saved[1], _ = _layer_fwd(xs, layers[1], 1)
    dx, fin_sums = _final_loss(xs, final_g[None, :], loss_target[0])
    loss_part = jnp.sum(fin_sums[1])
    dx, grads[1], _ = _layer_bwd(dx, layers[1], saved[1], 1)
    dx, grads[0], scattered1 = _layer_bwd(dx, layers[0], saved[0], 0, comm=_chip_scatter_plan(chip_sums_of(grads[1], 1)))
    scattered0 = _chip_scatter(chip_sums_of(grads[0], 0), name="grads_chip_scatter_l0")
    big = [jnp.stack([a, b], axis=1) for a, b in zip(scattered0, scattered1)]
    grad_x = dx[None]

    small = _pack_small(loss_part, fin_sums[0], grads)
    dmod_rows = jnp.concatenate([grads[li]["dmod"] for li in range(DEPTH)]).reshape(-1, 128)
    small_sum, dmod_gathered = _allreduce_small(small, dmod_rows, name="allreduce_small")
    out = {}

    def adam(name, w, g_parts, m, v):
        shp = w.shape
        r = int(math.prod(shp[:-1])) if len(shp) > 1 else 1
        w2, m2, v2 = (a.reshape(r, shp[-1]) for a in (w, m, v))
        g2 = g_parts.reshape(g_parts.shape[0], r, shp[-1])
        res = _adamw(w2, g2, m2, v2, name=f"adamw_{name}")
        out[name] = tuple(a.reshape(shp) for a in res)

    flat = small_sum.reshape(-1)
    shapes = [(128,), (d,)]
    for _ in range(DEPTH):
        shapes += [(d,), (N_HEADS,), (SSM_W,), (SSM_W,), (4, POOL_GW, POOL_GW), (POOL_W,), (3 * d,),
                   (SSM_GROUPS, SSM_STATE), (SSM_GROUPS, SSM_STATE), (SSM_GROUPS, SSM_STATE, SSM_GROUP),
                   (SSM_GROUPS, SSM_STATE, SSM_GROUP), (SSM_GROUPS, SSM_GROUP, SSM_STATE), (SSM_GROUPS, SSM_GROUP, SSM_STATE)]
    un = _unpack_small(flat, shapes)
    loss = un[0][0]
    g_final_g = un[1]
    per = 13
    gl = [un[2 + li * per: 2 + (li + 1) * per] for li in range(DEPTH)]
    st = lambda j: jnp.stack([gl[li][j] for li in range(DEPTH)])
    g_norm_g, g_sinks, g_ssm_d, g_b_glu, g_w_pool, g_pool_scale, g_b_ada = (st(j) for j in range(7))
    d_lr, d_li, d_bbr, d_bbi, g_c_re, g_c_im = (st(j) for j in range(7, 13))
    g_a_re, g_a_im, g_log_dt, g_b_re, g_b_im = disc_vjp((d_lr, d_li, d_bbr, d_bbi))

    dmod_all = lax.dynamic_slice(dmod_gathered.reshape(N_DEV, DEPTH, 3 * d), (0, 0, me * ada_w), (N_DEV, DEPTH, ada_w))
    dmod_all = dmod_all.transpose(1, 0, 2)
    g_w_ada = jnp.stack([_mm_tn(c_act, dmod_all[li], tm=d, tn=ada_w, tk=N_DEV, name=f"dw_ada_l{li}") for li in range(DEPTH)])

    adam("w_ada", w_ada, g_w_ada[None], m_w_ada, v_w_ada)
    adam("w_in", w_in, big[0], m_w_in, v_w_in)
    adam("w_glu", w_glu, big[1], m_w_glu, v_w_glu)
    adam("w_br_att", w_br_att, big[2], m_w_br_att, v_w_br_att)
    adam("w_br_ssm", w_br_ssm, big[3], m_w_br_ssm, v_w_br_ssm)
    adam("w_br_pool", w_br_pool, big[4], m_w_br_pool, v_w_br_pool)
    adam("w_out", w_out, big[5], m_w_out, v_w_out)

    small_names = ["norm_g", "b_ada", "attn_sinks", "ssm_a_re", "ssm_a_im", "ssm_log_dt", "ssm_b_re", "ssm_b_im",
                   "ssm_c_re", "ssm_c_im", "ssm_d", "b_glu", "w_pool", "pool_scale", "final_g"]
    small_w = [norm_g, b_ada, attn_sinks, ssm_a_re, ssm_a_im, ssm_log_dt, ssm_b_re, ssm_b_im, ssm_c_re, ssm_c_im,
               ssm_d, b_glu, w_pool, pool_scale, final_g]
    small_m = [m_norm_g, m_b_ada, m_attn_sinks, m_ssm_a_re, m_ssm_a_im, m_ssm_log_dt, m_ssm_b_re, m_ssm_b_im,
               m_ssm_c_re, m_ssm_c_im, m_ssm_d, m_b_glu, m_w_pool, m_pool_scale, m_final_g]
    small_v = [v_norm_g, v_b_ada, v_attn_sinks, v_ssm_a_re, v_ssm_a_im, v_ssm_log_dt, v_ssm_b_re, v_ssm_b_im,
               v_ssm_c_re, v_ssm_c_im, v_ssm_d, v_b_glu, v_w_pool, v_pool_scale, v_final_g]
    small_g = [g_norm_g, g_b_ada, g_sinks, g_a_re, g_a_im, g_log_dt, g_b_re, g_b_im, g_c_re, g_c_im,
               g_ssm_d, g_b_glu, g_w_pool, g_pool_scale, g_final_g]

    for nm, w, g, m, v in zip(small_names, small_w, small_g, small_m, small_v):
        adam(nm, w, g[None], m, v)

    order = ["norm_g", "w_ada", "b_ada", "w_in", "attn_sinks", "ssm_a_re", "ssm_a_im", "ssm_log_dt", "ssm_b_re",
             "ssm_b_im", "ssm_c_re", "ssm_c_im", "ssm_d", "w_glu", "b_glu", "w_pool", "pool_scale", "w_br_att",
             "w_br_ssm", "w_br_pool", "w_out", "final_g"]
    return (loss, grad_x, *[out[k][0] for k in order], *[out[k][1] for k in order],
            *[out[k][2] for k in order], *[out[k][3] for k in order])
```

```python
import functools
import math

import jax
import jax.numpy as jnp
from jax import lax
from jax.experimental import pallas as pl
from jax.experimental.pallas import tpu as pltpu

F32 = jnp.float32
BF16 = jnp.bfloat16

N_DEV = 8
D_MODEL = 1024
DEPTH = 2
CHUNK = 64
N_HEADS = 8
N_KV_HEADS = 2
HEAD_DIM = 64
Q_PER_KV = N_HEADS // N_KV_HEADS
WINDOW = 128
ATT_W = 512
KV_W = 128
SSM_W = 512
SSM_GROUP = 16
SSM_GROUPS = 32
SSM_STATE = 64
SSM_N = SSM_GROUPS * SSM_STATE
POOL_W = 512
POOL_WINDOWS = (2, 4, 8, 16)
POOL_GW = 128
POOL_HALO = 16
EPS = 1e-6
NEG_INF = -1e30
ADAM_LR = 0.001
ADAM_B1 = 0.9
ADAM_B2 = 0.999
ADAM_EPS = 1e-08
ADAM_WD = 0.01
ADAM_STEP = 10

SEQ_BLOCK = 256
VMEM_LIMIT = 56 * 1024 * 1024

NN = (((1,), (0,)), ((), ()))
NT = (((1,), (1,)), ((), ()))
TN = (((0,), (0,)), ((), ()))


def _dot(a, b, dims=NN):
    return lax.dot_general(a.astype(BF16), b.astype(BF16), dims, preferred_element_type=F32)


def _params(*sem):
    return pltpu.CompilerParams(dimension_semantics=sem, vmem_limit_bytes=VMEM_LIMIT)


def _sigmoid(x):
    return 1.0 / (1.0 + jnp.exp(-x))


def _silu_and_grad(z):
    s = _sigmoid(z)
    return z * s, s * (1.0 + z * (1.0 - s))


_GELU_K = math.sqrt(2.0 / math.pi)


def _gelu_and_grad(x):
    inner = _GELU_K * (x + 0.044715 * x * x * x)
    t = jnp.tanh(inner)
    val = 0.5 * x * (1.0 + t)
    grad = 0.5 * (1.0 + t) + 0.5 * x * (1.0 - t * t) * _GELU_K * (1.0 + 3.0 * 0.044715 * x * x)
    return val, grad


def _mm(a, b, *, nt=False, out_dtype=F32, tm=1024, tn=1024, name):
    m, k = a.shape
    n = b.shape[0] if nt else b.shape[1]
    tm, tn = min(tm, m), min(tn, n)
    assert m % tm == 0 and n % tn == 0
    dims = NT if nt else NN

    def body(a_ref, b_ref, o_ref):
        o_ref[...] = _dot(a_ref[...], b_ref[...], dims).astype(out_dtype)

    b_spec = pl.BlockSpec((tn, k), lambda i, j: (j, 0)) if nt else pl.BlockSpec((k, tn), lambda i, j: (0, j))
    return pl.pallas_call(
        body, grid=(m // tm, n // tn),
        in_specs=[pl.BlockSpec((tm, k), lambda i, j: (i, 0)), b_spec],
        out_specs=pl.BlockSpec((tm, tn), lambda i, j: (i, j)),
        out_shape=jax.ShapeDtypeStruct((m, n), out_dtype),
        compiler_params=_params("parallel", "parallel"), name=name)(a, b)


def _mm_nt_sum(pairs, *, out_dtype=F32, tm=512, tn=512, name):
    m = pairs[0][0].shape[0]
    n = pairs[0][1].shape[0]
    np_ = len(pairs)

    def body(*refs):
        o_ref = refs[-1]
        acc = _dot(refs[0][...], refs[1][...], NT)
        for p in range(1, np_):
            acc = acc + _dot(refs[2 * p][...], refs[2 * p + 1][...], NT)
        o_ref[...] = acc.astype(out_dtype)

    in_specs, args = [], []
    for a, b in pairs:
        in_specs.append(pl.BlockSpec((tm, a.shape[1]), lambda i, j: (i, 0)))
        in_specs.append(pl.BlockSpec((tn, b.shape[1]), lambda i, j: (j, 0)))
        args += [a, b]
    return pl.pallas_call(
        body, grid=(m // tm, n // tn), in_specs=in_specs,
        out_specs=pl.BlockSpec((tm, tn), lambda i, j: (i, j)),
        out_shape=jax.ShapeDtypeStruct((m, n), out_dtype),
        compiler_params=_params("parallel", "parallel"), name=name)(*args)


def _mm_tn(a, b, *, out_dtype=F32, tm=1024, tn=1024, tk=1024, name):
    k, m = a.shape
    n = b.shape[1]
    assert m % min(tm, m) == 0 and n % min(tn, n) == 0 and k % min(tk, k) == 0
    tm, tn, tk = min(tm, m), min(tn, n), min(tk, k)
    nk = k // tk

    def body(a_ref, b_ref, o_ref, acc_ref):
        kk = pl.program_id(2)

        @pl.when(kk == 0)
        def _():
            acc_ref[...] = jnp.zeros_like(acc_ref)

        acc_ref[...] += _dot(a_ref[...], b_ref[...], TN)

        @pl.when(kk == nk - 1)
        def _():
            o_ref[...] = acc_ref[...].astype(out_dtype)

    return pl.pallas_call(
        body, grid=(m // tm, n // tn, nk),
        in_specs=[pl.BlockSpec((tk, tm), lambda i, j, kk: (kk, i)), pl.BlockSpec((tk, tn), lambda i, j, kk: (kk, j))],
        out_specs=pl.BlockSpec((tm, tn), lambda i, j, kk: (i, j)),
        out_shape=jax.ShapeDtypeStruct((m, n), out_dtype),
        scratch_shapes=[pltpu.VMEM((tm, tn), F32)],
        compiler_params=_params("parallel", "parallel", "arbitrary"), name=name)(a, b)


def _ln_fwd(x, g, shift, scale, *, name, tm=512):
    l, d = x.shape

    def body(x_ref, g_ref, sh_ref, sc_ref, h_ref):
        xv = x_ref[...]
        n = xv * lax.rsqrt(jnp.mean(xv * xv, axis=-1, keepdims=True) + EPS)
        h_ref[...] = ((n * g_ref[...]) * (1.0 + sc_ref[...]) + sh_ref[...]).astype(BF16)

    vec = pl.BlockSpec((1, d), lambda i: (0, 0))
    return pl.pallas_call(
        body, grid=(l // tm,),
        in_specs=[pl.BlockSpec((tm, d), lambda i: (i, 0)), vec, vec, vec],
        out_specs=pl.BlockSpec((tm, d), lambda i: (i, 0)),
        out_shape=jax.ShapeDtypeStruct((l, d), BF16),
        compiler_params=_params("parallel"), name=name)(x, g, shift, scale)


def _ln_bwd(x, dh, dres, g, scale, *, name, tm=512):
    l, d = x.shape

    def body(x_ref, dh_ref, dres_ref, g_ref, sc_ref, dx_ref, sums_ref):
        xv = x_ref[...]
        dhv = dh_ref[...]
        rstd = lax.rsqrt(jnp.mean(xv * xv, axis=-1, keepdims=True) + EPS)
        n = xv * rstd
        gv = g_ref[...]
        dr = dhv * (1.0 + sc_ref[...])
        dn = dr * gv
        dx_ref[...] = dres_ref[...] + rstd * (dn - n * jnp.mean(dn * n, axis=-1, keepdims=True))

        @pl.when(pl.program_id(0) == 0)
        def _():
            sums_ref[...] = jnp.zeros_like(sums_ref)

        sums_ref[0:1, :] += jnp.sum(dhv, axis=0, keepdims=True)
        sums_ref[1:2, :] += jnp.sum(dhv * (n * gv), axis=0, keepdims=True)
        sums_ref[2:3, :] += jnp.sum(dr * n, axis=0, keepdims=True)

    vec = pl.BlockSpec((1, d), lambda i: (0, 0))
    row = pl.BlockSpec((tm, d), lambda i: (i, 0))
    return pl.pallas_call(
        body, grid=(l // tm,),
        in_specs=[row, row, row, vec, vec],
        out_specs=[row, pl.BlockSpec((8, d), lambda i: (0, 0))],
        out_shape=[jax.ShapeDtypeStruct((l, d), F32), jax.ShapeDtypeStruct((8, d), F32)],
        compiler_params=_params("arbitrary"), name=name)(x, dh, dres, g, scale)


def _final_loss(x, g, target, *, tm=512):
    l, d = x.shape

    def body(x_ref, g_ref, t_ref, dx_ref, sums_ref):
        xv = x_ref[...]
        rstd = lax.rsqrt(jnp.mean(xv * xv, axis=-1, keepdims=True) + EPS)
        n = xv * rstd
        gv = g_ref[...]
        err = n * gv - t_ref[...]
        dy = err * (1.0 / d)
        dn = dy * gv
        dx_ref[...] = rstd * (dn - n * jnp.mean(dn * n, axis=-1, keepdims=True))

        @pl.when(pl.program_id(0) == 0)
        def _():
            sums_ref[...] = jnp.zeros_like(sums_ref)

        sums_ref[0:1, :] += jnp.sum(dy * n, axis=0, keepdims=True)
        sums_ref[1:2, :] += jnp.sum(err * err, axis=0, keepdims=True) * (0.5 / d)

    vec = pl.BlockSpec((1, d), lambda i: (0, 0))
    row = pl.BlockSpec((tm, d), lambda i: (i, 0))
    dx, sums = pl.pallas_call(
        body, grid=(l // tm,),
        in_specs=[row, vec, row],
        out_specs=[row, pl.BlockSpec((8, d), lambda i: (0, 0))],
        out_shape=[jax.ShapeDtypeStruct((l, d), F32), jax.ShapeDtypeStruct((8, d), F32)],
        compiler_params=_params("arbitrary"), name="final_loss")(x, g, target)
    return dx, sums


def _attn_geometry(i, t):
    nk = t + WINDOW
    qi = lax.broadcasted_iota(jnp.int32, (t, nk), 0)
    kj = lax.broadcasted_iota(jnp.int32, (t, nk), 1)
    dist = jnp.abs(qi + WINDOW - kj).astype(F32)
    qc = jnp.right_shift(qi, 6)
    kc = jnp.right_shift(kj, 6)
    valid = (kc >= qc) & (kc <= qc + WINDOW // CHUNK) & ((i > 0) | (kj >= WINDOW))
    return dist, valid


def _attn_head(q, k_all, v_all, sink, slope, dist, valid):
    s = _dot(q, k_all, NT) * (1.0 / math.sqrt(HEAD_DIM)) - slope * dist
    s = jnp.where(valid, s, NEG_INF)
    m = jnp.maximum(jnp.max(s, axis=-1, keepdims=True), sink)
    e = jnp.exp(s - m)
    es = jnp.exp(sink - m)
    inv = 1.0 / (jnp.sum(e, axis=-1, keepdims=True) + es)
    p = e * inv
    o = _dot(p, v_all, NN)
    return p, o, es * inv


def _attn_specs(t):
    cur = pl.BlockSpec((t, ATT_W * 2 + KV_W * 2), lambda i: (i, 0))
    halo_blocks = t // WINDOW
    prev = pl.BlockSpec((WINDOW, 2 * KV_W), lambda i: (jnp.maximum(i * halo_blocks - 1, 0), (2 * ATT_W) // (2 * KV_W)))
    return cur, prev


def _attn_fwd(pa, sinks, *, name, t=SEQ_BLOCK, comm=None):
    l = pa.shape[0]
    t = min(t, l)
    nb = l // t
    c_args, c_in, c_out, c_shape, c_scratch = _comm_extra(comm)

    def body(sink_ref, cur_ref, prev_ref, ya_ref):
        i = pl.program_id(0)
        dist, valid = _attn_geometry(i, t)
        for h in range(N_HEADS):
            kh = h // Q_PER_KV
            q = cur_ref[:, h * HEAD_DIM:(h + 1) * HEAD_DIM]
            z = cur_ref[:, ATT_W + h * HEAD_DIM:ATT_W + (h + 1) * HEAD_DIM]
            k_all = jnp.concatenate([prev_ref[:, kh * HEAD_DIM:(kh + 1) * HEAD_DIM],
                                     cur_ref[:, 2 * ATT_W + kh * HEAD_DIM:2 * ATT_W + (kh + 1) * HEAD_DIM]], axis=0)
            v_all = jnp.concatenate([prev_ref[:, KV_W + kh * HEAD_DIM:KV_W + (kh + 1) * HEAD_DIM],
                                     cur_ref[:, 2 * ATT_W + KV_W + kh * HEAD_DIM:2 * ATT_W + KV_W + (kh + 1) * HEAD_DIM]], axis=0)
            _, o, _ = _attn_head(q, k_all, v_all, sink_ref[h], 2.0 ** (-(h + 1)), dist, valid)
            sz, _ = _silu_and_grad(z)
            ya_ref[:, h * HEAD_DIM:(h + 1) * HEAD_DIM] = (o * sz).astype(BF16)

    cur, prev = _attn_specs(t)
    res = pl.pallas_call(
        _with_comm(body, comm, 3, 1, nb, nb - 1), grid=(nb,),
        in_specs=[pl.BlockSpec(memory_space=pltpu.SMEM), cur, prev] + c_in,
        out_specs=[pl.BlockSpec((t, ATT_W), lambda i: (i, 0))] + c_out,
        out_shape=[jax.ShapeDtypeStruct((l, ATT_W), BF16)] + c_shape,
        scratch_shapes=c_scratch,
        compiler_params=_params("arbitrary"), name=name)(sinks, pa, pa, *c_args)
    return res[0], res[1:]


def _attn_bwd(pa, sinks, dya, *, name, t=SEQ_BLOCK):
    l = pa.shape[0]
    t = min(t, l)
    nb = l // t
    scale = 1.0 / math.sqrt(HEAD_DIM)

    def body(sink_ref, cur_ref, prev_ref, dya_ref, dpa_ref, dsink_ref, carry_ref):
        n = pl.program_id(0)
        i = nb - 1 - n
        dist, valid = _attn_geometry(i, t)

        @pl.when(n == 0)
        def _():
            carry_ref[...] = jnp.zeros_like(carry_ref)
            dsink_ref[...] = jnp.zeros_like(dsink_ref)

        dk_acc = [jnp.zeros((t + WINDOW, HEAD_DIM), F32) for _ in range(N_KV_HEADS)]
        dv_acc = [jnp.zeros((t + WINDOW, HEAD_DIM), F32) for _ in range(N_KV_HEADS)]
        for h in range(N_HEADS):
            kh = h // Q_PER_KV
            q = cur_ref[:, h * HEAD_DIM:(h + 1) * HEAD_DIM]
            z = cur_ref[:, ATT_W + h * HEAD_DIM:ATT_W + (h + 1) * HEAD_DIM]
            k_all = jnp.concatenate([prev_ref[:, kh * HEAD_DIM:(kh + 1) * HEAD_DIM],
                                     cur_ref[:, 2 * ATT_W + kh * HEAD_DIM:2 * ATT_W + (kh + 1) * HEAD_DIM]], axis=0)
            v_all = jnp.concatenate([prev_ref[:, KV_W + kh * HEAD_DIM:KV_W + (kh + 1) * HEAD_DIM],
                                     cur_ref[:, 2 * ATT_W + KV_W + kh * HEAD_DIM:2 * ATT_W + KV_W + (kh + 1) * HEAD_DIM]], axis=0)
            p, o, p_sink = _attn_head(q, k_all, v_all, sink_ref[h], 2.0 ** (-(h + 1)), dist, valid)
            dy = dya_ref[:, h * HEAD_DIM:(h + 1) * HEAD_DIM]
            sz, dsz = _silu_and_grad(z)
            do = dy * sz
            dpa_ref[:, ATT_W + h * HEAD_DIM:ATT_W + (h + 1) * HEAD_DIM] = (dy * o * dsz).astype(BF16)
            delta = jnp.sum(do * o, axis=-1, keepdims=True)
            dp = _dot(do, v_all, NT)
            ds = p * (dp - delta)
            dpa_ref[:, h * HEAD_DIM:(h + 1) * HEAD_DIM] = (_dot(ds, k_all, NN) * scale).astype(BF16)
            dk_acc[kh] = dk_acc[kh] + _dot(ds, q, TN) * scale
            dv_acc[kh] = dv_acc[kh] + _dot(p, do, TN)
            dsink_ref[h:h + 1, :] += jnp.broadcast_to(-jnp.sum(p_sink * delta, axis=0, keepdims=True), (1, 128))

        for kh in range(N_KV_HEADS):
            for which, acc in ((0, dk_acc[kh]), (1, dv_acc[kh])):
                c0 = which * KV_W + kh * HEAD_DIM
                own = acc[WINDOW:, :]
                tail = own[t - WINDOW:, :] + carry_ref[:, c0:c0 + HEAD_DIM]
                dpa_ref[0:t - WINDOW, 2 * ATT_W + c0:2 * ATT_W + c0 + HEAD_DIM] = own[:t - WINDOW, :].astype(BF16)
                dpa_ref[t - WINDOW:t, 2 * ATT_W + c0:2 * ATT_W + c0 + HEAD_DIM] = tail.astype(BF16)
                carry_ref[:, c0:c0 + HEAD_DIM] = acc[:WINDOW, :]

    halo_blocks = t // WINDOW
    wpa = 2 * ATT_W + 2 * KV_W
    cur = pl.BlockSpec((t, wpa), lambda n: (nb - 1 - n, 0))
    prev = pl.BlockSpec((WINDOW, 2 * KV_W),
                        lambda n: (jnp.maximum((nb - 1 - n) * halo_blocks - 1, 0), (2 * ATT_W) // (2 * KV_W)))
    return pl.pallas_call(
        body, grid=(nb,),
        in_specs=[pl.BlockSpec(memory_space=pltpu.SMEM), cur, prev, pl.BlockSpec((t, ATT_W), lambda n: (nb - 1 - n, 0))],
        out_specs=[pl.BlockSpec((t, wpa), lambda n: (nb - 1 - n, 0)), pl.BlockSpec((8, 128), lambda n: (0, 0))],
        out_shape=[jax.ShapeDtypeStruct((l, wpa), BF16), jax.ShapeDtypeStruct((8, 128), F32)],
        scratch_shapes=[pltpu.VMEM((WINDOW, 2 * KV_W), F32)],
        compiler_params=_params("arbitrary"), name=name)(sinks, pa, pa, dya)


def _scan(xr, xi, lr, li, t, reverse):
    row = lax.broadcasted_iota(jnp.int32, (t, 1), 0)
    d = 1
    pr, pi = lr, li
    while d < t:
        if reverse:
            sr = jnp.where(row < t - d, pltpu.roll(xr, t - d, 0), 0.0)
            si = jnp.where(row < t - d, pltpu.roll(xi, t - d, 0), 0.0)
        else:
            sr = jnp.where(row >= d, pltpu.roll(xr, d, 0), 0.0)
            si = jnp.where(row >= d, pltpu.roll(xi, d, 0), 0.0)
        xr, xi = xr + pr * sr - pi * si, xi + pr * si + pi * sr
        pr, pi = pr * pr - pi * pi, 2.0 * pr * pi
        d *= 2
    return xr, xi


SCAN_SUB = 8


def _split_hi_lo(a):
    hi = a.astype(BF16)
    lo = (a - hi.astype(F32)).astype(BF16)
    return jnp.concatenate([hi, lo], axis=0)


def _scan_mxu(xr, xi, tab, lam3, lam8, tri, expand, cr, ci, t, reverse):
    ns = t // SCAN_SUB
    n = xr.shape[1]
    v3 = lambda a: a.reshape(ns, SCAN_SUB, n)
    x3r, x3i = v3(xr), v3(xi)
    br = (x3r * tab[0] - x3i * tab[1]).reshape(t, n)
    bi = (x3r * tab[1] + x3i * tab[0]).reshape(t, n)
    pm = jnp.dot(tri, jnp.concatenate([br, bi], axis=1).astype(BF16), preferred_element_type=F32)
    p3r, p3i = v3(pm[:t, :n]), v3(pm[:t, n:])
    slr = p3r * tab[2] - p3i * tab[3]
    sli = p3r * tab[3] + p3i * tab[2]
    totr, toti = pm[t:, :n], pm[t:, n:]
    l3r, l3i = lam3
    l8r, l8i = lam8
    row = lax.broadcasted_iota(jnp.int32, (ns, 1), 0)
    edge = row == (ns - 1 if reverse else 0)
    er = totr * l3r - toti * l3i + jnp.where(edge, l8r * cr - l8i * ci, 0.0)
    ei = totr * l3i + toti * l3r + jnp.where(edge, l8r * ci + l8i * cr, 0.0)
    er, ei = _scan(er, ei, l8r, l8i, ns, reverse)
    shift = ns - 1 if reverse else 1
    nbr = jnp.where(edge, cr, pltpu.roll(er, shift, 0))
    nbi = jnp.where(edge, ci, pltpu.roll(ei, shift, 0))
    ex = jnp.dot(expand, _split_hi_lo(jnp.concatenate([nbr, nbi], axis=1)), preferred_element_type=F32)
    e3r, e3i = v3(ex[:, :n]), v3(ex[:, n:])
    sr = (slr + e3r * tab[4] - e3i * tab[5]).reshape(t, n)
    si = (sli + e3r * tab[5] + e3i * tab[4]).reshape(t, n)
    out = 0 if reverse else ns - 1
    return sr, si, er[out:out + 1, :], ei[out:out + 1, :]


def _scan_consts(t):
    import numpy as np
    ns = t // SCAN_SUB
    r = np.arange(t)
    same = (r[:, None] // SCAN_SUB) == (r[None, :] // SCAN_SUB)
    sums = (np.arange(ns)[:, None] == (r[None, :] // SCAN_SUB))
    tri = []
    for keep in (r[None, :] <= r[:, None], r[None, :] >= r[:, None]):
        tri.append(np.concatenate([same & keep, sums], axis=0).astype(np.float32))
    ex = ((r[:, None] // SCAN_SUB) == np.arange(ns)[None, :]).astype(np.float32)
    return jnp.asarray(np.stack(tri), BF16), jnp.asarray(np.concatenate([ex, ex], axis=1), BF16)


def _scan_tables(lr, li):
    den = lr * lr + li * li
    ir, ii = lr / den, -li / den
    mul = lambda a, b: (a[0] * b[0] - a[1] * b[1], a[0] * b[1] + a[1] * b[0])
    pw = {0: (jnp.ones_like(lr), jnp.zeros_like(lr))}
    for e in range(1, 9):
        pw[e] = mul(pw[e - 1], (lr, li))
    for e in range(-1, -5, -1):
        pw[e] = mul(pw[e + 1], (ir, ii))
    stack = lambda es, sign: (jnp.stack([pw[e][0] for e in es]), sign * jnp.stack([pw[e][1] for e in es]))
    j = range(SCAN_SUB)
    parts = [stack([4 - k for k in j], 1.0), stack([k - 4 for k in j], 1.0), stack([k + 1 for k in j], 1.0),
             stack([k - 3 for k in j], -1.0), stack([3 - k for k in j], -1.0), stack([8 - k for k in j], -1.0)]
    tabs = jnp.stack([a for pair in parts for a in pair])
    lam = jnp.zeros((8, lr.shape[0]), F32)
    for k, v in enumerate((lr, li, pw[3][0], pw[3][1], pw[8][0], pw[8][1])):
        lam = lam.at[k].set(v)
    return lam, tabs


SSM_HALVES = 2
SSM_HW = SSM_W // SSM_HALVES
SSM_HN = SSM_N // SSM_HALVES


def _bd_nn(x, w):
    a = w.shape[1]
    return jnp.concatenate([_dot(x[:, h * a:(h + 1) * a], w[h]) for h in range(SSM_HALVES)], axis=1)


def _bd_nt(x, w):
    b = w.shape[2]
    return jnp.concatenate([_dot(x[:, h * b:(h + 1) * b], w[h], NT) for h in range(SSM_HALVES)], axis=1)


def _bd_tn(x, y):
    a, b = x.shape[1] // SSM_HALVES, y.shape[1] // SSM_HALVES
    return jnp.stack([_dot(x[:, h * a:(h + 1) * a], y[:, h * b:(h + 1) * b], TN) for h in range(SSM_HALVES)])


def _ssm_states(u, s0r, s0i, lam_ref, tab_ref, tri_ref, ex_ref, bre, bim, t):
    tab = tuple(tab_ref[k] for k in range(6))
    return _scan_mxu(_bd_nn(u, bre), _bd_nn(u, bim), tab, (lam_ref[2:3, :], lam_ref[3:4, :]),
                     (lam_ref[4:5, :], lam_ref[5:6, :]), tri_ref[0], ex_ref[...], s0r, s0i, t, False)


def _ssm_head(u, z, xr, xi, cre, cim, dskip, wglu, bglu):
    y = _bd_nn(xr, cre) - _bd_nn(xi, cim) + dskip * u
    y2, dgelu = _gelu_and_grad(y)
    gate = _sigmoid(_dot(y2, wglu) + bglu)
    y3 = y2 * gate
    return y2, dgelu, gate, y3


def _with_comm(body, comm, n_in, n_out, nb, mid_step):
    if comm is None:
        return body
    nc = len(comm["args"])
    n_sem = len(comm["scratch"])

    def hosted(*refs):
        ins, cin = refs[:n_in], refs[n_in:n_in + nc]
        outs, cout = refs[n_in + nc:n_in + nc + n_out], refs[n_in + nc + n_out:n_in + 2 * nc + n_out]
        rest = refs[n_in + 2 * nc + n_out:]
        scratch, csem = rest[:len(rest) - n_sem], rest[len(rest) - n_sem:]
        start, forward, finish = comm["phases"](cin, cout, *csem)
        i = pl.program_id(0)
        pl.when(i == 0)(start)
        pl.when(i == mid_step)(forward)
        body(*ins, *outs, *scratch)
        pl.when(i == nb - 1)(finish)

    return hosted


def _comm_extra(comm):
    if comm is None:
        return [], [], [], [], []
    anyspec = pl.BlockSpec(memory_space=pl.ANY)
    nc = len(comm["args"])
    return comm["args"], [anyspec] * nc, [anyspec] * nc, comm["out_shape"], comm["scratch"]


def _ssm_fwd(ps, scan_ops, bblk, cblk, dskip, wglu, bglu, *, name, t=SEQ_BLOCK, comm=None):
    l = ps.shape[0]
    assert l % t == 0
    nb = l // t
    ns = t // SCAN_SUB
    c_args, c_in, c_out, c_shape, c_scratch = _comm_extra(comm)

    def body(ps_ref, lam_ref, tab_ref, tri_ref, ex_ref, b_ref, c_ref, d_ref, w_ref, bg_ref, ys_ref, chk_ref, xs_ref,
             st_ref):
        @pl.when(pl.program_id(0) == 0)
        def _():
            st_ref[...] = jnp.zeros_like(st_ref)

        chk_ref[...] = jnp.broadcast_to(st_ref[...], chk_ref.shape)
        u = ps_ref[:, :SSM_W]
        z = ps_ref[:, SSM_W:]
        xr, xi, er, ei = _ssm_states(u, st_ref[:, :SSM_N], st_ref[:, SSM_N:], lam_ref, tab_ref, tri_ref, ex_ref,
                                     b_ref[0], b_ref[1], t)
        st_ref[:, :SSM_N] = er
        st_ref[:, SSM_N:] = ei
        xr, xi = xr.astype(BF16), xi.astype(BF16)
        xs_ref[:, :SSM_N] = xr
        xs_ref[:, SSM_N:] = xi
        _, _, _, y3 = _ssm_head(u, z, xr, xi, c_ref[0], c_ref[1], d_ref[...], w_ref[...], bg_ref[...])
        sz, _ = _silu_and_grad(z)
        ys_ref[...] = (y3 * sz).astype(BF16)

    full = lambda shape: pl.BlockSpec(shape, lambda i: (0,) * len(shape))
    return pl.pallas_call(
        _with_comm(body, comm, 10, 3, nb, nb - 1), grid=(nb,),
        in_specs=[pl.BlockSpec((t, 2 * SSM_W), lambda i: (i, 0)), full((8, SSM_N)), full((12, SCAN_SUB, SSM_N)),
                  full((2, t + ns, t)), full((t, 2 * ns)), full((2, SSM_HALVES, SSM_HW, SSM_HN)),
                  full((2, SSM_HALVES, SSM_HN, SSM_HW)), full((1, SSM_W)), full((SSM_W, SSM_W)), full((1, SSM_W))] + c_in,
        out_specs=[pl.BlockSpec((t, SSM_W), lambda i: (i, 0)), pl.BlockSpec((8, 2 * SSM_N), lambda i: (i, 0)),
                   pl.BlockSpec((t, 2 * SSM_N), lambda i: (i, 0))] + c_out,
        out_shape=[jax.ShapeDtypeStruct((l, SSM_W), BF16), jax.ShapeDtypeStruct((nb * 8, 2 * SSM_N), F32),
                   jax.ShapeDtypeStruct((l, 2 * SSM_N), BF16)] + c_shape,
        scratch_shapes=[pltpu.VMEM((1, 2 * SSM_N), F32)] + c_scratch,
        compiler_params=_params("arbitrary"), name=name)(ps, *scan_ops, bblk, cblk, dskip, wglu, bglu, *c_args)


def _ssm_bwd(ps, dys, chk, states, scan_ops, bblk, cblk, dskip, wglu, bglu, *, name, t=SEQ_BLOCK, comm=None):
    l = ps.shape[0]
    assert l % t == 0
    nb = l // t
    ns = t // SCAN_SUB
    c_args, c_in, c_out, c_shape, c_scratch = _comm_extra(comm)

    def body(ps_ref, dys_ref, chk_ref, xs_ref, lam_ref, tab_ref, tri_ref, ex_ref, b_ref, c_ref, d_ref, w_ref, bg_ref,
             dps_ref, db_ref, dc_ref, dw_acc, sums_acc, gc_ref, db_acc, dc_acc):
        n = pl.program_id(0)

        @pl.when(n == 0)
        def _():
            gc_ref[...] = jnp.zeros_like(gc_ref)
            db_acc[...] = jnp.zeros_like(db_acc)
            dc_acc[...] = jnp.zeros_like(dc_acc)
            dw_acc[...] = jnp.zeros_like(dw_acc)
            sums_acc[...] = jnp.zeros_like(sums_acc)

        row = lax.broadcasted_iota(jnp.int32, (t, 1), 0)
        u = ps_ref[:, :SSM_W]
        z = ps_ref[:, SSM_W:]
        s0r, s0i = chk_ref[0:1, :SSM_N], chk_ref[0:1, SSM_N:]
        xr, xi = xs_ref[:, :SSM_N], xs_ref[:, SSM_N:]
        dskip = d_ref[...]
        y2, dgelu, gate, y3 = _ssm_head(u, z, xr, xi, c_ref[0], c_ref[1], dskip, w_ref[...], bg_ref[...])
        sz, dsz = _silu_and_grad(z)
        dys_v = dys_ref[...]
        dps_ref[:, SSM_W:] = (dys_v * y3 * dsz).astype(BF16)
        dy3 = dys_v * sz
        da = dy3 * y2 * gate * (1.0 - gate)
        dy2 = dy3 * gate + _dot(da, w_ref[...], NT)
        dw_acc[...] += _dot(y2, da, TN)
        dy = dy2 * dgelu
        sums_acc[2:3, :SSM_W] += jnp.sum(dy * u, axis=0, keepdims=True)
        sums_acc[3:4, :SSM_W] += jnp.sum(da, axis=0, keepdims=True)
        dc_acc[0] += _bd_tn(xr, dy)
        dc_acc[1] += -_bd_tn(xi, dy)
        rev_tab = tuple(tab_ref[k] for k in range(6, 12))
        gr, gi, gcr, gci = _scan_mxu(
            _bd_nt(dy, c_ref[0]), -_bd_nt(dy, c_ref[1]), rev_tab, (lam_ref[2:3, :], -lam_ref[3:4, :]),
            (lam_ref[4:5, :], -lam_ref[5:6, :]), tri_ref[1], ex_ref[...], gc_ref[:, :SSM_N], gc_ref[:, SSM_N:], t, True)
        gc_ref[:, :SSM_N] = gcr
        gc_ref[:, SSM_N:] = gci
        db_acc[0] += _bd_tn(u, gr)
        db_acc[1] += _bd_tn(u, gi)
        du = dskip * dy + _bd_nt(gr, b_ref[0]) + _bd_nt(gi, b_ref[1])
        dps_ref[:, :SSM_W] = du.astype(BF16)
        spr = jnp.where(row == 0, s0r, pltpu.roll(xr.astype(F32), 1, 0))
        spi = jnp.where(row == 0, s0i, pltpu.roll(xi.astype(F32), 1, 0))
        sums_acc[0:1, :] += jnp.sum(gr * spr + gi * spi, axis=0, keepdims=True)
        sums_acc[1:2, :] += jnp.sum(gi * spr - gr * spi, axis=0, keepdims=True)

        @pl.when(n == nb - 1)
        def _():
            per_half = SSM_GROUPS // SSM_HALVES
            for k in range(2):
                for g in range(SSM_GROUPS):
                    h, gl = divmod(g, per_half)
                    c0, p0 = gl * SSM_GROUP, gl * SSM_STATE
                    db_ref[k, g * SSM_GROUP:(g + 1) * SSM_GROUP, :] = db_acc[k, h, c0:c0 + SSM_GROUP, p0:p0 + SSM_STATE]
                    dc_ref[k, g * SSM_STATE:(g + 1) * SSM_STATE, :] = dc_acc[k, h, p0:p0 + SSM_STATE, c0:c0 + SSM_GROUP]

    full = lambda shape: pl.BlockSpec(shape, lambda n: (0,) * len(shape))
    return pl.pallas_call(
        _with_comm(body, comm, 13, 5, nb, 0), grid=(nb,),
        in_specs=[pl.BlockSpec((t, 2 * SSM_W), lambda n: (nb - 1 - n, 0)),
                  pl.BlockSpec((t, SSM_W), lambda n: (nb - 1 - n, 0)),
                  pl.BlockSpec((8, 2 * SSM_N), lambda n: (nb - 1 - n, 0)),
                  pl.BlockSpec((t, 2 * SSM_N), lambda n: (nb - 1 - n, 0)),
                  full((8, SSM_N)), full((12, SCAN_SUB, SSM_N)), full((2, t + ns, t)), full((t, 2 * ns)),
                  full((2, SSM_HALVES, SSM_HW, SSM_HN)), full((2, SSM_HALVES, SSM_HN, SSM_HW)), full((1, SSM_W)),
                  full((SSM_W, SSM_W)), full((1, SSM_W))] + c_in,
        out_specs=[pl.BlockSpec((t, 2 * SSM_W), lambda n: (nb - 1 - n, 0)), full((2, SSM_W, SSM_STATE)),
                   full((2, SSM_N, SSM_GROUP)), full((SSM_W, SSM_W)), full((8, SSM_N))] + c_out,
        out_shape=[jax.ShapeDtypeStruct((l, 2 * SSM_W), BF16),
                   jax.ShapeDtypeStruct((2, SSM_W, SSM_STATE), F32),
                   jax.ShapeDtypeStruct((2, SSM_N, SSM_GROUP), F32),
                   jax.ShapeDtypeStruct((SSM_W, SSM_W), F32),
                   jax.ShapeDtypeStruct((8, SSM_N), F32)] + c_shape,
        scratch_shapes=[pltpu.VMEM((1, 2 * SSM_N), F32), pltpu.VMEM((2, SSM_HALVES, SSM_HW, SSM_HN), F32),
                        pltpu.VMEM((2, SSM_HALVES, SSM_HN, SSM_HW), F32)] + c_scratch,
        compiler_params=_params("arbitrary"), name=name)(ps, dys, chk, states, *scan_ops, bblk, cblk, dskip, wglu, bglu,
                                                         *c_args)


def _pool_count(i, t):
    pos = lax.broadcasted_iota(jnp.int32, (t, POOL_W), 0) + i * t + 1
    col = lax.broadcasted_iota(jnp.int32, (t, POOL_W), 1)
    win = jnp.where(col < POOL_GW, 2, jnp.where(col < 2 * POOL_GW, 4, jnp.where(col < 3 * POOL_GW, 8, 16)))
    return 1.0 / jnp.minimum(pos, win).astype(F32), col


def _window_sums(ext, n_rows, forward):
    col = lax.broadcasted_iota(jnp.int32, ext.shape, 1)
    sh = (lambda a, d: pltpu.roll(a, d, 0)) if forward else (lambda a, d: pltpu.roll(a, n_rows - d, 0))
    a2 = ext + sh(ext, 1)
    a4 = a2 + sh(a2, 2)
    a8 = a4 + sh(a4, 4)
    a16 = a8 + sh(a8, 8)
    return jnp.where(col < POOL_GW, a2, jnp.where(col < 2 * POOL_GW, a4, jnp.where(col < 3 * POOL_GW, a8, a16)))


def _pool_mix(pooled, wp_ref):
    return jnp.concatenate([_dot(pooled[:, g * POOL_GW:(g + 1) * POOL_GW], wp_ref[g]) for g in range(4)], axis=1)


def _pool_pooled(i, cur_u, prev_u, t):
    prev = jnp.where(i > 0, prev_u, 0.0)
    ext = jnp.concatenate([prev, cur_u], axis=0)
    inv_cnt, _ = _pool_count(i, t)
    return _window_sums(ext, t + POOL_HALO, True)[POOL_HALO:, :] * inv_cnt - cur_u


def _pool_fwd(pp, wpool, pscale, *, name, t=SEQ_BLOCK):
    l = pp.shape[0]
    t = min(t, l)

    def body(cur_ref, prev_ref, wp_ref, sc_ref, yp_ref):
        i = pl.program_id(0)
        pooled = _pool_pooled(i, cur_ref[:, :POOL_W], prev_ref[...], t)
        lin = _pool_mix(pooled, wp_ref)
        sz, _ = _silu_and_grad(cur_ref[:, POOL_W:])
        yp_ref[...] = (lin * sc_ref[...] * sz).astype(BF16)

    hb = t // POOL_HALO
    return pl.pallas_call(
        body, grid=(l // t,),
        in_specs=[pl.BlockSpec((t, 2 * POOL_W), lambda i: (i, 0)),
                  pl.BlockSpec((POOL_HALO, POOL_W), lambda i: (jnp.maximum(i * hb - 1, 0), 0)),
                  pl.BlockSpec((4, POOL_GW, POOL_GW), lambda i: (0, 0, 0)),
                  pl.BlockSpec((1, POOL_W), lambda i: (0, 0))],
        out_specs=pl.BlockSpec((t, POOL_W), lambda i: (i, 0)),
        out_shape=jax.ShapeDtypeStruct((l, POOL_W), BF16),
        compiler_params=_params("parallel"), name=name)(pp, pp, wpool, pscale)


def _pool_bwd(pp, dyp, wpool, pscale, *, name, t=SEQ_BLOCK):
    l = pp.shape[0]
    t = min(t, l)
    nb = l // t

    def body(cur_ref, prev_ref, dyp_ref, wp_ref, sc_ref, dpp_ref, dwp_ref, sums_ref, carry_ref):
        n = pl.program_id(0)
        i = nb - 1 - n

        @pl.when(n == 0)
        def _():
            carry_ref[...] = jnp.zeros_like(carry_ref)
            dwp_ref[...] = jnp.zeros_like(dwp_ref)
            sums_ref[...] = jnp.zeros_like(sums_ref)

        cur_u = cur_ref[:, :POOL_W]
        pooled = _pool_pooled(i, cur_u, prev_ref[...], t)
        lin = _pool_mix(pooled, wp_ref)
        sz, dsz = _silu_and_grad(cur_ref[:, POOL_W:])
        dyp_v = dyp_ref[...]
        scale = sc_ref[...]
        dpp_ref[:, POOL_W:] = (dyp_v * lin * scale * dsz).astype(BF16)
        dpre = dyp_v * sz
        sums_ref[0:1, :] += jnp.sum(dpre * lin, axis=0, keepdims=True)
        dlin = dpre * scale
        dpooled = []
        for g in range(4):
            dl = dlin[:, g * POOL_GW:(g + 1) * POOL_GW]
            dwp_ref[g] += _dot(pooled[:, g * POOL_GW:(g + 1) * POOL_GW], dl, TN)
            dpooled.append(_dot(dl, wp_ref[g], NT))
        dpooled = jnp.concatenate(dpooled, axis=1)
        inv_cnt, _ = _pool_count(i, t)
        dq = dpooled * inv_cnt
        ext = jnp.concatenate([dq, carry_ref[...]], axis=0)
        du = _window_sums(ext, t + POOL_HALO, False)[:t, :] - dpooled
        dpp_ref[:, :POOL_W] = du.astype(BF16)
        carry_ref[...] = dq[:POOL_HALO, :]

    hb = t // POOL_HALO
    return pl.pallas_call(
        body, grid=(nb,),
        in_specs=[pl.BlockSpec((t, 2 * POOL_W), lambda n: (nb - 1 - n, 0)),
                  pl.BlockSpec((POOL_HALO, POOL_W), lambda n: (jnp.maximum((nb - 1 - n) * hb - 1, 0), 0)),
                  pl.BlockSpec((t, POOL_W), lambda n: (nb - 1 - n, 0)),
                  pl.BlockSpec((4, POOL_GW, POOL_GW), lambda n: (0, 0, 0)),
                  pl.BlockSpec((1, POOL_W), lambda n: (0, 0))],
        out_specs=[pl.BlockSpec((t, 2 * POOL_W), lambda n: (nb - 1 - n, 0)),
                   pl.BlockSpec((4, POOL_GW, POOL_GW), lambda n: (0, 0, 0)),
                   pl.BlockSpec((8, POOL_W), lambda n: (0, 0))],
        out_shape=[jax.ShapeDtypeStruct((l, 2 * POOL_W), BF16), jax.ShapeDtypeStruct((4, POOL_GW, POOL_GW), F32),
                   jax.ShapeDtypeStruct((8, POOL_W), F32)],
        scratch_shapes=[pltpu.VMEM((POOL_HALO, POOL_W), F32)],
        compiler_params=_params("arbitrary"), name=name)(pp, pp, dyp, wpool, pscale)


def _merge_fwd(ya, ys, yp, wa, ws, wp, pg, *, name, tm=256):
    l = ya.shape[0]
    tm = min(tm, l)
    d = D_MODEL

    def body(ya_ref, ys_ref, yp_ref, wa_ref, ws_ref, wp_ref, pg_ref, mg_ref, ba_ref, bs_ref, bp_ref):
        acc = None
        for k, (y_ref, w_ref, b_ref) in enumerate(((ya_ref, wa_ref, ba_ref), (ys_ref, ws_ref, bs_ref),
                                                   (yp_ref, wp_ref, bp_ref))):
            br = _dot(y_ref[...], w_ref[...])
            b_ref[...] = br
            term = _sigmoid(pg_ref[:, k * d:(k + 1) * d]) * br
            acc = term if acc is None else acc + term
        mg_ref[...] = acc.astype(BF16)

    rowy = pl.BlockSpec((tm, ATT_W), lambda i: (i, 0))
    wsp = pl.BlockSpec((ATT_W, d), lambda i: (0, 0))
    rowd = pl.BlockSpec((tm, d), lambda i: (i, 0))
    return pl.pallas_call(
        body, grid=(l // tm,),
        in_specs=[rowy, rowy, rowy, wsp, wsp, wsp, pl.BlockSpec((tm, 3 * d), lambda i: (i, 0))],
        out_specs=[rowd, rowd, rowd, rowd],
        out_shape=[jax.ShapeDtypeStruct((l, d), BF16)] + [jax.ShapeDtypeStruct((l, d), F32)] * 3,
        compiler_params=_params("parallel"), name=name)(ya, ys, yp, wa, ws, wp, pg)


def _out_fwd(merged, wout, x, gate, *, name, tm=512):
    l, d = x.shape
    tm = min(tm, l)

    def body(m_ref, w_ref, x_ref, g_ref, xn_ref, out_ref):
        out = _dot(m_ref[...], w_ref[...])
        out_ref[...] = out
        xn_ref[...] = x_ref[...] + g_ref[...] * out

    row = pl.BlockSpec((tm, d), lambda i: (i, 0))
    return pl.pallas_call(
        body, grid=(l // tm,),
        in_specs=[row, pl.BlockSpec((d, d), lambda i: (0, 0)), row, pl.BlockSpec((1, d), lambda i: (0, 0))],
        out_specs=[row, row],
        out_shape=[jax.ShapeDtypeStruct((l, d), F32)] * 2,
        compiler_params=_params("parallel"), name=name)(merged, wout, x, gate)


def _merge_bwd(dx, out, gate, wout, pg, ba, bs, bp, *, name, tm=256):
    l, d = dx.shape
    tm = min(tm, l)

    def body(dx_ref, out_ref, g_ref, w_ref, pg_ref, ba_ref, bs_ref, bp_ref,
             dmo_ref, dba_ref, dbs_ref, dbp_ref, dpg_ref, sums_ref):
        @pl.when(pl.program_id(0) == 0)
        def _():
            sums_ref[...] = jnp.zeros_like(sums_ref)

        dxv = dx_ref[...]
        sums_ref[0:1, :] += jnp.sum(dxv * out_ref[...], axis=0, keepdims=True)
        dmo = (dxv * g_ref[...]).astype(BF16)
        dmo_ref[...] = dmo
        dmerged = _dot(dmo, w_ref[...], NT)
        for k, (b_ref, db_ref) in enumerate(((ba_ref, dba_ref), (bs_ref, dbs_ref), (bp_ref, dbp_ref))):
            gk = _sigmoid(pg_ref[:, k * d:(k + 1) * d])
            db_ref[...] = (dmerged * gk).astype(BF16)
            dpg_ref[:, k * d:(k + 1) * d] = (dmerged * b_ref[...] * gk * (1.0 - gk)).astype(BF16)

    row = pl.BlockSpec((tm, d), lambda i: (i, 0))
    wide = pl.BlockSpec((tm, 3 * d), lambda i: (i, 0))
    return pl.pallas_call(
        body, grid=(l // tm,),
        in_specs=[row, row, pl.BlockSpec((1, d), lambda i: (0, 0)), pl.BlockSpec((d, d), lambda i: (0, 0)),
                  wide, row, row, row],
        out_specs=[row, row, row, row, wide, pl.BlockSpec((8, d), lambda i: (0, 0))],
        out_shape=[jax.ShapeDtypeStruct((l, d), BF16)] * 4 + [jax.ShapeDtypeStruct((l, 3 * d), BF16),
                                                             jax.ShapeDtypeStruct((8, d), F32)],
        compiler_params=_params("arbitrary"), name=name)(dx, out, gate, wout, pg, ba, bs, bp)


def _adamw(w, g, m, v, *, name, tr=256):
    r, c = w.shape
    p = g.shape[0]
    tr = min(tr, r)
    assert r % tr == 0
    c1 = 1.0 / (1.0 - ADAM_B1 ** ADAM_STEP)
    c2 = 1.0 / (1.0 - ADAM_B2 ** ADAM_STEP)

    def body(w_ref, g_ref, m_ref, v_ref, go_ref, d_ref, mo_ref, vo_ref):
        gv = g_ref[0].astype(F32)
        for k in range(1, p):
            gv = gv + g_ref[k].astype(F32)
        go_ref[...] = gv
        mn = ADAM_B1 * m_ref[...] + (1.0 - ADAM_B1) * gv
        vn = ADAM_B2 * v_ref[...] + (1.0 - ADAM_B2) * (gv * gv)
        mo_ref[...] = mn
        vo_ref[...] = vn
        d_ref[...] = -ADAM_LR * ((mn * c1) / (jnp.sqrt(vn * c2) + ADAM_EPS) + ADAM_WD * w_ref[...])

    row = pl.BlockSpec((tr, c), lambda i: (i, 0))
    return pl.pallas_call(
        body, grid=(r // tr,),
        in_specs=[row, pl.BlockSpec((p, tr, c), lambda i: (0, i, 0)), row, row],
        out_specs=[row] * 4,
        out_shape=[jax.ShapeDtypeStruct((r, c), F32)] * 4,
        compiler_params=_params("parallel"), name=name)(w, g, m, v)


def _exchange(arrs, *, scatter, name):
    n = len(arrs)
    out_shape = [jax.ShapeDtypeStruct(a.shape if scatter else (N_DEV,) + a.shape, a.dtype) for a in arrs]

    def body(*refs):
        ins, outs = refs[:n], refs[n:2 * n]
        send_sems, recv_sems, loc_sems = refs[2 * n:]
        me = 4 * lax.axis_index("x") + 2 * lax.axis_index("y") + lax.axis_index("c")
        local = []
        for k in range(n):
            src = ins[k].at[me] if scatter else ins[k]
            cp = pltpu.make_async_copy(src, outs[k].at[me], loc_sems.at[k])
            cp.start()
            local.append(cp)
        remote = []
        for r in range(1, N_DEV):
            peer = me ^ r
            for k in range(n):
                src = ins[k].at[peer] if scatter else ins[k]
                cp = pltpu.make_async_remote_copy(
                    src_ref=src, dst_ref=outs[k].at[me], send_sem=send_sems.at[k, r - 1], recv_sem=recv_sems.at[k, r - 1],
                    device_id=(peer // 4, (peer // 2) % 2, peer % 2), device_id_type=pl.DeviceIdType.MESH)
                cp.start()
                remote.append(cp)
        for cp in remote:
            cp.wait()
        for cp in local:
            cp.wait()

    anyspec = pl.BlockSpec(memory_space=pl.ANY)
    return pl.pallas_call(
        body, in_specs=[anyspec] * n, out_specs=[anyspec] * n, out_shape=out_shape,
        scratch_shapes=[pltpu.SemaphoreType.DMA((n, N_DEV - 1)), pltpu.SemaphoreType.DMA((n, N_DEV - 1)),
                        pltpu.SemaphoreType.DMA((n,))],
        name=name)(*arrs)


def _mesh_place():
    x, y, c = lax.axis_index("x"), lax.axis_index("y"), lax.axis_index("c")
    other_chips = [(1 - x, y), (x, 1 - y), (1 - x, 1 - y)]
    return x, y, c, other_chips


def _gather_two_level(arrs, *, name):
    n = len(arrs)
    plan = _gather_plan(arrs)

    def body(*refs):
        start, forward, finish = plan["phases"](refs[:n], refs[n:2 * n], *refs[2 * n:])
        start()
        forward()
        finish()

    anyspec = pl.BlockSpec(memory_space=pl.ANY)
    return pl.pallas_call(
        body, in_specs=[anyspec] * n, out_specs=[anyspec] * n, out_shape=plan["out_shape"],
        scratch_shapes=plan["scratch"], name=name)(*arrs)


def _gather_plan(arrs):
    n = len(arrs)

    def phases(ins, outs, send_sems, recv_sems, loc_sems):
        x, y, c, chips = _mesh_place()
        me = 4 * x + 2 * y + c
        slot = lambda px, py, pc: 4 * px + 2 * py + pc

        def copy(k, j, src, block, to):
            return pltpu.make_async_remote_copy(
                src_ref=src, dst_ref=outs[k].at[block], send_sem=send_sems.at[k, j], recv_sem=recv_sems.at[k, j],
                device_id=to, device_id_type=pl.DeviceIdType.MESH)

        local = [pltpu.make_async_copy(ins[k], outs[k].at[me], loc_sems.at[k]) for k in range(n)]
        first = []
        for k in range(n):
            first.append(copy(k, 0, ins[k], me, (x, y, 1 - c)))
            for j, chip in enumerate(chips):
                first.append(copy(k, 1 + j, ins[k], me, (*chip, c)))
        passed = [copy(k, 4 + j, outs[k].at[slot(*chip, c)], slot(*chip, c), (x, y, 1 - c))
                  for j, chip in enumerate(chips) for k in range(n)]

        def start():
            for cp in local + first:
                cp.start()

        def forward():
            for j, chip in enumerate(chips):
                for k in range(n):
                    copy(k, 1 + j, ins[k], slot(*chip, c), (x, y, c)).wait_recv()
                    passed[j * n + k].start()

        def finish():
            for k in range(n):
                copy(k, 0, ins[k], slot(x, y, 1 - c), (x, y, c)).wait_recv()
                for j, chip in enumerate(chips):
                    copy(k, 4 + j, ins[k], slot(*chip, 1 - c), (x, y, c)).wait_recv()
            for cp in first + passed:
                cp.wait_send()
            for cp in local:
                cp.wait()

        return start, forward, finish

    return dict(
        args=list(arrs), out_shape=[jax.ShapeDtypeStruct((N_DEV,) + a.shape, a.dtype) for a in arrs],
        scratch=[pltpu.SemaphoreType.DMA((n, 7)), pltpu.SemaphoreType.DMA((n, 7)), pltpu.SemaphoreType.DMA((n,))],
        phases=phases)


def _allreduce_small(small, extra, *, name):
    r, lanes = small.shape
    assert r % 16 == 0
    h = r // 2
    e = extra.shape[0]

    def body(s_ref, x_ref, out_ref, xall_ref, sib_ref, parts_ref, send_sems, recv_sems):
        x, y, c, chips = _mesh_place()
        me = 4 * x + 2 * y + c
        my_chip = 2 * x + y
        sibling = (x, y, 1 - c)
        mine = pl.ds(pl.multiple_of(c * h, 8), h)
        theirs = pl.ds(pl.multiple_of((1 - c) * h, 8), h)

        def remote(j, src, dst, to):
            return pltpu.make_async_remote_copy(src_ref=src, dst_ref=dst, send_sem=send_sems.at[j],
                                                recv_sem=recv_sems.at[j], device_id=to, device_id_type=pl.DeviceIdType.MESH)

        to_sibling = remote(0, s_ref.at[theirs], sib_ref, sibling)
        to_sibling.start()
        xall_ref[me] = x_ref[...]
        extras = []
        for rr in range(1, N_DEV):
            peer = me ^ rr
            cp = remote(4 + rr, x_ref, xall_ref.at[me], (peer // 4, (peer // 2) % 2, peer % 2))
            cp.start()
            extras.append(cp)
        to_sibling.wait_recv()
        parts_ref[my_chip] = s_ref[mine] + sib_ref[...]
        to_chips = [remote(1 + j, parts_ref.at[my_chip], parts_ref.at[my_chip], (px, py, c))
                    for j, (px, py) in enumerate(chips)]
        for cp in to_chips:
            cp.start()
        for cp in to_chips:
            cp.wait_recv()
        out_ref[mine] = (parts_ref[0] + parts_ref[1]) + (parts_ref[2] + parts_ref[3])
        done = remote(4, out_ref.at[mine], out_ref.at[mine], sibling)
        done.start()
        remote(4, out_ref.at[theirs], out_ref.at[theirs], sibling).wait_recv()
        for cp in extras:
            cp.wait()
        to_sibling.wait_send()
        for cp in to_chips:
            cp.wait_send()
        done.wait_send()

    vmem = pl.BlockSpec(memory_space=pltpu.VMEM)
    return pl.pallas_call(
        body, in_specs=[vmem, vmem], out_specs=[vmem, vmem],
        out_shape=[jax.ShapeDtypeStruct((r, lanes), F32), jax.ShapeDtypeStruct((N_DEV, e, lanes), F32)],
        scratch_shapes=[pltpu.VMEM((h, lanes), F32), pltpu.VMEM((4, h, lanes), F32),
                        pltpu.SemaphoreType.DMA((12,)), pltpu.SemaphoreType.DMA((12,))],
        compiler_params=pltpu.CompilerParams(vmem_limit_bytes=VMEM_LIMIT), name=name)(small, extra)


def _sibling_swap(arrs, *, name):
    n = len(arrs)
    out_shape = [jax.ShapeDtypeStruct(a.shape[1:], a.dtype) for a in arrs]

    def body(*refs):
        ins, outs = refs[:n], refs[n:2 * n]
        send_sems, recv_sems = refs[2 * n:]
        x, y, c, _ = _mesh_place()
        copies = [pltpu.make_async_remote_copy(
            src_ref=ins[k].at[1 - c], dst_ref=outs[k], send_sem=send_sems.at[k], recv_sem=recv_sems.at[k],
            device_id=(x, y, 1 - c), device_id_type=pl.DeviceIdType.MESH) for k in range(n)]
        for cp in copies:
            cp.start()
        for cp in copies:
            cp.wait()

    anyspec = pl.BlockSpec(memory_space=pl.ANY)
    return pl.pallas_call(
        body, in_specs=[anyspec] * n, out_specs=[anyspec] * n, out_shape=out_shape,
        scratch_shapes=[pltpu.SemaphoreType.DMA((n,)), pltpu.SemaphoreType.DMA((n,))], name=name)(*arrs)


def _pair_add(mine, theirs, core, *, name, tr=256):
    _, r, c = mine.shape
    tr = min(tr, r)
    assert r % tr == 0

    def body(core_ref, m_ref, t_ref, o_ref):
        o_ref[...] = (m_ref[0].astype(F32) + t_ref[...].astype(F32)).astype(BF16)

    return pl.pallas_call(
        body,
        grid_spec=pltpu.PrefetchScalarGridSpec(
            num_scalar_prefetch=1, grid=(r // tr,),
            in_specs=[pl.BlockSpec((1, tr, c), lambda i, core_ref: (core_ref[0], i, 0)),
                      pl.BlockSpec((tr, c), lambda i, core_ref: (i, 0))],
            out_specs=pl.BlockSpec((tr, c), lambda i, core_ref: (i, 0))),
        out_shape=jax.ShapeDtypeStruct((r, c), BF16),
        compiler_params=_params("parallel"), name=name)(core, mine, theirs)


def _chip_scatter(arrs, *, name):
    n = len(arrs)
    plan = _chip_scatter_plan(arrs)

    def body(*refs):
        start, _, finish = plan["phases"](refs[:n], refs[n:2 * n], *refs[2 * n:])
        start()
        finish()

    anyspec = pl.BlockSpec(memory_space=pl.ANY)
    return pl.pallas_call(
        body, in_specs=[anyspec] * n, out_specs=[anyspec] * n, out_shape=plan["out_shape"],
        scratch_shapes=plan["scratch"], name=name)(*arrs)


def _chip_scatter_plan(arrs):
    n = len(arrs)

    def phases(ins, outs, send_sems, recv_sems, loc_sems):
        x, y, c, chips = _mesh_place()
        mine = 2 * x + y
        local = [pltpu.make_async_copy(ins[k].at[mine], outs[k].at[mine], loc_sems.at[k]) for k in range(n)]
        remote = [pltpu.make_async_remote_copy(
            src_ref=ins[k].at[2 * px + py], dst_ref=outs[k].at[mine], send_sem=send_sems.at[k, j],
            recv_sem=recv_sems.at[k, j], device_id=(px, py, c), device_id_type=pl.DeviceIdType.MESH)
            for j, (px, py) in enumerate(chips) for k in range(n)]

        def start():
            for cp in local + remote:
                cp.start()

        def finish():
            for cp in remote:
                cp.wait()
            for cp in local:
                cp.wait()

        return start, (lambda: None), finish

    return dict(
        args=list(arrs), out_shape=[jax.ShapeDtypeStruct(a.shape, a.dtype) for a in arrs],
        scratch=[pltpu.SemaphoreType.DMA((n, 3)), pltpu.SemaphoreType.DMA((n, 3)), pltpu.SemaphoreType.DMA((n,))],
        phases=phases)


def _ssm_discretize(a_re, a_im, log_dt, b_re, b_im):
    dt = jnp.exp(log_dt)[:, None]
    mag = jnp.exp(a_re * dt)
    lr = mag * jnp.cos(a_im * dt)
    li = mag * jnp.sin(a_im * dt)
    den = a_re * a_re + a_im * a_im
    cr = ((lr - 1.0) * a_re + li * a_im) / den
    ci = (li * a_re - (lr - 1.0) * a_im) / den
    bbr = cr[..., None] * b_re - ci[..., None] * b_im
    bbi = cr[..., None] * b_im + ci[..., None] * b_re
    return lr, li, bbr, bbi


def _ssm_dense(lr, li, bbr, bbi, c_re, c_im):
    scan_ops = _scan_tables(lr.reshape(-1), li.reshape(-1)) + _scan_consts(SEQ_BLOCK)
    per_half = SSM_GROUPS // SSM_HALVES

    def halves(a, rows, cols):
        a = a.reshape(SSM_HALVES, per_half * rows, cols)
        tiled = jnp.tile(a, (1, 1, per_half))
        r = lax.broadcasted_iota(jnp.int32, tiled.shape, 1) // rows
        c = lax.broadcasted_iota(jnp.int32, tiled.shape, 2) // cols
        return jnp.where(r == c, tiled, 0.0)

    bblk = jnp.stack([halves(b.transpose(0, 2, 1), SSM_GROUP, SSM_STATE) for b in (bbr, bbi)]).astype(BF16)
    cblk = jnp.stack([halves(c.transpose(0, 2, 1), SSM_STATE, SSM_GROUP) for c in (c_re, c_im)]).astype(BF16)
    return scan_ops, bblk, cblk


def _ssm_extract(db, dc, sums):
    db = db.reshape(2, SSM_GROUPS, SSM_GROUP, SSM_STATE).transpose(0, 1, 3, 2)
    dc = dc.reshape(2, SSM_GROUPS, SSM_STATE, SSM_GROUP).transpose(0, 1, 3, 2)
    dlr = sums[0].reshape(SSM_GROUPS, SSM_STATE)
    dli = sums[1].reshape(SSM_GROUPS, SSM_STATE)
    return dlr, dli, db[0], db[1], dc[0], dc[1]


IN_SPLITS = (ATT_W, KV_W, KV_W, SSM_W, POOL_W, ATT_W, SSM_W, POOL_W, 3 * D_MODEL)


def _split_w_in(w):
    idx = [0]
    for s in IN_SPLITS:
        idx.append(idx[-1] + s)
    seg = [w[..., idx[k]:idx[k + 1]] for k in range(len(IN_SPLITS))]
    q, k, v, us, up, za, zs, zp, gl = seg
    return (jnp.concatenate([q, za, k, v], axis=-1), jnp.concatenate([us, zs], axis=-1),
            jnp.concatenate([up, zp], axis=-1), gl)


def _merge_w_in(da, ds, dp, dg):
    q, za, k, v = da[..., :ATT_W], da[..., ATT_W:2 * ATT_W], da[..., 2 * ATT_W:2 * ATT_W + KV_W], da[..., 2 * ATT_W + KV_W:]
    us, zs = ds[..., :SSM_W], ds[..., SSM_W:]
    up, zp = dp[..., :POOL_W], dp[..., POOL_W:]
    return jnp.concatenate([q, k, v, us, up, za, zs, zp, dg], axis=-1)


def _layer_fwd(x, lw, li, comm_attn=None, comm_ssm=None):
    tag = f"l{li}"
    h = _ln_fwd(x, lw["norm_g"], lw["shift"], lw["scale"], name=f"ln_fwd_{tag}")
    pa = _mm(h, lw["w_a"], tn=1280, name=f"proj_a_{tag}")
    ps = _mm(h, lw["w_s"], name=f"proj_s_{tag}")
    pp = _mm(h, lw["w_p"], name=f"proj_p_{tag}")
    pg = _mm(h, lw["w_g"], name=f"proj_g_{tag}")
    ya, from_attn = _attn_fwd(pa, lw["sinks"], name=f"attn_fwd_{tag}", comm=comm_attn)
    ys, chk, states, *from_ssm = _ssm_fwd(ps, lw["lam"], lw["bblk"], lw["cblk"], lw["ssm_d"], lw["w_glu"], lw["b_glu"],
                                          name=f"ssm_fwd_{tag}", comm=comm_ssm)
    yp = _pool_fwd(pp, lw["w_pool"], lw["pool_scale"], name=f"pool_fwd_{tag}")
    merged, ba, bs, bp = _merge_fwd(ya, ys, yp, lw["w_br_att"], lw["w_br_ssm"], lw["w_br_pool"], pg, name=f"merge_fwd_{tag}")
    x_new, out = _out_fwd(merged, lw["w_out"], x, lw["gate"], name=f"out_fwd_{tag}")
    saved = dict(x=x, h=h, pa=pa, ps=ps, pp=pp, pg=pg, ya=ya, ys=ys, yp=yp, chk=chk, states=states, merged=merged,
                 ba=ba, bs=bs, bp=bp, out=out)
    return x_new, saved, list(from_attn), list(from_ssm)


def _layer_bwd(dx, lw, sv, li, comm=None):
    tag = f"l{li}"
    dmo, dba, dbs, dbp, dpg, gate_sums = _merge_bwd(dx, sv["out"], lw["gate"], lw["w_out"], sv["pg"],
                                                    sv["ba"], sv["bs"], sv["bp"], name=f"merge_bwd_{tag}")
    g = {}
    g["w_out"] = _mm_tn(sv["merged"], dmo, out_dtype=BF16, name=f"dw_out_{tag}")
    dya = _mm(dba, lw["w_br_att"], nt=True, name=f"dy_att_{tag}")
    dys = _mm(dbs, lw["w_br_ssm"], nt=True, name=f"dy_ssm_{tag}")
    dyp = _mm(dbp, lw["w_br_pool"], nt=True, name=f"dy_pool_{tag}")
    g["w_br_att"] = _mm_tn(sv["ya"], dba, out_dtype=BF16, name=f"dw_br_att_{tag}")
    g["w_br_ssm"] = _mm_tn(sv["ys"], dbs, out_dtype=BF16, name=f"dw_br_ssm_{tag}")
    g["w_br_pool"] = _mm_tn(sv["yp"], dbp, out_dtype=BF16, name=f"dw_br_pool_{tag}")
    dpa, dsink = _attn_bwd(sv["pa"], lw["sinks"], dya, name=f"attn_bwd_{tag}")
    dps, db_dense, dc_dense, dwglu, ssm_sums, *exchanged = _ssm_bwd(
        sv["ps"], dys, sv["chk"], sv["states"], lw["lam"], lw["bblk"], lw["cblk"], lw["ssm_d"], lw["w_glu"], lw["b_glu"],
        name=f"ssm_bwd_{tag}", comm=comm)
    dpp, dwpool, pool_sums = _pool_bwd(sv["pp"], dyp, lw["w_pool"], lw["pool_scale"], name=f"pool_bwd_{tag}")
    dh = _mm_nt_sum([(dpa, lw["w_a"]), (dps, lw["w_s"]), (dpp, lw["w_p"]), (dpg, lw["w_g"])], name=f"dh_{tag}")
    h = sv["h"]
    g["w_in"] = _merge_w_in(_mm_tn(h, dpa, out_dtype=BF16, tn=1280, name=f"dw_a_{tag}"),
                            _mm_tn(h, dps, out_dtype=BF16, name=f"dw_s_{tag}"),
                            _mm_tn(h, dpp, out_dtype=BF16, name=f"dw_p_{tag}"),
                            _mm_tn(h, dpg, out_dtype=BF16, name=f"dw_g_{tag}"))
    dx_in, ln_sums = _ln_bwd(sv["x"], dh, dx, lw["norm_g"], lw["scale"], name=f"ln_bwd_{tag}")
    g["w_glu"] = dwglu.astype(BF16)
    g["dmod"] = jnp.concatenate([ln_sums[0], ln_sums[1], gate_sums[0]])
    g["norm_g"] = ln_sums[2]
    g["attn_sinks"] = dsink[:, 0]
    g["ssm_raw"] = _ssm_extract(db_dense, dc_dense, ssm_sums)
    g["ssm_d"] = ssm_sums[2, :SSM_W]
    g["b_glu"] = ssm_sums[3, :SSM_W]
    g["w_pool"] = dwpool
    g["pool_scale"] = pool_sums[0]
    return dx_in, g, exchanged


BIG_WEIGHTS = ("w_in", "w_glu", "w_br_att", "w_br_ssm", "w_br_pool", "w_out")
ROW_SHARDED = ("w_glu", "w_out")


def _full_weights(gathered):
    full = {}
    for k, g in zip(BIG_WEIGHTS, gathered):
        if k in ROW_SHARDED:
            full[k] = g.reshape(N_DEV * g.shape[1], g.shape[2])
        else:
            full[k] = g.transpose(1, 0, 2).reshape(g.shape[1], N_DEV * g.shape[2])
    return full


def _by_destination(grads):
    out = []
    for k in BIG_WEIGHTS:
        g = grads[k]
        if k in ROW_SHARDED:
            out.append(g.reshape(4, 2, g.shape[0] // N_DEV, g.shape[1]).transpose(1, 0, 2, 3))
        else:
            out.append(g.reshape(g.shape[0], 4, 2, g.shape[1] // N_DEV).transpose(2, 1, 0, 3))
    return out


def _prepare_layer(li, mod, norm_g, full, attn_sinks, disc, ssm_c_re, ssm_c_im, ssm_d, b_glu, w_pool, pool_scale):
    d = D_MODEL
    lr, li_, bbr, bbi = disc
    lam, bblk, cblk = _ssm_dense(lr[li], li_[li], bbr[li], bbi[li], ssm_c_re[li], ssm_c_im[li])
    w_a, w_s, w_p, w_g = _split_w_in(full["w_in"])
    return dict(
        norm_g=norm_g[li][None, :], shift=mod[li, :d][None, :], scale=mod[li, d:2 * d][None, :],
        gate=mod[li, 2 * d:][None, :], w_a=w_a, w_s=w_s, w_p=w_p, w_g=w_g,
        sinks=attn_sinks[li], lam=lam, bblk=bblk, cblk=cblk, ssm_d=ssm_d[li][None, :], w_glu=full["w_glu"],
        b_glu=b_glu[li][None, :], w_pool=w_pool[li].astype(BF16), pool_scale=pool_scale[li][None, :],
        w_br_att=full["w_br_att"], w_br_ssm=full["w_br_ssm"], w_br_pool=full["w_br_pool"], w_out=full["w_out"])


SMALL_ROWS = 64
SMALL_ORDER = ("norm_g", "attn_sinks", "ssm_d", "b_glu", "w_pool", "pool_scale", "dmod")


def _pack_small(loss, dfinal_g, layer_grads):
    parts = [jnp.broadcast_to(loss.reshape(1), (128,)), dfinal_g]
    for g in layer_grads:
        for k in SMALL_ORDER:
            v = g[k].reshape(-1)
            if v.shape[0] % 128:
                v = jnp.pad(v, (0, 128 - v.shape[0] % 128))
            parts.append(v)
        for v in g["ssm_raw"]:
            parts.append(v.reshape(-1))
    flat = jnp.concatenate(parts)
    return jnp.pad(flat, (0, (-flat.shape[0]) % (SMALL_ROWS * 128))).reshape(-1, 128)


def _unpack_small(flat, shapes):
    out, off = [], 0
    for s in shapes:
        n = int(math.prod(s))
        out.append(flat[off:off + n].reshape(s))
        off += n + (-n) % 128
    return out


def kernel(x, c, norm_g, w_ada, b_ada, w_in, attn_sinks, ssm_a_re, ssm_a_im, ssm_log_dt, ssm_b_re, ssm_b_im, ssm_c_re, ssm_c_im, ssm_d, w_glu, b_glu, w_pool, pool_scale, w_br_att, w_br_ssm, w_br_pool, w_out, final_g, loss_target, m_norm_g, m_w_ada, m_b_ada, m_w_in, m_attn_sinks, m_ssm_a_re, m_ssm_a_im, m_ssm_log_dt, m_ssm_b_re, m_ssm_b_im, m_ssm_c_re, m_ssm_c_im, m_ssm_d, m_w_glu, m_b_glu, m_w_pool, m_pool_scale, m_w_br_att, m_w_br_ssm, m_w_br_pool, m_w_out, m_final_g, v_norm_g, v_w_ada, v_b_ada, v_w_in, v_attn_sinks, v_ssm_a_re, v_ssm_a_im, v_ssm_log_dt, v_ssm_b_re, v_ssm_b_im, v_ssm_c_re, v_ssm_c_im, v_ssm_d, v_w_glu, v_b_glu, v_w_pool, v_pool_scale, v_w_br_att, v_w_br_ssm, v_w_br_pool, v_w_out, v_final_g):
    me = 4 * lax.axis_index("x") + 2 * lax.axis_index("y") + lax.axis_index("c")
    d = D_MODEL
    ada_w = 3 * d // N_DEV

    (c_all,) = _exchange([c.reshape(8, 128)], scatter=False, name="gather_c")
    c_act = jax.nn.silu(c_all.reshape(N_DEV, d))
    b_cols = lax.dynamic_slice(b_ada, (0, me * ada_w), (DEPTH, ada_w))
    mod_part = jnp.concatenate(
        [_mm(c_act, w_ada[li], name=f"ada_fwd_l{li}") + b_cols[li][None, :] for li in range(DEPTH)], axis=0)
    (mod_all,) = _exchange([mod_part], scatter=False, name="gather_mod")
    mod_all = mod_all.reshape(N_DEV, DEPTH, N_DEV, ada_w)
    mod_mine = lax.dynamic_index_in_dim(mod_all, me, axis=2, keepdims=False)
    mod_mine = mod_mine.transpose(1, 0, 2).reshape(DEPTH, 3 * d)

    sharded = dict(w_in=w_in, w_glu=w_glu, w_br_att=w_br_att, w_br_ssm=w_br_ssm, w_br_pool=w_br_pool, w_out=w_out)
    shards = lambda li: [sharded[k][li].astype(BF16) for k in BIG_WEIGHTS]
    disc, disc_vjp = jax.vjp(jax.vmap(_ssm_discretize), ssm_a_re, ssm_a_im, ssm_log_dt, ssm_b_re, ssm_b_im)
    layer = lambda li, gathered: _prepare_layer(li, mod_mine, norm_g, _full_weights(gathered), attn_sinks, disc,
                                                ssm_c_re, ssm_c_im, ssm_d, b_glu, w_pool, pool_scale)
    core = lax.axis_index("c").astype(jnp.int32).reshape(1)

    def chip_sums_of(grads_li, li):
        by_dest = _by_destination(grads_li)
        from_sibling = _sibling_swap(by_dest, name=f"grads_sibling_swap_l{li}")
        return [_pair_add(a.reshape(2, -1, a.shape[-1]), b.reshape(-1, b.shape[-1]), core,
                          name=f"grads_pair_add_l{li}_{k}").reshape(b.shape)
                for k, (a, b) in enumerate(zip(by_dest, from_sibling))]

    layers, saved, grads = [None] * DEPTH, [None] * DEPTH, [None] * DEPTH
    layers[0] = layer(0, _gather_two_level(shards(0), name="gather_weights_l0"))
    next_shards = shards(1)
    xs, saved[0], rest1, w_in1 = _layer_fwd(x[0], layers[0], 0, comm_attn=_gather_plan(next_shards[1:]),
                                            comm_ssm=_gather_plan(next_shards[:1]))
    layers[1] = layer(1, w_in1 + rest1)
    xs, saved[1], _, _ = _layer_fwd(xs, layers[1], 1)
    dx, fin_sums = _final_loss(xs, final_g[None, :], loss_target[0])
    loss_part = jnp.sum(fin_sums[1])
    dx, grads[1], _ = _layer_bwd(dx, layers[1], saved[1], 1)
    dx, grads[0], scattered1 = _layer_bwd(dx, layers[0], saved[0], 0, comm=_chip_scatter_plan(chip_sums_of(grads[1], 1)))
    scattered0 = _chip_scatter(chip_sums_of(grads[0], 0), name="grads_chip_scatter_l0")
    big = [jnp.stack([a, b], axis=1) for a, b in zip(scattered0, scattered1)]
    grad_x = dx[None]

    small = _pack_small(loss_part, fin_sums[0], grads)
    dmod_rows = jnp.concatenate([grads[li]["dmod"] for li in range(DEPTH)]).reshape(-1, 128)
    small_sum, dmod_gathered = _allreduce_small(small, dmod_rows, name="allreduce_small")
    out = {}

    def adam(name, w, g_parts, m, v):
        shp = w.shape
        r = int(math.prod(shp[:-1])) if len(shp) > 1 else 1
        w2, m2, v2 = (a.reshape(r, shp[-1]) for a in (w, m, v))
        g2 = g_parts.reshape(g_parts.shape[0], r, shp[-1])
        res = _adamw(w2, g2, m2, v2, name=f"adamw_{name}")
        out[name] = tuple(a.reshape(shp) for a in res)

    flat = small_sum.reshape(-1)
    shapes = [(128,), (d,)]
    for _ in range(DEPTH):
        shapes += [(d,), (N_HEADS,), (SSM_W,), (SSM_W,), (4, POOL_GW, POOL_GW), (POOL_W,), (3 * d,),
                   (SSM_GROUPS, SSM_STATE), (SSM_GROUPS, SSM_STATE), (SSM_GROUPS, SSM_STATE, SSM_GROUP),
                   (SSM_GROUPS, SSM_STATE, SSM_GROUP), (SSM_GROUPS, SSM_GROUP, SSM_STATE), (SSM_GROUPS, SSM_GROUP, SSM_STATE)]
    un = _unpack_small(flat, shapes)
    loss = un[0][0]
    g_final_g = un[1]
    per = 13
    gl = [un[2 + li * per: 2 + (li + 1) * per] for li in range(DEPTH)]
    st = lambda j: jnp.stack([gl[li][j] for li in range(DEPTH)])
    g_norm_g, g_sinks, g_ssm_d, g_b_glu, g_w_pool, g_pool_scale, g_b_ada = (st(j) for j in range(7))
    d_lr, d_li, d_bbr, d_bbi, g_c_re, g_c_im = (st(j) for j in range(7, 13))
    g_a_re, g_a_im, g_log_dt, g_b_re, g_b_im = disc_vjp((d_lr, d_li, d_bbr, d_bbi))

    dmod_all = lax.dynamic_slice(dmod_gathered.reshape(N_DEV, DEPTH, 3 * d), (0, 0, me * ada_w), (N_DEV, DEPTH, ada_w))
    dmod_all = dmod_all.transpose(1, 0, 2)
    g_w_ada = jnp.stack([_mm_tn(c_act, dmod_all[li], tm=d, tn=ada_w, tk=N_DEV, name=f"dw_ada_l{li}") for li in range(DEPTH)])

    adam("w_ada", w_ada, g_w_ada[None], m_w_ada, v_w_ada)
    adam("w_in", w_in, big[0], m_w_in, v_w_in)
    adam("w_glu", w_glu, big[1], m_w_glu, v_w_glu)
    adam("w_br_att", w_br_att, big[2], m_w_br_att, v_w_br_att)
    adam("w_br_ssm", w_br_ssm, big[3], m_w_br_ssm, v_w_br_ssm)
    adam("w_br_pool", w_br_pool, big[4], m_w_br_pool, v_w_br_pool)
    adam("w_out", w_out, big[5], m_w_out, v_w_out)

    small_names = ["norm_g", "b_ada", "attn_sinks", "ssm_a_re", "ssm_a_im", "ssm_log_dt", "ssm_b_re", "ssm_b_im",
                   "ssm_c_re", "ssm_c_im", "ssm_d", "b_glu", "w_pool", "pool_scale", "final_g"]
    small_w = [norm_g, b_ada, attn_sinks, ssm_a_re, ssm_a_im, ssm_log_dt, ssm_b_re, ssm_b_im, ssm_c_re, ssm_c_im,
               ssm_d, b_glu, w_pool, pool_scale, final_g]
    small_m = [m_norm_g, m_b_ada, m_attn_sinks, m_ssm_a_re, m_ssm_a_im, m_ssm_log_dt, m_ssm_b_re, m_ssm_b_im,
               m_ssm_c_re, m_ssm_c_im, m_ssm_d, m_b_glu, m_w_pool, m_pool_scale, m_final_g]
    small_v = [v_norm_g, v_b_ada, v_attn_sinks, v_ssm_a_re, v_ssm_a_im, v_ssm_log_dt, v_ssm_b_re, v_ssm_b_im,
               v_ssm_c_re, v_ssm_c_im, v_ssm_d, v_b_glu, v_w_pool, v_pool_scale, v_final_g]
    small_g = [g_norm_g, g_b_ada, g_sinks, g_a_re, g_a_im, g_log_dt, g_b_re, g_b_im, g_c_re, g_c_im,
               g_ssm_d, g_b_glu, g_w_pool, g_pool_scale, g_final_g]

    for nm, w, g, m, v in zip(small_names, small_w, small_g, small_m, small_v):
        adam(nm, w, g[None], m, v)

    order = ["norm_g", "w_ada", "b_ada", "w_in", "attn_sinks", "ssm_a_re", "ssm_a_im", "ssm_log_dt", "ssm_b_re",
             "ssm_b_im", "ssm_c_re", "ssm_c_im", "ssm_d", "w_glu", "b_glu", "w_pool", "pool_scale", "w_br_att",
             "w_br_ssm", "w_br_pool", "w_out", "final_g"]
    return (loss, grad_x, *[out[k][0] for k in order], *[out[k][1] for k in order],
            *[out[k][2] for k in order], *[out[k][3] for k in order])
```

```python
import functools
import math

import jax
import jax.numpy as jnp
from jax import lax
from jax.experimental import pallas as pl
from jax.experimental.pallas import tpu as pltpu

F32 = jnp.float32
BF16 = jnp.bfloat16

N_DEV = 8
D_MODEL = 1024
DEPTH = 2
CHUNK = 64
N_HEADS = 8
N_KV_HEADS = 2
HEAD_DIM = 64
Q_PER_KV = N_HEADS // N_KV_HEADS
WINDOW = 128
ATT_W = 512
KV_W = 128
SSM_W = 512
SSM_GROUP = 16
SSM_GROUPS = 32
SSM_STATE = 64
SSM_N = SSM_GROUPS * SSM_STATE
POOL_W = 512
POOL_WINDOWS = (2, 4, 8, 16)
POOL_GW = 128
POOL_HALO = 16
EPS = 1e-6
NEG_INF = -1e30
ADAM_LR = 0.001
ADAM_B1 = 0.9
ADAM_B2 = 0.999
ADAM_EPS = 1e-08
ADAM_WD = 0.01
ADAM_STEP = 10

SEQ_BLOCK = 256
VMEM_LIMIT = 56 * 1024 * 1024

NN = (((1,), (0,)), ((), ()))
NT = (((1,), (1,)), ((), ()))
TN = (((0,), (0,)), ((), ()))


def _dot(a, b, dims=NN):
    return lax.dot_general(a.astype(BF16), b.astype(BF16), dims, preferred_element_type=F32)


def _params(*sem):
    return pltpu.CompilerParams(dimension_semantics=sem, vmem_limit_bytes=VMEM_LIMIT)


def _sigmoid(x):
    return 1.0 / (1.0 + jnp.exp(-x))


def _silu_and_grad(z):
    s = _sigmoid(z)
    return z * s, s * (1.0 + z * (1.0 - s))


_GELU_K = math.sqrt(2.0 / math.pi)


def _gelu_and_grad(x):
    inner = _GELU_K * (x + 0.044715 * x * x * x)
    t = jnp.tanh(inner)
    val = 0.5 * x * (1.0 + t)
    grad = 0.5 * (1.0 + t) + 0.5 * x * (1.0 - t * t) * _GELU_K * (1.0 + 3.0 * 0.044715 * x * x)
    return val, grad


def _mm(a, b, *, nt=False, out_dtype=F32, tm=1024, tn=1024, name, comm=None):
    m, k = a.shape
    n = b.shape[0] if nt else b.shape[1]
    tm, tn = min(tm, m), min(tn, n)
    assert m % tm == 0 and n % tn == 0
    dims = NT if nt else NN
    grid = (m // tm, n // tn)
    c_args, c_in, c_out, c_shape, c_scratch = _comm_extra(comm)

    def body(a_ref, b_ref, o_ref):
        o_ref[...] = _dot(a_ref[...], b_ref[...], dims).astype(out_dtype)

    b_spec = pl.BlockSpec((tn, k), lambda i, j: (j, 0)) if nt else pl.BlockSpec((k, tn), lambda i, j: (0, j))
    res = pl.pallas_call(
        _with_comm(body, comm, 2, 1, grid, -1), grid=grid,
        in_specs=[pl.BlockSpec((tm, k), lambda i, j: (i, 0)), b_spec] + c_in,
        out_specs=[pl.BlockSpec((tm, tn), lambda i, j: (i, j))] + c_out,
        out_shape=[jax.ShapeDtypeStruct((m, n), out_dtype)] + c_shape,
        scratch_shapes=c_scratch,
        compiler_params=_params(*(("arbitrary",) * 2 if comm else ("parallel",) * 2)), name=name)(a, b, *c_args)
    return (res[0], list(res[1:])) if comm else res[0]


def _mm_nt_sum(pairs, *, out_dtype=F32, tm=512, tn=512, name, comm=None):
    m = pairs[0][0].shape[0]
    n = pairs[0][1].shape[0]
    np_ = len(pairs)
    grid = (m // tm, n // tn)
    c_args, c_in, c_out, c_shape, c_scratch = _comm_extra(comm)

    def body(*refs):
        o_ref = refs[-1]
        acc = _dot(refs[0][...], refs[1][...], NT)
        for p in range(1, np_):
            acc = acc + _dot(refs[2 * p][...], refs[2 * p + 1][...], NT)
        o_ref[...] = acc.astype(out_dtype)

    in_specs, args = [], []
    for a, b in pairs:
        in_specs.append(pl.BlockSpec((tm, a.shape[1]), lambda i, j: (i, 0)))
        in_specs.append(pl.BlockSpec((tn, b.shape[1]), lambda i, j: (j, 0)))
        args += [a, b]
    res = pl.pallas_call(
        _with_comm(body, comm, 2 * np_, 1, grid, -1), grid=grid, in_specs=in_specs + c_in,
        out_specs=[pl.BlockSpec((tm, tn), lambda i, j: (i, j))] + c_out,
        out_shape=[jax.ShapeDtypeStruct((m, n), out_dtype)] + c_shape,
        scratch_shapes=c_scratch,
        compiler_params=_params(*(("arbitrary",) * 2 if comm else ("parallel",) * 2)), name=name)(*args, *c_args)
    return (res[0], list(res[1:])) if comm else res[0]


def _mm_tn(a, b, *, out_dtype=F32, tm=1024, tn=1024, tk=1024, name):
    k, m = a.shape
    n = b.shape[1]
    assert m % min(tm, m) == 0 and n % min(tn, n) == 0 and k % min(tk, k) == 0
    tm, tn, tk = min(tm, m), min(tn, n), min(tk, k)
    nk = k // tk

    def body(a_ref, b_ref, o_ref, acc_ref):
        kk = pl.program_id(2)

        @pl.when(kk == 0)
        def _():
            acc_ref[...] = jnp.zeros_like(acc_ref)

        acc_ref[...] += _dot(a_ref[...], b_ref[...], TN)

        @pl.when(kk == nk - 1)
        def _():
            o_ref[...] = acc_ref[...].astype(out_dtype)

    return pl.pallas_call(
        body, grid=(m // tm, n // tn, nk),
        in_specs=[pl.BlockSpec((tk, tm), lambda i, j, kk: (kk, i)), pl.BlockSpec((tk, tn), lambda i, j, kk: (kk, j))],
        out_specs=pl.BlockSpec((tm, tn), lambda i, j, kk: (i, j)),
        out_shape=jax.ShapeDtypeStruct((m, n), out_dtype),
        scratch_shapes=[pltpu.VMEM((tm, tn), F32)],
        compiler_params=_params("parallel", "parallel", "arbitrary"), name=name)(a, b)


def _ln_fwd(x, g, shift, scale, *, name, tm=512):
    l, d = x.shape

    def body(x_ref, g_ref, sh_ref, sc_ref, h_ref):
        xv = x_ref[...]
        n = xv * lax.rsqrt(jnp.mean(xv * xv, axis=-1, keepdims=True) + EPS)
        h_ref[...] = ((n * g_ref[...]) * (1.0 + sc_ref[...]) + sh_ref[...]).astype(BF16)

    vec = pl.BlockSpec((1, d), lambda i: (0, 0))
    return pl.pallas_call(
        body, grid=(l // tm,),
        in_specs=[pl.BlockSpec((tm, d), lambda i: (i, 0)), vec, vec, vec],
        out_specs=pl.BlockSpec((tm, d), lambda i: (i, 0)),
        out_shape=jax.ShapeDtypeStruct((l, d), BF16),
        compiler_params=_params("parallel"), name=name)(x, g, shift, scale)


def _ln_bwd(x, dh, dres, g, scale, *, name, tm=512):
    l, d = x.shape

    def body(x_ref, dh_ref, dres_ref, g_ref, sc_ref, dx_ref, sums_ref):
        xv = x_ref[...]
        dhv = dh_ref[...]
        rstd = lax.rsqrt(jnp.mean(xv * xv, axis=-1, keepdims=True) + EPS)
        n = xv * rstd
        gv = g_ref[...]
        dr = dhv * (1.0 + sc_ref[...])
        dn = dr * gv
        dx_ref[...] = dres_ref[...] + rstd * (dn - n * jnp.mean(dn * n, axis=-1, keepdims=True))

        @pl.when(pl.program_id(0) == 0)
        def _():
            sums_ref[...] = jnp.zeros_like(sums_ref)

        sums_ref[0:1, :] += jnp.sum(dhv, axis=0, keepdims=True)
        sums_ref[1:2, :] += jnp.sum(dhv * (n * gv), axis=0, keepdims=True)
        sums_ref[2:3, :] += jnp.sum(dr * n, axis=0, keepdims=True)

    vec = pl.BlockSpec((1, d), lambda i: (0, 0))
    row = pl.BlockSpec((tm, d), lambda i: (i, 0))
    return pl.pallas_call(
        body, grid=(l // tm,),
        in_specs=[row, row, row, vec, vec],
        out_specs=[row, pl.BlockSpec((8, d), lambda i: (0, 0))],
        out_shape=[jax.ShapeDtypeStruct((l, d), F32), jax.ShapeDtypeStruct((8, d), F32)],
        compiler_params=_params("arbitrary"), name=name)(x, dh, dres, g, scale)


def _final_loss(x, g, target, *, tm=512):
    l, d = x.shape

    def body(x_ref, g_ref, t_ref, dx_ref, sums_ref):
        xv = x_ref[...]
        rstd = lax.rsqrt(jnp.mean(xv * xv, axis=-1, keepdims=True) + EPS)
        n = xv * rstd
        gv = g_ref[...]
        err = n * gv - t_ref[...]
        dy = err * (1.0 / d)
        dn = dy * gv
        dx_ref[...] = rstd * (dn - n * jnp.mean(dn * n, axis=-1, keepdims=True))

        @pl.when(pl.program_id(0) == 0)
        def _():
            sums_ref[...] = jnp.zeros_like(sums_ref)

        sums_ref[0:1, :] += jnp.sum(dy * n, axis=0, keepdims=True)
        sums_ref[1:2, :] += jnp.sum(err * err, axis=0, keepdims=True) * (0.5 / d)

    vec = pl.BlockSpec((1, d), lambda i: (0, 0))
    row = pl.BlockSpec((tm, d), lambda i: (i, 0))
    dx, sums = pl.pallas_call(
        body, grid=(l // tm,),
        in_specs=[row, vec, row],
        out_specs=[row, pl.BlockSpec((8, d), lambda i: (0, 0))],
        out_shape=[jax.ShapeDtypeStruct((l, d), F32), jax.ShapeDtypeStruct((8, d), F32)],
        compiler_params=_params("arbitrary"), name="final_loss")(x, g, target)
    return dx, sums


def _attn_geometry(i, t):
    nk = t + WINDOW
    qi = lax.broadcasted_iota(jnp.int32, (t, nk), 0)
    kj = lax.broadcasted_iota(jnp.int32, (t, nk), 1)
    dist = jnp.abs(qi + WINDOW - kj).astype(F32)
    qc = jnp.right_shift(qi, 6)
    kc = jnp.right_shift(kj, 6)
    valid = (kc >= qc) & (kc <= qc + WINDOW // CHUNK) & ((i > 0) | (kj >= WINDOW))
    return dist, valid


def _attn_head(q, k_all, v_all, sink, slope, dist, valid):
    s = _dot(q, k_all, NT) * (1.0 / math.sqrt(HEAD_DIM)) - slope * dist
    s = jnp.where(valid, s, NEG_INF)
    m = jnp.maximum(jnp.max(s, axis=-1, keepdims=True), sink)
    e = jnp.exp(s - m)
    es = jnp.exp(sink - m)
    inv = 1.0 / (jnp.sum(e, axis=-1, keepdims=True) + es)
    p = e * inv
    o = _dot(p, v_all, NN)
    return p, o, es * inv


def _attn_specs(t):
    cur = pl.BlockSpec((t, ATT_W * 2 + KV_W * 2), lambda i: (i, 0))
    halo_blocks = t // WINDOW
    prev = pl.BlockSpec((WINDOW, 2 * KV_W), lambda i: (jnp.maximum(i * halo_blocks - 1, 0), (2 * ATT_W) // (2 * KV_W)))
    return cur, prev


def _attn_fwd(pa, sinks, *, name, t=SEQ_BLOCK, comm=None):
    l = pa.shape[0]
    t = min(t, l)
    nb = l // t
    c_args, c_in, c_out, c_shape, c_scratch = _comm_extra(comm)

    def body(sink_ref, cur_ref, prev_ref, ya_ref):
        i = pl.program_id(0)
        dist, valid = _attn_geometry(i, t)
        for h in range(N_HEADS):
            kh = h // Q_PER_KV
            q = cur_ref[:, h * HEAD_DIM:(h + 1) * HEAD_DIM]
            z = cur_ref[:, ATT_W + h * HEAD_DIM:ATT_W + (h + 1) * HEAD_DIM].astype(F32)
            k_all = jnp.concatenate([prev_ref[:, kh * HEAD_DIM:(kh + 1) * HEAD_DIM],
                                     cur_ref[:, 2 * ATT_W + kh * HEAD_DIM:2 * ATT_W + (kh + 1) * HEAD_DIM]], axis=0)
            v_all = jnp.concatenate([prev_ref[:, KV_W + kh * HEAD_DIM:KV_W + (kh + 1) * HEAD_DIM],
                                     cur_ref[:, 2 * ATT_W + KV_W + kh * HEAD_DIM:2 * ATT_W + KV_W + (kh + 1) * HEAD_DIM]], axis=0)
            _, o, _ = _attn_head(q, k_all, v_all, sink_ref[h], 2.0 ** (-(h + 1)), dist, valid)
            sz, _ = _silu_and_grad(z)
            ya_ref[:, h * HEAD_DIM:(h + 1) * HEAD_DIM] = (o * sz).astype(BF16)

    cur, prev = _attn_specs(t)
    res = pl.pallas_call(
        _with_comm(body, comm, 3, 1, nb, nb - 1), grid=(nb,),
        in_specs=[pl.BlockSpec(memory_space=pltpu.SMEM), cur, prev] + c_in,
        out_specs=[pl.BlockSpec((t, ATT_W), lambda i: (i, 0))] + c_out,
        out_shape=[jax.ShapeDtypeStruct((l, ATT_W), BF16)] + c_shape,
        scratch_shapes=c_scratch,
        compiler_params=_params("arbitrary"), name=name)(sinks, pa, pa, *c_args)
    return res[0], res[1:]


def _attn_bwd(pa, sinks, dya, *, name, t=SEQ_BLOCK):
    l = pa.shape[0]
    t = min(t, l)
    nb = l // t
    scale = 1.0 / math.sqrt(HEAD_DIM)

    def body(sink_ref, cur_ref, prev_ref, dya_ref, dpa_ref, dsink_ref, carry_ref):
        n = pl.program_id(0)
        i = nb - 1 - n
        dist, valid = _attn_geometry(i, t)

        @pl.when(n == 0)
        def _():
            carry_ref[...] = jnp.zeros_like(carry_ref)
            dsink_ref[...] = jnp.zeros_like(dsink_ref)

        dk_acc = [jnp.zeros((t + WINDOW, HEAD_DIM), F32) for _ in range(N_KV_HEADS)]
        dv_acc = [jnp.zeros((t + WINDOW, HEAD_DIM), F32) for _ in range(N_KV_HEADS)]
        for h in range(N_HEADS):
            kh = h // Q_PER_KV
            q = cur_ref[:, h * HEAD_DIM:(h + 1) * HEAD_DIM]
            z = cur_ref[:, ATT_W + h * HEAD_DIM:ATT_W + (h + 1) * HEAD_DIM].astype(F32)
            k_all = jnp.concatenate([prev_ref[:, kh * HEAD_DIM:(kh + 1) * HEAD_DIM],
                                     cur_ref[:, 2 * ATT_W + kh * HEAD_DIM:2 * ATT_W + (kh + 1) * HEAD_DIM]], axis=0)
            v_all = jnp.concatenate([prev_ref[:, KV_W + kh * HEAD_DIM:KV_W + (kh + 1) * HEAD_DIM],
                                     cur_ref[:, 2 * ATT_W + KV_W + kh * HEAD_DIM:2 * ATT_W + KV_W + (kh + 1) * HEAD_DIM]], axis=0)
            p, o, p_sink = _attn_head(q, k_all, v_all, sink_ref[h], 2.0 ** (-(h + 1)), dist, valid)
            dy = dya_ref[:, h * HEAD_DIM:(h + 1) * HEAD_DIM]
            sz, dsz = _silu_and_grad(z)
            do = dy * sz
            dpa_ref[:, ATT_W + h * HEAD_DIM:ATT_W + (h + 1) * HEAD_DIM] = (dy * o * dsz).astype(BF16)
            delta = jnp.sum(do * o, axis=-1, keepdims=True)
            dp = _dot(do, v_all, NT)
            ds = p * (dp - delta)
            dpa_ref[:, h * HEAD_DIM:(h + 1) * HEAD_DIM] = (_dot(ds, k_all, NN) * scale).astype(BF16)
            dk_acc[kh] = dk_acc[kh] + _dot(ds, q, TN) * scale
            dv_acc[kh] = dv_acc[kh] + _dot(p, do, TN)
            dsink_ref[h:h + 1, :] += jnp.broadcast_to(-jnp.sum(p_sink * delta, axis=0, keepdims=True), (1, 128))

        for kh in range(N_KV_HEADS):
            for which, acc in ((0, dk_acc[kh]), (1, dv_acc[kh])):
                c0 = which * KV_W + kh * HEAD_DIM
                own = acc[WINDOW:, :]
                tail = own[t - WINDOW:, :] + carry_ref[:, c0:c0 + HEAD_DIM]
                dpa_ref[0:t - WINDOW, 2 * ATT_W + c0:2 * ATT_W + c0 + HEAD_DIM] = own[:t - WINDOW, :].astype(BF16)
                dpa_ref[t - WINDOW:t, 2 * ATT_W + c0:2 * ATT_W + c0 + HEAD_DIM] = tail.astype(BF16)
                carry_ref[:, c0:c0 + HEAD_DIM] = acc[:WINDOW, :]

    halo_blocks = t // WINDOW
    wpa = 2 * ATT_W + 2 * KV_W
    cur = pl.BlockSpec((t, wpa), lambda n: (nb - 1 - n, 0))
    prev = pl.BlockSpec((WINDOW, 2 * KV_W),
                        lambda n: (jnp.maximum((nb - 1 - n) * halo_blocks - 1, 0), (2 * ATT_W) // (2 * KV_W)))
    return pl.pallas_call(
        body, grid=(nb,),
        in_specs=[pl.BlockSpec(memory_space=pltpu.SMEM), cur, prev, pl.BlockSpec((t, ATT_W), lambda n: (nb - 1 - n, 0))],
        out_specs=[pl.BlockSpec((t, wpa), lambda n: (nb - 1 - n, 0)), pl.BlockSpec((8, 128), lambda n: (0, 0))],
        out_shape=[jax.ShapeDtypeStruct((l, wpa), BF16), jax.ShapeDtypeStruct((8, 128), F32)],
        scratch_shapes=[pltpu.VMEM((WINDOW, 2 * KV_W), F32)],
        compiler_params=_params("arbitrary"), name=name)(sinks, pa, pa, dya)


def _scan(xr, xi, lr, li, t, reverse):
    row = lax.broadcasted_iota(jnp.int32, (t, 1), 0)
    d = 1
    pr, pi = lr, li
    while d < t:
        if reverse:
            sr = jnp.where(row < t - d, pltpu.roll(xr, t - d, 0), 0.0)
            si = jnp.where(row < t - d, pltpu.roll(xi, t - d, 0), 0.0)
        else:
            sr = jnp.where(row >= d, pltpu.roll(xr, d, 0), 0.0)
            si = jnp.where(row >= d, pltpu.roll(xi, d, 0), 0.0)
        xr, xi = xr + pr * sr - pi * si, xi + pr * si + pi * sr
        pr, pi = pr * pr - pi * pi, 2.0 * pr * pi
        d *= 2
    return xr, xi


SCAN_SUB = 8


def _split_hi_lo(a):
    hi = a.astype(BF16)
    lo = (a - hi.astype(F32)).astype(BF16)
    return jnp.concatenate([hi, lo], axis=0)


def _scan_mxu(xr, xi, tab, lam3, lam8, tri, expand, cr, ci, t, reverse):
    ns = t // SCAN_SUB
    n = xr.shape[1]
    v3 = lambda a: a.reshape(ns, SCAN_SUB, n)
    x3r, x3i = v3(xr), v3(xi)
    br = (x3r * tab[0] - x3i * tab[1]).reshape(t, n)
    bi = (x3r * tab[1] + x3i * tab[0]).reshape(t, n)
    pm = jnp.dot(tri, jnp.concatenate([br, bi], axis=1).astype(BF16), preferred_element_type=F32)
    p3r, p3i = v3(pm[:t, :n]), v3(pm[:t, n:])
    slr = p3r * tab[2] - p3i * tab[3]
    sli = p3r * tab[3] + p3i * tab[2]
    totr, toti = pm[t:, :n], pm[t:, n:]
    l3r, l3i = lam3
    l8r, l8i = lam8
    row = lax.broadcasted_iota(jnp.int32, (ns, 1), 0)
    edge = row == (ns - 1 if reverse else 0)
    er = totr * l3r - toti * l3i + jnp.where(edge, l8r * cr - l8i * ci, 0.0)
    ei = totr * l3i + toti * l3r + jnp.where(edge, l8r * ci + l8i * cr, 0.0)
    er, ei = _scan(er, ei, l8r, l8i, ns, reverse)
    shift = ns - 1 if reverse else 1
    nbr = jnp.where(edge, cr, pltpu.roll(er, shift, 0))
    nbi = jnp.where(edge, ci, pltpu.roll(ei, shift, 0))
    ex = jnp.dot(expand, _split_hi_lo(jnp.concatenate([nbr, nbi], axis=1)), preferred_element_type=F32)
    e3r, e3i = v3(ex[:, :n]), v3(ex[:, n:])
    sr = (slr + e3r * tab[4] - e3i * tab[5]).reshape(t, n)
    si = (sli + e3r * tab[5] + e3i * tab[4]).reshape(t, n)
    out = 0 if reverse else ns - 1
    return sr, si, er[out:out + 1, :], ei[out:out + 1, :]


def _scan_consts(t):
    import numpy as np
    ns = t // SCAN_SUB
    r = np.arange(t)
    same = (r[:, None] // SCAN_SUB) == (r[None, :] // SCAN_SUB)
    sums = (np.arange(ns)[:, None] == (r[None, :] // SCAN_SUB))
    tri = []
    for keep in (r[None, :] <= r[:, None], r[None, :] >= r[:, None]):
        tri.append(np.concatenate([same & keep, sums], axis=0).astype(np.float32))
    ex = ((r[:, None] // SCAN_SUB) == np.arange(ns)[None, :]).astype(np.float32)
    return jnp.asarray(np.stack(tri), BF16), jnp.asarray(np.concatenate([ex, ex], axis=1), BF16)


def _scan_tables(lr, li):
    den = lr * lr + li * li
    ir, ii = lr / den, -li / den
    mul = lambda a, b: (a[0] * b[0] - a[1] * b[1], a[0] * b[1] + a[1] * b[0])
    pw = {0: (jnp.ones_like(lr), jnp.zeros_like(lr))}
    for e in range(1, 9):
        pw[e] = mul(pw[e - 1], (lr, li))
    for e in range(-1, -5, -1):
        pw[e] = mul(pw[e + 1], (ir, ii))
    stack = lambda es, sign: (jnp.stack([pw[e][0] for e in es]), sign * jnp.stack([pw[e][1] for e in es]))
    j = range(SCAN_SUB)
    parts = [stack([4 - k for k in j], 1.0), stack([k - 4 for k in j], 1.0), stack([k + 1 for k in j], 1.0),
             stack([k - 3 for k in j], -1.0), stack([3 - k for k in j], -1.0), stack([8 - k for k in j], -1.0)]
    tabs = jnp.stack([a for pair in parts for a in pair])
    lam = jnp.zeros((8, lr.shape[0]), F32)
    for k, v in enumerate((lr, li, pw[3][0], pw[3][1], pw[8][0], pw[8][1])):
        lam = lam.at[k].set(v)
    return lam, tabs


SSM_HALVES = 2
SSM_HW = SSM_W // SSM_HALVES
SSM_HN = SSM_N // SSM_HALVES


def _bd_nn(x, w):
    a = w.shape[1]
    return jnp.concatenate([_dot(x[:, h * a:(h + 1) * a], w[h]) for h in range(SSM_HALVES)], axis=1)


def _bd_nt(x, w):
    b = w.shape[2]
    return jnp.concatenate([_dot(x[:, h * b:(h + 1) * b], w[h], NT) for h in range(SSM_HALVES)], axis=1)


def _bd_tn(x, y):
    a, b = x.shape[1] // SSM_HALVES, y.shape[1] // SSM_HALVES
    return jnp.stack([_dot(x[:, h * a:(h + 1) * a], y[:, h * b:(h + 1) * b], TN) for h in range(SSM_HALVES)])


def _ssm_states(u, s0r, s0i, lam_ref, tab_ref, tri_ref, ex_ref, bre, bim, t):
    tab = tuple(tab_ref[k] for k in range(6))
    return _scan_mxu(_bd_nn(u, bre), _bd_nn(u, bim), tab, (lam_ref[2:3, :], lam_ref[3:4, :]),
                     (lam_ref[4:5, :], lam_ref[5:6, :]), tri_ref[0], ex_ref[...], s0r, s0i, t, False)


def _ssm_head(u, z, xr, xi, cre, cim, dskip, wglu, bglu):
    y = _bd_nn(xr, cre) - _bd_nn(xi, cim) + dskip * u
    y2, dgelu = _gelu_and_grad(y)
    gate = _sigmoid(_dot(y2, wglu) + bglu)
    y3 = y2 * gate
    return y2, dgelu, gate, y3


def _with_comm(body, comm, n_in, n_out, grid, mid_step):
    if comm is None:
        return body
    nc = len(comm["args"])
    n_sem = len(comm["scratch"])
    grid = (grid,) if isinstance(grid, int) else tuple(grid)
    total = math.prod(grid)

    def hosted(*refs):
        ins, cin = refs[:n_in], refs[n_in:n_in + nc]
        outs, cout = refs[n_in + nc:n_in + nc + n_out], refs[n_in + nc + n_out:n_in + 2 * nc + n_out]
        rest = refs[n_in + 2 * nc + n_out:]
        scratch, csem = rest[:len(rest) - n_sem], rest[len(rest) - n_sem:]
        start, forward, finish = comm["phases"](cin, cout, *csem)
        step = pl.program_id(0)
        for axis in range(1, len(grid)):
            step = step * grid[axis] + pl.program_id(axis)
        pl.when(step == 0)(start)
        pl.when(step == (mid_step if mid_step >= 0 else total + mid_step))(forward)
        body(*ins, *outs, *scratch)
        pl.when(step == total - 1)(finish)

    return hosted


def _comm_extra(comm):
    if comm is None:
        return [], [], [], [], []
    anyspec = pl.BlockSpec(memory_space=pl.ANY)
    nc = len(comm["args"])
    return comm["args"], [anyspec] * nc, [anyspec] * nc, comm["out_shape"], comm["scratch"]


def _ssm_fwd(ps, scan_ops, bblk, cblk, dskip, wglu, bglu, *, name, t=SEQ_BLOCK, comm=None):
    l = ps.shape[0]
    assert l % t == 0
    nb = l // t
    ns = t // SCAN_SUB
    c_args, c_in, c_out, c_shape, c_scratch = _comm_extra(comm)

    def body(ps_ref, lam_ref, tab_ref, tri_ref, ex_ref, b_ref, c_ref, d_ref, w_ref, bg_ref, ys_ref, chk_ref, xs_ref,
             st_ref):
        @pl.when(pl.program_id(0) == 0)
        def _():
            st_ref[...] = jnp.zeros_like(st_ref)

        chk_ref[...] = jnp.broadcast_to(st_ref[...], chk_ref.shape)
        u = ps_ref[:, :SSM_W].astype(F32)
        z = ps_ref[:, SSM_W:].astype(F32)
        xr, xi, er, ei = _ssm_states(u, st_ref[:, :SSM_N], st_ref[:, SSM_N:], lam_ref, tab_ref, tri_ref, ex_ref,
                                     b_ref[0], b_ref[1], t)
        st_ref[:, :SSM_N] = er
        st_ref[:, SSM_N:] = ei
        xr, xi = xr.astype(BF16), xi.astype(BF16)
        xs_ref[:, :SSM_N] = xr
        xs_ref[:, SSM_N:] = xi
        _, _, _, y3 = _ssm_head(u, z, xr, xi, c_ref[0], c_ref[1], d_ref[...], w_ref[...], bg_ref[...])
        sz, _ = _silu_and_grad(z)
        ys_ref[...] = (y3 * sz).astype(BF16)

    full = lambda shape: pl.BlockSpec(shape, lambda i: (0,) * len(shape))
    return pl.pallas_call(
        _with_comm(body, comm, 10, 3, nb, nb - 1), grid=(nb,),
        in_specs=[pl.BlockSpec((t, 2 * SSM_W), lambda i: (i, 0)), full((8, SSM_N)), full((12, SCAN_SUB, SSM_N)),
                  full((2, t + ns, t)), full((t, 2 * ns)), full((2, SSM_HALVES, SSM_HW, SSM_HN)),
                  full((2, SSM_HALVES, SSM_HN, SSM_HW)), full((1, SSM_W)), full((SSM_W, SSM_W)), full((1, SSM_W))] + c_in,
        out_specs=[pl.BlockSpec((t, SSM_W), lambda i: (i, 0)), pl.BlockSpec((8, 2 * SSM_N), lambda i: (i, 0)),
                   pl.BlockSpec((t, 2 * SSM_N), lambda i: (i, 0))] + c_out,
        out_shape=[jax.ShapeDtypeStruct((l, SSM_W), BF16), jax.ShapeDtypeStruct((nb * 8, 2 * SSM_N), F32),
                   jax.ShapeDtypeStruct((l, 2 * SSM_N), BF16)] + c_shape,
        scratch_shapes=[pltpu.VMEM((1, 2 * SSM_N), F32)] + c_scratch,
        compiler_params=_params("arbitrary"), name=name)(ps, *scan_ops, bblk, cblk, dskip, wglu, bglu, *c_args)


def _ssm_bwd(ps, dys, chk, states, scan_ops, bblk, cblk, dskip, wglu, bglu, *, name, t=SEQ_BLOCK, comm=None):
    l = ps.shape[0]
    assert l % t == 0
    nb = l // t
    ns = t // SCAN_SUB
    c_args, c_in, c_out, c_shape, c_scratch = _comm_extra(comm)

    def body(ps_ref, dys_ref, chk_ref, xs_ref, lam_ref, tab_ref, tri_ref, ex_ref, b_ref, c_ref, d_ref, w_ref, bg_ref,
             dps_ref, db_ref, dc_ref, dw_acc, sums_acc, gc_ref, db_acc, dc_acc):
        n = pl.program_id(0)

        @pl.when(n == 0)
        def _():
            gc_ref[...] = jnp.zeros_like(gc_ref)
            db_acc[...] = jnp.zeros_like(db_acc)
            dc_acc[...] = jnp.zeros_like(dc_acc)
            dw_acc[...] = jnp.zeros_like(dw_acc)
            sums_acc[...] = jnp.zeros_like(sums_acc)

        row = lax.broadcasted_iota(jnp.int32, (t, 1), 0)
        u = ps_ref[:, :SSM_W].astype(F32)
        z = ps_ref[:, SSM_W:].astype(F32)
        s0r, s0i = chk_ref[0:1, :SSM_N], chk_ref[0:1, SSM_N:]
        xr, xi = xs_ref[:, :SSM_N], xs_ref[:, SSM_N:]
        dskip = d_ref[...]
        y2, dgelu, gate, y3 = _ssm_head(u, z, xr, xi, c_ref[0], c_ref[1], dskip, w_ref[...], bg_ref[...])
        sz, dsz = _silu_and_grad(z)
        dys_v = dys_ref[...]
        dps_ref[:, SSM_W:] = (dys_v * y3 * dsz).astype(BF16)
        dy3 = dys_v * sz
        da = dy3 * y2 * gate * (1.0 - gate)
        dy2 = dy3 * gate + _dot(da, w_ref[...], NT)
        dw_acc[...] += _dot(y2, da, TN)
        dy = dy2 * dgelu
        sums_acc[2:3, :SSM_W] += jnp.sum(dy * u, axis=0, keepdims=True)
        sums_acc[3:4, :SSM_W] += jnp.sum(da, axis=0, keepdims=True)
        dc_acc[0] += _bd_tn(xr, dy)
        dc_acc[1] += -_bd_tn(xi, dy)
        rev_tab = tuple(tab_ref[k] for k in range(6, 12))
        gr, gi, gcr, gci = _scan_mxu(
            _bd_nt(dy, c_ref[0]), -_bd_nt(dy, c_ref[1]), rev_tab, (lam_ref[2:3, :], -lam_ref[3:4, :]),
            (lam_ref[4:5, :], -lam_ref[5:6, :]), tri_ref[1], ex_ref[...], gc_ref[:, :SSM_N], gc_ref[:, SSM_N:], t, True)
        gc_ref[:, :SSM_N] = gcr
        gc_ref[:, SSM_N:] = gci
        db_acc[0] += _bd_tn(u, gr)
        db_acc[1] += _bd_tn(u, gi)
        du = dskip * dy + _bd_nt(gr, b_ref[0]) + _bd_nt(gi, b_ref[1])
        dps_ref[:, :SSM_W] = du.astype(BF16)
        spr = jnp.where(row == 0, s0r, pltpu.roll(xr.astype(F32), 1, 0))
        spi = jnp.where(row == 0, s0i, pltpu.roll(xi.astype(F32), 1, 0))
        sums_acc[0:1, :] += jnp.sum(gr * spr + gi * spi, axis=0, keepdims=True)
        sums_acc[1:2, :] += jnp.sum(gi * spr - gr * spi, axis=0, keepdims=True)

        @pl.when(n == nb - 1)
        def _():
            per_half = SSM_GROUPS // SSM_HALVES
            for k in range(2):
                for g in range(SSM_GROUPS):
                    h, gl = divmod(g, per_half)
                    c0, p0 = gl * SSM_GROUP, gl * SSM_STATE
                    db_ref[k, g * SSM_GROUP:(g + 1) * SSM_GROUP, :] = db_acc[k, h, c0:c0 + SSM_GROUP, p0:p0 + SSM_STATE]
                    dc_ref[k, g * SSM_STATE:(g + 1) * SSM_STATE, :] = dc_acc[k, h, p0:p0 + SSM_STATE, c0:c0 + SSM_GROUP]

    full = lambda shape: pl.BlockSpec(shape, lambda n: (0,) * len(shape))
    return pl.pallas_call(
        _with_comm(body, comm, 13, 5, nb, 0), grid=(nb,),
        in_specs=[pl.BlockSpec((t, 2 * SSM_W), lambda n: (nb - 1 - n, 0)),
                  pl.BlockSpec((t, SSM_W), lambda n: (nb - 1 - n, 0)),
                  pl.BlockSpec((8, 2 * SSM_N), lambda n: (nb - 1 - n, 0)),
                  pl.BlockSpec((t, 2 * SSM_N), lambda n: (nb - 1 - n, 0)),
                  full((8, SSM_N)), full((12, SCAN_SUB, SSM_N)), full((2, t + ns, t)), full((t, 2 * ns)),
                  full((2, SSM_HALVES, SSM_HW, SSM_HN)), full((2, SSM_HALVES, SSM_HN, SSM_HW)), full((1, SSM_W)),
                  full((SSM_W, SSM_W)), full((1, SSM_W))] + c_in,
        out_specs=[pl.BlockSpec((t, 2 * SSM_W), lambda n: (nb - 1 - n, 0)), full((2, SSM_W, SSM_STATE)),
                   full((2, SSM_N, SSM_GROUP)), full((SSM_W, SSM_W)), full((8, SSM_N))] + c_out,
        out_shape=[jax.ShapeDtypeStruct((l, 2 * SSM_W), BF16),
                   jax.ShapeDtypeStruct((2, SSM_W, SSM_STATE), F32),
                   jax.ShapeDtypeStruct((2, SSM_N, SSM_GROUP), F32),
                   jax.ShapeDtypeStruct((SSM_W, SSM_W), F32),
                   jax.ShapeDtypeStruct((8, SSM_N), F32)] + c_shape,
        scratch_shapes=[pltpu.VMEM((1, 2 * SSM_N), F32), pltpu.VMEM((2, SSM_HALVES, SSM_HW, SSM_HN), F32),
                        pltpu.VMEM((2, SSM_HALVES, SSM_HN, SSM_HW), F32)] + c_scratch,
        compiler_params=_params("arbitrary"), name=name)(ps, dys, chk, states, *scan_ops, bblk, cblk, dskip, wglu, bglu,
                                                         *c_args)


def _pool_count(i, t):
    pos = lax.broadcasted_iota(jnp.int32, (t, POOL_W), 0) + i * t + 1
    col = lax.broadcasted_iota(jnp.int32, (t, POOL_W), 1)
    win = jnp.where(col < POOL_GW, 2, jnp.where(col < 2 * POOL_GW, 4, jnp.where(col < 3 * POOL_GW, 8, 16)))
    return 1.0 / jnp.minimum(pos, win).astype(F32), col


def _window_sums(ext, n_rows, forward):
    col = lax.broadcasted_iota(jnp.int32, ext.shape, 1)
    sh = (lambda a, d: pltpu.roll(a, d, 0)) if forward else (lambda a, d: pltpu.roll(a, n_rows - d, 0))
    a2 = ext + sh(ext, 1)
    a4 = a2 + sh(a2, 2)
    a8 = a4 + sh(a4, 4)
    a16 = a8 + sh(a8, 8)
    return jnp.where(col < POOL_GW, a2, jnp.where(col < 2 * POOL_GW, a4, jnp.where(col < 3 * POOL_GW, a8, a16)))


def _pool_mix(pooled, wp_ref):
    return jnp.concatenate([_dot(pooled[:, g * POOL_GW:(g + 1) * POOL_GW], wp_ref[g]) for g in range(4)], axis=1)


def _pool_pooled(i, cur_u, prev_u, t):
    prev = jnp.where(i > 0, prev_u, 0.0)
    ext = jnp.concatenate([prev, cur_u], axis=0)
    inv_cnt, _ = _pool_count(i, t)
    return _window_sums(ext, t + POOL_HALO, True)[POOL_HALO:, :] * inv_cnt - cur_u


def _pool_fwd(pp, wpool, pscale, *, name, t=SEQ_BLOCK):
    l = pp.shape[0]
    t = min(t, l)

    def body(cur_ref, prev_ref, wp_ref, sc_ref, yp_ref):
        i = pl.program_id(0)
        pooled = _pool_pooled(i, cur_ref[:, :POOL_W].astype(F32), prev_ref[...].astype(F32), t)
        lin = _pool_mix(pooled, wp_ref)
        sz, _ = _silu_and_grad(cur_ref[:, POOL_W:].astype(F32))
        yp_ref[...] = (lin * sc_ref[...] * sz).astype(BF16)

    hb = t // POOL_HALO
    return pl.pallas_call(
        body, grid=(l // t,),
        in_specs=[pl.BlockSpec((t, 2 * POOL_W), lambda i: (i, 0)),
                  pl.BlockSpec((POOL_HALO, POOL_W), lambda i: (jnp.maximum(i * hb - 1, 0), 0)),
                  pl.BlockSpec((4, POOL_GW, POOL_GW), lambda i: (0, 0, 0)),
                  pl.BlockSpec((1, POOL_W), lambda i: (0, 0))],
        out_specs=pl.BlockSpec((t, POOL_W), lambda i: (i, 0)),
        out_shape=jax.ShapeDtypeStruct((l, POOL_W), BF16),
        compiler_params=_params("parallel"), name=name)(pp, pp, wpool, pscale)


def _pool_bwd(pp, dyp, wpool, pscale, *, name, t=SEQ_BLOCK):
    l = pp.shape[0]
    t = min(t, l)
    nb = l // t

    def body(cur_ref, prev_ref, dyp_ref, wp_ref, sc_ref, dpp_ref, dwp_ref, sums_ref, carry_ref):
        n = pl.program_id(0)
        i = nb - 1 - n

        @pl.when(n == 0)
        def _():
            carry_ref[...] = jnp.zeros_like(carry_ref)
            dwp_ref[...] = jnp.zeros_like(dwp_ref)
            sums_ref[...] = jnp.zeros_like(sums_ref)

        cur_u = cur_ref[:, :POOL_W].astype(F32)
        pooled = _pool_pooled(i, cur_u, prev_ref[...].astype(F32), t)
        lin = _pool_mix(pooled, wp_ref)
        sz, dsz = _silu_and_grad(cur_ref[:, POOL_W:].astype(F32))
        dyp_v = dyp_ref[...]
        scale = sc_ref[...]
        dpp_ref[:, POOL_W:] = (dyp_v * lin * scale * dsz).astype(BF16)
        dpre = dyp_v * sz
        sums_ref[0:1, :] += jnp.sum(dpre * lin, axis=0, keepdims=True)
        dlin = dpre * scale
        dpooled = []
        for g in range(4):
            dl = dlin[:, g * POOL_GW:(g + 1) * POOL_GW]
            dwp_ref[g] += _dot(pooled[:, g * POOL_GW:(g + 1) * POOL_GW], dl, TN)
            dpooled.append(_dot(dl, wp_ref[g], NT))
        dpooled = jnp.concatenate(dpooled, axis=1)
        inv_cnt, _ = _pool_count(i, t)
        dq = dpooled * inv_cnt
        ext = jnp.concatenate([dq, carry_ref[...]], axis=0)
        du = _window_sums(ext, t + POOL_HALO, False)[:t, :] - dpooled
        dpp_ref[:, :POOL_W] = du.astype(BF16)
        carry_ref[...] = dq[:POOL_HALO, :]

    hb = t // POOL_HALO
    return pl.pallas_call(
        body, grid=(nb,),
        in_specs=[pl.BlockSpec((t, 2 * POOL_W), lambda n: (nb - 1 - n, 0)),
                  pl.BlockSpec((POOL_HALO, POOL_W), lambda n: (jnp.maximum((nb - 1 - n) * hb - 1, 0), 0)),
                  pl.BlockSpec((t, POOL_W), lambda n: (nb - 1 - n, 0)),
                  pl.BlockSpec((4, POOL_GW, POOL_GW), lambda n: (0, 0, 0)),
                  pl.BlockSpec((1, POOL_W), lambda n: (0, 0))],
        out_specs=[pl.BlockSpec((t, 2 * POOL_W), lambda n: (nb - 1 - n, 0)),
                   pl.BlockSpec((4, POOL_GW, POOL_GW), lambda n: (0, 0, 0)),
                   pl.BlockSpec((8, POOL_W), lambda n: (0, 0))],
        out_shape=[jax.ShapeDtypeStruct((l, 2 * POOL_W), BF16), jax.ShapeDtypeStruct((4, POOL_GW, POOL_GW), F32),
                   jax.ShapeDtypeStruct((8, POOL_W), F32)],
        scratch_shapes=[pltpu.VMEM((POOL_HALO, POOL_W), F32)],
        compiler_params=_params("arbitrary"), name=name)(pp, pp, dyp, wpool, pscale)


def _merge_fwd(ya, ys, yp, wa, ws, wp, pg, *, name, tm=512):
    l = ya.shape[0]
    tm = min(tm, l)
    d = D_MODEL

    def body(ya_ref, ys_ref, yp_ref, wa_ref, ws_ref, wp_ref, pg_ref, mg_ref, ba_ref, bs_ref, bp_ref):
        acc = None
        for k, (y_ref, w_ref, b_ref) in enumerate(((ya_ref, wa_ref, ba_ref), (ys_ref, ws_ref, bs_ref),
                                                   (yp_ref, wp_ref, bp_ref))):
            br = _dot(y_ref[...], w_ref[...])
            b_ref[...] = br.astype(BF16)
            term = _sigmoid(pg_ref[:, k * d:(k + 1) * d].astype(F32)) * br
            acc = term if acc is None else acc + term
        mg_ref[...] = acc.astype(BF16)

    rowy = pl.BlockSpec((tm, ATT_W), lambda i: (i, 0))
    wsp = pl.BlockSpec((ATT_W, d), lambda i: (0, 0))
    rowd = pl.BlockSpec((tm, d), lambda i: (i, 0))
    return pl.pallas_call(
        body, grid=(l // tm,),
        in_specs=[rowy, rowy, rowy, wsp, wsp, wsp, pl.BlockSpec((tm, 3 * d), lambda i: (i, 0))],
        out_specs=[rowd, rowd, rowd, rowd],
        out_shape=[jax.ShapeDtypeStruct((l, d), BF16)] * 4,
        compiler_params=_params("parallel"), name=name)(ya, ys, yp, wa, ws, wp, pg)


def _out_fwd(merged, wout, x, gate, *, name, tm=512):
    l, d = x.shape
    tm = min(tm, l)

    def body(m_ref, w_ref, x_ref, g_ref, xn_ref, out_ref):
        out = _dot(m_ref[...], w_ref[...])
        out_ref[...] = out.astype(BF16)
        xn_ref[...] = x_ref[...] + g_ref[...] * out

    row = pl.BlockSpec((tm, d), lambda i: (i, 0))
    return pl.pallas_call(
        body, grid=(l // tm,),
        in_specs=[row, pl.BlockSpec((d, d), lambda i: (0, 0)), row, pl.BlockSpec((1, d), lambda i: (0, 0))],
        out_specs=[row, row],
        out_shape=[jax.ShapeDtypeStruct((l, d), F32), jax.ShapeDtypeStruct((l, d), BF16)],
        compiler_params=_params("parallel"), name=name)(merged, wout, x, gate)


def _merge_bwd(dx, out, gate, wout, pg, ba, bs, bp, *, name, tm=512):
    l, d = dx.shape
    tm = min(tm, l)

    def body(dx_ref, out_ref, g_ref, w_ref, pg_ref, ba_ref, bs_ref, bp_ref,
             dmo_ref, dba_ref, dbs_ref, dbp_ref, dpg_ref, sums_ref):
        @pl.when(pl.program_id(0) == 0)
        def _():
            sums_ref[...] = jnp.zeros_like(sums_ref)

        dxv = dx_ref[...]
        sums_ref[0:1, :] += jnp.sum(dxv * out_ref[...].astype(F32), axis=0, keepdims=True)
        dmo = (dxv * g_ref[...]).astype(BF16)
        dmo_ref[...] = dmo
        dmerged = _dot(dmo, w_ref[...], NT)
        for k, (b_ref, db_ref) in enumerate(((ba_ref, dba_ref), (bs_ref, dbs_ref), (bp_ref, dbp_ref))):
            gk = _sigmoid(pg_ref[:, k * d:(k + 1) * d].astype(F32))
            db_ref[...] = (dmerged * gk).astype(BF16)
            dpg_ref[:, k * d:(k + 1) * d] = (dmerged * b_ref[...].astype(F32) * gk * (1.0 - gk)).astype(BF16)

    row = pl.BlockSpec((tm, d), lambda i: (i, 0))
    wide = pl.BlockSpec((tm, 3 * d), lambda i: (i, 0))
    return pl.pallas_call(
        body, grid=(l // tm,),
        in_specs=[row, row, pl.BlockSpec((1, d), lambda i: (0, 0)), pl.BlockSpec((d, d), lambda i: (0, 0)),
                  wide, row, row, row],
        out_specs=[row, row, row, row, wide, pl.BlockSpec((8, d), lambda i: (0, 0))],
        out_shape=[jax.ShapeDtypeStruct((l, d), BF16)] * 4 + [jax.ShapeDtypeStruct((l, 3 * d), BF16),
                                                             jax.ShapeDtypeStruct((8, d), F32)],
        compiler_params=_params("arbitrary"), name=name)(dx, out, gate, wout, pg, ba, bs, bp)


def _adamw(w, g, m, v, *, name, tr=256):
    r, c = w.shape
    p = g.shape[0]
    tr = min(tr, r)
    assert r % tr == 0
    c1 = 1.0 / (1.0 - ADAM_B1 ** ADAM_STEP)
    c2 = 1.0 / (1.0 - ADAM_B2 ** ADAM_STEP)

    def body(w_ref, g_ref, m_ref, v_ref, go_ref, d_ref, mo_ref, vo_ref):
        gv = g_ref[0].astype(F32)
        for k in range(1, p):
            gv = gv + g_ref[k].astype(F32)
        go_ref[...] = gv
        mn = ADAM_B1 * m_ref[...] + (1.0 - ADAM_B1) * gv
        vn = ADAM_B2 * v_ref[...] + (1.0 - ADAM_B2) * (gv * gv)
        mo_ref[...] = mn
        vo_ref[...] = vn
        d_ref[...] = -ADAM_LR * ((mn * c1) / (jnp.sqrt(vn * c2) + ADAM_EPS) + ADAM_WD * w_ref[...])

    row = pl.BlockSpec((tr, c), lambda i: (i, 0))
    return pl.pallas_call(
        body, grid=(r // tr,),
        in_specs=[row, pl.BlockSpec((p, tr, c), lambda i: (0, i, 0)), row, row],
        out_specs=[row] * 4,
        out_shape=[jax.ShapeDtypeStruct((r, c), F32)] * 4,
        compiler_params=_params("parallel"), name=name)(w, g, m, v)


def _exchange(arrs, *, scatter, name):
    n = len(arrs)
    out_shape = [jax.ShapeDtypeStruct(a.shape if scatter else (N_DEV,) + a.shape, a.dtype) for a in arrs]

    def body(*refs):
        ins, outs = refs[:n], refs[n:2 * n]
        send_sems, recv_sems, loc_sems = refs[2 * n:]
        me = 4 * lax.axis_index("x") + 2 * lax.axis_index("y") + lax.axis_index("c")
        local = []
        for k in range(n):
            src = ins[k].at[me] if scatter else ins[k]
            cp = pltpu.make_async_copy(src, outs[k].at[me], loc_sems.at[k])
            cp.start()
            local.append(cp)
        remote = []
        for r in range(1, N_DEV):
            peer = me ^ r
            for k in range(n):
                src = ins[k].at[peer] if scatter else ins[k]
                cp = pltpu.make_async_remote_copy(
                    src_ref=src, dst_ref=outs[k].at[me], send_sem=send_sems.at[k, r - 1], recv_sem=recv_sems.at[k, r - 1],
                    device_id=(peer // 4, (peer // 2) % 2, peer % 2), device_id_type=pl.DeviceIdType.MESH)
                cp.start()
                remote.append(cp)
        for cp in remote:
            cp.wait()
        for cp in local:
            cp.wait()

    anyspec = pl.BlockSpec(memory_space=pl.ANY)
    return pl.pallas_call(
        body, in_specs=[anyspec] * n, out_specs=[anyspec] * n, out_shape=out_shape,
        scratch_shapes=[pltpu.SemaphoreType.DMA((n, N_DEV - 1)), pltpu.SemaphoreType.DMA((n, N_DEV - 1)),
                        pltpu.SemaphoreType.DMA((n,))],
        name=name)(*arrs)


def _mesh_place():
    x, y, c = lax.axis_index("x"), lax.axis_index("y"), lax.axis_index("c")
    other_chips = [(1 - x, y), (x, 1 - y), (1 - x, 1 - y)]
    return x, y, c, other_chips


def _gather_two_level(arrs, *, name):
    n = len(arrs)
    plan = _gather_plan(arrs)

    def body(*refs):
        start, forward, finish = plan["phases"](refs[:n], refs[n:2 * n], *refs[2 * n:])
        start()
        forward()
        finish()

    anyspec = pl.BlockSpec(memory_space=pl.ANY)
    return pl.pallas_call(
        body, in_specs=[anyspec] * n, out_specs=[anyspec] * n, out_shape=plan["out_shape"],
        scratch_shapes=plan["scratch"], name=name)(*arrs)


def _gather_plan(arrs):
    n = len(arrs)

    def phases(ins, outs, send_sems, recv_sems, loc_sems):
        x, y, c, chips = _mesh_place()
        me = 4 * x + 2 * y + c
        slot = lambda px, py, pc: 4 * px + 2 * py + pc

        def copy(k, j, src, block, to):
            return pltpu.make_async_remote_copy(
                src_ref=src, dst_ref=outs[k].at[block], send_sem=send_sems.at[k, j], recv_sem=recv_sems.at[k, j],
                device_id=to, device_id_type=pl.DeviceIdType.MESH)

        local = [pltpu.make_async_copy(ins[k], outs[k].at[me], loc_sems.at[k]) for k in range(n)]
        first = []
        for k in range(n):
            first.append(copy(k, 0, ins[k], me, (x, y, 1 - c)))
            for j, chip in enumerate(chips):
                first.append(copy(k, 1 + j, ins[k], me, (*chip, c)))
        passed = [copy(k, 4 + j, outs[k].at[slot(*chip, c)], slot(*chip, c), (x, y, 1 - c))
                  for j, chip in enumerate(chips) for k in range(n)]

        def start():
            for cp in local + first:
                cp.start()

        def forward():
            for j, chip in enumerate(chips):
                for k in range(n):
                    copy(k, 1 + j, ins[k], slot(*chip, c), (x, y, c)).wait_recv()
                    passed[j * n + k].start()

        def finish():
            for k in range(n):
                copy(k, 0, ins[k], slot(x, y, 1 - c), (x, y, c)).wait_recv()
                for j, chip in enumerate(chips):
                    copy(k, 4 + j, ins[k], slot(*chip, 1 - c), (x, y, c)).wait_recv()
            for cp in first + passed:
                cp.wait_send()
            for cp in local:
                cp.wait()

        return start, forward, finish

    return dict(
        args=list(arrs), out_shape=[jax.ShapeDtypeStruct((N_DEV,) + a.shape, a.dtype) for a in arrs],
        scratch=[pltpu.SemaphoreType.DMA((n, 7)), pltpu.SemaphoreType.DMA((n, 7)), pltpu.SemaphoreType.DMA((n,))],
        phases=phases)


def _allreduce_small(small, extra, *, name):
    r, lanes = small.shape
    assert r % 16 == 0
    h = r // 2
    e = extra.shape[0]

    def body(s_ref, x_ref, out_ref, xall_ref, sib_ref, parts_ref, send_sems, recv_sems):
        x, y, c, chips = _mesh_place()
        me = 4 * x + 2 * y + c
        my_chip = 2 * x + y
        sibling = (x, y, 1 - c)
        mine = pl.ds(pl.multiple_of(c * h, 8), h)
        theirs = pl.ds(pl.multiple_of((1 - c) * h, 8), h)

        def remote(j, src, dst, to):
            return pltpu.make_async_remote_copy(src_ref=src, dst_ref=dst, send_sem=send_sems.at[j],
                                                recv_sem=recv_sems.at[j], device_id=to, device_id_type=pl.DeviceIdType.MESH)

        to_sibling = remote(0, s_ref.at[theirs], sib_ref, sibling)
        to_sibling.start()
        xall_ref[me] = x_ref[...]
        extras = []
        for rr in range(1, N_DEV):
            peer = me ^ rr
            cp = remote(4 + rr, x_ref, xall_ref.at[me], (peer // 4, (peer // 2) % 2, peer % 2))
            cp.start()
            extras.append(cp)
        to_sibling.wait_recv()
        parts_ref[my_chip] = s_ref[mine] + sib_ref[...]
        to_chips = [remote(1 + j, parts_ref.at[my_chip], parts_ref.at[my_chip], (px, py, c))
                    for j, (px, py) in enumerate(chips)]
        for cp in to_chips:
            cp.start()
        for cp in to_chips:
            cp.wait_recv()
        out_ref[mine] = (parts_ref[0] + parts_ref[1]) + (parts_ref[2] + parts_ref[3])
        done = remote(4, out_ref.at[mine], out_ref.at[mine], sibling)
        done.start()
        remote(4, out_ref.at[theirs], out_ref.at[theirs], sibling).wait_recv()
        for cp in extras:
            cp.wait()
        to_sibling.wait_send()
        for cp in to_chips:
            cp.wait_send()
        done.wait_send()

    vmem = pl.BlockSpec(memory_space=pltpu.VMEM)
    return pl.pallas_call(
        body, in_specs=[vmem, vmem], out_specs=[vmem, vmem],
        out_shape=[jax.ShapeDtypeStruct((r, lanes), F32), jax.ShapeDtypeStruct((N_DEV, e, lanes), F32)],
        scratch_shapes=[pltpu.VMEM((h, lanes), F32), pltpu.VMEM((4, h, lanes), F32),
                        pltpu.SemaphoreType.DMA((12,)), pltpu.SemaphoreType.DMA((12,))],
        compiler_params=pltpu.CompilerParams(vmem_limit_bytes=VMEM_LIMIT), name=name)(small, extra)


def _sibling_swap(arrs, *, name):
    n = len(arrs)
    out_shape = [jax.ShapeDtypeStruct(a.shape[1:], a.dtype) for a in arrs]

    def body(*refs):
        ins, outs = refs[:n], refs[n:2 * n]
        send_sems, recv_sems = refs[2 * n:]
        x, y, c, _ = _mesh_place()
        copies = [pltpu.make_async_remote_copy(
            src_ref=ins[k].at[1 - c], dst_ref=outs[k], send_sem=send_sems.at[k], recv_sem=recv_sems.at[k],
            device_id=(x, y, 1 - c), device_id_type=pl.DeviceIdType.MESH) for k in range(n)]
        for cp in copies:
            cp.start()
        for cp in copies:
            cp.wait()

    anyspec = pl.BlockSpec(memory_space=pl.ANY)
    return pl.pallas_call(
        body, in_specs=[anyspec] * n, out_specs=[anyspec] * n, out_shape=out_shape,
        scratch_shapes=[pltpu.SemaphoreType.DMA((n,)), pltpu.SemaphoreType.DMA((n,))], name=name)(*arrs)


def _pair_add(mine, theirs, core, *, name, tr=256):
    _, r, c = mine.shape
    tr = min(tr, r)
    assert r % tr == 0

    def body(core_ref, m_ref, t_ref, o_ref):
        o_ref[...] = (m_ref[0].astype(F32) + t_ref[...].astype(F32)).astype(BF16)

    return pl.pallas_call(
        body,
        grid_spec=pltpu.PrefetchScalarGridSpec(
            num_scalar_prefetch=1, grid=(r // tr,),
            in_specs=[pl.BlockSpec((1, tr, c), lambda i, core_ref: (core_ref[0], i, 0)),
                      pl.BlockSpec((tr, c), lambda i, core_ref: (i, 0))],
            out_specs=pl.BlockSpec((tr, c), lambda i, core_ref: (i, 0))),
        out_shape=jax.ShapeDtypeStruct((r, c), BF16),
        compiler_params=_params("parallel"), name=name)(core, mine, theirs)


def _chip_scatter(arrs, *, name):
    n = len(arrs)
    plan = _chip_scatter_plan(arrs)

    def body(*refs):
        start, _, finish = plan["phases"](refs[:n], refs[n:2 * n], *refs[2 * n:])
        start()
        finish()

    anyspec = pl.BlockSpec(memory_space=pl.ANY)
    return pl.pallas_call(
        body, in_specs=[anyspec] * n, out_specs=[anyspec] * n, out_shape=plan["out_shape"],
        scratch_shapes=plan["scratch"], name=name)(*arrs)


def _chip_scatter_plan(arrs):
    n = len(arrs)

    def phases(ins, outs, send_sems, recv_sems, loc_sems):
        x, y, c, chips = _mesh_place()
        mine = 2 * x + y
        local = [pltpu.make_async_copy(ins[k].at[mine], outs[k].at[mine], loc_sems.at[k]) for k in range(n)]
        remote = [pltpu.make_async_remote_copy(
            src_ref=ins[k].at[2 * px + py], dst_ref=outs[k].at[mine], send_sem=send_sems.at[k, j],
            recv_sem=recv_sems.at[k, j], device_id=(px, py, c), device_id_type=pl.DeviceIdType.MESH)
            for j, (px, py) in enumerate(chips) for k in range(n)]

        def start():
            for cp in local + remote:
                cp.start()

        def finish():
            for cp in remote:
                cp.wait()
            for cp in local:
                cp.wait()

        return start, (lambda: None), finish

    return dict(
        args=list(arrs), out_shape=[jax.ShapeDtypeStruct(a.shape, a.dtype) for a in arrs],
        scratch=[pltpu.SemaphoreType.DMA((n, 3)), pltpu.SemaphoreType.DMA((n, 3)), pltpu.SemaphoreType.DMA((n,))],
        phases=phases)


def _ssm_discretize(a_re, a_im, log_dt, b_re, b_im):
    dt = jnp.exp(log_dt)[:, None]
    mag = jnp.exp(a_re * dt)
    lr = mag * jnp.cos(a_im * dt)
    li = mag * jnp.sin(a_im * dt)
    den = a_re * a_re + a_im * a_im
    cr = ((lr - 1.0) * a_re + li * a_im) / den
    ci = (li * a_re - (lr - 1.0) * a_im) / den
    bbr = cr[..., None] * b_re - ci[..., None] * b_im
    bbi = cr[..., None] * b_im + ci[..., None] * b_re
    return lr, li, bbr, bbi


def _ssm_dense(lr, li, bbr, bbi, c_re, c_im):
    scan_ops = _scan_tables(lr.reshape(-1), li.reshape(-1)) + _scan_consts(SEQ_BLOCK)
    per_half = SSM_GROUPS // SSM_HALVES

    def halves(a, rows, cols):
        a = a.reshape(SSM_HALVES, per_half * rows, cols)
        tiled = jnp.tile(a, (1, 1, per_half))
        r = lax.broadcasted_iota(jnp.int32, tiled.shape, 1) // rows
        c = lax.broadcasted_iota(jnp.int32, tiled.shape, 2) // cols
        return jnp.where(r == c, tiled, 0.0)

    bblk = jnp.stack([halves(b.transpose(0, 2, 1), SSM_GROUP, SSM_STATE) for b in (bbr, bbi)]).astype(BF16)
    cblk = jnp.stack([halves(c.transpose(0, 2, 1), SSM_STATE, SSM_GROUP) for c in (c_re, c_im)]).astype(BF16)
    return scan_ops, bblk, cblk


def _ssm_extract(db, dc, sums):
    db = db.reshape(2, SSM_GROUPS, SSM_GROUP, SSM_STATE).transpose(0, 1, 3, 2)
    dc = dc.reshape(2, SSM_GROUPS, SSM_STATE, SSM_GROUP).transpose(0, 1, 3, 2)
    dlr = sums[0].reshape(SSM_GROUPS, SSM_STATE)
    dli = sums[1].reshape(SSM_GROUPS, SSM_STATE)
    return dlr, dli, db[0], db[1], dc[0], dc[1]


IN_SPLITS = (ATT_W, KV_W, KV_W, SSM_W, POOL_W, ATT_W, SSM_W, POOL_W, 3 * D_MODEL)


def _split_w_in(w):
    idx = [0]
    for s in IN_SPLITS:
        idx.append(idx[-1] + s)
    seg = [w[..., idx[k]:idx[k + 1]] for k in range(len(IN_SPLITS))]
    q, k, v, us, up, za, zs, zp, gl = seg
    return (jnp.concatenate([q, za, k, v], axis=-1), jnp.concatenate([us, zs], axis=-1),
            jnp.concatenate([up, zp], axis=-1), gl)


def _merge_w_in(da, ds, dp, dg):
    q, za, k, v = da[..., :ATT_W], da[..., ATT_W:2 * ATT_W], da[..., 2 * ATT_W:2 * ATT_W + KV_W], da[..., 2 * ATT_W + KV_W:]
    us, zs = ds[..., :SSM_W], ds[..., SSM_W:]
    up, zp = dp[..., :POOL_W], dp[..., POOL_W:]
    return jnp.concatenate([q, k, v, us, up, za, zs, zp, dg], axis=-1)


def _layer_fwd(x, lw, li, late=None, comm_attn=None, comm_ssm=None):
    tag = f"l{li}"
    h = _ln_fwd(x, lw["norm_g"], lw["shift"], lw["scale"], name=f"ln_fwd_{tag}")
    pa = _mm(h, lw["w_a"], tn=1280, out_dtype=BF16, name=f"proj_a_{tag}")
    ps = _mm(h, lw["w_s"], out_dtype=BF16, name=f"proj_s_{tag}")
    pp = _mm(h, lw["w_p"], out_dtype=BF16, name=f"proj_p_{tag}")
    if late is None:
        pg = _mm(h, lw["w_g"], out_dtype=BF16, name=f"proj_g_{tag}")
    else:
        pg, arrived = _mm(h, lw["w_g"], out_dtype=BF16, name=f"proj_g_{tag}", comm=late[0])
        lw = {**lw, **late[1](arrived)}
    ya, from_attn = _attn_fwd(pa, lw["sinks"], name=f"attn_fwd_{tag}", comm=comm_attn)
    ys, chk, states, *from_ssm = _ssm_fwd(ps, lw["lam"], lw["bblk"], lw["cblk"], lw["ssm_d"], lw["w_glu"], lw["b_glu"],
                                          name=f"ssm_fwd_{tag}", comm=comm_ssm)
    yp = _pool_fwd(pp, lw["w_pool"], lw["pool_scale"], name=f"pool_fwd_{tag}")
    merged, ba, bs, bp = _merge_fwd(ya, ys, yp, lw["w_br_att"], lw["w_br_ssm"], lw["w_br_pool"], pg, name=f"merge_fwd_{tag}")
    x_new, out = _out_fwd(merged, lw["w_out"], x, lw["gate"], name=f"out_fwd_{tag}")
    saved = dict(x=x, h=h, pa=pa, ps=ps, pp=pp, pg=pg, ya=ya, ys=ys, yp=yp, chk=chk, states=states, merged=merged,
                 ba=ba, bs=bs, bp=bp, out=out)
    return x_new, saved, lw, list(from_attn), list(from_ssm)


def _layer_bwd(dx, lw, sv, li, comm=None, early=None):
    tag = f"l{li}"
    dmo, dba, dbs, dbp, dpg, gate_sums = _merge_bwd(dx, sv["out"], lw["gate"], lw["w_out"], sv["pg"],
                                                    sv["ba"], sv["bs"], sv["bp"], name=f"merge_bwd_{tag}")
    g = {}
    g["w_out"] = _mm_tn(sv["merged"], dmo, out_dtype=BF16, name=f"dw_out_{tag}")
    dya = _mm(dba, lw["w_br_att"], nt=True, name=f"dy_att_{tag}")
    dys = _mm(dbs, lw["w_br_ssm"], nt=True, name=f"dy_ssm_{tag}")
    dyp = _mm(dbp, lw["w_br_pool"], nt=True, name=f"dy_pool_{tag}")
    g["w_br_att"] = _mm_tn(sv["ya"], dba, out_dtype=BF16, name=f"dw_br_att_{tag}")
    g["w_br_ssm"] = _mm_tn(sv["ys"], dbs, out_dtype=BF16, name=f"dw_br_ssm_{tag}")
    g["w_br_pool"] = _mm_tn(sv["yp"], dbp, out_dtype=BF16, name=f"dw_br_pool_{tag}")
    dpa, dsink = _attn_bwd(sv["pa"], lw["sinks"], dya, name=f"attn_bwd_{tag}")
    dps, db_dense, dc_dense, dwglu, ssm_sums, *exchanged = _ssm_bwd(
        sv["ps"], dys, sv["chk"], sv["states"], lw["lam"], lw["bblk"], lw["cblk"], lw["ssm_d"], lw["w_glu"], lw["b_glu"],
        name=f"ssm_bwd_{tag}", comm=comm)
    g["w_glu"] = dwglu.astype(BF16)
    dpp, dwpool, pool_sums = _pool_bwd(sv["pp"], dyp, lw["w_pool"], lw["pool_scale"], name=f"pool_bwd_{tag}")
    dh_pairs = [(dpa, lw["w_a"]), (dps, lw["w_s"]), (dpp, lw["w_p"]), (dpg, lw["w_g"])]
    if early is None:
        dh, from_early = _mm_nt_sum(dh_pairs, name=f"dh_{tag}"), []
    else:
        dh, from_early = _mm_nt_sum(dh_pairs, name=f"dh_{tag}", comm=early({k: g[k] for k in LATE_WEIGHTS}))
    h = sv["h"]
    g["w_in"] = _merge_w_in(_mm_tn(h, dpa, out_dtype=BF16, tn=1280, name=f"dw_a_{tag}"),
                            _mm_tn(h, dps, out_dtype=BF16, name=f"dw_s_{tag}"),
                            _mm_tn(h, dpp, out_dtype=BF16, name=f"dw_p_{tag}"),
                            _mm_tn(h, dpg, out_dtype=BF16, name=f"dw_g_{tag}"))
    dx_in, ln_sums = _ln_bwd(sv["x"], dh, dx, lw["norm_g"], lw["scale"], name=f"ln_bwd_{tag}")
    g["dmod"] = jnp.concatenate([ln_sums[0], ln_sums[1], gate_sums[0]])
    g["norm_g"] = ln_sums[2]
    g["attn_sinks"] = dsink[:, 0]
    g["ssm_raw"] = _ssm_extract(db_dense, dc_dense, ssm_sums)
    g["ssm_d"] = ssm_sums[2, :SSM_W]
    g["b_glu"] = ssm_sums[3, :SSM_W]
    g["w_pool"] = dwpool
    g["pool_scale"] = pool_sums[0]
    return dx_in, g, exchanged, from_early


BIG_WEIGHTS = ("w_in", "w_glu", "w_br_att", "w_br_ssm", "w_br_pool", "w_out")
ROW_SHARDED = ("w_glu", "w_out")


LATE_WEIGHTS = BIG_WEIGHTS[1:]


def _full_weights(keys, gathered):
    full = {}
    for k, g in zip(keys, gathered):
        if k in ROW_SHARDED:
            full[k] = g.reshape(N_DEV * g.shape[1], g.shape[2])
        else:
            full[k] = g.transpose(1, 0, 2).reshape(g.shape[1], N_DEV * g.shape[2])
    return full


def _by_destination(keys, grads):
    out = []
    for k in keys:
        g = grads[k]
        if k in ROW_SHARDED:
            out.append(g.reshape(4, 2, g.shape[0] // N_DEV, g.shape[1]).transpose(1, 0, 2, 3))
        else:
            out.append(g.reshape(g.shape[0], 4, 2, g.shape[1] // N_DEV).transpose(2, 1, 0, 3))
    return out


def _prepare_layer(li, mod, norm_g, w_in_full, attn_sinks, disc, ssm_c_re, ssm_c_im, ssm_d, b_glu, w_pool, pool_scale):
    d = D_MODEL
    lr, li_, bbr, bbi = disc
    lam, bblk, cblk = _ssm_dense(lr[li], li_[li], bbr[li], bbi[li], ssm_c_re[li], ssm_c_im[li])
    w_a, w_s, w_p, w_g = _split_w_in(w_in_full)
    return dict(
        norm_g=norm_g[li][None, :], shift=mod[li, :d][None, :], scale=mod[li, d:2 * d][None, :],
        gate=mod[li, 2 * d:][None, :], w_a=w_a, w_s=w_s, w_p=w_p, w_g=w_g,
        sinks=attn_sinks[li], lam=lam, bblk=bblk, cblk=cblk, ssm_d=ssm_d[li][None, :],
        b_glu=b_glu[li][None, :], w_pool=w_pool[li].astype(BF16), pool_scale=pool_scale[li][None, :])


SMALL_ROWS = 64
SMALL_ORDER = ("norm_g", "attn_sinks", "ssm_d", "b_glu", "w_pool", "pool_scale", "dmod")


def _pack_small(loss, dfinal_g, layer_grads):
    parts = [jnp.broadcast_to(loss.reshape(1), (128,)), dfinal_g]
    for g in layer_grads:
        for k in SMALL_ORDER:
            v = g[k].reshape(-1)
            if v.shape[0] % 128:
                v = jnp.pad(v, (0, 128 - v.shape[0] % 128))
            parts.append(v)
        for v in g["ssm_raw"]:
            parts.append(v.reshape(-1))
    flat = jnp.concatenate(parts)
    return jnp.pad(flat, (0, (-flat.shape[0]) % (SMALL_ROWS * 128))).reshape(-1, 128)


def _unpack_small(flat, shapes):
    out, off = [], 0
    for s in shapes:
        n = int(math.prod(s))
        out.append(flat[off:off + n].reshape(s))
        off += n + (-n) % 128
    return out


def kernel(x, c, norm_g, w_ada, b_ada, w_in, attn_sinks, ssm_a_re, ssm_a_im, ssm_log_dt, ssm_b_re, ssm_b_im, ssm_c_re, ssm_c_im, ssm_d, w_glu, b_glu, w_pool, pool_scale, w_br_att, w_br_ssm, w_br_pool, w_out, final_g, loss_target, m_norm_g, m_w_ada, m_b_ada, m_w_in, m_attn_sinks, m_ssm_a_re, m_ssm_a_im, m_ssm_log_dt, m_ssm_b_re, m_ssm_b_im, m_ssm_c_re, m_ssm_c_im, m_ssm_d, m_w_glu, m_b_glu, m_w_pool, m_pool_scale, m_w_br_att, m_w_br_ssm, m_w_br_pool, m_w_out, m_final_g, v_norm_g, v_w_ada, v_b_ada, v_w_in, v_attn_sinks, v_ssm_a_re, v_ssm_a_im, v_ssm_log_dt, v_ssm_b_re, v_ssm_b_im, v_ssm_c_re, v_ssm_c_im, v_ssm_d, v_w_glu, v_b_glu, v_w_pool, v_pool_scale, v_w_br_att, v_w_br_ssm, v_w_br_pool, v_w_out, v_final_g):
    me = 4 * lax.axis_index("x") + 2 * lax.axis_index("y") + lax.axis_index("c")
    d = D_MODEL
    ada_w = 3 * d // N_DEV

    (c_all,) = _exchange([c.reshape(8, 128)], scatter=False, name="gather_c")
    c_act = jax.nn.silu(c_all.reshape(N_DEV, d))
    b_cols = lax.dynamic_slice(b_ada, (0, me * ada_w), (DEPTH, ada_w))
    mod_part = jnp.concatenate(
        [_mm(c_act, w_ada[li], name=f"ada_fwd_l{li}") + b_cols[li][None, :] for li in range(DEPTH)], axis=0)
    (mod_all,) = _exchange([mod_part], scatter=False, name="gather_mod")
    mod_all = mod_all.reshape(N_DEV, DEPTH, N_DEV, ada_w)
    mod_mine = lax.dynamic_index_in_dim(mod_all, me, axis=2, keepdims=False)
    mod_mine = mod_mine.transpose(1, 0, 2).reshape(DEPTH, 3 * d)

    sharded = dict(w_in=w_in, w_glu=w_glu, w_br_att=w_br_att, w_br_ssm=w_br_ssm, w_br_pool=w_br_pool, w_out=w_out)
    shards = lambda li, keys: [sharded[k][li].astype(BF16) for k in keys]
    disc, disc_vjp = jax.vjp(jax.vmap(_ssm_discretize), ssm_a_re, ssm_a_im, ssm_log_dt, ssm_b_re, ssm_b_im)
    layer = lambda li, gathered_w_in: _prepare_layer(
        li, mod_mine, norm_g, _full_weights(("w_in",), gathered_w_in)["w_in"], attn_sinks, disc, ssm_c_re, ssm_c_im,
        ssm_d, b_glu, w_pool, pool_scale)
    late_weights = lambda gathered: _full_weights(LATE_WEIGHTS, gathered)
    core = lax.axis_index("c").astype(jnp.int32).reshape(1)

    def chip_sums_of(keys, grads_li, tag):
        by_dest = _by_destination(keys, grads_li)
        from_sibling = _sibling_swap(by_dest, name=f"grads_sibling_swap_{tag}")
        return [_pair_add(a.reshape(2, -1, a.shape[-1]), b.reshape(-1, b.shape[-1]), core,
                          name=f"grads_pair_add_{tag}_{k}").reshape(b.shape)
                for k, (a, b) in zip(keys, zip(by_dest, from_sibling))]

    layers, saved, grads = [None] * DEPTH, [None] * DEPTH, [None] * DEPTH
    layers[0] = layer(0, _gather_two_level(shards(0, ("w_in",)), name="gather_w_in_l0"))
    xs, saved[0], layers[0], late1, w_in1 = _layer_fwd(
        x[0], layers[0], 0, late=(_gather_plan(shards(0, LATE_WEIGHTS)), late_weights),
        comm_attn=_gather_plan(shards(1, LATE_WEIGHTS)), comm_ssm=_gather_plan(shards(1, ("w_in",))))
    layers[1] = {**layer(1, w_in1), **late_weights(late1)}
    xs, saved[1], _, _, _ = _layer_fwd(xs, layers[1], 1)
    dx, fin_sums = _final_loss(xs, final_g[None, :], loss_target[0])
    loss_part = jnp.sum(fin_sums[1])
    dx, grads[1], _, _ = _layer_bwd(dx, layers[1], saved[1], 1)
    dx, grads[0], scattered1, scattered0_late = _layer_bwd(
        dx, layers[0], saved[0], 0, comm=_chip_scatter_plan(chip_sums_of(BIG_WEIGHTS, grads[1], "l1")),
        early=lambda g_late: _chip_scatter_plan(chip_sums_of(LATE_WEIGHTS, g_late, "l0_late")))
    scattered0 = list(_chip_scatter(chip_sums_of(("w_in",), grads[0], "l0_w_in"), name="grads_chip_scatter_l0_w_in"))
    scattered0 += list(scattered0_late)
    big = [jnp.stack([a, b], axis=1) for a, b in zip(scattered0, scattered1)]
    grad_x = dx[None]

    small = _pack_small(loss_part, fin_sums[0], grads)
    dmod_rows = jnp.concatenate([grads[li]["dmod"] for li in range(DEPTH)]).reshape(-1, 128)
    small_sum, dmod_gathered = _allreduce_small(small, dmod_rows, name="allreduce_small")
    out = {}

    def adam(name, w, g_parts, m, v):
        shp = w.shape
        r = int(math.prod(shp[:-1])) if len(shp) > 1 else 1
        w2, m2, v2 = (a.reshape(r, shp[-1]) for a in (w, m, v))
        g2 = g_parts.reshape(g_parts.shape[0], r, shp[-1])
        res = _adamw(w2, g2, m2, v2, name=f"adamw_{name}")
        out[name] = tuple(a.reshape(shp) for a in res)

    flat = small_sum.reshape(-1)
    shapes = [(128,), (d,)]
    for _ in range(DEPTH):
        shapes += [(d,), (N_HEADS,), (SSM_W,), (SSM_W,), (4, POOL_GW, POOL_GW), (POOL_W,), (3 * d,),
                   (SSM_GROUPS, SSM_STATE), (SSM_GROUPS, SSM_STATE), (SSM_GROUPS, SSM_STATE, SSM_GROUP),
                   (SSM_GROUPS, SSM_STATE, SSM_GROUP), (SSM_GROUPS, SSM_GROUP, SSM_STATE), (SSM_GROUPS, SSM_GROUP, SSM_STATE)]
    un = _unpack_small(flat, shapes)
    loss = un[0][0]
    g_final_g = un[1]
    per = 13
    gl = [un[2 + li * per: 2 + (li + 1) * per] for li in range(DEPTH)]
    st = lambda j: jnp.stack([gl[li][j] for li in range(DEPTH)])
    g_norm_g, g_sinks, g_ssm_d, g_b_glu, g_w_pool, g_pool_scale, g_b_ada = (st(j) for j in range(7))
    d_lr, d_li, d_bbr, d_bbi, g_c_re, g_c_im = (st(j) for j in range(7, 13))
    g_a_re, g_a_im, g_log_dt, g_b_re, g_b_im = disc_vjp((d_lr, d_li, d_bbr, d_bbi))

    dmod_all = lax.dynamic_slice(dmod_gathered.reshape(N_DEV, DEPTH, 3 * d), (0, 0, me * ada_w), (N_DEV, DEPTH, ada_w))
    dmod_all = dmod_all.transpose(1, 0, 2)
    g_w_ada = jnp.stack([_mm_tn(c_act, dmod_all[li], tm=d, tn=ada_w, tk=N_DEV, name=f"dw_ada_l{li}") for li in range(DEPTH)])

    adam("w_ada", w_ada, g_w_ada[None], m_w_ada, v_w_ada)
    adam("w_in", w_in, big[0], m_w_in, v_w_in)
    adam("w_glu", w_glu, big[1], m_w_glu, v_w_glu)
    adam("w_br_att", w_br_att, big[2], m_w_br_att, v_w_br_att)
    adam("w_br_ssm", w_br_ssm, big[3], m_w_br_ssm, v_w_br_ssm)
    adam("w_br_pool", w_br_pool, big[4], m_w_br_pool, v_w_br_pool)
    adam("w_out", w_out, big[5], m_w_out, v_w_out)

    small_names = ["norm_g", "b_ada", "attn_sinks", "ssm_a_re", "ssm_a_im", "ssm_log_dt", "ssm_b_re", "ssm_b_im",
                   "ssm_c_re", "ssm_c_im", "ssm_d", "b_glu", "w_pool", "pool_scale", "final_g"]
    small_w = [norm_g, b_ada, attn_sinks, ssm_a_re, ssm_a_im, ssm_log_dt, ssm_b_re, ssm_b_im, ssm_c_re, ssm_c_im,
               ssm_d, b_glu, w_pool, pool_scale, final_g]
    small_m = [m_norm_g, m_b_ada, m_attn_sinks, m_ssm_a_re, m_ssm_a_im, m_ssm_log_dt, m_ssm_b_re, m_ssm_b_im,
               m_ssm_c_re, m_ssm_c_im, m_ssm_d, m_b_glu, m_w_pool, m_pool_scale, m_final_g]
    small_v = [v_norm_g, v_b_ada, v_attn_sinks, v_ssm_a_re, v_ssm_a_im, v_ssm_log_dt, v_ssm_b_re, v_ssm_b_im,
               v_ssm_c_re, v_ssm_c_im, v_ssm_d, v_b_glu, v_w_pool, v_pool_scale, v_final_g]
    small_g = [g_norm_g, g_b_ada, g_sinks, g_a_re, g_a_im, g_log_dt, g_b_re, g_b_im, g_c_re, g_c_im,
               g_ssm_d, g_b_glu, g_w_pool, g_pool_scale, g_final_g]

    for nm, w, g, m, v in zip(small_names, small_w, small_g, small_m, small_v):
        adam(nm, w, g[None], m, v)

    order = ["norm_g", "w_ada", "b_ada", "w_in", "attn_sinks", "ssm_a_re", "ssm_a_im", "ssm_log_dt", "ssm_b_re",
             "ssm_b_im", "ssm_c_re", "ssm_c_im", "ssm_d", "w_glu", "b_glu", "w_pool", "pool_scale", "w_br_att",
             "w_br_ssm", "w_br_pool", "w_out", "final_g"]
    return (loss, grad_x, *[out[k][0] for k in order], *[out[k][1] for k in order],
            *[out[k][2] for k in order], *[out[k][3] for k in order])
```

```python
import functools
import math

import jax
import jax.numpy as jnp
from jax import lax
from jax.experimental import pallas as pl
from jax.experimental.pallas import tpu as pltpu

F32 = jnp.float32
BF16 = jnp.bfloat16

N_DEV = 8
D_MODEL = 1024
DEPTH = 2
CHUNK = 64
N_HEADS = 8
N_KV_HEADS = 2
HEAD_DIM = 64
Q_PER_KV = N_HEADS // N_KV_HEADS
WINDOW = 128
ATT_W = 512
KV_W = 128
SSM_W = 512
SSM_GROUP = 16
SSM_GROUPS = 32
SSM_STATE = 64
SSM_N = SSM_GROUPS * SSM_STATE
POOL_W = 512
POOL_WINDOWS = (2, 4, 8, 16)
POOL_GW = 128
POOL_HALO = 16
EPS = 1e-6
NEG_INF = -1e30
ADAM_LR = 0.001
ADAM_B1 = 0.9
ADAM_B2 = 0.999
ADAM_EPS = 1e-08
ADAM_WD = 0.01
ADAM_STEP = 10

SEQ_BLOCK = 256
ATT_BLOCK = 128
VMEM_LIMIT = 56 * 1024 * 1024

NN = (((1,), (0,)), ((), ()))
NT = (((1,), (1,)), ((), ()))
TN = (((0,), (0,)), ((), ()))


def _dot(a, b, dims=NN):
    return lax.dot_general(a.astype(BF16), b.astype(BF16), dims, preferred_element_type=F32)


def _params(*sem):
    return pltpu.CompilerParams(dimension_semantics=sem, vmem_limit_bytes=VMEM_LIMIT)


def _sigmoid(x):
    return 1.0 / (1.0 + jnp.exp(-x))


def _silu_and_grad(z):
    s = _sigmoid(z)
    return z * s, s * (1.0 + z * (1.0 - s))


_GELU_K = math.sqrt(2.0 / math.pi)


def _gelu_and_grad(x):
    inner = _GELU_K * (x + 0.044715 * x * x * x)
    t = jnp.tanh(inner)
    val = 0.5 * x * (1.0 + t)
    grad = 0.5 * (1.0 + t) + 0.5 * x * (1.0 - t * t) * _GELU_K * (1.0 + 3.0 * 0.044715 * x * x)
    return val, grad


def _mm(a, b, *, nt=False, out_dtype=F32, tm=1024, tn=1024, name, comm=None):
    m, k = a.shape
    n = b.shape[0] if nt else b.shape[1]
    tm, tn = min(tm, m), min(tn, n)
    assert m % tm == 0 and n % tn == 0
    dims = NT if nt else NN
    grid = (m // tm, n // tn)
    c_args, c_in, c_out, c_shape, c_scratch = _comm_extra(comm)

    def body(a_ref, b_ref, o_ref):
        o_ref[...] = _dot(a_ref[...], b_ref[...], dims).astype(out_dtype)

    b_spec = pl.BlockSpec((tn, k), lambda i, j: (j, 0)) if nt else pl.BlockSpec((k, tn), lambda i, j: (0, j))
    res = pl.pallas_call(
        _with_comm(body, comm, 2, 1, grid, -1), grid=grid,
        in_specs=[pl.BlockSpec((tm, k), lambda i, j: (i, 0)), b_spec] + c_in,
        out_specs=[pl.BlockSpec((tm, tn), lambda i, j: (i, j))] + c_out,
        out_shape=[jax.ShapeDtypeStruct((m, n), out_dtype)] + c_shape,
        scratch_shapes=c_scratch,
        compiler_params=_params(*(("arbitrary",) * 2 if comm else ("parallel",) * 2)), name=name)(a, b, *c_args)
    return (res[0], list(res[1:])) if comm else res[0]


def _mm_nt_sum(pairs, *, out_dtype=F32, tm=512, tn=512, name, comm=None):
    m = pairs[0][0].shape[0]
    n = pairs[0][1].shape[0]
    np_ = len(pairs)
    grid = (m // tm, n // tn)
    c_args, c_in, c_out, c_shape, c_scratch = _comm_extra(comm)

    def body(*refs):
        o_ref = refs[-1]
        acc = _dot(refs[0][...], refs[1][...], NT)
        for p in range(1, np_):
            acc = acc + _dot(refs[2 * p][...], refs[2 * p + 1][...], NT)
        o_ref[...] = acc.astype(out_dtype)

    in_specs, args = [], []
    for a, b in pairs:
        in_specs.append(pl.BlockSpec((tm, a.shape[1]), lambda i, j: (i, 0)))
        in_specs.append(pl.BlockSpec((tn, b.shape[1]), lambda i, j: (j, 0)))
        args += [a, b]
    res = pl.pallas_call(
        _with_comm(body, comm, 2 * np_, 1, grid, -1), grid=grid, in_specs=in_specs + c_in,
        out_specs=[pl.BlockSpec((tm, tn), lambda i, j: (i, j))] + c_out,
        out_shape=[jax.ShapeDtypeStruct((m, n), out_dtype)] + c_shape,
        scratch_shapes=c_scratch,
        compiler_params=_params(*(("arbitrary",) * 2 if comm else ("parallel",) * 2)), name=name)(*args, *c_args)
    return (res[0], list(res[1:])) if comm else res[0]


def _mm_tn(a, b, *, out_dtype=F32, tm=1024, tn=1024, tk=1024, name, comm=None):
    k, m = a.shape
    n = b.shape[1]
    assert m % min(tm, m) == 0 and n % min(tn, n) == 0 and k % min(tk, k) == 0
    tm, tn, tk = min(tm, m), min(tn, n), min(tk, k)
    nk = k // tk
    grid = (m // tm, n // tn, nk)
    c_args, c_in, c_out, c_shape, c_scratch = _comm_extra(comm)

    def body(a_ref, b_ref, o_ref, acc_ref):
        kk = pl.program_id(2)

        @pl.when(kk == 0)
        def _():
            acc_ref[...] = jnp.zeros_like(acc_ref)

        acc_ref[...] += _dot(a_ref[...], b_ref[...], TN)

        @pl.when(kk == nk - 1)
        def _():
            o_ref[...] = acc_ref[...].astype(out_dtype)

    res = pl.pallas_call(
        _with_comm(body, comm, 2, 1, grid, -1), grid=grid,
        in_specs=[pl.BlockSpec((tk, tm), lambda i, j, kk: (kk, i)), pl.BlockSpec((tk, tn), lambda i, j, kk: (kk, j))] + c_in,
        out_specs=[pl.BlockSpec((tm, tn), lambda i, j, kk: (i, j))] + c_out,
        out_shape=[jax.ShapeDtypeStruct((m, n), out_dtype)] + c_shape,
        scratch_shapes=[pltpu.VMEM((tm, tn), F32)] + c_scratch,
        compiler_params=_params(*(("arbitrary",) * 3 if comm else ("parallel", "parallel", "arbitrary"))),
        name=name)(a, b, *c_args)
    return (res[0], list(res[1:])) if comm else res[0]


def _ln_fwd(x, g, shift, scale, *, name, tm=512):
    l, d = x.shape

    def body(x_ref, g_ref, sh_ref, sc_ref, h_ref):
        xv = x_ref[...]
        n = xv * lax.rsqrt(jnp.mean(xv * xv, axis=-1, keepdims=True) + EPS)
        h_ref[...] = ((n * g_ref[...]) * (1.0 + sc_ref[...]) + sh_ref[...]).astype(BF16)

    vec = pl.BlockSpec((1, d), lambda i: (0, 0))
    return pl.pallas_call(
        body, grid=(l // tm,),
        in_specs=[pl.BlockSpec((tm, d), lambda i: (i, 0)), vec, vec, vec],
        out_specs=pl.BlockSpec((tm, d), lambda i: (i, 0)),
        out_shape=jax.ShapeDtypeStruct((l, d), BF16),
        compiler_params=_params("parallel"), name=name)(x, g, shift, scale)


def _ln_bwd(x, dh, dres, g, scale, *, name, tm=512):
    l, d = x.shape

    def body(x_ref, dh_ref, dres_ref, g_ref, sc_ref, dx_ref, sums_ref):
        xv = x_ref[...]
        dhv = dh_ref[...]
        rstd = lax.rsqrt(jnp.mean(xv * xv, axis=-1, keepdims=True) + EPS)
        n = xv * rstd
        gv = g_ref[...]
        dr = dhv * (1.0 + sc_ref[...])
        dn = dr * gv
        dx_ref[...] = dres_ref[...] + rstd * (dn - n * jnp.mean(dn * n, axis=-1, keepdims=True))

        @pl.when(pl.program_id(0) == 0)
        def _():
            sums_ref[...] = jnp.zeros_like(sums_ref)

        sums_ref[0:1, :] += jnp.sum(dhv, axis=0, keepdims=True)
        sums_ref[1:2, :] += jnp.sum(dhv * (n * gv), axis=0, keepdims=True)
        sums_ref[2:3, :] += jnp.sum(dr * n, axis=0, keepdims=True)

    vec = pl.BlockSpec((1, d), lambda i: (0, 0))
    row = pl.BlockSpec((tm, d), lambda i: (i, 0))
    return pl.pallas_call(
        body, grid=(l // tm,),
        in_specs=[row, row, row, vec, vec],
        out_specs=[row, pl.BlockSpec((8, d), lambda i: (0, 0))],
        out_shape=[jax.ShapeDtypeStruct((l, d), F32), jax.ShapeDtypeStruct((8, d), F32)],
        compiler_params=_params("arbitrary"), name=name)(x, dh, dres, g, scale)


def _final_loss(x, g, target, *, tm=512):
    l, d = x.shape

    def body(x_ref, g_ref, t_ref, dx_ref, sums_ref):
        xv = x_ref[...]
        rstd = lax.rsqrt(jnp.mean(xv * xv, axis=-1, keepdims=True) + EPS)
        n = xv * rstd
        gv = g_ref[...]
        err = n * gv - t_ref[...]
        dy = err * (1.0 / d)
        dn = dy * gv
        dx_ref[...] = rstd * (dn - n * jnp.mean(dn * n, axis=-1, keepdims=True))

        @pl.when(pl.program_id(0) == 0)
        def _():
            sums_ref[...] = jnp.zeros_like(sums_ref)

        sums_ref[0:1, :] += jnp.sum(dy * n, axis=0, keepdims=True)
        sums_ref[1:2, :] += jnp.sum(err * err, axis=0, keepdims=True) * (0.5 / d)

    vec = pl.BlockSpec((1, d), lambda i: (0, 0))
    row = pl.BlockSpec((tm, d), lambda i: (i, 0))
    dx, sums = pl.pallas_call(
        body, grid=(l // tm,),
        in_specs=[row, vec, row],
        out_specs=[row, pl.BlockSpec((8, d), lambda i: (0, 0))],
        out_shape=[jax.ShapeDtypeStruct((l, d), F32), jax.ShapeDtypeStruct((8, d), F32)],
        compiler_params=_params("arbitrary"), name="final_loss")(x, g, target)
    return dx, sums


def _attn_geometry(i, t):
    nk = t + WINDOW
    qi = lax.broadcasted_iota(jnp.int32, (t, nk), 0)
    kj = lax.broadcasted_iota(jnp.int32, (t, nk), 1)
    dist = jnp.abs(qi + WINDOW - kj).astype(F32)
    qc = jnp.right_shift(qi, 6)
    kc = jnp.right_shift(kj, 6)
    valid = (kc >= qc) & (kc <= qc + WINDOW // CHUNK) & ((i > 0) | (kj >= WINDOW))
    return dist, valid


def _attn_head(q, k_all, v_all, sink, slope, dist, valid):
    s = _dot(q, k_all, NT) * (1.0 / math.sqrt(HEAD_DIM)) - slope * dist
    s = jnp.where(valid, s, NEG_INF)
    m = jnp.maximum(jnp.max(s, axis=-1, keepdims=True), sink)
    e = jnp.exp(s - m)
    es = jnp.exp(sink - m)
    inv = 1.0 / (jnp.sum(e, axis=-1, keepdims=True) + es)
    p = e * inv
    o = _dot(p, v_all, NN)
    return p, o, es * inv


def _attn_specs(t):
    cur = pl.BlockSpec((t, ATT_W * 2 + KV_W * 2), lambda i: (i, 0))
    halo_blocks = t // WINDOW
    prev = pl.BlockSpec((WINDOW, 2 * KV_W), lambda i: (jnp.maximum(i * halo_blocks - 1, 0), (2 * ATT_W) // (2 * KV_W)))
    return cur, prev


def _attn_fwd(pa, sinks, *, name, t=ATT_BLOCK, comm=None):
    l = pa.shape[0]
    t = min(t, l)
    nb = l // t
    c_args, c_in, c_out, c_shape, c_scratch = _comm_extra(comm)

    def body(sink_ref, cur_ref, prev_ref, ya_ref):
        i = pl.program_id(0)
        dist, valid = _attn_geometry(i, t)
        for h in range(N_HEADS):
            kh = h // Q_PER_KV
            q = cur_ref[:, h * HEAD_DIM:(h + 1) * HEAD_DIM]
            z = cur_ref[:, ATT_W + h * HEAD_DIM:ATT_W + (h + 1) * HEAD_DIM].astype(F32)
            k_all = jnp.concatenate([prev_ref[:, kh * HEAD_DIM:(kh + 1) * HEAD_DIM],
                                     cur_ref[:, 2 * ATT_W + kh * HEAD_DIM:2 * ATT_W + (kh + 1) * HEAD_DIM]], axis=0)
            v_all = jnp.concatenate([prev_ref[:, KV_W + kh * HEAD_DIM:KV_W + (kh + 1) * HEAD_DIM],
                                     cur_ref[:, 2 * ATT_W + KV_W + kh * HEAD_DIM:2 * ATT_W + KV_W + (kh + 1) * HEAD_DIM]], axis=0)
            _, o, _ = _attn_head(q, k_all, v_all, sink_ref[h], 2.0 ** (-(h + 1)), dist, valid)
            sz, _ = _silu_and_grad(z)
            ya_ref[:, h * HEAD_DIM:(h + 1) * HEAD_DIM] = (o * sz).astype(BF16)

    cur, prev = _attn_specs(t)
    res = pl.pallas_call(
        _with_comm(body, comm, 3, 1, nb, nb - 1), grid=(nb,),
        in_specs=[pl.BlockSpec(memory_space=pltpu.SMEM), cur, prev] + c_in,
        out_specs=[pl.BlockSpec((t, ATT_W), lambda i: (i, 0))] + c_out,
        out_shape=[jax.ShapeDtypeStruct((l, ATT_W), BF16)] + c_shape,
        scratch_shapes=c_scratch,
        compiler_params=_params("arbitrary"), name=name)(sinks, pa, pa, *c_args)
    return res[0], res[1:]


def _attn_bwd(pa, sinks, dya, *, name, t=SEQ_BLOCK):
    l = pa.shape[0]
    t = min(t, l)
    nb = l // t
    scale = 1.0 / math.sqrt(HEAD_DIM)

    def body(sink_ref, cur_ref, prev_ref, dya_ref, dpa_ref, dsink_ref, carry_ref):
        n = pl.program_id(0)
        i = nb - 1 - n
        dist, valid = _attn_geometry(i, t)

        @pl.when(n == 0)
        def _():
            carry_ref[...] = jnp.zeros_like(carry_ref)
            dsink_ref[...] = jnp.zeros_like(dsink_ref)

        dk_acc = [jnp.zeros((t + WINDOW, HEAD_DIM), F32) for _ in range(N_KV_HEADS)]
        dv_acc = [jnp.zeros((t + WINDOW, HEAD_DIM), F32) for _ in range(N_KV_HEADS)]
        for h in range(N_HEADS):
            kh = h // Q_PER_KV
            q = cur_ref[:, h * HEAD_DIM:(h + 1) * HEAD_DIM]
            z = cur_ref[:, ATT_W + h * HEAD_DIM:ATT_W + (h + 1) * HEAD_DIM].astype(F32)
            k_all = jnp.concatenate([prev_ref[:, kh * HEAD_DIM:(kh + 1) * HEAD_DIM],
                                     cur_ref[:, 2 * ATT_W + kh * HEAD_DIM:2 * ATT_W + (kh + 1) * HEAD_DIM]], axis=0)
            v_all = jnp.concatenate([prev_ref[:, KV_W + kh * HEAD_DIM:KV_W + (kh + 1) * HEAD_DIM],
                                     cur_ref[:, 2 * ATT_W + KV_W + kh * HEAD_DIM:2 * ATT_W + KV_W + (kh + 1) * HEAD_DIM]], axis=0)
            p, o, p_sink = _attn_head(q, k_all, v_all, sink_ref[h], 2.0 ** (-(h + 1)), dist, valid)
            dy = dya_ref[:, h * HEAD_DIM:(h + 1) * HEAD_DIM]
            sz, dsz = _silu_and_grad(z)
            do = dy * sz
            dpa_ref[:, ATT_W + h * HEAD_DIM:ATT_W + (h + 1) * HEAD_DIM] = (dy * o * dsz).astype(BF16)
            delta = jnp.sum(do * o, axis=-1, keepdims=True)
            dp = _dot(do, v_all, NT)
            ds = p * (dp - delta)
            dpa_ref[:, h * HEAD_DIM:(h + 1) * HEAD_DIM] = (_dot(ds, k_all, NN) * scale).astype(BF16)
            dk_acc[kh] = dk_acc[kh] + _dot(ds, q, TN) * scale
            dv_acc[kh] = dv_acc[kh] + _dot(p, do, TN)
            dsink_ref[h:h + 1, :] += jnp.broadcast_to(-jnp.sum(p_sink * delta, axis=0, keepdims=True), (1, 128))

        for kh in range(N_KV_HEADS):
            for which, acc in ((0, dk_acc[kh]), (1, dv_acc[kh])):
                c0 = which * KV_W + kh * HEAD_DIM
                own = acc[WINDOW:, :]
                tail = own[t - WINDOW:, :] + carry_ref[:, c0:c0 + HEAD_DIM]
                if t > WINDOW:
                    dpa_ref[0:t - WINDOW, 2 * ATT_W + c0:2 * ATT_W + c0 + HEAD_DIM] = own[:t - WINDOW, :].astype(BF16)
                dpa_ref[t - WINDOW:t, 2 * ATT_W + c0:2 * ATT_W + c0 + HEAD_DIM] = tail.astype(BF16)
                carry_ref[:, c0:c0 + HEAD_DIM] = acc[:WINDOW, :]

    halo_blocks = t // WINDOW
    wpa = 2 * ATT_W + 2 * KV_W
    cur = pl.BlockSpec((t, wpa), lambda n: (nb - 1 - n, 0))
    prev = pl.BlockSpec((WINDOW, 2 * KV_W),
                        lambda n: (jnp.maximum((nb - 1 - n) * halo_blocks - 1, 0), (2 * ATT_W) // (2 * KV_W)))
    return pl.pallas_call(
        body, grid=(nb,),
        in_specs=[pl.BlockSpec(memory_space=pltpu.SMEM), cur, prev, pl.BlockSpec((t, ATT_W), lambda n: (nb - 1 - n, 0))],
        out_specs=[pl.BlockSpec((t, wpa), lambda n: (nb - 1 - n, 0)), pl.BlockSpec((8, 128), lambda n: (0, 0))],
        out_shape=[jax.ShapeDtypeStruct((l, wpa), BF16), jax.ShapeDtypeStruct((8, 128), F32)],
        scratch_shapes=[pltpu.VMEM((WINDOW, 2 * KV_W), F32)],
        compiler_params=_params("arbitrary"), name=name)(sinks, pa, pa, dya)


def _scan(xr, xi, lr, li, t, reverse):
    row = lax.broadcasted_iota(jnp.int32, (t, 1), 0)
    d = 1
    pr, pi = lr, li
    while d < t:
        if reverse:
            sr = jnp.where(row < t - d, pltpu.roll(xr, t - d, 0), 0.0)
            si = jnp.where(row < t - d, pltpu.roll(xi, t - d, 0), 0.0)
        else:
            sr = jnp.where(row >= d, pltpu.roll(xr, d, 0), 0.0)
            si = jnp.where(row >= d, pltpu.roll(xi, d, 0), 0.0)
        xr, xi = xr + pr * sr - pi * si, xi + pr * si + pi * sr
        pr, pi = pr * pr - pi * pi, 2.0 * pr * pi
        d *= 2
    return xr, xi


SCAN_SUB = 8


def _split_hi_lo(a):
    hi = a.astype(BF16)
    lo = (a - hi.astype(F32)).astype(BF16)
    return jnp.concatenate([hi, lo], axis=0)


def _scan_mxu(xr, xi, tab, lam3, lam8, tri, expand, cr, ci, t, reverse):
    ns = t // SCAN_SUB
    n = xr.shape[1]
    v3 = lambda a: a.reshape(ns, SCAN_SUB, n)
    x3r, x3i = v3(xr), v3(xi)
    br = (x3r * tab[0] - x3i * tab[1]).reshape(t, n)
    bi = (x3r * tab[1] + x3i * tab[0]).reshape(t, n)
    pm = jnp.dot(tri, jnp.concatenate([br, bi], axis=1).astype(BF16), preferred_element_type=F32)
    p3r, p3i = v3(pm[:t, :n]), v3(pm[:t, n:])
    slr = p3r * tab[2] - p3i * tab[3]
    sli = p3r * tab[3] + p3i * tab[2]
    totr, toti = pm[t:, :n], pm[t:, n:]
    l3r, l3i = lam3
    l8r, l8i = lam8
    row = lax.broadcasted_iota(jnp.int32, (ns, 1), 0)
    edge = row == (ns - 1 if reverse else 0)
    er = totr * l3r - toti * l3i + jnp.where(edge, l8r * cr - l8i * ci, 0.0)
    ei = totr * l3i + toti * l3r + jnp.where(edge, l8r * ci + l8i * cr, 0.0)
    er, ei = _scan(er, ei, l8r, l8i, ns, reverse)
    shift = ns - 1 if reverse else 1
    nbr = jnp.where(edge, cr, pltpu.roll(er, shift, 0))
    nbi = jnp.where(edge, ci, pltpu.roll(ei, shift, 0))
    ex = jnp.dot(expand, _split_hi_lo(jnp.concatenate([nbr, nbi], axis=1)), preferred_element_type=F32)
    e3r, e3i = v3(ex[:, :n]), v3(ex[:, n:])
    sr = (slr + e3r * tab[4] - e3i * tab[5]).reshape(t, n)
    si = (sli + e3r * tab[5] + e3i * tab[4]).reshape(t, n)
    out = 0 if reverse else ns - 1
    return sr, si, er[out:out + 1, :], ei[out:out + 1, :]


def _scan_consts(t):
    import numpy as np
    ns = t // SCAN_SUB
    r = np.arange(t)
    same = (r[:, None] // SCAN_SUB) == (r[None, :] // SCAN_SUB)
    sums = (np.arange(ns)[:, None] == (r[None, :] // SCAN_SUB))
    tri = []
    for keep in (r[None, :] <= r[:, None], r[None, :] >= r[:, None]):
        tri.append(np.concatenate([same & keep, sums], axis=0).astype(np.float32))
    ex = ((r[:, None] // SCAN_SUB) == np.arange(ns)[None, :]).astype(np.float32)
    return jnp.asarray(np.stack(tri), BF16), jnp.asarray(np.concatenate([ex, ex], axis=1), BF16)


def _scan_tables(lr, li):
    den = lr * lr + li * li
    ir, ii = lr / den, -li / den
    mul = lambda a, b: (a[0] * b[0] - a[1] * b[1], a[0] * b[1] + a[1] * b[0])
    pw = {0: (jnp.ones_like(lr), jnp.zeros_like(lr))}
    for e in range(1, 9):
        pw[e] = mul(pw[e - 1], (lr, li))
    for e in range(-1, -5, -1):
        pw[e] = mul(pw[e + 1], (ir, ii))
    stack = lambda es, sign: (jnp.stack([pw[e][0] for e in es]), sign * jnp.stack([pw[e][1] for e in es]))
    j = range(SCAN_SUB)
    parts = [stack([4 - k for k in j], 1.0), stack([k - 4 for k in j], 1.0), stack([k + 1 for k in j], 1.0),
             stack([k - 3 for k in j], -1.0), stack([3 - k for k in j], -1.0), stack([8 - k for k in j], -1.0)]
    tabs = jnp.stack([a for pair in parts for a in pair])
    lam = jnp.zeros((8, lr.shape[0]), F32)
    for k, v in enumerate((lr, li, pw[3][0], pw[3][1], pw[8][0], pw[8][1])):
        lam = lam.at[k].set(v)
    return lam, tabs


SSM_HALVES = 2
SSM_HW = SSM_W // SSM_HALVES
SSM_HN = SSM_N // SSM_HALVES


def _bd_nn(x, w):
    a = w.shape[1]
    return jnp.concatenate([_dot(x[:, h * a:(h + 1) * a], w[h]) for h in range(SSM_HALVES)], axis=1)


def _bd_nt(x, w):
    b = w.shape[2]
    return jnp.concatenate([_dot(x[:, h * b:(h + 1) * b], w[h], NT) for h in range(SSM_HALVES)], axis=1)


def _bd_tn(x, y):
    a, b = x.shape[1] // SSM_HALVES, y.shape[1] // SSM_HALVES
    return jnp.stack([_dot(x[:, h * a:(h + 1) * a], y[:, h * b:(h + 1) * b], TN) for h in range(SSM_HALVES)])


def _ssm_states(u, s0r, s0i, lam_ref, tab_ref, tri_ref, ex_ref, bre, bim, t):
    tab = tuple(tab_ref[k] for k in range(6))
    return _scan_mxu(_bd_nn(u, bre), _bd_nn(u, bim), tab, (lam_ref[2:3, :], lam_ref[3:4, :]),
                     (lam_ref[4:5, :], lam_ref[5:6, :]), tri_ref[0], ex_ref[...], s0r, s0i, t, False)


def _ssm_head(u, z, xr, xi, cre, cim, dskip, wglu, bglu):
    y = _bd_nn(xr, cre) - _bd_nn(xi, cim) + dskip * u
    y2, dgelu = _gelu_and_grad(y)
    gate = _sigmoid(_dot(y2, wglu) + bglu)
    y3 = y2 * gate
    return y2, dgelu, gate, y3


def _with_comm(body, comm, n_in, n_out, grid, mid_step):
    if comm is None:
        return body
    nc = len(comm["args"])
    n_sem = len(comm["scratch"])
    grid = (grid,) if isinstance(grid, int) else tuple(grid)
    total = math.prod(grid)

    def hosted(*refs):
        ins, cin = refs[:n_in], refs[n_in:n_in + nc]
        outs, cout = refs[n_in + nc:n_in + nc + n_out], refs[n_in + nc + n_out:n_in + 2 * nc + n_out]
        rest = refs[n_in + 2 * nc + n_out:]
        scratch, csem = rest[:len(rest) - n_sem], rest[len(rest) - n_sem:]
        start, forward, finish = comm["phases"](cin, cout, *csem)
        step = pl.program_id(0)
        for axis in range(1, len(grid)):
            step = step * grid[axis] + pl.program_id(axis)
        pl.when(step == 0)(start)
        pl.when(step == (mid_step if mid_step >= 0 else total + mid_step))(forward)
        body(*ins, *outs, *scratch)
        pl.when(step == total - 1)(finish)

    return hosted


def _comm_extra(comm):
    if comm is None:
        return [], [], [], [], []
    anyspec = pl.BlockSpec(memory_space=pl.ANY)
    nc = len(comm["args"])
    return comm["args"], [anyspec] * nc, [anyspec] * nc, comm["out_shape"], comm["scratch"]


def _ssm_fwd(ps, scan_ops, bblk, cblk, dskip, wglu, bglu, *, name, t=SEQ_BLOCK, comm=None):
    l = ps.shape[0]
    assert l % t == 0
    nb = l // t
    ns = t // SCAN_SUB
    c_args, c_in, c_out, c_shape, c_scratch = _comm_extra(comm)

    def body(ps_ref, lam_ref, tab_ref, tri_ref, ex_ref, b_ref, c_ref, d_ref, w_ref, bg_ref, ys_ref, chk_ref, xs_ref,
             st_ref):
        @pl.when(pl.program_id(0) == 0)
        def _():
            st_ref[...] = jnp.zeros_like(st_ref)

        chk_ref[...] = jnp.broadcast_to(st_ref[...], chk_ref.shape)
        u = ps_ref[:, :SSM_W].astype(F32)
        z = ps_ref[:, SSM_W:].astype(F32)
        xr, xi, er, ei = _ssm_states(u, st_ref[:, :SSM_N], st_ref[:, SSM_N:], lam_ref, tab_ref, tri_ref, ex_ref,
                                     b_ref[0], b_ref[1], t)
        st_ref[:, :SSM_N] = er
        st_ref[:, SSM_N:] = ei
        xr, xi = xr.astype(BF16), xi.astype(BF16)
        xs_ref[:, :SSM_N] = xr
        xs_ref[:, SSM_N:] = xi
        _, _, _, y3 = _ssm_head(u, z, xr, xi, c_ref[0], c_ref[1], d_ref[...], w_ref[...], bg_ref[...])
        sz, _ = _silu_and_grad(z)
        ys_ref[...] = (y3 * sz).astype(BF16)

    full = lambda shape: pl.BlockSpec(shape, lambda i: (0,) * len(shape))
    return pl.pallas_call(
        _with_comm(body, comm, 10, 3, nb, nb - 1), grid=(nb,),
        in_specs=[pl.BlockSpec((t, 2 * SSM_W), lambda i: (i, 0)), full((8, SSM_N)), full((12, SCAN_SUB, SSM_N)),
                  full((2, t + ns, t)), full((t, 2 * ns)), full((2, SSM_HALVES, SSM_HW, SSM_HN)),
                  full((2, SSM_HALVES, SSM_HN, SSM_HW)), full((1, SSM_W)), full((SSM_W, SSM_W)), full((1, SSM_W))] + c_in,
        out_specs=[pl.BlockSpec((t, SSM_W), lambda i: (i, 0)), pl.BlockSpec((8, 2 * SSM_N), lambda i: (i, 0)),
                   pl.BlockSpec((t, 2 * SSM_N), lambda i: (i, 0))] + c_out,
        out_shape=[jax.ShapeDtypeStruct((l, SSM_W), BF16), jax.ShapeDtypeStruct((nb * 8, 2 * SSM_N), F32),
                   jax.ShapeDtypeStruct((l, 2 * SSM_N), BF16)] + c_shape,
        scratch_shapes=[pltpu.VMEM((1, 2 * SSM_N), F32)] + c_scratch,
        compiler_params=_params("arbitrary"), name=name)(ps, *scan_ops, bblk, cblk, dskip, wglu, bglu, *c_args)


def _ssm_bwd(ps, dys, chk, states, scan_ops, bblk, cblk, dskip, wglu, bglu, *, name, t=SEQ_BLOCK, comm=None):
    l = ps.shape[0]
    assert l % t == 0
    nb = l // t
    ns = t // SCAN_SUB
    c_args, c_in, c_out, c_shape, c_scratch = _comm_extra(comm)

    def body(ps_ref, dys_ref, chk_ref, xs_ref, lam_ref, tab_ref, tri_ref, ex_ref, b_ref, c_ref, d_ref, w_ref, bg_ref,
             dps_ref, db_ref, dc_ref, dw_acc, sums_acc, gc_ref, db_acc, dc_acc):
        n = pl.program_id(0)

        @pl.when(n == 0)
        def _():
            gc_ref[...] = jnp.zeros_like(gc_ref)
            db_acc[...] = jnp.zeros_like(db_acc)
            dc_acc[...] = jnp.zeros_like(dc_acc)
            dw_acc[...] = jnp.zeros_like(dw_acc)
            sums_acc[...] = jnp.zeros_like(sums_acc)

        row = lax.broadcasted_iota(jnp.int32, (t, 1), 0)
        u = ps_ref[:, :SSM_W].astype(F32)
        z = ps_ref[:, SSM_W:].astype(F32)
        s0r, s0i = chk_ref[0:1, :SSM_N], chk_ref[0:1, SSM_N:]
        xr, xi = xs_ref[:, :SSM_N], xs_ref[:, SSM_N:]
        dskip = d_ref[...]
        y2, dgelu, gate, y3 = _ssm_head(u, z, xr, xi, c_ref[0], c_ref[1], dskip, w_ref[...], bg_ref[...])
        sz, dsz = _silu_and_grad(z)
        dys_v = dys_ref[...]
        dps_ref[:, SSM_W:] = (dys_v * y3 * dsz).astype(BF16)
        dy3 = dys_v * sz
        da = dy3 * y2 * gate * (1.0 - gate)
        dy2 = dy3 * gate + _dot(da, w_ref[...], NT)
        dw_acc[...] += _dot(y2, da, TN)
        dy = dy2 * dgelu
        sums_acc[2:3, :SSM_W] += jnp.sum(dy * u, axis=0, keepdims=True)
        sums_acc[3:4, :SSM_W] += jnp.sum(da, axis=0, keepdims=True)
        dc_acc[0] += _bd_tn(xr, dy)
        dc_acc[1] += -_bd_tn(xi, dy)
        rev_tab = tuple(tab_ref[k] for k in range(6, 12))
        gr, gi, gcr, gci = _scan_mxu(
            _bd_nt(dy, c_ref[0]), -_bd_nt(dy, c_ref[1]), rev_tab, (lam_ref[2:3, :], -lam_ref[3:4, :]),
            (lam_ref[4:5, :], -lam_ref[5:6, :]), tri_ref[1], ex_ref[...], gc_ref[:, :SSM_N], gc_ref[:, SSM_N:], t, True)
        gc_ref[:, :SSM_N] = gcr
        gc_ref[:, SSM_N:] = gci
        db_acc[0] += _bd_tn(u, gr)
        db_acc[1] += _bd_tn(u, gi)
        du = dskip * dy + _bd_nt(gr, b_ref[0]) + _bd_nt(gi, b_ref[1])
        dps_ref[:, :SSM_W] = du.astype(BF16)
        spr = jnp.where(row == 0, s0r, pltpu.roll(xr.astype(F32), 1, 0))
        spi = jnp.where(row == 0, s0i, pltpu.roll(xi.astype(F32), 1, 0))
        sums_acc[0:1, :] += jnp.sum(gr * spr + gi * spi, axis=0, keepdims=True)
        sums_acc[1:2, :] += jnp.sum(gi * spr - gr * spi, axis=0, keepdims=True)

        @pl.when(n == nb - 1)
        def _():
            per_half = SSM_GROUPS // SSM_HALVES
            for k in range(2):
                for g in range(SSM_GROUPS):
                    h, gl = divmod(g, per_half)
                    c0, p0 = gl * SSM_GROUP, gl * SSM_STATE
                    db_ref[k, g * SSM_GROUP:(g + 1) * SSM_GROUP, :] = db_acc[k, h, c0:c0 + SSM_GROUP, p0:p0 + SSM_STATE]
                    dc_ref[k, g * SSM_STATE:(g + 1) * SSM_STATE, :] = dc_acc[k, h, p0:p0 + SSM_STATE, c0:c0 + SSM_GROUP]

    full = lambda shape: pl.BlockSpec(shape, lambda n: (0,) * len(shape))
    return pl.pallas_call(
        _with_comm(body, comm, 13, 5, nb, 0), grid=(nb,),
        in_specs=[pl.BlockSpec((t, 2 * SSM_W), lambda n: (nb - 1 - n, 0)),
                  pl.BlockSpec((t, SSM_W), lambda n: (nb - 1 - n, 0)),
                  pl.BlockSpec((8, 2 * SSM_N), lambda n: (nb - 1 - n, 0)),
                  pl.BlockSpec((t, 2 * SSM_N), lambda n: (nb - 1 - n, 0)),
                  full((8, SSM_N)), full((12, SCAN_SUB, SSM_N)), full((2, t + ns, t)), full((t, 2 * ns)),
                  full((2, SSM_HALVES, SSM_HW, SSM_HN)), full((2, SSM_HALVES, SSM_HN, SSM_HW)), full((1, SSM_W)),
                  full((SSM_W, SSM_W)), full((1, SSM_W))] + c_in,
        out_specs=[pl.BlockSpec((t, 2 * SSM_W), lambda n: (nb - 1 - n, 0)), full((2, SSM_W, SSM_STATE)),
                   full((2, SSM_N, SSM_GROUP)), full((SSM_W, SSM_W)), full((8, SSM_N))] + c_out,
        out_shape=[jax.ShapeDtypeStruct((l, 2 * SSM_W), BF16),
                   jax.ShapeDtypeStruct((2, SSM_W, SSM_STATE), F32),
                   jax.ShapeDtypeStruct((2, SSM_N, SSM_GROUP), F32),
                   jax.ShapeDtypeStruct((SSM_W, SSM_W), F32),
                   jax.ShapeDtypeStruct((8, SSM_N), F32)] + c_shape,
        scratch_shapes=[pltpu.VMEM((1, 2 * SSM_N), F32), pltpu.VMEM((2, SSM_HALVES, SSM_HW, SSM_HN), F32),
                        pltpu.VMEM((2, SSM_HALVES, SSM_HN, SSM_HW), F32)] + c_scratch,
        compiler_params=_params("arbitrary"), name=name)(ps, dys, chk, states, *scan_ops, bblk, cblk, dskip, wglu, bglu,
                                                         *c_args)


def _pool_count(i, t):
    pos = lax.broadcasted_iota(jnp.int32, (t, POOL_W), 0) + i * t + 1
    col = lax.broadcasted_iota(jnp.int32, (t, POOL_W), 1)
    win = jnp.where(col < POOL_GW, 2, jnp.where(col < 2 * POOL_GW, 4, jnp.where(col < 3 * POOL_GW, 8, 16)))
    return 1.0 / jnp.minimum(pos, win).astype(F32), col


def _window_sums(ext, n_rows, forward):
    col = lax.broadcasted_iota(jnp.int32, ext.shape, 1)
    sh = (lambda a, d: pltpu.roll(a, d, 0)) if forward else (lambda a, d: pltpu.roll(a, n_rows - d, 0))
    a2 = ext + sh(ext, 1)
    a4 = a2 + sh(a2, 2)
    a8 = a4 + sh(a4, 4)
    a16 = a8 + sh(a8, 8)
    return jnp.where(col < POOL_GW, a2, jnp.where(col < 2 * POOL_GW, a4, jnp.where(col < 3 * POOL_GW, a8, a16)))


def _pool_mix(pooled, wp_ref):
    return jnp.concatenate([_dot(pooled[:, g * POOL_GW:(g + 1) * POOL_GW], wp_ref[g]) for g in range(4)], axis=1)


def _pool_pooled(i, cur_u, prev_u, t):
    prev = jnp.where(i > 0, prev_u, 0.0)
    ext = jnp.concatenate([prev, cur_u], axis=0)
    inv_cnt, _ = _pool_count(i, t)
    return _window_sums(ext, t + POOL_HALO, True)[POOL_HALO:, :] * inv_cnt - cur_u


def _pool_fwd(pp, wpool, pscale, *, name, t=SEQ_BLOCK):
    l = pp.shape[0]
    t = min(t, l)

    def body(cur_ref, prev_ref, wp_ref, sc_ref, yp_ref):
        i = pl.program_id(0)
        pooled = _pool_pooled(i, cur_ref[:, :POOL_W].astype(F32), prev_ref[...].astype(F32), t)
        lin = _pool_mix(pooled, wp_ref)
        sz, _ = _silu_and_grad(cur_ref[:, POOL_W:].astype(F32))
        yp_ref[...] = (lin * sc_ref[...] * sz).astype(BF16)

    hb = t // POOL_HALO
    return pl.pallas_call(
        body, grid=(l // t,),
        in_specs=[pl.BlockSpec((t, 2 * POOL_W), lambda i: (i, 0)),
                  pl.BlockSpec((POOL_HALO, POOL_W), lambda i: (jnp.maximum(i * hb - 1, 0), 0)),
                  pl.BlockSpec((4, POOL_GW, POOL_GW), lambda i: (0, 0, 0)),
                  pl.BlockSpec((1, POOL_W), lambda i: (0, 0))],
        out_specs=pl.BlockSpec((t, POOL_W), lambda i: (i, 0)),
        out_shape=jax.ShapeDtypeStruct((l, POOL_W), BF16),
        compiler_params=_params("parallel"), name=name)(pp, pp, wpool, pscale)


def _pool_bwd(pp, dyp, wpool, pscale, *, name, t=SEQ_BLOCK):
    l = pp.shape[0]
    t = min(t, l)
    nb = l // t

    def body(cur_ref, prev_ref, dyp_ref, wp_ref, sc_ref, dpp_ref, dwp_ref, sums_ref, carry_ref):
        n = pl.program_id(0)
        i = nb - 1 - n

        @pl.when(n == 0)
        def _():
            carry_ref[...] = jnp.zeros_like(carry_ref)
            dwp_ref[...] = jnp.zeros_like(dwp_ref)
            sums_ref[...] = jnp.zeros_like(sums_ref)

        cur_u = cur_ref[:, :POOL_W].astype(F32)
        pooled = _pool_pooled(i, cur_u, prev_ref[...].astype(F32), t)
        lin = _pool_mix(pooled, wp_ref)
        sz, dsz = _silu_and_grad(cur_ref[:, POOL_W:].astype(F32))
        dyp_v = dyp_ref[...]
        scale = sc_ref[...]
        dpp_ref[:, POOL_W:] = (dyp_v * lin * scale * dsz).astype(BF16)
        dpre = dyp_v * sz
        sums_ref[0:1, :] += jnp.sum(dpre * lin, axis=0, keepdims=True)
        dlin = dpre * scale
        dpooled = []
        for g in range(4):
            dl = dlin[:, g * POOL_GW:(g + 1) * POOL_GW]
            dwp_ref[g] += _dot(pooled[:, g * POOL_GW:(g + 1) * POOL_GW], dl, TN)
            dpooled.append(_dot(dl, wp_ref[g], NT))
        dpooled = jnp.concatenate(dpooled, axis=1)
        inv_cnt, _ = _pool_count(i, t)
        dq = dpooled * inv_cnt
        ext = jnp.concatenate([dq, carry_ref[...]], axis=0)
        du = _window_sums(ext, t + POOL_HALO, False)[:t, :] - dpooled
        dpp_ref[:, :POOL_W] = du.astype(BF16)
        carry_ref[...] = dq[:POOL_HALO, :]

    hb = t // POOL_HALO
    return pl.pallas_call(
        body, grid=(nb,),
        in_specs=[pl.BlockSpec((t, 2 * POOL_W), lambda n: (nb - 1 - n, 0)),
                  pl.BlockSpec((POOL_HALO, POOL_W), lambda n: (jnp.maximum((nb - 1 - n) * hb - 1, 0), 0)),
                  pl.BlockSpec((t, POOL_W), lambda n: (nb - 1 - n, 0)),
                  pl.BlockSpec((4, POOL_GW, POOL_GW), lambda n: (0, 0, 0)),
                  pl.BlockSpec((1, POOL_W), lambda n: (0, 0))],
        out_specs=[pl.BlockSpec((t, 2 * POOL_W), lambda n: (nb - 1 - n, 0)),
                   pl.BlockSpec((4, POOL_GW, POOL_GW), lambda n: (0, 0, 0)),
                   pl.BlockSpec((8, POOL_W), lambda n: (0, 0))],
        out_shape=[jax.ShapeDtypeStruct((l, 2 * POOL_W), BF16), jax.ShapeDtypeStruct((4, POOL_GW, POOL_GW), F32),
                   jax.ShapeDtypeStruct((8, POOL_W), F32)],
        scratch_shapes=[pltpu.VMEM((POOL_HALO, POOL_W), F32)],
        compiler_params=_params("arbitrary"), name=name)(pp, pp, dyp, wpool, pscale)


def _merge_fwd(ya, ys, yp, wa, ws, wp, pg, *, name, tm=512):
    l = ya.shape[0]
    tm = min(tm, l)
    d = D_MODEL

    def body(ya_ref, ys_ref, yp_ref, wa_ref, ws_ref, wp_ref, pg_ref, mg_ref, ba_ref, bs_ref, bp_ref):
        acc = None
        for k, (y_ref, w_ref, b_ref) in enumerate(((ya_ref, wa_ref, ba_ref), (ys_ref, ws_ref, bs_ref),
                                                   (yp_ref, wp_ref, bp_ref))):
            br = _dot(y_ref[...], w_ref[...])
            b_ref[...] = br.astype(BF16)
            term = _sigmoid(pg_ref[:, k * d:(k + 1) * d].astype(F32)) * br
            acc = term if acc is None else acc + term
        mg_ref[...] = acc.astype(BF16)

    rowy = pl.BlockSpec((tm, ATT_W), lambda i: (i, 0))
    wsp = pl.BlockSpec((ATT_W, d), lambda i: (0, 0))
    rowd = pl.BlockSpec((tm, d), lambda i: (i, 0))
    return pl.pallas_call(
        body, grid=(l // tm,),
        in_specs=[rowy, rowy, rowy, wsp, wsp, wsp, pl.BlockSpec((tm, 3 * d), lambda i: (i, 0))],
        out_specs=[rowd, rowd, rowd, rowd],
        out_shape=[jax.ShapeDtypeStruct((l, d), BF16)] * 4,
        compiler_params=_params("parallel"), name=name)(ya, ys, yp, wa, ws, wp, pg)


def _out_fwd(merged, wout, x, gate, *, name, tm=512):
    l, d = x.shape
    tm = min(tm, l)

    def body(m_ref, w_ref, x_ref, g_ref, xn_ref, out_ref):
        out = _dot(m_ref[...], w_ref[...])
        out_ref[...] = out.astype(BF16)
        xn_ref[...] = x_ref[...] + g_ref[...] * out

    row = pl.BlockSpec((tm, d), lambda i: (i, 0))
    return pl.pallas_call(
        body, grid=(l // tm,),
        in_specs=[row, pl.BlockSpec((d, d), lambda i: (0, 0)), row, pl.BlockSpec((1, d), lambda i: (0, 0))],
        out_specs=[row, row],
        out_shape=[jax.ShapeDtypeStruct((l, d), F32), jax.ShapeDtypeStruct((l, d), BF16)],
        compiler_params=_params("parallel"), name=name)(merged, wout, x, gate)


def _merge_bwd(dx, out, gate, wout, pg, ba, bs, bp, *, name, tm=512):
    l, d = dx.shape
    tm = min(tm, l)

    def body(dx_ref, out_ref, g_ref, w_ref, pg_ref, ba_ref, bs_ref, bp_ref,
             dmo_ref, dba_ref, dbs_ref, dbp_ref, dpg_ref, sums_ref):
        @pl.when(pl.program_id(0) == 0)
        def _():
            sums_ref[...] = jnp.zeros_like(sums_ref)

        dxv = dx_ref[...]
        sums_ref[0:1, :] += jnp.sum(dxv * out_ref[...].astype(F32), axis=0, keepdims=True)
        dmo = (dxv * g_ref[...]).astype(BF16)
        dmo_ref[...] = dmo
        dmerged = _dot(dmo, w_ref[...], NT)
        for k, (b_ref, db_ref) in enumerate(((ba_ref, dba_ref), (bs_ref, dbs_ref), (bp_ref, dbp_ref))):
            gk = _sigmoid(pg_ref[:, k * d:(k + 1) * d].astype(F32))
            db_ref[...] = (dmerged * gk).astype(BF16)
            dpg_ref[:, k * d:(k + 1) * d] = (dmerged * b_ref[...].astype(F32) * gk * (1.0 - gk)).astype(BF16)

    row = pl.BlockSpec((tm, d), lambda i: (i, 0))
    wide = pl.BlockSpec((tm, 3 * d), lambda i: (i, 0))
    return pl.pallas_call(
        body, grid=(l // tm,),
        in_specs=[row, row, pl.BlockSpec((1, d), lambda i: (0, 0)), pl.BlockSpec((d, d), lambda i: (0, 0)),
                  wide, row, row, row],
        out_specs=[row, row, row, row, wide, pl.BlockSpec((8, d), lambda i: (0, 0))],
        out_shape=[jax.ShapeDtypeStruct((l, d), BF16)] * 4 + [jax.ShapeDtypeStruct((l, 3 * d), BF16),
                                                             jax.ShapeDtypeStruct((8, d), F32)],
        compiler_params=_params("arbitrary"), name=name)(dx, out, gate, wout, pg, ba, bs, bp)


def _adamw(w, g, m, v, *, name, tr=256):
    r, c = w.shape
    p = g.shape[0]
    tr = min(tr, r)
    assert r % tr == 0
    c1 = 1.0 / (1.0 - ADAM_B1 ** ADAM_STEP)
    c2 = 1.0 / (1.0 - ADAM_B2 ** ADAM_STEP)

    def body(w_ref, g_ref, m_ref, v_ref, go_ref, d_ref, mo_ref, vo_ref):
        gv = g_ref[0].astype(F32)
        for k in range(1, p):
            gv = gv + g_ref[k].astype(F32)
        go_ref[...] = gv
        mn = ADAM_B1 * m_ref[...] + (1.0 - ADAM_B1) * gv
        vn = ADAM_B2 * v_ref[...] + (1.0 - ADAM_B2) * (gv * gv)
        mo_ref[...] = mn
        vo_ref[...] = vn
        d_ref[...] = -ADAM_LR * ((mn * c1) / (jnp.sqrt(vn * c2) + ADAM_EPS) + ADAM_WD * w_ref[...])

    row = pl.BlockSpec((tr, c), lambda i: (i, 0))
    return pl.pallas_call(
        body, grid=(r // tr,),
        in_specs=[row, pl.BlockSpec((p, tr, c), lambda i: (0, i, 0)), row, row],
        out_specs=[row] * 4,
        out_shape=[jax.ShapeDtypeStruct((r, c), F32)] * 4,
        compiler_params=_params("parallel"), name=name)(w, g, m, v)


def _exchange(arrs, *, scatter, name):
    n = len(arrs)
    out_shape = [jax.ShapeDtypeStruct(a.shape if scatter else (N_DEV,) + a.shape, a.dtype) for a in arrs]

    def body(*refs):
        ins, outs = refs[:n], refs[n:2 * n]
        send_sems, recv_sems, loc_sems = refs[2 * n:]
        me = 4 * lax.axis_index("x") + 2 * lax.axis_index("y") + lax.axis_index("c")
        local = []
        for k in range(n):
            src = ins[k].at[me] if scatter else ins[k]
            cp = pltpu.make_async_copy(src, outs[k].at[me], loc_sems.at[k])
            cp.start()
            local.append(cp)
        remote = []
        for r in range(1, N_DEV):
            peer = me ^ r
            for k in range(n):
                src = ins[k].at[peer] if scatter else ins[k]
                cp = pltpu.make_async_remote_copy(
                    src_ref=src, dst_ref=outs[k].at[me], send_sem=send_sems.at[k, r - 1], recv_sem=recv_sems.at[k, r - 1],
                    device_id=(peer // 4, (peer // 2) % 2, peer % 2), device_id_type=pl.DeviceIdType.MESH)
                cp.start()
                remote.append(cp)
        for cp in remote:
            cp.wait()
        for cp in local:
            cp.wait()

    anyspec = pl.BlockSpec(memory_space=pl.ANY)
    return pl.pallas_call(
        body, in_specs=[anyspec] * n, out_specs=[anyspec] * n, out_shape=out_shape,
        scratch_shapes=[pltpu.SemaphoreType.DMA((n, N_DEV - 1)), pltpu.SemaphoreType.DMA((n, N_DEV - 1)),
                        pltpu.SemaphoreType.DMA((n,))],
        name=name)(*arrs)


def _mesh_place():
    x, y, c = lax.axis_index("x"), lax.axis_index("y"), lax.axis_index("c")
    other_chips = [(1 - x, y), (x, 1 - y), (1 - x, 1 - y)]
    return x, y, c, other_chips


def _gather_two_level(arrs, *, name):
    n = len(arrs)
    plan = _gather_plan(arrs)

    def body(*refs):
        start, forward, finish = plan["phases"](refs[:n], refs[n:2 * n], *refs[2 * n:])
        start()
        forward()
        finish()

    anyspec = pl.BlockSpec(memory_space=pl.ANY)
    return pl.pallas_call(
        body, in_specs=[anyspec] * n, out_specs=[anyspec] * n, out_shape=plan["out_shape"],
        scratch_shapes=plan["scratch"], name=name)(*arrs)


def _gather_plan(arrs):
    n = len(arrs)

    def phases(ins, outs, send_sems, recv_sems, loc_sems):
        x, y, c, chips = _mesh_place()
        me = 4 * x + 2 * y + c
        slot = lambda px, py, pc: 4 * px + 2 * py + pc

        def copy(k, j, src, block, to):
            return pltpu.make_async_remote_copy(
                src_ref=src, dst_ref=outs[k].at[block], send_sem=send_sems.at[k, j], recv_sem=recv_sems.at[k, j],
                device_id=to, device_id_type=pl.DeviceIdType.MESH)

        local = [pltpu.make_async_copy(ins[k], outs[k].at[me], loc_sems.at[k]) for k in range(n)]
        first = []
        for k in range(n):
            first.append(copy(k, 0, ins[k], me, (x, y, 1 - c)))
            for j, chip in enumerate(chips):
                first.append(copy(k, 1 + j, ins[k], me, (*chip, c)))
        passed = [copy(k, 4 + j, outs[k].at[slot(*chip, c)], slot(*chip, c), (x, y, 1 - c))
                  for j, chip in enumerate(chips) for k in range(n)]

        def start():
            for cp in local + first:
                cp.start()

        def forward():
            for j, chip in enumerate(chips):
                for k in range(n):
                    copy(k, 1 + j, ins[k], slot(*chip, c), (x, y, c)).wait_recv()
                    passed[j * n + k].start()

        def finish():
            for k in range(n):
                copy(k, 0, ins[k], slot(x, y, 1 - c), (x, y, c)).wait_recv()
                for j, chip in enumerate(chips):
                    copy(k, 4 + j, ins[k], slot(*chip, 1 - c), (x, y, c)).wait_recv()
            for cp in first + passed:
                cp.wait_send()
            for cp in local:
                cp.wait()

        return start, forward, finish

    return dict(
        args=list(arrs), out_shape=[jax.ShapeDtypeStruct((N_DEV,) + a.shape, a.dtype) for a in arrs],
        scratch=[pltpu.SemaphoreType.DMA((n, 7)), pltpu.SemaphoreType.DMA((n, 7)), pltpu.SemaphoreType.DMA((n,))],
        phases=phases)


def _allreduce_small(small, extra, *, name):
    r, lanes = small.shape
    assert r % 16 == 0
    h = r // 2
    e = extra.shape[0]

    def body(s_ref, x_ref, out_ref, xall_ref, sib_ref, parts_ref, send_sems, recv_sems):
        x, y, c, chips = _mesh_place()
        me = 4 * x + 2 * y + c
        my_chip = 2 * x + y
        sibling = (x, y, 1 - c)
        mine = pl.ds(pl.multiple_of(c * h, 8), h)
        theirs = pl.ds(pl.multiple_of((1 - c) * h, 8), h)

        def remote(j, src, dst, to):
            return pltpu.make_async_remote_copy(src_ref=src, dst_ref=dst, send_sem=send_sems.at[j],
                                                recv_sem=recv_sems.at[j], device_id=to, device_id_type=pl.DeviceIdType.MESH)

        to_sibling = remote(0, s_ref.at[theirs], sib_ref, sibling)
        to_sibling.start()
        xall_ref[me] = x_ref[...]
        extras = []
        for rr in range(1, N_DEV):
            peer = me ^ rr
            cp = remote(4 + rr, x_ref, xall_ref.at[me], (peer // 4, (peer // 2) % 2, peer % 2))
            cp.start()
            extras.append(cp)
        to_sibling.wait_recv()
        parts_ref[my_chip] = s_ref[mine] + sib_ref[...]
        to_chips = [remote(1 + j, parts_ref.at[my_chip], parts_ref.at[my_chip], (px, py, c))
                    for j, (px, py) in enumerate(chips)]
        for cp in to_chips:
            cp.start()
        for cp in to_chips:
            cp.wait_recv()
        out_ref[mine] = (parts_ref[0] + parts_ref[1]) + (parts_ref[2] + parts_ref[3])
        done = remote(4, out_ref.at[mine], out_ref.at[mine], sibling)
        done.start()
        remote(4, out_ref.at[theirs], out_ref.at[theirs], sibling).wait_recv()
        for cp in extras:
            cp.wait()
        to_sibling.wait_send()
        for cp in to_chips:
            cp.wait_send()
        done.wait_send()

    vmem = pl.BlockSpec(memory_space=pltpu.VMEM)
    return pl.pallas_call(
        body, in_specs=[vmem, vmem], out_specs=[vmem, vmem],
        out_shape=[jax.ShapeDtypeStruct((r, lanes), F32), jax.ShapeDtypeStruct((N_DEV, e, lanes), F32)],
        scratch_shapes=[pltpu.VMEM((h, lanes), F32), pltpu.VMEM((4, h, lanes), F32),
                        pltpu.SemaphoreType.DMA((12,)), pltpu.SemaphoreType.DMA((12,))],
        compiler_params=pltpu.CompilerParams(vmem_limit_bytes=VMEM_LIMIT), name=name)(small, extra)


def _sibling_swap(arrs, *, name):
    n = len(arrs)
    out_shape = [jax.ShapeDtypeStruct(a.shape[1:], a.dtype) for a in arrs]

    def body(*refs):
        ins, outs = refs[:n], refs[n:2 * n]
        send_sems, recv_sems = refs[2 * n:]
        x, y, c, _ = _mesh_place()
        copies = [pltpu.make_async_remote_copy(
            src_ref=ins[k].at[1 - c], dst_ref=outs[k], send_sem=send_sems.at[k], recv_sem=recv_sems.at[k],
            device_id=(x, y, 1 - c), device_id_type=pl.DeviceIdType.MESH) for k in range(n)]
        for cp in copies:
            cp.start()
        for cp in copies:
            cp.wait()

    anyspec = pl.BlockSpec(memory_space=pl.ANY)
    return pl.pallas_call(
        body, in_specs=[anyspec] * n, out_specs=[anyspec] * n, out_shape=out_shape,
        scratch_shapes=[pltpu.SemaphoreType.DMA((n,)), pltpu.SemaphoreType.DMA((n,))], name=name)(*arrs)


def _pair_add(mine, theirs, core, *, name, tr=256):
    _, r, c = mine.shape
    tr = min(tr, r)
    assert r % tr == 0

    def body(core_ref, m_ref, t_ref, o_ref):
        o_ref[...] = (m_ref[0].astype(F32) + t_ref[...].astype(F32)).astype(BF16)

    return pl.pallas_call(
        body,
        grid_spec=pltpu.PrefetchScalarGridSpec(
            num_scalar_prefetch=1, grid=(r // tr,),
            in_specs=[pl.BlockSpec((1, tr, c), lambda i, core_ref: (core_ref[0], i, 0)),
                      pl.BlockSpec((tr, c), lambda i, core_ref: (i, 0))],
            out_specs=pl.BlockSpec((tr, c), lambda i, core_ref: (i, 0))),
        out_shape=jax.ShapeDtypeStruct((r, c), BF16),
        compiler_params=_params("parallel"), name=name)(core, mine, theirs)


def _chip_scatter(arrs, *, name):
    n = len(arrs)
    plan = _chip_scatter_plan(arrs)

    def body(*refs):
        start, _, finish = plan["phases"](refs[:n], refs[n:2 * n], *refs[2 * n:])
        start()
        finish()

    anyspec = pl.BlockSpec(memory_space=pl.ANY)
    return pl.pallas_call(
        body, in_specs=[anyspec] * n, out_specs=[anyspec] * n, out_shape=plan["out_shape"],
        scratch_shapes=plan["scratch"], name=name)(*arrs)


def _chip_scatter_plan(arrs):
    n = len(arrs)

    def phases(ins, outs, send_sems, recv_sems, loc_sems):
        x, y, c, chips = _mesh_place()
        mine = 2 * x + y
        local = [pltpu.make_async_copy(ins[k].at[mine], outs[k].at[mine], loc_sems.at[k]) for k in range(n)]
        remote = [pltpu.make_async_remote_copy(
            src_ref=ins[k].at[2 * px + py], dst_ref=outs[k].at[mine], send_sem=send_sems.at[k, j],
            recv_sem=recv_sems.at[k, j], device_id=(px, py, c), device_id_type=pl.DeviceIdType.MESH)
            for j, (px, py) in enumerate(chips) for k in range(n)]

        def start():
            for cp in local + remote:
                cp.start()

        def finish():
            for cp in remote:
                cp.wait()
            for cp in local:
                cp.wait()

        return start, (lambda: None), finish

    return dict(
        args=list(arrs), out_shape=[jax.ShapeDtypeStruct(a.shape, a.dtype) for a in arrs],
        scratch=[pltpu.SemaphoreType.DMA((n, 3)), pltpu.SemaphoreType.DMA((n, 3)), pltpu.SemaphoreType.DMA((n,))],
        phases=phases)


def _ssm_discretize(a_re, a_im, log_dt, b_re, b_im):
    dt = jnp.exp(log_dt)[:, None]
    mag = jnp.exp(a_re * dt)
    lr = mag * jnp.cos(a_im * dt)
    li = mag * jnp.sin(a_im * dt)
    den = a_re * a_re + a_im * a_im
    cr = ((lr - 1.0) * a_re + li * a_im) / den
    ci = (li * a_re - (lr - 1.0) * a_im) / den
    bbr = cr[..., None] * b_re - ci[..., None] * b_im
    bbi = cr[..., None] * b_im + ci[..., None] * b_re
    return lr, li, bbr, bbi


def _ssm_dense(lr, li, bbr, bbi, c_re, c_im):
    scan_ops = _scan_tables(lr.reshape(-1), li.reshape(-1)) + _scan_consts(SEQ_BLOCK)
    per_half = SSM_GROUPS // SSM_HALVES

    def halves(a, rows, cols):
        a = a.reshape(SSM_HALVES, per_half * rows, cols)
        tiled = jnp.tile(a, (1, 1, per_half))
        r = lax.broadcasted_iota(jnp.int32, tiled.shape, 1) // rows
        c = lax.broadcasted_iota(jnp.int32, tiled.shape, 2) // cols
        return jnp.where(r == c, tiled, 0.0)

    bblk = jnp.stack([halves(b.transpose(0, 2, 1), SSM_GROUP, SSM_STATE) for b in (bbr, bbi)]).astype(BF16)
    cblk = jnp.stack([halves(c.transpose(0, 2, 1), SSM_STATE, SSM_GROUP) for c in (c_re, c_im)]).astype(BF16)
    return scan_ops, bblk, cblk


def _ssm_extract(db, dc, sums):
    db = db.reshape(2, SSM_GROUPS, SSM_GROUP, SSM_STATE).transpose(0, 1, 3, 2)
    dc = dc.reshape(2, SSM_GROUPS, SSM_STATE, SSM_GROUP).transpose(0, 1, 3, 2)
    dlr = sums[0].reshape(SSM_GROUPS, SSM_STATE)
    dli = sums[1].reshape(SSM_GROUPS, SSM_STATE)
    return dlr, dli, db[0], db[1], dc[0], dc[1]


IN_SPLITS = (ATT_W, KV_W, KV_W, SSM_W, POOL_W, ATT_W, SSM_W, POOL_W, 3 * D_MODEL)


def _split_w_in(w):
    idx = [0]
    for s in IN_SPLITS:
        idx.append(idx[-1] + s)
    seg = [w[..., idx[k]:idx[k + 1]] for k in range(len(IN_SPLITS))]
    q, k, v, us, up, za, zs, zp, gl = seg
    return (jnp.concatenate([q, za, k, v], axis=-1), jnp.concatenate([us, zs], axis=-1),
            jnp.concatenate([up, zp], axis=-1), gl)


def _merge_w_in(da, ds, dp, dg):
    q, za, k, v = da[..., :ATT_W], da[..., ATT_W:2 * ATT_W], da[..., 2 * ATT_W:2 * ATT_W + KV_W], da[..., 2 * ATT_W + KV_W:]
    us, zs = ds[..., :SSM_W], ds[..., SSM_W:]
    up, zp = dp[..., :POOL_W], dp[..., POOL_W:]
    return jnp.concatenate([q, k, v, us, up, za, zs, zp, dg], axis=-1)


def _layer_fwd(x, lw, li, late=None, comm_attn=None, comm_ssm=None):
    tag = f"l{li}"
    h = _ln_fwd(x, lw["norm_g"], lw["shift"], lw["scale"], name=f"ln_fwd_{tag}")
    pa = _mm(h, lw["w_a"], tn=1280, out_dtype=BF16, name=f"proj_a_{tag}")
    ps = _mm(h, lw["w_s"], out_dtype=BF16, name=f"proj_s_{tag}")
    pp = _mm(h, lw["w_p"], out_dtype=BF16, name=f"proj_p_{tag}")
    if late is None:
        pg = _mm(h, lw["w_g"], out_dtype=BF16, name=f"proj_g_{tag}")
    else:
        pg, arrived = _mm(h, lw["w_g"], out_dtype=BF16, name=f"proj_g_{tag}", comm=late[0])
        lw = {**lw, **late[1](arrived)}
    ya, from_attn = _attn_fwd(pa, lw["sinks"], name=f"attn_fwd_{tag}", comm=comm_attn)
    ys, chk, states, *from_ssm = _ssm_fwd(ps, lw["lam"], lw["bblk"], lw["cblk"], lw["ssm_d"], lw["w_glu"], lw["b_glu"],
                                          name=f"ssm_fwd_{tag}", comm=comm_ssm)
    yp = _pool_fwd(pp, lw["w_pool"], lw["pool_scale"], name=f"pool_fwd_{tag}")
    merged, ba, bs, bp = _merge_fwd(ya, ys, yp, lw["w_br_att"], lw["w_br_ssm"], lw["w_br_pool"], pg, name=f"merge_fwd_{tag}")
    x_new, out = _out_fwd(merged, lw["w_out"], x, lw["gate"], name=f"out_fwd_{tag}")
    saved = dict(x=x, h=h, pa=pa, ps=ps, pp=pp, pg=pg, ya=ya, ys=ys, yp=yp, chk=chk, states=states, merged=merged,
                 ba=ba, bs=bs, bp=bp, out=out)
    return x_new, saved, lw, list(from_attn), list(from_ssm)


def _layer_bwd(dx, lw, sv, li, comm=None, own=None):
    tag = f"l{li}"
    dmo, dba, dbs, dbp, dpg, gate_sums = _merge_bwd(dx, sv["out"], lw["gate"], lw["w_out"], sv["pg"],
                                                    sv["ba"], sv["bs"], sv["bp"], name=f"merge_bwd_{tag}")
    g = {}
    g["w_out"] = _mm_tn(sv["merged"], dmo, out_dtype=BF16, name=f"dw_out_{tag}")
    dya = _mm(dba, lw["w_br_att"], nt=True, name=f"dy_att_{tag}")
    dys = _mm(dbs, lw["w_br_ssm"], nt=True, name=f"dy_ssm_{tag}")
    dyp = _mm(dbp, lw["w_br_pool"], nt=True, name=f"dy_pool_{tag}")
    g["w_br_att"] = _mm_tn(sv["ya"], dba, out_dtype=BF16, name=f"dw_br_att_{tag}")
    g["w_br_ssm"] = _mm_tn(sv["ys"], dbs, out_dtype=BF16, name=f"dw_br_ssm_{tag}")
    g["w_br_pool"] = _mm_tn(sv["yp"], dbp, out_dtype=BF16, name=f"dw_br_pool_{tag}")
    dpa, dsink = _attn_bwd(sv["pa"], lw["sinks"], dya, name=f"attn_bwd_{tag}")
    dps, db_dense, dc_dense, dwglu, ssm_sums, *exchanged = _ssm_bwd(
        sv["ps"], dys, sv["chk"], sv["states"], lw["lam"], lw["bblk"], lw["cblk"], lw["ssm_d"], lw["w_glu"], lw["b_glu"],
        name=f"ssm_bwd_{tag}", comm=comm)
    g["w_glu"] = dwglu.astype(BF16)
    dpp, dwpool, pool_sums = _pool_bwd(sv["pp"], dyp, lw["w_pool"], lw["pool_scale"], name=f"pool_bwd_{tag}")
    h = sv["h"]
    dw_a = _mm_tn(h, dpa, out_dtype=BF16, tn=1280, name=f"dw_a_{tag}")
    dw_s = _mm_tn(h, dps, out_dtype=BF16, name=f"dw_s_{tag}")
    dw_p = _mm_tn(h, dpp, out_dtype=BF16, name=f"dw_p_{tag}")
    dh_pairs = [(dpa, lw["w_a"]), (dps, lw["w_s"]), (dpp, lw["w_p"]), (dpg, lw["w_g"])]
    if own is None:
        dw_g, from_late = _mm_tn(h, dpg, out_dtype=BF16, name=f"dw_g_{tag}"), []
        g["w_in"] = _merge_w_in(dw_a, dw_s, dw_p, dw_g)
        dh, from_w_in = _mm_nt_sum(dh_pairs, name=f"dh_{tag}"), []
    else:
        dw_g, from_late = _mm_tn(h, dpg, out_dtype=BF16, name=f"dw_g_{tag}", comm=own({k: g[k] for k in LATE_WEIGHTS}))
        g["w_in"] = _merge_w_in(dw_a, dw_s, dw_p, dw_g)
        dh, from_w_in = _mm_nt_sum(dh_pairs, name=f"dh_{tag}", comm=own({"w_in": g["w_in"]}))
    dx_in, ln_sums = _ln_bwd(sv["x"], dh, dx, lw["norm_g"], lw["scale"], name=f"ln_bwd_{tag}")
    g["dmod"] = jnp.concatenate([ln_sums[0], ln_sums[1], gate_sums[0]])
    g["norm_g"] = ln_sums[2]
    g["attn_sinks"] = dsink[:, 0]
    g["ssm_raw"] = _ssm_extract(db_dense, dc_dense, ssm_sums)
    g["ssm_d"] = ssm_sums[2, :SSM_W]
    g["b_glu"] = ssm_sums[3, :SSM_W]
    g["w_pool"] = dwpool
    g["pool_scale"] = pool_sums[0]
    return dx_in, g, exchanged, list(from_w_in) + list(from_late)


BIG_WEIGHTS = ("w_in", "w_glu", "w_br_att", "w_br_ssm", "w_br_pool", "w_out")
ROW_SHARDED = ("w_glu", "w_out")


LATE_WEIGHTS = BIG_WEIGHTS[1:]


def _full_weights(keys, gathered):
    full = {}
    for k, g in zip(keys, gathered):
        if k in ROW_SHARDED:
            full[k] = g.reshape(N_DEV * g.shape[1], g.shape[2])
        else:
            full[k] = g.transpose(1, 0, 2).reshape(g.shape[1], N_DEV * g.shape[2])
    return full


def _by_destination(keys, grads):
    out = []
    for k in keys:
        g = grads[k]
        if k in ROW_SHARDED:
            out.append(g.reshape(4, 2, g.shape[0] // N_DEV, g.shape[1]).transpose(1, 0, 2, 3))
        else:
            out.append(g.reshape(g.shape[0], 4, 2, g.shape[1] // N_DEV).transpose(2, 1, 0, 3))
    return out


def _prepare_layer(li, mod, norm_g, w_in_full, attn_sinks, disc, ssm_c_re, ssm_c_im, ssm_d, b_glu, w_pool, pool_scale):
    d = D_MODEL
    lr, li_, bbr, bbi = disc
    lam, bblk, cblk = _ssm_dense(lr[li], li_[li], bbr[li], bbi[li], ssm_c_re[li], ssm_c_im[li])
    w_a, w_s, w_p, w_g = _split_w_in(w_in_full)
    return dict(
        norm_g=norm_g[li][None, :], shift=mod[li, :d][None, :], scale=mod[li, d:2 * d][None, :],
        gate=mod[li, 2 * d:][None, :], w_a=w_a, w_s=w_s, w_p=w_p, w_g=w_g,
        sinks=attn_sinks[li], lam=lam, bblk=bblk, cblk=cblk, ssm_d=ssm_d[li][None, :],
        b_glu=b_glu[li][None, :], w_pool=w_pool[li].astype(BF16), pool_scale=pool_scale[li][None, :])


SMALL_ROWS = 64
SMALL_ORDER = ("norm_g", "attn_sinks", "ssm_d", "b_glu", "w_pool", "pool_scale", "dmod")


def _pack_small(loss, dfinal_g, layer_grads):
    parts = [jnp.broadcast_to(loss.reshape(1), (128,)), dfinal_g]
    for g in layer_grads:
        for k in SMALL_ORDER:
            v = g[k].reshape(-1)
            if v.shape[0] % 128:
                v = jnp.pad(v, (0, 128 - v.shape[0] % 128))
            parts.append(v)
        for v in g["ssm_raw"]:
            parts.append(v.reshape(-1))
    flat = jnp.concatenate(parts)
    return jnp.pad(flat, (0, (-flat.shape[0]) % (SMALL_ROWS * 128))).reshape(-1, 128)


def _unpack_small(flat, shapes):
    out, off = [], 0
    for s in shapes:
        n = int(math.prod(s))
        out.append(flat[off:off + n].reshape(s))
        off += n + (-n) % 128
    return out


def kernel(x, c, norm_g, w_ada, b_ada, w_in, attn_sinks, ssm_a_re, ssm_a_im, ssm_log_dt, ssm_b_re, ssm_b_im, ssm_c_re, ssm_c_im, ssm_d, w_glu, b_glu, w_pool, pool_scale, w_br_att, w_br_ssm, w_br_pool, w_out, final_g, loss_target, m_norm_g, m_w_ada, m_b_ada, m_w_in, m_attn_sinks, m_ssm_a_re, m_ssm_a_im, m_ssm_log_dt, m_ssm_b_re, m_ssm_b_im, m_ssm_c_re, m_ssm_c_im, m_ssm_d, m_w_glu, m_b_glu, m_w_pool, m_pool_scale, m_w_br_att, m_w_br_ssm, m_w_br_pool, m_w_out, m_final_g, v_norm_g, v_w_ada, v_b_ada, v_w_in, v_attn_sinks, v_ssm_a_re, v_ssm_a_im, v_ssm_log_dt, v_ssm_b_re, v_ssm_b_im, v_ssm_c_re, v_ssm_c_im, v_ssm_d, v_w_glu, v_b_glu, v_w_pool, v_pool_scale, v_w_br_att, v_w_br_ssm, v_w_br_pool, v_w_out, v_final_g):
    me = 4 * lax.axis_index("x") + 2 * lax.axis_index("y") + lax.axis_index("c")
    d = D_MODEL
    ada_w = 3 * d // N_DEV

    (c_all,) = _exchange([c.reshape(8, 128)], scatter=False, name="gather_c")
    c_act = jax.nn.silu(c_all.reshape(N_DEV, d))
    b_cols = lax.dynamic_slice(b_ada, (0, me * ada_w), (DEPTH, ada_w))
    mod_part = jnp.concatenate(
        [_mm(c_act, w_ada[li], name=f"ada_fwd_l{li}") + b_cols[li][None, :] for li in range(DEPTH)], axis=0)
    (mod_all,) = _exchange([mod_part], scatter=False, name="gather_mod")
    mod_all = mod_all.reshape(N_DEV, DEPTH, N_DEV, ada_w)
    mod_mine = lax.dynamic_index_in_dim(mod_all, me, axis=2, keepdims=False)
    mod_mine = mod_mine.transpose(1, 0, 2).reshape(DEPTH, 3 * d)

    sharded = dict(w_in=w_in, w_glu=w_glu, w_br_att=w_br_att, w_br_ssm=w_br_ssm, w_br_pool=w_br_pool, w_out=w_out)
    shards = lambda li, keys: [sharded[k][li].astype(BF16) for k in keys]
    disc, disc_vjp = jax.vjp(jax.vmap(_ssm_discretize), ssm_a_re, ssm_a_im, ssm_log_dt, ssm_b_re, ssm_b_im)
    layer = lambda li, gathered_w_in: _prepare_layer(
        li, mod_mine, norm_g, _full_weights(("w_in",), gathered_w_in)["w_in"], attn_sinks, disc, ssm_c_re, ssm_c_im,
        ssm_d, b_glu, w_pool, pool_scale)
    late_weights = lambda gathered: _full_weights(LATE_WEIGHTS, gathered)
    core = lax.axis_index("c").astype(jnp.int32).reshape(1)

    def chip_sums_of(keys, grads_li, tag):
        by_dest = _by_destination(keys, grads_li)
        from_sibling = _sibling_swap(by_dest, name=f"grads_sibling_swap_{tag}")
        return [_pair_add(a.reshape(2, -1, a.shape[-1]), b.reshape(-1, b.shape[-1]), core,
                          name=f"grads_pair_add_{tag}_{k}").reshape(b.shape)
                for k, (a, b) in zip(keys, zip(by_dest, from_sibling))]

    layers, saved, grads = [None] * DEPTH, [None] * DEPTH, [None] * DEPTH
    layers[0] = layer(0, _gather_two_level(shards(0, ("w_in",)), name="gather_w_in_l0"))
    xs, saved[0], layers[0], late1, w_in1 = _layer_fwd(
        x[0], layers[0], 0, late=(_gather_plan(shards(0, LATE_WEIGHTS)), late_weights),
        comm_attn=_gather_plan(shards(1, LATE_WEIGHTS)), comm_ssm=_gather_plan(shards(1, ("w_in",))))
    layers[1] = {**layer(1, w_in1), **late_weights(late1)}
    xs, saved[1], _, _, _ = _layer_fwd(xs, layers[1], 1)
    dx, fin_sums = _final_loss(xs, final_g[None, :], loss_target[0])
    loss_part = jnp.sum(fin_sums[1])
    dx, grads[1], _, _ = _layer_bwd(dx, layers[1], saved[1], 1)
    dx, grads[0], scattered1, scattered0 = _layer_bwd(
        dx, layers[0], saved[0], 0, comm=_chip_scatter_plan(chip_sums_of(BIG_WEIGHTS, grads[1], "l1")),
        own=lambda g: _chip_scatter_plan(chip_sums_of(tuple(g), g, "l0_" + "_".join(g))))
    big = [jnp.stack([a, b], axis=1) for a, b in zip(scattered0, scattered1)]
    grad_x = dx[None]

    small = _pack_small(loss_part, fin_sums[0], grads)
    dmod_rows = jnp.concatenate([grads[li]["dmod"] for li in range(DEPTH)]).reshape(-1, 128)
    small_sum, dmod_gathered = _allreduce_small(small, dmod_rows, name="allreduce_small")
    out = {}

    def adam(name, w, g_parts, m, v):
        shp = w.shape
        r = int(math.prod(shp[:-1])) if len(shp) > 1 else 1
        w2, m2, v2 = (a.reshape(r, shp[-1]) for a in (w, m, v))
        g2 = g_parts.reshape(g_parts.shape[0], r, shp[-1])
        res = _adamw(w2, g2, m2, v2, name=f"adamw_{name}")
        out[name] = tuple(a.reshape(shp) for a in res)

    flat = small_sum.reshape(-1)
    shapes = [(128,), (d,)]
    for _ in range(DEPTH):
        shapes += [(d,), (N_HEADS,), (SSM_W,), (SSM_W,), (4, POOL_GW, POOL_GW), (POOL_W,), (3 * d,),
                   (SSM_GROUPS, SSM_STATE), (SSM_GROUPS, SSM_STATE), (SSM_GROUPS, SSM_STATE, SSM_GROUP),
                   (SSM_GROUPS, SSM_STATE, SSM_GROUP), (SSM_GROUPS, SSM_GROUP, SSM_STATE), (SSM_GROUPS, SSM_GROUP, SSM_STATE)]
    un = _unpack_small(flat, shapes)
    loss = un[0][0]
    g_final_g = un[1]
    per = 13
    gl = [un[2 + li * per: 2 + (li + 1) * per] for li in range(DEPTH)]
    st = lambda j: jnp.stack([gl[li][j] for li in range(DEPTH)])
    g_norm_g, g_sinks, g_ssm_d, g_b_glu, g_w_pool, g_pool_scale, g_b_ada = (st(j) for j in range(7))
    d_lr, d_li, d_bbr, d_bbi, g_c_re, g_c_im = (st(j) for j in range(7, 13))
    g_a_re, g_a_im, g_log_dt, g_b_re, g_b_im = disc_vjp((d_lr, d_li, d_bbr, d_bbi))

    dmod_all = lax.dynamic_slice(dmod_gathered.reshape(N_DEV, DEPTH, 3 * d), (0, 0, me * ada_w), (N_DEV, DEPTH, ada_w))
    dmod_all = dmod_all.transpose(1, 0, 2)
    g_w_ada = jnp.stack([_mm_tn(c_act, dmod_all[li], tm=d, tn=ada_w, tk=N_DEV, name=f"dw_ada_l{li}") for li in range(DEPTH)])

    adam("w_ada", w_ada, g_w_ada[None], m_w_ada, v_w_ada)
    adam("w_in", w_in, big[0], m_w_in, v_w_in)
    adam("w_glu", w_glu, big[1], m_w_glu, v_w_glu)
    adam("w_br_att", w_br_att, big[2], m_w_br_att, v_w_br_att)
    adam("w_br_ssm", w_br_ssm, big[3], m_w_br_ssm, v_w_br_ssm)
    adam("w_br_pool", w_br_pool, big[4], m_w_br_pool, v_w_br_pool)
    adam("w_out", w_out, big[5], m_w_out, v_w_out)

    small_names = ["norm_g", "b_ada", "attn_sinks", "ssm_a_re", "ssm_a_im", "ssm_log_dt", "ssm_b_re", "ssm_b_im",
                   "ssm_c_re", "ssm_c_im", "ssm_d", "b_glu", "w_pool", "pool_scale", "final_g"]
    small_w = [norm_g, b_ada, attn_sinks, ssm_a_re, ssm_a_im, ssm_log_dt, ssm_b_re, ssm_b_im, ssm_c_re, ssm_c_im,
               ssm_d, b_glu, w_pool, pool_scale, final_g]
    small_m = [m_norm_g, m_b_ada, m_attn_sinks, m_ssm_a_re, m_ssm_a_im, m_ssm_log_dt, m_ssm_b_re, m_ssm_b_im,
               m_ssm_c_re, m_ssm_c_im, m_ssm_d, m_b_glu, m_w_pool, m_pool_scale, m_final_g]
    small_v = [v_norm_g, v_b_ada, v_attn_sinks, v_ssm_a_re, v_ssm_a_im, v_ssm_log_dt, v_ssm_b_re, v_ssm_b_im,
               v_ssm_c_re, v_ssm_c_im, v_ssm_d, v_b_glu, v_w_pool, v_pool_scale, v_final_g]
    small_g = [g_norm_g, g_b_ada, g_sinks, g_a_re, g_a_im, g_log_dt, g_b_re, g_b_im, g_c_re, g_c_im,
               g_ssm_d, g_b_glu, g_w_pool, g_pool_scale, g_final_g]

    for nm, w, g, m, v in zip(small_names, small_w, small_g, small_m, small_v):
        adam(nm, w, g[None], m, v)

    order = ["norm_g", "w_ada", "b_ada", "w_in", "attn_sinks", "ssm_a_re", "ssm_a_im", "ssm_log_dt", "ssm_b_re",
             "ssm_b_im", "ssm_c_re", "ssm_c_im", "ssm_d", "w_glu", "b_glu", "w_pool", "pool_scale", "w_br_att",
             "w_br_ssm", "w_br_pool", "w_out", "final_g"]
    return (loss, grad_x, *[out[k][0] for k in order], *[out[k][1] for k in order],
            *[out[k][2] for k in order], *[out[k][3] for k in order])
```

```python
import functools
import math

import jax
import jax.numpy as jnp
from jax import lax
from jax.experimental import pallas as pl
from jax.experimental.pallas import tpu as pltpu

F32 = jnp.float32
BF16 = jnp.bfloat16

N_DEV = 8
D_MODEL = 1024
DEPTH = 2
CHUNK = 64
N_HEADS = 8
N_KV_HEADS = 2
HEAD_DIM = 64
Q_PER_KV = N_HEADS // N_KV_HEADS
WINDOW = 128
ATT_W = 512
KV_W = 128
SSM_W = 512
SSM_GROUP = 16
SSM_GROUPS = 32
SSM_STATE = 64
SSM_N = SSM_GROUPS * SSM_STATE
POOL_W = 512
POOL_WINDOWS = (2, 4, 8, 16)
POOL_GW = 128
POOL_HALO = 16
EPS = 1e-6
NEG_INF = -1e30
ADAM_LR = 0.001
ADAM_B1 = 0.9
ADAM_B2 = 0.999
ADAM_EPS = 1e-08
ADAM_WD = 0.01
ADAM_STEP = 10

SEQ_BLOCK = 256
ATT_BLOCK = 128
VMEM_LIMIT = 56 * 1024 * 1024

NN = (((1,), (0,)), ((), ()))
NT = (((1,), (1,)), ((), ()))
TN = (((0,), (0,)), ((), ()))


def _dot(a, b, dims=NN):
    return lax.dot_general(a.astype(BF16), b.astype(BF16), dims, preferred_element_type=F32)


def _params(*sem):
    return pltpu.CompilerParams(dimension_semantics=sem, vmem_limit_bytes=VMEM_LIMIT)


def _sigmoid(x):
    return 0.5 + 0.5 * jnp.tanh(0.5 * x)


def _silu_and_grad(z):
    s = _sigmoid(z)
    return z * s, s * (1.0 + z * (1.0 - s))


_GELU_K = math.sqrt(2.0 / math.pi)


def _gelu_and_grad(x):
    inner = _GELU_K * (x + 0.044715 * x * x * x)
    t = jnp.tanh(inner)
    val = 0.5 * x * (1.0 + t)
    grad = 0.5 * (1.0 + t) + 0.5 * x * (1.0 - t * t) * _GELU_K * (1.0 + 3.0 * 0.044715 * x * x)
    return val, grad


def _mm(a, b, *, nt=False, out_dtype=F32, tm=1024, tn=1024, name, comm=None):
    m, k = a.shape
    n = b.shape[0] if nt else b.shape[1]
    tm, tn = min(tm, m), min(tn, n)
    assert m % tm == 0 and n % tn == 0
    dims = NT if nt else NN
    grid = (m // tm, n // tn)
    c_args, c_in, c_out, c_shape, c_scratch = _comm_extra(comm)

    def body(a_ref, b_ref, o_ref):
        o_ref[...] = _dot(a_ref[...], b_ref[...], dims).astype(out_dtype)

    b_spec = pl.BlockSpec((tn, k), lambda i, j: (j, 0)) if nt else pl.BlockSpec((k, tn), lambda i, j: (0, j))
    res = pl.pallas_call(
        _with_comm(body, comm, 2, 1, grid, -1), grid=grid,
        in_specs=[pl.BlockSpec((tm, k), lambda i, j: (i, 0)), b_spec] + c_in,
        out_specs=[pl.BlockSpec((tm, tn), lambda i, j: (i, j))] + c_out,
        out_shape=[jax.ShapeDtypeStruct((m, n), out_dtype)] + c_shape,
        scratch_shapes=c_scratch,
        compiler_params=_params(*(("arbitrary",) * 2 if comm else ("parallel",) * 2)), name=name)(a, b, *c_args)
    return (res[0], list(res[1:])) if comm else res[0]


def _mm_nt_sum(pairs, *, out_dtype=F32, tm=512, tn=512, name, comm=None):
    m = pairs[0][0].shape[0]
    n = pairs[0][1].shape[0]
    np_ = len(pairs)
    grid = (m // tm, n // tn)
    c_args, c_in, c_out, c_shape, c_scratch = _comm_extra(comm)

    def body(*refs):
        o_ref = refs[-1]
        acc = _dot(refs[0][...], refs[1][...], NT)
        for p in range(1, np_):
            acc = acc + _dot(refs[2 * p][...], refs[2 * p + 1][...], NT)
        o_ref[...] = acc.astype(out_dtype)

    in_specs, args = [], []
    for a, b in pairs:
        in_specs.append(pl.BlockSpec((tm, a.shape[1]), lambda i, j: (i, 0)))
        in_specs.append(pl.BlockSpec((tn, b.shape[1]), lambda i, j: (j, 0)))
        args += [a, b]
    res = pl.pallas_call(
        _with_comm(body, comm, 2 * np_, 1, grid, -1), grid=grid, in_specs=in_specs + c_in,
        out_specs=[pl.BlockSpec((tm, tn), lambda i, j: (i, j))] + c_out,
        out_shape=[jax.ShapeDtypeStruct((m, n), out_dtype)] + c_shape,
        scratch_shapes=c_scratch,
        compiler_params=_params(*(("arbitrary",) * 2 if comm else ("parallel",) * 2)), name=name)(*args, *c_args)
    return (res[0], list(res[1:])) if comm else res[0]


def _mm_tn(a, b, *, out_dtype=F32, tm=1024, tn=1024, tk=1024, name, comm=None):
    k, m = a.shape
    n = b.shape[1]
    assert m % min(tm, m) == 0 and n % min(tn, n) == 0 and k % min(tk, k) == 0
    tm, tn, tk = min(tm, m), min(tn, n), min(tk, k)
    nk = k // tk
    grid = (m // tm, n // tn, nk)
    c_args, c_in, c_out, c_shape, c_scratch = _comm_extra(comm)

    def body(a_ref, b_ref, o_ref, acc_ref):
        kk = pl.program_id(2)

        @pl.when(kk == 0)
        def _():
            acc_ref[...] = jnp.zeros_like(acc_ref)

        acc_ref[...] += _dot(a_ref[...], b_ref[...], TN)

        @pl.when(kk == nk - 1)
        def _():
            o_ref[...] = acc_ref[...].astype(out_dtype)

    res = pl.pallas_call(
        _with_comm(body, comm, 2, 1, grid, -1), grid=grid,
        in_specs=[pl.BlockSpec((tk, tm), lambda i, j, kk: (kk, i)), pl.BlockSpec((tk, tn), lambda i, j, kk: (kk, j))] + c_in,
        out_specs=[pl.BlockSpec((tm, tn), lambda i, j, kk: (i, j))] + c_out,
        out_shape=[jax.ShapeDtypeStruct((m, n), out_dtype)] + c_shape,
        scratch_shapes=[pltpu.VMEM((tm, tn), F32)] + c_scratch,
        compiler_params=_params(*(("arbitrary",) * 3 if comm else ("parallel", "parallel", "arbitrary"))),
        name=name)(a, b, *c_args)
    return (res[0], list(res[1:])) if comm else res[0]


def _ln_fwd(x, g, shift, scale, *, name, tm=512):
    l, d = x.shape

    def body(x_ref, g_ref, sh_ref, sc_ref, h_ref):
        xv = x_ref[...]
        n = xv * lax.rsqrt(jnp.mean(xv * xv, axis=-1, keepdims=True) + EPS)
        h_ref[...] = ((n * g_ref[...]) * (1.0 + sc_ref[...]) + sh_ref[...]).astype(BF16)

    vec = pl.BlockSpec((1, d), lambda i: (0, 0))
    return pl.pallas_call(
        body, grid=(l // tm,),
        in_specs=[pl.BlockSpec((tm, d), lambda i: (i, 0)), vec, vec, vec],
        out_specs=pl.BlockSpec((tm, d), lambda i: (i, 0)),
        out_shape=jax.ShapeDtypeStruct((l, d), BF16),
        compiler_params=_params("parallel"), name=name)(x, g, shift, scale)


def _ln_bwd(x, dh, dres, g, scale, *, name, tm=512):
    l, d = x.shape

    def body(x_ref, dh_ref, dres_ref, g_ref, sc_ref, dx_ref, sums_ref):
        xv = x_ref[...]
        dhv = dh_ref[...]
        rstd = lax.rsqrt(jnp.mean(xv * xv, axis=-1, keepdims=True) + EPS)
        n = xv * rstd
        gv = g_ref[...]
        dr = dhv * (1.0 + sc_ref[...])
        dn = dr * gv
        dx_ref[...] = dres_ref[...] + rstd * (dn - n * jnp.mean(dn * n, axis=-1, keepdims=True))

        @pl.when(pl.program_id(0) == 0)
        def _():
            sums_ref[...] = jnp.zeros_like(sums_ref)

        sums_ref[0:1, :] += jnp.sum(dhv, axis=0, keepdims=True)
        sums_ref[1:2, :] += jnp.sum(dhv * (n * gv), axis=0, keepdims=True)
        sums_ref[2:3, :] += jnp.sum(dr * n, axis=0, keepdims=True)

    vec = pl.BlockSpec((1, d), lambda i: (0, 0))
    row = pl.BlockSpec((tm, d), lambda i: (i, 0))
    return pl.pallas_call(
        body, grid=(l // tm,),
        in_specs=[row, row, row, vec, vec],
        out_specs=[row, pl.BlockSpec((8, d), lambda i: (0, 0))],
        out_shape=[jax.ShapeDtypeStruct((l, d), F32), jax.ShapeDtypeStruct((8, d), F32)],
        compiler_params=_params("arbitrary"), name=name)(x, dh, dres, g, scale)


def _final_loss(x, g, target, *, tm=512):
    l, d = x.shape

    def body(x_ref, g_ref, t_ref, dx_ref, sums_ref):
        xv = x_ref[...]
        rstd = lax.rsqrt(jnp.mean(xv * xv, axis=-1, keepdims=True) + EPS)
        n = xv * rstd
        gv = g_ref[...]
        err = n * gv - t_ref[...]
        dy = err * (1.0 / d)
        dn = dy * gv
        dx_ref[...] = rstd * (dn - n * jnp.mean(dn * n, axis=-1, keepdims=True))

        @pl.when(pl.program_id(0) == 0)
        def _():
            sums_ref[...] = jnp.zeros_like(sums_ref)

        sums_ref[0:1, :] += jnp.sum(dy * n, axis=0, keepdims=True)
        sums_ref[1:2, :] += jnp.sum(err * err, axis=0, keepdims=True) * (0.5 / d)

    vec = pl.BlockSpec((1, d), lambda i: (0, 0))
    row = pl.BlockSpec((tm, d), lambda i: (i, 0))
    dx, sums = pl.pallas_call(
        body, grid=(l // tm,),
        in_specs=[row, vec, row],
        out_specs=[row, pl.BlockSpec((8, d), lambda i: (0, 0))],
        out_shape=[jax.ShapeDtypeStruct((l, d), F32), jax.ShapeDtypeStruct((8, d), F32)],
        compiler_params=_params("arbitrary"), name="final_loss")(x, g, target)
    return dx, sums


def _attn_geometry(i, t):
    nk = t + WINDOW
    qi = lax.broadcasted_iota(jnp.int32, (t, nk), 0)
    kj = lax.broadcasted_iota(jnp.int32, (t, nk), 1)
    dist = jnp.abs(qi + WINDOW - kj).astype(F32)
    qc = jnp.right_shift(qi, 6)
    kc = jnp.right_shift(kj, 6)
    valid = (kc >= qc) & (kc <= qc + WINDOW // CHUNK) & ((i > 0) | (kj >= WINDOW))
    return dist, valid


def _attn_head(q, k_all, v_all, sink, slope, dist, valid):
    s = _dot(q, k_all, NT) * (1.0 / math.sqrt(HEAD_DIM)) - slope * dist
    s = jnp.where(valid, s, NEG_INF)
    m = jnp.maximum(jnp.max(s, axis=-1, keepdims=True), sink)
    e = jnp.exp(s - m)
    es = jnp.exp(sink - m)
    inv = 1.0 / (jnp.sum(e, axis=-1, keepdims=True) + es)
    p = e * inv
    o = _dot(p, v_all, NN)
    return p, o, es * inv


def _attn_specs(t):
    cur = pl.BlockSpec((t, ATT_W * 2 + KV_W * 2), lambda i: (i, 0))
    halo_blocks = t // WINDOW
    prev = pl.BlockSpec((WINDOW, 2 * KV_W), lambda i: (jnp.maximum(i * halo_blocks - 1, 0), (2 * ATT_W) // (2 * KV_W)))
    return cur, prev


def _attn_fwd(pa, sinks, *, name, t=ATT_BLOCK, comm=None):
    l = pa.shape[0]
    t = min(t, l)
    nb = l // t
    c_args, c_in, c_out, c_shape, c_scratch = _comm_extra(comm)

    def body(sink_ref, cur_ref, prev_ref, ya_ref):
        i = pl.program_id(0)
        dist, valid = _attn_geometry(i, t)
        for h in range(N_HEADS):
            kh = h // Q_PER_KV
            q = cur_ref[:, h * HEAD_DIM:(h + 1) * HEAD_DIM]
            z = cur_ref[:, ATT_W + h * HEAD_DIM:ATT_W + (h + 1) * HEAD_DIM].astype(F32)
            k_all = jnp.concatenate([prev_ref[:, kh * HEAD_DIM:(kh + 1) * HEAD_DIM],
                                     cur_ref[:, 2 * ATT_W + kh * HEAD_DIM:2 * ATT_W + (kh + 1) * HEAD_DIM]], axis=0)
            v_all = jnp.concatenate([prev_ref[:, KV_W + kh * HEAD_DIM:KV_W + (kh + 1) * HEAD_DIM],
                                     cur_ref[:, 2 * ATT_W + KV_W + kh * HEAD_DIM:2 * ATT_W + KV_W + (kh + 1) * HEAD_DIM]], axis=0)
            _, o, _ = _attn_head(q, k_all, v_all, sink_ref[h], 2.0 ** (-(h + 1)), dist, valid)
            sz, _ = _silu_and_grad(z)
            ya_ref[:, h * HEAD_DIM:(h + 1) * HEAD_DIM] = (o * sz).astype(BF16)

    cur, prev = _attn_specs(t)
    res = pl.pallas_call(
        _with_comm(body, comm, 3, 1, nb, nb - 1), grid=(nb,),
        in_specs=[pl.BlockSpec(memory_space=pltpu.SMEM), cur, prev] + c_in,
        out_specs=[pl.BlockSpec((t, ATT_W), lambda i: (i, 0))] + c_out,
        out_shape=[jax.ShapeDtypeStruct((l, ATT_W), BF16)] + c_shape,
        scratch_shapes=c_scratch,
        compiler_params=_params("arbitrary"), name=name)(sinks, pa, pa, *c_args)
    return res[0], res[1:]


def _attn_bwd(pa, sinks, dya, *, name, t=SEQ_BLOCK):
    l = pa.shape[0]
    t = min(t, l)
    nb = l // t
    scale = 1.0 / math.sqrt(HEAD_DIM)

    def body(sink_ref, cur_ref, prev_ref, dya_ref, dpa_ref, dsink_ref, carry_ref):
        n = pl.program_id(0)
        i = nb - 1 - n
        dist, valid = _attn_geometry(i, t)

        @pl.when(n == 0)
        def _():
            carry_ref[...] = jnp.zeros_like(carry_ref)
            dsink_ref[...] = jnp.zeros_like(dsink_ref)

        dk_acc = [jnp.zeros((HEAD_DIM, t + WINDOW), F32) for _ in range(N_KV_HEADS)]
        dv_acc = [jnp.zeros((HEAD_DIM, t + WINDOW), F32) for _ in range(N_KV_HEADS)]
        for h in range(N_HEADS):
            kh = h // Q_PER_KV
            q = cur_ref[:, h * HEAD_DIM:(h + 1) * HEAD_DIM]
            z = cur_ref[:, ATT_W + h * HEAD_DIM:ATT_W + (h + 1) * HEAD_DIM].astype(F32)
            k_all = jnp.concatenate([prev_ref[:, kh * HEAD_DIM:(kh + 1) * HEAD_DIM],
                                     cur_ref[:, 2 * ATT_W + kh * HEAD_DIM:2 * ATT_W + (kh + 1) * HEAD_DIM]], axis=0)
            v_all = jnp.concatenate([prev_ref[:, KV_W + kh * HEAD_DIM:KV_W + (kh + 1) * HEAD_DIM],
                                     cur_ref[:, 2 * ATT_W + KV_W + kh * HEAD_DIM:2 * ATT_W + KV_W + (kh + 1) * HEAD_DIM]], axis=0)
            p, o, p_sink = _attn_head(q, k_all, v_all, sink_ref[h], 2.0 ** (-(h + 1)), dist, valid)
            dy = dya_ref[:, h * HEAD_DIM:(h + 1) * HEAD_DIM]
            sz, dsz = _silu_and_grad(z)
            do = dy * sz
            dpa_ref[:, ATT_W + h * HEAD_DIM:ATT_W + (h + 1) * HEAD_DIM] = (dy * o * dsz).astype(BF16)
            delta = jnp.sum(do * o, axis=-1, keepdims=True)
            dp = _dot(do, v_all, NT)
            ds = p * (dp - delta)
            dpa_ref[:, h * HEAD_DIM:(h + 1) * HEAD_DIM] = (_dot(ds, k_all, NN) * scale).astype(BF16)
            dk_acc[kh] = dk_acc[kh] + _dot(q, ds, TN) * scale
            dv_acc[kh] = dv_acc[kh] + _dot(do, p, TN)
            dsink_ref[h:h + 1, :] += jnp.broadcast_to(-jnp.sum(p_sink * delta, axis=0, keepdims=True), (1, 128))

        acc = jnp.concatenate(dk_acc + dv_acc, axis=0).T
        own = acc[WINDOW:, :]
        tail = own[t - WINDOW:, :] + carry_ref[...]
        if t > WINDOW:
            dpa_ref[0:t - WINDOW, 2 * ATT_W:] = own[:t - WINDOW, :].astype(BF16)
        dpa_ref[t - WINDOW:t, 2 * ATT_W:] = tail.astype(BF16)
        carry_ref[...] = acc[:WINDOW, :]

    halo_blocks = t // WINDOW
    wpa = 2 * ATT_W + 2 * KV_W
    cur = pl.BlockSpec((t, wpa), lambda n: (nb - 1 - n, 0))
    prev = pl.BlockSpec((WINDOW, 2 * KV_W),
                        lambda n: (jnp.maximum((nb - 1 - n) * halo_blocks - 1, 0), (2 * ATT_W) // (2 * KV_W)))
    return pl.pallas_call(
        body, grid=(nb,),
        in_specs=[pl.BlockSpec(memory_space=pltpu.SMEM), cur, prev, pl.BlockSpec((t, ATT_W), lambda n: (nb - 1 - n, 0))],
        out_specs=[pl.BlockSpec((t, wpa), lambda n: (nb - 1 - n, 0)), pl.BlockSpec((8, 128), lambda n: (0, 0))],
        out_shape=[jax.ShapeDtypeStruct((l, wpa), BF16), jax.ShapeDtypeStruct((8, 128), F32)],
        scratch_shapes=[pltpu.VMEM((WINDOW, 2 * KV_W), F32)],
        compiler_params=_params("arbitrary"), name=name)(sinks, pa, pa, dya)


def _scan(xr, xi, lr, li, t, reverse):
    row = lax.broadcasted_iota(jnp.int32, (t, 1), 0)
    d = 1
    pr, pi = lr, li
    while d < t:
        if reverse:
            sr = jnp.where(row < t - d, pltpu.roll(xr, t - d, 0), 0.0)
            si = jnp.where(row < t - d, pltpu.roll(xi, t - d, 0), 0.0)
        else:
            sr = jnp.where(row >= d, pltpu.roll(xr, d, 0), 0.0)
            si = jnp.where(row >= d, pltpu.roll(xi, d, 0), 0.0)
        xr, xi = xr + pr * sr - pi * si, xi + pr * si + pi * sr
        pr, pi = pr * pr - pi * pi, 2.0 * pr * pi
        d *= 2
    return xr, xi


SCAN_SUB = 8


def _split_hi_lo(a):
    hi = a.astype(BF16)
    lo = (a - hi.astype(F32)).astype(BF16)
    return jnp.concatenate([hi, lo], axis=0)


def _scan_mxu(xr, xi, tab, lam3, lam8, tri, expand, cr, ci, t, reverse):
    ns = t // SCAN_SUB
    n = xr.shape[1]
    v3 = lambda a: a.reshape(ns, SCAN_SUB, n)
    x3r, x3i = v3(xr), v3(xi)
    br = (x3r * tab[0] - x3i * tab[1]).reshape(t, n)
    bi = (x3r * tab[1] + x3i * tab[0]).reshape(t, n)
    pm = jnp.dot(tri, jnp.concatenate([br, bi], axis=1).astype(BF16), preferred_element_type=F32)
    p3r, p3i = v3(pm[:t, :n]), v3(pm[:t, n:])
    slr = p3r * tab[2] - p3i * tab[3]
    sli = p3r * tab[3] + p3i * tab[2]
    totr, toti = pm[t:, :n], pm[t:, n:]
    l3r, l3i = lam3
    l8r, l8i = lam8
    row = lax.broadcasted_iota(jnp.int32, (ns, 1), 0)
    edge = row == (ns - 1 if reverse else 0)
    er = totr * l3r - toti * l3i + jnp.where(edge, l8r * cr - l8i * ci, 0.0)
    ei = totr * l3i + toti * l3r + jnp.where(edge, l8r * ci + l8i * cr, 0.0)
    er, ei = _scan(er, ei, l8r, l8i, ns, reverse)
    shift = ns - 1 if reverse else 1
    nbr = jnp.where(edge, cr, pltpu.roll(er, shift, 0))
    nbi = jnp.where(edge, ci, pltpu.roll(ei, shift, 0))
    ex = jnp.dot(expand, _split_hi_lo(jnp.concatenate([nbr, nbi], axis=1)), preferred_element_type=F32)
    e3r, e3i = v3(ex[:, :n]), v3(ex[:, n:])
    sr = (slr + e3r * tab[4] - e3i * tab[5]).reshape(t, n)
    si = (sli + e3r * tab[5] + e3i * tab[4]).reshape(t, n)
    out = 0 if reverse else ns - 1
    return sr, si, er[out:out + 1, :], ei[out:out + 1, :]


def _scan_consts(t):
    import numpy as np
    ns = t // SCAN_SUB
    r = np.arange(t)
    same = (r[:, None] // SCAN_SUB) == (r[None, :] // SCAN_SUB)
    sums = (np.arange(ns)[:, None] == (r[None, :] // SCAN_SUB))
    tri = []
    for keep in (r[None, :] <= r[:, None], r[None, :] >= r[:, None]):
        tri.append(np.concatenate([same & keep, sums], axis=0).astype(np.float32))
    ex = ((r[:, None] // SCAN_SUB) == np.arange(ns)[None, :]).astype(np.float32)
    return jnp.asarray(np.stack(tri), BF16), jnp.asarray(np.concatenate([ex, ex], axis=1), BF16)


def _scan_tables(lr, li):
    den = lr * lr + li * li
    ir, ii = lr / den, -li / den
    mul = lambda a, b: (a[0] * b[0] - a[1] * b[1], a[0] * b[1] + a[1] * b[0])
    pw = {0: (jnp.ones_like(lr), jnp.zeros_like(lr))}
    for e in range(1, 9):
        pw[e] = mul(pw[e - 1], (lr, li))
    for e in range(-1, -5, -1):
        pw[e] = mul(pw[e + 1], (ir, ii))
    stack = lambda es, sign: (jnp.stack([pw[e][0] for e in es]), sign * jnp.stack([pw[e][1] for e in es]))
    j = range(SCAN_SUB)
    parts = [stack([4 - k for k in j], 1.0), stack([k - 4 for k in j], 1.0), stack([k + 1 for k in j], 1.0),
             stack([k - 3 for k in j], -1.0), stack([3 - k for k in j], -1.0), stack([8 - k for k in j], -1.0)]
    tabs = jnp.stack([a for pair in parts for a in pair])
    lam = jnp.zeros((8, lr.shape[0]), F32)
    for k, v in enumerate((lr, li, pw[3][0], pw[3][1], pw[8][0], pw[8][1])):
        lam = lam.at[k].set(v)
    return lam, tabs


SSM_HALVES = 2
SSM_HW = SSM_W // SSM_HALVES
SSM_HN = SSM_N // SSM_HALVES


def _bd_nn(x, w):
    a = w.shape[1]
    return jnp.concatenate([_dot(x[:, h * a:(h + 1) * a], w[h]) for h in range(SSM_HALVES)], axis=1)


def _bd_nt(x, w):
    b = w.shape[2]
    return jnp.concatenate([_dot(x[:, h * b:(h + 1) * b], w[h], NT) for h in range(SSM_HALVES)], axis=1)


def _bd_tn(x, y):
    a, b = x.shape[1] // SSM_HALVES, y.shape[1] // SSM_HALVES
    return jnp.stack([_dot(x[:, h * a:(h + 1) * a], y[:, h * b:(h + 1) * b], TN) for h in range(SSM_HALVES)])


def _ssm_states(u, s0r, s0i, lam_ref, tab_ref, tri_ref, ex_ref, bre, bim, t):
    tab = tuple(tab_ref[k] for k in range(6))
    return _scan_mxu(_bd_nn(u, bre), _bd_nn(u, bim), tab, (lam_ref[2:3, :], lam_ref[3:4, :]),
                     (lam_ref[4:5, :], lam_ref[5:6, :]), tri_ref[0], ex_ref[...], s0r, s0i, t, False)


def _ssm_head(u, z, xr, xi, cre, cim, dskip, wglu, bglu):
    y = _bd_nn(xr, cre) - _bd_nn(xi, cim) + dskip * u
    y2, dgelu = _gelu_and_grad(y)
    gate = _sigmoid(_dot(y2, wglu) + bglu)
    y3 = y2 * gate
    return y2, dgelu, gate, y3


def _with_comm(body, comm, n_in, n_out, grid, mid_step):
    if comm is None:
        return body
    nc = len(comm["args"])
    n_sem = len(comm["scratch"])
    grid = (grid,) if isinstance(grid, int) else tuple(grid)
    total = math.prod(grid)

    def hosted(*refs):
        ins, cin = refs[:n_in], refs[n_in:n_in + nc]
        outs, cout = refs[n_in + nc:n_in + nc + n_out], refs[n_in + nc + n_out:n_in + 2 * nc + n_out]
        rest = refs[n_in + 2 * nc + n_out:]
        scratch, csem = rest[:len(rest) - n_sem], rest[len(rest) - n_sem:]
        start, forward, finish = comm["phases"](cin, cout, *csem)
        step = pl.program_id(0)
        for axis in range(1, len(grid)):
            step = step * grid[axis] + pl.program_id(axis)
        pl.when(step == 0)(start)
        pl.when(step == (mid_step if mid_step >= 0 else total + mid_step))(forward)
        body(*ins, *outs, *scratch)
        pl.when(step == total - 1)(finish)

    return hosted


def _comm_extra(comm):
    if comm is None:
        return [], [], [], [], []
    anyspec = pl.BlockSpec(memory_space=pl.ANY)
    nc = len(comm["args"])
    return comm["args"], [anyspec] * nc, [anyspec] * nc, comm["out_shape"], comm["scratch"]


def _ssm_fwd(ps, scan_ops, bblk, cblk, dskip, wglu, bglu, *, name, t=SEQ_BLOCK, comm=None):
    l = ps.shape[0]
    assert l % t == 0
    nb = l // t
    ns = t // SCAN_SUB
    c_args, c_in, c_out, c_shape, c_scratch = _comm_extra(comm)

    def body(ps_ref, lam_ref, tab_ref, tri_ref, ex_ref, b_ref, c_ref, d_ref, w_ref, bg_ref, ys_ref, chk_ref, xs_ref,
             st_ref):
        @pl.when(pl.program_id(0) == 0)
        def _():
            st_ref[...] = jnp.zeros_like(st_ref)

        chk_ref[...] = jnp.broadcast_to(st_ref[...], chk_ref.shape)
        u = ps_ref[:, :SSM_W].astype(F32)
        z = ps_ref[:, SSM_W:].astype(F32)
        xr, xi, er, ei = _ssm_states(u, st_ref[:, :SSM_N], st_ref[:, SSM_N:], lam_ref, tab_ref, tri_ref, ex_ref,
                                     b_ref[0], b_ref[1], t)
        st_ref[:, :SSM_N] = er
        st_ref[:, SSM_N:] = ei
        xr, xi = xr.astype(BF16), xi.astype(BF16)
        xs_ref[:, :SSM_N] = xr
        xs_ref[:, SSM_N:] = xi
        _, _, _, y3 = _ssm_head(u, z, xr, xi, c_ref[0], c_ref[1], d_ref[...], w_ref[...], bg_ref[...])
        sz, _ = _silu_and_grad(z)
        ys_ref[...] = (y3 * sz).astype(BF16)

    full = lambda shape: pl.BlockSpec(shape, lambda i: (0,) * len(shape))
    return pl.pallas_call(
        _with_comm(body, comm, 10, 3, nb, nb - 1), grid=(nb,),
        in_specs=[pl.BlockSpec((t, 2 * SSM_W), lambda i: (i, 0)), full((8, SSM_N)), full((12, SCAN_SUB, SSM_N)),
                  full((2, t + ns, t)), full((t, 2 * ns)), full((2, SSM_HALVES, SSM_HW, SSM_HN)),
                  full((2, SSM_HALVES, SSM_HN, SSM_HW)), full((1, SSM_W)), full((SSM_W, SSM_W)), full((1, SSM_W))] + c_in,
        out_specs=[pl.BlockSpec((t, SSM_W), lambda i: (i, 0)), pl.BlockSpec((8, 2 * SSM_N), lambda i: (i, 0)),
                   pl.BlockSpec((t, 2 * SSM_N), lambda i: (i, 0))] + c_out,
        out_shape=[jax.ShapeDtypeStruct((l, SSM_W), BF16), jax.ShapeDtypeStruct((nb * 8, 2 * SSM_N), F32),
                   jax.ShapeDtypeStruct((l, 2 * SSM_N), BF16)] + c_shape,
        scratch_shapes=[pltpu.VMEM((1, 2 * SSM_N), F32)] + c_scratch,
        compiler_params=_params("arbitrary"), name=name)(ps, *scan_ops, bblk, cblk, dskip, wglu, bglu, *c_args)


def _ssm_bwd(ps, dys, chk, states, scan_ops, bblk, cblk, dskip, wglu, bglu, *, name, t=SEQ_BLOCK, comm=None):
    l = ps.shape[0]
    assert l % t == 0
    nb = l // t
    ns = t // SCAN_SUB
    c_args, c_in, c_out, c_shape, c_scratch = _comm_extra(comm)

    def body(ps_ref, dys_ref, chk_ref, xs_ref, lam_ref, tab_ref, tri_ref, ex_ref, b_ref, c_ref, d_ref, w_ref, bg_ref,
             dps_ref, db_ref, dc_ref, dw_acc, sums_acc, gc_ref, db_acc, dc_acc):
        n = pl.program_id(0)

        @pl.when(n == 0)
        def _():
            gc_ref[...] = jnp.zeros_like(gc_ref)
            db_acc[...] = jnp.zeros_like(db_acc)
            dc_acc[...] = jnp.zeros_like(dc_acc)
            dw_acc[...] = jnp.zeros_like(dw_acc)
            sums_acc[...] = jnp.zeros_like(sums_acc)

        row = lax.broadcasted_iota(jnp.int32, (t, 1), 0)
        u = ps_ref[:, :SSM_W].astype(F32)
        z = ps_ref[:, SSM_W:].astype(F32)
        s0r, s0i = chk_ref[0:1, :SSM_N], chk_ref[0:1, SSM_N:]
        xr, xi = xs_ref[:, :SSM_N], xs_ref[:, SSM_N:]
        dskip = d_ref[...]
        y2, dgelu, gate, y3 = _ssm_head(u, z, xr, xi, c_ref[0], c_ref[1], dskip, w_ref[...], bg_ref[...])
        sz, dsz = _silu_and_grad(z)
        dys_v = dys_ref[...]
        dps_ref[:, SSM_W:] = (dys_v * y3 * dsz).astype(BF16)
        dy3 = dys_v * sz
        da = dy3 * y2 * gate * (1.0 - gate)
        dy2 = dy3 * gate + _dot(da, w_ref[...], NT)
        dw_acc[...] += _dot(y2, da, TN)
        dy = dy2 * dgelu
        sums_acc[2:3, :SSM_W] += jnp.sum(dy * u, axis=0, keepdims=True)
        sums_acc[3:4, :SSM_W] += jnp.sum(da, axis=0, keepdims=True)
        dc_acc[0] += _bd_tn(dy, xr)
        dc_acc[1] += -_bd_tn(dy, xi)
        rev_tab = tuple(tab_ref[k] for k in range(6, 12))
        gr, gi, gcr, gci = _scan_mxu(
            _bd_nt(dy, c_ref[0]), -_bd_nt(dy, c_ref[1]), rev_tab, (lam_ref[2:3, :], -lam_ref[3:4, :]),
            (lam_ref[4:5, :], -lam_ref[5:6, :]), tri_ref[1], ex_ref[...], gc_ref[:, :SSM_N], gc_ref[:, SSM_N:], t, True)
        gc_ref[:, :SSM_N] = gcr
        gc_ref[:, SSM_N:] = gci
        db_acc[0] += _bd_tn(u, gr)
        db_acc[1] += _bd_tn(u, gi)
        du = dskip * dy + _bd_nt(gr, b_ref[0]) + _bd_nt(gi, b_ref[1])
        dps_ref[:, :SSM_W] = du.astype(BF16)
        spr = jnp.where(row == 0, s0r, pltpu.roll(xr.astype(F32), 1, 0))
        spi = jnp.where(row == 0, s0i, pltpu.roll(xi.astype(F32), 1, 0))
        sums_acc[0:1, :] += jnp.sum(gr * spr + gi * spi, axis=0, keepdims=True)
        sums_acc[1:2, :] += jnp.sum(gi * spr - gr * spi, axis=0, keepdims=True)

        @pl.when(n == nb - 1)
        def _():
            per_half = SSM_GROUPS // SSM_HALVES
            for k in range(2):
                for g in range(SSM_GROUPS):
                    h, gl = divmod(g, per_half)
                    c0, p0 = gl * SSM_GROUP, gl * SSM_STATE
                    db_ref[k, g * SSM_GROUP:(g + 1) * SSM_GROUP, :] = db_acc[k, h, c0:c0 + SSM_GROUP, p0:p0 + SSM_STATE]
                    dc_ref[k, g * SSM_GROUP:(g + 1) * SSM_GROUP, :] = dc_acc[k, h, c0:c0 + SSM_GROUP, p0:p0 + SSM_STATE]

    full = lambda shape: pl.BlockSpec(shape, lambda n: (0,) * len(shape))
    return pl.pallas_call(
        _with_comm(body, comm, 13, 5, nb, 0), grid=(nb,),
        in_specs=[pl.BlockSpec((t, 2 * SSM_W), lambda n: (nb - 1 - n, 0)),
                  pl.BlockSpec((t, SSM_W), lambda n: (nb - 1 - n, 0)),
                  pl.BlockSpec((8, 2 * SSM_N), lambda n: (nb - 1 - n, 0)),
                  pl.BlockSpec((t, 2 * SSM_N), lambda n: (nb - 1 - n, 0)),
                  full((8, SSM_N)), full((12, SCAN_SUB, SSM_N)), full((2, t + ns, t)), full((t, 2 * ns)),
                  full((2, SSM_HALVES, SSM_HW, SSM_HN)), full((2, SSM_HALVES, SSM_HN, SSM_HW)), full((1, SSM_W)),
                  full((SSM_W, SSM_W)), full((1, SSM_W))] + c_in,
        out_specs=[pl.BlockSpec((t, 2 * SSM_W), lambda n: (nb - 1 - n, 0)), full((2, SSM_W, SSM_STATE)),
                   full((2, SSM_W, SSM_STATE)), full((SSM_W, SSM_W)), full((8, SSM_N))] + c_out,
        out_shape=[jax.ShapeDtypeStruct((l, 2 * SSM_W), BF16),
                   jax.ShapeDtypeStruct((2, SSM_W, SSM_STATE), F32),
                   jax.ShapeDtypeStruct((2, SSM_W, SSM_STATE), F32),
                   jax.ShapeDtypeStruct((SSM_W, SSM_W), F32),
                   jax.ShapeDtypeStruct((8, SSM_N), F32)] + c_shape,
        scratch_shapes=[pltpu.VMEM((1, 2 * SSM_N), F32), pltpu.VMEM((2, SSM_HALVES, SSM_HW, SSM_HN), F32),
                        pltpu.VMEM((2, SSM_HALVES, SSM_HW, SSM_HN), F32)] + c_scratch,
        compiler_params=_params("arbitrary"), name=name)(ps, dys, chk, states, *scan_ops, bblk, cblk, dskip, wglu, bglu,
                                                         *c_args)


def _pool_count(i, t):
    pos = lax.broadcasted_iota(jnp.int32, (t, POOL_W), 0) + i * t + 1
    col = lax.broadcasted_iota(jnp.int32, (t, POOL_W), 1)
    win = jnp.where(col < POOL_GW, 2, jnp.where(col < 2 * POOL_GW, 4, jnp.where(col < 3 * POOL_GW, 8, 16)))
    return 1.0 / jnp.minimum(pos, win).astype(F32), col


def _window_sums(ext, n_rows, forward):
    col = lax.broadcasted_iota(jnp.int32, ext.shape, 1)
    sh = (lambda a, d: pltpu.roll(a, d, 0)) if forward else (lambda a, d: pltpu.roll(a, n_rows - d, 0))
    a2 = ext + sh(ext, 1)
    a4 = a2 + sh(a2, 2)
    a8 = a4 + sh(a4, 4)
    a16 = a8 + sh(a8, 8)
    return jnp.where(col < POOL_GW, a2, jnp.where(col < 2 * POOL_GW, a4, jnp.where(col < 3 * POOL_GW, a8, a16)))


def _pool_mix(pooled, wp_ref):
    return jnp.concatenate([_dot(pooled[:, g * POOL_GW:(g + 1) * POOL_GW], wp_ref[g]) for g in range(4)], axis=1)


def _pool_pooled(i, cur_u, prev_u, t):
    prev = jnp.where(i > 0, prev_u, 0.0)
    ext = jnp.concatenate([prev, cur_u], axis=0)
    inv_cnt, _ = _pool_count(i, t)
    return _window_sums(ext, t + POOL_HALO, True)[POOL_HALO:, :] * inv_cnt - cur_u


def _pool_fwd(pp, wpool, pscale, *, name, t=SEQ_BLOCK):
    l = pp.shape[0]
    t = min(t, l)

    def body(cur_ref, prev_ref, wp_ref, sc_ref, yp_ref):
        i = pl.program_id(0)
        pooled = _pool_pooled(i, cur_ref[:, :POOL_W].astype(F32), prev_ref[...].astype(F32), t)
        lin = _pool_mix(pooled, wp_ref)
        sz, _ = _silu_and_grad(cur_ref[:, POOL_W:].astype(F32))
        yp_ref[...] = (lin * sc_ref[...] * sz).astype(BF16)

    hb = t // POOL_HALO
    return pl.pallas_call(
        body, grid=(l // t,),
        in_specs=[pl.BlockSpec((t, 2 * POOL_W), lambda i: (i, 0)),
                  pl.BlockSpec((POOL_HALO, POOL_W), lambda i: (jnp.maximum(i * hb - 1, 0), 0)),
                  pl.BlockSpec((4, POOL_GW, POOL_GW), lambda i: (0, 0, 0)),
                  pl.BlockSpec((1, POOL_W), lambda i: (0, 0))],
        out_specs=pl.BlockSpec((t, POOL_W), lambda i: (i, 0)),
        out_shape=jax.ShapeDtypeStruct((l, POOL_W), BF16),
        compiler_params=_params("parallel"), name=name)(pp, pp, wpool, pscale)


def _pool_bwd(pp, dyp, wpool, pscale, *, name, t=SEQ_BLOCK):
    l = pp.shape[0]
    t = min(t, l)
    nb = l // t

    def body(cur_ref, prev_ref, dyp_ref, wp_ref, sc_ref, dpp_ref, dwp_ref, sums_ref, carry_ref):
        n = pl.program_id(0)
        i = nb - 1 - n

        @pl.when(n == 0)
        def _():
            carry_ref[...] = jnp.zeros_like(carry_ref)
            dwp_ref[...] = jnp.zeros_like(dwp_ref)
            sums_ref[...] = jnp.zeros_like(sums_ref)

        cur_u = cur_ref[:, :POOL_W].astype(F32)
        pooled = _pool_pooled(i, cur_u, prev_ref[...].astype(F32), t)
        lin = _pool_mix(pooled, wp_ref)
        sz, dsz = _silu_and_grad(cur_ref[:, POOL_W:].astype(F32))
        dyp_v = dyp_ref[...]
        scale = sc_ref[...]
        dpp_ref[:, POOL_W:] = (dyp_v * lin * scale * dsz).astype(BF16)
        dpre = dyp_v * sz
        sums_ref[0:1, :] += jnp.sum(dpre * lin, axis=0, keepdims=True)
        dlin = dpre * scale
        dpooled = []
        for g in range(4):
            dl = dlin[:, g * POOL_GW:(g + 1) * POOL_GW]
            dwp_ref[g] += _dot(pooled[:, g * POOL_GW:(g + 1) * POOL_GW], dl, TN)
            dpooled.append(_dot(dl, wp_ref[g], NT))
        dpooled = jnp.concatenate(dpooled, axis=1)
        inv_cnt, _ = _pool_count(i, t)
        dq = dpooled * inv_cnt
        ext = jnp.concatenate([dq, carry_ref[...]], axis=0)
        du = _window_sums(ext, t + POOL_HALO, False)[:t, :] - dpooled
        dpp_ref[:, :POOL_W] = du.astype(BF16)
        carry_ref[...] = dq[:POOL_HALO, :]

    hb = t // POOL_HALO
    return pl.pallas_call(
        body, grid=(nb,),
        in_specs=[pl.BlockSpec((t, 2 * POOL_W), lambda n: (nb - 1 - n, 0)),
                  pl.BlockSpec((POOL_HALO, POOL_W), lambda n: (jnp.maximum((nb - 1 - n) * hb - 1, 0), 0)),
                  pl.BlockSpec((t, POOL_W), lambda n: (nb - 1 - n, 0)),
                  pl.BlockSpec((4, POOL_GW, POOL_GW), lambda n: (0, 0, 0)),
                  pl.BlockSpec((1, POOL_W), lambda n: (0, 0))],
        out_specs=[pl.BlockSpec((t, 2 * POOL_W), lambda n: (nb - 1 - n, 0)),
                   pl.BlockSpec((4, POOL_GW, POOL_GW), lambda n: (0, 0, 0)),
                   pl.BlockSpec((8, POOL_W), lambda n: (0, 0))],
        out_shape=[jax.ShapeDtypeStruct((l, 2 * POOL_W), BF16), jax.ShapeDtypeStruct((4, POOL_GW, POOL_GW), F32),
                   jax.ShapeDtypeStruct((8, POOL_W), F32)],
        scratch_shapes=[pltpu.VMEM((POOL_HALO, POOL_W), F32)],
        compiler_params=_params("arbitrary"), name=name)(pp, pp, dyp, wpool, pscale)


def _merge_fwd(ya, ys, yp, wa, ws, wp, pg, *, name, tm=512):
    l = ya.shape[0]
    tm = min(tm, l)
    d = D_MODEL

    def body(ya_ref, ys_ref, yp_ref, wa_ref, ws_ref, wp_ref, pg_ref, mg_ref, ba_ref, bs_ref, bp_ref):
        acc = None
        for k, (y_ref, w_ref, b_ref) in enumerate(((ya_ref, wa_ref, ba_ref), (ys_ref, ws_ref, bs_ref),
                                                   (yp_ref, wp_ref, bp_ref))):
            br = _dot(y_ref[...], w_ref[...])
            b_ref[...] = br.astype(BF16)
            term = _sigmoid(pg_ref[:, k * d:(k + 1) * d].astype(F32)) * br
            acc = term if acc is None else acc + term
        mg_ref[...] = acc.astype(BF16)

    rowy = pl.BlockSpec((tm, ATT_W), lambda i: (i, 0))
    wsp = pl.BlockSpec((ATT_W, d), lambda i: (0, 0))
    rowd = pl.BlockSpec((tm, d), lambda i: (i, 0))
    return pl.pallas_call(
        body, grid=(l // tm,),
        in_specs=[rowy, rowy, rowy, wsp, wsp, wsp, pl.BlockSpec((tm, 3 * d), lambda i: (i, 0))],
        out_specs=[rowd, rowd, rowd, rowd],
        out_shape=[jax.ShapeDtypeStruct((l, d), BF16)] * 4,
        compiler_params=_params("parallel"), name=name)(ya, ys, yp, wa, ws, wp, pg)


def _out_fwd(merged, wout, x, gate, *, name, tm=512):
    l, d = x.shape
    tm = min(tm, l)

    def body(m_ref, w_ref, x_ref, g_ref, xn_ref, out_ref):
        out = _dot(m_ref[...], w_ref[...])
        out_ref[...] = out.astype(BF16)
        xn_ref[...] = x_ref[...] + g_ref[...] * out

    row = pl.BlockSpec((tm, d), lambda i: (i, 0))
    return pl.pallas_call(
        body, grid=(l // tm,),
        in_specs=[row, pl.BlockSpec((d, d), lambda i: (0, 0)), row, pl.BlockSpec((1, d), lambda i: (0, 0))],
        out_specs=[row, row],
        out_shape=[jax.ShapeDtypeStruct((l, d), F32), jax.ShapeDtypeStruct((l, d), BF16)],
        compiler_params=_params("parallel"), name=name)(merged, wout, x, gate)


def _merge_bwd(dx, out, gate, wout, pg, ba, bs, bp, *, name, tm=512):
    l, d = dx.shape
    tm = min(tm, l)

    def body(dx_ref, out_ref, g_ref, w_ref, pg_ref, ba_ref, bs_ref, bp_ref,
             dmo_ref, dba_ref, dbs_ref, dbp_ref, dpg_ref, sums_ref):
        @pl.when(pl.program_id(0) == 0)
        def _():
            sums_ref[...] = jnp.zeros_like(sums_ref)

        dxv = dx_ref[...]
        sums_ref[0:1, :] += jnp.sum(dxv * out_ref[...].astype(F32), axis=0, keepdims=True)
        dmo = (dxv * g_ref[...]).astype(BF16)
        dmo_ref[...] = dmo
        dmerged = _dot(dmo, w_ref[...], NT)
        for k, (b_ref, db_ref) in enumerate(((ba_ref, dba_ref), (bs_ref, dbs_ref), (bp_ref, dbp_ref))):
            gk = _sigmoid(pg_ref[:, k * d:(k + 1) * d].astype(F32))
            db_ref[...] = (dmerged * gk).astype(BF16)
            dpg_ref[:, k * d:(k + 1) * d] = (dmerged * b_ref[...].astype(F32) * gk * (1.0 - gk)).astype(BF16)

    row = pl.BlockSpec((tm, d), lambda i: (i, 0))
    wide = pl.BlockSpec((tm, 3 * d), lambda i: (i, 0))
    return pl.pallas_call(
        body, grid=(l // tm,),
        in_specs=[row, row, pl.BlockSpec((1, d), lambda i: (0, 0)), pl.BlockSpec((d, d), lambda i: (0, 0)),
                  wide, row, row, row],
        out_specs=[row, row, row, row, wide, pl.BlockSpec((8, d), lambda i: (0, 0))],
        out_shape=[jax.ShapeDtypeStruct((l, d), BF16)] * 4 + [jax.ShapeDtypeStruct((l, 3 * d), BF16),
                                                             jax.ShapeDtypeStruct((8, d), F32)],
        compiler_params=_params("arbitrary"), name=name)(dx, out, gate, wout, pg, ba, bs, bp)


def _adamw(w, g, m, v, *, name, tr=256):
    r, c = w.shape
    p = g.shape[0]
    tr = min(tr, r)
    assert r % tr == 0
    c1 = 1.0 / (1.0 - ADAM_B1 ** ADAM_STEP)
    c2 = 1.0 / (1.0 - ADAM_B2 ** ADAM_STEP)

    def body(w_ref, g_ref, m_ref, v_ref, go_ref, d_ref, mo_ref, vo_ref):
        gv = g_ref[0].astype(F32)
        for k in range(1, p):
            gv = gv + g_ref[k].astype(F32)
        go_ref[...] = gv
        mn = ADAM_B1 * m_ref[...] + (1.0 - ADAM_B1) * gv
        vn = ADAM_B2 * v_ref[...] + (1.0 - ADAM_B2) * (gv * gv)
        mo_ref[...] = mn
        vo_ref[...] = vn
        d_ref[...] = -ADAM_LR * ((mn * c1) / (jnp.sqrt(vn * c2) + ADAM_EPS) + ADAM_WD * w_ref[...])

    row = pl.BlockSpec((tr, c), lambda i: (i, 0))
    return pl.pallas_call(
        body, grid=(r // tr,),
        in_specs=[row, pl.BlockSpec((p, tr, c), lambda i: (0, i, 0)), row, row],
        out_specs=[row] * 4,
        out_shape=[jax.ShapeDtypeStruct((r, c), F32)] * 4,
        compiler_params=_params("parallel"), name=name)(w, g, m, v)


def _exchange(arrs, *, scatter, name):
    n = len(arrs)
    out_shape = [jax.ShapeDtypeStruct(a.shape if scatter else (N_DEV,) + a.shape, a.dtype) for a in arrs]

    def body(*refs):
        ins, outs = refs[:n], refs[n:2 * n]
        send_sems, recv_sems, loc_sems = refs[2 * n:]
        me = 4 * lax.axis_index("x") + 2 * lax.axis_index("y") + lax.axis_index("c")
        local = []
        for k in range(n):
            src = ins[k].at[me] if scatter else ins[k]
            cp = pltpu.make_async_copy(src, outs[k].at[me], loc_sems.at[k])
            cp.start()
            local.append(cp)
        remote = []
        for r in range(1, N_DEV):
            peer = me ^ r
            for k in range(n):
                src = ins[k].at[peer] if scatter else ins[k]
                cp = pltpu.make_async_remote_copy(
                    src_ref=src, dst_ref=outs[k].at[me], send_sem=send_sems.at[k, r - 1], recv_sem=recv_sems.at[k, r - 1],
                    device_id=(peer // 4, (peer // 2) % 2, peer % 2), device_id_type=pl.DeviceIdType.MESH)
                cp.start()
                remote.append(cp)
        for cp in remote:
            cp.wait()
        for cp in local:
            cp.wait()

    anyspec = pl.BlockSpec(memory_space=pl.ANY)
    return pl.pallas_call(
        body, in_specs=[anyspec] * n, out_specs=[anyspec] * n, out_shape=out_shape,
        scratch_shapes=[pltpu.SemaphoreType.DMA((n, N_DEV - 1)), pltpu.SemaphoreType.DMA((n, N_DEV - 1)),
                        pltpu.SemaphoreType.DMA((n,))],
        name=name)(*arrs)


def _mesh_place():
    x, y, c = lax.axis_index("x"), lax.axis_index("y"), lax.axis_index("c")
    other_chips = [(1 - x, y), (x, 1 - y), (1 - x, 1 - y)]
    return x, y, c, other_chips


def _gather_two_level(arrs, *, name):
    n = len(arrs)
    plan = _gather_plan(arrs)

    def body(*refs):
        start, forward, finish = plan["phases"](refs[:n], refs[n:2 * n], *refs[2 * n:])
        start()
        forward()
        finish()

    anyspec = pl.BlockSpec(memory_space=pl.ANY)
    return pl.pallas_call(
        body, in_specs=[anyspec] * n, out_specs=[anyspec] * n, out_shape=plan["out_shape"],
        scratch_shapes=plan["scratch"], name=name)(*arrs)


def _gather_plan(arrs):
    n = len(arrs)

    def phases(ins, outs, send_sems, recv_sems, loc_sems):
        x, y, c, chips = _mesh_place()
        me = 4 * x + 2 * y + c
        slot = lambda px, py, pc: 4 * px + 2 * py + pc

        def copy(k, j, src, block, to):
            return pltpu.make_async_remote_copy(
                src_ref=src, dst_ref=outs[k].at[block], send_sem=send_sems.at[k, j], recv_sem=recv_sems.at[k, j],
                device_id=to, device_id_type=pl.DeviceIdType.MESH)

        local = [pltpu.make_async_copy(ins[k], outs[k].at[me], loc_sems.at[k]) for k in range(n)]
        first = []
        for k in range(n):
            first.append(copy(k, 0, ins[k], me, (x, y, 1 - c)))
            for j, chip in enumerate(chips):
                first.append(copy(k, 1 + j, ins[k], me, (*chip, c)))
        passed = [copy(k, 4 + j, outs[k].at[slot(*chip, c)], slot(*chip, c), (x, y, 1 - c))
                  for j, chip in enumerate(chips) for k in range(n)]

        def start():
            for cp in local + first:
                cp.start()

        def forward():
            for j, chip in enumerate(chips):
                for k in range(n):
                    copy(k, 1 + j, ins[k], slot(*chip, c), (x, y, c)).wait_recv()
                    passed[j * n + k].start()

        def finish():
            for k in range(n):
                copy(k, 0, ins[k], slot(x, y, 1 - c), (x, y, c)).wait_recv()
                for j, chip in enumerate(chips):
                    copy(k, 4 + j, ins[k], slot(*chip, 1 - c), (x, y, c)).wait_recv()
            for cp in first + passed:
                cp.wait_send()
            for cp in local:
                cp.wait()

        return start, forward, finish

    return dict(
        args=list(arrs), out_shape=[jax.ShapeDtypeStruct((N_DEV,) + a.shape, a.dtype) for a in arrs],
        scratch=[pltpu.SemaphoreType.DMA((n, 7)), pltpu.SemaphoreType.DMA((n, 7)), pltpu.SemaphoreType.DMA((n,))],
        phases=phases)


def _allreduce_small(small, extra, *, name):
    r, lanes = small.shape
    assert r % 16 == 0
    h = r // 2
    e = extra.shape[0]

    def body(s_ref, x_ref, out_ref, xall_ref, sib_ref, parts_ref, send_sems, recv_sems):
        x, y, c, chips = _mesh_place()
        me = 4 * x + 2 * y + c
        my_chip = 2 * x + y
        sibling = (x, y, 1 - c)
        mine = pl.ds(pl.multiple_of(c * h, 8), h)
        theirs = pl.ds(pl.multiple_of((1 - c) * h, 8), h)

        def remote(j, src, dst, to):
            return pltpu.make_async_remote_copy(src_ref=src, dst_ref=dst, send_sem=send_sems.at[j],
                                                recv_sem=recv_sems.at[j], device_id=to, device_id_type=pl.DeviceIdType.MESH)

        to_sibling = remote(0, s_ref.at[theirs], sib_ref, sibling)
        to_sibling.start()
        xall_ref[me] = x_ref[...]
        extras = []
        for rr in range(1, N_DEV):
            peer = me ^ rr
            cp = remote(4 + rr, x_ref, xall_ref.at[me], (peer // 4, (peer // 2) % 2, peer % 2))
            cp.start()
            extras.append(cp)
        to_sibling.wait_recv()
        parts_ref[my_chip] = s_ref[mine] + sib_ref[...]
        to_chips = [remote(1 + j, parts_ref.at[my_chip], parts_ref.at[my_chip], (px, py, c))
                    for j, (px, py) in enumerate(chips)]
        for cp in to_chips:
            cp.start()
        for cp in to_chips:
            cp.wait_recv()
        out_ref[mine] = (parts_ref[0] + parts_ref[1]) + (parts_ref[2] + parts_ref[3])
        done = remote(4, out_ref.at[mine], out_ref.at[mine], sibling)
        done.start()
        remote(4, out_ref.at[theirs], out_ref.at[theirs], sibling).wait_recv()
        for cp in extras:
            cp.wait()
        to_sibling.wait_send()
        for cp in to_chips:
            cp.wait_send()
        done.wait_send()

    vmem = pl.BlockSpec(memory_space=pltpu.VMEM)
    return pl.pallas_call(
        body, in_specs=[vmem, vmem], out_specs=[vmem, vmem],
        out_shape=[jax.ShapeDtypeStruct((r, lanes), F32), jax.ShapeDtypeStruct((N_DEV, e, lanes), F32)],
        scratch_shapes=[pltpu.VMEM((h, lanes), F32), pltpu.VMEM((4, h, lanes), F32),
                        pltpu.SemaphoreType.DMA((12,)), pltpu.SemaphoreType.DMA((12,))],
        compiler_params=pltpu.CompilerParams(vmem_limit_bytes=VMEM_LIMIT), name=name)(small, extra)


def _sibling_swap(arrs, *, name):
    n = len(arrs)
    out_shape = [jax.ShapeDtypeStruct(a.shape[1:], a.dtype) for a in arrs]

    def body(*refs):
        ins, outs = refs[:n], refs[n:2 * n]
        send_sems, recv_sems = refs[2 * n:]
        x, y, c, _ = _mesh_place()
        copies = [pltpu.make_async_remote_copy(
            src_ref=ins[k].at[1 - c], dst_ref=outs[k], send_sem=send_sems.at[k], recv_sem=recv_sems.at[k],
            device_id=(x, y, 1 - c), device_id_type=pl.DeviceIdType.MESH) for k in range(n)]
        for cp in copies:
            cp.start()
        for cp in copies:
            cp.wait()

    anyspec = pl.BlockSpec(memory_space=pl.ANY)
    return pl.pallas_call(
        body, in_specs=[anyspec] * n, out_specs=[anyspec] * n, out_shape=out_shape,
        scratch_shapes=[pltpu.SemaphoreType.DMA((n,)), pltpu.SemaphoreType.DMA((n,))], name=name)(*arrs)


def _pair_add(mine, theirs, core, *, name, tr=256):
    _, r, c = mine.shape
    tr = min(tr, r)
    assert r % tr == 0

    def body(core_ref, m_ref, t_ref, o_ref):
        o_ref[...] = (m_ref[0].astype(F32) + t_ref[...].astype(F32)).astype(BF16)

    return pl.pallas_call(
        body,
        grid_spec=pltpu.PrefetchScalarGridSpec(
            num_scalar_prefetch=1, grid=(r // tr,),
            in_specs=[pl.BlockSpec((1, tr, c), lambda i, core_ref: (core_ref[0], i, 0)),
                      pl.BlockSpec((tr, c), lambda i, core_ref: (i, 0))],
            out_specs=pl.BlockSpec((tr, c), lambda i, core_ref: (i, 0))),
        out_shape=jax.ShapeDtypeStruct((r, c), BF16),
        compiler_params=_params("parallel"), name=name)(core, mine, theirs)


def _chip_scatter(arrs, *, name):
    n = len(arrs)
    plan = _chip_scatter_plan(arrs)

    def body(*refs):
        start, _, finish = plan["phases"](refs[:n], refs[n:2 * n], *refs[2 * n:])
        start()
        finish()

    anyspec = pl.BlockSpec(memory_space=pl.ANY)
    return pl.pallas_call(
        body, in_specs=[anyspec] * n, out_specs=[anyspec] * n, out_shape=plan["out_shape"],
        scratch_shapes=plan["scratch"], name=name)(*arrs)


def _chip_scatter_plan(arrs):
    n = len(arrs)

    def phases(ins, outs, send_sems, recv_sems, loc_sems):
        x, y, c, chips = _mesh_place()
        mine = 2 * x + y
        local = [pltpu.make_async_copy(ins[k].at[mine], outs[k].at[mine], loc_sems.at[k]) for k in range(n)]
        remote = [pltpu.make_async_remote_copy(
            src_ref=ins[k].at[2 * px + py], dst_ref=outs[k].at[mine], send_sem=send_sems.at[k, j],
            recv_sem=recv_sems.at[k, j], device_id=(px, py, c), device_id_type=pl.DeviceIdType.MESH)
            for j, (px, py) in enumerate(chips) for k in range(n)]

        def start():
            for cp in local + remote:
                cp.start()

        def finish():
            for cp in remote:
                cp.wait()
            for cp in local:
                cp.wait()

        return start, (lambda: None), finish

    return dict(
        args=list(arrs), out_shape=[jax.ShapeDtypeStruct(a.shape, a.dtype) for a in arrs],
        scratch=[pltpu.SemaphoreType.DMA((n, 3)), pltpu.SemaphoreType.DMA((n, 3)), pltpu.SemaphoreType.DMA((n,))],
        phases=phases)


def _ssm_discretize(a_re, a_im, log_dt, b_re, b_im):
    dt = jnp.exp(log_dt)[:, None]
    mag = jnp.exp(a_re * dt)
    lr = mag * jnp.cos(a_im * dt)
    li = mag * jnp.sin(a_im * dt)
    den = a_re * a_re + a_im * a_im
    cr = ((lr - 1.0) * a_re + li * a_im) / den
    ci = (li * a_re - (lr - 1.0) * a_im) / den
    bbr = cr[..., None] * b_re - ci[..., None] * b_im
    bbi = cr[..., None] * b_im + ci[..., None] * b_re
    return lr, li, bbr, bbi


def _ssm_dense(lr, li, bbr, bbi, c_re, c_im):
    scan_ops = _scan_tables(lr.reshape(-1), li.reshape(-1)) + _scan_consts(SEQ_BLOCK)
    per_half = SSM_GROUPS // SSM_HALVES

    def halves(a, rows, cols):
        a = a.reshape(SSM_HALVES, per_half * rows, cols)
        tiled = jnp.tile(a, (1, 1, per_half))
        r = lax.broadcasted_iota(jnp.int32, tiled.shape, 1) // rows
        c = lax.broadcasted_iota(jnp.int32, tiled.shape, 2) // cols
        return jnp.where(r == c, tiled, 0.0)

    bblk = jnp.stack([halves(b.transpose(0, 2, 1), SSM_GROUP, SSM_STATE) for b in (bbr, bbi)]).astype(BF16)
    cblk = jnp.stack([halves(c.transpose(0, 2, 1), SSM_STATE, SSM_GROUP) for c in (c_re, c_im)]).astype(BF16)
    return scan_ops, bblk, cblk


def _ssm_extract(db, dc, sums):
    db = db.reshape(2, SSM_GROUPS, SSM_GROUP, SSM_STATE).transpose(0, 1, 3, 2)
    dc = dc.reshape(2, SSM_GROUPS, SSM_GROUP, SSM_STATE)
    dlr = sums[0].reshape(SSM_GROUPS, SSM_STATE)
    dli = sums[1].reshape(SSM_GROUPS, SSM_STATE)
    return dlr, dli, db[0], db[1], dc[0], dc[1]


IN_SPLITS = (ATT_W, KV_W, KV_W, SSM_W, POOL_W, ATT_W, SSM_W, POOL_W, 3 * D_MODEL)


def _split_w_in(w):
    idx = [0]
    for s in IN_SPLITS:
        idx.append(idx[-1] + s)
    seg = [w[..., idx[k]:idx[k + 1]] for k in range(len(IN_SPLITS))]
    q, k, v, us, up, za, zs, zp, gl = seg
    return (jnp.concatenate([q, za, k, v], axis=-1), jnp.concatenate([us, zs], axis=-1),
            jnp.concatenate([up, zp], axis=-1), gl)


def _merge_w_in(da, ds, dp, dg):
    q, za, k, v = da[..., :ATT_W], da[..., ATT_W:2 * ATT_W], da[..., 2 * ATT_W:2 * ATT_W + KV_W], da[..., 2 * ATT_W + KV_W:]
    us, zs = ds[..., :SSM_W], ds[..., SSM_W:]
    up, zp = dp[..., :POOL_W], dp[..., POOL_W:]
    return jnp.concatenate([q, k, v, us, up, za, zs, zp, dg], axis=-1)


def _layer_fwd(x, lw, li, late=None, comm_attn=None, comm_ssm=None):
    tag = f"l{li}"
    h = _ln_fwd(x, lw["norm_g"], lw["shift"], lw["scale"], name=f"ln_fwd_{tag}")
    pa = _mm(h, lw["w_a"], tn=1280, out_dtype=BF16, name=f"proj_a_{tag}")
    ps = _mm(h, lw["w_s"], out_dtype=BF16, name=f"proj_s_{tag}")
    pp = _mm(h, lw["w_p"], out_dtype=BF16, name=f"proj_p_{tag}")
    if late is None:
        pg = _mm(h, lw["w_g"], out_dtype=BF16, name=f"proj_g_{tag}")
    else:
        pg, arrived = _mm(h, lw["w_g"], out_dtype=BF16, name=f"proj_g_{tag}", comm=late[0])
        lw = {**lw, **late[1](arrived)}
    ya, from_attn = _attn_fwd(pa, lw["sinks"], name=f"attn_fwd_{tag}", comm=comm_attn)
    ys, chk, states, *from_ssm = _ssm_fwd(ps, lw["lam"], lw["bblk"], lw["cblk"], lw["ssm_d"], lw["w_glu"], lw["b_glu"],
                                          name=f"ssm_fwd_{tag}", comm=comm_ssm)
    yp = _pool_fwd(pp, lw["w_pool"], lw["pool_scale"], name=f"pool_fwd_{tag}")
    merged, ba, bs, bp = _merge_fwd(ya, ys, yp, lw["w_br_att"], lw["w_br_ssm"], lw["w_br_pool"], pg, name=f"merge_fwd_{tag}")
    x_new, out = _out_fwd(merged, lw["w_out"], x, lw["gate"], name=f"out_fwd_{tag}")
    saved = dict(x=x, h=h, pa=pa, ps=ps, pp=pp, pg=pg, ya=ya, ys=ys, yp=yp, chk=chk, states=states, merged=merged,
                 ba=ba, bs=bs, bp=bp, out=out)
    return x_new, saved, lw, list(from_attn), list(from_ssm)


def _layer_bwd(dx, lw, sv, li, comm=None, own=None):
    tag = f"l{li}"
    dmo, dba, dbs, dbp, dpg, gate_sums = _merge_bwd(dx, sv["out"], lw["gate"], lw["w_out"], sv["pg"],
                                                    sv["ba"], sv["bs"], sv["bp"], name=f"merge_bwd_{tag}")
    g = {}
    g["w_out"] = _mm_tn(sv["merged"], dmo, out_dtype=BF16, name=f"dw_out_{tag}")
    dya = _mm(dba, lw["w_br_att"], nt=True, name=f"dy_att_{tag}")
    dys = _mm(dbs, lw["w_br_ssm"], nt=True, name=f"dy_ssm_{tag}")
    dyp = _mm(dbp, lw["w_br_pool"], nt=True, name=f"dy_pool_{tag}")
    g["w_br_att"] = _mm_tn(sv["ya"], dba, out_dtype=BF16, name=f"dw_br_att_{tag}")
    g["w_br_ssm"] = _mm_tn(sv["ys"], dbs, out_dtype=BF16, name=f"dw_br_ssm_{tag}")
    g["w_br_pool"] = _mm_tn(sv["yp"], dbp, out_dtype=BF16, name=f"dw_br_pool_{tag}")
    dpa, dsink = _attn_bwd(sv["pa"], lw["sinks"], dya, name=f"attn_bwd_{tag}")
    dps, db_dense, dc_dense, dwglu, ssm_sums, *exchanged = _ssm_bwd(
        sv["ps"], dys, sv["chk"], sv["states"], lw["lam"], lw["bblk"], lw["cblk"], lw["ssm_d"], lw["w_glu"], lw["b_glu"],
        name=f"ssm_bwd_{tag}", comm=comm)
    g["w_glu"] = dwglu.astype(BF16)
    dpp, dwpool, pool_sums = _pool_bwd(sv["pp"], dyp, lw["w_pool"], lw["pool_scale"], name=f"pool_bwd_{tag}")
    h = sv["h"]
    dw_a = _mm_tn(h, dpa, out_dtype=BF16, tn=1280, name=f"dw_a_{tag}")
    dw_s = _mm_tn(h, dps, out_dtype=BF16, name=f"dw_s_{tag}")
    dw_p = _mm_tn(h, dpp, out_dtype=BF16, name=f"dw_p_{tag}")
    dh_pairs = [(dpa, lw["w_a"]), (dps, lw["w_s"]), (dpp, lw["w_p"]), (dpg, lw["w_g"])]
    if own is None:
        dw_g, from_late = _mm_tn(h, dpg, out_dtype=BF16, name=f"dw_g_{tag}"), []
        g["w_in"] = _merge_w_in(dw_a, dw_s, dw_p, dw_g)
        dh, from_w_in = _mm_nt_sum(dh_pairs, name=f"dh_{tag}"), []
    else:
        dw_g, from_late = _mm_tn(h, dpg, out_dtype=BF16, name=f"dw_g_{tag}", comm=own({k: g[k] for k in LATE_WEIGHTS}))
        g["w_in"] = _merge_w_in(dw_a, dw_s, dw_p, dw_g)
        dh, from_w_in = _mm_nt_sum(dh_pairs, name=f"dh_{tag}", comm=own({"w_in": g["w_in"]}))
    dx_in, ln_sums = _ln_bwd(sv["x"], dh, dx, lw["norm_g"], lw["scale"], name=f"ln_bwd_{tag}")
    g["dmod"] = jnp.concatenate([ln_sums[0], ln_sums[1], gate_sums[0]])
    g["norm_g"] = ln_sums[2]
    g["attn_sinks"] = dsink[:, 0]
    g["ssm_raw"] = _ssm_extract(db_dense, dc_dense, ssm_sums)
    g["ssm_d"] = ssm_sums[2, :SSM_W]
    g["b_glu"] = ssm_sums[3, :SSM_W]
    g["w_pool"] = dwpool
    g["pool_scale"] = pool_sums[0]
    return dx_in, g, exchanged, list(from_w_in) + list(from_late)


BIG_WEIGHTS = ("w_in", "w_glu", "w_br_att", "w_br_ssm", "w_br_pool", "w_out")
ROW_SHARDED = ("w_glu", "w_out")


LATE_WEIGHTS = BIG_WEIGHTS[1:]


def _full_weights(keys, gathered):
    full = {}
    for k, g in zip(keys, gathered):
        if k in ROW_SHARDED:
            full[k] = g.reshape(N_DEV * g.shape[1], g.shape[2])
        else:
            full[k] = g.transpose(1, 0, 2).reshape(g.shape[1], N_DEV * g.shape[2])
    return full


def _by_destination(keys, grads):
    out = []
    for k in keys:
        g = grads[k]
        if k in ROW_SHARDED:
            out.append(g.reshape(4, 2, g.shape[0] // N_DEV, g.shape[1]).transpose(1, 0, 2, 3))
        else:
            out.append(g.reshape(g.shape[0], 4, 2, g.shape[1] // N_DEV).transpose(2, 1, 0, 3))
    return out


def _prepare_layer(li, mod, norm_g, w_in_full, attn_sinks, disc, ssm_c_re, ssm_c_im, ssm_d, b_glu, w_pool, pool_scale):
    d = D_MODEL
    lr, li_, bbr, bbi = disc
    lam, bblk, cblk = _ssm_dense(lr[li], li_[li], bbr[li], bbi[li], ssm_c_re[li], ssm_c_im[li])
    w_a, w_s, w_p, w_g = _split_w_in(w_in_full)
    return dict(
        norm_g=norm_g[li][None, :], shift=mod[li, :d][None, :], scale=mod[li, d:2 * d][None, :],
        gate=mod[li, 2 * d:][None, :], w_a=w_a, w_s=w_s, w_p=w_p, w_g=w_g,
        sinks=attn_sinks[li], lam=lam, bblk=bblk, cblk=cblk, ssm_d=ssm_d[li][None, :],
        b_glu=b_glu[li][None, :], w_pool=w_pool[li].astype(BF16), pool_scale=pool_scale[li][None, :])


SMALL_ROWS = 64
SMALL_ORDER = ("norm_g", "attn_sinks", "ssm_d", "b_glu", "w_pool", "pool_scale", "dmod")


def _pack_small(loss, dfinal_g, layer_grads):
    parts = [jnp.broadcast_to(loss.reshape(1), (128,)), dfinal_g]
    for g in layer_grads:
        for k in SMALL_ORDER:
            v = g[k].reshape(-1)
            if v.shape[0] % 128:
                v = jnp.pad(v, (0, 128 - v.shape[0] % 128))
            parts.append(v)
        for v in g["ssm_raw"]:
            parts.append(v.reshape(-1))
    flat = jnp.concatenate(parts)
    return jnp.pad(flat, (0, (-flat.shape[0]) % (SMALL_ROWS * 128))).reshape(-1, 128)


def _unpack_small(flat, shapes):
    out, off = [], 0
    for s in shapes:
        n = int(math.prod(s))
        out.append(flat[off:off + n].reshape(s))
        off += n + (-n) % 128
    return out


def kernel(x, c, norm_g, w_ada, b_ada, w_in, attn_sinks, ssm_a_re, ssm_a_im, ssm_log_dt, ssm_b_re, ssm_b_im, ssm_c_re, ssm_c_im, ssm_d, w_glu, b_glu, w_pool, pool_scale, w_br_att, w_br_ssm, w_br_pool, w_out, final_g, loss_target, m_norm_g, m_w_ada, m_b_ada, m_w_in, m_attn_sinks, m_ssm_a_re, m_ssm_a_im, m_ssm_log_dt, m_ssm_b_re, m_ssm_b_im, m_ssm_c_re, m_ssm_c_im, m_ssm_d, m_w_glu, m_b_glu, m_w_pool, m_pool_scale, m_w_br_att, m_w_br_ssm, m_w_br_pool, m_w_out, m_final_g, v_norm_g, v_w_ada, v_b_ada, v_w_in, v_attn_sinks, v_ssm_a_re, v_ssm_a_im, v_ssm_log_dt, v_ssm_b_re, v_ssm_b_im, v_ssm_c_re, v_ssm_c_im, v_ssm_d, v_w_glu, v_b_glu, v_w_pool, v_pool_scale, v_w_br_att, v_w_br_ssm, v_w_br_pool, v_w_out, v_final_g):
    me = 4 * lax.axis_index("x") + 2 * lax.axis_index("y") + lax.axis_index("c")
    d = D_MODEL
    ada_w = 3 * d // N_DEV

    (c_all,) = _exchange([c.reshape(8, 128)], scatter=False, name="gather_c")
    c_act = jax.nn.silu(c_all.reshape(N_DEV, d))
    b_cols = lax.dynamic_slice(b_ada, (0, me * ada_w), (DEPTH, ada_w))
    mod_part = jnp.concatenate(
        [_mm(c_act, w_ada[li], name=f"ada_fwd_l{li}") + b_cols[li][None, :] for li in range(DEPTH)], axis=0)
    (mod_all,) = _exchange([mod_part], scatter=False, name="gather_mod")
    mod_all = mod_all.reshape(N_DEV, DEPTH, N_DEV, ada_w)
    mod_mine = lax.dynamic_index_in_dim(mod_all, me, axis=2, keepdims=False)
    mod_mine = mod_mine.transpose(1, 0, 2).reshape(DEPTH, 3 * d)

    sharded = dict(w_in=w_in, w_glu=w_glu, w_br_att=w_br_att, w_br_ssm=w_br_ssm, w_br_pool=w_br_pool, w_out=w_out)
    shards = lambda li, keys: [sharded[k][li].astype(BF16) for k in keys]
    disc, disc_vjp = jax.vjp(jax.vmap(_ssm_discretize), ssm_a_re, ssm_a_im, ssm_log_dt, ssm_b_re, ssm_b_im)
    layer = lambda li, gathered_w_in: _prepare_layer(
        li, mod_mine, norm_g, _full_weights(("w_in",), gathered_w_in)["w_in"], attn_sinks, disc, ssm_c_re, ssm_c_im,
        ssm_d, b_glu, w_pool, pool_scale)
    late_weights = lambda gathered: _full_weights(LATE_WEIGHTS, gathered)
    core = lax.axis_index("c").astype(jnp.int32).reshape(1)

    def chip_sums_of(keys, grads_li, tag):
        by_dest = _by_destination(keys, grads_li)
        from_sibling = _sibling_swap(by_dest, name=f"grads_sibling_swap_{tag}")
        return [_pair_add(a.reshape(2, -1, a.shape[-1]), b.reshape(-1, b.shape[-1]), core,
                          name=f"grads_pair_add_{tag}_{k}").reshape(b.shape)
                for k, (a, b) in zip(keys, zip(by_dest, from_sibling))]

    layers, saved, grads = [None] * DEPTH, [None] * DEPTH, [None] * DEPTH
    layers[0] = layer(0, _gather_two_level(shards(0, ("w_in",)), name="gather_w_in_l0"))
    xs, saved[0], layers[0], late1, w_in1 = _layer_fwd(
        x[0], layers[0], 0, late=(_gather_plan(shards(0, LATE_WEIGHTS)), late_weights),
        comm_attn=_gather_plan(shards(1, LATE_WEIGHTS)), comm_ssm=_gather_plan(shards(1, ("w_in",))))
    layers[1] = {**layer(1, w_in1), **late_weights(late1)}
    xs, saved[1], _, _, _ = _layer_fwd(xs, layers[1], 1)
    dx, fin_sums = _final_loss(xs, final_g[None, :], loss_target[0])
    loss_part = jnp.sum(fin_sums[1])
    dx, grads[1], _, _ = _layer_bwd(dx, layers[1], saved[1], 1)
    dx, grads[0], scattered1, scattered0 = _layer_bwd(
        dx, layers[0], saved[0], 0, comm=_chip_scatter_plan(chip_sums_of(BIG_WEIGHTS, grads[1], "l1")),
        own=lambda g: _chip_scatter_plan(chip_sums_of(tuple(g), g, "l0_" + "_".join(g))))
    big = [jnp.stack([a, b], axis=1) for a, b in zip(scattered0, scattered1)]
    grad_x = dx[None]

    small = _pack_small(loss_part, fin_sums[0], grads)
    dmod_rows = jnp.concatenate([grads[li]["dmod"] for li in range(DEPTH)]).reshape(-1, 128)
    small_sum, dmod_gathered = _allreduce_small(small, dmod_rows, name="allreduce_small")
    out = {}

    def adam(name, w, g_parts, m, v):
        shp = w.shape
        r = int(math.prod(shp[:-1])) if len(shp) > 1 else 1
        w2, m2, v2 = (a.reshape(r, shp[-1]) for a in (w, m, v))
        g2 = g_parts.reshape(g_parts.shape[0], r, shp[-1])
        res = _adamw(w2, g2, m2, v2, name=f"adamw_{name}")
        out[name] = tuple(a.reshape(shp) for a in res)

    flat = small_sum.reshape(-1)
    shapes = [(128,), (d,)]
    for _ in range(DEPTH):
        shapes += [(d,), (N_HEADS,), (SSM_W,), (SSM_W,), (4, POOL_GW, POOL_GW), (POOL_W,), (3 * d,),
                   (SSM_GROUPS, SSM_STATE), (SSM_GROUPS, SSM_STATE), (SSM_GROUPS, SSM_STATE, SSM_GROUP),
                   (SSM_GROUPS, SSM_STATE, SSM_GROUP), (SSM_GROUPS, SSM_GROUP, SSM_STATE), (SSM_GROUPS, SSM_GROUP, SSM_STATE)]
    un = _unpack_small(flat, shapes)
    loss = un[0][0]
    g_final_g = un[1]
    per = 13
    gl = [un[2 + li * per: 2 + (li + 1) * per] for li in range(DEPTH)]
    st = lambda j: jnp.stack([gl[li][j] for li in range(DEPTH)])
    g_norm_g, g_sinks, g_ssm_d, g_b_glu, g_w_pool, g_pool_scale, g_b_ada = (st(j) for j in range(7))
    d_lr, d_li, d_bbr, d_bbi, g_c_re, g_c_im = (st(j) for j in range(7, 13))
    g_a_re, g_a_im, g_log_dt, g_b_re, g_b_im = disc_vjp((d_lr, d_li, d_bbr, d_bbi))

    dmod_all = lax.dynamic_slice(dmod_gathered.reshape(N_DEV, DEPTH, 3 * d), (0, 0, me * ada_w), (N_DEV, DEPTH, ada_w))
    dmod_all = dmod_all.transpose(1, 0, 2)
    g_w_ada = jnp.stack([_mm_tn(c_act, dmod_all[li], tm=d, tn=ada_w, tk=N_DEV, name=f"dw_ada_l{li}") for li in range(DEPTH)])

    adam("w_ada", w_ada, g_w_ada[None], m_w_ada, v_w_ada)
    adam("w_in", w_in, big[0], m_w_in, v_w_in)
    adam("w_glu", w_glu, big[1], m_w_glu, v_w_glu)
    adam("w_br_att", w_br_att, big[2], m_w_br_att, v_w_br_att)
    adam("w_br_ssm", w_br_ssm, big[3], m_w_br_ssm, v_w_br_ssm)
    adam("w_br_pool", w_br_pool, big[4], m_w_br_pool, v_w_br_pool)
    adam("w_out", w_out, big[5], m_w_out, v_w_out)

    small_names = ["norm_g", "b_ada", "attn_sinks", "ssm_a_re", "ssm_a_im", "ssm_log_dt", "ssm_b_re", "ssm_b_im",
                   "ssm_c_re", "ssm_c_im", "ssm_d", "b_glu", "w_pool", "pool_scale", "final_g"]
    small_w = [norm_g, b_ada, attn_sinks, ssm_a_re, ssm_a_im, ssm_log_dt, ssm_b_re, ssm_b_im, ssm_c_re, ssm_c_im,
               ssm_d, b_glu, w_pool, pool_scale, final_g]
    small_m = [m_norm_g, m_b_ada, m_attn_sinks, m_ssm_a_re, m_ssm_a_im, m_ssm_log_dt, m_ssm_b_re, m_ssm_b_im,
               m_ssm_c_re, m_ssm_c_im, m_ssm_d, m_b_glu, m_w_pool, m_pool_scale, m_final_g]
    small_v = [v_norm_g, v_b_ada, v_attn_sinks, v_ssm_a_re, v_ssm_a_im, v_ssm_log_dt, v_ssm_b_re, v_ssm_b_im,
               v_ssm_c_re, v_ssm_c_im, v_ssm_d, v_b_glu, v_w_pool, v_pool_scale, v_final_g]
    small_g = [g_norm_g, g_b_ada, g_sinks, g_a_re, g_a_im, g_log_dt, g_b_re, g_b_im, g_c_re, g_c_im,
               g_ssm_d, g_b_glu, g_w_pool, g_pool_scale, g_final_g]

    for nm, w, g, m, v in zip(small_names, small_w, small_g, small_m, small_v):
        adam(nm, w, g[None], m, v)

    order = ["norm_g", "w_ada", "b_ada", "w_in", "attn_sinks", "ssm_a_re", "ssm_a_im", "ssm_log_dt", "ssm_b_re",
             "ssm_b_im", "ssm_c_re", "ssm_c_im", "ssm_d", "w_glu", "b_glu", "w_pool", "pool_scale", "w_br_att",
             "w_br_ssm", "w_br_pool", "w_out", "final_g"]
    return (loss, grad_x, *[out[k][0] for k in order], *[out[k][1] for k in order],
            *[out[k][2] for k in order], *[out[k][3] for k in order])
```

```python
import functools
import math

import jax
import jax.numpy as jnp
from jax import lax
from jax.experimental import pallas as pl
from jax.experimental.pallas import tpu as pltpu

F32 = jnp.float32
BF16 = jnp.bfloat16

N_DEV = 8
D_MODEL = 1024
DEPTH = 2
CHUNK = 64
N_HEADS = 8
N_KV_HEADS = 2
HEAD_DIM = 64
Q_PER_KV = N_HEADS // N_KV_HEADS
WINDOW = 128
ATT_W = 512
KV_W = 128
SSM_W = 512
SSM_GROUP = 16
SSM_GROUPS = 32
SSM_STATE = 64
SSM_N = SSM_GROUPS * SSM_STATE
POOL_W = 512
POOL_WINDOWS = (2, 4, 8, 16)
POOL_GW = 128
POOL_HALO = 16
EPS = 1e-6
NEG_INF = -1e30
ADAM_LR = 0.001
ADAM_B1 = 0.9
ADAM_B2 = 0.999
ADAM_EPS = 1e-08
ADAM_WD = 0.01
ADAM_STEP = 10

SEQ_BLOCK = 256
ATT_BLOCK = 128
VMEM_LIMIT = 56 * 1024 * 1024

NN = (((1,), (0,)), ((), ()))
NT = (((1,), (1,)), ((), ()))
TN = (((0,), (0,)), ((), ()))


def _dot(a, b, dims=NN):
    return lax.dot_general(a.astype(BF16), b.astype(BF16), dims, preferred_element_type=F32)


def _params(*sem):
    return pltpu.CompilerParams(dimension_semantics=sem, vmem_limit_bytes=VMEM_LIMIT)


def _sigmoid(x):
    return 0.5 + 0.5 * jnp.tanh(0.5 * x)


def _silu_and_grad(z):
    s = _sigmoid(z)
    return z * s, s * (1.0 + z * (1.0 - s))


_GELU_K = math.sqrt(2.0 / math.pi)


def _gelu_and_grad(x):
    inner = _GELU_K * (x + 0.044715 * x * x * x)
    t = jnp.tanh(inner)
    val = 0.5 * x * (1.0 + t)
    grad = 0.5 * (1.0 + t) + 0.5 * x * (1.0 - t * t) * _GELU_K * (1.0 + 3.0 * 0.044715 * x * x)
    return val, grad


def _mm(a, b, *, nt=False, out_dtype=F32, tm=1024, tn=1024, name, comm=None):
    m, k = a.shape
    n = b.shape[0] if nt else b.shape[1]
    tm, tn = min(tm, m), min(tn, n)
    assert m % tm == 0 and n % tn == 0
    dims = NT if nt else NN
    grid = (m // tm, n // tn)
    c_args, c_in, c_out, c_shape, c_scratch = _comm_extra(comm)

    def body(a_ref, b_ref, o_ref):
        o_ref[...] = _dot(a_ref[...], b_ref[...], dims).astype(out_dtype)

    b_spec = pl.BlockSpec((tn, k), lambda i, j: (j, 0)) if nt else pl.BlockSpec((k, tn), lambda i, j: (0, j))
    res = pl.pallas_call(
        _with_comm(body, comm, 2, 1, grid, -1), grid=grid,
        in_specs=[pl.BlockSpec((tm, k), lambda i, j: (i, 0)), b_spec] + c_in,
        out_specs=[pl.BlockSpec((tm, tn), lambda i, j: (i, j))] + c_out,
        out_shape=[jax.ShapeDtypeStruct((m, n), out_dtype)] + c_shape,
        scratch_shapes=c_scratch,
        compiler_params=_params(*(("arbitrary",) * 2 if comm else ("parallel",) * 2)), name=name)(a, b, *c_args)
    return (res[0], list(res[1:])) if comm else res[0]


def _mm_nt_sum(pairs, *, out_dtype=F32, tm=512, tn=512, name, comm=None):
    m = pairs[0][0].shape[0]
    n = pairs[0][1].shape[0]
    np_ = len(pairs)
    grid = (m // tm, n // tn)
    c_args, c_in, c_out, c_shape, c_scratch = _comm_extra(comm)

    def body(*refs):
        o_ref = refs[-1]
        acc = _dot(refs[0][...], refs[1][...], NT)
        for p in range(1, np_):
            acc = acc + _dot(refs[2 * p][...], refs[2 * p + 1][...], NT)
        o_ref[...] = acc.astype(out_dtype)

    in_specs, args = [], []
    for a, b in pairs:
        in_specs.append(pl.BlockSpec((tm, a.shape[1]), lambda i, j: (i, 0)))
        in_specs.append(pl.BlockSpec((tn, b.shape[1]), lambda i, j: (j, 0)))
        args += [a, b]
    res = pl.pallas_call(
        _with_comm(body, comm, 2 * np_, 1, grid, -1), grid=grid, in_specs=in_specs + c_in,
        out_specs=[pl.BlockSpec((tm, tn), lambda i, j: (i, j))] + c_out,
        out_shape=[jax.ShapeDtypeStruct((m, n), out_dtype)] + c_shape,
        scratch_shapes=c_scratch,
        compiler_params=_params(*(("arbitrary",) * 2 if comm else ("parallel",) * 2)), name=name)(*args, *c_args)
    return (res[0], list(res[1:])) if comm else res[0]


def _mm_tn(a, b, *, out_dtype=F32, tm=1024, tn=1024, tk=1024, name, comm=None):
    k, m = a.shape
    n = b.shape[1]
    assert m % min(tm, m) == 0 and n % min(tn, n) == 0 and k % min(tk, k) == 0
    tm, tn, tk = min(tm, m), min(tn, n), min(tk, k)
    nk = k // tk
    grid = (m // tm, n // tn, nk)
    c_args, c_in, c_out, c_shape, c_scratch = _comm_extra(comm)

    def body(a_ref, b_ref, o_ref, acc_ref):
        kk = pl.program_id(2)

        @pl.when(kk == 0)
        def _():
            acc_ref[...] = jnp.zeros_like(acc_ref)

        acc_ref[...] += _dot(a_ref[...], b_ref[...], TN)

        @pl.when(kk == nk - 1)
        def _():
            o_ref[...] = acc_ref[...].astype(out_dtype)

    res = pl.pallas_call(
        _with_comm(body, comm, 2, 1, grid, -1), grid=grid,
        in_specs=[pl.BlockSpec((tk, tm), lambda i, j, kk: (kk, i)), pl.BlockSpec((tk, tn), lambda i, j, kk: (kk, j))] + c_in,
        out_specs=[pl.BlockSpec((tm, tn), lambda i, j, kk: (i, j))] + c_out,
        out_shape=[jax.ShapeDtypeStruct((m, n), out_dtype)] + c_shape,
        scratch_shapes=[pltpu.VMEM((tm, tn), F32)] + c_scratch,
        compiler_params=_params(*(("arbitrary",) * 3 if comm else ("parallel", "parallel", "arbitrary"))),
        name=name)(a, b, *c_args)
    return (res[0], list(res[1:])) if comm else res[0]


def _ln_fwd(x, g, shift, scale, *, name, tm=512):
    l, d = x.shape

    def body(x_ref, g_ref, sh_ref, sc_ref, h_ref):
        xv = x_ref[...]
        n = xv * lax.rsqrt(jnp.mean(xv * xv, axis=-1, keepdims=True) + EPS)
        h_ref[...] = ((n * g_ref[...]) * (1.0 + sc_ref[...]) + sh_ref[...]).astype(BF16)

    vec = pl.BlockSpec((1, d), lambda i: (0, 0))
    return pl.pallas_call(
        body, grid=(l // tm,),
        in_specs=[pl.BlockSpec((tm, d), lambda i: (i, 0)), vec, vec, vec],
        out_specs=pl.BlockSpec((tm, d), lambda i: (i, 0)),
        out_shape=jax.ShapeDtypeStruct((l, d), BF16),
        compiler_params=_params("parallel"), name=name)(x, g, shift, scale)


def _ln_bwd(x, dh, dres, g, scale, *, name, tm=512):
    l, d = x.shape

    def body(x_ref, dh_ref, dres_ref, g_ref, sc_ref, dx_ref, sums_ref):
        xv = x_ref[...]
        dhv = dh_ref[...]
        rstd = lax.rsqrt(jnp.mean(xv * xv, axis=-1, keepdims=True) + EPS)
        n = xv * rstd
        gv = g_ref[...]
        dr = dhv * (1.0 + sc_ref[...])
        dn = dr * gv
        dx_ref[...] = dres_ref[...] + rstd * (dn - n * jnp.mean(dn * n, axis=-1, keepdims=True))

        @pl.when(pl.program_id(0) == 0)
        def _():
            sums_ref[...] = jnp.zeros_like(sums_ref)

        sums_ref[0:1, :] += jnp.sum(dhv, axis=0, keepdims=True)
        sums_ref[1:2, :] += jnp.sum(dhv * (n * gv), axis=0, keepdims=True)
        sums_ref[2:3, :] += jnp.sum(dr * n, axis=0, keepdims=True)

    vec = pl.BlockSpec((1, d), lambda i: (0, 0))
    row = pl.BlockSpec((tm, d), lambda i: (i, 0))
    return pl.pallas_call(
        body, grid=(l // tm,),
        in_specs=[row, row, row, vec, vec],
        out_specs=[row, pl.BlockSpec((8, d), lambda i: (0, 0))],
        out_shape=[jax.ShapeDtypeStruct((l, d), F32), jax.ShapeDtypeStruct((8, d), F32)],
        compiler_params=_params("arbitrary"), name=name)(x, dh, dres, g, scale)


def _final_loss(x, g, target, *, tm=512):
    l, d = x.shape

    def body(x_ref, g_ref, t_ref, dx_ref, sums_ref):
        xv = x_ref[...]
        rstd = lax.rsqrt(jnp.mean(xv * xv, axis=-1, keepdims=True) + EPS)
        n = xv * rstd
        gv = g_ref[...]
        err = n * gv - t_ref[...]
        dy = err * (1.0 / d)
        dn = dy * gv
        dx_ref[...] = rstd * (dn - n * jnp.mean(dn * n, axis=-1, keepdims=True))

        @pl.when(pl.program_id(0) == 0)
        def _():
            sums_ref[...] = jnp.zeros_like(sums_ref)

        sums_ref[0:1, :] += jnp.sum(dy * n, axis=0, keepdims=True)
        sums_ref[1:2, :] += jnp.sum(err * err, axis=0, keepdims=True) * (0.5 / d)

    vec = pl.BlockSpec((1, d), lambda i: (0, 0))
    row = pl.BlockSpec((tm, d), lambda i: (i, 0))
    dx, sums = pl.pallas_call(
        body, grid=(l // tm,),
        in_specs=[row, vec, row],
        out_specs=[row, pl.BlockSpec((8, d), lambda i: (0, 0))],
        out_shape=[jax.ShapeDtypeStruct((l, d), F32), jax.ShapeDtypeStruct((8, d), F32)],
        compiler_params=_params("arbitrary"), name="final_loss")(x, g, target)
    return dx, sums


def _attn_geometry(i, t):
    nk = t + WINDOW
    qi = lax.broadcasted_iota(jnp.int32, (t, nk), 0)
    kj = lax.broadcasted_iota(jnp.int32, (t, nk), 1)
    dist = jnp.abs(qi + WINDOW - kj).astype(F32)
    qc = jnp.right_shift(qi, 6)
    kc = jnp.right_shift(kj, 6)
    valid = (kc >= qc) & (kc <= qc + WINDOW // CHUNK) & ((i > 0) | (kj >= WINDOW))
    return dist, valid


def _attn_head(q, k_all, v_all, sink, slope, dist, valid):
    s = _dot(q, k_all, NT) * (1.0 / math.sqrt(HEAD_DIM)) - slope * dist
    s = jnp.where(valid, s, NEG_INF)
    m = jnp.maximum(jnp.max(s, axis=-1, keepdims=True), sink)
    e = jnp.exp(s - m)
    es = jnp.exp(sink - m)
    inv = 1.0 / (jnp.sum(e, axis=-1, keepdims=True) + es)
    p = e * inv
    o = _dot(p, v_all, NN)
    return p, o, es * inv


def _attn_specs(t):
    cur = pl.BlockSpec((t, ATT_W * 2 + KV_W * 2), lambda i: (i, 0))
    halo_blocks = t // WINDOW
    prev = pl.BlockSpec((WINDOW, 2 * KV_W), lambda i: (jnp.maximum(i * halo_blocks - 1, 0), (2 * ATT_W) // (2 * KV_W)))
    return cur, prev


def _attn_fwd(pa, sinks, *, name, t=ATT_BLOCK, comm=None):
    l = pa.shape[0]
    t = min(t, l)
    nb = l // t
    c_args, c_in, c_out, c_shape, c_scratch = _comm_extra(comm)

    def body(sink_ref, cur_ref, prev_ref, ya_ref):
        i = pl.program_id(0)
        dist, valid = _attn_geometry(i, t)
        for h in range(N_HEADS):
            kh = h // Q_PER_KV
            q = cur_ref[:, h * HEAD_DIM:(h + 1) * HEAD_DIM]
            z = cur_ref[:, ATT_W + h * HEAD_DIM:ATT_W + (h + 1) * HEAD_DIM].astype(F32)
            k_all = jnp.concatenate([prev_ref[:, kh * HEAD_DIM:(kh + 1) * HEAD_DIM],
                                     cur_ref[:, 2 * ATT_W + kh * HEAD_DIM:2 * ATT_W + (kh + 1) * HEAD_DIM]], axis=0)
            v_all = jnp.concatenate([prev_ref[:, KV_W + kh * HEAD_DIM:KV_W + (kh + 1) * HEAD_DIM],
                                     cur_ref[:, 2 * ATT_W + KV_W + kh * HEAD_DIM:2 * ATT_W + KV_W + (kh + 1) * HEAD_DIM]], axis=0)
            _, o, _ = _attn_head(q, k_all, v_all, sink_ref[h], 2.0 ** (-(h + 1)), dist, valid)
            sz, _ = _silu_and_grad(z)
            ya_ref[:, h * HEAD_DIM:(h + 1) * HEAD_DIM] = (o * sz).astype(BF16)

    cur, prev = _attn_specs(t)
    res = pl.pallas_call(
        _with_comm(body, comm, 3, 1, nb, nb - 1), grid=(nb,),
        in_specs=[pl.BlockSpec(memory_space=pltpu.SMEM), cur, prev] + c_in,
        out_specs=[pl.BlockSpec((t, ATT_W), lambda i: (i, 0))] + c_out,
        out_shape=[jax.ShapeDtypeStruct((l, ATT_W), BF16)] + c_shape,
        scratch_shapes=c_scratch,
        compiler_params=_params("arbitrary"), name=name)(sinks, pa, pa, *c_args)
    return res[0], res[1:]


def _attn_bwd(pa, sinks, dya, *, name, t=SEQ_BLOCK):
    l = pa.shape[0]
    t = min(t, l)
    nb = l // t
    scale = 1.0 / math.sqrt(HEAD_DIM)

    def body(sink_ref, cur_ref, prev_ref, dya_ref, dpa_ref, dsink_ref, carry_ref):
        n = pl.program_id(0)
        i = nb - 1 - n
        dist, valid = _attn_geometry(i, t)

        @pl.when(n == 0)
        def _():
            carry_ref[...] = jnp.zeros_like(carry_ref)
            dsink_ref[...] = jnp.zeros_like(dsink_ref)

        dk_acc = [jnp.zeros((HEAD_DIM, t + WINDOW), F32) for _ in range(N_KV_HEADS)]
        dv_acc = [jnp.zeros((HEAD_DIM, t + WINDOW), F32) for _ in range(N_KV_HEADS)]
        for h in range(N_HEADS):
            kh = h // Q_PER_KV
            q = cur_ref[:, h * HEAD_DIM:(h + 1) * HEAD_DIM]
            z = cur_ref[:, ATT_W + h * HEAD_DIM:ATT_W + (h + 1) * HEAD_DIM].astype(F32)
            k_all = jnp.concatenate([prev_ref[:, kh * HEAD_DIM:(kh + 1) * HEAD_DIM],
                                     cur_ref[:, 2 * ATT_W + kh * HEAD_DIM:2 * ATT_W + (kh + 1) * HEAD_DIM]], axis=0)
            v_all = jnp.concatenate([prev_ref[:, KV_W + kh * HEAD_DIM:KV_W + (kh + 1) * HEAD_DIM],
                                     cur_ref[:, 2 * ATT_W + KV_W + kh * HEAD_DIM:2 * ATT_W + KV_W + (kh + 1) * HEAD_DIM]], axis=0)
            p, o, p_sink = _attn_head(q, k_all, v_all, sink_ref[h], 2.0 ** (-(h + 1)), dist, valid)
            dy = dya_ref[:, h * HEAD_DIM:(h + 1) * HEAD_DIM]
            sz, dsz = _silu_and_grad(z)
            do = dy * sz
            dpa_ref[:, ATT_W + h * HEAD_DIM:ATT_W + (h + 1) * HEAD_DIM] = (dy * o * dsz).astype(BF16)
            delta = jnp.sum(do * o, axis=-1, keepdims=True)
            dp = _dot(do, v_all, NT)
            ds = p * (dp - delta)
            dpa_ref[:, h * HEAD_DIM:(h + 1) * HEAD_DIM] = (_dot(ds, k_all, NN) * scale).astype(BF16)
            dk_acc[kh] = dk_acc[kh] + _dot(q, ds, TN) * scale
            dv_acc[kh] = dv_acc[kh] + _dot(do, p, TN)
            dsink_ref[h:h + 1, :] += jnp.broadcast_to(-jnp.sum(p_sink * delta, axis=0, keepdims=True), (1, 128))

        acc = jnp.concatenate(dk_acc + dv_acc, axis=0).T
        own = acc[WINDOW:, :]
        tail = own[t - WINDOW:, :] + carry_ref[...]
        if t > WINDOW:
            dpa_ref[0:t - WINDOW, 2 * ATT_W:] = own[:t - WINDOW, :].astype(BF16)
        dpa_ref[t - WINDOW:t, 2 * ATT_W:] = tail.astype(BF16)
        carry_ref[...] = acc[:WINDOW, :]

    halo_blocks = t // WINDOW
    wpa = 2 * ATT_W + 2 * KV_W
    cur = pl.BlockSpec((t, wpa), lambda n: (nb - 1 - n, 0))
    prev = pl.BlockSpec((WINDOW, 2 * KV_W),
                        lambda n: (jnp.maximum((nb - 1 - n) * halo_blocks - 1, 0), (2 * ATT_W) // (2 * KV_W)))
    return pl.pallas_call(
        body, grid=(nb,),
        in_specs=[pl.BlockSpec(memory_space=pltpu.SMEM), cur, prev, pl.BlockSpec((t, ATT_W), lambda n: (nb - 1 - n, 0))],
        out_specs=[pl.BlockSpec((t, wpa), lambda n: (nb - 1 - n, 0)), pl.BlockSpec((8, 128), lambda n: (0, 0))],
        out_shape=[jax.ShapeDtypeStruct((l, wpa), BF16), jax.ShapeDtypeStruct((8, 128), F32)],
        scratch_shapes=[pltpu.VMEM((WINDOW, 2 * KV_W), F32)],
        compiler_params=_params("arbitrary"), name=name)(sinks, pa, pa, dya)


def _scan(xr, xi, lr, li, t, reverse):
    row = lax.broadcasted_iota(jnp.int32, (t, 1), 0)
    d = 1
    pr, pi = lr, li
    while d < t:
        if reverse:
            sr = jnp.where(row < t - d, pltpu.roll(xr, t - d, 0), 0.0)
            si = jnp.where(row < t - d, pltpu.roll(xi, t - d, 0), 0.0)
        else:
            sr = jnp.where(row >= d, pltpu.roll(xr, d, 0), 0.0)
            si = jnp.where(row >= d, pltpu.roll(xi, d, 0), 0.0)
        xr, xi = xr + pr * sr - pi * si, xi + pr * si + pi * sr
        pr, pi = pr * pr - pi * pi, 2.0 * pr * pi
        d *= 2
    return xr, xi


SCAN_SUB = 8


def _split_hi_lo(a):
    hi = a.astype(BF16)
    lo = (a - hi.astype(F32)).astype(BF16)
    return jnp.concatenate([hi, lo], axis=0)


def _scan_mxu(xr, xi, tab, lam3, lam8, tri, expand, cr, ci, t, reverse):
    ns = t // SCAN_SUB
    n = xr.shape[1]
    v3 = lambda a: a.reshape(ns, SCAN_SUB, n)
    x3r, x3i = v3(xr), v3(xi)
    br = (x3r * tab[0] - x3i * tab[1]).reshape(t, n)
    bi = (x3r * tab[1] + x3i * tab[0]).reshape(t, n)
    pm = jnp.dot(tri, jnp.concatenate([br, bi], axis=1).astype(BF16), preferred_element_type=F32)
    p3r, p3i = v3(pm[:t, :n]), v3(pm[:t, n:])
    slr = p3r * tab[2] - p3i * tab[3]
    sli = p3r * tab[3] + p3i * tab[2]
    totr, toti = pm[t:, :n], pm[t:, n:]
    l3r, l3i = lam3
    l8r, l8i = lam8
    row = lax.broadcasted_iota(jnp.int32, (ns, 1), 0)
    edge = row == (ns - 1 if reverse else 0)
    er = totr * l3r - toti * l3i + jnp.where(edge, l8r * cr - l8i * ci, 0.0)
    ei = totr * l3i + toti * l3r + jnp.where(edge, l8r * ci + l8i * cr, 0.0)
    er, ei = _scan(er, ei, l8r, l8i, ns, reverse)
    shift = ns - 1 if reverse else 1
    nbr = jnp.where(edge, cr, pltpu.roll(er, shift, 0))
    nbi = jnp.where(edge, ci, pltpu.roll(ei, shift, 0))
    ex = jnp.dot(expand, _split_hi_lo(jnp.concatenate([nbr, nbi], axis=1)), preferred_element_type=F32)
    e3r, e3i = v3(ex[:, :n]), v3(ex[:, n:])
    sr = (slr + e3r * tab[4] - e3i * tab[5]).reshape(t, n)
    si = (sli + e3r * tab[5] + e3i * tab[4]).reshape(t, n)
    out = 0 if reverse else ns - 1
    return sr, si, er[out:out + 1, :], ei[out:out + 1, :]


def _scan_consts(t):
    import numpy as np
    ns = t // SCAN_SUB
    r = np.arange(t)
    same = (r[:, None] // SCAN_SUB) == (r[None, :] // SCAN_SUB)
    sums = (np.arange(ns)[:, None] == (r[None, :] // SCAN_SUB))
    tri = []
    for keep in (r[None, :] <= r[:, None], r[None, :] >= r[:, None]):
        tri.append(np.concatenate([same & keep, sums], axis=0).astype(np.float32))
    ex = ((r[:, None] // SCAN_SUB) == np.arange(ns)[None, :]).astype(np.float32)
    return jnp.asarray(np.stack(tri), BF16), jnp.asarray(np.concatenate([ex, ex], axis=1), BF16)


def _scan_tables(lr, li):
    den = lr * lr + li * li
    ir, ii = lr / den, -li / den
    mul = lambda a, b: (a[0] * b[0] - a[1] * b[1], a[0] * b[1] + a[1] * b[0])
    pw = {0: (jnp.ones_like(lr), jnp.zeros_like(lr))}
    for e in range(1, 9):
        pw[e] = mul(pw[e - 1], (lr, li))
    for e in range(-1, -5, -1):
        pw[e] = mul(pw[e + 1], (ir, ii))
    stack = lambda es, sign: (jnp.stack([pw[e][0] for e in es]), sign * jnp.stack([pw[e][1] for e in es]))
    j = range(SCAN_SUB)
    parts = [stack([4 - k for k in j], 1.0), stack([k - 4 for k in j], 1.0), stack([k + 1 for k in j], 1.0),
             stack([k - 3 for k in j], -1.0), stack([3 - k for k in j], -1.0), stack([8 - k for k in j], -1.0)]
    tabs = jnp.stack([a for pair in parts for a in pair])
    lam = jnp.zeros((8, lr.shape[0]), F32)
    for k, v in enumerate((lr, li, pw[3][0], pw[3][1], pw[8][0], pw[8][1])):
        lam = lam.at[k].set(v)
    return lam, tabs


SSM_HALVES = 2
SSM_HW = SSM_W // SSM_HALVES
SSM_HN = SSM_N // SSM_HALVES


def _bd_nn(x, w):
    a = w.shape[1]
    return jnp.concatenate([_dot(x[:, h * a:(h + 1) * a], w[h]) for h in range(SSM_HALVES)], axis=1)


def _bd_nt(x, w):
    b = w.shape[2]
    return jnp.concatenate([_dot(x[:, h * b:(h + 1) * b], w[h], NT) for h in range(SSM_HALVES)], axis=1)


def _bd_tn(x, y):
    a, b = x.shape[1] // SSM_HALVES, y.shape[1] // SSM_HALVES
    return jnp.stack([_dot(x[:, h * a:(h + 1) * a], y[:, h * b:(h + 1) * b], TN) for h in range(SSM_HALVES)])


def _ssm_states(u, s0r, s0i, lam_ref, tab_ref, tri_ref, ex_ref, bre, bim, t):
    tab = tuple(tab_ref[k] for k in range(6))
    return _scan_mxu(_bd_nn(u, bre), _bd_nn(u, bim), tab, (lam_ref[2:3, :], lam_ref[3:4, :]),
                     (lam_ref[4:5, :], lam_ref[5:6, :]), tri_ref[0], ex_ref[...], s0r, s0i, t, False)


def _ssm_head(u, z, xr, xi, cre, cim, dskip, wglu, bglu):
    y = _bd_nn(xr, cre) - _bd_nn(xi, cim) + dskip * u
    y2, dgelu = _gelu_and_grad(y)
    gate = _sigmoid(_dot(y2, wglu) + bglu)
    y3 = y2 * gate
    return y2, dgelu, gate, y3


def _with_comm(body, comm, n_in, n_out, grid, mid_step):
    if comm is None:
        return body
    nc = len(comm["args"])
    n_sem = len(comm["scratch"])
    grid = (grid,) if isinstance(grid, int) else tuple(grid)
    total = math.prod(grid)

    def hosted(*refs):
        ins, cin = refs[:n_in], refs[n_in:n_in + nc]
        outs, cout = refs[n_in + nc:n_in + nc + n_out], refs[n_in + nc + n_out:n_in + 2 * nc + n_out]
        rest = refs[n_in + 2 * nc + n_out:]
        scratch, csem = rest[:len(rest) - n_sem], rest[len(rest) - n_sem:]
        start, forward, finish = comm["phases"](cin, cout, *csem)
        step = pl.program_id(0)
        for axis in range(1, len(grid)):
            step = step * grid[axis] + pl.program_id(axis)
        pl.when(step == 0)(start)
        pl.when(step == (mid_step if mid_step >= 0 else total + mid_step))(forward)
        body(*ins, *outs, *scratch)
        pl.when(step == total - 1)(finish)

    return hosted


def _comm_extra(comm):
    if comm is None:
        return [], [], [], [], []
    anyspec = pl.BlockSpec(memory_space=pl.ANY)
    nc = len(comm["args"])
    return comm["args"], [anyspec] * nc, [anyspec] * nc, comm["out_shape"], comm["scratch"]


def _ssm_fwd(ps, scan_ops, bblk, cblk, dskip, wglu, bglu, *, name, t=SEQ_BLOCK, comm=None):
    l = ps.shape[0]
    assert l % t == 0
    nb = l // t
    ns = t // SCAN_SUB
    c_args, c_in, c_out, c_shape, c_scratch = _comm_extra(comm)

    def body(ps_ref, lam_ref, tab_ref, tri_ref, ex_ref, b_ref, c_ref, d_ref, w_ref, bg_ref, ys_ref, chk_ref, xs_ref,
             st_ref):
        @pl.when(pl.program_id(0) == 0)
        def _():
            st_ref[...] = jnp.zeros_like(st_ref)

        chk_ref[...] = jnp.broadcast_to(st_ref[...], chk_ref.shape)
        u = ps_ref[:, :SSM_W].astype(F32)
        z = ps_ref[:, SSM_W:].astype(F32)
        xr, xi, er, ei = _ssm_states(u, st_ref[:, :SSM_N], st_ref[:, SSM_N:], lam_ref, tab_ref, tri_ref, ex_ref,
                                     b_ref[0], b_ref[1], t)
        st_ref[:, :SSM_N] = er
        st_ref[:, SSM_N:] = ei
        xr, xi = xr.astype(BF16), xi.astype(BF16)
        xs_ref[:, :SSM_N] = xr
        xs_ref[:, SSM_N:] = xi
        _, _, _, y3 = _ssm_head(u, z, xr, xi, c_ref[0], c_ref[1], d_ref[...], w_ref[...], bg_ref[...])
        sz, _ = _silu_and_grad(z)
        ys_ref[...] = (y3 * sz).astype(BF16)

    full = lambda shape: pl.BlockSpec(shape, lambda i: (0,) * len(shape))
    return pl.pallas_call(
        _with_comm(body, comm, 10, 3, nb, nb - 1), grid=(nb,),
        in_specs=[pl.BlockSpec((t, 2 * SSM_W), lambda i: (i, 0)), full((8, SSM_N)), full((12, SCAN_SUB, SSM_N)),
                  full((2, t + ns, t)), full((t, 2 * ns)), full((2, SSM_HALVES, SSM_HW, SSM_HN)),
                  full((2, SSM_HALVES, SSM_HN, SSM_HW)), full((1, SSM_W)), full((SSM_W, SSM_W)), full((1, SSM_W))] + c_in,
        out_specs=[pl.BlockSpec((t, SSM_W), lambda i: (i, 0)), pl.BlockSpec((8, 2 * SSM_N), lambda i: (i, 0)),
                   pl.BlockSpec((t, 2 * SSM_N), lambda i: (i, 0))] + c_out,
        out_shape=[jax.ShapeDtypeStruct((l, SSM_W), BF16), jax.ShapeDtypeStruct((nb * 8, 2 * SSM_N), F32),
                   jax.ShapeDtypeStruct((l, 2 * SSM_N), BF16)] + c_shape,
        scratch_shapes=[pltpu.VMEM((1, 2 * SSM_N), F32)] + c_scratch,
        compiler_params=_params("arbitrary"), name=name)(ps, *scan_ops, bblk, cblk, dskip, wglu, bglu, *c_args)


def _ssm_bwd(ps, dys, chk, states, scan_ops, bblk, cblk, dskip, wglu, bglu, *, name, t=SEQ_BLOCK, comm=None):
    l = ps.shape[0]
    assert l % t == 0
    nb = l // t
    ns = t // SCAN_SUB
    c_args, c_in, c_out, c_shape, c_scratch = _comm_extra(comm)

    def body(ps_ref, dys_ref, chk_ref, xs_ref, lam_ref, tab_ref, tri_ref, ex_ref, b_ref, c_ref, d_ref, w_ref, bg_ref,
             dps_ref, db_ref, dc_ref, dw_acc, sums_acc, gc_ref, db_acc, dc_acc):
        n = pl.program_id(0)

        @pl.when(n == 0)
        def _():
            gc_ref[...] = jnp.zeros_like(gc_ref)
            db_acc[...] = jnp.zeros_like(db_acc)
            dc_acc[...] = jnp.zeros_like(dc_acc)
            dw_acc[...] = jnp.zeros_like(dw_acc)
            sums_acc[...] = jnp.zeros_like(sums_acc)

        row = lax.broadcasted_iota(jnp.int32, (t, 1), 0)
        u = ps_ref[:, :SSM_W].astype(F32)
        z = ps_ref[:, SSM_W:].astype(F32)
        s0r, s0i = chk_ref[0:1, :SSM_N], chk_ref[0:1, SSM_N:]
        xr, xi = xs_ref[:, :SSM_N], xs_ref[:, SSM_N:]
        dskip = d_ref[...]
        y2, dgelu, gate, y3 = _ssm_head(u, z, xr, xi, c_ref[0], c_ref[1], dskip, w_ref[...], bg_ref[...])
        sz, dsz = _silu_and_grad(z)
        dys_v = dys_ref[...]
        dps_ref[:, SSM_W:] = (dys_v * y3 * dsz).astype(BF16)
        dy3 = dys_v * sz
        da = dy3 * y2 * gate * (1.0 - gate)
        dy2 = dy3 * gate + _dot(da, w_ref[...], NT)
        dw_acc[...] += _dot(y2, da, TN)
        dy = dy2 * dgelu
        sums_acc[2:3, :SSM_W] += jnp.sum(dy * u, axis=0, keepdims=True)
        sums_acc[3:4, :SSM_W] += jnp.sum(da, axis=0, keepdims=True)
        dc_acc[0] += _bd_tn(dy, xr)
        dc_acc[1] += -_bd_tn(dy, xi)
        rev_tab = tuple(tab_ref[k] for k in range(6, 12))
        gr, gi, gcr, gci = _scan_mxu(
            _bd_nt(dy, c_ref[0]), -_bd_nt(dy, c_ref[1]), rev_tab, (lam_ref[2:3, :], -lam_ref[3:4, :]),
            (lam_ref[4:5, :], -lam_ref[5:6, :]), tri_ref[1], ex_ref[...], gc_ref[:, :SSM_N], gc_ref[:, SSM_N:], t, True)
        gc_ref[:, :SSM_N] = gcr
        gc_ref[:, SSM_N:] = gci
        db_acc[0] += _bd_tn(u, gr)
        db_acc[1] += _bd_tn(u, gi)
        du = dskip * dy + _bd_nt(gr, b_ref[0]) + _bd_nt(gi, b_ref[1])
        dps_ref[:, :SSM_W] = du.astype(BF16)
        spr = jnp.where(row == 0, s0r, pltpu.roll(xr.astype(F32), 1, 0))
        spi = jnp.where(row == 0, s0i, pltpu.roll(xi.astype(F32), 1, 0))
        sums_acc[0:1, :] += jnp.sum(gr * spr + gi * spi, axis=0, keepdims=True)
        sums_acc[1:2, :] += jnp.sum(gi * spr - gr * spi, axis=0, keepdims=True)

        @pl.when(n == nb - 1)
        def _():
            per_half = SSM_GROUPS // SSM_HALVES
            for k in range(2):
                for g in range(SSM_GROUPS):
                    h, gl = divmod(g, per_half)
                    c0, p0 = gl * SSM_GROUP, gl * SSM_STATE
                    db_ref[k, g * SSM_GROUP:(g + 1) * SSM_GROUP, :] = db_acc[k, h, c0:c0 + SSM_GROUP, p0:p0 + SSM_STATE]
                    dc_ref[k, g * SSM_GROUP:(g + 1) * SSM_GROUP, :] = dc_acc[k, h, c0:c0 + SSM_GROUP, p0:p0 + SSM_STATE]

    full = lambda shape: pl.BlockSpec(shape, lambda n: (0,) * len(shape))
    return pl.pallas_call(
        _with_comm(body, comm, 13, 5, nb, 0), grid=(nb,),
        in_specs=[pl.BlockSpec((t, 2 * SSM_W), lambda n: (nb - 1 - n, 0)),
                  pl.BlockSpec((t, SSM_W), lambda n: (nb - 1 - n, 0)),
                  pl.BlockSpec((8, 2 * SSM_N), lambda n: (nb - 1 - n, 0)),
                  pl.BlockSpec((t, 2 * SSM_N), lambda n: (nb - 1 - n, 0)),
                  full((8, SSM_N)), full((12, SCAN_SUB, SSM_N)), full((2, t + ns, t)), full((t, 2 * ns)),
                  full((2, SSM_HALVES, SSM_HW, SSM_HN)), full((2, SSM_HALVES, SSM_HN, SSM_HW)), full((1, SSM_W)),
                  full((SSM_W, SSM_W)), full((1, SSM_W))] + c_in,
        out_specs=[pl.BlockSpec((t, 2 * SSM_W), lambda n: (nb - 1 - n, 0)), full((2, SSM_W, SSM_STATE)),
                   full((2, SSM_W, SSM_STATE)), full((SSM_W, SSM_W)), full((8, SSM_N))] + c_out,
        out_shape=[jax.ShapeDtypeStruct((l, 2 * SSM_W), BF16),
                   jax.ShapeDtypeStruct((2, SSM_W, SSM_STATE), F32),
                   jax.ShapeDtypeStruct((2, SSM_W, SSM_STATE), F32),
                   jax.ShapeDtypeStruct((SSM_W, SSM_W), F32),
                   jax.ShapeDtypeStruct((8, SSM_N), F32)] + c_shape,
        scratch_shapes=[pltpu.VMEM((1, 2 * SSM_N), F32), pltpu.VMEM((2, SSM_HALVES, SSM_HW, SSM_HN), F32),
                        pltpu.VMEM((2, SSM_HALVES, SSM_HW, SSM_HN), F32)] + c_scratch,
        compiler_params=_params("arbitrary"), name=name)(ps, dys, chk, states, *scan_ops, bblk, cblk, dskip, wglu, bglu,
                                                         *c_args)


def _pool_count(i, t):
    pos = lax.broadcasted_iota(jnp.int32, (t, POOL_W), 0) + i * t + 1
    col = lax.broadcasted_iota(jnp.int32, (t, POOL_W), 1)
    win = jnp.where(col < POOL_GW, 2, jnp.where(col < 2 * POOL_GW, 4, jnp.where(col < 3 * POOL_GW, 8, 16)))
    return 1.0 / jnp.minimum(pos, win).astype(F32), col


def _window_sums(ext, n_rows, forward):
    col = lax.broadcasted_iota(jnp.int32, ext.shape, 1)
    sh = (lambda a, d: pltpu.roll(a, d, 0)) if forward else (lambda a, d: pltpu.roll(a, n_rows - d, 0))
    a2 = ext + sh(ext, 1)
    a4 = a2 + sh(a2, 2)
    a8 = a4 + sh(a4, 4)
    a16 = a8 + sh(a8, 8)
    return jnp.where(col < POOL_GW, a2, jnp.where(col < 2 * POOL_GW, a4, jnp.where(col < 3 * POOL_GW, a8, a16)))


def _pool_mix(pooled, wp_ref):
    return jnp.concatenate([_dot(pooled[:, g * POOL_GW:(g + 1) * POOL_GW], wp_ref[g]) for g in range(4)], axis=1)


def _pool_pooled(i, cur_u, prev_u, t):
    prev = jnp.where(i > 0, prev_u, 0.0)
    ext = jnp.concatenate([prev, cur_u], axis=0)
    inv_cnt, _ = _pool_count(i, t)
    return _window_sums(ext, t + POOL_HALO, True)[POOL_HALO:, :] * inv_cnt - cur_u


def _pool_fwd(pp, wpool, pscale, *, name, t=SEQ_BLOCK):
    l = pp.shape[0]
    t = min(t, l)

    def body(cur_ref, prev_ref, wp_ref, sc_ref, yp_ref):
        i = pl.program_id(0)
        pooled = _pool_pooled(i, cur_ref[:, :POOL_W].astype(F32), prev_ref[...].astype(F32), t)
        lin = _pool_mix(pooled, wp_ref)
        sz, _ = _silu_and_grad(cur_ref[:, POOL_W:].astype(F32))
        yp_ref[...] = (lin * sc_ref[...] * sz).astype(BF16)

    hb = t // POOL_HALO
    return pl.pallas_call(
        body, grid=(l // t,),
        in_specs=[pl.BlockSpec((t, 2 * POOL_W), lambda i: (i, 0)),
                  pl.BlockSpec((POOL_HALO, POOL_W), lambda i: (jnp.maximum(i * hb - 1, 0), 0)),
                  pl.BlockSpec((4, POOL_GW, POOL_GW), lambda i: (0, 0, 0)),
                  pl.BlockSpec((1, POOL_W), lambda i: (0, 0))],
        out_specs=pl.BlockSpec((t, POOL_W), lambda i: (i, 0)),
        out_shape=jax.ShapeDtypeStruct((l, POOL_W), BF16),
        compiler_params=_params("parallel"), name=name)(pp, pp, wpool, pscale)


def _pool_bwd(pp, dyp, wpool, pscale, *, name, t=SEQ_BLOCK):
    l = pp.shape[0]
    t = min(t, l)
    nb = l // t

    def body(cur_ref, prev_ref, dyp_ref, wp_ref, sc_ref, dpp_ref, dwp_ref, sums_ref, carry_ref):
        n = pl.program_id(0)
        i = nb - 1 - n

        @pl.when(n == 0)
        def _():
            carry_ref[...] = jnp.zeros_like(carry_ref)
            dwp_ref[...] = jnp.zeros_like(dwp_ref)
            sums_ref[...] = jnp.zeros_like(sums_ref)

        cur_u = cur_ref[:, :POOL_W].astype(F32)
        pooled = _pool_pooled(i, cur_u, prev_ref[...].astype(F32), t)
        lin = _pool_mix(pooled, wp_ref)
        sz, dsz = _silu_and_grad(cur_ref[:, POOL_W:].astype(F32))
        dyp_v = dyp_ref[...]
        scale = sc_ref[...]
        dpp_ref[:, POOL_W:] = (dyp_v * lin * scale * dsz).astype(BF16)
        dpre = dyp_v * sz
        sums_ref[0:1, :] += jnp.sum(dpre * lin, axis=0, keepdims=True)
        dlin = dpre * scale
        dpooled = []
        for g in range(4):
            dl = dlin[:, g * POOL_GW:(g + 1) * POOL_GW]
            dwp_ref[g] += _dot(pooled[:, g * POOL_GW:(g + 1) * POOL_GW], dl, TN)
            dpooled.append(_dot(dl, wp_ref[g], NT))
        dpooled = jnp.concatenate(dpooled, axis=1)
        inv_cnt, _ = _pool_count(i, t)
        dq = dpooled * inv_cnt
        ext = jnp.concatenate([dq, carry_ref[...]], axis=0)
        du = _window_sums(ext, t + POOL_HALO, False)[:t, :] - dpooled
        dpp_ref[:, :POOL_W] = du.astype(BF16)
        carry_ref[...] = dq[:POOL_HALO, :]

    hb = t // POOL_HALO
    return pl.pallas_call(
        body, grid=(nb,),
        in_specs=[pl.BlockSpec((t, 2 * POOL_W), lambda n: (nb - 1 - n, 0)),
                  pl.BlockSpec((POOL_HALO, POOL_W), lambda n: (jnp.maximum((nb - 1 - n) * hb - 1, 0), 0)),
                  pl.BlockSpec((t, POOL_W), lambda n: (nb - 1 - n, 0)),
                  pl.BlockSpec((4, POOL_GW, POOL_GW), lambda n: (0, 0, 0)),
                  pl.BlockSpec((1, POOL_W), lambda n: (0, 0))],
        out_specs=[pl.BlockSpec((t, 2 * POOL_W), lambda n: (nb - 1 - n, 0)),
                   pl.BlockSpec((4, POOL_GW, POOL_GW), lambda n: (0, 0, 0)),
                   pl.BlockSpec((8, POOL_W), lambda n: (0, 0))],
        out_shape=[jax.ShapeDtypeStruct((l, 2 * POOL_W), BF16), jax.ShapeDtypeStruct((4, POOL_GW, POOL_GW), F32),
                   jax.ShapeDtypeStruct((8, POOL_W), F32)],
        scratch_shapes=[pltpu.VMEM((POOL_HALO, POOL_W), F32)],
        compiler_params=_params("arbitrary"), name=name)(pp, pp, dyp, wpool, pscale)


def _merge_fwd(ya, ys, yp, wa, ws, wp, pg, wout, x, gate, *, name, tm=512):
    l, d = x.shape
    tm = min(tm, l)

    def body(ya_ref, ys_ref, yp_ref, wa_ref, ws_ref, wp_ref, pg_ref, wo_ref, x_ref, g_ref,
             xn_ref, mg_ref, ba_ref, bs_ref, bp_ref, out_ref):
        acc = None
        for k, (y_ref, w_ref, b_ref) in enumerate(((ya_ref, wa_ref, ba_ref), (ys_ref, ws_ref, bs_ref),
                                                   (yp_ref, wp_ref, bp_ref))):
            br = _dot(y_ref[...], w_ref[...])
            b_ref[...] = br.astype(BF16)
            term = _sigmoid(pg_ref[:, k * d:(k + 1) * d].astype(F32)) * br
            acc = term if acc is None else acc + term
        merged = acc.astype(BF16)
        mg_ref[...] = merged
        out = _dot(merged, wo_ref[...])
        out_ref[...] = out.astype(BF16)
        xn_ref[...] = x_ref[...] + g_ref[...] * out

    rowy = pl.BlockSpec((tm, ATT_W), lambda i: (i, 0))
    wsp = pl.BlockSpec((ATT_W, d), lambda i: (0, 0))
    rowd = pl.BlockSpec((tm, d), lambda i: (i, 0))
    return pl.pallas_call(
        body, grid=(l // tm,),
        in_specs=[rowy, rowy, rowy, wsp, wsp, wsp, pl.BlockSpec((tm, 3 * d), lambda i: (i, 0)),
                  pl.BlockSpec((d, d), lambda i: (0, 0)), rowd, pl.BlockSpec((1, d), lambda i: (0, 0))],
        out_specs=[rowd] * 6,
        out_shape=[jax.ShapeDtypeStruct((l, d), F32)] + [jax.ShapeDtypeStruct((l, d), BF16)] * 5,
        compiler_params=_params("parallel"), name=name)(ya, ys, yp, wa, ws, wp, pg, wout, x, gate)


def _merge_bwd(dx, out, gate, wout, pg, brs, wbrs, ys, merged, *, name, tm=256):
    l, d = dx.shape
    tm = min(tm, l)
    nb = l // tm
    w = ys[0].shape[1]

    def body(dx_ref, out_ref, g_ref, w_ref, pg_ref, ba_ref, bs_ref, bp_ref, wa_ref, ws_ref, wp_ref,
             ya_ref, ys_ref, yp_ref, mg_ref,
             dya_ref, dys_ref, dyp_ref, dpg_ref, sums_ref, dwa_ref, dws_ref, dwp_ref, dwo_ref, acc_br, acc_out):
        i = pl.program_id(0)

        @pl.when(i == 0)
        def _():
            sums_ref[...] = jnp.zeros_like(sums_ref)
            acc_br[...] = jnp.zeros_like(acc_br)
            acc_out[...] = jnp.zeros_like(acc_out)

        dxv = dx_ref[...]
        sums_ref[0:1, :] += jnp.sum(dxv * out_ref[...].astype(F32), axis=0, keepdims=True)
        dmo = (dxv * g_ref[...]).astype(BF16)
        acc_out[...] += _dot(mg_ref[...], dmo, TN)
        dmerged = _dot(dmo, w_ref[...], NT)
        branches = ((ba_ref, wa_ref, ya_ref, dya_ref), (bs_ref, ws_ref, ys_ref, dys_ref), (bp_ref, wp_ref, yp_ref, dyp_ref))
        for k, (b_ref, wk_ref, y_ref, dy_ref) in enumerate(branches):
            gk = _sigmoid(pg_ref[:, k * d:(k + 1) * d].astype(F32))
            dbr = (dmerged * gk).astype(BF16)
            dpg_ref[:, k * d:(k + 1) * d] = (dmerged * b_ref[...].astype(F32) * gk * (1.0 - gk)).astype(BF16)
            dy_ref[...] = _dot(dbr, wk_ref[...], NT)
            acc_br[k] += _dot(y_ref[...], dbr, TN)

        @pl.when(i == nb - 1)
        def _():
            for k, dw_ref in enumerate((dwa_ref, dws_ref, dwp_ref)):
                dw_ref[...] = acc_br[k].astype(BF16)
            dwo_ref[...] = acc_out[...].astype(BF16)

    row = pl.BlockSpec((tm, d), lambda i: (i, 0))
    half = pl.BlockSpec((tm, w), lambda i: (i, 0))
    wide = pl.BlockSpec((tm, 3 * d), lambda i: (i, 0))
    const = lambda shape: pl.BlockSpec(shape, lambda i: (0,) * len(shape))
    return pl.pallas_call(
        body, grid=(nb,),
        in_specs=[row, row, const((1, d)), const((d, d)), wide, row, row, row, const((w, d)), const((w, d)), const((w, d)),
                  half, half, half, row],
        out_specs=[half, half, half, wide, const((8, d)), const((w, d)), const((w, d)), const((w, d)), const((d, d))],
        out_shape=[jax.ShapeDtypeStruct((l, w), F32)] * 3 + [jax.ShapeDtypeStruct((l, 3 * d), BF16),
                                                             jax.ShapeDtypeStruct((8, d), F32)]
                  + [jax.ShapeDtypeStruct((w, d), BF16)] * 3 + [jax.ShapeDtypeStruct((d, d), BF16)],
        scratch_shapes=[pltpu.VMEM((3, w, d), F32), pltpu.VMEM((d, d), F32)],
        compiler_params=_params("arbitrary"), name=name)(dx, out, gate, wout, pg, *brs, *wbrs, *ys, merged)


def _adamw(w, g, m, v, *, name, tr=256):
    r, c = w.shape
    p = g.shape[0]
    tr = min(tr, r)
    assert r % tr == 0
    c1 = 1.0 / (1.0 - ADAM_B1 ** ADAM_STEP)
    c2 = 1.0 / (1.0 - ADAM_B2 ** ADAM_STEP)

    def body(w_ref, g_ref, m_ref, v_ref, go_ref, d_ref, mo_ref, vo_ref):
        gv = g_ref[0].astype(F32)
        for k in range(1, p):
            gv = gv + g_ref[k].astype(F32)
        go_ref[...] = gv
        mn = ADAM_B1 * m_ref[...] + (1.0 - ADAM_B1) * gv
        vn = ADAM_B2 * v_ref[...] + (1.0 - ADAM_B2) * (gv * gv)
        mo_ref[...] = mn
        vo_ref[...] = vn
        d_ref[...] = -ADAM_LR * ((mn * c1) / (jnp.sqrt(vn * c2) + ADAM_EPS) + ADAM_WD * w_ref[...])

    row = pl.BlockSpec((tr, c), lambda i: (i, 0))
    return pl.pallas_call(
        body, grid=(r // tr,),
        in_specs=[row, pl.BlockSpec((p, tr, c), lambda i: (0, i, 0)), row, row],
        out_specs=[row] * 4,
        out_shape=[jax.ShapeDtypeStruct((r, c), F32)] * 4,
        compiler_params=_params("parallel"), name=name)(w, g, m, v)


def _exchange(arrs, *, scatter, name):
    n = len(arrs)
    out_shape = [jax.ShapeDtypeStruct(a.shape if scatter else (N_DEV,) + a.shape, a.dtype) for a in arrs]

    def body(*refs):
        ins, outs = refs[:n], refs[n:2 * n]
        send_sems, recv_sems, loc_sems = refs[2 * n:]
        me = 4 * lax.axis_index("x") + 2 * lax.axis_index("y") + lax.axis_index("c")
        local = []
        for k in range(n):
            src = ins[k].at[me] if scatter else ins[k]
            cp = pltpu.make_async_copy(src, outs[k].at[me], loc_sems.at[k])
            cp.start()
            local.append(cp)
        remote = []
        for r in range(1, N_DEV):
            peer = me ^ r
            for k in range(n):
                src = ins[k].at[peer] if scatter else ins[k]
                cp = pltpu.make_async_remote_copy(
                    src_ref=src, dst_ref=outs[k].at[me], send_sem=send_sems.at[k, r - 1], recv_sem=recv_sems.at[k, r - 1],
                    device_id=(peer // 4, (peer // 2) % 2, peer % 2), device_id_type=pl.DeviceIdType.MESH)
                cp.start()
                remote.append(cp)
        for cp in remote:
            cp.wait()
        for cp in local:
            cp.wait()

    anyspec = pl.BlockSpec(memory_space=pl.ANY)
    return pl.pallas_call(
        body, in_specs=[anyspec] * n, out_specs=[anyspec] * n, out_shape=out_shape,
        scratch_shapes=[pltpu.SemaphoreType.DMA((n, N_DEV - 1)), pltpu.SemaphoreType.DMA((n, N_DEV - 1)),
                        pltpu.SemaphoreType.DMA((n,))],
        name=name)(*arrs)


def _mesh_place():
    x, y, c = lax.axis_index("x"), lax.axis_index("y"), lax.axis_index("c")
    other_chips = [(1 - x, y), (x, 1 - y), (1 - x, 1 - y)]
    return x, y, c, other_chips


def _gather_two_level(arrs, *, name):
    n = len(arrs)
    plan = _gather_plan(arrs)

    def body(*refs):
        start, forward, finish = plan["phases"](refs[:n], refs[n:2 * n], *refs[2 * n:])
        start()
        forward()
        finish()

    anyspec = pl.BlockSpec(memory_space=pl.ANY)
    return pl.pallas_call(
        body, in_specs=[anyspec] * n, out_specs=[anyspec] * n, out_shape=plan["out_shape"],
        scratch_shapes=plan["scratch"], name=name)(*arrs)


def _gather_plan(arrs):
    n = len(arrs)

    def phases(ins, outs, send_sems, recv_sems, loc_sems):
        x, y, c, chips = _mesh_place()
        me = 4 * x + 2 * y + c
        slot = lambda px, py, pc: 4 * px + 2 * py + pc

        def copy(k, j, src, block, to):
            return pltpu.make_async_remote_copy(
                src_ref=src, dst_ref=outs[k].at[block], send_sem=send_sems.at[k, j], recv_sem=recv_sems.at[k, j],
                device_id=to, device_id_type=pl.DeviceIdType.MESH)

        local = [pltpu.make_async_copy(ins[k], outs[k].at[me], loc_sems.at[k]) for k in range(n)]
        first = []
        for k in range(n):
            first.append(copy(k, 0, ins[k], me, (x, y, 1 - c)))
            for j, chip in enumerate(chips):
                first.append(copy(k, 1 + j, ins[k], me, (*chip, c)))
        passed = [copy(k, 4 + j, outs[k].at[slot(*chip, c)], slot(*chip, c), (x, y, 1 - c))
                  for j, chip in enumerate(chips) for k in range(n)]

        def start():
            for cp in local + first:
                cp.start()

        def forward():
            for j, chip in enumerate(chips):
                for k in range(n):
                    copy(k, 1 + j, ins[k], slot(*chip, c), (x, y, c)).wait_recv()
                    passed[j * n + k].start()

        def finish():
            for k in range(n):
                copy(k, 0, ins[k], slot(x, y, 1 - c), (x, y, c)).wait_recv()
                for j, chip in enumerate(chips):
                    copy(k, 4 + j, ins[k], slot(*chip, 1 - c), (x, y, c)).wait_recv()
            for cp in first + passed:
                cp.wait_send()
            for cp in local:
                cp.wait()

        return start, forward, finish

    return dict(
        args=list(arrs), out_shape=[jax.ShapeDtypeStruct((N_DEV,) + a.shape, a.dtype) for a in arrs],
        scratch=[pltpu.SemaphoreType.DMA((n, 7)), pltpu.SemaphoreType.DMA((n, 7)), pltpu.SemaphoreType.DMA((n,))],
        phases=phases)


def _allreduce_small(small, extra, *, name):
    r, lanes = small.shape
    assert r % 16 == 0
    h = r // 2
    e = extra.shape[0]

    def body(s_ref, x_ref, out_ref, xall_ref, sib_ref, parts_ref, send_sems, recv_sems):
        x, y, c, chips = _mesh_place()
        me = 4 * x + 2 * y + c
        my_chip = 2 * x + y
        sibling = (x, y, 1 - c)
        mine = pl.ds(pl.multiple_of(c * h, 8), h)
        theirs = pl.ds(pl.multiple_of((1 - c) * h, 8), h)

        def remote(j, src, dst, to):
            return pltpu.make_async_remote_copy(src_ref=src, dst_ref=dst, send_sem=send_sems.at[j],
                                                recv_sem=recv_sems.at[j], device_id=to, device_id_type=pl.DeviceIdType.MESH)

        to_sibling = remote(0, s_ref.at[theirs], sib_ref, sibling)
        to_sibling.start()
        xall_ref[me] = x_ref[...]
        extras = []
        for rr in range(1, N_DEV):
            peer = me ^ rr
            cp = remote(4 + rr, x_ref, xall_ref.at[me], (peer // 4, (peer // 2) % 2, peer % 2))
            cp.start()
            extras.append(cp)
        to_sibling.wait_recv()
        parts_ref[my_chip] = s_ref[mine] + sib_ref[...]
        to_chips = [remote(1 + j, parts_ref.at[my_chip], parts_ref.at[my_chip], (px, py, c))
                    for j, (px, py) in enumerate(chips)]
        for cp in to_chips:
            cp.start()
        for cp in to_chips:
            cp.wait_recv()
        out_ref[mine] = (parts_ref[0] + parts_ref[1]) + (parts_ref[2] + parts_ref[3])
        done = remote(4, out_ref.at[mine], out_ref.at[mine], sibling)
        done.start()
        remote(4, out_ref.at[theirs], out_ref.at[theirs], sibling).wait_recv()
        for cp in extras:
            cp.wait()
        to_sibling.wait_send()
        for cp in to_chips:
            cp.wait_send()
        done.wait_send()

    vmem = pl.BlockSpec(memory_space=pltpu.VMEM)
    return pl.pallas_call(
        body, in_specs=[vmem, vmem], out_specs=[vmem, vmem],
        out_shape=[jax.ShapeDtypeStruct((r, lanes), F32), jax.ShapeDtypeStruct((N_DEV, e, lanes), F32)],
        scratch_shapes=[pltpu.VMEM((h, lanes), F32), pltpu.VMEM((4, h, lanes), F32),
                        pltpu.SemaphoreType.DMA((12,)), pltpu.SemaphoreType.DMA((12,))],
        compiler_params=pltpu.CompilerParams(vmem_limit_bytes=VMEM_LIMIT), name=name)(small, extra)


def _sibling_swap(arrs, *, name):
    n = len(arrs)
    out_shape = [jax.ShapeDtypeStruct(a.shape[1:], a.dtype) for a in arrs]

    def body(*refs):
        ins, outs = refs[:n], refs[n:2 * n]
        send_sems, recv_sems = refs[2 * n:]
        x, y, c, _ = _mesh_place()
        copies = [pltpu.make_async_remote_copy(
            src_ref=ins[k].at[1 - c], dst_ref=outs[k], send_sem=send_sems.at[k], recv_sem=recv_sems.at[k],
            device_id=(x, y, 1 - c), device_id_type=pl.DeviceIdType.MESH) for k in range(n)]
        for cp in copies:
            cp.start()
        for cp in copies:
            cp.wait()

    anyspec = pl.BlockSpec(memory_space=pl.ANY)
    return pl.pallas_call(
        body, in_specs=[anyspec] * n, out_specs=[anyspec] * n, out_shape=out_shape,
        scratch_shapes=[pltpu.SemaphoreType.DMA((n,)), pltpu.SemaphoreType.DMA((n,))], name=name)(*arrs)


def _pair_add(mine, theirs, core, *, name, tr=256):
    _, r, c = mine.shape
    tr = min(tr, r)
    assert r % tr == 0

    def body(core_ref, m_ref, t_ref, o_ref):
        o_ref[...] = (m_ref[0].astype(F32) + t_ref[...].astype(F32)).astype(BF16)

    return pl.pallas_call(
        body,
        grid_spec=pltpu.PrefetchScalarGridSpec(
            num_scalar_prefetch=1, grid=(r // tr,),
            in_specs=[pl.BlockSpec((1, tr, c), lambda i, core_ref: (core_ref[0], i, 0)),
                      pl.BlockSpec((tr, c), lambda i, core_ref: (i, 0))],
            out_specs=pl.BlockSpec((tr, c), lambda i, core_ref: (i, 0))),
        out_shape=jax.ShapeDtypeStruct((r, c), BF16),
        compiler_params=_params("parallel"), name=name)(core, mine, theirs)


def _chip_scatter(arrs, *, name):
    n = len(arrs)
    plan = _chip_scatter_plan(arrs)

    def body(*refs):
        start, _, finish = plan["phases"](refs[:n], refs[n:2 * n], *refs[2 * n:])
        start()
        finish()

    anyspec = pl.BlockSpec(memory_space=pl.ANY)
    return pl.pallas_call(
        body, in_specs=[anyspec] * n, out_specs=[anyspec] * n, out_shape=plan["out_shape"],
        scratch_shapes=plan["scratch"], name=name)(*arrs)


def _chip_scatter_plan(arrs):
    n = len(arrs)

    def phases(ins, outs, send_sems, recv_sems, loc_sems):
        x, y, c, chips = _mesh_place()
        mine = 2 * x + y
        local = [pltpu.make_async_copy(ins[k].at[mine], outs[k].at[mine], loc_sems.at[k]) for k in range(n)]
        remote = [pltpu.make_async_remote_copy(
            src_ref=ins[k].at[2 * px + py], dst_ref=outs[k].at[mine], send_sem=send_sems.at[k, j],
            recv_sem=recv_sems.at[k, j], device_id=(px, py, c), device_id_type=pl.DeviceIdType.MESH)
            for j, (px, py) in enumerate(chips) for k in range(n)]

        def start():
            for cp in local + remote:
                cp.start()

        def finish():
            for cp in remote:
                cp.wait()
            for cp in local:
                cp.wait()

        return start, (lambda: None), finish

    return dict(
        args=list(arrs), out_shape=[jax.ShapeDtypeStruct(a.shape, a.dtype) for a in arrs],
        scratch=[pltpu.SemaphoreType.DMA((n, 3)), pltpu.SemaphoreType.DMA((n, 3)), pltpu.SemaphoreType.DMA((n,))],
        phases=phases)


def _ssm_discretize(a_re, a_im, log_dt, b_re, b_im):
    dt = jnp.exp(log_dt)[:, None]
    mag = jnp.exp(a_re * dt)
    lr = mag * jnp.cos(a_im * dt)
    li = mag * jnp.sin(a_im * dt)
    den = a_re * a_re + a_im * a_im
    cr = ((lr - 1.0) * a_re + li * a_im) / den
    ci = (li * a_re - (lr - 1.0) * a_im) / den
    bbr = cr[..., None] * b_re - ci[..., None] * b_im
    bbi = cr[..., None] * b_im + ci[..., None] * b_re
    return lr, li, bbr, bbi


def _ssm_dense(lr, li, bbr, bbi, c_re, c_im):
    scan_ops = _scan_tables(lr.reshape(-1), li.reshape(-1)) + _scan_consts(SEQ_BLOCK)
    per_half = SSM_GROUPS // SSM_HALVES

    def halves(a, rows, cols):
        a = a.reshape(SSM_HALVES, per_half * rows, cols)
        tiled = jnp.tile(a, (1, 1, per_half))
        r = lax.broadcasted_iota(jnp.int32, tiled.shape, 1) // rows
        c = lax.broadcasted_iota(jnp.int32, tiled.shape, 2) // cols
        return jnp.where(r == c, tiled, 0.0)

    bblk = jnp.stack([halves(b.transpose(0, 2, 1), SSM_GROUP, SSM_STATE) for b in (bbr, bbi)]).astype(BF16)
    cblk = jnp.stack([halves(c.transpose(0, 2, 1), SSM_STATE, SSM_GROUP) for c in (c_re, c_im)]).astype(BF16)
    return scan_ops, bblk, cblk


def _ssm_extract(db, dc, sums):
    db = db.reshape(2, SSM_GROUPS, SSM_GROUP, SSM_STATE).transpose(0, 1, 3, 2)
    dc = dc.reshape(2, SSM_GROUPS, SSM_GROUP, SSM_STATE)
    dlr = sums[0].reshape(SSM_GROUPS, SSM_STATE)
    dli = sums[1].reshape(SSM_GROUPS, SSM_STATE)
    return dlr, dli, db[0], db[1], dc[0], dc[1]


IN_SPLITS = (ATT_W, KV_W, KV_W, SSM_W, POOL_W, ATT_W, SSM_W, POOL_W, 3 * D_MODEL)


def _split_w_in(w):
    idx = [0]
    for s in IN_SPLITS:
        idx.append(idx[-1] + s)
    seg = [w[..., idx[k]:idx[k + 1]] for k in range(len(IN_SPLITS))]
    q, k, v, us, up, za, zs, zp, gl = seg
    return (jnp.concatenate([q, za, k, v], axis=-1), jnp.concatenate([us, zs], axis=-1),
            jnp.concatenate([up, zp], axis=-1), gl)


def _merge_w_in(da, ds, dp, dg):
    q, za, k, v = da[..., :ATT_W], da[..., ATT_W:2 * ATT_W], da[..., 2 * ATT_W:2 * ATT_W + KV_W], da[..., 2 * ATT_W + KV_W:]
    us, zs = ds[..., :SSM_W], ds[..., SSM_W:]
    up, zp = dp[..., :POOL_W], dp[..., POOL_W:]
    return jnp.concatenate([q, k, v, us, up, za, zs, zp, dg], axis=-1)


def _layer_fwd(x, lw, li, late=None, comm_attn=None, comm_ssm=None):
    tag = f"l{li}"
    h = _ln_fwd(x, lw["norm_g"], lw["shift"], lw["scale"], name=f"ln_fwd_{tag}")
    pa = _mm(h, lw["w_a"], tn=1280, out_dtype=BF16, name=f"proj_a_{tag}")
    ps = _mm(h, lw["w_s"], out_dtype=BF16, name=f"proj_s_{tag}")
    pp = _mm(h, lw["w_p"], out_dtype=BF16, name=f"proj_p_{tag}")
    if late is None:
        pg = _mm(h, lw["w_g"], out_dtype=BF16, name=f"proj_g_{tag}")
    else:
        pg, arrived = _mm(h, lw["w_g"], out_dtype=BF16, name=f"proj_g_{tag}", comm=late[0])
        lw = {**lw, **late[1](arrived)}
    ya, from_attn = _attn_fwd(pa, lw["sinks"], name=f"attn_fwd_{tag}", comm=comm_attn)
    ys, chk, states, *from_ssm = _ssm_fwd(ps, lw["lam"], lw["bblk"], lw["cblk"], lw["ssm_d"], lw["w_glu"], lw["b_glu"],
                                          name=f"ssm_fwd_{tag}", comm=comm_ssm)
    yp = _pool_fwd(pp, lw["w_pool"], lw["pool_scale"], name=f"pool_fwd_{tag}")
    x_new, merged, ba, bs, bp, out = _merge_fwd(ya, ys, yp, lw["w_br_att"], lw["w_br_ssm"], lw["w_br_pool"], pg,
                                                lw["w_out"], x, lw["gate"], name=f"merge_fwd_{tag}")
    saved = dict(x=x, h=h, pa=pa, ps=ps, pp=pp, pg=pg, ya=ya, ys=ys, yp=yp, chk=chk, states=states, merged=merged,
                 ba=ba, bs=bs, bp=bp, out=out)
    return x_new, saved, lw, list(from_attn), list(from_ssm)


def _layer_bwd(dx, lw, sv, li, comm=None, own=None):
    tag = f"l{li}"
    g = {}
    dya, dys, dyp, dpg, gate_sums, g["w_br_att"], g["w_br_ssm"], g["w_br_pool"], g["w_out"] = _merge_bwd(
        dx, sv["out"], lw["gate"], lw["w_out"], sv["pg"], (sv["ba"], sv["bs"], sv["bp"]),
        (lw["w_br_att"], lw["w_br_ssm"], lw["w_br_pool"]), (sv["ya"], sv["ys"], sv["yp"]), sv["merged"],
        name=f"merge_bwd_{tag}")
    dpa, dsink = _attn_bwd(sv["pa"], lw["sinks"], dya, name=f"attn_bwd_{tag}")
    dps, db_dense, dc_dense, dwglu, ssm_sums, *exchanged = _ssm_bwd(
        sv["ps"], dys, sv["chk"], sv["states"], lw["lam"], lw["bblk"], lw["cblk"], lw["ssm_d"], lw["w_glu"], lw["b_glu"],
        name=f"ssm_bwd_{tag}", comm=comm)
    g["w_glu"] = dwglu.astype(BF16)
    dpp, dwpool, pool_sums = _pool_bwd(sv["pp"], dyp, lw["w_pool"], lw["pool_scale"], name=f"pool_bwd_{tag}")
    h = sv["h"]
    dw_a = _mm_tn(h, dpa, out_dtype=BF16, tn=1280, name=f"dw_a_{tag}")
    dw_s = _mm_tn(h, dps, out_dtype=BF16, name=f"dw_s_{tag}")
    dw_p = _mm_tn(h, dpp, out_dtype=BF16, name=f"dw_p_{tag}")
    dh_pairs = [(dpa, lw["w_a"]), (dps, lw["w_s"]), (dpp, lw["w_p"]), (dpg, lw["w_g"])]
    if own is None:
        dw_g, from_late = _mm_tn(h, dpg, out_dtype=BF16, name=f"dw_g_{tag}"), []
        g["w_in"] = _merge_w_in(dw_a, dw_s, dw_p, dw_g)
        dh, from_w_in = _mm_nt_sum(dh_pairs, name=f"dh_{tag}"), []
    else:
        dw_g, from_late = _mm_tn(h, dpg, out_dtype=BF16, name=f"dw_g_{tag}", comm=own({k: g[k] for k in LATE_WEIGHTS}))
        g["w_in"] = _merge_w_in(dw_a, dw_s, dw_p, dw_g)
        dh, from_w_in = _mm_nt_sum(dh_pairs, name=f"dh_{tag}", comm=own({"w_in": g["w_in"]}))
    dx_in, ln_sums = _ln_bwd(sv["x"], dh, dx, lw["norm_g"], lw["scale"], name=f"ln_bwd_{tag}")
    g["dmod"] = jnp.concatenate([ln_sums[0], ln_sums[1], gate_sums[0]])
    g["norm_g"] = ln_sums[2]
    g["attn_sinks"] = dsink[:, 0]
    g["ssm_raw"] = _ssm_extract(db_dense, dc_dense, ssm_sums)
    g["ssm_d"] = ssm_sums[2, :SSM_W]
    g["b_glu"] = ssm_sums[3, :SSM_W]
    g["w_pool"] = dwpool
    g["pool_scale"] = pool_sums[0]
    return dx_in, g, exchanged, list(from_w_in) + list(from_late)


BIG_WEIGHTS = ("w_in", "w_glu", "w_br_att", "w_br_ssm", "w_br_pool", "w_out")
ROW_SHARDED = ("w_glu", "w_out")


LATE_WEIGHTS = BIG_WEIGHTS[1:]


def _full_weights(keys, gathered):
    full = {}
    for k, g in zip(keys, gathered):
        if k in ROW_SHARDED:
            full[k] = g.reshape(N_DEV * g.shape[1], g.shape[2])
        else:
            full[k] = g.transpose(1, 0, 2).reshape(g.shape[1], N_DEV * g.shape[2])
    return full


def _by_destination(keys, grads):
    out = []
    for k in keys:
        g = grads[k]
        if k in ROW_SHARDED:
            out.append(g.reshape(4, 2, g.shape[0] // N_DEV, g.shape[1]).transpose(1, 0, 2, 3))
        else:
            out.append(g.reshape(g.shape[0], 4, 2, g.shape[1] // N_DEV).transpose(2, 1, 0, 3))
    return out


def _prepare_layer(li, mod, norm_g, w_in_full, attn_sinks, disc, ssm_c_re, ssm_c_im, ssm_d, b_glu, w_pool, pool_scale):
    d = D_MODEL
    lr, li_, bbr, bbi = disc
    lam, bblk, cblk = _ssm_dense(lr[li], li_[li], bbr[li], bbi[li], ssm_c_re[li], ssm_c_im[li])
    w_a, w_s, w_p, w_g = _split_w_in(w_in_full)
    return dict(
        norm_g=norm_g[li][None, :], shift=mod[li, :d][None, :], scale=mod[li, d:2 * d][None, :],
        gate=mod[li, 2 * d:][None, :], w_a=w_a, w_s=w_s, w_p=w_p, w_g=w_g,
        sinks=attn_sinks[li], lam=lam, bblk=bblk, cblk=cblk, ssm_d=ssm_d[li][None, :],
        b_glu=b_glu[li][None, :], w_pool=w_pool[li].astype(BF16), pool_scale=pool_scale[li][None, :])


SMALL_ROWS = 64
SMALL_ORDER = ("norm_g", "attn_sinks", "ssm_d", "b_glu", "w_pool", "pool_scale", "dmod")


def _pack_small(loss, dfinal_g, layer_grads):
    parts = [jnp.broadcast_to(loss.reshape(1), (128,)), dfinal_g]
    for g in layer_grads:
        for k in SMALL_ORDER:
            v = g[k].reshape(-1)
            if v.shape[0] % 128:
                v = jnp.pad(v, (0, 128 - v.shape[0] % 128))
            parts.append(v)
        for v in g["ssm_raw"]:
            parts.append(v.reshape(-1))
    flat = jnp.concatenate(parts)
    return jnp.pad(flat, (0, (-flat.shape[0]) % (SMALL_ROWS * 128))).reshape(-1, 128)


def _unpack_small(flat, shapes):
    out, off = [], 0
    for s in shapes:
        n = int(math.prod(s))
        out.append(flat[off:off + n].reshape(s))
        off += n + (-n) % 128
    return out


def kernel(x, c, norm_g, w_ada, b_ada, w_in, attn_sinks, ssm_a_re, ssm_a_im, ssm_log_dt, ssm_b_re, ssm_b_im, ssm_c_re, ssm_c_im, ssm_d, w_glu, b_glu, w_pool, pool_scale, w_br_att, w_br_ssm, w_br_pool, w_out, final_g, loss_target, m_norm_g, m_w_ada, m_b_ada, m_w_in, m_attn_sinks, m_ssm_a_re, m_ssm_a_im, m_ssm_log_dt, m_ssm_b_re, m_ssm_b_im, m_ssm_c_re, m_ssm_c_im, m_ssm_d, m_w_glu, m_b_glu, m_w_pool, m_pool_scale, m_w_br_att, m_w_br_ssm, m_w_br_pool, m_w_out, m_final_g, v_norm_g, v_w_ada, v_b_ada, v_w_in, v_attn_sinks, v_ssm_a_re, v_ssm_a_im, v_ssm_log_dt, v_ssm_b_re, v_ssm_b_im, v_ssm_c_re, v_ssm_c_im, v_ssm_d, v_w_glu, v_b_glu, v_w_pool, v_pool_scale, v_w_br_att, v_w_br_ssm, v_w_br_pool, v_w_out, v_final_g):
    me = 4 * lax.axis_index("x") + 2 * lax.axis_index("y") + lax.axis_index("c")
    d = D_MODEL
    ada_w = 3 * d // N_DEV

    (c_all,) = _exchange([c.reshape(8, 128)], scatter=False, name="gather_c")
    c_act = jax.nn.silu(c_all.reshape(N_DEV, d))
    b_cols = lax.dynamic_slice(b_ada, (0, me * ada_w), (DEPTH, ada_w))
    mod_part = jnp.concatenate(
        [_mm(c_act, w_ada[li], name=f"ada_fwd_l{li}") + b_cols[li][None, :] for li in range(DEPTH)], axis=0)
    (mod_all,) = _exchange([mod_part], scatter=False, name="gather_mod")
    mod_all = mod_all.reshape(N_DEV, DEPTH, N_DEV, ada_w)
    mod_mine = lax.dynamic_index_in_dim(mod_all, me, axis=2, keepdims=False)
    mod_mine = mod_mine.transpose(1, 0, 2).reshape(DEPTH, 3 * d)

    sharded = dict(w_in=w_in, w_glu=w_glu, w_br_att=w_br_att, w_br_ssm=w_br_ssm, w_br_pool=w_br_pool, w_out=w_out)
    shards = lambda li, keys: [sharded[k][li].astype(BF16) for k in keys]
    disc, disc_vjp = jax.vjp(jax.vmap(_ssm_discretize), ssm_a_re, ssm_a_im, ssm_log_dt, ssm_b_re, ssm_b_im)
    layer = lambda li, gathered_w_in: _prepare_layer(
        li, mod_mine, norm_g, _full_weights(("w_in",), gathered_w_in)["w_in"], attn_sinks, disc, ssm_c_re, ssm_c_im,
        ssm_d, b_glu, w_pool, pool_scale)
    late_weights = lambda gathered: _full_weights(LATE_WEIGHTS, gathered)
    core = lax.axis_index("c").astype(jnp.int32).reshape(1)

    def chip_sums_of(keys, grads_li, tag):
        by_dest = _by_destination(keys, grads_li)
        from_sibling = _sibling_swap(by_dest, name=f"grads_sibling_swap_{tag}")
        return [_pair_add(a.reshape(2, -1, a.shape[-1]), b.reshape(-1, b.shape[-1]), core,
                          name=f"grads_pair_add_{tag}_{k}").reshape(b.shape)
                for k, (a, b) in zip(keys, zip(by_dest, from_sibling))]

    layers, saved, grads = [None] * DEPTH, [None] * DEPTH, [None] * DEPTH
    layers[0] = layer(0, _gather_two_level(shards(0, ("w_in",)), name="gather_w_in_l0"))
    xs, saved[0], layers[0], late1, w_in1 = _layer_fwd(
        x[0], layers[0], 0, late=(_gather_plan(shards(0, LATE_WEIGHTS)), late_weights),
        comm_attn=_gather_plan(shards(1, LATE_WEIGHTS)), comm_ssm=_gather_plan(shards(1, ("w_in",))))
    layers[1] = {**layer(1, w_in1), **late_weights(late1)}
    xs, saved[1], _, _, _ = _layer_fwd(xs, layers[1], 1)
    dx, fin_sums = _final_loss(xs, final_g[None, :], loss_target[0])
    loss_part = jnp.sum(fin_sums[1])
    dx, grads[1], _, _ = _layer_bwd(dx, layers[1], saved[1], 1)
    dx, grads[0], scattered1, scattered0 = _layer_bwd(
        dx, layers[0], saved[0], 0, comm=_chip_scatter_plan(chip_sums_of(BIG_WEIGHTS, grads[1], "l1")),
        own=lambda g: _chip_scatter_plan(chip_sums_of(tuple(g), g, "l0_" + "_".join(g))))
    big = [jnp.stack([a, b], axis=1) for a, b in zip(scattered0, scattered1)]
    grad_x = dx[None]

    small = _pack_small(loss_part, fin_sums[0], grads)
    dmod_rows = jnp.concatenate([grads[li]["dmod"] for li in range(DEPTH)]).reshape(-1, 128)
    small_sum, dmod_gathered = _allreduce_small(small, dmod_rows, name="allreduce_small")
    out = {}

    def adam(name, w, g_parts, m, v):
        shp = w.shape
        r = int(math.prod(shp[:-1])) if len(shp) > 1 else 1
        w2, m2, v2 = (a.reshape(r, shp[-1]) for a in (w, m, v))
        g2 = g_parts.reshape(g_parts.shape[0], r, shp[-1])
        res = _adamw(w2, g2, m2, v2, name=f"adamw_{name}")
        out[name] = tuple(a.reshape(shp) for a in res)

    flat = small_sum.reshape(-1)
    shapes = [(128,), (d,)]
    for _ in range(DEPTH):
        shapes += [(d,), (N_HEADS,), (SSM_W,), (SSM_W,), (4, POOL_GW, POOL_GW), (POOL_W,), (3 * d,),
                   (SSM_GROUPS, SSM_STATE), (SSM_GROUPS, SSM_STATE), (SSM_GROUPS, SSM_STATE, SSM_GROUP),
                   (SSM_GROUPS, SSM_STATE, SSM_GROUP), (SSM_GROUPS, SSM_GROUP, SSM_STATE), (SSM_GROUPS, SSM_GROUP, SSM_STATE)]
    un = _unpack_small(flat, shapes)
    loss = un[0][0]
    g_final_g = un[1]
    per = 13
    gl = [un[2 + li * per: 2 + (li + 1) * per] for li in range(DEPTH)]
    st = lambda j: jnp.stack([gl[li][j] for li in range(DEPTH)])
    g_norm_g, g_sinks, g_ssm_d, g_b_glu, g_w_pool, g_pool_scale, g_b_ada = (st(j) for j in range(7))
    d_lr, d_li, d_bbr, d_bbi, g_c_re, g_c_im = (st(j) for j in range(7, 13))
    g_a_re, g_a_im, g_log_dt, g_b_re, g_b_im = disc_vjp((d_lr, d_li, d_bbr, d_bbi))

    dmod_all = lax.dynamic_slice(dmod_gathered.reshape(N_DEV, DEPTH, 3 * d), (0, 0, me * ada_w), (N_DEV, DEPTH, ada_w))
    dmod_all = dmod_all.transpose(1, 0, 2)
    g_w_ada = jnp.stack([_mm_tn(c_act, dmod_all[li], tm=d, tn=ada_w, tk=N_DEV, name=f"dw_ada_l{li}") for li in range(DEPTH)])

    adam("w_ada", w_ada, g_w_ada[None], m_w_ada, v_w_ada)
    adam("w_in", w_in, big[0], m_w_in, v_w_in)
    adam("w_glu", w_glu, big[1], m_w_glu, v_w_glu)
    adam("w_br_att", w_br_att, big[2], m_w_br_att, v_w_br_att)
    adam("w_br_ssm", w_br_ssm, big[3], m_w_br_ssm, v_w_br_ssm)
    adam("w_br_pool", w_br_pool, big[4], m_w_br_pool, v_w_br_pool)
    adam("w_out", w_out, big[5], m_w_out, v_w_out)

    small_names = ["norm_g", "b_ada", "attn_sinks", "ssm_a_re", "ssm_a_im", "ssm_log_dt", "ssm_b_re", "ssm_b_im",
                   "ssm_c_re", "ssm_c_im", "ssm_d", "b_glu", "w_pool", "pool_scale", "final_g"]
    small_w = [norm_g, b_ada, attn_sinks, ssm_a_re, ssm_a_im, ssm_log_dt, ssm_b_re, ssm_b_im, ssm_c_re, ssm_c_im,
               ssm_d, b_glu, w_pool, pool_scale, final_g]
    small_m = [m_norm_g, m_b_ada, m_attn_sinks, m_ssm_a_re, m_ssm_a_im, m_ssm_log_dt, m_ssm_b_re, m_ssm_b_im,
               m_ssm_c_re, m_ssm_c_im, m_ssm_d, m_b_glu, m_w_pool, m_pool_scale, m_final_g]
    small_v = [v_norm_g, v_b_ada, v_attn_sinks, v_ssm_a_re, v_ssm_a_im, v_ssm_log_dt, v_ssm_b_re, v_ssm_b_im,
               v_ssm_c_re, v_ssm_c_im, v_ssm_d, v_b_glu, v_w_pool, v_pool_scale, v_final_g]
    small_g = [g_norm_g, g_b_ada, g_sinks, g_a_re, g_a_im, g_log_dt, g_b_re, g_b_im, g_c_re, g_c_im,
               g_ssm_d, g_b_glu, g_w_pool, g_pool_scale, g_final_g]

    for nm, w, g, m, v in zip(small_names, small_w, small_g, small_m, small_v):
        adam(nm, w, g[None], m, v)

    order = ["norm_g", "w_ada", "b_ada", "w_in", "attn_sinks", "ssm_a_re", "ssm_a_im", "ssm_log_dt", "ssm_b_re",
             "ssm_b_im", "ssm_c_re", "ssm_c_im", "ssm_d", "w_glu", "b_glu", "w_pool", "pool_scale", "w_br_att",
             "w_br_ssm", "w_br_pool", "w_out", "final_g"]
    return (loss, grad_x, *[out[k][0] for k in order], *[out[k][1] for k in order],
            *[out[k][2] for k in order], *[out[k][3] for k in order])
```

```python
import functools
import math

import jax
import jax.numpy as jnp
from jax import lax
from jax.experimental import pallas as pl
from jax.experimental.pallas import tpu as pltpu

F32 = jnp.float32
BF16 = jnp.bfloat16

N_DEV = 8
D_MODEL = 1024
DEPTH = 2
CHUNK = 64
N_HEADS = 8
N_KV_HEADS = 2
HEAD_DIM = 64
Q_PER_KV = N_HEADS // N_KV_HEADS
WINDOW = 128
ATT_W = 512
KV_W = 128
SSM_W = 512
SSM_GROUP = 16
SSM_GROUPS = 32
SSM_STATE = 64
SSM_N = SSM_GROUPS * SSM_STATE
POOL_W = 512
POOL_WINDOWS = (2, 4, 8, 16)
POOL_GW = 128
POOL_HALO = 16
EPS = 1e-6
NEG_INF = -1e30
ADAM_LR = 0.001
ADAM_B1 = 0.9
ADAM_B2 = 0.999
ADAM_EPS = 1e-08
ADAM_WD = 0.01
ADAM_STEP = 10

SEQ_BLOCK = 256
ATT_BLOCK = 128
VMEM_LIMIT = 56 * 1024 * 1024

NN = (((1,), (0,)), ((), ()))
NT = (((1,), (1,)), ((), ()))
TN = (((0,), (0,)), ((), ()))


def _dot(a, b, dims=NN):
    return lax.dot_general(a.astype(BF16), b.astype(BF16), dims, preferred_element_type=F32)


def _params(*sem):
    return pltpu.CompilerParams(dimension_semantics=sem, vmem_limit_bytes=VMEM_LIMIT)


def _sigmoid(x):
    return 0.5 + 0.5 * jnp.tanh(0.5 * x)


def _silu_and_grad(z):
    s = _sigmoid(z)
    return z * s, s * (1.0 + z * (1.0 - s))


_GELU_K = math.sqrt(2.0 / math.pi)


def _gelu_and_grad(x):
    inner = _GELU_K * (x + 0.044715 * x * x * x)
    t = jnp.tanh(inner)
    val = 0.5 * x * (1.0 + t)
    grad = 0.5 * (1.0 + t) + 0.5 * x * (1.0 - t * t) * _GELU_K * (1.0 + 3.0 * 0.044715 * x * x)
    return val, grad


def _mm(a, b, *, nt=False, out_dtype=F32, tm=1024, tn=1024, name, comm=None):
    m, k = a.shape
    n = b.shape[0] if nt else b.shape[1]
    tm, tn = min(tm, m), min(tn, n)
    assert m % tm == 0 and n % tn == 0
    dims = NT if nt else NN
    grid = (m // tm, n // tn)
    c_args, c_in, c_out, c_shape, c_scratch = _comm_extra(comm)

    def body(a_ref, b_ref, o_ref):
        o_ref[...] = _dot(a_ref[...], b_ref[...], dims).astype(out_dtype)

    b_spec = pl.BlockSpec((tn, k), lambda i, j: (j, 0)) if nt else pl.BlockSpec((k, tn), lambda i, j: (0, j))
    res = pl.pallas_call(
        _with_comm(body, comm, 2, 1, grid, -1), grid=grid,
        in_specs=[pl.BlockSpec((tm, k), lambda i, j: (i, 0)), b_spec] + c_in,
        out_specs=[pl.BlockSpec((tm, tn), lambda i, j: (i, j))] + c_out,
        out_shape=[jax.ShapeDtypeStruct((m, n), out_dtype)] + c_shape,
        scratch_shapes=c_scratch,
        compiler_params=_params(*(("arbitrary",) * 2 if comm else ("parallel",) * 2)), name=name)(a, b, *c_args)
    return (res[0], list(res[1:])) if comm else res[0]


def _mm_nt_sum(pairs, *, out_dtype=F32, tm=512, tn=512, name, comm=None):
    m = pairs[0][0].shape[0]
    n = pairs[0][1].shape[0]
    np_ = len(pairs)
    grid = (m // tm, n // tn)
    c_args, c_in, c_out, c_shape, c_scratch = _comm_extra(comm)

    def body(*refs):
        o_ref = refs[-1]
        acc = _dot(refs[0][...], refs[1][...], NT)
        for p in range(1, np_):
            acc = acc + _dot(refs[2 * p][...], refs[2 * p + 1][...], NT)
        o_ref[...] = acc.astype(out_dtype)

    in_specs, args = [], []
    for a, b in pairs:
        in_specs.append(pl.BlockSpec((tm, a.shape[1]), lambda i, j: (i, 0)))
        in_specs.append(pl.BlockSpec((tn, b.shape[1]), lambda i, j: (j, 0)))
        args += [a, b]
    res = pl.pallas_call(
        _with_comm(body, comm, 2 * np_, 1, grid, -1), grid=grid, in_specs=in_specs + c_in,
        out_specs=[pl.BlockSpec((tm, tn), lambda i, j: (i, j))] + c_out,
        out_shape=[jax.ShapeDtypeStruct((m, n), out_dtype)] + c_shape,
        scratch_shapes=c_scratch,
        compiler_params=_params(*(("arbitrary",) * 2 if comm else ("parallel",) * 2)), name=name)(*args, *c_args)
    return (res[0], list(res[1:])) if comm else res[0]


def _mm_tn(a, b, *, out_dtype=F32, tm=1024, tn=1024, tk=1024, name, comm=None):
    k, m = a.shape
    n = b.shape[1]
    assert m % min(tm, m) == 0 and n % min(tn, n) == 0 and k % min(tk, k) == 0
    tm, tn, tk = min(tm, m), min(tn, n), min(tk, k)
    nk = k // tk
    grid = (m // tm, n // tn, nk)
    c_args, c_in, c_out, c_shape, c_scratch = _comm_extra(comm)

    def body(a_ref, b_ref, o_ref, acc_ref):
        kk = pl.program_id(2)

        @pl.when(kk == 0)
        def _():
            acc_ref[...] = jnp.zeros_like(acc_ref)

        acc_ref[...] += _dot(a_ref[...], b_ref[...], TN)

        @pl.when(kk == nk - 1)
        def _():
            o_ref[...] = acc_ref[...].astype(out_dtype)

    res = pl.pallas_call(
        _with_comm(body, comm, 2, 1, grid, -1), grid=grid,
        in_specs=[pl.BlockSpec((tk, tm), lambda i, j, kk: (kk, i)), pl.BlockSpec((tk, tn), lambda i, j, kk: (kk, j))] + c_in,
        out_specs=[pl.BlockSpec((tm, tn), lambda i, j, kk: (i, j))] + c_out,
        out_shape=[jax.ShapeDtypeStruct((m, n), out_dtype)] + c_shape,
        scratch_shapes=[pltpu.VMEM((tm, tn), F32)] + c_scratch,
        compiler_params=_params(*(("arbitrary",) * 3 if comm else ("parallel", "parallel", "arbitrary"))),
        name=name)(a, b, *c_args)
    return (res[0], list(res[1:])) if comm else res[0]


def _ln_fwd(x, g, shift, scale, *, name, tm=512):
    l, d = x.shape

    def body(x_ref, g_ref, sh_ref, sc_ref, h_ref):
        xv = x_ref[...]
        n = xv * lax.rsqrt(jnp.mean(xv * xv, axis=-1, keepdims=True) + EPS)
        h_ref[...] = ((n * g_ref[...]) * (1.0 + sc_ref[...]) + sh_ref[...]).astype(BF16)

    vec = pl.BlockSpec((1, d), lambda i: (0, 0))
    return pl.pallas_call(
        body, grid=(l // tm,),
        in_specs=[pl.BlockSpec((tm, d), lambda i: (i, 0)), vec, vec, vec],
        out_specs=pl.BlockSpec((tm, d), lambda i: (i, 0)),
        out_shape=jax.ShapeDtypeStruct((l, d), BF16),
        compiler_params=_params("parallel"), name=name)(x, g, shift, scale)


def _ln_bwd(x, dh, dres, g, scale, *, name, tm=512):
    l, d = x.shape

    def body(x_ref, dh_ref, dres_ref, g_ref, sc_ref, dx_ref, sums_ref):
        xv = x_ref[...]
        dhv = dh_ref[...]
        rstd = lax.rsqrt(jnp.mean(xv * xv, axis=-1, keepdims=True) + EPS)
        n = xv * rstd
        gv = g_ref[...]
        dr = dhv * (1.0 + sc_ref[...])
        dn = dr * gv
        dx_ref[...] = dres_ref[...] + rstd * (dn - n * jnp.mean(dn * n, axis=-1, keepdims=True))

        @pl.when(pl.program_id(0) == 0)
        def _():
            sums_ref[...] = jnp.zeros_like(sums_ref)

        sums_ref[0:1, :] += jnp.sum(dhv, axis=0, keepdims=True)
        sums_ref[1:2, :] += jnp.sum(dhv * (n * gv), axis=0, keepdims=True)
        sums_ref[2:3, :] += jnp.sum(dr * n, axis=0, keepdims=True)

    vec = pl.BlockSpec((1, d), lambda i: (0, 0))
    row = pl.BlockSpec((tm, d), lambda i: (i, 0))
    return pl.pallas_call(
        body, grid=(l // tm,),
        in_specs=[row, row, row, vec, vec],
        out_specs=[row, pl.BlockSpec((8, d), lambda i: (0, 0))],
        out_shape=[jax.ShapeDtypeStruct((l, d), F32), jax.ShapeDtypeStruct((8, d), F32)],
        compiler_params=_params("arbitrary"), name=name)(x, dh, dres, g, scale)


def _final_loss(x, g, target, *, tm=512):
    l, d = x.shape

    def body(x_ref, g_ref, t_ref, dx_ref, sums_ref):
        xv = x_ref[...]
        rstd = lax.rsqrt(jnp.mean(xv * xv, axis=-1, keepdims=True) + EPS)
        n = xv * rstd
        gv = g_ref[...]
        err = n * gv - t_ref[...]
        dy = err * (1.0 / d)
        dn = dy * gv
        dx_ref[...] = rstd * (dn - n * jnp.mean(dn * n, axis=-1, keepdims=True))

        @pl.when(pl.program_id(0) == 0)
        def _():
            sums_ref[...] = jnp.zeros_like(sums_ref)

        sums_ref[0:1, :] += jnp.sum(dy * n, axis=0, keepdims=True)
        sums_ref[1:2, :] += jnp.sum(err * err, axis=0, keepdims=True) * (0.5 / d)

    vec = pl.BlockSpec((1, d), lambda i: (0, 0))
    row = pl.BlockSpec((tm, d), lambda i: (i, 0))
    dx, sums = pl.pallas_call(
        body, grid=(l // tm,),
        in_specs=[row, vec, row],
        out_specs=[row, pl.BlockSpec((8, d), lambda i: (0, 0))],
        out_shape=[jax.ShapeDtypeStruct((l, d), F32), jax.ShapeDtypeStruct((8, d), F32)],
        compiler_params=_params("arbitrary"), name="final_loss")(x, g, target)
    return dx, sums


def _attn_geometry(i, t):
    nk = t + WINDOW
    qi = lax.broadcasted_iota(jnp.int32, (t, nk), 0)
    kj = lax.broadcasted_iota(jnp.int32, (t, nk), 1)
    dist = jnp.abs(qi + WINDOW - kj).astype(F32)
    qc = jnp.right_shift(qi, 6)
    kc = jnp.right_shift(kj, 6)
    valid = (kc >= qc) & (kc <= qc + WINDOW // CHUNK) & ((i > 0) | (kj >= WINDOW))
    return dist, valid


def _attn_head(q, k_all, v_all, sink, slope, dist, valid):
    s = _dot(q, k_all, NT) * (1.0 / math.sqrt(HEAD_DIM)) - slope * dist
    s = jnp.where(valid, s, NEG_INF)
    m = jnp.maximum(jnp.max(s, axis=-1, keepdims=True), sink)
    e = jnp.exp(s - m)
    es = jnp.exp(sink - m)
    inv = 1.0 / (jnp.sum(e, axis=-1, keepdims=True) + es)
    p = e * inv
    o = _dot(p, v_all, NN)
    return p, o, es * inv


def _attn_specs(t):
    cur = pl.BlockSpec((t, ATT_W * 2 + KV_W * 2), lambda i: (i, 0))
    halo_blocks = t // WINDOW
    prev = pl.BlockSpec((WINDOW, 2 * KV_W), lambda i: (jnp.maximum(i * halo_blocks - 1, 0), (2 * ATT_W) // (2 * KV_W)))
    return cur, prev


def _attn_fwd(pa, sinks, *, name, t=ATT_BLOCK, comm=None):
    l = pa.shape[0]
    t = min(t, l)
    nb = l // t
    c_args, c_in, c_out, c_shape, c_scratch = _comm_extra(comm)

    def body(sink_ref, cur_ref, prev_ref, ya_ref):
        i = pl.program_id(0)
        dist, valid = _attn_geometry(i, t)
        for h in range(N_HEADS):
            kh = h // Q_PER_KV
            q = cur_ref[:, h * HEAD_DIM:(h + 1) * HEAD_DIM]
            z = cur_ref[:, ATT_W + h * HEAD_DIM:ATT_W + (h + 1) * HEAD_DIM].astype(F32)
            k_all = jnp.concatenate([prev_ref[:, kh * HEAD_DIM:(kh + 1) * HEAD_DIM],
                                     cur_ref[:, 2 * ATT_W + kh * HEAD_DIM:2 * ATT_W + (kh + 1) * HEAD_DIM]], axis=0)
            v_all = jnp.concatenate([prev_ref[:, KV_W + kh * HEAD_DIM:KV_W + (kh + 1) * HEAD_DIM],
                                     cur_ref[:, 2 * ATT_W + KV_W + kh * HEAD_DIM:2 * ATT_W + KV_W + (kh + 1) * HEAD_DIM]], axis=0)
            _, o, _ = _attn_head(q, k_all, v_all, sink_ref[h], 2.0 ** (-(h + 1)), dist, valid)
            sz, _ = _silu_and_grad(z)
            ya_ref[:, h * HEAD_DIM:(h + 1) * HEAD_DIM] = (o * sz).astype(BF16)

    cur, prev = _attn_specs(t)
    res = pl.pallas_call(
        _with_comm(body, comm, 3, 1, nb, nb - 1), grid=(nb,),
        in_specs=[pl.BlockSpec(memory_space=pltpu.SMEM), cur, prev] + c_in,
        out_specs=[pl.BlockSpec((t, ATT_W), lambda i: (i, 0))] + c_out,
        out_shape=[jax.ShapeDtypeStruct((l, ATT_W), BF16)] + c_shape,
        scratch_shapes=c_scratch,
        compiler_params=_params("arbitrary"), name=name)(sinks, pa, pa, *c_args)
    return res[0], res[1:]


def _attn_bwd(pa, sinks, dya, *, name, t=SEQ_BLOCK):
    l = pa.shape[0]
    t = min(t, l)
    nb = l // t
    scale = 1.0 / math.sqrt(HEAD_DIM)

    def body(sink_ref, cur_ref, prev_ref, dya_ref, dpa_ref, dsink_ref, carry_ref):
        n = pl.program_id(0)
        i = nb - 1 - n
        dist, valid = _attn_geometry(i, t)

        @pl.when(n == 0)
        def _():
            carry_ref[...] = jnp.zeros_like(carry_ref)
            dsink_ref[...] = jnp.zeros_like(dsink_ref)

        dk_acc = [jnp.zeros((HEAD_DIM, t + WINDOW), F32) for _ in range(N_KV_HEADS)]
        dv_acc = [jnp.zeros((HEAD_DIM, t + WINDOW), F32) for _ in range(N_KV_HEADS)]
        for h in range(N_HEADS):
            kh = h // Q_PER_KV
            q = cur_ref[:, h * HEAD_DIM:(h + 1) * HEAD_DIM]
            z = cur_ref[:, ATT_W + h * HEAD_DIM:ATT_W + (h + 1) * HEAD_DIM].astype(F32)
            k_all = jnp.concatenate([prev_ref[:, kh * HEAD_DIM:(kh + 1) * HEAD_DIM],
                                     cur_ref[:, 2 * ATT_W + kh * HEAD_DIM:2 * ATT_W + (kh + 1) * HEAD_DIM]], axis=0)
            v_all = jnp.concatenate([prev_ref[:, KV_W + kh * HEAD_DIM:KV_W + (kh + 1) * HEAD_DIM],
                                     cur_ref[:, 2 * ATT_W + KV_W + kh * HEAD_DIM:2 * ATT_W + KV_W + (kh + 1) * HEAD_DIM]], axis=0)
            p, o, p_sink = _attn_head(q, k_all, v_all, sink_ref[h], 2.0 ** (-(h + 1)), dist, valid)
            dy = dya_ref[:, h * HEAD_DIM:(h + 1) * HEAD_DIM]
            sz, dsz = _silu_and_grad(z)
            do = dy * sz
            dpa_ref[:, ATT_W + h * HEAD_DIM:ATT_W + (h + 1) * HEAD_DIM] = (dy * o * dsz).astype(BF16)
            delta = jnp.sum(do * o, axis=-1, keepdims=True)
            dp = _dot(do, v_all, NT)
            ds = p * (dp - delta)
            dpa_ref[:, h * HEAD_DIM:(h + 1) * HEAD_DIM] = (_dot(ds, k_all, NN) * scale).astype(BF16)
            dk_acc[kh] = dk_acc[kh] + _dot(q, ds, TN) * scale
            dv_acc[kh] = dv_acc[kh] + _dot(do, p, TN)
            dsink_ref[h:h + 1, :] += jnp.broadcast_to(-jnp.sum(p_sink * delta, axis=0, keepdims=True), (1, 128))

        acc = jnp.concatenate(dk_acc + dv_acc, axis=0).T
        own = acc[WINDOW:, :]
        tail = own[t - WINDOW:, :] + carry_ref[...]
        if t > WINDOW:
            dpa_ref[0:t - WINDOW, 2 * ATT_W:] = own[:t - WINDOW, :].astype(BF16)
        dpa_ref[t - WINDOW:t, 2 * ATT_W:] = tail.astype(BF16)
        carry_ref[...] = acc[:WINDOW, :]

    halo_blocks = t // WINDOW
    wpa = 2 * ATT_W + 2 * KV_W
    cur = pl.BlockSpec((t, wpa), lambda n: (nb - 1 - n, 0))
    prev = pl.BlockSpec((WINDOW, 2 * KV_W),
                        lambda n: (jnp.maximum((nb - 1 - n) * halo_blocks - 1, 0), (2 * ATT_W) // (2 * KV_W)))
    return pl.pallas_call(
        body, grid=(nb,),
        in_specs=[pl.BlockSpec(memory_space=pltpu.SMEM), cur, prev, pl.BlockSpec((t, ATT_W), lambda n: (nb - 1 - n, 0))],
        out_specs=[pl.BlockSpec((t, wpa), lambda n: (nb - 1 - n, 0)), pl.BlockSpec((8, 128), lambda n: (0, 0))],
        out_shape=[jax.ShapeDtypeStruct((l, wpa), BF16), jax.ShapeDtypeStruct((8, 128), F32)],
        scratch_shapes=[pltpu.VMEM((WINDOW, 2 * KV_W), F32)],
        compiler_params=_params("arbitrary"), name=name)(sinks, pa, pa, dya)


def _scan(xr, xi, lr, li, t, reverse):
    row = lax.broadcasted_iota(jnp.int32, (t, 1), 0)
    d = 1
    pr, pi = lr, li
    while d < t:
        if reverse:
            sr = jnp.where(row < t - d, pltpu.roll(xr, t - d, 0), 0.0)
            si = jnp.where(row < t - d, pltpu.roll(xi, t - d, 0), 0.0)
        else:
            sr = jnp.where(row >= d, pltpu.roll(xr, d, 0), 0.0)
            si = jnp.where(row >= d, pltpu.roll(xi, d, 0), 0.0)
        xr, xi = xr + pr * sr - pi * si, xi + pr * si + pi * sr
        pr, pi = pr * pr - pi * pi, 2.0 * pr * pi
        d *= 2
    return xr, xi


SCAN_SUB = 8


def _split_hi_lo(a):
    hi = a.astype(BF16)
    lo = (a - hi.astype(F32)).astype(BF16)
    return jnp.concatenate([hi, lo], axis=0)


def _scan_mxu(xr, xi, tab, lam3, lam8, tri, expand, cr, ci, t, reverse):
    ns = t // SCAN_SUB
    n = xr.shape[1]
    v3 = lambda a: a.reshape(ns, SCAN_SUB, n)
    x3r, x3i = v3(xr), v3(xi)
    br = (x3r * tab[0] - x3i * tab[1]).reshape(t, n)
    bi = (x3r * tab[1] + x3i * tab[0]).reshape(t, n)
    pm = jnp.dot(tri, jnp.concatenate([br, bi], axis=1).astype(BF16), preferred_element_type=F32)
    p3r, p3i = v3(pm[:t, :n]), v3(pm[:t, n:])
    slr = p3r * tab[2] - p3i * tab[3]
    sli = p3r * tab[3] + p3i * tab[2]
    totr, toti = pm[t:, :n], pm[t:, n:]
    l3r, l3i = lam3
    l8r, l8i = lam8
    row = lax.broadcasted_iota(jnp.int32, (ns, 1), 0)
    edge = row == (ns - 1 if reverse else 0)
    er = totr * l3r - toti * l3i + jnp.where(edge, l8r * cr - l8i * ci, 0.0)
    ei = totr * l3i + toti * l3r + jnp.where(edge, l8r * ci + l8i * cr, 0.0)
    er, ei = _scan(er, ei, l8r, l8i, ns, reverse)
    shift = ns - 1 if reverse else 1
    nbr = jnp.where(edge, cr, pltpu.roll(er, shift, 0))
    nbi = jnp.where(edge, ci, pltpu.roll(ei, shift, 0))
    ex = jnp.dot(expand, _split_hi_lo(jnp.concatenate([nbr, nbi], axis=1)), preferred_element_type=F32)
    e3r, e3i = v3(ex[:, :n]), v3(ex[:, n:])
    sr = (slr + e3r * tab[4] - e3i * tab[5]).reshape(t, n)
    si = (sli + e3r * tab[5] + e3i * tab[4]).reshape(t, n)
    out = 0 if reverse else ns - 1
    return sr, si, er[out:out + 1, :], ei[out:out + 1, :]


def _scan_consts(t):
    import numpy as np
    ns = t // SCAN_SUB
    r = np.arange(t)
    same = (r[:, None] // SCAN_SUB) == (r[None, :] // SCAN_SUB)
    sums = (np.arange(ns)[:, None] == (r[None, :] // SCAN_SUB))
    tri = []
    for keep in (r[None, :] <= r[:, None], r[None, :] >= r[:, None]):
        tri.append(np.concatenate([same & keep, sums], axis=0).astype(np.float32))
    ex = ((r[:, None] // SCAN_SUB) == np.arange(ns)[None, :]).astype(np.float32)
    return jnp.asarray(np.stack(tri), BF16), jnp.asarray(np.concatenate([ex, ex], axis=1), BF16)


def _scan_tables(lr, li):
    import numpy as np
    den = lr * lr + li * li
    ir, ii = lr / den, -li / den
    mul = lambda a, b: (a[0] * b[0] - a[1] * b[1], a[0] * b[1] + a[1] * b[0])
    pw = {0: (jnp.ones_like(lr), jnp.zeros_like(lr))}
    for e in range(1, 9):
        pw[e] = mul(pw[e - 1], (lr, li))
    for e in range(-1, -5, -1):
        pw[e] = mul(pw[e + 1], (ir, ii))
    powers = jnp.stack([jnp.stack(pw[e]) for e in range(-4, 9)] + [jnp.zeros((2, lr.shape[0]), F32)])
    j = np.arange(SCAN_SUB)
    exps = [4 - j, j - 4, j + 1, j - 3, 3 - j, 8 - j]
    e_idx = np.stack([exps[t] + 4 for t in range(6) for _ in range(2)])
    c_idx = np.tile(np.array([0, 1])[:, None], (6, SCAN_SUB))
    sign = np.where((c_idx == 1) & (np.arange(12)[:, None] >= 6), -1.0, 1.0).astype(np.float32)
    tabs = powers[e_idx, c_idx] * sign[:, :, None]
    lam = powers[np.array([5, 5, 7, 7, 12, 12, 13, 13]), np.array([0, 1, 0, 1, 0, 1, 0, 0])]
    return lam, tabs


SSM_HALVES = 2
SSM_HW = SSM_W // SSM_HALVES
SSM_HN = SSM_N // SSM_HALVES


def _bd_nn(x, w):
    a = w.shape[1]
    return jnp.concatenate([_dot(x[:, h * a:(h + 1) * a], w[h]) for h in range(SSM_HALVES)], axis=1)


def _bd_nt(x, w):
    b = w.shape[2]
    return jnp.concatenate([_dot(x[:, h * b:(h + 1) * b], w[h], NT) for h in range(SSM_HALVES)], axis=1)


def _bd_tn(x, y):
    a, b = x.shape[1] // SSM_HALVES, y.shape[1] // SSM_HALVES
    return jnp.stack([_dot(x[:, h * a:(h + 1) * a], y[:, h * b:(h + 1) * b], TN) for h in range(SSM_HALVES)])


def _ssm_states(u, s0r, s0i, lam_ref, tab_ref, tri_ref, ex_ref, bre, bim, t):
    tab = tuple(tab_ref[k] for k in range(6))
    return _scan_mxu(_bd_nn(u, bre), _bd_nn(u, bim), tab, (lam_ref[2:3, :], lam_ref[3:4, :]),
                     (lam_ref[4:5, :], lam_ref[5:6, :]), tri_ref[0], ex_ref[...], s0r, s0i, t, False)


def _ssm_head(u, z, xr, xi, cre, cim, dskip, wglu, bglu):
    y = _bd_nn(xr, cre) - _bd_nn(xi, cim) + dskip * u
    y2, dgelu = _gelu_and_grad(y)
    gate = _sigmoid(_dot(y2, wglu) + bglu)
    y3 = y2 * gate
    return y2, dgelu, gate, y3


def _with_comm(body, comm, n_in, n_out, grid, mid_step):
    if comm is None:
        return body
    nc = len(comm["args"])
    n_sem = len(comm["scratch"])
    grid = (grid,) if isinstance(grid, int) else tuple(grid)
    total = math.prod(grid)

    def hosted(*refs):
        ins, cin = refs[:n_in], refs[n_in:n_in + nc]
        outs, cout = refs[n_in + nc:n_in + nc + n_out], refs[n_in + nc + n_out:n_in + 2 * nc + n_out]
        rest = refs[n_in + 2 * nc + n_out:]
        scratch, csem = rest[:len(rest) - n_sem], rest[len(rest) - n_sem:]
        start, forward, finish = comm["phases"](cin, cout, *csem)
        step = pl.program_id(0)
        for axis in range(1, len(grid)):
            step = step * grid[axis] + pl.program_id(axis)
        pl.when(step == 0)(start)
        pl.when(step == (mid_step if mid_step >= 0 else total + mid_step))(forward)
        body(*ins, *outs, *scratch)
        pl.when(step == total - 1)(finish)

    return hosted


def _comm_extra(comm):
    if comm is None:
        return [], [], [], [], []
    anyspec = pl.BlockSpec(memory_space=pl.ANY)
    nc = len(comm["args"])
    return comm["args"], [anyspec] * nc, [anyspec] * nc, comm["out_shape"], comm["scratch"]


def _ssm_fwd(ps, scan_ops, bblk, cblk, dskip, wglu, bglu, *, name, t=SEQ_BLOCK, comm=None):
    l = ps.shape[0]
    assert l % t == 0
    nb = l // t
    ns = t // SCAN_SUB
    c_args, c_in, c_out, c_shape, c_scratch = _comm_extra(comm)

    def body(ps_ref, lam_ref, tab_ref, tri_ref, ex_ref, b_ref, c_ref, d_ref, w_ref, bg_ref, ys_ref, chk_ref, xs_ref,
             st_ref):
        @pl.when(pl.program_id(0) == 0)
        def _():
            st_ref[...] = jnp.zeros_like(st_ref)

        chk_ref[...] = jnp.broadcast_to(st_ref[...], chk_ref.shape)
        u = ps_ref[:, :SSM_W].astype(F32)
        z = ps_ref[:, SSM_W:].astype(F32)
        xr, xi, er, ei = _ssm_states(u, st_ref[:, :SSM_N], st_ref[:, SSM_N:], lam_ref, tab_ref, tri_ref, ex_ref,
                                     b_ref[0], b_ref[1], t)
        st_ref[:, :SSM_N] = er
        st_ref[:, SSM_N:] = ei
        xr, xi = xr.astype(BF16), xi.astype(BF16)
        xs_ref[:, :SSM_N] = xr
        xs_ref[:, SSM_N:] = xi
        _, _, _, y3 = _ssm_head(u, z, xr, xi, c_ref[0], c_ref[1], d_ref[...], w_ref[...], bg_ref[...])
        sz, _ = _silu_and_grad(z)
        ys_ref[...] = (y3 * sz).astype(BF16)

    full = lambda shape: pl.BlockSpec(shape, lambda i: (0,) * len(shape))
    return pl.pallas_call(
        _with_comm(body, comm, 10, 3, nb, nb - 1), grid=(nb,),
        in_specs=[pl.BlockSpec((t, 2 * SSM_W), lambda i: (i, 0)), full((8, SSM_N)), full((12, SCAN_SUB, SSM_N)),
                  full((2, t + ns, t)), full((t, 2 * ns)), full((2, SSM_HALVES, SSM_HW, SSM_HN)),
                  full((2, SSM_HALVES, SSM_HN, SSM_HW)), full((1, SSM_W)), full((SSM_W, SSM_W)), full((1, SSM_W))] + c_in,
        out_specs=[pl.BlockSpec((t, SSM_W), lambda i: (i, 0)), pl.BlockSpec((8, 2 * SSM_N), lambda i: (i, 0)),
                   pl.BlockSpec((t, 2 * SSM_N), lambda i: (i, 0))] + c_out,
        out_shape=[jax.ShapeDtypeStruct((l, SSM_W), BF16), jax.ShapeDtypeStruct((nb * 8, 2 * SSM_N), F32),
                   jax.ShapeDtypeStruct((l, 2 * SSM_N), BF16)] + c_shape,
        scratch_shapes=[pltpu.VMEM((1, 2 * SSM_N), F32)] + c_scratch,
        compiler_params=_params("arbitrary"), name=name)(ps, *scan_ops, bblk, cblk, dskip, wglu, bglu, *c_args)


def _ssm_bwd(ps, dys, chk, states, scan_ops, bblk, cblk, dskip, wglu, bglu, *, name, t=SEQ_BLOCK, comm=None):
    l = ps.shape[0]
    assert l % t == 0
    nb = l // t
    ns = t // SCAN_SUB
    c_args, c_in, c_out, c_shape, c_scratch = _comm_extra(comm)

    def body(ps_ref, dys_ref, chk_ref, xs_ref, lam_ref, tab_ref, tri_ref, ex_ref, b_ref, c_ref, d_ref, w_ref, bg_ref,
             dps_ref, db_ref, dc_ref, dw_acc, sums_acc, gc_ref, db_acc, dc_acc):
        n = pl.program_id(0)

        @pl.when(n == 0)
        def _():
            gc_ref[...] = jnp.zeros_like(gc_ref)
            db_acc[...] = jnp.zeros_like(db_acc)
            dc_acc[...] = jnp.zeros_like(dc_acc)
            dw_acc[...] = jnp.zeros_like(dw_acc)
            sums_acc[...] = jnp.zeros_like(sums_acc)

        row = lax.broadcasted_iota(jnp.int32, (t, 1), 0)
        u = ps_ref[:, :SSM_W].astype(F32)
        z = ps_ref[:, SSM_W:].astype(F32)
        s0r, s0i = chk_ref[0:1, :SSM_N], chk_ref[0:1, SSM_N:]
        xr, xi = xs_ref[:, :SSM_N], xs_ref[:, SSM_N:]
        dskip = d_ref[...]
        y2, dgelu, gate, y3 = _ssm_head(u, z, xr, xi, c_ref[0], c_ref[1], dskip, w_ref[...], bg_ref[...])
        sz, dsz = _silu_and_grad(z)
        dys_v = dys_ref[...]
        dps_ref[:, SSM_W:] = (dys_v * y3 * dsz).astype(BF16)
        dy3 = dys_v * sz
        da = dy3 * y2 * gate * (1.0 - gate)
        dy2 = dy3 * gate + _dot(da, w_ref[...], NT)
        dw_acc[...] += _dot(y2, da, TN)
        dy = dy2 * dgelu
        sums_acc[2:3, :SSM_W] += jnp.sum(dy * u, axis=0, keepdims=True)
        sums_acc[3:4, :SSM_W] += jnp.sum(da, axis=0, keepdims=True)
        dc_acc[0] += _bd_tn(dy, xr)
        dc_acc[1] += -_bd_tn(dy, xi)
        rev_tab = tuple(tab_ref[k] for k in range(6, 12))
        gr, gi, gcr, gci = _scan_mxu(
            _bd_nt(dy, c_ref[0]), -_bd_nt(dy, c_ref[1]), rev_tab, (lam_ref[2:3, :], -lam_ref[3:4, :]),
            (lam_ref[4:5, :], -lam_ref[5:6, :]), tri_ref[1], ex_ref[...], gc_ref[:, :SSM_N], gc_ref[:, SSM_N:], t, True)
        gc_ref[:, :SSM_N] = gcr
        gc_ref[:, SSM_N:] = gci
        db_acc[0] += _bd_tn(u, gr)
        db_acc[1] += _bd_tn(u, gi)
        du = dskip * dy + _bd_nt(gr, b_ref[0]) + _bd_nt(gi, b_ref[1])
        dps_ref[:, :SSM_W] = du.astype(BF16)
        spr = jnp.where(row == 0, s0r, pltpu.roll(xr.astype(F32), 1, 0))
        spi = jnp.where(row == 0, s0i, pltpu.roll(xi.astype(F32), 1, 0))
        sums_acc[0:1, :] += jnp.sum(gr * spr + gi * spi, axis=0, keepdims=True)
        sums_acc[1:2, :] += jnp.sum(gi * spr - gr * spi, axis=0, keepdims=True)

        @pl.when(n == nb - 1)
        def _():
            per_half = SSM_GROUPS // SSM_HALVES
            for k in range(2):
                for g in range(SSM_GROUPS):
                    h, gl = divmod(g, per_half)
                    c0, p0 = gl * SSM_GROUP, gl * SSM_STATE
                    db_ref[k, g * SSM_GROUP:(g + 1) * SSM_GROUP, :] = db_acc[k, h, c0:c0 + SSM_GROUP, p0:p0 + SSM_STATE]
                    dc_ref[k, g * SSM_GROUP:(g + 1) * SSM_GROUP, :] = dc_acc[k, h, c0:c0 + SSM_GROUP, p0:p0 + SSM_STATE]

    full = lambda shape: pl.BlockSpec(shape, lambda n: (0,) * len(shape))
    return pl.pallas_call(
        _with_comm(body, comm, 13, 5, nb, 0), grid=(nb,),
        in_specs=[pl.BlockSpec((t, 2 * SSM_W), lambda n: (nb - 1 - n, 0)),
                  pl.BlockSpec((t, SSM_W), lambda n: (nb - 1 - n, 0)),
                  pl.BlockSpec((8, 2 * SSM_N), lambda n: (nb - 1 - n, 0)),
                  pl.BlockSpec((t, 2 * SSM_N), lambda n: (nb - 1 - n, 0)),
                  full((8, SSM_N)), full((12, SCAN_SUB, SSM_N)), full((2, t + ns, t)), full((t, 2 * ns)),
                  full((2, SSM_HALVES, SSM_HW, SSM_HN)), full((2, SSM_HALVES, SSM_HN, SSM_HW)), full((1, SSM_W)),
                  full((SSM_W, SSM_W)), full((1, SSM_W))] + c_in,
        out_specs=[pl.BlockSpec((t, 2 * SSM_W), lambda n: (nb - 1 - n, 0)), full((2, SSM_W, SSM_STATE)),
                   full((2, SSM_W, SSM_STATE)), full((SSM_W, SSM_W)), full((8, SSM_N))] + c_out,
        out_shape=[jax.ShapeDtypeStruct((l, 2 * SSM_W), BF16),
                   jax.ShapeDtypeStruct((2, SSM_W, SSM_STATE), F32),
                   jax.ShapeDtypeStruct((2, SSM_W, SSM_STATE), F32),
                   jax.ShapeDtypeStruct((SSM_W, SSM_W), F32),
                   jax.ShapeDtypeStruct((8, SSM_N), F32)] + c_shape,
        scratch_shapes=[pltpu.VMEM((1, 2 * SSM_N), F32), pltpu.VMEM((2, SSM_HALVES, SSM_HW, SSM_HN), F32),
                        pltpu.VMEM((2, SSM_HALVES, SSM_HW, SSM_HN), F32)] + c_scratch,
        compiler_params=_params("arbitrary"), name=name)(ps, dys, chk, states, *scan_ops, bblk, cblk, dskip, wglu, bglu,
                                                         *c_args)


def _pool_count(i, t):
    pos = lax.broadcasted_iota(jnp.int32, (t, POOL_W), 0) + i * t + 1
    col = lax.broadcasted_iota(jnp.int32, (t, POOL_W), 1)
    win = jnp.where(col < POOL_GW, 2, jnp.where(col < 2 * POOL_GW, 4, jnp.where(col < 3 * POOL_GW, 8, 16)))
    return 1.0 / jnp.minimum(pos, win).astype(F32), col


def _window_sums(ext, n_rows, forward):
    col = lax.broadcasted_iota(jnp.int32, ext.shape, 1)
    sh = (lambda a, d: pltpu.roll(a, d, 0)) if forward else (lambda a, d: pltpu.roll(a, n_rows - d, 0))
    a2 = ext + sh(ext, 1)
    a4 = a2 + sh(a2, 2)
    a8 = a4 + sh(a4, 4)
    a16 = a8 + sh(a8, 8)
    return jnp.where(col < POOL_GW, a2, jnp.where(col < 2 * POOL_GW, a4, jnp.where(col < 3 * POOL_GW, a8, a16)))


def _pool_mix(pooled, wp_ref):
    return jnp.concatenate([_dot(pooled[:, g * POOL_GW:(g + 1) * POOL_GW], wp_ref[g]) for g in range(4)], axis=1)


def _pool_pooled(i, cur_u, prev_u, t):
    prev = jnp.where(i > 0, prev_u, 0.0)
    ext = jnp.concatenate([prev, cur_u], axis=0)
    inv_cnt, _ = _pool_count(i, t)
    return _window_sums(ext, t + POOL_HALO, True)[POOL_HALO:, :] * inv_cnt - cur_u


def _pool_fwd(pp, wpool, pscale, *, name, t=SEQ_BLOCK):
    l = pp.shape[0]
    t = min(t, l)

    def body(cur_ref, prev_ref, wp_ref, sc_ref, yp_ref):
        i = pl.program_id(0)
        pooled = _pool_pooled(i, cur_ref[:, :POOL_W].astype(F32), prev_ref[...].astype(F32), t)
        lin = _pool_mix(pooled, wp_ref)
        sz, _ = _silu_and_grad(cur_ref[:, POOL_W:].astype(F32))
        yp_ref[...] = (lin * sc_ref[...] * sz).astype(BF16)

    hb = t // POOL_HALO
    return pl.pallas_call(
        body, grid=(l // t,),
        in_specs=[pl.BlockSpec((t, 2 * POOL_W), lambda i: (i, 0)),
                  pl.BlockSpec((POOL_HALO, POOL_W), lambda i: (jnp.maximum(i * hb - 1, 0), 0)),
                  pl.BlockSpec((4, POOL_GW, POOL_GW), lambda i: (0, 0, 0)),
                  pl.BlockSpec((1, POOL_W), lambda i: (0, 0))],
        out_specs=pl.BlockSpec((t, POOL_W), lambda i: (i, 0)),
        out_shape=jax.ShapeDtypeStruct((l, POOL_W), BF16),
        compiler_params=_params("parallel"), name=name)(pp, pp, wpool, pscale)


def _pool_bwd(pp, dyp, wpool, pscale, *, name, t=SEQ_BLOCK):
    l = pp.shape[0]
    t = min(t, l)
    nb = l // t

    def body(cur_ref, prev_ref, dyp_ref, wp_ref, sc_ref, dpp_ref, dwp_ref, sums_ref, carry_ref):
        n = pl.program_id(0)
        i = nb - 1 - n

        @pl.when(n == 0)
        def _():
            carry_ref[...] = jnp.zeros_like(carry_ref)
            dwp_ref[...] = jnp.zeros_like(dwp_ref)
            sums_ref[...] = jnp.zeros_like(sums_ref)

        cur_u = cur_ref[:, :POOL_W].astype(F32)
        pooled = _pool_pooled(i, cur_u, prev_ref[...].astype(F32), t)
        lin = _pool_mix(pooled, wp_ref)
        sz, dsz = _silu_and_grad(cur_ref[:, POOL_W:].astype(F32))
        dyp_v = dyp_ref[...]
        scale = sc_ref[...]
        dpp_ref[:, POOL_W:] = (dyp_v * lin * scale * dsz).astype(BF16)
        dpre = dyp_v * sz
        sums_ref[0:1, :] += jnp.sum(dpre * lin, axis=0, keepdims=True)
        dlin = dpre * scale
        dpooled = []
        for g in range(4):
            dl = dlin[:, g * POOL_GW:(g + 1) * POOL_GW]
            dwp_ref[g] += _dot(pooled[:, g * POOL_GW:(g + 1) * POOL_GW], dl, TN)
            dpooled.append(_dot(dl, wp_ref[g], NT))
        dpooled = jnp.concatenate(dpooled, axis=1)
        inv_cnt, _ = _pool_count(i, t)
        dq = dpooled * inv_cnt
        ext = jnp.concatenate([dq, carry_ref[...]], axis=0)
        du = _window_sums(ext, t + POOL_HALO, False)[:t, :] - dpooled
        dpp_ref[:, :POOL_W] = du.astype(BF16)
        carry_ref[...] = dq[:POOL_HALO, :]

    hb = t // POOL_HALO
    return pl.pallas_call(
        body, grid=(nb,),
        in_specs=[pl.BlockSpec((t, 2 * POOL_W), lambda n: (nb - 1 - n, 0)),
                  pl.BlockSpec((POOL_HALO, POOL_W), lambda n: (jnp.maximum((nb - 1 - n) * hb - 1, 0), 0)),
                  pl.BlockSpec((t, POOL_W), lambda n: (nb - 1 - n, 0)),
                  pl.BlockSpec((4, POOL_GW, POOL_GW), lambda n: (0, 0, 0)),
                  pl.BlockSpec((1, POOL_W), lambda n: (0, 0))],
        out_specs=[pl.BlockSpec((t, 2 * POOL_W), lambda n: (nb - 1 - n, 0)),
                   pl.BlockSpec((4, POOL_GW, POOL_GW), lambda n: (0, 0, 0)),
                   pl.BlockSpec((8, POOL_W), lambda n: (0, 0))],
        out_shape=[jax.ShapeDtypeStruct((l, 2 * POOL_W), BF16), jax.ShapeDtypeStruct((4, POOL_GW, POOL_GW), F32),
                   jax.ShapeDtypeStruct((8, POOL_W), F32)],
        scratch_shapes=[pltpu.VMEM((POOL_HALO, POOL_W), F32)],
        compiler_params=_params("arbitrary"), name=name)(pp, pp, dyp, wpool, pscale)


def _merge_fwd(ya, ys, yp, wa, ws, wp, pg, wout, x, gate, *, name, tm=512):
    l, d = x.shape
    tm = min(tm, l)

    def body(ya_ref, ys_ref, yp_ref, wa_ref, ws_ref, wp_ref, pg_ref, wo_ref, x_ref, g_ref,
             xn_ref, mg_ref, ba_ref, bs_ref, bp_ref, out_ref):
        acc = None
        for k, (y_ref, w_ref, b_ref) in enumerate(((ya_ref, wa_ref, ba_ref), (ys_ref, ws_ref, bs_ref),
                                                   (yp_ref, wp_ref, bp_ref))):
            br = _dot(y_ref[...], w_ref[...])
            b_ref[...] = br.astype(BF16)
            term = _sigmoid(pg_ref[:, k * d:(k + 1) * d].astype(F32)) * br
            acc = term if acc is None else acc + term
        merged = acc.astype(BF16)
        mg_ref[...] = merged
        out = _dot(merged, wo_ref[...])
        out_ref[...] = out.astype(BF16)
        xn_ref[...] = x_ref[...] + g_ref[...] * out

    rowy = pl.BlockSpec((tm, ATT_W), lambda i: (i, 0))
    wsp = pl.BlockSpec((ATT_W, d), lambda i: (0, 0))
    rowd = pl.BlockSpec((tm, d), lambda i: (i, 0))
    return pl.pallas_call(
        body, grid=(l // tm,),
        in_specs=[rowy, rowy, rowy, wsp, wsp, wsp, pl.BlockSpec((tm, 3 * d), lambda i: (i, 0)),
                  pl.BlockSpec((d, d), lambda i: (0, 0)), rowd, pl.BlockSpec((1, d), lambda i: (0, 0))],
        out_specs=[rowd] * 6,
        out_shape=[jax.ShapeDtypeStruct((l, d), F32)] + [jax.ShapeDtypeStruct((l, d), BF16)] * 5,
        compiler_params=_params("parallel"), name=name)(ya, ys, yp, wa, ws, wp, pg, wout, x, gate)


def _merge_bwd(dx, out, gate, wout, pg, brs, wbrs, ys, merged, *, name, tm=256):
    l, d = dx.shape
    tm = min(tm, l)
    nb = l // tm
    w = ys[0].shape[1]

    def body(dx_ref, out_ref, g_ref, w_ref, pg_ref, ba_ref, bs_ref, bp_ref, wa_ref, ws_ref, wp_ref,
             ya_ref, ys_ref, yp_ref, mg_ref,
             dya_ref, dys_ref, dyp_ref, dpg_ref, sums_ref, dwa_ref, dws_ref, dwp_ref, dwo_ref, acc_br, acc_out):
        i = pl.program_id(0)

        @pl.when(i == 0)
        def _():
            sums_ref[...] = jnp.zeros_like(sums_ref)
            acc_br[...] = jnp.zeros_like(acc_br)
            acc_out[...] = jnp.zeros_like(acc_out)

        dxv = dx_ref[...]
        sums_ref[0:1, :] += jnp.sum(dxv * out_ref[...].astype(F32), axis=0, keepdims=True)
        dmo = (dxv * g_ref[...]).astype(BF16)
        acc_out[...] += _dot(mg_ref[...], dmo, TN)
        dmerged = _dot(dmo, w_ref[...], NT)
        branches = ((ba_ref, wa_ref, ya_ref, dya_ref), (bs_ref, ws_ref, ys_ref, dys_ref), (bp_ref, wp_ref, yp_ref, dyp_ref))
        for k, (b_ref, wk_ref, y_ref, dy_ref) in enumerate(branches):
            gk = _sigmoid(pg_ref[:, k * d:(k + 1) * d].astype(F32))
            dbr = (dmerged * gk).astype(BF16)
            dpg_ref[:, k * d:(k + 1) * d] = (dmerged * b_ref[...].astype(F32) * gk * (1.0 - gk)).astype(BF16)
            dy_ref[...] = _dot(dbr, wk_ref[...], NT)
            acc_br[k] += _dot(y_ref[...], dbr, TN)

        @pl.when(i == nb - 1)
        def _():
            for k, dw_ref in enumerate((dwa_ref, dws_ref, dwp_ref)):
                dw_ref[...] = acc_br[k].astype(BF16)
            dwo_ref[...] = acc_out[...].astype(BF16)

    row = pl.BlockSpec((tm, d), lambda i: (i, 0))
    half = pl.BlockSpec((tm, w), lambda i: (i, 0))
    wide = pl.BlockSpec((tm, 3 * d), lambda i: (i, 0))
    const = lambda shape: pl.BlockSpec(shape, lambda i: (0,) * len(shape))
    return pl.pallas_call(
        body, grid=(nb,),
        in_specs=[row, row, const((1, d)), const((d, d)), wide, row, row, row, const((w, d)), const((w, d)), const((w, d)),
                  half, half, half, row],
        out_specs=[half, half, half, wide, const((8, d)), const((w, d)), const((w, d)), const((w, d)), const((d, d))],
        out_shape=[jax.ShapeDtypeStruct((l, w), F32)] * 3 + [jax.ShapeDtypeStruct((l, 3 * d), BF16),
                                                             jax.ShapeDtypeStruct((8, d), F32)]
                  + [jax.ShapeDtypeStruct((w, d), BF16)] * 3 + [jax.ShapeDtypeStruct((d, d), BF16)],
        scratch_shapes=[pltpu.VMEM((3, w, d), F32), pltpu.VMEM((d, d), F32)],
        compiler_params=_params("arbitrary"), name=name)(dx, out, gate, wout, pg, *brs, *wbrs, *ys, merged)


def _adamw(w, gs, m, v, *, name, tr=256):
    r, c = w.shape
    ns = len(gs)
    p, rs, _ = gs[0].shape
    assert rs * ns == r
    tr = min(tr, rs)
    assert rs % tr == 0
    nr = rs // tr
    c1 = 1.0 / (1.0 - ADAM_B1 ** ADAM_STEP)
    c2 = 1.0 / (1.0 - ADAM_B2 ** ADAM_STEP)

    def body(*refs):
        w_ref, g_refs, (m_ref, v_ref, go_ref, d_ref, mo_ref, vo_ref) = refs[0], refs[1:1 + ns], refs[1 + ns:]
        slab = pl.program_id(0)
        gv = None
        for k, g_ref in enumerate(g_refs):
            gk = g_ref[0].astype(F32)
            for j in range(1, p):
                gk = gk + g_ref[j].astype(F32)
            gv = gk if gv is None else jnp.where(slab == k, gk, gv)
        go_ref[...] = gv
        mn = ADAM_B1 * m_ref[...] + (1.0 - ADAM_B1) * gv
        vn = ADAM_B2 * v_ref[...] + (1.0 - ADAM_B2) * (gv * gv)
        mo_ref[...] = mn
        vo_ref[...] = vn
        d_ref[...] = -ADAM_LR * ((mn * c1) / (jnp.sqrt(vn * c2) + ADAM_EPS) + ADAM_WD * w_ref[...])

    row = pl.BlockSpec((tr, c), lambda s, i: (s * nr + i, 0))
    g_specs = [pl.BlockSpec((p, tr, c), lambda s, i, k=k: (0, jnp.where(s == k, i, 0), 0)) for k in range(ns)]
    return pl.pallas_call(
        body, grid=(ns, nr),
        in_specs=[row] + g_specs + [row, row],
        out_specs=[row] * 4,
        out_shape=[jax.ShapeDtypeStruct((r, c), F32)] * 4,
        compiler_params=_params("arbitrary", "arbitrary"), name=name)(w, *gs, m, v)


def _exchange(arrs, *, scatter, name):
    n = len(arrs)
    out_shape = [jax.ShapeDtypeStruct(a.shape if scatter else (N_DEV,) + a.shape, a.dtype) for a in arrs]

    def body(*refs):
        ins, outs = refs[:n], refs[n:2 * n]
        send_sems, recv_sems, loc_sems = refs[2 * n:]
        me = 4 * lax.axis_index("x") + 2 * lax.axis_index("y") + lax.axis_index("c")
        local = []
        for k in range(n):
            src = ins[k].at[me] if scatter else ins[k]
            cp = pltpu.make_async_copy(src, outs[k].at[me], loc_sems.at[k])
            cp.start()
            local.append(cp)
        remote = []
        for r in range(1, N_DEV):
            peer = me ^ r
            for k in range(n):
                src = ins[k].at[peer] if scatter else ins[k]
                cp = pltpu.make_async_remote_copy(
                    src_ref=src, dst_ref=outs[k].at[me], send_sem=send_sems.at[k, r - 1], recv_sem=recv_sems.at[k, r - 1],
                    device_id=(peer // 4, (peer // 2) % 2, peer % 2), device_id_type=pl.DeviceIdType.MESH)
                cp.start()
                remote.append(cp)
        for cp in remote:
            cp.wait()
        for cp in local:
            cp.wait()

    anyspec = pl.BlockSpec(memory_space=pl.ANY)
    return pl.pallas_call(
        body, in_specs=[anyspec] * n, out_specs=[anyspec] * n, out_shape=out_shape,
        scratch_shapes=[pltpu.SemaphoreType.DMA((n, N_DEV - 1)), pltpu.SemaphoreType.DMA((n, N_DEV - 1)),
                        pltpu.SemaphoreType.DMA((n,))],
        name=name)(*arrs)


def _mesh_place():
    x, y, c = lax.axis_index("x"), lax.axis_index("y"), lax.axis_index("c")
    other_chips = [(1 - x, y), (x, 1 - y), (1 - x, 1 - y)]
    return x, y, c, other_chips


def _gather_two_level(arrs, *, name):
    n = len(arrs)
    plan = _gather_plan(arrs)

    def body(*refs):
        start, forward, finish = plan["phases"](refs[:n], refs[n:2 * n], *refs[2 * n:])
        start()
        forward()
        finish()

    anyspec = pl.BlockSpec(memory_space=pl.ANY)
    return pl.pallas_call(
        body, in_specs=[anyspec] * n, out_specs=[anyspec] * n, out_shape=plan["out_shape"],
        scratch_shapes=plan["scratch"], name=name)(*arrs)


def _gather_plan(arrs):
    n = len(arrs)

    def phases(ins, outs, send_sems, recv_sems, loc_sems):
        x, y, c, chips = _mesh_place()
        me = 4 * x + 2 * y + c
        slot = lambda px, py, pc: 4 * px + 2 * py + pc

        def copy(k, j, src, block, to):
            return pltpu.make_async_remote_copy(
                src_ref=src, dst_ref=outs[k].at[block], send_sem=send_sems.at[k, j], recv_sem=recv_sems.at[k, j],
                device_id=to, device_id_type=pl.DeviceIdType.MESH)

        local = [pltpu.make_async_copy(ins[k], outs[k].at[me], loc_sems.at[k]) for k in range(n)]
        first = []
        for k in range(n):
            first.append(copy(k, 0, ins[k], me, (x, y, 1 - c)))
            for j, chip in enumerate(chips):
                first.append(copy(k, 1 + j, ins[k], me, (*chip, c)))
        passed = [copy(k, 4 + j, outs[k].at[slot(*chip, c)], slot(*chip, c), (x, y, 1 - c))
                  for j, chip in enumerate(chips) for k in range(n)]

        def start():
            for cp in local + first:
                cp.start()

        def forward():
            for j, chip in enumerate(chips):
                for k in range(n):
                    copy(k, 1 + j, ins[k], slot(*chip, c), (x, y, c)).wait_recv()
                    passed[j * n + k].start()

        def finish():
            for k in range(n):
                copy(k, 0, ins[k], slot(x, y, 1 - c), (x, y, c)).wait_recv()
                for j, chip in enumerate(chips):
                    copy(k, 4 + j, ins[k], slot(*chip, 1 - c), (x, y, c)).wait_recv()
            for cp in first + passed:
                cp.wait_send()
            for cp in local:
                cp.wait()

        return start, forward, finish

    return dict(
        args=list(arrs), out_shape=[jax.ShapeDtypeStruct((N_DEV,) + a.shape, a.dtype) for a in arrs],
        scratch=[pltpu.SemaphoreType.DMA((n, 7)), pltpu.SemaphoreType.DMA((n, 7)), pltpu.SemaphoreType.DMA((n,))],
        phases=phases)


def _allreduce_small(small, extra, *, name):
    r, lanes = small.shape
    assert r % 16 == 0
    h = r // 2
    e = extra.shape[0]

    def body(s_ref, x_ref, out_ref, xall_ref, sib_ref, parts_ref, send_sems, recv_sems):
        x, y, c, chips = _mesh_place()
        me = 4 * x + 2 * y + c
        my_chip = 2 * x + y
        sibling = (x, y, 1 - c)
        mine = pl.ds(pl.multiple_of(c * h, 8), h)
        theirs = pl.ds(pl.multiple_of((1 - c) * h, 8), h)

        def remote(j, src, dst, to):
            return pltpu.make_async_remote_copy(src_ref=src, dst_ref=dst, send_sem=send_sems.at[j],
                                                recv_sem=recv_sems.at[j], device_id=to, device_id_type=pl.DeviceIdType.MESH)

        to_sibling = remote(0, s_ref.at[theirs], sib_ref, sibling)
        to_sibling.start()
        xall_ref[me] = x_ref[...]
        extras = []
        for rr in range(1, N_DEV):
            peer = me ^ rr
            cp = remote(4 + rr, x_ref, xall_ref.at[me], (peer // 4, (peer // 2) % 2, peer % 2))
            cp.start()
            extras.append(cp)
        to_sibling.wait_recv()
        parts_ref[my_chip] = s_ref[mine] + sib_ref[...]
        to_chips = [remote(1 + j, parts_ref.at[my_chip], parts_ref.at[my_chip], (px, py, c))
                    for j, (px, py) in enumerate(chips)]
        for cp in to_chips:
            cp.start()
        for cp in to_chips:
            cp.wait_recv()
        out_ref[mine] = (parts_ref[0] + parts_ref[1]) + (parts_ref[2] + parts_ref[3])
        done = remote(4, out_ref.at[mine], out_ref.at[mine], sibling)
        done.start()
        remote(4, out_ref.at[theirs], out_ref.at[theirs], sibling).wait_recv()
        for cp in extras:
            cp.wait()
        to_sibling.wait_send()
        for cp in to_chips:
            cp.wait_send()
        done.wait_send()

    vmem = pl.BlockSpec(memory_space=pltpu.VMEM)
    return pl.pallas_call(
        body, in_specs=[vmem, vmem], out_specs=[vmem, vmem],
        out_shape=[jax.ShapeDtypeStruct((r, lanes), F32), jax.ShapeDtypeStruct((N_DEV, e, lanes), F32)],
        scratch_shapes=[pltpu.VMEM((h, lanes), F32), pltpu.VMEM((4, h, lanes), F32),
                        pltpu.SemaphoreType.DMA((12,)), pltpu.SemaphoreType.DMA((12,))],
        compiler_params=pltpu.CompilerParams(vmem_limit_bytes=VMEM_LIMIT), name=name)(small, extra)


def _sibling_swap(arrs, *, name):
    n = len(arrs)
    out_shape = [jax.ShapeDtypeStruct(a.shape[1:], a.dtype) for a in arrs]

    def body(*refs):
        ins, outs = refs[:n], refs[n:2 * n]
        send_sems, recv_sems = refs[2 * n:]
        x, y, c, _ = _mesh_place()
        copies = [pltpu.make_async_remote_copy(
            src_ref=ins[k].at[1 - c], dst_ref=outs[k], send_sem=send_sems.at[k], recv_sem=recv_sems.at[k],
            device_id=(x, y, 1 - c), device_id_type=pl.DeviceIdType.MESH) for k in range(n)]
        for cp in copies:
            cp.start()
        for cp in copies:
            cp.wait()

    anyspec = pl.BlockSpec(memory_space=pl.ANY)
    return pl.pallas_call(
        body, in_specs=[anyspec] * n, out_specs=[anyspec] * n, out_shape=out_shape,
        scratch_shapes=[pltpu.SemaphoreType.DMA((n,)), pltpu.SemaphoreType.DMA((n,))], name=name)(*arrs)


def _pair_add(mine, theirs, core, *, name, tr=256):
    _, r, c = mine.shape
    tr = min(tr, r)
    assert r % tr == 0

    def body(core_ref, m_ref, t_ref, o_ref):
        o_ref[...] = (m_ref[0].astype(F32) + t_ref[...].astype(F32)).astype(BF16)

    return pl.pallas_call(
        body,
        grid_spec=pltpu.PrefetchScalarGridSpec(
            num_scalar_prefetch=1, grid=(r // tr,),
            in_specs=[pl.BlockSpec((1, tr, c), lambda i, core_ref: (core_ref[0], i, 0)),
                      pl.BlockSpec((tr, c), lambda i, core_ref: (i, 0))],
            out_specs=pl.BlockSpec((tr, c), lambda i, core_ref: (i, 0))),
        out_shape=jax.ShapeDtypeStruct((r, c), BF16),
        compiler_params=_params("parallel"), name=name)(core, mine, theirs)


def _pair_add_small(mines, theirs, core, *, name):
    n = len(mines)

    def body(core_ref, *refs):
        for m_ref, t_ref, o_ref in zip(refs[:n], refs[n:2 * n], refs[2 * n:]):
            o_ref[...] = (m_ref[0].astype(F32) + t_ref[...].astype(F32)).astype(BF16)

    whole = lambda a: pl.BlockSpec(a.shape, lambda i, core_ref: (0,) * a.ndim)
    return pl.pallas_call(
        body,
        grid_spec=pltpu.PrefetchScalarGridSpec(
            num_scalar_prefetch=1, grid=(1,),
            in_specs=[pl.BlockSpec((1,) + m.shape[1:], lambda i, core_ref: (core_ref[0], 0, 0)) for m in mines]
                     + [whole(t) for t in theirs],
            out_specs=[whole(t) for t in theirs]),
        out_shape=[jax.ShapeDtypeStruct(t.shape, BF16) for t in theirs],
        compiler_params=_params("arbitrary"), name=name)(core, *mines, *theirs)


def _chip_scatter(arrs, *, name):
    n = len(arrs)
    plan = _chip_scatter_plan(arrs)

    def body(*refs):
        start, _, finish = plan["phases"](refs[:n], refs[n:2 * n], *refs[2 * n:])
        start()
        finish()

    anyspec = pl.BlockSpec(memory_space=pl.ANY)
    return pl.pallas_call(
        body, in_specs=[anyspec] * n, out_specs=[anyspec] * n, out_shape=plan["out_shape"],
        scratch_shapes=plan["scratch"], name=name)(*arrs)


def _chip_scatter_plan(arrs):
    n = len(arrs)

    def phases(ins, outs, send_sems, recv_sems, loc_sems):
        x, y, c, chips = _mesh_place()
        mine = 2 * x + y
        local = [pltpu.make_async_copy(ins[k].at[mine], outs[k].at[mine], loc_sems.at[k]) for k in range(n)]
        remote = [pltpu.make_async_remote_copy(
            src_ref=ins[k].at[2 * px + py], dst_ref=outs[k].at[mine], send_sem=send_sems.at[k, j],
            recv_sem=recv_sems.at[k, j], device_id=(px, py, c), device_id_type=pl.DeviceIdType.MESH)
            for j, (px, py) in enumerate(chips) for k in range(n)]

        def start():
            for cp in local + remote:
                cp.start()

        def finish():
            for cp in remote:
                cp.wait()
            for cp in local:
                cp.wait()

        return start, (lambda: None), finish

    return dict(
        args=list(arrs), out_shape=[jax.ShapeDtypeStruct(a.shape, a.dtype) for a in arrs],
        scratch=[pltpu.SemaphoreType.DMA((n, 3)), pltpu.SemaphoreType.DMA((n, 3)), pltpu.SemaphoreType.DMA((n,))],
        phases=phases)


def _ssm_discretize(a_re, a_im, log_dt, b_re, b_im):
    dt = jnp.exp(log_dt)[:, None]
    mag = jnp.exp(a_re * dt)
    lr = mag * jnp.cos(a_im * dt)
    li = mag * jnp.sin(a_im * dt)
    den = a_re * a_re + a_im * a_im
    cr = ((lr - 1.0) * a_re + li * a_im) / den
    ci = (li * a_re - (lr - 1.0) * a_im) / den
    bbr = cr[..., None] * b_re - ci[..., None] * b_im
    bbi = cr[..., None] * b_im + ci[..., None] * b_re
    return lr, li, bbr, bbi


def _ssm_dense(lr, li, bbr, bbi, c_re, c_im):
    scan_ops = _scan_tables(lr.reshape(-1), li.reshape(-1)) + _scan_consts(SEQ_BLOCK)
    per_half = SSM_GROUPS // SSM_HALVES

    def halves(a, rows, cols):
        a = a.reshape(SSM_HALVES, per_half, rows, 1, cols)
        shape = (SSM_HALVES, per_half, rows, per_half, cols)
        on_diagonal = lax.broadcasted_iota(jnp.int32, shape, 1) == lax.broadcasted_iota(jnp.int32, shape, 3)
        return jnp.where(on_diagonal, a, 0.0).reshape(SSM_HALVES, per_half * rows, per_half * cols)

    bblk = jnp.stack([halves(b.transpose(0, 2, 1), SSM_GROUP, SSM_STATE) for b in (bbr, bbi)]).astype(BF16)
    cblk = jnp.stack([halves(c.transpose(0, 2, 1), SSM_STATE, SSM_GROUP) for c in (c_re, c_im)]).astype(BF16)
    return scan_ops, bblk, cblk


def _ssm_extract(db, dc, sums):
    db = db.reshape(2, SSM_GROUPS, SSM_GROUP, SSM_STATE).transpose(0, 1, 3, 2)
    dc = dc.reshape(2, SSM_GROUPS, SSM_GROUP, SSM_STATE)
    dlr = sums[0].reshape(SSM_GROUPS, SSM_STATE)
    dli = sums[1].reshape(SSM_GROUPS, SSM_STATE)
    return dlr, dli, db[0], db[1], dc[0], dc[1]


IN_SPLITS = (ATT_W, KV_W, KV_W, SSM_W, POOL_W, ATT_W, SSM_W, POOL_W, 3 * D_MODEL)


def _split_w_in(w):
    idx = [0]
    for s in IN_SPLITS:
        idx.append(idx[-1] + s)
    seg = [w[..., idx[k]:idx[k + 1]] for k in range(len(IN_SPLITS))]
    q, k, v, us, up, za, zs, zp, gl = seg
    return (jnp.concatenate([q, za, k, v], axis=-1), jnp.concatenate([us, zs], axis=-1),
            jnp.concatenate([up, zp], axis=-1), gl)


def _merge_w_in(da, ds, dp, dg):
    q, za, k, v = da[..., :ATT_W], da[..., ATT_W:2 * ATT_W], da[..., 2 * ATT_W:2 * ATT_W + KV_W], da[..., 2 * ATT_W + KV_W:]
    us, zs = ds[..., :SSM_W], ds[..., SSM_W:]
    up, zp = dp[..., :POOL_W], dp[..., POOL_W:]
    return jnp.concatenate([q, k, v, us, up, za, zs, zp, dg], axis=-1)


def _layer_fwd(x, lw, li, late=None, comm_attn=None, comm_ssm=None):
    tag = f"l{li}"
    h = _ln_fwd(x, lw["norm_g"], lw["shift"], lw["scale"], name=f"ln_fwd_{tag}")
    pa = _mm(h, lw["w_a"], tn=1280, out_dtype=BF16, name=f"proj_a_{tag}")
    ps = _mm(h, lw["w_s"], out_dtype=BF16, name=f"proj_s_{tag}")
    pp = _mm(h, lw["w_p"], out_dtype=BF16, name=f"proj_p_{tag}")
    if late is None:
        pg = _mm(h, lw["w_g"], out_dtype=BF16, name=f"proj_g_{tag}")
    else:
        pg, arrived = _mm(h, lw["w_g"], out_dtype=BF16, name=f"proj_g_{tag}", comm=late[0])
        lw = {**lw, **late[1](arrived)}
    ya, from_attn = _attn_fwd(pa, lw["sinks"], name=f"attn_fwd_{tag}", comm=comm_attn)
    ys, chk, states, *from_ssm = _ssm_fwd(ps, lw["lam"], lw["bblk"], lw["cblk"], lw["ssm_d"], lw["w_glu"], lw["b_glu"],
                                          name=f"ssm_fwd_{tag}", comm=comm_ssm)
    yp = _pool_fwd(pp, lw["w_pool"], lw["pool_scale"], name=f"pool_fwd_{tag}")
    x_new, merged, ba, bs, bp, out = _merge_fwd(ya, ys, yp, lw["w_br_att"], lw["w_br_ssm"], lw["w_br_pool"], pg,
                                                lw["w_out"], x, lw["gate"], name=f"merge_fwd_{tag}")
    saved = dict(x=x, h=h, pa=pa, ps=ps, pp=pp, pg=pg, ya=ya, ys=ys, yp=yp, chk=chk, states=states, merged=merged,
                 ba=ba, bs=bs, bp=bp, out=out)
    return x_new, saved, lw, list(from_attn), list(from_ssm)


def _layer_bwd(dx, lw, sv, li, comm=None, own=None):
    tag = f"l{li}"
    g = {}
    dya, dys, dyp, dpg, gate_sums, g["w_br_att"], g["w_br_ssm"], g["w_br_pool"], g["w_out"] = _merge_bwd(
        dx, sv["out"], lw["gate"], lw["w_out"], sv["pg"], (sv["ba"], sv["bs"], sv["bp"]),
        (lw["w_br_att"], lw["w_br_ssm"], lw["w_br_pool"]), (sv["ya"], sv["ys"], sv["yp"]), sv["merged"],
        name=f"merge_bwd_{tag}")
    dpa, dsink = _attn_bwd(sv["pa"], lw["sinks"], dya, name=f"attn_bwd_{tag}")
    dps, db_dense, dc_dense, dwglu, ssm_sums, *exchanged = _ssm_bwd(
        sv["ps"], dys, sv["chk"], sv["states"], lw["lam"], lw["bblk"], lw["cblk"], lw["ssm_d"], lw["w_glu"], lw["b_glu"],
        name=f"ssm_bwd_{tag}", comm=comm)
    g["w_glu"] = dwglu.astype(BF16)
    dpp, dwpool, pool_sums = _pool_bwd(sv["pp"], dyp, lw["w_pool"], lw["pool_scale"], name=f"pool_bwd_{tag}")
    h = sv["h"]
    dw_a = _mm_tn(h, dpa, out_dtype=BF16, tn=1280, name=f"dw_a_{tag}")
    dw_s = _mm_tn(h, dps, out_dtype=BF16, name=f"dw_s_{tag}")
    dw_p = _mm_tn(h, dpp, out_dtype=BF16, name=f"dw_p_{tag}")
    dh_pairs = [(dpa, lw["w_a"]), (dps, lw["w_s"]), (dpp, lw["w_p"]), (dpg, lw["w_g"])]
    if own is None:
        dw_g, from_late = _mm_tn(h, dpg, out_dtype=BF16, name=f"dw_g_{tag}"), []
        g["w_in"] = _merge_w_in(dw_a, dw_s, dw_p, dw_g)
        dh, from_w_in = _mm_nt_sum(dh_pairs, name=f"dh_{tag}"), []
    else:
        dw_g, from_late = _mm_tn(h, dpg, out_dtype=BF16, name=f"dw_g_{tag}", comm=own({k: g[k] for k in LATE_WEIGHTS}))
        g["w_in"] = _merge_w_in(dw_a, dw_s, dw_p, dw_g)
        dh, from_w_in = _mm_nt_sum(dh_pairs, name=f"dh_{tag}", comm=own({"w_in": g["w_in"]}))
    dx_in, ln_sums = _ln_bwd(sv["x"], dh, dx, lw["norm_g"], lw["scale"], name=f"ln_bwd_{tag}")
    g["dmod"] = jnp.concatenate([ln_sums[0], ln_sums[1], gate_sums[0]])
    g["norm_g"] = ln_sums[2]
    g["attn_sinks"] = dsink[:, 0]
    g["ssm_raw"] = _ssm_extract(db_dense, dc_dense, ssm_sums)
    g["ssm_d"] = ssm_sums[2, :SSM_W]
    g["b_glu"] = ssm_sums[3, :SSM_W]
    g["w_pool"] = dwpool
    g["pool_scale"] = pool_sums[0]
    return dx_in, g, exchanged, list(from_w_in) + list(from_late)


BIG_WEIGHTS = ("w_in", "w_glu", "w_br_att", "w_br_ssm", "w_br_pool", "w_out")
ROW_SHARDED = ("w_glu", "w_out")


LATE_WEIGHTS = BIG_WEIGHTS[1:]


def _full_weights(keys, gathered):
    full = {}
    for k, g in zip(keys, gathered):
        if k in ROW_SHARDED:
            full[k] = g.reshape(N_DEV * g.shape[1], g.shape[2])
        else:
            full[k] = g.transpose(1, 0, 2).reshape(g.shape[1], N_DEV * g.shape[2])
    return full


def _by_destination(keys, grads):
    out = []
    for k in keys:
        g = grads[k]
        if k in ROW_SHARDED:
            out.append(g.reshape(4, 2, g.shape[0] // N_DEV, g.shape[1]).transpose(1, 0, 2, 3))
        else:
            out.append(g.reshape(g.shape[0], 4, 2, g.shape[1] // N_DEV).transpose(2, 1, 0, 3))
    return out


def _prepare_layer(li, mod, norm_g, w_in_full, attn_sinks, disc, ssm_c_re, ssm_c_im, ssm_d, b_glu, w_pool, pool_scale):
    d = D_MODEL
    lr, li_, bbr, bbi = disc
    lam, bblk, cblk = _ssm_dense(lr[li], li_[li], bbr[li], bbi[li], ssm_c_re[li], ssm_c_im[li])
    w_a, w_s, w_p, w_g = _split_w_in(w_in_full)
    return dict(
        norm_g=norm_g[li][None, :], shift=mod[li, :d][None, :], scale=mod[li, d:2 * d][None, :],
        gate=mod[li, 2 * d:][None, :], w_a=w_a, w_s=w_s, w_p=w_p, w_g=w_g,
        sinks=attn_sinks[li], lam=lam, bblk=bblk, cblk=cblk, ssm_d=ssm_d[li][None, :],
        b_glu=b_glu[li][None, :], w_pool=w_pool[li].astype(BF16), pool_scale=pool_scale[li][None, :])


SMALL_ROWS = 64
SMALL_ORDER = ("norm_g", "attn_sinks", "ssm_d", "b_glu", "w_pool", "pool_scale", "dmod")


def _pack_small(loss, dfinal_g, layer_grads):
    parts = [jnp.broadcast_to(loss.reshape(1), (128,)), dfinal_g]
    for g in layer_grads:
        for k in SMALL_ORDER:
            v = g[k].reshape(-1)
            if v.shape[0] % 128:
                v = jnp.pad(v, (0, 128 - v.shape[0] % 128))
            parts.append(v)
        for v in g["ssm_raw"]:
            parts.append(v.reshape(-1))
    flat = jnp.concatenate(parts)
    return jnp.pad(flat, (0, (-flat.shape[0]) % (SMALL_ROWS * 128))).reshape(-1, 128)


def _unpack_small(flat, shapes):
    out, off = [], 0
    for s in shapes:
        n = int(math.prod(s))
        out.append(flat[off:off + n].reshape(s))
        off += n + (-n) % 128
    return out


def kernel(x, c, norm_g, w_ada, b_ada, w_in, attn_sinks, ssm_a_re, ssm_a_im, ssm_log_dt, ssm_b_re, ssm_b_im, ssm_c_re, ssm_c_im, ssm_d, w_glu, b_glu, w_pool, pool_scale, w_br_att, w_br_ssm, w_br_pool, w_out, final_g, loss_target, m_norm_g, m_w_ada, m_b_ada, m_w_in, m_attn_sinks, m_ssm_a_re, m_ssm_a_im, m_ssm_log_dt, m_ssm_b_re, m_ssm_b_im, m_ssm_c_re, m_ssm_c_im, m_ssm_d, m_w_glu, m_b_glu, m_w_pool, m_pool_scale, m_w_br_att, m_w_br_ssm, m_w_br_pool, m_w_out, m_final_g, v_norm_g, v_w_ada, v_b_ada, v_w_in, v_attn_sinks, v_ssm_a_re, v_ssm_a_im, v_ssm_log_dt, v_ssm_b_re, v_ssm_b_im, v_ssm_c_re, v_ssm_c_im, v_ssm_d, v_w_glu, v_b_glu, v_w_pool, v_pool_scale, v_w_br_att, v_w_br_ssm, v_w_br_pool, v_w_out, v_final_g):
    me = 4 * lax.axis_index("x") + 2 * lax.axis_index("y") + lax.axis_index("c")
    d = D_MODEL
    ada_w = 3 * d // N_DEV

    (c_all,) = _exchange([c.reshape(8, 128)], scatter=False, name="gather_c")
    c_act = jax.nn.silu(c_all.reshape(N_DEV, d))
    b_cols = lax.dynamic_slice(b_ada, (0, me * ada_w), (DEPTH, ada_w))
    mod_part = jnp.concatenate(
        [_mm(c_act, w_ada[li], name=f"ada_fwd_l{li}") + b_cols[li][None, :] for li in range(DEPTH)], axis=0)
    (mod_all,) = _exchange([mod_part], scatter=False, name="gather_mod")
    mod_all = mod_all.reshape(N_DEV, DEPTH, N_DEV, ada_w)
    mod_mine = lax.dynamic_index_in_dim(mod_all, me, axis=2, keepdims=False)
    mod_mine = mod_mine.transpose(1, 0, 2).reshape(DEPTH, 3 * d)

    sharded = dict(w_in=w_in, w_glu=w_glu, w_br_att=w_br_att, w_br_ssm=w_br_ssm, w_br_pool=w_br_pool, w_out=w_out)
    shards = lambda li, keys: [sharded[k][li].astype(BF16) for k in keys]
    disc, disc_vjp = jax.vjp(jax.vmap(_ssm_discretize), ssm_a_re, ssm_a_im, ssm_log_dt, ssm_b_re, ssm_b_im)
    layer = lambda li, gathered_w_in: _prepare_layer(
        li, mod_mine, norm_g, _full_weights(("w_in",), gathered_w_in)["w_in"], attn_sinks, disc, ssm_c_re, ssm_c_im,
        ssm_d, b_glu, w_pool, pool_scale)
    late_weights = lambda gathered: _full_weights(LATE_WEIGHTS, gathered)
    core = lax.axis_index("c").astype(jnp.int32).reshape(1)

    def chip_sums_of(keys, grads_li, tag):
        by_dest = _by_destination(keys, grads_li)
        from_sibling = _sibling_swap(by_dest, name=f"grads_sibling_swap_{tag}")
        flat = {k: (a.reshape(2, -1, a.shape[-1]), b.reshape(-1, b.shape[-1]))
                for k, a, b in zip(keys, by_dest, from_sibling)}
        small = [k for k in keys if k != "w_in"]
        sums = {}
        if "w_in" in flat:
            sums["w_in"] = _pair_add(*flat["w_in"], core, name=f"grads_pair_add_{tag}_w_in")
        if small:
            added = _pair_add_small([flat[k][0] for k in small], [flat[k][1] for k in small], core,
                                    name=f"grads_pair_add_{tag}_late")
            sums.update(zip(small, added))
        return [sums[k].reshape(b.shape) for k, b in zip(keys, from_sibling)]

    layers, saved, grads = [None] * DEPTH, [None] * DEPTH, [None] * DEPTH
    layers[0] = layer(0, _gather_two_level(shards(0, ("w_in",)), name="gather_w_in_l0"))
    xs, saved[0], layers[0], late1, w_in1 = _layer_fwd(
        x[0], layers[0], 0, late=(_gather_plan(shards(0, LATE_WEIGHTS)), late_weights),
        comm_attn=_gather_plan(shards(1, LATE_WEIGHTS)), comm_ssm=_gather_plan(shards(1, ("w_in",))))
    layers[1] = {**layer(1, w_in1), **late_weights(late1)}
    xs, saved[1], _, _, _ = _layer_fwd(xs, layers[1], 1)
    dx, fin_sums = _final_loss(xs, final_g[None, :], loss_target[0])
    loss_part = jnp.sum(fin_sums[1])
    dx, grads[1], _, _ = _layer_bwd(dx, layers[1], saved[1], 1)
    dx, grads[0], scattered1, scattered0 = _layer_bwd(
        dx, layers[0], saved[0], 0, comm=_chip_scatter_plan(chip_sums_of(BIG_WEIGHTS, grads[1], "l1")),
        own=lambda g: _chip_scatter_plan(chip_sums_of(tuple(g), g, "l0_" + "_".join(g))))
    big = list(zip(scattered0, scattered1))
    grad_x = dx[None]

    small = _pack_small(loss_part, fin_sums[0], grads)
    dmod_rows = jnp.concatenate([grads[li]["dmod"] for li in range(DEPTH)]).reshape(-1, 128)
    small_sum, dmod_gathered = _allreduce_small(small, dmod_rows, name="allreduce_small")
    out = {}

    def adam(name, w, g_slabs, m, v):
        shp = w.shape
        r = int(math.prod(shp[:-1])) if len(shp) > 1 else 1
        w2, m2, v2 = (a.reshape(r, shp[-1]) for a in (w, m, v))
        gs = [g.reshape(g.shape[0], r // len(g_slabs), shp[-1]) for g in g_slabs]
        res = _adamw(w2, gs, m2, v2, name=f"adamw_{name}")
        out[name] = tuple(a.reshape(shp) for a in res)

    flat = small_sum.reshape(-1)
    shapes = [(128,), (d,)]
    for _ in range(DEPTH):
        shapes += [(d,), (N_HEADS,), (SSM_W,), (SSM_W,), (4, POOL_GW, POOL_GW), (POOL_W,), (3 * d,),
                   (SSM_GROUPS, SSM_STATE), (SSM_GROUPS, SSM_STATE), (SSM_GROUPS, SSM_STATE, SSM_GROUP),
                   (SSM_GROUPS, SSM_STATE, SSM_GROUP), (SSM_GROUPS, SSM_GROUP, SSM_STATE), (SSM_GROUPS, SSM_GROUP, SSM_STATE)]
    un = _unpack_small(flat, shapes)
    loss = un[0][0]
    g_final_g = un[1]
    per = 13
    gl = [un[2 + li * per: 2 + (li + 1) * per] for li in range(DEPTH)]
    st = lambda j: jnp.stack([gl[li][j] for li in range(DEPTH)])
    g_norm_g, g_sinks, g_ssm_d, g_b_glu, g_w_pool, g_pool_scale, g_b_ada = (st(j) for j in range(7))
    d_lr, d_li, d_bbr, d_bbi, g_c_re, g_c_im = (st(j) for j in range(7, 13))
    g_a_re, g_a_im, g_log_dt, g_b_re, g_b_im = disc_vjp((d_lr, d_li, d_bbr, d_bbi))

    dmod_all = lax.dynamic_slice(dmod_gathered.reshape(N_DEV, DEPTH, 3 * d), (0, 0, me * ada_w), (N_DEV, DEPTH, ada_w))
    dmod_all = dmod_all.transpose(1, 0, 2)
    g_w_ada = jnp.stack([_mm_tn(c_act, dmod_all[li], tm=d, tn=ada_w, tk=N_DEV, name=f"dw_ada_l{li}") for li in range(DEPTH)])

    adam("w_ada", w_ada, [g_w_ada[None]], m_w_ada, v_w_ada)
    adam("w_in", w_in, big[0], m_w_in, v_w_in)
    adam("w_glu", w_glu, big[1], m_w_glu, v_w_glu)
    adam("w_br_att", w_br_att, big[2], m_w_br_att, v_w_br_att)
    adam("w_br_ssm", w_br_ssm, big[3], m_w_br_ssm, v_w_br_ssm)
    adam("w_br_pool", w_br_pool, big[4], m_w_br_pool, v_w_br_pool)
    adam("w_out", w_out, big[5], m_w_out, v_w_out)

    small_names = ["norm_g", "b_ada", "attn_sinks", "ssm_a_re", "ssm_a_im", "ssm_log_dt", "ssm_b_re", "ssm_b_im",
                   "ssm_c_re", "ssm_c_im", "ssm_d", "b_glu", "w_pool", "pool_scale", "final_g"]
    small_w = [norm_g, b_ada, attn_sinks, ssm_a_re, ssm_a_im, ssm_log_dt, ssm_b_re, ssm_b_im, ssm_c_re, ssm_c_im,
               ssm_d, b_glu, w_pool, pool_scale, final_g]
    small_m = [m_norm_g, m_b_ada, m_attn_sinks, m_ssm_a_re, m_ssm_a_im, m_ssm_log_dt, m_ssm_b_re, m_ssm_b_im,
               m_ssm_c_re, m_ssm_c_im, m_ssm_d, m_b_glu, m_w_pool, m_pool_scale, m_final_g]
    small_v = [v_norm_g, v_b_ada, v_attn_sinks, v_ssm_a_re, v_ssm_a_im, v_ssm_log_dt, v_ssm_b_re, v_ssm_b_im,
               v_ssm_c_re, v_ssm_c_im, v_ssm_d, v_b_glu, v_w_pool, v_pool_scale, v_final_g]
    small_g = [g_norm_g, g_b_ada, g_sinks, g_a_re, g_a_im, g_log_dt, g_b_re, g_b_im, g_c_re, g_c_im,
               g_ssm_d, g_b_glu, g_w_pool, g_pool_scale, g_final_g]

    for nm, w, g, m, v in zip(small_names, small_w, small_g, small_m, small_v):
        adam(nm, w, [g[None]], m, v)

    order = ["norm_g", "w_ada", "b_ada", "w_in", "attn_sinks", "ssm_a_re", "ssm_a_im", "ssm_log_dt", "ssm_b_re",
             "ssm_b_im", "ssm_c_re", "ssm_c_im", "ssm_d", "w_glu", "b_glu", "w_pool", "pool_scale", "w_br_att",
             "w_br_ssm", "w_br_pool", "w_out", "final_g"]
    return (loss, grad_x, *[out[k][0] for k in order], *[out[k][1] for k in order],
            *[out[k][2] for k in order], *[out[k][3] for k in order])
```

```python
import functools
import math

import jax
import jax.numpy as jnp
from jax import lax
from jax.experimental import pallas as pl
from jax.experimental.pallas import tpu as pltpu

F32 = jnp.float32
BF16 = jnp.bfloat16

N_DEV = 8
D_MODEL = 1024
DEPTH = 2
CHUNK = 64
N_HEADS = 8
N_KV_HEADS = 2
HEAD_DIM = 64
Q_PER_KV = N_HEADS // N_KV_HEADS
WINDOW = 128
ATT_W = 512
KV_W = 128
SSM_W = 512
SSM_GROUP = 16
SSM_GROUPS = 32
SSM_STATE = 64
SSM_N = SSM_GROUPS * SSM_STATE
POOL_W = 512
POOL_WINDOWS = (2, 4, 8, 16)
POOL_GW = 128
POOL_HALO = 16
EPS = 1e-6
NEG_INF = -1e30
ADAM_LR = 0.001
ADAM_B1 = 0.9
ADAM_B2 = 0.999
ADAM_EPS = 1e-08
ADAM_WD = 0.01
ADAM_STEP = 10

SEQ_BLOCK = 256
ATT_BLOCK = 128
VMEM_LIMIT = 56 * 1024 * 1024

NN = (((1,), (0,)), ((), ()))
NT = (((1,), (1,)), ((), ()))
TN = (((0,), (0,)), ((), ()))


def _dot(a, b, dims=NN):
    return lax.dot_general(a.astype(BF16), b.astype(BF16), dims, preferred_element_type=F32)


def _params(*sem):
    return pltpu.CompilerParams(dimension_semantics=sem, vmem_limit_bytes=VMEM_LIMIT)


def _sigmoid(x):
    return 0.5 + 0.5 * jnp.tanh(0.5 * x)


def _silu_and_grad(z):
    s = _sigmoid(z)
    return z * s, s * (1.0 + z * (1.0 - s))


_GELU_K = math.sqrt(2.0 / math.pi)


def _gelu_and_grad(x):
    inner = _GELU_K * (x + 0.044715 * x * x * x)
    t = jnp.tanh(inner)
    val = 0.5 * x * (1.0 + t)
    grad = 0.5 * (1.0 + t) + 0.5 * x * (1.0 - t * t) * _GELU_K * (1.0 + 3.0 * 0.044715 * x * x)
    return val, grad


def _mm(a, b, *, nt=False, out_dtype=F32, tm=1024, tn=1024, name, comm=None):
    m, k = a.shape
    n = b.shape[0] if nt else b.shape[1]
    tm, tn = min(tm, m), min(tn, n)
    assert m % tm == 0 and n % tn == 0
    dims = NT if nt else NN
    grid = (m // tm, n // tn)
    c_args, c_in, c_out, c_shape, c_scratch = _comm_extra(comm)

    def body(a_ref, b_ref, o_ref):
        o_ref[...] = _dot(a_ref[...], b_ref[...], dims).astype(out_dtype)

    b_spec = pl.BlockSpec((tn, k), lambda i, j: (j, 0)) if nt else pl.BlockSpec((k, tn), lambda i, j: (0, j))
    res = pl.pallas_call(
        _with_comm(body, comm, 2, 1, grid, -1), grid=grid,
        in_specs=[pl.BlockSpec((tm, k), lambda i, j: (i, 0)), b_spec] + c_in,
        out_specs=[pl.BlockSpec((tm, tn), lambda i, j: (i, j))] + c_out,
        out_shape=[jax.ShapeDtypeStruct((m, n), out_dtype)] + c_shape,
        scratch_shapes=c_scratch,
        compiler_params=_params(*(("arbitrary",) * 2 if comm else ("parallel",) * 2)), name=name)(a, b, *c_args)
    return (res[0], list(res[1:])) if comm else res[0]


def _mm_nt_sum(pairs, *, out_dtype=F32, tm=512, tn=512, name, comm=None):
    m = pairs[0][0].shape[0]
    n = pairs[0][1].shape[0]
    np_ = len(pairs)
    grid = (m // tm, n // tn)
    c_args, c_in, c_out, c_shape, c_scratch = _comm_extra(comm)

    def body(*refs):
        o_ref = refs[-1]
        acc = _dot(refs[0][...], refs[1][...], NT)
        for p in range(1, np_):
            acc = acc + _dot(refs[2 * p][...], refs[2 * p + 1][...], NT)
        o_ref[...] = acc.astype(out_dtype)

    in_specs, args = [], []
    for a, b in pairs:
        in_specs.append(pl.BlockSpec((tm, a.shape[1]), lambda i, j: (i, 0)))
        in_specs.append(pl.BlockSpec((tn, b.shape[1]), lambda i, j: (j, 0)))
        args += [a, b]
    res = pl.pallas_call(
        _with_comm(body, comm, 2 * np_, 1, grid, -1), grid=grid, in_specs=in_specs + c_in,
        out_specs=[pl.BlockSpec((tm, tn), lambda i, j: (i, j))] + c_out,
        out_shape=[jax.ShapeDtypeStruct((m, n), out_dtype)] + c_shape,
        scratch_shapes=c_scratch,
        compiler_params=_params(*(("arbitrary",) * 2 if comm else ("parallel",) * 2)), name=name)(*args, *c_args)
    return (res[0], list(res[1:])) if comm else res[0]


def _mm_tn(a, b, *, out_dtype=F32, tm=1024, tn=1024, tk=1024, name, comm=None):
    k, m = a.shape
    n = b.shape[1]
    assert m % min(tm, m) == 0 and n % min(tn, n) == 0 and k % min(tk, k) == 0
    tm, tn, tk = min(tm, m), min(tn, n), min(tk, k)
    nk = k // tk
    grid = (m // tm, n // tn, nk)
    c_args, c_in, c_out, c_shape, c_scratch = _comm_extra(comm)

    def body(a_ref, b_ref, o_ref, acc_ref):
        kk = pl.program_id(2)

        @pl.when(kk == 0)
        def _():
            acc_ref[...] = jnp.zeros_like(acc_ref)

        acc_ref[...] += _dot(a_ref[...], b_ref[...], TN)

        @pl.when(kk == nk - 1)
        def _():
            o_ref[...] = acc_ref[...].astype(out_dtype)

    res = pl.pallas_call(
        _with_comm(body, comm, 2, 1, grid, -1), grid=grid,
        in_specs=[pl.BlockSpec((tk, tm), lambda i, j, kk: (kk, i)), pl.BlockSpec((tk, tn), lambda i, j, kk: (kk, j))] + c_in,
        out_specs=[pl.BlockSpec((tm, tn), lambda i, j, kk: (i, j))] + c_out,
        out_shape=[jax.ShapeDtypeStruct((m, n), out_dtype)] + c_shape,
        scratch_shapes=[pltpu.VMEM((tm, tn), F32)] + c_scratch,
        compiler_params=_params(*(("arbitrary",) * 3 if comm else ("parallel", "parallel", "arbitrary"))),
        name=name)(a, b, *c_args)
    return (res[0], list(res[1:])) if comm else res[0]


def _ln_proj(x, g, shift, scale, weights, *, name, tm=512, comm=None):
    l, d = x.shape
    tm = min(tm, l)
    nb = l // tm
    nw = len(weights)
    c_args, c_in, c_out, c_shape, c_scratch = _comm_extra(comm)

    def body(x_ref, g_ref, sh_ref, sc_ref, *refs):
        w_refs, h_ref, p_refs = refs[:nw], refs[nw], refs[nw + 1:]
        xv = x_ref[...]
        n = xv * lax.rsqrt(jnp.mean(xv * xv, axis=-1, keepdims=True) + EPS)
        h = ((n * g_ref[...]) * (1.0 + sc_ref[...]) + sh_ref[...]).astype(BF16)
        h_ref[...] = h
        for w_ref, p_ref in zip(w_refs, p_refs):
            p_ref[...] = _dot(h, w_ref[...]).astype(BF16)

    vec = pl.BlockSpec((1, d), lambda i: (0, 0))
    row = lambda n: pl.BlockSpec((tm, n), lambda i: (i, 0))
    res = pl.pallas_call(
        _with_comm(body, comm, 4 + nw, 1 + nw, nb, -1), grid=(nb,),
        in_specs=[row(d), vec, vec, vec] + [pl.BlockSpec(w.shape, lambda i: (0, 0)) for w in weights] + c_in,
        out_specs=[row(d)] + [row(w.shape[1]) for w in weights] + c_out,
        out_shape=[jax.ShapeDtypeStruct((l, d), BF16)] + [jax.ShapeDtypeStruct((l, w.shape[1]), BF16) for w in weights]
                  + c_shape,
        scratch_shapes=c_scratch,
        compiler_params=_params("arbitrary"), name=name)(x, g, shift, scale, *weights, *c_args)
    return res[0], list(res[1:1 + nw]), list(res[1 + nw:])


def _ln_bwd(x, dh, dres, g, scale, *, name, tm=512):
    l, d = x.shape

    def body(x_ref, dh_ref, dres_ref, g_ref, sc_ref, dx_ref, sums_ref):
        xv = x_ref[...]
        dhv = dh_ref[...]
        rstd = lax.rsqrt(jnp.mean(xv * xv, axis=-1, keepdims=True) + EPS)
        n = xv * rstd
        gv = g_ref[...]
        dr = dhv * (1.0 + sc_ref[...])
        dn = dr * gv
        dx_ref[...] = dres_ref[...] + rstd * (dn - n * jnp.mean(dn * n, axis=-1, keepdims=True))

        @pl.when(pl.program_id(0) == 0)
        def _():
            sums_ref[...] = jnp.zeros_like(sums_ref)

        sums_ref[0:1, :] += jnp.sum(dhv, axis=0, keepdims=True)
        sums_ref[1:2, :] += jnp.sum(dhv * (n * gv), axis=0, keepdims=True)
        sums_ref[2:3, :] += jnp.sum(dr * n, axis=0, keepdims=True)

    vec = pl.BlockSpec((1, d), lambda i: (0, 0))
    row = pl.BlockSpec((tm, d), lambda i: (i, 0))
    return pl.pallas_call(
        body, grid=(l // tm,),
        in_specs=[row, row, row, vec, vec],
        out_specs=[row, pl.BlockSpec((8, d), lambda i: (0, 0))],
        out_shape=[jax.ShapeDtypeStruct((l, d), F32), jax.ShapeDtypeStruct((8, d), F32)],
        compiler_params=_params("arbitrary"), name=name)(x, dh, dres, g, scale)


def _final_loss(x, g, target, *, tm=512):
    l, d = x.shape

    def body(x_ref, g_ref, t_ref, dx_ref, sums_ref):
        xv = x_ref[...]
        rstd = lax.rsqrt(jnp.mean(xv * xv, axis=-1, keepdims=True) + EPS)
        n = xv * rstd
        gv = g_ref[...]
        err = n * gv - t_ref[...]
        dy = err * (1.0 / d)
        dn = dy * gv
        dx_ref[...] = rstd * (dn - n * jnp.mean(dn * n, axis=-1, keepdims=True))

        @pl.when(pl.program_id(0) == 0)
        def _():
            sums_ref[...] = jnp.zeros_like(sums_ref)

        sums_ref[0:1, :] += jnp.sum(dy * n, axis=0, keepdims=True)
        sums_ref[1:2, :] += jnp.sum(err * err, axis=0, keepdims=True) * (0.5 / d)

    vec = pl.BlockSpec((1, d), lambda i: (0, 0))
    row = pl.BlockSpec((tm, d), lambda i: (i, 0))
    dx, sums = pl.pallas_call(
        body, grid=(l // tm,),
        in_specs=[row, vec, row],
        out_specs=[row, pl.BlockSpec((8, d), lambda i: (0, 0))],
        out_shape=[jax.ShapeDtypeStruct((l, d), F32), jax.ShapeDtypeStruct((8, d), F32)],
        compiler_params=_params("arbitrary"), name="final_loss")(x, g, target)
    return dx, sums


def _attn_geometry(i, t):
    nk = t + WINDOW
    qi = lax.broadcasted_iota(jnp.int32, (t, nk), 0)
    kj = lax.broadcasted_iota(jnp.int32, (t, nk), 1)
    dist = jnp.abs(qi + WINDOW - kj).astype(F32)
    qc = jnp.right_shift(qi, 6)
    kc = jnp.right_shift(kj, 6)
    valid = (kc >= qc) & (kc <= qc + WINDOW // CHUNK) & ((i > 0) | (kj >= WINDOW))
    return dist, valid


def _attn_head(q, k_all, v_all, sink, slope, dist, valid):
    s = _dot(q, k_all, NT) * (1.0 / math.sqrt(HEAD_DIM)) - slope * dist
    s = jnp.where(valid, s, NEG_INF)
    m = jnp.maximum(jnp.max(s, axis=-1, keepdims=True), sink)
    e = jnp.exp(s - m)
    es = jnp.exp(sink - m)
    inv = 1.0 / (jnp.sum(e, axis=-1, keepdims=True) + es)
    p = e * inv
    o = _dot(p, v_all, NN)
    return p, o, es * inv


def _attn_specs(t):
    cur = pl.BlockSpec((t, ATT_W * 2 + KV_W * 2), lambda i: (i, 0))
    halo_blocks = t // WINDOW
    prev = pl.BlockSpec((WINDOW, 2 * KV_W), lambda i: (jnp.maximum(i * halo_blocks - 1, 0), (2 * ATT_W) // (2 * KV_W)))
    return cur, prev


def _attn_fwd(pa, sinks, *, name, t=ATT_BLOCK, comm=None):
    l = pa.shape[0]
    t = min(t, l)
    nb = l // t
    c_args, c_in, c_out, c_shape, c_scratch = _comm_extra(comm)

    def body(sink_ref, cur_ref, prev_ref, ya_ref):
        i = pl.program_id(0)
        dist, valid = _attn_geometry(i, t)
        for h in range(N_HEADS):
            kh = h // Q_PER_KV
            q = cur_ref[:, h * HEAD_DIM:(h + 1) * HEAD_DIM]
            z = cur_ref[:, ATT_W + h * HEAD_DIM:ATT_W + (h + 1) * HEAD_DIM].astype(F32)
            k_all = jnp.concatenate([prev_ref[:, kh * HEAD_DIM:(kh + 1) * HEAD_DIM],
                                     cur_ref[:, 2 * ATT_W + kh * HEAD_DIM:2 * ATT_W + (kh + 1) * HEAD_DIM]], axis=0)
            v_all = jnp.concatenate([prev_ref[:, KV_W + kh * HEAD_DIM:KV_W + (kh + 1) * HEAD_DIM],
                                     cur_ref[:, 2 * ATT_W + KV_W + kh * HEAD_DIM:2 * ATT_W + KV_W + (kh + 1) * HEAD_DIM]], axis=0)
            _, o, _ = _attn_head(q, k_all, v_all, sink_ref[h], 2.0 ** (-(h + 1)), dist, valid)
            sz, _ = _silu_and_grad(z)
            ya_ref[:, h * HEAD_DIM:(h + 1) * HEAD_DIM] = (o * sz).astype(BF16)

    cur, prev = _attn_specs(t)
    res = pl.pallas_call(
        _with_comm(body, comm, 3, 1, nb, nb - 1), grid=(nb,),
        in_specs=[pl.BlockSpec(memory_space=pltpu.SMEM), cur, prev] + c_in,
        out_specs=[pl.BlockSpec((t, ATT_W), lambda i: (i, 0))] + c_out,
        out_shape=[jax.ShapeDtypeStruct((l, ATT_W), BF16)] + c_shape,
        scratch_shapes=c_scratch,
        compiler_params=_params("arbitrary"), name=name)(sinks, pa, pa, *c_args)
    return res[0], res[1:]


def _attn_bwd(pa, sinks, dya, *, name, t=SEQ_BLOCK):
    l = pa.shape[0]
    t = min(t, l)
    nb = l // t
    scale = 1.0 / math.sqrt(HEAD_DIM)

    def body(sink_ref, cur_ref, prev_ref, dya_ref, dpa_ref, dsink_ref, carry_ref):
        n = pl.program_id(0)
        i = nb - 1 - n
        dist, valid = _attn_geometry(i, t)

        @pl.when(n == 0)
        def _():
            carry_ref[...] = jnp.zeros_like(carry_ref)
            dsink_ref[...] = jnp.zeros_like(dsink_ref)

        dk_acc = [jnp.zeros((HEAD_DIM, t + WINDOW), F32) for _ in range(N_KV_HEADS)]
        dv_acc = [jnp.zeros((HEAD_DIM, t + WINDOW), F32) for _ in range(N_KV_HEADS)]
        for h in range(N_HEADS):
            kh = h // Q_PER_KV
            q = cur_ref[:, h * HEAD_DIM:(h + 1) * HEAD_DIM]
            z = cur_ref[:, ATT_W + h * HEAD_DIM:ATT_W + (h + 1) * HEAD_DIM].astype(F32)
            k_all = jnp.concatenate([prev_ref[:, kh * HEAD_DIM:(kh + 1) * HEAD_DIM],
                                     cur_ref[:, 2 * ATT_W + kh * HEAD_DIM:2 * ATT_W + (kh + 1) * HEAD_DIM]], axis=0)
            v_all = jnp.concatenate([prev_ref[:, KV_W + kh * HEAD_DIM:KV_W + (kh + 1) * HEAD_DIM],
                                     cur_ref[:, 2 * ATT_W + KV_W + kh * HEAD_DIM:2 * ATT_W + KV_W + (kh + 1) * HEAD_DIM]], axis=0)
            p, o, p_sink = _attn_head(q, k_all, v_all, sink_ref[h], 2.0 ** (-(h + 1)), dist, valid)
            dy = dya_ref[:, h * HEAD_DIM:(h + 1) * HEAD_DIM]
            sz, dsz = _silu_and_grad(z)
            do = dy * sz
            dpa_ref[:, ATT_W + h * HEAD_DIM:ATT_W + (h + 1) * HEAD_DIM] = (dy * o * dsz).astype(BF16)
            delta = jnp.sum(do * o, axis=-1, keepdims=True)
            dp = _dot(do, v_all, NT)
            ds = p * (dp - delta)
            dpa_ref[:, h * HEAD_DIM:(h + 1) * HEAD_DIM] = (_dot(ds, k_all, NN) * scale).astype(BF16)
            dk_acc[kh] = dk_acc[kh] + _dot(q, ds, TN) * scale
            dv_acc[kh] = dv_acc[kh] + _dot(do, p, TN)
            dsink_ref[h:h + 1, :] += jnp.broadcast_to(-jnp.sum(p_sink * delta, axis=0, keepdims=True), (1, 128))

        acc = jnp.concatenate(dk_acc + dv_acc, axis=0).T
        own = acc[WINDOW:, :]
        tail = own[t - WINDOW:, :] + carry_ref[...]
        if t > WINDOW:
            dpa_ref[0:t - WINDOW, 2 * ATT_W:] = own[:t - WINDOW, :].astype(BF16)
        dpa_ref[t - WINDOW:t, 2 * ATT_W:] = tail.astype(BF16)
        carry_ref[...] = acc[:WINDOW, :]

    halo_blocks = t // WINDOW
    wpa = 2 * ATT_W + 2 * KV_W
    cur = pl.BlockSpec((t, wpa), lambda n: (nb - 1 - n, 0))
    prev = pl.BlockSpec((WINDOW, 2 * KV_W),
                        lambda n: (jnp.maximum((nb - 1 - n) * halo_blocks - 1, 0), (2 * ATT_W) // (2 * KV_W)))
    return pl.pallas_call(
        body, grid=(nb,),
        in_specs=[pl.BlockSpec(memory_space=pltpu.SMEM), cur, prev, pl.BlockSpec((t, ATT_W), lambda n: (nb - 1 - n, 0))],
        out_specs=[pl.BlockSpec((t, wpa), lambda n: (nb - 1 - n, 0)), pl.BlockSpec((8, 128), lambda n: (0, 0))],
        out_shape=[jax.ShapeDtypeStruct((l, wpa), BF16), jax.ShapeDtypeStruct((8, 128), F32)],
        scratch_shapes=[pltpu.VMEM((WINDOW, 2 * KV_W), F32)],
        compiler_params=_params("arbitrary"), name=name)(sinks, pa, pa, dya)


def _scan(xr, xi, lr, li, t, reverse):
    row = lax.broadcasted_iota(jnp.int32, (t, 1), 0)
    d = 1
    pr, pi = lr, li
    while d < t:
        if reverse:
            sr = jnp.where(row < t - d, pltpu.roll(xr, t - d, 0), 0.0)
            si = jnp.where(row < t - d, pltpu.roll(xi, t - d, 0), 0.0)
        else:
            sr = jnp.where(row >= d, pltpu.roll(xr, d, 0), 0.0)
            si = jnp.where(row >= d, pltpu.roll(xi, d, 0), 0.0)
        xr, xi = xr + pr * sr - pi * si, xi + pr * si + pi * sr
        pr, pi = pr * pr - pi * pi, 2.0 * pr * pi
        d *= 2
    return xr, xi


SCAN_SUB = 8


def _split_hi_lo(a):
    hi = a.astype(BF16)
    lo = (a - hi.astype(F32)).astype(BF16)
    return jnp.concatenate([hi, lo], axis=0)


def _scan_mxu(xr, xi, tab, lam3, lam8, tri, expand, cr, ci, t, reverse):
    ns = t // SCAN_SUB
    n = xr.shape[1]
    v3 = lambda a: a.reshape(ns, SCAN_SUB, n)
    x3r, x3i = v3(xr), v3(xi)
    br = (x3r * tab[0] - x3i * tab[1]).reshape(t, n)
    bi = (x3r * tab[1] + x3i * tab[0]).reshape(t, n)
    pm = jnp.dot(tri, jnp.concatenate([br, bi], axis=1).astype(BF16), preferred_element_type=F32)
    p3r, p3i = v3(pm[:t, :n]), v3(pm[:t, n:])
    slr = p3r * tab[2] - p3i * tab[3]
    sli = p3r * tab[3] + p3i * tab[2]
    totr, toti = pm[t:, :n], pm[t:, n:]
    l3r, l3i = lam3
    l8r, l8i = lam8
    row = lax.broadcasted_iota(jnp.int32, (ns, 1), 0)
    edge = row == (ns - 1 if reverse else 0)
    er = totr * l3r - toti * l3i + jnp.where(edge, l8r * cr - l8i * ci, 0.0)
    ei = totr * l3i + toti * l3r + jnp.where(edge, l8r * ci + l8i * cr, 0.0)
    er, ei = _scan(er, ei, l8r, l8i, ns, reverse)
    shift = ns - 1 if reverse else 1
    nbr = jnp.where(edge, cr, pltpu.roll(er, shift, 0))
    nbi = jnp.where(edge, ci, pltpu.roll(ei, shift, 0))
    ex = jnp.dot(expand, _split_hi_lo(jnp.concatenate([nbr, nbi], axis=1)), preferred_element_type=F32)
    e3r, e3i = v3(ex[:, :n]), v3(ex[:, n:])
    sr = (slr + e3r * tab[4] - e3i * tab[5]).reshape(t, n)
    si = (sli + e3r * tab[5] + e3i * tab[4]).reshape(t, n)
    out = 0 if reverse else ns - 1
    return sr, si, er[out:out + 1, :], ei[out:out + 1, :]


def _scan_consts(t):
    import numpy as np
    ns = t // SCAN_SUB
    r = np.arange(t)
    same = (r[:, None] // SCAN_SUB) == (r[None, :] // SCAN_SUB)
    sums = (np.arange(ns)[:, None] == (r[None, :] // SCAN_SUB))
    tri = []
    for keep in (r[None, :] <= r[:, None], r[None, :] >= r[:, None]):
        tri.append(np.concatenate([same & keep, sums], axis=0).astype(np.float32))
    ex = ((r[:, None] // SCAN_SUB) == np.arange(ns)[None, :]).astype(np.float32)
    return jnp.asarray(np.stack(tri), BF16), jnp.asarray(np.concatenate([ex, ex], axis=1), BF16)


def _scan_tables(lr, li):
    import numpy as np
    den = lr * lr + li * li
    ir, ii = lr / den, -li / den
    mul = lambda a, b: (a[0] * b[0] - a[1] * b[1], a[0] * b[1] + a[1] * b[0])
    pw = {0: (jnp.ones_like(lr), jnp.zeros_like(lr))}
    for e in range(1, 9):
        pw[e] = mul(pw[e - 1], (lr, li))
    for e in range(-1, -5, -1):
        pw[e] = mul(pw[e + 1], (ir, ii))
    powers = jnp.stack([jnp.stack(pw[e]) for e in range(-4, 9)] + [jnp.zeros((2, lr.shape[0]), F32)])
    j = np.arange(SCAN_SUB)
    exps = [4 - j, j - 4, j + 1, j - 3, 3 - j, 8 - j]
    e_idx = np.stack([exps[t] + 4 for t in range(6) for _ in range(2)])
    c_idx = np.tile(np.array([0, 1])[:, None], (6, SCAN_SUB))
    sign = np.where((c_idx == 1) & (np.arange(12)[:, None] >= 6), -1.0, 1.0).astype(np.float32)
    tabs = powers[e_idx, c_idx] * sign[:, :, None]
    lam = powers[np.array([5, 5, 7, 7, 12, 12, 13, 13]), np.array([0, 1, 0, 1, 0, 1, 0, 0])]
    return lam, tabs


SSM_HALVES = 2
SSM_HW = SSM_W // SSM_HALVES
SSM_HN = SSM_N // SSM_HALVES


def _bd_nn(x, w):
    a = w.shape[1]
    return jnp.concatenate([_dot(x[:, h * a:(h + 1) * a], w[h]) for h in range(SSM_HALVES)], axis=1)


def _bd_nt(x, w):
    b = w.shape[2]
    return jnp.concatenate([_dot(x[:, h * b:(h + 1) * b], w[h], NT) for h in range(SSM_HALVES)], axis=1)


def _bd_tn(x, y):
    a, b = x.shape[1] // SSM_HALVES, y.shape[1] // SSM_HALVES
    return jnp.stack([_dot(x[:, h * a:(h + 1) * a], y[:, h * b:(h + 1) * b], TN) for h in range(SSM_HALVES)])


def _ssm_states(u, s0r, s0i, lam_ref, tab_ref, tri_ref, ex_ref, bre, bim, t):
    tab = tuple(tab_ref[k] for k in range(6))
    return _scan_mxu(_bd_nn(u, bre), _bd_nn(u, bim), tab, (lam_ref[2:3, :], lam_ref[3:4, :]),
                     (lam_ref[4:5, :], lam_ref[5:6, :]), tri_ref[0], ex_ref[...], s0r, s0i, t, False)


def _ssm_head(u, z, xr, xi, cre, cim, dskip, wglu, bglu):
    y = _bd_nn(xr, cre) - _bd_nn(xi, cim) + dskip * u
    y2, dgelu = _gelu_and_grad(y)
    gate = _sigmoid(_dot(y2, wglu) + bglu)
    y3 = y2 * gate
    return y2, dgelu, gate, y3


def _with_comm(body, comm, n_in, n_out, grid, mid_step):
    if comm is None:
        return body
    nc = len(comm["args"])
    n_sem = len(comm["scratch"])
    grid = (grid,) if isinstance(grid, int) else tuple(grid)
    total = math.prod(grid)

    def hosted(*refs):
        ins, cin = refs[:n_in], refs[n_in:n_in + nc]
        outs, cout = refs[n_in + nc:n_in + nc + n_out], refs[n_in + nc + n_out:n_in + 2 * nc + n_out]
        rest = refs[n_in + 2 * nc + n_out:]
        scratch, csem = rest[:len(rest) - n_sem], rest[len(rest) - n_sem:]
        start, forward, finish = comm["phases"](cin, cout, *csem)
        step = pl.program_id(0)
        for axis in range(1, len(grid)):
            step = step * grid[axis] + pl.program_id(axis)
        pl.when(step == 0)(start)
        pl.when(step == (mid_step if mid_step >= 0 else total + mid_step))(forward)
        body(*ins, *outs, *scratch)
        pl.when(step == total - 1)(finish)

    return hosted


def _comm_extra(comm):
    if comm is None:
        return [], [], [], [], []
    anyspec = pl.BlockSpec(memory_space=pl.ANY)
    nc = len(comm["args"])
    return comm["args"], [anyspec] * nc, [anyspec] * nc, comm["out_shape"], comm["scratch"]


def _ssm_fwd(ps, scan_ops, bblk, cblk, dskip, wglu, bglu, *, name, t=SEQ_BLOCK, comm=None):
    l = ps.shape[0]
    assert l % t == 0
    nb = l // t
    ns = t // SCAN_SUB
    c_args, c_in, c_out, c_shape, c_scratch = _comm_extra(comm)

    def body(ps_ref, lam_ref, tab_ref, tri_ref, ex_ref, b_ref, c_ref, d_ref, w_ref, bg_ref, ys_ref, chk_ref, xs_ref,
             st_ref):
        @pl.when(pl.program_id(0) == 0)
        def _():
            st_ref[...] = jnp.zeros_like(st_ref)

        chk_ref[...] = jnp.broadcast_to(st_ref[...], chk_ref.shape)
        u = ps_ref[:, :SSM_W].astype(F32)
        z = ps_ref[:, SSM_W:].astype(F32)
        xr, xi, er, ei = _ssm_states(u, st_ref[:, :SSM_N], st_ref[:, SSM_N:], lam_ref, tab_ref, tri_ref, ex_ref,
                                     b_ref[0], b_ref[1], t)
        st_ref[:, :SSM_N] = er
        st_ref[:, SSM_N:] = ei
        xr, xi = xr.astype(BF16), xi.astype(BF16)
        xs_ref[:, :SSM_N] = xr
        xs_ref[:, SSM_N:] = xi
        _, _, _, y3 = _ssm_head(u, z, xr, xi, c_ref[0], c_ref[1], d_ref[...], w_ref[...], bg_ref[...])
        sz, _ = _silu_and_grad(z)
        ys_ref[...] = (y3 * sz).astype(BF16)

    full = lambda shape: pl.BlockSpec(shape, lambda i: (0,) * len(shape))
    return pl.pallas_call(
        _with_comm(body, comm, 10, 3, nb, nb - 1), grid=(nb,),
        in_specs=[pl.BlockSpec((t, 2 * SSM_W), lambda i: (i, 0)), full((8, SSM_N)), full((12, SCAN_SUB, SSM_N)),
                  full((2, t + ns, t)), full((t, 2 * ns)), full((2, SSM_HALVES, SSM_HW, SSM_HN)),
                  full((2, SSM_HALVES, SSM_HN, SSM_HW)), full((1, SSM_W)), full((SSM_W, SSM_W)), full((1, SSM_W))] + c_in,
        out_specs=[pl.BlockSpec((t, SSM_W), lambda i: (i, 0)), pl.BlockSpec((8, 2 * SSM_N), lambda i: (i, 0)),
                   pl.BlockSpec((t, 2 * SSM_N), lambda i: (i, 0))] + c_out,
        out_shape=[jax.ShapeDtypeStruct((l, SSM_W), BF16), jax.ShapeDtypeStruct((nb * 8, 2 * SSM_N), F32),
                   jax.ShapeDtypeStruct((l, 2 * SSM_N), BF16)] + c_shape,
        scratch_shapes=[pltpu.VMEM((1, 2 * SSM_N), F32)] + c_scratch,
        compiler_params=_params("arbitrary"), name=name)(ps, *scan_ops, bblk, cblk, dskip, wglu, bglu, *c_args)


def _ssm_bwd(ps, dys, chk, states, scan_ops, bblk, cblk, dskip, wglu, bglu, *, name, t=SEQ_BLOCK, comm=None):
    l = ps.shape[0]
    assert l % t == 0
    nb = l // t
    ns = t // SCAN_SUB
    c_args, c_in, c_out, c_shape, c_scratch = _comm_extra(comm)

    def body(ps_ref, dys_ref, chk_ref, xs_ref, lam_ref, tab_ref, tri_ref, ex_ref, b_ref, c_ref, d_ref, w_ref, bg_ref,
             dps_ref, db_ref, dc_ref, dw_acc, sums_acc, gc_ref, db_acc, dc_acc):
        n = pl.program_id(0)

        @pl.when(n == 0)
        def _():
            gc_ref[...] = jnp.zeros_like(gc_ref)
            db_acc[...] = jnp.zeros_like(db_acc)
            dc_acc[...] = jnp.zeros_like(dc_acc)
            dw_acc[...] = jnp.zeros_like(dw_acc)
            sums_acc[...] = jnp.zeros_like(sums_acc)

        row = lax.broadcasted_iota(jnp.int32, (t, 1), 0)
        u = ps_ref[:, :SSM_W].astype(F32)
        z = ps_ref[:, SSM_W:].astype(F32)
        s0r, s0i = chk_ref[0:1, :SSM_N], chk_ref[0:1, SSM_N:]
        xr, xi = xs_ref[:, :SSM_N], xs_ref[:, SSM_N:]
        dskip = d_ref[...]
        y2, dgelu, gate, y3 = _ssm_head(u, z, xr, xi, c_ref[0], c_ref[1], dskip, w_ref[...], bg_ref[...])
        sz, dsz = _silu_and_grad(z)
        dys_v = dys_ref[...]
        dps_ref[:, SSM_W:] = (dys_v * y3 * dsz).astype(BF16)
        dy3 = dys_v * sz
        da = dy3 * y2 * gate * (1.0 - gate)
        dy2 = dy3 * gate + _dot(da, w_ref[...], NT)
        dw_acc[...] += _dot(y2, da, TN)
        dy = dy2 * dgelu
        sums_acc[2:3, :SSM_W] += jnp.sum(dy * u, axis=0, keepdims=True)
        sums_acc[3:4, :SSM_W] += jnp.sum(da, axis=0, keepdims=True)
        dc_acc[0] += _bd_tn(dy, xr)
        dc_acc[1] += -_bd_tn(dy, xi)
        rev_tab = tuple(tab_ref[k] for k in range(6, 12))
        gr, gi, gcr, gci = _scan_mxu(
            _bd_nt(dy, c_ref[0]), -_bd_nt(dy, c_ref[1]), rev_tab, (lam_ref[2:3, :], -lam_ref[3:4, :]),
            (lam_ref[4:5, :], -lam_ref[5:6, :]), tri_ref[1], ex_ref[...], gc_ref[:, :SSM_N], gc_ref[:, SSM_N:], t, True)
        gc_ref[:, :SSM_N] = gcr
        gc_ref[:, SSM_N:] = gci
        db_acc[0] += _bd_tn(u, gr)
        db_acc[1] += _bd_tn(u, gi)
        du = dskip * dy + _bd_nt(gr, b_ref[0]) + _bd_nt(gi, b_ref[1])
        dps_ref[:, :SSM_W] = du.astype(BF16)
        spr = jnp.where(row == 0, s0r, pltpu.roll(xr.astype(F32), 1, 0))
        spi = jnp.where(row == 0, s0i, pltpu.roll(xi.astype(F32), 1, 0))
        sums_acc[0:1, :] += jnp.sum(gr * spr + gi * spi, axis=0, keepdims=True)
        sums_acc[1:2, :] += jnp.sum(gi * spr - gr * spi, axis=0, keepdims=True)

        @pl.when(n == nb - 1)
        def _():
            per_half = SSM_GROUPS // SSM_HALVES
            for k in range(2):
                for g in range(SSM_GROUPS):
                    h, gl = divmod(g, per_half)
                    c0, p0 = gl * SSM_GROUP, gl * SSM_STATE
                    db_ref[k, g * SSM_GROUP:(g + 1) * SSM_GROUP, :] = db_acc[k, h, c0:c0 + SSM_GROUP, p0:p0 + SSM_STATE]
                    dc_ref[k, g * SSM_GROUP:(g + 1) * SSM_GROUP, :] = dc_acc[k, h, c0:c0 + SSM_GROUP, p0:p0 + SSM_STATE]

    full = lambda shape: pl.BlockSpec(shape, lambda n: (0,) * len(shape))
    return pl.pallas_call(
        _with_comm(body, comm, 13, 5, nb, 0), grid=(nb,),
        in_specs=[pl.BlockSpec((t, 2 * SSM_W), lambda n: (nb - 1 - n, 0)),
                  pl.BlockSpec((t, SSM_W), lambda n: (nb - 1 - n, 0)),
                  pl.BlockSpec((8, 2 * SSM_N), lambda n: (nb - 1 - n, 0)),
                  pl.BlockSpec((t, 2 * SSM_N), lambda n: (nb - 1 - n, 0)),
                  full((8, SSM_N)), full((12, SCAN_SUB, SSM_N)), full((2, t + ns, t)), full((t, 2 * ns)),
                  full((2, SSM_HALVES, SSM_HW, SSM_HN)), full((2, SSM_HALVES, SSM_HN, SSM_HW)), full((1, SSM_W)),
                  full((SSM_W, SSM_W)), full((1, SSM_W))] + c_in,
        out_specs=[pl.BlockSpec((t, 2 * SSM_W), lambda n: (nb - 1 - n, 0)), full((2, SSM_W, SSM_STATE)),
                   full((2, SSM_W, SSM_STATE)), full((SSM_W, SSM_W)), full((8, SSM_N))] + c_out,
        out_shape=[jax.ShapeDtypeStruct((l, 2 * SSM_W), BF16),
                   jax.ShapeDtypeStruct((2, SSM_W, SSM_STATE), F32),
                   jax.ShapeDtypeStruct((2, SSM_W, SSM_STATE), F32),
                   jax.ShapeDtypeStruct((SSM_W, SSM_W), F32),
                   jax.ShapeDtypeStruct((8, SSM_N), F32)] + c_shape,
        scratch_shapes=[pltpu.VMEM((1, 2 * SSM_N), F32), pltpu.VMEM((2, SSM_HALVES, SSM_HW, SSM_HN), F32),
                        pltpu.VMEM((2, SSM_HALVES, SSM_HW, SSM_HN), F32)] + c_scratch,
        compiler_params=_params("arbitrary"), name=name)(ps, dys, chk, states, *scan_ops, bblk, cblk, dskip, wglu, bglu,
                                                         *c_args)


def _pool_count(i, t):
    pos = lax.broadcasted_iota(jnp.int32, (t, POOL_W), 0) + i * t + 1
    col = lax.broadcasted_iota(jnp.int32, (t, POOL_W), 1)
    win = jnp.where(col < POOL_GW, 2, jnp.where(col < 2 * POOL_GW, 4, jnp.where(col < 3 * POOL_GW, 8, 16)))
    return 1.0 / jnp.minimum(pos, win).astype(F32), col


def _window_sums(ext, n_rows, forward):
    col = lax.broadcasted_iota(jnp.int32, ext.shape, 1)
    sh = (lambda a, d: pltpu.roll(a, d, 0)) if forward else (lambda a, d: pltpu.roll(a, n_rows - d, 0))
    a2 = ext + sh(ext, 1)
    a4 = a2 + sh(a2, 2)
    a8 = a4 + sh(a4, 4)
    a16 = a8 + sh(a8, 8)
    return jnp.where(col < POOL_GW, a2, jnp.where(col < 2 * POOL_GW, a4, jnp.where(col < 3 * POOL_GW, a8, a16)))


def _pool_mix(pooled, wp_ref):
    return jnp.concatenate([_dot(pooled[:, g * POOL_GW:(g + 1) * POOL_GW], wp_ref[g]) for g in range(4)], axis=1)


def _pool_pooled(i, cur_u, prev_u, t):
    prev = jnp.where(i > 0, prev_u, 0.0)
    ext = jnp.concatenate([prev, cur_u], axis=0)
    inv_cnt, _ = _pool_count(i, t)
    return _window_sums(ext, t + POOL_HALO, True)[POOL_HALO:, :] * inv_cnt - cur_u


def _pool_fwd(pp, wpool, pscale, *, name, t=SEQ_BLOCK):
    l = pp.shape[0]
    t = min(t, l)

    def body(cur_ref, prev_ref, wp_ref, sc_ref, yp_ref):
        i = pl.program_id(0)
        pooled = _pool_pooled(i, cur_ref[:, :POOL_W].astype(F32), prev_ref[...].astype(F32), t)
        lin = _pool_mix(pooled, wp_ref)
        sz, _ = _silu_and_grad(cur_ref[:, POOL_W:].astype(F32))
        yp_ref[...] = (lin * sc_ref[...] * sz).astype(BF16)

    hb = t // POOL_HALO
    return pl.pallas_call(
        body, grid=(l // t,),
        in_specs=[pl.BlockSpec((t, 2 * POOL_W), lambda i: (i, 0)),
                  pl.BlockSpec((POOL_HALO, POOL_W), lambda i: (jnp.maximum(i * hb - 1, 0), 0)),
                  pl.BlockSpec((4, POOL_GW, POOL_GW), lambda i: (0, 0, 0)),
                  pl.BlockSpec((1, POOL_W), lambda i: (0, 0))],
        out_specs=pl.BlockSpec((t, POOL_W), lambda i: (i, 0)),
        out_shape=jax.ShapeDtypeStruct((l, POOL_W), BF16),
        compiler_params=_params("parallel"), name=name)(pp, pp, wpool, pscale)


def _pool_bwd(pp, dyp, wpool, pscale, *, name, t=SEQ_BLOCK):
    l = pp.shape[0]
    t = min(t, l)
    nb = l // t

    def body(cur_ref, prev_ref, dyp_ref, wp_ref, sc_ref, dpp_ref, dwp_ref, sums_ref, carry_ref):
        n = pl.program_id(0)
        i = nb - 1 - n

        @pl.when(n == 0)
        def _():
            carry_ref[...] = jnp.zeros_like(carry_ref)
            dwp_ref[...] = jnp.zeros_like(dwp_ref)
            sums_ref[...] = jnp.zeros_like(sums_ref)

        cur_u = cur_ref[:, :POOL_W].astype(F32)
        pooled = _pool_pooled(i, cur_u, prev_ref[...].astype(F32), t)
        lin = _pool_mix(pooled, wp_ref)
        sz, dsz = _silu_and_grad(cur_ref[:, POOL_W:].astype(F32))
        dyp_v = dyp_ref[...]
        scale = sc_ref[...]
        dpp_ref[:, POOL_W:] = (dyp_v * lin * scale * dsz).astype(BF16)
        dpre = dyp_v * sz
        sums_ref[0:1, :] += jnp.sum(dpre * lin, axis=0, keepdims=True)
        dlin = dpre * scale
        dpooled = []
        for g in range(4):
            dl = dlin[:, g * POOL_GW:(g + 1) * POOL_GW]
            dwp_ref[g] += _dot(pooled[:, g * POOL_GW:(g + 1) * POOL_GW], dl, TN)
            dpooled.append(_dot(dl, wp_ref[g], NT))
        dpooled = jnp.concatenate(dpooled, axis=1)
        inv_cnt, _ = _pool_count(i, t)
        dq = dpooled * inv_cnt
        ext = jnp.concatenate([dq, carry_ref[...]], axis=0)
        du = _window_sums(ext, t + POOL_HALO, False)[:t, :] - dpooled
        dpp_ref[:, :POOL_W] = du.astype(BF16)
        carry_ref[...] = dq[:POOL_HALO, :]

    hb = t // POOL_HALO
    return pl.pallas_call(
        body, grid=(nb,),
        in_specs=[pl.BlockSpec((t, 2 * POOL_W), lambda n: (nb - 1 - n, 0)),
                  pl.BlockSpec((POOL_HALO, POOL_W), lambda n: (jnp.maximum((nb - 1 - n) * hb - 1, 0), 0)),
                  pl.BlockSpec((t, POOL_W), lambda n: (nb - 1 - n, 0)),
                  pl.BlockSpec((4, POOL_GW, POOL_GW), lambda n: (0, 0, 0)),
                  pl.BlockSpec((1, POOL_W), lambda n: (0, 0))],
        out_specs=[pl.BlockSpec((t, 2 * POOL_W), lambda n: (nb - 1 - n, 0)),
                   pl.BlockSpec((4, POOL_GW, POOL_GW), lambda n: (0, 0, 0)),
                   pl.BlockSpec((8, POOL_W), lambda n: (0, 0))],
        out_shape=[jax.ShapeDtypeStruct((l, 2 * POOL_W), BF16), jax.ShapeDtypeStruct((4, POOL_GW, POOL_GW), F32),
                   jax.ShapeDtypeStruct((8, POOL_W), F32)],
        scratch_shapes=[pltpu.VMEM((POOL_HALO, POOL_W), F32)],
        compiler_params=_params("arbitrary"), name=name)(pp, pp, dyp, wpool, pscale)


def _merge_fwd(ya, ys, yp, wa, ws, wp, pg, wout, x, gate, *, name, tm=512):
    l, d = x.shape
    tm = min(tm, l)

    def body(ya_ref, ys_ref, yp_ref, wa_ref, ws_ref, wp_ref, pg_ref, wo_ref, x_ref, g_ref,
             xn_ref, mg_ref, ba_ref, bs_ref, bp_ref, out_ref):
        acc = None
        for k, (y_ref, w_ref, b_ref) in enumerate(((ya_ref, wa_ref, ba_ref), (ys_ref, ws_ref, bs_ref),
                                                   (yp_ref, wp_ref, bp_ref))):
            br = _dot(y_ref[...], w_ref[...])
            b_ref[...] = br.astype(BF16)
            term = _sigmoid(pg_ref[:, k * d:(k + 1) * d].astype(F32)) * br
            acc = term if acc is None else acc + term
        merged = acc.astype(BF16)
        mg_ref[...] = merged
        out = _dot(merged, wo_ref[...])
        out_ref[...] = out.astype(BF16)
        xn_ref[...] = x_ref[...] + g_ref[...] * out

    rowy = pl.BlockSpec((tm, ATT_W), lambda i: (i, 0))
    wsp = pl.BlockSpec((ATT_W, d), lambda i: (0, 0))
    rowd = pl.BlockSpec((tm, d), lambda i: (i, 0))
    return pl.pallas_call(
        body, grid=(l // tm,),
        in_specs=[rowy, rowy, rowy, wsp, wsp, wsp, pl.BlockSpec((tm, 3 * d), lambda i: (i, 0)),
                  pl.BlockSpec((d, d), lambda i: (0, 0)), rowd, pl.BlockSpec((1, d), lambda i: (0, 0))],
        out_specs=[rowd] * 6,
        out_shape=[jax.ShapeDtypeStruct((l, d), F32)] + [jax.ShapeDtypeStruct((l, d), BF16)] * 5,
        compiler_params=_params("parallel"), name=name)(ya, ys, yp, wa, ws, wp, pg, wout, x, gate)


def _merge_bwd(dx, out, gate, wout, pg, brs, wbrs, ys, merged, *, name, tm=256, comm=None):
    l, d = dx.shape
    tm = min(tm, l)
    nb = l // tm
    w = ys[0].shape[1]

    def body(dx_ref, out_ref, g_ref, w_ref, pg_ref, ba_ref, bs_ref, bp_ref, wa_ref, ws_ref, wp_ref,
             ya_ref, ys_ref, yp_ref, mg_ref,
             dya_ref, dys_ref, dyp_ref, dpg_ref, sums_ref, dwa_ref, dws_ref, dwp_ref, dwo_ref, acc_br, acc_out):
        i = pl.program_id(0)

        @pl.when(i == 0)
        def _():
            sums_ref[...] = jnp.zeros_like(sums_ref)
            acc_br[...] = jnp.zeros_like(acc_br)
            acc_out[...] = jnp.zeros_like(acc_out)

        dxv = dx_ref[...]
        sums_ref[0:1, :] += jnp.sum(dxv * out_ref[...].astype(F32), axis=0, keepdims=True)
        dmo = (dxv * g_ref[...]).astype(BF16)
        acc_out[...] += _dot(mg_ref[...], dmo, TN)
        dmerged = _dot(dmo, w_ref[...], NT)
        branches = ((ba_ref, wa_ref, ya_ref, dya_ref), (bs_ref, ws_ref, ys_ref, dys_ref), (bp_ref, wp_ref, yp_ref, dyp_ref))
        for k, (b_ref, wk_ref, y_ref, dy_ref) in enumerate(branches):
            gk = _sigmoid(pg_ref[:, k * d:(k + 1) * d].astype(F32))
            dbr = (dmerged * gk).astype(BF16)
            dpg_ref[:, k * d:(k + 1) * d] = (dmerged * b_ref[...].astype(F32) * gk * (1.0 - gk)).astype(BF16)
            dy_ref[...] = _dot(dbr, wk_ref[...], NT)
            acc_br[k] += _dot(y_ref[...], dbr, TN)

        @pl.when(i == nb - 1)
        def _():
            for k, dw_ref in enumerate((dwa_ref, dws_ref, dwp_ref)):
                dw_ref[...] = acc_br[k].astype(BF16)
            dwo_ref[...] = acc_out[...].astype(BF16)

    row = pl.BlockSpec((tm, d), lambda i: (i, 0))
    half = pl.BlockSpec((tm, w), lambda i: (i, 0))
    wide = pl.BlockSpec((tm, 3 * d), lambda i: (i, 0))
    const = lambda shape: pl.BlockSpec(shape, lambda i: (0,) * len(shape))
    c_args, c_in, c_out, c_shape, c_scratch = _comm_extra(comm)
    res = pl.pallas_call(
        _with_comm(body, comm, 15, 9, nb, 0), grid=(nb,),
        in_specs=[row, row, const((1, d)), const((d, d)), wide, row, row, row, const((w, d)), const((w, d)), const((w, d)),
                  half, half, half, row] + c_in,
        out_specs=[half, half, half, wide, const((8, d)), const((w, d)), const((w, d)), const((w, d)), const((d, d))] + c_out,
        out_shape=[jax.ShapeDtypeStruct((l, w), F32)] * 3 + [jax.ShapeDtypeStruct((l, 3 * d), BF16),
                                                             jax.ShapeDtypeStruct((8, d), F32)]
                  + [jax.ShapeDtypeStruct((w, d), BF16)] * 3 + [jax.ShapeDtypeStruct((d, d), BF16)] + c_shape,
        scratch_shapes=[pltpu.VMEM((3, w, d), F32), pltpu.VMEM((d, d), F32)] + c_scratch,
        compiler_params=_params("arbitrary"), name=name)(dx, out, gate, wout, pg, *brs, *wbrs, *ys, merged, *c_args)
    return list(res[:9]), list(res[9:])


def _adamw(w, gs, m, v, *, name, tr=256):
    r, c = w.shape
    ns = len(gs)
    p, rs, _ = gs[0].shape
    assert rs * ns == r
    tr = min(tr, rs)
    assert rs % tr == 0
    nr = rs // tr
    c1 = 1.0 / (1.0 - ADAM_B1 ** ADAM_STEP)
    c2 = 1.0 / (1.0 - ADAM_B2 ** ADAM_STEP)

    def body(*refs):
        w_ref, g_refs, (m_ref, v_ref, go_ref, d_ref, mo_ref, vo_ref) = refs[0], refs[1:1 + ns], refs[1 + ns:]
        slab = pl.program_id(0)
        gv = None
        for k, g_ref in enumerate(g_refs):
            gk = g_ref[0].astype(F32)
            for j in range(1, p):
                gk = gk + g_ref[j].astype(F32)
            gv = gk if gv is None else jnp.where(slab == k, gk, gv)
        go_ref[...] = gv
        mn = ADAM_B1 * m_ref[...] + (1.0 - ADAM_B1) * gv
        vn = ADAM_B2 * v_ref[...] + (1.0 - ADAM_B2) * (gv * gv)
        mo_ref[...] = mn
        vo_ref[...] = vn
        d_ref[...] = -ADAM_LR * ((mn * c1) / (jnp.sqrt(vn * c2) + ADAM_EPS) + ADAM_WD * w_ref[...])

    row = pl.BlockSpec((tr, c), lambda s, i: (s * nr + i, 0))
    g_specs = [pl.BlockSpec((p, tr, c), lambda s, i, k=k: (0, jnp.where(s == k, i, 0), 0)) for k in range(ns)]
    return pl.pallas_call(
        body, grid=(ns, nr),
        in_specs=[row] + g_specs + [row, row],
        out_specs=[row] * 4,
        out_shape=[jax.ShapeDtypeStruct((r, c), F32)] * 4,
        compiler_params=_params("arbitrary", "arbitrary"), name=name)(w, *gs, m, v)


def _exchange(arrs, *, scatter, name):
    n = len(arrs)
    out_shape = [jax.ShapeDtypeStruct(a.shape if scatter else (N_DEV,) + a.shape, a.dtype) for a in arrs]

    def body(*refs):
        ins, outs = refs[:n], refs[n:2 * n]
        send_sems, recv_sems, loc_sems = refs[2 * n:]
        me = 4 * lax.axis_index("x") + 2 * lax.axis_index("y") + lax.axis_index("c")
        local = []
        for k in range(n):
            src = ins[k].at[me] if scatter else ins[k]
            cp = pltpu.make_async_copy(src, outs[k].at[me], loc_sems.at[k])
            cp.start()
            local.append(cp)
        remote = []
        for r in range(1, N_DEV):
            peer = me ^ r
            for k in range(n):
                src = ins[k].at[peer] if scatter else ins[k]
                cp = pltpu.make_async_remote_copy(
                    src_ref=src, dst_ref=outs[k].at[me], send_sem=send_sems.at[k, r - 1], recv_sem=recv_sems.at[k, r - 1],
                    device_id=(peer // 4, (peer // 2) % 2, peer % 2), device_id_type=pl.DeviceIdType.MESH)
                cp.start()
                remote.append(cp)
        for cp in remote:
            cp.wait()
        for cp in local:
            cp.wait()

    anyspec = pl.BlockSpec(memory_space=pl.ANY)
    return pl.pallas_call(
        body, in_specs=[anyspec] * n, out_specs=[anyspec] * n, out_shape=out_shape,
        scratch_shapes=[pltpu.SemaphoreType.DMA((n, N_DEV - 1)), pltpu.SemaphoreType.DMA((n, N_DEV - 1)),
                        pltpu.SemaphoreType.DMA((n,))],
        name=name)(*arrs)


def _mesh_place():
    x, y, c = lax.axis_index("x"), lax.axis_index("y"), lax.axis_index("c")
    other_chips = [(1 - x, y), (x, 1 - y), (1 - x, 1 - y)]
    return x, y, c, other_chips


def _gather_two_level(arrs, *, name):
    n = len(arrs)
    plan = _gather_plan(arrs)

    def body(*refs):
        start, forward, finish = plan["phases"](refs[:n], refs[n:2 * n], *refs[2 * n:])
        start()
        forward()
        finish()

    anyspec = pl.BlockSpec(memory_space=pl.ANY)
    return pl.pallas_call(
        body, in_specs=[anyspec] * n, out_specs=[anyspec] * n, out_shape=plan["out_shape"],
        scratch_shapes=plan["scratch"], name=name)(*arrs)


def _gather_plan(arrs):
    n = len(arrs)

    def phases(ins, outs, send_sems, recv_sems, loc_sems):
        x, y, c, chips = _mesh_place()
        me = 4 * x + 2 * y + c
        slot = lambda px, py, pc: 4 * px + 2 * py + pc

        def copy(k, j, src, block, to):
            return pltpu.make_async_remote_copy(
                src_ref=src, dst_ref=outs[k].at[block], send_sem=send_sems.at[k, j], recv_sem=recv_sems.at[k, j],
                device_id=to, device_id_type=pl.DeviceIdType.MESH)

        local = [pltpu.make_async_copy(ins[k], outs[k].at[me], loc_sems.at[k]) for k in range(n)]
        first = []
        for k in range(n):
            first.append(copy(k, 0, ins[k], me, (x, y, 1 - c)))
            for j, chip in enumerate(chips):
                first.append(copy(k, 1 + j, ins[k], me, (*chip, c)))
        passed = [copy(k, 4 + j, outs[k].at[slot(*chip, c)], slot(*chip, c), (x, y, 1 - c))
                  for j, chip in enumerate(chips) for k in range(n)]

        def start():
            for cp in local + first:
                cp.start()

        def forward():
            for j, chip in enumerate(chips):
                for k in range(n):
                    copy(k, 1 + j, ins[k], slot(*chip, c), (x, y, c)).wait_recv()
                    passed[j * n + k].start()

        def finish():
            for k in range(n):
                copy(k, 0, ins[k], slot(x, y, 1 - c), (x, y, c)).wait_recv()
                for j, chip in enumerate(chips):
                    copy(k, 4 + j, ins[k], slot(*chip, 1 - c), (x, y, c)).wait_recv()
            for cp in first + passed:
                cp.wait_send()
            for cp in local:
                cp.wait()

        return start, forward, finish

    return dict(
        args=list(arrs), out_shape=[jax.ShapeDtypeStruct((N_DEV,) + a.shape, a.dtype) for a in arrs],
        scratch=[pltpu.SemaphoreType.DMA((n, 7)), pltpu.SemaphoreType.DMA((n, 7)), pltpu.SemaphoreType.DMA((n,))],
        phases=phases)


def _allreduce_small(small, extra, *, name):
    r, lanes = small.shape
    assert r % 16 == 0
    h = r // 2
    e = extra.shape[0]

    def body(s_ref, x_ref, out_ref, xall_ref, sib_ref, parts_ref, send_sems, recv_sems):
        x, y, c, chips = _mesh_place()
        me = 4 * x + 2 * y + c
        my_chip = 2 * x + y
        sibling = (x, y, 1 - c)
        mine = pl.ds(pl.multiple_of(c * h, 8), h)
        theirs = pl.ds(pl.multiple_of((1 - c) * h, 8), h)

        def remote(j, src, dst, to):
            return pltpu.make_async_remote_copy(src_ref=src, dst_ref=dst, send_sem=send_sems.at[j],
                                                recv_sem=recv_sems.at[j], device_id=to, device_id_type=pl.DeviceIdType.MESH)

        to_sibling = remote(0, s_ref.at[theirs], sib_ref, sibling)
        to_sibling.start()
        xall_ref[me] = x_ref[...]
        extras = []
        for rr in range(1, N_DEV):
            peer = me ^ rr
            cp = remote(4 + rr, x_ref, xall_ref.at[me], (peer // 4, (peer // 2) % 2, peer % 2))
            cp.start()
            extras.append(cp)
        to_sibling.wait_recv()
        parts_ref[my_chip] = s_ref[mine] + sib_ref[...]
        to_chips = [remote(1 + j, parts_ref.at[my_chip], parts_ref.at[my_chip], (px, py, c))
                    for j, (px, py) in enumerate(chips)]
        for cp in to_chips:
            cp.start()
        for cp in to_chips:
            cp.wait_recv()
        out_ref[mine] = (parts_ref[0] + parts_ref[1]) + (parts_ref[2] + parts_ref[3])
        done = remote(4, out_ref.at[mine], out_ref.at[mine], sibling)
        done.start()
        remote(4, out_ref.at[theirs], out_ref.at[theirs], sibling).wait_recv()
        for cp in extras:
            cp.wait()
        to_sibling.wait_send()
        for cp in to_chips:
            cp.wait_send()
        done.wait_send()

    vmem = pl.BlockSpec(memory_space=pltpu.VMEM)
    return pl.pallas_call(
        body, in_specs=[vmem, vmem], out_specs=[vmem, vmem],
        out_shape=[jax.ShapeDtypeStruct((r, lanes), F32), jax.ShapeDtypeStruct((N_DEV, e, lanes), F32)],
        scratch_shapes=[pltpu.VMEM((h, lanes), F32), pltpu.VMEM((4, h, lanes), F32),
                        pltpu.SemaphoreType.DMA((12,)), pltpu.SemaphoreType.DMA((12,))],
        compiler_params=pltpu.CompilerParams(vmem_limit_bytes=VMEM_LIMIT), name=name)(small, extra)


def _sibling_swap(arrs, *, name):
    n = len(arrs)
    plan = _sibling_swap_plan(arrs)

    def body(*refs):
        start, _, finish = plan["phases"](refs[:n], refs[n:2 * n], *refs[2 * n:])
        start()
        finish()

    anyspec = pl.BlockSpec(memory_space=pl.ANY)
    return pl.pallas_call(
        body, in_specs=[anyspec] * n, out_specs=[anyspec] * n, out_shape=plan["out_shape"],
        scratch_shapes=plan["scratch"], name=name)(*arrs)


def _sibling_swap_plan(arrs):
    n = len(arrs)

    def phases(ins, outs, send_sems, recv_sems):
        x, y, c, _ = _mesh_place()
        copies = [pltpu.make_async_remote_copy(
            src_ref=ins[k].at[1 - c], dst_ref=outs[k], send_sem=send_sems.at[k], recv_sem=recv_sems.at[k],
            device_id=(x, y, 1 - c), device_id_type=pl.DeviceIdType.MESH) for k in range(n)]

        def start():
            for cp in copies:
                cp.start()

        def finish():
            for cp in copies:
                cp.wait()

        return start, (lambda: None), finish

    return dict(args=list(arrs), out_shape=[jax.ShapeDtypeStruct(a.shape[1:], a.dtype) for a in arrs],
                scratch=[pltpu.SemaphoreType.DMA((n,)), pltpu.SemaphoreType.DMA((n,))], phases=phases)


def _pair_add(mine, theirs, core, *, name, tr=256):
    _, r, c = mine.shape
    tr = min(tr, r)
    assert r % tr == 0

    def body(core_ref, m_ref, t_ref, o_ref):
        o_ref[...] = (m_ref[0].astype(F32) + t_ref[...].astype(F32)).astype(BF16)

    return pl.pallas_call(
        body,
        grid_spec=pltpu.PrefetchScalarGridSpec(
            num_scalar_prefetch=1, grid=(r // tr,),
            in_specs=[pl.BlockSpec((1, tr, c), lambda i, core_ref: (core_ref[0], i, 0)),
                      pl.BlockSpec((tr, c), lambda i, core_ref: (i, 0))],
            out_specs=pl.BlockSpec((tr, c), lambda i, core_ref: (i, 0))),
        out_shape=jax.ShapeDtypeStruct((r, c), BF16),
        compiler_params=_params("parallel"), name=name)(core, mine, theirs)


def _pair_add_small(mines, theirs, core, *, name):
    n = len(mines)

    def body(core_ref, *refs):
        for m_ref, t_ref, o_ref in zip(refs[:n], refs[n:2 * n], refs[2 * n:]):
            o_ref[...] = (m_ref[0].astype(F32) + t_ref[...].astype(F32)).astype(BF16)

    whole = lambda a: pl.BlockSpec(a.shape, lambda i, core_ref: (0,) * a.ndim)
    return pl.pallas_call(
        body,
        grid_spec=pltpu.PrefetchScalarGridSpec(
            num_scalar_prefetch=1, grid=(1,),
            in_specs=[pl.BlockSpec((1,) + m.shape[1:], lambda i, core_ref: (core_ref[0], 0, 0)) for m in mines]
                     + [whole(t) for t in theirs],
            out_specs=[whole(t) for t in theirs]),
        out_shape=[jax.ShapeDtypeStruct(t.shape, BF16) for t in theirs],
        compiler_params=_params("arbitrary"), name=name)(core, *mines, *theirs)


def _chip_scatter(arrs, *, name):
    n = len(arrs)
    plan = _chip_scatter_plan(arrs)

    def body(*refs):
        start, _, finish = plan["phases"](refs[:n], refs[n:2 * n], *refs[2 * n:])
        start()
        finish()

    anyspec = pl.BlockSpec(memory_space=pl.ANY)
    return pl.pallas_call(
        body, in_specs=[anyspec] * n, out_specs=[anyspec] * n, out_shape=plan["out_shape"],
        scratch_shapes=plan["scratch"], name=name)(*arrs)


def _chip_scatter_plan(arrs):
    n = len(arrs)

    def phases(ins, outs, send_sems, recv_sems, loc_sems):
        x, y, c, chips = _mesh_place()
        mine = 2 * x + y
        local = [pltpu.make_async_copy(ins[k].at[mine], outs[k].at[mine], loc_sems.at[k]) for k in range(n)]
        remote = [pltpu.make_async_remote_copy(
            src_ref=ins[k].at[2 * px + py], dst_ref=outs[k].at[mine], send_sem=send_sems.at[k, j],
            recv_sem=recv_sems.at[k, j], device_id=(px, py, c), device_id_type=pl.DeviceIdType.MESH)
            for j, (px, py) in enumerate(chips) for k in range(n)]

        def start():
            for cp in local + remote:
                cp.start()

        def finish():
            for cp in remote:
                cp.wait()
            for cp in local:
                cp.wait()

        return start, (lambda: None), finish

    return dict(
        args=list(arrs), out_shape=[jax.ShapeDtypeStruct(a.shape, a.dtype) for a in arrs],
        scratch=[pltpu.SemaphoreType.DMA((n, 3)), pltpu.SemaphoreType.DMA((n, 3)), pltpu.SemaphoreType.DMA((n,))],
        phases=phases)


def _ssm_discretize(a_re, a_im, log_dt, b_re, b_im):
    dt = jnp.exp(log_dt)[:, None]
    mag = jnp.exp(a_re * dt)
    lr = mag * jnp.cos(a_im * dt)
    li = mag * jnp.sin(a_im * dt)
    den = a_re * a_re + a_im * a_im
    cr = ((lr - 1.0) * a_re + li * a_im) / den
    ci = (li * a_re - (lr - 1.0) * a_im) / den
    bbr = cr[..., None] * b_re - ci[..., None] * b_im
    bbi = cr[..., None] * b_im + ci[..., None] * b_re
    return lr, li, bbr, bbi


def _ssm_dense(lr, li, bbr, bbi, c_re, c_im):
    scan_ops = _scan_tables(lr.reshape(-1), li.reshape(-1)) + _scan_consts(SEQ_BLOCK)
    per_half = SSM_GROUPS // SSM_HALVES

    def halves(a, rows, cols):
        a = a.reshape(SSM_HALVES, per_half, rows, 1, cols)
        shape = (SSM_HALVES, per_half, rows, per_half, cols)
        on_diagonal = lax.broadcasted_iota(jnp.int32, shape, 1) == lax.broadcasted_iota(jnp.int32, shape, 3)
        return jnp.where(on_diagonal, a, 0.0).reshape(SSM_HALVES, per_half * rows, per_half * cols)

    bblk = jnp.stack([halves(b.transpose(0, 2, 1), SSM_GROUP, SSM_STATE) for b in (bbr, bbi)]).astype(BF16)
    cblk = jnp.stack([halves(c.transpose(0, 2, 1), SSM_STATE, SSM_GROUP) for c in (c_re, c_im)]).astype(BF16)
    return scan_ops, bblk, cblk


def _ssm_extract(db, dc, sums):
    db = db.reshape(2, SSM_GROUPS, SSM_GROUP, SSM_STATE).transpose(0, 1, 3, 2)
    dc = dc.reshape(2, SSM_GROUPS, SSM_GROUP, SSM_STATE)
    dlr = sums[0].reshape(SSM_GROUPS, SSM_STATE)
    dli = sums[1].reshape(SSM_GROUPS, SSM_STATE)
    return dlr, dli, db[0], db[1], dc[0], dc[1]


IN_SPLITS = (ATT_W, KV_W, KV_W, SSM_W, POOL_W, ATT_W, SSM_W, POOL_W, 3 * D_MODEL)


def _split_w_in(w):
    idx = [0]
    for s in IN_SPLITS:
        idx.append(idx[-1] + s)
    seg = [w[..., idx[k]:idx[k + 1]] for k in range(len(IN_SPLITS))]
    q, k, v, us, up, za, zs, zp, gl = seg
    return (jnp.concatenate([q, za, k, v], axis=-1), jnp.concatenate([us, zs], axis=-1),
            jnp.concatenate([up, zp], axis=-1), gl)


def _merge_w_in(da, ds, dp, dg):
    q, za, k, v = da[..., :ATT_W], da[..., ATT_W:2 * ATT_W], da[..., 2 * ATT_W:2 * ATT_W + KV_W], da[..., 2 * ATT_W + KV_W:]
    us, zs = ds[..., :SSM_W], ds[..., SSM_W:]
    up, zp = dp[..., :POOL_W], dp[..., POOL_W:]
    return jnp.concatenate([q, k, v, us, up, za, zs, zp, dg], axis=-1)


def _layer_fwd(x, lw, li, late=None, comm_attn=None, comm_ssm=None):
    tag = f"l{li}"
    h, (pa, ps, pp, pg), arrived = _ln_proj(x, lw["norm_g"], lw["shift"], lw["scale"],
                                            (lw["w_a"], lw["w_s"], lw["w_p"], lw["w_g"]), name=f"ln_proj_{tag}",
                                            comm=None if late is None else late[0])
    if late is not None:
        lw = {**lw, **late[1](arrived)}
    ya, from_attn = _attn_fwd(pa, lw["sinks"], name=f"attn_fwd_{tag}", comm=comm_attn)
    ys, chk, states, *from_ssm = _ssm_fwd(ps, lw["lam"], lw["bblk"], lw["cblk"], lw["ssm_d"], lw["w_glu"], lw["b_glu"],
                                          name=f"ssm_fwd_{tag}", comm=comm_ssm)
    yp = _pool_fwd(pp, lw["w_pool"], lw["pool_scale"], name=f"pool_fwd_{tag}")
    x_new, merged, ba, bs, bp, out = _merge_fwd(ya, ys, yp, lw["w_br_att"], lw["w_br_ssm"], lw["w_br_pool"], pg,
                                                lw["w_out"], x, lw["gate"], name=f"merge_fwd_{tag}")
    saved = dict(x=x, h=h, pa=pa, ps=ps, pp=pp, pg=pg, ya=ya, ys=ys, yp=yp, chk=chk, states=states, merged=merged,
                 ba=ba, bs=bs, bp=bp, out=out)
    return x_new, saved, lw, list(from_attn), list(from_ssm)


def _layer_bwd(dx, lw, sv, li, later=None, own=None):
    tag = f"l{li}"
    g = {}
    merge_out, swapped = _merge_bwd(
        dx, sv["out"], lw["gate"], lw["w_out"], sv["pg"], (sv["ba"], sv["bs"], sv["bp"]),
        (lw["w_br_att"], lw["w_br_ssm"], lw["w_br_pool"]), (sv["ya"], sv["ys"], sv["yp"]), sv["merged"],
        name=f"merge_bwd_{tag}", comm=None if later is None else later[0])
    dya, dys, dyp, dpg, gate_sums, g["w_br_att"], g["w_br_ssm"], g["w_br_pool"], g["w_out"] = merge_out
    dpa, dsink = _attn_bwd(sv["pa"], lw["sinks"], dya, name=f"attn_bwd_{tag}")
    dps, db_dense, dc_dense, dwglu, ssm_sums, *exchanged = _ssm_bwd(
        sv["ps"], dys, sv["chk"], sv["states"], lw["lam"], lw["bblk"], lw["cblk"], lw["ssm_d"], lw["w_glu"], lw["b_glu"],
        name=f"ssm_bwd_{tag}", comm=None if later is None else later[1](swapped))
    g["w_glu"] = dwglu.astype(BF16)
    dpp, dwpool, pool_sums = _pool_bwd(sv["pp"], dyp, lw["w_pool"], lw["pool_scale"], name=f"pool_bwd_{tag}")
    h = sv["h"]
    dw_a = _mm_tn(h, dpa, out_dtype=BF16, tn=1280, name=f"dw_a_{tag}")
    dw_s = _mm_tn(h, dps, out_dtype=BF16, name=f"dw_s_{tag}")
    dw_p = _mm_tn(h, dpp, out_dtype=BF16, name=f"dw_p_{tag}")
    dh_pairs = [(dpa, lw["w_a"]), (dps, lw["w_s"]), (dpp, lw["w_p"]), (dpg, lw["w_g"])]
    if own is None:
        dw_g, from_late = _mm_tn(h, dpg, out_dtype=BF16, name=f"dw_g_{tag}"), []
        g["w_in"] = _merge_w_in(dw_a, dw_s, dw_p, dw_g)
        dh, from_w_in = _mm_nt_sum(dh_pairs, name=f"dh_{tag}"), []
    else:
        dw_g, from_late = _mm_tn(h, dpg, out_dtype=BF16, name=f"dw_g_{tag}", comm=own({k: g[k] for k in LATE_WEIGHTS}))
        g["w_in"] = _merge_w_in(dw_a, dw_s, dw_p, dw_g)
        dh, from_w_in = _mm_nt_sum(dh_pairs, name=f"dh_{tag}", comm=own({"w_in": g["w_in"]}))
    dx_in, ln_sums = _ln_bwd(sv["x"], dh, dx, lw["norm_g"], lw["scale"], name=f"ln_bwd_{tag}")
    g["dmod"] = jnp.concatenate([ln_sums[0], ln_sums[1], gate_sums[0]])
    g["norm_g"] = ln_sums[2]
    g["attn_sinks"] = dsink[:, 0]
    g["ssm_raw"] = _ssm_extract(db_dense, dc_dense, ssm_sums)
    g["ssm_d"] = ssm_sums[2, :SSM_W]
    g["b_glu"] = ssm_sums[3, :SSM_W]
    g["w_pool"] = dwpool
    g["pool_scale"] = pool_sums[0]
    return dx_in, g, exchanged, list(from_w_in) + list(from_late)


BIG_WEIGHTS = ("w_in", "w_glu", "w_br_att", "w_br_ssm", "w_br_pool", "w_out")
ROW_SHARDED = ("w_glu", "w_out")


LATE_WEIGHTS = BIG_WEIGHTS[1:]


def _full_weights(keys, gathered):
    full = {}
    for k, g in zip(keys, gathered):
        if k in ROW_SHARDED:
            full[k] = g.reshape(N_DEV * g.shape[1], g.shape[2])
        else:
            full[k] = g.transpose(1, 0, 2).reshape(g.shape[1], N_DEV * g.shape[2])
    return full


def _by_destination(keys, grads):
    out = []
    for k in keys:
        g = grads[k]
        if k in ROW_SHARDED:
            out.append(g.reshape(4, 2, g.shape[0] // N_DEV, g.shape[1]).transpose(1, 0, 2, 3))
        else:
            out.append(g.reshape(g.shape[0], 4, 2, g.shape[1] // N_DEV).transpose(2, 1, 0, 3))
    return out


def _prepare_layer(li, mod, norm_g, w_in_full, attn_sinks, disc, ssm_c_re, ssm_c_im, ssm_d, b_glu, w_pool, pool_scale):
    d = D_MODEL
    lr, li_, bbr, bbi = disc
    lam, bblk, cblk = _ssm_dense(lr[li], li_[li], bbr[li], bbi[li], ssm_c_re[li], ssm_c_im[li])
    w_a, w_s, w_p, w_g = _split_w_in(w_in_full)
    return dict(
        norm_g=norm_g[li][None, :], shift=mod[li, :d][None, :], scale=mod[li, d:2 * d][None, :],
        gate=mod[li, 2 * d:][None, :], w_a=w_a, w_s=w_s, w_p=w_p, w_g=w_g,
        sinks=attn_sinks[li], lam=lam, bblk=bblk, cblk=cblk, ssm_d=ssm_d[li][None, :],
        b_glu=b_glu[li][None, :], w_pool=w_pool[li].astype(BF16), pool_scale=pool_scale[li][None, :])


SMALL_ROWS = 64
SMALL_ORDER = ("norm_g", "attn_sinks", "ssm_d", "b_glu", "w_pool", "pool_scale", "dmod")


def _pack_small(loss, dfinal_g, layer_grads):
    parts = [jnp.broadcast_to(loss.reshape(1), (128,)), dfinal_g]
    for g in layer_grads:
        for k in SMALL_ORDER:
            v = g[k].reshape(-1)
            if v.shape[0] % 128:
                v = jnp.pad(v, (0, 128 - v.shape[0] % 128))
            parts.append(v)
        for v in g["ssm_raw"]:
            parts.append(v.reshape(-1))
    flat = jnp.concatenate(parts)
    return jnp.pad(flat, (0, (-flat.shape[0]) % (SMALL_ROWS * 128))).reshape(-1, 128)


def _unpack_small(flat, shapes):
    out, off = [], 0
    for s in shapes:
        n = int(math.prod(s))
        out.append(flat[off:off + n].reshape(s))
        off += n + (-n) % 128
    return out


def kernel(x, c, norm_g, w_ada, b_ada, w_in, attn_sinks, ssm_a_re, ssm_a_im, ssm_log_dt, ssm_b_re, ssm_b_im, ssm_c_re, ssm_c_im, ssm_d, w_glu, b_glu, w_pool, pool_scale, w_br_att, w_br_ssm, w_br_pool, w_out, final_g, loss_target, m_norm_g, m_w_ada, m_b_ada, m_w_in, m_attn_sinks, m_ssm_a_re, m_ssm_a_im, m_ssm_log_dt, m_ssm_b_re, m_ssm_b_im, m_ssm_c_re, m_ssm_c_im, m_ssm_d, m_w_glu, m_b_glu, m_w_pool, m_pool_scale, m_w_br_att, m_w_br_ssm, m_w_br_pool, m_w_out, m_final_g, v_norm_g, v_w_ada, v_b_ada, v_w_in, v_attn_sinks, v_ssm_a_re, v_ssm_a_im, v_ssm_log_dt, v_ssm_b_re, v_ssm_b_im, v_ssm_c_re, v_ssm_c_im, v_ssm_d, v_w_glu, v_b_glu, v_w_pool, v_pool_scale, v_w_br_att, v_w_br_ssm, v_w_br_pool, v_w_out, v_final_g):
    me = 4 * lax.axis_index("x") + 2 * lax.axis_index("y") + lax.axis_index("c")
    d = D_MODEL
    ada_w = 3 * d // N_DEV

    (c_all,) = _exchange([c.reshape(8, 128)], scatter=False, name="gather_c")
    c_act = jax.nn.silu(c_all.reshape(N_DEV, d))
    b_cols = lax.dynamic_slice(b_ada, (0, me * ada_w), (DEPTH, ada_w))
    mod_part = jnp.concatenate(
        [_mm(c_act, w_ada[li], name=f"ada_fwd_l{li}") + b_cols[li][None, :] for li in range(DEPTH)], axis=0)
    (mod_all,) = _exchange([mod_part], scatter=False, name="gather_mod")
    mod_all = mod_all.reshape(N_DEV, DEPTH, N_DEV, ada_w)
    mod_mine = lax.dynamic_index_in_dim(mod_all, me, axis=2, keepdims=False)
    mod_mine = mod_mine.transpose(1, 0, 2).reshape(DEPTH, 3 * d)

    sharded = dict(w_in=w_in, w_glu=w_glu, w_br_att=w_br_att, w_br_ssm=w_br_ssm, w_br_pool=w_br_pool, w_out=w_out)
    shards = lambda li, keys: [sharded[k][li].astype(BF16) for k in keys]
    disc, disc_vjp = jax.vjp(jax.vmap(_ssm_discretize), ssm_a_re, ssm_a_im, ssm_log_dt, ssm_b_re, ssm_b_im)
    layer = lambda li, gathered_w_in: _prepare_layer(
        li, mod_mine, norm_g, _full_weights(("w_in",), gathered_w_in)["w_in"], attn_sinks, disc, ssm_c_re, ssm_c_im,
        ssm_d, b_glu, w_pool, pool_scale)
    late_weights = lambda gathered: _full_weights(LATE_WEIGHTS, gathered)
    core = lax.axis_index("c").astype(jnp.int32).reshape(1)

    def add_pairs(keys, by_dest, from_sibling, tag):
        flat = {k: (a.reshape(2, -1, a.shape[-1]), b.reshape(-1, b.shape[-1]))
                for k, a, b in zip(keys, by_dest, from_sibling)}
        small = [k for k in keys if k != "w_in"]
        sums = {}
        if "w_in" in flat:
            sums["w_in"] = _pair_add(*flat["w_in"], core, name=f"grads_pair_add_{tag}_w_in")
        if small:
            added = _pair_add_small([flat[k][0] for k in small], [flat[k][1] for k in small], core,
                                    name=f"grads_pair_add_{tag}_late")
            sums.update(zip(small, added))
        return [sums[k].reshape(b.shape) for k, b in zip(keys, from_sibling)]

    def chip_sums_of(keys, grads_li, tag):
        by_dest = _by_destination(keys, grads_li)
        return add_pairs(keys, by_dest, _sibling_swap(by_dest, name=f"grads_sibling_swap_{tag}"), tag)

    layers, saved, grads = [None] * DEPTH, [None] * DEPTH, [None] * DEPTH
    layers[0] = layer(0, _gather_two_level(shards(0, ("w_in",)), name="gather_w_in_l0"))
    xs, saved[0], layers[0], late1, w_in1 = _layer_fwd(
        x[0], layers[0], 0, late=(_gather_plan(shards(0, LATE_WEIGHTS)), late_weights),
        comm_attn=_gather_plan(shards(1, LATE_WEIGHTS)), comm_ssm=_gather_plan(shards(1, ("w_in",))))
    layers[1] = {**layer(1, w_in1), **late_weights(late1)}
    xs, saved[1], _, _, _ = _layer_fwd(xs, layers[1], 1)
    dx, fin_sums = _final_loss(xs, final_g[None, :], loss_target[0])
    loss_part = jnp.sum(fin_sums[1])
    dx, grads[1], _, _ = _layer_bwd(dx, layers[1], saved[1], 1)
    by_dest1 = _by_destination(BIG_WEIGHTS, grads[1])
    dx, grads[0], scattered1, scattered0 = _layer_bwd(
        dx, layers[0], saved[0], 0,
        later=(_sibling_swap_plan(by_dest1),
               lambda swapped: _chip_scatter_plan(add_pairs(BIG_WEIGHTS, by_dest1, swapped, "l1"))),
        own=lambda g: _chip_scatter_plan(chip_sums_of(tuple(g), g, "l0_" + "_".join(g))))
    big = list(zip(scattered0, scattered1))
    grad_x = dx[None]

    small = _pack_small(loss_part, fin_sums[0], grads)
    dmod_rows = jnp.concatenate([grads[li]["dmod"] for li in range(DEPTH)]).reshape(-1, 128)
    small_sum, dmod_gathered = _allreduce_small(small, dmod_rows, name="allreduce_small")
    out = {}

    def adam(name, w, g_slabs, m, v):
        shp = w.shape
        r = int(math.prod(shp[:-1])) if len(shp) > 1 else 1
        w2, m2, v2 = (a.reshape(r, shp[-1]) for a in (w, m, v))
        gs = [g.reshape(g.shape[0], r // len(g_slabs), shp[-1]) for g in g_slabs]
        res = _adamw(w2, gs, m2, v2, name=f"adamw_{name}")
        out[name] = tuple(a.reshape(shp) for a in res)

    flat = small_sum.reshape(-1)
    shapes = [(128,), (d,)]
    for _ in range(DEPTH):
        shapes += [(d,), (N_HEADS,), (SSM_W,), (SSM_W,), (4, POOL_GW, POOL_GW), (POOL_W,), (3 * d,),
                   (SSM_GROUPS, SSM_STATE), (SSM_GROUPS, SSM_STATE), (SSM_GROUPS, SSM_STATE, SSM_GROUP),
                   (SSM_GROUPS, SSM_STATE, SSM_GROUP), (SSM_GROUPS, SSM_GROUP, SSM_STATE), (SSM_GROUPS, SSM_GROUP, SSM_STATE)]
    un = _unpack_small(flat, shapes)
    loss = un[0][0]
    g_final_g = un[1]
    per = 13
    gl = [un[2 + li * per: 2 + (li + 1) * per] for li in range(DEPTH)]
    st = lambda j: jnp.stack([gl[li][j] for li in range(DEPTH)])
    g_norm_g, g_sinks, g_ssm_d, g_b_glu, g_w_pool, g_pool_scale, g_b_ada = (st(j) for j in range(7))
    d_lr, d_li, d_bbr, d_bbi, g_c_re, g_c_im = (st(j) for j in range(7, 13))
    g_a_re, g_a_im, g_log_dt, g_b_re, g_b_im = disc_vjp((d_lr, d_li, d_bbr, d_bbi))

    dmod_all = lax.dynamic_slice(dmod_gathered.reshape(N_DEV, DEPTH, 3 * d), (0, 0, me * ada_w), (N_DEV, DEPTH, ada_w))
    dmod_all = dmod_all.transpose(1, 0, 2)
    g_w_ada = jnp.stack([_mm_tn(c_act, dmod_all[li], tm=d, tn=ada_w, tk=N_DEV, name=f"dw_ada_l{li}") for li in range(DEPTH)])

    adam("w_ada", w_ada, [g_w_ada[None]], m_w_ada, v_w_ada)
    adam("w_in", w_in, big[0], m_w_in, v_w_in)
    adam("w_glu", w_glu, big[1], m_w_glu, v_w_glu)
    adam("w_br_att", w_br_att, big[2], m_w_br_att, v_w_br_att)
    adam("w_br_ssm", w_br_ssm, big[3], m_w_br_ssm, v_w_br_ssm)
    adam("w_br_pool", w_br_pool, big[4], m_w_br_pool, v_w_br_pool)
    adam("w_out", w_out, big[5], m_w_out, v_w_out)

    small_names = ["norm_g", "b_ada", "attn_sinks", "ssm_a_re", "ssm_a_im", "ssm_log_dt", "ssm_b_re", "ssm_b_im",
                   "ssm_c_re", "ssm_c_im", "ssm_d", "b_glu", "w_pool", "pool_scale", "final_g"]
    small_w = [norm_g, b_ada, attn_sinks, ssm_a_re, ssm_a_im, ssm_log_dt, ssm_b_re, ssm_b_im, ssm_c_re, ssm_c_im,
               ssm_d, b_glu, w_pool, pool_scale, final_g]
    small_m = [m_norm_g, m_b_ada, m_attn_sinks, m_ssm_a_re, m_ssm_a_im, m_ssm_log_dt, m_ssm_b_re, m_ssm_b_im,
               m_ssm_c_re, m_ssm_c_im, m_ssm_d, m_b_glu, m_w_pool, m_pool_scale, m_final_g]
    small_v = [v_norm_g, v_b_ada, v_attn_sinks, v_ssm_a_re, v_ssm_a_im, v_ssm_log_dt, v_ssm_b_re, v_ssm_b_im,
               v_ssm_c_re, v_ssm_c_im, v_ssm_d, v_b_glu, v_w_pool, v_pool_scale, v_final_g]
    small_g = [g_norm_g, g_b_ada, g_sinks, g_a_re, g_a_im, g_log_dt, g_b_re, g_b_im, g_c_re, g_c_im,
               g_ssm_d, g_b_glu, g_w_pool, g_pool_scale, g_final_g]

    for nm, w, g, m, v in zip(small_names, small_w, small_g, small_m, small_v):
        adam(nm, w, [g[None]], m, v)

    order = ["norm_g", "w_ada", "b_ada", "w_in", "attn_sinks", "ssm_a_re", "ssm_a_im", "ssm_log_dt", "ssm_b_re",
             "ssm_b_im", "ssm_c_re", "ssm_c_im", "ssm_d", "w_glu", "b_glu", "w_pool", "pool_scale", "w_br_att",
             "w_br_ssm", "w_br_pool", "w_out", "final_g"]
    return (loss, grad_x, *[out[k][0] for k in order], *[out[k][1] for k in order],
            *[out[k][2] for k in order], *[out[k][3] for k in order])
```

```python
import functools
import math

import jax
import jax.numpy as jnp
from jax import lax
from jax.experimental import pallas as pl
from jax.experimental.pallas import tpu as pltpu

F32 = jnp.float32
BF16 = jnp.bfloat16

N_DEV = 8
D_MODEL = 1024
DEPTH = 2
CHUNK = 64
N_HEADS = 8
N_KV_HEADS = 2
HEAD_DIM = 64
Q_PER_KV = N_HEADS // N_KV_HEADS
WINDOW = 128
ATT_W = 512
KV_W = 128
SSM_W = 512
SSM_GROUP = 16
SSM_GROUPS = 32
SSM_STATE = 64
SSM_N = SSM_GROUPS * SSM_STATE
POOL_W = 512
POOL_WINDOWS = (2, 4, 8, 16)
POOL_GW = 128
POOL_HALO = 16
EPS = 1e-6
NEG_INF = -1e30
ADAM_LR = 0.001
ADAM_B1 = 0.9
ADAM_B2 = 0.999
ADAM_EPS = 1e-08
ADAM_WD = 0.01
ADAM_STEP = 10

SEQ_BLOCK = 256
ATT_BLOCK = 128
VMEM_LIMIT = 56 * 1024 * 1024

NN = (((1,), (0,)), ((), ()))
NT = (((1,), (1,)), ((), ()))
TN = (((0,), (0,)), ((), ()))


def _dot(a, b, dims=NN):
    return lax.dot_general(a.astype(BF16), b.astype(BF16), dims, preferred_element_type=F32)


def _params(*sem):
    return pltpu.CompilerParams(dimension_semantics=sem, vmem_limit_bytes=VMEM_LIMIT)


def _sigmoid(x):
    return 0.5 + 0.5 * jnp.tanh(0.5 * x)


def _silu_and_grad(z):
    s = _sigmoid(z)
    return z * s, s * (1.0 + z * (1.0 - s))


_GELU_K = math.sqrt(2.0 / math.pi)


def _gelu_and_grad(x):
    inner = _GELU_K * (x + 0.044715 * x * x * x)
    t = jnp.tanh(inner)
    val = 0.5 * x * (1.0 + t)
    grad = 0.5 * (1.0 + t) + 0.5 * x * (1.0 - t * t) * _GELU_K * (1.0 + 3.0 * 0.044715 * x * x)
    return val, grad


def _mm(a, b, *, nt=False, out_dtype=F32, tm=1024, tn=1024, name, comm=None):
    m, k = a.shape
    n = b.shape[0] if nt else b.shape[1]
    tm, tn = min(tm, m), min(tn, n)
    assert m % tm == 0 and n % tn == 0
    dims = NT if nt else NN
    grid = (m // tm, n // tn)
    c_args, c_in, c_out, c_shape, c_scratch = _comm_extra(comm)

    def body(a_ref, b_ref, o_ref):
        o_ref[...] = _dot(a_ref[...], b_ref[...], dims).astype(out_dtype)

    b_spec = pl.BlockSpec((tn, k), lambda i, j: (j, 0)) if nt else pl.BlockSpec((k, tn), lambda i, j: (0, j))
    res = pl.pallas_call(
        _with_comm(body, comm, 2, 1, grid, -1), grid=grid,
        in_specs=[pl.BlockSpec((tm, k), lambda i, j: (i, 0)), b_spec] + c_in,
        out_specs=[pl.BlockSpec((tm, tn), lambda i, j: (i, j))] + c_out,
        out_shape=[jax.ShapeDtypeStruct((m, n), out_dtype)] + c_shape,
        scratch_shapes=c_scratch,
        compiler_params=_params(*(("arbitrary",) * 2 if comm else ("parallel",) * 2)), name=name)(a, b, *c_args)
    return (res[0], list(res[1:])) if comm else res[0]


def _grouped_pieces(groups):
    out = []
    for ranges in groups:
        off, pieces = 0, []
        for lo, hi in ranges:
            pieces.append((off, lo, hi))
            off += hi - lo
        out.append(pieces)
    return out


def _mm_nt_grouped(ds, w, groups, *, out_dtype=F32, tm=512, tn=512, name, comm=None):
    m = ds[0].shape[0]
    n, k = w.shape
    nd = len(ds)
    pieces = _grouped_pieces(groups)
    grid = (m // tm, n // tn)
    c_args, c_in, c_out, c_shape, c_scratch = _comm_extra(comm)

    def body(*refs):
        d_refs, w_ref, o_ref = refs[:nd], refs[nd], refs[nd + 1]
        acc = None
        for d_ref, plist in zip(d_refs, pieces):
            for off, lo, hi in plist:
                term = _dot(d_ref[:, off:off + hi - lo], w_ref[:, lo:hi], NT)
                acc = term if acc is None else acc + term
        o_ref[...] = acc.astype(out_dtype)

    in_specs = [pl.BlockSpec((tm, a.shape[1]), lambda i, j: (i, 0)) for a in ds] + [pl.BlockSpec((tn, k), lambda i, j: (j, 0))]
    res = pl.pallas_call(
        _with_comm(body, comm, nd + 1, 1, grid, -1), grid=grid, in_specs=in_specs + c_in,
        out_specs=[pl.BlockSpec((tm, tn), lambda i, j: (i, j))] + c_out,
        out_shape=[jax.ShapeDtypeStruct((m, n), out_dtype)] + c_shape,
        scratch_shapes=c_scratch,
        compiler_params=_params(*(("arbitrary",) * 2 if comm else ("parallel",) * 2)), name=name)(*ds, w, *c_args)
    return (res[0], list(res[1:])) if comm else res[0]


def _mm_tn(a, b, *, out_dtype=F32, tm=1024, tn=1024, tk=1024, name, comm=None):
    k, m = a.shape
    n = b.shape[1]
    assert m % min(tm, m) == 0 and n % min(tn, n) == 0 and k % min(tk, k) == 0
    tm, tn, tk = min(tm, m), min(tn, n), min(tk, k)
    nk = k // tk
    grid = (m // tm, n // tn, nk)
    c_args, c_in, c_out, c_shape, c_scratch = _comm_extra(comm)

    def body(a_ref, b_ref, o_ref, acc_ref):
        kk = pl.program_id(2)

        @pl.when(kk == 0)
        def _():
            acc_ref[...] = jnp.zeros_like(acc_ref)

        acc_ref[...] += _dot(a_ref[...], b_ref[...], TN)

        @pl.when(kk == nk - 1)
        def _():
            o_ref[...] = acc_ref[...].astype(out_dtype)

    res = pl.pallas_call(
        _with_comm(body, comm, 2, 1, grid, -1), grid=grid,
        in_specs=[pl.BlockSpec((tk, tm), lambda i, j, kk: (kk, i)), pl.BlockSpec((tk, tn), lambda i, j, kk: (kk, j))] + c_in,
        out_specs=[pl.BlockSpec((tm, tn), lambda i, j, kk: (i, j))] + c_out,
        out_shape=[jax.ShapeDtypeStruct((m, n), out_dtype)] + c_shape,
        scratch_shapes=[pltpu.VMEM((tm, tn), F32)] + c_scratch,
        compiler_params=_params(*(("arbitrary",) * 3 if comm else ("parallel", "parallel", "arbitrary"))),
        name=name)(a, b, *c_args)
    return (res[0], list(res[1:])) if comm else res[0]


def _mm_tn_grouped(a, bs, groups, *, tm=512, tk=512, name, comm=None):
    k, m = a.shape
    tm, tk = min(tm, m), min(tk, k)
    assert m % tm == 0 and k % tk == 0
    nk, nb = k // tk, len(bs)
    pieces = _grouped_pieces(groups)
    n = sum(b.shape[1] for b in bs)
    grid = (m // tm, nk)
    c_args, c_in, c_out, c_shape, c_scratch = _comm_extra(comm)

    def body(a_ref, *refs):
        b_refs, o_ref, acc_refs = refs[:nb], refs[nb], refs[nb + 1:]
        kk = pl.program_id(1)
        av = a_ref[...]
        for b_ref, acc_ref, plist in zip(b_refs, acc_refs, pieces):
            @pl.when(kk == 0)
            def _():
                acc_ref[...] = jnp.zeros_like(acc_ref)

            acc_ref[...] += _dot(av, b_ref[...], TN)

            @pl.when(kk == nk - 1)
            def _():
                for off, lo, hi in plist:
                    o_ref[:, lo:hi] = acc_ref[:, off:off + hi - lo].astype(BF16)

    res = pl.pallas_call(
        _with_comm(body, comm, 1 + nb, 1, grid, -1), grid=grid,
        in_specs=[pl.BlockSpec((tk, tm), lambda i, kk: (kk, i))]
                 + [pl.BlockSpec((tk, b.shape[1]), lambda i, kk: (kk, 0)) for b in bs] + c_in,
        out_specs=[pl.BlockSpec((tm, n), lambda i, kk: (i, 0))] + c_out,
        out_shape=[jax.ShapeDtypeStruct((m, n), BF16)] + c_shape,
        scratch_shapes=[pltpu.VMEM((tm, b.shape[1]), F32) for b in bs] + c_scratch,
        compiler_params=_params("arbitrary", "arbitrary"), name=name)(a, *bs, *c_args)
    return res[0], list(res[1:])


def _ln_proj(x, g, shift, scale, w, groups, *, name, tm=512, comm=None):
    l, d = x.shape
    tm = min(tm, l)
    nb = l // tm
    pieces = _grouped_pieces(groups)
    widths = [sum(hi - lo for _, lo, hi in plist) for plist in pieces]
    nw = len(pieces)
    c_args, c_in, c_out, c_shape, c_scratch = _comm_extra(comm)

    def body(x_ref, g_ref, sh_ref, sc_ref, w_ref, h_ref, *p_refs):
        xv = x_ref[...]
        n = xv * lax.rsqrt(jnp.mean(xv * xv, axis=-1, keepdims=True) + EPS)
        h = ((n * g_ref[...]) * (1.0 + sc_ref[...]) + sh_ref[...]).astype(BF16)
        h_ref[...] = h
        for p_ref, plist in zip(p_refs, pieces):
            for off, lo, hi in plist:
                p_ref[:, off:off + hi - lo] = _dot(h, w_ref[:, lo:hi]).astype(BF16)

    vec = pl.BlockSpec((1, d), lambda i: (0, 0))
    row = lambda n: pl.BlockSpec((tm, n), lambda i: (i, 0))
    res = pl.pallas_call(
        _with_comm(body, comm, 5, 1 + nw, nb, -1), grid=(nb,),
        in_specs=[row(d), vec, vec, vec, pl.BlockSpec(w.shape, lambda i: (0, 0))] + c_in,
        out_specs=[row(d)] + [row(n) for n in widths] + c_out,
        out_shape=[jax.ShapeDtypeStruct((l, d), BF16)] + [jax.ShapeDtypeStruct((l, n), BF16) for n in widths] + c_shape,
        scratch_shapes=c_scratch,
        compiler_params=_params("arbitrary"), name=name)(x, g, shift, scale, w, *c_args)
    return res[0], list(res[1:1 + nw]), list(res[1 + nw:])


def _ln_bwd(x, dh, dres, g, scale, *, name, tm=512):
    l, d = x.shape

    def body(x_ref, dh_ref, dres_ref, g_ref, sc_ref, dx_ref, sums_ref):
        xv = x_ref[...]
        dhv = dh_ref[...]
        rstd = lax.rsqrt(jnp.mean(xv * xv, axis=-1, keepdims=True) + EPS)
        n = xv * rstd
        gv = g_ref[...]
        dr = dhv * (1.0 + sc_ref[...])
        dn = dr * gv
        dx_ref[...] = dres_ref[...] + rstd * (dn - n * jnp.mean(dn * n, axis=-1, keepdims=True))

        @pl.when(pl.program_id(0) == 0)
        def _():
            sums_ref[...] = jnp.zeros_like(sums_ref)

        sums_ref[0:1, :] += jnp.sum(dhv, axis=0, keepdims=True)
        sums_ref[1:2, :] += jnp.sum(dhv * (n * gv), axis=0, keepdims=True)
        sums_ref[2:3, :] += jnp.sum(dr * n, axis=0, keepdims=True)

    vec = pl.BlockSpec((1, d), lambda i: (0, 0))
    row = pl.BlockSpec((tm, d), lambda i: (i, 0))
    return pl.pallas_call(
        body, grid=(l // tm,),
        in_specs=[row, row, row, vec, vec],
        out_specs=[row, pl.BlockSpec((8, d), lambda i: (0, 0))],
        out_shape=[jax.ShapeDtypeStruct((l, d), F32), jax.ShapeDtypeStruct((8, d), F32)],
        compiler_params=_params("arbitrary"), name=name)(x, dh, dres, g, scale)


def _final_loss(x, g, target, *, tm=512):
    l, d = x.shape

    def body(x_ref, g_ref, t_ref, dx_ref, sums_ref):
        xv = x_ref[...]
        rstd = lax.rsqrt(jnp.mean(xv * xv, axis=-1, keepdims=True) + EPS)
        n = xv * rstd
        gv = g_ref[...]
        err = n * gv - t_ref[...]
        dy = err * (1.0 / d)
        dn = dy * gv
        dx_ref[...] = rstd * (dn - n * jnp.mean(dn * n, axis=-1, keepdims=True))

        @pl.when(pl.program_id(0) == 0)
        def _():
            sums_ref[...] = jnp.zeros_like(sums_ref)

        sums_ref[0:1, :] += jnp.sum(dy * n, axis=0, keepdims=True)
        sums_ref[1:2, :] += jnp.sum(err * err, axis=0, keepdims=True) * (0.5 / d)

    vec = pl.BlockSpec((1, d), lambda i: (0, 0))
    row = pl.BlockSpec((tm, d), lambda i: (i, 0))
    dx, sums = pl.pallas_call(
        body, grid=(l // tm,),
        in_specs=[row, vec, row],
        out_specs=[row, pl.BlockSpec((8, d), lambda i: (0, 0))],
        out_shape=[jax.ShapeDtypeStruct((l, d), F32), jax.ShapeDtypeStruct((8, d), F32)],
        compiler_params=_params("arbitrary"), name="final_loss")(x, g, target)
    return dx, sums


def _attn_geometry(i, t):
    nk = t + WINDOW
    qi = lax.broadcasted_iota(jnp.int32, (t, nk), 0)
    kj = lax.broadcasted_iota(jnp.int32, (t, nk), 1)
    dist = jnp.abs(qi + WINDOW - kj).astype(F32)
    qc = jnp.right_shift(qi, 6)
    kc = jnp.right_shift(kj, 6)
    valid = (kc >= qc) & (kc <= qc + WINDOW // CHUNK) & ((i > 0) | (kj >= WINDOW))
    return dist, valid


def _attn_head(q, k_all, v_all, sink, slope, dist, valid):
    s = _dot(q, k_all, NT) * (1.0 / math.sqrt(HEAD_DIM)) - slope * dist
    s = jnp.where(valid, s, NEG_INF)
    m = jnp.maximum(jnp.max(s, axis=-1, keepdims=True), sink)
    e = jnp.exp(s - m)
    es = jnp.exp(sink - m)
    inv = 1.0 / (jnp.sum(e, axis=-1, keepdims=True) + es)
    p = e * inv
    o = _dot(p, v_all, NN)
    return p, o, es * inv


def _attn_specs(t):
    cur = pl.BlockSpec((t, ATT_W * 2 + KV_W * 2), lambda i: (i, 0))
    halo_blocks = t // WINDOW
    prev = pl.BlockSpec((WINDOW, 2 * KV_W), lambda i: (jnp.maximum(i * halo_blocks - 1, 0), (2 * ATT_W) // (2 * KV_W)))
    return cur, prev


def _attn_fwd(pa, sinks, *, name, t=ATT_BLOCK, comm=None):
    l = pa.shape[0]
    t = min(t, l)
    nb = l // t
    c_args, c_in, c_out, c_shape, c_scratch = _comm_extra(comm)

    def body(sink_ref, cur_ref, prev_ref, ya_ref):
        i = pl.program_id(0)
        dist, valid = _attn_geometry(i, t)
        for h in range(N_HEADS):
            kh = h // Q_PER_KV
            q = cur_ref[:, h * HEAD_DIM:(h + 1) * HEAD_DIM]
            z = cur_ref[:, ATT_W + h * HEAD_DIM:ATT_W + (h + 1) * HEAD_DIM].astype(F32)
            k_all = jnp.concatenate([prev_ref[:, kh * HEAD_DIM:(kh + 1) * HEAD_DIM],
                                     cur_ref[:, 2 * ATT_W + kh * HEAD_DIM:2 * ATT_W + (kh + 1) * HEAD_DIM]], axis=0)
            v_all = jnp.concatenate([prev_ref[:, KV_W + kh * HEAD_DIM:KV_W + (kh + 1) * HEAD_DIM],
                                     cur_ref[:, 2 * ATT_W + KV_W + kh * HEAD_DIM:2 * ATT_W + KV_W + (kh + 1) * HEAD_DIM]], axis=0)
            _, o, _ = _attn_head(q, k_all, v_all, sink_ref[h], 2.0 ** (-(h + 1)), dist, valid)
            sz, _ = _silu_and_grad(z)
            ya_ref[:, h * HEAD_DIM:(h + 1) * HEAD_DIM] = (o * sz).astype(BF16)

    cur, prev = _attn_specs(t)
    res = pl.pallas_call(
        _with_comm(body, comm, 3, 1, nb, nb - 1), grid=(nb,),
        in_specs=[pl.BlockSpec(memory_space=pltpu.SMEM), cur, prev] + c_in,
        out_specs=[pl.BlockSpec((t, ATT_W), lambda i: (i, 0))] + c_out,
        out_shape=[jax.ShapeDtypeStruct((l, ATT_W), BF16)] + c_shape,
        scratch_shapes=c_scratch,
        compiler_params=_params("arbitrary"), name=name)(sinks, pa, pa, *c_args)
    return res[0], res[1:]


def _attn_bwd(pa, sinks, dya, *, name, t=SEQ_BLOCK):
    l = pa.shape[0]
    t = min(t, l)
    nb = l // t
    scale = 1.0 / math.sqrt(HEAD_DIM)

    def body(sink_ref, cur_ref, prev_ref, dya_ref, dpa_ref, dsink_ref, carry_ref):
        n = pl.program_id(0)
        i = nb - 1 - n
        dist, valid = _attn_geometry(i, t)

        @pl.when(n == 0)
        def _():
            carry_ref[...] = jnp.zeros_like(carry_ref)
            dsink_ref[...] = jnp.zeros_like(dsink_ref)

        dk_acc = [jnp.zeros((HEAD_DIM, t + WINDOW), F32) for _ in range(N_KV_HEADS)]
        dv_acc = [jnp.zeros((HEAD_DIM, t + WINDOW), F32) for _ in range(N_KV_HEADS)]
        for h in range(N_HEADS):
            kh = h // Q_PER_KV
            q = cur_ref[:, h * HEAD_DIM:(h + 1) * HEAD_DIM]
            z = cur_ref[:, ATT_W + h * HEAD_DIM:ATT_W + (h + 1) * HEAD_DIM].astype(F32)
            k_all = jnp.concatenate([prev_ref[:, kh * HEAD_DIM:(kh + 1) * HEAD_DIM],
                                     cur_ref[:, 2 * ATT_W + kh * HEAD_DIM:2 * ATT_W + (kh + 1) * HEAD_DIM]], axis=0)
            v_all = jnp.concatenate([prev_ref[:, KV_W + kh * HEAD_DIM:KV_W + (kh + 1) * HEAD_DIM],
                                     cur_ref[:, 2 * ATT_W + KV_W + kh * HEAD_DIM:2 * ATT_W + KV_W + (kh + 1) * HEAD_DIM]], axis=0)
            p, o, p_sink = _attn_head(q, k_all, v_all, sink_ref[h], 2.0 ** (-(h + 1)), dist, valid)
            dy = dya_ref[:, h * HEAD_DIM:(h + 1) * HEAD_DIM]
            sz, dsz = _silu_and_grad(z)
            do = dy * sz
            dpa_ref[:, ATT_W + h * HEAD_DIM:ATT_W + (h + 1) * HEAD_DIM] = (dy * o * dsz).astype(BF16)
            delta = jnp.sum(do * o, axis=-1, keepdims=True)
            dp = _dot(do, v_all, NT)
            ds = p * (dp - delta)
            dpa_ref[:, h * HEAD_DIM:(h + 1) * HEAD_DIM] = (_dot(ds, k_all, NN) * scale).astype(BF16)
            dk_acc[kh] = dk_acc[kh] + _dot(q, ds, TN) * scale
            dv_acc[kh] = dv_acc[kh] + _dot(do, p, TN)
            dsink_ref[h:h + 1, :] += jnp.broadcast_to(-jnp.sum(p_sink * delta, axis=0, keepdims=True), (1, 128))

        acc = jnp.concatenate(dk_acc + dv_acc, axis=0).T
        own = acc[WINDOW:, :]
        tail = own[t - WINDOW:, :] + carry_ref[...]
        if t > WINDOW:
            dpa_ref[0:t - WINDOW, 2 * ATT_W:] = own[:t - WINDOW, :].astype(BF16)
        dpa_ref[t - WINDOW:t, 2 * ATT_W:] = tail.astype(BF16)
        carry_ref[...] = acc[:WINDOW, :]

    halo_blocks = t // WINDOW
    wpa = 2 * ATT_W + 2 * KV_W
    cur = pl.BlockSpec((t, wpa), lambda n: (nb - 1 - n, 0))
    prev = pl.BlockSpec((WINDOW, 2 * KV_W),
                        lambda n: (jnp.maximum((nb - 1 - n) * halo_blocks - 1, 0), (2 * ATT_W) // (2 * KV_W)))
    return pl.pallas_call(
        body, grid=(nb,),
        in_specs=[pl.BlockSpec(memory_space=pltpu.SMEM), cur, prev, pl.BlockSpec((t, ATT_W), lambda n: (nb - 1 - n, 0))],
        out_specs=[pl.BlockSpec((t, wpa), lambda n: (nb - 1 - n, 0)), pl.BlockSpec((8, 128), lambda n: (0, 0))],
        out_shape=[jax.ShapeDtypeStruct((l, wpa), BF16), jax.ShapeDtypeStruct((8, 128), F32)],
        scratch_shapes=[pltpu.VMEM((WINDOW, 2 * KV_W), F32)],
        compiler_params=_params("arbitrary"), name=name)(sinks, pa, pa, dya)


def _scan(xr, xi, lr, li, t, reverse):
    row = lax.broadcasted_iota(jnp.int32, (t, 1), 0)
    d = 1
    pr, pi = lr, li
    while d < t:
        if reverse:
            sr = jnp.where(row < t - d, pltpu.roll(xr, t - d, 0), 0.0)
            si = jnp.where(row < t - d, pltpu.roll(xi, t - d, 0), 0.0)
        else:
            sr = jnp.where(row >= d, pltpu.roll(xr, d, 0), 0.0)
            si = jnp.where(row >= d, pltpu.roll(xi, d, 0), 0.0)
        xr, xi = xr + pr * sr - pi * si, xi + pr * si + pi * sr
        pr, pi = pr * pr - pi * pi, 2.0 * pr * pi
        d *= 2
    return xr, xi


SCAN_SUB = 8


def _split_hi_lo(a):
    hi = a.astype(BF16)
    lo = (a - hi.astype(F32)).astype(BF16)
    return jnp.concatenate([hi, lo], axis=0)


def _scan_mxu(xr, xi, tab, lam3, lam8, tri, expand, cr, ci, t, reverse):
    ns = t // SCAN_SUB
    n = xr.shape[1]
    v3 = lambda a: a.reshape(ns, SCAN_SUB, n)
    x3r, x3i = v3(xr), v3(xi)
    br = (x3r * tab[0] - x3i * tab[1]).reshape(t, n)
    bi = (x3r * tab[1] + x3i * tab[0]).reshape(t, n)
    pm = jnp.dot(tri, jnp.concatenate([br, bi], axis=1).astype(BF16), preferred_element_type=F32)
    p3r, p3i = v3(pm[:t, :n]), v3(pm[:t, n:])
    slr = p3r * tab[2] - p3i * tab[3]
    sli = p3r * tab[3] + p3i * tab[2]
    totr, toti = pm[t:, :n], pm[t:, n:]
    l3r, l3i = lam3
    l8r, l8i = lam8
    row = lax.broadcasted_iota(jnp.int32, (ns, 1), 0)
    edge = row == (ns - 1 if reverse else 0)
    er = totr * l3r - toti * l3i + jnp.where(edge, l8r * cr - l8i * ci, 0.0)
    ei = totr * l3i + toti * l3r + jnp.where(edge, l8r * ci + l8i * cr, 0.0)
    er, ei = _scan(er, ei, l8r, l8i, ns, reverse)
    shift = ns - 1 if reverse else 1
    nbr = jnp.where(edge, cr, pltpu.roll(er, shift, 0))
    nbi = jnp.where(edge, ci, pltpu.roll(ei, shift, 0))
    ex = jnp.dot(expand, _split_hi_lo(jnp.concatenate([nbr, nbi], axis=1)), preferred_element_type=F32)
    e3r, e3i = v3(ex[:, :n]), v3(ex[:, n:])
    sr = (slr + e3r * tab[4] - e3i * tab[5]).reshape(t, n)
    si = (sli + e3r * tab[5] + e3i * tab[4]).reshape(t, n)
    out = 0 if reverse else ns - 1
    return sr, si, er[out:out + 1, :], ei[out:out + 1, :]


def _scan_consts(t):
    import numpy as np
    ns = t // SCAN_SUB
    r = np.arange(t)
    same = (r[:, None] // SCAN_SUB) == (r[None, :] // SCAN_SUB)
    sums = (np.arange(ns)[:, None] == (r[None, :] // SCAN_SUB))
    tri = []
    for keep in (r[None, :] <= r[:, None], r[None, :] >= r[:, None]):
        tri.append(np.concatenate([same & keep, sums], axis=0).astype(np.float32))
    ex = ((r[:, None] // SCAN_SUB) == np.arange(ns)[None, :]).astype(np.float32)
    return jnp.asarray(np.stack(tri), BF16), jnp.asarray(np.concatenate([ex, ex], axis=1), BF16)


def _scan_tables(lr, li):
    import numpy as np
    den = lr * lr + li * li
    ir, ii = lr / den, -li / den
    mul = lambda a, b: (a[0] * b[0] - a[1] * b[1], a[0] * b[1] + a[1] * b[0])
    pw = {0: (jnp.ones_like(lr), jnp.zeros_like(lr))}
    for e in range(1, 9):
        pw[e] = mul(pw[e - 1], (lr, li))
    for e in range(-1, -5, -1):
        pw[e] = mul(pw[e + 1], (ir, ii))
    powers = jnp.stack([jnp.stack(pw[e]) for e in range(-4, 9)] + [jnp.zeros((2, lr.shape[0]), F32)])
    j = np.arange(SCAN_SUB)
    exps = [4 - j, j - 4, j + 1, j - 3, 3 - j, 8 - j]
    e_idx = np.stack([exps[t] + 4 for t in range(6) for _ in range(2)])
    c_idx = np.tile(np.array([0, 1])[:, None], (6, SCAN_SUB))
    sign = np.where((c_idx == 1) & (np.arange(12)[:, None] >= 6), -1.0, 1.0).astype(np.float32)
    tabs = powers[e_idx, c_idx] * sign[:, :, None]
    lam = powers[np.array([5, 5, 7, 7, 12, 12, 13, 13]), np.array([0, 1, 0, 1, 0, 1, 0, 0])]
    return lam, tabs


SSM_HALVES = 2
SSM_HW = SSM_W // SSM_HALVES
SSM_HN = SSM_N // SSM_HALVES


def _bd_nn(x, w):
    a = w.shape[1]
    return jnp.concatenate([_dot(x[:, h * a:(h + 1) * a], w[h]) for h in range(SSM_HALVES)], axis=1)


def _bd_nt(x, w):
    b = w.shape[2]
    return jnp.concatenate([_dot(x[:, h * b:(h + 1) * b], w[h], NT) for h in range(SSM_HALVES)], axis=1)


def _bd_tn(x, y):
    a, b = x.shape[1] // SSM_HALVES, y.shape[1] // SSM_HALVES
    return jnp.stack([_dot(x[:, h * a:(h + 1) * a], y[:, h * b:(h + 1) * b], TN) for h in range(SSM_HALVES)])


def _ssm_states(u, s0r, s0i, lam_ref, tab_ref, tri_ref, ex_ref, bre, bim, t):
    tab = tuple(tab_ref[k] for k in range(6))
    return _scan_mxu(_bd_nn(u, bre), _bd_nn(u, bim), tab, (lam_ref[2:3, :], lam_ref[3:4, :]),
                     (lam_ref[4:5, :], lam_ref[5:6, :]), tri_ref[0], ex_ref[...], s0r, s0i, t, False)


def _ssm_head(u, z, xr, xi, cre, cim, dskip, wglu, bglu):
    y = _bd_nn(xr, cre) - _bd_nn(xi, cim) + dskip * u
    y2, dgelu = _gelu_and_grad(y)
    gate = _sigmoid(_dot(y2, wglu) + bglu)
    y3 = y2 * gate
    return y2, dgelu, gate, y3


def _with_comm(body, comm, n_in, n_out, grid, mid_step):
    if comm is None:
        return body
    nc = len(comm["args"])
    n_sem = len(comm["scratch"])
    grid = (grid,) if isinstance(grid, int) else tuple(grid)
    total = math.prod(grid)

    def hosted(*refs):
        ins, cin = refs[:n_in], refs[n_in:n_in + nc]
        outs, cout = refs[n_in + nc:n_in + nc + n_out], refs[n_in + nc + n_out:n_in + 2 * nc + n_out]
        rest = refs[n_in + 2 * nc + n_out:]
        scratch, csem = rest[:len(rest) - n_sem], rest[len(rest) - n_sem:]
        start, forward, finish = comm["phases"](cin, cout, *csem)
        step = pl.program_id(0)
        for axis in range(1, len(grid)):
            step = step * grid[axis] + pl.program_id(axis)
        pl.when(step == 0)(start)
        pl.when(step == (mid_step if mid_step >= 0 else total + mid_step))(forward)
        body(*ins, *outs, *scratch)
        pl.when(step == total - 1)(finish)

    return hosted


def _comm_extra(comm):
    if comm is None:
        return [], [], [], [], []
    anyspec = pl.BlockSpec(memory_space=pl.ANY)
    nc = len(comm["args"])
    return comm["args"], [anyspec] * nc, [anyspec] * nc, comm["out_shape"], comm["scratch"]


def _ssm_fwd(ps, scan_ops, bblk, cblk, dskip, wglu, bglu, *, name, t=SEQ_BLOCK, comm=None):
    l = ps.shape[0]
    assert l % t == 0
    nb = l // t
    ns = t // SCAN_SUB
    c_args, c_in, c_out, c_shape, c_scratch = _comm_extra(comm)

    def body(ps_ref, lam_ref, tab_ref, tri_ref, ex_ref, b_ref, c_ref, d_ref, w_ref, bg_ref, ys_ref, chk_ref, xs_ref,
             st_ref):
        @pl.when(pl.program_id(0) == 0)
        def _():
            st_ref[...] = jnp.zeros_like(st_ref)

        chk_ref[...] = jnp.broadcast_to(st_ref[...], chk_ref.shape)
        u = ps_ref[:, :SSM_W].astype(F32)
        z = ps_ref[:, SSM_W:].astype(F32)
        xr, xi, er, ei = _ssm_states(u, st_ref[:, :SSM_N], st_ref[:, SSM_N:], lam_ref, tab_ref, tri_ref, ex_ref,
                                     b_ref[0], b_ref[1], t)
        st_ref[:, :SSM_N] = er
        st_ref[:, SSM_N:] = ei
        xr, xi = xr.astype(BF16), xi.astype(BF16)
        xs_ref[:, :SSM_N] = xr
        xs_ref[:, SSM_N:] = xi
        _, _, _, y3 = _ssm_head(u, z, xr, xi, c_ref[0], c_ref[1], d_ref[...], w_ref[...], bg_ref[...])
        sz, _ = _silu_and_grad(z)
        ys_ref[...] = (y3 * sz).astype(BF16)

    full = lambda shape: pl.BlockSpec(shape, lambda i: (0,) * len(shape))
    return pl.pallas_call(
        _with_comm(body, comm, 10, 3, nb, nb - 1), grid=(nb,),
        in_specs=[pl.BlockSpec((t, 2 * SSM_W), lambda i: (i, 0)), full((8, SSM_N)), full((12, SCAN_SUB, SSM_N)),
                  full((2, t + ns, t)), full((t, 2 * ns)), full((2, SSM_HALVES, SSM_HW, SSM_HN)),
                  full((2, SSM_HALVES, SSM_HN, SSM_HW)), full((1, SSM_W)), full((SSM_W, SSM_W)), full((1, SSM_W))] + c_in,
        out_specs=[pl.BlockSpec((t, SSM_W), lambda i: (i, 0)), pl.BlockSpec((8, 2 * SSM_N), lambda i: (i, 0)),
                   pl.BlockSpec((t, 2 * SSM_N), lambda i: (i, 0))] + c_out,
        out_shape=[jax.ShapeDtypeStruct((l, SSM_W), BF16), jax.ShapeDtypeStruct((nb * 8, 2 * SSM_N), F32),
                   jax.ShapeDtypeStruct((l, 2 * SSM_N), BF16)] + c_shape,
        scratch_shapes=[pltpu.VMEM((1, 2 * SSM_N), F32)] + c_scratch,
        compiler_params=_params("arbitrary"), name=name)(ps, *scan_ops, bblk, cblk, dskip, wglu, bglu, *c_args)


def _ssm_bwd(ps, dys, chk, states, scan_ops, bblk, cblk, dskip, wglu, bglu, *, name, t=SEQ_BLOCK, comm=None):
    l = ps.shape[0]
    assert l % t == 0
    nb = l // t
    ns = t // SCAN_SUB
    c_args, c_in, c_out, c_shape, c_scratch = _comm_extra(comm)

    def body(ps_ref, dys_ref, chk_ref, xs_ref, lam_ref, tab_ref, tri_ref, ex_ref, b_ref, c_ref, d_ref, w_ref, bg_ref,
             dps_ref, db_ref, dc_ref, dw_acc, sums_acc, gc_ref, db_acc, dc_acc):
        n = pl.program_id(0)

        @pl.when(n == 0)
        def _():
            gc_ref[...] = jnp.zeros_like(gc_ref)
            db_acc[...] = jnp.zeros_like(db_acc)
            dc_acc[...] = jnp.zeros_like(dc_acc)
            dw_acc[...] = jnp.zeros_like(dw_acc)
            sums_acc[...] = jnp.zeros_like(sums_acc)

        row = lax.broadcasted_iota(jnp.int32, (t, 1), 0)
        u = ps_ref[:, :SSM_W].astype(F32)
        z = ps_ref[:, SSM_W:].astype(F32)
        s0r, s0i = chk_ref[0:1, :SSM_N], chk_ref[0:1, SSM_N:]
        xr, xi = xs_ref[:, :SSM_N], xs_ref[:, SSM_N:]
        dskip = d_ref[...]
        y2, dgelu, gate, y3 = _ssm_head(u, z, xr, xi, c_ref[0], c_ref[1], dskip, w_ref[...], bg_ref[...])
        sz, dsz = _silu_and_grad(z)
        dys_v = dys_ref[...]
        dps_ref[:, SSM_W:] = (dys_v * y3 * dsz).astype(BF16)
        dy3 = dys_v * sz
        da = dy3 * y2 * gate * (1.0 - gate)
        dy2 = dy3 * gate + _dot(da, w_ref[...], NT)
        dw_acc[...] += _dot(y2, da, TN)
        dy = dy2 * dgelu
        sums_acc[2:3, :SSM_W] += jnp.sum(dy * u, axis=0, keepdims=True)
        sums_acc[3:4, :SSM_W] += jnp.sum(da, axis=0, keepdims=True)
        dc_acc[0] += _bd_tn(dy, xr)
        dc_acc[1] += -_bd_tn(dy, xi)
        rev_tab = tuple(tab_ref[k] for k in range(6, 12))
        gr, gi, gcr, gci = _scan_mxu(
            _bd_nt(dy, c_ref[0]), -_bd_nt(dy, c_ref[1]), rev_tab, (lam_ref[2:3, :], -lam_ref[3:4, :]),
            (lam_ref[4:5, :], -lam_ref[5:6, :]), tri_ref[1], ex_ref[...], gc_ref[:, :SSM_N], gc_ref[:, SSM_N:], t, True)
        gc_ref[:, :SSM_N] = gcr
        gc_ref[:, SSM_N:] = gci
        db_acc[0] += _bd_tn(u, gr)
        db_acc[1] += _bd_tn(u, gi)
        du = dskip * dy + _bd_nt(gr, b_ref[0]) + _bd_nt(gi, b_ref[1])
        dps_ref[:, :SSM_W] = du.astype(BF16)
        spr = jnp.where(row == 0, s0r, pltpu.roll(xr.astype(F32), 1, 0))
        spi = jnp.where(row == 0, s0i, pltpu.roll(xi.astype(F32), 1, 0))
        sums_acc[0:1, :] += jnp.sum(gr * spr + gi * spi, axis=0, keepdims=True)
        sums_acc[1:2, :] += jnp.sum(gi * spr - gr * spi, axis=0, keepdims=True)

        @pl.when(n == nb - 1)
        def _():
            per_half = SSM_GROUPS // SSM_HALVES
            for k in range(2):
                for g in range(SSM_GROUPS):
                    h, gl = divmod(g, per_half)
                    c0, p0 = gl * SSM_GROUP, gl * SSM_STATE
                    db_ref[k, g * SSM_GROUP:(g + 1) * SSM_GROUP, :] = db_acc[k, h, c0:c0 + SSM_GROUP, p0:p0 + SSM_STATE]
                    dc_ref[k, g * SSM_GROUP:(g + 1) * SSM_GROUP, :] = dc_acc[k, h, c0:c0 + SSM_GROUP, p0:p0 + SSM_STATE]

    full = lambda shape: pl.BlockSpec(shape, lambda n: (0,) * len(shape))
    return pl.pallas_call(
        _with_comm(body, comm, 13, 5, nb, 0), grid=(nb,),
        in_specs=[pl.BlockSpec((t, 2 * SSM_W), lambda n: (nb - 1 - n, 0)),
                  pl.BlockSpec((t, SSM_W), lambda n: (nb - 1 - n, 0)),
                  pl.BlockSpec((8, 2 * SSM_N), lambda n: (nb - 1 - n, 0)),
                  pl.BlockSpec((t, 2 * SSM_N), lambda n: (nb - 1 - n, 0)),
                  full((8, SSM_N)), full((12, SCAN_SUB, SSM_N)), full((2, t + ns, t)), full((t, 2 * ns)),
                  full((2, SSM_HALVES, SSM_HW, SSM_HN)), full((2, SSM_HALVES, SSM_HN, SSM_HW)), full((1, SSM_W)),
                  full((SSM_W, SSM_W)), full((1, SSM_W))] + c_in,
        out_specs=[pl.BlockSpec((t, 2 * SSM_W), lambda n: (nb - 1 - n, 0)), full((2, SSM_W, SSM_STATE)),
                   full((2, SSM_W, SSM_STATE)), full((SSM_W, SSM_W)), full((8, SSM_N))] + c_out,
        out_shape=[jax.ShapeDtypeStruct((l, 2 * SSM_W), BF16),
                   jax.ShapeDtypeStruct((2, SSM_W, SSM_STATE), F32),
                   jax.ShapeDtypeStruct((2, SSM_W, SSM_STATE), F32),
                   jax.ShapeDtypeStruct((SSM_W, SSM_W), F32),
                   jax.ShapeDtypeStruct((8, SSM_N), F32)] + c_shape,
        scratch_shapes=[pltpu.VMEM((1, 2 * SSM_N), F32), pltpu.VMEM((2, SSM_HALVES, SSM_HW, SSM_HN), F32),
                        pltpu.VMEM((2, SSM_HALVES, SSM_HW, SSM_HN), F32)] + c_scratch,
        compiler_params=_params("arbitrary"), name=name)(ps, dys, chk, states, *scan_ops, bblk, cblk, dskip, wglu, bglu,
                                                         *c_args)


def _pool_count(i, t):
    pos = lax.broadcasted_iota(jnp.int32, (t, POOL_W), 0) + i * t + 1
    col = lax.broadcasted_iota(jnp.int32, (t, POOL_W), 1)
    win = jnp.where(col < POOL_GW, 2, jnp.where(col < 2 * POOL_GW, 4, jnp.where(col < 3 * POOL_GW, 8, 16)))
    return 1.0 / jnp.minimum(pos, win).astype(F32), col


def _window_sums(ext, n_rows, forward):
    col = lax.broadcasted_iota(jnp.int32, ext.shape, 1)
    sh = (lambda a, d: pltpu.roll(a, d, 0)) if forward else (lambda a, d: pltpu.roll(a, n_rows - d, 0))
    a2 = ext + sh(ext, 1)
    a4 = a2 + sh(a2, 2)
    a8 = a4 + sh(a4, 4)
    a16 = a8 + sh(a8, 8)
    return jnp.where(col < POOL_GW, a2, jnp.where(col < 2 * POOL_GW, a4, jnp.where(col < 3 * POOL_GW, a8, a16)))


def _pool_mix(pooled, wp_ref):
    return jnp.concatenate([_dot(pooled[:, g * POOL_GW:(g + 1) * POOL_GW], wp_ref[g]) for g in range(4)], axis=1)


def _pool_pooled(i, cur_u, prev_u, t):
    prev = jnp.where(i > 0, prev_u, 0.0)
    ext = jnp.concatenate([prev, cur_u], axis=0)
    inv_cnt, _ = _pool_count(i, t)
    return _window_sums(ext, t + POOL_HALO, True)[POOL_HALO:, :] * inv_cnt - cur_u


def _pool_fwd(pp, wpool, pscale, *, name, t=SEQ_BLOCK):
    l = pp.shape[0]
    t = min(t, l)

    def body(cur_ref, prev_ref, wp_ref, sc_ref, yp_ref):
        i = pl.program_id(0)
        pooled = _pool_pooled(i, cur_ref[:, :POOL_W].astype(F32), prev_ref[...].astype(F32), t)
        lin = _pool_mix(pooled, wp_ref)
        sz, _ = _silu_and_grad(cur_ref[:, POOL_W:].astype(F32))
        yp_ref[...] = (lin * sc_ref[...] * sz).astype(BF16)

    hb = t // POOL_HALO
    return pl.pallas_call(
        body, grid=(l // t,),
        in_specs=[pl.BlockSpec((t, 2 * POOL_W), lambda i: (i, 0)),
                  pl.BlockSpec((POOL_HALO, POOL_W), lambda i: (jnp.maximum(i * hb - 1, 0), 0)),
                  pl.BlockSpec((4, POOL_GW, POOL_GW), lambda i: (0, 0, 0)),
                  pl.BlockSpec((1, POOL_W), lambda i: (0, 0))],
        out_specs=pl.BlockSpec((t, POOL_W), lambda i: (i, 0)),
        out_shape=jax.ShapeDtypeStruct((l, POOL_W), BF16),
        compiler_params=_params("parallel"), name=name)(pp, pp, wpool, pscale)


def _pool_bwd(pp, dyp, wpool, pscale, *, name, t=SEQ_BLOCK):
    l = pp.shape[0]
    t = min(t, l)
    nb = l // t

    def body(cur_ref, prev_ref, dyp_ref, wp_ref, sc_ref, dpp_ref, dwp_ref, sums_ref, carry_ref):
        n = pl.program_id(0)
        i = nb - 1 - n

        @pl.when(n == 0)
        def _():
            carry_ref[...] = jnp.zeros_like(carry_ref)
            dwp_ref[...] = jnp.zeros_like(dwp_ref)
            sums_ref[...] = jnp.zeros_like(sums_ref)

        cur_u = cur_ref[:, :POOL_W].astype(F32)
        pooled = _pool_pooled(i, cur_u, prev_ref[...].astype(F32), t)
        lin = _pool_mix(pooled, wp_ref)
        sz, dsz = _silu_and_grad(cur_ref[:, POOL_W:].astype(F32))
        dyp_v = dyp_ref[...]
        scale = sc_ref[...]
        dpp_ref[:, POOL_W:] = (dyp_v * lin * scale * dsz).astype(BF16)
        dpre = dyp_v * sz
        sums_ref[0:1, :] += jnp.sum(dpre * lin, axis=0, keepdims=True)
        dlin = dpre * scale
        dpooled = []
        for g in range(4):
            dl = dlin[:, g * POOL_GW:(g + 1) * POOL_GW]
            dwp_ref[g] += _dot(pooled[:, g * POOL_GW:(g + 1) * POOL_GW], dl, TN)
            dpooled.append(_dot(dl, wp_ref[g], NT))
        dpooled = jnp.concatenate(dpooled, axis=1)
        inv_cnt, _ = _pool_count(i, t)
        dq = dpooled * inv_cnt
        ext = jnp.concatenate([dq, carry_ref[...]], axis=0)
        du = _window_sums(ext, t + POOL_HALO, False)[:t, :] - dpooled
        dpp_ref[:, :POOL_W] = du.astype(BF16)
        carry_ref[...] = dq[:POOL_HALO, :]

    hb = t // POOL_HALO
    return pl.pallas_call(
        body, grid=(nb,),
        in_specs=[pl.BlockSpec((t, 2 * POOL_W), lambda n: (nb - 1 - n, 0)),
                  pl.BlockSpec((POOL_HALO, POOL_W), lambda n: (jnp.maximum((nb - 1 - n) * hb - 1, 0), 0)),
                  pl.BlockSpec((t, POOL_W), lambda n: (nb - 1 - n, 0)),
                  pl.BlockSpec((4, POOL_GW, POOL_GW), lambda n: (0, 0, 0)),
                  pl.BlockSpec((1, POOL_W), lambda n: (0, 0))],
        out_specs=[pl.BlockSpec((t, 2 * POOL_W), lambda n: (nb - 1 - n, 0)),
                   pl.BlockSpec((4, POOL_GW, POOL_GW), lambda n: (0, 0, 0)),
                   pl.BlockSpec((8, POOL_W), lambda n: (0, 0))],
        out_shape=[jax.ShapeDtypeStruct((l, 2 * POOL_W), BF16), jax.ShapeDtypeStruct((4, POOL_GW, POOL_GW), F32),
                   jax.ShapeDtypeStruct((8, POOL_W), F32)],
        scratch_shapes=[pltpu.VMEM((POOL_HALO, POOL_W), F32)],
        compiler_params=_params("arbitrary"), name=name)(pp, pp, dyp, wpool, pscale)


def _merge_fwd(ya, ys, yp, wa, ws, wp, pg, wout, x, gate, *, name, tm=512):
    l, d = x.shape
    tm = min(tm, l)

    def body(ya_ref, ys_ref, yp_ref, wa_ref, ws_ref, wp_ref, pg_ref, wo_ref, x_ref, g_ref,
             xn_ref, mg_ref, ba_ref, bs_ref, bp_ref, out_ref):
        acc = None
        for k, (y_ref, w_ref, b_ref) in enumerate(((ya_ref, wa_ref, ba_ref), (ys_ref, ws_ref, bs_ref),
                                                   (yp_ref, wp_ref, bp_ref))):
            br = _dot(y_ref[...], w_ref[...])
            b_ref[...] = br.astype(BF16)
            term = _sigmoid(pg_ref[:, k * d:(k + 1) * d].astype(F32)) * br
            acc = term if acc is None else acc + term
        merged = acc.astype(BF16)
        mg_ref[...] = merged
        out = _dot(merged, wo_ref[...])
        out_ref[...] = out.astype(BF16)
        xn_ref[...] = x_ref[...] + g_ref[...] * out

    rowy = pl.BlockSpec((tm, ATT_W), lambda i: (i, 0))
    wsp = pl.BlockSpec((ATT_W, d), lambda i: (0, 0))
    rowd = pl.BlockSpec((tm, d), lambda i: (i, 0))
    return pl.pallas_call(
        body, grid=(l // tm,),
        in_specs=[rowy, rowy, rowy, wsp, wsp, wsp, pl.BlockSpec((tm, 3 * d), lambda i: (i, 0)),
                  pl.BlockSpec((d, d), lambda i: (0, 0)), rowd, pl.BlockSpec((1, d), lambda i: (0, 0))],
        out_specs=[rowd] * 6,
        out_shape=[jax.ShapeDtypeStruct((l, d), F32)] + [jax.ShapeDtypeStruct((l, d), BF16)] * 5,
        compiler_params=_params("parallel"), name=name)(ya, ys, yp, wa, ws, wp, pg, wout, x, gate)


def _merge_bwd(dx, out, gate, wout, pg, brs, wbrs, ys, merged, *, name, tm=256, comm=None):
    l, d = dx.shape
    tm = min(tm, l)
    nb = l // tm
    w = ys[0].shape[1]

    def body(dx_ref, out_ref, g_ref, w_ref, pg_ref, ba_ref, bs_ref, bp_ref, wa_ref, ws_ref, wp_ref,
             ya_ref, ys_ref, yp_ref, mg_ref,
             dya_ref, dys_ref, dyp_ref, dpg_ref, sums_ref, dwa_ref, dws_ref, dwp_ref, dwo_ref, acc_br, acc_out):
        i = pl.program_id(0)

        @pl.when(i == 0)
        def _():
            sums_ref[...] = jnp.zeros_like(sums_ref)
            acc_br[...] = jnp.zeros_like(acc_br)
            acc_out[...] = jnp.zeros_like(acc_out)

        dxv = dx_ref[...]
        sums_ref[0:1, :] += jnp.sum(dxv * out_ref[...].astype(F32), axis=0, keepdims=True)
        dmo = (dxv * g_ref[...]).astype(BF16)
        acc_out[...] += _dot(mg_ref[...], dmo, TN)
        dmerged = _dot(dmo, w_ref[...], NT)
        branches = ((ba_ref, wa_ref, ya_ref, dya_ref), (bs_ref, ws_ref, ys_ref, dys_ref), (bp_ref, wp_ref, yp_ref, dyp_ref))
        for k, (b_ref, wk_ref, y_ref, dy_ref) in enumerate(branches):
            gk = _sigmoid(pg_ref[:, k * d:(k + 1) * d].astype(F32))
            dbr = (dmerged * gk).astype(BF16)
            dpg_ref[:, k * d:(k + 1) * d] = (dmerged * b_ref[...].astype(F32) * gk * (1.0 - gk)).astype(BF16)
            dy_ref[...] = _dot(dbr, wk_ref[...], NT)
            acc_br[k] += _dot(y_ref[...], dbr, TN)

        @pl.when(i == nb - 1)
        def _():
            for k, dw_ref in enumerate((dwa_ref, dws_ref, dwp_ref)):
                dw_ref[...] = acc_br[k].astype(BF16)
            dwo_ref[...] = acc_out[...].astype(BF16)

    row = pl.BlockSpec((tm, d), lambda i: (i, 0))
    half = pl.BlockSpec((tm, w), lambda i: (i, 0))
    wide = pl.BlockSpec((tm, 3 * d), lambda i: (i, 0))
    const = lambda shape: pl.BlockSpec(shape, lambda i: (0,) * len(shape))
    c_args, c_in, c_out, c_shape, c_scratch = _comm_extra(comm)
    res = pl.pallas_call(
        _with_comm(body, comm, 15, 9, nb, 0), grid=(nb,),
        in_specs=[row, row, const((1, d)), const((d, d)), wide, row, row, row, const((w, d)), const((w, d)), const((w, d)),
                  half, half, half, row] + c_in,
        out_specs=[half, half, half, wide, const((8, d)), const((w, d)), const((w, d)), const((w, d)), const((d, d))] + c_out,
        out_shape=[jax.ShapeDtypeStruct((l, w), F32)] * 3 + [jax.ShapeDtypeStruct((l, 3 * d), BF16),
                                                             jax.ShapeDtypeStruct((8, d), F32)]
                  + [jax.ShapeDtypeStruct((w, d), BF16)] * 3 + [jax.ShapeDtypeStruct((d, d), BF16)] + c_shape,
        scratch_shapes=[pltpu.VMEM((3, w, d), F32), pltpu.VMEM((d, d), F32)] + c_scratch,
        compiler_params=_params("arbitrary"), name=name)(dx, out, gate, wout, pg, *brs, *wbrs, *ys, merged, *c_args)
    return list(res[:9]), list(res[9:])


def _adamw(w, gs, m, v, *, name, tr=256):
    r, c = w.shape
    ns = len(gs)
    p, rs, _ = gs[0].shape
    assert rs * ns == r
    tr = min(tr, rs)
    assert rs % tr == 0
    nr = rs // tr
    c1 = 1.0 / (1.0 - ADAM_B1 ** ADAM_STEP)
    c2 = 1.0 / (1.0 - ADAM_B2 ** ADAM_STEP)

    def body(*refs):
        w_ref, g_refs, (m_ref, v_ref, go_ref, d_ref, mo_ref, vo_ref) = refs[0], refs[1:1 + ns], refs[1 + ns:]
        slab = pl.program_id(0)
        gv = None
        for k, g_ref in enumerate(g_refs):
            gk = g_ref[0].astype(F32)
            for j in range(1, p):
                gk = gk + g_ref[j].astype(F32)
            gv = gk if gv is None else jnp.where(slab == k, gk, gv)
        go_ref[...] = gv
        mn = ADAM_B1 * m_ref[...] + (1.0 - ADAM_B1) * gv
        vn = ADAM_B2 * v_ref[...] + (1.0 - ADAM_B2) * (gv * gv)
        mo_ref[...] = mn
        vo_ref[...] = vn
        d_ref[...] = -ADAM_LR * ((mn * c1) / (jnp.sqrt(vn * c2) + ADAM_EPS) + ADAM_WD * w_ref[...])

    row = pl.BlockSpec((tr, c), lambda s, i: (s * nr + i, 0))
    g_specs = [pl.BlockSpec((p, tr, c), lambda s, i, k=k: (0, jnp.where(s == k, i, 0), 0)) for k in range(ns)]
    return pl.pallas_call(
        body, grid=(ns, nr),
        in_specs=[row] + g_specs + [row, row],
        out_specs=[row] * 4,
        out_shape=[jax.ShapeDtypeStruct((r, c), F32)] * 4,
        compiler_params=_params("arbitrary", "arbitrary"), name=name)(w, *gs, m, v)


def _exchange(arrs, *, scatter, name):
    n = len(arrs)
    out_shape = [jax.ShapeDtypeStruct(a.shape if scatter else (N_DEV,) + a.shape, a.dtype) for a in arrs]

    def body(*refs):
        ins, outs = refs[:n], refs[n:2 * n]
        send_sems, recv_sems, loc_sems = refs[2 * n:]
        me = 4 * lax.axis_index("x") + 2 * lax.axis_index("y") + lax.axis_index("c")
        local = []
        for k in range(n):
            src = ins[k].at[me] if scatter else ins[k]
            cp = pltpu.make_async_copy(src, outs[k].at[me], loc_sems.at[k])
            cp.start()
            local.append(cp)
        remote = []
        for r in range(1, N_DEV):
            peer = me ^ r
            for k in range(n):
                src = ins[k].at[peer] if scatter else ins[k]
                cp = pltpu.make_async_remote_copy(
                    src_ref=src, dst_ref=outs[k].at[me], send_sem=send_sems.at[k, r - 1], recv_sem=recv_sems.at[k, r - 1],
                    device_id=(peer // 4, (peer // 2) % 2, peer % 2), device_id_type=pl.DeviceIdType.MESH)
                cp.start()
                remote.append(cp)
        for cp in remote:
            cp.wait()
        for cp in local:
            cp.wait()

    anyspec = pl.BlockSpec(memory_space=pl.ANY)
    return pl.pallas_call(
        body, in_specs=[anyspec] * n, out_specs=[anyspec] * n, out_shape=out_shape,
        scratch_shapes=[pltpu.SemaphoreType.DMA((n, N_DEV - 1)), pltpu.SemaphoreType.DMA((n, N_DEV - 1)),
                        pltpu.SemaphoreType.DMA((n,))],
        name=name)(*arrs)


def _mesh_place():
    x, y, c = lax.axis_index("x"), lax.axis_index("y"), lax.axis_index("c")
    other_chips = [(1 - x, y), (x, 1 - y), (1 - x, 1 - y)]
    return x, y, c, other_chips


def _gather_two_level(arrs, *, name):
    n = len(arrs)
    plan = _gather_plan(arrs)

    def body(*refs):
        start, forward, finish = plan["phases"](refs[:n], refs[n:2 * n], *refs[2 * n:])
        start()
        forward()
        finish()

    anyspec = pl.BlockSpec(memory_space=pl.ANY)
    return pl.pallas_call(
        body, in_specs=[anyspec] * n, out_specs=[anyspec] * n, out_shape=plan["out_shape"],
        scratch_shapes=plan["scratch"], name=name)(*arrs)


def _gather_plan(arrs):
    n = len(arrs)

    def phases(ins, outs, send_sems, recv_sems, loc_sems):
        x, y, c, chips = _mesh_place()
        me = 4 * x + 2 * y + c
        slot = lambda px, py, pc: 4 * px + 2 * py + pc

        def copy(k, j, src, block, to):
            return pltpu.make_async_remote_copy(
                src_ref=src, dst_ref=outs[k].at[block], send_sem=send_sems.at[k, j], recv_sem=recv_sems.at[k, j],
                device_id=to, device_id_type=pl.DeviceIdType.MESH)

        local = [pltpu.make_async_copy(ins[k], outs[k].at[me], loc_sems.at[k]) for k in range(n)]
        first = []
        for k in range(n):
            first.append(copy(k, 0, ins[k], me, (x, y, 1 - c)))
            for j, chip in enumerate(chips):
                first.append(copy(k, 1 + j, ins[k], me, (*chip, c)))
        passed = [copy(k, 4 + j, outs[k].at[slot(*chip, c)], slot(*chip, c), (x, y, 1 - c))
                  for j, chip in enumerate(chips) for k in range(n)]

        def start():
            for cp in local + first:
                cp.start()

        def forward():
            for j, chip in enumerate(chips):
                for k in range(n):
                    copy(k, 1 + j, ins[k], slot(*chip, c), (x, y, c)).wait_recv()
                    passed[j * n + k].start()

        def finish():
            for k in range(n):
                copy(k, 0, ins[k], slot(x, y, 1 - c), (x, y, c)).wait_recv()
                for j, chip in enumerate(chips):
                    copy(k, 4 + j, ins[k], slot(*chip, 1 - c), (x, y, c)).wait_recv()
            for cp in first + passed:
                cp.wait_send()
            for cp in local:
                cp.wait()

        return start, forward, finish

    return dict(
        args=list(arrs), out_shape=[jax.ShapeDtypeStruct((N_DEV,) + a.shape, a.dtype) for a in arrs],
        scratch=[pltpu.SemaphoreType.DMA((n, 7)), pltpu.SemaphoreType.DMA((n, 7)), pltpu.SemaphoreType.DMA((n,))],
        phases=phases)


def _allreduce_small(small, extra, *, name):
    r, lanes = small.shape
    assert r % 16 == 0
    h = r // 2
    e = extra.shape[0]

    def body(s_ref, x_ref, out_ref, xall_ref, sib_ref, parts_ref, send_sems, recv_sems):
        x, y, c, chips = _mesh_place()
        me = 4 * x + 2 * y + c
        my_chip = 2 * x + y
        sibling = (x, y, 1 - c)
        mine = pl.ds(pl.multiple_of(c * h, 8), h)
        theirs = pl.ds(pl.multiple_of((1 - c) * h, 8), h)

        def remote(j, src, dst, to):
            return pltpu.make_async_remote_copy(src_ref=src, dst_ref=dst, send_sem=send_sems.at[j],
                                                recv_sem=recv_sems.at[j], device_id=to, device_id_type=pl.DeviceIdType.MESH)

        to_sibling = remote(0, s_ref.at[theirs], sib_ref, sibling)
        to_sibling.start()
        xall_ref[me] = x_ref[...]
        extras = []
        for rr in range(1, N_DEV):
            peer = me ^ rr
            cp = remote(4 + rr, x_ref, xall_ref.at[me], (peer // 4, (peer // 2) % 2, peer % 2))
            cp.start()
            extras.append(cp)
        to_sibling.wait_recv()
        parts_ref[my_chip] = s_ref[mine] + sib_ref[...]
        to_chips = [remote(1 + j, parts_ref.at[my_chip], parts_ref.at[my_chip], (px, py, c))
                    for j, (px, py) in enumerate(chips)]
        for cp in to_chips:
            cp.start()
        for cp in to_chips:
            cp.wait_recv()
        out_ref[mine] = (parts_ref[0] + parts_ref[1]) + (parts_ref[2] + parts_ref[3])
        done = remote(4, out_ref.at[mine], out_ref.at[mine], sibling)
        done.start()
        remote(4, out_ref.at[theirs], out_ref.at[theirs], sibling).wait_recv()
        for cp in extras:
            cp.wait()
        to_sibling.wait_send()
        for cp in to_chips:
            cp.wait_send()
        done.wait_send()

    vmem = pl.BlockSpec(memory_space=pltpu.VMEM)
    return pl.pallas_call(
        body, in_specs=[vmem, vmem], out_specs=[vmem, vmem],
        out_shape=[jax.ShapeDtypeStruct((r, lanes), F32), jax.ShapeDtypeStruct((N_DEV, e, lanes), F32)],
        scratch_shapes=[pltpu.VMEM((h, lanes), F32), pltpu.VMEM((4, h, lanes), F32),
                        pltpu.SemaphoreType.DMA((12,)), pltpu.SemaphoreType.DMA((12,))],
        compiler_params=pltpu.CompilerParams(vmem_limit_bytes=VMEM_LIMIT), name=name)(small, extra)


def _sibling_swap(arrs, *, name):
    n = len(arrs)
    plan = _sibling_swap_plan(arrs)

    def body(*refs):
        start, _, finish = plan["phases"](refs[:n], refs[n:2 * n], *refs[2 * n:])
        start()
        finish()

    anyspec = pl.BlockSpec(memory_space=pl.ANY)
    return pl.pallas_call(
        body, in_specs=[anyspec] * n, out_specs=[anyspec] * n, out_shape=plan["out_shape"],
        scratch_shapes=plan["scratch"], name=name)(*arrs)


def _sibling_swap_plan(arrs):
    n = len(arrs)

    def phases(ins, outs, send_sems, recv_sems):
        x, y, c, _ = _mesh_place()
        copies = [pltpu.make_async_remote_copy(
            src_ref=ins[k].at[1 - c], dst_ref=outs[k], send_sem=send_sems.at[k], recv_sem=recv_sems.at[k],
            device_id=(x, y, 1 - c), device_id_type=pl.DeviceIdType.MESH) for k in range(n)]

        def start():
            for cp in copies:
                cp.start()

        def finish():
            for cp in copies:
                cp.wait()

        return start, (lambda: None), finish

    return dict(args=list(arrs), out_shape=[jax.ShapeDtypeStruct(a.shape[1:], a.dtype) for a in arrs],
                scratch=[pltpu.SemaphoreType.DMA((n,)), pltpu.SemaphoreType.DMA((n,))], phases=phases)


def _pair_add(mine, theirs, core, *, name, tr=256):
    _, r, c = mine.shape
    tr = min(tr, r)
    assert r % tr == 0

    def body(core_ref, m_ref, t_ref, o_ref):
        o_ref[...] = (m_ref[0].astype(F32) + t_ref[...].astype(F32)).astype(BF16)

    return pl.pallas_call(
        body,
        grid_spec=pltpu.PrefetchScalarGridSpec(
            num_scalar_prefetch=1, grid=(r // tr,),
            in_specs=[pl.BlockSpec((1, tr, c), lambda i, core_ref: (core_ref[0], i, 0)),
                      pl.BlockSpec((tr, c), lambda i, core_ref: (i, 0))],
            out_specs=pl.BlockSpec((tr, c), lambda i, core_ref: (i, 0))),
        out_shape=jax.ShapeDtypeStruct((r, c), BF16),
        compiler_params=_params("parallel"), name=name)(core, mine, theirs)


def _pair_add_small(mines, theirs, core, *, name):
    n = len(mines)

    def body(core_ref, *refs):
        for m_ref, t_ref, o_ref in zip(refs[:n], refs[n:2 * n], refs[2 * n:]):
            o_ref[...] = (m_ref[0].astype(F32) + t_ref[...].astype(F32)).astype(BF16)

    whole = lambda a: pl.BlockSpec(a.shape, lambda i, core_ref: (0,) * a.ndim)
    return pl.pallas_call(
        body,
        grid_spec=pltpu.PrefetchScalarGridSpec(
            num_scalar_prefetch=1, grid=(1,),
            in_specs=[pl.BlockSpec((1,) + m.shape[1:], lambda i, core_ref: (core_ref[0], 0, 0)) for m in mines]
                     + [whole(t) for t in theirs],
            out_specs=[whole(t) for t in theirs]),
        out_shape=[jax.ShapeDtypeStruct(t.shape, BF16) for t in theirs],
        compiler_params=_params("arbitrary"), name=name)(core, *mines, *theirs)


def _chip_scatter(arrs, *, name):
    n = len(arrs)
    plan = _chip_scatter_plan(arrs)

    def body(*refs):
        start, _, finish = plan["phases"](refs[:n], refs[n:2 * n], *refs[2 * n:])
        start()
        finish()

    anyspec = pl.BlockSpec(memory_space=pl.ANY)
    return pl.pallas_call(
        body, in_specs=[anyspec] * n, out_specs=[anyspec] * n, out_shape=plan["out_shape"],
        scratch_shapes=plan["scratch"], name=name)(*arrs)


def _chip_scatter_plan(arrs):
    n = len(arrs)

    def phases(ins, outs, send_sems, recv_sems, loc_sems):
        x, y, c, chips = _mesh_place()
        mine = 2 * x + y
        local = [pltpu.make_async_copy(ins[k].at[mine], outs[k].at[mine], loc_sems.at[k]) for k in range(n)]
        remote = [pltpu.make_async_remote_copy(
            src_ref=ins[k].at[2 * px + py], dst_ref=outs[k].at[mine], send_sem=send_sems.at[k, j],
            recv_sem=recv_sems.at[k, j], device_id=(px, py, c), device_id_type=pl.DeviceIdType.MESH)
            for j, (px, py) in enumerate(chips) for k in range(n)]

        def start():
            for cp in local + remote:
                cp.start()

        def finish():
            for cp in remote:
                cp.wait()
            for cp in local:
                cp.wait()

        return start, (lambda: None), finish

    return dict(
        args=list(arrs), out_shape=[jax.ShapeDtypeStruct(a.shape, a.dtype) for a in arrs],
        scratch=[pltpu.SemaphoreType.DMA((n, 3)), pltpu.SemaphoreType.DMA((n, 3)), pltpu.SemaphoreType.DMA((n,))],
        phases=phases)


def _ssm_discretize(a_re, a_im, log_dt, b_re, b_im):
    dt = jnp.exp(log_dt)[:, None]
    mag = jnp.exp(a_re * dt)
    lr = mag * jnp.cos(a_im * dt)
    li = mag * jnp.sin(a_im * dt)
    den = a_re * a_re + a_im * a_im
    cr = ((lr - 1.0) * a_re + li * a_im) / den
    ci = (li * a_re - (lr - 1.0) * a_im) / den
    bbr = cr[..., None] * b_re - ci[..., None] * b_im
    bbi = cr[..., None] * b_im + ci[..., None] * b_re
    return lr, li, bbr, bbi


def _ssm_dense(lr, li, bbr, bbi, c_re, c_im):
    scan_ops = _scan_tables(lr.reshape(-1), li.reshape(-1)) + _scan_consts(SEQ_BLOCK)
    per_half = SSM_GROUPS // SSM_HALVES

    def halves(a, rows, cols):
        a = a.reshape(SSM_HALVES, per_half, rows, 1, cols)
        shape = (SSM_HALVES, per_half, rows, per_half, cols)
        on_diagonal = lax.broadcasted_iota(jnp.int32, shape, 1) == lax.broadcasted_iota(jnp.int32, shape, 3)
        return jnp.where(on_diagonal, a, 0.0).reshape(SSM_HALVES, per_half * rows, per_half * cols)

    bblk = jnp.stack([halves(b.transpose(0, 2, 1), SSM_GROUP, SSM_STATE) for b in (bbr, bbi)]).astype(BF16)
    cblk = jnp.stack([halves(c.transpose(0, 2, 1), SSM_STATE, SSM_GROUP) for c in (c_re, c_im)]).astype(BF16)
    return scan_ops, bblk, cblk


def _ssm_extract(db, dc, sums):
    db = db.reshape(2, SSM_GROUPS, SSM_GROUP, SSM_STATE).transpose(0, 1, 3, 2)
    dc = dc.reshape(2, SSM_GROUPS, SSM_GROUP, SSM_STATE)
    dlr = sums[0].reshape(SSM_GROUPS, SSM_STATE)
    dli = sums[1].reshape(SSM_GROUPS, SSM_STATE)
    return dlr, dli, db[0], db[1], dc[0], dc[1]


def _in_groups():
    names = ("q", "k", "v", "u_ssm", "u_pool", "z_att", "z_ssm", "z_pool", "gates")
    sizes = (ATT_W, KV_W, KV_W, SSM_W, POOL_W, ATT_W, SSM_W, POOL_W, 3 * D_MODEL)
    r, lo = {}, 0
    for nm, s in zip(names, sizes):
        r[nm] = (lo, lo + s)
        lo += s
    kv = (r["k"][0], r["v"][1])
    return ((r["q"], r["z_att"], kv), (r["u_ssm"], r["z_ssm"]), (r["u_pool"], r["z_pool"]), (r["gates"],))


IN_GROUPS = _in_groups()


def _layer_fwd(x, lw, li, late=None, comm_attn=None, comm_ssm=None):
    tag = f"l{li}"
    h, (pa, ps, pp, pg), arrived = _ln_proj(x, lw["norm_g"], lw["shift"], lw["scale"], lw["w_in"], IN_GROUPS,
                                            name=f"ln_proj_{tag}", comm=None if late is None else late[0])
    if late is not None:
        lw = {**lw, **late[1](arrived)}
    ya, from_attn = _attn_fwd(pa, lw["sinks"], name=f"attn_fwd_{tag}", comm=comm_attn)
    ys, chk, states, *from_ssm = _ssm_fwd(ps, lw["lam"], lw["bblk"], lw["cblk"], lw["ssm_d"], lw["w_glu"], lw["b_glu"],
                                          name=f"ssm_fwd_{tag}", comm=comm_ssm)
    yp = _pool_fwd(pp, lw["w_pool"], lw["pool_scale"], name=f"pool_fwd_{tag}")
    x_new, merged, ba, bs, bp, out = _merge_fwd(ya, ys, yp, lw["w_br_att"], lw["w_br_ssm"], lw["w_br_pool"], pg,
                                                lw["w_out"], x, lw["gate"], name=f"merge_fwd_{tag}")
    saved = dict(x=x, h=h, pa=pa, ps=ps, pp=pp, pg=pg, ya=ya, ys=ys, yp=yp, chk=chk, states=states, merged=merged,
                 ba=ba, bs=bs, bp=bp, out=out)
    return x_new, saved, lw, list(from_attn), list(from_ssm)


def _layer_bwd(dx, lw, sv, li, later=None, own=None):
    tag = f"l{li}"
    g = {}
    merge_out, swapped = _merge_bwd(
        dx, sv["out"], lw["gate"], lw["w_out"], sv["pg"], (sv["ba"], sv["bs"], sv["bp"]),
        (lw["w_br_att"], lw["w_br_ssm"], lw["w_br_pool"]), (sv["ya"], sv["ys"], sv["yp"]), sv["merged"],
        name=f"merge_bwd_{tag}", comm=None if later is None else later[0])
    dya, dys, dyp, dpg, gate_sums, g["w_br_att"], g["w_br_ssm"], g["w_br_pool"], g["w_out"] = merge_out
    dpa, dsink = _attn_bwd(sv["pa"], lw["sinks"], dya, name=f"attn_bwd_{tag}")
    dps, db_dense, dc_dense, dwglu, ssm_sums, *exchanged = _ssm_bwd(
        sv["ps"], dys, sv["chk"], sv["states"], lw["lam"], lw["bblk"], lw["cblk"], lw["ssm_d"], lw["w_glu"], lw["b_glu"],
        name=f"ssm_bwd_{tag}", comm=None if later is None else later[1](swapped))
    g["w_glu"] = dwglu.astype(BF16)
    dpp, dwpool, pool_sums = _pool_bwd(sv["pp"], dyp, lw["w_pool"], lw["pool_scale"], name=f"pool_bwd_{tag}")
    h = sv["h"]
    dproj = (dpa, dps, dpp, dpg)
    g["w_in"], from_late = _mm_tn_grouped(h, dproj, IN_GROUPS, name=f"dw_in_{tag}",
                                          comm=None if own is None else own({k: g[k] for k in LATE_WEIGHTS}))
    if own is None:
        dh, from_w_in = _mm_nt_grouped(dproj, lw["w_in"], IN_GROUPS, name=f"dh_{tag}"), []
    else:
        dh, from_w_in = _mm_nt_grouped(dproj, lw["w_in"], IN_GROUPS, name=f"dh_{tag}", comm=own({"w_in": g["w_in"]}))
    dx_in, ln_sums = _ln_bwd(sv["x"], dh, dx, lw["norm_g"], lw["scale"], name=f"ln_bwd_{tag}")
    g["dmod"] = jnp.concatenate([ln_sums[0], ln_sums[1], gate_sums[0]])
    g["norm_g"] = ln_sums[2]
    g["attn_sinks"] = dsink[:, 0]
    g["ssm_raw"] = _ssm_extract(db_dense, dc_dense, ssm_sums)
    g["ssm_d"] = ssm_sums[2, :SSM_W]
    g["b_glu"] = ssm_sums[3, :SSM_W]
    g["w_pool"] = dwpool
    g["pool_scale"] = pool_sums[0]
    return dx_in, g, exchanged, list(from_w_in) + list(from_late)


BIG_WEIGHTS = ("w_in", "w_glu", "w_br_att", "w_br_ssm", "w_br_pool", "w_out")
ROW_SHARDED = ("w_glu", "w_out")


LATE_WEIGHTS = BIG_WEIGHTS[1:]


def _full_weights(keys, gathered):
    full = {}
    for k, g in zip(keys, gathered):
        if k in ROW_SHARDED:
            full[k] = g.reshape(N_DEV * g.shape[1], g.shape[2])
        else:
            full[k] = g.transpose(1, 0, 2).reshape(g.shape[1], N_DEV * g.shape[2])
    return full


def _by_destination(keys, grads):
    out = []
    for k in keys:
        g = grads[k]
        if k in ROW_SHARDED:
            out.append(g.reshape(4, 2, g.shape[0] // N_DEV, g.shape[1]).transpose(1, 0, 2, 3))
        else:
            out.append(g.reshape(g.shape[0], 4, 2, g.shape[1] // N_DEV).transpose(2, 1, 0, 3))
    return out


def _prepare_layer(li, mod, norm_g, w_in_full, attn_sinks, disc, ssm_c_re, ssm_c_im, ssm_d, b_glu, w_pool, pool_scale):
    d = D_MODEL
    lr, li_, bbr, bbi = disc
    lam, bblk, cblk = _ssm_dense(lr[li], li_[li], bbr[li], bbi[li], ssm_c_re[li], ssm_c_im[li])
    return dict(
        norm_g=norm_g[li][None, :], shift=mod[li, :d][None, :], scale=mod[li, d:2 * d][None, :],
        gate=mod[li, 2 * d:][None, :], w_in=w_in_full,
        sinks=attn_sinks[li], lam=lam, bblk=bblk, cblk=cblk, ssm_d=ssm_d[li][None, :],
        b_glu=b_glu[li][None, :], w_pool=w_pool[li].astype(BF16), pool_scale=pool_scale[li][None, :])


SMALL_ROWS = 64
SMALL_ORDER = ("norm_g", "attn_sinks", "ssm_d", "b_glu", "w_pool", "pool_scale", "dmod")


def _pack_small(loss, dfinal_g, layer_grads):
    parts = [jnp.broadcast_to(loss.reshape(1), (128,)), dfinal_g]
    for g in layer_grads:
        for k in SMALL_ORDER:
            v = g[k].reshape(-1)
            if v.shape[0] % 128:
                v = jnp.pad(v, (0, 128 - v.shape[0] % 128))
            parts.append(v)
        for v in g["ssm_raw"]:
            parts.append(v.reshape(-1))
    flat = jnp.concatenate(parts)
    return jnp.pad(flat, (0, (-flat.shape[0]) % (SMALL_ROWS * 128))).reshape(-1, 128)


def _unpack_small(flat, shapes):
    out, off = [], 0
    for s in shapes:
        n = int(math.prod(s))
        out.append(flat[off:off + n].reshape(s))
        off += n + (-n) % 128
    return out


def kernel(x, c, norm_g, w_ada, b_ada, w_in, attn_sinks, ssm_a_re, ssm_a_im, ssm_log_dt, ssm_b_re, ssm_b_im, ssm_c_re, ssm_c_im, ssm_d, w_glu, b_glu, w_pool, pool_scale, w_br_att, w_br_ssm, w_br_pool, w_out, final_g, loss_target, m_norm_g, m_w_ada, m_b_ada, m_w_in, m_attn_sinks, m_ssm_a_re, m_ssm_a_im, m_ssm_log_dt, m_ssm_b_re, m_ssm_b_im, m_ssm_c_re, m_ssm_c_im, m_ssm_d, m_w_glu, m_b_glu, m_w_pool, m_pool_scale, m_w_br_att, m_w_br_ssm, m_w_br_pool, m_w_out, m_final_g, v_norm_g, v_w_ada, v_b_ada, v_w_in, v_attn_sinks, v_ssm_a_re, v_ssm_a_im, v_ssm_log_dt, v_ssm_b_re, v_ssm_b_im, v_ssm_c_re, v_ssm_c_im, v_ssm_d, v_w_glu, v_b_glu, v_w_pool, v_pool_scale, v_w_br_att, v_w_br_ssm, v_w_br_pool, v_w_out, v_final_g):
    me = 4 * lax.axis_index("x") + 2 * lax.axis_index("y") + lax.axis_index("c")
    d = D_MODEL
    ada_w = 3 * d // N_DEV

    (c_all,) = _exchange([c.reshape(8, 128)], scatter=False, name="gather_c")
    c_act = jax.nn.silu(c_all.reshape(N_DEV, d))
    b_cols = lax.dynamic_slice(b_ada, (0, me * ada_w), (DEPTH, ada_w))
    mod_part = jnp.concatenate(
        [_mm(c_act, w_ada[li], name=f"ada_fwd_l{li}") + b_cols[li][None, :] for li in range(DEPTH)], axis=0)
    (mod_all,) = _exchange([mod_part], scatter=False, name="gather_mod")
    mod_all = mod_all.reshape(N_DEV, DEPTH, N_DEV, ada_w)
    mod_mine = lax.dynamic_index_in_dim(mod_all, me, axis=2, keepdims=False)
    mod_mine = mod_mine.transpose(1, 0, 2).reshape(DEPTH, 3 * d)

    sharded = dict(w_in=w_in, w_glu=w_glu, w_br_att=w_br_att, w_br_ssm=w_br_ssm, w_br_pool=w_br_pool, w_out=w_out)
    shards = lambda li, keys: [sharded[k][li].astype(BF16) for k in keys]
    disc, disc_vjp = jax.vjp(jax.vmap(_ssm_discretize), ssm_a_re, ssm_a_im, ssm_log_dt, ssm_b_re, ssm_b_im)
    layer = lambda li, gathered_w_in: _prepare_layer(
        li, mod_mine, norm_g, _full_weights(("w_in",), gathered_w_in)["w_in"], attn_sinks, disc, ssm_c_re, ssm_c_im,
        ssm_d, b_glu, w_pool, pool_scale)
    late_weights = lambda gathered: _full_weights(LATE_WEIGHTS, gathered)
    core = lax.axis_index("c").astype(jnp.int32).reshape(1)

    def add_pairs(keys, by_dest, from_sibling, tag):
        flat = {k: (a.reshape(2, -1, a.shape[-1]), b.reshape(-1, b.shape[-1]))
                for k, a, b in zip(keys, by_dest, from_sibling)}
        small = [k for k in keys if k != "w_in"]
        sums = {}
        if "w_in" in flat:
            sums["w_in"] = _pair_add(*flat["w_in"], core, name=f"grads_pair_add_{tag}_w_in")
        if small:
            added = _pair_add_small([flat[k][0] for k in small], [flat[k][1] for k in small], core,
                                    name=f"grads_pair_add_{tag}_late")
            sums.update(zip(small, added))
        return [sums[k].reshape(b.shape) for k, b in zip(keys, from_sibling)]

    def chip_sums_of(keys, grads_li, tag):
        by_dest = _by_destination(keys, grads_li)
        return add_pairs(keys, by_dest, _sibling_swap(by_dest, name=f"grads_sibling_swap_{tag}"), tag)

    layers, saved, grads = [None] * DEPTH, [None] * DEPTH, [None] * DEPTH
    layers[0] = layer(0, _gather_two_level(shards(0, ("w_in",)), name="gather_w_in_l0"))
    xs, saved[0], layers[0], late1, w_in1 = _layer_fwd(
        x[0], layers[0], 0, late=(_gather_plan(shards(0, LATE_WEIGHTS)), late_weights),
        comm_attn=_gather_plan(shards(1, LATE_WEIGHTS)), comm_ssm=_gather_plan(shards(1, ("w_in",))))
    layers[1] = {**layer(1, w_in1), **late_weights(late1)}
    xs, saved[1], _, _, _ = _layer_fwd(xs, layers[1], 1)
    dx, fin_sums = _final_loss(xs, final_g[None, :], loss_target[0])
    loss_part = jnp.sum(fin_sums[1])
    dx, grads[1], _, _ = _layer_bwd(dx, layers[1], saved[1], 1)
    by_dest1 = _by_destination(BIG_WEIGHTS, grads[1])
    dx, grads[0], scattered1, scattered0 = _layer_bwd(
        dx, layers[0], saved[0], 0,
        later=(_sibling_swap_plan(by_dest1),
               lambda swapped: _chip_scatter_plan(add_pairs(BIG_WEIGHTS, by_dest1, swapped, "l1"))),
        own=lambda g: _chip_scatter_plan(chip_sums_of(tuple(g), g, "l0_" + "_".join(g))))
    big = list(zip(scattered0, scattered1))
    grad_x = dx[None]

    small = _pack_small(loss_part, fin_sums[0], grads)
    dmod_rows = jnp.concatenate([grads[li]["dmod"] for li in range(DEPTH)]).reshape(-1, 128)
    small_sum, dmod_gathered = _allreduce_small(small, dmod_rows, name="allreduce_small")
    out = {}

    def adam(name, w, g_slabs, m, v):
        shp = w.shape
        r = int(math.prod(shp[:-1])) if len(shp) > 1 else 1
        w2, m2, v2 = (a.reshape(r, shp[-1]) for a in (w, m, v))
        gs = [g.reshape(g.shape[0], r // len(g_slabs), shp[-1]) for g in g_slabs]
        res = _adamw(w2, gs, m2, v2, name=f"adamw_{name}")
        out[name] = tuple(a.reshape(shp) for a in res)

    flat = small_sum.reshape(-1)
    shapes = [(128,), (d,)]
    for _ in range(DEPTH):
        shapes += [(d,), (N_HEADS,), (SSM_W,), (SSM_W,), (4, POOL_GW, POOL_GW), (POOL_W,), (3 * d,),
                   (SSM_GROUPS, SSM_STATE), (SSM_GROUPS, SSM_STATE), (SSM_GROUPS, SSM_STATE, SSM_GROUP),
                   (SSM_GROUPS, SSM_STATE, SSM_GROUP), (SSM_GROUPS, SSM_GROUP, SSM_STATE), (SSM_GROUPS, SSM_GROUP, SSM_STATE)]
    un = _unpack_small(flat, shapes)
    loss = un[0][0]
    g_final_g = un[1]
    per = 13
    gl = [un[2 + li * per: 2 + (li + 1) * per] for li in range(DEPTH)]
    st = lambda j: jnp.stack([gl[li][j] for li in range(DEPTH)])
    g_norm_g, g_sinks, g_ssm_d, g_b_glu, g_w_pool, g_pool_scale, g_b_ada = (st(j) for j in range(7))
    d_lr, d_li, d_bbr, d_bbi, g_c_re, g_c_im = (st(j) for j in range(7, 13))
    g_a_re, g_a_im, g_log_dt, g_b_re, g_b_im = disc_vjp((d_lr, d_li, d_bbr, d_bbi))

    dmod_all = lax.dynamic_slice(dmod_gathered.reshape(N_DEV, DEPTH, 3 * d), (0, 0, me * ada_w), (N_DEV, DEPTH, ada_w))
    dmod_all = dmod_all.transpose(1, 0, 2)
    g_w_ada = jnp.stack([_mm_tn(c_act, dmod_all[li], tm=d, tn=ada_w, tk=N_DEV, name=f"dw_ada_l{li}") for li in range(DEPTH)])

    adam("w_ada", w_ada, [g_w_ada[None]], m_w_ada, v_w_ada)
    adam("w_in", w_in, big[0], m_w_in, v_w_in)
    adam("w_glu", w_glu, big[1], m_w_glu, v_w_glu)
    adam("w_br_att", w_br_att, big[2], m_w_br_att, v_w_br_att)
    adam("w_br_ssm", w_br_ssm, big[3], m_w_br_ssm, v_w_br_ssm)
    adam("w_br_pool", w_br_pool, big[4], m_w_br_pool, v_w_br_pool)
    adam("w_out", w_out, big[5], m_w_out, v_w_out)

    small_names = ["norm_g", "b_ada", "attn_sinks", "ssm_a_re", "ssm_a_im", "ssm_log_dt", "ssm_b_re", "ssm_b_im",
                   "ssm_c_re", "ssm_c_im", "ssm_d", "b_glu", "w_pool", "pool_scale", "final_g"]
    small_w = [norm_g, b_ada, attn_sinks, ssm_a_re, ssm_a_im, ssm_log_dt, ssm_b_re, ssm_b_im, ssm_c_re, ssm_c_im,
               ssm_d, b_glu, w_pool, pool_scale, final_g]
    small_m = [m_norm_g, m_b_ada, m_attn_sinks, m_ssm_a_re, m_ssm_a_im, m_ssm_log_dt, m_ssm_b_re, m_ssm_b_im,
               m_ssm_c_re, m_ssm_c_im, m_ssm_d, m_b_glu, m_w_pool, m_pool_scale, m_final_g]
    small_v = [v_norm_g, v_b_ada, v_attn_sinks, v_ssm_a_re, v_ssm_a_im, v_ssm_log_dt, v_ssm_b_re, v_ssm_b_im,
               v_ssm_c_re, v_ssm_c_im, v_ssm_d, v_b_glu, v_w_pool, v_pool_scale, v_final_g]
    small_g = [g_norm_g, g_b_ada, g_sinks, g_a_re, g_a_im, g_log_dt, g_b_re, g_b_im, g_c_re, g_c_im,
               g_ssm_d, g_b_glu, g_w_pool, g_pool_scale, g_final_g]

    for nm, w, g, m, v in zip(small_names, small_w, small_g, small_m, small_v):
        adam(nm, w, [g[None]], m, v)

    order = ["norm_g", "w_ada", "b_ada", "w_in", "attn_sinks", "ssm_a_re", "ssm_a_im", "ssm_log_dt", "ssm_b_re",
             "ssm_b_im", "ssm_c_re", "ssm_c_im", "ssm_d", "w_glu", "b_glu", "w_pool", "pool_scale", "w_br_att",
             "w_br_ssm", "w_br_pool", "w_out", "final_g"]
    return (loss, grad_x, *[out[k][0] for k in order], *[out[k][1] for k in order],
            *[out[k][2] for k in order], *[out[k][3] for k in order])
```

```python
import functools
import math

import jax
import jax.numpy as jnp
from jax import lax
from jax.experimental import pallas as pl
from jax.experimental.pallas import tpu as pltpu

F32 = jnp.float32
BF16 = jnp.bfloat16

N_DEV = 8
D_MODEL = 1024
DEPTH = 2
CHUNK = 64
N_HEADS = 8
N_KV_HEADS = 2
HEAD_DIM = 64
Q_PER_KV = N_HEADS // N_KV_HEADS
WINDOW = 128
ATT_W = 512
KV_W = 128
SSM_W = 512
SSM_GROUP = 16
SSM_GROUPS = 32
SSM_STATE = 64
SSM_N = SSM_GROUPS * SSM_STATE
POOL_W = 512
POOL_WINDOWS = (2, 4, 8, 16)
POOL_GW = 128
POOL_HALO = 16
EPS = 1e-6
NEG_INF = -1e30
ADAM_LR = 0.001
ADAM_B1 = 0.9
ADAM_B2 = 0.999
ADAM_EPS = 1e-08
ADAM_WD = 0.01
ADAM_STEP = 10

SEQ_BLOCK = 256
ATT_BLOCK = 128
VMEM_LIMIT = 56 * 1024 * 1024

NN = (((1,), (0,)), ((), ()))
NT = (((1,), (1,)), ((), ()))
TN = (((0,), (0,)), ((), ()))


def _dot(a, b, dims=NN):
    return lax.dot_general(a.astype(BF16), b.astype(BF16), dims, preferred_element_type=F32)


def _params(*sem):
    return pltpu.CompilerParams(dimension_semantics=sem, vmem_limit_bytes=VMEM_LIMIT)


def _sigmoid(x):
    return 0.5 + 0.5 * jnp.tanh(0.5 * x)


def _silu_and_grad(z):
    s = _sigmoid(z)
    return z * s, s * (1.0 + z * (1.0 - s))


_GELU_K = math.sqrt(2.0 / math.pi)


def _gelu_and_grad(x):
    inner = _GELU_K * (x + 0.044715 * x * x * x)
    t = jnp.tanh(inner)
    val = 0.5 * x * (1.0 + t)
    grad = 0.5 * (1.0 + t) + 0.5 * x * (1.0 - t * t) * _GELU_K * (1.0 + 3.0 * 0.044715 * x * x)
    return val, grad


def _mm(a, b, *, nt=False, out_dtype=F32, tm=1024, tn=1024, name, comm=None):
    m, k = a.shape
    n = b.shape[0] if nt else b.shape[1]
    tm, tn = min(tm, m), min(tn, n)
    assert m % tm == 0 and n % tn == 0
    dims = NT if nt else NN
    grid = (m // tm, n // tn)
    c_args, c_in, c_out, c_shape, c_scratch = _comm_extra(comm)

    def body(a_ref, b_ref, o_ref):
        o_ref[...] = _dot(a_ref[...], b_ref[...], dims).astype(out_dtype)

    b_spec = pl.BlockSpec((tn, k), lambda i, j: (j, 0)) if nt else pl.BlockSpec((k, tn), lambda i, j: (0, j))
    res = pl.pallas_call(
        _with_comm(body, comm, 2, 1, grid, -1), grid=grid,
        in_specs=[pl.BlockSpec((tm, k), lambda i, j: (i, 0)), b_spec] + c_in,
        out_specs=[pl.BlockSpec((tm, tn), lambda i, j: (i, j))] + c_out,
        out_shape=[jax.ShapeDtypeStruct((m, n), out_dtype)] + c_shape,
        scratch_shapes=c_scratch,
        compiler_params=_params(*(("arbitrary",) * 2 if comm else ("parallel",) * 2)), name=name)(a, b, *c_args)
    return (res[0], list(res[1:])) if comm else res[0]


def _grouped_pieces(groups):
    out = []
    for ranges in groups:
        off, pieces = 0, []
        for lo, hi in ranges:
            pieces.append((off, lo, hi))
            off += hi - lo
        out.append(pieces)
    return out


def _mm_nt_grouped(ds, w, groups, *, out_dtype=F32, tm=512, tn=512, name, comm=None):
    m = ds[0].shape[0]
    n, k = w.shape
    nd = len(ds)
    pieces = _grouped_pieces(groups)
    grid = (m // tm, n // tn)
    c_args, c_in, c_out, c_shape, c_scratch = _comm_extra(comm)

    def body(*refs):
        d_refs, w_ref, o_ref = refs[:nd], refs[nd], refs[nd + 1]
        acc = None
        for d_ref, plist in zip(d_refs, pieces):
            for off, lo, hi in plist:
                term = _dot(d_ref[:, off:off + hi - lo], w_ref[:, lo:hi], NT)
                acc = term if acc is None else acc + term
        o_ref[...] = acc.astype(out_dtype)

    in_specs = [pl.BlockSpec((tm, a.shape[1]), lambda i, j: (i, 0)) for a in ds] + [pl.BlockSpec((tn, k), lambda i, j: (j, 0))]
    res = pl.pallas_call(
        _with_comm(body, comm, nd + 1, 1, grid, -1), grid=grid, in_specs=in_specs + c_in,
        out_specs=[pl.BlockSpec((tm, tn), lambda i, j: (i, j))] + c_out,
        out_shape=[jax.ShapeDtypeStruct((m, n), out_dtype)] + c_shape,
        scratch_shapes=c_scratch,
        compiler_params=_params(*(("arbitrary",) * 2 if comm else ("parallel",) * 2)), name=name)(*ds, w, *c_args)
    return (res[0], list(res[1:])) if comm else res[0]


def _mm_tn(a, b, *, out_dtype=F32, tm=1024, tn=1024, tk=1024, name, comm=None):
    k, m = a.shape
    n = b.shape[1]
    assert m % min(tm, m) == 0 and n % min(tn, n) == 0 and k % min(tk, k) == 0
    tm, tn, tk = min(tm, m), min(tn, n), min(tk, k)
    nk = k // tk
    grid = (m // tm, n // tn, nk)
    c_args, c_in, c_out, c_shape, c_scratch = _comm_extra(comm)

    def body(a_ref, b_ref, o_ref, acc_ref):
        kk = pl.program_id(2)

        @pl.when(kk == 0)
        def _():
            acc_ref[...] = jnp.zeros_like(acc_ref)

        acc_ref[...] += _dot(a_ref[...], b_ref[...], TN)

        @pl.when(kk == nk - 1)
        def _():
            o_ref[...] = acc_ref[...].astype(out_dtype)

    res = pl.pallas_call(
        _with_comm(body, comm, 2, 1, grid, -1), grid=grid,
        in_specs=[pl.BlockSpec((tk, tm), lambda i, j, kk: (kk, i)), pl.BlockSpec((tk, tn), lambda i, j, kk: (kk, j))] + c_in,
        out_specs=[pl.BlockSpec((tm, tn), lambda i, j, kk: (i, j))] + c_out,
        out_shape=[jax.ShapeDtypeStruct((m, n), out_dtype)] + c_shape,
        scratch_shapes=[pltpu.VMEM((tm, tn), F32)] + c_scratch,
        compiler_params=_params(*(("arbitrary",) * 3 if comm else ("parallel", "parallel", "arbitrary"))),
        name=name)(a, b, *c_args)
    return (res[0], list(res[1:])) if comm else res[0]


def _mm_tn_grouped(a, bs, groups, *, tm=512, tk=512, name, comm=None):
    k, m = a.shape
    tm, tk = min(tm, m), min(tk, k)
    assert m % tm == 0 and k % tk == 0
    nk, nb = k // tk, len(bs)
    pieces = _grouped_pieces(groups)
    n = sum(b.shape[1] for b in bs)
    grid = (m // tm, nk)
    c_args, c_in, c_out, c_shape, c_scratch = _comm_extra(comm)

    def body(a_ref, *refs):
        b_refs, o_ref, acc_refs = refs[:nb], refs[nb], refs[nb + 1:]
        kk = pl.program_id(1)
        av = a_ref[...]
        for b_ref, acc_ref, plist in zip(b_refs, acc_refs, pieces):
            @pl.when(kk == 0)
            def _():
                acc_ref[...] = jnp.zeros_like(acc_ref)

            acc_ref[...] += _dot(av, b_ref[...], TN)

            @pl.when(kk == nk - 1)
            def _():
                for off, lo, hi in plist:
                    o_ref[:, lo:hi] = acc_ref[:, off:off + hi - lo].astype(BF16)

    res = pl.pallas_call(
        _with_comm(body, comm, 1 + nb, 1, grid, -1), grid=grid,
        in_specs=[pl.BlockSpec((tk, tm), lambda i, kk: (kk, i))]
                 + [pl.BlockSpec((tk, b.shape[1]), lambda i, kk: (kk, 0)) for b in bs] + c_in,
        out_specs=[pl.BlockSpec((tm, n), lambda i, kk: (i, 0))] + c_out,
        out_shape=[jax.ShapeDtypeStruct((m, n), BF16)] + c_shape,
        scratch_shapes=[pltpu.VMEM((tm, b.shape[1]), F32) for b in bs] + c_scratch,
        compiler_params=_params("arbitrary", "arbitrary"), name=name)(a, *bs, *c_args)
    return res[0], list(res[1:])


def _ln_proj(x, g, shift, scale, w, groups, *, name, tm=512, comm=None):
    l, d = x.shape
    tm = min(tm, l)
    nb = l // tm
    pieces = _grouped_pieces(groups)
    widths = [sum(hi - lo for _, lo, hi in plist) for plist in pieces]
    nw = len(pieces)
    c_args, c_in, c_out, c_shape, c_scratch = _comm_extra(comm)

    def body(x_ref, g_ref, sh_ref, sc_ref, w_ref, h_ref, *p_refs):
        xv = x_ref[...]
        n = xv * lax.rsqrt(jnp.mean(xv * xv, axis=-1, keepdims=True) + EPS)
        h = ((n * g_ref[...]) * (1.0 + sc_ref[...]) + sh_ref[...]).astype(BF16)
        h_ref[...] = h
        for p_ref, plist in zip(p_refs, pieces):
            for off, lo, hi in plist:
                p_ref[:, off:off + hi - lo] = _dot(h, w_ref[:, lo:hi]).astype(BF16)

    vec = pl.BlockSpec((1, d), lambda i: (0, 0))
    row = lambda n: pl.BlockSpec((tm, n), lambda i: (i, 0))
    res = pl.pallas_call(
        _with_comm(body, comm, 5, 1 + nw, nb, -1), grid=(nb,),
        in_specs=[row(d), vec, vec, vec, pl.BlockSpec(w.shape, lambda i: (0, 0))] + c_in,
        out_specs=[row(d)] + [row(n) for n in widths] + c_out,
        out_shape=[jax.ShapeDtypeStruct((l, d), BF16)] + [jax.ShapeDtypeStruct((l, n), BF16) for n in widths] + c_shape,
        scratch_shapes=c_scratch,
        compiler_params=_params("arbitrary"), name=name)(x, g, shift, scale, w, *c_args)
    return res[0], list(res[1:1 + nw]), list(res[1 + nw:])


def _ln_proj_bwd(ds, w, groups, x, dres, g, scale, *, name, tm=256, comm=None):
    l, d = x.shape
    tm = min(tm, l)
    nb = l // tm
    nd = len(ds)
    pieces = _grouped_pieces(groups)
    c_args, c_in, c_out, c_shape, c_scratch = _comm_extra(comm)

    def body(*refs):
        d_refs = refs[:nd]
        w_ref, x_ref, dres_ref, g_ref, sc_ref, dx_ref, sums_ref = refs[nd:]
        dhv = None
        for d_ref, plist in zip(d_refs, pieces):
            for off, lo, hi in plist:
                term = _dot(d_ref[:, off:off + hi - lo], w_ref[:, lo:hi], NT)
                dhv = term if dhv is None else dhv + term
        xv = x_ref[...]
        rstd = lax.rsqrt(jnp.mean(xv * xv, axis=-1, keepdims=True) + EPS)
        n = xv * rstd
        gv = g_ref[...]
        dr = dhv * (1.0 + sc_ref[...])
        dn = dr * gv
        dx_ref[...] = dres_ref[...] + rstd * (dn - n * jnp.mean(dn * n, axis=-1, keepdims=True))

        @pl.when(pl.program_id(0) == 0)
        def _():
            sums_ref[...] = jnp.zeros_like(sums_ref)

        sums_ref[0:1, :] += jnp.sum(dhv, axis=0, keepdims=True)
        sums_ref[1:2, :] += jnp.sum(dhv * (n * gv), axis=0, keepdims=True)
        sums_ref[2:3, :] += jnp.sum(dr * n, axis=0, keepdims=True)

    vec = pl.BlockSpec((1, d), lambda i: (0, 0))
    row = pl.BlockSpec((tm, d), lambda i: (i, 0))
    res = pl.pallas_call(
        _with_comm(body, comm, nd + 5, 2, nb, -1), grid=(nb,),
        in_specs=[pl.BlockSpec((tm, a.shape[1]), lambda i: (i, 0)) for a in ds]
                 + [pl.BlockSpec(w.shape, lambda i: (0, 0)), row, row, vec, vec] + c_in,
        out_specs=[row, pl.BlockSpec((8, d), lambda i: (0, 0))] + c_out,
        out_shape=[jax.ShapeDtypeStruct((l, d), F32), jax.ShapeDtypeStruct((8, d), F32)] + c_shape,
        scratch_shapes=c_scratch,
        compiler_params=_params("arbitrary"), name=name)(*ds, w, x, dres, g, scale, *c_args)
    return res[0], res[1], list(res[2:])


def _final_loss(x, g, target, *, tm=512):
    l, d = x.shape

    def body(x_ref, g_ref, t_ref, dx_ref, sums_ref):
        xv = x_ref[...]
        rstd = lax.rsqrt(jnp.mean(xv * xv, axis=-1, keepdims=True) + EPS)
        n = xv * rstd
        gv = g_ref[...]
        err = n * gv - t_ref[...]
        dy = err * (1.0 / d)
        dn = dy * gv
        dx_ref[...] = rstd * (dn - n * jnp.mean(dn * n, axis=-1, keepdims=True))

        @pl.when(pl.program_id(0) == 0)
        def _():
            sums_ref[...] = jnp.zeros_like(sums_ref)

        sums_ref[0:1, :] += jnp.sum(dy * n, axis=0, keepdims=True)
        sums_ref[1:2, :] += jnp.sum(err * err, axis=0, keepdims=True) * (0.5 / d)

    vec = pl.BlockSpec((1, d), lambda i: (0, 0))
    row = pl.BlockSpec((tm, d), lambda i: (i, 0))
    dx, sums = pl.pallas_call(
        body, grid=(l // tm,),
        in_specs=[row, vec, row],
        out_specs=[row, pl.BlockSpec((8, d), lambda i: (0, 0))],
        out_shape=[jax.ShapeDtypeStruct((l, d), F32), jax.ShapeDtypeStruct((8, d), F32)],
        compiler_params=_params("arbitrary"), name="final_loss")(x, g, target)
    return dx, sums


def _attn_geometry(i, t):
    nk = t + WINDOW
    qi = lax.broadcasted_iota(jnp.int32, (t, nk), 0)
    kj = lax.broadcasted_iota(jnp.int32, (t, nk), 1)
    dist = jnp.abs(qi + WINDOW - kj).astype(F32)
    qc = jnp.right_shift(qi, 6)
    kc = jnp.right_shift(kj, 6)
    valid = (kc >= qc) & (kc <= qc + WINDOW // CHUNK) & ((i > 0) | (kj >= WINDOW))
    return dist, valid


def _attn_head(q, k_all, v_all, sink, slope, dist, valid):
    s = _dot(q, k_all, NT) * (1.0 / math.sqrt(HEAD_DIM)) - slope * dist
    s = jnp.where(valid, s, NEG_INF)
    m = jnp.maximum(jnp.max(s, axis=-1, keepdims=True), sink)
    e = jnp.exp(s - m)
    es = jnp.exp(sink - m)
    inv = 1.0 / (jnp.sum(e, axis=-1, keepdims=True) + es)
    p = e * inv
    o = _dot(p, v_all, NN)
    return p, o, es * inv


def _attn_specs(t):
    cur = pl.BlockSpec((t, ATT_W * 2 + KV_W * 2), lambda i: (i, 0))
    halo_blocks = t // WINDOW
    prev = pl.BlockSpec((WINDOW, 2 * KV_W), lambda i: (jnp.maximum(i * halo_blocks - 1, 0), (2 * ATT_W) // (2 * KV_W)))
    return cur, prev


def _attn_fwd(pa, sinks, *, name, t=ATT_BLOCK, comm=None):
    l = pa.shape[0]
    t = min(t, l)
    nb = l // t
    c_args, c_in, c_out, c_shape, c_scratch = _comm_extra(comm)

    def body(sink_ref, cur_ref, prev_ref, ya_ref):
        i = pl.program_id(0)
        dist, valid = _attn_geometry(i, t)
        for h in range(N_HEADS):
            kh = h // Q_PER_KV
            q = cur_ref[:, h * HEAD_DIM:(h + 1) * HEAD_DIM]
            z = cur_ref[:, ATT_W + h * HEAD_DIM:ATT_W + (h + 1) * HEAD_DIM].astype(F32)
            k_all = jnp.concatenate([prev_ref[:, kh * HEAD_DIM:(kh + 1) * HEAD_DIM],
                                     cur_ref[:, 2 * ATT_W + kh * HEAD_DIM:2 * ATT_W + (kh + 1) * HEAD_DIM]], axis=0)
            v_all = jnp.concatenate([prev_ref[:, KV_W + kh * HEAD_DIM:KV_W + (kh + 1) * HEAD_DIM],
                                     cur_ref[:, 2 * ATT_W + KV_W + kh * HEAD_DIM:2 * ATT_W + KV_W + (kh + 1) * HEAD_DIM]], axis=0)
            _, o, _ = _attn_head(q, k_all, v_all, sink_ref[h], 2.0 ** (-(h + 1)), dist, valid)
            sz, _ = _silu_and_grad(z)
            ya_ref[:, h * HEAD_DIM:(h + 1) * HEAD_DIM] = (o * sz).astype(BF16)

    cur, prev = _attn_specs(t)
    res = pl.pallas_call(
        _with_comm(body, comm, 3, 1, nb, nb - 1), grid=(nb,),
        in_specs=[pl.BlockSpec(memory_space=pltpu.SMEM), cur, prev] + c_in,
        out_specs=[pl.BlockSpec((t, ATT_W), lambda i: (i, 0))] + c_out,
        out_shape=[jax.ShapeDtypeStruct((l, ATT_W), BF16)] + c_shape,
        scratch_shapes=c_scratch,
        compiler_params=_params("arbitrary"), name=name)(sinks, pa, pa, *c_args)
    return res[0], res[1:]


def _attn_bwd(pa, sinks, dya, *, name, t=SEQ_BLOCK):
    l = pa.shape[0]
    t = min(t, l)
    nb = l // t
    scale = 1.0 / math.sqrt(HEAD_DIM)

    def body(sink_ref, cur_ref, prev_ref, dya_ref, dpa_ref, dsink_ref, carry_ref):
        n = pl.program_id(0)
        i = nb - 1 - n
        dist, valid = _attn_geometry(i, t)

        @pl.when(n == 0)
        def _():
            carry_ref[...] = jnp.zeros_like(carry_ref)
            dsink_ref[...] = jnp.zeros_like(dsink_ref)

        dk_acc = [jnp.zeros((HEAD_DIM, t + WINDOW), F32) for _ in range(N_KV_HEADS)]
        dv_acc = [jnp.zeros((HEAD_DIM, t + WINDOW), F32) for _ in range(N_KV_HEADS)]
        for h in range(N_HEADS):
            kh = h // Q_PER_KV
            q = cur_ref[:, h * HEAD_DIM:(h + 1) * HEAD_DIM]
            z = cur_ref[:, ATT_W + h * HEAD_DIM:ATT_W + (h + 1) * HEAD_DIM].astype(F32)
            k_all = jnp.concatenate([prev_ref[:, kh * HEAD_DIM:(kh + 1) * HEAD_DIM],
                                     cur_ref[:, 2 * ATT_W + kh * HEAD_DIM:2 * ATT_W + (kh + 1) * HEAD_DIM]], axis=0)
            v_all = jnp.concatenate([prev_ref[:, KV_W + kh * HEAD_DIM:KV_W + (kh + 1) * HEAD_DIM],
                                     cur_ref[:, 2 * ATT_W + KV_W + kh * HEAD_DIM:2 * ATT_W + KV_W + (kh + 1) * HEAD_DIM]], axis=0)
            p, o, p_sink = _attn_head(q, k_all, v_all, sink_ref[h], 2.0 ** (-(h + 1)), dist, valid)
            dy = dya_ref[:, h * HEAD_DIM:(h + 1) * HEAD_DIM]
            sz, dsz = _silu_and_grad(z)
            do = dy * sz
            dpa_ref[:, ATT_W + h * HEAD_DIM:ATT_W + (h + 1) * HEAD_DIM] = (dy * o * dsz).astype(BF16)
            delta = jnp.sum(do * o, axis=-1, keepdims=True)
            dp = _dot(do, v_all, NT)
            ds = p * (dp - delta)
            dpa_ref[:, h * HEAD_DIM:(h + 1) * HEAD_DIM] = (_dot(ds, k_all, NN) * scale).astype(BF16)
            dk_acc[kh] = dk_acc[kh] + _dot(q, ds, TN) * scale
            dv_acc[kh] = dv_acc[kh] + _dot(do, p, TN)
            dsink_ref[h:h + 1, :] += jnp.broadcast_to(-jnp.sum(p_sink * delta, axis=0, keepdims=True), (1, 128))

        acc = jnp.concatenate(dk_acc + dv_acc, axis=0).T
        own = acc[WINDOW:, :]
        tail = own[t - WINDOW:, :] + carry_ref[...]
        if t > WINDOW:
            dpa_ref[0:t - WINDOW, 2 * ATT_W:] = own[:t - WINDOW, :].astype(BF16)
        dpa_ref[t - WINDOW:t, 2 * ATT_W:] = tail.astype(BF16)
        carry_ref[...] = acc[:WINDOW, :]

    halo_blocks = t // WINDOW
    wpa = 2 * ATT_W + 2 * KV_W
    cur = pl.BlockSpec((t, wpa), lambda n: (nb - 1 - n, 0))
    prev = pl.BlockSpec((WINDOW, 2 * KV_W),
                        lambda n: (jnp.maximum((nb - 1 - n) * halo_blocks - 1, 0), (2 * ATT_W) // (2 * KV_W)))
    return pl.pallas_call(
        body, grid=(nb,),
        in_specs=[pl.BlockSpec(memory_space=pltpu.SMEM), cur, prev, pl.BlockSpec((t, ATT_W), lambda n: (nb - 1 - n, 0))],
        out_specs=[pl.BlockSpec((t, wpa), lambda n: (nb - 1 - n, 0)), pl.BlockSpec((8, 128), lambda n: (0, 0))],
        out_shape=[jax.ShapeDtypeStruct((l, wpa), BF16), jax.ShapeDtypeStruct((8, 128), F32)],
        scratch_shapes=[pltpu.VMEM((WINDOW, 2 * KV_W), F32)],
        compiler_params=_params("arbitrary"), name=name)(sinks, pa, pa, dya)


def _scan(xr, xi, lr, li, t, reverse):
    row = lax.broadcasted_iota(jnp.int32, (t, 1), 0)
    d = 1
    pr, pi = lr, li
    while d < t:
        if reverse:
            sr = jnp.where(row < t - d, pltpu.roll(xr, t - d, 0), 0.0)
            si = jnp.where(row < t - d, pltpu.roll(xi, t - d, 0), 0.0)
        else:
            sr = jnp.where(row >= d, pltpu.roll(xr, d, 0), 0.0)
            si = jnp.where(row >= d, pltpu.roll(xi, d, 0), 0.0)
        xr, xi = xr + pr * sr - pi * si, xi + pr * si + pi * sr
        pr, pi = pr * pr - pi * pi, 2.0 * pr * pi
        d *= 2
    return xr, xi


SCAN_SUB = 8


def _split_hi_lo(a):
    hi = a.astype(BF16)
    lo = (a - hi.astype(F32)).astype(BF16)
    return jnp.concatenate([hi, lo], axis=0)


def _scan_mxu(xr, xi, tab, lam3, lam8, tri, expand, cr, ci, t, reverse):
    ns = t // SCAN_SUB
    n = xr.shape[1]
    v3 = lambda a: a.reshape(ns, SCAN_SUB, n)
    x3r, x3i = v3(xr), v3(xi)
    br = (x3r * tab[0] - x3i * tab[1]).reshape(t, n)
    bi = (x3r * tab[1] + x3i * tab[0]).reshape(t, n)
    pm = jnp.dot(tri, jnp.concatenate([br, bi], axis=1).astype(BF16), preferred_element_type=F32)
    p3r, p3i = v3(pm[:t, :n]), v3(pm[:t, n:])
    slr = p3r * tab[2] - p3i * tab[3]
    sli = p3r * tab[3] + p3i * tab[2]
    totr, toti = pm[t:, :n], pm[t:, n:]
    l3r, l3i = lam3
    l8r, l8i = lam8
    row = lax.broadcasted_iota(jnp.int32, (ns, 1), 0)
    edge = row == (ns - 1 if reverse else 0)
    er = totr * l3r - toti * l3i + jnp.where(edge, l8r * cr - l8i * ci, 0.0)
    ei = totr * l3i + toti * l3r + jnp.where(edge, l8r * ci + l8i * cr, 0.0)
    er, ei = _scan(er, ei, l8r, l8i, ns, reverse)
    shift = ns - 1 if reverse else 1
    nbr = jnp.where(edge, cr, pltpu.roll(er, shift, 0))
    nbi = jnp.where(edge, ci, pltpu.roll(ei, shift, 0))
    ex = jnp.dot(expand, _split_hi_lo(jnp.concatenate([nbr, nbi], axis=1)), preferred_element_type=F32)
    e3r, e3i = v3(ex[:, :n]), v3(ex[:, n:])
    sr = (slr + e3r * tab[4] - e3i * tab[5]).reshape(t, n)
    si = (sli + e3r * tab[5] + e3i * tab[4]).reshape(t, n)
    out = 0 if reverse else ns - 1
    return sr, si, er[out:out + 1, :], ei[out:out + 1, :]


def _scan_consts(t):
    import numpy as np
    ns = t // SCAN_SUB
    r = np.arange(t)
    same = (r[:, None] // SCAN_SUB) == (r[None, :] // SCAN_SUB)
    sums = (np.arange(ns)[:, None] == (r[None, :] // SCAN_SUB))
    tri = []
    for keep in (r[None, :] <= r[:, None], r[None, :] >= r[:, None]):
        tri.append(np.concatenate([same & keep, sums], axis=0).astype(np.float32))
    ex = ((r[:, None] // SCAN_SUB) == np.arange(ns)[None, :]).astype(np.float32)
    return jnp.asarray(np.stack(tri), BF16), jnp.asarray(np.concatenate([ex, ex], axis=1), BF16)


def _scan_tables(lr, li):
    import numpy as np
    den = lr * lr + li * li
    ir, ii = lr / den, -li / den
    mul = lambda a, b: (a[0] * b[0] - a[1] * b[1], a[0] * b[1] + a[1] * b[0])
    pw = {0: (jnp.ones_like(lr), jnp.zeros_like(lr))}
    for e in range(1, 9):
        pw[e] = mul(pw[e - 1], (lr, li))
    for e in range(-1, -5, -1):
        pw[e] = mul(pw[e + 1], (ir, ii))
    powers = jnp.stack([jnp.stack(pw[e]) for e in range(-4, 9)] + [jnp.zeros((2, lr.shape[0]), F32)])
    j = np.arange(SCAN_SUB)
    exps = [4 - j, j - 4, j + 1, j - 3, 3 - j, 8 - j]
    e_idx = np.stack([exps[t] + 4 for t in range(6) for _ in range(2)])
    c_idx = np.tile(np.array([0, 1])[:, None], (6, SCAN_SUB))
    sign = np.where((c_idx == 1) & (np.arange(12)[:, None] >= 6), -1.0, 1.0).astype(np.float32)
    tabs = powers[e_idx, c_idx] * sign[:, :, None]
    lam = powers[np.array([5, 5, 7, 7, 12, 12, 13, 13]), np.array([0, 1, 0, 1, 0, 1, 0, 0])]
    return lam, tabs


SSM_HALVES = 2
SSM_HW = SSM_W // SSM_HALVES
SSM_HN = SSM_N // SSM_HALVES


def _bd_nn(x, w):
    a = w.shape[1]
    return jnp.concatenate([_dot(x[:, h * a:(h + 1) * a], w[h]) for h in range(SSM_HALVES)], axis=1)


def _bd_nt(x, w):
    b = w.shape[2]
    return jnp.concatenate([_dot(x[:, h * b:(h + 1) * b], w[h], NT) for h in range(SSM_HALVES)], axis=1)


def _bd_tn(x, y):
    a, b = x.shape[1] // SSM_HALVES, y.shape[1] // SSM_HALVES
    return jnp.stack([_dot(x[:, h * a:(h + 1) * a], y[:, h * b:(h + 1) * b], TN) for h in range(SSM_HALVES)])


def _ssm_states(u, s0r, s0i, lam_ref, tab_ref, tri_ref, ex_ref, bre, bim, t):
    tab = tuple(tab_ref[k] for k in range(6))
    return _scan_mxu(_bd_nn(u, bre), _bd_nn(u, bim), tab, (lam_ref[2:3, :], lam_ref[3:4, :]),
                     (lam_ref[4:5, :], lam_ref[5:6, :]), tri_ref[0], ex_ref[...], s0r, s0i, t, False)


def _ssm_head(u, z, xr, xi, cre, cim, dskip, wglu, bglu):
    y = _bd_nn(xr, cre) - _bd_nn(xi, cim) + dskip * u
    y2, dgelu = _gelu_and_grad(y)
    gate = _sigmoid(_dot(y2, wglu) + bglu)
    y3 = y2 * gate
    return y2, dgelu, gate, y3


def _with_comm(body, comm, n_in, n_out, grid, mid_step):
    if comm is None:
        return body
    nc = len(comm["args"])
    n_sem = len(comm["scratch"])
    grid = (grid,) if isinstance(grid, int) else tuple(grid)
    total = math.prod(grid)

    def hosted(*refs):
        ins, cin = refs[:n_in], refs[n_in:n_in + nc]
        outs, cout = refs[n_in + nc:n_in + nc + n_out], refs[n_in + nc + n_out:n_in + 2 * nc + n_out]
        rest = refs[n_in + 2 * nc + n_out:]
        scratch, csem = rest[:len(rest) - n_sem], rest[len(rest) - n_sem:]
        start, forward, finish = comm["phases"](cin, cout, *csem)
        step = pl.program_id(0)
        for axis in range(1, len(grid)):
            step = step * grid[axis] + pl.program_id(axis)
        pl.when(step == 0)(start)
        pl.when(step == (mid_step if mid_step >= 0 else total + mid_step))(forward)
        body(*ins, *outs, *scratch)
        pl.when(step == total - 1)(finish)

    return hosted


def _comm_extra(comm):
    if comm is None:
        return [], [], [], [], []
    anyspec = pl.BlockSpec(memory_space=pl.ANY)
    nc = len(comm["args"])
    return comm["args"], [anyspec] * nc, [anyspec] * nc, comm["out_shape"], comm["scratch"]


def _ssm_fwd(ps, scan_ops, bblk, cblk, dskip, wglu, bglu, *, name, t=SEQ_BLOCK, comm=None):
    l = ps.shape[0]
    assert l % t == 0
    nb = l // t
    ns = t // SCAN_SUB
    c_args, c_in, c_out, c_shape, c_scratch = _comm_extra(comm)

    def body(ps_ref, lam_ref, tab_ref, tri_ref, ex_ref, b_ref, c_ref, d_ref, w_ref, bg_ref, ys_ref, chk_ref, xs_ref,
             st_ref):
        @pl.when(pl.program_id(0) == 0)
        def _():
            st_ref[...] = jnp.zeros_like(st_ref)

        chk_ref[...] = jnp.broadcast_to(st_ref[...], chk_ref.shape)
        u = ps_ref[:, :SSM_W].astype(F32)
        z = ps_ref[:, SSM_W:].astype(F32)
        xr, xi, er, ei = _ssm_states(u, st_ref[:, :SSM_N], st_ref[:, SSM_N:], lam_ref, tab_ref, tri_ref, ex_ref,
                                     b_ref[0], b_ref[1], t)
        st_ref[:, :SSM_N] = er
        st_ref[:, SSM_N:] = ei
        xr, xi = xr.astype(BF16), xi.astype(BF16)
        xs_ref[:, :SSM_N] = xr
        xs_ref[:, SSM_N:] = xi
        _, _, _, y3 = _ssm_head(u, z, xr, xi, c_ref[0], c_ref[1], d_ref[...], w_ref[...], bg_ref[...])
        sz, _ = _silu_and_grad(z)
        ys_ref[...] = (y3 * sz).astype(BF16)

    full = lambda shape: pl.BlockSpec(shape, lambda i: (0,) * len(shape))
    return pl.pallas_call(
        _with_comm(body, comm, 10, 3, nb, nb - 1), grid=(nb,),
        in_specs=[pl.BlockSpec((t, 2 * SSM_W), lambda i: (i, 0)), full((8, SSM_N)), full((12, SCAN_SUB, SSM_N)),
                  full((2, t + ns, t)), full((t, 2 * ns)), full((2, SSM_HALVES, SSM_HW, SSM_HN)),
                  full((2, SSM_HALVES, SSM_HN, SSM_HW)), full((1, SSM_W)), full((SSM_W, SSM_W)), full((1, SSM_W))] + c_in,
        out_specs=[pl.BlockSpec((t, SSM_W), lambda i: (i, 0)), pl.BlockSpec((8, 2 * SSM_N), lambda i: (i, 0)),
                   pl.BlockSpec((t, 2 * SSM_N), lambda i: (i, 0))] + c_out,
        out_shape=[jax.ShapeDtypeStruct((l, SSM_W), BF16), jax.ShapeDtypeStruct((nb * 8, 2 * SSM_N), F32),
                   jax.ShapeDtypeStruct((l, 2 * SSM_N), BF16)] + c_shape,
        scratch_shapes=[pltpu.VMEM((1, 2 * SSM_N), F32)] + c_scratch,
        compiler_params=_params("arbitrary"), name=name)(ps, *scan_ops, bblk, cblk, dskip, wglu, bglu, *c_args)


def _ssm_bwd(ps, dys, chk, states, scan_ops, bblk, cblk, dskip, wglu, bglu, *, name, t=SEQ_BLOCK, comm=None):
    l = ps.shape[0]
    assert l % t == 0
    nb = l // t
    ns = t // SCAN_SUB
    c_args, c_in, c_out, c_shape, c_scratch = _comm_extra(comm)

    def body(ps_ref, dys_ref, chk_ref, xs_ref, lam_ref, tab_ref, tri_ref, ex_ref, b_ref, c_ref, d_ref, w_ref, bg_ref,
             dps_ref, db_ref, dc_ref, dw_acc, sums_acc, gc_ref, db_acc, dc_acc):
        n = pl.program_id(0)

        @pl.when(n == 0)
        def _():
            gc_ref[...] = jnp.zeros_like(gc_ref)
            db_acc[...] = jnp.zeros_like(db_acc)
            dc_acc[...] = jnp.zeros_like(dc_acc)
            dw_acc[...] = jnp.zeros_like(dw_acc)
            sums_acc[...] = jnp.zeros_like(sums_acc)

        row = lax.broadcasted_iota(jnp.int32, (t, 1), 0)
        u = ps_ref[:, :SSM_W].astype(F32)
        z = ps_ref[:, SSM_W:].astype(F32)
        s0r, s0i = chk_ref[0:1, :SSM_N], chk_ref[0:1, SSM_N:]
        xr, xi = xs_ref[:, :SSM_N], xs_ref[:, SSM_N:]
        dskip = d_ref[...]
        y2, dgelu, gate, y3 = _ssm_head(u, z, xr, xi, c_ref[0], c_ref[1], dskip, w_ref[...], bg_ref[...])
        sz, dsz = _silu_and_grad(z)
        dys_v = dys_ref[...]
        dps_ref[:, SSM_W:] = (dys_v * y3 * dsz).astype(BF16)
        dy3 = dys_v * sz
        da = dy3 * y2 * gate * (1.0 - gate)
        dy2 = dy3 * gate + _dot(da, w_ref[...], NT)
        dw_acc[...] += _dot(y2, da, TN)
        dy = dy2 * dgelu
        sums_acc[2:3, :SSM_W] += jnp.sum(dy * u, axis=0, keepdims=True)
        sums_acc[3:4, :SSM_W] += jnp.sum(da, axis=0, keepdims=True)
        dc_acc[0] += _bd_tn(dy, xr)
        dc_acc[1] += -_bd_tn(dy, xi)
        rev_tab = tuple(tab_ref[k] for k in range(6, 12))
        gr, gi, gcr, gci = _scan_mxu(
            _bd_nt(dy, c_ref[0]), -_bd_nt(dy, c_ref[1]), rev_tab, (lam_ref[2:3, :], -lam_ref[3:4, :]),
            (lam_ref[4:5, :], -lam_ref[5:6, :]), tri_ref[1], ex_ref[...], gc_ref[:, :SSM_N], gc_ref[:, SSM_N:], t, True)
        gc_ref[:, :SSM_N] = gcr
        gc_ref[:, SSM_N:] = gci
        db_acc[0] += _bd_tn(u, gr)
        db_acc[1] += _bd_tn(u, gi)
        du = dskip * dy + _bd_nt(gr, b_ref[0]) + _bd_nt(gi, b_ref[1])
        dps_ref[:, :SSM_W] = du.astype(BF16)
        spr = jnp.where(row == 0, s0r, pltpu.roll(xr.astype(F32), 1, 0))
        spi = jnp.where(row == 0, s0i, pltpu.roll(xi.astype(F32), 1, 0))
        sums_acc[0:1, :] += jnp.sum(gr * spr + gi * spi, axis=0, keepdims=True)
        sums_acc[1:2, :] += jnp.sum(gi * spr - gr * spi, axis=0, keepdims=True)

        @pl.when(n == nb - 1)
        def _():
            per_half = SSM_GROUPS // SSM_HALVES
            for k in range(2):
                for g in range(SSM_GROUPS):
                    h, gl = divmod(g, per_half)
                    c0, p0 = gl * SSM_GROUP, gl * SSM_STATE
                    db_ref[k, g * SSM_GROUP:(g + 1) * SSM_GROUP, :] = db_acc[k, h, c0:c0 + SSM_GROUP, p0:p0 + SSM_STATE]
                    dc_ref[k, g * SSM_GROUP:(g + 1) * SSM_GROUP, :] = dc_acc[k, h, c0:c0 + SSM_GROUP, p0:p0 + SSM_STATE]

    full = lambda shape: pl.BlockSpec(shape, lambda n: (0,) * len(shape))
    return pl.pallas_call(
        _with_comm(body, comm, 13, 5, nb, 0), grid=(nb,),
        in_specs=[pl.BlockSpec((t, 2 * SSM_W), lambda n: (nb - 1 - n, 0)),
                  pl.BlockSpec((t, SSM_W), lambda n: (nb - 1 - n, 0)),
                  pl.BlockSpec((8, 2 * SSM_N), lambda n: (nb - 1 - n, 0)),
                  pl.BlockSpec((t, 2 * SSM_N), lambda n: (nb - 1 - n, 0)),
                  full((8, SSM_N)), full((12, SCAN_SUB, SSM_N)), full((2, t + ns, t)), full((t, 2 * ns)),
                  full((2, SSM_HALVES, SSM_HW, SSM_HN)), full((2, SSM_HALVES, SSM_HN, SSM_HW)), full((1, SSM_W)),
                  full((SSM_W, SSM_W)), full((1, SSM_W))] + c_in,
        out_specs=[pl.BlockSpec((t, 2 * SSM_W), lambda n: (nb - 1 - n, 0)), full((2, SSM_W, SSM_STATE)),
                   full((2, SSM_W, SSM_STATE)), full((SSM_W, SSM_W)), full((8, SSM_N))] + c_out,
        out_shape=[jax.ShapeDtypeStruct((l, 2 * SSM_W), BF16),
                   jax.ShapeDtypeStruct((2, SSM_W, SSM_STATE), F32),
                   jax.ShapeDtypeStruct((2, SSM_W, SSM_STATE), F32),
                   jax.ShapeDtypeStruct((SSM_W, SSM_W), F32),
                   jax.ShapeDtypeStruct((8, SSM_N), F32)] + c_shape,
        scratch_shapes=[pltpu.VMEM((1, 2 * SSM_N), F32), pltpu.VMEM((2, SSM_HALVES, SSM_HW, SSM_HN), F32),
                        pltpu.VMEM((2, SSM_HALVES, SSM_HW, SSM_HN), F32)] + c_scratch,
        compiler_params=_params("arbitrary"), name=name)(ps, dys, chk, states, *scan_ops, bblk, cblk, dskip, wglu, bglu,
                                                         *c_args)


def _pool_count(i, t):
    pos = lax.broadcasted_iota(jnp.int32, (t, POOL_W), 0) + i * t + 1
    col = lax.broadcasted_iota(jnp.int32, (t, POOL_W), 1)
    win = jnp.where(col < POOL_GW, 2, jnp.where(col < 2 * POOL_GW, 4, jnp.where(col < 3 * POOL_GW, 8, 16)))
    return 1.0 / jnp.minimum(pos, win).astype(F32), col


def _window_sums(ext, n_rows, forward):
    col = lax.broadcasted_iota(jnp.int32, ext.shape, 1)
    sh = (lambda a, d: pltpu.roll(a, d, 0)) if forward else (lambda a, d: pltpu.roll(a, n_rows - d, 0))
    a2 = ext + sh(ext, 1)
    a4 = a2 + sh(a2, 2)
    a8 = a4 + sh(a4, 4)
    a16 = a8 + sh(a8, 8)
    return jnp.where(col < POOL_GW, a2, jnp.where(col < 2 * POOL_GW, a4, jnp.where(col < 3 * POOL_GW, a8, a16)))


def _pool_mix(pooled, wp_ref):
    return jnp.concatenate([_dot(pooled[:, g * POOL_GW:(g + 1) * POOL_GW], wp_ref[g]) for g in range(4)], axis=1)


def _pool_pooled(i, cur_u, prev_u, t):
    prev = jnp.where(i > 0, prev_u, 0.0)
    ext = jnp.concatenate([prev, cur_u], axis=0)
    inv_cnt, _ = _pool_count(i, t)
    return _window_sums(ext, t + POOL_HALO, True)[POOL_HALO:, :] * inv_cnt - cur_u


def _pool_fwd(pp, wpool, pscale, *, name, t=SEQ_BLOCK):
    l = pp.shape[0]
    t = min(t, l)

    def body(cur_ref, prev_ref, wp_ref, sc_ref, yp_ref):
        i = pl.program_id(0)
        pooled = _pool_pooled(i, cur_ref[:, :POOL_W].astype(F32), prev_ref[...].astype(F32), t)
        lin = _pool_mix(pooled, wp_ref)
        sz, _ = _silu_and_grad(cur_ref[:, POOL_W:].astype(F32))
        yp_ref[...] = (lin * sc_ref[...] * sz).astype(BF16)

    hb = t // POOL_HALO
    return pl.pallas_call(
        body, grid=(l // t,),
        in_specs=[pl.BlockSpec((t, 2 * POOL_W), lambda i: (i, 0)),
                  pl.BlockSpec((POOL_HALO, POOL_W), lambda i: (jnp.maximum(i * hb - 1, 0), 0)),
                  pl.BlockSpec((4, POOL_GW, POOL_GW), lambda i: (0, 0, 0)),
                  pl.BlockSpec((1, POOL_W), lambda i: (0, 0))],
        out_specs=pl.BlockSpec((t, POOL_W), lambda i: (i, 0)),
        out_shape=jax.ShapeDtypeStruct((l, POOL_W), BF16),
        compiler_params=_params("parallel"), name=name)(pp, pp, wpool, pscale)


def _pool_bwd(pp, dyp, wpool, pscale, *, name, t=SEQ_BLOCK):
    l = pp.shape[0]
    t = min(t, l)
    nb = l // t

    def body(cur_ref, prev_ref, dyp_ref, wp_ref, sc_ref, dpp_ref, dwp_ref, sums_ref, carry_ref):
        n = pl.program_id(0)
        i = nb - 1 - n

        @pl.when(n == 0)
        def _():
            carry_ref[...] = jnp.zeros_like(carry_ref)
            dwp_ref[...] = jnp.zeros_like(dwp_ref)
            sums_ref[...] = jnp.zeros_like(sums_ref)

        cur_u = cur_ref[:, :POOL_W].astype(F32)
        pooled = _pool_pooled(i, cur_u, prev_ref[...].astype(F32), t)
        lin = _pool_mix(pooled, wp_ref)
        sz, dsz = _silu_and_grad(cur_ref[:, POOL_W:].astype(F32))
        dyp_v = dyp_ref[...]
        scale = sc_ref[...]
        dpp_ref[:, POOL_W:] = (dyp_v * lin * scale * dsz).astype(BF16)
        dpre = dyp_v * sz
        sums_ref[0:1, :] += jnp.sum(dpre * lin, axis=0, keepdims=True)
        dlin = dpre * scale
        dpooled = []
        for g in range(4):
            dl = dlin[:, g * POOL_GW:(g + 1) * POOL_GW]
            dwp_ref[g] += _dot(pooled[:, g * POOL_GW:(g + 1) * POOL_GW], dl, TN)
            dpooled.append(_dot(dl, wp_ref[g], NT))
        dpooled = jnp.concatenate(dpooled, axis=1)
        inv_cnt, _ = _pool_count(i, t)
        dq = dpooled * inv_cnt
        ext = jnp.concatenate([dq, carry_ref[...]], axis=0)
        du = _window_sums(ext, t + POOL_HALO, False)[:t, :] - dpooled
        dpp_ref[:, :POOL_W] = du.astype(BF16)
        carry_ref[...] = dq[:POOL_HALO, :]

    hb = t // POOL_HALO
    return pl.pallas_call(
        body, grid=(nb,),
        in_specs=[pl.BlockSpec((t, 2 * POOL_W), lambda n: (nb - 1 - n, 0)),
                  pl.BlockSpec((POOL_HALO, POOL_W), lambda n: (jnp.maximum((nb - 1 - n) * hb - 1, 0), 0)),
                  pl.BlockSpec((t, POOL_W), lambda n: (nb - 1 - n, 0)),
                  pl.BlockSpec((4, POOL_GW, POOL_GW), lambda n: (0, 0, 0)),
                  pl.BlockSpec((1, POOL_W), lambda n: (0, 0))],
        out_specs=[pl.BlockSpec((t, 2 * POOL_W), lambda n: (nb - 1 - n, 0)),
                   pl.BlockSpec((4, POOL_GW, POOL_GW), lambda n: (0, 0, 0)),
                   pl.BlockSpec((8, POOL_W), lambda n: (0, 0))],
        out_shape=[jax.ShapeDtypeStruct((l, 2 * POOL_W), BF16), jax.ShapeDtypeStruct((4, POOL_GW, POOL_GW), F32),
                   jax.ShapeDtypeStruct((8, POOL_W), F32)],
        scratch_shapes=[pltpu.VMEM((POOL_HALO, POOL_W), F32)],
        compiler_params=_params("arbitrary"), name=name)(pp, pp, dyp, wpool, pscale)


def _merge_fwd(ya, ys, yp, wa, ws, wp, pg, wout, x, gate, *, name, tm=512):
    l, d = x.shape
    tm = min(tm, l)

    def body(ya_ref, ys_ref, yp_ref, wa_ref, ws_ref, wp_ref, pg_ref, wo_ref, x_ref, g_ref,
             xn_ref, mg_ref, ba_ref, bs_ref, bp_ref, out_ref):
        acc = None
        for k, (y_ref, w_ref, b_ref) in enumerate(((ya_ref, wa_ref, ba_ref), (ys_ref, ws_ref, bs_ref),
                                                   (yp_ref, wp_ref, bp_ref))):
            br = _dot(y_ref[...], w_ref[...])
            b_ref[...] = br.astype(BF16)
            term = _sigmoid(pg_ref[:, k * d:(k + 1) * d].astype(F32)) * br
            acc = term if acc is None else acc + term
        merged = acc.astype(BF16)
        mg_ref[...] = merged
        out = _dot(merged, wo_ref[...])
        out_ref[...] = out.astype(BF16)
        xn_ref[...] = x_ref[...] + g_ref[...] * out

    rowy = pl.BlockSpec((tm, ATT_W), lambda i: (i, 0))
    wsp = pl.BlockSpec((ATT_W, d), lambda i: (0, 0))
    rowd = pl.BlockSpec((tm, d), lambda i: (i, 0))
    return pl.pallas_call(
        body, grid=(l // tm,),
        in_specs=[rowy, rowy, rowy, wsp, wsp, wsp, pl.BlockSpec((tm, 3 * d), lambda i: (i, 0)),
                  pl.BlockSpec((d, d), lambda i: (0, 0)), rowd, pl.BlockSpec((1, d), lambda i: (0, 0))],
        out_specs=[rowd] * 6,
        out_shape=[jax.ShapeDtypeStruct((l, d), F32)] + [jax.ShapeDtypeStruct((l, d), BF16)] * 5,
        compiler_params=_params("parallel"), name=name)(ya, ys, yp, wa, ws, wp, pg, wout, x, gate)


def _merge_bwd(dx, out, gate, wout, pg, brs, wbrs, ys, merged, *, name, tm=256, comm=None):
    l, d = dx.shape
    tm = min(tm, l)
    nb = l // tm
    w = ys[0].shape[1]

    def body(dx_ref, out_ref, g_ref, w_ref, pg_ref, ba_ref, bs_ref, bp_ref, wa_ref, ws_ref, wp_ref,
             ya_ref, ys_ref, yp_ref, mg_ref,
             dya_ref, dys_ref, dyp_ref, dpg_ref, sums_ref, dwa_ref, dws_ref, dwp_ref, dwo_ref, acc_br, acc_out):
        i = pl.program_id(0)

        @pl.when(i == 0)
        def _():
            sums_ref[...] = jnp.zeros_like(sums_ref)
            acc_br[...] = jnp.zeros_like(acc_br)
            acc_out[...] = jnp.zeros_like(acc_out)

        dxv = dx_ref[...]
        sums_ref[0:1, :] += jnp.sum(dxv * out_ref[...].astype(F32), axis=0, keepdims=True)
        dmo = (dxv * g_ref[...]).astype(BF16)
        acc_out[...] += _dot(mg_ref[...], dmo, TN)
        dmerged = _dot(dmo, w_ref[...], NT)
        branches = ((ba_ref, wa_ref, ya_ref, dya_ref), (bs_ref, ws_ref, ys_ref, dys_ref), (bp_ref, wp_ref, yp_ref, dyp_ref))
        for k, (b_ref, wk_ref, y_ref, dy_ref) in enumerate(branches):
            gk = _sigmoid(pg_ref[:, k * d:(k + 1) * d].astype(F32))
            dbr = (dmerged * gk).astype(BF16)
            dpg_ref[:, k * d:(k + 1) * d] = (dmerged * b_ref[...].astype(F32) * gk * (1.0 - gk)).astype(BF16)
            dy_ref[...] = _dot(dbr, wk_ref[...], NT)
            acc_br[k] += _dot(y_ref[...], dbr, TN)

        @pl.when(i == nb - 1)
        def _():
            for k, dw_ref in enumerate((dwa_ref, dws_ref, dwp_ref)):
                dw_ref[...] = acc_br[k].astype(BF16)
            dwo_ref[...] = acc_out[...].astype(BF16)

    row = pl.BlockSpec((tm, d), lambda i: (i, 0))
    half = pl.BlockSpec((tm, w), lambda i: (i, 0))
    wide = pl.BlockSpec((tm, 3 * d), lambda i: (i, 0))
    const = lambda shape: pl.BlockSpec(shape, lambda i: (0,) * len(shape))
    c_args, c_in, c_out, c_shape, c_scratch = _comm_extra(comm)
    res = pl.pallas_call(
        _with_comm(body, comm, 15, 9, nb, 0), grid=(nb,),
        in_specs=[row, row, const((1, d)), const((d, d)), wide, row, row, row, const((w, d)), const((w, d)), const((w, d)),
                  half, half, half, row] + c_in,
        out_specs=[half, half, half, wide, const((8, d)), const((w, d)), const((w, d)), const((w, d)), const((d, d))] + c_out,
        out_shape=[jax.ShapeDtypeStruct((l, w), F32)] * 3 + [jax.ShapeDtypeStruct((l, 3 * d), BF16),
                                                             jax.ShapeDtypeStruct((8, d), F32)]
                  + [jax.ShapeDtypeStruct((w, d), BF16)] * 3 + [jax.ShapeDtypeStruct((d, d), BF16)] + c_shape,
        scratch_shapes=[pltpu.VMEM((3, w, d), F32), pltpu.VMEM((d, d), F32)] + c_scratch,
        compiler_params=_params("arbitrary"), name=name)(dx, out, gate, wout, pg, *brs, *wbrs, *ys, merged, *c_args)
    return list(res[:9]), list(res[9:])


def _adamw(w, gs, m, v, *, name, tr=256):
    r, c = w.shape
    ns = len(gs)
    p, rs, _ = gs[0].shape
    assert rs * ns == r
    tr = min(tr, rs)
    assert rs % tr == 0
    nr = rs // tr
    c1 = 1.0 / (1.0 - ADAM_B1 ** ADAM_STEP)
    c2 = 1.0 / (1.0 - ADAM_B2 ** ADAM_STEP)

    def body(*refs):
        w_ref, g_refs, (m_ref, v_ref, go_ref, d_ref, mo_ref, vo_ref) = refs[0], refs[1:1 + ns], refs[1 + ns:]
        slab = pl.program_id(0)
        gv = None
        for k, g_ref in enumerate(g_refs):
            gk = g_ref[0].astype(F32)
            for j in range(1, p):
                gk = gk + g_ref[j].astype(F32)
            gv = gk if gv is None else jnp.where(slab == k, gk, gv)
        go_ref[...] = gv
        mn = ADAM_B1 * m_ref[...] + (1.0 - ADAM_B1) * gv
        vn = ADAM_B2 * v_ref[...] + (1.0 - ADAM_B2) * (gv * gv)
        mo_ref[...] = mn
        vo_ref[...] = vn
        d_ref[...] = -ADAM_LR * ((mn * c1) / (jnp.sqrt(vn * c2) + ADAM_EPS) + ADAM_WD * w_ref[...])

    row = pl.BlockSpec((tr, c), lambda s, i: (s * nr + i, 0))
    g_specs = [pl.BlockSpec((p, tr, c), lambda s, i, k=k: (0, jnp.where(s == k, i, 0), 0)) for k in range(ns)]
    return pl.pallas_call(
        body, grid=(ns, nr),
        in_specs=[row] + g_specs + [row, row],
        out_specs=[row] * 4,
        out_shape=[jax.ShapeDtypeStruct((r, c), F32)] * 4,
        compiler_params=_params("arbitrary", "arbitrary"), name=name)(w, *gs, m, v)


def _exchange(arrs, *, scatter, name):
    n = len(arrs)
    out_shape = [jax.ShapeDtypeStruct(a.shape if scatter else (N_DEV,) + a.shape, a.dtype) for a in arrs]

    def body(*refs):
        ins, outs = refs[:n], refs[n:2 * n]
        send_sems, recv_sems, loc_sems = refs[2 * n:]
        me = 4 * lax.axis_index("x") + 2 * lax.axis_index("y") + lax.axis_index("c")
        local = []
        for k in range(n):
            src = ins[k].at[me] if scatter else ins[k]
            cp = pltpu.make_async_copy(src, outs[k].at[me], loc_sems.at[k])
            cp.start()
            local.append(cp)
        remote = []
        for r in range(1, N_DEV):
            peer = me ^ r
            for k in range(n):
                src = ins[k].at[peer] if scatter else ins[k]
                cp = pltpu.make_async_remote_copy(
                    src_ref=src, dst_ref=outs[k].at[me], send_sem=send_sems.at[k, r - 1], recv_sem=recv_sems.at[k, r - 1],
                    device_id=(peer // 4, (peer // 2) % 2, peer % 2), device_id_type=pl.DeviceIdType.MESH)
                cp.start()
                remote.append(cp)
        for cp in remote:
            cp.wait()
        for cp in local:
            cp.wait()

    anyspec = pl.BlockSpec(memory_space=pl.ANY)
    return pl.pallas_call(
        body, in_specs=[anyspec] * n, out_specs=[anyspec] * n, out_shape=out_shape,
        scratch_shapes=[pltpu.SemaphoreType.DMA((n, N_DEV - 1)), pltpu.SemaphoreType.DMA((n, N_DEV - 1)),
                        pltpu.SemaphoreType.DMA((n,))],
        name=name)(*arrs)


def _mesh_place():
    x, y, c = lax.axis_index("x"), lax.axis_index("y"), lax.axis_index("c")
    other_chips = [(1 - x, y), (x, 1 - y), (1 - x, 1 - y)]
    return x, y, c, other_chips


def _gather_two_level(arrs, *, name):
    n = len(arrs)
    plan = _gather_plan(arrs)

    def body(*refs):
        start, forward, finish = plan["phases"](refs[:n], refs[n:2 * n], *refs[2 * n:])
        start()
        forward()
        finish()

    anyspec = pl.BlockSpec(memory_space=pl.ANY)
    return pl.pallas_call(
        body, in_specs=[anyspec] * n, out_specs=[anyspec] * n, out_shape=plan["out_shape"],
        scratch_shapes=plan["scratch"], name=name)(*arrs)


def _gather_plan(arrs):
    n = len(arrs)

    def phases(ins, outs, send_sems, recv_sems, loc_sems):
        x, y, c, chips = _mesh_place()
        me = 4 * x + 2 * y + c
        slot = lambda px, py, pc: 4 * px + 2 * py + pc

        def copy(k, j, src, block, to):
            return pltpu.make_async_remote_copy(
                src_ref=src, dst_ref=outs[k].at[block], send_sem=send_sems.at[k, j], recv_sem=recv_sems.at[k, j],
                device_id=to, device_id_type=pl.DeviceIdType.MESH)

        local = [pltpu.make_async_copy(ins[k], outs[k].at[me], loc_sems.at[k]) for k in range(n)]
        first = []
        for k in range(n):
            first.append(copy(k, 0, ins[k], me, (x, y, 1 - c)))
            for j, chip in enumerate(chips):
                first.append(copy(k, 1 + j, ins[k], me, (*chip, c)))
        passed = [copy(k, 4 + j, outs[k].at[slot(*chip, c)], slot(*chip, c), (x, y, 1 - c))
                  for j, chip in enumerate(chips) for k in range(n)]

        def start():
            for cp in local + first:
                cp.start()

        def forward():
            for j, chip in enumerate(chips):
                for k in range(n):
                    copy(k, 1 + j, ins[k], slot(*chip, c), (x, y, c)).wait_recv()
                    passed[j * n + k].start()

        def finish():
            for k in range(n):
                copy(k, 0, ins[k], slot(x, y, 1 - c), (x, y, c)).wait_recv()
                for j, chip in enumerate(chips):
                    copy(k, 4 + j, ins[k], slot(*chip, 1 - c), (x, y, c)).wait_recv()
            for cp in first + passed:
                cp.wait_send()
            for cp in local:
                cp.wait()

        return start, forward, finish

    return dict(
        args=list(arrs), out_shape=[jax.ShapeDtypeStruct((N_DEV,) + a.shape, a.dtype) for a in arrs],
        scratch=[pltpu.SemaphoreType.DMA((n, 7)), pltpu.SemaphoreType.DMA((n, 7)), pltpu.SemaphoreType.DMA((n,))],
        phases=phases)


def _allreduce_small(small, extra, *, name):
    r, lanes = small.shape
    assert r % 16 == 0
    h = r // 2
    e = extra.shape[0]

    def body(s_ref, x_ref, out_ref, xall_ref, sib_ref, parts_ref, send_sems, recv_sems):
        x, y, c, chips = _mesh_place()
        me = 4 * x + 2 * y + c
        my_chip = 2 * x + y
        sibling = (x, y, 1 - c)
        mine = pl.ds(pl.multiple_of(c * h, 8), h)
        theirs = pl.ds(pl.multiple_of((1 - c) * h, 8), h)

        def remote(j, src, dst, to):
            return pltpu.make_async_remote_copy(src_ref=src, dst_ref=dst, send_sem=send_sems.at[j],
                                                recv_sem=recv_sems.at[j], device_id=to, device_id_type=pl.DeviceIdType.MESH)

        to_sibling = remote(0, s_ref.at[theirs], sib_ref, sibling)
        to_sibling.start()
        xall_ref[me] = x_ref[...]
        extras = []
        for rr in range(1, N_DEV):
            peer = me ^ rr
            cp = remote(4 + rr, x_ref, xall_ref.at[me], (peer // 4, (peer // 2) % 2, peer % 2))
            cp.start()
            extras.append(cp)
        to_sibling.wait_recv()
        parts_ref[my_chip] = s_ref[mine] + sib_ref[...]
        to_chips = [remote(1 + j, parts_ref.at[my_chip], parts_ref.at[my_chip], (px, py, c))
                    for j, (px, py) in enumerate(chips)]
        for cp in to_chips:
            cp.start()
        for cp in to_chips:
            cp.wait_recv()
        out_ref[mine] = (parts_ref[0] + parts_ref[1]) + (parts_ref[2] + parts_ref[3])
        done = remote(4, out_ref.at[mine], out_ref.at[mine], sibling)
        done.start()
        remote(4, out_ref.at[theirs], out_ref.at[theirs], sibling).wait_recv()
        for cp in extras:
            cp.wait()
        to_sibling.wait_send()
        for cp in to_chips:
            cp.wait_send()
        done.wait_send()

    vmem = pl.BlockSpec(memory_space=pltpu.VMEM)
    return pl.pallas_call(
        body, in_specs=[vmem, vmem], out_specs=[vmem, vmem],
        out_shape=[jax.ShapeDtypeStruct((r, lanes), F32), jax.ShapeDtypeStruct((N_DEV, e, lanes), F32)],
        scratch_shapes=[pltpu.VMEM((h, lanes), F32), pltpu.VMEM((4, h, lanes), F32),
                        pltpu.SemaphoreType.DMA((12,)), pltpu.SemaphoreType.DMA((12,))],
        compiler_params=pltpu.CompilerParams(vmem_limit_bytes=VMEM_LIMIT), name=name)(small, extra)


def _sibling_swap(arrs, *, name):
    n = len(arrs)
    plan = _sibling_swap_plan(arrs)

    def body(*refs):
        start, _, finish = plan["phases"](refs[:n], refs[n:2 * n], *refs[2 * n:])
        start()
        finish()

    anyspec = pl.BlockSpec(memory_space=pl.ANY)
    return pl.pallas_call(
        body, in_specs=[anyspec] * n, out_specs=[anyspec] * n, out_shape=plan["out_shape"],
        scratch_shapes=plan["scratch"], name=name)(*arrs)


def _sibling_swap_plan(arrs):
    n = len(arrs)

    def phases(ins, outs, send_sems, recv_sems):
        x, y, c, _ = _mesh_place()
        copies = [pltpu.make_async_remote_copy(
            src_ref=ins[k].at[1 - c], dst_ref=outs[k], send_sem=send_sems.at[k], recv_sem=recv_sems.at[k],
            device_id=(x, y, 1 - c), device_id_type=pl.DeviceIdType.MESH) for k in range(n)]

        def start():
            for cp in copies:
                cp.start()

        def finish():
            for cp in copies:
                cp.wait()

        return start, (lambda: None), finish

    return dict(args=list(arrs), out_shape=[jax.ShapeDtypeStruct(a.shape[1:], a.dtype) for a in arrs],
                scratch=[pltpu.SemaphoreType.DMA((n,)), pltpu.SemaphoreType.DMA((n,))], phases=phases)


def _pair_add(mine, theirs, core, *, name, tr=256):
    _, r, c = mine.shape
    tr = min(tr, r)
    assert r % tr == 0

    def body(core_ref, m_ref, t_ref, o_ref):
        o_ref[...] = (m_ref[0].astype(F32) + t_ref[...].astype(F32)).astype(BF16)

    return pl.pallas_call(
        body,
        grid_spec=pltpu.PrefetchScalarGridSpec(
            num_scalar_prefetch=1, grid=(r // tr,),
            in_specs=[pl.BlockSpec((1, tr, c), lambda i, core_ref: (core_ref[0], i, 0)),
                      pl.BlockSpec((tr, c), lambda i, core_ref: (i, 0))],
            out_specs=pl.BlockSpec((tr, c), lambda i, core_ref: (i, 0))),
        out_shape=jax.ShapeDtypeStruct((r, c), BF16),
        compiler_params=_params("parallel"), name=name)(core, mine, theirs)


def _pair_add_small(mines, theirs, core, *, name):
    n = len(mines)

    def body(core_ref, *refs):
        for m_ref, t_ref, o_ref in zip(refs[:n], refs[n:2 * n], refs[2 * n:]):
            o_ref[...] = (m_ref[0].astype(F32) + t_ref[...].astype(F32)).astype(BF16)

    whole = lambda a: pl.BlockSpec(a.shape, lambda i, core_ref: (0,) * a.ndim)
    return pl.pallas_call(
        body,
        grid_spec=pltpu.PrefetchScalarGridSpec(
            num_scalar_prefetch=1, grid=(1,),
            in_specs=[pl.BlockSpec((1,) + m.shape[1:], lambda i, core_ref: (core_ref[0], 0, 0)) for m in mines]
                     + [whole(t) for t in theirs],
            out_specs=[whole(t) for t in theirs]),
        out_shape=[jax.ShapeDtypeStruct(t.shape, BF16) for t in theirs],
        compiler_params=_params("arbitrary"), name=name)(core, *mines, *theirs)


def _chip_scatter(arrs, *, name):
    n = len(arrs)
    plan = _chip_scatter_plan(arrs)

    def body(*refs):
        start, _, finish = plan["phases"](refs[:n], refs[n:2 * n], *refs[2 * n:])
        start()
        finish()

    anyspec = pl.BlockSpec(memory_space=pl.ANY)
    return pl.pallas_call(
        body, in_specs=[anyspec] * n, out_specs=[anyspec] * n, out_shape=plan["out_shape"],
        scratch_shapes=plan["scratch"], name=name)(*arrs)


def _chip_scatter_plan(arrs):
    n = len(arrs)

    def phases(ins, outs, send_sems, recv_sems, loc_sems):
        x, y, c, chips = _mesh_place()
        mine = 2 * x + y
        local = [pltpu.make_async_copy(ins[k].at[mine], outs[k].at[mine], loc_sems.at[k]) for k in range(n)]
        remote = [pltpu.make_async_remote_copy(
            src_ref=ins[k].at[2 * px + py], dst_ref=outs[k].at[mine], send_sem=send_sems.at[k, j],
            recv_sem=recv_sems.at[k, j], device_id=(px, py, c), device_id_type=pl.DeviceIdType.MESH)
            for j, (px, py) in enumerate(chips) for k in range(n)]

        def start():
            for cp in local + remote:
                cp.start()

        def finish():
            for cp in remote:
                cp.wait()
            for cp in local:
                cp.wait()

        return start, (lambda: None), finish

    return dict(
        args=list(arrs), out_shape=[jax.ShapeDtypeStruct(a.shape, a.dtype) for a in arrs],
        scratch=[pltpu.SemaphoreType.DMA((n, 3)), pltpu.SemaphoreType.DMA((n, 3)), pltpu.SemaphoreType.DMA((n,))],
        phases=phases)


def _ssm_discretize(a_re, a_im, log_dt, b_re, b_im):
    dt = jnp.exp(log_dt)[:, None]
    mag = jnp.exp(a_re * dt)
    lr = mag * jnp.cos(a_im * dt)
    li = mag * jnp.sin(a_im * dt)
    den = a_re * a_re + a_im * a_im
    cr = ((lr - 1.0) * a_re + li * a_im) / den
    ci = (li * a_re - (lr - 1.0) * a_im) / den
    bbr = cr[..., None] * b_re - ci[..., None] * b_im
    bbi = cr[..., None] * b_im + ci[..., None] * b_re
    return lr, li, bbr, bbi


def _ssm_dense(lr, li, bbr, bbi, c_re, c_im):
    scan_ops = _scan_tables(lr.reshape(-1), li.reshape(-1)) + _scan_consts(SEQ_BLOCK)
    per_half = SSM_GROUPS // SSM_HALVES

    def halves(a, rows, cols):
        a = a.reshape(SSM_HALVES, per_half, rows, 1, cols)
        shape = (SSM_HALVES, per_half, rows, per_half, cols)
        on_diagonal = lax.broadcasted_iota(jnp.int32, shape, 1) == lax.broadcasted_iota(jnp.int32, shape, 3)
        return jnp.where(on_diagonal, a, 0.0).reshape(SSM_HALVES, per_half * rows, per_half * cols)

    bblk = jnp.stack([halves(b.transpose(0, 2, 1), SSM_GROUP, SSM_STATE) for b in (bbr, bbi)]).astype(BF16)
    cblk = jnp.stack([halves(c.transpose(0, 2, 1), SSM_STATE, SSM_GROUP) for c in (c_re, c_im)]).astype(BF16)
    return scan_ops, bblk, cblk


def _ssm_extract(db, dc, sums):
    db = db.reshape(2, SSM_GROUPS, SSM_GROUP, SSM_STATE).transpose(0, 1, 3, 2)
    dc = dc.reshape(2, SSM_GROUPS, SSM_GROUP, SSM_STATE)
    dlr = sums[0].reshape(SSM_GROUPS, SSM_STATE)
    dli = sums[1].reshape(SSM_GROUPS, SSM_STATE)
    return dlr, dli, db[0], db[1], dc[0], dc[1]


def _in_groups():
    names = ("q", "k", "v", "u_ssm", "u_pool", "z_att", "z_ssm", "z_pool", "gates")
    sizes = (ATT_W, KV_W, KV_W, SSM_W, POOL_W, ATT_W, SSM_W, POOL_W, 3 * D_MODEL)
    r, lo = {}, 0
    for nm, s in zip(names, sizes):
        r[nm] = (lo, lo + s)
        lo += s
    kv = (r["k"][0], r["v"][1])
    return ((r["q"], r["z_att"], kv), (r["u_ssm"], r["z_ssm"]), (r["u_pool"], r["z_pool"]), (r["gates"],))


IN_GROUPS = _in_groups()


def _layer_fwd(x, lw, li, late=None, comm_attn=None, comm_ssm=None):
    tag = f"l{li}"
    h, (pa, ps, pp, pg), arrived = _ln_proj(x, lw["norm_g"], lw["shift"], lw["scale"], lw["w_in"], IN_GROUPS,
                                            name=f"ln_proj_{tag}", comm=None if late is None else late[0])
    if late is not None:
        lw = {**lw, **late[1](arrived)}
    ya, from_attn = _attn_fwd(pa, lw["sinks"], name=f"attn_fwd_{tag}", comm=comm_attn)
    ys, chk, states, *from_ssm = _ssm_fwd(ps, lw["lam"], lw["bblk"], lw["cblk"], lw["ssm_d"], lw["w_glu"], lw["b_glu"],
                                          name=f"ssm_fwd_{tag}", comm=comm_ssm)
    yp = _pool_fwd(pp, lw["w_pool"], lw["pool_scale"], name=f"pool_fwd_{tag}")
    x_new, merged, ba, bs, bp, out = _merge_fwd(ya, ys, yp, lw["w_br_att"], lw["w_br_ssm"], lw["w_br_pool"], pg,
                                                lw["w_out"], x, lw["gate"], name=f"merge_fwd_{tag}")
    saved = dict(x=x, h=h, pa=pa, ps=ps, pp=pp, pg=pg, ya=ya, ys=ys, yp=yp, chk=chk, states=states, merged=merged,
                 ba=ba, bs=bs, bp=bp, out=out)
    return x_new, saved, lw, list(from_attn), list(from_ssm)


def _layer_bwd(dx, lw, sv, li, later=None, own=None):
    tag = f"l{li}"
    g = {}
    merge_out, swapped = _merge_bwd(
        dx, sv["out"], lw["gate"], lw["w_out"], sv["pg"], (sv["ba"], sv["bs"], sv["bp"]),
        (lw["w_br_att"], lw["w_br_ssm"], lw["w_br_pool"]), (sv["ya"], sv["ys"], sv["yp"]), sv["merged"],
        name=f"merge_bwd_{tag}", comm=None if later is None else later[0])
    dya, dys, dyp, dpg, gate_sums, g["w_br_att"], g["w_br_ssm"], g["w_br_pool"], g["w_out"] = merge_out
    dpa, dsink = _attn_bwd(sv["pa"], lw["sinks"], dya, name=f"attn_bwd_{tag}")
    dps, db_dense, dc_dense, dwglu, ssm_sums, *exchanged = _ssm_bwd(
        sv["ps"], dys, sv["chk"], sv["states"], lw["lam"], lw["bblk"], lw["cblk"], lw["ssm_d"], lw["w_glu"], lw["b_glu"],
        name=f"ssm_bwd_{tag}", comm=None if later is None else later[1](swapped))
    g["w_glu"] = dwglu.astype(BF16)
    dpp, dwpool, pool_sums = _pool_bwd(sv["pp"], dyp, lw["w_pool"], lw["pool_scale"], name=f"pool_bwd_{tag}")
    h = sv["h"]
    dproj = (dpa, dps, dpp, dpg)
    g["w_in"], from_late = _mm_tn_grouped(h, dproj, IN_GROUPS, name=f"dw_in_{tag}",
                                          comm=None if own is None else own({k: g[k] for k in LATE_WEIGHTS}))
    dx_in, ln_sums, from_w_in = _ln_proj_bwd(dproj, lw["w_in"], IN_GROUPS, sv["x"], dx, lw["norm_g"], lw["scale"],
                                             name=f"ln_proj_bwd_{tag}",
                                             comm=None if own is None else own({"w_in": g["w_in"]}))
    g["dmod"] = jnp.concatenate([ln_sums[0], ln_sums[1], gate_sums[0]])
    g["norm_g"] = ln_sums[2]
    g["attn_sinks"] = dsink[:, 0]
    g["ssm_raw"] = _ssm_extract(db_dense, dc_dense, ssm_sums)
    g["ssm_d"] = ssm_sums[2, :SSM_W]
    g["b_glu"] = ssm_sums[3, :SSM_W]
    g["w_pool"] = dwpool
    g["pool_scale"] = pool_sums[0]
    return dx_in, g, exchanged, list(from_w_in) + list(from_late)


BIG_WEIGHTS = ("w_in", "w_glu", "w_br_att", "w_br_ssm", "w_br_pool", "w_out")
ROW_SHARDED = ("w_glu", "w_out")


LATE_WEIGHTS = BIG_WEIGHTS[1:]


def _full_weights(keys, gathered):
    full = {}
    for k, g in zip(keys, gathered):
        if k in ROW_SHARDED:
            full[k] = g.reshape(N_DEV * g.shape[1], g.shape[2])
        else:
            full[k] = g.transpose(1, 0, 2).reshape(g.shape[1], N_DEV * g.shape[2])
    return full


def _by_destination(keys, grads):
    out = []
    for k in keys:
        g = grads[k]
        if k in ROW_SHARDED:
            out.append(g.reshape(4, 2, g.shape[0] // N_DEV, g.shape[1]).transpose(1, 0, 2, 3))
        else:
            out.append(g.reshape(g.shape[0], 4, 2, g.shape[1] // N_DEV).transpose(2, 1, 0, 3))
    return out


def _prepare_layer(li, mod, norm_g, w_in_full, attn_sinks, disc, ssm_c_re, ssm_c_im, ssm_d, b_glu, w_pool, pool_scale):
    d = D_MODEL
    lr, li_, bbr, bbi = disc
    lam, bblk, cblk = _ssm_dense(lr[li], li_[li], bbr[li], bbi[li], ssm_c_re[li], ssm_c_im[li])
    return dict(
        norm_g=norm_g[li][None, :], shift=mod[li, :d][None, :], scale=mod[li, d:2 * d][None, :],
        gate=mod[li, 2 * d:][None, :], w_in=w_in_full,
        sinks=attn_sinks[li], lam=lam, bblk=bblk, cblk=cblk, ssm_d=ssm_d[li][None, :],
        b_glu=b_glu[li][None, :], w_pool=w_pool[li].astype(BF16), pool_scale=pool_scale[li][None, :])


SMALL_ROWS = 64
SMALL_ORDER = ("norm_g", "attn_sinks", "ssm_d", "b_glu", "w_pool", "pool_scale", "dmod")


def _pack_small(loss, dfinal_g, layer_grads):
    parts = [jnp.broadcast_to(loss.reshape(1), (128,)), dfinal_g]
    for g in layer_grads:
        for k in SMALL_ORDER:
            v = g[k].reshape(-1)
            if v.shape[0] % 128:
                v = jnp.pad(v, (0, 128 - v.shape[0] % 128))
            parts.append(v)
        for v in g["ssm_raw"]:
            parts.append(v.reshape(-1))
    flat = jnp.concatenate(parts)
    return jnp.pad(flat, (0, (-flat.shape[0]) % (SMALL_ROWS * 128))).reshape(-1, 128)


def _unpack_small(flat, shapes):
    out, off = [], 0
    for s in shapes:
        n = int(math.prod(s))
        out.append(flat[off:off + n].reshape(s))
        off += n + (-n) % 128
    return out


def kernel(x, c, norm_g, w_ada, b_ada, w_in, attn_sinks, ssm_a_re, ssm_a_im, ssm_log_dt, ssm_b_re, ssm_b_im, ssm_c_re, ssm_c_im, ssm_d, w_glu, b_glu, w_pool, pool_scale, w_br_att, w_br_ssm, w_br_pool, w_out, final_g, loss_target, m_norm_g, m_w_ada, m_b_ada, m_w_in, m_attn_sinks, m_ssm_a_re, m_ssm_a_im, m_ssm_log_dt, m_ssm_b_re, m_ssm_b_im, m_ssm_c_re, m_ssm_c_im, m_ssm_d, m_w_glu, m_b_glu, m_w_pool, m_pool_scale, m_w_br_att, m_w_br_ssm, m_w_br_pool, m_w_out, m_final_g, v_norm_g, v_w_ada, v_b_ada, v_w_in, v_attn_sinks, v_ssm_a_re, v_ssm_a_im, v_ssm_log_dt, v_ssm_b_re, v_ssm_b_im, v_ssm_c_re, v_ssm_c_im, v_ssm_d, v_w_glu, v_b_glu, v_w_pool, v_pool_scale, v_w_br_att, v_w_br_ssm, v_w_br_pool, v_w_out, v_final_g):
    me = 4 * lax.axis_index("x") + 2 * lax.axis_index("y") + lax.axis_index("c")
    d = D_MODEL
    ada_w = 3 * d // N_DEV

    (c_all,) = _exchange([c.reshape(8, 128)], scatter=False, name="gather_c")
    c_act = jax.nn.silu(c_all.reshape(N_DEV, d))
    b_cols = lax.dynamic_slice(b_ada, (0, me * ada_w), (DEPTH, ada_w))
    mod_part = jnp.concatenate(
        [_mm(c_act, w_ada[li], name=f"ada_fwd_l{li}") + b_cols[li][None, :] for li in range(DEPTH)], axis=0)
    (mod_all,) = _exchange([mod_part], scatter=False, name="gather_mod")
    mod_all = mod_all.reshape(N_DEV, DEPTH, N_DEV, ada_w)
    mod_mine = lax.dynamic_index_in_dim(mod_all, me, axis=2, keepdims=False)
    mod_mine = mod_mine.transpose(1, 0, 2).reshape(DEPTH, 3 * d)

    sharded = dict(w_in=w_in, w_glu=w_glu, w_br_att=w_br_att, w_br_ssm=w_br_ssm, w_br_pool=w_br_pool, w_out=w_out)
    shards = lambda li, keys: [sharded[k][li].astype(BF16) for k in keys]
    disc, disc_vjp = jax.vjp(jax.vmap(_ssm_discretize), ssm_a_re, ssm_a_im, ssm_log_dt, ssm_b_re, ssm_b_im)
    layer = lambda li, gathered_w_in: _prepare_layer(
        li, mod_mine, norm_g, _full_weights(("w_in",), gathered_w_in)["w_in"], attn_sinks, disc, ssm_c_re, ssm_c_im,
        ssm_d, b_glu, w_pool, pool_scale)
    late_weights = lambda gathered: _full_weights(LATE_WEIGHTS, gathered)
    core = lax.axis_index("c").astype(jnp.int32).reshape(1)

    def add_pairs(keys, by_dest, from_sibling, tag):
        flat = {k: (a.reshape(2, -1, a.shape[-1]), b.reshape(-1, b.shape[-1]))
                for k, a, b in zip(keys, by_dest, from_sibling)}
        small = [k for k in keys if k != "w_in"]
        sums = {}
        if "w_in" in flat:
            sums["w_in"] = _pair_add(*flat["w_in"], core, name=f"grads_pair_add_{tag}_w_in")
        if small:
            added = _pair_add_small([flat[k][0] for k in small], [flat[k][1] for k in small], core,
                                    name=f"grads_pair_add_{tag}_late")
            sums.update(zip(small, added))
        return [sums[k].reshape(b.shape) for k, b in zip(keys, from_sibling)]

    def chip_sums_of(keys, grads_li, tag):
        by_dest = _by_destination(keys, grads_li)
        return add_pairs(keys, by_dest, _sibling_swap(by_dest, name=f"grads_sibling_swap_{tag}"), tag)

    layers, saved, grads = [None] * DEPTH, [None] * DEPTH, [None] * DEPTH
    layers[0] = layer(0, _gather_two_level(shards(0, ("w_in",)), name="gather_w_in_l0"))
    xs, saved[0], layers[0], late1, w_in1 = _layer_fwd(
        x[0], layers[0], 0, late=(_gather_plan(shards(0, LATE_WEIGHTS)), late_weights),
        comm_attn=_gather_plan(shards(1, LATE_WEIGHTS)), comm_ssm=_gather_plan(shards(1, ("w_in",))))
    layers[1] = {**layer(1, w_in1), **late_weights(late1)}
    xs, saved[1], _, _, _ = _layer_fwd(xs, layers[1], 1)
    dx, fin_sums = _final_loss(xs, final_g[None, :], loss_target[0])
    loss_part = jnp.sum(fin_sums[1])
    dx, grads[1], _, _ = _layer_bwd(dx, layers[1], saved[1], 1)
    by_dest1 = _by_destination(BIG_WEIGHTS, grads[1])
    dx, grads[0], scattered1, scattered0 = _layer_bwd(
        dx, layers[0], saved[0], 0,
        later=(_sibling_swap_plan(by_dest1),
               lambda swapped: _chip_scatter_plan(add_pairs(BIG_WEIGHTS, by_dest1, swapped, "l1"))),
        own=lambda g: _chip_scatter_plan(chip_sums_of(tuple(g), g, "l0_" + "_".join(g))))
    big = list(zip(scattered0, scattered1))
    grad_x = dx[None]

    small = _pack_small(loss_part, fin_sums[0], grads)
    dmod_rows = jnp.concatenate([grads[li]["dmod"] for li in range(DEPTH)]).reshape(-1, 128)
    small_sum, dmod_gathered = _allreduce_small(small, dmod_rows, name="allreduce_small")
    out = {}

    def adam(name, w, g_slabs, m, v):
        shp = w.shape
        r = int(math.prod(shp[:-1])) if len(shp) > 1 else 1
        w2, m2, v2 = (a.reshape(r, shp[-1]) for a in (w, m, v))
        gs = [g.reshape(g.shape[0], r // len(g_slabs), shp[-1]) for g in g_slabs]
        res = _adamw(w2, gs, m2, v2, name=f"adamw_{name}")
        out[name] = tuple(a.reshape(shp) for a in res)

    flat = small_sum.reshape(-1)
    shapes = [(128,), (d,)]
    for _ in range(DEPTH):
        shapes += [(d,), (N_HEADS,), (SSM_W,), (SSM_W,), (4, POOL_GW, POOL_GW), (POOL_W,), (3 * d,),
                   (SSM_GROUPS, SSM_STATE), (SSM_GROUPS, SSM_STATE), (SSM_GROUPS, SSM_STATE, SSM_GROUP),
                   (SSM_GROUPS, SSM_STATE, SSM_GROUP), (SSM_GROUPS, SSM_GROUP, SSM_STATE), (SSM_GROUPS, SSM_GROUP, SSM_STATE)]
    un = _unpack_small(flat, shapes)
    loss = un[0][0]
    g_final_g = un[1]
    per = 13
    gl = [un[2 + li * per: 2 + (li + 1) * per] for li in range(DEPTH)]
    st = lambda j: jnp.stack([gl[li][j] for li in range(DEPTH)])
    g_norm_g, g_sinks, g_ssm_d, g_b_glu, g_w_pool, g_pool_scale, g_b_ada = (st(j) for j in range(7))
    d_lr, d_li, d_bbr, d_bbi, g_c_re, g_c_im = (st(j) for j in range(7, 13))
    g_a_re, g_a_im, g_log_dt, g_b_re, g_b_im = disc_vjp((d_lr, d_li, d_bbr, d_bbi))

    dmod_all = lax.dynamic_slice(dmod_gathered.reshape(N_DEV, DEPTH, 3 * d), (0, 0, me * ada_w), (N_DEV, DEPTH, ada_w))
    dmod_all = dmod_all.transpose(1, 0, 2)
    g_w_ada = jnp.stack([_mm_tn(c_act, dmod_all[li], tm=d, tn=ada_w, tk=N_DEV, name=f"dw_ada_l{li}") for li in range(DEPTH)])

    adam("w_ada", w_ada, [g_w_ada[None]], m_w_ada, v_w_ada)
    adam("w_in", w_in, big[0], m_w_in, v_w_in)
    adam("w_glu", w_glu, big[1], m_w_glu, v_w_glu)
    adam("w_br_att", w_br_att, big[2], m_w_br_att, v_w_br_att)
    adam("w_br_ssm", w_br_ssm, big[3], m_w_br_ssm, v_w_br_ssm)
    adam("w_br_pool", w_br_pool, big[4], m_w_br_pool, v_w_br_pool)
    adam("w_out", w_out, big[5], m_w_out, v_w_out)

    small_names = ["norm_g", "b_ada", "attn_sinks", "ssm_a_re", "ssm_a_im", "ssm_log_dt", "ssm_b_re", "ssm_b_im",
                   "ssm_c_re", "ssm_c_im", "ssm_d", "b_glu", "w_pool", "pool_scale", "final_g"]
    small_w = [norm_g, b_ada, attn_sinks, ssm_a_re, ssm_a_im, ssm_log_dt, ssm_b_re, ssm_b_im, ssm_c_re, ssm_c_im,
               ssm_d, b_glu, w_pool, pool_scale, final_g]
    small_m = [m_norm_g, m_b_ada, m_attn_sinks, m_ssm_a_re, m_ssm_a_im, m_ssm_log_dt, m_ssm_b_re, m_ssm_b_im,
               m_ssm_c_re, m_ssm_c_im, m_ssm_d, m_b_glu, m_w_pool, m_pool_scale, m_final_g]
    small_v = [v_norm_g, v_b_ada, v_attn_sinks, v_ssm_a_re, v_ssm_a_im, v_ssm_log_dt, v_ssm_b_re, v_ssm_b_im,
               v_ssm_c_re, v_ssm_c_im, v_ssm_d, v_b_glu, v_w_pool, v_pool_scale, v_final_g]
    small_g = [g_norm_g, g_b_ada, g_sinks, g_a_re, g_a_im, g_log_dt, g_b_re, g_b_im, g_c_re, g_c_im,
               g_ssm_d, g_b_glu, g_w_pool, g_pool_scale, g_final_g]

    for nm, w, g, m, v in zip(small_names, small_w, small_g, small_m, small_v):
        adam(nm, w, [g[None]], m, v)

    order = ["norm_g", "w_ada", "b_ada", "w_in", "attn_sinks", "ssm_a_re", "ssm_a_im", "ssm_log_dt", "ssm_b_re",
             "ssm_b_im", "ssm_c_re", "ssm_c_im", "ssm_d", "w_glu", "b_glu", "w_pool", "pool_scale", "w_br_att",
             "w_br_ssm", "w_br_pool", "w_out", "final_g"]
    return (loss, grad_x, *[out[k][0] for k in order], *[out[k][1] for k in order],
            *[out[k][2] for k in order], *[out[k][3] for k in order])
```

```python
import functools
import math

import jax
import jax.numpy as jnp
from jax import lax
from jax.experimental import pallas as pl
from jax.experimental.pallas import tpu as pltpu

F32 = jnp.float32
BF16 = jnp.bfloat16

N_DEV = 8
D_MODEL = 1024
DEPTH = 2
CHUNK = 64
N_HEADS = 8
N_KV_HEADS = 2
HEAD_DIM = 64
Q_PER_KV = N_HEADS // N_KV_HEADS
WINDOW = 128
ATT_W = 512
KV_W = 128
SSM_W = 512
SSM_GROUP = 16
SSM_GROUPS = 32
SSM_STATE = 64
SSM_N = SSM_GROUPS * SSM_STATE
POOL_W = 512
POOL_WINDOWS = (2, 4, 8, 16)
POOL_GW = 128
POOL_HALO = 16
EPS = 1e-6
NEG_INF = -1e30
ADAM_LR = 0.001
ADAM_B1 = 0.9
ADAM_B2 = 0.999
ADAM_EPS = 1e-08
ADAM_WD = 0.01
ADAM_STEP = 10

SEQ_BLOCK = 256
ATT_BLOCK = 128
VMEM_LIMIT = 56 * 1024 * 1024

NN = (((1,), (0,)), ((), ()))
NT = (((1,), (1,)), ((), ()))
TN = (((0,), (0,)), ((), ()))


def _dot(a, b, dims=NN):
    return lax.dot_general(a.astype(BF16), b.astype(BF16), dims, preferred_element_type=F32)


def _params(*sem):
    return pltpu.CompilerParams(dimension_semantics=sem, vmem_limit_bytes=VMEM_LIMIT)


def _sigmoid(x):
    return 0.5 + 0.5 * jnp.tanh(0.5 * x)


def _silu_and_grad(z):
    s = _sigmoid(z)
    return z * s, s * (1.0 + z * (1.0 - s))


_GELU_K = math.sqrt(2.0 / math.pi)


def _gelu_and_grad(x):
    inner = _GELU_K * (x + 0.044715 * x * x * x)
    t = jnp.tanh(inner)
    val = 0.5 * x * (1.0 + t)
    grad = 0.5 * (1.0 + t) + 0.5 * x * (1.0 - t * t) * _GELU_K * (1.0 + 3.0 * 0.044715 * x * x)
    return val, grad


def _mm(a, b, *, nt=False, out_dtype=F32, tm=1024, tn=1024, name, comm=None):
    m, k = a.shape
    n = b.shape[0] if nt else b.shape[1]
    tm, tn = min(tm, m), min(tn, n)
    assert m % tm == 0 and n % tn == 0
    dims = NT if nt else NN
    grid = (m // tm, n // tn)
    c_args, c_in, c_out, c_shape, c_scratch = _comm_extra(comm)

    def body(a_ref, b_ref, o_ref):
        o_ref[...] = _dot(a_ref[...], b_ref[...], dims).astype(out_dtype)

    b_spec = pl.BlockSpec((tn, k), lambda i, j: (j, 0)) if nt else pl.BlockSpec((k, tn), lambda i, j: (0, j))
    res = pl.pallas_call(
        _with_comm(body, comm, 2, 1, grid, -1), grid=grid,
        in_specs=[pl.BlockSpec((tm, k), lambda i, j: (i, 0)), b_spec] + c_in,
        out_specs=[pl.BlockSpec((tm, tn), lambda i, j: (i, j))] + c_out,
        out_shape=[jax.ShapeDtypeStruct((m, n), out_dtype)] + c_shape,
        scratch_shapes=c_scratch,
        compiler_params=_params(*(("arbitrary",) * 2 if comm else ("parallel",) * 2)), name=name)(a, b, *c_args)
    return (res[0], list(res[1:])) if comm else res[0]


def _grouped_pieces(groups):
    out = []
    for ranges in groups:
        off, pieces = 0, []
        for lo, hi in ranges:
            pieces.append((off, lo, hi))
            off += hi - lo
        out.append(pieces)
    return out


def _mm_nt_grouped(ds, w, groups, *, out_dtype=F32, tm=512, tn=512, name, comm=None):
    m = ds[0].shape[0]
    n, k = w.shape
    nd = len(ds)
    pieces = _grouped_pieces(groups)
    grid = (m // tm, n // tn)
    c_args, c_in, c_out, c_shape, c_scratch = _comm_extra(comm)

    def body(*refs):
        d_refs, w_ref, o_ref = refs[:nd], refs[nd], refs[nd + 1]
        acc = None
        for d_ref, plist in zip(d_refs, pieces):
            for off, lo, hi in plist:
                term = _dot(d_ref[:, off:off + hi - lo], w_ref[:, lo:hi], NT)
                acc = term if acc is None else acc + term
        o_ref[...] = acc.astype(out_dtype)

    in_specs = [pl.BlockSpec((tm, a.shape[1]), lambda i, j: (i, 0)) for a in ds] + [pl.BlockSpec((tn, k), lambda i, j: (j, 0))]
    res = pl.pallas_call(
        _with_comm(body, comm, nd + 1, 1, grid, -1), grid=grid, in_specs=in_specs + c_in,
        out_specs=[pl.BlockSpec((tm, tn), lambda i, j: (i, j))] + c_out,
        out_shape=[jax.ShapeDtypeStruct((m, n), out_dtype)] + c_shape,
        scratch_shapes=c_scratch,
        compiler_params=_params(*(("arbitrary",) * 2 if comm else ("parallel",) * 2)), name=name)(*ds, w, *c_args)
    return (res[0], list(res[1:])) if comm else res[0]


def _mm_tn(a, b, *, out_dtype=F32, tm=1024, tn=1024, tk=1024, name, comm=None):
    k, m = a.shape
    n = b.shape[1]
    assert m % min(tm, m) == 0 and n % min(tn, n) == 0 and k % min(tk, k) == 0
    tm, tn, tk = min(tm, m), min(tn, n), min(tk, k)
    nk = k // tk
    grid = (m // tm, n // tn, nk)
    c_args, c_in, c_out, c_shape, c_scratch = _comm_extra(comm)

    def body(a_ref, b_ref, o_ref, acc_ref):
        kk = pl.program_id(2)

        @pl.when(kk == 0)
        def _():
            acc_ref[...] = jnp.zeros_like(acc_ref)

        acc_ref[...] += _dot(a_ref[...], b_ref[...], TN)

        @pl.when(kk == nk - 1)
        def _():
            o_ref[...] = acc_ref[...].astype(out_dtype)

    res = pl.pallas_call(
        _with_comm(body, comm, 2, 1, grid, -1), grid=grid,
        in_specs=[pl.BlockSpec((tk, tm), lambda i, j, kk: (kk, i)), pl.BlockSpec((tk, tn), lambda i, j, kk: (kk, j))] + c_in,
        out_specs=[pl.BlockSpec((tm, tn), lambda i, j, kk: (i, j))] + c_out,
        out_shape=[jax.ShapeDtypeStruct((m, n), out_dtype)] + c_shape,
        scratch_shapes=[pltpu.VMEM((tm, tn), F32)] + c_scratch,
        compiler_params=_params(*(("arbitrary",) * 3 if comm else ("parallel", "parallel", "arbitrary"))),
        name=name)(a, b, *c_args)
    return (res[0], list(res[1:])) if comm else res[0]


def _mm_tn_grouped(a, bs, groups, *, tm=512, tk=512, name, comm=None):
    k, m = a.shape
    tm, tk = min(tm, m), min(tk, k)
    assert m % tm == 0 and k % tk == 0
    nk, nb = k // tk, len(bs)
    pieces = _grouped_pieces(groups)
    n = sum(b.shape[1] for b in bs)
    grid = (m // tm, nk)
    c_args, c_in, c_out, c_shape, c_scratch = _comm_extra(comm)

    def body(a_ref, *refs):
        b_refs, o_ref, acc_refs = refs[:nb], refs[nb], refs[nb + 1:]
        kk = pl.program_id(1)
        av = a_ref[...]
        for b_ref, acc_ref, plist in zip(b_refs, acc_refs, pieces):
            @pl.when(kk == 0)
            def _():
                acc_ref[...] = jnp.zeros_like(acc_ref)

            acc_ref[...] += _dot(av, b_ref[...], TN)

            @pl.when(kk == nk - 1)
            def _():
                for off, lo, hi in plist:
                    o_ref[:, lo:hi] = acc_ref[:, off:off + hi - lo].astype(BF16)

    res = pl.pallas_call(
        _with_comm(body, comm, 1 + nb, 1, grid, -1), grid=grid,
        in_specs=[pl.BlockSpec((tk, tm), lambda i, kk: (kk, i))]
                 + [pl.BlockSpec((tk, b.shape[1]), lambda i, kk: (kk, 0)) for b in bs] + c_in,
        out_specs=[pl.BlockSpec((tm, n), lambda i, kk: (i, 0))] + c_out,
        out_shape=[jax.ShapeDtypeStruct((m, n), BF16)] + c_shape,
        scratch_shapes=[pltpu.VMEM((tm, b.shape[1]), F32) for b in bs] + c_scratch,
        compiler_params=_params("arbitrary", "arbitrary"), name=name)(a, *bs, *c_args)
    return res[0], list(res[1:])


def _ln_proj(x, g, shift, scale, w, groups, *, name, tm=512, comm=None):
    l, d = x.shape
    tm = min(tm, l)
    nb = l // tm
    pieces = _grouped_pieces(groups)
    widths = [sum(hi - lo for _, lo, hi in plist) for plist in pieces]
    nw = len(pieces)
    c_args, c_in, c_out, c_shape, c_scratch = _comm_extra(comm)

    def body(x_ref, g_ref, sh_ref, sc_ref, w_ref, h_ref, *p_refs):
        xv = x_ref[...]
        n = xv * lax.rsqrt(jnp.mean(xv * xv, axis=-1, keepdims=True) + EPS)
        h = ((n * g_ref[...]) * (1.0 + sc_ref[...]) + sh_ref[...]).astype(BF16)
        h_ref[...] = h
        for p_ref, plist in zip(p_refs, pieces):
            for off, lo, hi in plist:
                p_ref[:, off:off + hi - lo] = _dot(h, w_ref[:, lo:hi]).astype(BF16)

    vec = pl.BlockSpec((1, d), lambda i: (0, 0))
    row = lambda n: pl.BlockSpec((tm, n), lambda i: (i, 0))
    res = pl.pallas_call(
        _with_comm(body, comm, 5, 1 + nw, nb, -1), grid=(nb,),
        in_specs=[row(d), vec, vec, vec, pl.BlockSpec(w.shape, lambda i: (0, 0))] + c_in,
        out_specs=[row(d)] + [row(n) for n in widths] + c_out,
        out_shape=[jax.ShapeDtypeStruct((l, d), BF16)] + [jax.ShapeDtypeStruct((l, n), BF16) for n in widths] + c_shape,
        scratch_shapes=c_scratch,
        compiler_params=_params("arbitrary"), name=name)(x, g, shift, scale, w, *c_args)
    return res[0], list(res[1:1 + nw]), list(res[1 + nw:])


def _ln_proj_bwd(ds, w, groups, x, dres, g, scale, *, name, tm=256, comm=None):
    l, d = x.shape
    tm = min(tm, l)
    nb = l // tm
    nd = len(ds)
    pieces = _grouped_pieces(groups)
    c_args, c_in, c_out, c_shape, c_scratch = _comm_extra(comm)

    def body(*refs):
        d_refs = refs[:nd]
        w_ref, x_ref, dres_ref, g_ref, sc_ref, dx_ref, sums_ref = refs[nd:]
        dhv = None
        for d_ref, plist in zip(d_refs, pieces):
            for off, lo, hi in plist:
                term = _dot(d_ref[:, off:off + hi - lo], w_ref[:, lo:hi], NT)
                dhv = term if dhv is None else dhv + term
        xv = x_ref[...]
        rstd = lax.rsqrt(jnp.mean(xv * xv, axis=-1, keepdims=True) + EPS)
        n = xv * rstd
        gv = g_ref[...]
        dr = dhv * (1.0 + sc_ref[...])
        dn = dr * gv
        dx_ref[...] = dres_ref[...] + rstd * (dn - n * jnp.mean(dn * n, axis=-1, keepdims=True))

        @pl.when(pl.program_id(0) == 0)
        def _():
            sums_ref[...] = jnp.zeros_like(sums_ref)

        sums_ref[0:1, :] += jnp.sum(dhv, axis=0, keepdims=True)
        sums_ref[1:2, :] += jnp.sum(dhv * (n * gv), axis=0, keepdims=True)
        sums_ref[2:3, :] += jnp.sum(dr * n, axis=0, keepdims=True)

    vec = pl.BlockSpec((1, d), lambda i: (0, 0))
    row = pl.BlockSpec((tm, d), lambda i: (i, 0))
    res = pl.pallas_call(
        _with_comm(body, comm, nd + 5, 2, nb, -1), grid=(nb,),
        in_specs=[pl.BlockSpec((tm, a.shape[1]), lambda i: (i, 0)) for a in ds]
                 + [pl.BlockSpec(w.shape, lambda i: (0, 0)), row, row, vec, vec] + c_in,
        out_specs=[row, pl.BlockSpec((8, d), lambda i: (0, 0))] + c_out,
        out_shape=[jax.ShapeDtypeStruct((l, d), F32), jax.ShapeDtypeStruct((8, d), F32)] + c_shape,
        scratch_shapes=c_scratch,
        compiler_params=_params("arbitrary"), name=name)(*ds, w, x, dres, g, scale, *c_args)
    return res[0], res[1], list(res[2:])


def _final_loss(x, g, target, *, tm=512):
    l, d = x.shape

    def body(x_ref, g_ref, t_ref, dx_ref, sums_ref):
        xv = x_ref[...]
        rstd = lax.rsqrt(jnp.mean(xv * xv, axis=-1, keepdims=True) + EPS)
        n = xv * rstd
        gv = g_ref[...]
        err = n * gv - t_ref[...]
        dy = err * (1.0 / d)
        dn = dy * gv
        dx_ref[...] = rstd * (dn - n * jnp.mean(dn * n, axis=-1, keepdims=True))

        @pl.when(pl.program_id(0) == 0)
        def _():
            sums_ref[...] = jnp.zeros_like(sums_ref)

        sums_ref[0:1, :] += jnp.sum(dy * n, axis=0, keepdims=True)
        sums_ref[1:2, :] += jnp.sum(err * err, axis=0, keepdims=True) * (0.5 / d)

    vec = pl.BlockSpec((1, d), lambda i: (0, 0))
    row = pl.BlockSpec((tm, d), lambda i: (i, 0))
    dx, sums = pl.pallas_call(
        body, grid=(l // tm,),
        in_specs=[row, vec, row],
        out_specs=[row, pl.BlockSpec((8, d), lambda i: (0, 0))],
        out_shape=[jax.ShapeDtypeStruct((l, d), F32), jax.ShapeDtypeStruct((8, d), F32)],
        compiler_params=_params("arbitrary"), name="final_loss")(x, g, target)
    return dx, sums


def _attn_geometry(i, t):
    nk = t + WINDOW
    qi = lax.broadcasted_iota(jnp.int32, (t, nk), 0)
    kj = lax.broadcasted_iota(jnp.int32, (t, nk), 1)
    dist = jnp.abs(qi + WINDOW - kj).astype(F32)
    qc = jnp.right_shift(qi, 6)
    kc = jnp.right_shift(kj, 6)
    valid = (kc >= qc) & (kc <= qc + WINDOW // CHUNK) & ((i > 0) | (kj >= WINDOW))
    return dist, valid


def _attn_head(q, k_all, v_all, sink, slope, dist, valid):
    s = _dot(q, k_all, NT) * (1.0 / math.sqrt(HEAD_DIM)) - slope * dist
    s = jnp.where(valid, s, NEG_INF)
    m = jnp.maximum(jnp.max(s, axis=-1, keepdims=True), sink)
    e = jnp.exp(s - m)
    es = jnp.exp(sink - m)
    inv = 1.0 / (jnp.sum(e, axis=-1, keepdims=True) + es)
    p = e * inv
    o = _dot(p, v_all, NN)
    return p, o, es * inv


def _attn_specs(t):
    cur = pl.BlockSpec((t, ATT_W * 2 + KV_W * 2), lambda i: (i, 0))
    halo_blocks = t // WINDOW
    prev = pl.BlockSpec((WINDOW, 2 * KV_W), lambda i: (jnp.maximum(i * halo_blocks - 1, 0), (2 * ATT_W) // (2 * KV_W)))
    return cur, prev


def _attn_fwd(pa, sinks, *, name, t=ATT_BLOCK, comm=None):
    l = pa.shape[0]
    t = min(t, l)
    nb = l // t
    c_args, c_in, c_out, c_shape, c_scratch = _comm_extra(comm)

    def body(sink_ref, cur_ref, prev_ref, ya_ref):
        i = pl.program_id(0)
        dist, valid = _attn_geometry(i, t)
        for h in range(N_HEADS):
            kh = h // Q_PER_KV
            q = cur_ref[:, h * HEAD_DIM:(h + 1) * HEAD_DIM]
            z = cur_ref[:, ATT_W + h * HEAD_DIM:ATT_W + (h + 1) * HEAD_DIM].astype(F32)
            k_all = jnp.concatenate([prev_ref[:, kh * HEAD_DIM:(kh + 1) * HEAD_DIM],
                                     cur_ref[:, 2 * ATT_W + kh * HEAD_DIM:2 * ATT_W + (kh + 1) * HEAD_DIM]], axis=0)
            v_all = jnp.concatenate([prev_ref[:, KV_W + kh * HEAD_DIM:KV_W + (kh + 1) * HEAD_DIM],
                                     cur_ref[:, 2 * ATT_W + KV_W + kh * HEAD_DIM:2 * ATT_W + KV_W + (kh + 1) * HEAD_DIM]], axis=0)
            _, o, _ = _attn_head(q, k_all, v_all, sink_ref[h], 2.0 ** (-(h + 1)), dist, valid)
            sz, _ = _silu_and_grad(z)
            ya_ref[:, h * HEAD_DIM:(h + 1) * HEAD_DIM] = (o * sz).astype(BF16)

    cur, prev = _attn_specs(t)
    res = pl.pallas_call(
        _with_comm(body, comm, 3, 1, nb, nb - 1), grid=(nb,),
        in_specs=[pl.BlockSpec(memory_space=pltpu.SMEM), cur, prev] + c_in,
        out_specs=[pl.BlockSpec((t, ATT_W), lambda i: (i, 0))] + c_out,
        out_shape=[jax.ShapeDtypeStruct((l, ATT_W), BF16)] + c_shape,
        scratch_shapes=c_scratch,
        compiler_params=_params("arbitrary"), name=name)(sinks, pa, pa, *c_args)
    return res[0], res[1:]


def _attn_bwd(pa, sinks, dya, *, name, t=SEQ_BLOCK):
    l = pa.shape[0]
    t = min(t, l)
    nb = l // t
    scale = 1.0 / math.sqrt(HEAD_DIM)

    def body(sink_ref, cur_ref, prev_ref, dya_ref, dpa_ref, dsink_ref, carry_ref):
        n = pl.program_id(0)
        i = nb - 1 - n
        dist, valid = _attn_geometry(i, t)

        @pl.when(n == 0)
        def _():
            carry_ref[...] = jnp.zeros_like(carry_ref)
            dsink_ref[...] = jnp.zeros_like(dsink_ref)

        dk_acc = [jnp.zeros((HEAD_DIM, t + WINDOW), F32) for _ in range(N_KV_HEADS)]
        dv_acc = [jnp.zeros((HEAD_DIM, t + WINDOW), F32) for _ in range(N_KV_HEADS)]
        for h in range(N_HEADS):
            kh = h // Q_PER_KV
            q = cur_ref[:, h * HEAD_DIM:(h + 1) * HEAD_DIM]
            z = cur_ref[:, ATT_W + h * HEAD_DIM:ATT_W + (h + 1) * HEAD_DIM].astype(F32)
            k_all = jnp.concatenate([prev_ref[:, kh * HEAD_DIM:(kh + 1) * HEAD_DIM],
                                     cur_ref[:, 2 * ATT_W + kh * HEAD_DIM:2 * ATT_W + (kh + 1) * HEAD_DIM]], axis=0)
            v_all = jnp.concatenate([prev_ref[:, KV_W + kh * HEAD_DIM:KV_W + (kh + 1) * HEAD_DIM],
                                     cur_ref[:, 2 * ATT_W + KV_W + kh * HEAD_DIM:2 * ATT_W + KV_W + (kh + 1) * HEAD_DIM]], axis=0)
            p, o, p_sink = _attn_head(q, k_all, v_all, sink_ref[h], 2.0 ** (-(h + 1)), dist, valid)
            dy = dya_ref[:, h * HEAD_DIM:(h + 1) * HEAD_DIM]
            sz, dsz = _silu_and_grad(z)
            do = dy * sz
            dpa_ref[:, ATT_W + h * HEAD_DIM:ATT_W + (h + 1) * HEAD_DIM] = (dy * o * dsz).astype(BF16)
            delta = jnp.sum(do * o, axis=-1, keepdims=True)
            dp = _dot(do, v_all, NT)
            ds = p * (dp - delta)
            dpa_ref[:, h * HEAD_DIM:(h + 1) * HEAD_DIM] = (_dot(ds, k_all, NN) * scale).astype(BF16)
            dk_acc[kh] = dk_acc[kh] + _dot(q, ds, TN) * scale
            dv_acc[kh] = dv_acc[kh] + _dot(do, p, TN)
            dsink_ref[h:h + 1, :] += jnp.broadcast_to(-jnp.sum(p_sink * delta, axis=0, keepdims=True), (1, 128))

        acc = jnp.concatenate(dk_acc + dv_acc, axis=0).T
        own = acc[WINDOW:, :]
        tail = own[t - WINDOW:, :] + carry_ref[...]
        if t > WINDOW:
            dpa_ref[0:t - WINDOW, 2 * ATT_W:] = own[:t - WINDOW, :].astype(BF16)
        dpa_ref[t - WINDOW:t, 2 * ATT_W:] = tail.astype(BF16)
        carry_ref[...] = acc[:WINDOW, :]

    halo_blocks = t // WINDOW
    wpa = 2 * ATT_W + 2 * KV_W
    cur = pl.BlockSpec((t, wpa), lambda n: (nb - 1 - n, 0))
    prev = pl.BlockSpec((WINDOW, 2 * KV_W),
                        lambda n: (jnp.maximum((nb - 1 - n) * halo_blocks - 1, 0), (2 * ATT_W) // (2 * KV_W)))
    return pl.pallas_call(
        body, grid=(nb,),
        in_specs=[pl.BlockSpec(memory_space=pltpu.SMEM), cur, prev, pl.BlockSpec((t, ATT_W), lambda n: (nb - 1 - n, 0))],
        out_specs=[pl.BlockSpec((t, wpa), lambda n: (nb - 1 - n, 0)), pl.BlockSpec((8, 128), lambda n: (0, 0))],
        out_shape=[jax.ShapeDtypeStruct((l, wpa), BF16), jax.ShapeDtypeStruct((8, 128), F32)],
        scratch_shapes=[pltpu.VMEM((WINDOW, 2 * KV_W), F32)],
        compiler_params=_params("arbitrary"), name=name)(sinks, pa, pa, dya)


def _scan(xr, xi, lr, li, t, reverse):
    row = lax.broadcasted_iota(jnp.int32, (t, 1), 0)
    d = 1
    pr, pi = lr, li
    while d < t:
        if reverse:
            sr = jnp.where(row < t - d, pltpu.roll(xr, t - d, 0), 0.0)
            si = jnp.where(row < t - d, pltpu.roll(xi, t - d, 0), 0.0)
        else:
            sr = jnp.where(row >= d, pltpu.roll(xr, d, 0), 0.0)
            si = jnp.where(row >= d, pltpu.roll(xi, d, 0), 0.0)
        xr, xi = xr + pr * sr - pi * si, xi + pr * si + pi * sr
        pr, pi = pr * pr - pi * pi, 2.0 * pr * pi
        d *= 2
    return xr, xi


SCAN_SUB = 8


def _split_hi_lo(a):
    hi = a.astype(BF16)
    lo = (a - hi.astype(F32)).astype(BF16)
    return jnp.concatenate([hi, lo], axis=0)


def _scan_mxu(xr, xi, tab, lam3, lam8, tri, expand, cr, ci, t, reverse):
    ns = t // SCAN_SUB
    n = xr.shape[1]
    v3 = lambda a: a.reshape(ns, SCAN_SUB, n)
    x3r, x3i = v3(xr), v3(xi)
    br = (x3r * tab[0] - x3i * tab[1]).reshape(t, n)
    bi = (x3r * tab[1] + x3i * tab[0]).reshape(t, n)
    pm = jnp.dot(tri, jnp.concatenate([br, bi], axis=1).astype(BF16), preferred_element_type=F32)
    p3r, p3i = v3(pm[:t, :n]), v3(pm[:t, n:])
    slr = p3r * tab[2] - p3i * tab[3]
    sli = p3r * tab[3] + p3i * tab[2]
    totr, toti = pm[t:, :n], pm[t:, n:]
    l3r, l3i = lam3
    l8r, l8i = lam8
    row = lax.broadcasted_iota(jnp.int32, (ns, 1), 0)
    edge = row == (ns - 1 if reverse else 0)
    er = totr * l3r - toti * l3i + jnp.where(edge, l8r * cr - l8i * ci, 0.0)
    ei = totr * l3i + toti * l3r + jnp.where(edge, l8r * ci + l8i * cr, 0.0)
    er, ei = _scan(er, ei, l8r, l8i, ns, reverse)
    shift = ns - 1 if reverse else 1
    nbr = jnp.where(edge, cr, pltpu.roll(er, shift, 0))
    nbi = jnp.where(edge, ci, pltpu.roll(ei, shift, 0))
    ex = jnp.dot(expand, _split_hi_lo(jnp.concatenate([nbr, nbi], axis=1)), preferred_element_type=F32)
    e3r, e3i = v3(ex[:, :n]), v3(ex[:, n:])
    sr = (slr + e3r * tab[4] - e3i * tab[5]).reshape(t, n)
    si = (sli + e3r * tab[5] + e3i * tab[4]).reshape(t, n)
    out = 0 if reverse else ns - 1
    return sr, si, er[out:out + 1, :], ei[out:out + 1, :]


def _scan_consts(t):
    import numpy as np
    ns = t // SCAN_SUB
    r = np.arange(t)
    same = (r[:, None] // SCAN_SUB) == (r[None, :] // SCAN_SUB)
    sums = (np.arange(ns)[:, None] == (r[None, :] // SCAN_SUB))
    tri = []
    for keep in (r[None, :] <= r[:, None], r[None, :] >= r[:, None]):
        tri.append(np.concatenate([same & keep, sums], axis=0).astype(np.float32))
    ex = ((r[:, None] // SCAN_SUB) == np.arange(ns)[None, :]).astype(np.float32)
    return jnp.asarray(np.stack(tri), BF16), jnp.asarray(np.concatenate([ex, ex], axis=1), BF16)


def _scan_tables(lr, li):
    import numpy as np
    den = lr * lr + li * li
    ir, ii = lr / den, -li / den
    mul = lambda a, b: (a[0] * b[0] - a[1] * b[1], a[0] * b[1] + a[1] * b[0])
    pw = {0: (jnp.ones_like(lr), jnp.zeros_like(lr))}
    for e in range(1, 9):
        pw[e] = mul(pw[e - 1], (lr, li))
    for e in range(-1, -5, -1):
        pw[e] = mul(pw[e + 1], (ir, ii))
    powers = jnp.stack([jnp.stack(pw[e]) for e in range(-4, 9)] + [jnp.zeros((2, lr.shape[0]), F32)])
    j = np.arange(SCAN_SUB)
    exps = [4 - j, j - 4, j + 1, j - 3, 3 - j, 8 - j]
    e_idx = np.stack([exps[t] + 4 for t in range(6) for _ in range(2)])
    c_idx = np.tile(np.array([0, 1])[:, None], (6, SCAN_SUB))
    sign = np.where((c_idx == 1) & (np.arange(12)[:, None] >= 6), -1.0, 1.0).astype(np.float32)
    tabs = powers[e_idx, c_idx] * sign[:, :, None]
    lam = powers[np.array([5, 5, 7, 7, 12, 12, 13, 13]), np.array([0, 1, 0, 1, 0, 1, 0, 0])]
    return lam, tabs


SSM_HALVES = 2
SSM_HW = SSM_W // SSM_HALVES
SSM_HN = SSM_N // SSM_HALVES


def _bd_nn(x, w):
    a = w.shape[1]
    return jnp.concatenate([_dot(x[:, h * a:(h + 1) * a], w[h]) for h in range(SSM_HALVES)], axis=1)


def _bd_nt(x, w):
    b = w.shape[2]
    return jnp.concatenate([_dot(x[:, h * b:(h + 1) * b], w[h], NT) for h in range(SSM_HALVES)], axis=1)


def _bd_tn(x, y):
    a, b = x.shape[1] // SSM_HALVES, y.shape[1] // SSM_HALVES
    return jnp.stack([_dot(x[:, h * a:(h + 1) * a], y[:, h * b:(h + 1) * b], TN) for h in range(SSM_HALVES)])


def _ssm_states(u, s0r, s0i, lam_ref, tab_ref, tri_ref, ex_ref, bre, bim, t):
    tab = tuple(tab_ref[k] for k in range(6))
    return _scan_mxu(_bd_nn(u, bre), _bd_nn(u, bim), tab, (lam_ref[2:3, :], lam_ref[3:4, :]),
                     (lam_ref[4:5, :], lam_ref[5:6, :]), tri_ref[0], ex_ref[...], s0r, s0i, t, False)


def _ssm_head(u, z, xr, xi, cre, cim, dskip, wglu, bglu):
    y = _bd_nn(xr, cre) - _bd_nn(xi, cim) + dskip * u
    y2, dgelu = _gelu_and_grad(y)
    gate = _sigmoid(_dot(y2, wglu) + bglu)
    y3 = y2 * gate
    return y2, dgelu, gate, y3


def _with_comm(body, comm, n_in, n_out, grid, mid_step):
    if comm is None:
        return body
    nc = len(comm["args"])
    n_sem = len(comm["scratch"])
    grid = (grid,) if isinstance(grid, int) else tuple(grid)
    total = math.prod(grid)

    def hosted(*refs):
        ins, cin = refs[:n_in], refs[n_in:n_in + nc]
        outs, cout = refs[n_in + nc:n_in + nc + n_out], refs[n_in + nc + n_out:n_in + 2 * nc + n_out]
        rest = refs[n_in + 2 * nc + n_out:]
        scratch, csem = rest[:len(rest) - n_sem], rest[len(rest) - n_sem:]
        start, forward, finish = comm["phases"](cin, cout, *csem)
        step = pl.program_id(0)
        for axis in range(1, len(grid)):
            step = step * grid[axis] + pl.program_id(axis)
        pl.when(step == 0)(start)
        pl.when(step == (mid_step if mid_step >= 0 else total + mid_step))(forward)
        body(*ins, *outs, *scratch)
        pl.when(step == total - 1)(finish)

    return hosted


def _comm_extra(comm):
    if comm is None:
        return [], [], [], [], []
    anyspec = pl.BlockSpec(memory_space=pl.ANY)
    nc = len(comm["args"])
    return comm["args"], [anyspec] * nc, [anyspec] * nc, comm["out_shape"], comm["scratch"]


def _ssm_fwd(ps, scan_ops, bblk, cblk, dskip, wglu, bglu, *, name, t=SEQ_BLOCK, comm=None):
    l = ps.shape[0]
    assert l % t == 0
    nb = l // t
    ns = t // SCAN_SUB
    c_args, c_in, c_out, c_shape, c_scratch = _comm_extra(comm)

    def body(ps_ref, lam_ref, tab_ref, tri_ref, ex_ref, b_ref, c_ref, d_ref, w_ref, bg_ref, ys_ref, chk_ref, xs_ref,
             st_ref):
        @pl.when(pl.program_id(0) == 0)
        def _():
            st_ref[...] = jnp.zeros_like(st_ref)

        chk_ref[...] = jnp.broadcast_to(st_ref[...], chk_ref.shape)
        u = ps_ref[:, :SSM_W].astype(F32)
        z = ps_ref[:, SSM_W:].astype(F32)
        xr, xi, er, ei = _ssm_states(u, st_ref[:, :SSM_N], st_ref[:, SSM_N:], lam_ref, tab_ref, tri_ref, ex_ref,
                                     b_ref[0], b_ref[1], t)
        st_ref[:, :SSM_N] = er
        st_ref[:, SSM_N:] = ei
        xr, xi = xr.astype(BF16), xi.astype(BF16)
        xs_ref[:, :SSM_N] = xr
        xs_ref[:, SSM_N:] = xi
        _, _, _, y3 = _ssm_head(u, z, xr, xi, c_ref[0], c_ref[1], d_ref[...], w_ref[...], bg_ref[...])
        sz, _ = _silu_and_grad(z)
        ys_ref[...] = (y3 * sz).astype(BF16)

    full = lambda shape: pl.BlockSpec(shape, lambda i: (0,) * len(shape))
    return pl.pallas_call(
        _with_comm(body, comm, 10, 3, nb, nb - 1), grid=(nb,),
        in_specs=[pl.BlockSpec((t, 2 * SSM_W), lambda i: (i, 0)), full((8, SSM_N)), full((12, SCAN_SUB, SSM_N)),
                  full((2, t + ns, t)), full((t, 2 * ns)), full((2, SSM_HALVES, SSM_HW, SSM_HN)),
                  full((2, SSM_HALVES, SSM_HN, SSM_HW)), full((1, SSM_W)), full((SSM_W, SSM_W)), full((1, SSM_W))] + c_in,
        out_specs=[pl.BlockSpec((t, SSM_W), lambda i: (i, 0)), pl.BlockSpec((8, 2 * SSM_N), lambda i: (i, 0)),
                   pl.BlockSpec((t, 2 * SSM_N), lambda i: (i, 0))] + c_out,
        out_shape=[jax.ShapeDtypeStruct((l, SSM_W), BF16), jax.ShapeDtypeStruct((nb * 8, 2 * SSM_N), F32),
                   jax.ShapeDtypeStruct((l, 2 * SSM_N), BF16)] + c_shape,
        scratch_shapes=[pltpu.VMEM((1, 2 * SSM_N), F32)] + c_scratch,
        compiler_params=_params("arbitrary"), name=name)(ps, *scan_ops, bblk, cblk, dskip, wglu, bglu, *c_args)


def _ssm_bwd(ps, dys, chk, states, scan_ops, bblk, cblk, dskip, wglu, bglu, *, name, t=SEQ_BLOCK, comm=None):
    l = ps.shape[0]
    assert l % t == 0
    nb = l // t
    ns = t // SCAN_SUB
    c_args, c_in, c_out, c_shape, c_scratch = _comm_extra(comm)

    def body(ps_ref, dys_ref, chk_ref, xs_ref, lam_ref, tab_ref, tri_ref, ex_ref, b_ref, c_ref, d_ref, w_ref, bg_ref,
             dps_ref, db_ref, dc_ref, dw_acc, sums_acc, gc_ref, db_acc, dc_acc):
        n = pl.program_id(0)

        @pl.when(n == 0)
        def _():
            gc_ref[...] = jnp.zeros_like(gc_ref)
            db_acc[...] = jnp.zeros_like(db_acc)
            dc_acc[...] = jnp.zeros_like(dc_acc)
            dw_acc[...] = jnp.zeros_like(dw_acc)
            sums_acc[...] = jnp.zeros_like(sums_acc)

        row = lax.broadcasted_iota(jnp.int32, (t, 1), 0)
        u = ps_ref[:, :SSM_W].astype(F32)
        z = ps_ref[:, SSM_W:].astype(F32)
        s0r, s0i = chk_ref[0:1, :SSM_N], chk_ref[0:1, SSM_N:]
        xr, xi = xs_ref[:, :SSM_N], xs_ref[:, SSM_N:]
        dskip = d_ref[...]
        y2, dgelu, gate, y3 = _ssm_head(u, z, xr, xi, c_ref[0], c_ref[1], dskip, w_ref[...], bg_ref[...])
        sz, dsz = _silu_and_grad(z)
        dys_v = dys_ref[...]
        dps_ref[:, SSM_W:] = (dys_v * y3 * dsz).astype(BF16)
        dy3 = dys_v * sz
        da = dy3 * y2 * gate * (1.0 - gate)
        dy2 = dy3 * gate + _dot(da, w_ref[...], NT)
        dw_acc[...] += _dot(y2, da, TN)
        dy = dy2 * dgelu
        sums_acc[2:3, :SSM_W] += jnp.sum(dy * u, axis=0, keepdims=True)
        sums_acc[3:4, :SSM_W] += jnp.sum(da, axis=0, keepdims=True)
        dc_acc[0] += _bd_tn(dy, xr)
        dc_acc[1] += -_bd_tn(dy, xi)
        rev_tab = tuple(tab_ref[k] for k in range(6, 12))
        gr, gi, gcr, gci = _scan_mxu(
            _bd_nt(dy, c_ref[0]), -_bd_nt(dy, c_ref[1]), rev_tab, (lam_ref[2:3, :], -lam_ref[3:4, :]),
            (lam_ref[4:5, :], -lam_ref[5:6, :]), tri_ref[1], ex_ref[...], gc_ref[:, :SSM_N], gc_ref[:, SSM_N:], t, True)
        gc_ref[:, :SSM_N] = gcr
        gc_ref[:, SSM_N:] = gci
        db_acc[0] += _bd_tn(u, gr)
        db_acc[1] += _bd_tn(u, gi)
        du = dskip * dy + _bd_nt(gr, b_ref[0]) + _bd_nt(gi, b_ref[1])
        dps_ref[:, :SSM_W] = du.astype(BF16)
        spr = jnp.where(row == 0, s0r, pltpu.roll(xr.astype(F32), 1, 0))
        spi = jnp.where(row == 0, s0i, pltpu.roll(xi.astype(F32), 1, 0))
        sums_acc[0:1, :] += jnp.sum(gr * spr + gi * spi, axis=0, keepdims=True)
        sums_acc[1:2, :] += jnp.sum(gi * spr - gr * spi, axis=0, keepdims=True)

        @pl.when(n == nb - 1)
        def _():
            per_half = SSM_GROUPS // SSM_HALVES
            for k in range(2):
                for g in range(SSM_GROUPS):
                    h, gl = divmod(g, per_half)
                    c0, p0 = gl * SSM_GROUP, gl * SSM_STATE
                    db_ref[k, g * SSM_GROUP:(g + 1) * SSM_GROUP, :] = db_acc[k, h, c0:c0 + SSM_GROUP, p0:p0 + SSM_STATE]
                    dc_ref[k, g * SSM_GROUP:(g + 1) * SSM_GROUP, :] = dc_acc[k, h, c0:c0 + SSM_GROUP, p0:p0 + SSM_STATE]

    full = lambda shape: pl.BlockSpec(shape, lambda n: (0,) * len(shape))
    return pl.pallas_call(
        _with_comm(body, comm, 13, 5, nb, 0), grid=(nb,),
        in_specs=[pl.BlockSpec((t, 2 * SSM_W), lambda n: (nb - 1 - n, 0)),
                  pl.BlockSpec((t, SSM_W), lambda n: (nb - 1 - n, 0)),
                  pl.BlockSpec((8, 2 * SSM_N), lambda n: (nb - 1 - n, 0)),
                  pl.BlockSpec((t, 2 * SSM_N), lambda n: (nb - 1 - n, 0)),
                  full((8, SSM_N)), full((12, SCAN_SUB, SSM_N)), full((2, t + ns, t)), full((t, 2 * ns)),
                  full((2, SSM_HALVES, SSM_HW, SSM_HN)), full((2, SSM_HALVES, SSM_HN, SSM_HW)), full((1, SSM_W)),
                  full((SSM_W, SSM_W)), full((1, SSM_W))] + c_in,
        out_specs=[pl.BlockSpec((t, 2 * SSM_W), lambda n: (nb - 1 - n, 0)), full((2, SSM_W, SSM_STATE)),
                   full((2, SSM_W, SSM_STATE)), full((SSM_W, SSM_W)), full((8, SSM_N))] + c_out,
        out_shape=[jax.ShapeDtypeStruct((l, 2 * SSM_W), BF16),
                   jax.ShapeDtypeStruct((2, SSM_W, SSM_STATE), F32),
                   jax.ShapeDtypeStruct((2, SSM_W, SSM_STATE), F32),
                   jax.ShapeDtypeStruct((SSM_W, SSM_W), F32),
                   jax.ShapeDtypeStruct((8, SSM_N), F32)] + c_shape,
        scratch_shapes=[pltpu.VMEM((1, 2 * SSM_N), F32), pltpu.VMEM((2, SSM_HALVES, SSM_HW, SSM_HN), F32),
                        pltpu.VMEM((2, SSM_HALVES, SSM_HW, SSM_HN), F32)] + c_scratch,
        compiler_params=_params("arbitrary"), name=name)(ps, dys, chk, states, *scan_ops, bblk, cblk, dskip, wglu, bglu,
                                                         *c_args)


def _pool_count(i, t):
    pos = lax.broadcasted_iota(jnp.int32, (t, POOL_W), 0) + i * t + 1
    col = lax.broadcasted_iota(jnp.int32, (t, POOL_W), 1)
    win = jnp.where(col < POOL_GW, 2, jnp.where(col < 2 * POOL_GW, 4, jnp.where(col < 3 * POOL_GW, 8, 16)))
    return 1.0 / jnp.minimum(pos, win).astype(F32), col


def _window_sums(ext, n_rows, forward):
    col = lax.broadcasted_iota(jnp.int32, ext.shape, 1)
    sh = (lambda a, d: pltpu.roll(a, d, 0)) if forward else (lambda a, d: pltpu.roll(a, n_rows - d, 0))
    a2 = ext + sh(ext, 1)
    a4 = a2 + sh(a2, 2)
    a8 = a4 + sh(a4, 4)
    a16 = a8 + sh(a8, 8)
    return jnp.where(col < POOL_GW, a2, jnp.where(col < 2 * POOL_GW, a4, jnp.where(col < 3 * POOL_GW, a8, a16)))


def _pool_mix(pooled, wp_ref):
    return jnp.concatenate([_dot(pooled[:, g * POOL_GW:(g + 1) * POOL_GW], wp_ref[g]) for g in range(4)], axis=1)


def _pool_pooled(i, cur_u, prev_u, t):
    prev = jnp.where(i > 0, prev_u, 0.0)
    ext = jnp.concatenate([prev, cur_u], axis=0)
    inv_cnt, _ = _pool_count(i, t)
    return _window_sums(ext, t + POOL_HALO, True)[POOL_HALO:, :] * inv_cnt - cur_u


def _pool_fwd(pp, wpool, pscale, *, name, t=SEQ_BLOCK):
    l = pp.shape[0]
    t = min(t, l)

    def body(cur_ref, prev_ref, wp_ref, sc_ref, yp_ref):
        i = pl.program_id(0)
        pooled = _pool_pooled(i, cur_ref[:, :POOL_W].astype(F32), prev_ref[...].astype(F32), t)
        lin = _pool_mix(pooled, wp_ref)
        sz, _ = _silu_and_grad(cur_ref[:, POOL_W:].astype(F32))
        yp_ref[...] = (lin * sc_ref[...] * sz).astype(BF16)

    hb = t // POOL_HALO
    return pl.pallas_call(
        body, grid=(l // t,),
        in_specs=[pl.BlockSpec((t, 2 * POOL_W), lambda i: (i, 0)),
                  pl.BlockSpec((POOL_HALO, POOL_W), lambda i: (jnp.maximum(i * hb - 1, 0), 0)),
                  pl.BlockSpec((4, POOL_GW, POOL_GW), lambda i: (0, 0, 0)),
                  pl.BlockSpec((1, POOL_W), lambda i: (0, 0))],
        out_specs=pl.BlockSpec((t, POOL_W), lambda i: (i, 0)),
        out_shape=jax.ShapeDtypeStruct((l, POOL_W), BF16),
        compiler_params=_params("parallel"), name=name)(pp, pp, wpool, pscale)


def _pool_bwd(pp, dyp, wpool, pscale, *, name, t=SEQ_BLOCK):
    l = pp.shape[0]
    t = min(t, l)
    nb = l // t

    def body(cur_ref, prev_ref, dyp_ref, wp_ref, sc_ref, dpp_ref, dwp_ref, sums_ref, carry_ref):
        n = pl.program_id(0)
        i = nb - 1 - n

        @pl.when(n == 0)
        def _():
            carry_ref[...] = jnp.zeros_like(carry_ref)
            dwp_ref[...] = jnp.zeros_like(dwp_ref)
            sums_ref[...] = jnp.zeros_like(sums_ref)

        cur_u = cur_ref[:, :POOL_W].astype(F32)
        pooled = _pool_pooled(i, cur_u, prev_ref[...].astype(F32), t)
        lin = _pool_mix(pooled, wp_ref)
        sz, dsz = _silu_and_grad(cur_ref[:, POOL_W:].astype(F32))
        dyp_v = dyp_ref[...]
        scale = sc_ref[...]
        dpp_ref[:, POOL_W:] = (dyp_v * lin * scale * dsz).astype(BF16)
        dpre = dyp_v * sz
        sums_ref[0:1, :] += jnp.sum(dpre * lin, axis=0, keepdims=True)
        dlin = dpre * scale
        dpooled = []
        for g in range(4):
            dl = dlin[:, g * POOL_GW:(g + 1) * POOL_GW]
            dwp_ref[g] += _dot(pooled[:, g * POOL_GW:(g + 1) * POOL_GW], dl, TN)
            dpooled.append(_dot(dl, wp_ref[g], NT))
        dpooled = jnp.concatenate(dpooled, axis=1)
        inv_cnt, _ = _pool_count(i, t)
        dq = dpooled * inv_cnt
        ext = jnp.concatenate([dq, carry_ref[...]], axis=0)
        du = _window_sums(ext, t + POOL_HALO, False)[:t, :] - dpooled
        dpp_ref[:, :POOL_W] = du.astype(BF16)
        carry_ref[...] = dq[:POOL_HALO, :]

    hb = t // POOL_HALO
    return pl.pallas_call(
        body, grid=(nb,),
        in_specs=[pl.BlockSpec((t, 2 * POOL_W), lambda n: (nb - 1 - n, 0)),
                  pl.BlockSpec((POOL_HALO, POOL_W), lambda n: (jnp.maximum((nb - 1 - n) * hb - 1, 0), 0)),
                  pl.BlockSpec((t, POOL_W), lambda n: (nb - 1 - n, 0)),
                  pl.BlockSpec((4, POOL_GW, POOL_GW), lambda n: (0, 0, 0)),
                  pl.BlockSpec((1, POOL_W), lambda n: (0, 0))],
        out_specs=[pl.BlockSpec((t, 2 * POOL_W), lambda n: (nb - 1 - n, 0)),
                   pl.BlockSpec((4, POOL_GW, POOL_GW), lambda n: (0, 0, 0)),
                   pl.BlockSpec((8, POOL_W), lambda n: (0, 0))],
        out_shape=[jax.ShapeDtypeStruct((l, 2 * POOL_W), BF16), jax.ShapeDtypeStruct((4, POOL_GW, POOL_GW), F32),
                   jax.ShapeDtypeStruct((8, POOL_W), F32)],
        scratch_shapes=[pltpu.VMEM((POOL_HALO, POOL_W), F32)],
        compiler_params=_params("arbitrary"), name=name)(pp, pp, dyp, wpool, pscale)


def _merge_fwd(ya, ys, yp, wa, ws, wp, pg, wout, x, gate, *, name, tm=512):
    l, d = x.shape
    tm = min(tm, l)

    def body(ya_ref, ys_ref, yp_ref, wa_ref, ws_ref, wp_ref, pg_ref, wo_ref, x_ref, g_ref,
             xn_ref, mg_ref, ba_ref, bs_ref, bp_ref, out_ref):
        acc = None
        for k, (y_ref, w_ref, b_ref) in enumerate(((ya_ref, wa_ref, ba_ref), (ys_ref, ws_ref, bs_ref),
                                                   (yp_ref, wp_ref, bp_ref))):
            br = _dot(y_ref[...], w_ref[...])
            b_ref[...] = br.astype(BF16)
            term = _sigmoid(pg_ref[:, k * d:(k + 1) * d].astype(F32)) * br
            acc = term if acc is None else acc + term
        merged = acc.astype(BF16)
        mg_ref[...] = merged
        out = _dot(merged, wo_ref[...])
        out_ref[...] = out.astype(BF16)
        xn_ref[...] = x_ref[...] + g_ref[...] * out

    rowy = pl.BlockSpec((tm, ATT_W), lambda i: (i, 0))
    wsp = pl.BlockSpec((ATT_W, d), lambda i: (0, 0))
    rowd = pl.BlockSpec((tm, d), lambda i: (i, 0))
    return pl.pallas_call(
        body, grid=(l // tm,),
        in_specs=[rowy, rowy, rowy, wsp, wsp, wsp, pl.BlockSpec((tm, 3 * d), lambda i: (i, 0)),
                  pl.BlockSpec((d, d), lambda i: (0, 0)), rowd, pl.BlockSpec((1, d), lambda i: (0, 0))],
        out_specs=[rowd] * 6,
        out_shape=[jax.ShapeDtypeStruct((l, d), F32)] + [jax.ShapeDtypeStruct((l, d), BF16)] * 5,
        compiler_params=_params("parallel"), name=name)(ya, ys, yp, wa, ws, wp, pg, wout, x, gate)


def _merge_bwd(dx, out, gate, wout, pg, brs, wbrs, ys, merged, *, name, tm=256, comm=None):
    l, d = dx.shape
    tm = min(tm, l)
    nb = l // tm
    w = ys[0].shape[1]

    def body(dx_ref, out_ref, g_ref, w_ref, pg_ref, ba_ref, bs_ref, bp_ref, wa_ref, ws_ref, wp_ref,
             ya_ref, ys_ref, yp_ref, mg_ref,
             dya_ref, dys_ref, dyp_ref, dpg_ref, sums_ref, dwa_ref, dws_ref, dwp_ref, dwo_ref, acc_br, acc_out):
        i = pl.program_id(0)

        @pl.when(i == 0)
        def _():
            sums_ref[...] = jnp.zeros_like(sums_ref)
            acc_br[...] = jnp.zeros_like(acc_br)
            acc_out[...] = jnp.zeros_like(acc_out)

        dxv = dx_ref[...]
        sums_ref[0:1, :] += jnp.sum(dxv * out_ref[...].astype(F32), axis=0, keepdims=True)
        dmo = (dxv * g_ref[...]).astype(BF16)
        acc_out[...] += _dot(mg_ref[...], dmo, TN)
        dmerged = _dot(dmo, w_ref[...], NT)
        branches = ((ba_ref, wa_ref, ya_ref, dya_ref), (bs_ref, ws_ref, ys_ref, dys_ref), (bp_ref, wp_ref, yp_ref, dyp_ref))
        for k, (b_ref, wk_ref, y_ref, dy_ref) in enumerate(branches):
            gk = _sigmoid(pg_ref[:, k * d:(k + 1) * d].astype(F32))
            dbr = (dmerged * gk).astype(BF16)
            dpg_ref[:, k * d:(k + 1) * d] = (dmerged * b_ref[...].astype(F32) * gk * (1.0 - gk)).astype(BF16)
            dy_ref[...] = _dot(dbr, wk_ref[...], NT)
            acc_br[k] += _dot(y_ref[...], dbr, TN)

        @pl.when(i == nb - 1)
        def _():
            for k, dw_ref in enumerate((dwa_ref, dws_ref, dwp_ref)):
                dw_ref[...] = acc_br[k].astype(BF16)
            dwo_ref[...] = acc_out[...].astype(BF16)

    row = pl.BlockSpec((tm, d), lambda i: (i, 0))
    half = pl.BlockSpec((tm, w), lambda i: (i, 0))
    wide = pl.BlockSpec((tm, 3 * d), lambda i: (i, 0))
    const = lambda shape: pl.BlockSpec(shape, lambda i: (0,) * len(shape))
    c_args, c_in, c_out, c_shape, c_scratch = _comm_extra(comm)
    res = pl.pallas_call(
        _with_comm(body, comm, 15, 9, nb, 0), grid=(nb,),
        in_specs=[row, row, const((1, d)), const((d, d)), wide, row, row, row, const((w, d)), const((w, d)), const((w, d)),
                  half, half, half, row] + c_in,
        out_specs=[half, half, half, wide, const((8, d)), const((w, d)), const((w, d)), const((w, d)), const((d, d))] + c_out,
        out_shape=[jax.ShapeDtypeStruct((l, w), F32)] * 3 + [jax.ShapeDtypeStruct((l, 3 * d), BF16),
                                                             jax.ShapeDtypeStruct((8, d), F32)]
                  + [jax.ShapeDtypeStruct((w, d), BF16)] * 3 + [jax.ShapeDtypeStruct((d, d), BF16)] + c_shape,
        scratch_shapes=[pltpu.VMEM((3, w, d), F32), pltpu.VMEM((d, d), F32)] + c_scratch,
        compiler_params=_params("arbitrary"), name=name)(dx, out, gate, wout, pg, *brs, *wbrs, *ys, merged, *c_args)
    return list(res[:9]), list(res[9:])


def _adamw(w, gs, m, v, *, name, tr=256):
    r, c = w.shape
    ns = len(gs)
    p, rs, _ = gs[0].shape
    assert rs * ns == r
    tr = min(tr, rs)
    assert rs % tr == 0
    nr = rs // tr
    c1 = 1.0 / (1.0 - ADAM_B1 ** ADAM_STEP)
    c2 = 1.0 / (1.0 - ADAM_B2 ** ADAM_STEP)

    def body(*refs):
        w_ref, g_refs, (m_ref, v_ref, go_ref, d_ref, mo_ref, vo_ref) = refs[0], refs[1:1 + ns], refs[1 + ns:]
        slab = pl.program_id(0)
        gv = None
        for k, g_ref in enumerate(g_refs):
            gk = g_ref[0].astype(F32)
            for j in range(1, p):
                gk = gk + g_ref[j].astype(F32)
            gv = gk if gv is None else jnp.where(slab == k, gk, gv)
        go_ref[...] = gv
        mn = ADAM_B1 * m_ref[...] + (1.0 - ADAM_B1) * gv
        vn = ADAM_B2 * v_ref[...] + (1.0 - ADAM_B2) * (gv * gv)
        mo_ref[...] = mn
        vo_ref[...] = vn
        d_ref[...] = -ADAM_LR * ((mn * c1) / (jnp.sqrt(vn * c2) + ADAM_EPS) + ADAM_WD * w_ref[...])

    row = pl.BlockSpec((tr, c), lambda s, i: (s * nr + i, 0))
    g_specs = [pl.BlockSpec((p, tr, c), lambda s, i, k=k: (0, jnp.where(s == k, i, 0), 0)) for k in range(ns)]
    return pl.pallas_call(
        body, grid=(ns, nr),
        in_specs=[row] + g_specs + [row, row],
        out_specs=[row] * 4,
        out_shape=[jax.ShapeDtypeStruct((r, c), F32)] * 4,
        compiler_params=_params("arbitrary", "arbitrary"), name=name)(w, *gs, m, v)


def _exchange(arrs, *, scatter, name):
    n = len(arrs)
    out_shape = [jax.ShapeDtypeStruct(a.shape if scatter else (N_DEV,) + a.shape, a.dtype) for a in arrs]

    def body(*refs):
        ins, outs = refs[:n], refs[n:2 * n]
        send_sems, recv_sems, loc_sems = refs[2 * n:]
        me = 4 * lax.axis_index("x") + 2 * lax.axis_index("y") + lax.axis_index("c")
        local = []
        for k in range(n):
            src = ins[k].at[me] if scatter else ins[k]
            cp = pltpu.make_async_copy(src, outs[k].at[me], loc_sems.at[k])
            cp.start()
            local.append(cp)
        remote = []
        for r in range(1, N_DEV):
            peer = me ^ r
            for k in range(n):
                src = ins[k].at[peer] if scatter else ins[k]
                cp = pltpu.make_async_remote_copy(
                    src_ref=src, dst_ref=outs[k].at[me], send_sem=send_sems.at[k, r - 1], recv_sem=recv_sems.at[k, r - 1],
                    device_id=(peer // 4, (peer // 2) % 2, peer % 2), device_id_type=pl.DeviceIdType.MESH)
                cp.start()
                remote.append(cp)
        for cp in remote:
            cp.wait()
        for cp in local:
            cp.wait()

    anyspec = pl.BlockSpec(memory_space=pl.ANY)
    return pl.pallas_call(
        body, in_specs=[anyspec] * n, out_specs=[anyspec] * n, out_shape=out_shape,
        scratch_shapes=[pltpu.SemaphoreType.DMA((n, N_DEV - 1)), pltpu.SemaphoreType.DMA((n, N_DEV - 1)),
                        pltpu.SemaphoreType.DMA((n,))],
        name=name)(*arrs)


def _mesh_place():
    x, y, c = lax.axis_index("x"), lax.axis_index("y"), lax.axis_index("c")
    other_chips = [(1 - x, y), (x, 1 - y), (1 - x, 1 - y)]
    return x, y, c, other_chips


def _gather_two_level(arrs, *, name):
    n = len(arrs)
    plan = _gather_plan(arrs)

    def body(*refs):
        start, forward, finish = plan["phases"](refs[:n], refs[n:2 * n], *refs[2 * n:])
        start()
        forward()
        finish()

    anyspec = pl.BlockSpec(memory_space=pl.ANY)
    return pl.pallas_call(
        body, in_specs=[anyspec] * n, out_specs=[anyspec] * n, out_shape=plan["out_shape"],
        scratch_shapes=plan["scratch"], name=name)(*arrs)


def _gather_plan(arrs):
    n = len(arrs)

    def phases(ins, outs, send_sems, recv_sems, loc_sems):
        x, y, c, chips = _mesh_place()
        me = 4 * x + 2 * y + c
        slot = lambda px, py, pc: 4 * px + 2 * py + pc

        def copy(k, j, src, block, to):
            return pltpu.make_async_remote_copy(
                src_ref=src, dst_ref=outs[k].at[block], send_sem=send_sems.at[k, j], recv_sem=recv_sems.at[k, j],
                device_id=to, device_id_type=pl.DeviceIdType.MESH)

        local = [pltpu.make_async_copy(ins[k], outs[k].at[me], loc_sems.at[k]) for k in range(n)]
        first = []
        for k in range(n):
            first.append(copy(k, 0, ins[k], me, (x, y, 1 - c)))
            for j, chip in enumerate(chips):
                first.append(copy(k, 1 + j, ins[k], me, (*chip, c)))
        passed = [copy(k, 4 + j, outs[k].at[slot(*chip, c)], slot(*chip, c), (x, y, 1 - c))
                  for j, chip in enumerate(chips) for k in range(n)]

        def start():
            for cp in local + first:
                cp.start()

        def forward():
            for j, chip in enumerate(chips):
                for k in range(n):
                    copy(k, 1 + j, ins[k], slot(*chip, c), (x, y, c)).wait_recv()
                    passed[j * n + k].start()

        def finish():
            for k in range(n):
                copy(k, 0, ins[k], slot(x, y, 1 - c), (x, y, c)).wait_recv()
                for j, chip in enumerate(chips):
                    copy(k, 4 + j, ins[k], slot(*chip, 1 - c), (x, y, c)).wait_recv()
            for cp in first + passed:
                cp.wait_send()
            for cp in local:
                cp.wait()

        return start, forward, finish

    return dict(
        args=list(arrs), out_shape=[jax.ShapeDtypeStruct((N_DEV,) + a.shape, a.dtype) for a in arrs],
        scratch=[pltpu.SemaphoreType.DMA((n, 7)), pltpu.SemaphoreType.DMA((n, 7)), pltpu.SemaphoreType.DMA((n,))],
        phases=phases)


def _allreduce_small(small, extra, *, name):
    r, lanes = small.shape
    assert r % 16 == 0
    h = r // 2
    e = extra.shape[0]

    def body(s_ref, x_ref, out_ref, xall_ref, sib_ref, parts_ref, send_sems, recv_sems):
        x, y, c, chips = _mesh_place()
        me = 4 * x + 2 * y + c
        my_chip = 2 * x + y
        sibling = (x, y, 1 - c)
        mine = pl.ds(pl.multiple_of(c * h, 8), h)
        theirs = pl.ds(pl.multiple_of((1 - c) * h, 8), h)

        def remote(j, src, dst, to):
            return pltpu.make_async_remote_copy(src_ref=src, dst_ref=dst, send_sem=send_sems.at[j],
                                                recv_sem=recv_sems.at[j], device_id=to, device_id_type=pl.DeviceIdType.MESH)

        to_sibling = remote(0, s_ref.at[theirs], sib_ref, sibling)
        to_sibling.start()
        xall_ref[me] = x_ref[...]
        extras = []
        for rr in range(1, N_DEV):
            peer = me ^ rr
            cp = remote(4 + rr, x_ref, xall_ref.at[me], (peer // 4, (peer // 2) % 2, peer % 2))
            cp.start()
            extras.append(cp)
        to_sibling.wait_recv()
        parts_ref[my_chip] = s_ref[mine] + sib_ref[...]
        to_chips = [remote(1 + j, parts_ref.at[my_chip], parts_ref.at[my_chip], (px, py, c))
                    for j, (px, py) in enumerate(chips)]
        for cp in to_chips:
            cp.start()
        for cp in to_chips:
            cp.wait_recv()
        out_ref[mine] = (parts_ref[0] + parts_ref[1]) + (parts_ref[2] + parts_ref[3])
        done = remote(4, out_ref.at[mine], out_ref.at[mine], sibling)
        done.start()
        remote(4, out_ref.at[theirs], out_ref.at[theirs], sibling).wait_recv()
        for cp in extras:
            cp.wait()
        to_sibling.wait_send()
        for cp in to_chips:
            cp.wait_send()
        done.wait_send()

    vmem = pl.BlockSpec(memory_space=pltpu.VMEM)
    return pl.pallas_call(
        body, in_specs=[vmem, vmem], out_specs=[vmem, vmem],
        out_shape=[jax.ShapeDtypeStruct((r, lanes), F32), jax.ShapeDtypeStruct((N_DEV, e, lanes), F32)],
        scratch_shapes=[pltpu.VMEM((h, lanes), F32), pltpu.VMEM((4, h, lanes), F32),
                        pltpu.SemaphoreType.DMA((12,)), pltpu.SemaphoreType.DMA((12,))],
        compiler_params=pltpu.CompilerParams(vmem_limit_bytes=VMEM_LIMIT), name=name)(small, extra)


def _sibling_swap(arrs, *, name):
    n = len(arrs)
    plan = _sibling_swap_plan(arrs)

    def body(*refs):
        start, _, finish = plan["phases"](refs[:n], refs[n:2 * n], *refs[2 * n:])
        start()
        finish()

    anyspec = pl.BlockSpec(memory_space=pl.ANY)
    return pl.pallas_call(
        body, in_specs=[anyspec] * n, out_specs=[anyspec] * n, out_shape=plan["out_shape"],
        scratch_shapes=plan["scratch"], name=name)(*arrs)


def _sibling_swap_plan(arrs):
    n = len(arrs)

    def phases(ins, outs, send_sems, recv_sems):
        x, y, c, _ = _mesh_place()
        copies = [pltpu.make_async_remote_copy(
            src_ref=ins[k].at[1 - c], dst_ref=outs[k], send_sem=send_sems.at[k], recv_sem=recv_sems.at[k],
            device_id=(x, y, 1 - c), device_id_type=pl.DeviceIdType.MESH) for k in range(n)]

        def start():
            for cp in copies:
                cp.start()

        def finish():
            for cp in copies:
                cp.wait()

        return start, (lambda: None), finish

    return dict(args=list(arrs), out_shape=[jax.ShapeDtypeStruct(a.shape[1:], a.dtype) for a in arrs],
                scratch=[pltpu.SemaphoreType.DMA((n,)), pltpu.SemaphoreType.DMA((n,))], phases=phases)


def _pair_add(mine, theirs, core, *, name, tr=256):
    _, r, c = mine.shape
    tr = min(tr, r)
    assert r % tr == 0

    def body(core_ref, m_ref, t_ref, o_ref):
        o_ref[...] = (m_ref[0].astype(F32) + t_ref[...].astype(F32)).astype(BF16)

    return pl.pallas_call(
        body,
        grid_spec=pltpu.PrefetchScalarGridSpec(
            num_scalar_prefetch=1, grid=(r // tr,),
            in_specs=[pl.BlockSpec((1, tr, c), lambda i, core_ref: (core_ref[0], i, 0)),
                      pl.BlockSpec((tr, c), lambda i, core_ref: (i, 0))],
            out_specs=pl.BlockSpec((tr, c), lambda i, core_ref: (i, 0))),
        out_shape=jax.ShapeDtypeStruct((r, c), BF16),
        compiler_params=_params("parallel"), name=name)(core, mine, theirs)


def _pair_add_small(mines, theirs, core, *, name):
    n = len(mines)

    def body(core_ref, *refs):
        for m_ref, t_ref, o_ref in zip(refs[:n], refs[n:2 * n], refs[2 * n:]):
            o_ref[...] = (m_ref[0].astype(F32) + t_ref[...].astype(F32)).astype(BF16)

    whole = lambda a: pl.BlockSpec(a.shape, lambda i, core_ref: (0,) * a.ndim)
    return pl.pallas_call(
        body,
        grid_spec=pltpu.PrefetchScalarGridSpec(
            num_scalar_prefetch=1, grid=(1,),
            in_specs=[pl.BlockSpec((1,) + m.shape[1:], lambda i, core_ref: (core_ref[0], 0, 0)) for m in mines]
                     + [whole(t) for t in theirs],
            out_specs=[whole(t) for t in theirs]),
        out_shape=[jax.ShapeDtypeStruct(t.shape, BF16) for t in theirs],
        compiler_params=_params("arbitrary"), name=name)(core, *mines, *theirs)


def _chip_scatter(arrs, *, name):
    n = len(arrs)
    plan = _chip_scatter_plan(arrs)

    def body(*refs):
        start, _, finish = plan["phases"](refs[:n], refs[n:2 * n], *refs[2 * n:])
        start()
        finish()

    anyspec = pl.BlockSpec(memory_space=pl.ANY)
    return pl.pallas_call(
        body, in_specs=[anyspec] * n, out_specs=[anyspec] * n, out_shape=plan["out_shape"],
        scratch_shapes=plan["scratch"], name=name)(*arrs)


def _chip_scatter_plan(arrs):
    n = len(arrs)

    def phases(ins, outs, send_sems, recv_sems, loc_sems):
        x, y, c, chips = _mesh_place()
        mine = 2 * x + y
        local = [pltpu.make_async_copy(ins[k].at[mine], outs[k].at[mine], loc_sems.at[k]) for k in range(n)]
        remote = [pltpu.make_async_remote_copy(
            src_ref=ins[k].at[2 * px + py], dst_ref=outs[k].at[mine], send_sem=send_sems.at[k, j],
            recv_sem=recv_sems.at[k, j], device_id=(px, py, c), device_id_type=pl.DeviceIdType.MESH)
            for j, (px, py) in enumerate(chips) for k in range(n)]

        def start():
            for cp in local + remote:
                cp.start()

        def finish():
            for cp in remote:
                cp.wait()
            for cp in local:
                cp.wait()

        return start, (lambda: None), finish

    return dict(
        args=list(arrs), out_shape=[jax.ShapeDtypeStruct(a.shape, a.dtype) for a in arrs],
        scratch=[pltpu.SemaphoreType.DMA((n, 3)), pltpu.SemaphoreType.DMA((n, 3)), pltpu.SemaphoreType.DMA((n,))],
        phases=phases)


def _ssm_discretize(a_re, a_im, log_dt, b_re, b_im):
    dt = jnp.exp(log_dt)[:, None]
    mag = jnp.exp(a_re * dt)
    lr = mag * jnp.cos(a_im * dt)
    li = mag * jnp.sin(a_im * dt)
    den = a_re * a_re + a_im * a_im
    cr = ((lr - 1.0) * a_re + li * a_im) / den
    ci = (li * a_re - (lr - 1.0) * a_im) / den
    bbr = cr[..., None] * b_re - ci[..., None] * b_im
    bbi = cr[..., None] * b_im + ci[..., None] * b_re
    return lr, li, bbr, bbi


def _ssm_dense(lr, li, bbr, bbi, c_re, c_im, *, name):
    import numpy as np
    scan_ops = _scan_tables(lr.reshape(-1), li.reshape(-1)) + _scan_consts(SEQ_BLOCK)
    per_half = SSM_GROUPS // SSM_HALVES
    bt = jnp.stack([b.transpose(0, 2, 1).reshape(SSM_W, SSM_STATE) for b in (bbr, bbi)])
    ct = jnp.stack([c.transpose(0, 2, 1).reshape(SSM_N, SSM_GROUP) for c in (c_re, c_im)])
    rep_p = jnp.asarray(np.tile(np.eye(SSM_STATE, dtype=np.float32), (1, per_half)), BF16)
    rep_c = jnp.asarray(np.tile(np.eye(SSM_GROUP, dtype=np.float32), (1, per_half)), BF16)

    def body(bt_ref, ct_ref, rp_ref, rc_ref, b_ref, c_ref):
        def on_diagonal(shape, rows, cols):
            r = lax.broadcasted_iota(jnp.int32, shape, 0) // rows
            c = lax.broadcasted_iota(jnp.int32, shape, 1) // cols
            return r == c

        mask_b = on_diagonal((SSM_HW, SSM_HN), SSM_GROUP, SSM_STATE)
        mask_c = on_diagonal((SSM_HN, SSM_HW), SSM_STATE, SSM_GROUP)
        for k in range(2):
            for h in range(SSM_HALVES):
                b_rows = bt_ref[k, h * SSM_HW:(h + 1) * SSM_HW, :]
                b_ref[k, h] = jnp.where(mask_b, _dot(b_rows, rp_ref[...]), 0.0).astype(BF16)
                c_rows = ct_ref[k, h * SSM_HN:(h + 1) * SSM_HN, :]
                c_ref[k, h] = jnp.where(mask_c, _dot(c_rows, rc_ref[...]), 0.0).astype(BF16)

    vmem = pl.BlockSpec(memory_space=pltpu.VMEM)
    bblk, cblk = pl.pallas_call(
        body, in_specs=[vmem] * 4, out_specs=[vmem] * 2,
        out_shape=[jax.ShapeDtypeStruct((2, SSM_HALVES, SSM_HW, SSM_HN), BF16),
                   jax.ShapeDtypeStruct((2, SSM_HALVES, SSM_HN, SSM_HW), BF16)],
        compiler_params=pltpu.CompilerParams(vmem_limit_bytes=VMEM_LIMIT), name=name)(bt, ct, rep_p, rep_c)
    return scan_ops, bblk, cblk


def _ssm_extract(db, dc, sums):
    db = db.reshape(2, SSM_GROUPS, SSM_GROUP, SSM_STATE).transpose(0, 1, 3, 2)
    dc = dc.reshape(2, SSM_GROUPS, SSM_GROUP, SSM_STATE)
    dlr = sums[0].reshape(SSM_GROUPS, SSM_STATE)
    dli = sums[1].reshape(SSM_GROUPS, SSM_STATE)
    return dlr, dli, db[0], db[1], dc[0], dc[1]


def _in_groups():
    names = ("q", "k", "v", "u_ssm", "u_pool", "z_att", "z_ssm", "z_pool", "gates")
    sizes = (ATT_W, KV_W, KV_W, SSM_W, POOL_W, ATT_W, SSM_W, POOL_W, 3 * D_MODEL)
    r, lo = {}, 0
    for nm, s in zip(names, sizes):
        r[nm] = (lo, lo + s)
        lo += s
    kv = (r["k"][0], r["v"][1])
    return ((r["q"], r["z_att"], kv), (r["u_ssm"], r["z_ssm"]), (r["u_pool"], r["z_pool"]), (r["gates"],))


IN_GROUPS = _in_groups()


def _layer_fwd(x, lw, li, late=None, comm_attn=None, comm_ssm=None):
    tag = f"l{li}"
    h, (pa, ps, pp, pg), arrived = _ln_proj(x, lw["norm_g"], lw["shift"], lw["scale"], lw["w_in"], IN_GROUPS,
                                            name=f"ln_proj_{tag}", comm=None if late is None else late[0])
    if late is not None:
        lw = {**lw, **late[1](arrived)}
    ya, from_attn = _attn_fwd(pa, lw["sinks"], name=f"attn_fwd_{tag}", comm=comm_attn)
    ys, chk, states, *from_ssm = _ssm_fwd(ps, lw["lam"], lw["bblk"], lw["cblk"], lw["ssm_d"], lw["w_glu"], lw["b_glu"],
                                          name=f"ssm_fwd_{tag}", comm=comm_ssm)
    yp = _pool_fwd(pp, lw["w_pool"], lw["pool_scale"], name=f"pool_fwd_{tag}")
    x_new, merged, ba, bs, bp, out = _merge_fwd(ya, ys, yp, lw["w_br_att"], lw["w_br_ssm"], lw["w_br_pool"], pg,
                                                lw["w_out"], x, lw["gate"], name=f"merge_fwd_{tag}")
    saved = dict(x=x, h=h, pa=pa, ps=ps, pp=pp, pg=pg, ya=ya, ys=ys, yp=yp, chk=chk, states=states, merged=merged,
                 ba=ba, bs=bs, bp=bp, out=out)
    return x_new, saved, lw, list(from_attn), list(from_ssm)


def _layer_bwd(dx, lw, sv, li, later=None, own=None):
    tag = f"l{li}"
    g = {}
    merge_out, swapped = _merge_bwd(
        dx, sv["out"], lw["gate"], lw["w_out"], sv["pg"], (sv["ba"], sv["bs"], sv["bp"]),
        (lw["w_br_att"], lw["w_br_ssm"], lw["w_br_pool"]), (sv["ya"], sv["ys"], sv["yp"]), sv["merged"],
        name=f"merge_bwd_{tag}", comm=None if later is None else later[0])
    dya, dys, dyp, dpg, gate_sums, g["w_br_att"], g["w_br_ssm"], g["w_br_pool"], g["w_out"] = merge_out
    dpa, dsink = _attn_bwd(sv["pa"], lw["sinks"], dya, name=f"attn_bwd_{tag}")
    dps, db_dense, dc_dense, dwglu, ssm_sums, *exchanged = _ssm_bwd(
        sv["ps"], dys, sv["chk"], sv["states"], lw["lam"], lw["bblk"], lw["cblk"], lw["ssm_d"], lw["w_glu"], lw["b_glu"],
        name=f"ssm_bwd_{tag}", comm=None if later is None else later[1](swapped))
    g["w_glu"] = dwglu.astype(BF16)
    dpp, dwpool, pool_sums = _pool_bwd(sv["pp"], dyp, lw["w_pool"], lw["pool_scale"], name=f"pool_bwd_{tag}")
    h = sv["h"]
    dproj = (dpa, dps, dpp, dpg)
    g["w_in"], from_late = _mm_tn_grouped(h, dproj, IN_GROUPS, name=f"dw_in_{tag}",
                                          comm=None if own is None else own({k: g[k] for k in LATE_WEIGHTS}))
    dx_in, ln_sums, from_w_in = _ln_proj_bwd(dproj, lw["w_in"], IN_GROUPS, sv["x"], dx, lw["norm_g"], lw["scale"],
                                             name=f"ln_proj_bwd_{tag}",
                                             comm=None if own is None else own({"w_in": g["w_in"]}))
    g["dmod"] = jnp.concatenate([ln_sums[0], ln_sums[1], gate_sums[0]])
    g["norm_g"] = ln_sums[2]
    g["attn_sinks"] = dsink[:, 0]
    g["ssm_raw"] = _ssm_extract(db_dense, dc_dense, ssm_sums)
    g["ssm_d"] = ssm_sums[2, :SSM_W]
    g["b_glu"] = ssm_sums[3, :SSM_W]
    g["w_pool"] = dwpool
    g["pool_scale"] = pool_sums[0]
    return dx_in, g, exchanged, list(from_w_in) + list(from_late)


BIG_WEIGHTS = ("w_in", "w_glu", "w_br_att", "w_br_ssm", "w_br_pool", "w_out")
ROW_SHARDED = ("w_glu", "w_out")


LATE_WEIGHTS = BIG_WEIGHTS[1:]


def _full_weights(keys, gathered):
    full = {}
    for k, g in zip(keys, gathered):
        if k in ROW_SHARDED:
            full[k] = g.reshape(N_DEV * g.shape[1], g.shape[2])
        else:
            full[k] = g.transpose(1, 0, 2).reshape(g.shape[1], N_DEV * g.shape[2])
    return full


def _by_destination(keys, grads):
    out = []
    for k in keys:
        g = grads[k]
        if k in ROW_SHARDED:
            out.append(g.reshape(4, 2, g.shape[0] // N_DEV, g.shape[1]).transpose(1, 0, 2, 3))
        else:
            out.append(g.reshape(g.shape[0], 4, 2, g.shape[1] // N_DEV).transpose(2, 1, 0, 3))
    return out


def _prepare_layer(li, mod, norm_g, w_in_full, attn_sinks, disc, ssm_c_re, ssm_c_im, ssm_d, b_glu, w_pool, pool_scale):
    d = D_MODEL
    lr, li_, bbr, bbi = disc
    lam, bblk, cblk = _ssm_dense(lr[li], li_[li], bbr[li], bbi[li], ssm_c_re[li], ssm_c_im[li], name=f"ssm_dense_l{li}")
    return dict(
        norm_g=norm_g[li][None, :], shift=mod[li, :d][None, :], scale=mod[li, d:2 * d][None, :],
        gate=mod[li, 2 * d:][None, :], w_in=w_in_full,
        sinks=attn_sinks[li], lam=lam, bblk=bblk, cblk=cblk, ssm_d=ssm_d[li][None, :],
        b_glu=b_glu[li][None, :], w_pool=w_pool[li].astype(BF16), pool_scale=pool_scale[li][None, :])


SMALL_ROWS = 64
SMALL_ORDER = ("norm_g", "attn_sinks", "ssm_d", "b_glu", "w_pool", "pool_scale", "dmod")


def _pack_small(loss, dfinal_g, layer_grads):
    parts = [jnp.broadcast_to(loss.reshape(1), (128,)), dfinal_g]
    for g in layer_grads:
        for k in SMALL_ORDER:
            v = g[k].reshape(-1)
            if v.shape[0] % 128:
                v = jnp.pad(v, (0, 128 - v.shape[0] % 128))
            parts.append(v)
        for v in g["ssm_raw"]:
            parts.append(v.reshape(-1))
    flat = jnp.concatenate(parts)
    return jnp.pad(flat, (0, (-flat.shape[0]) % (SMALL_ROWS * 128))).reshape(-1, 128)


def _unpack_small(flat, shapes):
    out, off = [], 0
    for s in shapes:
        n = int(math.prod(s))
        out.append(flat[off:off + n].reshape(s))
        off += n + (-n) % 128
    return out


def kernel(x, c, norm_g, w_ada, b_ada, w_in, attn_sinks, ssm_a_re, ssm_a_im, ssm_log_dt, ssm_b_re, ssm_b_im, ssm_c_re, ssm_c_im, ssm_d, w_glu, b_glu, w_pool, pool_scale, w_br_att, w_br_ssm, w_br_pool, w_out, final_g, loss_target, m_norm_g, m_w_ada, m_b_ada, m_w_in, m_attn_sinks, m_ssm_a_re, m_ssm_a_im, m_ssm_log_dt, m_ssm_b_re, m_ssm_b_im, m_ssm_c_re, m_ssm_c_im, m_ssm_d, m_w_glu, m_b_glu, m_w_pool, m_pool_scale, m_w_br_att, m_w_br_ssm, m_w_br_pool, m_w_out, m_final_g, v_norm_g, v_w_ada, v_b_ada, v_w_in, v_attn_sinks, v_ssm_a_re, v_ssm_a_im, v_ssm_log_dt, v_ssm_b_re, v_ssm_b_im, v_ssm_c_re, v_ssm_c_im, v_ssm_d, v_w_glu, v_b_glu, v_w_pool, v_pool_scale, v_w_br_att, v_w_br_ssm, v_w_br_pool, v_w_out, v_final_g):
    me = 4 * lax.axis_index("x") + 2 * lax.axis_index("y") + lax.axis_index("c")
    d = D_MODEL
    ada_w = 3 * d // N_DEV

    (c_all,) = _exchange([c.reshape(8, 128)], scatter=False, name="gather_c")
    c_act = jax.nn.silu(c_all.reshape(N_DEV, d))
    b_cols = lax.dynamic_slice(b_ada, (0, me * ada_w), (DEPTH, ada_w))
    mod_part = jnp.concatenate(
        [_mm(c_act, w_ada[li], name=f"ada_fwd_l{li}") + b_cols[li][None, :] for li in range(DEPTH)], axis=0)
    (mod_all,) = _exchange([mod_part], scatter=False, name="gather_mod")
    mod_all = mod_all.reshape(N_DEV, DEPTH, N_DEV, ada_w)
    mod_mine = lax.dynamic_index_in_dim(mod_all, me, axis=2, keepdims=False)
    mod_mine = mod_mine.transpose(1, 0, 2).reshape(DEPTH, 3 * d)

    sharded = dict(w_in=w_in, w_glu=w_glu, w_br_att=w_br_att, w_br_ssm=w_br_ssm, w_br_pool=w_br_pool, w_out=w_out)
    shards = lambda li, keys: [sharded[k][li].astype(BF16) for k in keys]
    disc, disc_vjp = jax.vjp(jax.vmap(_ssm_discretize), ssm_a_re, ssm_a_im, ssm_log_dt, ssm_b_re, ssm_b_im)
    layer = lambda li, gathered_w_in: _prepare_layer(
        li, mod_mine, norm_g, _full_weights(("w_in",), gathered_w_in)["w_in"], attn_sinks, disc, ssm_c_re, ssm_c_im,
        ssm_d, b_glu, w_pool, pool_scale)
    late_weights = lambda gathered: _full_weights(LATE_WEIGHTS, gathered)
    core = lax.axis_index("c").astype(jnp.int32).reshape(1)

    def add_pairs(keys, by_dest, from_sibling, tag):
        flat = {k: (a.reshape(2, -1, a.shape[-1]), b.reshape(-1, b.shape[-1]))
                for k, a, b in zip(keys, by_dest, from_sibling)}
        small = [k for k in keys if k != "w_in"]
        sums = {}
        if "w_in" in flat:
            sums["w_in"] = _pair_add(*flat["w_in"], core, name=f"grads_pair_add_{tag}_w_in")
        if small:
            added = _pair_add_small([flat[k][0] for k in small], [flat[k][1] for k in small], core,
                                    name=f"grads_pair_add_{tag}_late")
            sums.update(zip(small, added))
        return [sums[k].reshape(b.shape) for k, b in zip(keys, from_sibling)]

    def chip_sums_of(keys, grads_li, tag):
        by_dest = _by_destination(keys, grads_li)
        return add_pairs(keys, by_dest, _sibling_swap(by_dest, name=f"grads_sibling_swap_{tag}"), tag)

    layers, saved, grads = [None] * DEPTH, [None] * DEPTH, [None] * DEPTH
    layers[0] = layer(0, _gather_two_level(shards(0, ("w_in",)), name="gather_w_in_l0"))
    xs, saved[0], layers[0], late1, w_in1 = _layer_fwd(
        x[0], layers[0], 0, late=(_gather_plan(shards(0, LATE_WEIGHTS)), late_weights),
        comm_attn=_gather_plan(shards(1, LATE_WEIGHTS)), comm_ssm=_gather_plan(shards(1, ("w_in",))))
    layers[1] = {**layer(1, w_in1), **late_weights(late1)}
    xs, saved[1], _, _, _ = _layer_fwd(xs, layers[1], 1)
    dx, fin_sums = _final_loss(xs, final_g[None, :], loss_target[0])
    loss_part = jnp.sum(fin_sums[1])
    dx, grads[1], _, _ = _layer_bwd(dx, layers[1], saved[1], 1)
    by_dest1 = _by_destination(BIG_WEIGHTS, grads[1])
    dx, grads[0], scattered1, scattered0 = _layer_bwd(
        dx, layers[0], saved[0], 0,
        later=(_sibling_swap_plan(by_dest1),
               lambda swapped: _chip_scatter_plan(add_pairs(BIG_WEIGHTS, by_dest1, swapped, "l1"))),
        own=lambda g: _chip_scatter_plan(chip_sums_of(tuple(g), g, "l0_" + "_".join(g))))
    big = list(zip(scattered0, scattered1))
    grad_x = dx[None]

    small = _pack_small(loss_part, fin_sums[0], grads)
    dmod_rows = jnp.concatenate([grads[li]["dmod"] for li in range(DEPTH)]).reshape(-1, 128)
    small_sum, dmod_gathered = _allreduce_small(small, dmod_rows, name="allreduce_small")
    out = {}

    def adam(name, w, g_slabs, m, v):
        shp = w.shape
        r = int(math.prod(shp[:-1])) if len(shp) > 1 else 1
        w2, m2, v2 = (a.reshape(r, shp[-1]) for a in (w, m, v))
        gs = [g.reshape(g.shape[0], r // len(g_slabs), shp[-1]) for g in g_slabs]
        res = _adamw(w2, gs, m2, v2, name=f"adamw_{name}")
        out[name] = tuple(a.reshape(shp) for a in res)

    flat = small_sum.reshape(-1)
    shapes = [(128,), (d,)]
    for _ in range(DEPTH):
        shapes += [(d,), (N_HEADS,), (SSM_W,), (SSM_W,), (4, POOL_GW, POOL_GW), (POOL_W,), (3 * d,),
                   (SSM_GROUPS, SSM_STATE), (SSM_GROUPS, SSM_STATE), (SSM_GROUPS, SSM_STATE, SSM_GROUP),
                   (SSM_GROUPS, SSM_STATE, SSM_GROUP), (SSM_GROUPS, SSM_GROUP, SSM_STATE), (SSM_GROUPS, SSM_GROUP, SSM_STATE)]
    un = _unpack_small(flat, shapes)
    loss = un[0][0]
    g_final_g = un[1]
    per = 13
    gl = [un[2 + li * per: 2 + (li + 1) * per] for li in range(DEPTH)]
    st = lambda j: jnp.stack([gl[li][j] for li in range(DEPTH)])
    g_norm_g, g_sinks, g_ssm_d, g_b_glu, g_w_pool, g_pool_scale, g_b_ada = (st(j) for j in range(7))
    d_lr, d_li, d_bbr, d_bbi, g_c_re, g_c_im = (st(j) for j in range(7, 13))
    g_a_re, g_a_im, g_log_dt, g_b_re, g_b_im = disc_vjp((d_lr, d_li, d_bbr, d_bbi))

    dmod_all = lax.dynamic_slice(dmod_gathered.reshape(N_DEV, DEPTH, 3 * d), (0, 0, me * ada_w), (N_DEV, DEPTH, ada_w))
    dmod_all = dmod_all.transpose(1, 0, 2)
    g_w_ada = jnp.stack([_mm_tn(c_act, dmod_all[li], tm=d, tn=ada_w, tk=N_DEV, name=f"dw_ada_l{li}") for li in range(DEPTH)])

    adam("w_ada", w_ada, [g_w_ada[None]], m_w_ada, v_w_ada)
    adam("w_in", w_in, big[0], m_w_in, v_w_in)
    adam("w_glu", w_glu, big[1], m_w_glu, v_w_glu)
    adam("w_br_att", w_br_att, big[2], m_w_br_att, v_w_br_att)
    adam("w_br_ssm", w_br_ssm, big[3], m_w_br_ssm, v_w_br_ssm)
    adam("w_br_pool", w_br_pool, big[4], m_w_br_pool, v_w_br_pool)
    adam("w_out", w_out, big[5], m_w_out, v_w_out)

    small_names = ["norm_g", "b_ada", "attn_sinks", "ssm_a_re", "ssm_a_im", "ssm_log_dt", "ssm_b_re", "ssm_b_im",
                   "ssm_c_re", "ssm_c_im", "ssm_d", "b_glu", "w_pool", "pool_scale", "final_g"]
    small_w = [norm_g, b_ada, attn_sinks, ssm_a_re, ssm_a_im, ssm_log_dt, ssm_b_re, ssm_b_im, ssm_c_re, ssm_c_im,
               ssm_d, b_glu, w_pool, pool_scale, final_g]
    small_m = [m_norm_g, m_b_ada, m_attn_sinks, m_ssm_a_re, m_ssm_a_im, m_ssm_log_dt, m_ssm_b_re, m_ssm_b_im,
               m_ssm_c_re, m_ssm_c_im, m_ssm_d, m_b_glu, m_w_pool, m_pool_scale, m_final_g]
    small_v = [v_norm_g, v_b_ada, v_attn_sinks, v_ssm_a_re, v_ssm_a_im, v_ssm_log_dt, v_ssm_b_re, v_ssm_b_im,
               v_ssm_c_re, v_ssm_c_im, v_ssm_d, v_b_glu, v_w_pool, v_pool_scale, v_final_g]
    small_g = [g_norm_g, g_b_ada, g_sinks, g_a_re, g_a_im, g_log_dt, g_b_re, g_b_im, g_c_re, g_c_im,
               g_ssm_d, g_b_glu, g_w_pool, g_pool_scale, g_final_g]

    for nm, w, g, m, v in zip(small_names, small_w, small_g, small_m, small_v):
        adam(nm, w, [g[None]], m, v)

    order = ["norm_g", "w_ada", "b_ada", "w_in", "attn_sinks", "ssm_a_re", "ssm_a_im", "ssm_log_dt", "ssm_b_re",
             "ssm_b_im", "ssm_c_re", "ssm_c_im", "ssm_d", "w_glu", "b_glu", "w_pool", "pool_scale", "w_br_att",
             "w_br_ssm", "w_br_pool", "w_out", "final_g"]
    return (loss, grad_x, *[out[k][0] for k in order], *[out[k][1] for k in order],
            *[out[k][2] for k in order], *[out[k][3] for k in order])
```

```python
import functools
import math

import jax
import jax.numpy as jnp
from jax import lax
from jax.experimental import pallas as pl
from jax.experimental.pallas import tpu as pltpu

F32 = jnp.float32
BF16 = jnp.bfloat16

N_DEV = 8
D_MODEL = 1024
DEPTH = 2
CHUNK = 64
N_HEADS = 8
N_KV_HEADS = 2
HEAD_DIM = 64
Q_PER_KV = N_HEADS // N_KV_HEADS
WINDOW = 128
ATT_W = 512
KV_W = 128
SSM_W = 512
SSM_GROUP = 16
SSM_GROUPS = 32
SSM_STATE = 64
SSM_N = SSM_GROUPS * SSM_STATE
POOL_W = 512
POOL_WINDOWS = (2, 4, 8, 16)
POOL_GW = 128
POOL_HALO = 16
EPS = 1e-6
NEG_INF = -1e30
ADAM_LR = 0.001
ADAM_B1 = 0.9
ADAM_B2 = 0.999
ADAM_EPS = 1e-08
ADAM_WD = 0.01
ADAM_STEP = 10

SEQ_BLOCK = 256
ATT_BLOCK = 128
VMEM_LIMIT = 56 * 1024 * 1024

NN = (((1,), (0,)), ((), ()))
NT = (((1,), (1,)), ((), ()))
TN = (((0,), (0,)), ((), ()))


def _dot(a, b, dims=NN):
    return lax.dot_general(a.astype(BF16), b.astype(BF16), dims, preferred_element_type=F32)


def _params(*sem):
    return pltpu.CompilerParams(dimension_semantics=sem, vmem_limit_bytes=VMEM_LIMIT)


def _sigmoid(x):
    return 0.5 + 0.5 * jnp.tanh(0.5 * x)


def _silu_and_grad(z):
    s = _sigmoid(z)
    return z * s, s * (1.0 + z * (1.0 - s))


_GELU_K = math.sqrt(2.0 / math.pi)


def _gelu_and_grad(x):
    inner = _GELU_K * (x + 0.044715 * x * x * x)
    t = jnp.tanh(inner)
    val = 0.5 * x * (1.0 + t)
    grad = 0.5 * (1.0 + t) + 0.5 * x * (1.0 - t * t) * _GELU_K * (1.0 + 3.0 * 0.044715 * x * x)
    return val, grad


def _mm(a, b, *, nt=False, out_dtype=F32, tm=1024, tn=1024, name, comm=None):
    m, k = a.shape
    n = b.shape[0] if nt else b.shape[1]
    tm, tn = min(tm, m), min(tn, n)
    assert m % tm == 0 and n % tn == 0
    dims = NT if nt else NN
    grid = (m // tm, n // tn)
    c_args, c_in, c_out, c_shape, c_scratch = _comm_extra(comm)

    def body(a_ref, b_ref, o_ref):
        o_ref[...] = _dot(a_ref[...], b_ref[...], dims).astype(out_dtype)

    b_spec = pl.BlockSpec((tn, k), lambda i, j: (j, 0)) if nt else pl.BlockSpec((k, tn), lambda i, j: (0, j))
    res = pl.pallas_call(
        _with_comm(body, comm, 2, 1, grid, -1), grid=grid,
        in_specs=[pl.BlockSpec((tm, k), lambda i, j: (i, 0)), b_spec] + c_in,
        out_specs=[pl.BlockSpec((tm, tn), lambda i, j: (i, j))] + c_out,
        out_shape=[jax.ShapeDtypeStruct((m, n), out_dtype)] + c_shape,
        scratch_shapes=c_scratch,
        compiler_params=_params(*(("arbitrary",) * 2 if comm else ("parallel",) * 2)), name=name)(a, b, *c_args)
    return (res[0], list(res[1:])) if comm else res[0]


def _grouped_pieces(groups):
    out = []
    for ranges in groups:
        off, pieces = 0, []
        for lo, hi in ranges:
            pieces.append((off, lo, hi))
            off += hi - lo
        out.append(pieces)
    return out


def _mm_tn(a, b, *, out_dtype=F32, tm=1024, tn=1024, tk=1024, name, comm=None):
    k, m = a.shape
    n = b.shape[1]
    assert m % min(tm, m) == 0 and n % min(tn, n) == 0 and k % min(tk, k) == 0
    tm, tn, tk = min(tm, m), min(tn, n), min(tk, k)
    nk = k // tk
    grid = (m // tm, n // tn, nk)
    c_args, c_in, c_out, c_shape, c_scratch = _comm_extra(comm)

    def body(a_ref, b_ref, o_ref, acc_ref):
        kk = pl.program_id(2)

        @pl.when(kk == 0)
        def _():
            acc_ref[...] = jnp.zeros_like(acc_ref)

        acc_ref[...] += _dot(a_ref[...], b_ref[...], TN)

        @pl.when(kk == nk - 1)
        def _():
            o_ref[...] = acc_ref[...].astype(out_dtype)

    res = pl.pallas_call(
        _with_comm(body, comm, 2, 1, grid, -1), grid=grid,
        in_specs=[pl.BlockSpec((tk, tm), lambda i, j, kk: (kk, i)), pl.BlockSpec((tk, tn), lambda i, j, kk: (kk, j))] + c_in,
        out_specs=[pl.BlockSpec((tm, tn), lambda i, j, kk: (i, j))] + c_out,
        out_shape=[jax.ShapeDtypeStruct((m, n), out_dtype)] + c_shape,
        scratch_shapes=[pltpu.VMEM((tm, tn), F32)] + c_scratch,
        compiler_params=_params(*(("arbitrary",) * 3 if comm else ("parallel", "parallel", "arbitrary"))),
        name=name)(a, b, *c_args)
    return (res[0], list(res[1:])) if comm else res[0]


def _mm_tn_grouped(a, bs, groups, *, tm=512, tk=512, name, comm=None):
    k, m = a.shape
    tm, tk = min(tm, m), min(tk, k)
    assert m % tm == 0 and k % tk == 0
    nk, nb = k // tk, len(bs)
    pieces = _grouped_pieces(groups)
    n = sum(b.shape[1] for b in bs)
    grid = (m // tm, nk)
    c_args, c_in, c_out, c_shape, c_scratch = _comm_extra(comm)

    def body(a_ref, *refs):
        b_refs, o_ref, acc_refs = refs[:nb], refs[nb], refs[nb + 1:]
        kk = pl.program_id(1)
        av = a_ref[...]
        for b_ref, acc_ref, plist in zip(b_refs, acc_refs, pieces):
            @pl.when(kk == 0)
            def _():
                acc_ref[...] = jnp.zeros_like(acc_ref)

            acc_ref[...] += _dot(av, b_ref[...], TN)

            @pl.when(kk == nk - 1)
            def _():
                for off, lo, hi in plist:
                    o_ref[:, lo:hi] = acc_ref[:, off:off + hi - lo].astype(BF16)

    res = pl.pallas_call(
        _with_comm(body, comm, 1 + nb, 1, grid, -1), grid=grid,
        in_specs=[pl.BlockSpec((tk, tm), lambda i, kk: (kk, i))]
                 + [pl.BlockSpec((tk, b.shape[1]), lambda i, kk: (kk, 0)) for b in bs] + c_in,
        out_specs=[pl.BlockSpec((tm, n), lambda i, kk: (i, 0))] + c_out,
        out_shape=[jax.ShapeDtypeStruct((m, n), BF16)] + c_shape,
        scratch_shapes=[pltpu.VMEM((tm, b.shape[1]), F32) for b in bs] + c_scratch,
        compiler_params=_params("arbitrary", "arbitrary"), name=name)(a, *bs, *c_args)
    return res[0], list(res[1:])


def _ln_proj(x, g, shift, scale, w, groups, *, name, tm=512, comm=None):
    l, d = x.shape
    tm = min(tm, l)
    nb = l // tm
    pieces = _grouped_pieces(groups)
    widths = [sum(hi - lo for _, lo, hi in plist) for plist in pieces]
    nw = len(pieces)
    c_args, c_in, c_out, c_shape, c_scratch = _comm_extra(comm)

    def body(x_ref, g_ref, sh_ref, sc_ref, w_ref, h_ref, *p_refs):
        xv = x_ref[...]
        n = xv * lax.rsqrt(jnp.mean(xv * xv, axis=-1, keepdims=True) + EPS)
        h = ((n * g_ref[...]) * (1.0 + sc_ref[...]) + sh_ref[...]).astype(BF16)
        h_ref[...] = h
        for p_ref, plist in zip(p_refs, pieces):
            for off, lo, hi in plist:
                p_ref[:, off:off + hi - lo] = _dot(h, w_ref[:, lo:hi]).astype(BF16)

    vec = pl.BlockSpec((1, d), lambda i: (0, 0))
    row = lambda n: pl.BlockSpec((tm, n), lambda i: (i, 0))
    res = pl.pallas_call(
        _with_comm(body, comm, 5, 1 + nw, nb, -1), grid=(nb,),
        in_specs=[row(d), vec, vec, vec, pl.BlockSpec(w.shape, lambda i: (0, 0))] + c_in,
        out_specs=[row(d)] + [row(n) for n in widths] + c_out,
        out_shape=[jax.ShapeDtypeStruct((l, d), BF16)] + [jax.ShapeDtypeStruct((l, n), BF16) for n in widths] + c_shape,
        scratch_shapes=c_scratch,
        compiler_params=_params("arbitrary"), name=name)(x, g, shift, scale, w, *c_args)
    return res[0], list(res[1:1 + nw]), list(res[1 + nw:])


def _ln_proj_bwd(ds, w, groups, x, dres, g, scale, *, name, tm=256, comm=None):
    l, d = x.shape
    tm = min(tm, l)
    nb = l // tm
    nd = len(ds)
    pieces = _grouped_pieces(groups)
    c_args, c_in, c_out, c_shape, c_scratch = _comm_extra(comm)

    def body(*refs):
        d_refs = refs[:nd]
        w_ref, x_ref, dres_ref, g_ref, sc_ref, dx_ref, sums_ref = refs[nd:]
        dhv = None
        for d_ref, plist in zip(d_refs, pieces):
            for off, lo, hi in plist:
                term = _dot(d_ref[:, off:off + hi - lo], w_ref[:, lo:hi], NT)
                dhv = term if dhv is None else dhv + term
        xv = x_ref[...]
        rstd = lax.rsqrt(jnp.mean(xv * xv, axis=-1, keepdims=True) + EPS)
        n = xv * rstd
        gv = g_ref[...]
        dr = dhv * (1.0 + sc_ref[...])
        dn = dr * gv
        dx_ref[...] = dres_ref[...] + rstd * (dn - n * jnp.mean(dn * n, axis=-1, keepdims=True))

        @pl.when(pl.program_id(0) == 0)
        def _():
            sums_ref[...] = jnp.zeros_like(sums_ref)

        sums_ref[0:1, :] += jnp.sum(dhv, axis=0, keepdims=True)
        sums_ref[1:2, :] += jnp.sum(dhv * (n * gv), axis=0, keepdims=True)
        sums_ref[2:3, :] += jnp.sum(dr * n, axis=0, keepdims=True)

    vec = pl.BlockSpec((1, d), lambda i: (0, 0))
    row = pl.BlockSpec((tm, d), lambda i: (i, 0))
    res = pl.pallas_call(
        _with_comm(body, comm, nd + 5, 2, nb, -1), grid=(nb,),
        in_specs=[pl.BlockSpec((tm, a.shape[1]), lambda i: (i, 0)) for a in ds]
                 + [pl.BlockSpec(w.shape, lambda i: (0, 0)), row, row, vec, vec] + c_in,
        out_specs=[row, pl.BlockSpec((8, d), lambda i: (0, 0))] + c_out,
        out_shape=[jax.ShapeDtypeStruct((l, d), F32), jax.ShapeDtypeStruct((8, d), F32)] + c_shape,
        scratch_shapes=c_scratch,
        compiler_params=_params("arbitrary"), name=name)(*ds, w, x, dres, g, scale, *c_args)
    return res[0], res[1], list(res[2:])


def _final_loss(x, g, target, *, tm=512):
    l, d = x.shape

    def body(x_ref, g_ref, t_ref, dx_ref, sums_ref):
        xv = x_ref[...]
        rstd = lax.rsqrt(jnp.mean(xv * xv, axis=-1, keepdims=True) + EPS)
        n = xv * rstd
        gv = g_ref[...]
        err = n * gv - t_ref[...]
        dy = err * (1.0 / d)
        dn = dy * gv
        dx_ref[...] = rstd * (dn - n * jnp.mean(dn * n, axis=-1, keepdims=True))

        @pl.when(pl.program_id(0) == 0)
        def _():
            sums_ref[...] = jnp.zeros_like(sums_ref)

        sums_ref[0:1, :] += jnp.sum(dy * n, axis=0, keepdims=True)
        sums_ref[1:2, :] += jnp.sum(err * err, axis=0, keepdims=True) * (0.5 / d)

    vec = pl.BlockSpec((1, d), lambda i: (0, 0))
    row = pl.BlockSpec((tm, d), lambda i: (i, 0))
    dx, sums = pl.pallas_call(
        body, grid=(l // tm,),
        in_specs=[row, vec, row],
        out_specs=[row, pl.BlockSpec((8, d), lambda i: (0, 0))],
        out_shape=[jax.ShapeDtypeStruct((l, d), F32), jax.ShapeDtypeStruct((8, d), F32)],
        compiler_params=_params("arbitrary"), name="final_loss")(x, g, target)
    return dx, sums


def _attn_geometry(i, t):
    nk = t + WINDOW
    qi = lax.broadcasted_iota(jnp.int32, (t, nk), 0)
    kj = lax.broadcasted_iota(jnp.int32, (t, nk), 1)
    dist = jnp.abs(qi + WINDOW - kj).astype(F32)
    qc = jnp.right_shift(qi, 6)
    kc = jnp.right_shift(kj, 6)
    valid = (kc >= qc) & (kc <= qc + WINDOW // CHUNK) & ((i > 0) | (kj >= WINDOW))
    return dist, valid


def _attn_head(q, k_all, v_all, sink, slope, dist, valid):
    s = _dot(q, k_all, NT) * (1.0 / math.sqrt(HEAD_DIM)) - slope * dist
    s = jnp.where(valid, s, NEG_INF)
    m = jnp.maximum(jnp.max(s, axis=-1, keepdims=True), sink)
    e = jnp.exp(s - m)
    es = jnp.exp(sink - m)
    inv = 1.0 / (jnp.sum(e, axis=-1, keepdims=True) + es)
    p = e * inv
    o = _dot(p, v_all, NN)
    return p, o, es * inv


def _attn_specs(t):
    cur = pl.BlockSpec((t, ATT_W * 2 + KV_W * 2), lambda i: (i, 0))
    halo_blocks = t // WINDOW
    prev = pl.BlockSpec((WINDOW, 2 * KV_W), lambda i: (jnp.maximum(i * halo_blocks - 1, 0), (2 * ATT_W) // (2 * KV_W)))
    return cur, prev


def _attn_fwd(pa, sinks, *, name, t=ATT_BLOCK, comm=None):
    l = pa.shape[0]
    t = min(t, l)
    nb = l // t
    c_args, c_in, c_out, c_shape, c_scratch = _comm_extra(comm)

    def body(sink_ref, cur_ref, prev_ref, ya_ref):
        i = pl.program_id(0)
        dist, valid = _attn_geometry(i, t)
        for h in range(N_HEADS):
            kh = h // Q_PER_KV
            q = cur_ref[:, h * HEAD_DIM:(h + 1) * HEAD_DIM]
            z = cur_ref[:, ATT_W + h * HEAD_DIM:ATT_W + (h + 1) * HEAD_DIM].astype(F32)
            k_all = jnp.concatenate([prev_ref[:, kh * HEAD_DIM:(kh + 1) * HEAD_DIM],
                                     cur_ref[:, 2 * ATT_W + kh * HEAD_DIM:2 * ATT_W + (kh + 1) * HEAD_DIM]], axis=0)
            v_all = jnp.concatenate([prev_ref[:, KV_W + kh * HEAD_DIM:KV_W + (kh + 1) * HEAD_DIM],
                                     cur_ref[:, 2 * ATT_W + KV_W + kh * HEAD_DIM:2 * ATT_W + KV_W + (kh + 1) * HEAD_DIM]], axis=0)
            _, o, _ = _attn_head(q, k_all, v_all, sink_ref[h], 2.0 ** (-(h + 1)), dist, valid)
            sz, _ = _silu_and_grad(z)
            ya_ref[:, h * HEAD_DIM:(h + 1) * HEAD_DIM] = (o * sz).astype(BF16)

    cur, prev = _attn_specs(t)
    res = pl.pallas_call(
        _with_comm(body, comm, 3, 1, nb, nb - 1), grid=(nb,),
        in_specs=[pl.BlockSpec(memory_space=pltpu.SMEM), cur, prev] + c_in,
        out_specs=[pl.BlockSpec((t, ATT_W), lambda i: (i, 0))] + c_out,
        out_shape=[jax.ShapeDtypeStruct((l, ATT_W), BF16)] + c_shape,
        scratch_shapes=c_scratch,
        compiler_params=_params("arbitrary"), name=name)(sinks, pa, pa, *c_args)
    return res[0], res[1:]


def _attn_bwd(pa, sinks, dya, *, name, t=SEQ_BLOCK):
    l = pa.shape[0]
    t = min(t, l)
    nb = l // t
    scale = 1.0 / math.sqrt(HEAD_DIM)

    def body(sink_ref, cur_ref, prev_ref, dya_ref, dpa_ref, dsink_ref, carry_ref):
        n = pl.program_id(0)
        i = nb - 1 - n
        dist, valid = _attn_geometry(i, t)

        @pl.when(n == 0)
        def _():
            carry_ref[...] = jnp.zeros_like(carry_ref)
            dsink_ref[...] = jnp.zeros_like(dsink_ref)

        dk_acc = [jnp.zeros((HEAD_DIM, t + WINDOW), F32) for _ in range(N_KV_HEADS)]
        dv_acc = [jnp.zeros((HEAD_DIM, t + WINDOW), F32) for _ in range(N_KV_HEADS)]
        for h in range(N_HEADS):
            kh = h // Q_PER_KV
            q = cur_ref[:, h * HEAD_DIM:(h + 1) * HEAD_DIM]
            z = cur_ref[:, ATT_W + h * HEAD_DIM:ATT_W + (h + 1) * HEAD_DIM].astype(F32)
            k_all = jnp.concatenate([prev_ref[:, kh * HEAD_DIM:(kh + 1) * HEAD_DIM],
                                     cur_ref[:, 2 * ATT_W + kh * HEAD_DIM:2 * ATT_W + (kh + 1) * HEAD_DIM]], axis=0)
            v_all = jnp.concatenate([prev_ref[:, KV_W + kh * HEAD_DIM:KV_W + (kh + 1) * HEAD_DIM],
                                     cur_ref[:, 2 * ATT_W + KV_W + kh * HEAD_DIM:2 * ATT_W + KV_W + (kh + 1) * HEAD_DIM]], axis=0)
            p, o, p_sink = _attn_head(q, k_all, v_all, sink_ref[h], 2.0 ** (-(h + 1)), dist, valid)
            dy = dya_ref[:, h * HEAD_DIM:(h + 1) * HEAD_DIM]
            sz, dsz = _silu_and_grad(z)
            do = dy * sz
            dpa_ref[:, ATT_W + h * HEAD_DIM:ATT_W + (h + 1) * HEAD_DIM] = (dy * o * dsz).astype(BF16)
            dp = _dot(do, v_all, NT)
            delta = jnp.sum(p * dp, axis=-1, keepdims=True)
            ds = p * (dp - delta)
            dpa_ref[:, h * HEAD_DIM:(h + 1) * HEAD_DIM] = (_dot(ds, k_all, NN) * scale).astype(BF16)
            dk_acc[kh] = dk_acc[kh] + _dot(q, ds, TN) * scale
            dv_acc[kh] = dv_acc[kh] + _dot(do, p, TN)
            dsink_ref[h:h + 1, :] += jnp.broadcast_to(-jnp.sum(p_sink * delta, axis=0, keepdims=True), (1, 128))

        acc = jnp.concatenate(dk_acc + dv_acc, axis=0).T
        own = acc[WINDOW:, :]
        tail = own[t - WINDOW:, :] + carry_ref[...]
        if t > WINDOW:
            dpa_ref[0:t - WINDOW, 2 * ATT_W:] = own[:t - WINDOW, :].astype(BF16)
        dpa_ref[t - WINDOW:t, 2 * ATT_W:] = tail.astype(BF16)
        carry_ref[...] = acc[:WINDOW, :]

    halo_blocks = t // WINDOW
    wpa = 2 * ATT_W + 2 * KV_W
    cur = pl.BlockSpec((t, wpa), lambda n: (nb - 1 - n, 0))
    prev = pl.BlockSpec((WINDOW, 2 * KV_W),
                        lambda n: (jnp.maximum((nb - 1 - n) * halo_blocks - 1, 0), (2 * ATT_W) // (2 * KV_W)))
    return pl.pallas_call(
        body, grid=(nb,),
        in_specs=[pl.BlockSpec(memory_space=pltpu.SMEM), cur, prev, pl.BlockSpec((t, ATT_W), lambda n: (nb - 1 - n, 0))],
        out_specs=[pl.BlockSpec((t, wpa), lambda n: (nb - 1 - n, 0)), pl.BlockSpec((8, 128), lambda n: (0, 0))],
        out_shape=[jax.ShapeDtypeStruct((l, wpa), BF16), jax.ShapeDtypeStruct((8, 128), F32)],
        scratch_shapes=[pltpu.VMEM((WINDOW, 2 * KV_W), F32)],
        compiler_params=_params("arbitrary"), name=name)(sinks, pa, pa, dya)


def _scan(xr, xi, lr, li, t, reverse):
    row = lax.broadcasted_iota(jnp.int32, (t, 1), 0)
    d = 1
    pr, pi = lr, li
    while d < t:
        if reverse:
            sr = jnp.where(row < t - d, pltpu.roll(xr, t - d, 0), 0.0)
            si = jnp.where(row < t - d, pltpu.roll(xi, t - d, 0), 0.0)
        else:
            sr = jnp.where(row >= d, pltpu.roll(xr, d, 0), 0.0)
            si = jnp.where(row >= d, pltpu.roll(xi, d, 0), 0.0)
        xr, xi = xr + pr * sr - pi * si, xi + pr * si + pi * sr
        pr, pi = pr * pr - pi * pi, 2.0 * pr * pi
        d *= 2
    return xr, xi


SCAN_SUB = 8


def _split_hi_lo(a):
    hi = a.astype(BF16)
    lo = (a - hi.astype(F32)).astype(BF16)
    return jnp.concatenate([hi, lo], axis=0)


def _scan_mxu(xr, xi, tab, lam3, lam8, tri, expand, cr, ci, t, reverse):
    ns = t // SCAN_SUB
    n = xr.shape[1]
    v3 = lambda a: a.reshape(ns, SCAN_SUB, n)
    x3r, x3i = v3(xr), v3(xi)
    br = (x3r * tab[0] - x3i * tab[1]).reshape(t, n)
    bi = (x3r * tab[1] + x3i * tab[0]).reshape(t, n)
    pm = jnp.dot(tri, jnp.concatenate([br, bi], axis=1).astype(BF16), preferred_element_type=F32)
    p3r, p3i = v3(pm[:t, :n]), v3(pm[:t, n:])
    slr = p3r * tab[2] - p3i * tab[3]
    sli = p3r * tab[3] + p3i * tab[2]
    totr, toti = pm[t:, :n], pm[t:, n:]
    l3r, l3i = lam3
    l8r, l8i = lam8
    row = lax.broadcasted_iota(jnp.int32, (ns, 1), 0)
    edge = row == (ns - 1 if reverse else 0)
    er = totr * l3r - toti * l3i + jnp.where(edge, l8r * cr - l8i * ci, 0.0)
    ei = totr * l3i + toti * l3r + jnp.where(edge, l8r * ci + l8i * cr, 0.0)
    er, ei = _scan(er, ei, l8r, l8i, ns, reverse)
    shift = ns - 1 if reverse else 1
    nbr = jnp.where(edge, cr, pltpu.roll(er, shift, 0))
    nbi = jnp.where(edge, ci, pltpu.roll(ei, shift, 0))
    ex = jnp.dot(expand, _split_hi_lo(jnp.concatenate([nbr, nbi], axis=1)), preferred_element_type=F32)
    e3r, e3i = v3(ex[:, :n]), v3(ex[:, n:])
    sr = (slr + e3r * tab[4] - e3i * tab[5]).reshape(t, n)
    si = (sli + e3r * tab[5] + e3i * tab[4]).reshape(t, n)
    out = 0 if reverse else ns - 1
    return sr, si, er[out:out + 1, :], ei[out:out + 1, :]


def _scan_consts(t):
    import numpy as np
    ns = t // SCAN_SUB
    r = np.arange(t)
    same = (r[:, None] // SCAN_SUB) == (r[None, :] // SCAN_SUB)
    sums = (np.arange(ns)[:, None] == (r[None, :] // SCAN_SUB))
    tri = []
    for keep in (r[None, :] <= r[:, None], r[None, :] >= r[:, None]):
        tri.append(np.concatenate([same & keep, sums], axis=0).astype(np.float32))
    ex = ((r[:, None] // SCAN_SUB) == np.arange(ns)[None, :]).astype(np.float32)
    return jnp.asarray(np.stack(tri), BF16), jnp.asarray(np.concatenate([ex, ex], axis=1), BF16)


def _scan_tables(lr, li):
    import numpy as np
    den = lr * lr + li * li
    ir, ii = lr / den, -li / den
    mul = lambda a, b: (a[0] * b[0] - a[1] * b[1], a[0] * b[1] + a[1] * b[0])
    pw = {0: (jnp.ones_like(lr), jnp.zeros_like(lr))}
    for e in range(1, 9):
        pw[e] = mul(pw[e - 1], (lr, li))
    for e in range(-1, -5, -1):
        pw[e] = mul(pw[e + 1], (ir, ii))
    powers = jnp.stack([jnp.stack(pw[e]) for e in range(-4, 9)] + [jnp.zeros((2, lr.shape[0]), F32)])
    j = np.arange(SCAN_SUB)
    exps = [4 - j, j - 4, j + 1, j - 3, 3 - j, 8 - j]
    e_idx = np.stack([exps[t] + 4 for t in range(6) for _ in range(2)])
    c_idx = np.tile(np.array([0, 1])[:, None], (6, SCAN_SUB))
    sign = np.where((c_idx == 1) & (np.arange(12)[:, None] >= 6), -1.0, 1.0).astype(np.float32)
    tabs = powers[e_idx, c_idx] * sign[:, :, None]
    lam = powers[np.array([5, 5, 7, 7, 12, 12, 13, 13]), np.array([0, 1, 0, 1, 0, 1, 0, 0])]
    return lam, tabs


SSM_HALVES = 2
SSM_HW = SSM_W // SSM_HALVES
SSM_HN = SSM_N // SSM_HALVES


def _bd_nn(x, w):
    a = w.shape[1]
    return jnp.concatenate([_dot(x[:, h * a:(h + 1) * a], w[h]) for h in range(SSM_HALVES)], axis=1)


def _bd_nt(x, w):
    b = w.shape[2]
    return jnp.concatenate([_dot(x[:, h * b:(h + 1) * b], w[h], NT) for h in range(SSM_HALVES)], axis=1)


def _bd_tn(x, y):
    a, b = x.shape[1] // SSM_HALVES, y.shape[1] // SSM_HALVES
    return jnp.stack([_dot(x[:, h * a:(h + 1) * a], y[:, h * b:(h + 1) * b], TN) for h in range(SSM_HALVES)])


def _ssm_states(u, s0r, s0i, lam_ref, tab_ref, tri_ref, ex_ref, bre, bim, t):
    tab = tuple(tab_ref[k] for k in range(6))
    return _scan_mxu(_bd_nn(u, bre), _bd_nn(u, bim), tab, (lam_ref[2:3, :], lam_ref[3:4, :]),
                     (lam_ref[4:5, :], lam_ref[5:6, :]), tri_ref[0], ex_ref[...], s0r, s0i, t, False)


def _ssm_head(u, z, xr, xi, cre, cim, dskip, wglu, bglu):
    y = _bd_nn(xr, cre) - _bd_nn(xi, cim) + dskip * u
    y2, dgelu = _gelu_and_grad(y)
    gate = _sigmoid(_dot(y2, wglu) + bglu)
    y3 = y2 * gate
    return y2, dgelu, gate, y3


def _with_comm(body, comm, n_in, n_out, grid, mid_step):
    if comm is None:
        return body
    nc = len(comm["args"])
    n_sem = len(comm["scratch"])
    grid = (grid,) if isinstance(grid, int) else tuple(grid)
    total = math.prod(grid)

    def hosted(*refs):
        ins, cin = refs[:n_in], refs[n_in:n_in + nc]
        outs, cout = refs[n_in + nc:n_in + nc + n_out], refs[n_in + nc + n_out:n_in + 2 * nc + n_out]
        rest = refs[n_in + 2 * nc + n_out:]
        scratch, csem = rest[:len(rest) - n_sem], rest[len(rest) - n_sem:]
        start, forward, finish = comm["phases"](cin, cout, *csem)
        step = pl.program_id(0)
        for axis in range(1, len(grid)):
            step = step * grid[axis] + pl.program_id(axis)
        pl.when(step == 0)(start)
        pl.when(step == (mid_step if mid_step >= 0 else total + mid_step))(forward)
        body(*ins, *outs, *scratch)
        pl.when(step == total - 1)(finish)

    return hosted


def _comm_extra(comm):
    if comm is None:
        return [], [], [], [], []
    anyspec = pl.BlockSpec(memory_space=pl.ANY)
    nc = len(comm["args"])
    return comm["args"], [anyspec] * nc, [anyspec] * nc, comm["out_shape"], comm["scratch"]


def _ssm_fwd(ps, scan_ops, bblk, cblk, dskip, wglu, bglu, *, name, t=SEQ_BLOCK, comm=None):
    l = ps.shape[0]
    assert l % t == 0
    nb = l // t
    ns = t // SCAN_SUB
    c_args, c_in, c_out, c_shape, c_scratch = _comm_extra(comm)

    def body(ps_ref, lam_ref, tab_ref, tri_ref, ex_ref, b_ref, c_ref, d_ref, w_ref, bg_ref, ys_ref, chk_ref, xs_ref,
             st_ref):
        @pl.when(pl.program_id(0) == 0)
        def _():
            st_ref[...] = jnp.zeros_like(st_ref)

        chk_ref[...] = jnp.broadcast_to(st_ref[...], chk_ref.shape)
        u = ps_ref[:, :SSM_W].astype(F32)
        z = ps_ref[:, SSM_W:].astype(F32)
        xr, xi, er, ei = _ssm_states(u, st_ref[:, :SSM_N], st_ref[:, SSM_N:], lam_ref, tab_ref, tri_ref, ex_ref,
                                     b_ref[0], b_ref[1], t)
        st_ref[:, :SSM_N] = er
        st_ref[:, SSM_N:] = ei
        xr, xi = xr.astype(BF16), xi.astype(BF16)
        xs_ref[:, :SSM_N] = xr
        xs_ref[:, SSM_N:] = xi
        _, _, _, y3 = _ssm_head(u, z, xr, xi, c_ref[0], c_ref[1], d_ref[...], w_ref[...], bg_ref[...])
        sz, _ = _silu_and_grad(z)
        ys_ref[...] = (y3 * sz).astype(BF16)

    full = lambda shape: pl.BlockSpec(shape, lambda i: (0,) * len(shape))
    return pl.pallas_call(
        _with_comm(body, comm, 10, 3, nb, nb - 1), grid=(nb,),
        in_specs=[pl.BlockSpec((t, 2 * SSM_W), lambda i: (i, 0)), full((8, SSM_N)), full((12, SCAN_SUB, SSM_N)),
                  full((2, t + ns, t)), full((t, 2 * ns)), full((2, SSM_HALVES, SSM_HW, SSM_HN)),
                  full((2, SSM_HALVES, SSM_HN, SSM_HW)), full((1, SSM_W)), full((SSM_W, SSM_W)), full((1, SSM_W))] + c_in,
        out_specs=[pl.BlockSpec((t, SSM_W), lambda i: (i, 0)), pl.BlockSpec((8, 2 * SSM_N), lambda i: (i, 0)),
                   pl.BlockSpec((t, 2 * SSM_N), lambda i: (i, 0))] + c_out,
        out_shape=[jax.ShapeDtypeStruct((l, SSM_W), BF16), jax.ShapeDtypeStruct((nb * 8, 2 * SSM_N), F32),
                   jax.ShapeDtypeStruct((l, 2 * SSM_N), BF16)] + c_shape,
        scratch_shapes=[pltpu.VMEM((1, 2 * SSM_N), F32)] + c_scratch,
        compiler_params=_params("arbitrary"), name=name)(ps, *scan_ops, bblk, cblk, dskip, wglu, bglu, *c_args)


def _ssm_bwd(ps, dys, chk, states, scan_ops, bblk, cblk, dskip, wglu, bglu, *, name, t=SEQ_BLOCK, comm=None):
    l = ps.shape[0]
    assert l % t == 0
    nb = l // t
    ns = t // SCAN_SUB
    c_args, c_in, c_out, c_shape, c_scratch = _comm_extra(comm)

    def body(ps_ref, dys_ref, chk_ref, xs_ref, lam_ref, tab_ref, tri_ref, ex_ref, b_ref, c_ref, d_ref, w_ref, bg_ref,
             dps_ref, db_ref, dc_ref, dw_acc, sums_acc, gc_ref, db_acc, dc_acc):
        n = pl.program_id(0)

        @pl.when(n == 0)
        def _():
            gc_ref[...] = jnp.zeros_like(gc_ref)
            db_acc[...] = jnp.zeros_like(db_acc)
            dc_acc[...] = jnp.zeros_like(dc_acc)
            dw_acc[...] = jnp.zeros_like(dw_acc)
            sums_acc[...] = jnp.zeros_like(sums_acc)

        row = lax.broadcasted_iota(jnp.int32, (t, 1), 0)
        u = ps_ref[:, :SSM_W].astype(F32)
        z = ps_ref[:, SSM_W:].astype(F32)
        s0r, s0i = chk_ref[0:1, :SSM_N], chk_ref[0:1, SSM_N:]
        xr, xi = xs_ref[:, :SSM_N], xs_ref[:, SSM_N:]
        dskip = d_ref[...]
        y2, dgelu, gate, y3 = _ssm_head(u, z, xr, xi, c_ref[0], c_ref[1], dskip, w_ref[...], bg_ref[...])
        sz, dsz = _silu_and_grad(z)
        dys_v = dys_ref[...]
        dps_ref[:, SSM_W:] = (dys_v * y3 * dsz).astype(BF16)
        dy3 = dys_v * sz
        da = dy3 * y2 * gate * (1.0 - gate)
        dy2 = dy3 * gate + _dot(da, w_ref[...], NT)
        dw_acc[...] += _dot(y2, da, TN)
        dy = dy2 * dgelu
        sums_acc[2:3, :SSM_W] += jnp.sum(dy * u, axis=0, keepdims=True)
        sums_acc[3:4, :SSM_W] += jnp.sum(da, axis=0, keepdims=True)
        dc_acc[0] += _bd_tn(dy, xr)
        dc_acc[1] += -_bd_tn(dy, xi)
        rev_tab = tuple(tab_ref[k] for k in range(6, 12))
        gr, gi, gcr, gci = _scan_mxu(
            _bd_nt(dy, c_ref[0]), -_bd_nt(dy, c_ref[1]), rev_tab, (lam_ref[2:3, :], -lam_ref[3:4, :]),
            (lam_ref[4:5, :], -lam_ref[5:6, :]), tri_ref[1], ex_ref[...], gc_ref[:, :SSM_N], gc_ref[:, SSM_N:], t, True)
        gc_ref[:, :SSM_N] = gcr
        gc_ref[:, SSM_N:] = gci
        db_acc[0] += _bd_tn(u, gr)
        db_acc[1] += _bd_tn(u, gi)
        du = dskip * dy + _bd_nt(gr, b_ref[0]) + _bd_nt(gi, b_ref[1])
        dps_ref[:, :SSM_W] = du.astype(BF16)
        spr = jnp.where(row == 0, s0r, pltpu.roll(xr.astype(F32), 1, 0))
        spi = jnp.where(row == 0, s0i, pltpu.roll(xi.astype(F32), 1, 0))
        sums_acc[0:1, :] += jnp.sum(gr * spr + gi * spi, axis=0, keepdims=True)
        sums_acc[1:2, :] += jnp.sum(gi * spr - gr * spi, axis=0, keepdims=True)

        @pl.when(n == nb - 1)
        def _():
            per_half = SSM_GROUPS // SSM_HALVES
            for k in range(2):
                for g in range(SSM_GROUPS):
                    h, gl = divmod(g, per_half)
                    c0, p0 = gl * SSM_GROUP, gl * SSM_STATE
                    db_ref[k, g * SSM_GROUP:(g + 1) * SSM_GROUP, :] = db_acc[k, h, c0:c0 + SSM_GROUP, p0:p0 + SSM_STATE]
                    dc_ref[k, g * SSM_GROUP:(g + 1) * SSM_GROUP, :] = dc_acc[k, h, c0:c0 + SSM_GROUP, p0:p0 + SSM_STATE]

    full = lambda shape: pl.BlockSpec(shape, lambda n: (0,) * len(shape))
    return pl.pallas_call(
        _with_comm(body, comm, 13, 5, nb, 0), grid=(nb,),
        in_specs=[pl.BlockSpec((t, 2 * SSM_W), lambda n: (nb - 1 - n, 0)),
                  pl.BlockSpec((t, SSM_W), lambda n: (nb - 1 - n, 0)),
                  pl.BlockSpec((8, 2 * SSM_N), lambda n: (nb - 1 - n, 0)),
                  pl.BlockSpec((t, 2 * SSM_N), lambda n: (nb - 1 - n, 0)),
                  full((8, SSM_N)), full((12, SCAN_SUB, SSM_N)), full((2, t + ns, t)), full((t, 2 * ns)),
                  full((2, SSM_HALVES, SSM_HW, SSM_HN)), full((2, SSM_HALVES, SSM_HN, SSM_HW)), full((1, SSM_W)),
                  full((SSM_W, SSM_W)), full((1, SSM_W))] + c_in,
        out_specs=[pl.BlockSpec((t, 2 * SSM_W), lambda n: (nb - 1 - n, 0)), full((2, SSM_W, SSM_STATE)),
                   full((2, SSM_W, SSM_STATE)), full((SSM_W, SSM_W)), full((8, SSM_N))] + c_out,
        out_shape=[jax.ShapeDtypeStruct((l, 2 * SSM_W), BF16),
                   jax.ShapeDtypeStruct((2, SSM_W, SSM_STATE), F32),
                   jax.ShapeDtypeStruct((2, SSM_W, SSM_STATE), F32),
                   jax.ShapeDtypeStruct((SSM_W, SSM_W), F32),
                   jax.ShapeDtypeStruct((8, SSM_N), F32)] + c_shape,
        scratch_shapes=[pltpu.VMEM((1, 2 * SSM_N), F32), pltpu.VMEM((2, SSM_HALVES, SSM_HW, SSM_HN), F32),
                        pltpu.VMEM((2, SSM_HALVES, SSM_HW, SSM_HN), F32)] + c_scratch,
        compiler_params=_params("arbitrary"), name=name)(ps, dys, chk, states, *scan_ops, bblk, cblk, dskip, wglu, bglu,
                                                         *c_args)


def _pool_count(i, t):
    pos = lax.broadcasted_iota(jnp.int32, (t, POOL_W), 0) + i * t + 1
    col = lax.broadcasted_iota(jnp.int32, (t, POOL_W), 1)
    win = jnp.where(col < POOL_GW, 2, jnp.where(col < 2 * POOL_GW, 4, jnp.where(col < 3 * POOL_GW, 8, 16)))
    return 1.0 / jnp.minimum(pos, win).astype(F32), col


def _window_sums(ext, n_rows, forward):
    col = lax.broadcasted_iota(jnp.int32, ext.shape, 1)
    sh = (lambda a, d: pltpu.roll(a, d, 0)) if forward else (lambda a, d: pltpu.roll(a, n_rows - d, 0))
    a2 = ext + sh(ext, 1)
    a4 = a2 + sh(a2, 2)
    a8 = a4 + sh(a4, 4)
    a16 = a8 + sh(a8, 8)
    return jnp.where(col < POOL_GW, a2, jnp.where(col < 2 * POOL_GW, a4, jnp.where(col < 3 * POOL_GW, a8, a16)))


def _pool_mix(pooled, wp_ref):
    return jnp.concatenate([_dot(pooled[:, g * POOL_GW:(g + 1) * POOL_GW], wp_ref[g]) for g in range(4)], axis=1)


def _pool_pooled(i, cur_u, prev_u, t):
    prev = jnp.where(i > 0, prev_u, 0.0)
    ext = jnp.concatenate([prev, cur_u], axis=0)
    inv_cnt, _ = _pool_count(i, t)
    return _window_sums(ext, t + POOL_HALO, True)[POOL_HALO:, :] * inv_cnt - cur_u


def _pool_fwd(pp, wpool, pscale, *, name, t=SEQ_BLOCK):
    l = pp.shape[0]
    t = min(t, l)

    def body(cur_ref, prev_ref, wp_ref, sc_ref, yp_ref):
        i = pl.program_id(0)
        pooled = _pool_pooled(i, cur_ref[:, :POOL_W].astype(F32), prev_ref[...].astype(F32), t)
        lin = _pool_mix(pooled, wp_ref)
        sz, _ = _silu_and_grad(cur_ref[:, POOL_W:].astype(F32))
        yp_ref[...] = (lin * sc_ref[...] * sz).astype(BF16)

    hb = t // POOL_HALO
    return pl.pallas_call(
        body, grid=(l // t,),
        in_specs=[pl.BlockSpec((t, 2 * POOL_W), lambda i: (i, 0)),
                  pl.BlockSpec((POOL_HALO, POOL_W), lambda i: (jnp.maximum(i * hb - 1, 0), 0)),
                  pl.BlockSpec((4, POOL_GW, POOL_GW), lambda i: (0, 0, 0)),
                  pl.BlockSpec((1, POOL_W), lambda i: (0, 0))],
        out_specs=pl.BlockSpec((t, POOL_W), lambda i: (i, 0)),
        out_shape=jax.ShapeDtypeStruct((l, POOL_W), BF16),
        compiler_params=_params("parallel"), name=name)(pp, pp, wpool, pscale)


def _pool_bwd(pp, dyp, wpool, pscale, *, name, t=SEQ_BLOCK):
    l = pp.shape[0]
    t = min(t, l)
    nb = l // t

    def body(cur_ref, prev_ref, dyp_ref, wp_ref, sc_ref, dpp_ref, dwp_ref, sums_ref, carry_ref):
        n = pl.program_id(0)
        i = nb - 1 - n

        @pl.when(n == 0)
        def _():
            carry_ref[...] = jnp.zeros_like(carry_ref)
            dwp_ref[...] = jnp.zeros_like(dwp_ref)
            sums_ref[...] = jnp.zeros_like(sums_ref)

        cur_u = cur_ref[:, :POOL_W].astype(F32)
        pooled = _pool_pooled(i, cur_u, prev_ref[...].astype(F32), t)
        lin = _pool_mix(pooled, wp_ref)
        sz, dsz = _silu_and_grad(cur_ref[:, POOL_W:].astype(F32))
        dyp_v = dyp_ref[...]
        scale = sc_ref[...]
        dpp_ref[:, POOL_W:] = (dyp_v * lin * scale * dsz).astype(BF16)
        dpre = dyp_v * sz
        sums_ref[0:1, :] += jnp.sum(dpre * lin, axis=0, keepdims=True)
        dlin = dpre * scale
        dpooled = []
        for g in range(4):
            dl = dlin[:, g * POOL_GW:(g + 1) * POOL_GW]
            dwp_ref[g] += _dot(pooled[:, g * POOL_GW:(g + 1) * POOL_GW], dl, TN)
            dpooled.append(_dot(dl, wp_ref[g], NT))
        dpooled = jnp.concatenate(dpooled, axis=1)
        inv_cnt, _ = _pool_count(i, t)
        dq = dpooled * inv_cnt
        ext = jnp.concatenate([dq, carry_ref[...]], axis=0)
        du = _window_sums(ext, t + POOL_HALO, False)[:t, :] - dpooled
        dpp_ref[:, :POOL_W] = du.astype(BF16)
        carry_ref[...] = dq[:POOL_HALO, :]

    hb = t // POOL_HALO
    return pl.pallas_call(
        body, grid=(nb,),
        in_specs=[pl.BlockSpec((t, 2 * POOL_W), lambda n: (nb - 1 - n, 0)),
                  pl.BlockSpec((POOL_HALO, POOL_W), lambda n: (jnp.maximum((nb - 1 - n) * hb - 1, 0), 0)),
                  pl.BlockSpec((t, POOL_W), lambda n: (nb - 1 - n, 0)),
                  pl.BlockSpec((4, POOL_GW, POOL_GW), lambda n: (0, 0, 0)),
                  pl.BlockSpec((1, POOL_W), lambda n: (0, 0))],
        out_specs=[pl.BlockSpec((t, 2 * POOL_W), lambda n: (nb - 1 - n, 0)),
                   pl.BlockSpec((4, POOL_GW, POOL_GW), lambda n: (0, 0, 0)),
                   pl.BlockSpec((8, POOL_W), lambda n: (0, 0))],
        out_shape=[jax.ShapeDtypeStruct((l, 2 * POOL_W), BF16), jax.ShapeDtypeStruct((4, POOL_GW, POOL_GW), F32),
                   jax.ShapeDtypeStruct((8, POOL_W), F32)],
        scratch_shapes=[pltpu.VMEM((POOL_HALO, POOL_W), F32)],
        compiler_params=_params("arbitrary"), name=name)(pp, pp, dyp, wpool, pscale)


def _merge_fwd(ya, ys, yp, wa, ws, wp, pg, wout, x, gate, *, name, tm=512):
    l, d = x.shape
    tm = min(tm, l)

    def body(ya_ref, ys_ref, yp_ref, wa_ref, ws_ref, wp_ref, pg_ref, wo_ref, x_ref, g_ref,
             xn_ref, mg_ref, ba_ref, bs_ref, bp_ref, out_ref):
        acc = None
        for k, (y_ref, w_ref, b_ref) in enumerate(((ya_ref, wa_ref, ba_ref), (ys_ref, ws_ref, bs_ref),
                                                   (yp_ref, wp_ref, bp_ref))):
            br = _dot(y_ref[...], w_ref[...])
            b_ref[...] = br.astype(BF16)
            term = _sigmoid(pg_ref[:, k * d:(k + 1) * d].astype(F32)) * br
            acc = term if acc is None else acc + term
        merged = acc.astype(BF16)
        mg_ref[...] = merged
        out = _dot(merged, wo_ref[...])
        out_ref[...] = out.astype(BF16)
        xn_ref[...] = x_ref[...] + g_ref[...] * out

    rowy = pl.BlockSpec((tm, ATT_W), lambda i: (i, 0))
    wsp = pl.BlockSpec((ATT_W, d), lambda i: (0, 0))
    rowd = pl.BlockSpec((tm, d), lambda i: (i, 0))
    return pl.pallas_call(
        body, grid=(l // tm,),
        in_specs=[rowy, rowy, rowy, wsp, wsp, wsp, pl.BlockSpec((tm, 3 * d), lambda i: (i, 0)),
                  pl.BlockSpec((d, d), lambda i: (0, 0)), rowd, pl.BlockSpec((1, d), lambda i: (0, 0))],
        out_specs=[rowd] * 6,
        out_shape=[jax.ShapeDtypeStruct((l, d), F32)] + [jax.ShapeDtypeStruct((l, d), BF16)] * 5,
        compiler_params=_params("parallel"), name=name)(ya, ys, yp, wa, ws, wp, pg, wout, x, gate)


def _merge_bwd(dx, out, gate, wout, pg, brs, wbrs, ys, merged, *, name, tm=256, comm=None):
    l, d = dx.shape
    tm = min(tm, l)
    nb = l // tm
    w = ys[0].shape[1]

    def body(dx_ref, out_ref, g_ref, w_ref, pg_ref, ba_ref, bs_ref, bp_ref, wa_ref, ws_ref, wp_ref,
             ya_ref, ys_ref, yp_ref, mg_ref,
             dya_ref, dys_ref, dyp_ref, dpg_ref, sums_ref, dwa_ref, dws_ref, dwp_ref, dwo_ref, acc_br, acc_out):
        i = pl.program_id(0)

        @pl.when(i == 0)
        def _():
            sums_ref[...] = jnp.zeros_like(sums_ref)
            acc_br[...] = jnp.zeros_like(acc_br)
            acc_out[...] = jnp.zeros_like(acc_out)

        dxv = dx_ref[...]
        sums_ref[0:1, :] += jnp.sum(dxv * out_ref[...].astype(F32), axis=0, keepdims=True)
        dmo = (dxv * g_ref[...]).astype(BF16)
        acc_out[...] += _dot(mg_ref[...], dmo, TN)
        dmerged = _dot(dmo, w_ref[...], NT)
        branches = ((ba_ref, wa_ref, ya_ref, dya_ref), (bs_ref, ws_ref, ys_ref, dys_ref), (bp_ref, wp_ref, yp_ref, dyp_ref))
        for k, (b_ref, wk_ref, y_ref, dy_ref) in enumerate(branches):
            gk = _sigmoid(pg_ref[:, k * d:(k + 1) * d].astype(F32))
            dbr = (dmerged * gk).astype(BF16)
            dpg_ref[:, k * d:(k + 1) * d] = (dmerged * b_ref[...].astype(F32) * gk * (1.0 - gk)).astype(BF16)
            dy_ref[...] = _dot(dbr, wk_ref[...], NT)
            acc_br[k] += _dot(y_ref[...], dbr, TN)

        @pl.when(i == nb - 1)
        def _():
            for k, dw_ref in enumerate((dwa_ref, dws_ref, dwp_ref)):
                dw_ref[...] = acc_br[k].astype(BF16)
            dwo_ref[...] = acc_out[...].astype(BF16)

    row = pl.BlockSpec((tm, d), lambda i: (i, 0))
    half = pl.BlockSpec((tm, w), lambda i: (i, 0))
    wide = pl.BlockSpec((tm, 3 * d), lambda i: (i, 0))
    const = lambda shape: pl.BlockSpec(shape, lambda i: (0,) * len(shape))
    c_args, c_in, c_out, c_shape, c_scratch = _comm_extra(comm)
    res = pl.pallas_call(
        _with_comm(body, comm, 15, 9, nb, 0), grid=(nb,),
        in_specs=[row, row, const((1, d)), const((d, d)), wide, row, row, row, const((w, d)), const((w, d)), const((w, d)),
                  half, half, half, row] + c_in,
        out_specs=[half, half, half, wide, const((8, d)), const((w, d)), const((w, d)), const((w, d)), const((d, d))] + c_out,
        out_shape=[jax.ShapeDtypeStruct((l, w), F32)] * 3 + [jax.ShapeDtypeStruct((l, 3 * d), BF16),
                                                             jax.ShapeDtypeStruct((8, d), F32)]
                  + [jax.ShapeDtypeStruct((w, d), BF16)] * 3 + [jax.ShapeDtypeStruct((d, d), BF16)] + c_shape,
        scratch_shapes=[pltpu.VMEM((3, w, d), F32), pltpu.VMEM((d, d), F32)] + c_scratch,
        compiler_params=_params("arbitrary"), name=name)(dx, out, gate, wout, pg, *brs, *wbrs, *ys, merged, *c_args)
    return list(res[:9]), list(res[9:])


def _adamw(w, gs, m, v, *, name, tr=256):
    r, c = w.shape
    ns = len(gs)
    p, rs, _ = gs[0].shape
    assert rs * ns == r
    tr = min(tr, rs)
    assert rs % tr == 0
    nr = rs // tr
    c1 = 1.0 / (1.0 - ADAM_B1 ** ADAM_STEP)
    c2 = 1.0 / (1.0 - ADAM_B2 ** ADAM_STEP)

    def body(*refs):
        w_ref, g_refs, (m_ref, v_ref, go_ref, d_ref, mo_ref, vo_ref) = refs[0], refs[1:1 + ns], refs[1 + ns:]
        slab = pl.program_id(0)
        gv = None
        for k, g_ref in enumerate(g_refs):
            gk = g_ref[0].astype(F32)
            for j in range(1, p):
                gk = gk + g_ref[j].astype(F32)
            gv = gk if gv is None else jnp.where(slab == k, gk, gv)
        go_ref[...] = gv
        mn = ADAM_B1 * m_ref[...] + (1.0 - ADAM_B1) * gv
        vn = ADAM_B2 * v_ref[...] + (1.0 - ADAM_B2) * (gv * gv)
        mo_ref[...] = mn
        vo_ref[...] = vn
        d_ref[...] = -ADAM_LR * ((mn * c1) / (jnp.sqrt(vn * c2) + ADAM_EPS) + ADAM_WD * w_ref[...])

    row = pl.BlockSpec((tr, c), lambda s, i: (s * nr + i, 0))
    g_specs = [pl.BlockSpec((p, tr, c), lambda s, i, k=k: (0, jnp.where(s == k, i, 0), 0)) for k in range(ns)]
    return pl.pallas_call(
        body, grid=(ns, nr),
        in_specs=[row] + g_specs + [row, row],
        out_specs=[row] * 4,
        out_shape=[jax.ShapeDtypeStruct((r, c), F32)] * 4,
        compiler_params=_params("arbitrary", "arbitrary"), name=name)(w, *gs, m, v)


def _exchange(arrs, *, scatter, name):
    n = len(arrs)
    out_shape = [jax.ShapeDtypeStruct(a.shape if scatter else (N_DEV,) + a.shape, a.dtype) for a in arrs]

    def body(*refs):
        ins, outs = refs[:n], refs[n:2 * n]
        send_sems, recv_sems, loc_sems = refs[2 * n:]
        me = 4 * lax.axis_index("x") + 2 * lax.axis_index("y") + lax.axis_index("c")
        local = []
        for k in range(n):
            src = ins[k].at[me] if scatter else ins[k]
            cp = pltpu.make_async_copy(src, outs[k].at[me], loc_sems.at[k])
            cp.start()
            local.append(cp)
        remote = []
        for r in range(1, N_DEV):
            peer = me ^ r
            for k in range(n):
                src = ins[k].at[peer] if scatter else ins[k]
                cp = pltpu.make_async_remote_copy(
                    src_ref=src, dst_ref=outs[k].at[me], send_sem=send_sems.at[k, r - 1], recv_sem=recv_sems.at[k, r - 1],
                    device_id=(peer // 4, (peer // 2) % 2, peer % 2), device_id_type=pl.DeviceIdType.MESH)
                cp.start()
                remote.append(cp)
        for cp in remote:
            cp.wait()
        for cp in local:
            cp.wait()

    anyspec = pl.BlockSpec(memory_space=pl.ANY)
    return pl.pallas_call(
        body, in_specs=[anyspec] * n, out_specs=[anyspec] * n, out_shape=out_shape,
        scratch_shapes=[pltpu.SemaphoreType.DMA((n, N_DEV - 1)), pltpu.SemaphoreType.DMA((n, N_DEV - 1)),
                        pltpu.SemaphoreType.DMA((n,))],
        name=name)(*arrs)


def _mesh_place():
    x, y, c = lax.axis_index("x"), lax.axis_index("y"), lax.axis_index("c")
    other_chips = [(1 - x, y), (x, 1 - y), (1 - x, 1 - y)]
    return x, y, c, other_chips


def _run_plan(plan, *, name):
    n = len(plan["args"])

    def body(*refs):
        start, forward, finish = plan["phases"](refs[:n], refs[n:2 * n], *refs[2 * n:])
        start()
        forward()
        finish()

    anyspec = pl.BlockSpec(memory_space=pl.ANY)
    return pl.pallas_call(
        body, in_specs=[anyspec] * n, out_specs=[anyspec] * n, out_shape=plan["out_shape"],
        scratch_shapes=plan["scratch"], name=name)(*plan["args"])


def _gather_plan(arrs):
    n = len(arrs)

    def phases(ins, outs, send_sems, recv_sems, loc_sems):
        x, y, c, chips = _mesh_place()
        me = 4 * x + 2 * y + c
        slot = lambda px, py, pc: 4 * px + 2 * py + pc

        def copy(k, j, src, block, to):
            return pltpu.make_async_remote_copy(
                src_ref=src, dst_ref=outs[k].at[block], send_sem=send_sems.at[k, j], recv_sem=recv_sems.at[k, j],
                device_id=to, device_id_type=pl.DeviceIdType.MESH)

        local = [pltpu.make_async_copy(ins[k], outs[k].at[me], loc_sems.at[k]) for k in range(n)]
        first = []
        for k in range(n):
            first.append(copy(k, 0, ins[k], me, (x, y, 1 - c)))
            for j, chip in enumerate(chips):
                first.append(copy(k, 1 + j, ins[k], me, (*chip, c)))
        passed = [copy(k, 4 + j, outs[k].at[slot(*chip, c)], slot(*chip, c), (x, y, 1 - c))
                  for j, chip in enumerate(chips) for k in range(n)]

        def start():
            for cp in local + first:
                cp.start()

        def forward():
            for j, chip in enumerate(chips):
                for k in range(n):
                    copy(k, 1 + j, ins[k], slot(*chip, c), (x, y, c)).wait_recv()
                    passed[j * n + k].start()

        def finish():
            for k in range(n):
                copy(k, 0, ins[k], slot(x, y, 1 - c), (x, y, c)).wait_recv()
                for j, chip in enumerate(chips):
                    copy(k, 4 + j, ins[k], slot(*chip, 1 - c), (x, y, c)).wait_recv()
            for cp in first + passed:
                cp.wait_send()
            for cp in local:
                cp.wait()

        return start, forward, finish

    return dict(
        args=list(arrs), out_shape=[jax.ShapeDtypeStruct((N_DEV,) + a.shape, a.dtype) for a in arrs],
        scratch=[pltpu.SemaphoreType.DMA((n, 7)), pltpu.SemaphoreType.DMA((n, 7)), pltpu.SemaphoreType.DMA((n,))],
        phases=phases)


def _allreduce_small(small, extra, *, name):
    r, lanes = small.shape
    assert r % 16 == 0
    h = r // 2
    e = extra.shape[0]

    def body(s_ref, x_ref, out_ref, xall_ref, sib_ref, parts_ref, send_sems, recv_sems):
        x, y, c, chips = _mesh_place()
        me = 4 * x + 2 * y + c
        my_chip = 2 * x + y
        sibling = (x, y, 1 - c)
        mine = pl.ds(pl.multiple_of(c * h, 8), h)
        theirs = pl.ds(pl.multiple_of((1 - c) * h, 8), h)

        def remote(j, src, dst, to):
            return pltpu.make_async_remote_copy(src_ref=src, dst_ref=dst, send_sem=send_sems.at[j],
                                                recv_sem=recv_sems.at[j], device_id=to, device_id_type=pl.DeviceIdType.MESH)

        to_sibling = remote(0, s_ref.at[theirs], sib_ref, sibling)
        to_sibling.start()
        xall_ref[me] = x_ref[...]
        extras = []
        for rr in range(1, N_DEV):
            peer = me ^ rr
            cp = remote(4 + rr, x_ref, xall_ref.at[me], (peer // 4, (peer // 2) % 2, peer % 2))
            cp.start()
            extras.append(cp)
        to_sibling.wait_recv()
        parts_ref[my_chip] = s_ref[mine] + sib_ref[...]
        to_chips = [remote(1 + j, parts_ref.at[my_chip], parts_ref.at[my_chip], (px, py, c))
                    for j, (px, py) in enumerate(chips)]
        for cp in to_chips:
            cp.start()
        for cp in to_chips:
            cp.wait_recv()
        out_ref[mine] = (parts_ref[0] + parts_ref[1]) + (parts_ref[2] + parts_ref[3])
        done = remote(4, out_ref.at[mine], out_ref.at[mine], sibling)
        done.start()
        remote(4, out_ref.at[theirs], out_ref.at[theirs], sibling).wait_recv()
        for cp in extras:
            cp.wait()
        to_sibling.wait_send()
        for cp in to_chips:
            cp.wait_send()
        done.wait_send()

    vmem = pl.BlockSpec(memory_space=pltpu.VMEM)
    return pl.pallas_call(
        body, in_specs=[vmem, vmem], out_specs=[vmem, vmem],
        out_shape=[jax.ShapeDtypeStruct((r, lanes), F32), jax.ShapeDtypeStruct((N_DEV, e, lanes), F32)],
        scratch_shapes=[pltpu.VMEM((h, lanes), F32), pltpu.VMEM((4, h, lanes), F32),
                        pltpu.SemaphoreType.DMA((12,)), pltpu.SemaphoreType.DMA((12,))],
        compiler_params=pltpu.CompilerParams(vmem_limit_bytes=VMEM_LIMIT), name=name)(small, extra)


def _sibling_swap_plan(arrs):
    n = len(arrs)

    def phases(ins, outs, send_sems, recv_sems):
        x, y, c, _ = _mesh_place()
        copies = [pltpu.make_async_remote_copy(
            src_ref=ins[k].at[1 - c], dst_ref=outs[k], send_sem=send_sems.at[k], recv_sem=recv_sems.at[k],
            device_id=(x, y, 1 - c), device_id_type=pl.DeviceIdType.MESH) for k in range(n)]

        def start():
            for cp in copies:
                cp.start()

        def finish():
            for cp in copies:
                cp.wait()

        return start, (lambda: None), finish

    return dict(args=list(arrs), out_shape=[jax.ShapeDtypeStruct(a.shape[1:], a.dtype) for a in arrs],
                scratch=[pltpu.SemaphoreType.DMA((n,)), pltpu.SemaphoreType.DMA((n,))], phases=phases)


def _pair_add(mine, theirs, core, *, name, tr=256):
    _, r, c = mine.shape
    tr = min(tr, r)
    assert r % tr == 0

    def body(core_ref, m_ref, t_ref, o_ref):
        o_ref[...] = (m_ref[0].astype(F32) + t_ref[...].astype(F32)).astype(BF16)

    return pl.pallas_call(
        body,
        grid_spec=pltpu.PrefetchScalarGridSpec(
            num_scalar_prefetch=1, grid=(r // tr,),
            in_specs=[pl.BlockSpec((1, tr, c), lambda i, core_ref: (core_ref[0], i, 0)),
                      pl.BlockSpec((tr, c), lambda i, core_ref: (i, 0))],
            out_specs=pl.BlockSpec((tr, c), lambda i, core_ref: (i, 0))),
        out_shape=jax.ShapeDtypeStruct((r, c), BF16),
        compiler_params=_params("parallel"), name=name)(core, mine, theirs)


def _pair_add_small(mines, theirs, core, *, name):
    n = len(mines)

    def body(core_ref, *refs):
        for m_ref, t_ref, o_ref in zip(refs[:n], refs[n:2 * n], refs[2 * n:]):
            o_ref[...] = (m_ref[0].astype(F32) + t_ref[...].astype(F32)).astype(BF16)

    whole = lambda a: pl.BlockSpec(a.shape, lambda i, core_ref: (0,) * a.ndim)
    return pl.pallas_call(
        body,
        grid_spec=pltpu.PrefetchScalarGridSpec(
            num_scalar_prefetch=1, grid=(1,),
            in_specs=[pl.BlockSpec((1,) + m.shape[1:], lambda i, core_ref: (core_ref[0], 0, 0)) for m in mines]
                     + [whole(t) for t in theirs],
            out_specs=[whole(t) for t in theirs]),
        out_shape=[jax.ShapeDtypeStruct(t.shape, BF16) for t in theirs],
        compiler_params=_params("arbitrary"), name=name)(core, *mines, *theirs)


def _chip_scatter_plan(arrs):
    n = len(arrs)

    def phases(ins, outs, send_sems, recv_sems, loc_sems):
        x, y, c, chips = _mesh_place()
        mine = 2 * x + y
        local = [pltpu.make_async_copy(ins[k].at[mine], outs[k].at[mine], loc_sems.at[k]) for k in range(n)]
        remote = [pltpu.make_async_remote_copy(
            src_ref=ins[k].at[2 * px + py], dst_ref=outs[k].at[mine], send_sem=send_sems.at[k, j],
            recv_sem=recv_sems.at[k, j], device_id=(px, py, c), device_id_type=pl.DeviceIdType.MESH)
            for j, (px, py) in enumerate(chips) for k in range(n)]

        def start():
            for cp in local + remote:
                cp.start()

        def finish():
            for cp in remote:
                cp.wait()
            for cp in local:
                cp.wait()

        return start, (lambda: None), finish

    return dict(
        args=list(arrs), out_shape=[jax.ShapeDtypeStruct(a.shape, a.dtype) for a in arrs],
        scratch=[pltpu.SemaphoreType.DMA((n, 3)), pltpu.SemaphoreType.DMA((n, 3)), pltpu.SemaphoreType.DMA((n,))],
        phases=phases)


def _ssm_discretize(a_re, a_im, log_dt, b_re, b_im):
    dt = jnp.exp(log_dt)[:, None]
    mag = jnp.exp(a_re * dt)
    lr = mag * jnp.cos(a_im * dt)
    li = mag * jnp.sin(a_im * dt)
    den = a_re * a_re + a_im * a_im
    cr = ((lr - 1.0) * a_re + li * a_im) / den
    ci = (li * a_re - (lr - 1.0) * a_im) / den
    bbr = cr[..., None] * b_re - ci[..., None] * b_im
    bbi = cr[..., None] * b_im + ci[..., None] * b_re
    return lr, li, bbr, bbi


def _ssm_dense(lr, li, bbr, bbi, c_re, c_im, *, name):
    import numpy as np
    scan_ops = _scan_tables(lr.reshape(-1), li.reshape(-1)) + _scan_consts(SEQ_BLOCK)
    per_half = SSM_GROUPS // SSM_HALVES
    bt = jnp.stack([b.transpose(0, 2, 1).reshape(SSM_W, SSM_STATE) for b in (bbr, bbi)])
    ct = jnp.stack([c.transpose(0, 2, 1).reshape(SSM_N, SSM_GROUP) for c in (c_re, c_im)])
    rep_p = jnp.asarray(np.tile(np.eye(SSM_STATE, dtype=np.float32), (1, per_half)), BF16)
    rep_c = jnp.asarray(np.tile(np.eye(SSM_GROUP, dtype=np.float32), (1, per_half)), BF16)

    def body(bt_ref, ct_ref, rp_ref, rc_ref, b_ref, c_ref):
        def on_diagonal(shape, rows, cols):
            r = lax.broadcasted_iota(jnp.int32, shape, 0) // rows
            c = lax.broadcasted_iota(jnp.int32, shape, 1) // cols
            return r == c

        mask_b = on_diagonal((SSM_HW, SSM_HN), SSM_GROUP, SSM_STATE)
        mask_c = on_diagonal((SSM_HN, SSM_HW), SSM_STATE, SSM_GROUP)
        for k in range(2):
            for h in range(SSM_HALVES):
                b_rows = bt_ref[k, h * SSM_HW:(h + 1) * SSM_HW, :]
                b_ref[k, h] = jnp.where(mask_b, _dot(b_rows, rp_ref[...]), 0.0).astype(BF16)
                c_rows = ct_ref[k, h * SSM_HN:(h + 1) * SSM_HN, :]
                c_ref[k, h] = jnp.where(mask_c, _dot(c_rows, rc_ref[...]), 0.0).astype(BF16)

    vmem = pl.BlockSpec(memory_space=pltpu.VMEM)
    bblk, cblk = pl.pallas_call(
        body, in_specs=[vmem] * 4, out_specs=[vmem] * 2,
        out_shape=[jax.ShapeDtypeStruct((2, SSM_HALVES, SSM_HW, SSM_HN), BF16),
                   jax.ShapeDtypeStruct((2, SSM_HALVES, SSM_HN, SSM_HW), BF16)],
        compiler_params=pltpu.CompilerParams(vmem_limit_bytes=VMEM_LIMIT), name=name)(bt, ct, rep_p, rep_c)
    return scan_ops, bblk, cblk


def _ssm_extract(db, dc, sums):
    db = db.reshape(2, SSM_GROUPS, SSM_GROUP, SSM_STATE).transpose(0, 1, 3, 2)
    dc = dc.reshape(2, SSM_GROUPS, SSM_GROUP, SSM_STATE)
    dlr = sums[0].reshape(SSM_GROUPS, SSM_STATE)
    dli = sums[1].reshape(SSM_GROUPS, SSM_STATE)
    return dlr, dli, db[0], db[1], dc[0], dc[1]


def _in_groups():
    names = ("q", "k", "v", "u_ssm", "u_pool", "z_att", "z_ssm", "z_pool", "gates")
    sizes = (ATT_W, KV_W, KV_W, SSM_W, POOL_W, ATT_W, SSM_W, POOL_W, 3 * D_MODEL)
    r, lo = {}, 0
    for nm, s in zip(names, sizes):
        r[nm] = (lo, lo + s)
        lo += s
    kv = (r["k"][0], r["v"][1])
    return ((r["q"], r["z_att"], kv), (r["u_ssm"], r["z_ssm"]), (r["u_pool"], r["z_pool"]), (r["gates"],))


IN_GROUPS = _in_groups()


def _layer_fwd(x, lw, li, late=None, comm_attn=None, comm_ssm=None):
    tag = f"l{li}"
    h, (pa, ps, pp, pg), arrived = _ln_proj(x, lw["norm_g"], lw["shift"], lw["scale"], lw["w_in"], IN_GROUPS,
                                            name=f"ln_proj_{tag}", comm=None if late is None else late[0])
    if late is not None:
        lw = {**lw, **late[1](arrived)}
    ya, from_attn = _attn_fwd(pa, lw["sinks"], name=f"attn_fwd_{tag}", comm=comm_attn)
    ys, chk, states, *from_ssm = _ssm_fwd(ps, lw["lam"], lw["bblk"], lw["cblk"], lw["ssm_d"], lw["w_glu"], lw["b_glu"],
                                          name=f"ssm_fwd_{tag}", comm=comm_ssm)
    yp = _pool_fwd(pp, lw["w_pool"], lw["pool_scale"], name=f"pool_fwd_{tag}")
    x_new, merged, ba, bs, bp, out = _merge_fwd(ya, ys, yp, lw["w_br_att"], lw["w_br_ssm"], lw["w_br_pool"], pg,
                                                lw["w_out"], x, lw["gate"], name=f"merge_fwd_{tag}")
    saved = dict(x=x, h=h, pa=pa, ps=ps, pp=pp, pg=pg, ya=ya, ys=ys, yp=yp, chk=chk, states=states, merged=merged,
                 ba=ba, bs=bs, bp=bp, out=out)
    return x_new, saved, lw, list(from_attn), list(from_ssm)


def _layer_bwd(dx, lw, sv, li, later=None, own=None):
    tag = f"l{li}"
    g = {}
    merge_out, swapped = _merge_bwd(
        dx, sv["out"], lw["gate"], lw["w_out"], sv["pg"], (sv["ba"], sv["bs"], sv["bp"]),
        (lw["w_br_att"], lw["w_br_ssm"], lw["w_br_pool"]), (sv["ya"], sv["ys"], sv["yp"]), sv["merged"],
        name=f"merge_bwd_{tag}", comm=None if later is None else later[0])
    dya, dys, dyp, dpg, gate_sums, g["w_br_att"], g["w_br_ssm"], g["w_br_pool"], g["w_out"] = merge_out
    dpa, dsink = _attn_bwd(sv["pa"], lw["sinks"], dya, name=f"attn_bwd_{tag}")
    dps, db_dense, dc_dense, dwglu, ssm_sums, *exchanged = _ssm_bwd(
        sv["ps"], dys, sv["chk"], sv["states"], lw["lam"], lw["bblk"], lw["cblk"], lw["ssm_d"], lw["w_glu"], lw["b_glu"],
        name=f"ssm_bwd_{tag}", comm=None if later is None else later[1](swapped))
    g["w_glu"] = dwglu.astype(BF16)
    dpp, dwpool, pool_sums = _pool_bwd(sv["pp"], dyp, lw["w_pool"], lw["pool_scale"], name=f"pool_bwd_{tag}")
    h = sv["h"]
    dproj = (dpa, dps, dpp, dpg)
    g["w_in"], from_late = _mm_tn_grouped(h, dproj, IN_GROUPS, name=f"dw_in_{tag}",
                                          comm=None if own is None else own({k: g[k] for k in LATE_WEIGHTS}))
    dx_in, ln_sums, from_w_in = _ln_proj_bwd(dproj, lw["w_in"], IN_GROUPS, sv["x"], dx, lw["norm_g"], lw["scale"],
                                             name=f"ln_proj_bwd_{tag}",
                                             comm=None if own is None else own({"w_in": g["w_in"]}))
    g["dmod"] = jnp.concatenate([ln_sums[0], ln_sums[1], gate_sums[0]])
    g["norm_g"] = ln_sums[2]
    g["attn_sinks"] = dsink[:, 0]
    g["ssm_raw"] = _ssm_extract(db_dense, dc_dense, ssm_sums)
    g["ssm_d"] = ssm_sums[2, :SSM_W]
    g["b_glu"] = ssm_sums[3, :SSM_W]
    g["w_pool"] = dwpool
    g["pool_scale"] = pool_sums[0]
    return dx_in, g, exchanged, list(from_w_in) + list(from_late)


BIG_WEIGHTS = ("w_in", "w_glu", "w_br_att", "w_br_ssm", "w_br_pool", "w_out")
ROW_SHARDED = ("w_glu", "w_out")


LATE_WEIGHTS = BIG_WEIGHTS[1:]


def _full_weights(keys, gathered):
    full = {}
    for k, g in zip(keys, gathered):
        if k in ROW_SHARDED:
            full[k] = g.reshape(N_DEV * g.shape[1], g.shape[2])
        else:
            full[k] = g.transpose(1, 0, 2).reshape(g.shape[1], N_DEV * g.shape[2])
    return full


def _by_destination(keys, grads):
    out = []
    for k in keys:
        g = grads[k]
        if k in ROW_SHARDED:
            out.append(g.reshape(4, 2, g.shape[0] // N_DEV, g.shape[1]).transpose(1, 0, 2, 3))
        else:
            out.append(g.reshape(g.shape[0], 4, 2, g.shape[1] // N_DEV).transpose(2, 1, 0, 3))
    return out


def _prepare_layer(li, mod, norm_g, w_in_full, attn_sinks, disc, ssm_c_re, ssm_c_im, ssm_d, b_glu, w_pool, pool_scale):
    d = D_MODEL
    lr, li_, bbr, bbi = disc
    lam, bblk, cblk = _ssm_dense(lr[li], li_[li], bbr[li], bbi[li], ssm_c_re[li], ssm_c_im[li], name=f"ssm_dense_l{li}")
    return dict(
        norm_g=norm_g[li][None, :], shift=mod[li, :d][None, :], scale=mod[li, d:2 * d][None, :],
        gate=mod[li, 2 * d:][None, :], w_in=w_in_full,
        sinks=attn_sinks[li], lam=lam, bblk=bblk, cblk=cblk, ssm_d=ssm_d[li][None, :],
        b_glu=b_glu[li][None, :], w_pool=w_pool[li].astype(BF16), pool_scale=pool_scale[li][None, :])


SMALL_ROWS = 64
SMALL_ORDER = ("norm_g", "attn_sinks", "ssm_d", "b_glu", "w_pool", "pool_scale", "dmod")


def _pack_small(loss, dfinal_g, layer_grads):
    parts = [jnp.broadcast_to(loss.reshape(1), (128,)), dfinal_g]
    for g in layer_grads:
        for k in SMALL_ORDER:
            v = g[k].reshape(-1)
            if v.shape[0] % 128:
                v = jnp.pad(v, (0, 128 - v.shape[0] % 128))
            parts.append(v)
        for v in g["ssm_raw"]:
            parts.append(v.reshape(-1))
    flat = jnp.concatenate(parts)
    return jnp.pad(flat, (0, (-flat.shape[0]) % (SMALL_ROWS * 128))).reshape(-1, 128)


def _unpack_small(flat, shapes):
    out, off = [], 0
    for s in shapes:
        n = int(math.prod(s))
        out.append(flat[off:off + n].reshape(s))
        off += n + (-n) % 128
    return out


def kernel(x, c, norm_g, w_ada, b_ada, w_in, attn_sinks, ssm_a_re, ssm_a_im, ssm_log_dt, ssm_b_re, ssm_b_im, ssm_c_re, ssm_c_im, ssm_d, w_glu, b_glu, w_pool, pool_scale, w_br_att, w_br_ssm, w_br_pool, w_out, final_g, loss_target, m_norm_g, m_w_ada, m_b_ada, m_w_in, m_attn_sinks, m_ssm_a_re, m_ssm_a_im, m_ssm_log_dt, m_ssm_b_re, m_ssm_b_im, m_ssm_c_re, m_ssm_c_im, m_ssm_d, m_w_glu, m_b_glu, m_w_pool, m_pool_scale, m_w_br_att, m_w_br_ssm, m_w_br_pool, m_w_out, m_final_g, v_norm_g, v_w_ada, v_b_ada, v_w_in, v_attn_sinks, v_ssm_a_re, v_ssm_a_im, v_ssm_log_dt, v_ssm_b_re, v_ssm_b_im, v_ssm_c_re, v_ssm_c_im, v_ssm_d, v_w_glu, v_b_glu, v_w_pool, v_pool_scale, v_w_br_att, v_w_br_ssm, v_w_br_pool, v_w_out, v_final_g):
    me = 4 * lax.axis_index("x") + 2 * lax.axis_index("y") + lax.axis_index("c")
    d = D_MODEL
    ada_w = 3 * d // N_DEV

    (c_all,) = _exchange([c.reshape(8, 128)], scatter=False, name="gather_c")
    c_act = jax.nn.silu(c_all.reshape(N_DEV, d))
    b_cols = lax.dynamic_slice(b_ada, (0, me * ada_w), (DEPTH, ada_w))
    mod_part = jnp.concatenate(
        [_mm(c_act, w_ada[li], name=f"ada_fwd_l{li}") + b_cols[li][None, :] for li in range(DEPTH)], axis=0)
    (mod_all,) = _exchange([mod_part], scatter=False, name="gather_mod")
    mod_all = mod_all.reshape(N_DEV, DEPTH, N_DEV, ada_w)
    mod_mine = lax.dynamic_index_in_dim(mod_all, me, axis=2, keepdims=False)
    mod_mine = mod_mine.transpose(1, 0, 2).reshape(DEPTH, 3 * d)

    sharded = dict(w_in=w_in, w_glu=w_glu, w_br_att=w_br_att, w_br_ssm=w_br_ssm, w_br_pool=w_br_pool, w_out=w_out)
    shards = lambda li, keys: [sharded[k][li].astype(BF16) for k in keys]
    disc, disc_vjp = jax.vjp(jax.vmap(_ssm_discretize), ssm_a_re, ssm_a_im, ssm_log_dt, ssm_b_re, ssm_b_im)
    layer = lambda li, gathered_w_in: _prepare_layer(
        li, mod_mine, norm_g, _full_weights(("w_in",), gathered_w_in)["w_in"], attn_sinks, disc, ssm_c_re, ssm_c_im,
        ssm_d, b_glu, w_pool, pool_scale)
    late_weights = lambda gathered: _full_weights(LATE_WEIGHTS, gathered)
    core = lax.axis_index("c").astype(jnp.int32).reshape(1)

    def add_pairs(keys, by_dest, from_sibling, tag):
        flat = {k: (a.reshape(2, -1, a.shape[-1]), b.reshape(-1, b.shape[-1]))
                for k, a, b in zip(keys, by_dest, from_sibling)}
        small = [k for k in keys if k != "w_in"]
        sums = {}
        if "w_in" in flat:
            sums["w_in"] = _pair_add(*flat["w_in"], core, name=f"grads_pair_add_{tag}_w_in")
        if small:
            added = _pair_add_small([flat[k][0] for k in small], [flat[k][1] for k in small], core,
                                    name=f"grads_pair_add_{tag}_late")
            sums.update(zip(small, added))
        return [sums[k].reshape(b.shape) for k, b in zip(keys, from_sibling)]

    def chip_sums_of(keys, grads_li, tag):
        by_dest = _by_destination(keys, grads_li)
        return add_pairs(keys, by_dest, _run_plan(_sibling_swap_plan(by_dest), name=f"grads_sibling_swap_{tag}"), tag)

    layers, saved, grads = [None] * DEPTH, [None] * DEPTH, [None] * DEPTH
    layers[0] = layer(0, _run_plan(_gather_plan(shards(0, ("w_in",))), name="gather_w_in_l0"))
    xs, saved[0], layers[0], late1, w_in1 = _layer_fwd(
        x[0], layers[0], 0, late=(_gather_plan(shards(0, LATE_WEIGHTS)), late_weights),
        comm_attn=_gather_plan(shards(1, LATE_WEIGHTS)), comm_ssm=_gather_plan(shards(1, ("w_in",))))
    layers[1] = {**layer(1, w_in1), **late_weights(late1)}
    xs, saved[1], _, _, _ = _layer_fwd(xs, layers[1], 1)
    dx, fin_sums = _final_loss(xs, final_g[None, :], loss_target[0])
    loss_part = jnp.sum(fin_sums[1])
    dx, grads[1], _, _ = _layer_bwd(dx, layers[1], saved[1], 1)
    by_dest1 = _by_destination(BIG_WEIGHTS, grads[1])
    dx, grads[0], scattered1, scattered0 = _layer_bwd(
        dx, layers[0], saved[0], 0,
        later=(_sibling_swap_plan(by_dest1),
               lambda swapped: _chip_scatter_plan(add_pairs(BIG_WEIGHTS, by_dest1, swapped, "l1"))),
        own=lambda g: _chip_scatter_plan(chip_sums_of(tuple(g), g, "l0_" + "_".join(g))))
    big = list(zip(scattered0, scattered1))
    grad_x = dx[None]

    small = _pack_small(loss_part, fin_sums[0], grads)
    dmod_rows = jnp.concatenate([grads[li]["dmod"] for li in range(DEPTH)]).reshape(-1, 128)
    small_sum, dmod_gathered = _allreduce_small(small, dmod_rows, name="allreduce_small")
    out = {}

    def adam(name, w, g_slabs, m, v):
        shp = w.shape
        r = int(math.prod(shp[:-1])) if len(shp) > 1 else 1
        w2, m2, v2 = (a.reshape(r, shp[-1]) for a in (w, m, v))
        gs = [g.reshape(g.shape[0], r // len(g_slabs), shp[-1]) for g in g_slabs]
        res = _adamw(w2, gs, m2, v2, name=f"adamw_{name}")
        out[name] = tuple(a.reshape(shp) for a in res)

    flat = small_sum.reshape(-1)
    shapes = [(128,), (d,)]
    for _ in range(DEPTH):
        shapes += [(d,), (N_HEADS,), (SSM_W,), (SSM_W,), (4, POOL_GW, POOL_GW), (POOL_W,), (3 * d,),
                   (SSM_GROUPS, SSM_STATE), (SSM_GROUPS, SSM_STATE), (SSM_GROUPS, SSM_STATE, SSM_GROUP),
                   (SSM_GROUPS, SSM_STATE, SSM_GROUP), (SSM_GROUPS, SSM_GROUP, SSM_STATE), (SSM_GROUPS, SSM_GROUP, SSM_STATE)]
    un = _unpack_small(flat, shapes)
    loss = un[0][0]
    g_final_g = un[1]
    per = 13
    gl = [un[2 + li * per: 2 + (li + 1) * per] for li in range(DEPTH)]
    st = lambda j: jnp.stack([gl[li][j] for li in range(DEPTH)])
    g_norm_g, g_sinks, g_ssm_d, g_b_glu, g_w_pool, g_pool_scale, g_b_ada = (st(j) for j in range(7))
    d_lr, d_li, d_bbr, d_bbi, g_c_re, g_c_im = (st(j) for j in range(7, 13))
    g_a_re, g_a_im, g_log_dt, g_b_re, g_b_im = disc_vjp((d_lr, d_li, d_bbr, d_bbi))

    dmod_all = lax.dynamic_slice(dmod_gathered.reshape(N_DEV, DEPTH, 3 * d), (0, 0, me * ada_w), (N_DEV, DEPTH, ada_w))
    dmod_all = dmod_all.transpose(1, 0, 2)
    g_w_ada = jnp.stack([_mm_tn(c_act, dmod_all[li], tm=d, tn=ada_w, tk=N_DEV, name=f"dw_ada_l{li}") for li in range(DEPTH)])

    adam("w_ada", w_ada, [g_w_ada[None]], m_w_ada, v_w_ada)
    adam("w_in", w_in, big[0], m_w_in, v_w_in)
    adam("w_glu", w_glu, big[1], m_w_glu, v_w_glu)
    adam("w_br_att", w_br_att, big[2], m_w_br_att, v_w_br_att)
    adam("w_br_ssm", w_br_ssm, big[3], m_w_br_ssm, v_w_br_ssm)
    adam("w_br_pool", w_br_pool, big[4], m_w_br_pool, v_w_br_pool)
    adam("w_out", w_out, big[5], m_w_out, v_w_out)

    small_names = ["norm_g", "b_ada", "attn_sinks", "ssm_a_re", "ssm_a_im", "ssm_log_dt", "ssm_b_re", "ssm_b_im",
                   "ssm_c_re", "ssm_c_im", "ssm_d", "b_glu", "w_pool", "pool_scale", "final_g"]
    small_w = [norm_g, b_ada, attn_sinks, ssm_a_re, ssm_a_im, ssm_log_dt, ssm_b_re, ssm_b_im, ssm_c_re, ssm_c_im,
               ssm_d, b_glu, w_pool, pool_scale, final_g]
    small_m = [m_norm_g, m_b_ada, m_attn_sinks, m_ssm_a_re, m_ssm_a_im, m_ssm_log_dt, m_ssm_b_re, m_ssm_b_im,
               m_ssm_c_re, m_ssm_c_im, m_ssm_d, m_b_glu, m_w_pool, m_pool_scale, m_final_g]
    small_v = [v_norm_g, v_b_ada, v_attn_sinks, v_ssm_a_re, v_ssm_a_im, v_ssm_log_dt, v_ssm_b_re, v_ssm_b_im,
               v_ssm_c_re, v_ssm_c_im, v_ssm_d, v_b_glu, v_w_pool, v_pool_scale, v_final_g]
    small_g = [g_norm_g, g_b_ada, g_sinks, g_a_re, g_a_im, g_log_dt, g_b_re, g_b_im, g_c_re, g_c_im,
               g_ssm_d, g_b_glu, g_w_pool, g_pool_scale, g_final_g]

    for nm, w, g, m, v in zip(small_names, small_w, small_g, small_m, small_v):
        adam(nm, w, [g[None]], m, v)

    order = ["norm_g", "w_ada", "b_ada", "w_in", "attn_sinks", "ssm_a_re", "ssm_a_im", "ssm_log_dt", "ssm_b_re",
             "ssm_b_im", "ssm_c_re", "ssm_c_im", "ssm_d", "w_glu", "b_glu", "w_pool", "pool_scale", "w_br_att",
             "w_br_ssm", "w_br_pool", "w_out", "final_g"]
    return (loss, grad_x, *[out[k][0] for k in order], *[out[k][1] for k in order],
            *[out[k][2] for k in order], *[out[k][3] for k in order])
```

```python
import functools
import math

import jax
import jax.numpy as jnp
from jax import lax
from jax.experimental import pallas as pl
from jax.experimental.pallas import tpu as pltpu

F32 = jnp.float32
BF16 = jnp.bfloat16

N_DEV = 8
D_MODEL = 1024
DEPTH = 2
CHUNK = 64
N_HEADS = 8
N_KV_HEADS = 2
HEAD_DIM = 64
Q_PER_KV = N_HEADS // N_KV_HEADS
WINDOW = 128
ATT_W = 512
KV_W = 128
SSM_W = 512
SSM_GROUP = 16
SSM_GROUPS = 32
SSM_STATE = 64
SSM_N = SSM_GROUPS * SSM_STATE
POOL_W = 512
POOL_WINDOWS = (2, 4, 8, 16)
POOL_GW = 128
POOL_HALO = 16
EPS = 1e-6
NEG_INF = -1e30
ADAM_LR = 0.001
ADAM_B1 = 0.9
ADAM_B2 = 0.999
ADAM_EPS = 1e-08
ADAM_WD = 0.01
ADAM_STEP = 10

SEQ_BLOCK = 256
ATT_BLOCK = 128
VMEM_LIMIT = 56 * 1024 * 1024

NN = (((1,), (0,)), ((), ()))
NT = (((1,), (1,)), ((), ()))
TN = (((0,), (0,)), ((), ()))


def _dot(a, b, dims=NN):
    return lax.dot_general(a.astype(BF16), b.astype(BF16), dims, preferred_element_type=F32)


def _params(*sem):
    return pltpu.CompilerParams(dimension_semantics=sem, vmem_limit_bytes=VMEM_LIMIT)


def _sigmoid(x):
    return 0.5 + 0.5 * jnp.tanh(0.5 * x)


def _silu_and_grad(z):
    s = _sigmoid(z)
    return z * s, s * (1.0 + z * (1.0 - s))


_GELU_K = math.sqrt(2.0 / math.pi)


def _gelu_and_grad(x):
    inner = _GELU_K * (x + 0.044715 * x * x * x)
    t = jnp.tanh(inner)
    val = 0.5 * x * (1.0 + t)
    grad = 0.5 * (1.0 + t) + 0.5 * x * (1.0 - t * t) * _GELU_K * (1.0 + 3.0 * 0.044715 * x * x)
    return val, grad


def _grouped_pieces(groups):
    out = []
    for ranges in groups:
        off, pieces = 0, []
        for lo, hi in ranges:
            pieces.append((off, lo, hi))
            off += hi - lo
        out.append(pieces)
    return out


def _mm_tn(a, b, *, out_dtype=F32, tm=1024, tn=1024, tk=1024, name, comm=None):
    k, m = a.shape
    n = b.shape[1]
    assert m % min(tm, m) == 0 and n % min(tn, n) == 0 and k % min(tk, k) == 0
    tm, tn, tk = min(tm, m), min(tn, n), min(tk, k)
    nk = k // tk
    grid = (m // tm, n // tn, nk)
    c_args, c_in, c_out, c_shape, c_scratch = _comm_extra(comm)

    def body(a_ref, b_ref, o_ref, acc_ref):
        kk = pl.program_id(2)

        @pl.when(kk == 0)
        def _():
            acc_ref[...] = jnp.zeros_like(acc_ref)

        acc_ref[...] += _dot(a_ref[...], b_ref[...], TN)

        @pl.when(kk == nk - 1)
        def _():
            o_ref[...] = acc_ref[...].astype(out_dtype)

    res = pl.pallas_call(
        _with_comm(body, comm, 2, 1, grid, -1), grid=grid,
        in_specs=[pl.BlockSpec((tk, tm), lambda i, j, kk: (kk, i)), pl.BlockSpec((tk, tn), lambda i, j, kk: (kk, j))] + c_in,
        out_specs=[pl.BlockSpec((tm, tn), lambda i, j, kk: (i, j))] + c_out,
        out_shape=[jax.ShapeDtypeStruct((m, n), out_dtype)] + c_shape,
        scratch_shapes=[pltpu.VMEM((tm, tn), F32)] + c_scratch,
        compiler_params=_params(*(("arbitrary",) * 3 if comm else ("parallel", "parallel", "arbitrary"))),
        name=name)(a, b, *c_args)
    return (res[0], list(res[1:])) if comm else res[0]


def _mm_tn_grouped(a, bs, groups, *, tm=512, tk=512, name, comm=None):
    k, m = a.shape
    tm, tk = min(tm, m), min(tk, k)
    assert m % tm == 0 and k % tk == 0
    nk, nb = k // tk, len(bs)
    pieces = _grouped_pieces(groups)
    n = sum(b.shape[1] for b in bs)
    grid = (m // tm, nk)
    c_args, c_in, c_out, c_shape, c_scratch = _comm_extra(comm)

    def body(a_ref, *refs):
        b_refs, o_ref, acc_refs = refs[:nb], refs[nb], refs[nb + 1:]
        kk = pl.program_id(1)
        av = a_ref[...]
        for b_ref, acc_ref, plist in zip(b_refs, acc_refs, pieces):
            @pl.when(kk == 0)
            def _():
                acc_ref[...] = jnp.zeros_like(acc_ref)

            acc_ref[...] += _dot(av, b_ref[...], TN)

            @pl.when(kk == nk - 1)
            def _():
                for off, lo, hi in plist:
                    o_ref[:, lo:hi] = acc_ref[:, off:off + hi - lo].astype(BF16)

    res = pl.pallas_call(
        _with_comm(body, comm, 1 + nb, 1, grid, -1), grid=grid,
        in_specs=[pl.BlockSpec((tk, tm), lambda i, kk: (kk, i))]
                 + [pl.BlockSpec((tk, b.shape[1]), lambda i, kk: (kk, 0)) for b in bs] + c_in,
        out_specs=[pl.BlockSpec((tm, n), lambda i, kk: (i, 0))] + c_out,
        out_shape=[jax.ShapeDtypeStruct((m, n), BF16)] + c_shape,
        scratch_shapes=[pltpu.VMEM((tm, b.shape[1]), F32) for b in bs] + c_scratch,
        compiler_params=_params("arbitrary", "arbitrary"), name=name)(a, *bs, *c_args)
    return res[0], list(res[1:])


def _ln_proj(x, g, shift, scale, w, groups, *, name, tm=512, comm=None):
    l, d = x.shape
    tm = min(tm, l)
    nb = l // tm
    pieces = _grouped_pieces(groups)
    widths = [sum(hi - lo for _, lo, hi in plist) for plist in pieces]
    nw = len(pieces)
    c_args, c_in, c_out, c_shape, c_scratch = _comm_extra(comm)

    def body(x_ref, g_ref, sh_ref, sc_ref, w_ref, h_ref, *p_refs):
        xv = x_ref[...]
        n = xv * lax.rsqrt(jnp.mean(xv * xv, axis=-1, keepdims=True) + EPS)
        h = ((n * g_ref[...]) * (1.0 + sc_ref[...]) + sh_ref[...]).astype(BF16)
        h_ref[...] = h
        for p_ref, plist in zip(p_refs, pieces):
            for off, lo, hi in plist:
                p_ref[:, off:off + hi - lo] = _dot(h, w_ref[:, lo:hi]).astype(BF16)

    vec = pl.BlockSpec((1, d), lambda i: (0, 0))
    row = lambda n: pl.BlockSpec((tm, n), lambda i: (i, 0))
    res = pl.pallas_call(
        _with_comm(body, comm, 5, 1 + nw, nb, -1), grid=(nb,),
        in_specs=[row(d), vec, vec, vec, pl.BlockSpec(w.shape, lambda i: (0, 0))] + c_in,
        out_specs=[row(d)] + [row(n) for n in widths] + c_out,
        out_shape=[jax.ShapeDtypeStruct((l, d), BF16)] + [jax.ShapeDtypeStruct((l, n), BF16) for n in widths] + c_shape,
        scratch_shapes=c_scratch,
        compiler_params=_params("arbitrary"), name=name)(x, g, shift, scale, w, *c_args)
    return res[0], list(res[1:1 + nw]), list(res[1 + nw:])


def _ln_proj_bwd(ds, w, groups, x, dres, g, scale, *, name, tm=256, comm=None):
    l, d = x.shape
    tm = min(tm, l)
    nb = l // tm
    nd = len(ds)
    pieces = _grouped_pieces(groups)
    c_args, c_in, c_out, c_shape, c_scratch = _comm_extra(comm)

    def body(*refs):
        d_refs = refs[:nd]
        w_ref, x_ref, dres_ref, g_ref, sc_ref, dx_ref, sums_ref = refs[nd:]
        dhv = None
        for d_ref, plist in zip(d_refs, pieces):
            for off, lo, hi in plist:
                term = _dot(d_ref[:, off:off + hi - lo], w_ref[:, lo:hi], NT)
                dhv = term if dhv is None else dhv + term
        xv = x_ref[...]
        rstd = lax.rsqrt(jnp.mean(xv * xv, axis=-1, keepdims=True) + EPS)
        n = xv * rstd
        gv = g_ref[...]
        dr = dhv * (1.0 + sc_ref[...])
        dn = dr * gv
        dx_ref[...] = dres_ref[...] + rstd * (dn - n * jnp.mean(dn * n, axis=-1, keepdims=True))

        @pl.when(pl.program_id(0) == 0)
        def _():
            sums_ref[...] = jnp.zeros_like(sums_ref)

        sums_ref[0:1, :] += jnp.sum(dhv, axis=0, keepdims=True)
        sums_ref[1:2, :] += jnp.sum(dhv * (n * gv), axis=0, keepdims=True)
        sums_ref[2:3, :] += jnp.sum(dr * n, axis=0, keepdims=True)

    vec = pl.BlockSpec((1, d), lambda i: (0, 0))
    row = pl.BlockSpec((tm, d), lambda i: (i, 0))
    res = pl.pallas_call(
        _with_comm(body, comm, nd + 5, 2, nb, -1), grid=(nb,),
        in_specs=[pl.BlockSpec((tm, a.shape[1]), lambda i: (i, 0)) for a in ds]
                 + [pl.BlockSpec(w.shape, lambda i: (0, 0)), row, row, vec, vec] + c_in,
        out_specs=[row, pl.BlockSpec((8, d), lambda i: (0, 0))] + c_out,
        out_shape=[jax.ShapeDtypeStruct((l, d), F32), jax.ShapeDtypeStruct((8, d), F32)] + c_shape,
        scratch_shapes=c_scratch,
        compiler_params=_params("arbitrary"), name=name)(*ds, w, x, dres, g, scale, *c_args)
    return res[0], res[1], list(res[2:])


def _final_loss(x, g, target, *, tm=512):
    l, d = x.shape

    def body(x_ref, g_ref, t_ref, dx_ref, sums_ref):
        xv = x_ref[...]
        rstd = lax.rsqrt(jnp.mean(xv * xv, axis=-1, keepdims=True) + EPS)
        n = xv * rstd
        gv = g_ref[...]
        err = n * gv - t_ref[...]
        dy = err * (1.0 / d)
        dn = dy * gv
        dx_ref[...] = rstd * (dn - n * jnp.mean(dn * n, axis=-1, keepdims=True))

        @pl.when(pl.program_id(0) == 0)
        def _():
            sums_ref[...] = jnp.zeros_like(sums_ref)

        sums_ref[0:1, :] += jnp.sum(dy * n, axis=0, keepdims=True)
        sums_ref[1:2, :] += jnp.sum(err * err, axis=0, keepdims=True) * (0.5 / d)

    vec = pl.BlockSpec((1, d), lambda i: (0, 0))
    row = pl.BlockSpec((tm, d), lambda i: (i, 0))
    dx, sums = pl.pallas_call(
        body, grid=(l // tm,),
        in_specs=[row, vec, row],
        out_specs=[row, pl.BlockSpec((8, d), lambda i: (0, 0))],
        out_shape=[jax.ShapeDtypeStruct((l, d), F32), jax.ShapeDtypeStruct((8, d), F32)],
        compiler_params=_params("arbitrary"), name="final_loss")(x, g, target)
    return dx, sums


def _attn_geometry(i, t):
    nk = t + WINDOW
    qi = lax.broadcasted_iota(jnp.int32, (t, nk), 0)
    kj = lax.broadcasted_iota(jnp.int32, (t, nk), 1)
    dist = jnp.abs(qi + WINDOW - kj).astype(F32)
    qc = jnp.right_shift(qi, 6)
    kc = jnp.right_shift(kj, 6)
    valid = (kc >= qc) & (kc <= qc + WINDOW // CHUNK) & ((i > 0) | (kj >= WINDOW))
    return dist, valid


def _attn_head(q, k_all, v_all, sink, slope, dist, valid):
    s = _dot(q, k_all, NT) * (1.0 / math.sqrt(HEAD_DIM)) - slope * dist
    s = jnp.where(valid, s, NEG_INF)
    m = jnp.maximum(jnp.max(s, axis=-1, keepdims=True), sink)
    e = jnp.exp(s - m)
    es = jnp.exp(sink - m)
    inv = 1.0 / (jnp.sum(e, axis=-1, keepdims=True) + es)
    p = e * inv
    o = _dot(p, v_all, NN)
    return p, o, es * inv


def _attn_specs(t):
    cur = pl.BlockSpec((t, ATT_W * 2 + KV_W * 2), lambda i: (i, 0))
    halo_blocks = t // WINDOW
    prev = pl.BlockSpec((WINDOW, 2 * KV_W), lambda i: (jnp.maximum(i * halo_blocks - 1, 0), (2 * ATT_W) // (2 * KV_W)))
    return cur, prev


def _attn_fwd(pa, sinks, *, name, t=ATT_BLOCK, comm=None):
    l = pa.shape[0]
    t = min(t, l)
    nb = l // t
    c_args, c_in, c_out, c_shape, c_scratch = _comm_extra(comm)

    def body(sink_ref, cur_ref, prev_ref, ya_ref):
        i = pl.program_id(0)
        dist, valid = _attn_geometry(i, t)
        for h in range(N_HEADS):
            kh = h // Q_PER_KV
            q = cur_ref[:, h * HEAD_DIM:(h + 1) * HEAD_DIM]
            z = cur_ref[:, ATT_W + h * HEAD_DIM:ATT_W + (h + 1) * HEAD_DIM].astype(F32)
            k_all = jnp.concatenate([prev_ref[:, kh * HEAD_DIM:(kh + 1) * HEAD_DIM],
                                     cur_ref[:, 2 * ATT_W + kh * HEAD_DIM:2 * ATT_W + (kh + 1) * HEAD_DIM]], axis=0)
            v_all = jnp.concatenate([prev_ref[:, KV_W + kh * HEAD_DIM:KV_W + (kh + 1) * HEAD_DIM],
                                     cur_ref[:, 2 * ATT_W + KV_W + kh * HEAD_DIM:2 * ATT_W + KV_W + (kh + 1) * HEAD_DIM]], axis=0)
            _, o, _ = _attn_head(q, k_all, v_all, sink_ref[h], 2.0 ** (-(h + 1)), dist, valid)
            sz, _ = _silu_and_grad(z)
            ya_ref[:, h * HEAD_DIM:(h + 1) * HEAD_DIM] = (o * sz).astype(BF16)

    cur, prev = _attn_specs(t)
    res = pl.pallas_call(
        _with_comm(body, comm, 3, 1, nb, nb - 1), grid=(nb,),
        in_specs=[pl.BlockSpec(memory_space=pltpu.SMEM), cur, prev] + c_in,
        out_specs=[pl.BlockSpec((t, ATT_W), lambda i: (i, 0))] + c_out,
        out_shape=[jax.ShapeDtypeStruct((l, ATT_W), BF16)] + c_shape,
        scratch_shapes=c_scratch,
        compiler_params=_params("arbitrary"), name=name)(sinks, pa, pa, *c_args)
    return res[0], res[1:]


def _attn_bwd(pa, sinks, dya, *, name, t=SEQ_BLOCK):
    l = pa.shape[0]
    t = min(t, l)
    nb = l // t
    scale = 1.0 / math.sqrt(HEAD_DIM)

    def body(sink_ref, cur_ref, prev_ref, dya_ref, dpa_ref, dsink_ref, carry_ref):
        n = pl.program_id(0)
        i = nb - 1 - n
        dist, valid = _attn_geometry(i, t)

        @pl.when(n == 0)
        def _():
            carry_ref[...] = jnp.zeros_like(carry_ref)
            dsink_ref[...] = jnp.zeros_like(dsink_ref)

        dk_acc = [jnp.zeros((HEAD_DIM, t + WINDOW), F32) for _ in range(N_KV_HEADS)]
        dv_acc = [jnp.zeros((HEAD_DIM, t + WINDOW), F32) for _ in range(N_KV_HEADS)]
        for h in range(N_HEADS):
            kh = h // Q_PER_KV
            q = cur_ref[:, h * HEAD_DIM:(h + 1) * HEAD_DIM]
            z = cur_ref[:, ATT_W + h * HEAD_DIM:ATT_W + (h + 1) * HEAD_DIM].astype(F32)
            k_all = jnp.concatenate([prev_ref[:, kh * HEAD_DIM:(kh + 1) * HEAD_DIM],
                                     cur_ref[:, 2 * ATT_W + kh * HEAD_DIM:2 * ATT_W + (kh + 1) * HEAD_DIM]], axis=0)
            v_all = jnp.concatenate([prev_ref[:, KV_W + kh * HEAD_DIM:KV_W + (kh + 1) * HEAD_DIM],
                                     cur_ref[:, 2 * ATT_W + KV_W + kh * HEAD_DIM:2 * ATT_W + KV_W + (kh + 1) * HEAD_DIM]], axis=0)
            p, o, p_sink = _attn_head(q, k_all, v_all, sink_ref[h], 2.0 ** (-(h + 1)), dist, valid)
            dy = dya_ref[:, h * HEAD_DIM:(h + 1) * HEAD_DIM]
            sz, dsz = _silu_and_grad(z)
            do = dy * sz
            dpa_ref[:, ATT_W + h * HEAD_DIM:ATT_W + (h + 1) * HEAD_DIM] = (dy * o * dsz).astype(BF16)
            dp = _dot(do, v_all, NT)
            delta = jnp.sum(p * dp, axis=-1, keepdims=True)
            ds = p * (dp - delta)
            dpa_ref[:, h * HEAD_DIM:(h + 1) * HEAD_DIM] = (_dot(ds, k_all, NN) * scale).astype(BF16)
            dk_acc[kh] = dk_acc[kh] + _dot(q, ds, TN) * scale
            dv_acc[kh] = dv_acc[kh] + _dot(do, p, TN)
            dsink_ref[h:h + 1, :] += jnp.broadcast_to(-jnp.sum(p_sink * delta, axis=0, keepdims=True), (1, 128))

        acc = jnp.concatenate(dk_acc + dv_acc, axis=0).T
        own = acc[WINDOW:, :]
        tail = own[t - WINDOW:, :] + carry_ref[...]
        if t > WINDOW:
            dpa_ref[0:t - WINDOW, 2 * ATT_W:] = own[:t - WINDOW, :].astype(BF16)
        dpa_ref[t - WINDOW:t, 2 * ATT_W:] = tail.astype(BF16)
        carry_ref[...] = acc[:WINDOW, :]

    halo_blocks = t // WINDOW
    wpa = 2 * ATT_W + 2 * KV_W
    cur = pl.BlockSpec((t, wpa), lambda n: (nb - 1 - n, 0))
    prev = pl.BlockSpec((WINDOW, 2 * KV_W),
                        lambda n: (jnp.maximum((nb - 1 - n) * halo_blocks - 1, 0), (2 * ATT_W) // (2 * KV_W)))
    return pl.pallas_call(
        body, grid=(nb,),
        in_specs=[pl.BlockSpec(memory_space=pltpu.SMEM), cur, prev, pl.BlockSpec((t, ATT_W), lambda n: (nb - 1 - n, 0))],
        out_specs=[pl.BlockSpec((t, wpa), lambda n: (nb - 1 - n, 0)), pl.BlockSpec((8, 128), lambda n: (0, 0))],
        out_shape=[jax.ShapeDtypeStruct((l, wpa), BF16), jax.ShapeDtypeStruct((8, 128), F32)],
        scratch_shapes=[pltpu.VMEM((WINDOW, 2 * KV_W), F32)],
        compiler_params=_params("arbitrary"), name=name)(sinks, pa, pa, dya)


def _scan(xr, xi, lr, li, t, reverse):
    row = lax.broadcasted_iota(jnp.int32, (t, 1), 0)
    d = 1
    pr, pi = lr, li
    while d < t:
        if reverse:
            sr = jnp.where(row < t - d, pltpu.roll(xr, t - d, 0), 0.0)
            si = jnp.where(row < t - d, pltpu.roll(xi, t - d, 0), 0.0)
        else:
            sr = jnp.where(row >= d, pltpu.roll(xr, d, 0), 0.0)
            si = jnp.where(row >= d, pltpu.roll(xi, d, 0), 0.0)
        xr, xi = xr + pr * sr - pi * si, xi + pr * si + pi * sr
        pr, pi = pr * pr - pi * pi, 2.0 * pr * pi
        d *= 2
    return xr, xi


SCAN_SUB = 8


def _split_hi_lo(a):
    hi = a.astype(BF16)
    lo = (a - hi.astype(F32)).astype(BF16)
    return jnp.concatenate([hi, lo], axis=0)


def _scan_mxu(xr, xi, tab, lam3, lam8, tri, expand, cr, ci, t, reverse):
    ns = t // SCAN_SUB
    n = xr.shape[1]
    v3 = lambda a: a.reshape(ns, SCAN_SUB, n)
    x3r, x3i = v3(xr), v3(xi)
    br = (x3r * tab[0] - x3i * tab[1]).reshape(t, n)
    bi = (x3r * tab[1] + x3i * tab[0]).reshape(t, n)
    pm = jnp.dot(tri, jnp.concatenate([br, bi], axis=1).astype(BF16), preferred_element_type=F32)
    p3r, p3i = v3(pm[:t, :n]), v3(pm[:t, n:])
    slr = p3r * tab[2] - p3i * tab[3]
    sli = p3r * tab[3] + p3i * tab[2]
    totr, toti = pm[t:, :n], pm[t:, n:]
    l3r, l3i = lam3
    l8r, l8i = lam8
    row = lax.broadcasted_iota(jnp.int32, (ns, 1), 0)
    edge = row == (ns - 1 if reverse else 0)
    er = totr * l3r - toti * l3i + jnp.where(edge, l8r * cr - l8i * ci, 0.0)
    ei = totr * l3i + toti * l3r + jnp.where(edge, l8r * ci + l8i * cr, 0.0)
    er, ei = _scan(er, ei, l8r, l8i, ns, reverse)
    shift = ns - 1 if reverse else 1
    nbr = jnp.where(edge, cr, pltpu.roll(er, shift, 0))
    nbi = jnp.where(edge, ci, pltpu.roll(ei, shift, 0))
    ex = jnp.dot(expand, _split_hi_lo(jnp.concatenate([nbr, nbi], axis=1)), preferred_element_type=F32)
    e3r, e3i = v3(ex[:, :n]), v3(ex[:, n:])
    sr = (slr + e3r * tab[4] - e3i * tab[5]).reshape(t, n)
    si = (sli + e3r * tab[5] + e3i * tab[4]).reshape(t, n)
    out = 0 if reverse else ns - 1
    return sr, si, er[out:out + 1, :], ei[out:out + 1, :]


def _scan_consts(t):
    import numpy as np
    ns = t // SCAN_SUB
    r = np.arange(t)
    same = (r[:, None] // SCAN_SUB) == (r[None, :] // SCAN_SUB)
    sums = (np.arange(ns)[:, None] == (r[None, :] // SCAN_SUB))
    tri = []
    for keep in (r[None, :] <= r[:, None], r[None, :] >= r[:, None]):
        tri.append(np.concatenate([same & keep, sums], axis=0).astype(np.float32))
    ex = ((r[:, None] // SCAN_SUB) == np.arange(ns)[None, :]).astype(np.float32)
    return jnp.asarray(np.stack(tri), BF16), jnp.asarray(np.concatenate([ex, ex], axis=1), BF16)


def _scan_tables(lr, li):
    import numpy as np
    den = lr * lr + li * li
    ir, ii = lr / den, -li / den
    mul = lambda a, b: (a[0] * b[0] - a[1] * b[1], a[0] * b[1] + a[1] * b[0])
    pw = {0: (jnp.ones_like(lr), jnp.zeros_like(lr))}
    for e in range(1, 9):
        pw[e] = mul(pw[e - 1], (lr, li))
    for e in range(-1, -5, -1):
        pw[e] = mul(pw[e + 1], (ir, ii))
    powers = jnp.stack([jnp.stack(pw[e]) for e in range(-4, 9)] + [jnp.zeros((2, lr.shape[0]), F32)])
    j = np.arange(SCAN_SUB)
    exps = [4 - j, j - 4, j + 1, j - 3, 3 - j, 8 - j]
    e_idx = np.stack([exps[t] + 4 for t in range(6) for _ in range(2)])
    c_idx = np.tile(np.array([0, 1])[:, None], (6, SCAN_SUB))
    sign = np.where((c_idx == 1) & (np.arange(12)[:, None] >= 6), -1.0, 1.0).astype(np.float32)
    tabs = powers[e_idx, c_idx] * sign[:, :, None]
    lam = powers[np.array([5, 5, 7, 7, 12, 12, 13, 13]), np.array([0, 1, 0, 1, 0, 1, 0, 0])]
    return lam, tabs


SSM_HALVES = 2
SSM_HW = SSM_W // SSM_HALVES
SSM_HN = SSM_N // SSM_HALVES


def _bd_nn(x, w):
    a = w.shape[1]
    return jnp.concatenate([_dot(x[:, h * a:(h + 1) * a], w[h]) for h in range(SSM_HALVES)], axis=1)


def _bd_nt(x, w):
    b = w.shape[2]
    return jnp.concatenate([_dot(x[:, h * b:(h + 1) * b], w[h], NT) for h in range(SSM_HALVES)], axis=1)


def _bd_tn(x, y):
    a, b = x.shape[1] // SSM_HALVES, y.shape[1] // SSM_HALVES
    return jnp.stack([_dot(x[:, h * a:(h + 1) * a], y[:, h * b:(h + 1) * b], TN) for h in range(SSM_HALVES)])


def _ssm_states(u, s0r, s0i, lam_ref, tab_ref, tri_ref, ex_ref, bre, bim, t):
    tab = tuple(tab_ref[k] for k in range(6))
    return _scan_mxu(_bd_nn(u, bre), _bd_nn(u, bim), tab, (lam_ref[2:3, :], lam_ref[3:4, :]),
                     (lam_ref[4:5, :], lam_ref[5:6, :]), tri_ref[0], ex_ref[...], s0r, s0i, t, False)


def _ssm_head(u, z, xr, xi, cre, cim, dskip, wglu, bglu):
    y = _bd_nn(xr, cre) - _bd_nn(xi, cim) + dskip * u
    y2, dgelu = _gelu_and_grad(y)
    gate = _sigmoid(_dot(y2, wglu) + bglu)
    y3 = y2 * gate
    return y2, dgelu, gate, y3


def _with_comm(body, comm, n_in, n_out, grid, mid_step):
    if comm is None:
        return body
    nc = len(comm["args"])
    n_sem = len(comm["scratch"])
    grid = (grid,) if isinstance(grid, int) else tuple(grid)
    total = math.prod(grid)

    def hosted(*refs):
        ins, cin = refs[:n_in], refs[n_in:n_in + nc]
        outs, cout = refs[n_in + nc:n_in + nc + n_out], refs[n_in + nc + n_out:n_in + 2 * nc + n_out]
        rest = refs[n_in + 2 * nc + n_out:]
        scratch, csem = rest[:len(rest) - n_sem], rest[len(rest) - n_sem:]
        start, forward, finish = comm["phases"](cin, cout, *csem)
        step = pl.program_id(0)
        for axis in range(1, len(grid)):
            step = step * grid[axis] + pl.program_id(axis)
        pl.when(step == 0)(start)
        pl.when(step == (mid_step if mid_step >= 0 else total + mid_step))(forward)
        body(*ins, *outs, *scratch)
        pl.when(step == total - 1)(finish)

    return hosted


def _comm_extra(comm):
    if comm is None:
        return [], [], [], [], []
    anyspec = pl.BlockSpec(memory_space=pl.ANY)
    nc = len(comm["args"])
    return comm["args"], [anyspec] * nc, [anyspec] * nc, comm["out_shape"], comm["scratch"]


def _ssm_fwd(ps, scan_ops, bblk, cblk, dskip, wglu, bglu, *, name, t=SEQ_BLOCK, comm=None):
    l = ps.shape[0]
    assert l % t == 0
    nb = l // t
    ns = t // SCAN_SUB
    c_args, c_in, c_out, c_shape, c_scratch = _comm_extra(comm)

    def body(ps_ref, lam_ref, tab_ref, tri_ref, ex_ref, b_ref, c_ref, d_ref, w_ref, bg_ref, ys_ref, chk_ref, xs_ref,
             st_ref):
        @pl.when(pl.program_id(0) == 0)
        def _():
            st_ref[...] = jnp.zeros_like(st_ref)

        chk_ref[...] = jnp.broadcast_to(st_ref[...], chk_ref.shape)
        u = ps_ref[:, :SSM_W].astype(F32)
        z = ps_ref[:, SSM_W:].astype(F32)
        xr, xi, er, ei = _ssm_states(u, st_ref[:, :SSM_N], st_ref[:, SSM_N:], lam_ref, tab_ref, tri_ref, ex_ref,
                                     b_ref[0], b_ref[1], t)
        st_ref[:, :SSM_N] = er
        st_ref[:, SSM_N:] = ei
        xr, xi = xr.astype(BF16), xi.astype(BF16)
        xs_ref[:, :SSM_N] = xr
        xs_ref[:, SSM_N:] = xi
        _, _, _, y3 = _ssm_head(u, z, xr, xi, c_ref[0], c_ref[1], d_ref[...], w_ref[...], bg_ref[...])
        sz, _ = _silu_and_grad(z)
        ys_ref[...] = (y3 * sz).astype(BF16)

    full = lambda shape: pl.BlockSpec(shape, lambda i: (0,) * len(shape))
    return pl.pallas_call(
        _with_comm(body, comm, 10, 3, nb, nb - 1), grid=(nb,),
        in_specs=[pl.BlockSpec((t, 2 * SSM_W), lambda i: (i, 0)), full((8, SSM_N)), full((12, SCAN_SUB, SSM_N)),
                  full((2, t + ns, t)), full((t, 2 * ns)), full((2, SSM_HALVES, SSM_HW, SSM_HN)),
                  full((2, SSM_HALVES, SSM_HN, SSM_HW)), full((1, SSM_W)), full((SSM_W, SSM_W)), full((1, SSM_W))] + c_in,
        out_specs=[pl.BlockSpec((t, SSM_W), lambda i: (i, 0)), pl.BlockSpec((8, 2 * SSM_N), lambda i: (i, 0)),
                   pl.BlockSpec((t, 2 * SSM_N), lambda i: (i, 0))] + c_out,
        out_shape=[jax.ShapeDtypeStruct((l, SSM_W), BF16), jax.ShapeDtypeStruct((nb * 8, 2 * SSM_N), F32),
                   jax.ShapeDtypeStruct((l, 2 * SSM_N), BF16)] + c_shape,
        scratch_shapes=[pltpu.VMEM((1, 2 * SSM_N), F32)] + c_scratch,
        compiler_params=_params("arbitrary"), name=name)(ps, *scan_ops, bblk, cblk, dskip, wglu, bglu, *c_args)


def _ssm_bwd(ps, dys, chk, states, scan_ops, bblk, cblk, dskip, wglu, bglu, *, name, t=SEQ_BLOCK, comm=None):
    l = ps.shape[0]
    assert l % t == 0
    nb = l // t
    ns = t // SCAN_SUB
    c_args, c_in, c_out, c_shape, c_scratch = _comm_extra(comm)

    def body(ps_ref, dys_ref, chk_ref, xs_ref, lam_ref, tab_ref, tri_ref, ex_ref, b_ref, c_ref, d_ref, w_ref, bg_ref,
             dps_ref, db_ref, dc_ref, dw_acc, sums_acc, gc_ref, db_acc, dc_acc):
        n = pl.program_id(0)

        @pl.when(n == 0)
        def _():
            gc_ref[...] = jnp.zeros_like(gc_ref)
            db_acc[...] = jnp.zeros_like(db_acc)
            dc_acc[...] = jnp.zeros_like(dc_acc)
            dw_acc[...] = jnp.zeros_like(dw_acc)
            sums_acc[...] = jnp.zeros_like(sums_acc)

        row = lax.broadcasted_iota(jnp.int32, (t, 1), 0)
        u = ps_ref[:, :SSM_W].astype(F32)
        z = ps_ref[:, SSM_W:].astype(F32)
        s0r, s0i = chk_ref[0:1, :SSM_N], chk_ref[0:1, SSM_N:]
        xr, xi = xs_ref[:, :SSM_N], xs_ref[:, SSM_N:]
        dskip = d_ref[...]
        y2, dgelu, gate, y3 = _ssm_head(u, z, xr, xi, c_ref[0], c_ref[1], dskip, w_ref[...], bg_ref[...])
        sz, dsz = _silu_and_grad(z)
        dys_v = dys_ref[...]
        dps_ref[:, SSM_W:] = (dys_v * y3 * dsz).astype(BF16)
        dy3 = dys_v * sz
        da = dy3 * y2 * gate * (1.0 - gate)
        dy2 = dy3 * gate + _dot(da, w_ref[...], NT)
        dw_acc[...] += _dot(y2, da, TN)
        dy = dy2 * dgelu
        sums_acc[2:3, :SSM_W] += jnp.sum(dy * u, axis=0, keepdims=True)
        sums_acc[3:4, :SSM_W] += jnp.sum(da, axis=0, keepdims=True)
        dc_acc[0] += _bd_tn(dy, xr)
        dc_acc[1] += -_bd_tn(dy, xi)
        rev_tab = tuple(tab_ref[k] for k in range(6, 12))
        gr, gi, gcr, gci = _scan_mxu(
            _bd_nt(dy, c_ref[0]), -_bd_nt(dy, c_ref[1]), rev_tab, (lam_ref[2:3, :], -lam_ref[3:4, :]),
            (lam_ref[4:5, :], -lam_ref[5:6, :]), tri_ref[1], ex_ref[...], gc_ref[:, :SSM_N], gc_ref[:, SSM_N:], t, True)
        gc_ref[:, :SSM_N] = gcr
        gc_ref[:, SSM_N:] = gci
        db_acc[0] += _bd_tn(u, gr)
        db_acc[1] += _bd_tn(u, gi)
        du = dskip * dy + _bd_nt(gr, b_ref[0]) + _bd_nt(gi, b_ref[1])
        dps_ref[:, :SSM_W] = du.astype(BF16)
        spr = jnp.where(row == 0, s0r, pltpu.roll(xr.astype(F32), 1, 0))
        spi = jnp.where(row == 0, s0i, pltpu.roll(xi.astype(F32), 1, 0))
        sums_acc[0:1, :] += jnp.sum(gr * spr + gi * spi, axis=0, keepdims=True)
        sums_acc[1:2, :] += jnp.sum(gi * spr - gr * spi, axis=0, keepdims=True)

        @pl.when(n == nb - 1)
        def _():
            per_half = SSM_GROUPS // SSM_HALVES
            for k in range(2):
                for g in range(SSM_GROUPS):
                    h, gl = divmod(g, per_half)
                    c0, p0 = gl * SSM_GROUP, gl * SSM_STATE
                    db_ref[k, g * SSM_GROUP:(g + 1) * SSM_GROUP, :] = db_acc[k, h, c0:c0 + SSM_GROUP, p0:p0 + SSM_STATE]
                    dc_ref[k, g * SSM_GROUP:(g + 1) * SSM_GROUP, :] = dc_acc[k, h, c0:c0 + SSM_GROUP, p0:p0 + SSM_STATE]

    full = lambda shape: pl.BlockSpec(shape, lambda n: (0,) * len(shape))
    return pl.pallas_call(
        _with_comm(body, comm, 13, 5, nb, 0), grid=(nb,),
        in_specs=[pl.BlockSpec((t, 2 * SSM_W), lambda n: (nb - 1 - n, 0)),
                  pl.BlockSpec((t, SSM_W), lambda n: (nb - 1 - n, 0)),
                  pl.BlockSpec((8, 2 * SSM_N), lambda n: (nb - 1 - n, 0)),
                  pl.BlockSpec((t, 2 * SSM_N), lambda n: (nb - 1 - n, 0)),
                  full((8, SSM_N)), full((12, SCAN_SUB, SSM_N)), full((2, t + ns, t)), full((t, 2 * ns)),
                  full((2, SSM_HALVES, SSM_HW, SSM_HN)), full((2, SSM_HALVES, SSM_HN, SSM_HW)), full((1, SSM_W)),
                  full((SSM_W, SSM_W)), full((1, SSM_W))] + c_in,
        out_specs=[pl.BlockSpec((t, 2 * SSM_W), lambda n: (nb - 1 - n, 0)), full((2, SSM_W, SSM_STATE)),
                   full((2, SSM_W, SSM_STATE)), full((SSM_W, SSM_W)), full((8, SSM_N))] + c_out,
        out_shape=[jax.ShapeDtypeStruct((l, 2 * SSM_W), BF16),
                   jax.ShapeDtypeStruct((2, SSM_W, SSM_STATE), F32),
                   jax.ShapeDtypeStruct((2, SSM_W, SSM_STATE), F32),
                   jax.ShapeDtypeStruct((SSM_W, SSM_W), F32),
                   jax.ShapeDtypeStruct((8, SSM_N), F32)] + c_shape,
        scratch_shapes=[pltpu.VMEM((1, 2 * SSM_N), F32), pltpu.VMEM((2, SSM_HALVES, SSM_HW, SSM_HN), F32),
                        pltpu.VMEM((2, SSM_HALVES, SSM_HW, SSM_HN), F32)] + c_scratch,
        compiler_params=_params("arbitrary"), name=name)(ps, dys, chk, states, *scan_ops, bblk, cblk, dskip, wglu, bglu,
                                                         *c_args)


def _pool_count(i, t):
    pos = lax.broadcasted_iota(jnp.int32, (t, POOL_W), 0) + i * t + 1
    col = lax.broadcasted_iota(jnp.int32, (t, POOL_W), 1)
    win = jnp.where(col < POOL_GW, 2, jnp.where(col < 2 * POOL_GW, 4, jnp.where(col < 3 * POOL_GW, 8, 16)))
    return 1.0 / jnp.minimum(pos, win).astype(F32), col


def _window_sums(ext, n_rows, forward):
    col = lax.broadcasted_iota(jnp.int32, ext.shape, 1)
    sh = (lambda a, d: pltpu.roll(a, d, 0)) if forward else (lambda a, d: pltpu.roll(a, n_rows - d, 0))
    a2 = ext + sh(ext, 1)
    a4 = a2 + sh(a2, 2)
    a8 = a4 + sh(a4, 4)
    a16 = a8 + sh(a8, 8)
    return jnp.where(col < POOL_GW, a2, jnp.where(col < 2 * POOL_GW, a4, jnp.where(col < 3 * POOL_GW, a8, a16)))


def _pool_mix(pooled, wp_ref):
    return jnp.concatenate([_dot(pooled[:, g * POOL_GW:(g + 1) * POOL_GW], wp_ref[g]) for g in range(4)], axis=1)


def _pool_pooled(i, cur_u, prev_u, t):
    prev = jnp.where(i > 0, prev_u, 0.0)
    ext = jnp.concatenate([prev, cur_u], axis=0)
    inv_cnt, _ = _pool_count(i, t)
    return _window_sums(ext, t + POOL_HALO, True)[POOL_HALO:, :] * inv_cnt - cur_u


def _pool_fwd(pp, wpool, pscale, *, name, t=SEQ_BLOCK):
    l = pp.shape[0]
    t = min(t, l)

    def body(cur_ref, prev_ref, wp_ref, sc_ref, yp_ref):
        i = pl.program_id(0)
        pooled = _pool_pooled(i, cur_ref[:, :POOL_W].astype(F32), prev_ref[...].astype(F32), t)
        lin = _pool_mix(pooled, wp_ref)
        sz, _ = _silu_and_grad(cur_ref[:, POOL_W:].astype(F32))
        yp_ref[...] = (lin * sc_ref[...] * sz).astype(BF16)

    hb = t // POOL_HALO
    return pl.pallas_call(
        body, grid=(l // t,),
        in_specs=[pl.BlockSpec((t, 2 * POOL_W), lambda i: (i, 0)),
                  pl.BlockSpec((POOL_HALO, POOL_W), lambda i: (jnp.maximum(i * hb - 1, 0), 0)),
                  pl.BlockSpec((4, POOL_GW, POOL_GW), lambda i: (0, 0, 0)),
                  pl.BlockSpec((1, POOL_W), lambda i: (0, 0))],
        out_specs=pl.BlockSpec((t, POOL_W), lambda i: (i, 0)),
        out_shape=jax.ShapeDtypeStruct((l, POOL_W), BF16),
        compiler_params=_params("parallel"), name=name)(pp, pp, wpool, pscale)


def _pool_bwd(pp, dyp, wpool, pscale, *, name, t=SEQ_BLOCK):
    l = pp.shape[0]
    t = min(t, l)
    nb = l // t

    def body(cur_ref, prev_ref, dyp_ref, wp_ref, sc_ref, dpp_ref, dwp_ref, sums_ref, carry_ref):
        n = pl.program_id(0)
        i = nb - 1 - n

        @pl.when(n == 0)
        def _():
            carry_ref[...] = jnp.zeros_like(carry_ref)
            dwp_ref[...] = jnp.zeros_like(dwp_ref)
            sums_ref[...] = jnp.zeros_like(sums_ref)

        cur_u = cur_ref[:, :POOL_W].astype(F32)
        pooled = _pool_pooled(i, cur_u, prev_ref[...].astype(F32), t)
        lin = _pool_mix(pooled, wp_ref)
        sz, dsz = _silu_and_grad(cur_ref[:, POOL_W:].astype(F32))
        dyp_v = dyp_ref[...]
        scale = sc_ref[...]
        dpp_ref[:, POOL_W:] = (dyp_v * lin * scale * dsz).astype(BF16)
        dpre = dyp_v * sz
        sums_ref[0:1, :] += jnp.sum(dpre * lin, axis=0, keepdims=True)
        dlin = dpre * scale
        dpooled = []
        for g in range(4):
            dl = dlin[:, g * POOL_GW:(g + 1) * POOL_GW]
            dwp_ref[g] += _dot(pooled[:, g * POOL_GW:(g + 1) * POOL_GW], dl, TN)
            dpooled.append(_dot(dl, wp_ref[g], NT))
        dpooled = jnp.concatenate(dpooled, axis=1)
        inv_cnt, _ = _pool_count(i, t)
        dq = dpooled * inv_cnt
        ext = jnp.concatenate([dq, carry_ref[...]], axis=0)
        du = _window_sums(ext, t + POOL_HALO, False)[:t, :] - dpooled
        dpp_ref[:, :POOL_W] = du.astype(BF16)
        carry_ref[...] = dq[:POOL_HALO, :]

    hb = t // POOL_HALO
    return pl.pallas_call(
        body, grid=(nb,),
        in_specs=[pl.BlockSpec((t, 2 * POOL_W), lambda n: (nb - 1 - n, 0)),
                  pl.BlockSpec((POOL_HALO, POOL_W), lambda n: (jnp.maximum((nb - 1 - n) * hb - 1, 0), 0)),
                  pl.BlockSpec((t, POOL_W), lambda n: (nb - 1 - n, 0)),
                  pl.BlockSpec((4, POOL_GW, POOL_GW), lambda n: (0, 0, 0)),
                  pl.BlockSpec((1, POOL_W), lambda n: (0, 0))],
        out_specs=[pl.BlockSpec((t, 2 * POOL_W), lambda n: (nb - 1 - n, 0)),
                   pl.BlockSpec((4, POOL_GW, POOL_GW), lambda n: (0, 0, 0)),
                   pl.BlockSpec((8, POOL_W), lambda n: (0, 0))],
        out_shape=[jax.ShapeDtypeStruct((l, 2 * POOL_W), BF16), jax.ShapeDtypeStruct((4, POOL_GW, POOL_GW), F32),
                   jax.ShapeDtypeStruct((8, POOL_W), F32)],
        scratch_shapes=[pltpu.VMEM((POOL_HALO, POOL_W), F32)],
        compiler_params=_params("arbitrary"), name=name)(pp, pp, dyp, wpool, pscale)


def _merge_fwd(ya, ys, yp, wa, ws, wp, pg, wout, x, gate, *, name, tm=512):
    l, d = x.shape
    tm = min(tm, l)

    def body(ya_ref, ys_ref, yp_ref, wa_ref, ws_ref, wp_ref, pg_ref, wo_ref, x_ref, g_ref,
             xn_ref, mg_ref, ba_ref, bs_ref, bp_ref, out_ref):
        acc = None
        for k, (y_ref, w_ref, b_ref) in enumerate(((ya_ref, wa_ref, ba_ref), (ys_ref, ws_ref, bs_ref),
                                                   (yp_ref, wp_ref, bp_ref))):
            br = _dot(y_ref[...], w_ref[...])
            b_ref[...] = br.astype(BF16)
            term = _sigmoid(pg_ref[:, k * d:(k + 1) * d].astype(F32)) * br
            acc = term if acc is None else acc + term
        merged = acc.astype(BF16)
        mg_ref[...] = merged
        out = _dot(merged, wo_ref[...])
        out_ref[...] = out.astype(BF16)
        xn_ref[...] = x_ref[...] + g_ref[...] * out

    rowy = pl.BlockSpec((tm, ATT_W), lambda i: (i, 0))
    wsp = pl.BlockSpec((ATT_W, d), lambda i: (0, 0))
    rowd = pl.BlockSpec((tm, d), lambda i: (i, 0))
    return pl.pallas_call(
        body, grid=(l // tm,),
        in_specs=[rowy, rowy, rowy, wsp, wsp, wsp, pl.BlockSpec((tm, 3 * d), lambda i: (i, 0)),
                  pl.BlockSpec((d, d), lambda i: (0, 0)), rowd, pl.BlockSpec((1, d), lambda i: (0, 0))],
        out_specs=[rowd] * 6,
        out_shape=[jax.ShapeDtypeStruct((l, d), F32)] + [jax.ShapeDtypeStruct((l, d), BF16)] * 5,
        compiler_params=_params("parallel"), name=name)(ya, ys, yp, wa, ws, wp, pg, wout, x, gate)


def _merge_bwd(dx, out, gate, wout, pg, brs, wbrs, ys, merged, *, name, tm=256, comm=None):
    l, d = dx.shape
    tm = min(tm, l)
    nb = l // tm
    w = ys[0].shape[1]

    def body(dx_ref, out_ref, g_ref, w_ref, pg_ref, ba_ref, bs_ref, bp_ref, wa_ref, ws_ref, wp_ref,
             ya_ref, ys_ref, yp_ref, mg_ref,
             dya_ref, dys_ref, dyp_ref, dpg_ref, sums_ref, dwa_ref, dws_ref, dwp_ref, dwo_ref, acc_br, acc_out):
        i = pl.program_id(0)

        @pl.when(i == 0)
        def _():
            sums_ref[...] = jnp.zeros_like(sums_ref)
            acc_br[...] = jnp.zeros_like(acc_br)
            acc_out[...] = jnp.zeros_like(acc_out)

        dxv = dx_ref[...]
        sums_ref[0:1, :] += jnp.sum(dxv * out_ref[...].astype(F32), axis=0, keepdims=True)
        dmo = (dxv * g_ref[...]).astype(BF16)
        acc_out[...] += _dot(mg_ref[...], dmo, TN)
        dmerged = _dot(dmo, w_ref[...], NT)
        branches = ((ba_ref, wa_ref, ya_ref, dya_ref), (bs_ref, ws_ref, ys_ref, dys_ref), (bp_ref, wp_ref, yp_ref, dyp_ref))
        for k, (b_ref, wk_ref, y_ref, dy_ref) in enumerate(branches):
            gk = _sigmoid(pg_ref[:, k * d:(k + 1) * d].astype(F32))
            dbr = (dmerged * gk).astype(BF16)
            dpg_ref[:, k * d:(k + 1) * d] = (dmerged * b_ref[...].astype(F32) * gk * (1.0 - gk)).astype(BF16)
            dy_ref[...] = _dot(dbr, wk_ref[...], NT)
            acc_br[k] += _dot(y_ref[...], dbr, TN)

        @pl.when(i == nb - 1)
        def _():
            for k, dw_ref in enumerate((dwa_ref, dws_ref, dwp_ref)):
                dw_ref[...] = acc_br[k].astype(BF16)
            dwo_ref[...] = acc_out[...].astype(BF16)

    row = pl.BlockSpec((tm, d), lambda i: (i, 0))
    half = pl.BlockSpec((tm, w), lambda i: (i, 0))
    wide = pl.BlockSpec((tm, 3 * d), lambda i: (i, 0))
    const = lambda shape: pl.BlockSpec(shape, lambda i: (0,) * len(shape))
    c_args, c_in, c_out, c_shape, c_scratch = _comm_extra(comm)
    res = pl.pallas_call(
        _with_comm(body, comm, 15, 9, nb, 0), grid=(nb,),
        in_specs=[row, row, const((1, d)), const((d, d)), wide, row, row, row, const((w, d)), const((w, d)), const((w, d)),
                  half, half, half, row] + c_in,
        out_specs=[half, half, half, wide, const((8, d)), const((w, d)), const((w, d)), const((w, d)), const((d, d))] + c_out,
        out_shape=[jax.ShapeDtypeStruct((l, w), F32)] * 3 + [jax.ShapeDtypeStruct((l, 3 * d), BF16),
                                                             jax.ShapeDtypeStruct((8, d), F32)]
                  + [jax.ShapeDtypeStruct((w, d), BF16)] * 3 + [jax.ShapeDtypeStruct((d, d), BF16)] + c_shape,
        scratch_shapes=[pltpu.VMEM((3, w, d), F32), pltpu.VMEM((d, d), F32)] + c_scratch,
        compiler_params=_params("arbitrary"), name=name)(dx, out, gate, wout, pg, *brs, *wbrs, *ys, merged, *c_args)
    return list(res[:9]), list(res[9:])


def _adamw(w, gs, m, v, *, name, tr=256):
    r, c = w.shape
    ns = len(gs)
    p, rs, _ = gs[0].shape
    assert rs * ns == r
    tr = min(tr, rs)
    assert rs % tr == 0
    nr = rs // tr
    c1 = 1.0 / (1.0 - ADAM_B1 ** ADAM_STEP)
    c2 = 1.0 / (1.0 - ADAM_B2 ** ADAM_STEP)

    def body(*refs):
        w_ref, g_refs, (m_ref, v_ref, go_ref, d_ref, mo_ref, vo_ref) = refs[0], refs[1:1 + ns], refs[1 + ns:]
        slab = pl.program_id(0)
        gv = None
        for k, g_ref in enumerate(g_refs):
            gk = g_ref[0].astype(F32)
            for j in range(1, p):
                gk = gk + g_ref[j].astype(F32)
            gv = gk if gv is None else jnp.where(slab == k, gk, gv)
        go_ref[...] = gv
        mn = ADAM_B1 * m_ref[...] + (1.0 - ADAM_B1) * gv
        vn = ADAM_B2 * v_ref[...] + (1.0 - ADAM_B2) * (gv * gv)
        mo_ref[...] = mn
        vo_ref[...] = vn
        d_ref[...] = -ADAM_LR * ((mn * c1) / (jnp.sqrt(vn * c2) + ADAM_EPS) + ADAM_WD * w_ref[...])

    row = pl.BlockSpec((tr, c), lambda s, i: (s * nr + i, 0))
    g_specs = [pl.BlockSpec((p, tr, c), lambda s, i, k=k: (0, jnp.where(s == k, i, 0), 0)) for k in range(ns)]
    return pl.pallas_call(
        body, grid=(ns, nr),
        in_specs=[row] + g_specs + [row, row],
        out_specs=[row] * 4,
        out_shape=[jax.ShapeDtypeStruct((r, c), F32)] * 4,
        compiler_params=_params("arbitrary", "arbitrary"), name=name)(w, *gs, m, v)


def _mesh_place():
    x, y, c = lax.axis_index("x"), lax.axis_index("y"), lax.axis_index("c")
    other_chips = [(1 - x, y), (x, 1 - y), (1 - x, 1 - y)]
    return x, y, c, other_chips


def _run_plan(plan, *, name):
    n = len(plan["args"])

    def body(*refs):
        start, forward, finish = plan["phases"](refs[:n], refs[n:2 * n], *refs[2 * n:])
        start()
        forward()
        finish()

    anyspec = pl.BlockSpec(memory_space=pl.ANY)
    return pl.pallas_call(
        body, in_specs=[anyspec] * n, out_specs=[anyspec] * n, out_shape=plan["out_shape"],
        scratch_shapes=plan["scratch"], name=name)(*plan["args"])


def _gather_plan(arrs):
    n = len(arrs)

    def phases(ins, outs, send_sems, recv_sems, loc_sems):
        x, y, c, chips = _mesh_place()
        me = 4 * x + 2 * y + c
        slot = lambda px, py, pc: 4 * px + 2 * py + pc

        def copy(k, j, src, block, to):
            return pltpu.make_async_remote_copy(
                src_ref=src, dst_ref=outs[k].at[block], send_sem=send_sems.at[k, j], recv_sem=recv_sems.at[k, j],
                device_id=to, device_id_type=pl.DeviceIdType.MESH)

        local = [pltpu.make_async_copy(ins[k], outs[k].at[me], loc_sems.at[k]) for k in range(n)]
        first = []
        for k in range(n):
            first.append(copy(k, 0, ins[k], me, (x, y, 1 - c)))
            for j, chip in enumerate(chips):
                first.append(copy(k, 1 + j, ins[k], me, (*chip, c)))
        passed = [copy(k, 4 + j, outs[k].at[slot(*chip, c)], slot(*chip, c), (x, y, 1 - c))
                  for j, chip in enumerate(chips) for k in range(n)]

        def start():
            for cp in local + first:
                cp.start()

        def forward():
            for j, chip in enumerate(chips):
                for k in range(n):
                    copy(k, 1 + j, ins[k], slot(*chip, c), (x, y, c)).wait_recv()
                    passed[j * n + k].start()

        def finish():
            for k in range(n):
                copy(k, 0, ins[k], slot(x, y, 1 - c), (x, y, c)).wait_recv()
                for j, chip in enumerate(chips):
                    copy(k, 4 + j, ins[k], slot(*chip, 1 - c), (x, y, c)).wait_recv()
            for cp in first + passed:
                cp.wait_send()
            for cp in local:
                cp.wait()

        return start, forward, finish

    return dict(
        args=list(arrs), out_shape=[jax.ShapeDtypeStruct((N_DEV,) + a.shape, a.dtype) for a in arrs],
        scratch=[pltpu.SemaphoreType.DMA((n, 7)), pltpu.SemaphoreType.DMA((n, 7)), pltpu.SemaphoreType.DMA((n,))],
        phases=phases)


def _allreduce_small(small, extra, *, name):
    r, lanes = small.shape
    assert r % 16 == 0
    h = r // 2
    e = extra.shape[0]

    def body(s_ref, x_ref, out_ref, xall_ref, sib_ref, parts_ref, send_sems, recv_sems):
        x, y, c, chips = _mesh_place()
        me = 4 * x + 2 * y + c
        my_chip = 2 * x + y
        sibling = (x, y, 1 - c)
        mine = pl.ds(pl.multiple_of(c * h, 8), h)
        theirs = pl.ds(pl.multiple_of((1 - c) * h, 8), h)

        def remote(j, src, dst, to):
            return pltpu.make_async_remote_copy(src_ref=src, dst_ref=dst, send_sem=send_sems.at[j],
                                                recv_sem=recv_sems.at[j], device_id=to, device_id_type=pl.DeviceIdType.MESH)

        to_sibling = remote(0, s_ref.at[theirs], sib_ref, sibling)
        to_sibling.start()
        xall_ref[me] = x_ref[...]
        extras = []
        for rr in range(1, N_DEV):
            peer = me ^ rr
            cp = remote(4 + rr, x_ref, xall_ref.at[me], (peer // 4, (peer // 2) % 2, peer % 2))
            cp.start()
            extras.append(cp)
        to_sibling.wait_recv()
        parts_ref[my_chip] = s_ref[mine] + sib_ref[...]
        to_chips = [remote(1 + j, parts_ref.at[my_chip], parts_ref.at[my_chip], (px, py, c))
                    for j, (px, py) in enumerate(chips)]
        for cp in to_chips:
            cp.start()
        for cp in to_chips:
            cp.wait_recv()
        out_ref[mine] = (parts_ref[0] + parts_ref[1]) + (parts_ref[2] + parts_ref[3])
        done = remote(4, out_ref.at[mine], out_ref.at[mine], sibling)
        done.start()
        remote(4, out_ref.at[theirs], out_ref.at[theirs], sibling).wait_recv()
        for cp in extras:
            cp.wait()
        to_sibling.wait_send()
        for cp in to_chips:
            cp.wait_send()
        done.wait_send()

    vmem = pl.BlockSpec(memory_space=pltpu.VMEM)
    return pl.pallas_call(
        body, in_specs=[vmem, vmem], out_specs=[vmem, vmem],
        out_shape=[jax.ShapeDtypeStruct((r, lanes), F32), jax.ShapeDtypeStruct((N_DEV, e, lanes), F32)],
        scratch_shapes=[pltpu.VMEM((h, lanes), F32), pltpu.VMEM((4, h, lanes), F32),
                        pltpu.SemaphoreType.DMA((12,)), pltpu.SemaphoreType.DMA((12,))],
        compiler_params=pltpu.CompilerParams(vmem_limit_bytes=VMEM_LIMIT), name=name)(small, extra)


def _ada_modulation(c, w_ada, b_cols, comm):
    depth, d, cols = w_ada.shape
    c_args, c_in, c_out, c_shape, c_scratch = _comm_extra(comm)
    nc = len(c_args)

    def body(c_ref, w_ref, b_ref, *refs):
        cin, (cact_ref, mod_ref), cout = refs[:nc], refs[nc:nc + 2], refs[nc + 2:2 * nc + 2]
        call_ref, part_ref, send_sems, recv_sems = refs[2 * nc + 2:2 * nc + 6]
        start, forward, finish = comm["phases"](cin, cout, *refs[2 * nc + 6:])
        start()
        x, y, core, _ = _mesh_place()
        me = 4 * x + 2 * y + core

        def to_all(j0, src, dst):
            copies = []
            for r in range(1, N_DEV):
                peer = me ^ r
                copies.append(pltpu.make_async_remote_copy(
                    src_ref=src, dst_ref=dst, send_sem=send_sems.at[j0 + r - 1], recv_sem=recv_sems.at[j0 + r - 1],
                    device_id=(peer // 4, (peer // 2) % 2, peer % 2), device_id_type=pl.DeviceIdType.MESH))
            for cp in copies:
                cp.start()
            for cp in copies:
                cp.wait()

        call_ref[me] = c_ref[...]
        to_all(0, c_ref, call_ref.at[me])
        c_act = jnp.concatenate([call_ref[k] for k in range(N_DEV)], axis=0)
        c_act = c_act * _sigmoid(c_act)
        cact_ref[...] = c_act
        for li in range(depth):
            part_ref[li] = _dot(c_act, w_ref[li]) + b_ref[li:li + 1, :]
        mod_ref[me] = part_ref[...]
        to_all(N_DEV - 1, part_ref, mod_ref.at[me])
        forward()
        finish()

    vmem = pl.BlockSpec(memory_space=pltpu.VMEM)
    res = pl.pallas_call(
        body, in_specs=[vmem] * 3 + c_in, out_specs=[vmem] * 2 + c_out,
        out_shape=[jax.ShapeDtypeStruct((N_DEV, d), F32), jax.ShapeDtypeStruct((N_DEV, depth, N_DEV, cols), F32)] + c_shape,
        scratch_shapes=[pltpu.VMEM((N_DEV, 1, d), F32), pltpu.VMEM((depth, N_DEV, cols), F32),
                        pltpu.SemaphoreType.DMA((2 * (N_DEV - 1),)), pltpu.SemaphoreType.DMA((2 * (N_DEV - 1),))] + c_scratch,
        compiler_params=pltpu.CompilerParams(vmem_limit_bytes=VMEM_LIMIT), name="ada_modulation")(c, w_ada, b_cols, *c_args)
    return res[0], res[1], list(res[2:])


def _sibling_swap_plan(arrs):
    n = len(arrs)

    def phases(ins, outs, send_sems, recv_sems):
        x, y, c, _ = _mesh_place()
        copies = [pltpu.make_async_remote_copy(
            src_ref=ins[k].at[1 - c], dst_ref=outs[k], send_sem=send_sems.at[k], recv_sem=recv_sems.at[k],
            device_id=(x, y, 1 - c), device_id_type=pl.DeviceIdType.MESH) for k in range(n)]

        def start():
            for cp in copies:
                cp.start()

        def finish():
            for cp in copies:
                cp.wait()

        return start, (lambda: None), finish

    return dict(args=list(arrs), out_shape=[jax.ShapeDtypeStruct(a.shape[1:], a.dtype) for a in arrs],
                scratch=[pltpu.SemaphoreType.DMA((n,)), pltpu.SemaphoreType.DMA((n,))], phases=phases)


def _pair_add(mine, theirs, core, *, name, tr=256):
    _, r, c = mine.shape
    tr = min(tr, r)
    assert r % tr == 0

    def body(core_ref, m_ref, t_ref, o_ref):
        o_ref[...] = (m_ref[0].astype(F32) + t_ref[...].astype(F32)).astype(BF16)

    return pl.pallas_call(
        body,
        grid_spec=pltpu.PrefetchScalarGridSpec(
            num_scalar_prefetch=1, grid=(r // tr,),
            in_specs=[pl.BlockSpec((1, tr, c), lambda i, core_ref: (core_ref[0], i, 0)),
                      pl.BlockSpec((tr, c), lambda i, core_ref: (i, 0))],
            out_specs=pl.BlockSpec((tr, c), lambda i, core_ref: (i, 0))),
        out_shape=jax.ShapeDtypeStruct((r, c), BF16),
        compiler_params=_params("parallel"), name=name)(core, mine, theirs)


def _pair_add_small(mines, theirs, core, *, name):
    n = len(mines)

    def body(core_ref, *refs):
        for m_ref, t_ref, o_ref in zip(refs[:n], refs[n:2 * n], refs[2 * n:]):
            o_ref[...] = (m_ref[0].astype(F32) + t_ref[...].astype(F32)).astype(BF16)

    whole = lambda a: pl.BlockSpec(a.shape, lambda i, core_ref: (0,) * a.ndim)
    return pl.pallas_call(
        body,
        grid_spec=pltpu.PrefetchScalarGridSpec(
            num_scalar_prefetch=1, grid=(1,),
            in_specs=[pl.BlockSpec((1,) + m.shape[1:], lambda i, core_ref: (core_ref[0], 0, 0)) for m in mines]
                     + [whole(t) for t in theirs],
            out_specs=[whole(t) for t in theirs]),
        out_shape=[jax.ShapeDtypeStruct(t.shape, BF16) for t in theirs],
        compiler_params=_params("arbitrary"), name=name)(core, *mines, *theirs)


def _chip_scatter_plan(arrs):
    n = len(arrs)

    def phases(ins, outs, send_sems, recv_sems, loc_sems):
        x, y, c, chips = _mesh_place()
        mine = 2 * x + y
        local = [pltpu.make_async_copy(ins[k].at[mine], outs[k].at[mine], loc_sems.at[k]) for k in range(n)]
        remote = [pltpu.make_async_remote_copy(
            src_ref=ins[k].at[2 * px + py], dst_ref=outs[k].at[mine], send_sem=send_sems.at[k, j],
            recv_sem=recv_sems.at[k, j], device_id=(px, py, c), device_id_type=pl.DeviceIdType.MESH)
            for j, (px, py) in enumerate(chips) for k in range(n)]

        def start():
            for cp in local + remote:
                cp.start()

        def finish():
            for cp in remote:
                cp.wait()
            for cp in local:
                cp.wait()

        return start, (lambda: None), finish

    return dict(
        args=list(arrs), out_shape=[jax.ShapeDtypeStruct(a.shape, a.dtype) for a in arrs],
        scratch=[pltpu.SemaphoreType.DMA((n, 3)), pltpu.SemaphoreType.DMA((n, 3)), pltpu.SemaphoreType.DMA((n,))],
        phases=phases)


def _ssm_discretize(a_re, a_im, log_dt, b_re, b_im):
    dt = jnp.exp(log_dt)[:, None]
    mag = jnp.exp(a_re * dt)
    lr = mag * jnp.cos(a_im * dt)
    li = mag * jnp.sin(a_im * dt)
    den = a_re * a_re + a_im * a_im
    cr = ((lr - 1.0) * a_re + li * a_im) / den
    ci = (li * a_re - (lr - 1.0) * a_im) / den
    bbr = cr[..., None] * b_re - ci[..., None] * b_im
    bbi = cr[..., None] * b_im + ci[..., None] * b_re
    return lr, li, bbr, bbi


def _ssm_dense(lr, li, bbr, bbi, c_re, c_im, *, name):
    import numpy as np
    scan_ops = _scan_tables(lr.reshape(-1), li.reshape(-1)) + _scan_consts(SEQ_BLOCK)
    per_half = SSM_GROUPS // SSM_HALVES
    bt = jnp.stack([b.transpose(0, 2, 1).reshape(SSM_W, SSM_STATE) for b in (bbr, bbi)])
    ct = jnp.stack([c.transpose(0, 2, 1).reshape(SSM_N, SSM_GROUP) for c in (c_re, c_im)])
    rep_p = jnp.asarray(np.tile(np.eye(SSM_STATE, dtype=np.float32), (1, per_half)), BF16)
    rep_c = jnp.asarray(np.tile(np.eye(SSM_GROUP, dtype=np.float32), (1, per_half)), BF16)

    def body(bt_ref, ct_ref, rp_ref, rc_ref, b_ref, c_ref):
        def on_diagonal(shape, rows, cols):
            r = lax.broadcasted_iota(jnp.int32, shape, 0) // rows
            c = lax.broadcasted_iota(jnp.int32, shape, 1) // cols
            return r == c

        mask_b = on_diagonal((SSM_HW, SSM_HN), SSM_GROUP, SSM_STATE)
        mask_c = on_diagonal((SSM_HN, SSM_HW), SSM_STATE, SSM_GROUP)
        for k in range(2):
            for h in range(SSM_HALVES):
                b_rows = bt_ref[k, h * SSM_HW:(h + 1) * SSM_HW, :]
                b_ref[k, h] = jnp.where(mask_b, _dot(b_rows, rp_ref[...]), 0.0).astype(BF16)
                c_rows = ct_ref[k, h * SSM_HN:(h + 1) * SSM_HN, :]
                c_ref[k, h] = jnp.where(mask_c, _dot(c_rows, rc_ref[...]), 0.0).astype(BF16)

    vmem = pl.BlockSpec(memory_space=pltpu.VMEM)
    bblk, cblk = pl.pallas_call(
        body, in_specs=[vmem] * 4, out_specs=[vmem] * 2,
        out_shape=[jax.ShapeDtypeStruct((2, SSM_HALVES, SSM_HW, SSM_HN), BF16),
                   jax.ShapeDtypeStruct((2, SSM_HALVES, SSM_HN, SSM_HW), BF16)],
        compiler_params=pltpu.CompilerParams(vmem_limit_bytes=VMEM_LIMIT), name=name)(bt, ct, rep_p, rep_c)
    return scan_ops, bblk, cblk


def _ssm_extract(db, dc, sums):
    db = db.reshape(2, SSM_GROUPS, SSM_GROUP, SSM_STATE).transpose(0, 1, 3, 2)
    dc = dc.reshape(2, SSM_GROUPS, SSM_GROUP, SSM_STATE)
    dlr = sums[0].reshape(SSM_GROUPS, SSM_STATE)
    dli = sums[1].reshape(SSM_GROUPS, SSM_STATE)
    return dlr, dli, db[0], db[1], dc[0], dc[1]


def _in_groups():
    names = ("q", "k", "v", "u_ssm", "u_pool", "z_att", "z_ssm", "z_pool", "gates")
    sizes = (ATT_W, KV_W, KV_W, SSM_W, POOL_W, ATT_W, SSM_W, POOL_W, 3 * D_MODEL)
    r, lo = {}, 0
    for nm, s in zip(names, sizes):
        r[nm] = (lo, lo + s)
        lo += s
    kv = (r["k"][0], r["v"][1])
    return ((r["q"], r["z_att"], kv), (r["u_ssm"], r["z_ssm"]), (r["u_pool"], r["z_pool"]), (r["gates"],))


IN_GROUPS = _in_groups()


def _layer_fwd(x, lw, li, late=None, comm_attn=None, comm_ssm=None):
    tag = f"l{li}"
    h, (pa, ps, pp, pg), arrived = _ln_proj(x, lw["norm_g"], lw["shift"], lw["scale"], lw["w_in"], IN_GROUPS,
                                            name=f"ln_proj_{tag}", comm=None if late is None else late[0])
    if late is not None:
        lw = {**lw, **late[1](arrived)}
    ya, from_attn = _attn_fwd(pa, lw["sinks"], name=f"attn_fwd_{tag}", comm=comm_attn)
    ys, chk, states, *from_ssm = _ssm_fwd(ps, lw["lam"], lw["bblk"], lw["cblk"], lw["ssm_d"], lw["w_glu"], lw["b_glu"],
                                          name=f"ssm_fwd_{tag}", comm=comm_ssm)
    yp = _pool_fwd(pp, lw["w_pool"], lw["pool_scale"], name=f"pool_fwd_{tag}")
    x_new, merged, ba, bs, bp, out = _merge_fwd(ya, ys, yp, lw["w_br_att"], lw["w_br_ssm"], lw["w_br_pool"], pg,
                                                lw["w_out"], x, lw["gate"], name=f"merge_fwd_{tag}")
    saved = dict(x=x, h=h, pa=pa, ps=ps, pp=pp, pg=pg, ya=ya, ys=ys, yp=yp, chk=chk, states=states, merged=merged,
                 ba=ba, bs=bs, bp=bp, out=out)
    return x_new, saved, lw, list(from_attn), list(from_ssm)


def _layer_bwd(dx, lw, sv, li, later=None, own=None):
    tag = f"l{li}"
    g = {}
    merge_out, swapped = _merge_bwd(
        dx, sv["out"], lw["gate"], lw["w_out"], sv["pg"], (sv["ba"], sv["bs"], sv["bp"]),
        (lw["w_br_att"], lw["w_br_ssm"], lw["w_br_pool"]), (sv["ya"], sv["ys"], sv["yp"]), sv["merged"],
        name=f"merge_bwd_{tag}", comm=None if later is None else later[0])
    dya, dys, dyp, dpg, gate_sums, g["w_br_att"], g["w_br_ssm"], g["w_br_pool"], g["w_out"] = merge_out
    dpa, dsink = _attn_bwd(sv["pa"], lw["sinks"], dya, name=f"attn_bwd_{tag}")
    dps, db_dense, dc_dense, dwglu, ssm_sums, *exchanged = _ssm_bwd(
        sv["ps"], dys, sv["chk"], sv["states"], lw["lam"], lw["bblk"], lw["cblk"], lw["ssm_d"], lw["w_glu"], lw["b_glu"],
        name=f"ssm_bwd_{tag}", comm=None if later is None else later[1](swapped))
    g["w_glu"] = dwglu.astype(BF16)
    dpp, dwpool, pool_sums = _pool_bwd(sv["pp"], dyp, lw["w_pool"], lw["pool_scale"], name=f"pool_bwd_{tag}")
    h = sv["h"]
    dproj = (dpa, dps, dpp, dpg)
    g["w_in"], from_late = _mm_tn_grouped(h, dproj, IN_GROUPS, name=f"dw_in_{tag}",
                                          comm=None if own is None else own({k: g[k] for k in LATE_WEIGHTS}))
    dx_in, ln_sums, from_w_in = _ln_proj_bwd(dproj, lw["w_in"], IN_GROUPS, sv["x"], dx, lw["norm_g"], lw["scale"],
                                             name=f"ln_proj_bwd_{tag}",
                                             comm=None if own is None else own({"w_in": g["w_in"]}))
    g["dmod"] = jnp.concatenate([ln_sums[0], ln_sums[1], gate_sums[0]])
    g["norm_g"] = ln_sums[2]
    g["attn_sinks"] = dsink[:, 0]
    g["ssm_raw"] = _ssm_extract(db_dense, dc_dense, ssm_sums)
    g["ssm_d"] = ssm_sums[2, :SSM_W]
    g["b_glu"] = ssm_sums[3, :SSM_W]
    g["w_pool"] = dwpool
    g["pool_scale"] = pool_sums[0]
    return dx_in, g, exchanged, list(from_w_in) + list(from_late)


BIG_WEIGHTS = ("w_in", "w_glu", "w_br_att", "w_br_ssm", "w_br_pool", "w_out")
ROW_SHARDED = ("w_glu", "w_out")


LATE_WEIGHTS = BIG_WEIGHTS[1:]


def _full_weights(keys, gathered):
    full = {}
    for k, g in zip(keys, gathered):
        if k in ROW_SHARDED:
            full[k] = g.reshape(N_DEV * g.shape[1], g.shape[2])
        else:
            full[k] = g.transpose(1, 0, 2).reshape(g.shape[1], N_DEV * g.shape[2])
    return full


def _by_destination(keys, grads):
    out = []
    for k in keys:
        g = grads[k]
        if k in ROW_SHARDED:
            out.append(g.reshape(4, 2, g.shape[0] // N_DEV, g.shape[1]).transpose(1, 0, 2, 3))
        else:
            out.append(g.reshape(g.shape[0], 4, 2, g.shape[1] // N_DEV).transpose(2, 1, 0, 3))
    return out


def _prepare_layer(li, mod, norm_g, w_in_full, attn_sinks, disc, ssm_c_re, ssm_c_im, ssm_d, b_glu, w_pool, pool_scale):
    d = D_MODEL
    lr, li_, bbr, bbi = disc
    lam, bblk, cblk = _ssm_dense(lr[li], li_[li], bbr[li], bbi[li], ssm_c_re[li], ssm_c_im[li], name=f"ssm_dense_l{li}")
    return dict(
        norm_g=norm_g[li][None, :], shift=mod[li, :d][None, :], scale=mod[li, d:2 * d][None, :],
        gate=mod[li, 2 * d:][None, :], w_in=w_in_full,
        sinks=attn_sinks[li], lam=lam, bblk=bblk, cblk=cblk, ssm_d=ssm_d[li][None, :],
        b_glu=b_glu[li][None, :], w_pool=w_pool[li].astype(BF16), pool_scale=pool_scale[li][None, :])


SMALL_ROWS = 64
SMALL_ORDER = ("norm_g", "attn_sinks", "ssm_d", "b_glu", "w_pool", "pool_scale", "dmod")


def _pack_small(loss, dfinal_g, layer_grads):
    parts = [jnp.broadcast_to(loss.reshape(1), (128,)), dfinal_g]
    for g in layer_grads:
        for k in SMALL_ORDER:
            v = g[k].reshape(-1)
            if v.shape[0] % 128:
                v = jnp.pad(v, (0, 128 - v.shape[0] % 128))
            parts.append(v)
        for v in g["ssm_raw"]:
            parts.append(v.reshape(-1))
    flat = jnp.concatenate(parts)
    return jnp.pad(flat, (0, (-flat.shape[0]) % (SMALL_ROWS * 128))).reshape(-1, 128)


def _unpack_small(flat, shapes):
    out, off = [], 0
    for s in shapes:
        n = int(math.prod(s))
        out.append(flat[off:off + n].reshape(s))
        off += n + (-n) % 128
    return out


def kernel(x, c, norm_g, w_ada, b_ada, w_in, attn_sinks, ssm_a_re, ssm_a_im, ssm_log_dt, ssm_b_re, ssm_b_im, ssm_c_re, ssm_c_im, ssm_d, w_glu, b_glu, w_pool, pool_scale, w_br_att, w_br_ssm, w_br_pool, w_out, final_g, loss_target, m_norm_g, m_w_ada, m_b_ada, m_w_in, m_attn_sinks, m_ssm_a_re, m_ssm_a_im, m_ssm_log_dt, m_ssm_b_re, m_ssm_b_im, m_ssm_c_re, m_ssm_c_im, m_ssm_d, m_w_glu, m_b_glu, m_w_pool, m_pool_scale, m_w_br_att, m_w_br_ssm, m_w_br_pool, m_w_out, m_final_g, v_norm_g, v_w_ada, v_b_ada, v_w_in, v_attn_sinks, v_ssm_a_re, v_ssm_a_im, v_ssm_log_dt, v_ssm_b_re, v_ssm_b_im, v_ssm_c_re, v_ssm_c_im, v_ssm_d, v_w_glu, v_b_glu, v_w_pool, v_pool_scale, v_w_br_att, v_w_br_ssm, v_w_br_pool, v_w_out, v_final_g):
    me = 4 * lax.axis_index("x") + 2 * lax.axis_index("y") + lax.axis_index("c")
    d = D_MODEL
    ada_w = 3 * d // N_DEV

    sharded = dict(w_in=w_in, w_glu=w_glu, w_br_att=w_br_att, w_br_ssm=w_br_ssm, w_br_pool=w_br_pool, w_out=w_out)
    shards = lambda li, keys: [sharded[k][li].astype(BF16) for k in keys]

    b_cols = lax.dynamic_slice(b_ada, (0, me * ada_w), (DEPTH, ada_w))
    c_act, mod_all, w_in0 = _ada_modulation(c, w_ada, b_cols, _gather_plan(shards(0, ("w_in",))))
    mod_mine = lax.dynamic_index_in_dim(mod_all, me, axis=2, keepdims=False)
    mod_mine = mod_mine.transpose(1, 0, 2).reshape(DEPTH, 3 * d)

    disc, disc_vjp = jax.vjp(jax.vmap(_ssm_discretize), ssm_a_re, ssm_a_im, ssm_log_dt, ssm_b_re, ssm_b_im)
    layer = lambda li, gathered_w_in: _prepare_layer(
        li, mod_mine, norm_g, _full_weights(("w_in",), gathered_w_in)["w_in"], attn_sinks, disc, ssm_c_re, ssm_c_im,
        ssm_d, b_glu, w_pool, pool_scale)
    late_weights = lambda gathered: _full_weights(LATE_WEIGHTS, gathered)
    core = lax.axis_index("c").astype(jnp.int32).reshape(1)

    def add_pairs(keys, by_dest, from_sibling, tag):
        flat = {k: (a.reshape(2, -1, a.shape[-1]), b.reshape(-1, b.shape[-1]))
                for k, a, b in zip(keys, by_dest, from_sibling)}
        small = [k for k in keys if k != "w_in"]
        sums = {}
        if "w_in" in flat:
            sums["w_in"] = _pair_add(*flat["w_in"], core, name=f"grads_pair_add_{tag}_w_in")
        if small:
            added = _pair_add_small([flat[k][0] for k in small], [flat[k][1] for k in small], core,
                                    name=f"grads_pair_add_{tag}_late")
            sums.update(zip(small, added))
        return [sums[k].reshape(b.shape) for k, b in zip(keys, from_sibling)]

    def chip_sums_of(keys, grads_li, tag):
        by_dest = _by_destination(keys, grads_li)
        return add_pairs(keys, by_dest, _run_plan(_sibling_swap_plan(by_dest), name=f"grads_sibling_swap_{tag}"), tag)

    layers, saved, grads = [None] * DEPTH, [None] * DEPTH, [None] * DEPTH
    layers[0] = layer(0, w_in0)
    xs, saved[0], layers[0], late1, w_in1 = _layer_fwd(
        x[0], layers[0], 0, late=(_gather_plan(shards(0, LATE_WEIGHTS)), late_weights),
        comm_attn=_gather_plan(shards(1, LATE_WEIGHTS)), comm_ssm=_gather_plan(shards(1, ("w_in",))))
    layers[1] = {**layer(1, w_in1), **late_weights(late1)}
    xs, saved[1], _, _, _ = _layer_fwd(xs, layers[1], 1)
    dx, fin_sums = _final_loss(xs, final_g[None, :], loss_target[0])
    loss_part = jnp.sum(fin_sums[1])
    dx, grads[1], _, _ = _layer_bwd(dx, layers[1], saved[1], 1)
    by_dest1 = _by_destination(BIG_WEIGHTS, grads[1])
    dx, grads[0], scattered1, scattered0 = _layer_bwd(
        dx, layers[0], saved[0], 0,
        later=(_sibling_swap_plan(by_dest1),
               lambda swapped: _chip_scatter_plan(add_pairs(BIG_WEIGHTS, by_dest1, swapped, "l1"))),
        own=lambda g: _chip_scatter_plan(chip_sums_of(tuple(g), g, "l0_" + "_".join(g))))
    big = list(zip(scattered0, scattered1))
    grad_x = dx[None]

    small = _pack_small(loss_part, fin_sums[0], grads)
    dmod_rows = jnp.concatenate([grads[li]["dmod"] for li in range(DEPTH)]).reshape(-1, 128)
    small_sum, dmod_gathered = _allreduce_small(small, dmod_rows, name="allreduce_small")
    out = {}

    def adam(name, w, g_slabs, m, v):
        shp = w.shape
        r = int(math.prod(shp[:-1])) if len(shp) > 1 else 1
        w2, m2, v2 = (a.reshape(r, shp[-1]) for a in (w, m, v))
        gs = [g.reshape(g.shape[0], r // len(g_slabs), shp[-1]) for g in g_slabs]
        res = _adamw(w2, gs, m2, v2, name=f"adamw_{name}", tr=256 if shp[-1] >= 128 else 2048)
        out[name] = tuple(a.reshape(shp) for a in res)

    flat = small_sum.reshape(-1)
    shapes = [(128,), (d,)]
    for _ in range(DEPTH):
        shapes += [(d,), (N_HEADS,), (SSM_W,), (SSM_W,), (4, POOL_GW, POOL_GW), (POOL_W,), (3 * d,),
                   (SSM_GROUPS, SSM_STATE), (SSM_GROUPS, SSM_STATE), (SSM_GROUPS, SSM_STATE, SSM_GROUP),
                   (SSM_GROUPS, SSM_STATE, SSM_GROUP), (SSM_GROUPS, SSM_GROUP, SSM_STATE), (SSM_GROUPS, SSM_GROUP, SSM_STATE)]
    un = _unpack_small(flat, shapes)
    loss = un[0][0]
    g_final_g = un[1]
    per = 13
    gl = [un[2 + li * per: 2 + (li + 1) * per] for li in range(DEPTH)]
    st = lambda j: jnp.stack([gl[li][j] for li in range(DEPTH)])
    g_norm_g, g_sinks, g_ssm_d, g_b_glu, g_w_pool, g_pool_scale, g_b_ada = (st(j) for j in range(7))
    d_lr, d_li, d_bbr, d_bbi, g_c_re, g_c_im = (st(j) for j in range(7, 13))
    g_a_re, g_a_im, g_log_dt, g_b_re, g_b_im = disc_vjp((d_lr, d_li, d_bbr, d_bbi))

    dmod_all = lax.dynamic_slice(dmod_gathered.reshape(N_DEV, DEPTH, 3 * d), (0, 0, me * ada_w), (N_DEV, DEPTH, ada_w))
    dmod_all = dmod_all.transpose(1, 0, 2)
    g_w_ada = jnp.stack([_mm_tn(c_act, dmod_all[li], tm=d, tn=ada_w, tk=N_DEV, name=f"dw_ada_l{li}") for li in range(DEPTH)])

    adam("w_ada", w_ada, [g_w_ada[None]], m_w_ada, v_w_ada)
    adam("w_in", w_in, big[0], m_w_in, v_w_in)
    adam("w_glu", w_glu, big[1], m_w_glu, v_w_glu)
    adam("w_br_att", w_br_att, big[2], m_w_br_att, v_w_br_att)
    adam("w_br_ssm", w_br_ssm, big[3], m_w_br_ssm, v_w_br_ssm)
    adam("w_br_pool", w_br_pool, big[4], m_w_br_pool, v_w_br_pool)
    adam("w_out", w_out, big[5], m_w_out, v_w_out)

    small_names = ["norm_g", "b_ada", "attn_sinks", "ssm_a_re", "ssm_a_im", "ssm_log_dt", "ssm_b_re", "ssm_b_im",
                   "ssm_c_re", "ssm_c_im", "ssm_d", "b_glu", "w_pool", "pool_scale", "final_g"]
    small_w = [norm_g, b_ada, attn_sinks, ssm_a_re, ssm_a_im, ssm_log_dt, ssm_b_re, ssm_b_im, ssm_c_re, ssm_c_im,
               ssm_d, b_glu, w_pool, pool_scale, final_g]
    small_m = [m_norm_g, m_b_ada, m_attn_sinks, m_ssm_a_re, m_ssm_a_im, m_ssm_log_dt, m_ssm_b_re, m_ssm_b_im,
               m_ssm_c_re, m_ssm_c_im, m_ssm_d, m_b_glu, m_w_pool, m_pool_scale, m_final_g]
    small_v = [v_norm_g, v_b_ada, v_attn_sinks, v_ssm_a_re, v_ssm_a_im, v_ssm_log_dt, v_ssm_b_re, v_ssm_b_im,
               v_ssm_c_re, v_ssm_c_im, v_ssm_d, v_b_glu, v_w_pool, v_pool_scale, v_final_g]
    small_g = [g_norm_g, g_b_ada, g_sinks, g_a_re, g_a_im, g_log_dt, g_b_re, g_b_im, g_c_re, g_c_im,
               g_ssm_d, g_b_glu, g_w_pool, g_pool_scale, g_final_g]

    for nm, w, g, m, v in zip(small_names, small_w, small_g, small_m, small_v):
        adam(nm, w, [g[None]], m, v)

    order = ["norm_g", "w_ada", "b_ada", "w_in", "attn_sinks", "ssm_a_re", "ssm_a_im", "ssm_log_dt", "ssm_b_re",
             "ssm_b_im", "ssm_c_re", "ssm_c_im", "ssm_d", "w_glu", "b_glu", "w_pool", "pool_scale", "w_br_att",
             "w_br_ssm", "w_br_pool", "w_out", "final_g"]
    return (loss, grad_x, *[out[k][0] for k in order], *[out[k][1] for k in order],
            *[out[k][2] for k in order], *[out[k][3] for k in order])
```

```python
import functools
import math

import jax
import jax.numpy as jnp
from jax import lax
from jax.experimental import pallas as pl
from jax.experimental.pallas import tpu as pltpu

F32 = jnp.float32
BF16 = jnp.bfloat16

N_DEV = 8
D_MODEL = 1024
DEPTH = 2
CHUNK = 64
N_HEADS = 8
N_KV_HEADS = 2
HEAD_DIM = 64
Q_PER_KV = N_HEADS // N_KV_HEADS
WINDOW = 128
ATT_W = 512
KV_W = 128
SSM_W = 512
SSM_GROUP = 16
SSM_GROUPS = 32
SSM_STATE = 64
SSM_N = SSM_GROUPS * SSM_STATE
POOL_W = 512
POOL_WINDOWS = (2, 4, 8, 16)
POOL_GW = 128
POOL_HALO = 16
EPS = 1e-6
NEG_INF = -1e30
ADAM_LR = 0.001
ADAM_B1 = 0.9
ADAM_B2 = 0.999
ADAM_EPS = 1e-08
ADAM_WD = 0.01
ADAM_STEP = 10

SEQ_BLOCK = 256
ATT_BLOCK = 128
VMEM_LIMIT = 56 * 1024 * 1024

NN = (((1,), (0,)), ((), ()))
NT = (((1,), (1,)), ((), ()))
TN = (((0,), (0,)), ((), ()))


def _dot(a, b, dims=NN):
    return lax.dot_general(a.astype(BF16), b.astype(BF16), dims, preferred_element_type=F32)


def _params(*sem):
    return pltpu.CompilerParams(dimension_semantics=sem, vmem_limit_bytes=VMEM_LIMIT)


def _sigmoid(x):
    return 0.5 + 0.5 * jnp.tanh(0.5 * x)


def _silu_and_grad(z):
    s = _sigmoid(z)
    return z * s, s * (1.0 + z * (1.0 - s))


_GELU_K = math.sqrt(2.0 / math.pi)


def _gelu_and_grad(x):
    inner = _GELU_K * (x + 0.044715 * x * x * x)
    t = jnp.tanh(inner)
    val = 0.5 * x * (1.0 + t)
    grad = 0.5 * (1.0 + t) + 0.5 * x * (1.0 - t * t) * _GELU_K * (1.0 + 3.0 * 0.044715 * x * x)
    return val, grad


def _grouped_pieces(groups):
    out = []
    for ranges in groups:
        off, pieces = 0, []
        for lo, hi in ranges:
            pieces.append((off, lo, hi))
            off += hi - lo
        out.append(pieces)
    return out


def _mm_tn(a, b, *, out_dtype=F32, tm=1024, tn=1024, tk=1024, name, comm=None):
    k, m = a.shape
    n = b.shape[1]
    assert m % min(tm, m) == 0 and n % min(tn, n) == 0 and k % min(tk, k) == 0
    tm, tn, tk = min(tm, m), min(tn, n), min(tk, k)
    nk = k // tk
    grid = (m // tm, n // tn, nk)
    c_args, c_in, c_out, c_shape, c_scratch = _comm_extra(comm)

    def body(a_ref, b_ref, o_ref, acc_ref):
        kk = pl.program_id(2)

        @pl.when(kk == 0)
        def _():
            acc_ref[...] = jnp.zeros_like(acc_ref)

        acc_ref[...] += _dot(a_ref[...], b_ref[...], TN)

        @pl.when(kk == nk - 1)
        def _():
            o_ref[...] = acc_ref[...].astype(out_dtype)

    res = pl.pallas_call(
        _with_comm(body, comm, 2, 1, grid, -1), grid=grid,
        in_specs=[pl.BlockSpec((tk, tm), lambda i, j, kk: (kk, i)), pl.BlockSpec((tk, tn), lambda i, j, kk: (kk, j))] + c_in,
        out_specs=[pl.BlockSpec((tm, tn), lambda i, j, kk: (i, j))] + c_out,
        out_shape=[jax.ShapeDtypeStruct((m, n), out_dtype)] + c_shape,
        scratch_shapes=[pltpu.VMEM((tm, tn), F32)] + c_scratch,
        compiler_params=_params(*(("arbitrary",) * 3 if comm else ("parallel", "parallel", "arbitrary"))),
        name=name)(a, b, *c_args)
    return (res[0], list(res[1:])) if comm else res[0]


def _mm_tn_grouped(a, bs, groups, *, tm=512, tk=512, name, comm=None):
    k, m = a.shape
    tm, tk = min(tm, m), min(tk, k)
    assert m % tm == 0 and k % tk == 0
    nk, nb = k // tk, len(bs)
    n = sum(b.shape[1] for b in bs)
    ns = n // N_DEV
    pieces = []
    for plist in _grouped_pieces(groups):
        sub = []
        for off, lo, hi in plist:
            pos = lo
            while pos < hi:
                s = pos // ns
                end = min(hi, (s + 1) * ns)
                sub.append((s, pos - s * ns, end - s * ns, off + pos - lo))
                pos = end
        pieces.append(sub)
    grid = (m // tm, nk)
    c_args, c_in, c_out, c_shape, c_scratch = _comm_extra(comm)

    def body(a_ref, *refs):
        b_refs, o_ref, acc_refs = refs[:nb], refs[nb], refs[nb + 1:]
        kk = pl.program_id(1)
        av = a_ref[...]
        for b_ref, acc_ref, plist in zip(b_refs, acc_refs, pieces):
            @pl.when(kk == 0)
            def _():
                acc_ref[...] = jnp.zeros_like(acc_ref)

            acc_ref[...] += _dot(av, b_ref[...], TN)

            @pl.when(kk == nk - 1)
            def _():
                for s, c0, c1, off in plist:
                    o_ref[s % 2, s // 2, :, c0:c1] = acc_ref[:, off:off + c1 - c0].astype(BF16)

    res = pl.pallas_call(
        _with_comm(body, comm, 1 + nb, 1, grid, -1), grid=grid,
        in_specs=[pl.BlockSpec((tk, tm), lambda i, kk: (kk, i))]
                 + [pl.BlockSpec((tk, b.shape[1]), lambda i, kk: (kk, 0)) for b in bs] + c_in,
        out_specs=[pl.BlockSpec((2, N_DEV // 2, tm, ns), lambda i, kk: (0, 0, i, 0))] + c_out,
        out_shape=[jax.ShapeDtypeStruct((2, N_DEV // 2, m, ns), BF16)] + c_shape,
        scratch_shapes=[pltpu.VMEM((tm, b.shape[1]), F32) for b in bs] + c_scratch,
        compiler_params=_params("arbitrary", "arbitrary"), name=name)(a, *bs, *c_args)
    return res[0], list(res[1:])


def _ln_proj(x, g, shift, scale, w, groups, *, name, tm=512, comm=None):
    l, d = x.shape
    tm = min(tm, l)
    nb = l // tm
    pieces = _grouped_pieces(groups)
    widths = [sum(hi - lo for _, lo, hi in plist) for plist in pieces]
    nw = len(pieces)
    c_args, c_in, c_out, c_shape, c_scratch = _comm_extra(comm)

    def body(x_ref, g_ref, sh_ref, sc_ref, w_ref, h_ref, *p_refs):
        xv = x_ref[...]
        n = xv * lax.rsqrt(jnp.mean(xv * xv, axis=-1, keepdims=True) + EPS)
        h = ((n * g_ref[...]) * (1.0 + sc_ref[...]) + sh_ref[...]).astype(BF16)
        h_ref[...] = h
        for p_ref, plist in zip(p_refs, pieces):
            for off, lo, hi in plist:
                p_ref[:, off:off + hi - lo] = _dot(h, w_ref[:, lo:hi]).astype(BF16)

    vec = pl.BlockSpec((1, d), lambda i: (0, 0))
    row = lambda n: pl.BlockSpec((tm, n), lambda i: (i, 0))
    res = pl.pallas_call(
        _with_comm(body, comm, 5, 1 + nw, nb, -1), grid=(nb,),
        in_specs=[row(d), vec, vec, vec, pl.BlockSpec(w.shape, lambda i: (0, 0))] + c_in,
        out_specs=[row(d)] + [row(n) for n in widths] + c_out,
        out_shape=[jax.ShapeDtypeStruct((l, d), BF16)] + [jax.ShapeDtypeStruct((l, n), BF16) for n in widths] + c_shape,
        scratch_shapes=c_scratch,
        compiler_params=_params("arbitrary"), name=name)(x, g, shift, scale, w, *c_args)
    return res[0], list(res[1:1 + nw]), list(res[1 + nw:])


def _ln_proj_bwd(ds, w, groups, x, dres, g, scale, *, name, tm=256, comm=None):
    l, d = x.shape
    tm = min(tm, l)
    nb = l // tm
    nd = len(ds)
    pieces = _grouped_pieces(groups)
    c_args, c_in, c_out, c_shape, c_scratch = _comm_extra(comm)

    def body(*refs):
        d_refs = refs[:nd]
        w_ref, x_ref, dres_ref, g_ref, sc_ref, dx_ref, sums_ref = refs[nd:]
        dhv = None
        for d_ref, plist in zip(d_refs, pieces):
            for off, lo, hi in plist:
                term = _dot(d_ref[:, off:off + hi - lo], w_ref[:, lo:hi], NT)
                dhv = term if dhv is None else dhv + term
        xv = x_ref[...]
        rstd = lax.rsqrt(jnp.mean(xv * xv, axis=-1, keepdims=True) + EPS)
        n = xv * rstd
        gv = g_ref[...]
        dr = dhv * (1.0 + sc_ref[...])
        dn = dr * gv
        dx_ref[...] = dres_ref[...] + rstd * (dn - n * jnp.mean(dn * n, axis=-1, keepdims=True))

        @pl.when(pl.program_id(0) == 0)
        def _():
            sums_ref[...] = jnp.zeros_like(sums_ref)

        sums_ref[0:1, :] += jnp.sum(dhv, axis=0, keepdims=True)
        sums_ref[1:2, :] += jnp.sum(dhv * (n * gv), axis=0, keepdims=True)
        sums_ref[2:3, :] += jnp.sum(dr * n, axis=0, keepdims=True)

    vec = pl.BlockSpec((1, d), lambda i: (0, 0))
    row = pl.BlockSpec((tm, d), lambda i: (i, 0))
    res = pl.pallas_call(
        _with_comm(body, comm, nd + 5, 2, nb, -1), grid=(nb,),
        in_specs=[pl.BlockSpec((tm, a.shape[1]), lambda i: (i, 0)) for a in ds]
                 + [pl.BlockSpec(w.shape, lambda i: (0, 0)), row, row, vec, vec] + c_in,
        out_specs=[row, pl.BlockSpec((8, d), lambda i: (0, 0))] + c_out,
        out_shape=[jax.ShapeDtypeStruct((l, d), F32), jax.ShapeDtypeStruct((8, d), F32)] + c_shape,
        scratch_shapes=c_scratch,
        compiler_params=_params("arbitrary"), name=name)(*ds, w, x, dres, g, scale, *c_args)
    return res[0], res[1], list(res[2:])


def _final_loss(x, g, target, *, tm=512):
    l, d = x.shape

    def body(x_ref, g_ref, t_ref, dx_ref, sums_ref):
        xv = x_ref[...]
        rstd = lax.rsqrt(jnp.mean(xv * xv, axis=-1, keepdims=True) + EPS)
        n = xv * rstd
        gv = g_ref[...]
        err = n * gv - t_ref[...]
        dy = err * (1.0 / d)
        dn = dy * gv
        dx_ref[...] = rstd * (dn - n * jnp.mean(dn * n, axis=-1, keepdims=True))

        @pl.when(pl.program_id(0) == 0)
        def _():
            sums_ref[...] = jnp.zeros_like(sums_ref)

        sums_ref[0:1, :] += jnp.sum(dy * n, axis=0, keepdims=True)
        sums_ref[1:2, :] += jnp.sum(err * err, axis=0, keepdims=True) * (0.5 / d)

    vec = pl.BlockSpec((1, d), lambda i: (0, 0))
    row = pl.BlockSpec((tm, d), lambda i: (i, 0))
    dx, sums = pl.pallas_call(
        body, grid=(l // tm,),
        in_specs=[row, vec, row],
        out_specs=[row, pl.BlockSpec((8, d), lambda i: (0, 0))],
        out_shape=[jax.ShapeDtypeStruct((l, d), F32), jax.ShapeDtypeStruct((8, d), F32)],
        compiler_params=_params("arbitrary"), name="final_loss")(x, g, target)
    return dx, sums


def _attn_geometry(i, t):
    nk = t + WINDOW
    qi = lax.broadcasted_iota(jnp.int32, (t, nk), 0)
    kj = lax.broadcasted_iota(jnp.int32, (t, nk), 1)
    dist = jnp.abs(qi + WINDOW - kj).astype(F32)
    qc = jnp.right_shift(qi, 6)
    kc = jnp.right_shift(kj, 6)
    valid = (kc >= qc) & (kc <= qc + WINDOW // CHUNK) & ((i > 0) | (kj >= WINDOW))
    return dist, valid


def _attn_head(q, k_all, v_all, sink, slope, dist, valid):
    s = _dot(q, k_all, NT) * (1.0 / math.sqrt(HEAD_DIM)) - slope * dist
    s = jnp.where(valid, s, NEG_INF)
    m = jnp.maximum(jnp.max(s, axis=-1, keepdims=True), sink)
    e = jnp.exp(s - m)
    es = jnp.exp(sink - m)
    inv = 1.0 / (jnp.sum(e, axis=-1, keepdims=True) + es)
    p = e * inv
    o = _dot(p, v_all, NN)
    return p, o, es * inv


def _attn_specs(t):
    cur = pl.BlockSpec((t, ATT_W * 2 + KV_W * 2), lambda i: (i, 0))
    halo_blocks = t // WINDOW
    prev = pl.BlockSpec((WINDOW, 2 * KV_W), lambda i: (jnp.maximum(i * halo_blocks - 1, 0), (2 * ATT_W) // (2 * KV_W)))
    return cur, prev


def _attn_fwd(pa, sinks, *, name, t=ATT_BLOCK, comm=None):
    l = pa.shape[0]
    t = min(t, l)
    nb = l // t
    c_args, c_in, c_out, c_shape, c_scratch = _comm_extra(comm)

    def body(sink_ref, cur_ref, prev_ref, ya_ref):
        i = pl.program_id(0)
        dist, valid = _attn_geometry(i, t)
        for h in range(N_HEADS):
            kh = h // Q_PER_KV
            q = cur_ref[:, h * HEAD_DIM:(h + 1) * HEAD_DIM]
            z = cur_ref[:, ATT_W + h * HEAD_DIM:ATT_W + (h + 1) * HEAD_DIM].astype(F32)
            k_all = jnp.concatenate([prev_ref[:, kh * HEAD_DIM:(kh + 1) * HEAD_DIM],
                                     cur_ref[:, 2 * ATT_W + kh * HEAD_DIM:2 * ATT_W + (kh + 1) * HEAD_DIM]], axis=0)
            v_all = jnp.concatenate([prev_ref[:, KV_W + kh * HEAD_DIM:KV_W + (kh + 1) * HEAD_DIM],
                                     cur_ref[:, 2 * ATT_W + KV_W + kh * HEAD_DIM:2 * ATT_W + KV_W + (kh + 1) * HEAD_DIM]], axis=0)
            _, o, _ = _attn_head(q, k_all, v_all, sink_ref[h], 2.0 ** (-(h + 1)), dist, valid)
            sz, _ = _silu_and_grad(z)
            ya_ref[:, h * HEAD_DIM:(h + 1) * HEAD_DIM] = (o * sz).astype(BF16)

    cur, prev = _attn_specs(t)
    res = pl.pallas_call(
        _with_comm(body, comm, 3, 1, nb, nb - 1), grid=(nb,),
        in_specs=[pl.BlockSpec(memory_space=pltpu.SMEM), cur, prev] + c_in,
        out_specs=[pl.BlockSpec((t, ATT_W), lambda i: (i, 0))] + c_out,
        out_shape=[jax.ShapeDtypeStruct((l, ATT_W), BF16)] + c_shape,
        scratch_shapes=c_scratch,
        compiler_params=_params("arbitrary"), name=name)(sinks, pa, pa, *c_args)
    return res[0], res[1:]


def _attn_bwd(pa, sinks, dya, *, name, t=SEQ_BLOCK):
    l = pa.shape[0]
    t = min(t, l)
    nb = l // t
    scale = 1.0 / math.sqrt(HEAD_DIM)

    def body(sink_ref, cur_ref, prev_ref, dya_ref, dpa_ref, dsink_ref, carry_ref):
        n = pl.program_id(0)
        i = nb - 1 - n
        dist, valid = _attn_geometry(i, t)

        @pl.when(n == 0)
        def _():
            carry_ref[...] = jnp.zeros_like(carry_ref)
            dsink_ref[...] = jnp.zeros_like(dsink_ref)

        dk_acc = [jnp.zeros((HEAD_DIM, t + WINDOW), F32) for _ in range(N_KV_HEADS)]
        dv_acc = [jnp.zeros((HEAD_DIM, t + WINDOW), F32) for _ in range(N_KV_HEADS)]
        for h in range(N_HEADS):
            kh = h // Q_PER_KV
            q = cur_ref[:, h * HEAD_DIM:(h + 1) * HEAD_DIM]
            z = cur_ref[:, ATT_W + h * HEAD_DIM:ATT_W + (h + 1) * HEAD_DIM].astype(F32)
            k_all = jnp.concatenate([prev_ref[:, kh * HEAD_DIM:(kh + 1) * HEAD_DIM],
                                     cur_ref[:, 2 * ATT_W + kh * HEAD_DIM:2 * ATT_W + (kh + 1) * HEAD_DIM]], axis=0)
            v_all = jnp.concatenate([prev_ref[:, KV_W + kh * HEAD_DIM:KV_W + (kh + 1) * HEAD_DIM],
                                     cur_ref[:, 2 * ATT_W + KV_W + kh * HEAD_DIM:2 * ATT_W + KV_W + (kh + 1) * HEAD_DIM]], axis=0)
            p, o, p_sink = _attn_head(q, k_all, v_all, sink_ref[h], 2.0 ** (-(h + 1)), dist, valid)
            dy = dya_ref[:, h * HEAD_DIM:(h + 1) * HEAD_DIM]
            sz, dsz = _silu_and_grad(z)
            do = dy * sz
            dpa_ref[:, ATT_W + h * HEAD_DIM:ATT_W + (h + 1) * HEAD_DIM] = (dy * o * dsz).astype(BF16)
            dp = _dot(do, v_all, NT)
            delta = jnp.sum(p * dp, axis=-1, keepdims=True)
            ds = p * (dp - delta)
            dpa_ref[:, h * HEAD_DIM:(h + 1) * HEAD_DIM] = (_dot(ds, k_all, NN) * scale).astype(BF16)
            dk_acc[kh] = dk_acc[kh] + _dot(q, ds, TN) * scale
            dv_acc[kh] = dv_acc[kh] + _dot(do, p, TN)
            dsink_ref[h:h + 1, :] += jnp.broadcast_to(-jnp.sum(p_sink * delta, axis=0, keepdims=True), (1, 128))

        acc = jnp.concatenate(dk_acc + dv_acc, axis=0).T
        own = acc[WINDOW:, :]
        tail = own[t - WINDOW:, :] + carry_ref[...]
        if t > WINDOW:
            dpa_ref[0:t - WINDOW, 2 * ATT_W:] = own[:t - WINDOW, :].astype(BF16)
        dpa_ref[t - WINDOW:t, 2 * ATT_W:] = tail.astype(BF16)
        carry_ref[...] = acc[:WINDOW, :]

    halo_blocks = t // WINDOW
    wpa = 2 * ATT_W + 2 * KV_W
    cur = pl.BlockSpec((t, wpa), lambda n: (nb - 1 - n, 0))
    prev = pl.BlockSpec((WINDOW, 2 * KV_W),
                        lambda n: (jnp.maximum((nb - 1 - n) * halo_blocks - 1, 0), (2 * ATT_W) // (2 * KV_W)))
    return pl.pallas_call(
        body, grid=(nb,),
        in_specs=[pl.BlockSpec(memory_space=pltpu.SMEM), cur, prev, pl.BlockSpec((t, ATT_W), lambda n: (nb - 1 - n, 0))],
        out_specs=[pl.BlockSpec((t, wpa), lambda n: (nb - 1 - n, 0)), pl.BlockSpec((8, 128), lambda n: (0, 0))],
        out_shape=[jax.ShapeDtypeStruct((l, wpa), BF16), jax.ShapeDtypeStruct((8, 128), F32)],
        scratch_shapes=[pltpu.VMEM((WINDOW, 2 * KV_W), F32)],
        compiler_params=_params("arbitrary"), name=name)(sinks, pa, pa, dya)


def _scan(xr, xi, lr, li, t, reverse):
    row = lax.broadcasted_iota(jnp.int32, (t, 1), 0)
    d = 1
    pr, pi = lr, li
    while d < t:
        if reverse:
            sr = jnp.where(row < t - d, pltpu.roll(xr, t - d, 0), 0.0)
            si = jnp.where(row < t - d, pltpu.roll(xi, t - d, 0), 0.0)
        else:
            sr = jnp.where(row >= d, pltpu.roll(xr, d, 0), 0.0)
            si = jnp.where(row >= d, pltpu.roll(xi, d, 0), 0.0)
        xr, xi = xr + pr * sr - pi * si, xi + pr * si + pi * sr
        pr, pi = pr * pr - pi * pi, 2.0 * pr * pi
        d *= 2
    return xr, xi


SCAN_SUB = 8


def _split_hi_lo(a):
    hi = a.astype(BF16)
    lo = (a - hi.astype(F32)).astype(BF16)
    return jnp.concatenate([hi, lo], axis=0)


def _scan_mxu(xr, xi, tab, lam3, lam8, tri, expand, cr, ci, t, reverse):
    ns = t // SCAN_SUB
    n = xr.shape[1]
    v3 = lambda a: a.reshape(ns, SCAN_SUB, n)
    x3r, x3i = v3(xr), v3(xi)
    br = (x3r * tab[0] - x3i * tab[1]).reshape(t, n)
    bi = (x3r * tab[1] + x3i * tab[0]).reshape(t, n)
    pm = jnp.dot(tri, jnp.concatenate([br, bi], axis=1).astype(BF16), preferred_element_type=F32)
    p3r, p3i = v3(pm[:t, :n]), v3(pm[:t, n:])
    slr = p3r * tab[2] - p3i * tab[3]
    sli = p3r * tab[3] + p3i * tab[2]
    totr, toti = pm[t:, :n], pm[t:, n:]
    l3r, l3i = lam3
    l8r, l8i = lam8
    row = lax.broadcasted_iota(jnp.int32, (ns, 1), 0)
    edge = row == (ns - 1 if reverse else 0)
    er = totr * l3r - toti * l3i + jnp.where(edge, l8r * cr - l8i * ci, 0.0)
    ei = totr * l3i + toti * l3r + jnp.where(edge, l8r * ci + l8i * cr, 0.0)
    er, ei = _scan(er, ei, l8r, l8i, ns, reverse)
    shift = ns - 1 if reverse else 1
    nbr = jnp.where(edge, cr, pltpu.roll(er, shift, 0))
    nbi = jnp.where(edge, ci, pltpu.roll(ei, shift, 0))
    ex = jnp.dot(expand, _split_hi_lo(jnp.concatenate([nbr, nbi], axis=1)), preferred_element_type=F32)
    e3r, e3i = v3(ex[:, :n]), v3(ex[:, n:])
    sr = (slr + e3r * tab[4] - e3i * tab[5]).reshape(t, n)
    si = (sli + e3r * tab[5] + e3i * tab[4]).reshape(t, n)
    out = 0 if reverse else ns - 1
    return sr, si, er[out:out + 1, :], ei[out:out + 1, :]


def _scan_consts(t):
    import numpy as np
    ns = t // SCAN_SUB
    r = np.arange(t)
    same = (r[:, None] // SCAN_SUB) == (r[None, :] // SCAN_SUB)
    sums = (np.arange(ns)[:, None] == (r[None, :] // SCAN_SUB))
    tri = []
    for keep in (r[None, :] <= r[:, None], r[None, :] >= r[:, None]):
        tri.append(np.concatenate([same & keep, sums], axis=0).astype(np.float32))
    ex = ((r[:, None] // SCAN_SUB) == np.arange(ns)[None, :]).astype(np.float32)
    return jnp.asarray(np.stack(tri), BF16), jnp.asarray(np.concatenate([ex, ex], axis=1), BF16)


def _scan_tables(lr, li):
    import numpy as np
    den = lr * lr + li * li
    ir, ii = lr / den, -li / den
    mul = lambda a, b: (a[0] * b[0] - a[1] * b[1], a[0] * b[1] + a[1] * b[0])
    pw = {0: (jnp.ones_like(lr), jnp.zeros_like(lr))}
    for e in range(1, 9):
        pw[e] = mul(pw[e - 1], (lr, li))
    for e in range(-1, -5, -1):
        pw[e] = mul(pw[e + 1], (ir, ii))
    powers = jnp.stack([jnp.stack(pw[e]) for e in range(-4, 9)] + [jnp.zeros((2, lr.shape[0]), F32)])
    j = np.arange(SCAN_SUB)
    exps = [4 - j, j - 4, j + 1, j - 3, 3 - j, 8 - j]
    e_idx = np.stack([exps[t] + 4 for t in range(6) for _ in range(2)])
    c_idx = np.tile(np.array([0, 1])[:, None], (6, SCAN_SUB))
    sign = np.where((c_idx == 1) & (np.arange(12)[:, None] >= 6), -1.0, 1.0).astype(np.float32)
    tabs = powers[e_idx, c_idx] * sign[:, :, None]
    lam = powers[np.array([5, 5, 7, 7, 12, 12, 13, 13]), np.array([0, 1, 0, 1, 0, 1, 0, 0])]
    return lam, tabs


SSM_HALVES = 2
SSM_HW = SSM_W // SSM_HALVES
SSM_HN = SSM_N // SSM_HALVES


def _bd_nn(x, w):
    a = w.shape[1]
    return jnp.concatenate([_dot(x[:, h * a:(h + 1) * a], w[h]) for h in range(SSM_HALVES)], axis=1)


def _bd_nt(x, w):
    b = w.shape[2]
    return jnp.concatenate([_dot(x[:, h * b:(h + 1) * b], w[h], NT) for h in range(SSM_HALVES)], axis=1)


def _bd_tn(x, y):
    a, b = x.shape[1] // SSM_HALVES, y.shape[1] // SSM_HALVES
    return jnp.stack([_dot(x[:, h * a:(h + 1) * a], y[:, h * b:(h + 1) * b], TN) for h in range(SSM_HALVES)])


def _ssm_states(u, s0r, s0i, lam_ref, tab_ref, tri_ref, ex_ref, bre, bim, t):
    tab = tuple(tab_ref[k] for k in range(6))
    return _scan_mxu(_bd_nn(u, bre), _bd_nn(u, bim), tab, (lam_ref[2:3, :], lam_ref[3:4, :]),
                     (lam_ref[4:5, :], lam_ref[5:6, :]), tri_ref[0], ex_ref[...], s0r, s0i, t, False)


def _ssm_head(u, z, xr, xi, cre, cim, dskip, wglu, bglu):
    y = _bd_nn(xr, cre) - _bd_nn(xi, cim) + dskip * u
    y2, dgelu = _gelu_and_grad(y)
    gate = _sigmoid(_dot(y2, wglu) + bglu)
    y3 = y2 * gate
    return y2, dgelu, gate, y3


def _with_comm(body, comm, n_in, n_out, grid, mid_step):
    if comm is None:
        return body
    nc = len(comm["args"])
    n_sem = len(comm["scratch"])
    grid = (grid,) if isinstance(grid, int) else tuple(grid)
    total = math.prod(grid)

    def hosted(*refs):
        ins, cin = refs[:n_in], refs[n_in:n_in + nc]
        outs, cout = refs[n_in + nc:n_in + nc + n_out], refs[n_in + nc + n_out:n_in + 2 * nc + n_out]
        rest = refs[n_in + 2 * nc + n_out:]
        scratch, csem = rest[:len(rest) - n_sem], rest[len(rest) - n_sem:]
        start, forward, finish = comm["phases"](cin, cout, *csem)
        step = pl.program_id(0)
        for axis in range(1, len(grid)):
            step = step * grid[axis] + pl.program_id(axis)
        pl.when(step == 0)(start)
        pl.when(step == (mid_step if mid_step >= 0 else total + mid_step))(forward)
        body(*ins, *outs, *scratch)
        pl.when(step == total - 1)(finish)

    return hosted


def _comm_extra(comm):
    if comm is None:
        return [], [], [], [], []
    anyspec = pl.BlockSpec(memory_space=pl.ANY)
    nc = len(comm["args"])
    return comm["args"], [anyspec] * nc, [anyspec] * nc, comm["out_shape"], comm["scratch"]


def _ssm_fwd(ps, scan_ops, bblk, cblk, dskip, wglu, bglu, *, name, t=SEQ_BLOCK, comm=None):
    l = ps.shape[0]
    assert l % t == 0
    nb = l // t
    ns = t // SCAN_SUB
    c_args, c_in, c_out, c_shape, c_scratch = _comm_extra(comm)

    def body(ps_ref, lam_ref, tab_ref, tri_ref, ex_ref, b_ref, c_ref, d_ref, w_ref, bg_ref, ys_ref, chk_ref, xs_ref,
             st_ref):
        @pl.when(pl.program_id(0) == 0)
        def _():
            st_ref[...] = jnp.zeros_like(st_ref)

        chk_ref[...] = jnp.broadcast_to(st_ref[...], chk_ref.shape)
        u = ps_ref[:, :SSM_W].astype(F32)
        z = ps_ref[:, SSM_W:].astype(F32)
        xr, xi, er, ei = _ssm_states(u, st_ref[:, :SSM_N], st_ref[:, SSM_N:], lam_ref, tab_ref, tri_ref, ex_ref,
                                     b_ref[0], b_ref[1], t)
        st_ref[:, :SSM_N] = er
        st_ref[:, SSM_N:] = ei
        xr, xi = xr.astype(BF16), xi.astype(BF16)
        xs_ref[:, :SSM_N] = xr
        xs_ref[:, SSM_N:] = xi
        _, _, _, y3 = _ssm_head(u, z, xr, xi, c_ref[0], c_ref[1], d_ref[...], w_ref[...], bg_ref[...])
        sz, _ = _silu_and_grad(z)
        ys_ref[...] = (y3 * sz).astype(BF16)

    full = lambda shape: pl.BlockSpec(shape, lambda i: (0,) * len(shape))
    return pl.pallas_call(
        _with_comm(body, comm, 10, 3, nb, nb - 1), grid=(nb,),
        in_specs=[pl.BlockSpec((t, 2 * SSM_W), lambda i: (i, 0)), full((8, SSM_N)), full((12, SCAN_SUB, SSM_N)),
                  full((2, t + ns, t)), full((t, 2 * ns)), full((2, SSM_HALVES, SSM_HW, SSM_HN)),
                  full((2, SSM_HALVES, SSM_HN, SSM_HW)), full((1, SSM_W)), full((SSM_W, SSM_W)), full((1, SSM_W))] + c_in,
        out_specs=[pl.BlockSpec((t, SSM_W), lambda i: (i, 0)), pl.BlockSpec((8, 2 * SSM_N), lambda i: (i, 0)),
                   pl.BlockSpec((t, 2 * SSM_N), lambda i: (i, 0))] + c_out,
        out_shape=[jax.ShapeDtypeStruct((l, SSM_W), BF16), jax.ShapeDtypeStruct((nb * 8, 2 * SSM_N), F32),
                   jax.ShapeDtypeStruct((l, 2 * SSM_N), BF16)] + c_shape,
        scratch_shapes=[pltpu.VMEM((1, 2 * SSM_N), F32)] + c_scratch,
        compiler_params=_params("arbitrary"), name=name)(ps, *scan_ops, bblk, cblk, dskip, wglu, bglu, *c_args)


def _ssm_bwd(ps, dys, chk, states, scan_ops, bblk, cblk, dskip, wglu, bglu, *, name, t=SEQ_BLOCK, comm=None):
    l = ps.shape[0]
    assert l % t == 0
    nb = l // t
    ns = t // SCAN_SUB
    c_args, c_in, c_out, c_shape, c_scratch = _comm_extra(comm)

    def body(ps_ref, dys_ref, chk_ref, xs_ref, lam_ref, tab_ref, tri_ref, ex_ref, b_ref, c_ref, d_ref, w_ref, bg_ref,
             dps_ref, db_ref, dc_ref, dw_acc, sums_acc, gc_ref, db_acc, dc_acc):
        n = pl.program_id(0)

        @pl.when(n == 0)
        def _():
            gc_ref[...] = jnp.zeros_like(gc_ref)
            db_acc[...] = jnp.zeros_like(db_acc)
            dc_acc[...] = jnp.zeros_like(dc_acc)
            dw_acc[...] = jnp.zeros_like(dw_acc)
            sums_acc[...] = jnp.zeros_like(sums_acc)

        row = lax.broadcasted_iota(jnp.int32, (t, 1), 0)
        u = ps_ref[:, :SSM_W].astype(F32)
        z = ps_ref[:, SSM_W:].astype(F32)
        s0r, s0i = chk_ref[0:1, :SSM_N], chk_ref[0:1, SSM_N:]
        xr, xi = xs_ref[:, :SSM_N], xs_ref[:, SSM_N:]
        dskip = d_ref[...]
        y2, dgelu, gate, y3 = _ssm_head(u, z, xr, xi, c_ref[0], c_ref[1], dskip, w_ref[...], bg_ref[...])
        sz, dsz = _silu_and_grad(z)
        dys_v = dys_ref[...]
        dps_ref[:, SSM_W:] = (dys_v * y3 * dsz).astype(BF16)
        dy3 = dys_v * sz
        da = dy3 * y2 * gate * (1.0 - gate)
        dy2 = dy3 * gate + _dot(da, w_ref[...], NT)
        dw_acc[...] += _dot(y2, da, TN)
        dy = dy2 * dgelu
        sums_acc[2:3, :SSM_W] += jnp.sum(dy * u, axis=0, keepdims=True)
        sums_acc[3:4, :SSM_W] += jnp.sum(da, axis=0, keepdims=True)
        dc_acc[0] += _bd_tn(dy, xr)
        dc_acc[1] += -_bd_tn(dy, xi)
        rev_tab = tuple(tab_ref[k] for k in range(6, 12))
        gr, gi, gcr, gci = _scan_mxu(
            _bd_nt(dy, c_ref[0]), -_bd_nt(dy, c_ref[1]), rev_tab, (lam_ref[2:3, :], -lam_ref[3:4, :]),
            (lam_ref[4:5, :], -lam_ref[5:6, :]), tri_ref[1], ex_ref[...], gc_ref[:, :SSM_N], gc_ref[:, SSM_N:], t, True)
        gc_ref[:, :SSM_N] = gcr
        gc_ref[:, SSM_N:] = gci
        db_acc[0] += _bd_tn(u, gr)
        db_acc[1] += _bd_tn(u, gi)
        du = dskip * dy + _bd_nt(gr, b_ref[0]) + _bd_nt(gi, b_ref[1])
        dps_ref[:, :SSM_W] = du.astype(BF16)
        spr = jnp.where(row == 0, s0r, pltpu.roll(xr.astype(F32), 1, 0))
        spi = jnp.where(row == 0, s0i, pltpu.roll(xi.astype(F32), 1, 0))
        sums_acc[0:1, :] += jnp.sum(gr * spr + gi * spi, axis=0, keepdims=True)
        sums_acc[1:2, :] += jnp.sum(gi * spr - gr * spi, axis=0, keepdims=True)

        @pl.when(n == nb - 1)
        def _():
            per_half = SSM_GROUPS // SSM_HALVES
            for k in range(2):
                for g in range(SSM_GROUPS):
                    h, gl = divmod(g, per_half)
                    c0, p0 = gl * SSM_GROUP, gl * SSM_STATE
                    db_ref[k, g * SSM_GROUP:(g + 1) * SSM_GROUP, :] = db_acc[k, h, c0:c0 + SSM_GROUP, p0:p0 + SSM_STATE]
                    dc_ref[k, g * SSM_GROUP:(g + 1) * SSM_GROUP, :] = dc_acc[k, h, c0:c0 + SSM_GROUP, p0:p0 + SSM_STATE]

    full = lambda shape: pl.BlockSpec(shape, lambda n: (0,) * len(shape))
    return pl.pallas_call(
        _with_comm(body, comm, 13, 5, nb, 0), grid=(nb,),
        in_specs=[pl.BlockSpec((t, 2 * SSM_W), lambda n: (nb - 1 - n, 0)),
                  pl.BlockSpec((t, SSM_W), lambda n: (nb - 1 - n, 0)),
                  pl.BlockSpec((8, 2 * SSM_N), lambda n: (nb - 1 - n, 0)),
                  pl.BlockSpec((t, 2 * SSM_N), lambda n: (nb - 1 - n, 0)),
                  full((8, SSM_N)), full((12, SCAN_SUB, SSM_N)), full((2, t + ns, t)), full((t, 2 * ns)),
                  full((2, SSM_HALVES, SSM_HW, SSM_HN)), full((2, SSM_HALVES, SSM_HN, SSM_HW)), full((1, SSM_W)),
                  full((SSM_W, SSM_W)), full((1, SSM_W))] + c_in,
        out_specs=[pl.BlockSpec((t, 2 * SSM_W), lambda n: (nb - 1 - n, 0)), full((2, SSM_W, SSM_STATE)),
                   full((2, SSM_W, SSM_STATE)), full((SSM_W, SSM_W)), full((8, SSM_N))] + c_out,
        out_shape=[jax.ShapeDtypeStruct((l, 2 * SSM_W), BF16),
                   jax.ShapeDtypeStruct((2, SSM_W, SSM_STATE), F32),
                   jax.ShapeDtypeStruct((2, SSM_W, SSM_STATE), F32),
                   jax.ShapeDtypeStruct((SSM_W, SSM_W), F32),
                   jax.ShapeDtypeStruct((8, SSM_N), F32)] + c_shape,
        scratch_shapes=[pltpu.VMEM((1, 2 * SSM_N), F32), pltpu.VMEM((2, SSM_HALVES, SSM_HW, SSM_HN), F32),
                        pltpu.VMEM((2, SSM_HALVES, SSM_HW, SSM_HN), F32)] + c_scratch,
        compiler_params=_params("arbitrary"), name=name)(ps, dys, chk, states, *scan_ops, bblk, cblk, dskip, wglu, bglu,
                                                         *c_args)


def _pool_count(i, t):
    pos = lax.broadcasted_iota(jnp.int32, (t, POOL_W), 0) + i * t + 1
    col = lax.broadcasted_iota(jnp.int32, (t, POOL_W), 1)
    win = jnp.where(col < POOL_GW, 2, jnp.where(col < 2 * POOL_GW, 4, jnp.where(col < 3 * POOL_GW, 8, 16)))
    return 1.0 / jnp.minimum(pos, win).astype(F32), col


def _window_sums(ext, n_rows, forward):
    col = lax.broadcasted_iota(jnp.int32, ext.shape, 1)
    sh = (lambda a, d: pltpu.roll(a, d, 0)) if forward else (lambda a, d: pltpu.roll(a, n_rows - d, 0))
    a2 = ext + sh(ext, 1)
    a4 = a2 + sh(a2, 2)
    a8 = a4 + sh(a4, 4)
    a16 = a8 + sh(a8, 8)
    return jnp.where(col < POOL_GW, a2, jnp.where(col < 2 * POOL_GW, a4, jnp.where(col < 3 * POOL_GW, a8, a16)))


def _pool_mix(pooled, wp_ref):
    return jnp.concatenate([_dot(pooled[:, g * POOL_GW:(g + 1) * POOL_GW], wp_ref[g]) for g in range(4)], axis=1)


def _pool_pooled(i, cur_u, prev_u, t):
    prev = jnp.where(i > 0, prev_u, 0.0)
    ext = jnp.concatenate([prev, cur_u], axis=0)
    inv_cnt, _ = _pool_count(i, t)
    return _window_sums(ext, t + POOL_HALO, True)[POOL_HALO:, :] * inv_cnt - cur_u


def _pool_fwd(pp, wpool, pscale, *, name, t=SEQ_BLOCK):
    l = pp.shape[0]
    t = min(t, l)

    def body(cur_ref, prev_ref, wp_ref, sc_ref, yp_ref):
        i = pl.program_id(0)
        pooled = _pool_pooled(i, cur_ref[:, :POOL_W].astype(F32), prev_ref[...].astype(F32), t)
        lin = _pool_mix(pooled, wp_ref)
        sz, _ = _silu_and_grad(cur_ref[:, POOL_W:].astype(F32))
        yp_ref[...] = (lin * sc_ref[...] * sz).astype(BF16)

    hb = t // POOL_HALO
    return pl.pallas_call(
        body, grid=(l // t,),
        in_specs=[pl.BlockSpec((t, 2 * POOL_W), lambda i: (i, 0)),
                  pl.BlockSpec((POOL_HALO, POOL_W), lambda i: (jnp.maximum(i * hb - 1, 0), 0)),
                  pl.BlockSpec((4, POOL_GW, POOL_GW), lambda i: (0, 0, 0)),
                  pl.BlockSpec((1, POOL_W), lambda i: (0, 0))],
        out_specs=pl.BlockSpec((t, POOL_W), lambda i: (i, 0)),
        out_shape=jax.ShapeDtypeStruct((l, POOL_W), BF16),
        compiler_params=_params("parallel"), name=name)(pp, pp, wpool, pscale)


def _pool_bwd(pp, dyp, wpool, pscale, *, name, t=SEQ_BLOCK):
    l = pp.shape[0]
    t = min(t, l)
    nb = l // t

    def body(cur_ref, prev_ref, dyp_ref, wp_ref, sc_ref, dpp_ref, dwp_ref, sums_ref, carry_ref):
        n = pl.program_id(0)
        i = nb - 1 - n

        @pl.when(n == 0)
        def _():
            carry_ref[...] = jnp.zeros_like(carry_ref)
            dwp_ref[...] = jnp.zeros_like(dwp_ref)
            sums_ref[...] = jnp.zeros_like(sums_ref)

        cur_u = cur_ref[:, :POOL_W].astype(F32)
        pooled = _pool_pooled(i, cur_u, prev_ref[...].astype(F32), t)
        lin = _pool_mix(pooled, wp_ref)
        sz, dsz = _silu_and_grad(cur_ref[:, POOL_W:].astype(F32))
        dyp_v = dyp_ref[...]
        scale = sc_ref[...]
        dpp_ref[:, POOL_W:] = (dyp_v * lin * scale * dsz).astype(BF16)
        dpre = dyp_v * sz
        sums_ref[0:1, :] += jnp.sum(dpre * lin, axis=0, keepdims=True)
        dlin = dpre * scale
        dpooled = []
        for g in range(4):
            dl = dlin[:, g * POOL_GW:(g + 1) * POOL_GW]
            dwp_ref[g] += _dot(pooled[:, g * POOL_GW:(g + 1) * POOL_GW], dl, TN)
            dpooled.append(_dot(dl, wp_ref[g], NT))
        dpooled = jnp.concatenate(dpooled, axis=1)
        inv_cnt, _ = _pool_count(i, t)
        dq = dpooled * inv_cnt
        ext = jnp.concatenate([dq, carry_ref[...]], axis=0)
        du = _window_sums(ext, t + POOL_HALO, False)[:t, :] - dpooled
        dpp_ref[:, :POOL_W] = du.astype(BF16)
        carry_ref[...] = dq[:POOL_HALO, :]

    hb = t // POOL_HALO
    return pl.pallas_call(
        body, grid=(nb,),
        in_specs=[pl.BlockSpec((t, 2 * POOL_W), lambda n: (nb - 1 - n, 0)),
                  pl.BlockSpec((POOL_HALO, POOL_W), lambda n: (jnp.maximum((nb - 1 - n) * hb - 1, 0), 0)),
                  pl.BlockSpec((t, POOL_W), lambda n: (nb - 1 - n, 0)),
                  pl.BlockSpec((4, POOL_GW, POOL_GW), lambda n: (0, 0, 0)),
                  pl.BlockSpec((1, POOL_W), lambda n: (0, 0))],
        out_specs=[pl.BlockSpec((t, 2 * POOL_W), lambda n: (nb - 1 - n, 0)),
                   pl.BlockSpec((4, POOL_GW, POOL_GW), lambda n: (0, 0, 0)),
                   pl.BlockSpec((8, POOL_W), lambda n: (0, 0))],
        out_shape=[jax.ShapeDtypeStruct((l, 2 * POOL_W), BF16), jax.ShapeDtypeStruct((4, POOL_GW, POOL_GW), F32),
                   jax.ShapeDtypeStruct((8, POOL_W), F32)],
        scratch_shapes=[pltpu.VMEM((POOL_HALO, POOL_W), F32)],
        compiler_params=_params("arbitrary"), name=name)(pp, pp, dyp, wpool, pscale)


def _merge_fwd(ya, ys, yp, wa, ws, wp, pg, wout, x, gate, *, name, tm=512):
    l, d = x.shape
    tm = min(tm, l)

    def body(ya_ref, ys_ref, yp_ref, wa_ref, ws_ref, wp_ref, pg_ref, wo_ref, x_ref, g_ref,
             xn_ref, mg_ref, ba_ref, bs_ref, bp_ref, out_ref):
        acc = None
        for k, (y_ref, w_ref, b_ref) in enumerate(((ya_ref, wa_ref, ba_ref), (ys_ref, ws_ref, bs_ref),
                                                   (yp_ref, wp_ref, bp_ref))):
            br = _dot(y_ref[...], w_ref[...])
            b_ref[...] = br.astype(BF16)
            term = _sigmoid(pg_ref[:, k * d:(k + 1) * d].astype(F32)) * br
            acc = term if acc is None else acc + term
        merged = acc.astype(BF16)
        mg_ref[...] = merged
        out = _dot(merged, wo_ref[...])
        out_ref[...] = out.astype(BF16)
        xn_ref[...] = x_ref[...] + g_ref[...] * out

    rowy = pl.BlockSpec((tm, ATT_W), lambda i: (i, 0))
    wsp = pl.BlockSpec((ATT_W, d), lambda i: (0, 0))
    rowd = pl.BlockSpec((tm, d), lambda i: (i, 0))
    return pl.pallas_call(
        body, grid=(l // tm,),
        in_specs=[rowy, rowy, rowy, wsp, wsp, wsp, pl.BlockSpec((tm, 3 * d), lambda i: (i, 0)),
                  pl.BlockSpec((d, d), lambda i: (0, 0)), rowd, pl.BlockSpec((1, d), lambda i: (0, 0))],
        out_specs=[rowd] * 6,
        out_shape=[jax.ShapeDtypeStruct((l, d), F32)] + [jax.ShapeDtypeStruct((l, d), BF16)] * 5,
        compiler_params=_params("parallel"), name=name)(ya, ys, yp, wa, ws, wp, pg, wout, x, gate)


def _merge_bwd(dx, out, gate, wout, pg, brs, wbrs, ys, merged, *, name, tm=256, comm=None):
    l, d = dx.shape
    tm = min(tm, l)
    nb = l // tm
    w = ys[0].shape[1]

    def body(dx_ref, out_ref, g_ref, w_ref, pg_ref, ba_ref, bs_ref, bp_ref, wa_ref, ws_ref, wp_ref,
             ya_ref, ys_ref, yp_ref, mg_ref,
             dya_ref, dys_ref, dyp_ref, dpg_ref, sums_ref, dwa_ref, dws_ref, dwp_ref, dwo_ref, acc_br, acc_out):
        i = pl.program_id(0)

        @pl.when(i == 0)
        def _():
            sums_ref[...] = jnp.zeros_like(sums_ref)
            acc_br[...] = jnp.zeros_like(acc_br)
            acc_out[...] = jnp.zeros_like(acc_out)

        dxv = dx_ref[...]
        sums_ref[0:1, :] += jnp.sum(dxv * out_ref[...].astype(F32), axis=0, keepdims=True)
        dmo = (dxv * g_ref[...]).astype(BF16)
        acc_out[...] += _dot(mg_ref[...], dmo, TN)
        dmerged = _dot(dmo, w_ref[...], NT)
        branches = ((ba_ref, wa_ref, ya_ref, dya_ref), (bs_ref, ws_ref, ys_ref, dys_ref), (bp_ref, wp_ref, yp_ref, dyp_ref))
        for k, (b_ref, wk_ref, y_ref, dy_ref) in enumerate(branches):
            gk = _sigmoid(pg_ref[:, k * d:(k + 1) * d].astype(F32))
            dbr = (dmerged * gk).astype(BF16)
            dpg_ref[:, k * d:(k + 1) * d] = (dmerged * b_ref[...].astype(F32) * gk * (1.0 - gk)).astype(BF16)
            dy_ref[...] = _dot(dbr, wk_ref[...], NT)
            acc_br[k] += _dot(y_ref[...], dbr, TN)

        @pl.when(i == nb - 1)
        def _():
            for k, dw_ref in enumerate((dwa_ref, dws_ref, dwp_ref)):
                dw_ref[...] = acc_br[k].astype(BF16)
            dwo_ref[...] = acc_out[...].astype(BF16)

    row = pl.BlockSpec((tm, d), lambda i: (i, 0))
    half = pl.BlockSpec((tm, w), lambda i: (i, 0))
    wide = pl.BlockSpec((tm, 3 * d), lambda i: (i, 0))
    const = lambda shape: pl.BlockSpec(shape, lambda i: (0,) * len(shape))
    c_args, c_in, c_out, c_shape, c_scratch = _comm_extra(comm)
    res = pl.pallas_call(
        _with_comm(body, comm, 15, 9, nb, 0), grid=(nb,),
        in_specs=[row, row, const((1, d)), const((d, d)), wide, row, row, row, const((w, d)), const((w, d)), const((w, d)),
                  half, half, half, row] + c_in,
        out_specs=[half, half, half, wide, const((8, d)), const((w, d)), const((w, d)), const((w, d)), const((d, d))] + c_out,
        out_shape=[jax.ShapeDtypeStruct((l, w), F32)] * 3 + [jax.ShapeDtypeStruct((l, 3 * d), BF16),
                                                             jax.ShapeDtypeStruct((8, d), F32)]
                  + [jax.ShapeDtypeStruct((w, d), BF16)] * 3 + [jax.ShapeDtypeStruct((d, d), BF16)] + c_shape,
        scratch_shapes=[pltpu.VMEM((3, w, d), F32), pltpu.VMEM((d, d), F32)] + c_scratch,
        compiler_params=_params("arbitrary"), name=name)(dx, out, gate, wout, pg, *brs, *wbrs, *ys, merged, *c_args)
    return list(res[:9]), list(res[9:])


def _adamw(w, gs, m, v, *, name, tr=256):
    r, c = w.shape
    ns = len(gs)
    p, rs, _ = gs[0].shape
    assert rs * ns == r
    tr = min(tr, rs)
    assert rs % tr == 0
    nr = rs // tr
    c1 = 1.0 / (1.0 - ADAM_B1 ** ADAM_STEP)
    c2 = 1.0 / (1.0 - ADAM_B2 ** ADAM_STEP)

    def body(*refs):
        w_ref, g_refs, (m_ref, v_ref, go_ref, d_ref, mo_ref, vo_ref) = refs[0], refs[1:1 + ns], refs[1 + ns:]
        slab = pl.program_id(0)
        gv = None
        for k, g_ref in enumerate(g_refs):
            gk = g_ref[0].astype(F32)
            for j in range(1, p):
                gk = gk + g_ref[j].astype(F32)
            gv = gk if gv is None else jnp.where(slab == k, gk, gv)
        go_ref[...] = gv
        mn = ADAM_B1 * m_ref[...] + (1.0 - ADAM_B1) * gv
        vn = ADAM_B2 * v_ref[...] + (1.0 - ADAM_B2) * (gv * gv)
        mo_ref[...] = mn
        vo_ref[...] = vn
        d_ref[...] = -ADAM_LR * ((mn * c1) / (jnp.sqrt(vn * c2) + ADAM_EPS) + ADAM_WD * w_ref[...])

    row = pl.BlockSpec((tr, c), lambda s, i: (s * nr + i, 0))
    g_specs = [pl.BlockSpec((p, tr, c), lambda s, i, k=k: (0, jnp.where(s == k, i, 0), 0)) for k in range(ns)]
    return pl.pallas_call(
        body, grid=(ns, nr),
        in_specs=[row] + g_specs + [row, row],
        out_specs=[row] * 4,
        out_shape=[jax.ShapeDtypeStruct((r, c), F32)] * 4,
        compiler_params=_params("arbitrary", "arbitrary"), name=name)(w, *gs, m, v)


def _mesh_place():
    x, y, c = lax.axis_index("x"), lax.axis_index("y"), lax.axis_index("c")
    other_chips = [(1 - x, y), (x, 1 - y), (1 - x, 1 - y)]
    return x, y, c, other_chips


def _run_plan(plan, *, name):
    n = len(plan["args"])

    def body(*refs):
        start, forward, finish = plan["phases"](refs[:n], refs[n:2 * n], *refs[2 * n:])
        start()
        forward()
        finish()

    anyspec = pl.BlockSpec(memory_space=pl.ANY)
    return pl.pallas_call(
        body, in_specs=[anyspec] * n, out_specs=[anyspec] * n, out_shape=plan["out_shape"],
        scratch_shapes=plan["scratch"], name=name)(*plan["args"])


def _gather_plan(arrs):
    n = len(arrs)

    def phases(ins, outs, send_sems, recv_sems, loc_sems):
        x, y, c, chips = _mesh_place()
        me = 4 * x + 2 * y + c
        slot = lambda px, py, pc: 4 * px + 2 * py + pc

        def copy(k, j, src, block, to):
            return pltpu.make_async_remote_copy(
                src_ref=src, dst_ref=outs[k].at[block], send_sem=send_sems.at[k, j], recv_sem=recv_sems.at[k, j],
                device_id=to, device_id_type=pl.DeviceIdType.MESH)

        local = [pltpu.make_async_copy(ins[k], outs[k].at[me], loc_sems.at[k]) for k in range(n)]
        first = []
        for k in range(n):
            first.append(copy(k, 0, ins[k], me, (x, y, 1 - c)))
            for j, chip in enumerate(chips):
                first.append(copy(k, 1 + j, ins[k], me, (*chip, c)))
        passed = [copy(k, 4 + j, outs[k].at[slot(*chip, c)], slot(*chip, c), (x, y, 1 - c))
                  for j, chip in enumerate(chips) for k in range(n)]

        def start():
            for cp in local + first:
                cp.start()

        def forward():
            for j, chip in enumerate(chips):
                for k in range(n):
                    copy(k, 1 + j, ins[k], slot(*chip, c), (x, y, c)).wait_recv()
                    passed[j * n + k].start()

        def finish():
            for k in range(n):
                copy(k, 0, ins[k], slot(x, y, 1 - c), (x, y, c)).wait_recv()
                for j, chip in enumerate(chips):
                    copy(k, 4 + j, ins[k], slot(*chip, 1 - c), (x, y, c)).wait_recv()
            for cp in first + passed:
                cp.wait_send()
            for cp in local:
                cp.wait()

        return start, forward, finish

    return dict(
        args=list(arrs), out_shape=[jax.ShapeDtypeStruct((N_DEV,) + a.shape, a.dtype) for a in arrs],
        scratch=[pltpu.SemaphoreType.DMA((n, 7)), pltpu.SemaphoreType.DMA((n, 7)), pltpu.SemaphoreType.DMA((n,))],
        phases=phases)


def _allreduce_small(small, extra, *, name):
    r, lanes = small.shape
    assert r % 16 == 0
    h = r // 2
    e = extra.shape[0]

    def body(s_ref, x_ref, out_ref, xall_ref, sib_ref, parts_ref, send_sems, recv_sems):
        x, y, c, chips = _mesh_place()
        me = 4 * x + 2 * y + c
        my_chip = 2 * x + y
        sibling = (x, y, 1 - c)
        mine = pl.ds(pl.multiple_of(c * h, 8), h)
        theirs = pl.ds(pl.multiple_of((1 - c) * h, 8), h)

        def remote(j, src, dst, to):
            return pltpu.make_async_remote_copy(src_ref=src, dst_ref=dst, send_sem=send_sems.at[j],
                                                recv_sem=recv_sems.at[j], device_id=to, device_id_type=pl.DeviceIdType.MESH)

        to_sibling = remote(0, s_ref.at[theirs], sib_ref, sibling)
        to_sibling.start()
        xall_ref[me] = x_ref[...]
        extras = []
        for rr in range(1, N_DEV):
            peer = me ^ rr
            cp = remote(4 + rr, x_ref, xall_ref.at[me], (peer // 4, (peer // 2) % 2, peer % 2))
            cp.start()
            extras.append(cp)
        to_sibling.wait_recv()
        parts_ref[my_chip] = s_ref[mine] + sib_ref[...]
        to_chips = [remote(1 + j, parts_ref.at[my_chip], parts_ref.at[my_chip], (px, py, c))
                    for j, (px, py) in enumerate(chips)]
        for cp in to_chips:
            cp.start()
        for cp in to_chips:
            cp.wait_recv()
        out_ref[mine] = (parts_ref[0] + parts_ref[1]) + (parts_ref[2] + parts_ref[3])
        done = remote(4, out_ref.at[mine], out_ref.at[mine], sibling)
        done.start()
        remote(4, out_ref.at[theirs], out_ref.at[theirs], sibling).wait_recv()
        for cp in extras:
            cp.wait()
        to_sibling.wait_send()
        for cp in to_chips:
            cp.wait_send()
        done.wait_send()

    vmem = pl.BlockSpec(memory_space=pltpu.VMEM)
    return pl.pallas_call(
        body, in_specs=[vmem, vmem], out_specs=[vmem, vmem],
        out_shape=[jax.ShapeDtypeStruct((r, lanes), F32), jax.ShapeDtypeStruct((N_DEV, e, lanes), F32)],
        scratch_shapes=[pltpu.VMEM((h, lanes), F32), pltpu.VMEM((4, h, lanes), F32),
                        pltpu.SemaphoreType.DMA((12,)), pltpu.SemaphoreType.DMA((12,))],
        compiler_params=pltpu.CompilerParams(vmem_limit_bytes=VMEM_LIMIT), name=name)(small, extra)


def _ada_modulation(c, w_ada, b_cols, comm):
    depth, d, cols = w_ada.shape
    c_args, c_in, c_out, c_shape, c_scratch = _comm_extra(comm)
    nc = len(c_args)

    def body(c_ref, w_ref, b_ref, *refs):
        cin, (cact_ref, mod_ref), cout = refs[:nc], refs[nc:nc + 2], refs[nc + 2:2 * nc + 2]
        call_ref, part_ref, send_sems, recv_sems = refs[2 * nc + 2:2 * nc + 6]
        start, forward, finish = comm["phases"](cin, cout, *refs[2 * nc + 6:])
        start()
        x, y, core, _ = _mesh_place()
        me = 4 * x + 2 * y + core

        def to_all(j0, src, dst):
            copies = []
            for r in range(1, N_DEV):
                peer = me ^ r
                copies.append(pltpu.make_async_remote_copy(
                    src_ref=src, dst_ref=dst, send_sem=send_sems.at[j0 + r - 1], recv_sem=recv_sems.at[j0 + r - 1],
                    device_id=(peer // 4, (peer // 2) % 2, peer % 2), device_id_type=pl.DeviceIdType.MESH))
            for cp in copies:
                cp.start()
            for cp in copies:
                cp.wait()

        call_ref[me] = c_ref[...]
        to_all(0, c_ref, call_ref.at[me])
        c_act = jnp.concatenate([call_ref[k] for k in range(N_DEV)], axis=0)
        c_act = c_act * _sigmoid(c_act)
        cact_ref[...] = c_act
        for li in range(depth):
            part_ref[li] = _dot(c_act, w_ref[li]) + b_ref[li:li + 1, :]
        mod_ref[me] = part_ref[...]
        to_all(N_DEV - 1, part_ref, mod_ref.at[me])
        forward()
        finish()

    vmem = pl.BlockSpec(memory_space=pltpu.VMEM)
    res = pl.pallas_call(
        body, in_specs=[vmem] * 3 + c_in, out_specs=[vmem] * 2 + c_out,
        out_shape=[jax.ShapeDtypeStruct((N_DEV, d), F32), jax.ShapeDtypeStruct((N_DEV, depth, N_DEV, cols), F32)] + c_shape,
        scratch_shapes=[pltpu.VMEM((N_DEV, 1, d), F32), pltpu.VMEM((depth, N_DEV, cols), F32),
                        pltpu.SemaphoreType.DMA((2 * (N_DEV - 1),)), pltpu.SemaphoreType.DMA((2 * (N_DEV - 1),))] + c_scratch,
        compiler_params=pltpu.CompilerParams(vmem_limit_bytes=VMEM_LIMIT), name="ada_modulation")(c, w_ada, b_cols, *c_args)
    return res[0], res[1], list(res[2:])


def _sibling_swap_plan(arrs):
    n = len(arrs)

    def phases(ins, outs, send_sems, recv_sems):
        x, y, c, _ = _mesh_place()
        copies = [pltpu.make_async_remote_copy(
            src_ref=ins[k].at[1 - c], dst_ref=outs[k], send_sem=send_sems.at[k], recv_sem=recv_sems.at[k],
            device_id=(x, y, 1 - c), device_id_type=pl.DeviceIdType.MESH) for k in range(n)]

        def start():
            for cp in copies:
                cp.start()

        def finish():
            for cp in copies:
                cp.wait()

        return start, (lambda: None), finish

    return dict(args=list(arrs), out_shape=[jax.ShapeDtypeStruct(a.shape[1:], a.dtype) for a in arrs],
                scratch=[pltpu.SemaphoreType.DMA((n,)), pltpu.SemaphoreType.DMA((n,))], phases=phases)


def _pair_add(mine, theirs, core, *, name, tr=256):
    _, r, c = mine.shape
    tr = min(tr, r)
    assert r % tr == 0

    def body(core_ref, m_ref, t_ref, o_ref):
        o_ref[...] = (m_ref[0].astype(F32) + t_ref[...].astype(F32)).astype(BF16)

    return pl.pallas_call(
        body,
        grid_spec=pltpu.PrefetchScalarGridSpec(
            num_scalar_prefetch=1, grid=(r // tr,),
            in_specs=[pl.BlockSpec((1, tr, c), lambda i, core_ref: (core_ref[0], i, 0)),
                      pl.BlockSpec((tr, c), lambda i, core_ref: (i, 0))],
            out_specs=pl.BlockSpec((tr, c), lambda i, core_ref: (i, 0))),
        out_shape=jax.ShapeDtypeStruct((r, c), BF16),
        compiler_params=_params("parallel"), name=name)(core, mine, theirs)


def _pair_add_small(mines, theirs, core, *, name):
    n = len(mines)

    def body(core_ref, *refs):
        for m_ref, t_ref, o_ref in zip(refs[:n], refs[n:2 * n], refs[2 * n:]):
            o_ref[...] = (m_ref[0].astype(F32) + t_ref[...].astype(F32)).astype(BF16)

    whole = lambda a: pl.BlockSpec(a.shape, lambda i, core_ref: (0,) * a.ndim)
    return pl.pallas_call(
        body,
        grid_spec=pltpu.PrefetchScalarGridSpec(
            num_scalar_prefetch=1, grid=(1,),
            in_specs=[pl.BlockSpec((1,) + m.shape[1:], lambda i, core_ref: (core_ref[0], 0, 0)) for m in mines]
                     + [whole(t) for t in theirs],
            out_specs=[whole(t) for t in theirs]),
        out_shape=[jax.ShapeDtypeStruct(t.shape, BF16) for t in theirs],
        compiler_params=_params("arbitrary"), name=name)(core, *mines, *theirs)


def _chip_scatter_plan(arrs):
    n = len(arrs)

    def phases(ins, outs, send_sems, recv_sems, loc_sems):
        x, y, c, chips = _mesh_place()
        mine = 2 * x + y
        local = [pltpu.make_async_copy(ins[k].at[mine], outs[k].at[mine], loc_sems.at[k]) for k in range(n)]
        remote = [pltpu.make_async_remote_copy(
            src_ref=ins[k].at[2 * px + py], dst_ref=outs[k].at[mine], send_sem=send_sems.at[k, j],
            recv_sem=recv_sems.at[k, j], device_id=(px, py, c), device_id_type=pl.DeviceIdType.MESH)
            for j, (px, py) in enumerate(chips) for k in range(n)]

        def start():
            for cp in local + remote:
                cp.start()

        def finish():
            for cp in remote:
                cp.wait()
            for cp in local:
                cp.wait()

        return start, (lambda: None), finish

    return dict(
        args=list(arrs), out_shape=[jax.ShapeDtypeStruct(a.shape, a.dtype) for a in arrs],
        scratch=[pltpu.SemaphoreType.DMA((n, 3)), pltpu.SemaphoreType.DMA((n, 3)), pltpu.SemaphoreType.DMA((n,))],
        phases=phases)


def _ssm_discretize(a_re, a_im, log_dt, b_re, b_im):
    dt = jnp.exp(log_dt)[:, None]
    mag = jnp.exp(a_re * dt)
    lr = mag * jnp.cos(a_im * dt)
    li = mag * jnp.sin(a_im * dt)
    den = a_re * a_re + a_im * a_im
    cr = ((lr - 1.0) * a_re + li * a_im) / den
    ci = (li * a_re - (lr - 1.0) * a_im) / den
    bbr = cr[..., None] * b_re - ci[..., None] * b_im
    bbi = cr[..., None] * b_im + ci[..., None] * b_re
    return lr, li, bbr, bbi


def _ssm_dense(lr, li, bbr, bbi, c_re, c_im, *, name):
    import numpy as np
    scan_ops = _scan_tables(lr.reshape(-1), li.reshape(-1)) + _scan_consts(SEQ_BLOCK)
    per_half = SSM_GROUPS // SSM_HALVES
    bt = jnp.stack([b.transpose(0, 2, 1).reshape(SSM_W, SSM_STATE) for b in (bbr, bbi)])
    ct = jnp.stack([c.transpose(0, 2, 1).reshape(SSM_N, SSM_GROUP) for c in (c_re, c_im)])
    rep_p = jnp.asarray(np.tile(np.eye(SSM_STATE, dtype=np.float32), (1, per_half)), BF16)
    rep_c = jnp.asarray(np.tile(np.eye(SSM_GROUP, dtype=np.float32), (1, per_half)), BF16)

    def body(bt_ref, ct_ref, rp_ref, rc_ref, b_ref, c_ref):
        def on_diagonal(shape, rows, cols):
            r = lax.broadcasted_iota(jnp.int32, shape, 0) // rows
            c = lax.broadcasted_iota(jnp.int32, shape, 1) // cols
            return r == c

        mask_b = on_diagonal((SSM_HW, SSM_HN), SSM_GROUP, SSM_STATE)
        mask_c = on_diagonal((SSM_HN, SSM_HW), SSM_STATE, SSM_GROUP)
        for k in range(2):
            for h in range(SSM_HALVES):
                b_rows = bt_ref[k, h * SSM_HW:(h + 1) * SSM_HW, :]
                b_ref[k, h] = jnp.where(mask_b, _dot(b_rows, rp_ref[...]), 0.0).astype(BF16)
                c_rows = ct_ref[k, h * SSM_HN:(h + 1) * SSM_HN, :]
                c_ref[k, h] = jnp.where(mask_c, _dot(c_rows, rc_ref[...]), 0.0).astype(BF16)

    vmem = pl.BlockSpec(memory_space=pltpu.VMEM)
    bblk, cblk = pl.pallas_call(
        body, in_specs=[vmem] * 4, out_specs=[vmem] * 2,
        out_shape=[jax.ShapeDtypeStruct((2, SSM_HALVES, SSM_HW, SSM_HN), BF16),
                   jax.ShapeDtypeStruct((2, SSM_HALVES, SSM_HN, SSM_HW), BF16)],
        compiler_params=pltpu.CompilerParams(vmem_limit_bytes=VMEM_LIMIT), name=name)(bt, ct, rep_p, rep_c)
    return scan_ops, bblk, cblk


def _ssm_extract(db, dc, sums):
    db = db.reshape(2, SSM_GROUPS, SSM_GROUP, SSM_STATE).transpose(0, 1, 3, 2)
    dc = dc.reshape(2, SSM_GROUPS, SSM_GROUP, SSM_STATE)
    dlr = sums[0].reshape(SSM_GROUPS, SSM_STATE)
    dli = sums[1].reshape(SSM_GROUPS, SSM_STATE)
    return dlr, dli, db[0], db[1], dc[0], dc[1]


def _in_groups():
    names = ("q", "k", "v", "u_ssm", "u_pool", "z_att", "z_ssm", "z_pool", "gates")
    sizes = (ATT_W, KV_W, KV_W, SSM_W, POOL_W, ATT_W, SSM_W, POOL_W, 3 * D_MODEL)
    r, lo = {}, 0
    for nm, s in zip(names, sizes):
        r[nm] = (lo, lo + s)
        lo += s
    kv = (r["k"][0], r["v"][1])
    return ((r["q"], r["z_att"], kv), (r["u_ssm"], r["z_ssm"]), (r["u_pool"], r["z_pool"]), (r["gates"],))


IN_GROUPS = _in_groups()


def _layer_fwd(x, lw, li, late=None, comm_attn=None, comm_ssm=None):
    tag = f"l{li}"
    h, (pa, ps, pp, pg), arrived = _ln_proj(x, lw["norm_g"], lw["shift"], lw["scale"], lw["w_in"], IN_GROUPS,
                                            name=f"ln_proj_{tag}", comm=None if late is None else late[0])
    if late is not None:
        lw = {**lw, **late[1](arrived)}
    ya, from_attn = _attn_fwd(pa, lw["sinks"], name=f"attn_fwd_{tag}", comm=comm_attn)
    ys, chk, states, *from_ssm = _ssm_fwd(ps, lw["lam"], lw["bblk"], lw["cblk"], lw["ssm_d"], lw["w_glu"], lw["b_glu"],
                                          name=f"ssm_fwd_{tag}", comm=comm_ssm)
    yp = _pool_fwd(pp, lw["w_pool"], lw["pool_scale"], name=f"pool_fwd_{tag}")
    x_new, merged, ba, bs, bp, out = _merge_fwd(ya, ys, yp, lw["w_br_att"], lw["w_br_ssm"], lw["w_br_pool"], pg,
                                                lw["w_out"], x, lw["gate"], name=f"merge_fwd_{tag}")
    saved = dict(x=x, h=h, pa=pa, ps=ps, pp=pp, pg=pg, ya=ya, ys=ys, yp=yp, chk=chk, states=states, merged=merged,
                 ba=ba, bs=bs, bp=bp, out=out)
    return x_new, saved, lw, list(from_attn), list(from_ssm)


def _layer_bwd(dx, lw, sv, li, later=None, own=None):
    tag = f"l{li}"
    g = {}
    merge_out, swapped = _merge_bwd(
        dx, sv["out"], lw["gate"], lw["w_out"], sv["pg"], (sv["ba"], sv["bs"], sv["bp"]),
        (lw["w_br_att"], lw["w_br_ssm"], lw["w_br_pool"]), (sv["ya"], sv["ys"], sv["yp"]), sv["merged"],
        name=f"merge_bwd_{tag}", comm=None if later is None else later[0])
    dya, dys, dyp, dpg, gate_sums, g["w_br_att"], g["w_br_ssm"], g["w_br_pool"], g["w_out"] = merge_out
    dpa, dsink = _attn_bwd(sv["pa"], lw["sinks"], dya, name=f"attn_bwd_{tag}")
    dps, db_dense, dc_dense, dwglu, ssm_sums, *exchanged = _ssm_bwd(
        sv["ps"], dys, sv["chk"], sv["states"], lw["lam"], lw["bblk"], lw["cblk"], lw["ssm_d"], lw["w_glu"], lw["b_glu"],
        name=f"ssm_bwd_{tag}", comm=None if later is None else later[1](swapped))
    g["w_glu"] = dwglu.astype(BF16)
    dpp, dwpool, pool_sums = _pool_bwd(sv["pp"], dyp, lw["w_pool"], lw["pool_scale"], name=f"pool_bwd_{tag}")
    h = sv["h"]
    dproj = (dpa, dps, dpp, dpg)
    g["w_in"], from_late = _mm_tn_grouped(h, dproj, IN_GROUPS, name=f"dw_in_{tag}",
                                          comm=None if own is None else own({k: g[k] for k in LATE_WEIGHTS}))
    dx_in, ln_sums, from_w_in = _ln_proj_bwd(dproj, lw["w_in"], IN_GROUPS, sv["x"], dx, lw["norm_g"], lw["scale"],
                                             name=f"ln_proj_bwd_{tag}",
                                             comm=None if own is None else own({"w_in": g["w_in"]}))
    g["dmod"] = jnp.concatenate([ln_sums[0], ln_sums[1], gate_sums[0]])
    g["norm_g"] = ln_sums[2]
    g["attn_sinks"] = dsink[:, 0]
    g["ssm_raw"] = _ssm_extract(db_dense, dc_dense, ssm_sums)
    g["ssm_d"] = ssm_sums[2, :SSM_W]
    g["b_glu"] = ssm_sums[3, :SSM_W]
    g["w_pool"] = dwpool
    g["pool_scale"] = pool_sums[0]
    return dx_in, g, exchanged, list(from_w_in) + list(from_late)


BIG_WEIGHTS = ("w_in", "w_glu", "w_br_att", "w_br_ssm", "w_br_pool", "w_out")
ROW_SHARDED = ("w_glu", "w_out")


LATE_WEIGHTS = BIG_WEIGHTS[1:]


def _side_by_side(g, *, tm=256):
    n, r, c = g.shape

    def body(g_ref, o_ref):
        for s in range(n):
            o_ref[:, s * c:(s + 1) * c] = g_ref[s]

    return pl.pallas_call(
        body, grid=(r // tm,),
        in_specs=[pl.BlockSpec((n, tm, c), lambda i: (0, i, 0))],
        out_specs=pl.BlockSpec((tm, n * c), lambda i: (i, 0)),
        out_shape=jax.ShapeDtypeStruct((r, n * c), g.dtype),
        compiler_params=_params("parallel"), name="side_by_side")(g)


def _full_weights(keys, gathered):
    full = {}
    for k, g in zip(keys, gathered):
        if k in ROW_SHARDED:
            full[k] = g.reshape(N_DEV * g.shape[1], g.shape[2])
        elif g.shape[2] % 128:
            full[k] = _side_by_side(g)
        else:
            full[k] = g.transpose(1, 0, 2).reshape(g.shape[1], N_DEV * g.shape[2])
    return full


def _by_destination(keys, grads):
    out = []
    for k in keys:
        g = grads[k]
        if g.ndim == 4:
            out.append(g)
        elif k in ROW_SHARDED:
            out.append(g.reshape(4, 2, g.shape[0] // N_DEV, g.shape[1]).transpose(1, 0, 2, 3))
        else:
            out.append(g.reshape(g.shape[0], 4, 2, g.shape[1] // N_DEV).transpose(2, 1, 0, 3))
    return out


def _prepare_layer(li, mod, norm_g, w_in_full, attn_sinks, disc, ssm_c_re, ssm_c_im, ssm_d, b_glu, w_pool, pool_scale):
    d = D_MODEL
    lr, li_, bbr, bbi = disc
    lam, bblk, cblk = _ssm_dense(lr[li], li_[li], bbr[li], bbi[li], ssm_c_re[li], ssm_c_im[li], name=f"ssm_dense_l{li}")
    return dict(
        norm_g=norm_g[li][None, :], shift=mod[li, :d][None, :], scale=mod[li, d:2 * d][None, :],
        gate=mod[li, 2 * d:][None, :], w_in=w_in_full,
        sinks=attn_sinks[li], lam=lam, bblk=bblk, cblk=cblk, ssm_d=ssm_d[li][None, :],
        b_glu=b_glu[li][None, :], w_pool=w_pool[li].astype(BF16), pool_scale=pool_scale[li][None, :])


SMALL_ROWS = 64
SMALL_ORDER = ("norm_g", "attn_sinks", "ssm_d", "b_glu", "w_pool", "pool_scale", "dmod")


def _pack_small(loss, dfinal_g, layer_grads):
    parts = [jnp.broadcast_to(loss.reshape(1), (128,)), dfinal_g]
    for g in layer_grads:
        for k in SMALL_ORDER:
            v = g[k].reshape(-1)
            if v.shape[0] % 128:
                v = jnp.pad(v, (0, 128 - v.shape[0] % 128))
            parts.append(v)
        for v in g["ssm_raw"]:
            parts.append(v.reshape(-1))
    flat = jnp.concatenate(parts)
    return jnp.pad(flat, (0, (-flat.shape[0]) % (SMALL_ROWS * 128))).reshape(-1, 128)


def _unpack_small(flat, shapes):
    out, off = [], 0
    for s in shapes:
        n = int(math.prod(s))
        out.append(flat[off:off + n].reshape(s))
        off += n + (-n) % 128
    return out


def kernel(x, c, norm_g, w_ada, b_ada, w_in, attn_sinks, ssm_a_re, ssm_a_im, ssm_log_dt, ssm_b_re, ssm_b_im, ssm_c_re, ssm_c_im, ssm_d, w_glu, b_glu, w_pool, pool_scale, w_br_att, w_br_ssm, w_br_pool, w_out, final_g, loss_target, m_norm_g, m_w_ada, m_b_ada, m_w_in, m_attn_sinks, m_ssm_a_re, m_ssm_a_im, m_ssm_log_dt, m_ssm_b_re, m_ssm_b_im, m_ssm_c_re, m_ssm_c_im, m_ssm_d, m_w_glu, m_b_glu, m_w_pool, m_pool_scale, m_w_br_att, m_w_br_ssm, m_w_br_pool, m_w_out, m_final_g, v_norm_g, v_w_ada, v_b_ada, v_w_in, v_attn_sinks, v_ssm_a_re, v_ssm_a_im, v_ssm_log_dt, v_ssm_b_re, v_ssm_b_im, v_ssm_c_re, v_ssm_c_im, v_ssm_d, v_w_glu, v_b_glu, v_w_pool, v_pool_scale, v_w_br_att, v_w_br_ssm, v_w_br_pool, v_w_out, v_final_g):
    me = 4 * lax.axis_index("x") + 2 * lax.axis_index("y") + lax.axis_index("c")
    d = D_MODEL
    ada_w = 3 * d // N_DEV

    sharded = dict(w_in=w_in, w_glu=w_glu, w_br_att=w_br_att, w_br_ssm=w_br_ssm, w_br_pool=w_br_pool, w_out=w_out)
    shards = lambda li, keys: [sharded[k][li].astype(BF16) for k in keys]

    b_cols = lax.dynamic_slice(b_ada, (0, me * ada_w), (DEPTH, ada_w))
    c_act, mod_all, w_in0 = _ada_modulation(c, w_ada, b_cols, _gather_plan(shards(0, ("w_in",))))
    mod_mine = lax.dynamic_index_in_dim(mod_all, me, axis=2, keepdims=False)
    mod_mine = mod_mine.transpose(1, 0, 2).reshape(DEPTH, 3 * d)

    disc, disc_vjp = jax.vjp(jax.vmap(_ssm_discretize), ssm_a_re, ssm_a_im, ssm_log_dt, ssm_b_re, ssm_b_im)
    layer = lambda li, gathered_w_in: _prepare_layer(
        li, mod_mine, norm_g, _full_weights(("w_in",), gathered_w_in)["w_in"], attn_sinks, disc, ssm_c_re, ssm_c_im,
        ssm_d, b_glu, w_pool, pool_scale)
    late_weights = lambda gathered: _full_weights(LATE_WEIGHTS, gathered)
    core = lax.axis_index("c").astype(jnp.int32).reshape(1)

    def add_pairs(keys, by_dest, from_sibling, tag):
        flat = {k: (a.reshape(2, -1, a.shape[-1]), b.reshape(-1, b.shape[-1]))
                for k, a, b in zip(keys, by_dest, from_sibling)}
        small = [k for k in keys if k != "w_in"]
        sums = {}
        if "w_in" in flat:
            sums["w_in"] = _pair_add(*flat["w_in"], core, name=f"grads_pair_add_{tag}_w_in")
        if small:
            added = _pair_add_small([flat[k][0] for k in small], [flat[k][1] for k in small], core,
                                    name=f"grads_pair_add_{tag}_late")
            sums.update(zip(small, added))
        return [sums[k].reshape(b.shape) for k, b in zip(keys, from_sibling)]

    def chip_sums_of(keys, grads_li, tag):
        by_dest = _by_destination(keys, grads_li)
        return add_pairs(keys, by_dest, _run_plan(_sibling_swap_plan(by_dest), name=f"grads_sibling_swap_{tag}"), tag)

    layers, saved, grads = [None] * DEPTH, [None] * DEPTH, [None] * DEPTH
    layers[0] = layer(0, w_in0)
    xs, saved[0], layers[0], late1, w_in1 = _layer_fwd(
        x[0], layers[0], 0, late=(_gather_plan(shards(0, LATE_WEIGHTS)), late_weights),
        comm_attn=_gather_plan(shards(1, LATE_WEIGHTS)), comm_ssm=_gather_plan(shards(1, ("w_in",))))
    layers[1] = {**layer(1, w_in1), **late_weights(late1)}
    xs, saved[1], _, _, _ = _layer_fwd(xs, layers[1], 1)
    dx, fin_sums = _final_loss(xs, final_g[None, :], loss_target[0])
    loss_part = jnp.sum(fin_sums[1])
    dx, grads[1], _, _ = _layer_bwd(dx, layers[1], saved[1], 1)
    by_dest1 = _by_destination(BIG_WEIGHTS, grads[1])
    dx, grads[0], scattered1, scattered0 = _layer_bwd(
        dx, layers[0], saved[0], 0,
        later=(_sibling_swap_plan(by_dest1),
               lambda swapped: _chip_scatter_plan(add_pairs(BIG_WEIGHTS, by_dest1, swapped, "l1"))),
        own=lambda g: _chip_scatter_plan(chip_sums_of(tuple(g), g, "l0_" + "_".join(g))))
    big = list(zip(scattered0, scattered1))
    grad_x = dx[None]

    small = _pack_small(loss_part, fin_sums[0], grads)
    dmod_rows = jnp.concatenate([grads[li]["dmod"] for li in range(DEPTH)]).reshape(-1, 128)
    small_sum, dmod_gathered = _allreduce_small(small, dmod_rows, name="allreduce_small")
    out = {}

    def adam(name, w, g_slabs, m, v):
        shp = w.shape
        r = int(math.prod(shp[:-1])) if len(shp) > 1 else 1
        w2, m2, v2 = (a.reshape(r, shp[-1]) for a in (w, m, v))
        gs = [g.reshape(g.shape[0], r // len(g_slabs), shp[-1]) for g in g_slabs]
        res = _adamw(w2, gs, m2, v2, name=f"adamw_{name}", tr=256 if shp[-1] >= 128 else 2048)
        out[name] = tuple(a.reshape(shp) for a in res)

    flat = small_sum.reshape(-1)
    shapes = [(128,), (d,)]
    for _ in range(DEPTH):
        shapes += [(d,), (N_HEADS,), (SSM_W,), (SSM_W,), (4, POOL_GW, POOL_GW), (POOL_W,), (3 * d,),
                   (SSM_GROUPS, SSM_STATE), (SSM_GROUPS, SSM_STATE), (SSM_GROUPS, SSM_STATE, SSM_GROUP),
                   (SSM_GROUPS, SSM_STATE, SSM_GROUP), (SSM_GROUPS, SSM_GROUP, SSM_STATE), (SSM_GROUPS, SSM_GROUP, SSM_STATE)]
    un = _unpack_small(flat, shapes)
    loss = un[0][0]
    g_final_g = un[1]
    per = 13
    gl = [un[2 + li * per: 2 + (li + 1) * per] for li in range(DEPTH)]
    st = lambda j: jnp.stack([gl[li][j] for li in range(DEPTH)])
    g_norm_g, g_sinks, g_ssm_d, g_b_glu, g_w_pool, g_pool_scale, g_b_ada = (st(j) for j in range(7))
    d_lr, d_li, d_bbr, d_bbi, g_c_re, g_c_im = (st(j) for j in range(7, 13))
    g_a_re, g_a_im, g_log_dt, g_b_re, g_b_im = disc_vjp((d_lr, d_li, d_bbr, d_bbi))

    dmod_all = lax.dynamic_slice(dmod_gathered.reshape(N_DEV, DEPTH, 3 * d), (0, 0, me * ada_w), (N_DEV, DEPTH, ada_w))
    dmod_all = dmod_all.transpose(1, 0, 2)
    g_w_ada = jnp.stack([_mm_tn(c_act, dmod_all[li], tm=d, tn=ada_w, tk=N_DEV, name=f"dw_ada_l{li}") for li in range(DEPTH)])

    adam("w_ada", w_ada, [g_w_ada[None]], m_w_ada, v_w_ada)
    adam("w_in", w_in, big[0], m_w_in, v_w_in)
    adam("w_glu", w_glu, big[1], m_w_glu, v_w_glu)
    adam("w_br_att", w_br_att, big[2], m_w_br_att, v_w_br_att)
    adam("w_br_ssm", w_br_ssm, big[3], m_w_br_ssm, v_w_br_ssm)
    adam("w_br_pool", w_br_pool, big[4], m_w_br_pool, v_w_br_pool)
    adam("w_out", w_out, big[5], m_w_out, v_w_out)

    small_names = ["norm_g", "b_ada", "attn_sinks", "ssm_a_re", "ssm_a_im", "ssm_log_dt", "ssm_b_re", "ssm_b_im",
                   "ssm_c_re", "ssm_c_im", "ssm_d", "b_glu", "w_pool", "pool_scale", "final_g"]
    small_w = [norm_g, b_ada, attn_sinks, ssm_a_re, ssm_a_im, ssm_log_dt, ssm_b_re, ssm_b_im, ssm_c_re, ssm_c_im,
               ssm_d, b_glu, w_pool, pool_scale, final_g]
    small_m = [m_norm_g, m_b_ada, m_attn_sinks, m_ssm_a_re, m_ssm_a_im, m_ssm_log_dt, m_ssm_b_re, m_ssm_b_im,
               m_ssm_c_re, m_ssm_c_im, m_ssm_d, m_b_glu, m_w_pool, m_pool_scale, m_final_g]
    small_v = [v_norm_g, v_b_ada, v_attn_sinks, v_ssm_a_re, v_ssm_a_im, v_ssm_log_dt, v_ssm_b_re, v_ssm_b_im,
               v_ssm_c_re, v_ssm_c_im, v_ssm_d, v_b_glu, v_w_pool, v_pool_scale, v_final_g]
    small_g = [g_norm_g, g_b_ada, g_sinks, g_a_re, g_a_im, g_log_dt, g_b_re, g_b_im, g_c_re, g_c_im,
               g_ssm_d, g_b_glu, g_w_pool, g_pool_scale, g_final_g]

    for nm, w, g, m, v in zip(small_names, small_w, small_g, small_m, small_v):
        adam(nm, w, [g[None]], m, v)

    order = ["norm_g", "w_ada", "b_ada", "w_in", "attn_sinks", "ssm_a_re", "ssm_a_im", "ssm_log_dt", "ssm_b_re",
             "ssm_b_im", "ssm_c_re", "ssm_c_im", "ssm_d", "w_glu", "b_glu", "w_pool", "pool_scale", "w_br_att",
             "w_br_ssm", "w_br_pool", "w_out", "final_g"]
    return (loss, grad_x, *[out[k][0] for k in order], *[out[k][1] for k in order],
            *[out[k][2] for k in order], *[out[k][3] for k in order])
```

```python
import functools
import math

import jax
import jax.numpy as jnp
from jax import lax
from jax.experimental import pallas as pl
from jax.experimental.pallas import tpu as pltpu

F32 = jnp.float32
BF16 = jnp.bfloat16

N_DEV = 8
D_MODEL = 1024
DEPTH = 2
CHUNK = 64
N_HEADS = 8
N_KV_HEADS = 2
HEAD_DIM = 64
Q_PER_KV = N_HEADS // N_KV_HEADS
WINDOW = 128
ATT_W = 512
KV_W = 128
SSM_W = 512
SSM_GROUP = 16
SSM_GROUPS = 32
SSM_STATE = 64
SSM_N = SSM_GROUPS * SSM_STATE
POOL_W = 512
POOL_WINDOWS = (2, 4, 8, 16)
POOL_GW = 128
POOL_HALO = 16
EPS = 1e-6
NEG_INF = -1e30
ADAM_LR = 0.001
ADAM_B1 = 0.9
ADAM_B2 = 0.999
ADAM_EPS = 1e-08
ADAM_WD = 0.01
ADAM_STEP = 10

SEQ_BLOCK = 256
POOL_BLOCK = 512
ATT_BLOCK = 128
VMEM_LIMIT = 56 * 1024 * 1024

NN = (((1,), (0,)), ((), ()))
NT = (((1,), (1,)), ((), ()))
TN = (((0,), (0,)), ((), ()))


def _dot(a, b, dims=NN):
    return lax.dot_general(a.astype(BF16), b.astype(BF16), dims, preferred_element_type=F32)


def _params(*sem):
    return pltpu.CompilerParams(dimension_semantics=sem, vmem_limit_bytes=VMEM_LIMIT)


def _sigmoid(x):
    return 0.5 + 0.5 * jnp.tanh(0.5 * x)


def _silu_and_grad(z):
    s = _sigmoid(z)
    return z * s, s * (1.0 + z * (1.0 - s))


_GELU_K = math.sqrt(2.0 / math.pi)


def _gelu_and_grad(x):
    inner = _GELU_K * (x + 0.044715 * x * x * x)
    t = jnp.tanh(inner)
    val = 0.5 * x * (1.0 + t)
    grad = 0.5 * (1.0 + t) + 0.5 * x * (1.0 - t * t) * _GELU_K * (1.0 + 3.0 * 0.044715 * x * x)
    return val, grad


def _grouped_pieces(groups):
    out = []
    for ranges in groups:
        off, pieces = 0, []
        for lo, hi in ranges:
            pieces.append((off, lo, hi))
            off += hi - lo
        out.append(pieces)
    return out


def _mm_tn(a, b, *, out_dtype=F32, tm=1024, tn=1024, tk=1024, name, comm=None):
    k, m = a.shape
    n = b.shape[1]
    assert m % min(tm, m) == 0 and n % min(tn, n) == 0 and k % min(tk, k) == 0
    tm, tn, tk = min(tm, m), min(tn, n), min(tk, k)
    nk = k // tk
    grid = (m // tm, n // tn, nk)
    c_args, c_in, c_out, c_shape, c_scratch = _comm_extra(comm)

    def body(a_ref, b_ref, o_ref, acc_ref):
        kk = pl.program_id(2)

        @pl.when(kk == 0)
        def _():
            acc_ref[...] = jnp.zeros_like(acc_ref)

        acc_ref[...] += _dot(a_ref[...], b_ref[...], TN)

        @pl.when(kk == nk - 1)
        def _():
            o_ref[...] = acc_ref[...].astype(out_dtype)

    res = pl.pallas_call(
        _with_comm(body, comm, 2, 1, grid, -1), grid=grid,
        in_specs=[pl.BlockSpec((tk, tm), lambda i, j, kk: (kk, i)), pl.BlockSpec((tk, tn), lambda i, j, kk: (kk, j))] + c_in,
        out_specs=[pl.BlockSpec((tm, tn), lambda i, j, kk: (i, j))] + c_out,
        out_shape=[jax.ShapeDtypeStruct((m, n), out_dtype)] + c_shape,
        scratch_shapes=[pltpu.VMEM((tm, tn), F32)] + c_scratch,
        compiler_params=_params(*(("arbitrary",) * 3 if comm else ("parallel", "parallel", "arbitrary"))),
        name=name)(a, b, *c_args)
    return (res[0], list(res[1:])) if comm else res[0]


def _mm_tn_grouped(a, bs, groups, *, tm=512, tk=512, name, comm=None):
    k, m = a.shape
    tm, tk = min(tm, m), min(tk, k)
    assert m % tm == 0 and k % tk == 0
    nk, nb = k // tk, len(bs)
    n = sum(b.shape[1] for b in bs)
    ns = n // N_DEV
    pieces = []
    for plist in _grouped_pieces(groups):
        sub = []
        for off, lo, hi in plist:
            pos = lo
            while pos < hi:
                s = pos // ns
                end = min(hi, (s + 1) * ns)
                sub.append((s, pos - s * ns, end - s * ns, off + pos - lo))
                pos = end
        pieces.append(sub)
    grid = (m // tm, nk)
    c_args, c_in, c_out, c_shape, c_scratch = _comm_extra(comm)

    def body(a_ref, *refs):
        b_refs, o_ref, acc_refs = refs[:nb], refs[nb], refs[nb + 1:]
        kk = pl.program_id(1)
        av = a_ref[...]
        for b_ref, acc_ref, plist in zip(b_refs, acc_refs, pieces):
            @pl.when(kk == 0)
            def _():
                acc_ref[...] = jnp.zeros_like(acc_ref)

            acc_ref[...] += _dot(av, b_ref[...], TN)

            @pl.when(kk == nk - 1)
            def _():
                for s, c0, c1, off in plist:
                    o_ref[s % 2, s // 2, :, c0:c1] = acc_ref[:, off:off + c1 - c0].astype(BF16)

    res = pl.pallas_call(
        _with_comm(body, comm, 1 + nb, 1, grid, -1), grid=grid,
        in_specs=[pl.BlockSpec((tk, tm), lambda i, kk: (kk, i))]
                 + [pl.BlockSpec((tk, b.shape[1]), lambda i, kk: (kk, 0)) for b in bs] + c_in,
        out_specs=[pl.BlockSpec((2, N_DEV // 2, tm, ns), lambda i, kk: (0, 0, i, 0))] + c_out,
        out_shape=[jax.ShapeDtypeStruct((2, N_DEV // 2, m, ns), BF16)] + c_shape,
        scratch_shapes=[pltpu.VMEM((tm, b.shape[1]), F32) for b in bs] + c_scratch,
        compiler_params=_params("arbitrary", "arbitrary"), name=name)(a, *bs, *c_args)
    return res[0], list(res[1:])


def _ln_proj(x, g, shift, scale, w, groups, *, name, tm=512, comm=None):
    l, d = x.shape
    tm = min(tm, l)
    nb = l // tm
    pieces = _grouped_pieces(groups)
    widths = [sum(hi - lo for _, lo, hi in plist) for plist in pieces]
    nw = len(pieces)
    c_args, c_in, c_out, c_shape, c_scratch = _comm_extra(comm)

    def body(x_ref, g_ref, sh_ref, sc_ref, w_ref, h_ref, *p_refs):
        xv = x_ref[...]
        n = xv * lax.rsqrt(jnp.mean(xv * xv, axis=-1, keepdims=True) + EPS)
        h = ((n * g_ref[...]) * (1.0 + sc_ref[...]) + sh_ref[...]).astype(BF16)
        h_ref[...] = h
        for p_ref, plist in zip(p_refs, pieces):
            for off, lo, hi in plist:
                p_ref[:, off:off + hi - lo] = _dot(h, w_ref[:, lo:hi]).astype(BF16)

    vec = pl.BlockSpec((1, d), lambda i: (0, 0))
    row = lambda n: pl.BlockSpec((tm, n), lambda i: (i, 0))
    res = pl.pallas_call(
        _with_comm(body, comm, 5, 1 + nw, nb, -1), grid=(nb,),
        in_specs=[row(d), vec, vec, vec, pl.BlockSpec(w.shape, lambda i: (0, 0))] + c_in,
        out_specs=[row(d)] + [row(n) for n in widths] + c_out,
        out_shape=[jax.ShapeDtypeStruct((l, d), BF16)] + [jax.ShapeDtypeStruct((l, n), BF16) for n in widths] + c_shape,
        scratch_shapes=c_scratch,
        compiler_params=_params("arbitrary"), name=name)(x, g, shift, scale, w, *c_args)
    return res[0], list(res[1:1 + nw]), list(res[1 + nw:])


def _ln_proj_bwd(ds, w, groups, x, dres, g, scale, *, name, tm=256, comm=None):
    l, d = x.shape
    tm = min(tm, l)
    nb = l // tm
    nd = len(ds)
    pieces = _grouped_pieces(groups)
    c_args, c_in, c_out, c_shape, c_scratch = _comm_extra(comm)

    def body(*refs):
        d_refs = refs[:nd]
        w_ref, x_ref, dres_ref, g_ref, sc_ref, dx_ref, sums_ref = refs[nd:]
        dhv = None
        for d_ref, plist in zip(d_refs, pieces):
            for off, lo, hi in plist:
                term = _dot(d_ref[:, off:off + hi - lo], w_ref[:, lo:hi], NT)
                dhv = term if dhv is None else dhv + term
        xv = x_ref[...]
        rstd = lax.rsqrt(jnp.mean(xv * xv, axis=-1, keepdims=True) + EPS)
        n = xv * rstd
        gv = g_ref[...]
        dr = dhv * (1.0 + sc_ref[...])
        dn = dr * gv
        dx_ref[...] = dres_ref[...] + rstd * (dn - n * jnp.mean(dn * n, axis=-1, keepdims=True))

        @pl.when(pl.program_id(0) == 0)
        def _():
            sums_ref[...] = jnp.zeros_like(sums_ref)

        sums_ref[0:1, :] += jnp.sum(dhv, axis=0, keepdims=True)
        sums_ref[1:2, :] += jnp.sum(dhv * (n * gv), axis=0, keepdims=True)
        sums_ref[2:3, :] += jnp.sum(dr * n, axis=0, keepdims=True)

    vec = pl.BlockSpec((1, d), lambda i: (0, 0))
    row = pl.BlockSpec((tm, d), lambda i: (i, 0))
    res = pl.pallas_call(
        _with_comm(body, comm, nd + 5, 2, nb, -1), grid=(nb,),
        in_specs=[pl.BlockSpec((tm, a.shape[1]), lambda i: (i, 0)) for a in ds]
                 + [pl.BlockSpec(w.shape, lambda i: (0, 0)), row, row, vec, vec] + c_in,
        out_specs=[row, pl.BlockSpec((8, d), lambda i: (0, 0))] + c_out,
        out_shape=[jax.ShapeDtypeStruct((l, d), F32), jax.ShapeDtypeStruct((8, d), F32)] + c_shape,
        scratch_shapes=c_scratch,
        compiler_params=_params("arbitrary"), name=name)(*ds, w, x, dres, g, scale, *c_args)
    return res[0], res[1], list(res[2:])


def _final_loss(x, g, target, *, tm=512):
    l, d = x.shape

    def body(x_ref, g_ref, t_ref, dx_ref, sums_ref):
        xv = x_ref[...]
        rstd = lax.rsqrt(jnp.mean(xv * xv, axis=-1, keepdims=True) + EPS)
        n = xv * rstd
        gv = g_ref[...]
        err = n * gv - t_ref[...]
        dy = err * (1.0 / d)
        dn = dy * gv
        dx_ref[...] = rstd * (dn - n * jnp.mean(dn * n, axis=-1, keepdims=True))

        @pl.when(pl.program_id(0) == 0)
        def _():
            sums_ref[...] = jnp.zeros_like(sums_ref)

        sums_ref[0:1, :] += jnp.sum(dy * n, axis=0, keepdims=True)
        sums_ref[1:2, :] += jnp.sum(err * err, axis=0, keepdims=True) * (0.5 / d)

    vec = pl.BlockSpec((1, d), lambda i: (0, 0))
    row = pl.BlockSpec((tm, d), lambda i: (i, 0))
    dx, sums = pl.pallas_call(
        body, grid=(l // tm,),
        in_specs=[row, vec, row],
        out_specs=[row, pl.BlockSpec((8, d), lambda i: (0, 0))],
        out_shape=[jax.ShapeDtypeStruct((l, d), F32), jax.ShapeDtypeStruct((8, d), F32)],
        compiler_params=_params("arbitrary"), name="final_loss")(x, g, target)
    return dx, sums


def _attn_geometry(i, t):
    nk = t + WINDOW
    qi = lax.broadcasted_iota(jnp.int32, (t, nk), 0)
    kj = lax.broadcasted_iota(jnp.int32, (t, nk), 1)
    dist = jnp.abs(qi + WINDOW - kj).astype(F32)
    qc = jnp.right_shift(qi, 6)
    kc = jnp.right_shift(kj, 6)
    valid = (kc >= qc) & (kc <= qc + WINDOW // CHUNK) & ((i > 0) | (kj >= WINDOW))
    return dist, valid


def _attn_head(q, k_all, v_all, sink, slope, dist, valid):
    s = _dot(q, k_all, NT) * (1.0 / math.sqrt(HEAD_DIM)) - slope * dist
    s = jnp.where(valid, s, NEG_INF)
    m = jnp.maximum(jnp.max(s, axis=-1, keepdims=True), sink)
    e = jnp.exp(s - m)
    es = jnp.exp(sink - m)
    inv = 1.0 / (jnp.sum(e, axis=-1, keepdims=True) + es)
    p = e * inv
    o = _dot(p, v_all, NN)
    return p, o, es * inv


def _attn_specs(t):
    cur = pl.BlockSpec((t, ATT_W * 2 + KV_W * 2), lambda i: (i, 0))
    halo_blocks = t // WINDOW
    prev = pl.BlockSpec((WINDOW, 2 * KV_W), lambda i: (jnp.maximum(i * halo_blocks - 1, 0), (2 * ATT_W) // (2 * KV_W)))
    return cur, prev


def _attn_fwd(pa, sinks, *, name, t=ATT_BLOCK, comm=None):
    l = pa.shape[0]
    t = min(t, l)
    nb = l // t
    c_args, c_in, c_out, c_shape, c_scratch = _comm_extra(comm)

    def body(sink_ref, cur_ref, prev_ref, ya_ref):
        i = pl.program_id(0)
        dist, valid = _attn_geometry(i, t)
        for h in range(N_HEADS):
            kh = h // Q_PER_KV
            q = cur_ref[:, h * HEAD_DIM:(h + 1) * HEAD_DIM]
            z = cur_ref[:, ATT_W + h * HEAD_DIM:ATT_W + (h + 1) * HEAD_DIM].astype(F32)
            k_all = jnp.concatenate([prev_ref[:, kh * HEAD_DIM:(kh + 1) * HEAD_DIM],
                                     cur_ref[:, 2 * ATT_W + kh * HEAD_DIM:2 * ATT_W + (kh + 1) * HEAD_DIM]], axis=0)
            v_all = jnp.concatenate([prev_ref[:, KV_W + kh * HEAD_DIM:KV_W + (kh + 1) * HEAD_DIM],
                                     cur_ref[:, 2 * ATT_W + KV_W + kh * HEAD_DIM:2 * ATT_W + KV_W + (kh + 1) * HEAD_DIM]], axis=0)
            _, o, _ = _attn_head(q, k_all, v_all, sink_ref[h], 2.0 ** (-(h + 1)), dist, valid)
            sz, _ = _silu_and_grad(z)
            ya_ref[:, h * HEAD_DIM:(h + 1) * HEAD_DIM] = (o * sz).astype(BF16)

    cur, prev = _attn_specs(t)
    res = pl.pallas_call(
        _with_comm(body, comm, 3, 1, nb, nb - 1), grid=(nb,),
        in_specs=[pl.BlockSpec(memory_space=pltpu.SMEM), cur, prev] + c_in,
        out_specs=[pl.BlockSpec((t, ATT_W), lambda i: (i, 0))] + c_out,
        out_shape=[jax.ShapeDtypeStruct((l, ATT_W), BF16)] + c_shape,
        scratch_shapes=c_scratch,
        compiler_params=_params("arbitrary"), name=name)(sinks, pa, pa, *c_args)
    return res[0], res[1:]


def _attn_bwd(pa, sinks, dya, *, name, t=SEQ_BLOCK):
    l = pa.shape[0]
    t = min(t, l)
    nb = l // t
    scale = 1.0 / math.sqrt(HEAD_DIM)

    def body(sink_ref, cur_ref, prev_ref, dya_ref, dpa_ref, dsink_ref, carry_ref):
        n = pl.program_id(0)
        i = nb - 1 - n
        dist, valid = _attn_geometry(i, t)

        @pl.when(n == 0)
        def _():
            carry_ref[...] = jnp.zeros_like(carry_ref)
            dsink_ref[...] = jnp.zeros_like(dsink_ref)

        dk_acc = [jnp.zeros((HEAD_DIM, t + WINDOW), F32) for _ in range(N_KV_HEADS)]
        dv_acc = [jnp.zeros((HEAD_DIM, t + WINDOW), F32) for _ in range(N_KV_HEADS)]
        for h in range(N_HEADS):
            kh = h // Q_PER_KV
            q = cur_ref[:, h * HEAD_DIM:(h + 1) * HEAD_DIM]
            z = cur_ref[:, ATT_W + h * HEAD_DIM:ATT_W + (h + 1) * HEAD_DIM].astype(F32)
            k_all = jnp.concatenate([prev_ref[:, kh * HEAD_DIM:(kh + 1) * HEAD_DIM],
                                     cur_ref[:, 2 * ATT_W + kh * HEAD_DIM:2 * ATT_W + (kh + 1) * HEAD_DIM]], axis=0)
            v_all = jnp.concatenate([prev_ref[:, KV_W + kh * HEAD_DIM:KV_W + (kh + 1) * HEAD_DIM],
                                     cur_ref[:, 2 * ATT_W + KV_W + kh * HEAD_DIM:2 * ATT_W + KV_W + (kh + 1) * HEAD_DIM]], axis=0)
            p, o, p_sink = _attn_head(q, k_all, v_all, sink_ref[h], 2.0 ** (-(h + 1)), dist, valid)
            dy = dya_ref[:, h * HEAD_DIM:(h + 1) * HEAD_DIM]
            sz, dsz = _silu_and_grad(z)
            do = dy * sz
            dpa_ref[:, ATT_W + h * HEAD_DIM:ATT_W + (h + 1) * HEAD_DIM] = (dy * o * dsz).astype(BF16)
            dp = _dot(do, v_all, NT)
            delta = jnp.sum(p * dp, axis=-1, keepdims=True)
            ds = p * (dp - delta)
            dpa_ref[:, h * HEAD_DIM:(h + 1) * HEAD_DIM] = (_dot(ds, k_all, NN) * scale).astype(BF16)
            dk_acc[kh] = dk_acc[kh] + _dot(q, ds, TN) * scale
            dv_acc[kh] = dv_acc[kh] + _dot(do, p, TN)
            dsink_ref[h:h + 1, :] += jnp.broadcast_to(-jnp.sum(p_sink * delta, axis=0, keepdims=True), (1, 128))

        acc = jnp.concatenate(dk_acc + dv_acc, axis=0).T
        own = acc[WINDOW:, :]
        tail = own[t - WINDOW:, :] + carry_ref[...]
        if t > WINDOW:
            dpa_ref[0:t - WINDOW, 2 * ATT_W:] = own[:t - WINDOW, :].astype(BF16)
        dpa_ref[t - WINDOW:t, 2 * ATT_W:] = tail.astype(BF16)
        carry_ref[...] = acc[:WINDOW, :]

    halo_blocks = t // WINDOW
    wpa = 2 * ATT_W + 2 * KV_W
    cur = pl.BlockSpec((t, wpa), lambda n: (nb - 1 - n, 0))
    prev = pl.BlockSpec((WINDOW, 2 * KV_W),
                        lambda n: (jnp.maximum((nb - 1 - n) * halo_blocks - 1, 0), (2 * ATT_W) // (2 * KV_W)))
    return pl.pallas_call(
        body, grid=(nb,),
        in_specs=[pl.BlockSpec(memory_space=pltpu.SMEM), cur, prev, pl.BlockSpec((t, ATT_W), lambda n: (nb - 1 - n, 0))],
        out_specs=[pl.BlockSpec((t, wpa), lambda n: (nb - 1 - n, 0)), pl.BlockSpec((8, 128), lambda n: (0, 0))],
        out_shape=[jax.ShapeDtypeStruct((l, wpa), BF16), jax.ShapeDtypeStruct((8, 128), F32)],
        scratch_shapes=[pltpu.VMEM((WINDOW, 2 * KV_W), F32)],
        compiler_params=_params("arbitrary"), name=name)(sinks, pa, pa, dya)


def _scan(xr, xi, lr, li, t, reverse):
    row = lax.broadcasted_iota(jnp.int32, (t, 1), 0)
    d = 1
    pr, pi = lr, li
    while d < t:
        if reverse:
            sr = jnp.where(row < t - d, pltpu.roll(xr, t - d, 0), 0.0)
            si = jnp.where(row < t - d, pltpu.roll(xi, t - d, 0), 0.0)
        else:
            sr = jnp.where(row >= d, pltpu.roll(xr, d, 0), 0.0)
            si = jnp.where(row >= d, pltpu.roll(xi, d, 0), 0.0)
        xr, xi = xr + pr * sr - pi * si, xi + pr * si + pi * sr
        pr, pi = pr * pr - pi * pi, 2.0 * pr * pi
        d *= 2
    return xr, xi


SCAN_SUB = 8


def _split_hi_lo(a):
    hi = a.astype(BF16)
    lo = (a - hi.astype(F32)).astype(BF16)
    return jnp.concatenate([hi, lo], axis=0)


def _scan_mxu(xr, xi, tab, lam3, lam8, tri, expand, cr, ci, t, reverse):
    ns = t // SCAN_SUB
    n = xr.shape[1]
    v3 = lambda a: a.reshape(ns, SCAN_SUB, n)
    x3r, x3i = v3(xr), v3(xi)
    br = (x3r * tab[0] - x3i * tab[1]).reshape(t, n)
    bi = (x3r * tab[1] + x3i * tab[0]).reshape(t, n)
    pm = jnp.dot(tri, jnp.concatenate([br, bi], axis=1).astype(BF16), preferred_element_type=F32)
    p3r, p3i = v3(pm[:t, :n]), v3(pm[:t, n:])
    slr = p3r * tab[2] - p3i * tab[3]
    sli = p3r * tab[3] + p3i * tab[2]
    totr, toti = pm[t:, :n], pm[t:, n:]
    l3r, l3i = lam3
    l8r, l8i = lam8
    row = lax.broadcasted_iota(jnp.int32, (ns, 1), 0)
    edge = row == (ns - 1 if reverse else 0)
    er = totr * l3r - toti * l3i + jnp.where(edge, l8r * cr - l8i * ci, 0.0)
    ei = totr * l3i + toti * l3r + jnp.where(edge, l8r * ci + l8i * cr, 0.0)
    er, ei = _scan(er, ei, l8r, l8i, ns, reverse)
    shift = ns - 1 if reverse else 1
    nbr = jnp.where(edge, cr, pltpu.roll(er, shift, 0))
    nbi = jnp.where(edge, ci, pltpu.roll(ei, shift, 0))
    ex = jnp.dot(expand, _split_hi_lo(jnp.concatenate([nbr, nbi], axis=1)), preferred_element_type=F32)
    e3r, e3i = v3(ex[:, :n]), v3(ex[:, n:])
    sr = (slr + e3r * tab[4] - e3i * tab[5]).reshape(t, n)
    si = (sli + e3r * tab[5] + e3i * tab[4]).reshape(t, n)
    out = 0 if reverse else ns - 1
    return sr, si, er[out:out + 1, :], ei[out:out + 1, :]


def _scan_consts(t):
    import numpy as np
    ns = t // SCAN_SUB
    r = np.arange(t)
    same = (r[:, None] // SCAN_SUB) == (r[None, :] // SCAN_SUB)
    sums = (np.arange(ns)[:, None] == (r[None, :] // SCAN_SUB))
    tri = []
    for keep in (r[None, :] <= r[:, None], r[None, :] >= r[:, None]):
        tri.append(np.concatenate([same & keep, sums], axis=0).astype(np.float32))
    ex = ((r[:, None] // SCAN_SUB) == np.arange(ns)[None, :]).astype(np.float32)
    return jnp.asarray(np.stack(tri), BF16), jnp.asarray(np.concatenate([ex, ex], axis=1), BF16)


def _scan_tables(lr, li):
    import numpy as np
    den = lr * lr + li * li
    ir, ii = lr / den, -li / den
    mul = lambda a, b: (a[0] * b[0] - a[1] * b[1], a[0] * b[1] + a[1] * b[0])
    pw = {0: (jnp.ones_like(lr), jnp.zeros_like(lr))}
    for e in range(1, 9):
        pw[e] = mul(pw[e - 1], (lr, li))
    for e in range(-1, -5, -1):
        pw[e] = mul(pw[e + 1], (ir, ii))
    powers = jnp.stack([jnp.stack(pw[e]) for e in range(-4, 9)] + [jnp.zeros((2, lr.shape[0]), F32)])
    j = np.arange(SCAN_SUB)
    exps = [4 - j, j - 4, j + 1, j - 3, 3 - j, 8 - j]
    e_idx = np.stack([exps[t] + 4 for t in range(6) for _ in range(2)])
    c_idx = np.tile(np.array([0, 1])[:, None], (6, SCAN_SUB))
    sign = np.where((c_idx == 1) & (np.arange(12)[:, None] >= 6), -1.0, 1.0).astype(np.float32)
    tabs = powers[e_idx, c_idx] * sign[:, :, None]
    lam = powers[np.array([5, 5, 7, 7, 12, 12, 13, 13]), np.array([0, 1, 0, 1, 0, 1, 0, 0])]
    return lam, tabs


SSM_HALVES = 2
SSM_HW = SSM_W // SSM_HALVES
SSM_HN = SSM_N // SSM_HALVES


def _bd_nn(x, w):
    a = w.shape[1]
    return jnp.concatenate([_dot(x[:, h * a:(h + 1) * a], w[h]) for h in range(SSM_HALVES)], axis=1)


def _bd_nt(x, w):
    b = w.shape[2]
    return jnp.concatenate([_dot(x[:, h * b:(h + 1) * b], w[h], NT) for h in range(SSM_HALVES)], axis=1)


def _bd_tn(x, y):
    a, b = x.shape[1] // SSM_HALVES, y.shape[1] // SSM_HALVES
    return jnp.stack([_dot(x[:, h * a:(h + 1) * a], y[:, h * b:(h + 1) * b], TN) for h in range(SSM_HALVES)])


def _ssm_states(u, s0r, s0i, lam_ref, tab_ref, tri_ref, ex_ref, bre, bim, t):
    tab = tuple(tab_ref[k] for k in range(6))
    return _scan_mxu(_bd_nn(u, bre), _bd_nn(u, bim), tab, (lam_ref[2:3, :], lam_ref[3:4, :]),
                     (lam_ref[4:5, :], lam_ref[5:6, :]), tri_ref[0], ex_ref[...], s0r, s0i, t, False)


def _ssm_head(u, z, xr, xi, cre, cim, dskip, wglu, bglu):
    y = _bd_nn(xr, cre) - _bd_nn(xi, cim) + dskip * u
    y2, dgelu = _gelu_and_grad(y)
    gate = _sigmoid(_dot(y2, wglu) + bglu)
    y3 = y2 * gate
    return y2, dgelu, gate, y3


def _with_comm(body, comm, n_in, n_out, grid, mid_step):
    if comm is None:
        return body
    nc = len(comm["args"])
    n_sem = len(comm["scratch"])
    grid = (grid,) if isinstance(grid, int) else tuple(grid)
    total = math.prod(grid)

    def hosted(*refs):
        ins, cin = refs[:n_in], refs[n_in:n_in + nc]
        outs, cout = refs[n_in + nc:n_in + nc + n_out], refs[n_in + nc + n_out:n_in + 2 * nc + n_out]
        rest = refs[n_in + 2 * nc + n_out:]
        scratch, csem = rest[:len(rest) - n_sem], rest[len(rest) - n_sem:]
        start, forward, finish = comm["phases"](cin, cout, *csem)
        step = pl.program_id(0)
        for axis in range(1, len(grid)):
            step = step * grid[axis] + pl.program_id(axis)
        pl.when(step == 0)(start)
        pl.when(step == (mid_step if mid_step >= 0 else total + mid_step))(forward)
        body(*ins, *outs, *scratch)
        pl.when(step == total - 1)(finish)

    return hosted


def _comm_extra(comm):
    if comm is None:
        return [], [], [], [], []
    anyspec = pl.BlockSpec(memory_space=pl.ANY)
    nc = len(comm["args"])
    return comm["args"], [anyspec] * nc, [anyspec] * nc, comm["out_shape"], comm["scratch"]


def _ssm_fwd(ps, scan_ops, bblk, cblk, dskip, wglu, bglu, *, name, t=SEQ_BLOCK, comm=None):
    l = ps.shape[0]
    assert l % t == 0
    nb = l // t
    ns = t // SCAN_SUB
    c_args, c_in, c_out, c_shape, c_scratch = _comm_extra(comm)

    def body(ps_ref, lam_ref, tab_ref, tri_ref, ex_ref, b_ref, c_ref, d_ref, w_ref, bg_ref, ys_ref, chk_ref, xs_ref,
             st_ref):
        @pl.when(pl.program_id(0) == 0)
        def _():
            st_ref[...] = jnp.zeros_like(st_ref)

        chk_ref[...] = jnp.broadcast_to(st_ref[...], chk_ref.shape)
        u = ps_ref[:, :SSM_W].astype(F32)
        z = ps_ref[:, SSM_W:].astype(F32)
        xr, xi, er, ei = _ssm_states(u, st_ref[:, :SSM_N], st_ref[:, SSM_N:], lam_ref, tab_ref, tri_ref, ex_ref,
                                     b_ref[0], b_ref[1], t)
        st_ref[:, :SSM_N] = er
        st_ref[:, SSM_N:] = ei
        xr, xi = xr.astype(BF16), xi.astype(BF16)
        xs_ref[:, :SSM_N] = xr
        xs_ref[:, SSM_N:] = xi
        _, _, _, y3 = _ssm_head(u, z, xr, xi, c_ref[0], c_ref[1], d_ref[...], w_ref[...], bg_ref[...])
        sz, _ = _silu_and_grad(z)
        ys_ref[...] = (y3 * sz).astype(BF16)

    full = lambda shape: pl.BlockSpec(shape, lambda i: (0,) * len(shape))
    return pl.pallas_call(
        _with_comm(body, comm, 10, 3, nb, nb - 1), grid=(nb,),
        in_specs=[pl.BlockSpec((t, 2 * SSM_W), lambda i: (i, 0)), full((8, SSM_N)), full((12, SCAN_SUB, SSM_N)),
                  full((2, t + ns, t)), full((t, 2 * ns)), full((2, SSM_HALVES, SSM_HW, SSM_HN)),
                  full((2, SSM_HALVES, SSM_HN, SSM_HW)), full((1, SSM_W)), full((SSM_W, SSM_W)), full((1, SSM_W))] + c_in,
        out_specs=[pl.BlockSpec((t, SSM_W), lambda i: (i, 0)), pl.BlockSpec((8, 2 * SSM_N), lambda i: (i, 0)),
                   pl.BlockSpec((t, 2 * SSM_N), lambda i: (i, 0))] + c_out,
        out_shape=[jax.ShapeDtypeStruct((l, SSM_W), BF16), jax.ShapeDtypeStruct((nb * 8, 2 * SSM_N), F32),
                   jax.ShapeDtypeStruct((l, 2 * SSM_N), BF16)] + c_shape,
        scratch_shapes=[pltpu.VMEM((1, 2 * SSM_N), F32)] + c_scratch,
        compiler_params=_params("arbitrary"), name=name)(ps, *scan_ops, bblk, cblk, dskip, wglu, bglu, *c_args)


def _ssm_bwd(ps, dys, chk, states, scan_ops, bblk, cblk, dskip, wglu, bglu, *, name, t=SEQ_BLOCK, comm=None):
    l = ps.shape[0]
    assert l % t == 0
    nb = l // t
    ns = t // SCAN_SUB
    c_args, c_in, c_out, c_shape, c_scratch = _comm_extra(comm)

    def body(ps_ref, dys_ref, chk_ref, xs_ref, lam_ref, tab_ref, tri_ref, ex_ref, b_ref, c_ref, d_ref, w_ref, bg_ref,
             dps_ref, db_ref, dc_ref, dw_acc, sums_acc, gc_ref, db_acc, dc_acc):
        n = pl.program_id(0)

        @pl.when(n == 0)
        def _():
            gc_ref[...] = jnp.zeros_like(gc_ref)
            db_acc[...] = jnp.zeros_like(db_acc)
            dc_acc[...] = jnp.zeros_like(dc_acc)
            dw_acc[...] = jnp.zeros_like(dw_acc)
            sums_acc[...] = jnp.zeros_like(sums_acc)

        row = lax.broadcasted_iota(jnp.int32, (t, 1), 0)
        u = ps_ref[:, :SSM_W].astype(F32)
        z = ps_ref[:, SSM_W:].astype(F32)
        s0r, s0i = chk_ref[0:1, :SSM_N], chk_ref[0:1, SSM_N:]
        xr, xi = xs_ref[:, :SSM_N], xs_ref[:, SSM_N:]
        dskip = d_ref[...]
        y2, dgelu, gate, y3 = _ssm_head(u, z, xr, xi, c_ref[0], c_ref[1], dskip, w_ref[...], bg_ref[...])
        sz, dsz = _silu_and_grad(z)
        dys_v = dys_ref[...]
        dps_ref[:, SSM_W:] = (dys_v * y3 * dsz).astype(BF16)
        dy3 = dys_v * sz
        da = dy3 * y2 * gate * (1.0 - gate)
        dy2 = dy3 * gate + _dot(da, w_ref[...], NT)
        dw_acc[...] += _dot(y2, da, TN)
        dy = dy2 * dgelu
        sums_acc[2:3, :SSM_W] += jnp.sum(dy * u, axis=0, keepdims=True)
        sums_acc[3:4, :SSM_W] += jnp.sum(da, axis=0, keepdims=True)
        dc_acc[0] += _bd_tn(dy, xr)
        dc_acc[1] += -_bd_tn(dy, xi)
        rev_tab = tuple(tab_ref[k] for k in range(6, 12))
        gr, gi, gcr, gci = _scan_mxu(
            _bd_nt(dy, c_ref[0]), -_bd_nt(dy, c_ref[1]), rev_tab, (lam_ref[2:3, :], -lam_ref[3:4, :]),
            (lam_ref[4:5, :], -lam_ref[5:6, :]), tri_ref[1], ex_ref[...], gc_ref[:, :SSM_N], gc_ref[:, SSM_N:], t, True)
        gc_ref[:, :SSM_N] = gcr
        gc_ref[:, SSM_N:] = gci
        db_acc[0] += _bd_tn(u, gr)
        db_acc[1] += _bd_tn(u, gi)
        du = dskip * dy + _bd_nt(gr, b_ref[0]) + _bd_nt(gi, b_ref[1])
        dps_ref[:, :SSM_W] = du.astype(BF16)
        spr = jnp.where(row == 0, s0r, pltpu.roll(xr.astype(F32), 1, 0))
        spi = jnp.where(row == 0, s0i, pltpu.roll(xi.astype(F32), 1, 0))
        sums_acc[0:1, :] += jnp.sum(gr * spr + gi * spi, axis=0, keepdims=True)
        sums_acc[1:2, :] += jnp.sum(gi * spr - gr * spi, axis=0, keepdims=True)

        @pl.when(n == nb - 1)
        def _():
            per_half = SSM_GROUPS // SSM_HALVES
            for k in range(2):
                for g in range(SSM_GROUPS):
                    h, gl = divmod(g, per_half)
                    c0, p0 = gl * SSM_GROUP, gl * SSM_STATE
                    db_ref[k, g * SSM_GROUP:(g + 1) * SSM_GROUP, :] = db_acc[k, h, c0:c0 + SSM_GROUP, p0:p0 + SSM_STATE]
                    dc_ref[k, g * SSM_GROUP:(g + 1) * SSM_GROUP, :] = dc_acc[k, h, c0:c0 + SSM_GROUP, p0:p0 + SSM_STATE]

    full = lambda shape: pl.BlockSpec(shape, lambda n: (0,) * len(shape))
    return pl.pallas_call(
        _with_comm(body, comm, 13, 5, nb, 0), grid=(nb,),
        in_specs=[pl.BlockSpec((t, 2 * SSM_W), lambda n: (nb - 1 - n, 0)),
                  pl.BlockSpec((t, SSM_W), lambda n: (nb - 1 - n, 0)),
                  pl.BlockSpec((8, 2 * SSM_N), lambda n: (nb - 1 - n, 0)),
                  pl.BlockSpec((t, 2 * SSM_N), lambda n: (nb - 1 - n, 0)),
                  full((8, SSM_N)), full((12, SCAN_SUB, SSM_N)), full((2, t + ns, t)), full((t, 2 * ns)),
                  full((2, SSM_HALVES, SSM_HW, SSM_HN)), full((2, SSM_HALVES, SSM_HN, SSM_HW)), full((1, SSM_W)),
                  full((SSM_W, SSM_W)), full((1, SSM_W))] + c_in,
        out_specs=[pl.BlockSpec((t, 2 * SSM_W), lambda n: (nb - 1 - n, 0)), full((2, SSM_W, SSM_STATE)),
                   full((2, SSM_W, SSM_STATE)), full((SSM_W, SSM_W)), full((8, SSM_N))] + c_out,
        out_shape=[jax.ShapeDtypeStruct((l, 2 * SSM_W), BF16),
                   jax.ShapeDtypeStruct((2, SSM_W, SSM_STATE), F32),
                   jax.ShapeDtypeStruct((2, SSM_W, SSM_STATE), F32),
                   jax.ShapeDtypeStruct((SSM_W, SSM_W), F32),
                   jax.ShapeDtypeStruct((8, SSM_N), F32)] + c_shape,
        scratch_shapes=[pltpu.VMEM((1, 2 * SSM_N), F32), pltpu.VMEM((2, SSM_HALVES, SSM_HW, SSM_HN), F32),
                        pltpu.VMEM((2, SSM_HALVES, SSM_HW, SSM_HN), F32)] + c_scratch,
        compiler_params=_params("arbitrary"), name=name)(ps, dys, chk, states, *scan_ops, bblk, cblk, dskip, wglu, bglu,
                                                         *c_args)


def _pool_count(i, t):
    pos = lax.broadcasted_iota(jnp.int32, (t, POOL_W), 0) + i * t + 1
    col = lax.broadcasted_iota(jnp.int32, (t, POOL_W), 1)
    win = jnp.where(col < POOL_GW, 2, jnp.where(col < 2 * POOL_GW, 4, jnp.where(col < 3 * POOL_GW, 8, 16)))
    return 1.0 / jnp.minimum(pos, win).astype(F32), col


def _window_sums(ext, n_rows, forward):
    col = lax.broadcasted_iota(jnp.int32, ext.shape, 1)
    sh = (lambda a, d: pltpu.roll(a, d, 0)) if forward else (lambda a, d: pltpu.roll(a, n_rows - d, 0))
    a2 = ext + sh(ext, 1)
    a4 = a2 + sh(a2, 2)
    a8 = a4 + sh(a4, 4)
    a16 = a8 + sh(a8, 8)
    return jnp.where(col < POOL_GW, a2, jnp.where(col < 2 * POOL_GW, a4, jnp.where(col < 3 * POOL_GW, a8, a16)))


def _pool_mix(pooled, wp_ref):
    return jnp.concatenate([_dot(pooled[:, g * POOL_GW:(g + 1) * POOL_GW], wp_ref[g]) for g in range(4)], axis=1)


def _pool_pooled(i, cur_u, prev_u, t):
    prev = jnp.where(i > 0, prev_u, 0.0)
    ext = jnp.concatenate([prev, cur_u], axis=0)
    inv_cnt, _ = _pool_count(i, t)
    return _window_sums(ext, t + POOL_HALO, True)[POOL_HALO:, :] * inv_cnt - cur_u


def _pool_fwd(pp, wpool, pscale, *, name, t=POOL_BLOCK):
    l = pp.shape[0]
    t = min(t, l)

    def body(cur_ref, prev_ref, wp_ref, sc_ref, yp_ref):
        i = pl.program_id(0)
        pooled = _pool_pooled(i, cur_ref[:, :POOL_W].astype(F32), prev_ref[...].astype(F32), t)
        lin = _pool_mix(pooled, wp_ref)
        sz, _ = _silu_and_grad(cur_ref[:, POOL_W:].astype(F32))
        yp_ref[...] = (lin * sc_ref[...] * sz).astype(BF16)

    hb = t // POOL_HALO
    return pl.pallas_call(
        body, grid=(l // t,),
        in_specs=[pl.BlockSpec((t, 2 * POOL_W), lambda i: (i, 0)),
                  pl.BlockSpec((POOL_HALO, POOL_W), lambda i: (jnp.maximum(i * hb - 1, 0), 0)),
                  pl.BlockSpec((4, POOL_GW, POOL_GW), lambda i: (0, 0, 0)),
                  pl.BlockSpec((1, POOL_W), lambda i: (0, 0))],
        out_specs=pl.BlockSpec((t, POOL_W), lambda i: (i, 0)),
        out_shape=jax.ShapeDtypeStruct((l, POOL_W), BF16),
        compiler_params=_params("parallel"), name=name)(pp, pp, wpool, pscale)


def _pool_bwd(pp, dyp, wpool, pscale, *, name, t=POOL_BLOCK):
    l = pp.shape[0]
    t = min(t, l)
    nb = l // t

    def body(cur_ref, prev_ref, dyp_ref, wp_ref, sc_ref, dpp_ref, dwp_ref, sums_ref, carry_ref):
        n = pl.program_id(0)
        i = nb - 1 - n

        @pl.when(n == 0)
        def _():
            carry_ref[...] = jnp.zeros_like(carry_ref)
            dwp_ref[...] = jnp.zeros_like(dwp_ref)
            sums_ref[...] = jnp.zeros_like(sums_ref)

        cur_u = cur_ref[:, :POOL_W].astype(F32)
        pooled = _pool_pooled(i, cur_u, prev_ref[...].astype(F32), t)
        lin = _pool_mix(pooled, wp_ref)
        sz, dsz = _silu_and_grad(cur_ref[:, POOL_W:].astype(F32))
        dyp_v = dyp_ref[...]
        scale = sc_ref[...]
        dpp_ref[:, POOL_W:] = (dyp_v * lin * scale * dsz).astype(BF16)
        dpre = dyp_v * sz
        sums_ref[0:1, :] += jnp.sum(dpre * lin, axis=0, keepdims=True)
        dlin = dpre * scale
        dpooled = []
        for g in range(4):
            dl = dlin[:, g * POOL_GW:(g + 1) * POOL_GW]
            dwp_ref[g] += _dot(pooled[:, g * POOL_GW:(g + 1) * POOL_GW], dl, TN)
            dpooled.append(_dot(dl, wp_ref[g], NT))
        dpooled = jnp.concatenate(dpooled, axis=1)
        inv_cnt, _ = _pool_count(i, t)
        dq = dpooled * inv_cnt
        ext = jnp.concatenate([dq, carry_ref[...]], axis=0)
        du = _window_sums(ext, t + POOL_HALO, False)[:t, :] - dpooled
        dpp_ref[:, :POOL_W] = du.astype(BF16)
        carry_ref[...] = dq[:POOL_HALO, :]

    hb = t // POOL_HALO
    return pl.pallas_call(
        body, grid=(nb,),
        in_specs=[pl.BlockSpec((t, 2 * POOL_W), lambda n: (nb - 1 - n, 0)),
                  pl.BlockSpec((POOL_HALO, POOL_W), lambda n: (jnp.maximum((nb - 1 - n) * hb - 1, 0), 0)),
                  pl.BlockSpec((t, POOL_W), lambda n: (nb - 1 - n, 0)),
                  pl.BlockSpec((4, POOL_GW, POOL_GW), lambda n: (0, 0, 0)),
                  pl.BlockSpec((1, POOL_W), lambda n: (0, 0))],
        out_specs=[pl.BlockSpec((t, 2 * POOL_W), lambda n: (nb - 1 - n, 0)),
                   pl.BlockSpec((4, POOL_GW, POOL_GW), lambda n: (0, 0, 0)),
                   pl.BlockSpec((8, POOL_W), lambda n: (0, 0))],
        out_shape=[jax.ShapeDtypeStruct((l, 2 * POOL_W), BF16), jax.ShapeDtypeStruct((4, POOL_GW, POOL_GW), F32),
                   jax.ShapeDtypeStruct((8, POOL_W), F32)],
        scratch_shapes=[pltpu.VMEM((POOL_HALO, POOL_W), F32)],
        compiler_params=_params("arbitrary"), name=name)(pp, pp, dyp, wpool, pscale)


def _merge_fwd(ya, ys, yp, wa, ws, wp, pg, wout, x, gate, *, name, tm=512):
    l, d = x.shape
    tm = min(tm, l)

    def body(ya_ref, ys_ref, yp_ref, wa_ref, ws_ref, wp_ref, pg_ref, wo_ref, x_ref, g_ref,
             xn_ref, mg_ref, ba_ref, bs_ref, bp_ref, out_ref):
        acc = None
        for k, (y_ref, w_ref, b_ref) in enumerate(((ya_ref, wa_ref, ba_ref), (ys_ref, ws_ref, bs_ref),
                                                   (yp_ref, wp_ref, bp_ref))):
            br = _dot(y_ref[...], w_ref[...])
            b_ref[...] = br.astype(BF16)
            term = _sigmoid(pg_ref[:, k * d:(k + 1) * d].astype(F32)) * br
            acc = term if acc is None else acc + term
        merged = acc.astype(BF16)
        mg_ref[...] = merged
        out = _dot(merged, wo_ref[...])
        out_ref[...] = out.astype(BF16)
        xn_ref[...] = x_ref[...] + g_ref[...] * out

    rowy = pl.BlockSpec((tm, ATT_W), lambda i: (i, 0))
    wsp = pl.BlockSpec((ATT_W, d), lambda i: (0, 0))
    rowd = pl.BlockSpec((tm, d), lambda i: (i, 0))
    return pl.pallas_call(
        body, grid=(l // tm,),
        in_specs=[rowy, rowy, rowy, wsp, wsp, wsp, pl.BlockSpec((tm, 3 * d), lambda i: (i, 0)),
                  pl.BlockSpec((d, d), lambda i: (0, 0)), rowd, pl.BlockSpec((1, d), lambda i: (0, 0))],
        out_specs=[rowd] * 6,
        out_shape=[jax.ShapeDtypeStruct((l, d), F32)] + [jax.ShapeDtypeStruct((l, d), BF16)] * 5,
        compiler_params=_params("parallel"), name=name)(ya, ys, yp, wa, ws, wp, pg, wout, x, gate)


def _merge_bwd(dx, out, gate, wout, pg, brs, wbrs, ys, merged, *, name, tm=256, comm=None):
    l, d = dx.shape
    tm = min(tm, l)
    nb = l // tm
    w = ys[0].shape[1]

    def body(dx_ref, out_ref, g_ref, w_ref, pg_ref, ba_ref, bs_ref, bp_ref, wa_ref, ws_ref, wp_ref,
             ya_ref, ys_ref, yp_ref, mg_ref,
             dya_ref, dys_ref, dyp_ref, dpg_ref, sums_ref, dwa_ref, dws_ref, dwp_ref, dwo_ref, acc_br, acc_out):
        i = pl.program_id(0)

        @pl.when(i == 0)
        def _():
            sums_ref[...] = jnp.zeros_like(sums_ref)
            acc_br[...] = jnp.zeros_like(acc_br)
            acc_out[...] = jnp.zeros_like(acc_out)

        dxv = dx_ref[...]
        sums_ref[0:1, :] += jnp.sum(dxv * out_ref[...].astype(F32), axis=0, keepdims=True)
        dmo = (dxv * g_ref[...]).astype(BF16)
        acc_out[...] += _dot(mg_ref[...], dmo, TN)
        dmerged = _dot(dmo, w_ref[...], NT)
        branches = ((ba_ref, wa_ref, ya_ref, dya_ref), (bs_ref, ws_ref, ys_ref, dys_ref), (bp_ref, wp_ref, yp_ref, dyp_ref))
        for k, (b_ref, wk_ref, y_ref, dy_ref) in enumerate(branches):
            gk = _sigmoid(pg_ref[:, k * d:(k + 1) * d].astype(F32))
            dbr = (dmerged * gk).astype(BF16)
            dpg_ref[:, k * d:(k + 1) * d] = (dmerged * b_ref[...].astype(F32) * gk * (1.0 - gk)).astype(BF16)
            dy_ref[...] = _dot(dbr, wk_ref[...], NT)
            acc_br[k] += _dot(y_ref[...], dbr, TN)

        @pl.when(i == nb - 1)
        def _():
            for k, dw_ref in enumerate((dwa_ref, dws_ref, dwp_ref)):
                dw_ref[...] = acc_br[k].astype(BF16)
            dwo_ref[...] = acc_out[...].astype(BF16)

    row = pl.BlockSpec((tm, d), lambda i: (i, 0))
    half = pl.BlockSpec((tm, w), lambda i: (i, 0))
    wide = pl.BlockSpec((tm, 3 * d), lambda i: (i, 0))
    const = lambda shape: pl.BlockSpec(shape, lambda i: (0,) * len(shape))
    c_args, c_in, c_out, c_shape, c_scratch = _comm_extra(comm)
    res = pl.pallas_call(
        _with_comm(body, comm, 15, 9, nb, 0), grid=(nb,),
        in_specs=[row, row, const((1, d)), const((d, d)), wide, row, row, row, const((w, d)), const((w, d)), const((w, d)),
                  half, half, half, row] + c_in,
        out_specs=[half, half, half, wide, const((8, d)), const((w, d)), const((w, d)), const((w, d)), const((d, d))] + c_out,
        out_shape=[jax.ShapeDtypeStruct((l, w), F32)] * 3 + [jax.ShapeDtypeStruct((l, 3 * d), BF16),
                                                             jax.ShapeDtypeStruct((8, d), F32)]
                  + [jax.ShapeDtypeStruct((w, d), BF16)] * 3 + [jax.ShapeDtypeStruct((d, d), BF16)] + c_shape,
        scratch_shapes=[pltpu.VMEM((3, w, d), F32), pltpu.VMEM((d, d), F32)] + c_scratch,
        compiler_params=_params("arbitrary"), name=name)(dx, out, gate, wout, pg, *brs, *wbrs, *ys, merged, *c_args)
    return list(res[:9]), list(res[9:])


def _adamw(w, gs, m, v, *, name, tr=256):
    r, c = w.shape
    ns = len(gs)
    p, rs, _ = gs[0].shape
    assert rs * ns == r
    tr = min(tr, rs)
    assert rs % tr == 0
    nr = rs // tr
    c1 = 1.0 / (1.0 - ADAM_B1 ** ADAM_STEP)
    c2 = 1.0 / (1.0 - ADAM_B2 ** ADAM_STEP)

    def body(*refs):
        w_ref, g_refs, (m_ref, v_ref, go_ref, d_ref, mo_ref, vo_ref) = refs[0], refs[1:1 + ns], refs[1 + ns:]
        slab = pl.program_id(0)
        gv = None
        for k, g_ref in enumerate(g_refs):
            gk = g_ref[0].astype(F32)
            for j in range(1, p):
                gk = gk + g_ref[j].astype(F32)
            gv = gk if gv is None else jnp.where(slab == k, gk, gv)
        go_ref[...] = gv
        mn = ADAM_B1 * m_ref[...] + (1.0 - ADAM_B1) * gv
        vn = ADAM_B2 * v_ref[...] + (1.0 - ADAM_B2) * (gv * gv)
        mo_ref[...] = mn
        vo_ref[...] = vn
        d_ref[...] = -ADAM_LR * ((mn * c1) / (jnp.sqrt(vn * c2) + ADAM_EPS) + ADAM_WD * w_ref[...])

    row = pl.BlockSpec((tr, c), lambda s, i: (s * nr + i, 0))
    g_specs = [pl.BlockSpec((p, tr, c), lambda s, i, k=k: (0, jnp.where(s == k, i, 0), 0)) for k in range(ns)]
    return pl.pallas_call(
        body, grid=(ns, nr),
        in_specs=[row] + g_specs + [row, row],
        out_specs=[row] * 4,
        out_shape=[jax.ShapeDtypeStruct((r, c), F32)] * 4,
        compiler_params=_params("arbitrary", "arbitrary"), name=name)(w, *gs, m, v)


def _mesh_place():
    x, y, c = lax.axis_index("x"), lax.axis_index("y"), lax.axis_index("c")
    other_chips = [(1 - x, y), (x, 1 - y), (1 - x, 1 - y)]
    return x, y, c, other_chips


def _run_plan(plan, *, name):
    n = len(plan["args"])

    def body(*refs):
        start, forward, finish = plan["phases"](refs[:n], refs[n:2 * n], *refs[2 * n:])
        start()
        forward()
        finish()

    anyspec = pl.BlockSpec(memory_space=pl.ANY)
    return pl.pallas_call(
        body, in_specs=[anyspec] * n, out_specs=[anyspec] * n, out_shape=plan["out_shape"],
        scratch_shapes=plan["scratch"], name=name)(*plan["args"])


def _gather_plan(arrs):
    n = len(arrs)

    def phases(ins, outs, send_sems, recv_sems, loc_sems):
        x, y, c, chips = _mesh_place()
        me = 4 * x + 2 * y + c
        slot = lambda px, py, pc: 4 * px + 2 * py + pc

        def copy(k, j, src, block, to):
            return pltpu.make_async_remote_copy(
                src_ref=src, dst_ref=outs[k].at[block], send_sem=send_sems.at[k, j], recv_sem=recv_sems.at[k, j],
                device_id=to, device_id_type=pl.DeviceIdType.MESH)

        local = [pltpu.make_async_copy(ins[k], outs[k].at[me], loc_sems.at[k]) for k in range(n)]
        first = []
        for k in range(n):
            first.append(copy(k, 0, ins[k], me, (x, y, 1 - c)))
            for j, chip in enumerate(chips):
                first.append(copy(k, 1 + j, ins[k], me, (*chip, c)))
        passed = [copy(k, 4 + j, outs[k].at[slot(*chip, c)], slot(*chip, c), (x, y, 1 - c))
                  for j, chip in enumerate(chips) for k in range(n)]

        def start():
            for cp in local + first:
                cp.start()

        def forward():
            for j, chip in enumerate(chips):
                for k in range(n):
                    copy(k, 1 + j, ins[k], slot(*chip, c), (x, y, c)).wait_recv()
                    passed[j * n + k].start()

        def finish():
            for k in range(n):
                copy(k, 0, ins[k], slot(x, y, 1 - c), (x, y, c)).wait_recv()
                for j, chip in enumerate(chips):
                    copy(k, 4 + j, ins[k], slot(*chip, 1 - c), (x, y, c)).wait_recv()
            for cp in first + passed:
                cp.wait_send()
            for cp in local:
                cp.wait()

        return start, forward, finish

    return dict(
        args=list(arrs), out_shape=[jax.ShapeDtypeStruct((N_DEV,) + a.shape, a.dtype) for a in arrs],
        scratch=[pltpu.SemaphoreType.DMA((n, 7)), pltpu.SemaphoreType.DMA((n, 7)), pltpu.SemaphoreType.DMA((n,))],
        phases=phases)


def _allreduce_small(small, extra, *, name):
    r, lanes = small.shape
    assert r % 16 == 0
    h = r // 2
    e = extra.shape[0]

    def body(s_ref, x_ref, out_ref, xall_ref, sib_ref, parts_ref, send_sems, recv_sems):
        x, y, c, chips = _mesh_place()
        me = 4 * x + 2 * y + c
        my_chip = 2 * x + y
        sibling = (x, y, 1 - c)
        mine = pl.ds(pl.multiple_of(c * h, 8), h)
        theirs = pl.ds(pl.multiple_of((1 - c) * h, 8), h)

        def remote(j, src, dst, to):
            return pltpu.make_async_remote_copy(src_ref=src, dst_ref=dst, send_sem=send_sems.at[j],
                                                recv_sem=recv_sems.at[j], device_id=to, device_id_type=pl.DeviceIdType.MESH)

        to_sibling = remote(0, s_ref.at[theirs], sib_ref, sibling)
        to_sibling.start()
        xall_ref[me] = x_ref[...]
        extras = []
        for rr in range(1, N_DEV):
            peer = me ^ rr
            cp = remote(4 + rr, x_ref, xall_ref.at[me], (peer // 4, (peer // 2) % 2, peer % 2))
            cp.start()
            extras.append(cp)
        to_sibling.wait_recv()
        parts_ref[my_chip] = s_ref[mine] + sib_ref[...]
        to_chips = [remote(1 + j, parts_ref.at[my_chip], parts_ref.at[my_chip], (px, py, c))
                    for j, (px, py) in enumerate(chips)]
        for cp in to_chips:
            cp.start()
        for cp in to_chips:
            cp.wait_recv()
        out_ref[mine] = (parts_ref[0] + parts_ref[1]) + (parts_ref[2] + parts_ref[3])
        done = remote(4, out_ref.at[mine], out_ref.at[mine], sibling)
        done.start()
        remote(4, out_ref.at[theirs], out_ref.at[theirs], sibling).wait_recv()
        for cp in extras:
            cp.wait()
        to_sibling.wait_send()
        for cp in to_chips:
            cp.wait_send()
        done.wait_send()

    vmem = pl.BlockSpec(memory_space=pltpu.VMEM)
    return pl.pallas_call(
        body, in_specs=[vmem, vmem], out_specs=[vmem, vmem],
        out_shape=[jax.ShapeDtypeStruct((r, lanes), F32), jax.ShapeDtypeStruct((N_DEV, e, lanes), F32)],
        scratch_shapes=[pltpu.VMEM((h, lanes), F32), pltpu.VMEM((4, h, lanes), F32),
                        pltpu.SemaphoreType.DMA((12,)), pltpu.SemaphoreType.DMA((12,))],
        compiler_params=pltpu.CompilerParams(vmem_limit_bytes=VMEM_LIMIT), name=name)(small, extra)


def _ada_modulation(c, w_ada, b_cols, comm):
    depth, d, cols = w_ada.shape
    c_args, c_in, c_out, c_shape, c_scratch = _comm_extra(comm)
    nc = len(c_args)

    def body(c_ref, w_ref, b_ref, *refs):
        cin, (cact_ref, mod_ref), cout = refs[:nc], refs[nc:nc + 2], refs[nc + 2:2 * nc + 2]
        call_ref, part_ref, send_sems, recv_sems = refs[2 * nc + 2:2 * nc + 6]
        start, forward, finish = comm["phases"](cin, cout, *refs[2 * nc + 6:])
        start()
        x, y, core, _ = _mesh_place()
        me = 4 * x + 2 * y + core

        def to_all(j0, src, dst):
            copies = []
            for r in range(1, N_DEV):
                peer = me ^ r
                copies.append(pltpu.make_async_remote_copy(
                    src_ref=src, dst_ref=dst, send_sem=send_sems.at[j0 + r - 1], recv_sem=recv_sems.at[j0 + r - 1],
                    device_id=(peer // 4, (peer // 2) % 2, peer % 2), device_id_type=pl.DeviceIdType.MESH))
            for cp in copies:
                cp.start()
            for cp in copies:
                cp.wait()

        call_ref[me] = c_ref[...]
        to_all(0, c_ref, call_ref.at[me])
        c_act = jnp.concatenate([call_ref[k] for k in range(N_DEV)], axis=0)
        c_act = c_act * _sigmoid(c_act)
        cact_ref[...] = c_act
        for li in range(depth):
            part_ref[li] = _dot(c_act, w_ref[li]) + b_ref[li:li + 1, :]
        mod_ref[me] = part_ref[...]
        to_all(N_DEV - 1, part_ref, mod_ref.at[me])
        forward()
        finish()

    vmem = pl.BlockSpec(memory_space=pltpu.VMEM)
    res = pl.pallas_call(
        body, in_specs=[vmem] * 3 + c_in, out_specs=[vmem] * 2 + c_out,
        out_shape=[jax.ShapeDtypeStruct((N_DEV, d), F32), jax.ShapeDtypeStruct((N_DEV, depth, N_DEV, cols), F32)] + c_shape,
        scratch_shapes=[pltpu.VMEM((N_DEV, 1, d), F32), pltpu.VMEM((depth, N_DEV, cols), F32),
                        pltpu.SemaphoreType.DMA((2 * (N_DEV - 1),)), pltpu.SemaphoreType.DMA((2 * (N_DEV - 1),))] + c_scratch,
        compiler_params=pltpu.CompilerParams(vmem_limit_bytes=VMEM_LIMIT), name="ada_modulation")(c, w_ada, b_cols, *c_args)
    return res[0], res[1], list(res[2:])


def _sibling_swap_plan(arrs):
    n = len(arrs)

    def phases(ins, outs, send_sems, recv_sems):
        x, y, c, _ = _mesh_place()
        copies = [pltpu.make_async_remote_copy(
            src_ref=ins[k].at[1 - c], dst_ref=outs[k], send_sem=send_sems.at[k], recv_sem=recv_sems.at[k],
            device_id=(x, y, 1 - c), device_id_type=pl.DeviceIdType.MESH) for k in range(n)]

        def start():
            for cp in copies:
                cp.start()

        def finish():
            for cp in copies:
                cp.wait()

        return start, (lambda: None), finish

    return dict(args=list(arrs), out_shape=[jax.ShapeDtypeStruct(a.shape[1:], a.dtype) for a in arrs],
                scratch=[pltpu.SemaphoreType.DMA((n,)), pltpu.SemaphoreType.DMA((n,))], phases=phases)


def _pair_add(mine, theirs, core, *, name, tr=256):
    _, r, c = mine.shape
    tr = min(tr, r)
    assert r % tr == 0

    def body(core_ref, m_ref, t_ref, o_ref):
        o_ref[...] = (m_ref[0].astype(F32) + t_ref[...].astype(F32)).astype(BF16)

    return pl.pallas_call(
        body,
        grid_spec=pltpu.PrefetchScalarGridSpec(
            num_scalar_prefetch=1, grid=(r // tr,),
            in_specs=[pl.BlockSpec((1, tr, c), lambda i, core_ref: (core_ref[0], i, 0)),
                      pl.BlockSpec((tr, c), lambda i, core_ref: (i, 0))],
            out_specs=pl.BlockSpec((tr, c), lambda i, core_ref: (i, 0))),
        out_shape=jax.ShapeDtypeStruct((r, c), BF16),
        compiler_params=_params("parallel"), name=name)(core, mine, theirs)


def _pair_add_small(mines, theirs, core, *, name):
    n = len(mines)

    def body(core_ref, *refs):
        for m_ref, t_ref, o_ref in zip(refs[:n], refs[n:2 * n], refs[2 * n:]):
            o_ref[...] = (m_ref[0].astype(F32) + t_ref[...].astype(F32)).astype(BF16)

    whole = lambda a: pl.BlockSpec(a.shape, lambda i, core_ref: (0,) * a.ndim)
    return pl.pallas_call(
        body,
        grid_spec=pltpu.PrefetchScalarGridSpec(
            num_scalar_prefetch=1, grid=(1,),
            in_specs=[pl.BlockSpec((1,) + m.shape[1:], lambda i, core_ref: (core_ref[0], 0, 0)) for m in mines]
                     + [whole(t) for t in theirs],
            out_specs=[whole(t) for t in theirs]),
        out_shape=[jax.ShapeDtypeStruct(t.shape, BF16) for t in theirs],
        compiler_params=_params("arbitrary"), name=name)(core, *mines, *theirs)


def _chip_scatter_plan(arrs):
    n = len(arrs)

    def phases(ins, outs, send_sems, recv_sems, loc_sems):
        x, y, c, chips = _mesh_place()
        mine = 2 * x + y
        local = [pltpu.make_async_copy(ins[k].at[mine], outs[k].at[mine], loc_sems.at[k]) for k in range(n)]
        remote = [pltpu.make_async_remote_copy(
            src_ref=ins[k].at[2 * px + py], dst_ref=outs[k].at[mine], send_sem=send_sems.at[k, j],
            recv_sem=recv_sems.at[k, j], device_id=(px, py, c), device_id_type=pl.DeviceIdType.MESH)
            for j, (px, py) in enumerate(chips) for k in range(n)]

        def start():
            for cp in local + remote:
                cp.start()

        def finish():
            for cp in remote:
                cp.wait()
            for cp in local:
                cp.wait()

        return start, (lambda: None), finish

    return dict(
        args=list(arrs), out_shape=[jax.ShapeDtypeStruct(a.shape, a.dtype) for a in arrs],
        scratch=[pltpu.SemaphoreType.DMA((n, 3)), pltpu.SemaphoreType.DMA((n, 3)), pltpu.SemaphoreType.DMA((n,))],
        phases=phases)


def _ssm_discretize(a_re, a_im, log_dt, b_re, b_im):
    dt = jnp.exp(log_dt)[:, None]
    mag = jnp.exp(a_re * dt)
    lr = mag * jnp.cos(a_im * dt)
    li = mag * jnp.sin(a_im * dt)
    den = a_re * a_re + a_im * a_im
    cr = ((lr - 1.0) * a_re + li * a_im) / den
    ci = (li * a_re - (lr - 1.0) * a_im) / den
    bbr = cr[..., None] * b_re - ci[..., None] * b_im
    bbi = cr[..., None] * b_im + ci[..., None] * b_re
    return lr, li, bbr, bbi


def _ssm_dense(lr, li, bbr, bbi, c_re, c_im, *, name):
    import numpy as np
    scan_ops = _scan_tables(lr.reshape(-1), li.reshape(-1)) + _scan_consts(SEQ_BLOCK)
    per_half = SSM_GROUPS // SSM_HALVES
    bt = jnp.stack([b.transpose(0, 2, 1).reshape(SSM_W, SSM_STATE) for b in (bbr, bbi)])
    ct = jnp.stack([c.transpose(0, 2, 1).reshape(SSM_N, SSM_GROUP) for c in (c_re, c_im)])
    rep_p = jnp.asarray(np.tile(np.eye(SSM_STATE, dtype=np.float32), (1, per_half)), BF16)
    rep_c = jnp.asarray(np.tile(np.eye(SSM_GROUP, dtype=np.float32), (1, per_half)), BF16)

    def body(bt_ref, ct_ref, rp_ref, rc_ref, b_ref, c_ref):
        def on_diagonal(shape, rows, cols):
            r = lax.broadcasted_iota(jnp.int32, shape, 0) // rows
            c = lax.broadcasted_iota(jnp.int32, shape, 1) // cols
            return r == c

        mask_b = on_diagonal((SSM_HW, SSM_HN), SSM_GROUP, SSM_STATE)
        mask_c = on_diagonal((SSM_HN, SSM_HW), SSM_STATE, SSM_GROUP)
        for k in range(2):
            for h in range(SSM_HALVES):
                b_rows = bt_ref[k, h * SSM_HW:(h + 1) * SSM_HW, :]
                b_ref[k, h] = jnp.where(mask_b, _dot(b_rows, rp_ref[...]), 0.0).astype(BF16)
                c_rows = ct_ref[k, h * SSM_HN:(h + 1) * SSM_HN, :]
                c_ref[k, h] = jnp.where(mask_c, _dot(c_rows, rc_ref[...]), 0.0).astype(BF16)

    vmem = pl.BlockSpec(memory_space=pltpu.VMEM)
    bblk, cblk = pl.pallas_call(
        body, in_specs=[vmem] * 4, out_specs=[vmem] * 2,
        out_shape=[jax.ShapeDtypeStruct((2, SSM_HALVES, SSM_HW, SSM_HN), BF16),
                   jax.ShapeDtypeStruct((2, SSM_HALVES, SSM_HN, SSM_HW), BF16)],
        compiler_params=pltpu.CompilerParams(vmem_limit_bytes=VMEM_LIMIT), name=name)(bt, ct, rep_p, rep_c)
    return scan_ops, bblk, cblk


def _ssm_extract(db, dc, sums):
    db = db.reshape(2, SSM_GROUPS, SSM_GROUP, SSM_STATE).transpose(0, 1, 3, 2)
    dc = dc.reshape(2, SSM_GROUPS, SSM_GROUP, SSM_STATE)
    dlr = sums[0].reshape(SSM_GROUPS, SSM_STATE)
    dli = sums[1].reshape(SSM_GROUPS, SSM_STATE)
    return dlr, dli, db[0], db[1], dc[0], dc[1]


def _in_groups():
    names = ("q", "k", "v", "u_ssm", "u_pool", "z_att", "z_ssm", "z_pool", "gates")
    sizes = (ATT_W, KV_W, KV_W, SSM_W, POOL_W, ATT_W, SSM_W, POOL_W, 3 * D_MODEL)
    r, lo = {}, 0
    for nm, s in zip(names, sizes):
        r[nm] = (lo, lo + s)
        lo += s
    kv = (r["k"][0], r["v"][1])
    return ((r["q"], r["z_att"], kv), (r["u_ssm"], r["z_ssm"]), (r["u_pool"], r["z_pool"]), (r["gates"],))


IN_GROUPS = _in_groups()


def _layer_fwd(x, lw, li, late=None, comm_attn=None, comm_ssm=None):
    tag = f"l{li}"
    h, (pa, ps, pp, pg), arrived = _ln_proj(x, lw["norm_g"], lw["shift"], lw["scale"], lw["w_in"], IN_GROUPS,
                                            name=f"ln_proj_{tag}", comm=None if late is None else late[0])
    if late is not None:
        lw = {**lw, **late[1](arrived)}
    ya, from_attn = _attn_fwd(pa, lw["sinks"], name=f"attn_fwd_{tag}", comm=comm_attn)
    ys, chk, states, *from_ssm = _ssm_fwd(ps, lw["lam"], lw["bblk"], lw["cblk"], lw["ssm_d"], lw["w_glu"], lw["b_glu"],
                                          name=f"ssm_fwd_{tag}", comm=comm_ssm)
    yp = _pool_fwd(pp, lw["w_pool"], lw["pool_scale"], name=f"pool_fwd_{tag}")
    x_new, merged, ba, bs, bp, out = _merge_fwd(ya, ys, yp, lw["w_br_att"], lw["w_br_ssm"], lw["w_br_pool"], pg,
                                                lw["w_out"], x, lw["gate"], name=f"merge_fwd_{tag}")
    saved = dict(x=x, h=h, pa=pa, ps=ps, pp=pp, pg=pg, ya=ya, ys=ys, yp=yp, chk=chk, states=states, merged=merged,
                 ba=ba, bs=bs, bp=bp, out=out)
    return x_new, saved, lw, list(from_attn), list(from_ssm)


def _layer_bwd(dx, lw, sv, li, later=None, own=None):
    tag = f"l{li}"
    g = {}
    merge_out, swapped = _merge_bwd(
        dx, sv["out"], lw["gate"], lw["w_out"], sv["pg"], (sv["ba"], sv["bs"], sv["bp"]),
        (lw["w_br_att"], lw["w_br_ssm"], lw["w_br_pool"]), (sv["ya"], sv["ys"], sv["yp"]), sv["merged"],
        name=f"merge_bwd_{tag}", comm=None if later is None else later[0])
    dya, dys, dyp, dpg, gate_sums, g["w_br_att"], g["w_br_ssm"], g["w_br_pool"], g["w_out"] = merge_out
    dpa, dsink = _attn_bwd(sv["pa"], lw["sinks"], dya, name=f"attn_bwd_{tag}")
    dps, db_dense, dc_dense, dwglu, ssm_sums, *exchanged = _ssm_bwd(
        sv["ps"], dys, sv["chk"], sv["states"], lw["lam"], lw["bblk"], lw["cblk"], lw["ssm_d"], lw["w_glu"], lw["b_glu"],
        name=f"ssm_bwd_{tag}", comm=None if later is None else later[1](swapped))
    g["w_glu"] = dwglu.astype(BF16)
    dpp, dwpool, pool_sums = _pool_bwd(sv["pp"], dyp, lw["w_pool"], lw["pool_scale"], name=f"pool_bwd_{tag}")
    h = sv["h"]
    dproj = (dpa, dps, dpp, dpg)
    g["w_in"], from_late = _mm_tn_grouped(h, dproj, IN_GROUPS, name=f"dw_in_{tag}",
                                          comm=None if own is None else own({k: g[k] for k in LATE_WEIGHTS}))
    dx_in, ln_sums, from_w_in = _ln_proj_bwd(dproj, lw["w_in"], IN_GROUPS, sv["x"], dx, lw["norm_g"], lw["scale"],
                                             name=f"ln_proj_bwd_{tag}",
                                             comm=None if own is None else own({"w_in": g["w_in"]}))
    g["dmod"] = jnp.concatenate([ln_sums[0], ln_sums[1], gate_sums[0]])
    g["norm_g"] = ln_sums[2]
    g["attn_sinks"] = dsink[:, 0]
    g["ssm_raw"] = _ssm_extract(db_dense, dc_dense, ssm_sums)
    g["ssm_d"] = ssm_sums[2, :SSM_W]
    g["b_glu"] = ssm_sums[3, :SSM_W]
    g["w_pool"] = dwpool
    g["pool_scale"] = pool_sums[0]
    return dx_in, g, exchanged, list(from_w_in) + list(from_late)


BIG_WEIGHTS = ("w_in", "w_glu", "w_br_att", "w_br_ssm", "w_br_pool", "w_out")
ROW_SHARDED = ("w_glu", "w_out")


LATE_WEIGHTS = BIG_WEIGHTS[1:]


def _side_by_side(g, *, tm=256):
    n, r, c = g.shape

    def body(g_ref, o_ref):
        for s in range(n):
            o_ref[:, s * c:(s + 1) * c] = g_ref[s]

    return pl.pallas_call(
        body, grid=(r // tm,),
        in_specs=[pl.BlockSpec((n, tm, c), lambda i: (0, i, 0))],
        out_specs=pl.BlockSpec((tm, n * c), lambda i: (i, 0)),
        out_shape=jax.ShapeDtypeStruct((r, n * c), g.dtype),
        compiler_params=_params("parallel"), name="side_by_side")(g)


def _full_weights(keys, gathered):
    full = {}
    for k, g in zip(keys, gathered):
        if k in ROW_SHARDED:
            full[k] = g.reshape(N_DEV * g.shape[1], g.shape[2])
        elif g.shape[2] % 128:
            full[k] = _side_by_side(g)
        else:
            full[k] = g.transpose(1, 0, 2).reshape(g.shape[1], N_DEV * g.shape[2])
    return full


def _by_destination(keys, grads):
    out = []
    for k in keys:
        g = grads[k]
        if g.ndim == 4:
            out.append(g)
        elif k in ROW_SHARDED:
            out.append(g.reshape(4, 2, g.shape[0] // N_DEV, g.shape[1]).transpose(1, 0, 2, 3))
        else:
            out.append(g.reshape(g.shape[0], 4, 2, g.shape[1] // N_DEV).transpose(2, 1, 0, 3))
    return out


def _prepare_layer(li, mod, norm_g, w_in_full, attn_sinks, disc, ssm_c_re, ssm_c_im, ssm_d, b_glu, w_pool, pool_scale):
    d = D_MODEL
    lr, li_, bbr, bbi = disc
    lam, bblk, cblk = _ssm_dense(lr[li], li_[li], bbr[li], bbi[li], ssm_c_re[li], ssm_c_im[li], name=f"ssm_dense_l{li}")
    return dict(
        norm_g=norm_g[li][None, :], shift=mod[li, :d][None, :], scale=mod[li, d:2 * d][None, :],
        gate=mod[li, 2 * d:][None, :], w_in=w_in_full,
        sinks=attn_sinks[li], lam=lam, bblk=bblk, cblk=cblk, ssm_d=ssm_d[li][None, :],
        b_glu=b_glu[li][None, :], w_pool=w_pool[li].astype(BF16), pool_scale=pool_scale[li][None, :])


SMALL_ROWS = 64
SMALL_ORDER = ("norm_g", "attn_sinks", "ssm_d", "b_glu", "w_pool", "pool_scale", "dmod")


def _pack_small(loss, dfinal_g, layer_grads):
    parts = [jnp.broadcast_to(loss.reshape(1), (128,)), dfinal_g]
    for g in layer_grads:
        for k in SMALL_ORDER:
            v = g[k].reshape(-1)
            if v.shape[0] % 128:
                v = jnp.pad(v, (0, 128 - v.shape[0] % 128))
            parts.append(v)
        for v in g["ssm_raw"]:
            parts.append(v.reshape(-1))
    flat = jnp.concatenate(parts)
    return jnp.pad(flat, (0, (-flat.shape[0]) % (SMALL_ROWS * 128))).reshape(-1, 128)


def _unpack_small(flat, shapes):
    out, off = [], 0
    for s in shapes:
        n = int(math.prod(s))
        out.append(flat[off:off + n].reshape(s))
        off += n + (-n) % 128
    return out


def kernel(x, c, norm_g, w_ada, b_ada, w_in, attn_sinks, ssm_a_re, ssm_a_im, ssm_log_dt, ssm_b_re, ssm_b_im, ssm_c_re, ssm_c_im, ssm_d, w_glu, b_glu, w_pool, pool_scale, w_br_att, w_br_ssm, w_br_pool, w_out, final_g, loss_target, m_norm_g, m_w_ada, m_b_ada, m_w_in, m_attn_sinks, m_ssm_a_re, m_ssm_a_im, m_ssm_log_dt, m_ssm_b_re, m_ssm_b_im, m_ssm_c_re, m_ssm_c_im, m_ssm_d, m_w_glu, m_b_glu, m_w_pool, m_pool_scale, m_w_br_att, m_w_br_ssm, m_w_br_pool, m_w_out, m_final_g, v_norm_g, v_w_ada, v_b_ada, v_w_in, v_attn_sinks, v_ssm_a_re, v_ssm_a_im, v_ssm_log_dt, v_ssm_b_re, v_ssm_b_im, v_ssm_c_re, v_ssm_c_im, v_ssm_d, v_w_glu, v_b_glu, v_w_pool, v_pool_scale, v_w_br_att, v_w_br_ssm, v_w_br_pool, v_w_out, v_final_g):
    me = 4 * lax.axis_index("x") + 2 * lax.axis_index("y") + lax.axis_index("c")
    d = D_MODEL
    ada_w = 3 * d // N_DEV

    sharded = dict(w_in=w_in, w_glu=w_glu, w_br_att=w_br_att, w_br_ssm=w_br_ssm, w_br_pool=w_br_pool, w_out=w_out)
    shards = lambda li, keys: [sharded[k][li].astype(BF16) for k in keys]

    b_cols = lax.dynamic_slice(b_ada, (0, me * ada_w), (DEPTH, ada_w))
    c_act, mod_all, w_in0 = _ada_modulation(c, w_ada, b_cols, _gather_plan(shards(0, ("w_in",))))
    mod_mine = lax.dynamic_index_in_dim(mod_all, me, axis=2, keepdims=False)
    mod_mine = mod_mine.transpose(1, 0, 2).reshape(DEPTH, 3 * d)

    disc, disc_vjp = jax.vjp(jax.vmap(_ssm_discretize), ssm_a_re, ssm_a_im, ssm_log_dt, ssm_b_re, ssm_b_im)
    layer = lambda li, gathered_w_in: _prepare_layer(
        li, mod_mine, norm_g, _full_weights(("w_in",), gathered_w_in)["w_in"], attn_sinks, disc, ssm_c_re, ssm_c_im,
        ssm_d, b_glu, w_pool, pool_scale)
    late_weights = lambda gathered: _full_weights(LATE_WEIGHTS, gathered)
    core = lax.axis_index("c").astype(jnp.int32).reshape(1)

    def add_pairs(keys, by_dest, from_sibling, tag):
        flat = {k: (a.reshape(2, -1, a.shape[-1]), b.reshape(-1, b.shape[-1]))
                for k, a, b in zip(keys, by_dest, from_sibling)}
        small = [k for k in keys if k != "w_in"]
        sums = {}
        if "w_in" in flat:
            sums["w_in"] = _pair_add(*flat["w_in"], core, name=f"grads_pair_add_{tag}_w_in")
        if small:
            added = _pair_add_small([flat[k][0] for k in small], [flat[k][1] for k in small], core,
                                    name=f"grads_pair_add_{tag}_late")
            sums.update(zip(small, added))
        return [sums[k].reshape(b.shape) for k, b in zip(keys, from_sibling)]

    def chip_sums_of(keys, grads_li, tag):
        by_dest = _by_destination(keys, grads_li)
        return add_pairs(keys, by_dest, _run_plan(_sibling_swap_plan(by_dest), name=f"grads_sibling_swap_{tag}"), tag)

    layers, saved, grads = [None] * DEPTH, [None] * DEPTH, [None] * DEPTH
    layers[0] = layer(0, w_in0)
    xs, saved[0], layers[0], late1, w_in1 = _layer_fwd(
        x[0], layers[0], 0, late=(_gather_plan(shards(0, LATE_WEIGHTS)), late_weights),
        comm_attn=_gather_plan(shards(1, LATE_WEIGHTS)), comm_ssm=_gather_plan(shards(1, ("w_in",))))
    layers[1] = {**layer(1, w_in1), **late_weights(late1)}
    xs, saved[1], _, _, _ = _layer_fwd(xs, layers[1], 1)
    dx, fin_sums = _final_loss(xs, final_g[None, :], loss_target[0])
    loss_part = jnp.sum(fin_sums[1])
    dx, grads[1], _, _ = _layer_bwd(dx, layers[1], saved[1], 1)
    by_dest1 = _by_destination(BIG_WEIGHTS, grads[1])
    dx, grads[0], scattered1, scattered0 = _layer_bwd(
        dx, layers[0], saved[0], 0,
        later=(_sibling_swap_plan(by_dest1),
               lambda swapped: _chip_scatter_plan(add_pairs(BIG_WEIGHTS, by_dest1, swapped, "l1"))),
        own=lambda g: _chip_scatter_plan(chip_sums_of(tuple(g), g, "l0_" + "_".join(g))))
    big = list(zip(scattered0, scattered1))
    grad_x = dx[None]

    small = _pack_small(loss_part, fin_sums[0], grads)
    dmod_rows = jnp.concatenate([grads[li]["dmod"] for li in range(DEPTH)]).reshape(-1, 128)
    small_sum, dmod_gathered = _allreduce_small(small, dmod_rows, name="allreduce_small")
    out = {}

    def adam(name, w, g_slabs, m, v):
        shp = w.shape
        r = int(math.prod(shp[:-1])) if len(shp) > 1 else 1
        w2, m2, v2 = (a.reshape(r, shp[-1]) for a in (w, m, v))
        gs = [g.reshape(g.shape[0], r // len(g_slabs), shp[-1]) for g in g_slabs]
        res = _adamw(w2, gs, m2, v2, name=f"adamw_{name}", tr=256 if shp[-1] >= 128 else 2048)
        out[name] = tuple(a.reshape(shp) for a in res)

    flat = small_sum.reshape(-1)
    shapes = [(128,), (d,)]
    for _ in range(DEPTH):
        shapes += [(d,), (N_HEADS,), (SSM_W,), (SSM_W,), (4, POOL_GW, POOL_GW), (POOL_W,), (3 * d,),
                   (SSM_GROUPS, SSM_STATE), (SSM_GROUPS, SSM_STATE), (SSM_GROUPS, SSM_STATE, SSM_GROUP),
                   (SSM_GROUPS, SSM_STATE, SSM_GROUP), (SSM_GROUPS, SSM_GROUP, SSM_STATE), (SSM_GROUPS, SSM_GROUP, SSM_STATE)]
    un = _unpack_small(flat, shapes)
    loss = un[0][0]
    g_final_g = un[1]
    per = 13
    gl = [un[2 + li * per: 2 + (li + 1) * per] for li in range(DEPTH)]
    st = lambda j: jnp.stack([gl[li][j] for li in range(DEPTH)])
    g_norm_g, g_sinks, g_ssm_d, g_b_glu, g_w_pool, g_pool_scale, g_b_ada = (st(j) for j in range(7))
    d_lr, d_li, d_bbr, d_bbi, g_c_re, g_c_im = (st(j) for j in range(7, 13))
    g_a_re, g_a_im, g_log_dt, g_b_re, g_b_im = disc_vjp((d_lr, d_li, d_bbr, d_bbi))

    dmod_all = lax.dynamic_slice(dmod_gathered.reshape(N_DEV, DEPTH, 3 * d), (0, 0, me * ada_w), (N_DEV, DEPTH, ada_w))
    dmod_all = dmod_all.transpose(1, 0, 2)
    g_w_ada = jnp.stack([_mm_tn(c_act, dmod_all[li], tm=d, tn=ada_w, tk=N_DEV, name=f"dw_ada_l{li}") for li in range(DEPTH)])

    adam("w_ada", w_ada, [g_w_ada[None]], m_w_ada, v_w_ada)
    adam("w_in", w_in, big[0], m_w_in, v_w_in)
    adam("w_glu", w_glu, big[1], m_w_glu, v_w_glu)
    adam("w_br_att", w_br_att, big[2], m_w_br_att, v_w_br_att)
    adam("w_br_ssm", w_br_ssm, big[3], m_w_br_ssm, v_w_br_ssm)
    adam("w_br_pool", w_br_pool, big[4], m_w_br_pool, v_w_br_pool)
    adam("w_out", w_out, big[5], m_w_out, v_w_out)

    small_names = ["norm_g", "b_ada", "attn_sinks", "ssm_a_re", "ssm_a_im", "ssm_log_dt", "ssm_b_re", "ssm_b_im",
                   "ssm_c_re", "ssm_c_im", "ssm_d", "b_glu", "w_pool", "pool_scale", "final_g"]
    small_w = [norm_g, b_ada, attn_sinks, ssm_a_re, ssm_a_im, ssm_log_dt, ssm_b_re, ssm_b_im, ssm_c_re, ssm_c_im,
               ssm_d, b_glu, w_pool, pool_scale, final_g]
    small_m = [m_norm_g, m_b_ada, m_attn_sinks, m_ssm_a_re, m_ssm_a_im, m_ssm_log_dt, m_ssm_b_re, m_ssm_b_im,
               m_ssm_c_re, m_ssm_c_im, m_ssm_d, m_b_glu, m_w_pool, m_pool_scale, m_final_g]
    small_v = [v_norm_g, v_b_ada, v_attn_sinks, v_ssm_a_re, v_ssm_a_im, v_ssm_log_dt, v_ssm_b_re, v_ssm_b_im,
               v_ssm_c_re, v_ssm_c_im, v_ssm_d, v_b_glu, v_w_pool, v_pool_scale, v_final_g]
    small_g = [g_norm_g, g_b_ada, g_sinks, g_a_re, g_a_im, g_log_dt, g_b_re, g_b_im, g_c_re, g_c_im,
               g_ssm_d, g_b_glu, g_w_pool, g_pool_scale, g_final_g]

    for nm, w, g, m, v in zip(small_names, small_w, small_g, small_m, small_v):
        adam(nm, w, [g[None]], m, v)

    order = ["norm_g", "w_ada", "b_ada", "w_in", "attn_sinks", "ssm_a_re", "ssm_a_im", "ssm_log_dt", "ssm_b_re",
             "ssm_b_im", "ssm_c_re", "ssm_c_im", "ssm_d", "w_glu", "b_glu", "w_pool", "pool_scale", "w_br_att",
             "w_br_ssm", "w_br_pool", "w_out", "final_g"]
    return (loss, grad_x, *[out[k][0] for k in order], *[out[k][1] for k in order],
            *[out[k][2] for k in order], *[out[k][3] for k in order])
```

```python
import functools
import math

import jax
import jax.numpy as jnp
from jax import lax
from jax.experimental import pallas as pl
from jax.experimental.pallas import tpu as pltpu

F32 = jnp.float32
BF16 = jnp.bfloat16

N_DEV = 8
D_MODEL = 1024
DEPTH = 2
CHUNK = 64
N_HEADS = 8
N_KV_HEADS = 2
HEAD_DIM = 64
Q_PER_KV = N_HEADS // N_KV_HEADS
WINDOW = 128
ATT_W = 512
KV_W = 128
SSM_W = 512
SSM_GROUP = 16
SSM_GROUPS = 32
SSM_STATE = 64
SSM_N = SSM_GROUPS * SSM_STATE
POOL_W = 512
POOL_WINDOWS = (2, 4, 8, 16)
POOL_GW = 128
POOL_HALO = 16
EPS = 1e-6
NEG_INF = -1e30
ADAM_LR = 0.001
ADAM_B1 = 0.9
ADAM_B2 = 0.999
ADAM_EPS = 1e-08
ADAM_WD = 0.01
ADAM_STEP = 10

SEQ_BLOCK = 256
POOL_BLOCK = 512
ATT_BLOCK = 128
VMEM_LIMIT = 56 * 1024 * 1024

NN = (((1,), (0,)), ((), ()))
NT = (((1,), (1,)), ((), ()))
TN = (((0,), (0,)), ((), ()))


def _dot(a, b, dims=NN):
    return lax.dot_general(a.astype(BF16), b.astype(BF16), dims, preferred_element_type=F32)


def _params(*sem):
    return pltpu.CompilerParams(dimension_semantics=sem, vmem_limit_bytes=VMEM_LIMIT)


def _sigmoid(x):
    return 0.5 + 0.5 * jnp.tanh(0.5 * x)


def _silu_and_grad(z):
    s = _sigmoid(z)
    return z * s, s * (1.0 + z * (1.0 - s))


_GELU_K = math.sqrt(2.0 / math.pi)


def _gelu_and_grad(x):
    inner = _GELU_K * (x + 0.044715 * x * x * x)
    t = jnp.tanh(inner)
    val = 0.5 * x * (1.0 + t)
    grad = 0.5 * (1.0 + t) + 0.5 * x * (1.0 - t * t) * _GELU_K * (1.0 + 3.0 * 0.044715 * x * x)
    return val, grad


def _grouped_pieces(groups):
    out = []
    for ranges in groups:
        off, pieces = 0, []
        for lo, hi in ranges:
            pieces.append((off, lo, hi))
            off += hi - lo
        out.append(pieces)
    return out


def _mm_tn(a, b, *, out_dtype=F32, tm=1024, tn=1024, tk=1024, name, comm=None):
    k, m = a.shape
    n = b.shape[1]
    assert m % min(tm, m) == 0 and n % min(tn, n) == 0 and k % min(tk, k) == 0
    tm, tn, tk = min(tm, m), min(tn, n), min(tk, k)
    nk = k // tk
    grid = (m // tm, n // tn, nk)
    c_args, c_in, c_out, c_shape, c_scratch = _comm_extra(comm)

    def body(a_ref, b_ref, o_ref, acc_ref):
        kk = pl.program_id(2)

        @pl.when(kk == 0)
        def _():
            acc_ref[...] = jnp.zeros_like(acc_ref)

        acc_ref[...] += _dot(a_ref[...], b_ref[...], TN)

        @pl.when(kk == nk - 1)
        def _():
            o_ref[...] = acc_ref[...].astype(out_dtype)

    res = pl.pallas_call(
        _with_comm(body, comm, 2, 1, grid, -1), grid=grid,
        in_specs=[pl.BlockSpec((tk, tm), lambda i, j, kk: (kk, i)), pl.BlockSpec((tk, tn), lambda i, j, kk: (kk, j))] + c_in,
        out_specs=[pl.BlockSpec((tm, tn), lambda i, j, kk: (i, j))] + c_out,
        out_shape=[jax.ShapeDtypeStruct((m, n), out_dtype)] + c_shape,
        scratch_shapes=[pltpu.VMEM((tm, tn), F32)] + c_scratch,
        compiler_params=_params(*(("arbitrary",) * 3 if comm else ("parallel", "parallel", "arbitrary"))),
        name=name)(a, b, *c_args)
    return (res[0], list(res[1:])) if comm else res[0]


def _mm_tn_grouped(a, bs, groups, *, tm=512, tk=512, name, comm=None):
    k, m = a.shape
    tm, tk = min(tm, m), min(tk, k)
    assert m % tm == 0 and k % tk == 0
    nk, nb = k // tk, len(bs)
    n = sum(b.shape[1] for b in bs)
    ns = n // N_DEV
    pieces = []
    for plist in _grouped_pieces(groups):
        sub = []
        for off, lo, hi in plist:
            pos = lo
            while pos < hi:
                s = pos // ns
                end = min(hi, (s + 1) * ns)
                sub.append((s, pos - s * ns, end - s * ns, off + pos - lo))
                pos = end
        pieces.append(sub)
    grid = (m // tm, nk)
    c_args, c_in, c_out, c_shape, c_scratch = _comm_extra(comm)

    def body(a_ref, *refs):
        b_refs, o_ref, acc_refs = refs[:nb], refs[nb], refs[nb + 1:]
        kk = pl.program_id(1)
        av = a_ref[...]
        for b_ref, acc_ref, plist in zip(b_refs, acc_refs, pieces):
            @pl.when(kk == 0)
            def _():
                acc_ref[...] = jnp.zeros_like(acc_ref)

            acc_ref[...] += _dot(av, b_ref[...], TN)

            @pl.when(kk == nk - 1)
            def _():
                for s, c0, c1, off in plist:
                    o_ref[s % 2, s // 2, :, c0:c1] = acc_ref[:, off:off + c1 - c0].astype(BF16)

    res = pl.pallas_call(
        _with_comm(body, comm, 1 + nb, 1, grid, -1), grid=grid,
        in_specs=[pl.BlockSpec((tk, tm), lambda i, kk: (kk, i))]
                 + [pl.BlockSpec((tk, b.shape[1]), lambda i, kk: (kk, 0)) for b in bs] + c_in,
        out_specs=[pl.BlockSpec((2, N_DEV // 2, tm, ns), lambda i, kk: (0, 0, i, 0))] + c_out,
        out_shape=[jax.ShapeDtypeStruct((2, N_DEV // 2, m, ns), BF16)] + c_shape,
        scratch_shapes=[pltpu.VMEM((tm, b.shape[1]), F32) for b in bs] + c_scratch,
        compiler_params=_params("arbitrary", "arbitrary"), name=name)(a, *bs, *c_args)
    return res[0], list(res[1:])


def _ln_proj(x, g, shift, scale, w, groups, *, name, tm=512, comm=None):
    l, d = x.shape
    tm = min(tm, l)
    nb = l // tm
    pieces = _grouped_pieces(groups)
    widths = [sum(hi - lo for _, lo, hi in plist) for plist in pieces]
    nw = len(pieces)
    c_args, c_in, c_out, c_shape, c_scratch = _comm_extra(comm)

    def body(x_ref, g_ref, sh_ref, sc_ref, w_ref, h_ref, *p_refs):
        xv = x_ref[...]
        n = xv * lax.rsqrt(jnp.mean(xv * xv, axis=-1, keepdims=True) + EPS)
        h = ((n * g_ref[...]) * (1.0 + sc_ref[...]) + sh_ref[...]).astype(BF16)
        h_ref[...] = h
        for p_ref, plist in zip(p_refs, pieces):
            for off, lo, hi in plist:
                p_ref[:, off:off + hi - lo] = _dot(h, w_ref[:, lo:hi]).astype(BF16)

    vec = pl.BlockSpec((1, d), lambda i: (0, 0))
    row = lambda n: pl.BlockSpec((tm, n), lambda i: (i, 0))
    res = pl.pallas_call(
        _with_comm(body, comm, 5, 1 + nw, nb, -1), grid=(nb,),
        in_specs=[row(d), vec, vec, vec, pl.BlockSpec(w.shape, lambda i: (0, 0))] + c_in,
        out_specs=[row(d)] + [row(n) for n in widths] + c_out,
        out_shape=[jax.ShapeDtypeStruct((l, d), BF16)] + [jax.ShapeDtypeStruct((l, n), BF16) for n in widths] + c_shape,
        scratch_shapes=c_scratch,
        compiler_params=_params("arbitrary"), name=name)(x, g, shift, scale, w, *c_args)
    return res[0], list(res[1:1 + nw]), list(res[1 + nw:])


def _ln_proj_bwd(ds, w, groups, x, dres, g, scale, *, name, tm=256, comm=None):
    l, d = x.shape
    tm = min(tm, l)
    nb = l // tm
    nd = len(ds)
    pieces = _grouped_pieces(groups)
    c_args, c_in, c_out, c_shape, c_scratch = _comm_extra(comm)

    def body(*refs):
        d_refs = refs[:nd]
        w_ref, x_ref, dres_ref, g_ref, sc_ref, dx_ref, sums_ref = refs[nd:]
        dhv = None
        for d_ref, plist in zip(d_refs, pieces):
            for off, lo, hi in plist:
                term = _dot(d_ref[:, off:off + hi - lo], w_ref[:, lo:hi], NT)
                dhv = term if dhv is None else dhv + term
        xv = x_ref[...]
        rstd = lax.rsqrt(jnp.mean(xv * xv, axis=-1, keepdims=True) + EPS)
        n = xv * rstd
        gv = g_ref[...]
        dr = dhv * (1.0 + sc_ref[...])
        dn = dr * gv
        dx_ref[...] = dres_ref[...] + rstd * (dn - n * jnp.mean(dn * n, axis=-1, keepdims=True))

        @pl.when(pl.program_id(0) == 0)
        def _():
            sums_ref[...] = jnp.zeros_like(sums_ref)

        sums_ref[0:1, :] += jnp.sum(dhv, axis=0, keepdims=True)
        sums_ref[1:2, :] += jnp.sum(dhv * (n * gv), axis=0, keepdims=True)
        sums_ref[2:3, :] += jnp.sum(dr * n, axis=0, keepdims=True)

    vec = pl.BlockSpec((1, d), lambda i: (0, 0))
    row = pl.BlockSpec((tm, d), lambda i: (i, 0))
    res = pl.pallas_call(
        _with_comm(body, comm, nd + 5, 2, nb, -1), grid=(nb,),
        in_specs=[pl.BlockSpec((tm, a.shape[1]), lambda i: (i, 0)) for a in ds]
                 + [pl.BlockSpec(w.shape, lambda i: (0, 0)), row, row, vec, vec] + c_in,
        out_specs=[row, pl.BlockSpec((8, d), lambda i: (0, 0))] + c_out,
        out_shape=[jax.ShapeDtypeStruct((l, d), F32), jax.ShapeDtypeStruct((8, d), F32)] + c_shape,
        scratch_shapes=c_scratch,
        compiler_params=_params("arbitrary"), name=name)(*ds, w, x, dres, g, scale, *c_args)
    return res[0], res[1], list(res[2:])


def _final_loss(x, g, target, *, tm=512):
    l, d = x.shape

    def body(x_ref, g_ref, t_ref, dx_ref, sums_ref):
        xv = x_ref[...]
        rstd = lax.rsqrt(jnp.mean(xv * xv, axis=-1, keepdims=True) + EPS)
        n = xv * rstd
        gv = g_ref[...]
        err = n * gv - t_ref[...]
        dy = err * (1.0 / d)
        dn = dy * gv
        dx_ref[...] = rstd * (dn - n * jnp.mean(dn * n, axis=-1, keepdims=True))

        @pl.when(pl.program_id(0) == 0)
        def _():
            sums_ref[...] = jnp.zeros_like(sums_ref)

        sums_ref[0:1, :] += jnp.sum(dy * n, axis=0, keepdims=True)
        sums_ref[1:2, :] += jnp.sum(err * err, axis=0, keepdims=True) * (0.5 / d)

    vec = pl.BlockSpec((1, d), lambda i: (0, 0))
    row = pl.BlockSpec((tm, d), lambda i: (i, 0))
    dx, sums = pl.pallas_call(
        body, grid=(l // tm,),
        in_specs=[row, vec, row],
        out_specs=[row, pl.BlockSpec((8, d), lambda i: (0, 0))],
        out_shape=[jax.ShapeDtypeStruct((l, d), F32), jax.ShapeDtypeStruct((8, d), F32)],
        compiler_params=_params("arbitrary"), name="final_loss")(x, g, target)
    return dx, sums


def _attn_geometry(i, t):
    nk = t + WINDOW
    qi = lax.broadcasted_iota(jnp.int32, (t, nk), 0)
    kj = lax.broadcasted_iota(jnp.int32, (t, nk), 1)
    dist = jnp.abs(qi + WINDOW - kj).astype(F32)
    qc = jnp.right_shift(qi, 6)
    kc = jnp.right_shift(kj, 6)
    valid = (kc >= qc) & (kc <= qc + WINDOW // CHUNK) & ((i > 0) | (kj >= WINDOW))
    return dist, valid


def _attn_head(q, k_all, v_all, sink, slope, dist, valid):
    s = _dot(q, k_all, NT) * (1.0 / math.sqrt(HEAD_DIM)) - slope * dist
    s = jnp.where(valid, s, NEG_INF)
    m = jnp.maximum(jnp.max(s, axis=-1, keepdims=True), sink)
    e = jnp.exp(s - m)
    es = jnp.exp(sink - m)
    inv = 1.0 / (jnp.sum(e, axis=-1, keepdims=True) + es)
    p = e * inv
    o = _dot(p, v_all, NN)
    return p, o, es * inv


def _attn_specs(t):
    cur = pl.BlockSpec((t, ATT_W * 2 + KV_W * 2), lambda i: (i, 0))
    halo_blocks = t // WINDOW
    prev = pl.BlockSpec((WINDOW, 2 * KV_W), lambda i: (jnp.maximum(i * halo_blocks - 1, 0), (2 * ATT_W) // (2 * KV_W)))
    return cur, prev


def _attn_fwd(pa, sinks, *, name, t=ATT_BLOCK, comm=None):
    l = pa.shape[0]
    t = min(t, l)
    nb = l // t
    c_args, c_in, c_out, c_shape, c_scratch = _comm_extra(comm)

    def body(sink_ref, cur_ref, prev_ref, ya_ref):
        i = pl.program_id(0)
        dist, valid = _attn_geometry(i, t)
        for h in range(N_HEADS):
            kh = h // Q_PER_KV
            q = cur_ref[:, h * HEAD_DIM:(h + 1) * HEAD_DIM]
            z = cur_ref[:, ATT_W + h * HEAD_DIM:ATT_W + (h + 1) * HEAD_DIM].astype(F32)
            k_all = jnp.concatenate([prev_ref[:, kh * HEAD_DIM:(kh + 1) * HEAD_DIM],
                                     cur_ref[:, 2 * ATT_W + kh * HEAD_DIM:2 * ATT_W + (kh + 1) * HEAD_DIM]], axis=0)
            v_all = jnp.concatenate([prev_ref[:, KV_W + kh * HEAD_DIM:KV_W + (kh + 1) * HEAD_DIM],
                                     cur_ref[:, 2 * ATT_W + KV_W + kh * HEAD_DIM:2 * ATT_W + KV_W + (kh + 1) * HEAD_DIM]], axis=0)
            _, o, _ = _attn_head(q, k_all, v_all, sink_ref[h], 2.0 ** (-(h + 1)), dist, valid)
            sz, _ = _silu_and_grad(z)
            ya_ref[:, h * HEAD_DIM:(h + 1) * HEAD_DIM] = (o * sz).astype(BF16)

    cur, prev = _attn_specs(t)
    res = pl.pallas_call(
        _with_comm(body, comm, 3, 1, nb, nb - 1), grid=(nb,),
        in_specs=[pl.BlockSpec(memory_space=pltpu.SMEM), cur, prev] + c_in,
        out_specs=[pl.BlockSpec((t, ATT_W), lambda i: (i, 0))] + c_out,
        out_shape=[jax.ShapeDtypeStruct((l, ATT_W), BF16)] + c_shape,
        scratch_shapes=c_scratch,
        compiler_params=_params("arbitrary"), name=name)(sinks, pa, pa, *c_args)
    return res[0], res[1:]


def _attn_bwd(pa, sinks, dya, *, name, t=SEQ_BLOCK):
    l = pa.shape[0]
    t = min(t, l)
    nb = l // t
    scale = 1.0 / math.sqrt(HEAD_DIM)

    def body(sink_ref, cur_ref, prev_ref, dya_ref, dpa_ref, dsink_ref, carry_ref):
        n = pl.program_id(0)
        i = nb - 1 - n
        dist, valid = _attn_geometry(i, t)

        @pl.when(n == 0)
        def _():
            carry_ref[...] = jnp.zeros_like(carry_ref)
            dsink_ref[...] = jnp.zeros_like(dsink_ref)

        dk_acc = [jnp.zeros((HEAD_DIM, t + WINDOW), F32) for _ in range(N_KV_HEADS)]
        dv_acc = [jnp.zeros((HEAD_DIM, t + WINDOW), F32) for _ in range(N_KV_HEADS)]
        for h in range(N_HEADS):
            kh = h // Q_PER_KV
            q = cur_ref[:, h * HEAD_DIM:(h + 1) * HEAD_DIM]
            z = cur_ref[:, ATT_W + h * HEAD_DIM:ATT_W + (h + 1) * HEAD_DIM].astype(F32)
            k_all = jnp.concatenate([prev_ref[:, kh * HEAD_DIM:(kh + 1) * HEAD_DIM],
                                     cur_ref[:, 2 * ATT_W + kh * HEAD_DIM:2 * ATT_W + (kh + 1) * HEAD_DIM]], axis=0)
            v_all = jnp.concatenate([prev_ref[:, KV_W + kh * HEAD_DIM:KV_W + (kh + 1) * HEAD_DIM],
                                     cur_ref[:, 2 * ATT_W + KV_W + kh * HEAD_DIM:2 * ATT_W + KV_W + (kh + 1) * HEAD_DIM]], axis=0)
            p, o, p_sink = _attn_head(q, k_all, v_all, sink_ref[h], 2.0 ** (-(h + 1)), dist, valid)
            dy = dya_ref[:, h * HEAD_DIM:(h + 1) * HEAD_DIM]
            sz, dsz = _silu_and_grad(z)
            do = dy * sz
            dpa_ref[:, ATT_W + h * HEAD_DIM:ATT_W + (h + 1) * HEAD_DIM] = (dy * o * dsz).astype(BF16)
            dp = _dot(do, v_all, NT)
            delta = jnp.sum(p * dp, axis=-1, keepdims=True)
            ds = p * (dp - delta)
            dpa_ref[:, h * HEAD_DIM:(h + 1) * HEAD_DIM] = (_dot(ds, k_all, NN) * scale).astype(BF16)
            dk_acc[kh] = dk_acc[kh] + _dot(q, ds, TN) * scale
            dv_acc[kh] = dv_acc[kh] + _dot(do, p, TN)
            dsink_ref[h:h + 1, :] += jnp.broadcast_to(-jnp.sum(p_sink * delta, axis=0, keepdims=True), (1, 128))

        acc = jnp.concatenate(dk_acc + dv_acc, axis=0).T
        own = acc[WINDOW:, :]
        tail = own[t - WINDOW:, :] + carry_ref[...]
        if t > WINDOW:
            dpa_ref[0:t - WINDOW, 2 * ATT_W:] = own[:t - WINDOW, :].astype(BF16)
        dpa_ref[t - WINDOW:t, 2 * ATT_W:] = tail.astype(BF16)
        carry_ref[...] = acc[:WINDOW, :]

    halo_blocks = t // WINDOW
    wpa = 2 * ATT_W + 2 * KV_W
    cur = pl.BlockSpec((t, wpa), lambda n: (nb - 1 - n, 0))
    prev = pl.BlockSpec((WINDOW, 2 * KV_W),
                        lambda n: (jnp.maximum((nb - 1 - n) * halo_blocks - 1, 0), (2 * ATT_W) // (2 * KV_W)))
    return pl.pallas_call(
        body, grid=(nb,),
        in_specs=[pl.BlockSpec(memory_space=pltpu.SMEM), cur, prev, pl.BlockSpec((t, ATT_W), lambda n: (nb - 1 - n, 0))],
        out_specs=[pl.BlockSpec((t, wpa), lambda n: (nb - 1 - n, 0)), pl.BlockSpec((8, 128), lambda n: (0, 0))],
        out_shape=[jax.ShapeDtypeStruct((l, wpa), BF16), jax.ShapeDtypeStruct((8, 128), F32)],
        scratch_shapes=[pltpu.VMEM((WINDOW, 2 * KV_W), F32)],
        compiler_params=_params("arbitrary"), name=name)(sinks, pa, pa, dya)


def _scan(xr, xi, lr, li, t, reverse):
    row = lax.broadcasted_iota(jnp.int32, (t, 1), 0)
    d = 1
    pr, pi = lr, li
    while d < t:
        if reverse:
            sr = jnp.where(row < t - d, pltpu.roll(xr, t - d, 0), 0.0)
            si = jnp.where(row < t - d, pltpu.roll(xi, t - d, 0), 0.0)
        else:
            sr = jnp.where(row >= d, pltpu.roll(xr, d, 0), 0.0)
            si = jnp.where(row >= d, pltpu.roll(xi, d, 0), 0.0)
        xr, xi = xr + pr * sr - pi * si, xi + pr * si + pi * sr
        pr, pi = pr * pr - pi * pi, 2.0 * pr * pi
        d *= 2
    return xr, xi


SCAN_SUB = 8


def _split_hi_lo(a):
    hi = a.astype(BF16)
    lo = (a - hi.astype(F32)).astype(BF16)
    return jnp.concatenate([hi, lo], axis=0)


def _scan_mxu(xr, xi, tab, lam3, lam8, tri, expand, cr, ci, t, reverse):
    ns = t // SCAN_SUB
    n = xr.shape[1]
    v3 = lambda a: a.reshape(ns, SCAN_SUB, n)
    x3r, x3i = v3(xr), v3(xi)
    br = (x3r * tab[0] - x3i * tab[1]).reshape(t, n)
    bi = (x3r * tab[1] + x3i * tab[0]).reshape(t, n)
    pm = jnp.dot(tri, jnp.concatenate([br, bi], axis=1).astype(BF16), preferred_element_type=F32)
    p3r, p3i = v3(pm[:t, :n]), v3(pm[:t, n:])
    slr = p3r * tab[2] - p3i * tab[3]
    sli = p3r * tab[3] + p3i * tab[2]
    totr, toti = pm[t:, :n], pm[t:, n:]
    l3r, l3i = lam3
    l8r, l8i = lam8
    row = lax.broadcasted_iota(jnp.int32, (ns, 1), 0)
    edge = row == (ns - 1 if reverse else 0)
    er = totr * l3r - toti * l3i + jnp.where(edge, l8r * cr - l8i * ci, 0.0)
    ei = totr * l3i + toti * l3r + jnp.where(edge, l8r * ci + l8i * cr, 0.0)
    er, ei = _scan(er, ei, l8r, l8i, ns, reverse)
    shift = ns - 1 if reverse else 1
    nbr = jnp.where(edge, cr, pltpu.roll(er, shift, 0))
    nbi = jnp.where(edge, ci, pltpu.roll(ei, shift, 0))
    ex = jnp.dot(expand, _split_hi_lo(jnp.concatenate([nbr, nbi], axis=1)), preferred_element_type=F32)
    e3r, e3i = v3(ex[:, :n]), v3(ex[:, n:])
    sr = (slr + e3r * tab[4] - e3i * tab[5]).reshape(t, n)
    si = (sli + e3r * tab[5] + e3i * tab[4]).reshape(t, n)
    out = 0 if reverse else ns - 1
    return sr, si, er[out:out + 1, :], ei[out:out + 1, :]


def _scan_consts(t):
    import numpy as np
    ns = t // SCAN_SUB
    r = np.arange(t)
    same = (r[:, None] // SCAN_SUB) == (r[None, :] // SCAN_SUB)
    sums = (np.arange(ns)[:, None] == (r[None, :] // SCAN_SUB))
    tri = []
    for keep in (r[None, :] <= r[:, None], r[None, :] >= r[:, None]):
        tri.append(np.concatenate([same & keep, sums], axis=0).astype(np.float32))
    ex = ((r[:, None] // SCAN_SUB) == np.arange(ns)[None, :]).astype(np.float32)
    return jnp.asarray(np.stack(tri), BF16), jnp.asarray(np.concatenate([ex, ex], axis=1), BF16)


def _scan_tables(lr, li):
    import numpy as np
    den = lr * lr + li * li
    ir, ii = lr / den, -li / den
    mul = lambda a, b: (a[0] * b[0] - a[1] * b[1], a[0] * b[1] + a[1] * b[0])
    pw = {0: (jnp.ones_like(lr), jnp.zeros_like(lr))}
    for e in range(1, 9):
        pw[e] = mul(pw[e - 1], (lr, li))
    for e in range(-1, -5, -1):
        pw[e] = mul(pw[e + 1], (ir, ii))
    powers = jnp.stack([jnp.stack(pw[e]) for e in range(-4, 9)] + [jnp.zeros((2, lr.shape[0]), F32)])
    j = np.arange(SCAN_SUB)
    exps = [4 - j, j - 4, j + 1, j - 3, 3 - j, 8 - j]
    e_idx = np.stack([exps[t] + 4 for t in range(6) for _ in range(2)])
    c_idx = np.tile(np.array([0, 1])[:, None], (6, SCAN_SUB))
    sign = np.where((c_idx == 1) & (np.arange(12)[:, None] >= 6), -1.0, 1.0).astype(np.float32)
    tabs = powers[e_idx, c_idx] * sign[:, :, None]
    lam = powers[np.array([5, 5, 7, 7, 12, 12, 13, 13]), np.array([0, 1, 0, 1, 0, 1, 0, 0])]
    return lam, tabs


SSM_HALVES = 2
SSM_HW = SSM_W // SSM_HALVES
SSM_HN = SSM_N // SSM_HALVES


def _bd_nn(x, w):
    a = w.shape[1]
    return jnp.concatenate([_dot(x[:, h * a:(h + 1) * a], w[h]) for h in range(SSM_HALVES)], axis=1)


def _bd_nt(x, w):
    b = w.shape[2]
    return jnp.concatenate([_dot(x[:, h * b:(h + 1) * b], w[h], NT) for h in range(SSM_HALVES)], axis=1)


def _bd_tn(x, y):
    a, b = x.shape[1] // SSM_HALVES, y.shape[1] // SSM_HALVES
    return jnp.stack([_dot(x[:, h * a:(h + 1) * a], y[:, h * b:(h + 1) * b], TN) for h in range(SSM_HALVES)])


def _ssm_states(u, s0r, s0i, lam_ref, tab_ref, tri_ref, ex_ref, bre, bim, t):
    tab = tuple(tab_ref[k] for k in range(6))
    return _scan_mxu(_bd_nn(u, bre), _bd_nn(u, bim), tab, (lam_ref[2:3, :], lam_ref[3:4, :]),
                     (lam_ref[4:5, :], lam_ref[5:6, :]), tri_ref[0], ex_ref[...], s0r, s0i, t, False)


def _ssm_head(u, z, xr, xi, cre, cim, dskip, wglu, bglu):
    y = _bd_nn(xr, cre) - _bd_nn(xi, cim) + dskip * u
    y2, dgelu = _gelu_and_grad(y)
    gate = _sigmoid(_dot(y2, wglu) + bglu)
    y3 = y2 * gate
    return y2, dgelu, gate, y3


def _with_comm(body, comm, n_in, n_out, grid, mid_step):
    if comm is None:
        return body
    nc = len(comm["args"])
    n_sem = len(comm["scratch"])
    grid = (grid,) if isinstance(grid, int) else tuple(grid)
    total = math.prod(grid)

    def hosted(*refs):
        ins, cin = refs[:n_in], refs[n_in:n_in + nc]
        outs, cout = refs[n_in + nc:n_in + nc + n_out], refs[n_in + nc + n_out:n_in + 2 * nc + n_out]
        rest = refs[n_in + 2 * nc + n_out:]
        scratch, csem = rest[:len(rest) - n_sem], rest[len(rest) - n_sem:]
        start, forward, finish = comm["phases"](cin, cout, *csem)
        step = pl.program_id(0)
        for axis in range(1, len(grid)):
            step = step * grid[axis] + pl.program_id(axis)
        pl.when(step == 0)(start)
        pl.when(step == (mid_step if mid_step >= 0 else total + mid_step))(forward)
        body(*ins, *outs, *scratch)
        pl.when(step == total - 1)(finish)

    return hosted


def _comm_extra(comm):
    if comm is None:
        return [], [], [], [], []
    anyspec = pl.BlockSpec(memory_space=pl.ANY)
    nc = len(comm["args"])
    return comm["args"], [anyspec] * nc, [anyspec] * nc, comm["out_shape"], comm["scratch"]


def _ssm_fwd(ps, scan_ops, bblk, cblk, dskip, wglu, bglu, *, name, t=SEQ_BLOCK, comm=None):
    l = ps.shape[0]
    assert l % t == 0
    nb = l // t
    ns = t // SCAN_SUB
    c_args, c_in, c_out, c_shape, c_scratch = _comm_extra(comm)

    def body(ps_ref, lam_ref, tab_ref, tri_ref, ex_ref, b_ref, c_ref, d_ref, w_ref, bg_ref, ys_ref, chk_ref, xs_ref,
             st_ref):
        @pl.when(pl.program_id(0) == 0)
        def _():
            st_ref[...] = jnp.zeros_like(st_ref)

        chk_ref[...] = jnp.broadcast_to(st_ref[...], chk_ref.shape)
        u = ps_ref[:, :SSM_W].astype(F32)
        z = ps_ref[:, SSM_W:].astype(F32)
        xr, xi, er, ei = _ssm_states(u, st_ref[:, :SSM_N], st_ref[:, SSM_N:], lam_ref, tab_ref, tri_ref, ex_ref,
                                     b_ref[0], b_ref[1], t)
        st_ref[:, :SSM_N] = er
        st_ref[:, SSM_N:] = ei
        xr, xi = xr.astype(BF16), xi.astype(BF16)
        xs_ref[:, :SSM_N] = xr
        xs_ref[:, SSM_N:] = xi
        _, _, _, y3 = _ssm_head(u, z, xr, xi, c_ref[0], c_ref[1], d_ref[...], w_ref[...], bg_ref[...])
        sz, _ = _silu_and_grad(z)
        ys_ref[...] = (y3 * sz).astype(BF16)

    full = lambda shape: pl.BlockSpec(shape, lambda i: (0,) * len(shape))
    return pl.pallas_call(
        _with_comm(body, comm, 10, 3, nb, nb - 1), grid=(nb,),
        in_specs=[pl.BlockSpec((t, 2 * SSM_W), lambda i: (i, 0)), full((8, SSM_N)), full((12, SCAN_SUB, SSM_N)),
                  full((2, t + ns, t)), full((t, 2 * ns)), full((2, SSM_HALVES, SSM_HW, SSM_HN)),
                  full((2, SSM_HALVES, SSM_HN, SSM_HW)), full((1, SSM_W)), full((SSM_W, SSM_W)), full((1, SSM_W))] + c_in,
        out_specs=[pl.BlockSpec((t, SSM_W), lambda i: (i, 0)), pl.BlockSpec((8, 2 * SSM_N), lambda i: (i, 0)),
                   pl.BlockSpec((t, 2 * SSM_N), lambda i: (i, 0))] + c_out,
        out_shape=[jax.ShapeDtypeStruct((l, SSM_W), BF16), jax.ShapeDtypeStruct((nb * 8, 2 * SSM_N), F32),
                   jax.ShapeDtypeStruct((l, 2 * SSM_N), BF16)] + c_shape,
        scratch_shapes=[pltpu.VMEM((1, 2 * SSM_N), F32)] + c_scratch,
        compiler_params=_params("arbitrary"), name=name)(ps, *scan_ops, bblk, cblk, dskip, wglu, bglu, *c_args)


def _ssm_bwd(ps, dys, chk, states, scan_ops, bblk, cblk, dskip, wglu, bglu, *, name, t=SEQ_BLOCK, comm=None):
    l = ps.shape[0]
    assert l % t == 0
    nb = l // t
    ns = t // SCAN_SUB
    c_args, c_in, c_out, c_shape, c_scratch = _comm_extra(comm)

    def body(ps_ref, dys_ref, chk_ref, xs_ref, lam_ref, tab_ref, tri_ref, ex_ref, b_ref, c_ref, d_ref, w_ref, bg_ref,
             dps_ref, db_ref, dc_ref, dw_acc, sums_acc, gc_ref, db_acc, dc_acc):
        n = pl.program_id(0)

        @pl.when(n == 0)
        def _():
            gc_ref[...] = jnp.zeros_like(gc_ref)
            db_acc[...] = jnp.zeros_like(db_acc)
            dc_acc[...] = jnp.zeros_like(dc_acc)
            dw_acc[...] = jnp.zeros_like(dw_acc)
            sums_acc[...] = jnp.zeros_like(sums_acc)

        row = lax.broadcasted_iota(jnp.int32, (t, 1), 0)
        u = ps_ref[:, :SSM_W].astype(F32)
        z = ps_ref[:, SSM_W:].astype(F32)
        s0r, s0i = chk_ref[0:1, :SSM_N], chk_ref[0:1, SSM_N:]
        xr, xi = xs_ref[:, :SSM_N], xs_ref[:, SSM_N:]
        dskip = d_ref[...]
        y2, dgelu, gate, y3 = _ssm_head(u, z, xr, xi, c_ref[0], c_ref[1], dskip, w_ref[...], bg_ref[...])
        sz, dsz = _silu_and_grad(z)
        dys_v = dys_ref[...]
        dps_ref[:, SSM_W:] = (dys_v * y3 * dsz).astype(BF16)
        dy3 = dys_v * sz
        da = dy3 * y2 * gate * (1.0 - gate)
        dy2 = dy3 * gate + _dot(da, w_ref[...], NT)
        dw_acc[...] += _dot(y2, da, TN)
        dy = dy2 * dgelu
        sums_acc[2:3, :SSM_W] += jnp.sum(dy * u, axis=0, keepdims=True)
        sums_acc[3:4, :SSM_W] += jnp.sum(da, axis=0, keepdims=True)
        dc_acc[0] += _bd_tn(dy, xr)
        dc_acc[1] += -_bd_tn(dy, xi)
        rev_tab = tuple(tab_ref[k] for k in range(6, 12))
        gr, gi, gcr, gci = _scan_mxu(
            _bd_nt(dy, c_ref[0]), -_bd_nt(dy, c_ref[1]), rev_tab, (lam_ref[2:3, :], -lam_ref[3:4, :]),
            (lam_ref[4:5, :], -lam_ref[5:6, :]), tri_ref[1], ex_ref[...], gc_ref[:, :SSM_N], gc_ref[:, SSM_N:], t, True)
        gc_ref[:, :SSM_N] = gcr
        gc_ref[:, SSM_N:] = gci
        db_acc[0] += _bd_tn(u, gr)
        db_acc[1] += _bd_tn(u, gi)
        du = dskip * dy + _bd_nt(gr, b_ref[0]) + _bd_nt(gi, b_ref[1])
        dps_ref[:, :SSM_W] = du.astype(BF16)
        spr = jnp.where(row == 0, s0r, pltpu.roll(xr.astype(F32), 1, 0))
        spi = jnp.where(row == 0, s0i, pltpu.roll(xi.astype(F32), 1, 0))
        sums_acc[0:1, :] += jnp.sum(gr * spr + gi * spi, axis=0, keepdims=True)
        sums_acc[1:2, :] += jnp.sum(gi * spr - gr * spi, axis=0, keepdims=True)

        @pl.when(n == nb - 1)
        def _():
            per_half = SSM_GROUPS // SSM_HALVES
            for k in range(2):
                for g in range(SSM_GROUPS):
                    h, gl = divmod(g, per_half)
                    c0, p0 = gl * SSM_GROUP, gl * SSM_STATE
                    db_ref[k, g * SSM_GROUP:(g + 1) * SSM_GROUP, :] = db_acc[k, h, c0:c0 + SSM_GROUP, p0:p0 + SSM_STATE]
                    dc_ref[k, g * SSM_GROUP:(g + 1) * SSM_GROUP, :] = dc_acc[k, h, c0:c0 + SSM_GROUP, p0:p0 + SSM_STATE]

    full = lambda shape: pl.BlockSpec(shape, lambda n: (0,) * len(shape))
    return pl.pallas_call(
        _with_comm(body, comm, 13, 5, nb, 0), grid=(nb,),
        in_specs=[pl.BlockSpec((t, 2 * SSM_W), lambda n: (nb - 1 - n, 0)),
                  pl.BlockSpec((t, SSM_W), lambda n: (nb - 1 - n, 0)),
                  pl.BlockSpec((8, 2 * SSM_N), lambda n: (nb - 1 - n, 0)),
                  pl.BlockSpec((t, 2 * SSM_N), lambda n: (nb - 1 - n, 0)),
                  full((8, SSM_N)), full((12, SCAN_SUB, SSM_N)), full((2, t + ns, t)), full((t, 2 * ns)),
                  full((2, SSM_HALVES, SSM_HW, SSM_HN)), full((2, SSM_HALVES, SSM_HN, SSM_HW)), full((1, SSM_W)),
                  full((SSM_W, SSM_W)), full((1, SSM_W))] + c_in,
        out_specs=[pl.BlockSpec((t, 2 * SSM_W), lambda n: (nb - 1 - n, 0)), full((2, SSM_W, SSM_STATE)),
                   full((2, SSM_W, SSM_STATE)), full((SSM_W, SSM_W)), full((8, SSM_N))] + c_out,
        out_shape=[jax.ShapeDtypeStruct((l, 2 * SSM_W), BF16),
                   jax.ShapeDtypeStruct((2, SSM_W, SSM_STATE), F32),
                   jax.ShapeDtypeStruct((2, SSM_W, SSM_STATE), F32),
                   jax.ShapeDtypeStruct((SSM_W, SSM_W), F32),
                   jax.ShapeDtypeStruct((8, SSM_N), F32)] + c_shape,
        scratch_shapes=[pltpu.VMEM((1, 2 * SSM_N), F32), pltpu.VMEM((2, SSM_HALVES, SSM_HW, SSM_HN), F32),
                        pltpu.VMEM((2, SSM_HALVES, SSM_HW, SSM_HN), F32)] + c_scratch,
        compiler_params=_params("arbitrary"), name=name)(ps, dys, chk, states, *scan_ops, bblk, cblk, dskip, wglu, bglu,
                                                         *c_args)


def _pool_count(i, t):
    pos = lax.broadcasted_iota(jnp.int32, (t, POOL_W), 0) + i * t + 1
    col = lax.broadcasted_iota(jnp.int32, (t, POOL_W), 1)
    win = jnp.where(col < POOL_GW, 2, jnp.where(col < 2 * POOL_GW, 4, jnp.where(col < 3 * POOL_GW, 8, 16)))
    return 1.0 / jnp.minimum(pos, win).astype(F32), col


def _window_sums(ext, n_rows, forward):
    col = lax.broadcasted_iota(jnp.int32, ext.shape, 1)
    sh = (lambda a, d: pltpu.roll(a, d, 0)) if forward else (lambda a, d: pltpu.roll(a, n_rows - d, 0))
    a2 = ext + sh(ext, 1)
    a4 = a2 + sh(a2, 2)
    a8 = a4 + sh(a4, 4)
    a16 = a8 + sh(a8, 8)
    return jnp.where(col < POOL_GW, a2, jnp.where(col < 2 * POOL_GW, a4, jnp.where(col < 3 * POOL_GW, a8, a16)))


def _pool_mix(pooled, wp_ref):
    return jnp.concatenate([_dot(pooled[:, g * POOL_GW:(g + 1) * POOL_GW], wp_ref[g]) for g in range(4)], axis=1)


def _pool_pooled(i, cur_u, prev_u, t):
    prev = jnp.where(i > 0, prev_u, 0.0)
    ext = jnp.concatenate([prev, cur_u], axis=0)
    inv_cnt, _ = _pool_count(i, t)
    return _window_sums(ext, t + POOL_HALO, True)[POOL_HALO:, :] * inv_cnt - cur_u


def _pool_fwd(pp, wpool, pscale, *, name, t=POOL_BLOCK):
    l = pp.shape[0]
    t = min(t, l)

    def body(cur_ref, prev_ref, wp_ref, sc_ref, yp_ref):
        i = pl.program_id(0)
        pooled = _pool_pooled(i, cur_ref[:, :POOL_W].astype(F32), prev_ref[...].astype(F32), t)
        lin = _pool_mix(pooled, wp_ref)
        sz, _ = _silu_and_grad(cur_ref[:, POOL_W:].astype(F32))
        yp_ref[...] = (lin * sc_ref[...] * sz).astype(BF16)

    hb = t // POOL_HALO
    return pl.pallas_call(
        body, grid=(l // t,),
        in_specs=[pl.BlockSpec((t, 2 * POOL_W), lambda i: (i, 0)),
                  pl.BlockSpec((POOL_HALO, POOL_W), lambda i: (jnp.maximum(i * hb - 1, 0), 0)),
                  pl.BlockSpec((4, POOL_GW, POOL_GW), lambda i: (0, 0, 0)),
                  pl.BlockSpec((1, POOL_W), lambda i: (0, 0))],
        out_specs=pl.BlockSpec((t, POOL_W), lambda i: (i, 0)),
        out_shape=jax.ShapeDtypeStruct((l, POOL_W), BF16),
        compiler_params=_params("parallel"), name=name)(pp, pp, wpool, pscale)


def _pool_bwd(pp, dyp, wpool, pscale, *, name, t=POOL_BLOCK):
    l = pp.shape[0]
    t = min(t, l)
    nb = l // t

    def body(cur_ref, prev_ref, dyp_ref, wp_ref, sc_ref, dpp_ref, dwp_ref, sums_ref, carry_ref):
        n = pl.program_id(0)
        i = nb - 1 - n

        @pl.when(n == 0)
        def _():
            carry_ref[...] = jnp.zeros_like(carry_ref)
            dwp_ref[...] = jnp.zeros_like(dwp_ref)
            sums_ref[...] = jnp.zeros_like(sums_ref)

        cur_u = cur_ref[:, :POOL_W].astype(F32)
        pooled = _pool_pooled(i, cur_u, prev_ref[...].astype(F32), t)
        lin = _pool_mix(pooled, wp_ref)
        sz, dsz = _silu_and_grad(cur_ref[:, POOL_W:].astype(F32))
        dyp_v = dyp_ref[...]
        scale = sc_ref[...]
        dpp_ref[:, POOL_W:] = (dyp_v * lin * scale * dsz).astype(BF16)
        dpre = dyp_v * sz
        sums_ref[0:1, :] += jnp.sum(dpre * lin, axis=0, keepdims=True)
        dlin = dpre * scale
        dpooled = []
        for g in range(4):
            dl = dlin[:, g * POOL_GW:(g + 1) * POOL_GW]
            dwp_ref[g] += _dot(pooled[:, g * POOL_GW:(g + 1) * POOL_GW], dl, TN)
            dpooled.append(_dot(dl, wp_ref[g], NT))
        dpooled = jnp.concatenate(dpooled, axis=1)
        inv_cnt, _ = _pool_count(i, t)
        dq = dpooled * inv_cnt
        ext = jnp.concatenate([dq, carry_ref[...]], axis=0)
        du = _window_sums(ext, t + POOL_HALO, False)[:t, :] - dpooled
        dpp_ref[:, :POOL_W] = du.astype(BF16)
        carry_ref[...] = dq[:POOL_HALO, :]

    hb = t // POOL_HALO
    return pl.pallas_call(
        body, grid=(nb,),
        in_specs=[pl.BlockSpec((t, 2 * POOL_W), lambda n: (nb - 1 - n, 0)),
                  pl.BlockSpec((POOL_HALO, POOL_W), lambda n: (jnp.maximum((nb - 1 - n) * hb - 1, 0), 0)),
                  pl.BlockSpec((t, POOL_W), lambda n: (nb - 1 - n, 0)),
                  pl.BlockSpec((4, POOL_GW, POOL_GW), lambda n: (0, 0, 0)),
                  pl.BlockSpec((1, POOL_W), lambda n: (0, 0))],
        out_specs=[pl.BlockSpec((t, 2 * POOL_W), lambda n: (nb - 1 - n, 0)),
                   pl.BlockSpec((4, POOL_GW, POOL_GW), lambda n: (0, 0, 0)),
                   pl.BlockSpec((8, POOL_W), lambda n: (0, 0))],
        out_shape=[jax.ShapeDtypeStruct((l, 2 * POOL_W), BF16), jax.ShapeDtypeStruct((4, POOL_GW, POOL_GW), F32),
                   jax.ShapeDtypeStruct((8, POOL_W), F32)],
        scratch_shapes=[pltpu.VMEM((POOL_HALO, POOL_W), F32)],
        compiler_params=_params("arbitrary"), name=name)(pp, pp, dyp, wpool, pscale)


def _merge_fwd(ya, ys, yp, wa, ws, wp, pg, wout, x, gate, *, name, tm=512):
    l, d = x.shape
    tm = min(tm, l)

    def body(ya_ref, ys_ref, yp_ref, wa_ref, ws_ref, wp_ref, pg_ref, wo_ref, x_ref, g_ref,
             xn_ref, mg_ref, ba_ref, bs_ref, bp_ref, out_ref):
        acc = None
        for k, (y_ref, w_ref, b_ref) in enumerate(((ya_ref, wa_ref, ba_ref), (ys_ref, ws_ref, bs_ref),
                                                   (yp_ref, wp_ref, bp_ref))):
            br = _dot(y_ref[...], w_ref[...])
            b_ref[...] = br.astype(BF16)
            term = _sigmoid(pg_ref[:, k * d:(k + 1) * d].astype(F32)) * br
            acc = term if acc is None else acc + term
        merged = acc.astype(BF16)
        mg_ref[...] = merged
        out = _dot(merged, wo_ref[...])
        out_ref[...] = out.astype(BF16)
        xn_ref[...] = x_ref[...] + g_ref[...] * out

    rowy = pl.BlockSpec((tm, ATT_W), lambda i: (i, 0))
    wsp = pl.BlockSpec((ATT_W, d), lambda i: (0, 0))
    rowd = pl.BlockSpec((tm, d), lambda i: (i, 0))
    return pl.pallas_call(
        body, grid=(l // tm,),
        in_specs=[rowy, rowy, rowy, wsp, wsp, wsp, pl.BlockSpec((tm, 3 * d), lambda i: (i, 0)),
                  pl.BlockSpec((d, d), lambda i: (0, 0)), rowd, pl.BlockSpec((1, d), lambda i: (0, 0))],
        out_specs=[rowd] * 6,
        out_shape=[jax.ShapeDtypeStruct((l, d), F32)] + [jax.ShapeDtypeStruct((l, d), BF16)] * 5,
        compiler_params=_params("parallel"), name=name)(ya, ys, yp, wa, ws, wp, pg, wout, x, gate)


def _merge_bwd(dx, out, gate, wout, pg, brs, wbrs, ys, merged, *, name, tm=256, comm=None):
    l, d = dx.shape
    tm = min(tm, l)
    nb = l // tm
    w = ys[0].shape[1]

    def body(dx_ref, out_ref, g_ref, w_ref, pg_ref, ba_ref, bs_ref, bp_ref, wa_ref, ws_ref, wp_ref,
             ya_ref, ys_ref, yp_ref, mg_ref,
             dya_ref, dys_ref, dyp_ref, dpg_ref, sums_ref, dwa_ref, dws_ref, dwp_ref, dwo_ref, acc_br, acc_out):
        i = pl.program_id(0)

        @pl.when(i == 0)
        def _():
            sums_ref[...] = jnp.zeros_like(sums_ref)
            acc_br[...] = jnp.zeros_like(acc_br)
            acc_out[...] = jnp.zeros_like(acc_out)

        dxv = dx_ref[...]
        sums_ref[0:1, :] += jnp.sum(dxv * out_ref[...].astype(F32), axis=0, keepdims=True)
        dmo = (dxv * g_ref[...]).astype(BF16)
        acc_out[...] += _dot(mg_ref[...], dmo, TN)
        dmerged = _dot(dmo, w_ref[...], NT)
        branches = ((ba_ref, wa_ref, ya_ref, dya_ref), (bs_ref, ws_ref, ys_ref, dys_ref), (bp_ref, wp_ref, yp_ref, dyp_ref))
        for k, (b_ref, wk_ref, y_ref, dy_ref) in enumerate(branches):
            gk = _sigmoid(pg_ref[:, k * d:(k + 1) * d].astype(F32))
            dbr = (dmerged * gk).astype(BF16)
            dpg_ref[:, k * d:(k + 1) * d] = (dmerged * b_ref[...].astype(F32) * gk * (1.0 - gk)).astype(BF16)
            dy_ref[...] = _dot(dbr, wk_ref[...], NT)
            acc_br[k] += _dot(y_ref[...], dbr, TN)

        @pl.when(i == nb - 1)
        def _():
            for k, dw_ref in enumerate((dwa_ref, dws_ref, dwp_ref)):
                dw_ref[...] = acc_br[k].astype(BF16)
            dwo_ref[...] = acc_out[...].astype(BF16)

    row = pl.BlockSpec((tm, d), lambda i: (i, 0))
    half = pl.BlockSpec((tm, w), lambda i: (i, 0))
    wide = pl.BlockSpec((tm, 3 * d), lambda i: (i, 0))
    const = lambda shape: pl.BlockSpec(shape, lambda i: (0,) * len(shape))
    c_args, c_in, c_out, c_shape, c_scratch = _comm_extra(comm)
    res = pl.pallas_call(
        _with_comm(body, comm, 15, 9, nb, 0), grid=(nb,),
        in_specs=[row, row, const((1, d)), const((d, d)), wide, row, row, row, const((w, d)), const((w, d)), const((w, d)),
                  half, half, half, row] + c_in,
        out_specs=[half, half, half, wide, const((8, d)), const((w, d)), const((w, d)), const((w, d)), const((d, d))] + c_out,
        out_shape=[jax.ShapeDtypeStruct((l, w), F32)] * 3 + [jax.ShapeDtypeStruct((l, 3 * d), BF16),
                                                             jax.ShapeDtypeStruct((8, d), F32)]
                  + [jax.ShapeDtypeStruct((w, d), BF16)] * 3 + [jax.ShapeDtypeStruct((d, d), BF16)] + c_shape,
        scratch_shapes=[pltpu.VMEM((3, w, d), F32), pltpu.VMEM((d, d), F32)] + c_scratch,
        compiler_params=_params("arbitrary"), name=name)(dx, out, gate, wout, pg, *brs, *wbrs, *ys, merged, *c_args)
    return list(res[:9]), list(res[9:])


def _adamw(w, gs, m, v, *, name, tr=256):
    r, c = w.shape
    ns = len(gs)
    p, rs, _ = gs[0].shape
    assert rs * ns == r
    tr = min(tr, rs)
    assert rs % tr == 0
    nr = rs // tr
    c1 = 1.0 / (1.0 - ADAM_B1 ** ADAM_STEP)
    c2 = 1.0 / (1.0 - ADAM_B2 ** ADAM_STEP)

    def body(*refs):
        w_ref, g_refs, (m_ref, v_ref, go_ref, d_ref, mo_ref, vo_ref) = refs[0], refs[1:1 + ns], refs[1 + ns:]
        slab = pl.program_id(0)
        gv = None
        for k, g_ref in enumerate(g_refs):
            gk = g_ref[0].astype(F32)
            for j in range(1, p):
                gk = gk + g_ref[j].astype(F32)
            gv = gk if gv is None else jnp.where(slab == k, gk, gv)
        go_ref[...] = gv
        mn = ADAM_B1 * m_ref[...] + (1.0 - ADAM_B1) * gv
        vn = ADAM_B2 * v_ref[...] + (1.0 - ADAM_B2) * (gv * gv)
        mo_ref[...] = mn
        vo_ref[...] = vn
        d_ref[...] = -ADAM_LR * ((mn * c1) / (jnp.sqrt(vn * c2) + ADAM_EPS) + ADAM_WD * w_ref[...])

    row = pl.BlockSpec((tr, c), lambda s, i: (s * nr + i, 0))
    g_specs = [pl.BlockSpec((p, tr, c), lambda s, i, k=k: (0, jnp.where(s == k, i, 0), 0)) for k in range(ns)]
    return pl.pallas_call(
        body, grid=(ns, nr),
        in_specs=[row] + g_specs + [row, row],
        out_specs=[row] * 4,
        out_shape=[jax.ShapeDtypeStruct((r, c), F32)] * 4,
        compiler_params=_params("arbitrary", "arbitrary"), name=name)(w, *gs, m, v)


def _mesh_place():
    x, y, c = lax.axis_index("x"), lax.axis_index("y"), lax.axis_index("c")
    other_chips = [(1 - x, y), (x, 1 - y), (1 - x, 1 - y)]
    return x, y, c, other_chips


def _run_plan(plan, *, name):
    n = len(plan["args"])

    def body(*refs):
        start, forward, finish = plan["phases"](refs[:n], refs[n:2 * n], *refs[2 * n:])
        start()
        forward()
        finish()

    anyspec = pl.BlockSpec(memory_space=pl.ANY)
    return pl.pallas_call(
        body, in_specs=[anyspec] * n, out_specs=[anyspec] * n, out_shape=plan["out_shape"],
        scratch_shapes=plan["scratch"], name=name)(*plan["args"])


def _gather_plan(arrs):
    n = len(arrs)

    def phases(ins, outs, send_sems, recv_sems, loc_sems):
        x, y, c, chips = _mesh_place()
        me = 4 * x + 2 * y + c
        slot = lambda px, py, pc: 4 * px + 2 * py + pc

        def copy(k, j, src, block, to):
            return pltpu.make_async_remote_copy(
                src_ref=src, dst_ref=outs[k].at[block], send_sem=send_sems.at[k, j], recv_sem=recv_sems.at[k, j],
                device_id=to, device_id_type=pl.DeviceIdType.MESH)

        local = [pltpu.make_async_copy(ins[k], outs[k].at[me], loc_sems.at[k]) for k in range(n)]
        first = []
        for k in range(n):
            first.append(copy(k, 0, ins[k], me, (x, y, 1 - c)))
            for j, chip in enumerate(chips):
                first.append(copy(k, 1 + j, ins[k], me, (*chip, c)))
        passed = [copy(k, 4 + j, outs[k].at[slot(*chip, c)], slot(*chip, c), (x, y, 1 - c))
                  for j, chip in enumerate(chips) for k in range(n)]

        def start():
            for cp in local + first:
                cp.start()

        def forward():
            for j, chip in enumerate(chips):
                for k in range(n):
                    copy(k, 1 + j, ins[k], slot(*chip, c), (x, y, c)).wait_recv()
                    passed[j * n + k].start()

        def finish():
            for k in range(n):
                copy(k, 0, ins[k], slot(x, y, 1 - c), (x, y, c)).wait_recv()
                for j, chip in enumerate(chips):
                    copy(k, 4 + j, ins[k], slot(*chip, 1 - c), (x, y, c)).wait_recv()
            for cp in first + passed:
                cp.wait_send()
            for cp in local:
                cp.wait()

        return start, forward, finish

    return dict(
        args=list(arrs), out_shape=[jax.ShapeDtypeStruct((N_DEV,) + a.shape, a.dtype) for a in arrs],
        scratch=[pltpu.SemaphoreType.DMA((n, 7)), pltpu.SemaphoreType.DMA((n, 7)), pltpu.SemaphoreType.DMA((n,))],
        phases=phases)


def _allreduce_small(small, extra, *, name):
    r, lanes = small.shape
    assert r % 16 == 0
    h = r // 2
    e = extra.shape[0]

    def body(s_ref, x_ref, out_ref, xall_ref, sib_ref, parts_ref, send_sems, recv_sems):
        x, y, c, chips = _mesh_place()
        me = 4 * x + 2 * y + c
        my_chip = 2 * x + y
        sibling = (x, y, 1 - c)
        mine = pl.ds(pl.multiple_of(c * h, 8), h)
        theirs = pl.ds(pl.multiple_of((1 - c) * h, 8), h)

        def remote(j, src, dst, to):
            return pltpu.make_async_remote_copy(src_ref=src, dst_ref=dst, send_sem=send_sems.at[j],
                                                recv_sem=recv_sems.at[j], device_id=to, device_id_type=pl.DeviceIdType.MESH)

        to_sibling = remote(0, s_ref.at[theirs], sib_ref, sibling)
        to_sibling.start()
        xall_ref[me] = x_ref[...]
        extras = []
        for rr in range(1, N_DEV):
            peer = me ^ rr
            cp = remote(4 + rr, x_ref, xall_ref.at[me], (peer // 4, (peer // 2) % 2, peer % 2))
            cp.start()
            extras.append(cp)
        to_sibling.wait_recv()
        parts_ref[my_chip] = s_ref[mine] + sib_ref[...]
        to_chips = [remote(1 + j, parts_ref.at[my_chip], parts_ref.at[my_chip], (px, py, c))
                    for j, (px, py) in enumerate(chips)]
        for cp in to_chips:
            cp.start()
        for cp in to_chips:
            cp.wait_recv()
        out_ref[mine] = (parts_ref[0] + parts_ref[1]) + (parts_ref[2] + parts_ref[3])
        done = remote(4, out_ref.at[mine], out_ref.at[mine], sibling)
        done.start()
        remote(4, out_ref.at[theirs], out_ref.at[theirs], sibling).wait_recv()
        for cp in extras:
            cp.wait()
        to_sibling.wait_send()
        for cp in to_chips:
            cp.wait_send()
        done.wait_send()

    vmem = pl.BlockSpec(memory_space=pltpu.VMEM)
    return pl.pallas_call(
        body, in_specs=[vmem, vmem], out_specs=[vmem, vmem],
        out_shape=[jax.ShapeDtypeStruct((r, lanes), F32), jax.ShapeDtypeStruct((N_DEV, e, lanes), F32)],
        scratch_shapes=[pltpu.VMEM((h, lanes), F32), pltpu.VMEM((4, h, lanes), F32),
                        pltpu.SemaphoreType.DMA((12,)), pltpu.SemaphoreType.DMA((12,))],
        compiler_params=pltpu.CompilerParams(vmem_limit_bytes=VMEM_LIMIT), name=name)(small, extra)


def _ada_modulation(c, w_ada, b_cols, comm):
    depth, d, cols = w_ada.shape
    c_args, c_in, c_out, c_shape, c_scratch = _comm_extra(comm)
    nc = len(c_args)

    def body(c_ref, w_ref, b_ref, *refs):
        cin, (cact_ref, mod_ref), cout = refs[:nc], refs[nc:nc + 2], refs[nc + 2:2 * nc + 2]
        call_ref, part_ref, send_sems, recv_sems = refs[2 * nc + 2:2 * nc + 6]
        start, forward, finish = comm["phases"](cin, cout, *refs[2 * nc + 6:])
        start()
        x, y, core, _ = _mesh_place()
        me = 4 * x + 2 * y + core

        def to_all(j0, src, dst):
            copies = []
            for r in range(1, N_DEV):
                peer = me ^ r
                copies.append(pltpu.make_async_remote_copy(
                    src_ref=src, dst_ref=dst, send_sem=send_sems.at[j0 + r - 1], recv_sem=recv_sems.at[j0 + r - 1],
                    device_id=(peer // 4, (peer // 2) % 2, peer % 2), device_id_type=pl.DeviceIdType.MESH))
            for cp in copies:
                cp.start()
            for cp in copies:
                cp.wait()

        call_ref[me] = c_ref[...]
        to_all(0, c_ref, call_ref.at[me])
        c_act = jnp.concatenate([call_ref[k] for k in range(N_DEV)], axis=0)
        c_act = c_act * _sigmoid(c_act)
        cact_ref[...] = c_act
        for li in range(depth):
            part_ref[li] = _dot(c_act, w_ref[li]) + b_ref[li:li + 1, :]
        mod_ref[me] = part_ref[...]
        to_all(N_DEV - 1, part_ref, mod_ref.at[me])
        forward()
        finish()

    vmem = pl.BlockSpec(memory_space=pltpu.VMEM)
    res = pl.pallas_call(
        body, in_specs=[vmem] * 3 + c_in, out_specs=[vmem] * 2 + c_out,
        out_shape=[jax.ShapeDtypeStruct((N_DEV, d), F32), jax.ShapeDtypeStruct((N_DEV, depth, N_DEV, cols), F32)] + c_shape,
        scratch_shapes=[pltpu.VMEM((N_DEV, 1, d), F32), pltpu.VMEM((depth, N_DEV, cols), F32),
                        pltpu.SemaphoreType.DMA((2 * (N_DEV - 1),)), pltpu.SemaphoreType.DMA((2 * (N_DEV - 1),))] + c_scratch,
        compiler_params=pltpu.CompilerParams(vmem_limit_bytes=VMEM_LIMIT), name="ada_modulation")(c, w_ada, b_cols, *c_args)
    return res[0], res[1], list(res[2:])


def _sibling_swap_plan(arrs):
    n = len(arrs)

    def phases(ins, outs, send_sems, recv_sems):
        x, y, c, _ = _mesh_place()
        copies = [pltpu.make_async_remote_copy(
            src_ref=ins[k].at[1 - c], dst_ref=outs[k], send_sem=send_sems.at[k], recv_sem=recv_sems.at[k],
            device_id=(x, y, 1 - c), device_id_type=pl.DeviceIdType.MESH) for k in range(n)]

        def start():
            for cp in copies:
                cp.start()

        def finish():
            for cp in copies:
                cp.wait()

        return start, (lambda: None), finish

    return dict(args=list(arrs), out_shape=[jax.ShapeDtypeStruct(a.shape[1:], a.dtype) for a in arrs],
                scratch=[pltpu.SemaphoreType.DMA((n,)), pltpu.SemaphoreType.DMA((n,))], phases=phases)


def _pair_add(mine, theirs, core, *, name, tr=1024):
    _, r, c = mine.shape
    tr = min(tr, r)
    assert r % tr == 0

    def body(core_ref, m_ref, t_ref, o_ref):
        o_ref[...] = (m_ref[0].astype(F32) + t_ref[...].astype(F32)).astype(BF16)

    return pl.pallas_call(
        body,
        grid_spec=pltpu.PrefetchScalarGridSpec(
            num_scalar_prefetch=1, grid=(r // tr,),
            in_specs=[pl.BlockSpec((1, tr, c), lambda i, core_ref: (core_ref[0], i, 0)),
                      pl.BlockSpec((tr, c), lambda i, core_ref: (i, 0))],
            out_specs=pl.BlockSpec((tr, c), lambda i, core_ref: (i, 0))),
        out_shape=jax.ShapeDtypeStruct((r, c), BF16),
        compiler_params=_params("parallel"), name=name)(core, mine, theirs)


def _pair_add_small(mines, theirs, core, *, name):
    n = len(mines)

    def body(core_ref, *refs):
        for m_ref, t_ref, o_ref in zip(refs[:n], refs[n:2 * n], refs[2 * n:]):
            o_ref[...] = (m_ref[0].astype(F32) + t_ref[...].astype(F32)).astype(BF16)

    whole = lambda a: pl.BlockSpec(a.shape, lambda i, core_ref: (0,) * a.ndim)
    return pl.pallas_call(
        body,
        grid_spec=pltpu.PrefetchScalarGridSpec(
            num_scalar_prefetch=1, grid=(1,),
            in_specs=[pl.BlockSpec((1,) + m.shape[1:], lambda i, core_ref: (core_ref[0], 0, 0)) for m in mines]
                     + [whole(t) for t in theirs],
            out_specs=[whole(t) for t in theirs]),
        out_shape=[jax.ShapeDtypeStruct(t.shape, BF16) for t in theirs],
        compiler_params=_params("arbitrary"), name=name)(core, *mines, *theirs)


def _chip_scatter_plan(arrs):
    n = len(arrs)

    def phases(ins, outs, send_sems, recv_sems, loc_sems):
        x, y, c, chips = _mesh_place()
        mine = 2 * x + y
        local = [pltpu.make_async_copy(ins[k].at[mine], outs[k].at[mine], loc_sems.at[k]) for k in range(n)]
        remote = [pltpu.make_async_remote_copy(
            src_ref=ins[k].at[2 * px + py], dst_ref=outs[k].at[mine], send_sem=send_sems.at[k, j],
            recv_sem=recv_sems.at[k, j], device_id=(px, py, c), device_id_type=pl.DeviceIdType.MESH)
            for j, (px, py) in enumerate(chips) for k in range(n)]

        def start():
            for cp in local + remote:
                cp.start()

        def finish():
            for cp in remote:
                cp.wait()
            for cp in local:
                cp.wait()

        return start, (lambda: None), finish

    return dict(
        args=list(arrs), out_shape=[jax.ShapeDtypeStruct(a.shape, a.dtype) for a in arrs],
        scratch=[pltpu.SemaphoreType.DMA((n, 3)), pltpu.SemaphoreType.DMA((n, 3)), pltpu.SemaphoreType.DMA((n,))],
        phases=phases)


def _ssm_discretize(a_re, a_im, log_dt, b_re, b_im):
    dt = jnp.exp(log_dt)[:, None]
    mag = jnp.exp(a_re * dt)
    lr = mag * jnp.cos(a_im * dt)
    li = mag * jnp.sin(a_im * dt)
    den = a_re * a_re + a_im * a_im
    cr = ((lr - 1.0) * a_re + li * a_im) / den
    ci = (li * a_re - (lr - 1.0) * a_im) / den
    bbr = cr[..., None] * b_re - ci[..., None] * b_im
    bbi = cr[..., None] * b_im + ci[..., None] * b_re
    return lr, li, bbr, bbi


def _ssm_dense(lr, li, bbr, bbi, c_re, c_im, *, name):
    import numpy as np
    scan_ops = _scan_tables(lr.reshape(-1), li.reshape(-1)) + _scan_consts(SEQ_BLOCK)
    per_half = SSM_GROUPS // SSM_HALVES
    bt = jnp.stack([b.transpose(0, 2, 1).reshape(SSM_W, SSM_STATE) for b in (bbr, bbi)])
    ct = jnp.stack([c.transpose(0, 2, 1).reshape(SSM_N, SSM_GROUP) for c in (c_re, c_im)])
    rep_p = jnp.asarray(np.tile(np.eye(SSM_STATE, dtype=np.float32), (1, per_half)), BF16)
    rep_c = jnp.asarray(np.tile(np.eye(SSM_GROUP, dtype=np.float32), (1, per_half)), BF16)

    def body(bt_ref, ct_ref, rp_ref, rc_ref, b_ref, c_ref):
        def on_diagonal(shape, rows, cols):
            r = lax.broadcasted_iota(jnp.int32, shape, 0) // rows
            c = lax.broadcasted_iota(jnp.int32, shape, 1) // cols
            return r == c

        mask_b = on_diagonal((SSM_HW, SSM_HN), SSM_GROUP, SSM_STATE)
        mask_c = on_diagonal((SSM_HN, SSM_HW), SSM_STATE, SSM_GROUP)
        for k in range(2):
            for h in range(SSM_HALVES):
                b_rows = bt_ref[k, h * SSM_HW:(h + 1) * SSM_HW, :]
                b_ref[k, h] = jnp.where(mask_b, _dot(b_rows, rp_ref[...]), 0.0).astype(BF16)
                c_rows = ct_ref[k, h * SSM_HN:(h + 1) * SSM_HN, :]
                c_ref[k, h] = jnp.where(mask_c, _dot(c_rows, rc_ref[...]), 0.0).astype(BF16)

    vmem = pl.BlockSpec(memory_space=pltpu.VMEM)
    bblk, cblk = pl.pallas_call(
        body, in_specs=[vmem] * 4, out_specs=[vmem] * 2,
        out_shape=[jax.ShapeDtypeStruct((2, SSM_HALVES, SSM_HW, SSM_HN), BF16),
                   jax.ShapeDtypeStruct((2, SSM_HALVES, SSM_HN, SSM_HW), BF16)],
        compiler_params=pltpu.CompilerParams(vmem_limit_bytes=VMEM_LIMIT), name=name)(bt, ct, rep_p, rep_c)
    return scan_ops, bblk, cblk


def _ssm_extract(db, dc, sums):
    db = db.reshape(2, SSM_GROUPS, SSM_GROUP, SSM_STATE).transpose(0, 1, 3, 2)
    dc = dc.reshape(2, SSM_GROUPS, SSM_GROUP, SSM_STATE)
    dlr = sums[0].reshape(SSM_GROUPS, SSM_STATE)
    dli = sums[1].reshape(SSM_GROUPS, SSM_STATE)
    return dlr, dli, db[0], db[1], dc[0], dc[1]


def _in_groups():
    names = ("q", "k", "v", "u_ssm", "u_pool", "z_att", "z_ssm", "z_pool", "gates")
    sizes = (ATT_W, KV_W, KV_W, SSM_W, POOL_W, ATT_W, SSM_W, POOL_W, 3 * D_MODEL)
    r, lo = {}, 0
    for nm, s in zip(names, sizes):
        r[nm] = (lo, lo + s)
        lo += s
    kv = (r["k"][0], r["v"][1])
    return ((r["q"], r["z_att"], kv), (r["u_ssm"], r["z_ssm"]), (r["u_pool"], r["z_pool"]), (r["gates"],))


IN_GROUPS = _in_groups()


def _layer_fwd(x, lw, li, late=None, comm_attn=None, comm_ssm=None):
    tag = f"l{li}"
    h, (pa, ps, pp, pg), arrived = _ln_proj(x, lw["norm_g"], lw["shift"], lw["scale"], lw["w_in"], IN_GROUPS,
                                            name=f"ln_proj_{tag}", comm=None if late is None else late[0])
    if late is not None:
        lw = {**lw, **late[1](arrived)}
    ya, from_attn = _attn_fwd(pa, lw["sinks"], name=f"attn_fwd_{tag}", comm=comm_attn)
    ys, chk, states, *from_ssm = _ssm_fwd(ps, lw["lam"], lw["bblk"], lw["cblk"], lw["ssm_d"], lw["w_glu"], lw["b_glu"],
                                          name=f"ssm_fwd_{tag}", comm=comm_ssm)
    yp = _pool_fwd(pp, lw["w_pool"], lw["pool_scale"], name=f"pool_fwd_{tag}")
    x_new, merged, ba, bs, bp, out = _merge_fwd(ya, ys, yp, lw["w_br_att"], lw["w_br_ssm"], lw["w_br_pool"], pg,
                                                lw["w_out"], x, lw["gate"], name=f"merge_fwd_{tag}")
    saved = dict(x=x, h=h, pa=pa, ps=ps, pp=pp, pg=pg, ya=ya, ys=ys, yp=yp, chk=chk, states=states, merged=merged,
                 ba=ba, bs=bs, bp=bp, out=out)
    return x_new, saved, lw, list(from_attn), list(from_ssm)


def _layer_bwd(dx, lw, sv, li, later=None, own=None):
    tag = f"l{li}"
    g = {}
    merge_out, swapped = _merge_bwd(
        dx, sv["out"], lw["gate"], lw["w_out"], sv["pg"], (sv["ba"], sv["bs"], sv["bp"]),
        (lw["w_br_att"], lw["w_br_ssm"], lw["w_br_pool"]), (sv["ya"], sv["ys"], sv["yp"]), sv["merged"],
        name=f"merge_bwd_{tag}", comm=None if later is None else later[0])
    dya, dys, dyp, dpg, gate_sums, g["w_br_att"], g["w_br_ssm"], g["w_br_pool"], g["w_out"] = merge_out
    dpa, dsink = _attn_bwd(sv["pa"], lw["sinks"], dya, name=f"attn_bwd_{tag}")
    dps, db_dense, dc_dense, dwglu, ssm_sums, *exchanged = _ssm_bwd(
        sv["ps"], dys, sv["chk"], sv["states"], lw["lam"], lw["bblk"], lw["cblk"], lw["ssm_d"], lw["w_glu"], lw["b_glu"],
        name=f"ssm_bwd_{tag}", comm=None if later is None else later[1](swapped))
    g["w_glu"] = dwglu.astype(BF16)
    dpp, dwpool, pool_sums = _pool_bwd(sv["pp"], dyp, lw["w_pool"], lw["pool_scale"], name=f"pool_bwd_{tag}")
    h = sv["h"]
    dproj = (dpa, dps, dpp, dpg)
    g["w_in"], from_late = _mm_tn_grouped(h, dproj, IN_GROUPS, name=f"dw_in_{tag}",
                                          comm=None if own is None else own({k: g[k] for k in LATE_WEIGHTS}))
    dx_in, ln_sums, from_w_in = _ln_proj_bwd(dproj, lw["w_in"], IN_GROUPS, sv["x"], dx, lw["norm_g"], lw["scale"],
                                             name=f"ln_proj_bwd_{tag}",
                                             comm=None if own is None else own({"w_in": g["w_in"]}))
    g["dmod"] = jnp.concatenate([ln_sums[0], ln_sums[1], gate_sums[0]])
    g["norm_g"] = ln_sums[2]
    g["attn_sinks"] = dsink[:, 0]
    g["ssm_raw"] = _ssm_extract(db_dense, dc_dense, ssm_sums)
    g["ssm_d"] = ssm_sums[2, :SSM_W]
    g["b_glu"] = ssm_sums[3, :SSM_W]
    g["w_pool"] = dwpool
    g["pool_scale"] = pool_sums[0]
    return dx_in, g, exchanged, list(from_w_in) + list(from_late)


BIG_WEIGHTS = ("w_in", "w_glu", "w_br_att", "w_br_ssm", "w_br_pool", "w_out")
ROW_SHARDED = ("w_glu", "w_out")


LATE_WEIGHTS = BIG_WEIGHTS[1:]


def _side_by_side(g, *, tm=256):
    n, r, c = g.shape

    def body(g_ref, o_ref):
        for s in range(n):
            o_ref[:, s * c:(s + 1) * c] = g_ref[s]

    return pl.pallas_call(
        body, grid=(r // tm,),
        in_specs=[pl.BlockSpec((n, tm, c), lambda i: (0, i, 0))],
        out_specs=pl.BlockSpec((tm, n * c), lambda i: (i, 0)),
        out_shape=jax.ShapeDtypeStruct((r, n * c), g.dtype),
        compiler_params=_params("parallel"), name="side_by_side")(g)


def _full_weights(keys, gathered):
    full = {}
    for k, g in zip(keys, gathered):
        if k in ROW_SHARDED:
            full[k] = g.reshape(N_DEV * g.shape[1], g.shape[2])
        elif g.shape[2] % 128:
            full[k] = _side_by_side(g)
        else:
            full[k] = g.transpose(1, 0, 2).reshape(g.shape[1], N_DEV * g.shape[2])
    return full


def _by_destination(keys, grads):
    out = []
    for k in keys:
        g = grads[k]
        if g.ndim == 4:
            out.append(g)
        elif k in ROW_SHARDED:
            out.append(g.reshape(4, 2, g.shape[0] // N_DEV, g.shape[1]).transpose(1, 0, 2, 3))
        else:
            out.append(g.reshape(g.shape[0], 4, 2, g.shape[1] // N_DEV).transpose(2, 1, 0, 3))
    return out


def _prepare_layer(li, mod, norm_g, w_in_full, attn_sinks, disc, ssm_c_re, ssm_c_im, ssm_d, b_glu, w_pool, pool_scale):
    d = D_MODEL
    lr, li_, bbr, bbi = disc
    lam, bblk, cblk = _ssm_dense(lr[li], li_[li], bbr[li], bbi[li], ssm_c_re[li], ssm_c_im[li], name=f"ssm_dense_l{li}")
    return dict(
        norm_g=norm_g[li][None, :], shift=mod[li, :d][None, :], scale=mod[li, d:2 * d][None, :],
        gate=mod[li, 2 * d:][None, :], w_in=w_in_full,
        sinks=attn_sinks[li], lam=lam, bblk=bblk, cblk=cblk, ssm_d=ssm_d[li][None, :],
        b_glu=b_glu[li][None, :], w_pool=w_pool[li].astype(BF16), pool_scale=pool_scale[li][None, :])


SMALL_ROWS = 64
SMALL_ORDER = ("norm_g", "attn_sinks", "ssm_d", "b_glu", "w_pool", "pool_scale", "dmod")


def _pack_small(loss, dfinal_g, layer_grads):
    parts = [jnp.broadcast_to(loss.reshape(1), (128,)), dfinal_g]
    for g in layer_grads:
        for k in SMALL_ORDER:
            v = g[k].reshape(-1)
            if v.shape[0] % 128:
                v = jnp.pad(v, (0, 128 - v.shape[0] % 128))
            parts.append(v)
        for v in g["ssm_raw"]:
            parts.append(v.reshape(-1))
    flat = jnp.concatenate(parts)
    return jnp.pad(flat, (0, (-flat.shape[0]) % (SMALL_ROWS * 128))).reshape(-1, 128)


def _unpack_small(flat, shapes):
    out, off = [], 0
    for s in shapes:
        n = int(math.prod(s))
        out.append(flat[off:off + n].reshape(s))
        off += n + (-n) % 128
    return out


def kernel(x, c, norm_g, w_ada, b_ada, w_in, attn_sinks, ssm_a_re, ssm_a_im, ssm_log_dt, ssm_b_re, ssm_b_im, ssm_c_re, ssm_c_im, ssm_d, w_glu, b_glu, w_pool, pool_scale, w_br_att, w_br_ssm, w_br_pool, w_out, final_g, loss_target, m_norm_g, m_w_ada, m_b_ada, m_w_in, m_attn_sinks, m_ssm_a_re, m_ssm_a_im, m_ssm_log_dt, m_ssm_b_re, m_ssm_b_im, m_ssm_c_re, m_ssm_c_im, m_ssm_d, m_w_glu, m_b_glu, m_w_pool, m_pool_scale, m_w_br_att, m_w_br_ssm, m_w_br_pool, m_w_out, m_final_g, v_norm_g, v_w_ada, v_b_ada, v_w_in, v_attn_sinks, v_ssm_a_re, v_ssm_a_im, v_ssm_log_dt, v_ssm_b_re, v_ssm_b_im, v_ssm_c_re, v_ssm_c_im, v_ssm_d, v_w_glu, v_b_glu, v_w_pool, v_pool_scale, v_w_br_att, v_w_br_ssm, v_w_br_pool, v_w_out, v_final_g):
    me = 4 * lax.axis_index("x") + 2 * lax.axis_index("y") + lax.axis_index("c")
    d = D_MODEL
    ada_w = 3 * d // N_DEV

    sharded = dict(w_in=w_in, w_glu=w_glu, w_br_att=w_br_att, w_br_ssm=w_br_ssm, w_br_pool=w_br_pool, w_out=w_out)
    shards = lambda li, keys: [sharded[k][li].astype(BF16) for k in keys]

    b_cols = lax.dynamic_slice(b_ada, (0, me * ada_w), (DEPTH, ada_w))
    c_act, mod_all, w_in0 = _ada_modulation(c, w_ada, b_cols, _gather_plan(shards(0, ("w_in",))))
    mod_mine = lax.dynamic_index_in_dim(mod_all, me, axis=2, keepdims=False)
    mod_mine = mod_mine.transpose(1, 0, 2).reshape(DEPTH, 3 * d)

    disc, disc_vjp = jax.vjp(jax.vmap(_ssm_discretize), ssm_a_re, ssm_a_im, ssm_log_dt, ssm_b_re, ssm_b_im)
    layer = lambda li, gathered_w_in: _prepare_layer(
        li, mod_mine, norm_g, _full_weights(("w_in",), gathered_w_in)["w_in"], attn_sinks, disc, ssm_c_re, ssm_c_im,
        ssm_d, b_glu, w_pool, pool_scale)
    late_weights = lambda gathered: _full_weights(LATE_WEIGHTS, gathered)
    core = lax.axis_index("c").astype(jnp.int32).reshape(1)

    def add_pairs(keys, by_dest, from_sibling, tag):
        flat = {k: (a.reshape(2, -1, a.shape[-1]), b.reshape(-1, b.shape[-1]))
                for k, a, b in zip(keys, by_dest, from_sibling)}
        small = [k for k in keys if k != "w_in"]
        sums = {}
        if "w_in" in flat:
            sums["w_in"] = _pair_add(*flat["w_in"], core, name=f"grads_pair_add_{tag}_w_in")
        if small:
            added = _pair_add_small([flat[k][0] for k in small], [flat[k][1] for k in small], core,
                                    name=f"grads_pair_add_{tag}_late")
            sums.update(zip(small, added))
        return [sums[k].reshape(b.shape) for k, b in zip(keys, from_sibling)]

    def chip_sums_of(keys, grads_li, tag):
        by_dest = _by_destination(keys, grads_li)
        return add_pairs(keys, by_dest, _run_plan(_sibling_swap_plan(by_dest), name=f"grads_sibling_swap_{tag}"), tag)

    layers, saved, grads = [None] * DEPTH, [None] * DEPTH, [None] * DEPTH
    layers[0] = layer(0, w_in0)
    xs, saved[0], layers[0], late1, w_in1 = _layer_fwd(
        x[0], layers[0], 0, late=(_gather_plan(shards(0, LATE_WEIGHTS)), late_weights),
        comm_attn=_gather_plan(shards(1, LATE_WEIGHTS)), comm_ssm=_gather_plan(shards(1, ("w_in",))))
    layers[1] = {**layer(1, w_in1), **late_weights(late1)}
    xs, saved[1], _, _, _ = _layer_fwd(xs, layers[1], 1)
    dx, fin_sums = _final_loss(xs, final_g[None, :], loss_target[0])
    loss_part = jnp.sum(fin_sums[1])
    dx, grads[1], _, _ = _layer_bwd(dx, layers[1], saved[1], 1)
    by_dest1 = _by_destination(BIG_WEIGHTS, grads[1])
    dx, grads[0], scattered1, scattered0 = _layer_bwd(
        dx, layers[0], saved[0], 0,
        later=(_sibling_swap_plan(by_dest1),
               lambda swapped: _chip_scatter_plan(add_pairs(BIG_WEIGHTS, by_dest1, swapped, "l1"))),
        own=lambda g: _chip_scatter_plan(chip_sums_of(tuple(g), g, "l0_" + "_".join(g))))
    big = list(zip(scattered0, scattered1))
    grad_x = dx[None]

    small = _pack_small(loss_part, fin_sums[0], grads)
    dmod_rows = jnp.concatenate([grads[li]["dmod"] for li in range(DEPTH)]).reshape(-1, 128)
    small_sum, dmod_gathered = _allreduce_small(small, dmod_rows, name="allreduce_small")
    out = {}

    def adam(name, w, g_slabs, m, v):
        shp = w.shape
        r = int(math.prod(shp[:-1])) if len(shp) > 1 else 1
        w2, m2, v2 = (a.reshape(r, shp[-1]) for a in (w, m, v))
        gs = [g.reshape(g.shape[0], r // len(g_slabs), shp[-1]) for g in g_slabs]
        res = _adamw(w2, gs, m2, v2, name=f"adamw_{name}", tr=512 if shp[-1] >= 128 else 2048)
        out[name] = tuple(a.reshape(shp) for a in res)

    flat = small_sum.reshape(-1)
    shapes = [(128,), (d,)]
    for _ in range(DEPTH):
        shapes += [(d,), (N_HEADS,), (SSM_W,), (SSM_W,), (4, POOL_GW, POOL_GW), (POOL_W,), (3 * d,),
                   (SSM_GROUPS, SSM_STATE), (SSM_GROUPS, SSM_STATE), (SSM_GROUPS, SSM_STATE, SSM_GROUP),
                   (SSM_GROUPS, SSM_STATE, SSM_GROUP), (SSM_GROUPS, SSM_GROUP, SSM_STATE), (SSM_GROUPS, SSM_GROUP, SSM_STATE)]
    un = _unpack_small(flat, shapes)
    loss = un[0][0]
    g_final_g = un[1]
    per = 13
    gl = [un[2 + li * per: 2 + (li + 1) * per] for li in range(DEPTH)]
    st = lambda j: jnp.stack([gl[li][j] for li in range(DEPTH)])
    g_norm_g, g_sinks, g_ssm_d, g_b_glu, g_w_pool, g_pool_scale, g_b_ada = (st(j) for j in range(7))
    d_lr, d_li, d_bbr, d_bbi, g_c_re, g_c_im = (st(j) for j in range(7, 13))
    g_a_re, g_a_im, g_log_dt, g_b_re, g_b_im = disc_vjp((d_lr, d_li, d_bbr, d_bbi))

    dmod_all = lax.dynamic_slice(dmod_gathered.reshape(N_DEV, DEPTH, 3 * d), (0, 0, me * ada_w), (N_DEV, DEPTH, ada_w))
    dmod_all = dmod_all.transpose(1, 0, 2)
    g_w_ada = jnp.stack([_mm_tn(c_act, dmod_all[li], tm=d, tn=ada_w, tk=N_DEV, name=f"dw_ada_l{li}") for li in range(DEPTH)])

    adam("w_ada", w_ada, [g_w_ada[None]], m_w_ada, v_w_ada)
    adam("w_in", w_in, big[0], m_w_in, v_w_in)
    adam("w_glu", w_glu, big[1], m_w_glu, v_w_glu)
    adam("w_br_att", w_br_att, big[2], m_w_br_att, v_w_br_att)
    adam("w_br_ssm", w_br_ssm, big[3], m_w_br_ssm, v_w_br_ssm)
    adam("w_br_pool", w_br_pool, big[4], m_w_br_pool, v_w_br_pool)
    adam("w_out", w_out, big[5], m_w_out, v_w_out)

    small_names = ["norm_g", "b_ada", "attn_sinks", "ssm_a_re", "ssm_a_im", "ssm_log_dt", "ssm_b_re", "ssm_b_im",
                   "ssm_c_re", "ssm_c_im", "ssm_d", "b_glu", "w_pool", "pool_scale", "final_g"]
    small_w = [norm_g, b_ada, attn_sinks, ssm_a_re, ssm_a_im, ssm_log_dt, ssm_b_re, ssm_b_im, ssm_c_re, ssm_c_im,
               ssm_d, b_glu, w_pool, pool_scale, final_g]
    small_m = [m_norm_g, m_b_ada, m_attn_sinks, m_ssm_a_re, m_ssm_a_im, m_ssm_log_dt, m_ssm_b_re, m_ssm_b_im,
               m_ssm_c_re, m_ssm_c_im, m_ssm_d, m_b_glu, m_w_pool, m_pool_scale, m_final_g]
    small_v = [v_norm_g, v_b_ada, v_attn_sinks, v_ssm_a_re, v_ssm_a_im, v_ssm_log_dt, v_ssm_b_re, v_ssm_b_im,
               v_ssm_c_re, v_ssm_c_im, v_ssm_d, v_b_glu, v_w_pool, v_pool_scale, v_final_g]
    small_g = [g_norm_g, g_b_ada, g_sinks, g_a_re, g_a_im, g_log_dt, g_b_re, g_b_im, g_c_re, g_c_im,
               g_ssm_d, g_b_glu, g_w_pool, g_pool_scale, g_final_g]

    for nm, w, g, m, v in zip(small_names, small_w, small_g, small_m, small_v):
        adam(nm, w, [g[None]], m, v)

    order = ["norm_g", "w_ada", "b_ada", "w_in", "attn_sinks", "ssm_a_re", "ssm_a_im", "ssm_log_dt", "ssm_b_re",
             "ssm_b_im", "ssm_c_re", "ssm_c_im", "ssm_d", "w_glu", "b_glu", "w_pool", "pool_scale", "w_br_att",
             "w_br_ssm", "w_br_pool", "w_out", "final_g"]
    return (loss, grad_x, *[out[k][0] for k in order], *[out[k][1] for k in order],
            *[out[k][2] for k in order], *[out[k][3] for k in order])
```

```python
import functools
import math

import jax
import jax.numpy as jnp
from jax import lax
from jax.experimental import pallas as pl
from jax.experimental.pallas import tpu as pltpu

F32 = jnp.float32
BF16 = jnp.bfloat16

N_DEV = 8
D_MODEL = 1024
DEPTH = 2
CHUNK = 64
N_HEADS = 8
N_KV_HEADS = 2
HEAD_DIM = 64
Q_PER_KV = N_HEADS // N_KV_HEADS
WINDOW = 128
ATT_W = 512
KV_W = 128
SSM_W = 512
SSM_GROUP = 16
SSM_GROUPS = 32
SSM_STATE = 64
SSM_N = SSM_GROUPS * SSM_STATE
POOL_W = 512
POOL_WINDOWS = (2, 4, 8, 16)
POOL_GW = 128
POOL_HALO = 16
EPS = 1e-6
NEG_INF = -1e30
ADAM_LR = 0.001
ADAM_B1 = 0.9
ADAM_B2 = 0.999
ADAM_EPS = 1e-08
ADAM_WD = 0.01
ADAM_STEP = 10

SEQ_BLOCK = 256
POOL_BLOCK = 512
ATT_BLOCK = 128
VMEM_LIMIT = 56 * 1024 * 1024

NN = (((1,), (0,)), ((), ()))
NT = (((1,), (1,)), ((), ()))
TN = (((0,), (0,)), ((), ()))


def _dot(a, b, dims=NN):
    return lax.dot_general(a.astype(BF16), b.astype(BF16), dims, preferred_element_type=F32)


def _params(*sem):
    return pltpu.CompilerParams(dimension_semantics=sem, vmem_limit_bytes=VMEM_LIMIT)


def _sigmoid(x):
    return 0.5 + 0.5 * jnp.tanh(0.5 * x)


def _silu_and_grad(z):
    s = _sigmoid(z)
    return z * s, s * (1.0 + z * (1.0 - s))


_GELU_K = math.sqrt(2.0 / math.pi)


def _gelu_and_grad(x):
    inner = _GELU_K * (x + 0.044715 * x * x * x)
    t = jnp.tanh(inner)
    val = 0.5 * x * (1.0 + t)
    grad = 0.5 * (1.0 + t) + 0.5 * x * (1.0 - t * t) * _GELU_K * (1.0 + 3.0 * 0.044715 * x * x)
    return val, grad


def _grouped_pieces(groups):
    out = []
    for ranges in groups:
        off, pieces = 0, []
        for lo, hi in ranges:
            pieces.append((off, lo, hi))
            off += hi - lo
        out.append(pieces)
    return out


def _mm_tn(a, b, *, out_dtype=F32, tm=1024, tn=1024, tk=1024, name, comm=None):
    k, m = a.shape
    n = b.shape[1]
    assert m % min(tm, m) == 0 and n % min(tn, n) == 0 and k % min(tk, k) == 0
    tm, tn, tk = min(tm, m), min(tn, n), min(tk, k)
    nk = k // tk
    grid = (m // tm, n // tn, nk)
    c_args, c_in, c_out, c_shape, c_scratch = _comm_extra(comm)

    def body(a_ref, b_ref, o_ref, acc_ref):
        kk = pl.program_id(2)

        @pl.when(kk == 0)
        def _():
            acc_ref[...] = jnp.zeros_like(acc_ref)

        acc_ref[...] += _dot(a_ref[...], b_ref[...], TN)

        @pl.when(kk == nk - 1)
        def _():
            o_ref[...] = acc_ref[...].astype(out_dtype)

    res = pl.pallas_call(
        _with_comm(body, comm, 2, 1, grid, -1), grid=grid,
        in_specs=[pl.BlockSpec((tk, tm), lambda i, j, kk: (kk, i)), pl.BlockSpec((tk, tn), lambda i, j, kk: (kk, j))] + c_in,
        out_specs=[pl.BlockSpec((tm, tn), lambda i, j, kk: (i, j))] + c_out,
        out_shape=[jax.ShapeDtypeStruct((m, n), out_dtype)] + c_shape,
        scratch_shapes=[pltpu.VMEM((tm, tn), F32)] + c_scratch,
        compiler_params=_params(*(("arbitrary",) * 3 if comm else ("parallel", "parallel", "arbitrary"))),
        name=name)(a, b, *c_args)
    return (res[0], list(res[1:])) if comm else res[0]


def _mm_tn_grouped(a, bs, groups, *, tm=512, tk=512, name, comm=None):
    k, m = a.shape
    tm, tk = min(tm, m), min(tk, k)
    assert m % tm == 0 and k % tk == 0
    nk, nb = k // tk, len(bs)
    n = sum(b.shape[1] for b in bs)
    ns = n // N_DEV
    pieces = []
    for plist in _grouped_pieces(groups):
        sub = []
        for off, lo, hi in plist:
            pos = lo
            while pos < hi:
                s = pos // ns
                end = min(hi, (s + 1) * ns)
                sub.append((s, pos - s * ns, end - s * ns, off + pos - lo))
                pos = end
        pieces.append(sub)
    grid = (m // tm, nk)
    c_args, c_in, c_out, c_shape, c_scratch = _comm_extra(comm)

    def body(a_ref, *refs):
        b_refs, o_ref, acc_refs = refs[:nb], refs[nb], refs[nb + 1:]
        kk = pl.program_id(1)
        av = a_ref[...]
        for b_ref, acc_ref, plist in zip(b_refs, acc_refs, pieces):
            @pl.when(kk == 0)
            def _():
                acc_ref[...] = jnp.zeros_like(acc_ref)

            acc_ref[...] += _dot(av, b_ref[...], TN)

            @pl.when(kk == nk - 1)
            def _():
                for s, c0, c1, off in plist:
                    o_ref[s % 2, s // 2, :, c0:c1] = acc_ref[:, off:off + c1 - c0].astype(BF16)

    res = pl.pallas_call(
        _with_comm(body, comm, 1 + nb, 1, grid, -1), grid=grid,
        in_specs=[pl.BlockSpec((tk, tm), lambda i, kk: (kk, i))]
                 + [pl.BlockSpec((tk, b.shape[1]), lambda i, kk: (kk, 0)) for b in bs] + c_in,
        out_specs=[pl.BlockSpec((2, N_DEV // 2, tm, ns), lambda i, kk: (0, 0, i, 0))] + c_out,
        out_shape=[jax.ShapeDtypeStruct((2, N_DEV // 2, m, ns), BF16)] + c_shape,
        scratch_shapes=[pltpu.VMEM((tm, b.shape[1]), F32) for b in bs] + c_scratch,
        compiler_params=_params("arbitrary", "arbitrary"), name=name)(a, *bs, *c_args)
    return res[0], list(res[1:])


def _ln_proj(x, g, shift, scale, w, groups, *, name, tm=512, comm=None):
    l, d = x.shape
    tm = min(tm, l)
    nb = l // tm
    pieces = _grouped_pieces(groups)
    widths = [sum(hi - lo for _, lo, hi in plist) for plist in pieces]
    nw = len(pieces)
    c_args, c_in, c_out, c_shape, c_scratch = _comm_extra(comm)

    def body(x_ref, g_ref, sh_ref, sc_ref, w_ref, h_ref, *p_refs):
        xv = x_ref[...]
        n = xv * lax.rsqrt(jnp.mean(xv * xv, axis=-1, keepdims=True) + EPS)
        h = ((n * g_ref[...]) * (1.0 + sc_ref[...]) + sh_ref[...]).astype(BF16)
        h_ref[...] = h
        for p_ref, plist in zip(p_refs, pieces):
            for off, lo, hi in plist:
                p_ref[:, off:off + hi - lo] = _dot(h, w_ref[:, lo:hi]).astype(BF16)

    vec = pl.BlockSpec((1, d), lambda i: (0, 0))
    row = lambda n: pl.BlockSpec((tm, n), lambda i: (i, 0))
    res = pl.pallas_call(
        _with_comm(body, comm, 5, 1 + nw, nb, -2), grid=(nb,),
        in_specs=[row(d), vec, vec, vec, pl.BlockSpec(w.shape, lambda i: (0, 0))] + c_in,
        out_specs=[row(d)] + [row(n) for n in widths] + c_out,
        out_shape=[jax.ShapeDtypeStruct((l, d), BF16)] + [jax.ShapeDtypeStruct((l, n), BF16) for n in widths] + c_shape,
        scratch_shapes=c_scratch,
        compiler_params=_params("arbitrary"), name=name)(x, g, shift, scale, w, *c_args)
    return res[0], list(res[1:1 + nw]), list(res[1 + nw:])


def _ln_proj_bwd(ds, w, groups, x, dres, g, scale, *, name, tm=256, comm=None):
    l, d = x.shape
    tm = min(tm, l)
    nb = l // tm
    nd = len(ds)
    pieces = _grouped_pieces(groups)
    c_args, c_in, c_out, c_shape, c_scratch = _comm_extra(comm)

    def body(*refs):
        d_refs = refs[:nd]
        w_ref, x_ref, dres_ref, g_ref, sc_ref, dx_ref, sums_ref = refs[nd:]
        dhv = None
        for d_ref, plist in zip(d_refs, pieces):
            for off, lo, hi in plist:
                term = _dot(d_ref[:, off:off + hi - lo], w_ref[:, lo:hi], NT)
                dhv = term if dhv is None else dhv + term
        xv = x_ref[...]
        rstd = lax.rsqrt(jnp.mean(xv * xv, axis=-1, keepdims=True) + EPS)
        n = xv * rstd
        gv = g_ref[...]
        dr = dhv * (1.0 + sc_ref[...])
        dn = dr * gv
        dx_ref[...] = dres_ref[...] + rstd * (dn - n * jnp.mean(dn * n, axis=-1, keepdims=True))

        @pl.when(pl.program_id(0) == 0)
        def _():
            sums_ref[...] = jnp.zeros_like(sums_ref)

        sums_ref[0:1, :] += jnp.sum(dhv, axis=0, keepdims=True)
        sums_ref[1:2, :] += jnp.sum(dhv * (n * gv), axis=0, keepdims=True)
        sums_ref[2:3, :] += jnp.sum(dr * n, axis=0, keepdims=True)

    vec = pl.BlockSpec((1, d), lambda i: (0, 0))
    row = pl.BlockSpec((tm, d), lambda i: (i, 0))
    res = pl.pallas_call(
        _with_comm(body, comm, nd + 5, 2, nb, -1), grid=(nb,),
        in_specs=[pl.BlockSpec((tm, a.shape[1]), lambda i: (i, 0)) for a in ds]
                 + [pl.BlockSpec(w.shape, lambda i: (0, 0)), row, row, vec, vec] + c_in,
        out_specs=[row, pl.BlockSpec((8, d), lambda i: (0, 0))] + c_out,
        out_shape=[jax.ShapeDtypeStruct((l, d), F32), jax.ShapeDtypeStruct((8, d), F32)] + c_shape,
        scratch_shapes=c_scratch,
        compiler_params=_params("arbitrary"), name=name)(*ds, w, x, dres, g, scale, *c_args)
    return res[0], res[1], list(res[2:])


def _final_loss(x, g, target, *, tm=512):
    l, d = x.shape

    def body(x_ref, g_ref, t_ref, dx_ref, sums_ref):
        xv = x_ref[...]
        rstd = lax.rsqrt(jnp.mean(xv * xv, axis=-1, keepdims=True) + EPS)
        n = xv * rstd
        gv = g_ref[...]
        err = n * gv - t_ref[...]
        dy = err * (1.0 / d)
        dn = dy * gv
        dx_ref[...] = rstd * (dn - n * jnp.mean(dn * n, axis=-1, keepdims=True))

        @pl.when(pl.program_id(0) == 0)
        def _():
            sums_ref[...] = jnp.zeros_like(sums_ref)

        sums_ref[0:1, :] += jnp.sum(dy * n, axis=0, keepdims=True)
        sums_ref[1:2, :] += jnp.sum(err * err, axis=0, keepdims=True) * (0.5 / d)

    vec = pl.BlockSpec((1, d), lambda i: (0, 0))
    row = pl.BlockSpec((tm, d), lambda i: (i, 0))
    dx, sums = pl.pallas_call(
        body, grid=(l // tm,),
        in_specs=[row, vec, row],
        out_specs=[row, pl.BlockSpec((8, d), lambda i: (0, 0))],
        out_shape=[jax.ShapeDtypeStruct((l, d), F32), jax.ShapeDtypeStruct((8, d), F32)],
        compiler_params=_params("arbitrary"), name="final_loss")(x, g, target)
    return dx, sums


def _attn_geometry(i, t):
    nk = t + WINDOW
    qi = lax.broadcasted_iota(jnp.int32, (t, nk), 0)
    kj = lax.broadcasted_iota(jnp.int32, (t, nk), 1)
    dist = jnp.abs(qi + WINDOW - kj).astype(F32)
    qc = jnp.right_shift(qi, 6)
    kc = jnp.right_shift(kj, 6)
    valid = (kc >= qc) & (kc <= qc + WINDOW // CHUNK) & ((i > 0) | (kj >= WINDOW))
    return dist, valid


def _attn_head(q, k_all, v_all, sink, slope, dist, valid):
    s = _dot(q, k_all, NT) * (1.0 / math.sqrt(HEAD_DIM)) - slope * dist
    s = jnp.where(valid, s, NEG_INF)
    m = jnp.maximum(jnp.max(s, axis=-1, keepdims=True), sink)
    e = jnp.exp(s - m)
    es = jnp.exp(sink - m)
    inv = 1.0 / (jnp.sum(e, axis=-1, keepdims=True) + es)
    p = e * inv
    o = _dot(p, v_all, NN)
    return p, o, es * inv


def _attn_specs(t):
    cur = pl.BlockSpec((t, ATT_W * 2 + KV_W * 2), lambda i: (i, 0))
    halo_blocks = t // WINDOW
    prev = pl.BlockSpec((WINDOW, 2 * KV_W), lambda i: (jnp.maximum(i * halo_blocks - 1, 0), (2 * ATT_W) // (2 * KV_W)))
    return cur, prev


def _attn_fwd(pa, sinks, *, name, t=ATT_BLOCK, comm=None):
    l = pa.shape[0]
    t = min(t, l)
    nb = l // t
    c_args, c_in, c_out, c_shape, c_scratch = _comm_extra(comm)

    def body(sink_ref, cur_ref, prev_ref, ya_ref):
        i = pl.program_id(0)
        dist, valid = _attn_geometry(i, t)
        for h in range(N_HEADS):
            kh = h // Q_PER_KV
            q = cur_ref[:, h * HEAD_DIM:(h + 1) * HEAD_DIM]
            z = cur_ref[:, ATT_W + h * HEAD_DIM:ATT_W + (h + 1) * HEAD_DIM].astype(F32)
            k_all = jnp.concatenate([prev_ref[:, kh * HEAD_DIM:(kh + 1) * HEAD_DIM],
                                     cur_ref[:, 2 * ATT_W + kh * HEAD_DIM:2 * ATT_W + (kh + 1) * HEAD_DIM]], axis=0)
            v_all = jnp.concatenate([prev_ref[:, KV_W + kh * HEAD_DIM:KV_W + (kh + 1) * HEAD_DIM],
                                     cur_ref[:, 2 * ATT_W + KV_W + kh * HEAD_DIM:2 * ATT_W + KV_W + (kh + 1) * HEAD_DIM]], axis=0)
            _, o, _ = _attn_head(q, k_all, v_all, sink_ref[h], 2.0 ** (-(h + 1)), dist, valid)
            sz, _ = _silu_and_grad(z)
            ya_ref[:, h * HEAD_DIM:(h + 1) * HEAD_DIM] = (o * sz).astype(BF16)

    cur, prev = _attn_specs(t)
    res = pl.pallas_call(
        _with_comm(body, comm, 3, 1, nb, nb - 4), grid=(nb,),
        in_specs=[pl.BlockSpec(memory_space=pltpu.SMEM), cur, prev] + c_in,
        out_specs=[pl.BlockSpec((t, ATT_W), lambda i: (i, 0))] + c_out,
        out_shape=[jax.ShapeDtypeStruct((l, ATT_W), BF16)] + c_shape,
        scratch_shapes=c_scratch,
        compiler_params=_params("arbitrary"), name=name)(sinks, pa, pa, *c_args)
    return res[0], res[1:]


def _attn_bwd(pa, sinks, dya, *, name, t=SEQ_BLOCK):
    l = pa.shape[0]
    t = min(t, l)
    nb = l // t
    scale = 1.0 / math.sqrt(HEAD_DIM)

    def body(sink_ref, cur_ref, prev_ref, dya_ref, dpa_ref, dsink_ref, carry_ref):
        n = pl.program_id(0)
        i = nb - 1 - n
        dist, valid = _attn_geometry(i, t)

        @pl.when(n == 0)
        def _():
            carry_ref[...] = jnp.zeros_like(carry_ref)
            dsink_ref[...] = jnp.zeros_like(dsink_ref)

        dk_acc = [jnp.zeros((HEAD_DIM, t + WINDOW), F32) for _ in range(N_KV_HEADS)]
        dv_acc = [jnp.zeros((HEAD_DIM, t + WINDOW), F32) for _ in range(N_KV_HEADS)]
        for h in range(N_HEADS):
            kh = h // Q_PER_KV
            q = cur_ref[:, h * HEAD_DIM:(h + 1) * HEAD_DIM]
            z = cur_ref[:, ATT_W + h * HEAD_DIM:ATT_W + (h + 1) * HEAD_DIM].astype(F32)
            k_all = jnp.concatenate([prev_ref[:, kh * HEAD_DIM:(kh + 1) * HEAD_DIM],
                                     cur_ref[:, 2 * ATT_W + kh * HEAD_DIM:2 * ATT_W + (kh + 1) * HEAD_DIM]], axis=0)
            v_all = jnp.concatenate([prev_ref[:, KV_W + kh * HEAD_DIM:KV_W + (kh + 1) * HEAD_DIM],
                                     cur_ref[:, 2 * ATT_W + KV_W + kh * HEAD_DIM:2 * ATT_W + KV_W + (kh + 1) * HEAD_DIM]], axis=0)
            p, o, p_sink = _attn_head(q, k_all, v_all, sink_ref[h], 2.0 ** (-(h + 1)), dist, valid)
            dy = dya_ref[:, h * HEAD_DIM:(h + 1) * HEAD_DIM]
            sz, dsz = _silu_and_grad(z)
            do = dy * sz
            dpa_ref[:, ATT_W + h * HEAD_DIM:ATT_W + (h + 1) * HEAD_DIM] = (dy * o * dsz).astype(BF16)
            dp = _dot(do, v_all, NT)
            delta = jnp.sum(p * dp, axis=-1, keepdims=True)
            ds = p * (dp - delta)
            dpa_ref[:, h * HEAD_DIM:(h + 1) * HEAD_DIM] = (_dot(ds, k_all, NN) * scale).astype(BF16)
            dk_acc[kh] = dk_acc[kh] + _dot(q, ds, TN) * scale
            dv_acc[kh] = dv_acc[kh] + _dot(do, p, TN)
            dsink_ref[h:h + 1, :] += jnp.broadcast_to(-jnp.sum(p_sink * delta, axis=0, keepdims=True), (1, 128))

        acc = jnp.concatenate(dk_acc + dv_acc, axis=0).T
        own = acc[WINDOW:, :]
        tail = own[t - WINDOW:, :] + carry_ref[...]
        if t > WINDOW:
            dpa_ref[0:t - WINDOW, 2 * ATT_W:] = own[:t - WINDOW, :].astype(BF16)
        dpa_ref[t - WINDOW:t, 2 * ATT_W:] = tail.astype(BF16)
        carry_ref[...] = acc[:WINDOW, :]

    halo_blocks = t // WINDOW
    wpa = 2 * ATT_W + 2 * KV_W
    cur = pl.BlockSpec((t, wpa), lambda n: (nb - 1 - n, 0))
    prev = pl.BlockSpec((WINDOW, 2 * KV_W),
                        lambda n: (jnp.maximum((nb - 1 - n) * halo_blocks - 1, 0), (2 * ATT_W) // (2 * KV_W)))
    return pl.pallas_call(
        body, grid=(nb,),
        in_specs=[pl.BlockSpec(memory_space=pltpu.SMEM), cur, prev, pl.BlockSpec((t, ATT_W), lambda n: (nb - 1 - n, 0))],
        out_specs=[pl.BlockSpec((t, wpa), lambda n: (nb - 1 - n, 0)), pl.BlockSpec((8, 128), lambda n: (0, 0))],
        out_shape=[jax.ShapeDtypeStruct((l, wpa), BF16), jax.ShapeDtypeStruct((8, 128), F32)],
        scratch_shapes=[pltpu.VMEM((WINDOW, 2 * KV_W), F32)],
        compiler_params=_params("arbitrary"), name=name)(sinks, pa, pa, dya)


def _scan(xr, xi, lr, li, t, reverse):
    row = lax.broadcasted_iota(jnp.int32, (t, 1), 0)
    d = 1
    pr, pi = lr, li
    while d < t:
        if reverse:
            sr = jnp.where(row < t - d, pltpu.roll(xr, t - d, 0), 0.0)
            si = jnp.where(row < t - d, pltpu.roll(xi, t - d, 0), 0.0)
        else:
            sr = jnp.where(row >= d, pltpu.roll(xr, d, 0), 0.0)
            si = jnp.where(row >= d, pltpu.roll(xi, d, 0), 0.0)
        xr, xi = xr + pr * sr - pi * si, xi + pr * si + pi * sr
        pr, pi = pr * pr - pi * pi, 2.0 * pr * pi
        d *= 2
    return xr, xi


SCAN_SUB = 8


def _split_hi_lo(a):
    hi = a.astype(BF16)
    lo = (a - hi.astype(F32)).astype(BF16)
    return jnp.concatenate([hi, lo], axis=0)


def _scan_mxu(xr, xi, tab, lam3, lam8, tri, expand, cr, ci, t, reverse):
    ns = t // SCAN_SUB
    n = xr.shape[1]
    v3 = lambda a: a.reshape(ns, SCAN_SUB, n)
    x3r, x3i = v3(xr), v3(xi)
    br = (x3r * tab[0] - x3i * tab[1]).reshape(t, n)
    bi = (x3r * tab[1] + x3i * tab[0]).reshape(t, n)
    pm = jnp.dot(tri, jnp.concatenate([br, bi], axis=1).astype(BF16), preferred_element_type=F32)
    p3r, p3i = v3(pm[:t, :n]), v3(pm[:t, n:])
    slr = p3r * tab[2] - p3i * tab[3]
    sli = p3r * tab[3] + p3i * tab[2]
    totr, toti = pm[t:, :n], pm[t:, n:]
    l3r, l3i = lam3
    l8r, l8i = lam8
    row = lax.broadcasted_iota(jnp.int32, (ns, 1), 0)
    edge = row == (ns - 1 if reverse else 0)
    er = totr * l3r - toti * l3i + jnp.where(edge, l8r * cr - l8i * ci, 0.0)
    ei = totr * l3i + toti * l3r + jnp.where(edge, l8r * ci + l8i * cr, 0.0)
    er, ei = _scan(er, ei, l8r, l8i, ns, reverse)
    shift = ns - 1 if reverse else 1
    nbr = jnp.where(edge, cr, pltpu.roll(er, shift, 0))
    nbi = jnp.where(edge, ci, pltpu.roll(ei, shift, 0))
    ex = jnp.dot(expand, _split_hi_lo(jnp.concatenate([nbr, nbi], axis=1)), preferred_element_type=F32)
    e3r, e3i = v3(ex[:, :n]), v3(ex[:, n:])
    sr = (slr + e3r * tab[4] - e3i * tab[5]).reshape(t, n)
    si = (sli + e3r * tab[5] + e3i * tab[4]).reshape(t, n)
    out = 0 if reverse else ns - 1
    return sr, si, er[out:out + 1, :], ei[out:out + 1, :]


def _scan_consts(t):
    import numpy as np
    ns = t // SCAN_SUB
    r = np.arange(t)
    same = (r[:, None] // SCAN_SUB) == (r[None, :] // SCAN_SUB)
    sums = (np.arange(ns)[:, None] == (r[None, :] // SCAN_SUB))
    tri = []
    for keep in (r[None, :] <= r[:, None], r[None, :] >= r[:, None]):
        tri.append(np.concatenate([same & keep, sums], axis=0).astype(np.float32))
    ex = ((r[:, None] // SCAN_SUB) == np.arange(ns)[None, :]).astype(np.float32)
    return jnp.asarray(np.stack(tri), BF16), jnp.asarray(np.concatenate([ex, ex], axis=1), BF16)


def _scan_tables(lr, li):
    import numpy as np
    den = lr * lr + li * li
    ir, ii = lr / den, -li / den
    mul = lambda a, b: (a[0] * b[0] - a[1] * b[1], a[0] * b[1] + a[1] * b[0])
    pw = {0: (jnp.ones_like(lr), jnp.zeros_like(lr))}
    for e in range(1, 9):
        pw[e] = mul(pw[e - 1], (lr, li))
    for e in range(-1, -5, -1):
        pw[e] = mul(pw[e + 1], (ir, ii))
    powers = jnp.stack([jnp.stack(pw[e]) for e in range(-4, 9)] + [jnp.zeros((2, lr.shape[0]), F32)])
    j = np.arange(SCAN_SUB)
    exps = [4 - j, j - 4, j + 1, j - 3, 3 - j, 8 - j]
    e_idx = np.stack([exps[t] + 4 for t in range(6) for _ in range(2)])
    c_idx = np.tile(np.array([0, 1])[:, None], (6, SCAN_SUB))
    sign = np.where((c_idx == 1) & (np.arange(12)[:, None] >= 6), -1.0, 1.0).astype(np.float32)
    tabs = powers[e_idx, c_idx] * sign[:, :, None]
    lam = powers[np.array([5, 5, 7, 7, 12, 12, 13, 13]), np.array([0, 1, 0, 1, 0, 1, 0, 0])]
    return lam, tabs


SSM_HALVES = 2
SSM_HW = SSM_W // SSM_HALVES
SSM_HN = SSM_N // SSM_HALVES


def _bd_nn(x, w):
    a = w.shape[1]
    return jnp.concatenate([_dot(x[:, h * a:(h + 1) * a], w[h]) for h in range(SSM_HALVES)], axis=1)


def _bd_nt(x, w):
    b = w.shape[2]
    return jnp.concatenate([_dot(x[:, h * b:(h + 1) * b], w[h], NT) for h in range(SSM_HALVES)], axis=1)


def _bd_tn(x, y):
    a, b = x.shape[1] // SSM_HALVES, y.shape[1] // SSM_HALVES
    return jnp.stack([_dot(x[:, h * a:(h + 1) * a], y[:, h * b:(h + 1) * b], TN) for h in range(SSM_HALVES)])


def _ssm_states(u, s0r, s0i, lam_ref, tab_ref, tri_ref, ex_ref, bre, bim, t):
    tab = tuple(tab_ref[k] for k in range(6))
    return _scan_mxu(_bd_nn(u, bre), _bd_nn(u, bim), tab, (lam_ref[2:3, :], lam_ref[3:4, :]),
                     (lam_ref[4:5, :], lam_ref[5:6, :]), tri_ref[0], ex_ref[...], s0r, s0i, t, False)


def _ssm_head(u, z, xr, xi, cre, cim, dskip, wglu, bglu):
    y = _bd_nn(xr, cre) - _bd_nn(xi, cim) + dskip * u
    y2, dgelu = _gelu_and_grad(y)
    gate = _sigmoid(_dot(y2, wglu) + bglu)
    y3 = y2 * gate
    return y2, dgelu, gate, y3


def _with_comm(body, comm, n_in, n_out, grid, mid_step):
    if comm is None:
        return body
    nc = len(comm["args"])
    n_sem = len(comm["scratch"])
    grid = (grid,) if isinstance(grid, int) else tuple(grid)
    total = math.prod(grid)

    def hosted(*refs):
        ins, cin = refs[:n_in], refs[n_in:n_in + nc]
        outs, cout = refs[n_in + nc:n_in + nc + n_out], refs[n_in + nc + n_out:n_in + 2 * nc + n_out]
        rest = refs[n_in + 2 * nc + n_out:]
        scratch, csem = rest[:len(rest) - n_sem], rest[len(rest) - n_sem:]
        start, forward, finish = comm["phases"](cin, cout, *csem)
        step = pl.program_id(0)
        for axis in range(1, len(grid)):
            step = step * grid[axis] + pl.program_id(axis)
        pl.when(step == 0)(start)
        pl.when(step == (mid_step if mid_step >= 0 else total + mid_step))(forward)
        body(*ins, *outs, *scratch)
        pl.when(step == total - 1)(finish)

    return hosted


def _comm_extra(comm):
    if comm is None:
        return [], [], [], [], []
    anyspec = pl.BlockSpec(memory_space=pl.ANY)
    nc = len(comm["args"])
    return comm["args"], [anyspec] * nc, [anyspec] * nc, comm["out_shape"], comm["scratch"]


def _ssm_fwd(ps, scan_ops, bblk, cblk, dskip, wglu, bglu, *, name, t=SEQ_BLOCK, comm=None):
    l = ps.shape[0]
    assert l % t == 0
    nb = l // t
    ns = t // SCAN_SUB
    c_args, c_in, c_out, c_shape, c_scratch = _comm_extra(comm)

    def body(ps_ref, lam_ref, tab_ref, tri_ref, ex_ref, b_ref, c_ref, d_ref, w_ref, bg_ref, ys_ref, chk_ref, xs_ref,
             st_ref):
        @pl.when(pl.program_id(0) == 0)
        def _():
            st_ref[...] = jnp.zeros_like(st_ref)

        chk_ref[...] = jnp.broadcast_to(st_ref[...], chk_ref.shape)
        u = ps_ref[:, :SSM_W].astype(F32)
        z = ps_ref[:, SSM_W:].astype(F32)
        xr, xi, er, ei = _ssm_states(u, st_ref[:, :SSM_N], st_ref[:, SSM_N:], lam_ref, tab_ref, tri_ref, ex_ref,
                                     b_ref[0], b_ref[1], t)
        st_ref[:, :SSM_N] = er
        st_ref[:, SSM_N:] = ei
        xr, xi = xr.astype(BF16), xi.astype(BF16)
        xs_ref[:, :SSM_N] = xr
        xs_ref[:, SSM_N:] = xi
        _, _, _, y3 = _ssm_head(u, z, xr, xi, c_ref[0], c_ref[1], d_ref[...], w_ref[...], bg_ref[...])
        sz, _ = _silu_and_grad(z)
        ys_ref[...] = (y3 * sz).astype(BF16)

    full = lambda shape: pl.BlockSpec(shape, lambda i: (0,) * len(shape))
    return pl.pallas_call(
        _with_comm(body, comm, 10, 3, nb, nb - 1), grid=(nb,),
        in_specs=[pl.BlockSpec((t, 2 * SSM_W), lambda i: (i, 0)), full((8, SSM_N)), full((12, SCAN_SUB, SSM_N)),
                  full((2, t + ns, t)), full((t, 2 * ns)), full((2, SSM_HALVES, SSM_HW, SSM_HN)),
                  full((2, SSM_HALVES, SSM_HN, SSM_HW)), full((1, SSM_W)), full((SSM_W, SSM_W)), full((1, SSM_W))] + c_in,
        out_specs=[pl.BlockSpec((t, SSM_W), lambda i: (i, 0)), pl.BlockSpec((8, 2 * SSM_N), lambda i: (i, 0)),
                   pl.BlockSpec((t, 2 * SSM_N), lambda i: (i, 0))] + c_out,
        out_shape=[jax.ShapeDtypeStruct((l, SSM_W), BF16), jax.ShapeDtypeStruct((nb * 8, 2 * SSM_N), F32),
                   jax.ShapeDtypeStruct((l, 2 * SSM_N), BF16)] + c_shape,
        scratch_shapes=[pltpu.VMEM((1, 2 * SSM_N), F32)] + c_scratch,
        compiler_params=_params("arbitrary"), name=name)(ps, *scan_ops, bblk, cblk, dskip, wglu, bglu, *c_args)


def _ssm_bwd(ps, dys, chk, states, scan_ops, bblk, cblk, dskip, wglu, bglu, *, name, t=SEQ_BLOCK, comm=None):
    l = ps.shape[0]
    assert l % t == 0
    nb = l // t
    ns = t // SCAN_SUB
    c_args, c_in, c_out, c_shape, c_scratch = _comm_extra(comm)

    def body(ps_ref, dys_ref, chk_ref, xs_ref, lam_ref, tab_ref, tri_ref, ex_ref, b_ref, c_ref, d_ref, w_ref, bg_ref,
             dps_ref, db_ref, dc_ref, dw_acc, sums_acc, gc_ref, db_acc, dc_acc):
        n = pl.program_id(0)

        @pl.when(n == 0)
        def _():
            gc_ref[...] = jnp.zeros_like(gc_ref)
            db_acc[...] = jnp.zeros_like(db_acc)
            dc_acc[...] = jnp.zeros_like(dc_acc)
            dw_acc[...] = jnp.zeros_like(dw_acc)
            sums_acc[...] = jnp.zeros_like(sums_acc)

        row = lax.broadcasted_iota(jnp.int32, (t, 1), 0)
        u = ps_ref[:, :SSM_W].astype(F32)
        z = ps_ref[:, SSM_W:].astype(F32)
        s0r, s0i = chk_ref[0:1, :SSM_N], chk_ref[0:1, SSM_N:]
        xr, xi = xs_ref[:, :SSM_N], xs_ref[:, SSM_N:]
        dskip = d_ref[...]
        y2, dgelu, gate, y3 = _ssm_head(u, z, xr, xi, c_ref[0], c_ref[1], dskip, w_ref[...], bg_ref[...])
        sz, dsz = _silu_and_grad(z)
        dys_v = dys_ref[...]
        dps_ref[:, SSM_W:] = (dys_v * y3 * dsz).astype(BF16)
        dy3 = dys_v * sz
        da = dy3 * y2 * gate * (1.0 - gate)
        dy2 = dy3 * gate + _dot(da, w_ref[...], NT)
        dw_acc[...] += _dot(y2, da, TN)
        dy = dy2 * dgelu
        sums_acc[2:3, :SSM_W] += jnp.sum(dy * u, axis=0, keepdims=True)
        sums_acc[3:4, :SSM_W] += jnp.sum(da, axis=0, keepdims=True)
        dc_acc[0] += _bd_tn(dy, xr)
        dc_acc[1] += -_bd_tn(dy, xi)
        rev_tab = tuple(tab_ref[k] for k in range(6, 12))
        gr, gi, gcr, gci = _scan_mxu(
            _bd_nt(dy, c_ref[0]), -_bd_nt(dy, c_ref[1]), rev_tab, (lam_ref[2:3, :], -lam_ref[3:4, :]),
            (lam_ref[4:5, :], -lam_ref[5:6, :]), tri_ref[1], ex_ref[...], gc_ref[:, :SSM_N], gc_ref[:, SSM_N:], t, True)
        gc_ref[:, :SSM_N] = gcr
        gc_ref[:, SSM_N:] = gci
        db_acc[0] += _bd_tn(u, gr)
        db_acc[1] += _bd_tn(u, gi)
        du = dskip * dy + _bd_nt(gr, b_ref[0]) + _bd_nt(gi, b_ref[1])
        dps_ref[:, :SSM_W] = du.astype(BF16)
        spr = jnp.where(row == 0, s0r, pltpu.roll(xr.astype(F32), 1, 0))
        spi = jnp.where(row == 0, s0i, pltpu.roll(xi.astype(F32), 1, 0))
        sums_acc[0:1, :] += jnp.sum(gr * spr + gi * spi, axis=0, keepdims=True)
        sums_acc[1:2, :] += jnp.sum(gi * spr - gr * spi, axis=0, keepdims=True)

        @pl.when(n == nb - 1)
        def _():
            per_half = SSM_GROUPS // SSM_HALVES
            for k in range(2):
                for g in range(SSM_GROUPS):
                    h, gl = divmod(g, per_half)
                    c0, p0 = gl * SSM_GROUP, gl * SSM_STATE
                    db_ref[k, g * SSM_GROUP:(g + 1) * SSM_GROUP, :] = db_acc[k, h, c0:c0 + SSM_GROUP, p0:p0 + SSM_STATE]
                    dc_ref[k, g * SSM_GROUP:(g + 1) * SSM_GROUP, :] = dc_acc[k, h, c0:c0 + SSM_GROUP, p0:p0 + SSM_STATE]

    full = lambda shape: pl.BlockSpec(shape, lambda n: (0,) * len(shape))
    return pl.pallas_call(
        _with_comm(body, comm, 13, 5, nb, 0), grid=(nb,),
        in_specs=[pl.BlockSpec((t, 2 * SSM_W), lambda n: (nb - 1 - n, 0)),
                  pl.BlockSpec((t, SSM_W), lambda n: (nb - 1 - n, 0)),
                  pl.BlockSpec((8, 2 * SSM_N), lambda n: (nb - 1 - n, 0)),
                  pl.BlockSpec((t, 2 * SSM_N), lambda n: (nb - 1 - n, 0)),
                  full((8, SSM_N)), full((12, SCAN_SUB, SSM_N)), full((2, t + ns, t)), full((t, 2 * ns)),
                  full((2, SSM_HALVES, SSM_HW, SSM_HN)), full((2, SSM_HALVES, SSM_HN, SSM_HW)), full((1, SSM_W)),
                  full((SSM_W, SSM_W)), full((1, SSM_W))] + c_in,
        out_specs=[pl.BlockSpec((t, 2 * SSM_W), lambda n: (nb - 1 - n, 0)), full((2, SSM_W, SSM_STATE)),
                   full((2, SSM_W, SSM_STATE)), full((SSM_W, SSM_W)), full((8, SSM_N))] + c_out,
        out_shape=[jax.ShapeDtypeStruct((l, 2 * SSM_W), BF16),
                   jax.ShapeDtypeStruct((2, SSM_W, SSM_STATE), F32),
                   jax.ShapeDtypeStruct((2, SSM_W, SSM_STATE), F32),
                   jax.ShapeDtypeStruct((SSM_W, SSM_W), F32),
                   jax.ShapeDtypeStruct((8, SSM_N), F32)] + c_shape,
        scratch_shapes=[pltpu.VMEM((1, 2 * SSM_N), F32), pltpu.VMEM((2, SSM_HALVES, SSM_HW, SSM_HN), F32),
                        pltpu.VMEM((2, SSM_HALVES, SSM_HW, SSM_HN), F32)] + c_scratch,
        compiler_params=_params("arbitrary"), name=name)(ps, dys, chk, states, *scan_ops, bblk, cblk, dskip, wglu, bglu,
                                                         *c_args)


def _pool_count(i, t):
    pos = lax.broadcasted_iota(jnp.int32, (t, POOL_W), 0) + i * t + 1
    col = lax.broadcasted_iota(jnp.int32, (t, POOL_W), 1)
    win = jnp.where(col < POOL_GW, 2, jnp.where(col < 2 * POOL_GW, 4, jnp.where(col < 3 * POOL_GW, 8, 16)))
    return 1.0 / jnp.minimum(pos, win).astype(F32), col


def _window_sums(ext, n_rows, forward):
    col = lax.broadcasted_iota(jnp.int32, ext.shape, 1)
    sh = (lambda a, d: pltpu.roll(a, d, 0)) if forward else (lambda a, d: pltpu.roll(a, n_rows - d, 0))
    a2 = ext + sh(ext, 1)
    a4 = a2 + sh(a2, 2)
    a8 = a4 + sh(a4, 4)
    a16 = a8 + sh(a8, 8)
    return jnp.where(col < POOL_GW, a2, jnp.where(col < 2 * POOL_GW, a4, jnp.where(col < 3 * POOL_GW, a8, a16)))


def _pool_mix(pooled, wp_ref):
    return jnp.concatenate([_dot(pooled[:, g * POOL_GW:(g + 1) * POOL_GW], wp_ref[g]) for g in range(4)], axis=1)


def _pool_pooled(i, cur_u, prev_u, t):
    prev = jnp.where(i > 0, prev_u, 0.0)
    ext = jnp.concatenate([prev, cur_u], axis=0)
    inv_cnt, _ = _pool_count(i, t)
    return _window_sums(ext, t + POOL_HALO, True)[POOL_HALO:, :] * inv_cnt - cur_u


def _pool_fwd(pp, wpool, pscale, *, name, t=POOL_BLOCK):
    l = pp.shape[0]
    t = min(t, l)

    def body(cur_ref, prev_ref, wp_ref, sc_ref, yp_ref):
        i = pl.program_id(0)
        pooled = _pool_pooled(i, cur_ref[:, :POOL_W].astype(F32), prev_ref[...].astype(F32), t)
        lin = _pool_mix(pooled, wp_ref)
        sz, _ = _silu_and_grad(cur_ref[:, POOL_W:].astype(F32))
        yp_ref[...] = (lin * sc_ref[...] * sz).astype(BF16)

    hb = t // POOL_HALO
    return pl.pallas_call(
        body, grid=(l // t,),
        in_specs=[pl.BlockSpec((t, 2 * POOL_W), lambda i: (i, 0)),
                  pl.BlockSpec((POOL_HALO, POOL_W), lambda i: (jnp.maximum(i * hb - 1, 0), 0)),
                  pl.BlockSpec((4, POOL_GW, POOL_GW), lambda i: (0, 0, 0)),
                  pl.BlockSpec((1, POOL_W), lambda i: (0, 0))],
        out_specs=pl.BlockSpec((t, POOL_W), lambda i: (i, 0)),
        out_shape=jax.ShapeDtypeStruct((l, POOL_W), BF16),
        compiler_params=_params("parallel"), name=name)(pp, pp, wpool, pscale)


def _pool_bwd(pp, dyp, wpool, pscale, *, name, t=POOL_BLOCK):
    l = pp.shape[0]
    t = min(t, l)
    nb = l // t

    def body(cur_ref, prev_ref, dyp_ref, wp_ref, sc_ref, dpp_ref, dwp_ref, sums_ref, carry_ref):
        n = pl.program_id(0)
        i = nb - 1 - n

        @pl.when(n == 0)
        def _():
            carry_ref[...] = jnp.zeros_like(carry_ref)
            dwp_ref[...] = jnp.zeros_like(dwp_ref)
            sums_ref[...] = jnp.zeros_like(sums_ref)

        cur_u = cur_ref[:, :POOL_W].astype(F32)
        pooled = _pool_pooled(i, cur_u, prev_ref[...].astype(F32), t)
        lin = _pool_mix(pooled, wp_ref)
        sz, dsz = _silu_and_grad(cur_ref[:, POOL_W:].astype(F32))
        dyp_v = dyp_ref[...]
        scale = sc_ref[...]
        dpp_ref[:, POOL_W:] = (dyp_v * lin * scale * dsz).astype(BF16)
        dpre = dyp_v * sz
        sums_ref[0:1, :] += jnp.sum(dpre * lin, axis=0, keepdims=True)
        dlin = dpre * scale
        dpooled = []
        for g in range(4):
            dl = dlin[:, g * POOL_GW:(g + 1) * POOL_GW]
            dwp_ref[g] += _dot(pooled[:, g * POOL_GW:(g + 1) * POOL_GW], dl, TN)
            dpooled.append(_dot(dl, wp_ref[g], NT))
        dpooled = jnp.concatenate(dpooled, axis=1)
        inv_cnt, _ = _pool_count(i, t)
        dq = dpooled * inv_cnt
        ext = jnp.concatenate([dq, carry_ref[...]], axis=0)
        du = _window_sums(ext, t + POOL_HALO, False)[:t, :] - dpooled
        dpp_ref[:, :POOL_W] = du.astype(BF16)
        carry_ref[...] = dq[:POOL_HALO, :]

    hb = t // POOL_HALO
    return pl.pallas_call(
        body, grid=(nb,),
        in_specs=[pl.BlockSpec((t, 2 * POOL_W), lambda n: (nb - 1 - n, 0)),
                  pl.BlockSpec((POOL_HALO, POOL_W), lambda n: (jnp.maximum((nb - 1 - n) * hb - 1, 0), 0)),
                  pl.BlockSpec((t, POOL_W), lambda n: (nb - 1 - n, 0)),
                  pl.BlockSpec((4, POOL_GW, POOL_GW), lambda n: (0, 0, 0)),
                  pl.BlockSpec((1, POOL_W), lambda n: (0, 0))],
        out_specs=[pl.BlockSpec((t, 2 * POOL_W), lambda n: (nb - 1 - n, 0)),
                   pl.BlockSpec((4, POOL_GW, POOL_GW), lambda n: (0, 0, 0)),
                   pl.BlockSpec((8, POOL_W), lambda n: (0, 0))],
        out_shape=[jax.ShapeDtypeStruct((l, 2 * POOL_W), BF16), jax.ShapeDtypeStruct((4, POOL_GW, POOL_GW), F32),
                   jax.ShapeDtypeStruct((8, POOL_W), F32)],
        scratch_shapes=[pltpu.VMEM((POOL_HALO, POOL_W), F32)],
        compiler_params=_params("arbitrary"), name=name)(pp, pp, dyp, wpool, pscale)


def _merge_fwd(ya, ys, yp, wa, ws, wp, pg, wout, x, gate, *, name, tm=512):
    l, d = x.shape
    tm = min(tm, l)

    def body(ya_ref, ys_ref, yp_ref, wa_ref, ws_ref, wp_ref, pg_ref, wo_ref, x_ref, g_ref,
             xn_ref, mg_ref, ba_ref, bs_ref, bp_ref, out_ref):
        acc = None
        for k, (y_ref, w_ref, b_ref) in enumerate(((ya_ref, wa_ref, ba_ref), (ys_ref, ws_ref, bs_ref),
                                                   (yp_ref, wp_ref, bp_ref))):
            br = _dot(y_ref[...], w_ref[...])
            b_ref[...] = br.astype(BF16)
            term = _sigmoid(pg_ref[:, k * d:(k + 1) * d].astype(F32)) * br
            acc = term if acc is None else acc + term
        merged = acc.astype(BF16)
        mg_ref[...] = merged
        out = _dot(merged, wo_ref[...])
        out_ref[...] = out.astype(BF16)
        xn_ref[...] = x_ref[...] + g_ref[...] * out

    rowy = pl.BlockSpec((tm, ATT_W), lambda i: (i, 0))
    wsp = pl.BlockSpec((ATT_W, d), lambda i: (0, 0))
    rowd = pl.BlockSpec((tm, d), lambda i: (i, 0))
    return pl.pallas_call(
        body, grid=(l // tm,),
        in_specs=[rowy, rowy, rowy, wsp, wsp, wsp, pl.BlockSpec((tm, 3 * d), lambda i: (i, 0)),
                  pl.BlockSpec((d, d), lambda i: (0, 0)), rowd, pl.BlockSpec((1, d), lambda i: (0, 0))],
        out_specs=[rowd] * 6,
        out_shape=[jax.ShapeDtypeStruct((l, d), F32)] + [jax.ShapeDtypeStruct((l, d), BF16)] * 5,
        compiler_params=_params("parallel"), name=name)(ya, ys, yp, wa, ws, wp, pg, wout, x, gate)


def _merge_bwd(dx, out, gate, wout, pg, brs, wbrs, ys, merged, *, name, tm=256, comm=None):
    l, d = dx.shape
    tm = min(tm, l)
    nb = l // tm
    w = ys[0].shape[1]

    def body(dx_ref, out_ref, g_ref, w_ref, pg_ref, ba_ref, bs_ref, bp_ref, wa_ref, ws_ref, wp_ref,
             ya_ref, ys_ref, yp_ref, mg_ref,
             dya_ref, dys_ref, dyp_ref, dpg_ref, sums_ref, dwa_ref, dws_ref, dwp_ref, dwo_ref, acc_br, acc_out):
        i = pl.program_id(0)

        @pl.when(i == 0)
        def _():
            sums_ref[...] = jnp.zeros_like(sums_ref)
            acc_br[...] = jnp.zeros_like(acc_br)
            acc_out[...] = jnp.zeros_like(acc_out)

        dxv = dx_ref[...]
        sums_ref[0:1, :] += jnp.sum(dxv * out_ref[...].astype(F32), axis=0, keepdims=True)
        dmo = (dxv * g_ref[...]).astype(BF16)
        acc_out[...] += _dot(mg_ref[...], dmo, TN)
        dmerged = _dot(dmo, w_ref[...], NT)
        branches = ((ba_ref, wa_ref, ya_ref, dya_ref), (bs_ref, ws_ref, ys_ref, dys_ref), (bp_ref, wp_ref, yp_ref, dyp_ref))
        for k, (b_ref, wk_ref, y_ref, dy_ref) in enumerate(branches):
            gk = _sigmoid(pg_ref[:, k * d:(k + 1) * d].astype(F32))
            dbr = (dmerged * gk).astype(BF16)
            dpg_ref[:, k * d:(k + 1) * d] = (dmerged * b_ref[...].astype(F32) * gk * (1.0 - gk)).astype(BF16)
            dy_ref[...] = _dot(dbr, wk_ref[...], NT)
            acc_br[k] += _dot(y_ref[...], dbr, TN)

        @pl.when(i == nb - 1)
        def _():
            for k, dw_ref in enumerate((dwa_ref, dws_ref, dwp_ref)):
                dw_ref[...] = acc_br[k].astype(BF16)
            dwo_ref[...] = acc_out[...].astype(BF16)

    row = pl.BlockSpec((tm, d), lambda i: (i, 0))
    half = pl.BlockSpec((tm, w), lambda i: (i, 0))
    wide = pl.BlockSpec((tm, 3 * d), lambda i: (i, 0))
    const = lambda shape: pl.BlockSpec(shape, lambda i: (0,) * len(shape))
    c_args, c_in, c_out, c_shape, c_scratch = _comm_extra(comm)
    res = pl.pallas_call(
        _with_comm(body, comm, 15, 9, nb, 0), grid=(nb,),
        in_specs=[row, row, const((1, d)), const((d, d)), wide, row, row, row, const((w, d)), const((w, d)), const((w, d)),
                  half, half, half, row] + c_in,
        out_specs=[half, half, half, wide, const((8, d)), const((w, d)), const((w, d)), const((w, d)), const((d, d))] + c_out,
        out_shape=[jax.ShapeDtypeStruct((l, w), F32)] * 3 + [jax.ShapeDtypeStruct((l, 3 * d), BF16),
                                                             jax.ShapeDtypeStruct((8, d), F32)]
                  + [jax.ShapeDtypeStruct((w, d), BF16)] * 3 + [jax.ShapeDtypeStruct((d, d), BF16)] + c_shape,
        scratch_shapes=[pltpu.VMEM((3, w, d), F32), pltpu.VMEM((d, d), F32)] + c_scratch,
        compiler_params=_params("arbitrary"), name=name)(dx, out, gate, wout, pg, *brs, *wbrs, *ys, merged, *c_args)
    return list(res[:9]), list(res[9:])


def _adamw(w, gs, m, v, *, name, tr=256):
    r, c = w.shape
    ns = len(gs)
    p, rs, _ = gs[0].shape
    assert rs * ns == r
    tr = min(tr, rs)
    assert rs % tr == 0
    nr = rs // tr
    c1 = 1.0 / (1.0 - ADAM_B1 ** ADAM_STEP)
    c2 = 1.0 / (1.0 - ADAM_B2 ** ADAM_STEP)

    def body(*refs):
        w_ref, g_refs, (m_ref, v_ref, go_ref, d_ref, mo_ref, vo_ref) = refs[0], refs[1:1 + ns], refs[1 + ns:]
        slab = pl.program_id(0)
        gv = None
        for k, g_ref in enumerate(g_refs):
            gk = g_ref[0].astype(F32)
            for j in range(1, p):
                gk = gk + g_ref[j].astype(F32)
            gv = gk if gv is None else jnp.where(slab == k, gk, gv)
        go_ref[...] = gv
        mn = ADAM_B1 * m_ref[...] + (1.0 - ADAM_B1) * gv
        vn = ADAM_B2 * v_ref[...] + (1.0 - ADAM_B2) * (gv * gv)
        mo_ref[...] = mn
        vo_ref[...] = vn
        d_ref[...] = -ADAM_LR * ((mn * c1) / (jnp.sqrt(vn * c2) + ADAM_EPS) + ADAM_WD * w_ref[...])

    row = pl.BlockSpec((tr, c), lambda s, i: (s * nr + i, 0))
    g_specs = [pl.BlockSpec((p, tr, c), lambda s, i, k=k: (0, jnp.where(s == k, i, 0), 0)) for k in range(ns)]
    return pl.pallas_call(
        body, grid=(ns, nr),
        in_specs=[row] + g_specs + [row, row],
        out_specs=[row] * 4,
        out_shape=[jax.ShapeDtypeStruct((r, c), F32)] * 4,
        compiler_params=_params("arbitrary", "arbitrary"), name=name)(w, *gs, m, v)


def _mesh_place():
    x, y, c = lax.axis_index("x"), lax.axis_index("y"), lax.axis_index("c")
    other_chips = [(1 - x, y), (x, 1 - y), (1 - x, 1 - y)]
    return x, y, c, other_chips


def _run_plan(plan, *, name):
    n = len(plan["args"])

    def body(*refs):
        start, forward, finish = plan["phases"](refs[:n], refs[n:2 * n], *refs[2 * n:])
        start()
        forward()
        finish()

    anyspec = pl.BlockSpec(memory_space=pl.ANY)
    return pl.pallas_call(
        body, in_specs=[anyspec] * n, out_specs=[anyspec] * n, out_shape=plan["out_shape"],
        scratch_shapes=plan["scratch"], name=name)(*plan["args"])


def _gather_plan(arrs):
    n = len(arrs)

    def phases(ins, outs, send_sems, recv_sems, loc_sems):
        x, y, c, chips = _mesh_place()
        me = 4 * x + 2 * y + c
        slot = lambda px, py, pc: 4 * px + 2 * py + pc

        def copy(k, j, src, block, to):
            return pltpu.make_async_remote_copy(
                src_ref=src, dst_ref=outs[k].at[block], send_sem=send_sems.at[k, j], recv_sem=recv_sems.at[k, j],
                device_id=to, device_id_type=pl.DeviceIdType.MESH)

        local = [pltpu.make_async_copy(ins[k], outs[k].at[me], loc_sems.at[k]) for k in range(n)]
        first = []
        for k in range(n):
            first.append(copy(k, 0, ins[k], me, (x, y, 1 - c)))
            for j, chip in enumerate(chips):
                first.append(copy(k, 1 + j, ins[k], me, (*chip, c)))
        passed = [copy(k, 4 + j, outs[k].at[slot(*chip, c)], slot(*chip, c), (x, y, 1 - c))
                  for j, chip in enumerate(chips) for k in range(n)]

        def start():
            for cp in local + first:
                cp.start()

        def forward():
            for j, chip in enumerate(chips):
                for k in range(n):
                    copy(k, 1 + j, ins[k], slot(*chip, c), (x, y, c)).wait_recv()
                    passed[j * n + k].start()

        def finish():
            for k in range(n):
                copy(k, 0, ins[k], slot(x, y, 1 - c), (x, y, c)).wait_recv()
                for j, chip in enumerate(chips):
                    copy(k, 4 + j, ins[k], slot(*chip, 1 - c), (x, y, c)).wait_recv()
            for cp in first + passed:
                cp.wait_send()
            for cp in local:
                cp.wait()

        return start, forward, finish

    return dict(
        args=list(arrs), out_shape=[jax.ShapeDtypeStruct((N_DEV,) + a.shape, a.dtype) for a in arrs],
        scratch=[pltpu.SemaphoreType.DMA((n, 7)), pltpu.SemaphoreType.DMA((n, 7)), pltpu.SemaphoreType.DMA((n,))],
        phases=phases)


def _allreduce_small(small, extra, *, name):
    r, lanes = small.shape
    assert r % 16 == 0
    h = r // 2
    e = extra.shape[0]

    def body(s_ref, x_ref, out_ref, xall_ref, sib_ref, parts_ref, send_sems, recv_sems):
        x, y, c, chips = _mesh_place()
        me = 4 * x + 2 * y + c
        my_chip = 2 * x + y
        sibling = (x, y, 1 - c)
        mine = pl.ds(pl.multiple_of(c * h, 8), h)
        theirs = pl.ds(pl.multiple_of((1 - c) * h, 8), h)

        def remote(j, src, dst, to):
            return pltpu.make_async_remote_copy(src_ref=src, dst_ref=dst, send_sem=send_sems.at[j],
                                                recv_sem=recv_sems.at[j], device_id=to, device_id_type=pl.DeviceIdType.MESH)

        to_sibling = remote(0, s_ref.at[theirs], sib_ref, sibling)
        to_sibling.start()
        xall_ref[me] = x_ref[...]
        extras = []
        for rr in range(1, N_DEV):
            peer = me ^ rr
            cp = remote(4 + rr, x_ref, xall_ref.at[me], (peer // 4, (peer // 2) % 2, peer % 2))
            cp.start()
            extras.append(cp)
        to_sibling.wait_recv()
        parts_ref[my_chip] = s_ref[mine] + sib_ref[...]
        to_chips = [remote(1 + j, parts_ref.at[my_chip], parts_ref.at[my_chip], (px, py, c))
                    for j, (px, py) in enumerate(chips)]
        for cp in to_chips:
            cp.start()
        for cp in to_chips:
            cp.wait_recv()
        out_ref[mine] = (parts_ref[0] + parts_ref[1]) + (parts_ref[2] + parts_ref[3])
        done = remote(4, out_ref.at[mine], out_ref.at[mine], sibling)
        done.start()
        remote(4, out_ref.at[theirs], out_ref.at[theirs], sibling).wait_recv()
        for cp in extras:
            cp.wait()
        to_sibling.wait_send()
        for cp in to_chips:
            cp.wait_send()
        done.wait_send()

    vmem = pl.BlockSpec(memory_space=pltpu.VMEM)
    return pl.pallas_call(
        body, in_specs=[vmem, vmem], out_specs=[vmem, vmem],
        out_shape=[jax.ShapeDtypeStruct((r, lanes), F32), jax.ShapeDtypeStruct((N_DEV, e, lanes), F32)],
        scratch_shapes=[pltpu.VMEM((h, lanes), F32), pltpu.VMEM((4, h, lanes), F32),
                        pltpu.SemaphoreType.DMA((12,)), pltpu.SemaphoreType.DMA((12,))],
        compiler_params=pltpu.CompilerParams(vmem_limit_bytes=VMEM_LIMIT), name=name)(small, extra)


def _ada_modulation(c, w_ada, b_cols, comm):
    depth, d, cols = w_ada.shape
    c_args, c_in, c_out, c_shape, c_scratch = _comm_extra(comm)
    nc = len(c_args)

    def body(c_ref, w_ref, b_ref, *refs):
        cin, (cact_ref, mod_ref), cout = refs[:nc], refs[nc:nc + 2], refs[nc + 2:2 * nc + 2]
        call_ref, part_ref, send_sems, recv_sems = refs[2 * nc + 2:2 * nc + 6]
        start, forward, finish = comm["phases"](cin, cout, *refs[2 * nc + 6:])
        start()
        x, y, core, _ = _mesh_place()
        me = 4 * x + 2 * y + core

        def to_all(j0, src, dst):
            copies = []
            for r in range(1, N_DEV):
                peer = me ^ r
                copies.append(pltpu.make_async_remote_copy(
                    src_ref=src, dst_ref=dst, send_sem=send_sems.at[j0 + r - 1], recv_sem=recv_sems.at[j0 + r - 1],
                    device_id=(peer // 4, (peer // 2) % 2, peer % 2), device_id_type=pl.DeviceIdType.MESH))
            for cp in copies:
                cp.start()
            for cp in copies:
                cp.wait()

        call_ref[me] = c_ref[...]
        to_all(0, c_ref, call_ref.at[me])
        c_act = jnp.concatenate([call_ref[k] for k in range(N_DEV)], axis=0)
        c_act = c_act * _sigmoid(c_act)
        cact_ref[...] = c_act
        for li in range(depth):
            part_ref[li] = _dot(c_act, w_ref[li]) + b_ref[li:li + 1, :]
        mod_ref[me] = part_ref[...]
        to_all(N_DEV - 1, part_ref, mod_ref.at[me])
        forward()
        finish()

    vmem = pl.BlockSpec(memory_space=pltpu.VMEM)
    res = pl.pallas_call(
        body, in_specs=[vmem] * 3 + c_in, out_specs=[vmem] * 2 + c_out,
        out_shape=[jax.ShapeDtypeStruct((N_DEV, d), F32), jax.ShapeDtypeStruct((N_DEV, depth, N_DEV, cols), F32)] + c_shape,
        scratch_shapes=[pltpu.VMEM((N_DEV, 1, d), F32), pltpu.VMEM((depth, N_DEV, cols), F32),
                        pltpu.SemaphoreType.DMA((2 * (N_DEV - 1),)), pltpu.SemaphoreType.DMA((2 * (N_DEV - 1),))] + c_scratch,
        compiler_params=pltpu.CompilerParams(vmem_limit_bytes=VMEM_LIMIT), name="ada_modulation")(c, w_ada, b_cols, *c_args)
    return res[0], res[1], list(res[2:])


def _sibling_swap_plan(arrs):
    n = len(arrs)

    def phases(ins, outs, send_sems, recv_sems):
        x, y, c, _ = _mesh_place()
        copies = [pltpu.make_async_remote_copy(
            src_ref=ins[k].at[1 - c], dst_ref=outs[k], send_sem=send_sems.at[k], recv_sem=recv_sems.at[k],
            device_id=(x, y, 1 - c), device_id_type=pl.DeviceIdType.MESH) for k in range(n)]

        def start():
            for cp in copies:
                cp.start()

        def finish():
            for cp in copies:
                cp.wait()

        return start, (lambda: None), finish

    return dict(args=list(arrs), out_shape=[jax.ShapeDtypeStruct(a.shape[1:], a.dtype) for a in arrs],
                scratch=[pltpu.SemaphoreType.DMA((n,)), pltpu.SemaphoreType.DMA((n,))], phases=phases)


def _pair_add(mine, theirs, core, *, name, tr=1024):
    _, r, c = mine.shape
    tr = min(tr, r)
    assert r % tr == 0

    def body(core_ref, m_ref, t_ref, o_ref):
        o_ref[...] = (m_ref[0].astype(F32) + t_ref[...].astype(F32)).astype(BF16)

    return pl.pallas_call(
        body,
        grid_spec=pltpu.PrefetchScalarGridSpec(
            num_scalar_prefetch=1, grid=(r // tr,),
            in_specs=[pl.BlockSpec((1, tr, c), lambda i, core_ref: (core_ref[0], i, 0)),
                      pl.BlockSpec((tr, c), lambda i, core_ref: (i, 0))],
            out_specs=pl.BlockSpec((tr, c), lambda i, core_ref: (i, 0))),
        out_shape=jax.ShapeDtypeStruct((r, c), BF16),
        compiler_params=_params("parallel"), name=name)(core, mine, theirs)


def _pair_add_small(mines, theirs, core, *, name):
    n = len(mines)

    def body(core_ref, *refs):
        for m_ref, t_ref, o_ref in zip(refs[:n], refs[n:2 * n], refs[2 * n:]):
            o_ref[...] = (m_ref[0].astype(F32) + t_ref[...].astype(F32)).astype(BF16)

    whole = lambda a: pl.BlockSpec(a.shape, lambda i, core_ref: (0,) * a.ndim)
    return pl.pallas_call(
        body,
        grid_spec=pltpu.PrefetchScalarGridSpec(
            num_scalar_prefetch=1, grid=(1,),
            in_specs=[pl.BlockSpec((1,) + m.shape[1:], lambda i, core_ref: (core_ref[0], 0, 0)) for m in mines]
                     + [whole(t) for t in theirs],
            out_specs=[whole(t) for t in theirs]),
        out_shape=[jax.ShapeDtypeStruct(t.shape, BF16) for t in theirs],
        compiler_params=_params("arbitrary"), name=name)(core, *mines, *theirs)


def _chip_scatter_plan(arrs):
    n = len(arrs)

    def phases(ins, outs, send_sems, recv_sems, loc_sems):
        x, y, c, chips = _mesh_place()
        mine = 2 * x + y
        local = [pltpu.make_async_copy(ins[k].at[mine], outs[k].at[mine], loc_sems.at[k]) for k in range(n)]
        remote = [pltpu.make_async_remote_copy(
            src_ref=ins[k].at[2 * px + py], dst_ref=outs[k].at[mine], send_sem=send_sems.at[k, j],
            recv_sem=recv_sems.at[k, j], device_id=(px, py, c), device_id_type=pl.DeviceIdType.MESH)
            for j, (px, py) in enumerate(chips) for k in range(n)]

        def start():
            for cp in local + remote:
                cp.start()

        def finish():
            for cp in remote:
                cp.wait()
            for cp in local:
                cp.wait()

        return start, (lambda: None), finish

    return dict(
        args=list(arrs), out_shape=[jax.ShapeDtypeStruct(a.shape, a.dtype) for a in arrs],
        scratch=[pltpu.SemaphoreType.DMA((n, 3)), pltpu.SemaphoreType.DMA((n, 3)), pltpu.SemaphoreType.DMA((n,))],
        phases=phases)


def _ssm_discretize(a_re, a_im, log_dt, b_re, b_im):
    dt = jnp.exp(log_dt)[:, None]
    mag = jnp.exp(a_re * dt)
    lr = mag * jnp.cos(a_im * dt)
    li = mag * jnp.sin(a_im * dt)
    den = a_re * a_re + a_im * a_im
    cr = ((lr - 1.0) * a_re + li * a_im) / den
    ci = (li * a_re - (lr - 1.0) * a_im) / den
    bbr = cr[..., None] * b_re - ci[..., None] * b_im
    bbi = cr[..., None] * b_im + ci[..., None] * b_re
    return lr, li, bbr, bbi


def _ssm_dense(lr, li, bbr, bbi, c_re, c_im, *, name):
    import numpy as np
    scan_ops = _scan_tables(lr.reshape(-1), li.reshape(-1)) + _scan_consts(SEQ_BLOCK)
    per_half = SSM_GROUPS // SSM_HALVES
    bt = jnp.stack([b.transpose(0, 2, 1).reshape(SSM_W, SSM_STATE) for b in (bbr, bbi)])
    ct = jnp.stack([c.transpose(0, 2, 1).reshape(SSM_N, SSM_GROUP) for c in (c_re, c_im)])
    rep_p = jnp.asarray(np.tile(np.eye(SSM_STATE, dtype=np.float32), (1, per_half)), BF16)
    rep_c = jnp.asarray(np.tile(np.eye(SSM_GROUP, dtype=np.float32), (1, per_half)), BF16)

    def body(bt_ref, ct_ref, rp_ref, rc_ref, b_ref, c_ref):
        def on_diagonal(shape, rows, cols):
            r = lax.broadcasted_iota(jnp.int32, shape, 0) // rows
            c = lax.broadcasted_iota(jnp.int32, shape, 1) // cols
            return r == c

        mask_b = on_diagonal((SSM_HW, SSM_HN), SSM_GROUP, SSM_STATE)
        mask_c = on_diagonal((SSM_HN, SSM_HW), SSM_STATE, SSM_GROUP)
        for k in range(2):
            for h in range(SSM_HALVES):
                b_rows = bt_ref[k, h * SSM_HW:(h + 1) * SSM_HW, :]
                b_ref[k, h] = jnp.where(mask_b, _dot(b_rows, rp_ref[...]), 0.0).astype(BF16)
                c_rows = ct_ref[k, h * SSM_HN:(h + 1) * SSM_HN, :]
                c_ref[k, h] = jnp.where(mask_c, _dot(c_rows, rc_ref[...]), 0.0).astype(BF16)

    vmem = pl.BlockSpec(memory_space=pltpu.VMEM)
    bblk, cblk = pl.pallas_call(
        body, in_specs=[vmem] * 4, out_specs=[vmem] * 2,
        out_shape=[jax.ShapeDtypeStruct((2, SSM_HALVES, SSM_HW, SSM_HN), BF16),
                   jax.ShapeDtypeStruct((2, SSM_HALVES, SSM_HN, SSM_HW), BF16)],
        compiler_params=pltpu.CompilerParams(vmem_limit_bytes=VMEM_LIMIT), name=name)(bt, ct, rep_p, rep_c)
    return scan_ops, bblk, cblk


def _ssm_extract(db, dc, sums):
    db = db.reshape(2, SSM_GROUPS, SSM_GROUP, SSM_STATE).transpose(0, 1, 3, 2)
    dc = dc.reshape(2, SSM_GROUPS, SSM_GROUP, SSM_STATE)
    dlr = sums[0].reshape(SSM_GROUPS, SSM_STATE)
    dli = sums[1].reshape(SSM_GROUPS, SSM_STATE)
    return dlr, dli, db[0], db[1], dc[0], dc[1]


def _in_groups():
    names = ("q", "k", "v", "u_ssm", "u_pool", "z_att", "z_ssm", "z_pool", "gates")
    sizes = (ATT_W, KV_W, KV_W, SSM_W, POOL_W, ATT_W, SSM_W, POOL_W, 3 * D_MODEL)
    r, lo = {}, 0
    for nm, s in zip(names, sizes):
        r[nm] = (lo, lo + s)
        lo += s
    kv = (r["k"][0], r["v"][1])
    return ((r["q"], r["z_att"], kv), (r["u_ssm"], r["z_ssm"]), (r["u_pool"], r["z_pool"]), (r["gates"],))


IN_GROUPS = _in_groups()


def _layer_fwd(x, lw, li, late=None, comm_attn=None, comm_ssm=None):
    tag = f"l{li}"
    h, (pa, ps, pp, pg), arrived = _ln_proj(x, lw["norm_g"], lw["shift"], lw["scale"], lw["w_in"], IN_GROUPS,
                                            name=f"ln_proj_{tag}", comm=None if late is None else late[0])
    if late is not None:
        lw = {**lw, **late[1](arrived)}
    ya, from_attn = _attn_fwd(pa, lw["sinks"], name=f"attn_fwd_{tag}", comm=comm_attn)
    ys, chk, states, *from_ssm = _ssm_fwd(ps, lw["lam"], lw["bblk"], lw["cblk"], lw["ssm_d"], lw["w_glu"], lw["b_glu"],
                                          name=f"ssm_fwd_{tag}", comm=comm_ssm)
    yp = _pool_fwd(pp, lw["w_pool"], lw["pool_scale"], name=f"pool_fwd_{tag}")
    x_new, merged, ba, bs, bp, out = _merge_fwd(ya, ys, yp, lw["w_br_att"], lw["w_br_ssm"], lw["w_br_pool"], pg,
                                                lw["w_out"], x, lw["gate"], name=f"merge_fwd_{tag}")
    saved = dict(x=x, h=h, pa=pa, ps=ps, pp=pp, pg=pg, ya=ya, ys=ys, yp=yp, chk=chk, states=states, merged=merged,
                 ba=ba, bs=bs, bp=bp, out=out)
    return x_new, saved, lw, list(from_attn), list(from_ssm)


def _layer_bwd(dx, lw, sv, li, later=None, own=None):
    tag = f"l{li}"
    g = {}
    merge_out, swapped = _merge_bwd(
        dx, sv["out"], lw["gate"], lw["w_out"], sv["pg"], (sv["ba"], sv["bs"], sv["bp"]),
        (lw["w_br_att"], lw["w_br_ssm"], lw["w_br_pool"]), (sv["ya"], sv["ys"], sv["yp"]), sv["merged"],
        name=f"merge_bwd_{tag}", comm=None if later is None else later[0])
    dya, dys, dyp, dpg, gate_sums, g["w_br_att"], g["w_br_ssm"], g["w_br_pool"], g["w_out"] = merge_out
    dpa, dsink = _attn_bwd(sv["pa"], lw["sinks"], dya, name=f"attn_bwd_{tag}")
    dps, db_dense, dc_dense, dwglu, ssm_sums, *exchanged = _ssm_bwd(
        sv["ps"], dys, sv["chk"], sv["states"], lw["lam"], lw["bblk"], lw["cblk"], lw["ssm_d"], lw["w_glu"], lw["b_glu"],
        name=f"ssm_bwd_{tag}", comm=None if later is None else later[1](swapped))
    g["w_glu"] = dwglu.astype(BF16)
    dpp, dwpool, pool_sums = _pool_bwd(sv["pp"], dyp, lw["w_pool"], lw["pool_scale"], name=f"pool_bwd_{tag}")
    h = sv["h"]
    dproj = (dpa, dps, dpp, dpg)
    g["w_in"], from_late = _mm_tn_grouped(h, dproj, IN_GROUPS, name=f"dw_in_{tag}",
                                          comm=None if own is None else own({k: g[k] for k in LATE_WEIGHTS}))
    dx_in, ln_sums, from_w_in = _ln_proj_bwd(dproj, lw["w_in"], IN_GROUPS, sv["x"], dx, lw["norm_g"], lw["scale"],
                                             name=f"ln_proj_bwd_{tag}",
                                             comm=None if own is None else own({"w_in": g["w_in"]}))
    g["dmod"] = jnp.concatenate([ln_sums[0], ln_sums[1], gate_sums[0]])
    g["norm_g"] = ln_sums[2]
    g["attn_sinks"] = dsink[:, 0]
    g["ssm_raw"] = _ssm_extract(db_dense, dc_dense, ssm_sums)
    g["ssm_d"] = ssm_sums[2, :SSM_W]
    g["b_glu"] = ssm_sums[3, :SSM_W]
    g["w_pool"] = dwpool
    g["pool_scale"] = pool_sums[0]
    return dx_in, g, exchanged, list(from_w_in) + list(from_late)


BIG_WEIGHTS = ("w_in", "w_glu", "w_br_att", "w_br_ssm", "w_br_pool", "w_out")
ROW_SHARDED = ("w_glu", "w_out")


LATE_WEIGHTS = BIG_WEIGHTS[1:]


def _side_by_side(g, *, tm=256):
    n, r, c = g.shape

    def body(g_ref, o_ref):
        for s in range(n):
            o_ref[:, s * c:(s + 1) * c] = g_ref[s]

    return pl.pallas_call(
        body, grid=(r // tm,),
        in_specs=[pl.BlockSpec((n, tm, c), lambda i: (0, i, 0))],
        out_specs=pl.BlockSpec((tm, n * c), lambda i: (i, 0)),
        out_shape=jax.ShapeDtypeStruct((r, n * c), g.dtype),
        compiler_params=_params("parallel"), name="side_by_side")(g)


def _full_weights(keys, gathered):
    full = {}
    for k, g in zip(keys, gathered):
        if k in ROW_SHARDED:
            full[k] = g.reshape(N_DEV * g.shape[1], g.shape[2])
        elif g.shape[2] % 128:
            full[k] = _side_by_side(g)
        else:
            full[k] = g.transpose(1, 0, 2).reshape(g.shape[1], N_DEV * g.shape[2])
    return full


def _by_destination(keys, grads):
    out = []
    for k in keys:
        g = grads[k]
        if g.ndim == 4:
            out.append(g)
        elif k in ROW_SHARDED:
            out.append(g.reshape(4, 2, g.shape[0] // N_DEV, g.shape[1]).transpose(1, 0, 2, 3))
        else:
            out.append(g.reshape(g.shape[0], 4, 2, g.shape[1] // N_DEV).transpose(2, 1, 0, 3))
    return out


def _prepare_layer(li, mod, norm_g, w_in_full, attn_sinks, disc, ssm_c_re, ssm_c_im, ssm_d, b_glu, w_pool, pool_scale):
    d = D_MODEL
    lr, li_, bbr, bbi = disc
    lam, bblk, cblk = _ssm_dense(lr[li], li_[li], bbr[li], bbi[li], ssm_c_re[li], ssm_c_im[li], name=f"ssm_dense_l{li}")
    return dict(
        norm_g=norm_g[li][None, :], shift=mod[li, :d][None, :], scale=mod[li, d:2 * d][None, :],
        gate=mod[li, 2 * d:][None, :], w_in=w_in_full,
        sinks=attn_sinks[li], lam=lam, bblk=bblk, cblk=cblk, ssm_d=ssm_d[li][None, :],
        b_glu=b_glu[li][None, :], w_pool=w_pool[li].astype(BF16), pool_scale=pool_scale[li][None, :])


SMALL_ROWS = 64
SMALL_ORDER = ("norm_g", "attn_sinks", "ssm_d", "b_glu", "w_pool", "pool_scale", "dmod")


def _pack_small(loss, dfinal_g, layer_grads):
    parts = [jnp.broadcast_to(loss.reshape(1), (128,)), dfinal_g]
    for g in layer_grads:
        for k in SMALL_ORDER:
            v = g[k].reshape(-1)
            if v.shape[0] % 128:
                v = jnp.pad(v, (0, 128 - v.shape[0] % 128))
            parts.append(v)
        for v in g["ssm_raw"]:
            parts.append(v.reshape(-1))
    flat = jnp.concatenate(parts)
    return jnp.pad(flat, (0, (-flat.shape[0]) % (SMALL_ROWS * 128))).reshape(-1, 128)


def _unpack_small(flat, shapes):
    out, off = [], 0
    for s in shapes:
        n = int(math.prod(s))
        out.append(flat[off:off + n].reshape(s))
        off += n + (-n) % 128
    return out


def kernel(x, c, norm_g, w_ada, b_ada, w_in, attn_sinks, ssm_a_re, ssm_a_im, ssm_log_dt, ssm_b_re, ssm_b_im, ssm_c_re, ssm_c_im, ssm_d, w_glu, b_glu, w_pool, pool_scale, w_br_att, w_br_ssm, w_br_pool, w_out, final_g, loss_target, m_norm_g, m_w_ada, m_b_ada, m_w_in, m_attn_sinks, m_ssm_a_re, m_ssm_a_im, m_ssm_log_dt, m_ssm_b_re, m_ssm_b_im, m_ssm_c_re, m_ssm_c_im, m_ssm_d, m_w_glu, m_b_glu, m_w_pool, m_pool_scale, m_w_br_att, m_w_br_ssm, m_w_br_pool, m_w_out, m_final_g, v_norm_g, v_w_ada, v_b_ada, v_w_in, v_attn_sinks, v_ssm_a_re, v_ssm_a_im, v_ssm_log_dt, v_ssm_b_re, v_ssm_b_im, v_ssm_c_re, v_ssm_c_im, v_ssm_d, v_w_glu, v_b_glu, v_w_pool, v_pool_scale, v_w_br_att, v_w_br_ssm, v_w_br_pool, v_w_out, v_final_g):
    me = 4 * lax.axis_index("x") + 2 * lax.axis_index("y") + lax.axis_index("c")
    d = D_MODEL
    ada_w = 3 * d // N_DEV

    sharded = dict(w_in=w_in, w_glu=w_glu, w_br_att=w_br_att, w_br_ssm=w_br_ssm, w_br_pool=w_br_pool, w_out=w_out)
    shards = lambda li, keys: [sharded[k][li].astype(BF16) for k in keys]

    b_cols = lax.dynamic_slice(b_ada, (0, me * ada_w), (DEPTH, ada_w))
    c_act, mod_all, w_in0 = _ada_modulation(c, w_ada, b_cols, _gather_plan(shards(0, ("w_in",))))
    mod_mine = lax.dynamic_index_in_dim(mod_all, me, axis=2, keepdims=False)
    mod_mine = mod_mine.transpose(1, 0, 2).reshape(DEPTH, 3 * d)

    disc, disc_vjp = jax.vjp(jax.vmap(_ssm_discretize), ssm_a_re, ssm_a_im, ssm_log_dt, ssm_b_re, ssm_b_im)
    layer = lambda li, gathered_w_in: _prepare_layer(
        li, mod_mine, norm_g, _full_weights(("w_in",), gathered_w_in)["w_in"], attn_sinks, disc, ssm_c_re, ssm_c_im,
        ssm_d, b_glu, w_pool, pool_scale)
    late_weights = lambda gathered: _full_weights(LATE_WEIGHTS, gathered)
    core = lax.axis_index("c").astype(jnp.int32).reshape(1)

    def add_pairs(keys, by_dest, from_sibling, tag):
        flat = {k: (a.reshape(2, -1, a.shape[-1]), b.reshape(-1, b.shape[-1]))
                for k, a, b in zip(keys, by_dest, from_sibling)}
        small = [k for k in keys if k != "w_in"]
        sums = {}
        if "w_in" in flat:
            sums["w_in"] = _pair_add(*flat["w_in"], core, name=f"grads_pair_add_{tag}_w_in")
        if small:
            added = _pair_add_small([flat[k][0] for k in small], [flat[k][1] for k in small], core,
                                    name=f"grads_pair_add_{tag}_late")
            sums.update(zip(small, added))
        return [sums[k].reshape(b.shape) for k, b in zip(keys, from_sibling)]

    def chip_sums_of(keys, grads_li, tag):
        by_dest = _by_destination(keys, grads_li)
        return add_pairs(keys, by_dest, _run_plan(_sibling_swap_plan(by_dest), name=f"grads_sibling_swap_{tag}"), tag)

    layers, saved, grads = [None] * DEPTH, [None] * DEPTH, [None] * DEPTH
    layers[0] = layer(0, w_in0)
    xs, saved[0], layers[0], late1, w_in1 = _layer_fwd(
        x[0], layers[0], 0, late=(_gather_plan(shards(0, LATE_WEIGHTS)), late_weights),
        comm_attn=_gather_plan(shards(1, LATE_WEIGHTS)), comm_ssm=_gather_plan(shards(1, ("w_in",))))
    layers[1] = {**layer(1, w_in1), **late_weights(late1)}
    xs, saved[1], _, _, _ = _layer_fwd(xs, layers[1], 1)
    dx, fin_sums = _final_loss(xs, final_g[None, :], loss_target[0])
    loss_part = jnp.sum(fin_sums[1])
    dx, grads[1], _, _ = _layer_bwd(dx, layers[1], saved[1], 1)
    by_dest1 = _by_destination(BIG_WEIGHTS, grads[1])
    dx, grads[0], scattered1, scattered0 = _layer_bwd(
        dx, layers[0], saved[0], 0,
        later=(_sibling_swap_plan(by_dest1),
               lambda swapped: _chip_scatter_plan(add_pairs(BIG_WEIGHTS, by_dest1, swapped, "l1"))),
        own=lambda g: _chip_scatter_plan(chip_sums_of(tuple(g), g, "l0_" + "_".join(g))))
    big = list(zip(scattered0, scattered1))
    grad_x = dx[None]

    small = _pack_small(loss_part, fin_sums[0], grads)
    dmod_rows = jnp.concatenate([grads[li]["dmod"] for li in range(DEPTH)]).reshape(-1, 128)
    small_sum, dmod_gathered = _allreduce_small(small, dmod_rows, name="allreduce_small")
    out = {}

    def adam(name, w, g_slabs, m, v):
        shp = w.shape
        r = int(math.prod(shp[:-1])) if len(shp) > 1 else 1
        w2, m2, v2 = (a.reshape(r, shp[-1]) for a in (w, m, v))
        gs = [g.reshape(g.shape[0], r // len(g_slabs), shp[-1]) for g in g_slabs]
        res = _adamw(w2, gs, m2, v2, name=f"adamw_{name}", tr=512 if shp[-1] >= 128 else 2048)
        out[name] = tuple(a.reshape(shp) for a in res)

    flat = small_sum.reshape(-1)
    shapes = [(128,), (d,)]
    for _ in range(DEPTH):
        shapes += [(d,), (N_HEADS,), (SSM_W,), (SSM_W,), (4, POOL_GW, POOL_GW), (POOL_W,), (3 * d,),
                   (SSM_GROUPS, SSM_STATE), (SSM_GROUPS, SSM_STATE), (SSM_GROUPS, SSM_STATE, SSM_GROUP),
                   (SSM_GROUPS, SSM_STATE, SSM_GROUP), (SSM_GROUPS, SSM_GROUP, SSM_STATE), (SSM_GROUPS, SSM_GROUP, SSM_STATE)]
    un = _unpack_small(flat, shapes)
    loss = un[0][0]
    g_final_g = un[1]
    per = 13
    gl = [un[2 + li * per: 2 + (li + 1) * per] for li in range(DEPTH)]
    st = lambda j: jnp.stack([gl[li][j] for li in range(DEPTH)])
    g_norm_g, g_sinks, g_ssm_d, g_b_glu, g_w_pool, g_pool_scale, g_b_ada = (st(j) for j in range(7))
    d_lr, d_li, d_bbr, d_bbi, g_c_re, g_c_im = (st(j) for j in range(7, 13))
    g_a_re, g_a_im, g_log_dt, g_b_re, g_b_im = disc_vjp((d_lr, d_li, d_bbr, d_bbi))

    dmod_all = lax.dynamic_slice(dmod_gathered.reshape(N_DEV, DEPTH, 3 * d), (0, 0, me * ada_w), (N_DEV, DEPTH, ada_w))
    dmod_all = dmod_all.transpose(1, 0, 2)
    g_w_ada = jnp.stack([_mm_tn(c_act, dmod_all[li], tm=d, tn=ada_w, tk=N_DEV, name=f"dw_ada_l{li}") for li in range(DEPTH)])

    adam("w_ada", w_ada, [g_w_ada[None]], m_w_ada, v_w_ada)
    adam("w_in", w_in, big[0], m_w_in, v_w_in)
    adam("w_glu", w_glu, big[1], m_w_glu, v_w_glu)
    adam("w_br_att", w_br_att, big[2], m_w_br_att, v_w_br_att)
    adam("w_br_ssm", w_br_ssm, big[3], m_w_br_ssm, v_w_br_ssm)
    adam("w_br_pool", w_br_pool, big[4], m_w_br_pool, v_w_br_pool)
    adam("w_out", w_out, big[5], m_w_out, v_w_out)

    small_names = ["norm_g", "b_ada", "attn_sinks", "ssm_a_re", "ssm_a_im", "ssm_log_dt", "ssm_b_re", "ssm_b_im",
                   "ssm_c_re", "ssm_c_im", "ssm_d", "b_glu", "w_pool", "pool_scale", "final_g"]
    small_w = [norm_g, b_ada, attn_sinks, ssm_a_re, ssm_a_im, ssm_log_dt, ssm_b_re, ssm_b_im, ssm_c_re, ssm_c_im,
               ssm_d, b_glu, w_pool, pool_scale, final_g]
    small_m = [m_norm_g, m_b_ada, m_attn_sinks, m_ssm_a_re, m_ssm_a_im, m_ssm_log_dt, m_ssm_b_re, m_ssm_b_im,
               m_ssm_c_re, m_ssm_c_im, m_ssm_d, m_b_glu, m_w_pool, m_pool_scale, m_final_g]
    small_v = [v_norm_g, v_b_ada, v_attn_sinks, v_ssm_a_re, v_ssm_a_im, v_ssm_log_dt, v_ssm_b_re, v_ssm_b_im,
               v_ssm_c_re, v_ssm_c_im, v_ssm_d, v_b_glu, v_w_pool, v_pool_scale, v_final_g]
    small_g = [g_norm_g, g_b_ada, g_sinks, g_a_re, g_a_im, g_log_dt, g_b_re, g_b_im, g_c_re, g_c_im,
               g_ssm_d, g_b_glu, g_w_pool, g_pool_scale, g_final_g]

    for nm, w, g, m, v in zip(small_names, small_w, small_g, small_m, small_v):
        adam(nm, w, [g[None]], m, v)

    order = ["norm_g", "w_ada", "b_ada", "w_in", "attn_sinks", "ssm_a_re", "ssm_a_im", "ssm_log_dt", "ssm_b_re",
             "ssm_b_im", "ssm_c_re", "ssm_c_im", "ssm_d", "w_glu", "b_glu", "w_pool", "pool_scale", "w_br_att",
             "w_br_ssm", "w_br_pool", "w_out", "final_g"]
    return (loss, grad_x, *[out[k][0] for k in order], *[out[k][1] for k in order],
            *[out[k][2] for k in order], *[out[k][3] for k in order])
```

```python
import functools
import math

import jax
import jax.numpy as jnp
from jax import lax
from jax.experimental import pallas as pl
from jax.experimental.pallas import tpu as pltpu

F32 = jnp.float32
BF16 = jnp.bfloat16

N_DEV = 8
D_MODEL = 1024
DEPTH = 2
CHUNK = 64
N_HEADS = 8
N_KV_HEADS = 2
HEAD_DIM = 64
Q_PER_KV = N_HEADS // N_KV_HEADS
WINDOW = 128
ATT_W = 512
KV_W = 128
SSM_W = 512
SSM_GROUP = 16
SSM_GROUPS = 32
SSM_STATE = 64
SSM_N = SSM_GROUPS * SSM_STATE
POOL_W = 512
POOL_WINDOWS = (2, 4, 8, 16)
POOL_GW = 128
POOL_HALO = 16
EPS = 1e-6
NEG_INF = -1e30
ADAM_LR = 0.001
ADAM_B1 = 0.9
ADAM_B2 = 0.999
ADAM_EPS = 1e-08
ADAM_WD = 0.01
ADAM_STEP = 10

SEQ_BLOCK = 256
POOL_BLOCK = 512
ATT_BLOCK = 128
VMEM_LIMIT = 56 * 1024 * 1024

NN = (((1,), (0,)), ((), ()))
NT = (((1,), (1,)), ((), ()))
TN = (((0,), (0,)), ((), ()))


def _dot(a, b, dims=NN):
    return lax.dot_general(a.astype(BF16), b.astype(BF16), dims, preferred_element_type=F32)


def _params(*sem):
    return pltpu.CompilerParams(dimension_semantics=sem, vmem_limit_bytes=VMEM_LIMIT)


def _sigmoid(x):
    return 0.5 + 0.5 * jnp.tanh(0.5 * x)


def _silu_and_grad(z):
    s = _sigmoid(z)
    return z * s, s * (1.0 + z * (1.0 - s))


_GELU_K = math.sqrt(2.0 / math.pi)


def _gelu_and_grad(x):
    inner = _GELU_K * (x + 0.044715 * x * x * x)
    t = jnp.tanh(inner)
    val = 0.5 * x * (1.0 + t)
    grad = 0.5 * (1.0 + t) + 0.5 * x * (1.0 - t * t) * _GELU_K * (1.0 + 3.0 * 0.044715 * x * x)
    return val, grad


def _grouped_pieces(groups):
    out = []
    for ranges in groups:
        off, pieces = 0, []
        for lo, hi in ranges:
            pieces.append((off, lo, hi))
            off += hi - lo
        out.append(pieces)
    return out


def _mm_tn(a, b, *, out_dtype=F32, tm=1024, tn=1024, tk=1024, name, comm=None):
    k, m = a.shape
    n = b.shape[1]
    assert m % min(tm, m) == 0 and n % min(tn, n) == 0 and k % min(tk, k) == 0
    tm, tn, tk = min(tm, m), min(tn, n), min(tk, k)
    nk = k // tk
    grid = (m // tm, n // tn, nk)
    c_args, c_in, c_out, c_shape, c_scratch = _comm_extra(comm)

    def body(a_ref, b_ref, o_ref, acc_ref):
        kk = pl.program_id(2)

        @pl.when(kk == 0)
        def _():
            acc_ref[...] = jnp.zeros_like(acc_ref)

        acc_ref[...] += _dot(a_ref[...], b_ref[...], TN)

        @pl.when(kk == nk - 1)
        def _():
            o_ref[...] = acc_ref[...].astype(out_dtype)

    res = pl.pallas_call(
        _with_comm(body, comm, 2, 1, grid, -1), grid=grid,
        in_specs=[pl.BlockSpec((tk, tm), lambda i, j, kk: (kk, i)), pl.BlockSpec((tk, tn), lambda i, j, kk: (kk, j))] + c_in,
        out_specs=[pl.BlockSpec((tm, tn), lambda i, j, kk: (i, j))] + c_out,
        out_shape=[jax.ShapeDtypeStruct((m, n), out_dtype)] + c_shape,
        scratch_shapes=[pltpu.VMEM((tm, tn), F32)] + c_scratch,
        compiler_params=_params(*(("arbitrary",) * 3 if comm else ("parallel", "parallel", "arbitrary"))),
        name=name)(a, b, *c_args)
    return (res[0], list(res[1:])) if comm else res[0]


def _mm_tn_grouped(a, bs, groups, *, tm=512, tk=512, name, comm=None):
    k, m = a.shape
    tm, tk = min(tm, m), min(tk, k)
    assert m % tm == 0 and k % tk == 0
    nk, nb = k // tk, len(bs)
    n = sum(b.shape[1] for b in bs)
    ns = n // N_DEV
    pieces = []
    for plist in _grouped_pieces(groups):
        sub = []
        for off, lo, hi in plist:
            pos = lo
            while pos < hi:
                s = pos // ns
                end = min(hi, (s + 1) * ns)
                sub.append((s, pos - s * ns, end - s * ns, off + pos - lo))
                pos = end
        pieces.append(sub)
    grid = (m // tm, nk)
    c_args, c_in, c_out, c_shape, c_scratch = _comm_extra(comm)

    def body(a_ref, *refs):
        b_refs, o_ref, acc_refs = refs[:nb], refs[nb], refs[nb + 1:]
        kk = pl.program_id(1)
        av = a_ref[...]
        for b_ref, acc_ref, plist in zip(b_refs, acc_refs, pieces):
            @pl.when(kk == 0)
            def _():
                acc_ref[...] = jnp.zeros_like(acc_ref)

            acc_ref[...] += _dot(av, b_ref[...], TN)

            @pl.when(kk == nk - 1)
            def _():
                for s, c0, c1, off in plist:
                    o_ref[s % 2, s // 2, :, c0:c1] = acc_ref[:, off:off + c1 - c0].astype(BF16)

    res = pl.pallas_call(
        _with_comm(body, comm, 1 + nb, 1, grid, -1), grid=grid,
        in_specs=[pl.BlockSpec((tk, tm), lambda i, kk: (kk, i))]
                 + [pl.BlockSpec((tk, b.shape[1]), lambda i, kk: (kk, 0)) for b in bs] + c_in,
        out_specs=[pl.BlockSpec((2, N_DEV // 2, tm, ns), lambda i, kk: (0, 0, i, 0))] + c_out,
        out_shape=[jax.ShapeDtypeStruct((2, N_DEV // 2, m, ns), BF16)] + c_shape,
        scratch_shapes=[pltpu.VMEM((tm, b.shape[1]), F32) for b in bs] + c_scratch,
        compiler_params=_params("arbitrary", "arbitrary"), name=name)(a, *bs, *c_args)
    return res[0], list(res[1:])


def _ln_proj(x, g, shift, scale, w, groups, *, name, tm=512, comm=None):
    l, d = x.shape
    tm = min(tm, l)
    nb = l // tm
    pieces = _grouped_pieces(groups)
    widths = [sum(hi - lo for _, lo, hi in plist) for plist in pieces]
    nw = len(pieces)
    c_args, c_in, c_out, c_shape, c_scratch = _comm_extra(comm)

    def body(x_ref, g_ref, sh_ref, sc_ref, w_ref, h_ref, *p_refs):
        xv = x_ref[...]
        n = xv * lax.rsqrt(jnp.mean(xv * xv, axis=-1, keepdims=True) + EPS)
        h = ((n * g_ref[...]) * (1.0 + sc_ref[...]) + sh_ref[...]).astype(BF16)
        h_ref[...] = h
        for p_ref, plist in zip(p_refs, pieces):
            for off, lo, hi in plist:
                p_ref[:, off:off + hi - lo] = _dot(h, w_ref[:, lo:hi]).astype(BF16)

    vec = pl.BlockSpec((1, d), lambda i: (0, 0))
    row = lambda n: pl.BlockSpec((tm, n), lambda i: (i, 0))
    res = pl.pallas_call(
        _with_comm(body, comm, 5, 1 + nw, nb, -2), grid=(nb,),
        in_specs=[row(d), vec, vec, vec, pl.BlockSpec(w.shape, lambda i: (0, 0))] + c_in,
        out_specs=[row(d)] + [row(n) for n in widths] + c_out,
        out_shape=[jax.ShapeDtypeStruct((l, d), BF16)] + [jax.ShapeDtypeStruct((l, n), BF16) for n in widths] + c_shape,
        scratch_shapes=c_scratch,
        compiler_params=_params("arbitrary"), name=name)(x, g, shift, scale, w, *c_args)
    return res[0], list(res[1:1 + nw]), list(res[1 + nw:])


def _ln_proj_bwd(ds, w, groups, x, dres, g, scale, *, name, tm=256, comm=None):
    l, d = x.shape
    tm = min(tm, l)
    nb = l // tm
    nd = len(ds)
    pieces = _grouped_pieces(groups)
    c_args, c_in, c_out, c_shape, c_scratch = _comm_extra(comm)

    def body(*refs):
        d_refs = refs[:nd]
        w_ref, x_ref, dres_ref, g_ref, sc_ref, dx_ref, sums_ref = refs[nd:]
        dhv = None
        for d_ref, plist in zip(d_refs, pieces):
            for off, lo, hi in plist:
                term = _dot(d_ref[:, off:off + hi - lo], w_ref[:, lo:hi], NT)
                dhv = term if dhv is None else dhv + term
        xv = x_ref[...]
        rstd = lax.rsqrt(jnp.mean(xv * xv, axis=-1, keepdims=True) + EPS)
        n = xv * rstd
        gv = g_ref[...]
        dr = dhv * (1.0 + sc_ref[...])
        dn = dr * gv
        dx_ref[...] = dres_ref[...] + rstd * (dn - n * jnp.mean(dn * n, axis=-1, keepdims=True))

        @pl.when(pl.program_id(0) == 0)
        def _():
            sums_ref[...] = jnp.zeros_like(sums_ref)

        sums_ref[0:1, :] += jnp.sum(dhv, axis=0, keepdims=True)
        sums_ref[1:2, :] += jnp.sum(dhv * (n * gv), axis=0, keepdims=True)
        sums_ref[2:3, :] += jnp.sum(dr * n, axis=0, keepdims=True)

    vec = pl.BlockSpec((1, d), lambda i: (0, 0))
    row = pl.BlockSpec((tm, d), lambda i: (i, 0))
    res = pl.pallas_call(
        _with_comm(body, comm, nd + 5, 2, nb, -1), grid=(nb,),
        in_specs=[pl.BlockSpec((tm, a.shape[1]), lambda i: (i, 0)) for a in ds]
                 + [pl.BlockSpec(w.shape, lambda i: (0, 0)), row, row, vec, vec] + c_in,
        out_specs=[row, pl.BlockSpec((8, d), lambda i: (0, 0))] + c_out,
        out_shape=[jax.ShapeDtypeStruct((l, d), F32), jax.ShapeDtypeStruct((8, d), F32)] + c_shape,
        scratch_shapes=c_scratch,
        compiler_params=_params("arbitrary"), name=name)(*ds, w, x, dres, g, scale, *c_args)
    return res[0], res[1], list(res[2:])


def _final_loss(x, g, target, *, tm=512):
    l, d = x.shape

    def body(x_ref, g_ref, t_ref, dx_ref, sums_ref):
        xv = x_ref[...]
        rstd = lax.rsqrt(jnp.mean(xv * xv, axis=-1, keepdims=True) + EPS)
        n = xv * rstd
        gv = g_ref[...]
        err = n * gv - t_ref[...]
        dy = err * (1.0 / d)
        dn = dy * gv
        dx_ref[...] = rstd * (dn - n * jnp.mean(dn * n, axis=-1, keepdims=True))

        @pl.when(pl.program_id(0) == 0)
        def _():
            sums_ref[...] = jnp.zeros_like(sums_ref)

        sums_ref[0:1, :] += jnp.sum(dy * n, axis=0, keepdims=True)
        sums_ref[1:2, :] += jnp.sum(err * err, axis=0, keepdims=True) * (0.5 / d)

    vec = pl.BlockSpec((1, d), lambda i: (0, 0))
    row = pl.BlockSpec((tm, d), lambda i: (i, 0))
    dx, sums = pl.pallas_call(
        body, grid=(l // tm,),
        in_specs=[row, vec, row],
        out_specs=[row, pl.BlockSpec((8, d), lambda i: (0, 0))],
        out_shape=[jax.ShapeDtypeStruct((l, d), F32), jax.ShapeDtypeStruct((8, d), F32)],
        compiler_params=_params("arbitrary"), name="final_loss")(x, g, target)
    return dx, sums


def _attn_geometry(i, t):
    nk = t + WINDOW
    qi = lax.broadcasted_iota(jnp.int32, (t, nk), 0)
    kj = lax.broadcasted_iota(jnp.int32, (t, nk), 1)
    dist = jnp.abs(qi + WINDOW - kj).astype(F32)
    qc = jnp.right_shift(qi, 6)
    kc = jnp.right_shift(kj, 6)
    valid = (kc >= qc) & (kc <= qc + WINDOW // CHUNK) & ((i > 0) | (kj >= WINDOW))
    return dist, valid


def _attn_head(q, k_all, v_all, sink, slope, dist, valid):
    s = _dot(q, k_all, NT) * (1.0 / math.sqrt(HEAD_DIM)) - slope * dist
    s = jnp.where(valid, s, NEG_INF)
    m = jnp.maximum(jnp.max(s, axis=-1, keepdims=True), sink)
    e = jnp.exp(s - m)
    es = jnp.exp(sink - m)
    inv = 1.0 / (jnp.sum(e, axis=-1, keepdims=True) + es)
    p = e * inv
    o = _dot(p, v_all, NN)
    return p, o, es * inv


def _attn_specs(t):
    cur = pl.BlockSpec((t, ATT_W * 2 + KV_W * 2), lambda i: (i, 0))
    halo_blocks = t // WINDOW
    prev = pl.BlockSpec((WINDOW, 2 * KV_W), lambda i: (jnp.maximum(i * halo_blocks - 1, 0), (2 * ATT_W) // (2 * KV_W)))
    return cur, prev


def _attn_fwd(pa, sinks, *, name, t=ATT_BLOCK, comm=None):
    l = pa.shape[0]
    t = min(t, l)
    nb = l // t
    c_args, c_in, c_out, c_shape, c_scratch = _comm_extra(comm)

    def body(sink_ref, cur_ref, prev_ref, ya_ref):
        i = pl.program_id(0)
        dist, valid = _attn_geometry(i, t)
        for h in range(N_HEADS):
            kh = h // Q_PER_KV
            q = cur_ref[:, h * HEAD_DIM:(h + 1) * HEAD_DIM]
            z = cur_ref[:, ATT_W + h * HEAD_DIM:ATT_W + (h + 1) * HEAD_DIM].astype(F32)
            k_all = jnp.concatenate([prev_ref[:, kh * HEAD_DIM:(kh + 1) * HEAD_DIM],
                                     cur_ref[:, 2 * ATT_W + kh * HEAD_DIM:2 * ATT_W + (kh + 1) * HEAD_DIM]], axis=0)
            v_all = jnp.concatenate([prev_ref[:, KV_W + kh * HEAD_DIM:KV_W + (kh + 1) * HEAD_DIM],
                                     cur_ref[:, 2 * ATT_W + KV_W + kh * HEAD_DIM:2 * ATT_W + KV_W + (kh + 1) * HEAD_DIM]], axis=0)
            _, o, _ = _attn_head(q, k_all, v_all, sink_ref[h], 2.0 ** (-(h + 1)), dist, valid)
            sz, _ = _silu_and_grad(z)
            ya_ref[:, h * HEAD_DIM:(h + 1) * HEAD_DIM] = (o * sz).astype(BF16)

    cur, prev = _attn_specs(t)
    res = pl.pallas_call(
        _with_comm(body, comm, 3, 1, nb, nb - 4), grid=(nb,),
        in_specs=[pl.BlockSpec(memory_space=pltpu.SMEM), cur, prev] + c_in,
        out_specs=[pl.BlockSpec((t, ATT_W), lambda i: (i, 0))] + c_out,
        out_shape=[jax.ShapeDtypeStruct((l, ATT_W), BF16)] + c_shape,
        scratch_shapes=c_scratch,
        compiler_params=_params("arbitrary"), name=name)(sinks, pa, pa, *c_args)
    return res[0], res[1:]


def _attn_bwd(pa, sinks, dya, *, name, t=SEQ_BLOCK):
    l = pa.shape[0]
    t = min(t, l)
    nb = l // t
    scale = 1.0 / math.sqrt(HEAD_DIM)

    def body(sink_ref, cur_ref, prev_ref, dya_ref, dpa_ref, dsink_ref, carry_ref):
        n = pl.program_id(0)
        i = nb - 1 - n
        dist, valid = _attn_geometry(i, t)

        @pl.when(n == 0)
        def _():
            carry_ref[...] = jnp.zeros_like(carry_ref)
            dsink_ref[...] = jnp.zeros_like(dsink_ref)

        dk_acc = [jnp.zeros((HEAD_DIM, t + WINDOW), F32) for _ in range(N_KV_HEADS)]
        dv_acc = [jnp.zeros((HEAD_DIM, t + WINDOW), F32) for _ in range(N_KV_HEADS)]
        for h in range(N_HEADS):
            kh = h // Q_PER_KV
            q = cur_ref[:, h * HEAD_DIM:(h + 1) * HEAD_DIM]
            z = cur_ref[:, ATT_W + h * HEAD_DIM:ATT_W + (h + 1) * HEAD_DIM].astype(F32)
            k_all = jnp.concatenate([prev_ref[:, kh * HEAD_DIM:(kh + 1) * HEAD_DIM],
                                     cur_ref[:, 2 * ATT_W + kh * HEAD_DIM:2 * ATT_W + (kh + 1) * HEAD_DIM]], axis=0)
            v_all = jnp.concatenate([prev_ref[:, KV_W + kh * HEAD_DIM:KV_W + (kh + 1) * HEAD_DIM],
                                     cur_ref[:, 2 * ATT_W + KV_W + kh * HEAD_DIM:2 * ATT_W + KV_W + (kh + 1) * HEAD_DIM]], axis=0)
            p, o, p_sink = _attn_head(q, k_all, v_all, sink_ref[h], 2.0 ** (-(h + 1)), dist, valid)
            dy = dya_ref[:, h * HEAD_DIM:(h + 1) * HEAD_DIM]
            sz, dsz = _silu_and_grad(z)
            do = dy * sz
            dpa_ref[:, ATT_W + h * HEAD_DIM:ATT_W + (h + 1) * HEAD_DIM] = (dy * o * dsz).astype(BF16)
            dp = _dot(do, v_all, NT)
            delta = jnp.sum(p * dp, axis=-1, keepdims=True)
            ds = p * (dp - delta)
            dpa_ref[:, h * HEAD_DIM:(h + 1) * HEAD_DIM] = (_dot(ds, k_all, NN) * scale).astype(BF16)
            dk_acc[kh] = dk_acc[kh] + _dot(q, ds, TN) * scale
            dv_acc[kh] = dv_acc[kh] + _dot(do, p, TN)
            dsink_ref[h:h + 1, :] += jnp.broadcast_to(-jnp.sum(p_sink * delta, axis=0, keepdims=True), (1, 128))

        acc = jnp.concatenate(dk_acc + dv_acc, axis=0).T
        own = acc[WINDOW:, :]
        tail = own[t - WINDOW:, :] + carry_ref[...]
        if t > WINDOW:
            dpa_ref[0:t - WINDOW, 2 * ATT_W:] = own[:t - WINDOW, :].astype(BF16)
        dpa_ref[t - WINDOW:t, 2 * ATT_W:] = tail.astype(BF16)
        carry_ref[...] = acc[:WINDOW, :]

    halo_blocks = t // WINDOW
    wpa = 2 * ATT_W + 2 * KV_W
    cur = pl.BlockSpec((t, wpa), lambda n: (nb - 1 - n, 0))
    prev = pl.BlockSpec((WINDOW, 2 * KV_W),
                        lambda n: (jnp.maximum((nb - 1 - n) * halo_blocks - 1, 0), (2 * ATT_W) // (2 * KV_W)))
    return pl.pallas_call(
        body, grid=(nb,),
        in_specs=[pl.BlockSpec(memory_space=pltpu.SMEM), cur, prev, pl.BlockSpec((t, ATT_W), lambda n: (nb - 1 - n, 0))],
        out_specs=[pl.BlockSpec((t, wpa), lambda n: (nb - 1 - n, 0)), pl.BlockSpec((8, 128), lambda n: (0, 0))],
        out_shape=[jax.ShapeDtypeStruct((l, wpa), BF16), jax.ShapeDtypeStruct((8, 128), F32)],
        scratch_shapes=[pltpu.VMEM((WINDOW, 2 * KV_W), F32)],
        compiler_params=_params("arbitrary"), name=name)(sinks, pa, pa, dya)


def _scan(xr, xi, lr, li, t, reverse):
    row = lax.broadcasted_iota(jnp.int32, (t, 1), 0)
    d = 1
    pr, pi = lr, li
    while d < t:
        if reverse:
            sr = jnp.where(row < t - d, pltpu.roll(xr, t - d, 0), 0.0)
            si = jnp.where(row < t - d, pltpu.roll(xi, t - d, 0), 0.0)
        else:
            sr = jnp.where(row >= d, pltpu.roll(xr, d, 0), 0.0)
            si = jnp.where(row >= d, pltpu.roll(xi, d, 0), 0.0)
        xr, xi = xr + pr * sr - pi * si, xi + pr * si + pi * sr
        pr, pi = pr * pr - pi * pi, 2.0 * pr * pi
        d *= 2
    return xr, xi


SCAN_SUB = 8


def _split_hi_lo(a):
    hi = a.astype(BF16)
    lo = (a - hi.astype(F32)).astype(BF16)
    return jnp.concatenate([hi, lo], axis=0)


def _scan_mxu(xr, xi, tab, lam3, lam8, tri, expand, cr, ci, t, reverse):
    ns = t // SCAN_SUB
    n = xr.shape[1]
    v3 = lambda a: a.reshape(ns, SCAN_SUB, n)
    x3r, x3i = v3(xr), v3(xi)
    br = (x3r * tab[0] - x3i * tab[1]).reshape(t, n)
    bi = (x3r * tab[1] + x3i * tab[0]).reshape(t, n)
    pm = jnp.dot(tri, jnp.concatenate([br, bi], axis=1).astype(BF16), preferred_element_type=F32)
    p3r, p3i = v3(pm[:t, :n]), v3(pm[:t, n:])
    slr = p3r * tab[2] - p3i * tab[3]
    sli = p3r * tab[3] + p3i * tab[2]
    totr, toti = pm[t:, :n], pm[t:, n:]
    l3r, l3i = lam3
    l8r, l8i = lam8
    row = lax.broadcasted_iota(jnp.int32, (ns, 1), 0)
    edge = row == (ns - 1 if reverse else 0)
    er = totr * l3r - toti * l3i + jnp.where(edge, l8r * cr - l8i * ci, 0.0)
    ei = totr * l3i + toti * l3r + jnp.where(edge, l8r * ci + l8i * cr, 0.0)
    er, ei = _scan(er, ei, l8r, l8i, ns, reverse)
    shift = ns - 1 if reverse else 1
    nbr = jnp.where(edge, cr, pltpu.roll(er, shift, 0))
    nbi = jnp.where(edge, ci, pltpu.roll(ei, shift, 0))
    ex = jnp.dot(expand, _split_hi_lo(jnp.concatenate([nbr, nbi], axis=1)), preferred_element_type=F32)
    e3r, e3i = v3(ex[:, :n]), v3(ex[:, n:])
    sr = (slr + e3r * tab[4] - e3i * tab[5]).reshape(t, n)
    si = (sli + e3r * tab[5] + e3i * tab[4]).reshape(t, n)
    out = 0 if reverse else ns - 1
    return sr, si, er[out:out + 1, :], ei[out:out + 1, :]


def _scan_consts(t):
    import numpy as np
    ns = t // SCAN_SUB
    r = np.arange(t)
    same = (r[:, None] // SCAN_SUB) == (r[None, :] // SCAN_SUB)
    sums = (np.arange(ns)[:, None] == (r[None, :] // SCAN_SUB))
    tri = []
    for keep in (r[None, :] <= r[:, None], r[None, :] >= r[:, None]):
        tri.append(np.concatenate([same & keep, sums], axis=0).astype(np.float32))
    ex = ((r[:, None] // SCAN_SUB) == np.arange(ns)[None, :]).astype(np.float32)
    return jnp.asarray(np.stack(tri), BF16), jnp.asarray(np.concatenate([ex, ex], axis=1), BF16)


def _scan_tables(lr, li):
    import numpy as np
    den = lr * lr + li * li
    ir, ii = lr / den, -li / den
    mul = lambda a, b: (a[0] * b[0] - a[1] * b[1], a[0] * b[1] + a[1] * b[0])
    pw = {0: (jnp.ones_like(lr), jnp.zeros_like(lr))}
    for e in range(1, 9):
        pw[e] = mul(pw[e - 1], (lr, li))
    for e in range(-1, -5, -1):
        pw[e] = mul(pw[e + 1], (ir, ii))
    powers = jnp.stack([jnp.stack(pw[e]) for e in range(-4, 9)] + [jnp.zeros((2, lr.shape[0]), F32)])
    j = np.arange(SCAN_SUB)
    exps = [4 - j, j - 4, j + 1, j - 3, 3 - j, 8 - j]
    e_idx = np.stack([exps[t] + 4 for t in range(6) for _ in range(2)])
    c_idx = np.tile(np.array([0, 1])[:, None], (6, SCAN_SUB))
    sign = np.where((c_idx == 1) & (np.arange(12)[:, None] >= 6), -1.0, 1.0).astype(np.float32)
    tabs = powers[e_idx, c_idx] * sign[:, :, None]
    lam = powers[np.array([5, 5, 7, 7, 12, 12, 13, 13]), np.array([0, 1, 0, 1, 0, 1, 0, 0])]
    return lam, tabs


SSM_HALVES = 2
SSM_HW = SSM_W // SSM_HALVES
SSM_HN = SSM_N // SSM_HALVES


def _bd_nn(x, w):
    a = w.shape[1]
    return jnp.concatenate([_dot(x[:, h * a:(h + 1) * a], w[h]) for h in range(SSM_HALVES)], axis=1)


def _bd_nt(x, w):
    b = w.shape[2]
    return jnp.concatenate([_dot(x[:, h * b:(h + 1) * b], w[h], NT) for h in range(SSM_HALVES)], axis=1)


def _bd_tn(x, y):
    a, b = x.shape[1] // SSM_HALVES, y.shape[1] // SSM_HALVES
    return jnp.stack([_dot(x[:, h * a:(h + 1) * a], y[:, h * b:(h + 1) * b], TN) for h in range(SSM_HALVES)])


def _ssm_states(u, s0r, s0i, lam_ref, tab_ref, tri_ref, ex_ref, bre, bim, t):
    tab = tuple(tab_ref[k] for k in range(6))
    return _scan_mxu(_bd_nn(u, bre), _bd_nn(u, bim), tab, (lam_ref[2:3, :], lam_ref[3:4, :]),
                     (lam_ref[4:5, :], lam_ref[5:6, :]), tri_ref[0], ex_ref[...], s0r, s0i, t, False)


def _ssm_head(u, z, xr, xi, cre, cim, dskip, wglu, bglu):
    y = _bd_nn(xr, cre) - _bd_nn(xi, cim) + dskip * u
    y2, dgelu = _gelu_and_grad(y)
    gate = _sigmoid(_dot(y2, wglu) + bglu)
    y3 = y2 * gate
    return y2, dgelu, gate, y3


def _with_comm(body, comm, n_in, n_out, grid, mid_step):
    if comm is None:
        return body
    nc = len(comm["args"])
    n_sem = len(comm["scratch"])
    grid = (grid,) if isinstance(grid, int) else tuple(grid)
    total = math.prod(grid)

    def hosted(*refs):
        ins, cin = refs[:n_in], refs[n_in:n_in + nc]
        outs, cout = refs[n_in + nc:n_in + nc + n_out], refs[n_in + nc + n_out:n_in + 2 * nc + n_out]
        rest = refs[n_in + 2 * nc + n_out:]
        scratch, csem = rest[:len(rest) - n_sem], rest[len(rest) - n_sem:]
        start, forward, finish = comm["phases"](cin, cout, *csem)
        step = pl.program_id(0)
        for axis in range(1, len(grid)):
            step = step * grid[axis] + pl.program_id(axis)
        pl.when(step == 0)(start)
        pl.when(step == (mid_step if mid_step >= 0 else total + mid_step))(forward)
        body(*ins, *outs, *scratch)
        pl.when(step == total - 1)(finish)

    return hosted


def _comm_extra(comm):
    if comm is None:
        return [], [], [], [], []
    anyspec = pl.BlockSpec(memory_space=pl.ANY)
    nc = len(comm["args"])
    return comm["args"], [anyspec] * nc, [anyspec] * nc, comm["out_shape"], comm["scratch"]


def _ssm_fwd(ps, scan_ops, bblk, cblk, dskip, wglu, bglu, *, name, t=SEQ_BLOCK, comm=None):
    l = ps.shape[0]
    assert l % t == 0
    nb = l // t
    ns = t // SCAN_SUB
    c_args, c_in, c_out, c_shape, c_scratch = _comm_extra(comm)

    def body(ps_ref, lam_ref, tab_ref, tri_ref, ex_ref, b_ref, c_ref, d_ref, w_ref, bg_ref, ys_ref, chk_ref, xs_ref,
             st_ref):
        @pl.when(pl.program_id(0) == 0)
        def _():
            st_ref[...] = jnp.zeros_like(st_ref)

        chk_ref[...] = jnp.broadcast_to(st_ref[...], chk_ref.shape)
        u = ps_ref[:, :SSM_W].astype(F32)
        z = ps_ref[:, SSM_W:].astype(F32)
        xr, xi, er, ei = _ssm_states(u, st_ref[:, :SSM_N], st_ref[:, SSM_N:], lam_ref, tab_ref, tri_ref, ex_ref,
                                     b_ref[0], b_ref[1], t)
        st_ref[:, :SSM_N] = er
        st_ref[:, SSM_N:] = ei
        xr, xi = xr.astype(BF16), xi.astype(BF16)
        xs_ref[:, :SSM_N] = xr
        xs_ref[:, SSM_N:] = xi
        _, _, _, y3 = _ssm_head(u, z, xr, xi, c_ref[0], c_ref[1], d_ref[...], w_ref[...], bg_ref[...])
        sz, _ = _silu_and_grad(z)
        ys_ref[...] = (y3 * sz).astype(BF16)

    full = lambda shape: pl.BlockSpec(shape, lambda i: (0,) * len(shape))
    return pl.pallas_call(
        _with_comm(body, comm, 10, 3, nb, nb - 1), grid=(nb,),
        in_specs=[pl.BlockSpec((t, 2 * SSM_W), lambda i: (i, 0)), full((8, SSM_N)), full((12, SCAN_SUB, SSM_N)),
                  full((2, t + ns, t)), full((t, 2 * ns)), full((2, SSM_HALVES, SSM_HW, SSM_HN)),
                  full((2, SSM_HALVES, SSM_HN, SSM_HW)), full((1, SSM_W)), full((SSM_W, SSM_W)), full((1, SSM_W))] + c_in,
        out_specs=[pl.BlockSpec((t, SSM_W), lambda i: (i, 0)), pl.BlockSpec((8, 2 * SSM_N), lambda i: (i, 0)),
                   pl.BlockSpec((t, 2 * SSM_N), lambda i: (i, 0))] + c_out,
        out_shape=[jax.ShapeDtypeStruct((l, SSM_W), BF16), jax.ShapeDtypeStruct((nb * 8, 2 * SSM_N), F32),
                   jax.ShapeDtypeStruct((l, 2 * SSM_N), BF16)] + c_shape,
        scratch_shapes=[pltpu.VMEM((1, 2 * SSM_N), F32)] + c_scratch,
        compiler_params=_params("arbitrary"), name=name)(ps, *scan_ops, bblk, cblk, dskip, wglu, bglu, *c_args)


def _ssm_bwd(ps, dys, chk, states, scan_ops, bblk, cblk, dskip, wglu, bglu, *, name, t=SEQ_BLOCK, comm=None):
    l = ps.shape[0]
    assert l % t == 0
    nb = l // t
    ns = t // SCAN_SUB
    c_args, c_in, c_out, c_shape, c_scratch = _comm_extra(comm)

    def body(ps_ref, dys_ref, chk_ref, xs_ref, lam_ref, tab_ref, tri_ref, ex_ref, b_ref, c_ref, d_ref, w_ref, bg_ref,
             dps_ref, db_ref, dc_ref, dw_acc, sums_acc, gc_ref, db_acc, dc_acc):
        n = pl.program_id(0)

        @pl.when(n == 0)
        def _():
            gc_ref[...] = jnp.zeros_like(gc_ref)
            db_acc[...] = jnp.zeros_like(db_acc)
            dc_acc[...] = jnp.zeros_like(dc_acc)
            dw_acc[...] = jnp.zeros_like(dw_acc)
            sums_acc[...] = jnp.zeros_like(sums_acc)

        row = lax.broadcasted_iota(jnp.int32, (t, 1), 0)
        u = ps_ref[:, :SSM_W].astype(F32)
        z = ps_ref[:, SSM_W:].astype(F32)
        s0r, s0i = chk_ref[0:1, :SSM_N], chk_ref[0:1, SSM_N:]
        xr, xi = xs_ref[:, :SSM_N], xs_ref[:, SSM_N:]
        dskip = d_ref[...]
        y2, dgelu, gate, y3 = _ssm_head(u, z, xr, xi, c_ref[0], c_ref[1], dskip, w_ref[...], bg_ref[...])
        sz, dsz = _silu_and_grad(z)
        dys_v = dys_ref[...]
        dps_ref[:, SSM_W:] = (dys_v * y3 * dsz).astype(BF16)
        dy3 = dys_v * sz
        da = dy3 * y2 * gate * (1.0 - gate)
        dy2 = dy3 * gate + _dot(da, w_ref[...], NT)
        dw_acc[...] += _dot(y2, da, TN)
        dy = dy2 * dgelu
        sums_acc[2:3, :SSM_W] += jnp.sum(dy * u, axis=0, keepdims=True)
        sums_acc[3:4, :SSM_W] += jnp.sum(da, axis=0, keepdims=True)
        dc_acc[0] += _bd_tn(dy, xr)
        dc_acc[1] += -_bd_tn(dy, xi)
        rev_tab = tuple(tab_ref[k] for k in range(6, 12))
        gr, gi, gcr, gci = _scan_mxu(
            _bd_nt(dy, c_ref[0]), -_bd_nt(dy, c_ref[1]), rev_tab, (lam_ref[2:3, :], -lam_ref[3:4, :]),
            (lam_ref[4:5, :], -lam_ref[5:6, :]), tri_ref[1], ex_ref[...], gc_ref[:, :SSM_N], gc_ref[:, SSM_N:], t, True)
        gc_ref[:, :SSM_N] = gcr
        gc_ref[:, SSM_N:] = gci
        db_acc[0] += _bd_tn(u, gr)
        db_acc[1] += _bd_tn(u, gi)
        du = dskip * dy + _bd_nt(gr, b_ref[0]) + _bd_nt(gi, b_ref[1])
        dps_ref[:, :SSM_W] = du.astype(BF16)
        spr = jnp.where(row == 0, s0r, pltpu.roll(xr.astype(F32), 1, 0))
        spi = jnp.where(row == 0, s0i, pltpu.roll(xi.astype(F32), 1, 0))
        sums_acc[0:1, :] += jnp.sum(gr * spr + gi * spi, axis=0, keepdims=True)
        sums_acc[1:2, :] += jnp.sum(gi * spr - gr * spi, axis=0, keepdims=True)

        @pl.when(n == nb - 1)
        def _():
            per_half = SSM_GROUPS // SSM_HALVES
            for k in range(2):
                for g in range(SSM_GROUPS):
                    h, gl = divmod(g, per_half)
                    c0, p0 = gl * SSM_GROUP, gl * SSM_STATE
                    db_ref[k, g * SSM_GROUP:(g + 1) * SSM_GROUP, :] = db_acc[k, h, c0:c0 + SSM_GROUP, p0:p0 + SSM_STATE]
                    dc_ref[k, g * SSM_GROUP:(g + 1) * SSM_GROUP, :] = dc_acc[k, h, c0:c0 + SSM_GROUP, p0:p0 + SSM_STATE]

    full = lambda shape: pl.BlockSpec(shape, lambda n: (0,) * len(shape))
    return pl.pallas_call(
        _with_comm(body, comm, 13, 5, nb, 0), grid=(nb,),
        in_specs=[pl.BlockSpec((t, 2 * SSM_W), lambda n: (nb - 1 - n, 0)),
                  pl.BlockSpec((t, SSM_W), lambda n: (nb - 1 - n, 0)),
                  pl.BlockSpec((8, 2 * SSM_N), lambda n: (nb - 1 - n, 0)),
                  pl.BlockSpec((t, 2 * SSM_N), lambda n: (nb - 1 - n, 0)),
                  full((8, SSM_N)), full((12, SCAN_SUB, SSM_N)), full((2, t + ns, t)), full((t, 2 * ns)),
                  full((2, SSM_HALVES, SSM_HW, SSM_HN)), full((2, SSM_HALVES, SSM_HN, SSM_HW)), full((1, SSM_W)),
                  full((SSM_W, SSM_W)), full((1, SSM_W))] + c_in,
        out_specs=[pl.BlockSpec((t, 2 * SSM_W), lambda n: (nb - 1 - n, 0)), full((2, SSM_W, SSM_STATE)),
                   full((2, SSM_W, SSM_STATE)), full((SSM_W, SSM_W)), full((8, SSM_N))] + c_out,
        out_shape=[jax.ShapeDtypeStruct((l, 2 * SSM_W), BF16),
                   jax.ShapeDtypeStruct((2, SSM_W, SSM_STATE), F32),
                   jax.ShapeDtypeStruct((2, SSM_W, SSM_STATE), F32),
                   jax.ShapeDtypeStruct((SSM_W, SSM_W), F32),
                   jax.ShapeDtypeStruct((8, SSM_N), F32)] + c_shape,
        scratch_shapes=[pltpu.VMEM((1, 2 * SSM_N), F32), pltpu.VMEM((2, SSM_HALVES, SSM_HW, SSM_HN), F32),
                        pltpu.VMEM((2, SSM_HALVES, SSM_HW, SSM_HN), F32)] + c_scratch,
        compiler_params=_params("arbitrary"), name=name)(ps, dys, chk, states, *scan_ops, bblk, cblk, dskip, wglu, bglu,
                                                         *c_args)


def _pool_count(i, t):
    pos = lax.broadcasted_iota(jnp.int32, (t, POOL_W), 0) + i * t + 1
    col = lax.broadcasted_iota(jnp.int32, (t, POOL_W), 1)
    win = jnp.where(col < POOL_GW, 2, jnp.where(col < 2 * POOL_GW, 4, jnp.where(col < 3 * POOL_GW, 8, 16)))
    return 1.0 / jnp.minimum(pos, win).astype(F32), col


def _window_sums(ext, n_rows, forward):
    col = lax.broadcasted_iota(jnp.int32, ext.shape, 1)
    sh = (lambda a, d: pltpu.roll(a, d, 0)) if forward else (lambda a, d: pltpu.roll(a, n_rows - d, 0))
    a2 = ext + sh(ext, 1)
    a4 = a2 + sh(a2, 2)
    a8 = a4 + sh(a4, 4)
    a16 = a8 + sh(a8, 8)
    return jnp.where(col < POOL_GW, a2, jnp.where(col < 2 * POOL_GW, a4, jnp.where(col < 3 * POOL_GW, a8, a16)))


def _pool_mix(pooled, wp_ref):
    return jnp.concatenate([_dot(pooled[:, g * POOL_GW:(g + 1) * POOL_GW], wp_ref[g]) for g in range(4)], axis=1)


def _pool_pooled(i, cur_u, prev_u, t):
    prev = jnp.where(i > 0, prev_u, 0.0)
    ext = jnp.concatenate([prev, cur_u], axis=0)
    inv_cnt, _ = _pool_count(i, t)
    return _window_sums(ext, t + POOL_HALO, True)[POOL_HALO:, :] * inv_cnt - cur_u


def _pool_fwd(pp, wpool, pscale, *, name, t=POOL_BLOCK):
    l = pp.shape[0]
    t = min(t, l)

    def body(cur_ref, prev_ref, wp_ref, sc_ref, yp_ref):
        i = pl.program_id(0)
        pooled = _pool_pooled(i, cur_ref[:, :POOL_W].astype(F32), prev_ref[...].astype(F32), t)
        lin = _pool_mix(pooled, wp_ref)
        sz, _ = _silu_and_grad(cur_ref[:, POOL_W:].astype(F32))
        yp_ref[...] = (lin * sc_ref[...] * sz).astype(BF16)

    hb = t // POOL_HALO
    return pl.pallas_call(
        body, grid=(l // t,),
        in_specs=[pl.BlockSpec((t, 2 * POOL_W), lambda i: (i, 0)),
                  pl.BlockSpec((POOL_HALO, POOL_W), lambda i: (jnp.maximum(i * hb - 1, 0), 0)),
                  pl.BlockSpec((4, POOL_GW, POOL_GW), lambda i: (0, 0, 0)),
                  pl.BlockSpec((1, POOL_W), lambda i: (0, 0))],
        out_specs=pl.BlockSpec((t, POOL_W), lambda i: (i, 0)),
        out_shape=jax.ShapeDtypeStruct((l, POOL_W), BF16),
        compiler_params=_params("parallel"), name=name)(pp, pp, wpool, pscale)


def _pool_bwd(pp, dyp, wpool, pscale, *, name, t=POOL_BLOCK):
    l = pp.shape[0]
    t = min(t, l)
    nb = l // t

    def body(cur_ref, prev_ref, dyp_ref, wp_ref, sc_ref, dpp_ref, dwp_ref, sums_ref, carry_ref):
        n = pl.program_id(0)
        i = nb - 1 - n

        @pl.when(n == 0)
        def _():
            carry_ref[...] = jnp.zeros_like(carry_ref)
            dwp_ref[...] = jnp.zeros_like(dwp_ref)
            sums_ref[...] = jnp.zeros_like(sums_ref)

        cur_u = cur_ref[:, :POOL_W].astype(F32)
        pooled = _pool_pooled(i, cur_u, prev_ref[...].astype(F32), t)
        lin = _pool_mix(pooled, wp_ref)
        sz, dsz = _silu_and_grad(cur_ref[:, POOL_W:].astype(F32))
        dyp_v = dyp_ref[...]
        scale = sc_ref[...]
        dpp_ref[:, POOL_W:] = (dyp_v * lin * scale * dsz).astype(BF16)
        dpre = dyp_v * sz
        sums_ref[0:1, :] += jnp.sum(dpre * lin, axis=0, keepdims=True)
        dlin = dpre * scale
        dpooled = []
        for g in range(4):
            dl = dlin[:, g * POOL_GW:(g + 1) * POOL_GW]
            dwp_ref[g] += _dot(pooled[:, g * POOL_GW:(g + 1) * POOL_GW], dl, TN)
            dpooled.append(_dot(dl, wp_ref[g], NT))
        dpooled = jnp.concatenate(dpooled, axis=1)
        inv_cnt, _ = _pool_count(i, t)
        dq = dpooled * inv_cnt
        ext = jnp.concatenate([dq, carry_ref[...]], axis=0)
        du = _window_sums(ext, t + POOL_HALO, False)[:t, :] - dpooled
        dpp_ref[:, :POOL_W] = du.astype(BF16)
        carry_ref[...] = dq[:POOL_HALO, :]

    hb = t // POOL_HALO
    return pl.pallas_call(
        body, grid=(nb,),
        in_specs=[pl.BlockSpec((t, 2 * POOL_W), lambda n: (nb - 1 - n, 0)),
                  pl.BlockSpec((POOL_HALO, POOL_W), lambda n: (jnp.maximum((nb - 1 - n) * hb - 1, 0), 0)),
                  pl.BlockSpec((t, POOL_W), lambda n: (nb - 1 - n, 0)),
                  pl.BlockSpec((4, POOL_GW, POOL_GW), lambda n: (0, 0, 0)),
                  pl.BlockSpec((1, POOL_W), lambda n: (0, 0))],
        out_specs=[pl.BlockSpec((t, 2 * POOL_W), lambda n: (nb - 1 - n, 0)),
                   pl.BlockSpec((4, POOL_GW, POOL_GW), lambda n: (0, 0, 0)),
                   pl.BlockSpec((8, POOL_W), lambda n: (0, 0))],
        out_shape=[jax.ShapeDtypeStruct((l, 2 * POOL_W), BF16), jax.ShapeDtypeStruct((4, POOL_GW, POOL_GW), F32),
                   jax.ShapeDtypeStruct((8, POOL_W), F32)],
        scratch_shapes=[pltpu.VMEM((POOL_HALO, POOL_W), F32)],
        compiler_params=_params("arbitrary"), name=name)(pp, pp, dyp, wpool, pscale)


def _merge_fwd(ya, ys, yp, wa, ws, wp, pg, wout, x, gate, *, name, tm=512):
    l, d = x.shape
    tm = min(tm, l)

    def body(ya_ref, ys_ref, yp_ref, wa_ref, ws_ref, wp_ref, pg_ref, wo_ref, x_ref, g_ref,
             xn_ref, mg_ref, out_ref):
        acc = None
        for k, (y_ref, w_ref) in enumerate(((ya_ref, wa_ref), (ys_ref, ws_ref), (yp_ref, wp_ref))):
            br = _dot(y_ref[...], w_ref[...])
            term = _sigmoid(pg_ref[:, k * d:(k + 1) * d].astype(F32)) * br
            acc = term if acc is None else acc + term
        merged = acc.astype(BF16)
        mg_ref[...] = merged
        out = _dot(merged, wo_ref[...])
        out_ref[...] = out.astype(BF16)
        xn_ref[...] = x_ref[...] + g_ref[...] * out

    rowy = pl.BlockSpec((tm, ATT_W), lambda i: (i, 0))
    wsp = pl.BlockSpec((ATT_W, d), lambda i: (0, 0))
    rowd = pl.BlockSpec((tm, d), lambda i: (i, 0))
    return pl.pallas_call(
        body, grid=(l // tm,),
        in_specs=[rowy, rowy, rowy, wsp, wsp, wsp, pl.BlockSpec((tm, 3 * d), lambda i: (i, 0)),
                  pl.BlockSpec((d, d), lambda i: (0, 0)), rowd, pl.BlockSpec((1, d), lambda i: (0, 0))],
        out_specs=[rowd] * 3,
        out_shape=[jax.ShapeDtypeStruct((l, d), F32)] + [jax.ShapeDtypeStruct((l, d), BF16)] * 2,
        compiler_params=_params("parallel"), name=name)(ya, ys, yp, wa, ws, wp, pg, wout, x, gate)


def _merge_bwd(dx, out, gate, wout, pg, wbrs, ys, merged, *, name, tm=256, comm=None):
    l, d = dx.shape
    tm = min(tm, l)
    nb = l // tm
    w = ys[0].shape[1]

    def body(dx_ref, out_ref, g_ref, w_ref, pg_ref, wa_ref, ws_ref, wp_ref,
             ya_ref, ys_ref, yp_ref, mg_ref,
             dya_ref, dys_ref, dyp_ref, dpg_ref, sums_ref, dwa_ref, dws_ref, dwp_ref, dwo_ref, acc_br, acc_out):
        i = pl.program_id(0)

        @pl.when(i == 0)
        def _():
            sums_ref[...] = jnp.zeros_like(sums_ref)
            acc_br[...] = jnp.zeros_like(acc_br)
            acc_out[...] = jnp.zeros_like(acc_out)

        dxv = dx_ref[...]
        sums_ref[0:1, :] += jnp.sum(dxv * out_ref[...].astype(F32), axis=0, keepdims=True)
        dmo = (dxv * g_ref[...]).astype(BF16)
        acc_out[...] += _dot(mg_ref[...], dmo, TN)
        dmerged = _dot(dmo, w_ref[...], NT)
        branches = ((wa_ref, ya_ref, dya_ref), (ws_ref, ys_ref, dys_ref), (wp_ref, yp_ref, dyp_ref))
        for k, (wk_ref, y_ref, dy_ref) in enumerate(branches):
            gk = _sigmoid(pg_ref[:, k * d:(k + 1) * d].astype(F32))
            dbr = (dmerged * gk).astype(BF16)
            br = _dot(y_ref[...], wk_ref[...])
            dpg_ref[:, k * d:(k + 1) * d] = (dmerged * br * gk * (1.0 - gk)).astype(BF16)
            dy_ref[...] = _dot(dbr, wk_ref[...], NT)
            acc_br[k] += _dot(y_ref[...], dbr, TN)

        @pl.when(i == nb - 1)
        def _():
            for k, dw_ref in enumerate((dwa_ref, dws_ref, dwp_ref)):
                dw_ref[...] = acc_br[k].astype(BF16)
            dwo_ref[...] = acc_out[...].astype(BF16)

    row = pl.BlockSpec((tm, d), lambda i: (i, 0))
    half = pl.BlockSpec((tm, w), lambda i: (i, 0))
    wide = pl.BlockSpec((tm, 3 * d), lambda i: (i, 0))
    const = lambda shape: pl.BlockSpec(shape, lambda i: (0,) * len(shape))
    c_args, c_in, c_out, c_shape, c_scratch = _comm_extra(comm)
    res = pl.pallas_call(
        _with_comm(body, comm, 12, 9, nb, 0), grid=(nb,),
        in_specs=[row, row, const((1, d)), const((d, d)), wide, const((w, d)), const((w, d)), const((w, d)),
                  half, half, half, row] + c_in,
        out_specs=[half, half, half, wide, const((8, d)), const((w, d)), const((w, d)), const((w, d)), const((d, d))] + c_out,
        out_shape=[jax.ShapeDtypeStruct((l, w), F32)] * 3 + [jax.ShapeDtypeStruct((l, 3 * d), BF16),
                                                             jax.ShapeDtypeStruct((8, d), F32)]
                  + [jax.ShapeDtypeStruct((w, d), BF16)] * 3 + [jax.ShapeDtypeStruct((d, d), BF16)] + c_shape,
        scratch_shapes=[pltpu.VMEM((3, w, d), F32), pltpu.VMEM((d, d), F32)] + c_scratch,
        compiler_params=_params("arbitrary"), name=name)(dx, out, gate, wout, pg, *wbrs, *ys, merged, *c_args)
    return list(res[:9]), list(res[9:])


def _adamw(w, gs, m, v, *, name, tr=256):
    r, c = w.shape
    ns = len(gs)
    p, rs, _ = gs[0].shape
    assert rs * ns == r
    tr = min(tr, rs)
    assert rs % tr == 0
    nr = rs // tr
    c1 = 1.0 / (1.0 - ADAM_B1 ** ADAM_STEP)
    c2 = 1.0 / (1.0 - ADAM_B2 ** ADAM_STEP)

    def body(*refs):
        w_ref, g_refs, (m_ref, v_ref, go_ref, d_ref, mo_ref, vo_ref) = refs[0], refs[1:1 + ns], refs[1 + ns:]
        slab = pl.program_id(0)
        gv = None
        for k, g_ref in enumerate(g_refs):
            gk = g_ref[0].astype(F32)
            for j in range(1, p):
                gk = gk + g_ref[j].astype(F32)
            gv = gk if gv is None else jnp.where(slab == k, gk, gv)
        go_ref[...] = gv
        mn = ADAM_B1 * m_ref[...] + (1.0 - ADAM_B1) * gv
        vn = ADAM_B2 * v_ref[...] + (1.0 - ADAM_B2) * (gv * gv)
        mo_ref[...] = mn
        vo_ref[...] = vn
        d_ref[...] = -ADAM_LR * ((mn * c1) / (jnp.sqrt(vn * c2) + ADAM_EPS) + ADAM_WD * w_ref[...])

    row = pl.BlockSpec((tr, c), lambda s, i: (s * nr + i, 0))
    g_specs = [pl.BlockSpec((p, tr, c), lambda s, i, k=k: (0, jnp.where(s == k, i, 0), 0)) for k in range(ns)]
    return pl.pallas_call(
        body, grid=(ns, nr),
        in_specs=[row] + g_specs + [row, row],
        out_specs=[row] * 4,
        out_shape=[jax.ShapeDtypeStruct((r, c), F32)] * 4,
        compiler_params=_params("arbitrary", "arbitrary"), name=name)(w, *gs, m, v)


def _mesh_place():
    x, y, c = lax.axis_index("x"), lax.axis_index("y"), lax.axis_index("c")
    other_chips = [(1 - x, y), (x, 1 - y), (1 - x, 1 - y)]
    return x, y, c, other_chips


def _run_plan(plan, *, name):
    n = len(plan["args"])

    def body(*refs):
        start, forward, finish = plan["phases"](refs[:n], refs[n:2 * n], *refs[2 * n:])
        start()
        forward()
        finish()

    anyspec = pl.BlockSpec(memory_space=pl.ANY)
    return pl.pallas_call(
        body, in_specs=[anyspec] * n, out_specs=[anyspec] * n, out_shape=plan["out_shape"],
        scratch_shapes=plan["scratch"], name=name)(*plan["args"])


def _gather_plan(arrs):
    n = len(arrs)

    def phases(ins, outs, send_sems, recv_sems, loc_sems):
        x, y, c, chips = _mesh_place()
        me = 4 * x + 2 * y + c
        slot = lambda px, py, pc: 4 * px + 2 * py + pc

        def copy(k, j, src, block, to):
            return pltpu.make_async_remote_copy(
                src_ref=src, dst_ref=outs[k].at[block], send_sem=send_sems.at[k, j], recv_sem=recv_sems.at[k, j],
                device_id=to, device_id_type=pl.DeviceIdType.MESH)

        local = [pltpu.make_async_copy(ins[k], outs[k].at[me], loc_sems.at[k]) for k in range(n)]
        first = []
        for k in range(n):
            first.append(copy(k, 0, ins[k], me, (x, y, 1 - c)))
            for j, chip in enumerate(chips):
                first.append(copy(k, 1 + j, ins[k], me, (*chip, c)))
        passed = [copy(k, 4 + j, outs[k].at[slot(*chip, c)], slot(*chip, c), (x, y, 1 - c))
                  for j, chip in enumerate(chips) for k in range(n)]

        def start():
            for cp in local + first:
                cp.start()

        def forward():
            for j, chip in enumerate(chips):
                for k in range(n):
                    copy(k, 1 + j, ins[k], slot(*chip, c), (x, y, c)).wait_recv()
                    passed[j * n + k].start()

        def finish():
            for k in range(n):
                copy(k, 0, ins[k], slot(x, y, 1 - c), (x, y, c)).wait_recv()
                for j, chip in enumerate(chips):
                    copy(k, 4 + j, ins[k], slot(*chip, 1 - c), (x, y, c)).wait_recv()
            for cp in first + passed:
                cp.wait_send()
            for cp in local:
                cp.wait()

        return start, forward, finish

    return dict(
        args=list(arrs), out_shape=[jax.ShapeDtypeStruct((N_DEV,) + a.shape, a.dtype) for a in arrs],
        scratch=[pltpu.SemaphoreType.DMA((n, 7)), pltpu.SemaphoreType.DMA((n, 7)), pltpu.SemaphoreType.DMA((n,))],
        phases=phases)


def _allreduce_small(small, extra, *, name):
    r, lanes = small.shape
    assert r % 16 == 0
    h = r // 2
    e = extra.shape[0]

    def body(s_ref, x_ref, out_ref, xall_ref, sib_ref, parts_ref, send_sems, recv_sems):
        x, y, c, chips = _mesh_place()
        me = 4 * x + 2 * y + c
        my_chip = 2 * x + y
        sibling = (x, y, 1 - c)
        mine = pl.ds(pl.multiple_of(c * h, 8), h)
        theirs = pl.ds(pl.multiple_of((1 - c) * h, 8), h)

        def remote(j, src, dst, to):
            return pltpu.make_async_remote_copy(src_ref=src, dst_ref=dst, send_sem=send_sems.at[j],
                                                recv_sem=recv_sems.at[j], device_id=to, device_id_type=pl.DeviceIdType.MESH)

        to_sibling = remote(0, s_ref.at[theirs], sib_ref, sibling)
        to_sibling.start()
        xall_ref[me] = x_ref[...]
        extras = []
        for rr in range(1, N_DEV):
            peer = me ^ rr
            cp = remote(4 + rr, x_ref, xall_ref.at[me], (peer // 4, (peer // 2) % 2, peer % 2))
            cp.start()
            extras.append(cp)
        to_sibling.wait_recv()
        parts_ref[my_chip] = s_ref[mine] + sib_ref[...]
        to_chips = [remote(1 + j, parts_ref.at[my_chip], parts_ref.at[my_chip], (px, py, c))
                    for j, (px, py) in enumerate(chips)]
        for cp in to_chips:
            cp.start()
        for cp in to_chips:
            cp.wait_recv()
        out_ref[mine] = (parts_ref[0] + parts_ref[1]) + (parts_ref[2] + parts_ref[3])
        done = remote(4, out_ref.at[mine], out_ref.at[mine], sibling)
        done.start()
        remote(4, out_ref.at[theirs], out_ref.at[theirs], sibling).wait_recv()
        for cp in extras:
            cp.wait()
        to_sibling.wait_send()
        for cp in to_chips:
            cp.wait_send()
        done.wait_send()

    vmem = pl.BlockSpec(memory_space=pltpu.VMEM)
    return pl.pallas_call(
        body, in_specs=[vmem, vmem], out_specs=[vmem, vmem],
        out_shape=[jax.ShapeDtypeStruct((r, lanes), F32), jax.ShapeDtypeStruct((N_DEV, e, lanes), F32)],
        scratch_shapes=[pltpu.VMEM((h, lanes), F32), pltpu.VMEM((4, h, lanes), F32),
                        pltpu.SemaphoreType.DMA((12,)), pltpu.SemaphoreType.DMA((12,))],
        compiler_params=pltpu.CompilerParams(vmem_limit_bytes=VMEM_LIMIT), name=name)(small, extra)


def _ada_modulation(c, w_ada, b_cols, comm):
    depth, d, cols = w_ada.shape
    c_args, c_in, c_out, c_shape, c_scratch = _comm_extra(comm)
    nc = len(c_args)

    def body(c_ref, w_ref, b_ref, *refs):
        cin, (cact_ref, mod_ref), cout = refs[:nc], refs[nc:nc + 2], refs[nc + 2:2 * nc + 2]
        call_ref, part_ref, send_sems, recv_sems = refs[2 * nc + 2:2 * nc + 6]
        start, forward, finish = comm["phases"](cin, cout, *refs[2 * nc + 6:])
        start()
        x, y, core, _ = _mesh_place()
        me = 4 * x + 2 * y + core

        def to_all(j0, src, dst):
            copies = []
            for r in range(1, N_DEV):
                peer = me ^ r
                copies.append(pltpu.make_async_remote_copy(
                    src_ref=src, dst_ref=dst, send_sem=send_sems.at[j0 + r - 1], recv_sem=recv_sems.at[j0 + r - 1],
                    device_id=(peer // 4, (peer // 2) % 2, peer % 2), device_id_type=pl.DeviceIdType.MESH))
            for cp in copies:
                cp.start()
            for cp in copies:
                cp.wait()

        call_ref[me] = c_ref[...]
        to_all(0, c_ref, call_ref.at[me])
        c_act = jnp.concatenate([call_ref[k] for k in range(N_DEV)], axis=0)
        c_act = c_act * _sigmoid(c_act)
        cact_ref[...] = c_act
        for li in range(depth):
            part_ref[li] = _dot(c_act, w_ref[li]) + b_ref[li:li + 1, :]
        mod_ref[me] = part_ref[...]
        to_all(N_DEV - 1, part_ref, mod_ref.at[me])
        forward()
        finish()

    vmem = pl.BlockSpec(memory_space=pltpu.VMEM)
    res = pl.pallas_call(
        body, in_specs=[vmem] * 3 + c_in, out_specs=[vmem] * 2 + c_out,
        out_shape=[jax.ShapeDtypeStruct((N_DEV, d), F32), jax.ShapeDtypeStruct((N_DEV, depth, N_DEV, cols), F32)] + c_shape,
        scratch_shapes=[pltpu.VMEM((N_DEV, 1, d), F32), pltpu.VMEM((depth, N_DEV, cols), F32),
                        pltpu.SemaphoreType.DMA((2 * (N_DEV - 1),)), pltpu.SemaphoreType.DMA((2 * (N_DEV - 1),))] + c_scratch,
        compiler_params=pltpu.CompilerParams(vmem_limit_bytes=VMEM_LIMIT), name="ada_modulation")(c, w_ada, b_cols, *c_args)
    return res[0], res[1], list(res[2:])


def _sibling_swap_plan(arrs):
    n = len(arrs)

    def phases(ins, outs, send_sems, recv_sems):
        x, y, c, _ = _mesh_place()
        copies = [pltpu.make_async_remote_copy(
            src_ref=ins[k].at[1 - c], dst_ref=outs[k], send_sem=send_sems.at[k], recv_sem=recv_sems.at[k],
            device_id=(x, y, 1 - c), device_id_type=pl.DeviceIdType.MESH) for k in range(n)]

        def start():
            for cp in copies:
                cp.start()

        def finish():
            for cp in copies:
                cp.wait()

        return start, (lambda: None), finish

    return dict(args=list(arrs), out_shape=[jax.ShapeDtypeStruct(a.shape[1:], a.dtype) for a in arrs],
                scratch=[pltpu.SemaphoreType.DMA((n,)), pltpu.SemaphoreType.DMA((n,))], phases=phases)


def _pair_add(mine, theirs, core, *, name, tr=1024):
    _, r, c = mine.shape
    tr = min(tr, r)
    assert r % tr == 0

    def body(core_ref, m_ref, t_ref, o_ref):
        o_ref[...] = (m_ref[0].astype(F32) + t_ref[...].astype(F32)).astype(BF16)

    return pl.pallas_call(
        body,
        grid_spec=pltpu.PrefetchScalarGridSpec(
            num_scalar_prefetch=1, grid=(r // tr,),
            in_specs=[pl.BlockSpec((1, tr, c), lambda i, core_ref: (core_ref[0], i, 0)),
                      pl.BlockSpec((tr, c), lambda i, core_ref: (i, 0))],
            out_specs=pl.BlockSpec((tr, c), lambda i, core_ref: (i, 0))),
        out_shape=jax.ShapeDtypeStruct((r, c), BF16),
        compiler_params=_params("parallel"), name=name)(core, mine, theirs)


def _pair_add_small(mines, theirs, core, *, name):
    n = len(mines)

    def body(core_ref, *refs):
        for m_ref, t_ref, o_ref in zip(refs[:n], refs[n:2 * n], refs[2 * n:]):
            o_ref[...] = (m_ref[0].astype(F32) + t_ref[...].astype(F32)).astype(BF16)

    whole = lambda a: pl.BlockSpec(a.shape, lambda i, core_ref: (0,) * a.ndim)
    return pl.pallas_call(
        body,
        grid_spec=pltpu.PrefetchScalarGridSpec(
            num_scalar_prefetch=1, grid=(1,),
            in_specs=[pl.BlockSpec((1,) + m.shape[1:], lambda i, core_ref: (core_ref[0], 0, 0)) for m in mines]
                     + [whole(t) for t in theirs],
            out_specs=[whole(t) for t in theirs]),
        out_shape=[jax.ShapeDtypeStruct(t.shape, BF16) for t in theirs],
        compiler_params=_params("arbitrary"), name=name)(core, *mines, *theirs)


def _chip_scatter_plan(arrs):
    n = len(arrs)

    def phases(ins, outs, send_sems, recv_sems, loc_sems):
        x, y, c, chips = _mesh_place()
        mine = 2 * x + y
        local = [pltpu.make_async_copy(ins[k].at[mine], outs[k].at[mine], loc_sems.at[k]) for k in range(n)]
        remote = [pltpu.make_async_remote_copy(
            src_ref=ins[k].at[2 * px + py], dst_ref=outs[k].at[mine], send_sem=send_sems.at[k, j],
            recv_sem=recv_sems.at[k, j], device_id=(px, py, c), device_id_type=pl.DeviceIdType.MESH)
            for j, (px, py) in enumerate(chips) for k in range(n)]

        def start():
            for cp in local + remote:
                cp.start()

        def finish():
            for cp in remote:
                cp.wait()
            for cp in local:
                cp.wait()

        return start, (lambda: None), finish

    return dict(
        args=list(arrs), out_shape=[jax.ShapeDtypeStruct(a.shape, a.dtype) for a in arrs],
        scratch=[pltpu.SemaphoreType.DMA((n, 3)), pltpu.SemaphoreType.DMA((n, 3)), pltpu.SemaphoreType.DMA((n,))],
        phases=phases)


def _ssm_discretize(a_re, a_im, log_dt, b_re, b_im):
    dt = jnp.exp(log_dt)[:, None]
    mag = jnp.exp(a_re * dt)
    lr = mag * jnp.cos(a_im * dt)
    li = mag * jnp.sin(a_im * dt)
    den = a_re * a_re + a_im * a_im
    cr = ((lr - 1.0) * a_re + li * a_im) / den
    ci = (li * a_re - (lr - 1.0) * a_im) / den
    bbr = cr[..., None] * b_re - ci[..., None] * b_im
    bbi = cr[..., None] * b_im + ci[..., None] * b_re
    return lr, li, bbr, bbi


def _ssm_dense(lr, li, bbr, bbi, c_re, c_im, *, name):
    import numpy as np
    scan_ops = _scan_tables(lr.reshape(-1), li.reshape(-1)) + _scan_consts(SEQ_BLOCK)
    per_half = SSM_GROUPS // SSM_HALVES
    bt = jnp.stack([b.transpose(0, 2, 1).reshape(SSM_W, SSM_STATE) for b in (bbr, bbi)])
    ct = jnp.stack([c.transpose(0, 2, 1).reshape(SSM_N, SSM_GROUP) for c in (c_re, c_im)])
    rep_p = jnp.asarray(np.tile(np.eye(SSM_STATE, dtype=np.float32), (1, per_half)), BF16)
    rep_c = jnp.asarray(np.tile(np.eye(SSM_GROUP, dtype=np.float32), (1, per_half)), BF16)

    def body(bt_ref, ct_ref, rp_ref, rc_ref, b_ref, c_ref):
        def on_diagonal(shape, rows, cols):
            r = lax.broadcasted_iota(jnp.int32, shape, 0) // rows
            c = lax.broadcasted_iota(jnp.int32, shape, 1) // cols
            return r == c

        mask_b = on_diagonal((SSM_HW, SSM_HN), SSM_GROUP, SSM_STATE)
        mask_c = on_diagonal((SSM_HN, SSM_HW), SSM_STATE, SSM_GROUP)
        for k in range(2):
            for h in range(SSM_HALVES):
                b_rows = bt_ref[k, h * SSM_HW:(h + 1) * SSM_HW, :]
                b_ref[k, h] = jnp.where(mask_b, _dot(b_rows, rp_ref[...]), 0.0).astype(BF16)
                c_rows = ct_ref[k, h * SSM_HN:(h + 1) * SSM_HN, :]
                c_ref[k, h] = jnp.where(mask_c, _dot(c_rows, rc_ref[...]), 0.0).astype(BF16)

    vmem = pl.BlockSpec(memory_space=pltpu.VMEM)
    bblk, cblk = pl.pallas_call(
        body, in_specs=[vmem] * 4, out_specs=[vmem] * 2,
        out_shape=[jax.ShapeDtypeStruct((2, SSM_HALVES, SSM_HW, SSM_HN), BF16),
                   jax.ShapeDtypeStruct((2, SSM_HALVES, SSM_HN, SSM_HW), BF16)],
        compiler_params=pltpu.CompilerParams(vmem_limit_bytes=VMEM_LIMIT), name=name)(bt, ct, rep_p, rep_c)
    return scan_ops, bblk, cblk


def _ssm_extract(db, dc, sums):
    db = db.reshape(2, SSM_GROUPS, SSM_GROUP, SSM_STATE).transpose(0, 1, 3, 2)
    dc = dc.reshape(2, SSM_GROUPS, SSM_GROUP, SSM_STATE)
    dlr = sums[0].reshape(SSM_GROUPS, SSM_STATE)
    dli = sums[1].reshape(SSM_GROUPS, SSM_STATE)
    return dlr, dli, db[0], db[1], dc[0], dc[1]


def _in_groups():
    names = ("q", "k", "v", "u_ssm", "u_pool", "z_att", "z_ssm", "z_pool", "gates")
    sizes = (ATT_W, KV_W, KV_W, SSM_W, POOL_W, ATT_W, SSM_W, POOL_W, 3 * D_MODEL)
    r, lo = {}, 0
    for nm, s in zip(names, sizes):
        r[nm] = (lo, lo + s)
        lo += s
    kv = (r["k"][0], r["v"][1])
    return ((r["q"], r["z_att"], kv), (r["u_ssm"], r["z_ssm"]), (r["u_pool"], r["z_pool"]), (r["gates"],))


IN_GROUPS = _in_groups()


def _layer_fwd(x, lw, li, late=None, comm_attn=None, comm_ssm=None):
    tag = f"l{li}"
    h, (pa, ps, pp, pg), arrived = _ln_proj(x, lw["norm_g"], lw["shift"], lw["scale"], lw["w_in"], IN_GROUPS,
                                            name=f"ln_proj_{tag}", comm=None if late is None else late[0])
    if late is not None:
        lw = {**lw, **late[1](arrived)}
    ya, from_attn = _attn_fwd(pa, lw["sinks"], name=f"attn_fwd_{tag}", comm=comm_attn)
    ys, chk, states, *from_ssm = _ssm_fwd(ps, lw["lam"], lw["bblk"], lw["cblk"], lw["ssm_d"], lw["w_glu"], lw["b_glu"],
                                          name=f"ssm_fwd_{tag}", comm=comm_ssm)
    yp = _pool_fwd(pp, lw["w_pool"], lw["pool_scale"], name=f"pool_fwd_{tag}")
    x_new, merged, out = _merge_fwd(ya, ys, yp, lw["w_br_att"], lw["w_br_ssm"], lw["w_br_pool"], pg,
                                    lw["w_out"], x, lw["gate"], name=f"merge_fwd_{tag}")
    saved = dict(x=x, h=h, pa=pa, ps=ps, pp=pp, pg=pg, ya=ya, ys=ys, yp=yp, chk=chk, states=states, merged=merged,
                 out=out)
    return x_new, saved, lw, list(from_attn), list(from_ssm)


def _layer_bwd(dx, lw, sv, li, later=None, own=None):
    tag = f"l{li}"
    g = {}
    merge_out, swapped = _merge_bwd(
        dx, sv["out"], lw["gate"], lw["w_out"], sv["pg"],
        (lw["w_br_att"], lw["w_br_ssm"], lw["w_br_pool"]), (sv["ya"], sv["ys"], sv["yp"]), sv["merged"],
        name=f"merge_bwd_{tag}", comm=None if later is None else later[0])
    dya, dys, dyp, dpg, gate_sums, g["w_br_att"], g["w_br_ssm"], g["w_br_pool"], g["w_out"] = merge_out
    dpa, dsink = _attn_bwd(sv["pa"], lw["sinks"], dya, name=f"attn_bwd_{tag}")
    dps, db_dense, dc_dense, dwglu, ssm_sums, *exchanged = _ssm_bwd(
        sv["ps"], dys, sv["chk"], sv["states"], lw["lam"], lw["bblk"], lw["cblk"], lw["ssm_d"], lw["w_glu"], lw["b_glu"],
        name=f"ssm_bwd_{tag}", comm=None if later is None else later[1](swapped))
    g["w_glu"] = dwglu.astype(BF16)
    dpp, dwpool, pool_sums = _pool_bwd(sv["pp"], dyp, lw["w_pool"], lw["pool_scale"], name=f"pool_bwd_{tag}")
    h = sv["h"]
    dproj = (dpa, dps, dpp, dpg)
    g["w_in"], from_late = _mm_tn_grouped(h, dproj, IN_GROUPS, name=f"dw_in_{tag}",
                                          comm=None if own is None else own({k: g[k] for k in LATE_WEIGHTS}))
    dx_in, ln_sums, from_w_in = _ln_proj_bwd(dproj, lw["w_in"], IN_GROUPS, sv["x"], dx, lw["norm_g"], lw["scale"],
                                             name=f"ln_proj_bwd_{tag}",
                                             comm=None if own is None else own({"w_in": g["w_in"]}))
    g["dmod"] = jnp.concatenate([ln_sums[0], ln_sums[1], gate_sums[0]])
    g["norm_g"] = ln_sums[2]
    g["attn_sinks"] = dsink[:, 0]
    g["ssm_raw"] = _ssm_extract(db_dense, dc_dense, ssm_sums)
    g["ssm_d"] = ssm_sums[2, :SSM_W]
    g["b_glu"] = ssm_sums[3, :SSM_W]
    g["w_pool"] = dwpool
    g["pool_scale"] = pool_sums[0]
    return dx_in, g, exchanged, list(from_w_in) + list(from_late)


BIG_WEIGHTS = ("w_in", "w_glu", "w_br_att", "w_br_ssm", "w_br_pool", "w_out")
ROW_SHARDED = ("w_glu", "w_out")


LATE_WEIGHTS = BIG_WEIGHTS[1:]


def _side_by_side(g, *, tm=256):
    n, r, c = g.shape

    def body(g_ref, o_ref):
        for s in range(n):
            o_ref[:, s * c:(s + 1) * c] = g_ref[s]

    return pl.pallas_call(
        body, grid=(r // tm,),
        in_specs=[pl.BlockSpec((n, tm, c), lambda i: (0, i, 0))],
        out_specs=pl.BlockSpec((tm, n * c), lambda i: (i, 0)),
        out_shape=jax.ShapeDtypeStruct((r, n * c), g.dtype),
        compiler_params=_params("parallel"), name="side_by_side")(g)


def _full_weights(keys, gathered):
    full = {}
    for k, g in zip(keys, gathered):
        if k in ROW_SHARDED:
            full[k] = g.reshape(N_DEV * g.shape[1], g.shape[2])
        elif g.shape[2] % 128:
            full[k] = _side_by_side(g)
        else:
            full[k] = g.transpose(1, 0, 2).reshape(g.shape[1], N_DEV * g.shape[2])
    return full


def _by_destination(keys, grads):
    out = []
    for k in keys:
        g = grads[k]
        if g.ndim == 4:
            out.append(g)
        elif k in ROW_SHARDED:
            out.append(g.reshape(4, 2, g.shape[0] // N_DEV, g.shape[1]).transpose(1, 0, 2, 3))
        else:
            out.append(g.reshape(g.shape[0], 4, 2, g.shape[1] // N_DEV).transpose(2, 1, 0, 3))
    return out


def _prepare_layer(li, mod, norm_g, w_in_full, attn_sinks, disc, ssm_c_re, ssm_c_im, ssm_d, b_glu, w_pool, pool_scale):
    d = D_MODEL
    lr, li_, bbr, bbi = disc
    lam, bblk, cblk = _ssm_dense(lr[li], li_[li], bbr[li], bbi[li], ssm_c_re[li], ssm_c_im[li], name=f"ssm_dense_l{li}")
    return dict(
        norm_g=norm_g[li][None, :], shift=mod[li, :d][None, :], scale=mod[li, d:2 * d][None, :],
        gate=mod[li, 2 * d:][None, :], w_in=w_in_full,
        sinks=attn_sinks[li], lam=lam, bblk=bblk, cblk=cblk, ssm_d=ssm_d[li][None, :],
        b_glu=b_glu[li][None, :], w_pool=w_pool[li].astype(BF16), pool_scale=pool_scale[li][None, :])


SMALL_ROWS = 64
SMALL_ORDER = ("norm_g", "attn_sinks", "ssm_d", "b_glu", "w_pool", "pool_scale", "dmod")


def _pack_small(loss, dfinal_g, layer_grads):
    parts = [jnp.broadcast_to(loss.reshape(1), (128,)), dfinal_g]
    for g in layer_grads:
        for k in SMALL_ORDER:
            v = g[k].reshape(-1)
            if v.shape[0] % 128:
                v = jnp.pad(v, (0, 128 - v.shape[0] % 128))
            parts.append(v)
        for v in g["ssm_raw"]:
            parts.append(v.reshape(-1))
    flat = jnp.concatenate(parts)
    return jnp.pad(flat, (0, (-flat.shape[0]) % (SMALL_ROWS * 128))).reshape(-1, 128)


def _unpack_small(flat, shapes):
    out, off = [], 0
    for s in shapes:
        n = int(math.prod(s))
        out.append(flat[off:off + n].reshape(s))
        off += n + (-n) % 128
    return out


def kernel(x, c, norm_g, w_ada, b_ada, w_in, attn_sinks, ssm_a_re, ssm_a_im, ssm_log_dt, ssm_b_re, ssm_b_im, ssm_c_re, ssm_c_im, ssm_d, w_glu, b_glu, w_pool, pool_scale, w_br_att, w_br_ssm, w_br_pool, w_out, final_g, loss_target, m_norm_g, m_w_ada, m_b_ada, m_w_in, m_attn_sinks, m_ssm_a_re, m_ssm_a_im, m_ssm_log_dt, m_ssm_b_re, m_ssm_b_im, m_ssm_c_re, m_ssm_c_im, m_ssm_d, m_w_glu, m_b_glu, m_w_pool, m_pool_scale, m_w_br_att, m_w_br_ssm, m_w_br_pool, m_w_out, m_final_g, v_norm_g, v_w_ada, v_b_ada, v_w_in, v_attn_sinks, v_ssm_a_re, v_ssm_a_im, v_ssm_log_dt, v_ssm_b_re, v_ssm_b_im, v_ssm_c_re, v_ssm_c_im, v_ssm_d, v_w_glu, v_b_glu, v_w_pool, v_pool_scale, v_w_br_att, v_w_br_ssm, v_w_br_pool, v_w_out, v_final_g):
    me = 4 * lax.axis_index("x") + 2 * lax.axis_index("y") + lax.axis_index("c")
    d = D_MODEL
    ada_w = 3 * d // N_DEV

    sharded = dict(w_in=w_in, w_glu=w_glu, w_br_att=w_br_att, w_br_ssm=w_br_ssm, w_br_pool=w_br_pool, w_out=w_out)
    shards = lambda li, keys: [sharded[k][li].astype(BF16) for k in keys]

    b_cols = lax.dynamic_slice(b_ada, (0, me * ada_w), (DEPTH, ada_w))
    c_act, mod_all, w_in0 = _ada_modulation(c, w_ada, b_cols, _gather_plan(shards(0, ("w_in",))))
    mod_mine = lax.dynamic_index_in_dim(mod_all, me, axis=2, keepdims=False)
    mod_mine = mod_mine.transpose(1, 0, 2).reshape(DEPTH, 3 * d)

    disc, disc_vjp = jax.vjp(jax.vmap(_ssm_discretize), ssm_a_re, ssm_a_im, ssm_log_dt, ssm_b_re, ssm_b_im)
    layer = lambda li, gathered_w_in: _prepare_layer(
        li, mod_mine, norm_g, _full_weights(("w_in",), gathered_w_in)["w_in"], attn_sinks, disc, ssm_c_re, ssm_c_im,
        ssm_d, b_glu, w_pool, pool_scale)
    late_weights = lambda gathered: _full_weights(LATE_WEIGHTS, gathered)
    core = lax.axis_index("c").astype(jnp.int32).reshape(1)

    def add_pairs(keys, by_dest, from_sibling, tag):
        flat = {k: (a.reshape(2, -1, a.shape[-1]), b.reshape(-1, b.shape[-1]))
                for k, a, b in zip(keys, by_dest, from_sibling)}
        small = [k for k in keys if k != "w_in"]
        sums = {}
        if "w_in" in flat:
            sums["w_in"] = _pair_add(*flat["w_in"], core, name=f"grads_pair_add_{tag}_w_in")
        if small:
            added = _pair_add_small([flat[k][0] for k in small], [flat[k][1] for k in small], core,
                                    name=f"grads_pair_add_{tag}_late")
            sums.update(zip(small, added))
        return [sums[k].reshape(b.shape) for k, b in zip(keys, from_sibling)]

    def chip_sums_of(keys, grads_li, tag):
        by_dest = _by_destination(keys, grads_li)
        return add_pairs(keys, by_dest, _run_plan(_sibling_swap_plan(by_dest), name=f"grads_sibling_swap_{tag}"), tag)

    layers, saved, grads = [None] * DEPTH, [None] * DEPTH, [None] * DEPTH
    layers[0] = layer(0, w_in0)
    xs, saved[0], layers[0], late1, w_in1 = _layer_fwd(
        x[0], layers[0], 0, late=(_gather_plan(shards(0, LATE_WEIGHTS)), late_weights),
        comm_attn=_gather_plan(shards(1, LATE_WEIGHTS)), comm_ssm=_gather_plan(shards(1, ("w_in",))))
    layers[1] = {**layer(1, w_in1), **late_weights(late1)}
    xs, saved[1], _, _, _ = _layer_fwd(xs, layers[1], 1)
    dx, fin_sums = _final_loss(xs, final_g[None, :], loss_target[0])
    loss_part = jnp.sum(fin_sums[1])
    dx, grads[1], _, _ = _layer_bwd(dx, layers[1], saved[1], 1)
    by_dest1 = _by_destination(BIG_WEIGHTS, grads[1])
    dx, grads[0], scattered1, scattered0 = _layer_bwd(
        dx, layers[0], saved[0], 0,
        later=(_sibling_swap_plan(by_dest1),
               lambda swapped: _chip_scatter_plan(add_pairs(BIG_WEIGHTS, by_dest1, swapped, "l1"))),
        own=lambda g: _chip_scatter_plan(chip_sums_of(tuple(g), g, "l0_" + "_".join(g))))
    big = list(zip(scattered0, scattered1))
    grad_x = dx[None]

    small = _pack_small(loss_part, fin_sums[0], grads)
    dmod_rows = jnp.concatenate([grads[li]["dmod"] for li in range(DEPTH)]).reshape(-1, 128)
    small_sum, dmod_gathered = _allreduce_small(small, dmod_rows, name="allreduce_small")
    out = {}

    def adam(name, w, g_slabs, m, v):
        shp = w.shape
        r = int(math.prod(shp[:-1])) if len(shp) > 1 else 1
        w2, m2, v2 = (a.reshape(r, shp[-1]) for a in (w, m, v))
        gs = [g.reshape(g.shape[0], r // len(g_slabs), shp[-1]) for g in g_slabs]
        res = _adamw(w2, gs, m2, v2, name=f"adamw_{name}", tr=512 if shp[-1] >= 128 else 2048)
        out[name] = tuple(a.reshape(shp) for a in res)

    flat = small_sum.reshape(-1)
    shapes = [(128,), (d,)]
    for _ in range(DEPTH):
        shapes += [(d,), (N_HEADS,), (SSM_W,), (SSM_W,), (4, POOL_GW, POOL_GW), (POOL_W,), (3 * d,),
                   (SSM_GROUPS, SSM_STATE), (SSM_GROUPS, SSM_STATE), (SSM_GROUPS, SSM_STATE, SSM_GROUP),
                   (SSM_GROUPS, SSM_STATE, SSM_GROUP), (SSM_GROUPS, SSM_GROUP, SSM_STATE), (SSM_GROUPS, SSM_GROUP, SSM_STATE)]
    un = _unpack_small(flat, shapes)
    loss = un[0][0]
    g_final_g = un[1]
    per = 13
    gl = [un[2 + li * per: 2 + (li + 1) * per] for li in range(DEPTH)]
    st = lambda j: jnp.stack([gl[li][j] for li in range(DEPTH)])
    g_norm_g, g_sinks, g_ssm_d, g_b_glu, g_w_pool, g_pool_scale, g_b_ada = (st(j) for j in range(7))
    d_lr, d_li, d_bbr, d_bbi, g_c_re, g_c_im = (st(j) for j in range(7, 13))
    g_a_re, g_a_im, g_log_dt, g_b_re, g_b_im = disc_vjp((d_lr, d_li, d_bbr, d_bbi))

    dmod_all = lax.dynamic_slice(dmod_gathered.reshape(N_DEV, DEPTH, 3 * d), (0, 0, me * ada_w), (N_DEV, DEPTH, ada_w))
    dmod_all = dmod_all.transpose(1, 0, 2)
    g_w_ada = jnp.stack([_mm_tn(c_act, dmod_all[li], tm=d, tn=ada_w, tk=N_DEV, name=f"dw_ada_l{li}") for li in range(DEPTH)])

    adam("w_ada", w_ada, [g_w_ada[None]], m_w_ada, v_w_ada)
    adam("w_in", w_in, big[0], m_w_in, v_w_in)
    adam("w_glu", w_glu, big[1], m_w_glu, v_w_glu)
    adam("w_br_att", w_br_att, big[2], m_w_br_att, v_w_br_att)
    adam("w_br_ssm", w_br_ssm, big[3], m_w_br_ssm, v_w_br_ssm)
    adam("w_br_pool", w_br_pool, big[4], m_w_br_pool, v_w_br_pool)
    adam("w_out", w_out, big[5], m_w_out, v_w_out)

    small_names = ["norm_g", "b_ada", "attn_sinks", "ssm_a_re", "ssm_a_im", "ssm_log_dt", "ssm_b_re", "ssm_b_im",
                   "ssm_c_re", "ssm_c_im", "ssm_d", "b_glu", "w_pool", "pool_scale", "final_g"]
    small_w = [norm_g, b_ada, attn_sinks, ssm_a_re, ssm_a_im, ssm_log_dt, ssm_b_re, ssm_b_im, ssm_c_re, ssm_c_im,
               ssm_d, b_glu, w_pool, pool_scale, final_g]
    small_m = [m_norm_g, m_b_ada, m_attn_sinks, m_ssm_a_re, m_ssm_a_im, m_ssm_log_dt, m_ssm_b_re, m_ssm_b_im,
               m_ssm_c_re, m_ssm_c_im, m_ssm_d, m_b_glu, m_w_pool, m_pool_scale, m_final_g]
    small_v = [v_norm_g, v_b_ada, v_attn_sinks, v_ssm_a_re, v_ssm_a_im, v_ssm_log_dt, v_ssm_b_re, v_ssm_b_im,
               v_ssm_c_re, v_ssm_c_im, v_ssm_d, v_b_glu, v_w_pool, v_pool_scale, v_final_g]
    small_g = [g_norm_g, g_b_ada, g_sinks, g_a_re, g_a_im, g_log_dt, g_b_re, g_b_im, g_c_re, g_c_im,
               g_ssm_d, g_b_glu, g_w_pool, g_pool_scale, g_final_g]

    for nm, w, g, m, v in zip(small_names, small_w, small_g, small_m, small_v):
        adam(nm, w, [g[None]], m, v)

    order = ["norm_g", "w_ada", "b_ada", "w_in", "attn_sinks", "ssm_a_re", "ssm_a_im", "ssm_log_dt", "ssm_b_re",
             "ssm_b_im", "ssm_c_re", "ssm_c_im", "ssm_d", "w_glu", "b_glu", "w_pool", "pool_scale", "w_br_att",
             "w_br_ssm", "w_br_pool", "w_out", "final_g"]
    return (loss, grad_x, *[out[k][0] for k in order], *[out[k][1] for k in order],
            *[out[k][2] for k in order], *[out[k][3] for k in order])
```
